```python
import math
import jax, jax.numpy as jnp
from jax import lax
import numpy as np

D_MODEL = 1024
BATCH = 8
SEQ = 4096
DEPTH = 2

N_MEM = 256
NORM_EPS = 1e-6
ROPE_THETA = 500000.0
HEAD_DIM = 64
ROT_DIM = HEAD_DIM // 4
BLOCK = 128
LRU_WIDTH = D_MODEL
LRU_HEADS = 4
LRU_HEAD_DIM = LRU_WIDTH // LRU_HEADS
CONV_WIDTH = 4
LRU_C = 8.0
B_HEADS = 8
B_WIDTH = B_HEADS * HEAD_DIM
DILATED_PATTERN = ((128, 1), (512, 4), (2048, 16))
C_HEADS = 16
C_KV_HEADS = 2
C_WINDOW = 128
C_Q_WIDTH = C_HEADS * HEAD_DIM
C_KV_WIDTH = C_KV_HEADS * HEAD_DIM
C_QKV = C_Q_WIDTH + 2 * C_KV_WIDTH
XA_HEADS = 4
XA_HEAD_DIM = 128
XA_WIDTH = XA_HEADS * XA_HEAD_DIM
D_FF = -(-8 * D_MODEL // (3 * 256)) * 256
AB_IN = 2 * LRU_WIDTH + 3 * B_WIDTH
AB_SPLITS = (LRU_WIDTH, 2 * LRU_WIDTH, 2 * LRU_WIDTH + B_WIDTH, 2 * LRU_WIDTH + 2 * B_WIDTH)
AB_OUT = LRU_WIDTH + B_WIDTH
N_EVEN = (DEPTH + 1) // 2
N_ODD = DEPTH // 2
NEG = -1e30

kernel_name = "hybrid_rglru_dilated_swa_sink_block"


def rms_norm(x, g):
    xf = x.astype(jnp.float32)
    xf = xf * lax.rsqrt(jnp.mean(xf * xf, axis=-1, keepdims=True) + NORM_EPS)
    return (xf * g.astype(jnp.float32)).astype(x.dtype)


def rope_tables(L):
    inv = ROPE_THETA ** (-jnp.arange(0, ROT_DIM, 2, dtype=jnp.float32) / ROT_DIM)
    ang = jnp.arange(L, dtype=jnp.float32)[:, None] * inv[None, :]
    return jnp.cos(ang), jnp.sin(ang)


def partial_rope(x, cos, sin):
    half = ROT_DIM // 2
    cos = cos.astype(x.dtype)
    sin = sin.astype(x.dtype)
    x1 = x[..., :half]
    x2 = x[..., half:ROT_DIM]
    return jnp.concatenate([x1 * cos - x2 * sin, x2 * cos + x1 * sin, x[..., ROT_DIM:]], axis=-1)


def banded_attention(q, k, v, max_dist, sinks=None):
    Bsz, Hq, L, hd = q.shape
    Hkv = k.shape[1]
    G = Hq // Hkv
    nb = -(-L // BLOCK)
    Lp = nb * BLOCK
    pad = ((0, 0), (0, 0), (0, Lp - L), (0, 0))
    qb = jnp.pad(q, pad).reshape(Bsz, Hkv, G, nb, BLOCK, hd)
    kb = jnp.pad(k, pad).reshape(Bsz, Hkv, nb, BLOCK, hd)
    vb = jnp.pad(v, pad).reshape(Bsz, Hkv, nb, BLOCK, hd)

    def band(t):
        prev = jnp.pad(t, ((0, 0), (0, 0), (1, 0), (0, 0), (0, 0)))[:, :, :nb]
        return jnp.concatenate([prev, t], axis=3)

    kband, vband = band(kb), band(vb)
    s = jnp.einsum('bhgnqd,bhnkd->bhgnqk', qb, kband).astype(jnp.float32) * (hd ** -0.5)
    blk = jnp.arange(nb)[:, None, None]
    qpos = blk * BLOCK + jnp.arange(BLOCK)[None, :, None]
    kpos = (blk - 1) * BLOCK + jnp.arange(2 * BLOCK)[None, None, :]
    dist = qpos - kpos
    mask = (dist >= 0) & (dist <= max_dist) & (kpos >= 0)
    s = jnp.where(mask, s, NEG)
    m = jnp.max(s, axis=-1, keepdims=True)
    if sinks is not None:
        sk = sinks.astype(jnp.float32).reshape(1, Hkv, G, 1, 1, 1)
        m = jnp.maximum(m, sk)
        e = jnp.exp(s - m)
        den = jnp.sum(e, axis=-1, keepdims=True) + jnp.exp(sk - m)
    else:
        e = jnp.exp(s - m)
        den = jnp.sum(e, axis=-1, keepdims=True)
    p = e / den
    lse = (m + jnp.log(den))[..., 0]
    o = jnp.einsum('bhgnqk,bhnkd->bhgnqd', p.astype(v.dtype), vband)
    o = o.reshape(Bsz, Hq, Lp, hd)[:, :, :L]
    lse = lse.reshape(Bsz, Hq, Lp)[:, :, :L]
    return o, lse


def dilated_attention(q, k, v):
    Bsz, H, L, hd = q.shape
    outs, lses = [], []
    for window, d in DILATED_PATTERN:
        Ld = L // d

        def to_strided(t):
            return t.reshape(Bsz, H, Ld, d, hd).transpose(0, 1, 3, 2, 4).reshape(Bsz, H * d, Ld, hd)

        o, lse = banded_attention(to_strided(q), to_strided(k), to_strided(v), window // d)
        outs.append(o.reshape(Bsz, H, d, Ld, hd).transpose(0, 1, 3, 2, 4).reshape(Bsz, H, L, hd))
        lses.append(lse.reshape(Bsz, H, d, Ld).transpose(0, 1, 3, 2).reshape(Bsz, H, L))
    w = jax.nn.softmax(jnp.stack(lses, axis=0), axis=0)
    o = jnp.sum(w[..., None] * jnp.stack(outs, axis=0).astype(jnp.float32), axis=0)
    return o.astype(q.dtype)


def causal_depthwise_conv(x, w, b):
    L = x.shape[1]
    xp = jnp.pad(x, ((0, 0), (CONV_WIDTH - 1, 0), (0, 0)))
    y = b + xp[:, 0:L] * w[0]
    for tap in range(1, CONV_WIDTH):
        y = y + xp[:, tap:tap + L] * w[tap]
    return y


def block_diag_linear(x, w, b):
    Bsz, L, _ = x.shape
    xh = x.reshape(Bsz, L, LRU_HEADS, LRU_HEAD_DIM)
    y = jnp.einsum('blhi,hij->blhj', xh, w) + b
    return y.reshape(Bsz, L, LRU_WIDTH)


def rg_lru(x, wa, ba, wx, bx, lam):
    r = jax.nn.sigmoid(block_diag_linear(x, wa, ba).astype(jnp.float32))
    i = jax.nn.sigmoid(block_diag_linear(x, wx, bx).astype(jnp.float32))
    log_a = -LRU_C * r * jax.nn.softplus(-lam.astype(jnp.float32))
    a = jnp.exp(log_a)
    u = jnp.sqrt(-jnp.expm1(2.0 * log_a)) * (i * x.astype(jnp.float32))

    def combine(left, right):
        a1, b1 = left
        a2, b2 = right
        return a1 * a2, a2 * b1 + b2

    _, h = lax.associative_scan(combine, (a, u), axis=1)
    return h.astype(x.dtype)


def lru_dilated_mixer(h, cos, sin, w_in, conv_w, conv_b, wa, ba, wx, bx, lam, w_out):
    Bsz, L, _ = h.shape
    proj = h @ w_in
    x_br, y_br, q, k, v = jnp.split(proj, list(AB_SPLITS), axis=-1)
    rec = rg_lru(causal_depthwise_conv(x_br, conv_w, conv_b), wa, ba, wx, bx, lam) * jax.nn.gelu(y_br)

    def heads(t):
        return t.reshape(Bsz, L, B_HEADS, HEAD_DIM).transpose(0, 2, 1, 3)

    q = partial_rope(heads(q), cos, sin)
    k = partial_rope(heads(k), cos, sin)
    att = dilated_attention(q, k, heads(v)).transpose(0, 2, 1, 3).reshape(Bsz, L, B_WIDTH)
    return jnp.concatenate([rec, att], axis=-1) @ w_out


def swa_sink_mixer(h, cos, sin, w_qkv, b_qkv, sinks, w_out, b_out):
    Bsz, L, _ = h.shape
    proj = h @ w_qkv + b_qkv
    q, k, v = jnp.split(proj, [C_Q_WIDTH, C_Q_WIDTH + C_KV_WIDTH], axis=-1)
    q = partial_rope(q.reshape(Bsz, L, C_HEADS, HEAD_DIM).transpose(0, 2, 1, 3), cos, sin)
    k = partial_rope(k.reshape(Bsz, L, C_KV_HEADS, HEAD_DIM).transpose(0, 2, 1, 3), cos, sin)
    v = v.reshape(Bsz, L, C_KV_HEADS, HEAD_DIM).transpose(0, 2, 1, 3)
    o, _ = banded_attention(q, k, v, C_WINDOW - 1, sinks)
    return o.transpose(0, 2, 1, 3).reshape(Bsz, L, C_Q_WIDTH) @ w_out + b_out


def memory_cross_attention(h, mem_n, wq, wkv, wo):
    Bsz, L, _ = h.shape
    M = mem_n.shape[1]
    q = (h @ wq).reshape(Bsz, L, XA_HEADS, XA_HEAD_DIM)
    k, v = jnp.split(mem_n @ wkv, 2, axis=-1)
    k = k.reshape(Bsz, M, XA_HEADS, XA_HEAD_DIM)
    v = v.reshape(Bsz, M, XA_HEADS, XA_HEAD_DIM)
    s = jnp.einsum('blhd,bmhd->bhlm', q, k).astype(jnp.float32) * (XA_HEAD_DIM ** -0.5)
    p = jax.nn.softmax(s, axis=-1)
    o = jnp.einsum('bhlm,bmhd->blhd', p.astype(v.dtype), v).reshape(Bsz, L, XA_WIDTH)
    return o @ wo


def swiglu(h, w_gate_up, w_down):
    gate, up = jnp.split(h @ w_gate_up, 2, axis=-1)
    return (jax.nn.silu(gate) * up) @ w_down


def _fwd_setup_inputs(seed: int = 0) -> dict:
    key = jax.random.key(seed)
    ks = jax.random.split(key, 32)
    f32 = jnp.float32

    def nrm(k, shape, fan_in):
        return jax.random.normal(k, shape, f32) * (fan_in ** -0.5)

    def gain(k, shape):
        return 1.0 + 0.02 * jax.random.normal(k, shape, f32)

    def small(k, shape):
        return 0.01 * jax.random.normal(k, shape, f32)

    a_c = jax.random.uniform(ks[8], (N_EVEN, LRU_WIDTH), f32, 0.9, 0.999)
    a0 = a_c ** (1.0 / LRU_C)
    lru_lambda = jnp.log(a0) - jnp.log1p(-a0)

    return {
        "x": jax.random.normal(ks[0], (BATCH, SEQ, D_MODEL), f32),
        "mem": jax.random.normal(ks[1], (BATCH, N_MEM, D_MODEL), f32),
        "mix_norm": gain(ks[2], (DEPTH, D_MODEL)),
        "ab_w_in": nrm(ks[3], (N_EVEN, D_MODEL, AB_IN), D_MODEL),
        "lru_conv_w": nrm(ks[4], (N_EVEN, CONV_WIDTH, LRU_WIDTH), CONV_WIDTH),
        "lru_conv_b": small(ks[5], (N_EVEN, LRU_WIDTH)),
        "lru_wa": nrm(ks[6], (N_EVEN, LRU_HEADS, LRU_HEAD_DIM, LRU_HEAD_DIM), LRU_HEAD_DIM),
        "lru_ba": small(ks[7], (N_EVEN, LRU_HEADS, LRU_HEAD_DIM)),
        "lru_wx": nrm(ks[9], (N_EVEN, LRU_HEADS, LRU_HEAD_DIM, LRU_HEAD_DIM), LRU_HEAD_DIM),
        "lru_bx": small(ks[10], (N_EVEN, LRU_HEADS, LRU_HEAD_DIM)),
        "lru_lambda": lru_lambda,
        "ab_w_out": nrm(ks[11], (N_EVEN, AB_OUT, D_MODEL), AB_OUT),
        "c_w_qkv": nrm(ks[12], (N_ODD, D_MODEL, C_QKV), D_MODEL),
        "c_b_qkv": small(ks[13], (N_ODD, C_QKV)),
        "c_sinks": 0.5 * jax.random.normal(ks[14], (N_ODD, C_HEADS), f32),
        "c_w_out": nrm(ks[15], (N_ODD, C_Q_WIDTH, D_MODEL), C_Q_WIDTH),
        "c_b_out": small(ks[16], (N_ODD, D_MODEL)),
        "xa_norm": gain(ks[17], (DEPTH, D_MODEL)),
        "xa_mem_norm": gain(ks[18], (DEPTH, D_MODEL)),
        "xa_wq": nrm(ks[19], (DEPTH, D_MODEL, XA_WIDTH), D_MODEL),
        "xa_wkv": nrm(ks[20], (DEPTH, D_MODEL, 2 * XA_WIDTH), D_MODEL),
        "xa_wo": nrm(ks[21], (DEPTH, XA_WIDTH, D_MODEL), XA_WIDTH),
        "ffn_norm": gain(ks[22], (DEPTH, D_MODEL)),
        "ffn_w_gate_up": nrm(ks[23], (DEPTH, D_MODEL, 2 * D_FF), D_MODEL),
        "ffn_w_down": nrm(ks[24], (DEPTH, D_FF, D_MODEL), D_FF),
        "final_norm": gain(ks[25], (D_MODEL,)),
    }


def _fwd_reference(x, mem, mix_norm, ab_w_in, lru_conv_w, lru_conv_b, lru_wa, lru_ba, lru_wx, lru_bx,
              lru_lambda, ab_w_out, c_w_qkv, c_b_qkv, c_sinks, c_w_out, c_b_out, xa_norm,
              xa_mem_norm, xa_wq, xa_wkv, xa_wo, ffn_norm, ffn_w_gate_up, ffn_w_down, final_norm):
    cos, sin = rope_tables(x.shape[1])
    h = x
    for layer in range(DEPTH):
        j = layer // 2
        hn = rms_norm(h, mix_norm[layer])
        if layer % 2 == 0:
            h = h + lru_dilated_mixer(hn, cos, sin, ab_w_in[j], lru_conv_w[j], lru_conv_b[j],
                                      lru_wa[j], lru_ba[j], lru_wx[j], lru_bx[j], lru_lambda[j],
                                      ab_w_out[j])
        else:
            h = h + swa_sink_mixer(hn, cos, sin, c_w_qkv[j], c_b_qkv[j], c_sinks[j],
                                   c_w_out[j], c_b_out[j])
        h = h + memory_cross_attention(rms_norm(h, xa_norm[layer]), rms_norm(mem, xa_mem_norm[layer]),
                                       xa_wq[layer], xa_wkv[layer], xa_wo[layer])
        h = h + swiglu(rms_norm(h, ffn_norm[layer]), ffn_w_gate_up[layer], ffn_w_down[layer])
    return rms_norm(h, final_norm)


import jax as _jax
import jax.numpy as _jnp

TWIN_FORMAT = 'train_step'
FWD_PARAMS = ['x', 'mem', 'mix_norm', 'ab_w_in', 'lru_conv_w', 'lru_conv_b', 'lru_wa', 'lru_ba', 'lru_wx', 'lru_bx', 'lru_lambda', 'ab_w_out', 'c_w_qkv', 'c_b_qkv', 'c_sinks', 'c_w_out', 'c_b_out', 'xa_norm', 'xa_mem_norm', 'xa_wq', 'xa_wkv', 'xa_wo', 'ffn_norm', 'ffn_w_gate_up', 'ffn_w_down', 'final_norm']
TWIN_WEIGHTS = ['mix_norm', 'ab_w_in', 'lru_conv_w', 'lru_conv_b', 'lru_wa', 'lru_ba', 'lru_wx', 'lru_bx', 'lru_lambda', 'ab_w_out', 'c_w_qkv', 'c_b_qkv', 'c_sinks', 'c_w_out', 'c_b_out', 'xa_norm', 'xa_mem_norm', 'xa_wq', 'xa_wkv', 'xa_wo', 'ffn_norm', 'ffn_w_gate_up', 'ffn_w_down', 'final_norm']
TWIN_DIFF_INPUT = 'x'
TWIN_INPUTS = ['x', 'mem', 'mix_norm', 'ab_w_in', 'lru_conv_w', 'lru_conv_b', 'lru_wa', 'lru_ba', 'lru_wx', 'lru_bx', 'lru_lambda', 'ab_w_out', 'c_w_qkv', 'c_b_qkv', 'c_sinks', 'c_w_out', 'c_b_out', 'xa_norm', 'xa_mem_norm', 'xa_wq', 'xa_wkv', 'xa_wo', 'ffn_norm', 'ffn_w_gate_up', 'ffn_w_down', 'final_norm', 'loss_target', 'm_mix_norm', 'm_ab_w_in', 'm_lru_conv_w', 'm_lru_conv_b', 'm_lru_wa', 'm_lru_ba', 'm_lru_wx', 'm_lru_bx', 'm_lru_lambda', 'm_ab_w_out', 'm_c_w_qkv', 'm_c_b_qkv', 'm_c_sinks', 'm_c_w_out', 'm_c_b_out', 'm_xa_norm', 'm_xa_mem_norm', 'm_xa_wq', 'm_xa_wkv', 'm_xa_wo', 'm_ffn_norm', 'm_ffn_w_gate_up', 'm_ffn_w_down', 'm_final_norm', 'v_mix_norm', 'v_ab_w_in', 'v_lru_conv_w', 'v_lru_conv_b', 'v_lru_wa', 'v_lru_ba', 'v_lru_wx', 'v_lru_bx', 'v_lru_lambda', 'v_ab_w_out', 'v_c_w_qkv', 'v_c_b_qkv', 'v_c_sinks', 'v_c_w_out', 'v_c_b_out', 'v_xa_norm', 'v_xa_mem_norm', 'v_xa_wq', 'v_xa_wkv', 'v_xa_wo', 'v_ffn_norm', 'v_ffn_w_gate_up', 'v_ffn_w_down', 'v_final_norm']
TWIN_OUTPUTS = ['loss', 'grad_x', 'grad_mix_norm', 'grad_ab_w_in', 'grad_lru_conv_w', 'grad_lru_conv_b', 'grad_lru_wa', 'grad_lru_ba', 'grad_lru_wx', 'grad_lru_bx', 'grad_lru_lambda', 'grad_ab_w_out', 'grad_c_w_qkv', 'grad_c_b_qkv', 'grad_c_sinks', 'grad_c_w_out', 'grad_c_b_out', 'grad_xa_norm', 'grad_xa_mem_norm', 'grad_xa_wq', 'grad_xa_wkv', 'grad_xa_wo', 'grad_ffn_norm', 'grad_ffn_w_gate_up', 'grad_ffn_w_down', 'grad_final_norm', 'delta_mix_norm', 'delta_ab_w_in', 'delta_lru_conv_w', 'delta_lru_conv_b', 'delta_lru_wa', 'delta_lru_ba', 'delta_lru_wx', 'delta_lru_bx', 'delta_lru_lambda', 'delta_ab_w_out', 'delta_c_w_qkv', 'delta_c_b_qkv', 'delta_c_sinks', 'delta_c_w_out', 'delta_c_b_out', 'delta_xa_norm', 'delta_xa_mem_norm', 'delta_xa_wq', 'delta_xa_wkv', 'delta_xa_wo', 'delta_ffn_norm', 'delta_ffn_w_gate_up', 'delta_ffn_w_down', 'delta_final_norm', 'new_m_mix_norm', 'new_m_ab_w_in', 'new_m_lru_conv_w', 'new_m_lru_conv_b', 'new_m_lru_wa', 'new_m_lru_ba', 'new_m_lru_wx', 'new_m_lru_bx', 'new_m_lru_lambda', 'new_m_ab_w_out', 'new_m_c_w_qkv', 'new_m_c_b_qkv', 'new_m_c_sinks', 'new_m_c_w_out', 'new_m_c_b_out', 'new_m_xa_norm', 'new_m_xa_mem_norm', 'new_m_xa_wq', 'new_m_xa_wkv', 'new_m_xa_wo', 'new_m_ffn_norm', 'new_m_ffn_w_gate_up', 'new_m_ffn_w_down', 'new_m_final_norm', 'new_v_mix_norm', 'new_v_ab_w_in', 'new_v_lru_conv_w', 'new_v_lru_conv_b', 'new_v_lru_wa', 'new_v_lru_ba', 'new_v_lru_wx', 'new_v_lru_bx', 'new_v_lru_lambda', 'new_v_ab_w_out', 'new_v_c_w_qkv', 'new_v_c_b_qkv', 'new_v_c_sinks', 'new_v_c_w_out', 'new_v_c_b_out', 'new_v_xa_norm', 'new_v_xa_mem_norm', 'new_v_xa_wq', 'new_v_xa_wkv', 'new_v_xa_wo', 'new_v_ffn_norm', 'new_v_ffn_w_gate_up', 'new_v_ffn_w_down', 'new_v_final_norm']
TWIN_LEAF_KINDS = {'loss': 'loss', 'grad_x': 'grad_x', 'grad_mix_norm': 'grad_w', 'grad_ab_w_in': 'grad_w', 'grad_lru_conv_w': 'grad_w', 'grad_lru_conv_b': 'grad_w', 'grad_lru_wa': 'grad_w', 'grad_lru_ba': 'grad_w', 'grad_lru_wx': 'grad_w', 'grad_lru_bx': 'grad_w', 'grad_lru_lambda': 'grad_w', 'grad_ab_w_out': 'grad_w', 'grad_c_w_qkv': 'grad_w', 'grad_c_b_qkv': 'grad_w', 'grad_c_sinks': 'grad_w', 'grad_c_w_out': 'grad_w', 'grad_c_b_out': 'grad_w', 'grad_xa_norm': 'grad_w', 'grad_xa_mem_norm': 'grad_w', 'grad_xa_wq': 'grad_w', 'grad_xa_wkv': 'grad_w', 'grad_xa_wo': 'grad_w', 'grad_ffn_norm': 'grad_w', 'grad_ffn_w_gate_up': 'grad_w', 'grad_ffn_w_down': 'grad_w', 'grad_final_norm': 'grad_w', 'delta_mix_norm': 'delta_w', 'delta_ab_w_in': 'delta_w', 'delta_lru_conv_w': 'delta_w', 'delta_lru_conv_b': 'delta_w', 'delta_lru_wa': 'delta_w', 'delta_lru_ba': 'delta_w', 'delta_lru_wx': 'delta_w', 'delta_lru_bx': 'delta_w', 'delta_lru_lambda': 'delta_w', 'delta_ab_w_out': 'delta_w', 'delta_c_w_qkv': 'delta_w', 'delta_c_b_qkv': 'delta_w', 'delta_c_sinks': 'delta_w', 'delta_c_w_out': 'delta_w', 'delta_c_b_out': 'delta_w', 'delta_xa_norm': 'delta_w', 'delta_xa_mem_norm': 'delta_w', 'delta_xa_wq': 'delta_w', 'delta_xa_wkv': 'delta_w', 'delta_xa_wo': 'delta_w', 'delta_ffn_norm': 'delta_w', 'delta_ffn_w_gate_up': 'delta_w', 'delta_ffn_w_down': 'delta_w', 'delta_final_norm': 'delta_w', 'new_m_mix_norm': 'new_m', 'new_m_ab_w_in': 'new_m', 'new_m_lru_conv_w': 'new_m', 'new_m_lru_conv_b': 'new_m', 'new_m_lru_wa': 'new_m', 'new_m_lru_ba': 'new_m', 'new_m_lru_wx': 'new_m', 'new_m_lru_bx': 'new_m', 'new_m_lru_lambda': 'new_m', 'new_m_ab_w_out': 'new_m', 'new_m_c_w_qkv': 'new_m', 'new_m_c_b_qkv': 'new_m', 'new_m_c_sinks': 'new_m', 'new_m_c_w_out': 'new_m', 'new_m_c_b_out': 'new_m', 'new_m_xa_norm': 'new_m', 'new_m_xa_mem_norm': 'new_m', 'new_m_xa_wq': 'new_m', 'new_m_xa_wkv': 'new_m', 'new_m_xa_wo': 'new_m', 'new_m_ffn_norm': 'new_m', 'new_m_ffn_w_gate_up': 'new_m', 'new_m_ffn_w_down': 'new_m', 'new_m_final_norm': 'new_m', 'new_v_mix_norm': 'new_v', 'new_v_ab_w_in': 'new_v', 'new_v_lru_conv_w': 'new_v', 'new_v_lru_conv_b': 'new_v', 'new_v_lru_wa': 'new_v', 'new_v_lru_ba': 'new_v', 'new_v_lru_wx': 'new_v', 'new_v_lru_bx': 'new_v', 'new_v_lru_lambda': 'new_v', 'new_v_ab_w_out': 'new_v', 'new_v_c_w_qkv': 'new_v', 'new_v_c_b_qkv': 'new_v', 'new_v_c_sinks': 'new_v', 'new_v_c_w_out': 'new_v', 'new_v_c_b_out': 'new_v', 'new_v_xa_norm': 'new_v', 'new_v_xa_mem_norm': 'new_v', 'new_v_xa_wq': 'new_v', 'new_v_xa_wkv': 'new_v', 'new_v_xa_wo': 'new_v', 'new_v_ffn_norm': 'new_v', 'new_v_ffn_w_gate_up': 'new_v', 'new_v_ffn_w_down': 'new_v', 'new_v_final_norm': 'new_v'}


def _forward(args):
    return _fwd_reference(*[args[k] for k in FWD_PARAMS])


def _output_shape():
    def fwd():
        inp = _fwd_setup_inputs(0)
        return _fwd_reference(*[inp[k] for k in FWD_PARAMS])
    out = _jax.eval_shape(fwd)
    return out.shape, out.dtype

N_MICROBATCH = 1
ADAM_LR = 0.001
ADAM_B1 = 0.9
ADAM_B2 = 0.999
ADAM_EPS = 1e-08
ADAM_WD = 0.01
ADAM_STEP = 10
PER_EXAMPLE_BATCH_AXIS = {'x': 0, 'mem': 0, 'loss_target': 0}
SHARED_INPUTS = []
_WEIGHT_DTYPES = {'mix_norm': _jnp.float32, 'ab_w_in': _jnp.float32, 'lru_conv_w': _jnp.float32, 'lru_conv_b': _jnp.float32, 'lru_wa': _jnp.float32, 'lru_ba': _jnp.float32, 'lru_wx': _jnp.float32, 'lru_bx': _jnp.float32, 'lru_lambda': _jnp.float32, 'ab_w_out': _jnp.float32, 'c_w_qkv': _jnp.float32, 'c_b_qkv': _jnp.float32, 'c_sinks': _jnp.float32, 'c_w_out': _jnp.float32, 'c_b_out': _jnp.float32, 'xa_norm': _jnp.float32, 'xa_mem_norm': _jnp.float32, 'xa_wq': _jnp.float32, 'xa_wkv': _jnp.float32, 'xa_wo': _jnp.float32, 'ffn_norm': _jnp.float32, 'ffn_w_gate_up': _jnp.float32, 'ffn_w_down': _jnp.float32, 'final_norm': _jnp.float32}
MOMENT_SCALE = {'mix_norm': 7.971386e-02, 'ab_w_in': 5.290144e-02, 'lru_conv_w': 7.692543e-02, 'lru_conv_b': 8.312301e-01, 'lru_wa': 2.118709e-02, 'lru_ba': 1.684379e-02, 'lru_wx': 3.751621e-02, 'lru_bx': 2.694011e-02, 'lru_lambda': 3.118253e-02, 'ab_w_out': 6.799515e-02, 'c_w_qkv': 4.759507e-02, 'c_b_qkv': 1.569676e-01, 'c_sinks': 2.348155e-02, 'c_w_out': 3.592483e-02, 'c_b_out': 1.654158e-01, 'xa_norm': 2.017192e-02, 'xa_mem_norm': 3.025784e-02, 'xa_wq': 2.748827e-02, 'xa_wkv': 2.849555e-02, 'xa_wo': 2.060179e-02, 'ffn_norm': 1.296229e-01, 'ffn_w_gate_up': 5.377942e-02, 'ffn_w_down': 8.754458e-02, 'final_norm': 3.200791e+01}


def _to_microbatches(a, axis):
    t = _jnp.moveaxis(a, axis, 0)
    t = t.reshape((N_MICROBATCH, t.shape[0] // N_MICROBATCH) + t.shape[1:])
    return _jnp.moveaxis(t, 1, axis + 1)


def setup_inputs(seed: int = 0) -> dict:
    inp = _fwd_setup_inputs(seed)
    key = _jax.random.fold_in(_jax.random.key(seed), 7919)
    shape, _ = _output_shape()
    out = dict(inp)
    out["loss_target"] = _jax.random.normal(_jax.random.fold_in(key, 0), shape, _jnp.float32)
    for i, name in enumerate(TWIN_WEIGHTS):
        w = inp[name].astype(_jnp.float32)
        if MOMENT_SCALE is None:
            s = _jnp.sqrt(_jnp.mean(_jnp.square(w)) + 1e-30)
        else:
            s = MOMENT_SCALE[name]
        km, kv = _jax.random.split(_jax.random.fold_in(key, i + 1))
        out[name] = w
        out["m_" + name] = s * _jax.random.normal(km, w.shape, _jnp.float32)
        out["v_" + name] = (s * s) * _jax.random.uniform(kv, w.shape, _jnp.float32, 0.5, 1.5)
    if N_MICROBATCH > 1:
        for name, axis in PER_EXAMPLE_BATCH_AXIS.items():
            out[name] = _to_microbatches(out[name], axis)
    return {'x': out['x'], 'mem': out['mem'], 'mix_norm': out['mix_norm'], 'ab_w_in': out['ab_w_in'], 'lru_conv_w': out['lru_conv_w'], 'lru_conv_b': out['lru_conv_b'], 'lru_wa': out['lru_wa'], 'lru_ba': out['lru_ba'], 'lru_wx': out['lru_wx'], 'lru_bx': out['lru_bx'], 'lru_lambda': out['lru_lambda'], 'ab_w_out': out['ab_w_out'], 'c_w_qkv': out['c_w_qkv'], 'c_b_qkv': out['c_b_qkv'], 'c_sinks': out['c_sinks'], 'c_w_out': out['c_w_out'], 'c_b_out': out['c_b_out'], 'xa_norm': out['xa_norm'], 'xa_mem_norm': out['xa_mem_norm'], 'xa_wq': out['xa_wq'], 'xa_wkv': out['xa_wkv'], 'xa_wo': out['xa_wo'], 'ffn_norm': out['ffn_norm'], 'ffn_w_gate_up': out['ffn_w_gate_up'], 'ffn_w_down': out['ffn_w_down'], 'final_norm': out['final_norm'], 'loss_target': out['loss_target'], 'm_mix_norm': out['m_mix_norm'], 'm_ab_w_in': out['m_ab_w_in'], 'm_lru_conv_w': out['m_lru_conv_w'], 'm_lru_conv_b': out['m_lru_conv_b'], 'm_lru_wa': out['m_lru_wa'], 'm_lru_ba': out['m_lru_ba'], 'm_lru_wx': out['m_lru_wx'], 'm_lru_bx': out['m_lru_bx'], 'm_lru_lambda': out['m_lru_lambda'], 'm_ab_w_out': out['m_ab_w_out'], 'm_c_w_qkv': out['m_c_w_qkv'], 'm_c_b_qkv': out['m_c_b_qkv'], 'm_c_sinks': out['m_c_sinks'], 'm_c_w_out': out['m_c_w_out'], 'm_c_b_out': out['m_c_b_out'], 'm_xa_norm': out['m_xa_norm'], 'm_xa_mem_norm': out['m_xa_mem_norm'], 'm_xa_wq': out['m_xa_wq'], 'm_xa_wkv': out['m_xa_wkv'], 'm_xa_wo': out['m_xa_wo'], 'm_ffn_norm': out['m_ffn_norm'], 'm_ffn_w_gate_up': out['m_ffn_w_gate_up'], 'm_ffn_w_down': out['m_ffn_w_down'], 'm_final_norm': out['m_final_norm'], 'v_mix_norm': out['v_mix_norm'], 'v_ab_w_in': out['v_ab_w_in'], 'v_lru_conv_w': out['v_lru_conv_w'], 'v_lru_conv_b': out['v_lru_conv_b'], 'v_lru_wa': out['v_lru_wa'], 'v_lru_ba': out['v_lru_ba'], 'v_lru_wx': out['v_lru_wx'], 'v_lru_bx': out['v_lru_bx'], 'v_lru_lambda': out['v_lru_lambda'], 'v_ab_w_out': out['v_ab_w_out'], 'v_c_w_qkv': out['v_c_w_qkv'], 'v_c_b_qkv': out['v_c_b_qkv'], 'v_c_sinks': out['v_c_sinks'], 'v_c_w_out': out['v_c_w_out'], 'v_c_b_out': out['v_c_b_out'], 'v_xa_norm': out['v_xa_norm'], 'v_xa_mem_norm': out['v_xa_mem_norm'], 'v_xa_wq': out['v_xa_wq'], 'v_xa_wkv': out['v_xa_wkv'], 'v_xa_wo': out['v_xa_wo'], 'v_ffn_norm': out['v_ffn_norm'], 'v_ffn_w_gate_up': out['v_ffn_w_gate_up'], 'v_ffn_w_down': out['v_ffn_w_down'], 'v_final_norm': out['v_final_norm']}


def _loss(weights, diff, rest, loss_target):
    with _jax.named_scope("forward"):
        args = {**rest, TWIN_DIFF_INPUT: diff, **{k: w.astype(_WEIGHT_DTYPES[k]) for k, w in weights.items()}}
        y = _forward(args)
    with _jax.named_scope("loss_head"):
        err = _jnp.square(y.astype(_jnp.float32) - loss_target)
        return 0.5 * _jnp.sum(_jnp.mean(err, axis=-1)) if err.ndim else 0.5 * err


def _adamw(w, g, m, v):
    m = ADAM_B1 * m + (1.0 - ADAM_B1) * g
    v = ADAM_B2 * v + (1.0 - ADAM_B2) * _jnp.square(g)
    m_hat = m / (1.0 - ADAM_B1 ** ADAM_STEP)
    v_hat = v / (1.0 - ADAM_B2 ** ADAM_STEP)
    delta = -ADAM_LR * (m_hat / (_jnp.sqrt(v_hat) + ADAM_EPS) + ADAM_WD * w)
    return delta, m, v


def reference(x, mem, mix_norm, ab_w_in, lru_conv_w, lru_conv_b, lru_wa, lru_ba, lru_wx, lru_bx, lru_lambda, ab_w_out, c_w_qkv, c_b_qkv, c_sinks, c_w_out, c_b_out, xa_norm, xa_mem_norm, xa_wq, xa_wkv, xa_wo, ffn_norm, ffn_w_gate_up, ffn_w_down, final_norm, loss_target, m_mix_norm, m_ab_w_in, m_lru_conv_w, m_lru_conv_b, m_lru_wa, m_lru_ba, m_lru_wx, m_lru_bx, m_lru_lambda, m_ab_w_out, m_c_w_qkv, m_c_b_qkv, m_c_sinks, m_c_w_out, m_c_b_out, m_xa_norm, m_xa_mem_norm, m_xa_wq, m_xa_wkv, m_xa_wo, m_ffn_norm, m_ffn_w_gate_up, m_ffn_w_down, m_final_norm, v_mix_norm, v_ab_w_in, v_lru_conv_w, v_lru_conv_b, v_lru_wa, v_lru_ba, v_lru_wx, v_lru_bx, v_lru_lambda, v_ab_w_out, v_c_w_qkv, v_c_b_qkv, v_c_sinks, v_c_w_out, v_c_b_out, v_xa_norm, v_xa_mem_norm, v_xa_wq, v_xa_wkv, v_xa_wo, v_ffn_norm, v_ffn_w_gate_up, v_ffn_w_down, v_final_norm):
    given = dict(x=x, mem=mem, mix_norm=mix_norm, ab_w_in=ab_w_in, lru_conv_w=lru_conv_w, lru_conv_b=lru_conv_b, lru_wa=lru_wa, lru_ba=lru_ba, lru_wx=lru_wx, lru_bx=lru_bx, lru_lambda=lru_lambda, ab_w_out=ab_w_out, c_w_qkv=c_w_qkv, c_b_qkv=c_b_qkv, c_sinks=c_sinks, c_w_out=c_w_out, c_b_out=c_b_out, xa_norm=xa_norm, xa_mem_norm=xa_mem_norm, xa_wq=xa_wq, xa_wkv=xa_wkv, xa_wo=xa_wo, ffn_norm=ffn_norm, ffn_w_gate_up=ffn_w_gate_up, ffn_w_down=ffn_w_down, final_norm=final_norm, loss_target=loss_target, m_mix_norm=m_mix_norm, m_ab_w_in=m_ab_w_in, m_lru_conv_w=m_lru_conv_w, m_lru_conv_b=m_lru_conv_b, m_lru_wa=m_lru_wa, m_lru_ba=m_lru_ba, m_lru_wx=m_lru_wx, m_lru_bx=m_lru_bx, m_lru_lambda=m_lru_lambda, m_ab_w_out=m_ab_w_out, m_c_w_qkv=m_c_w_qkv, m_c_b_qkv=m_c_b_qkv, m_c_sinks=m_c_sinks, m_c_w_out=m_c_w_out, m_c_b_out=m_c_b_out, m_xa_norm=m_xa_norm, m_xa_mem_norm=m_xa_mem_norm, m_xa_wq=m_xa_wq, m_xa_wkv=m_xa_wkv, m_xa_wo=m_xa_wo, m_ffn_norm=m_ffn_norm, m_ffn_w_gate_up=m_ffn_w_gate_up, m_ffn_w_down=m_ffn_w_down, m_final_norm=m_final_norm, v_mix_norm=v_mix_norm, v_ab_w_in=v_ab_w_in, v_lru_conv_w=v_lru_conv_w, v_lru_conv_b=v_lru_conv_b, v_lru_wa=v_lru_wa, v_lru_ba=v_lru_ba, v_lru_wx=v_lru_wx, v_lru_bx=v_lru_bx, v_lru_lambda=v_lru_lambda, v_ab_w_out=v_ab_w_out, v_c_w_qkv=v_c_w_qkv, v_c_b_qkv=v_c_b_qkv, v_c_sinks=v_c_sinks, v_c_w_out=v_c_w_out, v_c_b_out=v_c_b_out, v_xa_norm=v_xa_norm, v_xa_mem_norm=v_xa_mem_norm, v_xa_wq=v_xa_wq, v_xa_wkv=v_xa_wkv, v_xa_wo=v_xa_wo, v_ffn_norm=v_ffn_norm, v_ffn_w_gate_up=v_ffn_w_gate_up, v_ffn_w_down=v_ffn_w_down, v_final_norm=v_final_norm)
    weights = {n: given[n] for n in TWIN_WEIGHTS}
    shared = {n: given[n] for n in SHARED_INPUTS}
    per_example = {n: given[n] for n in ['x', 'mem']}
    grad_fn = _jax.value_and_grad(_loss, argnums=(0, 1))

    def one_microbatch(ex, loss_target):
        ex = dict(ex)
        diff = ex.pop(TWIN_DIFF_INPUT)
        return grad_fn(weights, diff, {**shared, **ex}, loss_target)

    if N_MICROBATCH == 1:
        loss, (grad_w, grad_x) = one_microbatch(per_example, given["loss_target"])
    else:
        def body(carry, xs):
            loss_sum, grad_sum = carry
            l_k, (gw_k, gx_k) = one_microbatch(xs[0], xs[1])
            with _jax.named_scope("update"):
                return (loss_sum + l_k, _jax.tree.map(_jnp.add, grad_sum, gw_k)), gx_k

        init = (_jnp.zeros((), _jnp.float32), _jax.tree.map(_jnp.zeros_like, weights))
        (loss, grad_w), grad_x = _jax.lax.scan(body, init, (per_example, given["loss_target"]))
    with _jax.named_scope("update"):
        delta_w, new_m, new_v = {}, {}, {}
        for n in TWIN_WEIGHTS:
            delta_w[n], new_m[n], new_v[n] = _adamw(weights[n], grad_w[n], given["m_" + n], given["v_" + n])
    return (loss, grad_x, *[grad_w[n] for n in TWIN_WEIGHTS], *[delta_w[n] for n in TWIN_WEIGHTS],
            *[new_m[n] for n in TWIN_WEIGHTS], *[new_v[n] for n in TWIN_WEIGHTS])
```

```python
import functools
import math

import jax
import jax.numpy as jnp
from jax import lax
from jax.experimental import pallas as pl
from jax.experimental.pallas import tpu as pltpu

F32 = jnp.float32
BF16 = jnp.bfloat16
SDS = jax.ShapeDtypeStruct
BS = pl.BlockSpec

N_DEV = 8
NORM_EPS = 1e-6
ROPE_THETA = 500000.0
HEAD_DIM = 64
ROT_DIM = 16
ATT_BLOCK = 128
LRU_C = 8.0
LRU_HEADS = 4
DILATIONS = (1, 4, 16)
B_HEADS = 8
C_HEADS = 16
C_KV_HEADS = 2
XA_HEADS = 4
XA_HEAD_DIM = 128
NEG = -1e30
ADAM_LR, ADAM_B1, ADAM_B2, ADAM_EPS, ADAM_WD, ADAM_STEP = 0.001, 0.9, 0.999, 1e-08, 0.01, 10
LANES = 128
VMEM_LIMIT = 48 * 1024 * 1024

WEIGHT_NAMES = ['mix_norm', 'ab_w_in', 'lru_conv_w', 'lru_conv_b', 'lru_wa', 'lru_ba', 'lru_wx', 'lru_bx',
                'lru_lambda', 'ab_w_out', 'c_w_qkv', 'c_b_qkv', 'c_sinks', 'c_w_out', 'c_b_out', 'xa_norm',
                'xa_mem_norm', 'xa_wq', 'xa_wkv', 'xa_wo', 'ffn_norm', 'ffn_w_gate_up', 'ffn_w_down', 'final_norm']
SHARD_AXIS = {'ab_w_in': 2, 'lru_conv_w': 2, 'lru_wa': 2, 'lru_ba': 2, 'lru_wx': 2, 'lru_bx': 2, 'ab_w_out': 1,
              'c_w_qkv': 2, 'c_b_qkv': 1, 'c_w_out': 1, 'c_b_out': 1, 'xa_wq': 1, 'xa_wkv': 1, 'xa_wo': 2,
              'ffn_w_gate_up': 2, 'ffn_w_down': 1}
BIG = ['ab_w_in', 'lru_wa', 'lru_wx', 'ab_w_out', 'c_w_qkv', 'c_w_out', 'xa_wq', 'xa_wkv', 'xa_wo',
       'ffn_w_gate_up', 'ffn_w_down']
SMALL = ['lru_conv_w', 'lru_ba', 'lru_bx', 'c_b_qkv', 'c_b_out']
SHARDED = BIG + SMALL
REPLICATED = [n for n in WEIGHT_NAMES if n not in SHARD_AXIS]
PACK_QUANTUM = 2048


def _pcall(body, **kw):
    return pl.pallas_call(body, **kw)


def _cparams(sem=None):
    return pltpu.CompilerParams(dimension_semantics=sem, vmem_limit_bytes=VMEM_LIMIT)


def _tile(n, target, mult=LANES):
    if n <= target:
        return n
    t = (target // mult) * mult
    while t >= mult:
        if n % t == 0:
            return t
        t -= mult
    return n


def _sigmoid(x):
    return 1.0 / (1.0 + jnp.exp(-x))


def _expm1(x):
    small = x * (1.0 + x * (0.5 + x * (1.0 / 6.0 + x * (1.0 / 24.0))))
    return jnp.where(jnp.abs(x) < 0.03, small, jnp.exp(x) - 1.0)


_GELU_C = math.sqrt(2.0 / math.pi)


def _gelu_parts(y):
    y2 = y * y
    th = jnp.tanh(_GELU_C * (y + 0.044715 * y * y2))
    g = 0.5 * y * (1.0 + th)
    dg = 0.5 * (1.0 + th) + 0.5 * y * (1.0 - th * th) * _GELU_C * (1.0 + 3.0 * 0.044715 * y2)
    return g, dg


def _eye(n):
    return lax.broadcasted_iota(jnp.int32, (n, n), 0) == lax.broadcasted_iota(jnp.int32, (n, n), 1)


def _col_to_row(c):
    n = c.shape[1]
    return jnp.sum(jnp.where(_eye(n)[None], c, 0.0), axis=1, keepdims=True)


def _row_to_col(r):
    n = r.shape[2]
    return jnp.sum(jnp.where(_eye(n)[None], r, 0.0), axis=2, keepdims=True)


def _mm(a, b, *, ta=False, tb=False, bias=None, add=None, name):
    M, K = (a.shape[1], a.shape[0]) if ta else a.shape
    N = b.shape[0] if tb else b.shape[1]
    tm, tn, tk = _tile(M, 512), _tile(N, 512), _tile(K, 2048)
    nk = K // tk
    dn = (((0 if ta else 1,), (1 if tb else 0,)), ((), ()))

    def body(*refs):
        a_ref, b_ref = refs[0], refs[1]
        pos = 2
        bias_ref = add_ref = None
        if bias is not None:
            bias_ref = refs[pos]
            pos += 1
        if add is not None:
            add_ref = refs[pos]
            pos += 1
        o_ref, acc_ref = refs[pos], refs[pos + 1]
        k = pl.program_id(2)

        @pl.when(k == 0)
        def _():
            acc_ref[...] = jnp.zeros_like(acc_ref)

        acc_ref[...] += lax.dot_general(a_ref[...].astype(BF16), b_ref[...].astype(BF16), dn,
                                        preferred_element_type=F32)

        @pl.when(k == nk - 1)
        def _():
            r = acc_ref[...]
            if bias_ref is not None:
                r = r + bias_ref[...]
            if add_ref is not None:
                r = r + add_ref[...]
            o_ref[...] = r

    in_specs = [BS((tk, tm), lambda i, j, k: (k, i)) if ta else BS((tm, tk), lambda i, j, k: (i, k)),
                BS((tn, tk), lambda i, j, k: (j, k)) if tb else BS((tk, tn), lambda i, j, k: (k, j))]
    args = [a, b]
    if bias is not None:
        in_specs.append(BS((1, tn), lambda i, j, k: (0, j)))
        args.append(bias.reshape(1, N))
    if add is not None:
        in_specs.append(BS((tm, tn), lambda i, j, k: (i, j)))
        args.append(add)
    return _pcall(body, name=name, grid=(M // tm, N // tn, nk), in_specs=in_specs,
                  out_specs=BS((tm, tn), lambda i, j, k: (i, j)), out_shape=SDS((M, N), F32),
                  scratch_shapes=[pltpu.VMEM((tm, tn), F32)],
                  compiler_params=_cparams(("parallel", "parallel", "arbitrary")))(*args)


def _colsum(x, name):
    T, N = x.shape
    tt = _tile(T, 512, 8)

    def body(x_ref, o_ref):
        @pl.when(pl.program_id(0) == 0)
        def _():
            o_ref[...] = jnp.zeros_like(o_ref)

        o_ref[...] += jnp.sum(x_ref[...], axis=0, keepdims=True)

    return _pcall(body, name=name, grid=(T // tt,), in_specs=[BS((tt, N), lambda i: (i, 0))],
                  out_specs=BS((1, N), lambda i: (0, 0)), out_shape=SDS((1, N), F32),
                  compiler_params=_cparams(("arbitrary",)))(x)


@functools.partial(jax.custom_vjp, nondiff_argnums=(4,))
def linear(a, w, bias, add, name):
    return _mm(a, w.astype(BF16), bias=bias, add=add, name=name)


def _linear_fwd(a, w, bias, add, name):
    wb = w.astype(BF16)
    return _mm(a, wb, bias=bias, add=add, name=name), (a, wb, bias is not None, add is not None)


def _linear_bwd(name, res, g):
    a, wb, has_bias, has_add = res
    da = _mm(g, wb, tb=True, name=name + "_da")
    dw = _mm(a, g, ta=True, name=name + "_dw")
    dbias = _colsum(g, name + "_db").reshape(-1) if has_bias else None
    return da, dw, dbias, (g if has_add else None)


linear.defvjp(_linear_fwd, _linear_bwd)


def _rms_fwd_call(x, g, name):
    T, Dm = x.shape
    tt = _tile(T, 512, 8)

    def body(x_ref, g_ref, o_ref):
        xv = x_ref[...]
        r = lax.rsqrt(jnp.mean(xv * xv, axis=-1, keepdims=True) + NORM_EPS)
        o_ref[...] = (xv * r) * g_ref[...]

    return _pcall(body, name=name, grid=(T // tt,),
                  in_specs=[BS((tt, Dm), lambda i: (i, 0)), BS((1, Dm), lambda i: (0, 0))],
                  out_specs=BS((tt, Dm), lambda i: (i, 0)), out_shape=SDS((T, Dm), F32),
                  compiler_params=_cparams(("parallel",)))(x, g.reshape(1, Dm))


def _rms_bwd_call(x, g, dy, name):
    T, Dm = x.shape
    tt = _tile(T, 512, 8)

    def body(x_ref, g_ref, dy_ref, dx_ref, dg_ref):
        xv = x_ref[...]
        r = lax.rsqrt(jnp.mean(xv * xv, axis=-1, keepdims=True) + NORM_EPS)
        xh = xv * r
        dy = dy_ref[...]
        dyg = dy * g_ref[...]
        dx_ref[...] = r * (dyg - xh * jnp.mean(dyg * xh, axis=-1, keepdims=True))

        @pl.when(pl.program_id(0) == 0)
        def _():
            dg_ref[...] = jnp.zeros_like(dg_ref)

        dg_ref[...] += jnp.sum(dy * xh, axis=0, keepdims=True)

    row = BS((tt, Dm), lambda i: (i, 0))
    vec = BS((1, Dm), lambda i: (0, 0))
    return _pcall(body, name=name, grid=(T // tt,), in_specs=[row, vec, row], out_specs=[row, vec],
                  out_shape=[SDS((T, Dm), F32), SDS((1, Dm), F32)],
                  compiler_params=_cparams(("arbitrary",)))(x, g.reshape(1, Dm), dy)


@functools.partial(jax.custom_vjp, nondiff_argnums=(2,))
def rmsnorm(x, g, name):
    return _rms_fwd_call(x, g, name)


def _rmsnorm_fwd(x, g, name):
    return _rms_fwd_call(x, g, name), (x, g)


def _rmsnorm_bwd(name, res, dy):
    x, g = res
    dx, dg = _rms_bwd_call(x, g, dy, name + "_bwd")
    return dx, dg.reshape(g.shape)


rmsnorm.defvjp(_rmsnorm_fwd, _rmsnorm_bwd)


def _rope_tables(T):
    half = ROT_DIM // 2
    inv = ROPE_THETA ** (-jnp.arange(0, ROT_DIM, 2, dtype=F32) / ROT_DIM)
    ang = jnp.arange(T, dtype=F32)[:, None] * inv[None, :]
    cos, sin = jnp.cos(ang), jnp.sin(ang)
    ones = jnp.ones((T, HEAD_DIM - ROT_DIM), F32)
    zeros = jnp.zeros((T, HEAD_DIM - ROT_DIM), F32)
    zh = jnp.zeros((T, half), F32)
    c = jnp.concatenate([cos, cos, ones], axis=1)
    sa = jnp.concatenate([zh, sin, zeros], axis=1)
    sb = jnp.concatenate([-sin, zh, zeros], axis=1)
    two = lambda t: jnp.concatenate([t, t], axis=1)
    return two(c), two(sa), two(sb)


def _rope_call(x, tabs, inverse, name):
    T, W = x.shape
    tt = _tile(T, 512, 8)
    reps = W // LANES
    half = ROT_DIM // 2

    def body(x_ref, c_ref, sa_ref, sb_ref, o_ref):
        xv = x_ref[...]
        c = jnp.tile(c_ref[...], (1, reps))
        sa = jnp.tile(sa_ref[...], (1, reps))
        sb = jnp.tile(sb_ref[...], (1, reps))
        if not inverse:
            o_ref[...] = xv * c + pltpu.roll(xv, half, axis=1) * sa + pltpu.roll(xv, W - half, axis=1) * sb
        else:
            o_ref[...] = xv * c + pltpu.roll(xv * sa, W - half, axis=1) + pltpu.roll(xv * sb, half, axis=1)

    row = BS((tt, W), lambda i: (i, 0))
    tab = BS((tt, LANES), lambda i: (i, 0))
    return _pcall(body, name=name, grid=(T // tt,), in_specs=[row, tab, tab, tab], out_specs=row,
                  out_shape=SDS((T, W), F32), compiler_params=_cparams(("parallel",)))(x, *tabs)


@functools.partial(jax.custom_vjp, nondiff_argnums=(2,))
def rope(x, tabs, name):
    return _rope_call(x, tabs, False, name)


def _rope_fwd(x, tabs, name):
    return _rope_call(x, tabs, False, name), tabs


def _rope_bwd(name, tabs, dy):
    return _rope_call(dy, tabs, True, name + "_bwd"), jax.tree.map(jnp.zeros_like, tabs)


rope.defvjp(_rope_fwd, _rope_bwd)


def _conv_fwd_call(x, w, b, name):
    T, C = x.shape
    tt = _tile(T, 512, 8)
    per = tt // 8

    def body(x_ref, halo_ref, w_ref, b_ref, o_ref):
        i = pl.program_id(0)
        halo = jnp.where(i > 0, halo_ref[...], 0.0)
        e = jnp.concatenate([halo, x_ref[...]], axis=0)
        acc = b_ref[...]
        for k in (3, 2, 1):
            acc = acc + pltpu.roll(e, k, axis=0)[8:, :] * w_ref[3 - k:4 - k, :]
        o_ref[...] = acc + x_ref[...] * w_ref[3:4, :]

    return _pcall(body, name=name, grid=(T // tt,),
                  in_specs=[BS((tt, C), lambda i: (i, 0)), BS((8, C), lambda i: (jnp.maximum(i * per - 1, 0), 0)),
                            BS((4, C), lambda i: (0, 0)), BS((1, C), lambda i: (0, 0))],
                  out_specs=BS((tt, C), lambda i: (i, 0)), out_shape=SDS((T, C), F32),
                  compiler_params=_cparams(("parallel",)))(x, x, w, b.reshape(1, C))


def _conv_bwd_call(x, w, dy, name):
    T, C = x.shape
    tt = _tile(T, 512, 8)
    per = tt // 8
    nt = T // tt

    def body(x_ref, halo_ref, w_ref, dy_ref, nxt_ref, dx_ref, dwb_ref):
        i = pl.program_id(0)
        halo = jnp.where(i > 0, halo_ref[...], 0.0)
        e = jnp.concatenate([halo, x_ref[...]], axis=0)
        dy = dy_ref[...]
        nxt = jnp.where(i < nt - 1, nxt_ref[...], 0.0)
        f = jnp.concatenate([dy, nxt], axis=0)
        dx = dy * w_ref[3:4, :]
        rows = [None] * 4
        rows[3] = jnp.sum(dy * x_ref[...], axis=0, keepdims=True)
        for k in (1, 2, 3):
            dx = dx + pltpu.roll(f, tt + 8 - k, axis=0)[:tt, :] * w_ref[3 - k:4 - k, :]
            rows[3 - k] = jnp.sum(dy * pltpu.roll(e, k, axis=0)[8:, :], axis=0, keepdims=True)
        dx_ref[...] = dx
        upd = jnp.concatenate(rows + [jnp.sum(dy, axis=0, keepdims=True), jnp.zeros((3, C), F32)], axis=0)

        @pl.when(i == 0)
        def _():
            dwb_ref[...] = jnp.zeros_like(dwb_ref)

        dwb_ref[...] += upd

    row = BS((tt, C), lambda i: (i, 0))
    return _pcall(body, name=name, grid=(nt,),
                  in_specs=[row, BS((8, C), lambda i: (jnp.maximum(i * per - 1, 0), 0)), BS((4, C), lambda i: (0, 0)),
                            row, BS((8, C), lambda i: (jnp.minimum((i + 1) * per, T // 8 - 1), 0))],
                  out_specs=[row, BS((8, C), lambda i: (0, 0))],
                  out_shape=[SDS((T, C), F32), SDS((8, C), F32)],
                  compiler_params=_cparams(("arbitrary",)))(x, x, w, dy, dy)


@functools.partial(jax.custom_vjp, nondiff_argnums=(3,))
def conv4(x, w, b, name):
    return _conv_fwd_call(x, w, b, name)


def _conv4_fwd(x, w, b, name):
    return _conv_fwd_call(x, w, b, name), (x, w)


def _conv4_bwd(name, res, dy):
    x, w = res
    dx, dwb = _conv_bwd_call(x, w, dy, name + "_bwd")
    return dx, dwb[0:4], dwb[4]


conv4.defvjp(_conv4_fwd, _conv4_bwd)


def _gates_fwd_call(xc, wa, ba, wx, bx, name):
    T, C = xc.shape
    hd = C // LRU_HEADS
    tt = _tile(T, 512, 8)

    def body(x_ref, wa_ref, ba_ref, wx_ref, bx_ref, ga_ref, gx_ref):
        xb = x_ref[...].astype(BF16)
        ga_ref[...] = jnp.dot(xb, wa_ref[0].astype(BF16), preferred_element_type=F32) + ba_ref[...]
        gx_ref[...] = jnp.dot(xb, wx_ref[0].astype(BF16), preferred_element_type=F32) + bx_ref[...]

    blk = BS((tt, hd), lambda i, h: (i, h))
    wsp = BS((1, hd, hd), lambda i, h: (h, 0, 0))
    bsp = BS((1, hd), lambda i, h: (0, h))
    return _pcall(body, name=name, grid=(T // tt, LRU_HEADS), in_specs=[blk, wsp, bsp, wsp, bsp],
                  out_specs=[blk, blk], out_shape=[SDS((T, C), F32)] * 2,
                  compiler_params=_cparams(("parallel", "parallel")))(xc, wa, ba.reshape(1, C), wx, bx.reshape(1, C))


def _gates_bwd_x_call(dga, dgx, wa, wx, name):
    T, C = dga.shape
    hd = C // LRU_HEADS
    tt = _tile(T, 512, 8)
    dn = (((1,), (1,)), ((), ()))

    def body(da_ref, dx_ref, wa_ref, wx_ref, o_ref):
        o_ref[...] = (lax.dot_general(da_ref[...].astype(BF16), wa_ref[0].astype(BF16), dn, preferred_element_type=F32)
                      + lax.dot_general(dx_ref[...].astype(BF16), wx_ref[0].astype(BF16), dn, preferred_element_type=F32))

    blk = BS((tt, hd), lambda i, h: (i, h))
    wsp = BS((1, hd, hd), lambda i, h: (h, 0, 0))
    return _pcall(body, name=name, grid=(T // tt, LRU_HEADS), in_specs=[blk, blk, wsp, wsp], out_specs=blk,
                  out_shape=SDS((T, C), F32), compiler_params=_cparams(("parallel", "parallel")))(dga, dgx, wa, wx)


def _gates_bwd_w_call(xc, dga, dgx, name):
    T, C = xc.shape
    hd = C // LRU_HEADS
    tt = _tile(T, 512, 8)
    dn = (((0,), (0,)), ((), ()))

    def body(x_ref, da_ref, dx_ref, dwa_ref, dwx_ref, dba_ref, dbx_ref):
        @pl.when(pl.program_id(1) == 0)
        def _():
            dwa_ref[...] = jnp.zeros_like(dwa_ref)
            dwx_ref[...] = jnp.zeros_like(dwx_ref)
            dba_ref[...] = jnp.zeros_like(dba_ref)
            dbx_ref[...] = jnp.zeros_like(dbx_ref)

        xb = x_ref[...].astype(BF16)
        da, dx = da_ref[...], dx_ref[...]
        dwa_ref[0] += lax.dot_general(xb, da.astype(BF16), dn, preferred_element_type=F32)
        dwx_ref[0] += lax.dot_general(xb, dx.astype(BF16), dn, preferred_element_type=F32)
        dba_ref[...] += jnp.sum(da, axis=0, keepdims=True)
        dbx_ref[...] += jnp.sum(dx, axis=0, keepdims=True)

    blk = BS((tt, hd), lambda h, i: (i, h))
    wsp = BS((1, hd, hd), lambda h, i: (h, 0, 0))
    bsp = BS((1, hd), lambda h, i: (0, h))
    return _pcall(body, name=name, grid=(LRU_HEADS, T // tt), in_specs=[blk, blk, blk],
                  out_specs=[wsp, wsp, bsp, bsp],
                  out_shape=[SDS((LRU_HEADS, hd, hd), F32)] * 2 + [SDS((1, C), F32)] * 2,
                  compiler_params=_cparams(("parallel", "arbitrary")))(xc, dga, dgx)


@functools.partial(jax.custom_vjp, nondiff_argnums=(5,))
def lru_gates(xc, wa, ba, wx, bx, name):
    return tuple(_gates_fwd_call(xc, wa, ba, wx, bx, name))


def _lru_gates_fwd(xc, wa, ba, wx, bx, name):
    return tuple(_gates_fwd_call(xc, wa, ba, wx, bx, name)), (xc, wa, wx, ba.shape)


def _lru_gates_bwd(name, res, g):
    xc, wa, wx, bshape = res
    dga, dgx = g
    dxc = _gates_bwd_x_call(dga, dgx, wa, wx, name + "_dx")
    dwa, dwx, dba, dbx = _gates_bwd_w_call(xc, dga, dgx, name + "_dw")
    return dxc, dwa, dba.reshape(bshape), dwx, dbx.reshape(bshape)


lru_gates.defvjp(_lru_gates_fwd, _lru_gates_bwd)


def _lru_coeffs(xc, ga, gx, lam):
    r = _sigmoid(ga)
    ig = _sigmoid(gx)
    z = -lam
    sp = jnp.maximum(z, 0.0) + jnp.log(1.0 + jnp.exp(-jnp.abs(z)))
    la = -LRU_C * r * sp
    a = jnp.exp(la)
    s = jnp.sqrt(-_expm1(2.0 * la))
    return r, ig, sp, a, s


LRU_TT = 256


def _scan_fwd_call(xc, ga, gx, y, lam, name):
    T, C = xc.shape
    tt = _tile(T, LRU_TT, 8)

    def body(xc_ref, ga_ref, gx_ref, y_ref, lam_ref, h_ref, rec_ref, a_buf, carry):
        @pl.when(pl.program_id(0) == 0)
        def _():
            carry[...] = jnp.zeros_like(carry)

        xcv = xc_ref[...]
        _, ig, _, a, s = _lru_coeffs(xcv, ga_ref[...], gx_ref[...], lam_ref[...])
        a_buf[...] = a
        h_ref[...] = s * (ig * xcv)

        def step(t, h):
            hn = a_buf[pl.ds(t, 1), :] * h + h_ref[pl.ds(t, 1), :]
            h_ref[pl.ds(t, 1), :] = hn
            return hn

        carry[0:1, :] = lax.fori_loop(0, tt, step, carry[0:1, :], unroll=8)
        g, _ = _gelu_parts(y_ref[...])
        rec_ref[...] = h_ref[...] * g

    row = BS((tt, C), lambda i: (i, 0))
    vec = BS((1, C), lambda i: (0, 0))
    return _pcall(body, name=name, grid=(T // tt,), in_specs=[row, row, row, row, vec], out_specs=[row, row],
                  out_shape=[SDS((T, C), F32)] * 2,
                  scratch_shapes=[pltpu.VMEM((tt, C), F32), pltpu.VMEM((8, C), F32)],
                  compiler_params=_cparams(("arbitrary",)))(xc, ga, gx, y, lam.reshape(1, C))


def _scan_bwd_call(xc, ga, gx, y, lam, h, drec, name):
    T, C = xc.shape
    tt = _tile(T, LRU_TT, 8)
    nt = T // tt
    per = tt // 8

    def body(xc_ref, ga_ref, gx_ref, y_ref, lam_ref, h_ref, halo_ref, dr_ref,
             dga_ref, dgx_ref, dxc_ref, dy_ref, dlam_ref, a_buf, g_buf, carry):
        i = pl.program_id(0)

        @pl.when(i == 0)
        def _():
            carry[...] = jnp.zeros_like(carry)
            dlam_ref[...] = jnp.zeros_like(dlam_ref)

        xcv, lam = xc_ref[...], lam_ref[...]
        r, ig, sp, a, s = _lru_coeffs(xcv, ga_ref[...], gx_ref[...], lam)
        gel, dgel = _gelu_parts(y_ref[...])
        drec = dr_ref[...]
        hv = h_ref[...]
        dy_ref[...] = drec * hv * dgel
        a_buf[...] = a
        g_buf[...] = drec * gel

        def step(j, q):
            t = tt - 1 - j
            g = g_buf[pl.ds(t, 1), :] + q
            g_buf[pl.ds(t, 1), :] = g
            return a_buf[pl.ds(t, 1), :] * g

        carry[0:1, :] = lax.fori_loop(0, tt, step, carry[0:1, :], unroll=8)
        g = g_buf[...]
        halo = jnp.where(i < nt - 1, halo_ref[...], 0.0)
        hprev = pltpu.roll(jnp.concatenate([halo, hv], axis=0), 1, axis=0)[8:, :]
        da = g * hprev
        dig = g * s * xcv
        ds = g * ig * xcv
        dla = da * a - ds * (a * a) / s
        dga_ref[...] = dla * (-LRU_C * sp) * r * (1.0 - r)
        dgx_ref[...] = dig * ig * (1.0 - ig)
        dxc_ref[...] = g * s * ig
        dlam_ref[...] += jnp.sum(dla * r, axis=0, keepdims=True) * (LRU_C * _sigmoid(-lam))

    row = BS((tt, C), lambda i: (nt - 1 - i, 0))
    vec = BS((1, C), lambda i: (0, 0))
    halo = BS((8, C), lambda i: (jnp.maximum((nt - 1 - i) * per - 1, 0), 0))
    return _pcall(body, name=name, grid=(nt,), in_specs=[row, row, row, row, vec, row, halo, row],
                  out_specs=[row, row, row, row, vec], out_shape=[SDS((T, C), F32)] * 4 + [SDS((1, C), F32)],
                  scratch_shapes=[pltpu.VMEM((tt, C), F32), pltpu.VMEM((tt, C), F32), pltpu.VMEM((8, C), F32)],
                  compiler_params=_cparams(("arbitrary",)))(xc, ga, gx, y, lam.reshape(1, C), h, h, drec)


@functools.partial(jax.custom_vjp, nondiff_argnums=(5,))
def lru_scan(xc, ga, gx, y, lam, name):
    return _scan_fwd_call(xc, ga, gx, y, lam, name)[1]


def _lru_scan_fwd(xc, ga, gx, y, lam, name):
    h, rec = _scan_fwd_call(xc, ga, gx, y, lam, name)
    return rec, (xc, ga, gx, y, lam, h)


def _lru_scan_bwd(name, res, drec):
    xc, ga, gx, y, lam, h = res
    dga, dgx, dxc, dy, dlam = _scan_bwd_call(xc, ga, gx, y, lam, h, drec, name + "_bwd")
    return dxc, dga, dgx, dy, dlam.reshape(lam.shape)


lru_scan.defvjp(_lru_scan_fwd, _lru_scan_bwd)


ATT_HB = 8


def _band_scores(qb, kc, kp, n, max_dist, scale):
    hb = qb.shape[0]
    dn = (((2,), (2,)), ((0,), (0,)))
    s_c = lax.dot_general(qb, kc, dn, preferred_element_type=F32) * scale
    s_p = lax.dot_general(qb, kp, dn, preferred_element_type=F32) * scale
    qi = lax.broadcasted_iota(jnp.int32, (1, ATT_BLOCK, ATT_BLOCK), 1)
    kj = lax.broadcasted_iota(jnp.int32, (1, ATT_BLOCK, ATT_BLOCK), 2)
    s_c = jnp.where(kj <= qi, s_c, NEG)
    s_p = jnp.where((kj >= qi + (ATT_BLOCK - max_dist)) & (n > 0), s_p, NEG)
    return s_c, s_p


def _kv_block(ref, hb):
    v = ref[...].astype(BF16)
    if v.shape[0] != hb:
        v = jnp.broadcast_to(v, (hb,) + v.shape[1:])
    return v


def _band_fwd_call(q, k, v, sinks, max_dist, name):
    H, L, hd = q.shape
    G = H // k.shape[0]
    hb = ATT_HB
    kb = hb // G
    nb = L // ATT_BLOCK
    scale = hd ** -0.5
    dn_pv = (((2,), (1,)), ((0,), (0,)))

    def body(q_ref, kc_ref, kp_ref, vc_ref, vp_ref, sk_ref, o_ref, lse_ref):
        n = pl.program_id(1)
        qb = q_ref[...].astype(BF16)
        s_c, s_p = _band_scores(qb, _kv_block(kc_ref, hb), _kv_block(kp_ref, hb), n, max_dist, scale)
        sk = sk_ref[...]
        m = jnp.maximum(jnp.maximum(jnp.max(s_c, axis=-1, keepdims=True), jnp.max(s_p, axis=-1, keepdims=True)), sk)
        e_c, e_p = jnp.exp(s_c - m), jnp.exp(s_p - m)
        den = jnp.sum(e_c, axis=-1, keepdims=True) + jnp.sum(e_p, axis=-1, keepdims=True) + jnp.exp(sk - m)
        inv = 1.0 / den
        o_ref[...] = (lax.dot_general((e_c * inv).astype(BF16), _kv_block(vc_ref, hb), dn_pv, preferred_element_type=F32)
                      + lax.dot_general((e_p * inv).astype(BF16), _kv_block(vp_ref, hb), dn_pv, preferred_element_type=F32))
        lse_ref[:, 0, :, :] = _col_to_row(m + jnp.log(den))

    qsp = BS((hb, ATT_BLOCK, hd), lambda h, n: (h, n, 0))
    csp = BS((kb, ATT_BLOCK, hd), lambda h, n: (h, n, 0))
    psp = BS((kb, ATT_BLOCK, hd), lambda h, n: (h, jnp.maximum(n - 1, 0), 0))
    return _pcall(body, name=name, grid=(H // hb, nb),
                  in_specs=[qsp, csp, psp, csp, psp, BS((hb, 1, 1), lambda h, n: (h, 0, 0))],
                  out_specs=[qsp, BS((hb, 1, 1, ATT_BLOCK), lambda h, n: (h, n, 0, 0))],
                  out_shape=[SDS((H, L, hd), F32), SDS((H, nb, 1, ATT_BLOCK), F32)],
                  compiler_params=_cparams(("parallel", "parallel")))(q, k, k, v, v, sinks)


def _band_bwd_call(q, k, v, sinks, o, lse, do, dlse, max_dist, name):
    H, L, hd = q.shape
    G = H // k.shape[0]
    hb = ATT_HB
    kb = hb // G
    nb = L // ATT_BLOCK
    scale = hd ** -0.5
    dn_nt = (((2,), (2,)), ((0,), (0,)))
    dn_nn = (((2,), (1,)), ((0,), (0,)))
    dn_tn = (((1,), (1,)), ((0,), (0,)))

    def group_sum(t):
        return t if kb == hb else jnp.sum(t, axis=0, keepdims=True)

    def body(q_ref, kc_ref, kp_ref, vc_ref, vp_ref, sk_ref, o_ref, lse_ref, do_ref, dlse_ref,
             dq_ref, dk_ref, dv_ref, dsk_ref, ck, cv):
        n = pl.program_id(1)

        @pl.when(n == 0)
        def _():
            ck[...] = jnp.zeros_like(ck)
            cv[...] = jnp.zeros_like(cv)
            dsk_ref[...] = jnp.zeros_like(dsk_ref)

        @pl.when(n < nb)
        def _():
            qb = q_ref[...].astype(BF16)
            kc, kp = _kv_block(kc_ref, hb), _kv_block(kp_ref, hb)
            vc, vp = _kv_block(vc_ref, hb), _kv_block(vp_ref, hb)
            s_c, s_p = _band_scores(qb, kc, kp, n, max_dist, scale)
            lse = _row_to_col(lse_ref[:, 0, :, :])
            dlse = _row_to_col(dlse_ref[:, 0, :, :])
            p_c, p_p = jnp.exp(s_c - lse), jnp.exp(s_p - lse)
            dov = do_ref[...]
            dob = dov.astype(BF16)
            corr = dlse - jnp.sum(dov * o_ref[...], axis=-1, keepdims=True)
            ds_c = p_c * (lax.dot_general(dob, vc, dn_nt, preferred_element_type=F32) + corr)
            ds_p = p_p * (lax.dot_general(dob, vp, dn_nt, preferred_element_type=F32) + corr)
            ds_cb, ds_pb = ds_c.astype(BF16), ds_p.astype(BF16)
            dq_ref[...] = (lax.dot_general(ds_cb, kc, dn_nn, preferred_element_type=F32)
                           + lax.dot_general(ds_pb, kp, dn_nn, preferred_element_type=F32)) * scale
            dk_p = lax.dot_general(ds_pb, qb, dn_tn, preferred_element_type=F32) * scale
            dk_c = lax.dot_general(ds_cb, qb, dn_tn, preferred_element_type=F32) * scale
            dv_p = lax.dot_general(p_p.astype(BF16), dob, dn_tn, preferred_element_type=F32)
            dv_c = lax.dot_general(p_c.astype(BF16), dob, dn_tn, preferred_element_type=F32)
            dk_ref[...] = ck[...] + group_sum(dk_p)
            dv_ref[...] = cv[...] + group_sum(dv_p)
            ck[...] = group_sum(dk_c)
            cv[...] = group_sum(dv_c)
            dsk = jnp.sum(jnp.exp(sk_ref[...] - lse) * corr, axis=1, keepdims=True)
            dsk_ref[...] += jnp.broadcast_to(dsk, dsk_ref.shape)

        @pl.when(n == nb)
        def _():
            dk_ref[...] = ck[...]
            dv_ref[...] = cv[...]

    cur = lambda n: jnp.minimum(n, nb - 1)
    qsp = BS((hb, ATT_BLOCK, hd), lambda h, n: (h, cur(n), 0))
    csp = BS((kb, ATT_BLOCK, hd), lambda h, n: (h, cur(n), 0))
    psp = BS((kb, ATT_BLOCK, hd), lambda h, n: (h, jnp.maximum(cur(n) - 1, 0), 0))
    lsp = BS((hb, 1, 1, ATT_BLOCK), lambda h, n: (h, cur(n), 0, 0))
    osp = BS((kb, ATT_BLOCK, hd), lambda h, n: (h, jnp.maximum(n - 1, 0), 0))
    return _pcall(body, name=name, grid=(H // hb, nb + 1),
                  in_specs=[qsp, csp, psp, csp, psp, BS((hb, 1, 1), lambda h, n: (h, 0, 0)), qsp, lsp, qsp, lsp],
                  out_specs=[qsp, osp, osp, BS((hb, 1, LANES), lambda h, n: (h, 0, 0))],
                  out_shape=[SDS(q.shape, F32), SDS(k.shape, F32), SDS(v.shape, F32), SDS((H, 1, LANES), F32)],
                  scratch_shapes=[pltpu.VMEM((kb, ATT_BLOCK, hd), F32)] * 2,
                  compiler_params=_cparams(("parallel", "arbitrary")))(q, k, k, v, v, sinks, o, lse, do, dlse)


@functools.partial(jax.custom_vjp, nondiff_argnums=(4, 5))
def band_attention(q, k, v, sinks, max_dist, name):
    return tuple(_band_fwd_call(q, k, v, sinks, max_dist, name))


def _band_attention_fwd(q, k, v, sinks, max_dist, name):
    o, lse = _band_fwd_call(q, k, v, sinks, max_dist, name)
    return (o, lse), (q, k, v, sinks, o, lse)


def _band_attention_bwd(max_dist, name, res, g):
    q, k, v, sinks, o, lse = res
    do, dlse = g
    dq, dk, dv, dsk = _band_bwd_call(q, k, v, sinks, o, lse, do, dlse, max_dist, name + "_bwd")
    return dq, dk, dv, dsk[:, :, 0:1]


band_attention.defvjp(_band_attention_fwd, _band_attention_bwd)


def _merge_weights(l_refs):
    ls = [_row_to_col(r[:, 0, :, :]) for r in l_refs]
    mx = jnp.maximum(jnp.maximum(ls[0], ls[1]), ls[2])
    es = [jnp.exp(l - mx) for l in ls]
    inv = 1.0 / (es[0] + es[1] + es[2])
    return [e * inv for e in es]


def _merge_fwd_call(os_, ls_, name):
    H, L, hd = os_[0].shape
    nb = L // ATT_BLOCK

    def body(o1, o2, o3, l1, l2, l3, out_ref):
        w = _merge_weights((l1, l2, l3))
        out_ref[...] = w[0] * o1[...] + w[1] * o2[...] + w[2] * o3[...]

    osp = BS((H, ATT_BLOCK, hd), lambda n: (0, n, 0))
    lsp = BS((H, 1, 1, ATT_BLOCK), lambda n: (0, n, 0, 0))
    return _pcall(body, name=name, grid=(nb,), in_specs=[osp] * 3 + [lsp] * 3, out_specs=osp,
                  out_shape=SDS((H, L, hd), F32), compiler_params=_cparams(("parallel",)))(*os_, *ls_)


def _merge_bwd_call(os_, ls_, do, name):
    H, L, hd = os_[0].shape
    nb = L // ATT_BLOCK

    def body(o1, o2, o3, l1, l2, l3, do_ref, d1, d2, d3, e1, e2, e3):
        w = _merge_weights((l1, l2, l3))
        dov = do_ref[...]
        dws = [jnp.sum(dov * o[...], axis=-1, keepdims=True) for o in (o1, o2, o3)]
        mean = w[0] * dws[0] + w[1] * dws[1] + w[2] * dws[2]
        for wi, dwi, dref, eref in zip(w, dws, (d1, d2, d3), (e1, e2, e3)):
            dref[...] = wi * dov
            eref[:, 0, :, :] = _col_to_row(wi * (dwi - mean))

    osp = BS((H, ATT_BLOCK, hd), lambda n: (0, n, 0))
    lsp = BS((H, 1, 1, ATT_BLOCK), lambda n: (0, n, 0, 0))
    return _pcall(body, name=name, grid=(nb,), in_specs=[osp] * 3 + [lsp] * 3 + [osp], out_specs=[osp] * 3 + [lsp] * 3,
                  out_shape=[SDS((H, L, hd), F32)] * 3 + [SDS((H, nb, 1, ATT_BLOCK), F32)] * 3,
                  compiler_params=_cparams(("parallel",)))(*os_, *ls_, do)


@functools.partial(jax.custom_vjp, nondiff_argnums=(2,))
def merge3(os_, ls_, name):
    return _merge_fwd_call(os_, ls_, name)


def _merge3_fwd(os_, ls_, name):
    return _merge_fwd_call(os_, ls_, name), (os_, ls_)


def _merge3_bwd(name, res, do):
    os_, ls_ = res
    out = _merge_bwd_call(os_, ls_, do, name + "_bwd")
    return tuple(out[:3]), tuple(out[3:])


merge3.defvjp(_merge3_fwd, _merge3_bwd)


def _xa_probs(qb, kb, scale):
    s = lax.dot_general(qb, kb, (((1,), (1,)), ((), ())), preferred_element_type=F32) * scale
    e = jnp.exp(s - jnp.max(s, axis=-1, keepdims=True))
    return e / jnp.sum(e, axis=-1, keepdims=True)


def _xa_fwd_call(q, kv, name):
    T, W = q.shape
    M = kv.shape[0]
    hd = XA_HEAD_DIM
    tq = _tile(T, 512, 8)
    scale = hd ** -0.5

    def body(q_ref, k_ref, v_ref, o_ref):
        p = _xa_probs(q_ref[...].astype(BF16), k_ref[...].astype(BF16), scale)
        o_ref[...] = jnp.dot(p.astype(BF16), v_ref[...].astype(BF16), preferred_element_type=F32)

    qsp = BS((tq, hd), lambda i, h: (i, h))
    return _pcall(body, name=name, grid=(T // tq, XA_HEADS),
                  in_specs=[qsp, BS((M, hd), lambda i, h: (0, h)), BS((M, hd), lambda i, h: (0, XA_HEADS + h))],
                  out_specs=qsp, out_shape=SDS((T, W), F32),
                  compiler_params=_cparams(("parallel", "parallel")))(q, kv, kv)


def _xa_bwd_call(q, kv, do, name):
    T, W = q.shape
    M = kv.shape[0]
    hd = XA_HEAD_DIM
    tq = _tile(T, 512, 8)
    scale = hd ** -0.5
    dn_nt = (((1,), (1,)), ((), ()))
    dn_tn = (((0,), (0,)), ((), ()))

    def body(q_ref, k_ref, v_ref, do_ref, dq_ref, dk_ref, dv_ref):
        @pl.when(pl.program_id(1) == 0)
        def _():
            dk_ref[...] = jnp.zeros_like(dk_ref)
            dv_ref[...] = jnp.zeros_like(dv_ref)

        qb, kb, vb = q_ref[...].astype(BF16), k_ref[...].astype(BF16), v_ref[...].astype(BF16)
        p = _xa_probs(qb, kb, scale)
        dob = do_ref[...].astype(BF16)
        dp = lax.dot_general(dob, vb, dn_nt, preferred_element_type=F32)
        ds = (p * (dp - jnp.sum(p * dp, axis=-1, keepdims=True))).astype(BF16)
        dq_ref[...] = jnp.dot(ds, kb, preferred_element_type=F32) * scale
        dk_ref[...] += lax.dot_general(ds, qb, dn_tn, preferred_element_type=F32) * scale
        dv_ref[...] += lax.dot_general(p.astype(BF16), dob, dn_tn, preferred_element_type=F32)

    qsp = BS((tq, hd), lambda h, i: (i, h))
    ksp = BS((M, hd), lambda h, i: (0, h))
    return _pcall(body, name=name, grid=(XA_HEADS, T // tq),
                  in_specs=[qsp, ksp, BS((M, hd), lambda h, i: (0, XA_HEADS + h)), qsp],
                  out_specs=[qsp, ksp, ksp], out_shape=[SDS((T, W), F32), SDS((M, W), F32), SDS((M, W), F32)],
                  compiler_params=_cparams(("parallel", "arbitrary")))(q, kv, kv, do)


@functools.partial(jax.custom_vjp, nondiff_argnums=(2,))
def cross_attention(q, kv, name):
    return _xa_fwd_call(q, kv, name)


def _cross_attention_fwd(q, kv, name):
    return _xa_fwd_call(q, kv, name), (q, kv)


def _cross_attention_bwd(name, res, do):
    q, kv = res
    dq, dk, dv = _xa_bwd_call(q, kv, do, name + "_bwd")
    return dq, jnp.concatenate([dk, dv], axis=1)


cross_attention.defvjp(_cross_attention_fwd, _cross_attention_bwd)


def _swiglu_fwd_call(gu, name):
    T, W2 = gu.shape
    F = W2 // 2
    tt = _tile(T, 256, 8)

    def body(g_ref, u_ref, o_ref):
        g = g_ref[...]
        o_ref[...] = (g * _sigmoid(g)) * u_ref[...]

    return _pcall(body, name=name, grid=(T // tt,),
                  in_specs=[BS((tt, F), lambda i: (i, 0)), BS((tt, F), lambda i: (i, 1))],
                  out_specs=BS((tt, F), lambda i: (i, 0)), out_shape=SDS((T, F), F32),
                  compiler_params=_cparams(("parallel",)))(gu, gu)


def _swiglu_bwd_call(gu, dact, name):
    T, W2 = gu.shape
    F = W2 // 2
    tt = _tile(T, 128, 8)

    def body(g_ref, u_ref, d_ref, o_ref):
        g, d = g_ref[...], d_ref[...]
        sg = _sigmoid(g)
        o_ref[:, :F] = d * u_ref[...] * (sg * (1.0 + g * (1.0 - sg)))
        o_ref[:, F:] = d * (g * sg)

    return _pcall(body, name=name, grid=(T // tt,),
                  in_specs=[BS((tt, F), lambda i: (i, 0)), BS((tt, F), lambda i: (i, 1)), BS((tt, F), lambda i: (i, 0))],
                  out_specs=BS((tt, W2), lambda i: (i, 0)), out_shape=SDS((T, W2), F32),
                  compiler_params=_cparams(("parallel",)))(gu, gu, dact)


@functools.partial(jax.custom_vjp, nondiff_argnums=(1,))
def swiglu_act(gu, name):
    return _swiglu_fwd_call(gu, name)


def _swiglu_act_fwd(gu, name):
    return _swiglu_fwd_call(gu, name), gu


def _swiglu_act_bwd(name, gu, dact):
    return (_swiglu_bwd_call(gu, dact, name + "_bwd"),)


swiglu_act.defvjp(_swiglu_act_fwd, _swiglu_act_bwd)


def _final_call(h, g, target, name):
    T, Dm = h.shape
    tt = _tile(T, 512, 8)

    def body(x_ref, g_ref, t_ref, loss_ref, dx_ref, dg_ref):
        @pl.when(pl.program_id(0) == 0)
        def _():
            loss_ref[...] = jnp.zeros_like(loss_ref)
            dg_ref[...] = jnp.zeros_like(dg_ref)

        xv, gv = x_ref[...], g_ref[...]
        r = lax.rsqrt(jnp.mean(xv * xv, axis=-1, keepdims=True) + NORM_EPS)
        xh = xv * r
        err = xh * gv - t_ref[...]
        loss_ref[...] += 0.5 * jnp.sum(jnp.mean(err * err, axis=-1, keepdims=True), axis=0, keepdims=True)
        dy = err * (1.0 / Dm)
        dyg = dy * gv
        dx_ref[...] = r * (dyg - xh * jnp.mean(dyg * xh, axis=-1, keepdims=True))
        dg_ref[...] += jnp.sum(dy * xh, axis=0, keepdims=True)

    row = BS((tt, Dm), lambda i: (i, 0))
    vec = BS((1, Dm), lambda i: (0, 0))
    return _pcall(body, name=name, grid=(T // tt,), in_specs=[row, vec, row],
                  out_specs=[BS((1, 1), lambda i: (0, 0)), row, vec],
                  out_shape=[SDS((1, 1), F32), SDS((T, Dm), F32), SDS((1, Dm), F32)],
                  compiler_params=_cparams(("arbitrary",)))(h, g.reshape(1, Dm), target)


def _adamw_call(parts, w, m, v, name):
    R = w.shape[0]
    tr = _tile(R, 512, 8)
    c1 = 1.0 - ADAM_B1 ** ADAM_STEP
    c2 = 1.0 - ADAM_B2 ** ADAM_STEP

    def body(p_ref, w_ref, m_ref, v_ref, g_ref, d_ref, nm_ref, nv_ref):
        g = p_ref[0]
        for j in range(1, N_DEV):
            g = g + p_ref[j]
        nm = ADAM_B1 * m_ref[...] + (1.0 - ADAM_B1) * g
        nv = ADAM_B2 * v_ref[...] + (1.0 - ADAM_B2) * (g * g)
        g_ref[...] = g
        nm_ref[...] = nm
        nv_ref[...] = nv
        d_ref[...] = -ADAM_LR * ((nm / c1) / (jnp.sqrt(nv / c2) + ADAM_EPS) + ADAM_WD * w_ref[...])

    row = BS((tr, LANES), lambda i: (i, 0))
    return _pcall(body, name=name, grid=(R // tr,), in_specs=[BS((N_DEV, tr, LANES), lambda i: (0, i, 0)), row, row, row],
                  out_specs=[row] * 4, out_shape=[SDS((R, LANES), F32)] * 4,
                  compiler_params=_cparams(("parallel",)))(parts, w, m, v)


def _place():
    return lax.axis_index("x"), lax.axis_index("y"), lax.axis_index("c")


def _all_gather(x, name):
    R = x.shape[0]

    def body(x_ref, out_ref, send_sems, recv_sems, local_sem):
        x_, y_, c_ = _place()
        me, sibling = (x_, y_, c_), (x_, y_, 1 - c_)
        chips = [(1 - x_, y_), (x_, 1 - y_), (1 - x_, 1 - y_)]

        def slot(px, py, pc):
            return out_ref.at[4 * px + 2 * py + pc]

        def copy(k, block, to, src=None):
            return pltpu.make_async_remote_copy(
                src_ref=slot(*block) if src is None else src, dst_ref=slot(*block),
                send_sem=send_sems.at[k], recv_sem=recv_sems.at[k], device_id=to, device_id_type=pl.DeviceIdType.MESH)

        mine = pltpu.make_async_copy(x_ref, slot(*me), local_sem)
        mine.start()
        first = [copy(0, me, sibling, src=x_ref)]
        first += [copy(1 + j, me, (*chip, c_), src=x_ref) for j, chip in enumerate(chips)]
        for cp in first:
            cp.start()
        passed = [copy(4 + j, (*chip, c_), sibling) for j, chip in enumerate(chips)]
        for j, chip in enumerate(chips):
            copy(1 + j, (*chip, c_), me).wait_recv()
            passed[j].start()
        copy(0, sibling, me).wait_recv()
        for j, chip in enumerate(chips):
            copy(4 + j, (*chip, 1 - c_), me).wait_recv()
        for cp in first + passed:
            cp.wait_send()
        mine.wait()

    return _pcall(body, name=name, in_specs=[BS(memory_space=pl.ANY)], out_specs=BS(memory_space=pl.ANY),
                  out_shape=SDS((N_DEV, R, LANES), x.dtype),
                  scratch_shapes=[pltpu.SemaphoreType.DMA((7,)), pltpu.SemaphoreType.DMA((7,)), pltpu.SemaphoreType.DMA],
                  compiler_params=pltpu.CompilerParams(has_side_effects=True))(x)


def _all_to_all(x, name):
    R = x.shape[1]

    def body(x_ref, out_ref, send_sems, recv_sems, local_sem):
        x_, y_, c_ = _place()
        me = 4 * x_ + 2 * y_ + c_
        mine = pltpu.make_async_copy(x_ref.at[me], out_ref.at[me], local_sem)
        mine.start()
        copies = []
        for k in range(1, N_DEV):
            fx, fy, fc = (k >> 2) & 1, (k >> 1) & 1, k & 1
            px = fx + x_ - 2 * fx * x_
            py = fy + y_ - 2 * fy * y_
            pc = fc + c_ - 2 * fc * c_
            cp = pltpu.make_async_remote_copy(
                src_ref=x_ref.at[4 * px + 2 * py + pc], dst_ref=out_ref.at[me],
                send_sem=send_sems.at[k - 1], recv_sem=recv_sems.at[k - 1],
                device_id=(px, py, pc), device_id_type=pl.DeviceIdType.MESH)
            cp.start()
            copies.append(cp)
        for cp in copies:
            cp.wait()
        mine.wait()

    return _pcall(body, name=name, in_specs=[BS(memory_space=pl.ANY)], out_specs=BS(memory_space=pl.ANY),
                  out_shape=SDS((N_DEV, R, LANES), x.dtype),
                  scratch_shapes=[pltpu.SemaphoreType.DMA((7,)), pltpu.SemaphoreType.DMA((7,)), pltpu.SemaphoreType.DMA],
                  compiler_params=pltpu.CompilerParams(has_side_effects=True))(x)


def _pad_flat(t, quantum=PACK_QUANTUM):
    f = t.reshape(-1)
    pad = (-f.shape[0]) % quantum
    return jnp.pad(f, (0, pad)) if pad else f


def _pack(arrs, dtype):
    return jnp.concatenate([_pad_flat(a.astype(dtype)) for a in arrs]).reshape(-1, LANES)


def _unpack(buf, shapes, lead=()):
    flat = buf.reshape(lead + (-1,))
    out, off = [], 0
    for s in shapes:
        n = math.prod(s)
        out.append(flat[..., off:off + n].reshape(lead + tuple(s)))
        off += n + (-n) % PACK_QUANTUM
    return out


def _full_from_gathered(g, axis):
    t = jnp.moveaxis(g, 0, axis)
    s = t.shape
    return t.reshape(s[:axis] + (s[axis] * s[axis + 1],) + s[axis + 2:])


def _parts_from_full(t, axis):
    s = t.shape
    t = t.reshape(s[:axis] + (N_DEV, s[axis] // N_DEV) + s[axis + 1:])
    return jnp.moveaxis(t, axis, 0)


def _heads(t, n_heads):
    T = t.shape[0]
    return t.reshape(T, n_heads, HEAD_DIM).transpose(1, 0, 2)


def _unheads(t):
    H, T, hd = t.shape
    return t.transpose(1, 0, 2).reshape(T, H * hd)


def _dilated_attention(q, k, v, name):
    H, T, hd = q.shape
    no_sink = jnp.full((1, 1, 1), NEG, F32)
    outs, lses = [], []
    for d in DILATIONS:
        Ld = T // d

        def strided(t):
            return t.reshape(H, Ld, d, hd).transpose(0, 2, 1, 3).reshape(H * d, Ld, hd)

        o, lse = band_attention(strided(q), strided(k), strided(v), jnp.broadcast_to(no_sink, (H * d, 1, 1)),
                                ATT_BLOCK, f"{name}_d{d}")
        outs.append(o.reshape(H, d, Ld, hd).transpose(0, 2, 1, 3).reshape(H, T, hd))
        lses.append(lse.reshape(H, d, Ld).transpose(0, 2, 1).reshape(H, T // ATT_BLOCK, 1, ATT_BLOCK))
    return merge3(tuple(outs), tuple(lses), name + "_merge")


def _trunk(W, x, mem):
    T = x.shape[0]
    tabs = _rope_tables(T)
    h = x
    for layer in range(2):
        L = f"l{layer}"
        hn = rmsnorm(h, W['mix_norm'][layer], L + "_mix_norm")
        if layer == 0:
            C = W['lru_conv_w'].shape[-1]
            proj = linear(hn, W['ab_w_in'][0], None, None, L + "_w_in")
            xc = conv4(proj[:, :C], W['lru_conv_w'][0], W['lru_conv_b'][0], L + "_conv")
            ga, gx = lru_gates(xc, W['lru_wa'][0], W['lru_ba'][0], W['lru_wx'][0], W['lru_bx'][0], L + "_gates")
            rec = lru_scan(xc, ga, gx, proj[:, C:2 * C], W['lru_lambda'][0], L + "_scan")
            bw = B_HEADS * HEAD_DIM
            q = rope(proj[:, 2 * C:2 * C + bw], tabs, L + "_rope_q")
            k = rope(proj[:, 2 * C + bw:2 * C + 2 * bw], tabs, L + "_rope_k")
            v = proj[:, 2 * C + 2 * bw:]
            att = _unheads(_dilated_attention(_heads(q, B_HEADS), _heads(k, B_HEADS), _heads(v, B_HEADS), L + "_att"))
            w_out = W['ab_w_out'][0]
            h = linear(att, w_out[C:], None, h, L + "_w_out_att")
            h = linear(rec, w_out[:C], None, h, L + "_w_out_rec")
        else:
            qw = C_HEADS * HEAD_DIM
            kw = C_KV_HEADS * HEAD_DIM
            qkv = linear(hn, W['c_w_qkv'][0], W['c_b_qkv'][0], None, L + "_w_qkv")
            q = rope(qkv[:, :qw], tabs, L + "_rope_q")
            k = rope(qkv[:, qw:qw + kw], tabs, L + "_rope_k")
            v = qkv[:, qw + kw:]
            o, _ = band_attention(_heads(q, C_HEADS), _heads(k, C_KV_HEADS), _heads(v, C_KV_HEADS),
                                  W['c_sinks'][0].reshape(C_HEADS, 1, 1), ATT_BLOCK - 1, L + "_att")
            h = linear(_unheads(o), W['c_w_out'][0], W['c_b_out'][0], h, L + "_w_out")
        xq = linear(rmsnorm(h, W['xa_norm'][layer], L + "_xa_norm"), W['xa_wq'][layer], None, None, L + "_xa_wq")
        xkv = linear(rmsnorm(mem, W['xa_mem_norm'][layer], L + "_xa_mem_norm"), W['xa_wkv'][layer], None, None,
                     L + "_xa_wkv")
        h = linear(cross_attention(xq, xkv, L + "_xa"), W['xa_wo'][layer], None, h, L + "_xa_wo")
        gu = linear(rmsnorm(h, W['ffn_norm'][layer], L + "_ffn_norm"), W['ffn_w_gate_up'][layer], None, None,
                    L + "_ffn_gu")
        h = linear(swiglu_act(gu, L + "_swiglu"), W['ffn_w_down'][layer], None, h, L + "_ffn_down")
    return h


def kernel(x, mem, mix_norm, ab_w_in, lru_conv_w, lru_conv_b, lru_wa, lru_ba, lru_wx, lru_bx, lru_lambda, ab_w_out, c_w_qkv, c_b_qkv, c_sinks, c_w_out, c_b_out, xa_norm, xa_mem_norm, xa_wq, xa_wkv, xa_wo, ffn_norm, ffn_w_gate_up, ffn_w_down, final_norm, loss_target, m_mix_norm, m_ab_w_in, m_lru_conv_w, m_lru_conv_b, m_lru_wa, m_lru_ba, m_lru_wx, m_lru_bx, m_lru_lambda, m_ab_w_out, m_c_w_qkv, m_c_b_qkv, m_c_sinks, m_c_w_out, m_c_b_out, m_xa_norm, m_xa_mem_norm, m_xa_wq, m_xa_wkv, m_xa_wo, m_ffn_norm, m_ffn_w_gate_up, m_ffn_w_down, m_final_norm, v_mix_norm, v_ab_w_in, v_lru_conv_w, v_lru_conv_b, v_lru_wa, v_lru_ba, v_lru_wx, v_lru_bx, v_lru_lambda, v_ab_w_out, v_c_w_qkv, v_c_b_qkv, v_c_sinks, v_c_w_out, v_c_b_out, v_xa_norm, v_xa_mem_norm, v_xa_wq, v_xa_wkv, v_xa_wo, v_ffn_norm, v_ffn_w_gate_up, v_ffn_w_down, v_final_norm):
    w_loc = dict(zip(WEIGHT_NAMES, (mix_norm, ab_w_in, lru_conv_w, lru_conv_b, lru_wa, lru_ba, lru_wx, lru_bx, lru_lambda, ab_w_out, c_w_qkv, c_b_qkv, c_sinks, c_w_out, c_b_out, xa_norm, xa_mem_norm, xa_wq, xa_wkv, xa_wo, ffn_norm, ffn_w_gate_up, ffn_w_down, final_norm)))
    m_loc = dict(zip(WEIGHT_NAMES, (m_mix_norm, m_ab_w_in, m_lru_conv_w, m_lru_conv_b, m_lru_wa, m_lru_ba, m_lru_wx, m_lru_bx, m_lru_lambda, m_ab_w_out, m_c_w_qkv, m_c_b_qkv, m_c_sinks, m_c_w_out, m_c_b_out, m_xa_norm, m_xa_mem_norm, m_xa_wq, m_xa_wkv, m_xa_wo, m_ffn_norm, m_ffn_w_gate_up, m_ffn_w_down, m_final_norm)))
    v_loc = dict(zip(WEIGHT_NAMES, (v_mix_norm, v_ab_w_in, v_lru_conv_w, v_lru_conv_b, v_lru_wa, v_lru_ba, v_lru_wx, v_lru_bx, v_lru_lambda, v_ab_w_out, v_c_w_qkv, v_c_b_qkv, v_c_sinks, v_c_w_out, v_c_b_out, v_xa_norm, v_xa_mem_norm, v_xa_wq, v_xa_wkv, v_xa_wo, v_ffn_norm, v_ffn_w_gate_up, v_ffn_w_down, v_final_norm)))

    big_g = _all_gather(_pack([w_loc[n] for n in BIG], BF16), "gather_big")
    small_g = _all_gather(_pack([w_loc[n] for n in SMALL], F32), "gather_small")
    W = {n: w_loc[n] for n in REPLICATED}
    for names, g in ((BIG, big_g), (SMALL, small_g)):
        for n, t in zip(names, _unpack(g, [w_loc[n].shape for n in names], lead=(N_DEV,))):
            W[n] = _full_from_gathered(t, SHARD_AXIS[n]).astype(F32)

    trunk_w = {n: W[n] for n in WEIGHT_NAMES if n != 'final_norm'}
    h_out, vjp_fn = jax.vjp(lambda tw, xx: _trunk(tw, xx, mem[0]), trunk_w, x[0])
    loss_part, dh, dg_final = _final_call(h_out, W['final_norm'], loss_target[0], "final_loss")
    grads, dx = vjp_fn(dh)
    grads = dict(grads)
    grads['final_norm'] = dg_final.reshape(final_norm.shape)
    loss = lax.psum(loss_part[0, 0], ("x", "y", "c"))

    out = {}
    parts = [_parts_from_full(grads[n], SHARD_AXIS[n]) for n in SHARDED]
    send = jnp.stack([_pack([p[j] for p in parts], F32) for j in range(N_DEV)])
    recv = _all_to_all(send, "grad_exchange")
    res = _adamw_call(recv, _pack([w_loc[n] for n in SHARDED], F32), _pack([m_loc[n] for n in SHARDED], F32),
                      _pack([v_loc[n] for n in SHARDED], F32), "adamw_sharded")
    for kind, buf in zip(("grad", "delta", "new_m", "new_v"), res):
        for n, t in zip(SHARDED, _unpack(buf, [w_loc[n].shape for n in SHARDED])):
            out[kind, n] = t
    rep_g = _all_gather(_pack([grads[n] for n in REPLICATED], F32), "gather_rep_grads")
    res = _adamw_call(rep_g, _pack([w_loc[n] for n in REPLICATED], F32), _pack([m_loc[n] for n in REPLICATED], F32),
                      _pack([v_loc[n] for n in REPLICATED], F32), "adamw_replicated")
    for kind, buf in zip(("grad", "delta", "new_m", "new_v"), res):
        for n, t in zip(REPLICATED, _unpack(buf, [w_loc[n].shape for n in REPLICATED])):
            out[kind, n] = t

    return (loss, dx[None], *[out[kind, n] for kind in ("grad", "delta", "new_m", "new_v") for n in WEIGHT_NAMES])
```

```python
import functools
import math

import jax
import jax.numpy as jnp
from jax import lax
from jax.experimental import pallas as pl
from jax.experimental.pallas import tpu as pltpu

F32 = jnp.float32
BF16 = jnp.bfloat16
SDS = jax.ShapeDtypeStruct
BS = pl.BlockSpec

N_DEV = 8
NORM_EPS = 1e-6
ROPE_THETA = 500000.0
HEAD_DIM = 64
ROT_DIM = 16
ATT_BLOCK = 128
LRU_C = 8.0
LRU_HEADS = 4
DILATIONS = (1, 4, 16)
B_HEADS = 8
C_HEADS = 16
C_KV_HEADS = 2
XA_HEADS = 4
XA_HEAD_DIM = 128
NEG = -1e30
ADAM_LR, ADAM_B1, ADAM_B2, ADAM_EPS, ADAM_WD, ADAM_STEP = 0.001, 0.9, 0.999, 1e-08, 0.01, 10
LANES = 128
VMEM_LIMIT = 48 * 1024 * 1024

WEIGHT_NAMES = ['mix_norm', 'ab_w_in', 'lru_conv_w', 'lru_conv_b', 'lru_wa', 'lru_ba', 'lru_wx', 'lru_bx',
                'lru_lambda', 'ab_w_out', 'c_w_qkv', 'c_b_qkv', 'c_sinks', 'c_w_out', 'c_b_out', 'xa_norm',
                'xa_mem_norm', 'xa_wq', 'xa_wkv', 'xa_wo', 'ffn_norm', 'ffn_w_gate_up', 'ffn_w_down', 'final_norm']
SHARD_AXIS = {'ab_w_in': 2, 'lru_conv_w': 2, 'lru_wa': 2, 'lru_ba': 2, 'lru_wx': 2, 'lru_bx': 2, 'ab_w_out': 1,
              'c_w_qkv': 2, 'c_b_qkv': 1, 'c_w_out': 1, 'c_b_out': 1, 'xa_wq': 1, 'xa_wkv': 1, 'xa_wo': 2,
              'ffn_w_gate_up': 2, 'ffn_w_down': 1}
BIG = ['ab_w_in', 'lru_wa', 'lru_wx', 'ab_w_out', 'c_w_qkv', 'c_w_out', 'xa_wq', 'xa_wkv', 'xa_wo',
       'ffn_w_gate_up', 'ffn_w_down']
SMALL = ['lru_conv_w', 'lru_ba', 'lru_bx', 'c_b_qkv', 'c_b_out']
SHARDED = BIG + SMALL
REPLICATED = [n for n in WEIGHT_NAMES if n not in SHARD_AXIS]
PACK_QUANTUM = 2048


def _pcall(body, **kw):
    return pl.pallas_call(body, **kw)


def _cparams(sem=None):
    return pltpu.CompilerParams(dimension_semantics=sem, vmem_limit_bytes=VMEM_LIMIT)


def _tile(n, target, mult=LANES):
    if n <= target:
        return n
    t = (target // mult) * mult
    while t >= mult:
        if n % t == 0:
            return t
        t -= mult
    return n


def _sigmoid(x):
    return 1.0 / (1.0 + jnp.exp(-x))


def _expm1(x):
    small = x * (1.0 + x * (0.5 + x * (1.0 / 6.0 + x * (1.0 / 24.0))))
    return jnp.where(jnp.abs(x) < 0.03, small, jnp.exp(x) - 1.0)


_GELU_C = math.sqrt(2.0 / math.pi)


def _gelu_parts(y):
    y2 = y * y
    th = jnp.tanh(_GELU_C * (y + 0.044715 * y * y2))
    g = 0.5 * y * (1.0 + th)
    dg = 0.5 * (1.0 + th) + 0.5 * y * (1.0 - th * th) * _GELU_C * (1.0 + 3.0 * 0.044715 * y2)
    return g, dg


def _eye(n):
    return lax.broadcasted_iota(jnp.int32, (n, n), 0) == lax.broadcasted_iota(jnp.int32, (n, n), 1)


def _col_to_row(c):
    n = c.shape[1]
    return jnp.sum(jnp.where(_eye(n)[None], c, 0.0), axis=1, keepdims=True)


def _row_to_col(r):
    n = r.shape[2]
    return jnp.sum(jnp.where(_eye(n)[None], r, 0.0), axis=2, keepdims=True)


def _mm(a, b, *, ta=False, tb=False, bias=None, add=None, name):
    M, K = (a.shape[1], a.shape[0]) if ta else a.shape
    N = b.shape[0] if tb else b.shape[1]
    tm, tn, tk = _tile(M, 512), _tile(N, 512), _tile(K, 2048)
    nk = K // tk
    dn = (((0 if ta else 1,), (1 if tb else 0,)), ((), ()))

    def body(*refs):
        a_ref, b_ref = refs[0], refs[1]
        pos = 2
        bias_ref = add_ref = None
        if bias is not None:
            bias_ref = refs[pos]
            pos += 1
        if add is not None:
            add_ref = refs[pos]
            pos += 1
        o_ref, acc_ref = refs[pos], refs[pos + 1]
        k = pl.program_id(2)

        @pl.when(k == 0)
        def _():
            acc_ref[...] = jnp.zeros_like(acc_ref)

        acc_ref[...] += lax.dot_general(a_ref[...].astype(BF16), b_ref[...].astype(BF16), dn,
                                        preferred_element_type=F32)

        @pl.when(k == nk - 1)
        def _():
            r = acc_ref[...]
            if bias_ref is not None:
                r = r + bias_ref[...]
            if add_ref is not None:
                r = r + add_ref[...]
            o_ref[...] = r

    in_specs = [BS((tk, tm), lambda i, j, k: (k, i)) if ta else BS((tm, tk), lambda i, j, k: (i, k)),
                BS((tn, tk), lambda i, j, k: (j, k)) if tb else BS((tk, tn), lambda i, j, k: (k, j))]
    args = [a, b]
    if bias is not None:
        in_specs.append(BS((1, tn), lambda i, j, k: (0, j)))
        args.append(bias.reshape(1, N))
    if add is not None:
        in_specs.append(BS((tm, tn), lambda i, j, k: (i, j)))
        args.append(add)
    return _pcall(body, name=name, grid=(M // tm, N // tn, nk), in_specs=in_specs,
                  out_specs=BS((tm, tn), lambda i, j, k: (i, j)), out_shape=SDS((M, N), F32),
                  scratch_shapes=[pltpu.VMEM((tm, tn), F32)],
                  compiler_params=_cparams(("parallel", "parallel", "arbitrary")))(*args)


def _colsum(x, name):
    T, N = x.shape
    tt = _tile(T, 512, 8)

    def body(x_ref, o_ref):
        @pl.when(pl.program_id(0) == 0)
        def _():
            o_ref[...] = jnp.zeros_like(o_ref)

        o_ref[...] += jnp.sum(x_ref[...], axis=0, keepdims=True)

    return _pcall(body, name=name, grid=(T // tt,), in_specs=[BS((tt, N), lambda i: (i, 0))],
                  out_specs=BS((1, N), lambda i: (0, 0)), out_shape=SDS((1, N), F32),
                  compiler_params=_cparams(("arbitrary",)))(x)


@functools.partial(jax.custom_vjp, nondiff_argnums=(5,))
def linear(a, wb, wc, bias, add, name):
    return _mm(a, wb, bias=bias, add=add, name=name)


def _linear_fwd(a, wb, wc, bias, add, name):
    return _mm(a, wb, bias=bias, add=add, name=name), (a, wb, bias is not None, add is not None)


def _linear_bwd(name, res, g):
    a, wb, has_bias, has_add = res
    da = _mm(g, wb, tb=True, name=name + "_da")
    dw = _mm(a, g, ta=True, name=name + "_dw")
    dbias = _colsum(g, name + "_db").reshape(-1) if has_bias else None
    return da, jnp.zeros_like(wb), dw, dbias, (g if has_add else None)


linear.defvjp(_linear_fwd, _linear_bwd)


def _rms_fwd_call(x, g, name):
    T, Dm = x.shape
    tt = _tile(T, 512, 8)

    def body(x_ref, g_ref, o_ref):
        xv = x_ref[...]
        r = lax.rsqrt(jnp.mean(xv * xv, axis=-1, keepdims=True) + NORM_EPS)
        o_ref[...] = (xv * r) * g_ref[...]

    return _pcall(body, name=name, grid=(T // tt,),
                  in_specs=[BS((tt, Dm), lambda i: (i, 0)), BS((1, Dm), lambda i: (0, 0))],
                  out_specs=BS((tt, Dm), lambda i: (i, 0)), out_shape=SDS((T, Dm), F32),
                  compiler_params=_cparams(("parallel",)))(x, g.reshape(1, Dm))


def _rms_bwd_call(x, g, dy, name):
    T, Dm = x.shape
    tt = _tile(T, 512, 8)

    def body(x_ref, g_ref, dy_ref, dx_ref, dg_ref):
        xv = x_ref[...]
        r = lax.rsqrt(jnp.mean(xv * xv, axis=-1, keepdims=True) + NORM_EPS)
        xh = xv * r
        dy = dy_ref[...]
        dyg = dy * g_ref[...]
        dx_ref[...] = r * (dyg - xh * jnp.mean(dyg * xh, axis=-1, keepdims=True))

        @pl.when(pl.program_id(0) == 0)
        def _():
            dg_ref[...] = jnp.zeros_like(dg_ref)

        dg_ref[...] += jnp.sum(dy * xh, axis=0, keepdims=True)

    row = BS((tt, Dm), lambda i: (i, 0))
    vec = BS((1, Dm), lambda i: (0, 0))
    return _pcall(body, name=name, grid=(T // tt,), in_specs=[row, vec, row], out_specs=[row, vec],
                  out_shape=[SDS((T, Dm), F32), SDS((1, Dm), F32)],
                  compiler_params=_cparams(("arbitrary",)))(x, g.reshape(1, Dm), dy)


@functools.partial(jax.custom_vjp, nondiff_argnums=(2,))
def rmsnorm(x, g, name):
    return _rms_fwd_call(x, g, name)


def _rmsnorm_fwd(x, g, name):
    return _rms_fwd_call(x, g, name), (x, g)


def _rmsnorm_bwd(name, res, dy):
    x, g = res
    dx, dg = _rms_bwd_call(x, g, dy, name + "_bwd")
    return dx, dg.reshape(g.shape)


rmsnorm.defvjp(_rmsnorm_fwd, _rmsnorm_bwd)


def _rope_tables(T):
    half = ROT_DIM // 2
    inv = ROPE_THETA ** (-jnp.arange(0, ROT_DIM, 2, dtype=F32) / ROT_DIM)
    ang = jnp.arange(T, dtype=F32)[:, None] * inv[None, :]
    cos, sin = jnp.cos(ang), jnp.sin(ang)
    ones = jnp.ones((T, HEAD_DIM - ROT_DIM), F32)
    zeros = jnp.zeros((T, HEAD_DIM - ROT_DIM), F32)
    zh = jnp.zeros((T, half), F32)
    c = jnp.concatenate([cos, cos, ones], axis=1)
    sa = jnp.concatenate([zh, sin, zeros], axis=1)
    sb = jnp.concatenate([-sin, zh, zeros], axis=1)
    two = lambda t: jnp.concatenate([t, t], axis=1)
    return two(c), two(sa), two(sb)


def _rope_call(x, tabs, inverse, name):
    T, W = x.shape
    tt = _tile(T, 512, 8)
    reps = W // LANES
    half = ROT_DIM // 2

    def body(x_ref, c_ref, sa_ref, sb_ref, o_ref):
        xv = x_ref[...]
        c = jnp.tile(c_ref[...], (1, reps))
        sa = jnp.tile(sa_ref[...], (1, reps))
        sb = jnp.tile(sb_ref[...], (1, reps))
        if not inverse:
            o_ref[...] = xv * c + pltpu.roll(xv, half, axis=1) * sa + pltpu.roll(xv, W - half, axis=1) * sb
        else:
            o_ref[...] = xv * c + pltpu.roll(xv * sa, W - half, axis=1) + pltpu.roll(xv * sb, half, axis=1)

    row = BS((tt, W), lambda i: (i, 0))
    tab = BS((tt, LANES), lambda i: (i, 0))
    return _pcall(body, name=name, grid=(T // tt,), in_specs=[row, tab, tab, tab], out_specs=row,
                  out_shape=SDS((T, W), F32), compiler_params=_cparams(("parallel",)))(x, *tabs)


@functools.partial(jax.custom_vjp, nondiff_argnums=(2,))
def rope(x, tabs, name):
    return _rope_call(x, tabs, False, name)


def _rope_fwd(x, tabs, name):
    return _rope_call(x, tabs, False, name), tabs


def _rope_bwd(name, tabs, dy):
    return _rope_call(dy, tabs, True, name + "_bwd"), jax.tree.map(jnp.zeros_like, tabs)


rope.defvjp(_rope_fwd, _rope_bwd)


def _conv_fwd_call(x, w, b, name):
    T, C = x.shape
    tt = _tile(T, 512, 8)
    per = tt // 8

    def body(x_ref, halo_ref, w_ref, b_ref, o_ref):
        i = pl.program_id(0)
        halo = jnp.where(i > 0, halo_ref[...], 0.0)
        e = jnp.concatenate([halo, x_ref[...]], axis=0)
        acc = b_ref[...]
        for k in (3, 2, 1):
            acc = acc + pltpu.roll(e, k, axis=0)[8:, :] * w_ref[3 - k:4 - k, :]
        o_ref[...] = acc + x_ref[...] * w_ref[3:4, :]

    return _pcall(body, name=name, grid=(T // tt,),
                  in_specs=[BS((tt, C), lambda i: (i, 0)), BS((8, C), lambda i: (jnp.maximum(i * per - 1, 0), 0)),
                            BS((4, C), lambda i: (0, 0)), BS((1, C), lambda i: (0, 0))],
                  out_specs=BS((tt, C), lambda i: (i, 0)), out_shape=SDS((T, C), F32),
                  compiler_params=_cparams(("parallel",)))(x, x, w, b.reshape(1, C))


def _conv_bwd_call(x, w, dy, name):
    T, C = x.shape
    tt = _tile(T, 512, 8)
    per = tt // 8
    nt = T // tt

    def body(x_ref, halo_ref, w_ref, dy_ref, nxt_ref, dx_ref, dwb_ref):
        i = pl.program_id(0)
        halo = jnp.where(i > 0, halo_ref[...], 0.0)
        e = jnp.concatenate([halo, x_ref[...]], axis=0)
        dy = dy_ref[...]
        nxt = jnp.where(i < nt - 1, nxt_ref[...], 0.0)
        f = jnp.concatenate([dy, nxt], axis=0)
        dx = dy * w_ref[3:4, :]
        rows = [None] * 4
        rows[3] = jnp.sum(dy * x_ref[...], axis=0, keepdims=True)
        for k in (1, 2, 3):
            dx = dx + pltpu.roll(f, tt + 8 - k, axis=0)[:tt, :] * w_ref[3 - k:4 - k, :]
            rows[3 - k] = jnp.sum(dy * pltpu.roll(e, k, axis=0)[8:, :], axis=0, keepdims=True)
        dx_ref[...] = dx
        upd = jnp.concatenate(rows + [jnp.sum(dy, axis=0, keepdims=True), jnp.zeros((3, C), F32)], axis=0)

        @pl.when(i == 0)
        def _():
            dwb_ref[...] = jnp.zeros_like(dwb_ref)

        dwb_ref[...] += upd

    row = BS((tt, C), lambda i: (i, 0))
    return _pcall(body, name=name, grid=(nt,),
                  in_specs=[row, BS((8, C), lambda i: (jnp.maximum(i * per - 1, 0), 0)), BS((4, C), lambda i: (0, 0)),
                            row, BS((8, C), lambda i: (jnp.minimum((i + 1) * per, T // 8 - 1), 0))],
                  out_specs=[row, BS((8, C), lambda i: (0, 0))],
                  out_shape=[SDS((T, C), F32), SDS((8, C), F32)],
                  compiler_params=_cparams(("arbitrary",)))(x, x, w, dy, dy)


@functools.partial(jax.custom_vjp, nondiff_argnums=(3,))
def conv4(x, w, b, name):
    return _conv_fwd_call(x, w, b, name)


def _conv4_fwd(x, w, b, name):
    return _conv_fwd_call(x, w, b, name), (x, w)


def _conv4_bwd(name, res, dy):
    x, w = res
    dx, dwb = _conv_bwd_call(x, w, dy, name + "_bwd")
    return dx, dwb[0:4], dwb[4]


conv4.defvjp(_conv4_fwd, _conv4_bwd)


def _gates_fwd_call(xc, wa, ba, wx, bx, name):
    T, C = xc.shape
    hd = C // LRU_HEADS
    tt = _tile(T, 512, 8)

    def body(x_ref, wa_ref, ba_ref, wx_ref, bx_ref, ga_ref, gx_ref):
        xb = x_ref[...].astype(BF16)
        ga_ref[...] = jnp.dot(xb, wa_ref[0].astype(BF16), preferred_element_type=F32) + ba_ref[...]
        gx_ref[...] = jnp.dot(xb, wx_ref[0].astype(BF16), preferred_element_type=F32) + bx_ref[...]

    blk = BS((tt, hd), lambda i, h: (i, h))
    wsp = BS((1, hd, hd), lambda i, h: (h, 0, 0))
    bsp = BS((1, hd), lambda i, h: (0, h))
    return _pcall(body, name=name, grid=(T // tt, LRU_HEADS), in_specs=[blk, wsp, bsp, wsp, bsp],
                  out_specs=[blk, blk], out_shape=[SDS((T, C), F32)] * 2,
                  compiler_params=_cparams(("parallel", "parallel")))(xc, wa, ba.reshape(1, C), wx, bx.reshape(1, C))


def _gates_bwd_x_call(dga, dgx, wa, wx, name):
    T, C = dga.shape
    hd = C // LRU_HEADS
    tt = _tile(T, 512, 8)
    dn = (((1,), (1,)), ((), ()))

    def body(da_ref, dx_ref, wa_ref, wx_ref, o_ref):
        o_ref[...] = (lax.dot_general(da_ref[...].astype(BF16), wa_ref[0].astype(BF16), dn, preferred_element_type=F32)
                      + lax.dot_general(dx_ref[...].astype(BF16), wx_ref[0].astype(BF16), dn, preferred_element_type=F32))

    blk = BS((tt, hd), lambda i, h: (i, h))
    wsp = BS((1, hd, hd), lambda i, h: (h, 0, 0))
    return _pcall(body, name=name, grid=(T // tt, LRU_HEADS), in_specs=[blk, blk, wsp, wsp], out_specs=blk,
                  out_shape=SDS((T, C), F32), compiler_params=_cparams(("parallel", "parallel")))(dga, dgx, wa, wx)


def _gates_bwd_w_call(xc, dga, dgx, name):
    T, C = xc.shape
    hd = C // LRU_HEADS
    tt = _tile(T, 512, 8)
    dn = (((0,), (0,)), ((), ()))

    def body(x_ref, da_ref, dx_ref, dwa_ref, dwx_ref, dba_ref, dbx_ref):
        @pl.when(pl.program_id(1) == 0)
        def _():
            dwa_ref[...] = jnp.zeros_like(dwa_ref)
            dwx_ref[...] = jnp.zeros_like(dwx_ref)
            dba_ref[...] = jnp.zeros_like(dba_ref)
            dbx_ref[...] = jnp.zeros_like(dbx_ref)

        xb = x_ref[...].astype(BF16)
        da, dx = da_ref[...], dx_ref[...]
        dwa_ref[0] += lax.dot_general(xb, da.astype(BF16), dn, preferred_element_type=F32)
        dwx_ref[0] += lax.dot_general(xb, dx.astype(BF16), dn, preferred_element_type=F32)
        dba_ref[...] += jnp.sum(da, axis=0, keepdims=True)
        dbx_ref[...] += jnp.sum(dx, axis=0, keepdims=True)

    blk = BS((tt, hd), lambda h, i: (i, h))
    wsp = BS((1, hd, hd), lambda h, i: (h, 0, 0))
    bsp = BS((1, hd), lambda h, i: (0, h))
    return _pcall(body, name=name, grid=(LRU_HEADS, T // tt), in_specs=[blk, blk, blk],
                  out_specs=[wsp, wsp, bsp, bsp],
                  out_shape=[SDS((LRU_HEADS, hd, hd), F32)] * 2 + [SDS((1, C), F32)] * 2,
                  compiler_params=_cparams(("parallel", "arbitrary")))(xc, dga, dgx)


@functools.partial(jax.custom_vjp, nondiff_argnums=(7,))
def lru_gates(xc, wa, wa_c, ba, wx, wx_c, bx, name):
    return tuple(_gates_fwd_call(xc, wa, ba, wx, bx, name))


def _lru_gates_fwd(xc, wa, wa_c, ba, wx, wx_c, bx, name):
    return tuple(_gates_fwd_call(xc, wa, ba, wx, bx, name)), (xc, wa, wx, ba.shape)


def _lru_gates_bwd(name, res, g):
    xc, wa, wx, bshape = res
    dga, dgx = g
    dxc = _gates_bwd_x_call(dga, dgx, wa, wx, name + "_dx")
    dwa, dwx, dba, dbx = _gates_bwd_w_call(xc, dga, dgx, name + "_dw")
    return dxc, jnp.zeros_like(wa), dwa, dba.reshape(bshape), jnp.zeros_like(wx), dwx, dbx.reshape(bshape)


lru_gates.defvjp(_lru_gates_fwd, _lru_gates_bwd)


def _lru_coeffs(xc, ga, gx, lam):
    r = _sigmoid(ga)
    ig = _sigmoid(gx)
    z = -lam
    sp = jnp.maximum(z, 0.0) + jnp.log(1.0 + jnp.exp(-jnp.abs(z)))
    la = -LRU_C * r * sp
    a = jnp.exp(la)
    s = jnp.sqrt(-_expm1(2.0 * la))
    return r, ig, sp, a, s


LRU_TT = 256


def _scan_fwd_call(xc, ga, gx, y, lam, name):
    T, C = xc.shape
    tt = _tile(T, LRU_TT, 8)

    def body(xc_ref, ga_ref, gx_ref, y_ref, lam_ref, h_ref, rec_ref, a_buf, carry):
        @pl.when(pl.program_id(0) == 0)
        def _():
            carry[...] = jnp.zeros_like(carry)

        xcv = xc_ref[...]
        _, ig, _, a, s = _lru_coeffs(xcv, ga_ref[...], gx_ref[...], lam_ref[...])
        a_buf[...] = a
        h_ref[...] = s * (ig * xcv)

        def step(t, h):
            hn = a_buf[pl.ds(t, 1), :] * h + h_ref[pl.ds(t, 1), :]
            h_ref[pl.ds(t, 1), :] = hn
            return hn

        carry[0:1, :] = lax.fori_loop(0, tt, step, carry[0:1, :], unroll=8)
        g, _ = _gelu_parts(y_ref[...])
        rec_ref[...] = h_ref[...] * g

    row = BS((tt, C), lambda i: (i, 0))
    vec = BS((1, C), lambda i: (0, 0))
    return _pcall(body, name=name, grid=(T // tt,), in_specs=[row, row, row, row, vec], out_specs=[row, row],
                  out_shape=[SDS((T, C), F32)] * 2,
                  scratch_shapes=[pltpu.VMEM((tt, C), F32), pltpu.VMEM((8, C), F32)],
                  compiler_params=_cparams(("arbitrary",)))(xc, ga, gx, y, lam.reshape(1, C))


def _scan_bwd_call(xc, ga, gx, y, lam, h, drec, name):
    T, C = xc.shape
    tt = _tile(T, LRU_TT, 8)
    nt = T // tt
    per = tt // 8

    def body(xc_ref, ga_ref, gx_ref, y_ref, lam_ref, h_ref, halo_ref, dr_ref,
             dga_ref, dgx_ref, dxc_ref, dy_ref, dlam_ref, a_buf, g_buf, carry):
        i = pl.program_id(0)

        @pl.when(i == 0)
        def _():
            carry[...] = jnp.zeros_like(carry)
            dlam_ref[...] = jnp.zeros_like(dlam_ref)

        xcv, lam = xc_ref[...], lam_ref[...]
        r, ig, sp, a, s = _lru_coeffs(xcv, ga_ref[...], gx_ref[...], lam)
        gel, dgel = _gelu_parts(y_ref[...])
        drec = dr_ref[...]
        hv = h_ref[...]
        dy_ref[...] = drec * hv * dgel
        a_buf[...] = a
        g_buf[...] = drec * gel

        def step(j, q):
            t = tt - 1 - j
            g = g_buf[pl.ds(t, 1), :] + q
            g_buf[pl.ds(t, 1), :] = g
            return a_buf[pl.ds(t, 1), :] * g

        carry[0:1, :] = lax.fori_loop(0, tt, step, carry[0:1, :], unroll=8)
        g = g_buf[...]
        halo = jnp.where(i < nt - 1, halo_ref[...], 0.0)
        hprev = pltpu.roll(jnp.concatenate([halo, hv], axis=0), 1, axis=0)[8:, :]
        da = g * hprev
        dig = g * s * xcv
        ds = g * ig * xcv
        dla = da * a - ds * (a * a) / s
        dga_ref[...] = dla * (-LRU_C * sp) * r * (1.0 - r)
        dgx_ref[...] = dig * ig * (1.0 - ig)
        dxc_ref[...] = g * s * ig
        dlam_ref[...] += jnp.sum(dla * r, axis=0, keepdims=True) * (LRU_C * _sigmoid(-lam))

    row = BS((tt, C), lambda i: (nt - 1 - i, 0))
    vec = BS((1, C), lambda i: (0, 0))
    halo = BS((8, C), lambda i: (jnp.maximum((nt - 1 - i) * per - 1, 0), 0))
    return _pcall(body, name=name, grid=(nt,), in_specs=[row, row, row, row, vec, row, halo, row],
                  out_specs=[row, row, row, row, vec], out_shape=[SDS((T, C), F32)] * 4 + [SDS((1, C), F32)],
                  scratch_shapes=[pltpu.VMEM((tt, C), F32), pltpu.VMEM((tt, C), F32), pltpu.VMEM((8, C), F32)],
                  compiler_params=_cparams(("arbitrary",)))(xc, ga, gx, y, lam.reshape(1, C), h, h, drec)


@functools.partial(jax.custom_vjp, nondiff_argnums=(5,))
def lru_scan(xc, ga, gx, y, lam, name):
    return _scan_fwd_call(xc, ga, gx, y, lam, name)[1]


def _lru_scan_fwd(xc, ga, gx, y, lam, name):
    h, rec = _scan_fwd_call(xc, ga, gx, y, lam, name)
    return rec, (xc, ga, gx, y, lam, h)


def _lru_scan_bwd(name, res, drec):
    xc, ga, gx, y, lam, h = res
    dga, dgx, dxc, dy, dlam = _scan_bwd_call(xc, ga, gx, y, lam, h, drec, name + "_bwd")
    return dxc, dga, dgx, dy, dlam.reshape(lam.shape)


lru_scan.defvjp(_lru_scan_fwd, _lru_scan_bwd)


ATT_HB = 8


def _band_scores(qb, kc, kp, n, max_dist, scale):
    hb = qb.shape[0]
    dn = (((2,), (2,)), ((0,), (0,)))
    s_c = lax.dot_general(qb, kc, dn, preferred_element_type=F32) * scale
    s_p = lax.dot_general(qb, kp, dn, preferred_element_type=F32) * scale
    qi = lax.broadcasted_iota(jnp.int32, (1, ATT_BLOCK, ATT_BLOCK), 1)
    kj = lax.broadcasted_iota(jnp.int32, (1, ATT_BLOCK, ATT_BLOCK), 2)
    s_c = jnp.where(kj <= qi, s_c, NEG)
    s_p = jnp.where((kj >= qi + (ATT_BLOCK - max_dist)) & (n > 0), s_p, NEG)
    return s_c, s_p


def _kv_block(ref, hb):
    v = ref[...].astype(BF16)
    if v.shape[0] != hb:
        v = jnp.broadcast_to(v, (hb,) + v.shape[1:])
    return v


def _band_fwd_call(q, k, v, sinks, max_dist, name):
    H, L, hd = q.shape
    G = H // k.shape[0]
    hb = ATT_HB
    kb = hb // G
    nb = L // ATT_BLOCK
    scale = hd ** -0.5
    dn_pv = (((2,), (1,)), ((0,), (0,)))

    def body(q_ref, kc_ref, kp_ref, vc_ref, vp_ref, sk_ref, o_ref, lse_ref):
        n = pl.program_id(1)
        qb = q_ref[...].astype(BF16)
        s_c, s_p = _band_scores(qb, _kv_block(kc_ref, hb), _kv_block(kp_ref, hb), n, max_dist, scale)
        sk = sk_ref[...]
        m = jnp.maximum(jnp.maximum(jnp.max(s_c, axis=-1, keepdims=True), jnp.max(s_p, axis=-1, keepdims=True)), sk)
        e_c, e_p = jnp.exp(s_c - m), jnp.exp(s_p - m)
        den = jnp.sum(e_c, axis=-1, keepdims=True) + jnp.sum(e_p, axis=-1, keepdims=True) + jnp.exp(sk - m)
        inv = 1.0 / den
        o_ref[...] = (lax.dot_general((e_c * inv).astype(BF16), _kv_block(vc_ref, hb), dn_pv, preferred_element_type=F32)
                      + lax.dot_general((e_p * inv).astype(BF16), _kv_block(vp_ref, hb), dn_pv, preferred_element_type=F32))
        lse_ref[:, 0, :, :] = _col_to_row(m + jnp.log(den))

    qsp = BS((hb, ATT_BLOCK, hd), lambda h, n: (h, n, 0))
    csp = BS((kb, ATT_BLOCK, hd), lambda h, n: (h, n, 0))
    psp = BS((kb, ATT_BLOCK, hd), lambda h, n: (h, jnp.maximum(n - 1, 0), 0))
    return _pcall(body, name=name, grid=(H // hb, nb),
                  in_specs=[qsp, csp, psp, csp, psp, BS((hb, 1, 1), lambda h, n: (h, 0, 0))],
                  out_specs=[qsp, BS((hb, 1, 1, ATT_BLOCK), lambda h, n: (h, n, 0, 0))],
                  out_shape=[SDS((H, L, hd), F32), SDS((H, nb, 1, ATT_BLOCK), F32)],
                  compiler_params=_cparams(("parallel", "parallel")))(q, k, k, v, v, sinks)


def _band_bwd_call(q, k, v, sinks, o, lse, do, dlse, max_dist, name):
    H, L, hd = q.shape
    G = H // k.shape[0]
    hb = ATT_HB
    kb = hb // G
    nb = L // ATT_BLOCK
    scale = hd ** -0.5
    dn_nt = (((2,), (2,)), ((0,), (0,)))
    dn_nn = (((2,), (1,)), ((0,), (0,)))
    dn_tn = (((1,), (1,)), ((0,), (0,)))

    def group_sum(t):
        return t if kb == hb else jnp.sum(t, axis=0, keepdims=True)

    def body(q_ref, kc_ref, kp_ref, vc_ref, vp_ref, sk_ref, o_ref, lse_ref, do_ref, dlse_ref,
             dq_ref, dk_ref, dv_ref, dsk_ref, ck, cv):
        n = pl.program_id(1)

        @pl.when(n == 0)
        def _():
            ck[...] = jnp.zeros_like(ck)
            cv[...] = jnp.zeros_like(cv)
            dsk_ref[...] = jnp.zeros_like(dsk_ref)

        @pl.when(n < nb)
        def _():
            qb = q_ref[...].astype(BF16)
            kc, kp = _kv_block(kc_ref, hb), _kv_block(kp_ref, hb)
            vc, vp = _kv_block(vc_ref, hb), _kv_block(vp_ref, hb)
            s_c, s_p = _band_scores(qb, kc, kp, n, max_dist, scale)
            lse = _row_to_col(lse_ref[:, 0, :, :])
            dlse = _row_to_col(dlse_ref[:, 0, :, :])
            p_c, p_p = jnp.exp(s_c - lse), jnp.exp(s_p - lse)
            dov = do_ref[...]
            dob = dov.astype(BF16)
            corr = dlse - jnp.sum(dov * o_ref[...], axis=-1, keepdims=True)
            ds_c = p_c * (lax.dot_general(dob, vc, dn_nt, preferred_element_type=F32) + corr)
            ds_p = p_p * (lax.dot_general(dob, vp, dn_nt, preferred_element_type=F32) + corr)
            ds_cb, ds_pb = ds_c.astype(BF16), ds_p.astype(BF16)
            dq_ref[...] = (lax.dot_general(ds_cb, kc, dn_nn, preferred_element_type=F32)
                           + lax.dot_general(ds_pb, kp, dn_nn, preferred_element_type=F32)) * scale
            dk_p = lax.dot_general(ds_pb, qb, dn_tn, preferred_element_type=F32) * scale
            dk_c = lax.dot_general(ds_cb, qb, dn_tn, preferred_element_type=F32) * scale
            dv_p = lax.dot_general(p_p.astype(BF16), dob, dn_tn, preferred_element_type=F32)
            dv_c = lax.dot_general(p_c.astype(BF16), dob, dn_tn, preferred_element_type=F32)
            dk_ref[...] = ck[...] + group_sum(dk_p)
            dv_ref[...] = cv[...] + group_sum(dv_p)
            ck[...] = group_sum(dk_c)
            cv[...] = group_sum(dv_c)
            dsk = jnp.sum(jnp.exp(sk_ref[...] - lse) * corr, axis=1, keepdims=True)
            dsk_ref[...] += jnp.broadcast_to(dsk, dsk_ref.shape)

        @pl.when(n == nb)
        def _():
            dk_ref[...] = ck[...]
            dv_ref[...] = cv[...]

    cur = lambda n: jnp.minimum(n, nb - 1)
    qsp = BS((hb, ATT_BLOCK, hd), lambda h, n: (h, cur(n), 0))
    csp = BS((kb, ATT_BLOCK, hd), lambda h, n: (h, cur(n), 0))
    psp = BS((kb, ATT_BLOCK, hd), lambda h, n: (h, jnp.maximum(cur(n) - 1, 0), 0))
    lsp = BS((hb, 1, 1, ATT_BLOCK), lambda h, n: (h, cur(n), 0, 0))
    osp = BS((kb, ATT_BLOCK, hd), lambda h, n: (h, jnp.maximum(n - 1, 0), 0))
    return _pcall(body, name=name, grid=(H // hb, nb + 1),
                  in_specs=[qsp, csp, psp, csp, psp, BS((hb, 1, 1), lambda h, n: (h, 0, 0)), qsp, lsp, qsp, lsp],
                  out_specs=[qsp, osp, osp, BS((hb, 1, LANES), lambda h, n: (h, 0, 0))],
                  out_shape=[SDS(q.shape, F32), SDS(k.shape, F32), SDS(v.shape, F32), SDS((H, 1, LANES), F32)],
                  scratch_shapes=[pltpu.VMEM((kb, ATT_BLOCK, hd), F32)] * 2,
                  compiler_params=_cparams(("parallel", "arbitrary")))(q, k, k, v, v, sinks, o, lse, do, dlse)


@functools.partial(jax.custom_vjp, nondiff_argnums=(4, 5))
def band_attention(q, k, v, sinks, max_dist, name):
    return tuple(_band_fwd_call(q, k, v, sinks, max_dist, name))


def _band_attention_fwd(q, k, v, sinks, max_dist, name):
    o, lse = _band_fwd_call(q, k, v, sinks, max_dist, name)
    return (o, lse), (q, k, v, sinks, o, lse)


def _band_attention_bwd(max_dist, name, res, g):
    q, k, v, sinks, o, lse = res
    do, dlse = g
    dq, dk, dv, dsk = _band_bwd_call(q, k, v, sinks, o, lse, do, dlse, max_dist, name + "_bwd")
    return dq, dk, dv, dsk[:, :, 0:1]


band_attention.defvjp(_band_attention_fwd, _band_attention_bwd)


def _merge_weights(l_refs):
    ls = [_row_to_col(r[:, 0, :, :]) for r in l_refs]
    mx = jnp.maximum(jnp.maximum(ls[0], ls[1]), ls[2])
    es = [jnp.exp(l - mx) for l in ls]
    inv = 1.0 / (es[0] + es[1] + es[2])
    return [e * inv for e in es]


def _merge_fwd_call(os_, ls_, name):
    H, L, hd = os_[0].shape
    nb = L // ATT_BLOCK

    def body(o1, o2, o3, l1, l2, l3, out_ref):
        w = _merge_weights((l1, l2, l3))
        out_ref[...] = w[0] * o1[...] + w[1] * o2[...] + w[2] * o3[...]

    osp = BS((H, ATT_BLOCK, hd), lambda n: (0, n, 0))
    lsp = BS((H, 1, 1, ATT_BLOCK), lambda n: (0, n, 0, 0))
    return _pcall(body, name=name, grid=(nb,), in_specs=[osp] * 3 + [lsp] * 3, out_specs=osp,
                  out_shape=SDS((H, L, hd), F32), compiler_params=_cparams(("parallel",)))(*os_, *ls_)


def _merge_bwd_call(os_, ls_, do, name):
    H, L, hd = os_[0].shape
    nb = L // ATT_BLOCK

    def body(o1, o2, o3, l1, l2, l3, do_ref, d1, d2, d3, e1, e2, e3):
        w = _merge_weights((l1, l2, l3))
        dov = do_ref[...]
        dws = [jnp.sum(dov * o[...], axis=-1, keepdims=True) for o in (o1, o2, o3)]
        mean = w[0] * dws[0] + w[1] * dws[1] + w[2] * dws[2]
        for wi, dwi, dref, eref in zip(w, dws, (d1, d2, d3), (e1, e2, e3)):
            dref[...] = wi * dov
            eref[:, 0, :, :] = _col_to_row(wi * (dwi - mean))

    osp = BS((H, ATT_BLOCK, hd), lambda n: (0, n, 0))
    lsp = BS((H, 1, 1, ATT_BLOCK), lambda n: (0, n, 0, 0))
    return _pcall(body, name=name, grid=(nb,), in_specs=[osp] * 3 + [lsp] * 3 + [osp], out_specs=[osp] * 3 + [lsp] * 3,
                  out_shape=[SDS((H, L, hd), F32)] * 3 + [SDS((H, nb, 1, ATT_BLOCK), F32)] * 3,
                  compiler_params=_cparams(("parallel",)))(*os_, *ls_, do)


@functools.partial(jax.custom_vjp, nondiff_argnums=(2,))
def merge3(os_, ls_, name):
    return _merge_fwd_call(os_, ls_, name)


def _merge3_fwd(os_, ls_, name):
    return _merge_fwd_call(os_, ls_, name), (os_, ls_)


def _merge3_bwd(name, res, do):
    os_, ls_ = res
    out = _merge_bwd_call(os_, ls_, do, name + "_bwd")
    return tuple(out[:3]), tuple(out[3:])


merge3.defvjp(_merge3_fwd, _merge3_bwd)


def _xa_probs(qb, kb, scale):
    s = lax.dot_general(qb, kb, (((1,), (1,)), ((), ())), preferred_element_type=F32) * scale
    e = jnp.exp(s - jnp.max(s, axis=-1, keepdims=True))
    return e / jnp.sum(e, axis=-1, keepdims=True)


def _xa_fwd_call(q, kv, name):
    T, W = q.shape
    M = kv.shape[0]
    hd = XA_HEAD_DIM
    tq = _tile(T, 512, 8)
    scale = hd ** -0.5

    def body(q_ref, k_ref, v_ref, o_ref):
        p = _xa_probs(q_ref[...].astype(BF16), k_ref[...].astype(BF16), scale)
        o_ref[...] = jnp.dot(p.astype(BF16), v_ref[...].astype(BF16), preferred_element_type=F32)

    qsp = BS((tq, hd), lambda i, h: (i, h))
    return _pcall(body, name=name, grid=(T // tq, XA_HEADS),
                  in_specs=[qsp, BS((M, hd), lambda i, h: (0, h)), BS((M, hd), lambda i, h: (0, XA_HEADS + h))],
                  out_specs=qsp, out_shape=SDS((T, W), F32),
                  compiler_params=_cparams(("parallel", "parallel")))(q, kv, kv)


def _xa_bwd_call(q, kv, do, name):
    T, W = q.shape
    M = kv.shape[0]
    hd = XA_HEAD_DIM
    tq = _tile(T, 512, 8)
    scale = hd ** -0.5
    dn_nt = (((1,), (1,)), ((), ()))
    dn_tn = (((0,), (0,)), ((), ()))

    def body(q_ref, k_ref, v_ref, do_ref, dq_ref, dk_ref, dv_ref):
        @pl.when(pl.program_id(1) == 0)
        def _():
            dk_ref[...] = jnp.zeros_like(dk_ref)
            dv_ref[...] = jnp.zeros_like(dv_ref)

        qb, kb, vb = q_ref[...].astype(BF16), k_ref[...].astype(BF16), v_ref[...].astype(BF16)
        p = _xa_probs(qb, kb, scale)
        dob = do_ref[...].astype(BF16)
        dp = lax.dot_general(dob, vb, dn_nt, preferred_element_type=F32)
        ds = (p * (dp - jnp.sum(p * dp, axis=-1, keepdims=True))).astype(BF16)
        dq_ref[...] = jnp.dot(ds, kb, preferred_element_type=F32) * scale
        dk_ref[...] += lax.dot_general(ds, qb, dn_tn, preferred_element_type=F32) * scale
        dv_ref[...] += lax.dot_general(p.astype(BF16), dob, dn_tn, preferred_element_type=F32)

    qsp = BS((tq, hd), lambda h, i: (i, h))
    ksp = BS((M, hd), lambda h, i: (0, h))
    return _pcall(body, name=name, grid=(XA_HEADS, T // tq),
                  in_specs=[qsp, ksp, BS((M, hd), lambda h, i: (0, XA_HEADS + h)), qsp],
                  out_specs=[qsp, ksp, ksp], out_shape=[SDS((T, W), F32), SDS((M, W), F32), SDS((M, W), F32)],
                  compiler_params=_cparams(("parallel", "arbitrary")))(q, kv, kv, do)


@functools.partial(jax.custom_vjp, nondiff_argnums=(2,))
def cross_attention(q, kv, name):
    return _xa_fwd_call(q, kv, name)


def _cross_attention_fwd(q, kv, name):
    return _xa_fwd_call(q, kv, name), (q, kv)


def _cross_attention_bwd(name, res, do):
    q, kv = res
    dq, dk, dv = _xa_bwd_call(q, kv, do, name + "_bwd")
    return dq, jnp.concatenate([dk, dv], axis=1)


cross_attention.defvjp(_cross_attention_fwd, _cross_attention_bwd)


def _swiglu_fwd_call(gu, name):
    T, W2 = gu.shape
    F = W2 // 2
    tt = _tile(T, 256, 8)

    def body(g_ref, u_ref, o_ref):
        g = g_ref[...]
        o_ref[...] = (g * _sigmoid(g)) * u_ref[...]

    return _pcall(body, name=name, grid=(T // tt,),
                  in_specs=[BS((tt, F), lambda i: (i, 0)), BS((tt, F), lambda i: (i, 1))],
                  out_specs=BS((tt, F), lambda i: (i, 0)), out_shape=SDS((T, F), F32),
                  compiler_params=_cparams(("parallel",)))(gu, gu)


def _swiglu_bwd_call(gu, dact, name):
    T, W2 = gu.shape
    F = W2 // 2
    tt = _tile(T, 128, 8)

    def body(g_ref, u_ref, d_ref, o_ref):
        g, d = g_ref[...], d_ref[...]
        sg = _sigmoid(g)
        o_ref[:, :F] = d * u_ref[...] * (sg * (1.0 + g * (1.0 - sg)))
        o_ref[:, F:] = d * (g * sg)

    return _pcall(body, name=name, grid=(T // tt,),
                  in_specs=[BS((tt, F), lambda i: (i, 0)), BS((tt, F), lambda i: (i, 1)), BS((tt, F), lambda i: (i, 0))],
                  out_specs=BS((tt, W2), lambda i: (i, 0)), out_shape=SDS((T, W2), F32),
                  compiler_params=_cparams(("parallel",)))(gu, gu, dact)


@functools.partial(jax.custom_vjp, nondiff_argnums=(1,))
def swiglu_act(gu, name):
    return _swiglu_fwd_call(gu, name)


def _swiglu_act_fwd(gu, name):
    return _swiglu_fwd_call(gu, name), gu


def _swiglu_act_bwd(name, gu, dact):
    return (_swiglu_bwd_call(gu, dact, name + "_bwd"),)


swiglu_act.defvjp(_swiglu_act_fwd, _swiglu_act_bwd)


def _final_call(h, g, target, name):
    T, Dm = h.shape
    tt = _tile(T, 512, 8)

    def body(x_ref, g_ref, t_ref, loss_ref, dx_ref, dg_ref):
        @pl.when(pl.program_id(0) == 0)
        def _():
            loss_ref[...] = jnp.zeros_like(loss_ref)
            dg_ref[...] = jnp.zeros_like(dg_ref)

        xv, gv = x_ref[...], g_ref[...]
        r = lax.rsqrt(jnp.mean(xv * xv, axis=-1, keepdims=True) + NORM_EPS)
        xh = xv * r
        err = xh * gv - t_ref[...]
        loss_ref[...] += 0.5 * jnp.sum(jnp.mean(err * err, axis=-1, keepdims=True), axis=0, keepdims=True)
        dy = err * (1.0 / Dm)
        dyg = dy * gv
        dx_ref[...] = r * (dyg - xh * jnp.mean(dyg * xh, axis=-1, keepdims=True))
        dg_ref[...] += jnp.sum(dy * xh, axis=0, keepdims=True)

    row = BS((tt, Dm), lambda i: (i, 0))
    vec = BS((1, Dm), lambda i: (0, 0))
    return _pcall(body, name=name, grid=(T // tt,), in_specs=[row, vec, row],
                  out_specs=[BS((1, 1), lambda i: (0, 0)), row, vec],
                  out_shape=[SDS((1, 1), F32), SDS((T, Dm), F32), SDS((1, Dm), F32)],
                  compiler_params=_cparams(("arbitrary",)))(h, g.reshape(1, Dm), target)


ADAMW_BLOCK_ELEMS = 64 * 1024


def _adamw_call(parts, w, m, v, name):
    shape = w.shape
    r, N = shape[-2], shape[-1]
    Ld = math.prod(shape[:-2])
    parts = parts.reshape(N_DEV, Ld, r, N)
    w, m, v = (t.reshape(Ld, r, N) for t in (w, m, v))
    tr = _tile(r, max(8, ADAMW_BLOCK_ELEMS // N), 8)
    c1 = 1.0 - ADAM_B1 ** ADAM_STEP
    c2 = 1.0 - ADAM_B2 ** ADAM_STEP

    def body(p_ref, w_ref, m_ref, v_ref, g_ref, d_ref, nm_ref, nv_ref):
        g = p_ref[0]
        for j in range(1, N_DEV):
            g = g + p_ref[j]
        nm = ADAM_B1 * m_ref[...] + (1.0 - ADAM_B1) * g
        nv = ADAM_B2 * v_ref[...] + (1.0 - ADAM_B2) * (g * g)
        g_ref[...] = g
        nm_ref[...] = nm
        nv_ref[...] = nv
        d_ref[...] = -ADAM_LR * ((nm / c1) / (jnp.sqrt(nv / c2) + ADAM_EPS) + ADAM_WD * w_ref[...])

    row = BS((1, tr, N), lambda l, i: (l, i, 0))
    out = _pcall(body, name=name, grid=(Ld, r // tr),
                 in_specs=[BS((N_DEV, 1, tr, N), lambda l, i: (0, l, i, 0)), row, row, row],
                 out_specs=[row] * 4, out_shape=[SDS((Ld, r, N), F32)] * 4,
                 compiler_params=_cparams(("parallel", "parallel")))(parts, w, m, v)
    return [t.reshape(shape) for t in out]


def _place():
    return lax.axis_index("x"), lax.axis_index("y"), lax.axis_index("c")


def _all_gather(xs, name):
    n = len(xs)
    pairs = [(i, l) for i, x in enumerate(xs) for l in range(x.shape[0])]

    def body(*refs):
        x_refs, o_refs = refs[:n], refs[n:2 * n]
        send_sems, recv_sems, local_sems = refs[2 * n:]
        x_, y_, c_ = _place()
        me, sibling = (x_, y_, c_), (x_, y_, 1 - c_)
        chips = [(1 - x_, y_), (x_, 1 - y_), (1 - x_, 1 - y_)]

        def copy(e, k, block, to, from_input=False):
            i, l = pairs[e]
            px, py, pc = block
            dst = o_refs[i].at[l, 4 * px + 2 * py + pc]
            return pltpu.make_async_remote_copy(
                src_ref=x_refs[i].at[l] if from_input else dst, dst_ref=dst,
                send_sem=send_sems.at[7 * e + k], recv_sem=recv_sems.at[7 * e + k],
                device_id=to, device_id_type=pl.DeviceIdType.MESH)

        every = range(len(pairs))
        mine = [pltpu.make_async_copy(x_refs[i].at[l], o_refs[i].at[l, 4 * x_ + 2 * y_ + c_], local_sems.at[e])
                for e, (i, l) in enumerate(pairs)]
        for cp in mine:
            cp.start()
        first = [copy(e, 0, me, sibling, True) for e in every]
        first += [copy(e, 1 + j, me, (*chip, c_), True) for j, chip in enumerate(chips) for e in every]
        for cp in first:
            cp.start()
        passed = []
        for j, chip in enumerate(chips):
            for e in every:
                copy(e, 1 + j, (*chip, c_), me).wait_recv()
            for e in every:
                cp = copy(e, 4 + j, (*chip, c_), sibling)
                cp.start()
                passed.append(cp)
        for e in every:
            copy(e, 0, sibling, me).wait_recv()
        for j, chip in enumerate(chips):
            for e in every:
                copy(e, 4 + j, (*chip, 1 - c_), me).wait_recv()
        for cp in first + passed:
            cp.wait_send()
        for cp in mine:
            cp.wait()

    any_spec = BS(memory_space=pl.ANY)
    return _pcall(body, name=name, in_specs=[any_spec] * n, out_specs=[any_spec] * n,
                  out_shape=[SDS((x.shape[0], N_DEV) + x.shape[1:], x.dtype) for x in xs],
                  scratch_shapes=[pltpu.SemaphoreType.DMA((7 * len(pairs),)), pltpu.SemaphoreType.DMA((7 * len(pairs),)),
                                  pltpu.SemaphoreType.DMA((len(pairs),))],
                  compiler_params=pltpu.CompilerParams(has_side_effects=True))(*xs)


def _all_to_all(parts, name):
    flat = [p for ps in parts for p in ps]
    pairs = [(i, l) for i, ps in enumerate(parts) for l in range(len(ps))]
    n_in, n_out = len(flat), len(parts)

    def body(*refs):
        in_refs, o_refs = refs[:n_in], refs[n_in:n_in + n_out]
        send_sems, recv_sems, local_sems = refs[n_in + n_out:]
        x_, y_, c_ = _place()
        me = 4 * x_ + 2 * y_ + c_
        mine = [pltpu.make_async_copy(in_refs[e].at[me], o_refs[i].at[me, l], local_sems.at[e])
                for e, (i, l) in enumerate(pairs)]
        for cp in mine:
            cp.start()
        copies = []
        for k in range(1, N_DEV):
            fx, fy, fc = (k >> 2) & 1, (k >> 1) & 1, k & 1
            px = fx + x_ - 2 * fx * x_
            py = fy + y_ - 2 * fy * y_
            pc = fc + c_ - 2 * fc * c_
            for e, (i, l) in enumerate(pairs):
                cp = pltpu.make_async_remote_copy(
                    src_ref=in_refs[e].at[4 * px + 2 * py + pc], dst_ref=o_refs[i].at[me, l],
                    send_sem=send_sems.at[7 * e + k - 1], recv_sem=recv_sems.at[7 * e + k - 1],
                    device_id=(px, py, pc), device_id_type=pl.DeviceIdType.MESH)
                cp.start()
                copies.append(cp)
        for cp in copies:
            cp.wait()
        for cp in mine:
            cp.wait()

    any_spec = BS(memory_space=pl.ANY)
    return _pcall(body, name=name, in_specs=[any_spec] * n_in, out_specs=[any_spec] * n_out,
                  out_shape=[SDS((N_DEV, len(ps)) + ps[0].shape[1:], ps[0].dtype) for ps in parts],
                  scratch_shapes=[pltpu.SemaphoreType.DMA((7 * n_in,)), pltpu.SemaphoreType.DMA((7 * n_in,)),
                                  pltpu.SemaphoreType.DMA((n_in,))],
                  compiler_params=pltpu.CompilerParams(has_side_effects=True))(*flat)


def _pad_flat(t, quantum=PACK_QUANTUM):
    f = t.reshape(-1)
    pad = (-f.shape[0]) % quantum
    return jnp.pad(f, (0, pad)) if pad else f


def _pack(arrs, dtype):
    return jnp.concatenate([_pad_flat(a.astype(dtype)) for a in arrs]).reshape(-1, LANES)


def _unpack(buf, shapes, lead=()):
    flat = buf.reshape(lead + (-1,))
    out, off = [], 0
    for s in shapes:
        n = math.prod(s)
        out.append(flat[..., off:off + n].reshape(lead + tuple(s)))
        off += n + (-n) % PACK_QUANTUM
    return out


def _full_from_gathered(g, axis):
    t = jnp.moveaxis(g, 0, axis)
    s = t.shape
    return t.reshape(s[:axis] + (s[axis] * s[axis + 1],) + s[axis + 2:])


def _parts_from_full(t, axis):
    s = t.shape
    t = t.reshape(s[:axis] + (N_DEV, s[axis] // N_DEV) + s[axis + 1:])
    return jnp.moveaxis(t, axis, 0)


def _heads(t, n_heads):
    T = t.shape[0]
    return t.reshape(T, n_heads, HEAD_DIM).transpose(1, 0, 2)


def _unheads(t):
    H, T, hd = t.shape
    return t.transpose(1, 0, 2).reshape(T, H * hd)


def _dilated_attention(q, k, v, name):
    H, T, hd = q.shape
    no_sink = jnp.full((1, 1, 1), NEG, F32)
    outs, lses = [], []
    for d in DILATIONS:
        Ld = T // d

        def strided(t):
            return t.reshape(H, Ld, d, hd).transpose(0, 2, 1, 3).reshape(H * d, Ld, hd)

        o, lse = band_attention(strided(q), strided(k), strided(v), jnp.broadcast_to(no_sink, (H * d, 1, 1)),
                                ATT_BLOCK, f"{name}_d{d}")
        outs.append(o.reshape(H, d, Ld, hd).transpose(0, 2, 1, 3).reshape(H, T, hd))
        lses.append(lse.reshape(H, d, Ld).transpose(0, 2, 1).reshape(H, T // ATT_BLOCK, 1, ATT_BLOCK))
    return merge3(tuple(outs), tuple(lses), name + "_merge")


def _trunk(S, Cw, Wb, x, mem):
    T = x.shape[0]
    tabs = _rope_tables(T)

    def lin(a, key, bias, add, name, rows=None):
        wb, wc = Wb[key], Cw[key]
        if rows is not None:
            wb, wc = wb[rows], wc[rows]
        return linear(a, wb, wc, bias, add, name)

    h = x
    for layer in range(2):
        L = f"l{layer}"
        hn = rmsnorm(h, S['mix_norm'][layer], L + "_mix_norm")
        if layer == 0:
            C = S['lru_conv_w'].shape[-1]
            proj = lin(hn, ('ab_w_in', 0), None, None, L + "_w_in")
            xc = conv4(proj[:, :C], S['lru_conv_w'][0], S['lru_conv_b'][0], L + "_conv")
            ga, gx = lru_gates(xc, Wb['lru_wa', 0], Cw['lru_wa', 0], S['lru_ba'][0],
                               Wb['lru_wx', 0], Cw['lru_wx', 0], S['lru_bx'][0], L + "_gates")
            rec = lru_scan(xc, ga, gx, proj[:, C:2 * C], S['lru_lambda'][0], L + "_scan")
            bw = B_HEADS * HEAD_DIM
            q = rope(proj[:, 2 * C:2 * C + bw], tabs, L + "_rope_q")
            k = rope(proj[:, 2 * C + bw:2 * C + 2 * bw], tabs, L + "_rope_k")
            v = proj[:, 2 * C + 2 * bw:]
            att = _unheads(_dilated_attention(_heads(q, B_HEADS), _heads(k, B_HEADS), _heads(v, B_HEADS), L + "_att"))
            h = lin(att, ('ab_w_out', 0), None, h, L + "_w_out_att", slice(C, None))
            h = lin(rec, ('ab_w_out', 0), None, h, L + "_w_out_rec", slice(0, C))
        else:
            qw = C_HEADS * HEAD_DIM
            kw = C_KV_HEADS * HEAD_DIM
            qkv = lin(hn, ('c_w_qkv', 0), S['c_b_qkv'][0], None, L + "_w_qkv")
            q = rope(qkv[:, :qw], tabs, L + "_rope_q")
            k = rope(qkv[:, qw:qw + kw], tabs, L + "_rope_k")
            v = qkv[:, qw + kw:]
            o, _ = band_attention(_heads(q, C_HEADS), _heads(k, C_KV_HEADS), _heads(v, C_KV_HEADS),
                                  S['c_sinks'][0].reshape(C_HEADS, 1, 1), ATT_BLOCK - 1, L + "_att")
            h = lin(_unheads(o), ('c_w_out', 0), S['c_b_out'][0], h, L + "_w_out")
        xq = lin(rmsnorm(h, S['xa_norm'][layer], L + "_xa_norm"), ('xa_wq', layer), None, None, L + "_xa_wq")
        xkv = lin(rmsnorm(mem, S['xa_mem_norm'][layer], L + "_xa_mem_norm"), ('xa_wkv', layer), None, None, L + "_xa_wkv")
        h = lin(cross_attention(xq, xkv, L + "_xa"), ('xa_wo', layer), None, h, L + "_xa_wo")
        gu = lin(rmsnorm(h, S['ffn_norm'][layer], L + "_ffn_norm"), ('ffn_w_gate_up', layer), None, None, L + "_ffn_gu")
        h = lin(swiglu_act(gu, L + "_swiglu"), ('ffn_w_down', layer), None, h, L + "_ffn_down")
    return h


def kernel(x, mem, mix_norm, ab_w_in, lru_conv_w, lru_conv_b, lru_wa, lru_ba, lru_wx, lru_bx, lru_lambda, ab_w_out, c_w_qkv, c_b_qkv, c_sinks, c_w_out, c_b_out, xa_norm, xa_mem_norm, xa_wq, xa_wkv, xa_wo, ffn_norm, ffn_w_gate_up, ffn_w_down, final_norm, loss_target, m_mix_norm, m_ab_w_in, m_lru_conv_w, m_lru_conv_b, m_lru_wa, m_lru_ba, m_lru_wx, m_lru_bx, m_lru_lambda, m_ab_w_out, m_c_w_qkv, m_c_b_qkv, m_c_sinks, m_c_w_out, m_c_b_out, m_xa_norm, m_xa_mem_norm, m_xa_wq, m_xa_wkv, m_xa_wo, m_ffn_norm, m_ffn_w_gate_up, m_ffn_w_down, m_final_norm, v_mix_norm, v_ab_w_in, v_lru_conv_w, v_lru_conv_b, v_lru_wa, v_lru_ba, v_lru_wx, v_lru_bx, v_lru_lambda, v_ab_w_out, v_c_w_qkv, v_c_b_qkv, v_c_sinks, v_c_w_out, v_c_b_out, v_xa_norm, v_xa_mem_norm, v_xa_wq, v_xa_wkv, v_xa_wo, v_ffn_norm, v_ffn_w_gate_up, v_ffn_w_down, v_final_norm):
    w_loc = dict(zip(WEIGHT_NAMES, (mix_norm, ab_w_in, lru_conv_w, lru_conv_b, lru_wa, lru_ba, lru_wx, lru_bx, lru_lambda, ab_w_out, c_w_qkv, c_b_qkv, c_sinks, c_w_out, c_b_out, xa_norm, xa_mem_norm, xa_wq, xa_wkv, xa_wo, ffn_norm, ffn_w_gate_up, ffn_w_down, final_norm)))
    m_loc = dict(zip(WEIGHT_NAMES, (m_mix_norm, m_ab_w_in, m_lru_conv_w, m_lru_conv_b, m_lru_wa, m_lru_ba, m_lru_wx, m_lru_bx, m_lru_lambda, m_ab_w_out, m_c_w_qkv, m_c_b_qkv, m_c_sinks, m_c_w_out, m_c_b_out, m_xa_norm, m_xa_mem_norm, m_xa_wq, m_xa_wkv, m_xa_wo, m_ffn_norm, m_ffn_w_gate_up, m_ffn_w_down, m_final_norm)))
    v_loc = dict(zip(WEIGHT_NAMES, (v_mix_norm, v_ab_w_in, v_lru_conv_w, v_lru_conv_b, v_lru_wa, v_lru_ba, v_lru_wx, v_lru_bx, v_lru_lambda, v_ab_w_out, v_c_w_qkv, v_c_b_qkv, v_c_sinks, v_c_w_out, v_c_b_out, v_xa_norm, v_xa_mem_norm, v_xa_wq, v_xa_wkv, v_xa_wo, v_ffn_norm, v_ffn_w_gate_up, v_ffn_w_down, v_final_norm)))

    big_g = _all_gather([w_loc[n].astype(BF16) for n in BIG], "gather_big")
    small_g = _all_gather([_pack([w_loc[n] for n in SMALL], F32)[None]], "gather_small")[0][0]
    S = {n: w_loc[n] for n in REPLICATED}
    for n, t in zip(SMALL, _unpack(small_g, [w_loc[n].shape for n in SMALL], lead=(N_DEV,))):
        S[n] = _full_from_gathered(t, SHARD_AXIS[n])
    Wb, Cw = {}, {}
    for n, g in zip(BIG, big_g):
        for l in range(g.shape[0]):
            Wb[n, l] = _full_from_gathered(g[l], SHARD_AXIS[n] - 1)
            Cw[n, l] = jnp.zeros(Wb[n, l].shape, F32)

    trunk_s = {n: S[n] for n in S if n != 'final_norm'}
    h_out, vjp_fn = jax.vjp(lambda s, cw, xx: _trunk(s, cw, Wb, xx, mem[0]), trunk_s, Cw, x[0])
    loss_part, dh, dg_final = _final_call(h_out, S['final_norm'], loss_target[0], "final_loss")
    grads, big_grads, dx = vjp_fn(dh)
    grads = dict(grads)
    grads['final_norm'] = dg_final.reshape(final_norm.shape)
    loss = lax.psum(loss_part[0, 0], ("x", "y", "c"))

    out = {}

    def adamw(parts, names, call_name):
        if len(names) == 1:
            res = _adamw_call(parts, w_loc[names[0]], m_loc[names[0]], v_loc[names[0]], call_name)
            for kind, t in zip(("grad", "delta", "new_m", "new_v"), res):
                out[kind, names[0]] = t
        else:
            res = _adamw_call(parts, *[_pack([d[n] for n in names], F32) for d in (w_loc, m_loc, v_loc)], call_name)
            for kind, buf in zip(("grad", "delta", "new_m", "new_v"), res):
                for n, t in zip(names, _unpack(buf, [w_loc[n].shape for n in names])):
                    out[kind, n] = t

    send = [[_parts_from_full(big_grads[n, l], SHARD_AXIS[n] - 1) for l in range(w_loc[n].shape[0])] for n in BIG]
    small_parts = [_parts_from_full(grads[n], SHARD_AXIS[n]) for n in SMALL]
    send.append([jnp.stack([_pack([p[j] for p in small_parts], F32) for j in range(N_DEV)])])
    recv = _all_to_all(send, "grad_exchange")
    for n, r in zip(BIG, recv):
        adamw(r, [n], "adamw_" + n)
    adamw(recv[-1][:, 0], SMALL, "adamw_small")
    rep_g = _all_gather([_pack([grads[n] for n in REPLICATED], F32)[None]], "gather_rep_grads")[0][0]
    adamw(rep_g, REPLICATED, "adamw_replicated")

    return (loss, dx[None], *[out[kind, n] for kind in ("grad", "delta", "new_m", "new_v") for n in WEIGHT_NAMES])
```

```python
import functools
import math

import jax
import jax.numpy as jnp
from jax import lax
from jax.experimental import pallas as pl
from jax.experimental.pallas import tpu as pltpu

F32 = jnp.float32
BF16 = jnp.bfloat16
SDS = jax.ShapeDtypeStruct
BS = pl.BlockSpec

N_DEV = 8
NORM_EPS = 1e-6
ROPE_THETA = 500000.0
HEAD_DIM = 64
ROT_DIM = 16
ATT_BLOCK = 128
LRU_C = 8.0
LRU_HEADS = 4
DILATIONS = (1, 4, 16)
B_HEADS = 8
C_HEADS = 16
C_KV_HEADS = 2
XA_HEADS = 4
XA_HEAD_DIM = 128
NEG = -1e30
ADAM_LR, ADAM_B1, ADAM_B2, ADAM_EPS, ADAM_WD, ADAM_STEP = 0.001, 0.9, 0.999, 1e-08, 0.01, 10
LANES = 128
VMEM_LIMIT = 48 * 1024 * 1024

WEIGHT_NAMES = ['mix_norm', 'ab_w_in', 'lru_conv_w', 'lru_conv_b', 'lru_wa', 'lru_ba', 'lru_wx', 'lru_bx',
                'lru_lambda', 'ab_w_out', 'c_w_qkv', 'c_b_qkv', 'c_sinks', 'c_w_out', 'c_b_out', 'xa_norm',
                'xa_mem_norm', 'xa_wq', 'xa_wkv', 'xa_wo', 'ffn_norm', 'ffn_w_gate_up', 'ffn_w_down', 'final_norm']
SHARD_AXIS = {'ab_w_in': 2, 'lru_conv_w': 2, 'lru_wa': 2, 'lru_ba': 2, 'lru_wx': 2, 'lru_bx': 2, 'ab_w_out': 1,
              'c_w_qkv': 2, 'c_b_qkv': 1, 'c_w_out': 1, 'c_b_out': 1, 'xa_wq': 1, 'xa_wkv': 1, 'xa_wo': 2,
              'ffn_w_gate_up': 2, 'ffn_w_down': 1}
BIG = ['ab_w_in', 'lru_wa', 'lru_wx', 'ab_w_out', 'c_w_qkv', 'c_w_out', 'xa_wq', 'xa_wkv', 'xa_wo',
       'ffn_w_gate_up', 'ffn_w_down']
SMALL = ['lru_conv_w', 'lru_ba', 'lru_bx', 'c_b_qkv', 'c_b_out']
SHARDED = BIG + SMALL
REPLICATED = [n for n in WEIGHT_NAMES if n not in SHARD_AXIS]
PACK_QUANTUM = 2048


def _pcall(body, **kw):
    return pl.pallas_call(body, **kw)


def _cparams(sem=None):
    return pltpu.CompilerParams(dimension_semantics=sem, vmem_limit_bytes=VMEM_LIMIT)


def _tile(n, target, mult=LANES):
    if n <= target:
        return n
    t = (target // mult) * mult
    while t >= mult:
        if n % t == 0:
            return t
        t -= mult
    return n


def _sigmoid(x):
    return 1.0 / (1.0 + jnp.exp(-x))


def _expm1(x):
    small = x * (1.0 + x * (0.5 + x * (1.0 / 6.0 + x * (1.0 / 24.0))))
    return jnp.where(jnp.abs(x) < 0.03, small, jnp.exp(x) - 1.0)


_GELU_C = math.sqrt(2.0 / math.pi)


def _gelu_parts(y):
    y2 = y * y
    th = jnp.tanh(_GELU_C * (y + 0.044715 * y * y2))
    g = 0.5 * y * (1.0 + th)
    dg = 0.5 * (1.0 + th) + 0.5 * y * (1.0 - th * th) * _GELU_C * (1.0 + 3.0 * 0.044715 * y2)
    return g, dg


def _eye(n):
    return lax.broadcasted_iota(jnp.int32, (n, n), 0) == lax.broadcasted_iota(jnp.int32, (n, n), 1)


def _col_to_row(c):
    n = c.shape[1]
    return jnp.sum(jnp.where(_eye(n)[None], c, 0.0), axis=1, keepdims=True)


def _row_to_col(r):
    n = r.shape[2]
    return jnp.sum(jnp.where(_eye(n)[None], r, 0.0), axis=2, keepdims=True)


def _mm(a, b, *, ta=False, tb=False, bias=None, add=None, name):
    M, K = (a.shape[1], a.shape[0]) if ta else a.shape
    N = b.shape[0] if tb else b.shape[1]
    tm, tn, tk = _tile(M, 512), _tile(N, 512), _tile(K, 2048)
    nk = K // tk
    dn = (((0 if ta else 1,), (1 if tb else 0,)), ((), ()))

    def body(*refs):
        a_ref, b_ref = refs[0], refs[1]
        pos = 2
        bias_ref = add_ref = None
        if bias is not None:
            bias_ref = refs[pos]
            pos += 1
        if add is not None:
            add_ref = refs[pos]
            pos += 1
        o_ref, acc_ref = refs[pos], refs[pos + 1]
        k = pl.program_id(2)

        @pl.when(k == 0)
        def _():
            acc_ref[...] = jnp.zeros_like(acc_ref)

        acc_ref[...] += lax.dot_general(a_ref[...].astype(BF16), b_ref[...].astype(BF16), dn,
                                        preferred_element_type=F32)

        @pl.when(k == nk - 1)
        def _():
            r = acc_ref[...]
            if bias_ref is not None:
                r = r + bias_ref[...]
            if add_ref is not None:
                r = r + add_ref[...]
            o_ref[...] = r

    in_specs = [BS((tk, tm), lambda i, j, k: (k, i)) if ta else BS((tm, tk), lambda i, j, k: (i, k)),
                BS((tn, tk), lambda i, j, k: (j, k)) if tb else BS((tk, tn), lambda i, j, k: (k, j))]
    args = [a, b]
    if bias is not None:
        in_specs.append(BS((1, tn), lambda i, j, k: (0, j)))
        args.append(bias.reshape(1, N))
    if add is not None:
        in_specs.append(BS((tm, tn), lambda i, j, k: (i, j)))
        args.append(add)
    return _pcall(body, name=name, grid=(M // tm, N // tn, nk), in_specs=in_specs,
                  out_specs=BS((tm, tn), lambda i, j, k: (i, j)), out_shape=SDS((M, N), F32),
                  scratch_shapes=[pltpu.VMEM((tm, tn), F32)],
                  compiler_params=_cparams(("parallel", "parallel", "arbitrary")))(*args)


def _colsum(x, name):
    T, N = x.shape
    tt = _tile(T, 512, 8)

    def body(x_ref, o_ref):
        @pl.when(pl.program_id(0) == 0)
        def _():
            o_ref[...] = jnp.zeros_like(o_ref)

        o_ref[...] += jnp.sum(x_ref[...], axis=0, keepdims=True)

    return _pcall(body, name=name, grid=(T // tt,), in_specs=[BS((tt, N), lambda i: (i, 0))],
                  out_specs=BS((1, N), lambda i: (0, 0)), out_shape=SDS((1, N), F32),
                  compiler_params=_cparams(("arbitrary",)))(x)


@functools.partial(jax.custom_vjp, nondiff_argnums=(5,))
def linear(a, wb, wc, bias, add, name):
    return _mm(a, wb, bias=bias, add=add, name=name)


def _linear_fwd(a, wb, wc, bias, add, name):
    return _mm(a, wb, bias=bias, add=add, name=name), (a, wb, bias is not None, add is not None)


def _linear_bwd(name, res, g):
    a, wb, has_bias, has_add = res
    da = _mm(g, wb, tb=True, name=name + "_da")
    dw = _mm(a, g, ta=True, name=name + "_dw")
    dbias = _colsum(g, name + "_db").reshape(-1) if has_bias else None
    return da, jnp.zeros_like(wb), dw, dbias, (g if has_add else None)


linear.defvjp(_linear_fwd, _linear_bwd)


def _rms_fwd_call(x, g, name):
    T, Dm = x.shape
    tt = _tile(T, 512, 8)

    def body(x_ref, g_ref, o_ref):
        xv = x_ref[...]
        r = lax.rsqrt(jnp.mean(xv * xv, axis=-1, keepdims=True) + NORM_EPS)
        o_ref[...] = (xv * r) * g_ref[...]

    return _pcall(body, name=name, grid=(T // tt,),
                  in_specs=[BS((tt, Dm), lambda i: (i, 0)), BS((1, Dm), lambda i: (0, 0))],
                  out_specs=BS((tt, Dm), lambda i: (i, 0)), out_shape=SDS((T, Dm), F32),
                  compiler_params=_cparams(("parallel",)))(x, g.reshape(1, Dm))


def _rms_bwd_call(x, g, dy, name):
    T, Dm = x.shape
    tt = _tile(T, 512, 8)

    def body(x_ref, g_ref, dy_ref, dx_ref, dg_ref):
        xv = x_ref[...]
        r = lax.rsqrt(jnp.mean(xv * xv, axis=-1, keepdims=True) + NORM_EPS)
        xh = xv * r
        dy = dy_ref[...]
        dyg = dy * g_ref[...]
        dx_ref[...] = r * (dyg - xh * jnp.mean(dyg * xh, axis=-1, keepdims=True))

        @pl.when(pl.program_id(0) == 0)
        def _():
            dg_ref[...] = jnp.zeros_like(dg_ref)

        dg_ref[...] += jnp.sum(dy * xh, axis=0, keepdims=True)

    row = BS((tt, Dm), lambda i: (i, 0))
    vec = BS((1, Dm), lambda i: (0, 0))
    return _pcall(body, name=name, grid=(T // tt,), in_specs=[row, vec, row], out_specs=[row, vec],
                  out_shape=[SDS((T, Dm), F32), SDS((1, Dm), F32)],
                  compiler_params=_cparams(("arbitrary",)))(x, g.reshape(1, Dm), dy)


@functools.partial(jax.custom_vjp, nondiff_argnums=(2,))
def rmsnorm(x, g, name):
    return _rms_fwd_call(x, g, name)


def _rmsnorm_fwd(x, g, name):
    return _rms_fwd_call(x, g, name), (x, g)


def _rmsnorm_bwd(name, res, dy):
    x, g = res
    dx, dg = _rms_bwd_call(x, g, dy, name + "_bwd")
    return dx, dg.reshape(g.shape)


rmsnorm.defvjp(_rmsnorm_fwd, _rmsnorm_bwd)


def _rope_tables(T):
    half = ROT_DIM // 2
    inv = ROPE_THETA ** (-jnp.arange(0, ROT_DIM, 2, dtype=F32) / ROT_DIM)
    ang = jnp.arange(T, dtype=F32)[:, None] * inv[None, :]
    cos, sin = jnp.cos(ang), jnp.sin(ang)
    ones = jnp.ones((T, HEAD_DIM - ROT_DIM), F32)
    zeros = jnp.zeros((T, HEAD_DIM - ROT_DIM), F32)
    zh = jnp.zeros((T, half), F32)
    c = jnp.concatenate([cos, cos, ones], axis=1)
    sa = jnp.concatenate([zh, sin, zeros], axis=1)
    sb = jnp.concatenate([-sin, zh, zeros], axis=1)
    two = lambda t: jnp.concatenate([t, t], axis=1)
    return two(c), two(sa), two(sb)


def _rope_call(x, tabs, inverse, name):
    T, W = x.shape
    tt = _tile(T, 512, 8)
    reps = W // LANES
    half = ROT_DIM // 2

    def body(x_ref, c_ref, sa_ref, sb_ref, o_ref):
        xv = x_ref[...]
        c = jnp.tile(c_ref[...], (1, reps))
        sa = jnp.tile(sa_ref[...], (1, reps))
        sb = jnp.tile(sb_ref[...], (1, reps))
        if not inverse:
            o_ref[...] = xv * c + pltpu.roll(xv, half, axis=1) * sa + pltpu.roll(xv, W - half, axis=1) * sb
        else:
            o_ref[...] = xv * c + pltpu.roll(xv * sa, W - half, axis=1) + pltpu.roll(xv * sb, half, axis=1)

    row = BS((tt, W), lambda i: (i, 0))
    tab = BS((tt, LANES), lambda i: (i, 0))
    return _pcall(body, name=name, grid=(T // tt,), in_specs=[row, tab, tab, tab], out_specs=row,
                  out_shape=SDS((T, W), F32), compiler_params=_cparams(("parallel",)))(x, *tabs)


@functools.partial(jax.custom_vjp, nondiff_argnums=(2,))
def rope(x, tabs, name):
    return _rope_call(x, tabs, False, name)


def _rope_fwd(x, tabs, name):
    return _rope_call(x, tabs, False, name), tabs


def _rope_bwd(name, tabs, dy):
    return _rope_call(dy, tabs, True, name + "_bwd"), jax.tree.map(jnp.zeros_like, tabs)


rope.defvjp(_rope_fwd, _rope_bwd)


def _conv_fwd_call(x, w, b, name):
    T, C = x.shape
    tt = _tile(T, 512, 8)
    per = tt // 8

    def body(x_ref, halo_ref, w_ref, b_ref, o_ref):
        i = pl.program_id(0)
        halo = jnp.where(i > 0, halo_ref[...], 0.0)
        e = jnp.concatenate([halo, x_ref[...]], axis=0)
        acc = b_ref[...]
        for k in (3, 2, 1):
            acc = acc + pltpu.roll(e, k, axis=0)[8:, :] * w_ref[3 - k:4 - k, :]
        o_ref[...] = acc + x_ref[...] * w_ref[3:4, :]

    return _pcall(body, name=name, grid=(T // tt,),
                  in_specs=[BS((tt, C), lambda i: (i, 0)), BS((8, C), lambda i: (jnp.maximum(i * per - 1, 0), 0)),
                            BS((4, C), lambda i: (0, 0)), BS((1, C), lambda i: (0, 0))],
                  out_specs=BS((tt, C), lambda i: (i, 0)), out_shape=SDS((T, C), F32),
                  compiler_params=_cparams(("parallel",)))(x, x, w, b.reshape(1, C))


def _conv_bwd_call(x, w, dy, name):
    T, C = x.shape
    tt = _tile(T, 512, 8)
    per = tt // 8
    nt = T // tt

    def body(x_ref, halo_ref, w_ref, dy_ref, nxt_ref, dx_ref, dwb_ref):
        i = pl.program_id(0)
        halo = jnp.where(i > 0, halo_ref[...], 0.0)
        e = jnp.concatenate([halo, x_ref[...]], axis=0)
        dy = dy_ref[...]
        nxt = jnp.where(i < nt - 1, nxt_ref[...], 0.0)
        f = jnp.concatenate([dy, nxt], axis=0)
        dx = dy * w_ref[3:4, :]
        rows = [None] * 4
        rows[3] = jnp.sum(dy * x_ref[...], axis=0, keepdims=True)
        for k in (1, 2, 3):
            dx = dx + pltpu.roll(f, tt + 8 - k, axis=0)[:tt, :] * w_ref[3 - k:4 - k, :]
            rows[3 - k] = jnp.sum(dy * pltpu.roll(e, k, axis=0)[8:, :], axis=0, keepdims=True)
        dx_ref[...] = dx
        upd = jnp.concatenate(rows + [jnp.sum(dy, axis=0, keepdims=True), jnp.zeros((3, C), F32)], axis=0)

        @pl.when(i == 0)
        def _():
            dwb_ref[...] = jnp.zeros_like(dwb_ref)

        dwb_ref[...] += upd

    row = BS((tt, C), lambda i: (i, 0))
    return _pcall(body, name=name, grid=(nt,),
                  in_specs=[row, BS((8, C), lambda i: (jnp.maximum(i * per - 1, 0), 0)), BS((4, C), lambda i: (0, 0)),
                            row, BS((8, C), lambda i: (jnp.minimum((i + 1) * per, T // 8 - 1), 0))],
                  out_specs=[row, BS((8, C), lambda i: (0, 0))],
                  out_shape=[SDS((T, C), F32), SDS((8, C), F32)],
                  compiler_params=_cparams(("arbitrary",)))(x, x, w, dy, dy)


@functools.partial(jax.custom_vjp, nondiff_argnums=(3,))
def conv4(x, w, b, name):
    return _conv_fwd_call(x, w, b, name)


def _conv4_fwd(x, w, b, name):
    return _conv_fwd_call(x, w, b, name), (x, w)


def _conv4_bwd(name, res, dy):
    x, w = res
    dx, dwb = _conv_bwd_call(x, w, dy, name + "_bwd")
    return dx, dwb[0:4], dwb[4]


conv4.defvjp(_conv4_fwd, _conv4_bwd)


def _gates_fwd_call(xc, wa, ba, wx, bx, name):
    T, C = xc.shape
    hd = C // LRU_HEADS
    tt = _tile(T, 512, 8)

    def body(x_ref, wa_ref, ba_ref, wx_ref, bx_ref, ga_ref, gx_ref):
        xb = x_ref[...].astype(BF16)
        ga_ref[...] = jnp.dot(xb, wa_ref[0].astype(BF16), preferred_element_type=F32) + ba_ref[...]
        gx_ref[...] = jnp.dot(xb, wx_ref[0].astype(BF16), preferred_element_type=F32) + bx_ref[...]

    blk = BS((tt, hd), lambda i, h: (i, h))
    wsp = BS((1, hd, hd), lambda i, h: (h, 0, 0))
    bsp = BS((1, hd), lambda i, h: (0, h))
    return _pcall(body, name=name, grid=(T // tt, LRU_HEADS), in_specs=[blk, wsp, bsp, wsp, bsp],
                  out_specs=[blk, blk], out_shape=[SDS((T, C), F32)] * 2,
                  compiler_params=_cparams(("parallel", "parallel")))(xc, wa, ba.reshape(1, C), wx, bx.reshape(1, C))


def _gates_bwd_x_call(dga, dgx, wa, wx, name):
    T, C = dga.shape
    hd = C // LRU_HEADS
    tt = _tile(T, 512, 8)
    dn = (((1,), (1,)), ((), ()))

    def body(da_ref, dx_ref, wa_ref, wx_ref, o_ref):
        o_ref[...] = (lax.dot_general(da_ref[...].astype(BF16), wa_ref[0].astype(BF16), dn, preferred_element_type=F32)
                      + lax.dot_general(dx_ref[...].astype(BF16), wx_ref[0].astype(BF16), dn, preferred_element_type=F32))

    blk = BS((tt, hd), lambda i, h: (i, h))
    wsp = BS((1, hd, hd), lambda i, h: (h, 0, 0))
    return _pcall(body, name=name, grid=(T // tt, LRU_HEADS), in_specs=[blk, blk, wsp, wsp], out_specs=blk,
                  out_shape=SDS((T, C), F32), compiler_params=_cparams(("parallel", "parallel")))(dga, dgx, wa, wx)


def _gates_bwd_w_call(xc, dga, dgx, name):
    T, C = xc.shape
    hd = C // LRU_HEADS
    tt = _tile(T, 512, 8)
    dn = (((0,), (0,)), ((), ()))

    def body(x_ref, da_ref, dx_ref, dwa_ref, dwx_ref, dba_ref, dbx_ref):
        @pl.when(pl.program_id(1) == 0)
        def _():
            dwa_ref[...] = jnp.zeros_like(dwa_ref)
            dwx_ref[...] = jnp.zeros_like(dwx_ref)
            dba_ref[...] = jnp.zeros_like(dba_ref)
            dbx_ref[...] = jnp.zeros_like(dbx_ref)

        xb = x_ref[...].astype(BF16)
        da, dx = da_ref[...], dx_ref[...]
        dwa_ref[0] += lax.dot_general(xb, da.astype(BF16), dn, preferred_element_type=F32)
        dwx_ref[0] += lax.dot_general(xb, dx.astype(BF16), dn, preferred_element_type=F32)
        dba_ref[...] += jnp.sum(da, axis=0, keepdims=True)
        dbx_ref[...] += jnp.sum(dx, axis=0, keepdims=True)

    blk = BS((tt, hd), lambda h, i: (i, h))
    wsp = BS((1, hd, hd), lambda h, i: (h, 0, 0))
    bsp = BS((1, hd), lambda h, i: (0, h))
    return _pcall(body, name=name, grid=(LRU_HEADS, T // tt), in_specs=[blk, blk, blk],
                  out_specs=[wsp, wsp, bsp, bsp],
                  out_shape=[SDS((LRU_HEADS, hd, hd), F32)] * 2 + [SDS((1, C), F32)] * 2,
                  compiler_params=_cparams(("parallel", "arbitrary")))(xc, dga, dgx)


@functools.partial(jax.custom_vjp, nondiff_argnums=(7,))
def lru_gates(xc, wa, wa_c, ba, wx, wx_c, bx, name):
    return tuple(_gates_fwd_call(xc, wa, ba, wx, bx, name))


def _lru_gates_fwd(xc, wa, wa_c, ba, wx, wx_c, bx, name):
    return tuple(_gates_fwd_call(xc, wa, ba, wx, bx, name)), (xc, wa, wx, ba.shape)


def _lru_gates_bwd(name, res, g):
    xc, wa, wx, bshape = res
    dga, dgx = g
    dxc = _gates_bwd_x_call(dga, dgx, wa, wx, name + "_dx")
    dwa, dwx, dba, dbx = _gates_bwd_w_call(xc, dga, dgx, name + "_dw")
    return dxc, jnp.zeros_like(wa), dwa, dba.reshape(bshape), jnp.zeros_like(wx), dwx, dbx.reshape(bshape)


lru_gates.defvjp(_lru_gates_fwd, _lru_gates_bwd)


def _lru_coeffs(xc, ga, gx, lam):
    r = _sigmoid(ga)
    ig = _sigmoid(gx)
    z = -lam
    sp = jnp.maximum(z, 0.0) + jnp.log(1.0 + jnp.exp(-jnp.abs(z)))
    la = -LRU_C * r * sp
    a = jnp.exp(la)
    s = jnp.sqrt(-_expm1(2.0 * la))
    return r, ig, sp, a, s


LRU_TT = 256


def _scan_fwd_call(xc, ga, gx, y, lam, name):
    T, C = xc.shape
    tt = _tile(T, LRU_TT, 8)

    def body(xc_ref, ga_ref, gx_ref, y_ref, lam_ref, h_ref, rec_ref, a_buf, carry):
        @pl.when(pl.program_id(0) == 0)
        def _():
            carry[...] = jnp.zeros_like(carry)

        xcv = xc_ref[...]
        _, ig, _, a, s = _lru_coeffs(xcv, ga_ref[...], gx_ref[...], lam_ref[...])
        a_buf[...] = a
        h_ref[...] = s * (ig * xcv)

        def step(t, h):
            hn = a_buf[pl.ds(t, 1), :] * h + h_ref[pl.ds(t, 1), :]
            h_ref[pl.ds(t, 1), :] = hn
            return hn

        carry[0:1, :] = lax.fori_loop(0, tt, step, carry[0:1, :], unroll=8)
        g, _ = _gelu_parts(y_ref[...])
        rec_ref[...] = h_ref[...] * g

    row = BS((tt, C), lambda i: (i, 0))
    vec = BS((1, C), lambda i: (0, 0))
    return _pcall(body, name=name, grid=(T // tt,), in_specs=[row, row, row, row, vec], out_specs=[row, row],
                  out_shape=[SDS((T, C), F32)] * 2,
                  scratch_shapes=[pltpu.VMEM((tt, C), F32), pltpu.VMEM((8, C), F32)],
                  compiler_params=_cparams(("arbitrary",)))(xc, ga, gx, y, lam.reshape(1, C))


def _scan_bwd_call(xc, ga, gx, y, lam, h, drec, name):
    T, C = xc.shape
    tt = _tile(T, LRU_TT, 8)
    nt = T // tt
    per = tt // 8

    def body(xc_ref, ga_ref, gx_ref, y_ref, lam_ref, h_ref, halo_ref, dr_ref,
             dga_ref, dgx_ref, dxc_ref, dy_ref, dlam_ref, a_buf, g_buf, carry):
        i = pl.program_id(0)

        @pl.when(i == 0)
        def _():
            carry[...] = jnp.zeros_like(carry)
            dlam_ref[...] = jnp.zeros_like(dlam_ref)

        xcv, lam = xc_ref[...], lam_ref[...]
        r, ig, sp, a, s = _lru_coeffs(xcv, ga_ref[...], gx_ref[...], lam)
        gel, dgel = _gelu_parts(y_ref[...])
        drec = dr_ref[...]
        hv = h_ref[...]
        dy_ref[...] = drec * hv * dgel
        a_buf[...] = a
        g_buf[...] = drec * gel

        def step(j, q):
            t = tt - 1 - j
            g = g_buf[pl.ds(t, 1), :] + q
            g_buf[pl.ds(t, 1), :] = g
            return a_buf[pl.ds(t, 1), :] * g

        carry[0:1, :] = lax.fori_loop(0, tt, step, carry[0:1, :], unroll=8)
        g = g_buf[...]
        halo = jnp.where(i < nt - 1, halo_ref[...], 0.0)
        hprev = pltpu.roll(jnp.concatenate([halo, hv], axis=0), 1, axis=0)[8:, :]
        da = g * hprev
        dig = g * s * xcv
        ds = g * ig * xcv
        dla = da * a - ds * (a * a) / s
        dga_ref[...] = dla * (-LRU_C * sp) * r * (1.0 - r)
        dgx_ref[...] = dig * ig * (1.0 - ig)
        dxc_ref[...] = g * s * ig
        dlam_ref[...] += jnp.sum(dla * r, axis=0, keepdims=True) * (LRU_C * _sigmoid(-lam))

    row = BS((tt, C), lambda i: (nt - 1 - i, 0))
    vec = BS((1, C), lambda i: (0, 0))
    halo = BS((8, C), lambda i: (jnp.maximum((nt - 1 - i) * per - 1, 0), 0))
    return _pcall(body, name=name, grid=(nt,), in_specs=[row, row, row, row, vec, row, halo, row],
                  out_specs=[row, row, row, row, vec], out_shape=[SDS((T, C), F32)] * 4 + [SDS((1, C), F32)],
                  scratch_shapes=[pltpu.VMEM((tt, C), F32), pltpu.VMEM((tt, C), F32), pltpu.VMEM((8, C), F32)],
                  compiler_params=_cparams(("arbitrary",)))(xc, ga, gx, y, lam.reshape(1, C), h, h, drec)


@functools.partial(jax.custom_vjp, nondiff_argnums=(5,))
def lru_scan(xc, ga, gx, y, lam, name):
    return _scan_fwd_call(xc, ga, gx, y, lam, name)[1]


def _lru_scan_fwd(xc, ga, gx, y, lam, name):
    h, rec = _scan_fwd_call(xc, ga, gx, y, lam, name)
    return rec, (xc, ga, gx, y, lam, h)


def _lru_scan_bwd(name, res, drec):
    xc, ga, gx, y, lam, h = res
    dga, dgx, dxc, dy, dlam = _scan_bwd_call(xc, ga, gx, y, lam, h, drec, name + "_bwd")
    return dxc, dga, dgx, dy, dlam.reshape(lam.shape)


lru_scan.defvjp(_lru_scan_fwd, _lru_scan_bwd)


ATT_HB = 8


def _band_scores(qb, kc, kp, n, max_dist, scale):
    hb = qb.shape[0]
    dn = (((2,), (2,)), ((0,), (0,)))
    s_c = lax.dot_general(qb, kc, dn, preferred_element_type=F32) * scale
    s_p = lax.dot_general(qb, kp, dn, preferred_element_type=F32) * scale
    qi = lax.broadcasted_iota(jnp.int32, (1, ATT_BLOCK, ATT_BLOCK), 1)
    kj = lax.broadcasted_iota(jnp.int32, (1, ATT_BLOCK, ATT_BLOCK), 2)
    s_c = jnp.where(kj <= qi, s_c, NEG)
    s_p = jnp.where((kj >= qi + (ATT_BLOCK - max_dist)) & (n > 0), s_p, NEG)
    return s_c, s_p


def _kv_block(ref, hb):
    v = ref[...].astype(BF16)
    if v.shape[0] != hb:
        v = jnp.broadcast_to(v, (hb,) + v.shape[1:])
    return v


def _band_fwd_call(q, k, v, sinks, max_dist, name):
    H, L, hd = q.shape
    G = H // k.shape[0]
    hb = ATT_HB
    kb = hb // G
    nb = L // ATT_BLOCK
    scale = hd ** -0.5
    dn_pv = (((2,), (1,)), ((0,), (0,)))

    def body(q_ref, kc_ref, kp_ref, vc_ref, vp_ref, sk_ref, o_ref, lse_ref):
        n = pl.program_id(1)
        qb = q_ref[...].astype(BF16)
        s_c, s_p = _band_scores(qb, _kv_block(kc_ref, hb), _kv_block(kp_ref, hb), n, max_dist, scale)
        sk = sk_ref[...]
        m = jnp.maximum(jnp.maximum(jnp.max(s_c, axis=-1, keepdims=True), jnp.max(s_p, axis=-1, keepdims=True)), sk)
        e_c, e_p = jnp.exp(s_c - m), jnp.exp(s_p - m)
        den = jnp.sum(e_c, axis=-1, keepdims=True) + jnp.sum(e_p, axis=-1, keepdims=True) + jnp.exp(sk - m)
        inv = 1.0 / den
        o_ref[...] = (lax.dot_general((e_c * inv).astype(BF16), _kv_block(vc_ref, hb), dn_pv, preferred_element_type=F32)
                      + lax.dot_general((e_p * inv).astype(BF16), _kv_block(vp_ref, hb), dn_pv, preferred_element_type=F32))
        lse_ref[:, 0, :, :] = _col_to_row(m + jnp.log(den))

    qsp = BS((hb, ATT_BLOCK, hd), lambda h, n: (h, n, 0))
    csp = BS((kb, ATT_BLOCK, hd), lambda h, n: (h, n, 0))
    psp = BS((kb, ATT_BLOCK, hd), lambda h, n: (h, jnp.maximum(n - 1, 0), 0))
    return _pcall(body, name=name, grid=(H // hb, nb),
                  in_specs=[qsp, csp, psp, csp, psp, BS((hb, 1, 1), lambda h, n: (h, 0, 0))],
                  out_specs=[qsp, BS((hb, 1, 1, ATT_BLOCK), lambda h, n: (h, n, 0, 0))],
                  out_shape=[SDS((H, L, hd), F32), SDS((H, nb, 1, ATT_BLOCK), F32)],
                  compiler_params=_cparams(("parallel", "parallel")))(q, k, k, v, v, sinks)


def _band_bwd_call(q, k, v, sinks, o, lse, do, dlse, max_dist, name):
    H, L, hd = q.shape
    G = H // k.shape[0]
    hb = ATT_HB
    kb = hb // G
    nb = L // ATT_BLOCK
    scale = hd ** -0.5
    dn_nt = (((2,), (2,)), ((0,), (0,)))
    dn_nn = (((2,), (1,)), ((0,), (0,)))
    dn_tn = (((1,), (1,)), ((0,), (0,)))

    def group_sum(t):
        return t if kb == hb else jnp.sum(t, axis=0, keepdims=True)

    def body(q_ref, kc_ref, kp_ref, vc_ref, vp_ref, sk_ref, o_ref, lse_ref, do_ref, dlse_ref,
             dq_ref, dk_ref, dv_ref, dsk_ref, ck, cv):
        n = pl.program_id(1)

        @pl.when(n == 0)
        def _():
            ck[...] = jnp.zeros_like(ck)
            cv[...] = jnp.zeros_like(cv)
            dsk_ref[...] = jnp.zeros_like(dsk_ref)

        @pl.when(n < nb)
        def _():
            qb = q_ref[...].astype(BF16)
            kc, kp = _kv_block(kc_ref, hb), _kv_block(kp_ref, hb)
            vc, vp = _kv_block(vc_ref, hb), _kv_block(vp_ref, hb)
            s_c, s_p = _band_scores(qb, kc, kp, n, max_dist, scale)
            lse = _row_to_col(lse_ref[:, 0, :, :])
            dlse = _row_to_col(dlse_ref[:, 0, :, :])
            p_c, p_p = jnp.exp(s_c - lse), jnp.exp(s_p - lse)
            dov = do_ref[...]
            dob = dov.astype(BF16)
            corr = dlse - jnp.sum(dov * o_ref[...], axis=-1, keepdims=True)
            ds_c = p_c * (lax.dot_general(dob, vc, dn_nt, preferred_element_type=F32) + corr)
            ds_p = p_p * (lax.dot_general(dob, vp, dn_nt, preferred_element_type=F32) + corr)
            ds_cb, ds_pb = ds_c.astype(BF16), ds_p.astype(BF16)
            dq_ref[...] = (lax.dot_general(ds_cb, kc, dn_nn, preferred_element_type=F32)
                           + lax.dot_general(ds_pb, kp, dn_nn, preferred_element_type=F32)) * scale
            dk_p = lax.dot_general(ds_pb, qb, dn_tn, preferred_element_type=F32) * scale
            dk_c = lax.dot_general(ds_cb, qb, dn_tn, preferred_element_type=F32) * scale
            dv_p = lax.dot_general(p_p.astype(BF16), dob, dn_tn, preferred_element_type=F32)
            dv_c = lax.dot_general(p_c.astype(BF16), dob, dn_tn, preferred_element_type=F32)
            dk_ref[...] = ck[...] + group_sum(dk_p)
            dv_ref[...] = cv[...] + group_sum(dv_p)
            ck[...] = group_sum(dk_c)
            cv[...] = group_sum(dv_c)
            dsk = jnp.sum(jnp.exp(sk_ref[...] - lse) * corr, axis=1, keepdims=True)
            dsk_ref[...] += jnp.broadcast_to(dsk, dsk_ref.shape)

        @pl.when(n == nb)
        def _():
            dk_ref[...] = ck[...]
            dv_ref[...] = cv[...]

    cur = lambda n: jnp.minimum(n, nb - 1)
    qsp = BS((hb, ATT_BLOCK, hd), lambda h, n: (h, cur(n), 0))
    csp = BS((kb, ATT_BLOCK, hd), lambda h, n: (h, cur(n), 0))
    psp = BS((kb, ATT_BLOCK, hd), lambda h, n: (h, jnp.maximum(cur(n) - 1, 0), 0))
    lsp = BS((hb, 1, 1, ATT_BLOCK), lambda h, n: (h, cur(n), 0, 0))
    osp = BS((kb, ATT_BLOCK, hd), lambda h, n: (h, jnp.maximum(n - 1, 0), 0))
    return _pcall(body, name=name, grid=(H // hb, nb + 1),
                  in_specs=[qsp, csp, psp, csp, psp, BS((hb, 1, 1), lambda h, n: (h, 0, 0)), qsp, lsp, qsp, lsp],
                  out_specs=[qsp, osp, osp, BS((hb, 1, LANES), lambda h, n: (h, 0, 0))],
                  out_shape=[SDS(q.shape, F32), SDS(k.shape, F32), SDS(v.shape, F32), SDS((H, 1, LANES), F32)],
                  scratch_shapes=[pltpu.VMEM((kb, ATT_BLOCK, hd), F32)] * 2,
                  compiler_params=_cparams(("parallel", "arbitrary")))(q, k, k, v, v, sinks, o, lse, do, dlse)


@functools.partial(jax.custom_vjp, nondiff_argnums=(4, 5))
def band_attention(q, k, v, sinks, max_dist, name):
    return tuple(_band_fwd_call(q, k, v, sinks, max_dist, name))


def _band_attention_fwd(q, k, v, sinks, max_dist, name):
    o, lse = _band_fwd_call(q, k, v, sinks, max_dist, name)
    return (o, lse), (q, k, v, sinks, o, lse)


def _band_attention_bwd(max_dist, name, res, g):
    q, k, v, sinks, o, lse = res
    do, dlse = g
    dq, dk, dv, dsk = _band_bwd_call(q, k, v, sinks, o, lse, do, dlse, max_dist, name + "_bwd")
    return dq, dk, dv, dsk[:, :, 0:1]


band_attention.defvjp(_band_attention_fwd, _band_attention_bwd)


def _merge_weights(l_refs):
    ls = [_row_to_col(r[:, 0, :, :]) for r in l_refs]
    mx = jnp.maximum(jnp.maximum(ls[0], ls[1]), ls[2])
    es = [jnp.exp(l - mx) for l in ls]
    inv = 1.0 / (es[0] + es[1] + es[2])
    return [e * inv for e in es]


def _merge_fwd_call(os_, ls_, name):
    H, L, hd = os_[0].shape
    nb = L // ATT_BLOCK

    def body(o1, o2, o3, l1, l2, l3, out_ref):
        w = _merge_weights((l1, l2, l3))
        out_ref[...] = w[0] * o1[...] + w[1] * o2[...] + w[2] * o3[...]

    osp = BS((H, ATT_BLOCK, hd), lambda n: (0, n, 0))
    lsp = BS((H, 1, 1, ATT_BLOCK), lambda n: (0, n, 0, 0))
    return _pcall(body, name=name, grid=(nb,), in_specs=[osp] * 3 + [lsp] * 3, out_specs=osp,
                  out_shape=SDS((H, L, hd), F32), compiler_params=_cparams(("parallel",)))(*os_, *ls_)


def _merge_bwd_call(os_, ls_, do, name):
    H, L, hd = os_[0].shape
    nb = L // ATT_BLOCK

    def body(o1, o2, o3, l1, l2, l3, do_ref, d1, d2, d3, e1, e2, e3):
        w = _merge_weights((l1, l2, l3))
        dov = do_ref[...]
        dws = [jnp.sum(dov * o[...], axis=-1, keepdims=True) for o in (o1, o2, o3)]
        mean = w[0] * dws[0] + w[1] * dws[1] + w[2] * dws[2]
        for wi, dwi, dref, eref in zip(w, dws, (d1, d2, d3), (e1, e2, e3)):
            dref[...] = wi * dov
            eref[:, 0, :, :] = _col_to_row(wi * (dwi - mean))

    osp = BS((H, ATT_BLOCK, hd), lambda n: (0, n, 0))
    lsp = BS((H, 1, 1, ATT_BLOCK), lambda n: (0, n, 0, 0))
    return _pcall(body, name=name, grid=(nb,), in_specs=[osp] * 3 + [lsp] * 3 + [osp], out_specs=[osp] * 3 + [lsp] * 3,
                  out_shape=[SDS((H, L, hd), F32)] * 3 + [SDS((H, nb, 1, ATT_BLOCK), F32)] * 3,
                  compiler_params=_cparams(("parallel",)))(*os_, *ls_, do)


@functools.partial(jax.custom_vjp, nondiff_argnums=(2,))
def merge3(os_, ls_, name):
    return _merge_fwd_call(os_, ls_, name)


def _merge3_fwd(os_, ls_, name):
    return _merge_fwd_call(os_, ls_, name), (os_, ls_)


def _merge3_bwd(name, res, do):
    os_, ls_ = res
    out = _merge_bwd_call(os_, ls_, do, name + "_bwd")
    return tuple(out[:3]), tuple(out[3:])


merge3.defvjp(_merge3_fwd, _merge3_bwd)


def _xa_probs(qb, kb, scale):
    s = lax.dot_general(qb, kb, (((1,), (1,)), ((), ())), preferred_element_type=F32) * scale
    e = jnp.exp(s - jnp.max(s, axis=-1, keepdims=True))
    return e / jnp.sum(e, axis=-1, keepdims=True)


def _xa_fwd_call(q, kv, name):
    T, W = q.shape
    M = kv.shape[0]
    hd = XA_HEAD_DIM
    tq = _tile(T, 512, 8)
    scale = hd ** -0.5

    def body(q_ref, k_ref, v_ref, o_ref):
        p = _xa_probs(q_ref[...].astype(BF16), k_ref[...].astype(BF16), scale)
        o_ref[...] = jnp.dot(p.astype(BF16), v_ref[...].astype(BF16), preferred_element_type=F32)

    qsp = BS((tq, hd), lambda i, h: (i, h))
    return _pcall(body, name=name, grid=(T // tq, XA_HEADS),
                  in_specs=[qsp, BS((M, hd), lambda i, h: (0, h)), BS((M, hd), lambda i, h: (0, XA_HEADS + h))],
                  out_specs=qsp, out_shape=SDS((T, W), F32),
                  compiler_params=_cparams(("parallel", "parallel")))(q, kv, kv)


def _xa_bwd_call(q, kv, do, name):
    T, W = q.shape
    M = kv.shape[0]
    hd = XA_HEAD_DIM
    tq = _tile(T, 512, 8)
    scale = hd ** -0.5
    dn_nt = (((1,), (1,)), ((), ()))
    dn_tn = (((0,), (0,)), ((), ()))

    def body(q_ref, k_ref, v_ref, do_ref, dq_ref, dk_ref, dv_ref):
        @pl.when(pl.program_id(1) == 0)
        def _():
            dk_ref[...] = jnp.zeros_like(dk_ref)
            dv_ref[...] = jnp.zeros_like(dv_ref)

        qb, kb, vb = q_ref[...].astype(BF16), k_ref[...].astype(BF16), v_ref[...].astype(BF16)
        p = _xa_probs(qb, kb, scale)
        dob = do_ref[...].astype(BF16)
        dp = lax.dot_general(dob, vb, dn_nt, preferred_element_type=F32)
        ds = (p * (dp - jnp.sum(p * dp, axis=-1, keepdims=True))).astype(BF16)
        dq_ref[...] = jnp.dot(ds, kb, preferred_element_type=F32) * scale
        dk_ref[...] += lax.dot_general(ds, qb, dn_tn, preferred_element_type=F32) * scale
        dv_ref[...] += lax.dot_general(p.astype(BF16), dob, dn_tn, preferred_element_type=F32)

    qsp = BS((tq, hd), lambda h, i: (i, h))
    ksp = BS((M, hd), lambda h, i: (0, h))
    return _pcall(body, name=name, grid=(XA_HEADS, T // tq),
                  in_specs=[qsp, ksp, BS((M, hd), lambda h, i: (0, XA_HEADS + h)), qsp],
                  out_specs=[qsp, ksp, ksp], out_shape=[SDS((T, W), F32), SDS((M, W), F32), SDS((M, W), F32)],
                  compiler_params=_cparams(("parallel", "arbitrary")))(q, kv, kv, do)


@functools.partial(jax.custom_vjp, nondiff_argnums=(2,))
def cross_attention(q, kv, name):
    return _xa_fwd_call(q, kv, name)


def _cross_attention_fwd(q, kv, name):
    return _xa_fwd_call(q, kv, name), (q, kv)


def _cross_attention_bwd(name, res, do):
    q, kv = res
    dq, dk, dv = _xa_bwd_call(q, kv, do, name + "_bwd")
    return dq, jnp.concatenate([dk, dv], axis=1)


cross_attention.defvjp(_cross_attention_fwd, _cross_attention_bwd)


def _swiglu_fwd_call(gu, name):
    T, W2 = gu.shape
    F = W2 // 2
    tt = _tile(T, 256, 8)

    def body(g_ref, u_ref, o_ref):
        g = g_ref[...]
        o_ref[...] = (g * _sigmoid(g)) * u_ref[...]

    return _pcall(body, name=name, grid=(T // tt,),
                  in_specs=[BS((tt, F), lambda i: (i, 0)), BS((tt, F), lambda i: (i, 1))],
                  out_specs=BS((tt, F), lambda i: (i, 0)), out_shape=SDS((T, F), F32),
                  compiler_params=_cparams(("parallel",)))(gu, gu)


def _swiglu_bwd_call(gu, dact, name):
    T, W2 = gu.shape
    F = W2 // 2
    tt = _tile(T, 128, 8)

    def body(g_ref, u_ref, d_ref, o_ref):
        g, d = g_ref[...], d_ref[...]
        sg = _sigmoid(g)
        o_ref[:, :F] = d * u_ref[...] * (sg * (1.0 + g * (1.0 - sg)))
        o_ref[:, F:] = d * (g * sg)

    return _pcall(body, name=name, grid=(T // tt,),
                  in_specs=[BS((tt, F), lambda i: (i, 0)), BS((tt, F), lambda i: (i, 1)), BS((tt, F), lambda i: (i, 0))],
                  out_specs=BS((tt, W2), lambda i: (i, 0)), out_shape=SDS((T, W2), F32),
                  compiler_params=_cparams(("parallel",)))(gu, gu, dact)


@functools.partial(jax.custom_vjp, nondiff_argnums=(1,))
def swiglu_act(gu, name):
    return _swiglu_fwd_call(gu, name)


def _swiglu_act_fwd(gu, name):
    return _swiglu_fwd_call(gu, name), gu


def _swiglu_act_bwd(name, gu, dact):
    return (_swiglu_bwd_call(gu, dact, name + "_bwd"),)


swiglu_act.defvjp(_swiglu_act_fwd, _swiglu_act_bwd)


def _final_call(h, g, target, name):
    T, Dm = h.shape
    tt = _tile(T, 512, 8)

    def body(x_ref, g_ref, t_ref, loss_ref, dx_ref, dg_ref):
        @pl.when(pl.program_id(0) == 0)
        def _():
            loss_ref[...] = jnp.zeros_like(loss_ref)
            dg_ref[...] = jnp.zeros_like(dg_ref)

        xv, gv = x_ref[...], g_ref[...]
        r = lax.rsqrt(jnp.mean(xv * xv, axis=-1, keepdims=True) + NORM_EPS)
        xh = xv * r
        err = xh * gv - t_ref[...]
        loss_ref[...] += 0.5 * jnp.sum(jnp.mean(err * err, axis=-1, keepdims=True), axis=0, keepdims=True)
        dy = err * (1.0 / Dm)
        dyg = dy * gv
        dx_ref[...] = r * (dyg - xh * jnp.mean(dyg * xh, axis=-1, keepdims=True))
        dg_ref[...] += jnp.sum(dy * xh, axis=0, keepdims=True)

    row = BS((tt, Dm), lambda i: (i, 0))
    vec = BS((1, Dm), lambda i: (0, 0))
    return _pcall(body, name=name, grid=(T // tt,), in_specs=[row, vec, row],
                  out_specs=[BS((1, 1), lambda i: (0, 0)), row, vec],
                  out_shape=[SDS((1, 1), F32), SDS((T, Dm), F32), SDS((1, Dm), F32)],
                  compiler_params=_cparams(("arbitrary",)))(h, g.reshape(1, Dm), target)


ADAMW_BLOCK_ELEMS = 64 * 1024


def _adamw_call(parts, w, m, v, name):
    shape = w.shape
    if not isinstance(parts, (list, tuple)):
        parts, shape3 = [parts], (1,) + shape
    else:
        shape3 = shape
    n_lead = shape3[0]
    r, N = shape3[-2], shape3[-1]
    Ld = math.prod(shape3[1:-2])
    w, m, v = (t.reshape(n_lead * Ld, r, N) for t in (w, m, v))
    tr = _tile(r, max(8, ADAMW_BLOCK_ELEMS // N), 8)
    c1 = 1.0 - ADAM_B1 ** ADAM_STEP
    c2 = 1.0 - ADAM_B2 ** ADAM_STEP
    outs = None
    for lead, p in enumerate(parts):
        def body(p_ref, w_ref, m_ref, v_ref, *rest):
            g_ref, d_ref, nm_ref, nv_ref = rest[-4:]
            g = p_ref[0]
            for j in range(1, N_DEV):
                g = g + p_ref[j]
            nm = ADAM_B1 * m_ref[...] + (1.0 - ADAM_B1) * g
            nv = ADAM_B2 * v_ref[...] + (1.0 - ADAM_B2) * (g * g)
            g_ref[...] = g
            nm_ref[...] = nm
            nv_ref[...] = nv
            d_ref[...] = -ADAM_LR * ((nm / c1) / (jnp.sqrt(nv / c2) + ADAM_EPS) + ADAM_WD * w_ref[...])

        base = lead * Ld
        row = BS((1, tr, N), lambda l, i, base=base: (base + l, i, 0))
        prev = [] if outs is None else list(outs)
        outs = _pcall(body, name=f"{name}_{lead}", grid=(Ld, r // tr),
                      in_specs=[BS((N_DEV, 1, tr, N), lambda l, i: (0, l, i, 0)), row, row, row]
                      + [BS(memory_space=pl.ANY)] * len(prev),
                      out_specs=[row] * 4, out_shape=[SDS((n_lead * Ld, r, N), F32)] * 4,
                      input_output_aliases={4 + j: j for j in range(len(prev))},
                      compiler_params=_cparams(("parallel", "parallel")))(p.reshape(N_DEV, Ld, r, N), w, m, v, *prev)
    return [t.reshape(shape) for t in outs]


def _place():
    return lax.axis_index("x"), lax.axis_index("y"), lax.axis_index("c")


def _all_gather(xs, name):
    n = len(xs)
    pairs = [(i, l) for i, x in enumerate(xs) for l in range(x.shape[0])]

    def body(*refs):
        x_refs, o_refs = refs[:n], refs[n:2 * n]
        send_sems, recv_sems, local_sems = refs[2 * n:]
        x_, y_, c_ = _place()
        me, sibling = (x_, y_, c_), (x_, y_, 1 - c_)
        chips = [(1 - x_, y_), (x_, 1 - y_), (1 - x_, 1 - y_)]

        def copy(e, k, block, to, from_input=False):
            i, l = pairs[e]
            px, py, pc = block
            dst = o_refs[i].at[l, 4 * px + 2 * py + pc]
            return pltpu.make_async_remote_copy(
                src_ref=x_refs[i].at[l] if from_input else dst, dst_ref=dst,
                send_sem=send_sems.at[7 * e + k], recv_sem=recv_sems.at[7 * e + k],
                device_id=to, device_id_type=pl.DeviceIdType.MESH)

        every = range(len(pairs))
        mine = [pltpu.make_async_copy(x_refs[i].at[l], o_refs[i].at[l, 4 * x_ + 2 * y_ + c_], local_sems.at[e])
                for e, (i, l) in enumerate(pairs)]
        for cp in mine:
            cp.start()
        first = [copy(e, 0, me, sibling, True) for e in every]
        first += [copy(e, 1 + j, me, (*chip, c_), True) for j, chip in enumerate(chips) for e in every]
        for cp in first:
            cp.start()
        passed = []
        for j, chip in enumerate(chips):
            for e in every:
                copy(e, 1 + j, (*chip, c_), me).wait_recv()
            for e in every:
                cp = copy(e, 4 + j, (*chip, c_), sibling)
                cp.start()
                passed.append(cp)
        for e in every:
            copy(e, 0, sibling, me).wait_recv()
        for j, chip in enumerate(chips):
            for e in every:
                copy(e, 4 + j, (*chip, 1 - c_), me).wait_recv()
        for cp in first + passed:
            cp.wait_send()
        for cp in mine:
            cp.wait()

    any_spec = BS(memory_space=pl.ANY)
    return _pcall(body, name=name, in_specs=[any_spec] * n, out_specs=[any_spec] * n,
                  out_shape=[SDS((x.shape[0], N_DEV) + x.shape[1:], x.dtype) for x in xs],
                  scratch_shapes=[pltpu.SemaphoreType.DMA((7 * len(pairs),)), pltpu.SemaphoreType.DMA((7 * len(pairs),)),
                                  pltpu.SemaphoreType.DMA((len(pairs),))],
                  compiler_params=pltpu.CompilerParams(has_side_effects=True))(*xs)


def _peer_of(k, place):
    x_, y_, c_ = place
    fx, fy, fc = (k >> 2) & 1, (k >> 1) & 1, k & 1
    return fx + x_ - 2 * fx * x_, fy + y_ - 2 * fy * y_, fc + c_ - 2 * fc * c_


def _split_copy(src_ref, land_ref, send_sems, recv_sems, e, k, place, scatter):
    x_, y_, c_ = place
    px, py, pc = _peer_of(k, place)
    return pltpu.make_async_remote_copy(
        src_ref=src_ref.at[4 * px + 2 * py + pc] if scatter else src_ref, dst_ref=land_ref.at[4 * x_ + 2 * y_ + c_],
        send_sem=send_sems.at[7 * e + k - 1], recv_sem=recv_sems.at[7 * e + k - 1],
        device_id=(px, py, pc), device_id_type=pl.DeviceIdType.MESH)


_HBM_SPEC = BS(memory_space=pltpu.HBM)
_SEM_SPEC = BS(memory_space=pltpu.SEMAPHORE)
_EFFECT = pltpu.SideEffectType.DATAFLOW_SIDE_EFFECTING


def _copies_start(srcs, scatter, name, thru=None):
    n = len(srcs)
    lands = [lax.empty(s.shape if scatter else (N_DEV,) + s.shape, s.dtype) for s in srcs]
    passed = srcs + lands + ([] if thru is None else [thru])

    def body(*refs):
        src_refs, land_refs = refs[:n], refs[n:2 * n]
        send_sems, recv_sems = refs[len(passed)], refs[len(passed) + 1]
        token = refs[-1]
        place = _place()
        for e in range(n):
            for k in range(1, N_DEV):
                _split_copy(src_refs[e], land_refs[e], send_sems, recv_sems, e, k, place, scatter).start()
        token[...] = jnp.zeros_like(token)

    hbm = lambda t: pltpu.with_memory_space_constraint(t, pltpu.HBM)
    out = _pcall(body, name=name,
                 out_shape=(pltpu.SemaphoreType.DMA((7 * n,)), pltpu.SemaphoreType.DMA((7 * n,)),
                            *[pltpu.HBM(t.shape, t.dtype) for t in passed], SDS((8, LANES), F32)),
                 in_specs=[_HBM_SPEC] * len(passed),
                 out_specs=(_SEM_SPEC, _SEM_SPEC, *[_HBM_SPEC] * len(passed), BS(memory_space=pltpu.VMEM)),
                 input_output_aliases={i: 2 + i for i in range(len(passed))},
                 compiler_params=pltpu.CompilerParams(has_side_effects=_EFFECT))(*[hbm(t) for t in passed])
    return out[0], out[1], list(out[2:2 + n]), list(out[2 + n:2 + 2 * n]), (None if thru is None else out[2 + 2 * n])


def _copies_wait(started, which, scatter, after, name):
    send_sems, recv_sems, srcs, lands, _ = started
    n = len(which)

    def body(*refs):
        src_refs, land_refs = refs[:n], refs[n:2 * n]
        send_s, recv_s = refs[2 * n], refs[2 * n + 1]
        place = _place()
        for j, e in enumerate(which):
            for k in range(1, N_DEV):
                cp = _split_copy(src_refs[j], land_refs[j], send_s, recv_s, e, k, place, scatter)
                cp.wait_send()
                cp.wait_recv()

    args = [srcs[e] for e in which] + [lands[e] for e in which]
    out = _pcall(body, name=name, out_shape=tuple(pltpu.HBM(t.shape, t.dtype) for t in args),
                 in_specs=[_HBM_SPEC] * (2 * n) + [_SEM_SPEC, _SEM_SPEC, BS(memory_space=pl.ANY)],
                 out_specs=tuple([_HBM_SPEC] * (2 * n)), input_output_aliases={i: i for i in range(2 * n)},
                 compiler_params=pltpu.CompilerParams(has_side_effects=_EFFECT))(*args, send_sems, recv_sems, after)
    return list(out[:n]), list(out[n:])


def _with_own_block(land, own_block):
    me = 4 * lax.axis_index("x") + 2 * lax.axis_index("y") + lax.axis_index("c")
    return lax.dynamic_update_index_in_dim(land, own_block, me, 0)


def _pad_flat(t, quantum=PACK_QUANTUM):
    f = t.reshape(-1)
    pad = (-f.shape[0]) % quantum
    return jnp.pad(f, (0, pad)) if pad else f


def _pack(arrs, dtype):
    return jnp.concatenate([_pad_flat(a.astype(dtype)) for a in arrs]).reshape(-1, LANES)


def _unpack(buf, shapes, lead=()):
    flat = buf.reshape(lead + (-1,))
    out, off = [], 0
    for s in shapes:
        n = math.prod(s)
        out.append(flat[..., off:off + n].reshape(lead + tuple(s)))
        off += n + (-n) % PACK_QUANTUM
    return out


def _full_from_gathered(g, axis):
    t = jnp.moveaxis(g, 0, axis)
    s = t.shape
    return t.reshape(s[:axis] + (s[axis] * s[axis + 1],) + s[axis + 2:])


def _parts_from_full(t, axis):
    s = t.shape
    t = t.reshape(s[:axis] + (N_DEV, s[axis] // N_DEV) + s[axis + 1:])
    return jnp.moveaxis(t, axis, 0)


def _heads(t, n_heads):
    T = t.shape[0]
    return t.reshape(T, n_heads, HEAD_DIM).transpose(1, 0, 2)


def _unheads(t):
    H, T, hd = t.shape
    return t.transpose(1, 0, 2).reshape(T, H * hd)


def _dilated_attention(q, k, v, name):
    H, T, hd = q.shape
    no_sink = jnp.full((1, 1, 1), NEG, F32)
    outs, lses = [], []
    for d in DILATIONS:
        Ld = T // d

        def strided(t):
            return t.reshape(H, Ld, d, hd).transpose(0, 2, 1, 3).reshape(H * d, Ld, hd)

        o, lse = band_attention(strided(q), strided(k), strided(v), jnp.broadcast_to(no_sink, (H * d, 1, 1)),
                                ATT_BLOCK, f"{name}_d{d}")
        outs.append(o.reshape(H, d, Ld, hd).transpose(0, 2, 1, 3).reshape(H, T, hd))
        lses.append(lse.reshape(H, d, Ld).transpose(0, 2, 1).reshape(H, T // ATT_BLOCK, 1, ATT_BLOCK))
    return merge3(tuple(outs), tuple(lses), name + "_merge")


STAGES = (
    ("mixer0", ('mix_norm', 'lru_conv_w', 'lru_conv_b', 'lru_ba', 'lru_bx', 'lru_lambda'),
     (('ab_w_in', 0), ('lru_wa', 0), ('lru_wx', 0), ('ab_w_out', 0))),
    ("xa0", ('xa_norm', 'xa_mem_norm'), (('xa_wq', 0), ('xa_wkv', 0), ('xa_wo', 0))),
    ("ffn0", ('ffn_norm',), (('ffn_w_gate_up', 0), ('ffn_w_down', 0))),
    ("mixer1", ('mix_norm', 'c_b_qkv', 'c_sinks', 'c_b_out'), (('c_w_qkv', 0), ('c_w_out', 0))),
    ("xa1", ('xa_norm', 'xa_mem_norm'), (('xa_wq', 1), ('xa_wkv', 1), ('xa_wo', 1))),
    ("ffn1", ('ffn_norm',), (('ffn_w_gate_up', 1), ('ffn_w_down', 1))),
)


def _stage_fn(stage, Wb, tabs, mem):
    layer = int(stage[-1])
    L = f"l{layer}"

    def run(S, Cw, h):
        def lin(a, key, bias, add, name, rows=None):
            wb, wc = Wb[key], Cw[key]
            if rows is not None:
                wb, wc = wb[rows], wc[rows]
            return linear(a, wb, wc, bias, add, name)

        if stage == "mixer0":
            hn = rmsnorm(h, S['mix_norm'][0], L + "_mix_norm")
            C = S['lru_conv_w'].shape[-1]
            proj = lin(hn, ('ab_w_in', 0), None, None, L + "_w_in")
            xc = conv4(proj[:, :C], S['lru_conv_w'][0], S['lru_conv_b'][0], L + "_conv")
            ga, gx = lru_gates(xc, Wb['lru_wa', 0], Cw['lru_wa', 0], S['lru_ba'][0],
                               Wb['lru_wx', 0], Cw['lru_wx', 0], S['lru_bx'][0], L + "_gates")
            rec = lru_scan(xc, ga, gx, proj[:, C:2 * C], S['lru_lambda'][0], L + "_scan")
            bw = B_HEADS * HEAD_DIM
            q = rope(proj[:, 2 * C:2 * C + bw], tabs, L + "_rope_q")
            k = rope(proj[:, 2 * C + bw:2 * C + 2 * bw], tabs, L + "_rope_k")
            v = proj[:, 2 * C + 2 * bw:]
            att = _unheads(_dilated_attention(_heads(q, B_HEADS), _heads(k, B_HEADS), _heads(v, B_HEADS), L + "_att"))
            h = lin(att, ('ab_w_out', 0), None, h, L + "_w_out_att", slice(C, None))
            return lin(rec, ('ab_w_out', 0), None, h, L + "_w_out_rec", slice(0, C))
        if stage == "mixer1":
            hn = rmsnorm(h, S['mix_norm'][1], L + "_mix_norm")
            qw = C_HEADS * HEAD_DIM
            kw = C_KV_HEADS * HEAD_DIM
            qkv = lin(hn, ('c_w_qkv', 0), S['c_b_qkv'][0], None, L + "_w_qkv")
            q = rope(qkv[:, :qw], tabs, L + "_rope_q")
            k = rope(qkv[:, qw:qw + kw], tabs, L + "_rope_k")
            v = qkv[:, qw + kw:]
            o, _ = band_attention(_heads(q, C_HEADS), _heads(k, C_KV_HEADS), _heads(v, C_KV_HEADS),
                                  S['c_sinks'][0].reshape(C_HEADS, 1, 1), ATT_BLOCK - 1, L + "_att")
            return lin(_unheads(o), ('c_w_out', 0), S['c_b_out'][0], h, L + "_w_out")
        if stage.startswith("xa"):
            xq = lin(rmsnorm(h, S['xa_norm'][layer], L + "_xa_norm"), ('xa_wq', layer), None, None, L + "_xa_wq")
            xkv = lin(rmsnorm(mem, S['xa_mem_norm'][layer], L + "_xa_mem_norm"), ('xa_wkv', layer), None, None,
                      L + "_xa_wkv")
            return lin(cross_attention(xq, xkv, L + "_xa"), ('xa_wo', layer), None, h, L + "_xa_wo")
        gu = lin(rmsnorm(h, S['ffn_norm'][layer], L + "_ffn_norm"), ('ffn_w_gate_up', layer), None, None, L + "_ffn_gu")
        return lin(swiglu_act(gu, L + "_swiglu"), ('ffn_w_down', layer), None, h, L + "_ffn_down")

    return run


def kernel(x, mem, mix_norm, ab_w_in, lru_conv_w, lru_conv_b, lru_wa, lru_ba, lru_wx, lru_bx, lru_lambda, ab_w_out, c_w_qkv, c_b_qkv, c_sinks, c_w_out, c_b_out, xa_norm, xa_mem_norm, xa_wq, xa_wkv, xa_wo, ffn_norm, ffn_w_gate_up, ffn_w_down, final_norm, loss_target, m_mix_norm, m_ab_w_in, m_lru_conv_w, m_lru_conv_b, m_lru_wa, m_lru_ba, m_lru_wx, m_lru_bx, m_lru_lambda, m_ab_w_out, m_c_w_qkv, m_c_b_qkv, m_c_sinks, m_c_w_out, m_c_b_out, m_xa_norm, m_xa_mem_norm, m_xa_wq, m_xa_wkv, m_xa_wo, m_ffn_norm, m_ffn_w_gate_up, m_ffn_w_down, m_final_norm, v_mix_norm, v_ab_w_in, v_lru_conv_w, v_lru_conv_b, v_lru_wa, v_lru_ba, v_lru_wx, v_lru_bx, v_lru_lambda, v_ab_w_out, v_c_w_qkv, v_c_b_qkv, v_c_sinks, v_c_w_out, v_c_b_out, v_xa_norm, v_xa_mem_norm, v_xa_wq, v_xa_wkv, v_xa_wo, v_ffn_norm, v_ffn_w_gate_up, v_ffn_w_down, v_final_norm):
    w_loc = dict(zip(WEIGHT_NAMES, (mix_norm, ab_w_in, lru_conv_w, lru_conv_b, lru_wa, lru_ba, lru_wx, lru_bx, lru_lambda, ab_w_out, c_w_qkv, c_b_qkv, c_sinks, c_w_out, c_b_out, xa_norm, xa_mem_norm, xa_wq, xa_wkv, xa_wo, ffn_norm, ffn_w_gate_up, ffn_w_down, final_norm)))
    m_loc = dict(zip(WEIGHT_NAMES, (m_mix_norm, m_ab_w_in, m_lru_conv_w, m_lru_conv_b, m_lru_wa, m_lru_ba, m_lru_wx, m_lru_bx, m_lru_lambda, m_ab_w_out, m_c_w_qkv, m_c_b_qkv, m_c_sinks, m_c_w_out, m_c_b_out, m_xa_norm, m_xa_mem_norm, m_xa_wq, m_xa_wkv, m_xa_wo, m_ffn_norm, m_ffn_w_gate_up, m_ffn_w_down, m_final_norm)))
    v_loc = dict(zip(WEIGHT_NAMES, (v_mix_norm, v_ab_w_in, v_lru_conv_w, v_lru_conv_b, v_lru_wa, v_lru_ba, v_lru_wx, v_lru_bx, v_lru_lambda, v_ab_w_out, v_c_w_qkv, v_c_b_qkv, v_c_sinks, v_c_w_out, v_c_b_out, v_xa_norm, v_xa_mem_norm, v_xa_wq, v_xa_wkv, v_xa_wo, v_ffn_norm, v_ffn_w_gate_up, v_ffn_w_down, v_final_norm)))

    me = 4 * lax.axis_index("x") + 2 * lax.axis_index("y") + lax.axis_index("c")

    keys = [key for _, _, stage_keys in STAGES for key in stage_keys]
    shards = [w_loc[n][l].astype(BF16) for n, l in keys]
    gather = _copies_start(shards, False, "gather_start")
    small_g = _all_gather([_pack([w_loc[n] for n in SMALL], F32)[None]], "gather_small")[0][0]
    S = {n: w_loc[n] for n in REPLICATED}
    for n, t in zip(SMALL, _unpack(small_g, [w_loc[n].shape for n in SMALL], lead=(N_DEV,))):
        S[n] = _full_from_gathered(t, SHARD_AXIS[n])

    tabs = _rope_tables(x.shape[1])
    Wb, vjps = {}, []
    h = x[0]
    for stage, small_names, stage_keys in STAGES:
        which = [keys.index(key) for key in stage_keys]
        _, lands = _copies_wait(gather, which, False, h, "gather_wait_" + stage)
        for key, e, land in zip(stage_keys, which, lands):
            Wb[key] = _full_from_gathered(_with_own_block(land, shards[e]), SHARD_AXIS[key[0]] - 1)
        carriers = {key: jnp.zeros(Wb[key].shape, F32) for key in stage_keys}
        h, vjp_fn = jax.vjp(_stage_fn(stage, Wb, tabs, mem[0]), {n: S[n] for n in small_names}, carriers, h)
        vjps.append(vjp_fn)
    loss_part, dh, dg_final = _final_call(h, S['final_norm'], loss_target[0], "final_loss")
    loss = lax.psum(loss_part[0, 0], ("x", "y", "c"))

    grads = {'final_norm': dg_final.reshape(final_norm.shape)}
    exchanges, send_keys, send_parts = [], [], []
    for (stage, small_names, stage_keys), vjp_fn in zip(reversed(STAGES), reversed(vjps)):
        g_small, g_big, dh = vjp_fn(dh)
        for n in small_names:
            grads[n] = grads[n] + g_small[n] if n in grads else g_small[n]
        send_keys += list(stage_keys)
        send_parts += [_parts_from_full(g_big[key], SHARD_AXIS[key[0]] - 1) for key in stage_keys]
        if stage == "xa1":
            continue
        if stage == "mixer0":
            small_parts = [_parts_from_full(grads[n], SHARD_AXIS[n]) for n in SMALL]
            send_keys.append("small")
            send_parts.append(jnp.stack([_pack([p[j] for p in small_parts], F32) for j in range(N_DEV)]))
        started = _copies_start(send_parts, True, "grad_start_" + stage, thru=dh)
        dh = started[4]
        exchanges.append((stage, started, send_keys))
        send_keys, send_parts = [], []
    dx = dh

    parts = {}
    for stage, started, ex_keys in exchanges:
        srcs, lands = _copies_wait(started, list(range(len(ex_keys))), True, dx, "grad_wait_" + stage)
        for key, src, land in zip(ex_keys, srcs, lands):
            parts[key] = _with_own_block(land, lax.dynamic_index_in_dim(src, me, 0, keepdims=False))
    out = {}

    def adamw(p, names, call_name):
        if len(names) == 1:
            res = _adamw_call(p, w_loc[names[0]], m_loc[names[0]], v_loc[names[0]], call_name)
            for kind, t in zip(("grad", "delta", "new_m", "new_v"), res):
                out[kind, names[0]] = t
        else:
            res = _adamw_call(p, *[_pack([d[n] for n in names], F32) for d in (w_loc, m_loc, v_loc)], call_name)
            for kind, buf in zip(("grad", "delta", "new_m", "new_v"), res):
                for n, t in zip(names, _unpack(buf, [w_loc[n].shape for n in names])):
                    out[kind, n] = t

    for n in BIG:
        adamw([parts[n, l] for l in range(w_loc[n].shape[0])], [n], "adamw_" + n)
    adamw(parts["small"], SMALL, "adamw_small")
    rep_g = _all_gather([_pack([grads[n] for n in REPLICATED], F32)[None]], "gather_rep_grads")[0][0]
    adamw(rep_g, REPLICATED, "adamw_replicated")

    return (loss, dx[None], *[out[kind, n] for kind in ("grad", "delta", "new_m", "new_v") for n in WEIGHT_NAMES])
```

```python
import functools
import math

import jax
import jax.numpy as jnp
from jax import lax
from jax.experimental import pallas as pl
from jax.experimental.pallas import tpu as pltpu

F32 = jnp.float32
BF16 = jnp.bfloat16
SDS = jax.ShapeDtypeStruct
BS = pl.BlockSpec

N_DEV = 8
NORM_EPS = 1e-6
ROPE_THETA = 500000.0
HEAD_DIM = 64
ROT_DIM = 16
ATT_BLOCK = 128
LRU_C = 8.0
LRU_HEADS = 4
DILATIONS = (1, 4, 16)
B_HEADS = 8
C_HEADS = 16
C_KV_HEADS = 2
XA_HEADS = 4
XA_HEAD_DIM = 128
NEG = -1e30
ADAM_LR, ADAM_B1, ADAM_B2, ADAM_EPS, ADAM_WD, ADAM_STEP = 0.001, 0.9, 0.999, 1e-08, 0.01, 10
LANES = 128
VMEM_LIMIT = 48 * 1024 * 1024

WEIGHT_NAMES = ['mix_norm', 'ab_w_in', 'lru_conv_w', 'lru_conv_b', 'lru_wa', 'lru_ba', 'lru_wx', 'lru_bx',
                'lru_lambda', 'ab_w_out', 'c_w_qkv', 'c_b_qkv', 'c_sinks', 'c_w_out', 'c_b_out', 'xa_norm',
                'xa_mem_norm', 'xa_wq', 'xa_wkv', 'xa_wo', 'ffn_norm', 'ffn_w_gate_up', 'ffn_w_down', 'final_norm']
SHARD_AXIS = {'ab_w_in': 2, 'lru_conv_w': 2, 'lru_wa': 2, 'lru_ba': 2, 'lru_wx': 2, 'lru_bx': 2, 'ab_w_out': 1,
              'c_w_qkv': 2, 'c_b_qkv': 1, 'c_w_out': 1, 'c_b_out': 1, 'xa_wq': 1, 'xa_wkv': 1, 'xa_wo': 2,
              'ffn_w_gate_up': 2, 'ffn_w_down': 1}
BIG = ['ab_w_in', 'lru_wa', 'lru_wx', 'ab_w_out', 'c_w_qkv', 'c_w_out', 'xa_wq', 'xa_wkv', 'xa_wo',
       'ffn_w_gate_up', 'ffn_w_down']
SMALL = ['lru_conv_w', 'lru_ba', 'lru_bx', 'c_b_qkv', 'c_b_out']
SHARDED = BIG + SMALL
REPLICATED = [n for n in WEIGHT_NAMES if n not in SHARD_AXIS]
PACK_QUANTUM = 2048


def _pcall(body, **kw):
    return pl.pallas_call(body, **kw)


def _cparams(sem=None):
    return pltpu.CompilerParams(dimension_semantics=sem, vmem_limit_bytes=VMEM_LIMIT)


def _tile(n, target, mult=LANES):
    if n <= target:
        return n
    t = (target // mult) * mult
    while t >= mult:
        if n % t == 0:
            return t
        t -= mult
    return n


def _sigmoid(x):
    return 1.0 / (1.0 + jnp.exp(-x))


def _expm1(x):
    small = x * (1.0 + x * (0.5 + x * (1.0 / 6.0 + x * (1.0 / 24.0))))
    return jnp.where(jnp.abs(x) < 0.03, small, jnp.exp(x) - 1.0)


_GELU_C = math.sqrt(2.0 / math.pi)


def _gelu_parts(y):
    y2 = y * y
    th = jnp.tanh(_GELU_C * (y + 0.044715 * y * y2))
    g = 0.5 * y * (1.0 + th)
    dg = 0.5 * (1.0 + th) + 0.5 * y * (1.0 - th * th) * _GELU_C * (1.0 + 3.0 * 0.044715 * y2)
    return g, dg


def _eye(n):
    return lax.broadcasted_iota(jnp.int32, (n, n), 0) == lax.broadcasted_iota(jnp.int32, (n, n), 1)


def _col_to_row(c):
    n = c.shape[1]
    return jnp.sum(jnp.where(_eye(n)[None], c, 0.0), axis=1, keepdims=True)


def _row_to_col(r):
    n = r.shape[2]
    return jnp.sum(jnp.where(_eye(n)[None], r, 0.0), axis=2, keepdims=True)


def _mm(a, b, *, ta=False, tb=False, bias=None, add=None, name, tm=512, tn=512, tk=2048):
    M, K = (a.shape[1], a.shape[0]) if ta else a.shape
    N = b.shape[0] if tb else b.shape[1]
    tm, tn, tk = _tile(M, tm), _tile(N, tn), _tile(K, tk)
    nk = K // tk
    dn = (((0 if ta else 1,), (1 if tb else 0,)), ((), ()))

    def body(*refs):
        a_ref, b_ref = refs[0], refs[1]
        pos = 2
        bias_ref = add_ref = None
        if bias is not None:
            bias_ref = refs[pos]
            pos += 1
        if add is not None:
            add_ref = refs[pos]
            pos += 1
        o_ref = refs[pos]
        part = lax.dot_general(a_ref[...].astype(BF16), b_ref[...].astype(BF16), dn, preferred_element_type=F32)

        def finish(r):
            if bias_ref is not None:
                r = r + bias_ref[...]
            if add_ref is not None:
                r = r + add_ref[...]
            o_ref[...] = r

        if nk == 1:
            finish(part)
            return
        acc_ref = refs[pos + 1]
        k = pl.program_id(2)

        @pl.when(k == 0)
        def _():
            acc_ref[...] = part

        @pl.when((k > 0) & (k < nk - 1))
        def _():
            acc_ref[...] += part

        @pl.when(k == nk - 1)
        def _():
            finish(acc_ref[...] + part)

    in_specs = [BS((tk, tm), lambda i, j, k: (k, i)) if ta else BS((tm, tk), lambda i, j, k: (i, k)),
                BS((tn, tk), lambda i, j, k: (j, k)) if tb else BS((tk, tn), lambda i, j, k: (k, j))]
    args = [a, b]
    if bias is not None:
        in_specs.append(BS((1, tn), lambda i, j, k: (0, j)))
        args.append(bias.reshape(1, N))
    if add is not None:
        in_specs.append(BS((tm, tn), lambda i, j, k: (i, j)))
        args.append(add)
    return _pcall(body, name=name, grid=(M // tm, N // tn, nk), in_specs=in_specs,
                  out_specs=BS((tm, tn), lambda i, j, k: (i, j)), out_shape=SDS((M, N), F32),
                  scratch_shapes=[pltpu.VMEM((tm, tn), F32)] if nk > 1 else [],
                  compiler_params=_cparams(("parallel", "parallel", "arbitrary")))(*args)


def _colsum(x, name):
    T, N = x.shape
    tt = _tile(T, 512, 8)

    def body(x_ref, o_ref):
        @pl.when(pl.program_id(0) == 0)
        def _():
            o_ref[...] = jnp.zeros_like(o_ref)

        o_ref[...] += jnp.sum(x_ref[...], axis=0, keepdims=True)

    return _pcall(body, name=name, grid=(T // tt,), in_specs=[BS((tt, N), lambda i: (i, 0))],
                  out_specs=BS((1, N), lambda i: (0, 0)), out_shape=SDS((1, N), F32),
                  compiler_params=_cparams(("arbitrary",)))(x)


@functools.partial(jax.custom_vjp, nondiff_argnums=(5,))
def linear(a, wb, wc, bias, add, name):
    return _mm(a, wb, bias=bias, add=add, name=name)


def _linear_fwd(a, wb, wc, bias, add, name):
    return _mm(a, wb, bias=bias, add=add, name=name), (a, wb, bias is not None, add is not None)


def _linear_bwd(name, res, g):
    a, wb, has_bias, has_add = res
    da = _mm(g, wb, tb=True, name=name + "_da")
    dw = _mm(a, g, ta=True, name=name + "_dw")
    dbias = _colsum(g, name + "_db").reshape(-1) if has_bias else None
    return da, jnp.zeros_like(wb), dw, dbias, (g if has_add else None)


linear.defvjp(_linear_fwd, _linear_bwd)


def _rms_fwd_call(x, g, name, out_dtype=F32):
    T, Dm = x.shape
    tt = _tile(T, 512, 16)

    def body(x_ref, g_ref, o_ref):
        xv = x_ref[...]
        r = lax.rsqrt(jnp.mean(xv * xv, axis=-1, keepdims=True) + NORM_EPS)
        o_ref[...] = ((xv * r) * g_ref[...]).astype(out_dtype)

    return _pcall(body, name=name, grid=(T // tt,),
                  in_specs=[BS((tt, Dm), lambda i: (i, 0)), BS((1, Dm), lambda i: (0, 0))],
                  out_specs=BS((tt, Dm), lambda i: (i, 0)), out_shape=SDS((T, Dm), out_dtype),
                  compiler_params=_cparams(("parallel",)))(x, g.reshape(1, Dm))


def _rms_bwd_call(x, g, dy, name, add=None):
    T, Dm = x.shape
    tt = _tile(T, 512, 8)

    def body(*refs):
        x_ref, g_ref, dy_ref = refs[:3]
        dx_ref, dg_ref = refs[-2:]
        xv = x_ref[...]
        r = lax.rsqrt(jnp.mean(xv * xv, axis=-1, keepdims=True) + NORM_EPS)
        xh = xv * r
        dy = dy_ref[...]
        dyg = dy * g_ref[...]
        dx = r * (dyg - xh * jnp.mean(dyg * xh, axis=-1, keepdims=True))
        dx_ref[...] = dx if add is None else dx + refs[3][...]

        @pl.when(pl.program_id(0) == 0)
        def _():
            dg_ref[...] = jnp.zeros_like(dg_ref)

        dg_ref[...] += jnp.sum(dy * xh, axis=0, keepdims=True)

    row = BS((tt, Dm), lambda i: (i, 0))
    vec = BS((1, Dm), lambda i: (0, 0))
    extra = [] if add is None else [add]
    return _pcall(body, name=name, grid=(T // tt,), in_specs=[row, vec, row] + [row] * len(extra), out_specs=[row, vec],
                  out_shape=[SDS((T, Dm), F32), SDS((1, Dm), F32)],
                  compiler_params=_cparams(("arbitrary",)))(x, g.reshape(1, Dm), dy, *extra)


@functools.partial(jax.custom_vjp, nondiff_argnums=(2,))
def rmsnorm(x, g, name):
    return _rms_fwd_call(x, g, name)


def _rmsnorm_fwd(x, g, name):
    return _rms_fwd_call(x, g, name), (x, g)


def _rmsnorm_bwd(name, res, dy):
    x, g = res
    dx, dg = _rms_bwd_call(x, g, dy, name + "_bwd")
    return dx, dg.reshape(g.shape)


rmsnorm.defvjp(_rmsnorm_fwd, _rmsnorm_bwd)


@functools.partial(jax.custom_vjp, nondiff_argnums=(5,))
def norm_linear(x, g, wb, wc, bias, name):
    return _mm(_rms_fwd_call(x, g, name + "_norm", BF16), wb, bias=bias, name=name, tm=1024, tn=1408, tk=1024)


def _norm_linear_fwd(x, g, wb, wc, bias, name):
    hn = _rms_fwd_call(x, g, name + "_norm", BF16)
    return _mm(hn, wb, bias=bias, name=name, tm=1024, tn=1408, tk=1024), (x, g, hn, wb, bias is not None)


def _norm_linear_bwd(name, res, dy):
    x, g, hn, wb, has_bias = res
    dw = _mm(hn, dy, ta=True, name=name + "_dw", tm=1024, tn=1408, tk=1024)
    dhn = _mm(dy, wb, tb=True, name=name + "_da", tm=1024, tn=1024, tk=1408)
    dx, dg = _rms_bwd_call(x, g, dhn, name + "_norm_bwd")
    dbias = _colsum(dy, name + "_db").reshape(-1) if has_bias else None
    return dx, dg.reshape(g.shape), jnp.zeros_like(wb), dw, dbias


norm_linear.defvjp(_norm_linear_fwd, _norm_linear_bwd)


def _rope_tables(T):
    half = ROT_DIM // 2
    inv = ROPE_THETA ** (-jnp.arange(0, ROT_DIM, 2, dtype=F32) / ROT_DIM)
    ang = jnp.arange(T, dtype=F32)[:, None] * inv[None, :]
    cos, sin = jnp.cos(ang), jnp.sin(ang)
    ones = jnp.ones((T, HEAD_DIM - ROT_DIM), F32)
    zeros = jnp.zeros((T, HEAD_DIM - ROT_DIM), F32)
    zh = jnp.zeros((T, half), F32)
    c = jnp.concatenate([cos, cos, ones], axis=1)
    sa = jnp.concatenate([zh, sin, zeros], axis=1)
    sb = jnp.concatenate([-sin, zh, zeros], axis=1)
    two = lambda t: jnp.concatenate([t, t], axis=1)
    return two(c), two(sa), two(sb)


def _rope_call(x, tabs, inverse, name):
    T, W = x.shape
    tt = _tile(T, 512, 8)
    reps = W // LANES
    half = ROT_DIM // 2

    def body(x_ref, c_ref, sa_ref, sb_ref, o_ref):
        xv = x_ref[...]
        c = jnp.tile(c_ref[...], (1, reps))
        sa = jnp.tile(sa_ref[...], (1, reps))
        sb = jnp.tile(sb_ref[...], (1, reps))
        if not inverse:
            o_ref[...] = xv * c + pltpu.roll(xv, half, axis=1) * sa + pltpu.roll(xv, W - half, axis=1) * sb
        else:
            o_ref[...] = xv * c + pltpu.roll(xv * sa, W - half, axis=1) + pltpu.roll(xv * sb, half, axis=1)

    row = BS((tt, W), lambda i: (i, 0))
    tab = BS((tt, LANES), lambda i: (i, 0))
    return _pcall(body, name=name, grid=(T // tt,), in_specs=[row, tab, tab, tab], out_specs=row,
                  out_shape=SDS((T, W), F32), compiler_params=_cparams(("parallel",)))(x, *tabs)


@functools.partial(jax.custom_vjp, nondiff_argnums=(2,))
def rope(x, tabs, name):
    return _rope_call(x, tabs, False, name)


def _rope_fwd(x, tabs, name):
    return _rope_call(x, tabs, False, name), tabs


def _rope_bwd(name, tabs, dy):
    return _rope_call(dy, tabs, True, name + "_bwd"), jax.tree.map(jnp.zeros_like, tabs)


rope.defvjp(_rope_fwd, _rope_bwd)


def _conv_fwd_call(x, w, b, name):
    T, C = x.shape
    tt = _tile(T, 512, 8)
    per = tt // 8

    def body(x_ref, halo_ref, w_ref, b_ref, o_ref):
        i = pl.program_id(0)
        halo = jnp.where(i > 0, halo_ref[...], 0.0)
        e = jnp.concatenate([halo, x_ref[...]], axis=0)
        acc = b_ref[...]
        for k in (3, 2, 1):
            acc = acc + pltpu.roll(e, k, axis=0)[8:, :] * w_ref[3 - k:4 - k, :]
        o_ref[...] = acc + x_ref[...] * w_ref[3:4, :]

    return _pcall(body, name=name, grid=(T // tt,),
                  in_specs=[BS((tt, C), lambda i: (i, 0)), BS((8, C), lambda i: (jnp.maximum(i * per - 1, 0), 0)),
                            BS((4, C), lambda i: (0, 0)), BS((1, C), lambda i: (0, 0))],
                  out_specs=BS((tt, C), lambda i: (i, 0)), out_shape=SDS((T, C), F32),
                  compiler_params=_cparams(("parallel",)))(x, x, w, b.reshape(1, C))


def _conv_bwd_call(x, w, dy, name):
    T, C = x.shape
    tt = _tile(T, 512, 8)
    per = tt // 8
    nt = T // tt

    def body(x_ref, halo_ref, w_ref, dy_ref, nxt_ref, dx_ref, dwb_ref):
        i = pl.program_id(0)
        halo = jnp.where(i > 0, halo_ref[...], 0.0)
        e = jnp.concatenate([halo, x_ref[...]], axis=0)
        dy = dy_ref[...]
        nxt = jnp.where(i < nt - 1, nxt_ref[...], 0.0)
        f = jnp.concatenate([dy, nxt], axis=0)
        dx = dy * w_ref[3:4, :]
        rows = [None] * 4
        rows[3] = jnp.sum(dy * x_ref[...], axis=0, keepdims=True)
        for k in (1, 2, 3):
            dx = dx + pltpu.roll(f, tt + 8 - k, axis=0)[:tt, :] * w_ref[3 - k:4 - k, :]
            rows[3 - k] = jnp.sum(dy * pltpu.roll(e, k, axis=0)[8:, :], axis=0, keepdims=True)
        dx_ref[...] = dx
        upd = jnp.concatenate(rows + [jnp.sum(dy, axis=0, keepdims=True), jnp.zeros((3, C), F32)], axis=0)

        @pl.when(i == 0)
        def _():
            dwb_ref[...] = jnp.zeros_like(dwb_ref)

        dwb_ref[...] += upd

    row = BS((tt, C), lambda i: (i, 0))
    return _pcall(body, name=name, grid=(nt,),
                  in_specs=[row, BS((8, C), lambda i: (jnp.maximum(i * per - 1, 0), 0)), BS((4, C), lambda i: (0, 0)),
                            row, BS((8, C), lambda i: (jnp.minimum((i + 1) * per, T // 8 - 1), 0))],
                  out_specs=[row, BS((8, C), lambda i: (0, 0))],
                  out_shape=[SDS((T, C), F32), SDS((8, C), F32)],
                  compiler_params=_cparams(("arbitrary",)))(x, x, w, dy, dy)


@functools.partial(jax.custom_vjp, nondiff_argnums=(3,))
def conv4(x, w, b, name):
    return _conv_fwd_call(x, w, b, name)


def _conv4_fwd(x, w, b, name):
    return _conv_fwd_call(x, w, b, name), (x, w)


def _conv4_bwd(name, res, dy):
    x, w = res
    dx, dwb = _conv_bwd_call(x, w, dy, name + "_bwd")
    return dx, dwb[0:4], dwb[4]


conv4.defvjp(_conv4_fwd, _conv4_bwd)


def _gates_fwd_call(xc, wa, ba, wx, bx, name):
    T, C = xc.shape
    hd = C // LRU_HEADS
    tt = _tile(T, 512, 8)

    def body(x_ref, wa_ref, ba_ref, wx_ref, bx_ref, ga_ref, gx_ref):
        xb = x_ref[...].astype(BF16)
        ga_ref[...] = jnp.dot(xb, wa_ref[0].astype(BF16), preferred_element_type=F32) + ba_ref[...]
        gx_ref[...] = jnp.dot(xb, wx_ref[0].astype(BF16), preferred_element_type=F32) + bx_ref[...]

    blk = BS((tt, hd), lambda i, h: (i, h))
    wsp = BS((1, hd, hd), lambda i, h: (h, 0, 0))
    bsp = BS((1, hd), lambda i, h: (0, h))
    return _pcall(body, name=name, grid=(T // tt, LRU_HEADS), in_specs=[blk, wsp, bsp, wsp, bsp],
                  out_specs=[blk, blk], out_shape=[SDS((T, C), F32)] * 2,
                  compiler_params=_cparams(("parallel", "parallel")))(xc, wa, ba.reshape(1, C), wx, bx.reshape(1, C))


def _gates_bwd_x_call(dga, dgx, wa, wx, name):
    T, C = dga.shape
    hd = C // LRU_HEADS
    tt = _tile(T, 512, 8)
    dn = (((1,), (1,)), ((), ()))

    def body(da_ref, dx_ref, wa_ref, wx_ref, o_ref):
        o_ref[...] = (lax.dot_general(da_ref[...].astype(BF16), wa_ref[0].astype(BF16), dn, preferred_element_type=F32)
                      + lax.dot_general(dx_ref[...].astype(BF16), wx_ref[0].astype(BF16), dn, preferred_element_type=F32))

    blk = BS((tt, hd), lambda i, h: (i, h))
    wsp = BS((1, hd, hd), lambda i, h: (h, 0, 0))
    return _pcall(body, name=name, grid=(T // tt, LRU_HEADS), in_specs=[blk, blk, wsp, wsp], out_specs=blk,
                  out_shape=SDS((T, C), F32), compiler_params=_cparams(("parallel", "parallel")))(dga, dgx, wa, wx)


def _gates_bwd_w_call(xc, dga, dgx, name):
    T, C = xc.shape
    hd = C // LRU_HEADS
    tt = _tile(T, 512, 8)
    dn = (((0,), (0,)), ((), ()))

    def body(x_ref, da_ref, dx_ref, dwa_ref, dwx_ref, dba_ref, dbx_ref):
        @pl.when(pl.program_id(1) == 0)
        def _():
            dwa_ref[...] = jnp.zeros_like(dwa_ref)
            dwx_ref[...] = jnp.zeros_like(dwx_ref)
            dba_ref[...] = jnp.zeros_like(dba_ref)
            dbx_ref[...] = jnp.zeros_like(dbx_ref)

        xb = x_ref[...].astype(BF16)
        da, dx = da_ref[...], dx_ref[...]
        dwa_ref[0] += lax.dot_general(xb, da.astype(BF16), dn, preferred_element_type=F32)
        dwx_ref[0] += lax.dot_general(xb, dx.astype(BF16), dn, preferred_element_type=F32)
        dba_ref[...] += jnp.sum(da, axis=0, keepdims=True)
        dbx_ref[...] += jnp.sum(dx, axis=0, keepdims=True)

    blk = BS((tt, hd), lambda h, i: (i, h))
    wsp = BS((1, hd, hd), lambda h, i: (h, 0, 0))
    bsp = BS((1, hd), lambda h, i: (0, h))
    return _pcall(body, name=name, grid=(LRU_HEADS, T // tt), in_specs=[blk, blk, blk],
                  out_specs=[wsp, wsp, bsp, bsp],
                  out_shape=[SDS((LRU_HEADS, hd, hd), F32)] * 2 + [SDS((1, C), F32)] * 2,
                  compiler_params=_cparams(("parallel", "arbitrary")))(xc, dga, dgx)


@functools.partial(jax.custom_vjp, nondiff_argnums=(7,))
def lru_gates(xc, wa, wa_c, ba, wx, wx_c, bx, name):
    return tuple(_gates_fwd_call(xc, wa, ba, wx, bx, name))


def _lru_gates_fwd(xc, wa, wa_c, ba, wx, wx_c, bx, name):
    return tuple(_gates_fwd_call(xc, wa, ba, wx, bx, name)), (xc, wa, wx, ba.shape)


def _lru_gates_bwd(name, res, g):
    xc, wa, wx, bshape = res
    dga, dgx = g
    dxc = _gates_bwd_x_call(dga, dgx, wa, wx, name + "_dx")
    dwa, dwx, dba, dbx = _gates_bwd_w_call(xc, dga, dgx, name + "_dw")
    return dxc, jnp.zeros_like(wa), dwa, dba.reshape(bshape), jnp.zeros_like(wx), dwx, dbx.reshape(bshape)


lru_gates.defvjp(_lru_gates_fwd, _lru_gates_bwd)


def _lru_coeffs(xc, ga, gx, lam):
    r = _sigmoid(ga)
    ig = _sigmoid(gx)
    z = -lam
    sp = jnp.maximum(z, 0.0) + jnp.log(1.0 + jnp.exp(-jnp.abs(z)))
    la = -LRU_C * r * sp
    a = jnp.exp(la)
    s = jnp.sqrt(-_expm1(2.0 * la))
    return r, ig, sp, a, s


LRU_TT = 256


def _scan_fwd_call(xc, ga, gx, y, lam, name):
    T, C = xc.shape
    tt = _tile(T, LRU_TT, 8)

    def body(xc_ref, ga_ref, gx_ref, y_ref, lam_ref, h_ref, rec_ref, a_buf, carry):
        @pl.when(pl.program_id(0) == 0)
        def _():
            carry[...] = jnp.zeros_like(carry)

        xcv = xc_ref[...]
        _, ig, _, a, s = _lru_coeffs(xcv, ga_ref[...], gx_ref[...], lam_ref[...])
        a_buf[...] = a
        h_ref[...] = s * (ig * xcv)

        def step(t, h):
            hn = a_buf[pl.ds(t, 1), :] * h + h_ref[pl.ds(t, 1), :]
            h_ref[pl.ds(t, 1), :] = hn
            return hn

        carry[0:1, :] = lax.fori_loop(0, tt, step, carry[0:1, :], unroll=8)
        g, _ = _gelu_parts(y_ref[...])
        rec_ref[...] = h_ref[...] * g

    row = BS((tt, C), lambda i: (i, 0))
    vec = BS((1, C), lambda i: (0, 0))
    return _pcall(body, name=name, grid=(T // tt,), in_specs=[row, row, row, row, vec], out_specs=[row, row],
                  out_shape=[SDS((T, C), F32)] * 2,
                  scratch_shapes=[pltpu.VMEM((tt, C), F32), pltpu.VMEM((8, C), F32)],
                  compiler_params=_cparams(("arbitrary",)))(xc, ga, gx, y, lam.reshape(1, C))


def _scan_bwd_call(xc, ga, gx, y, lam, h, drec, name):
    T, C = xc.shape
    tt = _tile(T, LRU_TT, 8)
    nt = T // tt
    per = tt // 8

    def body(xc_ref, ga_ref, gx_ref, y_ref, lam_ref, h_ref, halo_ref, dr_ref,
             dga_ref, dgx_ref, dxc_ref, dy_ref, dlam_ref, a_buf, g_buf, carry):
        i = pl.program_id(0)

        @pl.when(i == 0)
        def _():
            carry[...] = jnp.zeros_like(carry)
            dlam_ref[...] = jnp.zeros_like(dlam_ref)

        xcv, lam = xc_ref[...], lam_ref[...]
        r, ig, sp, a, s = _lru_coeffs(xcv, ga_ref[...], gx_ref[...], lam)
        gel, dgel = _gelu_parts(y_ref[...])
        drec = dr_ref[...]
        hv = h_ref[...]
        dy_ref[...] = drec * hv * dgel
        a_buf[...] = a
        g_buf[...] = drec * gel

        def step(j, q):
            t = tt - 1 - j
            g = g_buf[pl.ds(t, 1), :] + q
            g_buf[pl.ds(t, 1), :] = g
            return a_buf[pl.ds(t, 1), :] * g

        carry[0:1, :] = lax.fori_loop(0, tt, step, carry[0:1, :], unroll=8)
        g = g_buf[...]
        halo = jnp.where(i < nt - 1, halo_ref[...], 0.0)
        hprev = pltpu.roll(jnp.concatenate([halo, hv], axis=0), 1, axis=0)[8:, :]
        da = g * hprev
        dig = g * s * xcv
        ds = g * ig * xcv
        dla = da * a - ds * (a * a) / s
        dga_ref[...] = dla * (-LRU_C * sp) * r * (1.0 - r)
        dgx_ref[...] = dig * ig * (1.0 - ig)
        dxc_ref[...] = g * s * ig
        dlam_ref[...] += jnp.sum(dla * r, axis=0, keepdims=True) * (LRU_C * _sigmoid(-lam))

    row = BS((tt, C), lambda i: (nt - 1 - i, 0))
    vec = BS((1, C), lambda i: (0, 0))
    halo = BS((8, C), lambda i: (jnp.maximum((nt - 1 - i) * per - 1, 0), 0))
    return _pcall(body, name=name, grid=(nt,), in_specs=[row, row, row, row, vec, row, halo, row],
                  out_specs=[row, row, row, row, vec], out_shape=[SDS((T, C), F32)] * 4 + [SDS((1, C), F32)],
                  scratch_shapes=[pltpu.VMEM((tt, C), F32), pltpu.VMEM((tt, C), F32), pltpu.VMEM((8, C), F32)],
                  compiler_params=_cparams(("arbitrary",)))(xc, ga, gx, y, lam.reshape(1, C), h, h, drec)


@functools.partial(jax.custom_vjp, nondiff_argnums=(5,))
def lru_scan(xc, ga, gx, y, lam, name):
    return _scan_fwd_call(xc, ga, gx, y, lam, name)[1]


def _lru_scan_fwd(xc, ga, gx, y, lam, name):
    h, rec = _scan_fwd_call(xc, ga, gx, y, lam, name)
    return rec, (xc, ga, gx, y, lam, h)


def _lru_scan_bwd(name, res, drec):
    xc, ga, gx, y, lam, h = res
    dga, dgx, dxc, dy, dlam = _scan_bwd_call(xc, ga, gx, y, lam, h, drec, name + "_bwd")
    return dxc, dga, dgx, dy, dlam.reshape(lam.shape)


lru_scan.defvjp(_lru_scan_fwd, _lru_scan_bwd)


ATT_HB = 8


def _band_scores(qb, kc, kp, n, max_dist, scale):
    hb = qb.shape[0]
    dn = (((2,), (2,)), ((0,), (0,)))
    s_c = lax.dot_general(qb, kc, dn, preferred_element_type=F32) * scale
    s_p = lax.dot_general(qb, kp, dn, preferred_element_type=F32) * scale
    qi = lax.broadcasted_iota(jnp.int32, (1, ATT_BLOCK, ATT_BLOCK), 1)
    kj = lax.broadcasted_iota(jnp.int32, (1, ATT_BLOCK, ATT_BLOCK), 2)
    s_c = jnp.where(kj <= qi, s_c, NEG)
    s_p = jnp.where((kj >= qi + (ATT_BLOCK - max_dist)) & (n > 0), s_p, NEG)
    return s_c, s_p


def _kv_block(ref, hb):
    v = ref[...].astype(BF16)
    if v.shape[0] != hb:
        v = jnp.broadcast_to(v, (hb,) + v.shape[1:])
    return v


def _band_fwd_call(q, k, v, sinks, max_dist, name):
    H, L, hd = q.shape
    G = H // k.shape[0]
    hb = ATT_HB
    kb = hb // G
    nb = L // ATT_BLOCK
    scale = hd ** -0.5
    dn_pv = (((2,), (1,)), ((0,), (0,)))

    def body(q_ref, kc_ref, kp_ref, vc_ref, vp_ref, sk_ref, o_ref, lse_ref):
        n = pl.program_id(1)
        qb = q_ref[...].astype(BF16)
        s_c, s_p = _band_scores(qb, _kv_block(kc_ref, hb), _kv_block(kp_ref, hb), n, max_dist, scale)
        sk = sk_ref[...]
        m = jnp.maximum(jnp.maximum(jnp.max(s_c, axis=-1, keepdims=True), jnp.max(s_p, axis=-1, keepdims=True)), sk)
        e_c, e_p = jnp.exp(s_c - m), jnp.exp(s_p - m)
        den = jnp.sum(e_c, axis=-1, keepdims=True) + jnp.sum(e_p, axis=-1, keepdims=True) + jnp.exp(sk - m)
        inv = 1.0 / den
        o_ref[...] = (lax.dot_general((e_c * inv).astype(BF16), _kv_block(vc_ref, hb), dn_pv, preferred_element_type=F32)
                      + lax.dot_general((e_p * inv).astype(BF16), _kv_block(vp_ref, hb), dn_pv, preferred_element_type=F32))
        lse_ref[:, 0, :, :] = _col_to_row(m + jnp.log(den))

    qsp = BS((hb, ATT_BLOCK, hd), lambda h, n: (h, n, 0))
    csp = BS((kb, ATT_BLOCK, hd), lambda h, n: (h, n, 0))
    psp = BS((kb, ATT_BLOCK, hd), lambda h, n: (h, jnp.maximum(n - 1, 0), 0))
    return _pcall(body, name=name, grid=(H // hb, nb),
                  in_specs=[qsp, csp, psp, csp, psp, BS((hb, 1, 1), lambda h, n: (h, 0, 0))],
                  out_specs=[qsp, BS((hb, 1, 1, ATT_BLOCK), lambda h, n: (h, n, 0, 0))],
                  out_shape=[SDS((H, L, hd), F32), SDS((H, nb, 1, ATT_BLOCK), F32)],
                  compiler_params=_cparams(("parallel", "parallel")))(q, k, k, v, v, sinks)


def _band_bwd_call(q, k, v, sinks, o, lse, do, dlse, max_dist, name):
    H, L, hd = q.shape
    G = H // k.shape[0]
    hb = ATT_HB
    kb = hb // G
    nb = L // ATT_BLOCK
    scale = hd ** -0.5
    dn_nt = (((2,), (2,)), ((0,), (0,)))
    dn_nn = (((2,), (1,)), ((0,), (0,)))
    dn_tn = (((1,), (1,)), ((0,), (0,)))

    def group_sum(t):
        return t if kb == hb else jnp.sum(t, axis=0, keepdims=True)

    def body(q_ref, kc_ref, kp_ref, vc_ref, vp_ref, sk_ref, o_ref, lse_ref, do_ref, dlse_ref,
             dq_ref, dk_ref, dv_ref, dsk_ref, ck, cv):
        n = pl.program_id(1)

        @pl.when(n == 0)
        def _():
            ck[...] = jnp.zeros_like(ck)
            cv[...] = jnp.zeros_like(cv)
            dsk_ref[...] = jnp.zeros_like(dsk_ref)

        @pl.when(n < nb)
        def _():
            qb = q_ref[...].astype(BF16)
            kc, kp = _kv_block(kc_ref, hb), _kv_block(kp_ref, hb)
            vc, vp = _kv_block(vc_ref, hb), _kv_block(vp_ref, hb)
            s_c, s_p = _band_scores(qb, kc, kp, n, max_dist, scale)
            lse = _row_to_col(lse_ref[:, 0, :, :])
            dlse = _row_to_col(dlse_ref[:, 0, :, :])
            p_c, p_p = jnp.exp(s_c - lse), jnp.exp(s_p - lse)
            dov = do_ref[...]
            dob = dov.astype(BF16)
            corr = dlse - jnp.sum(dov * o_ref[...], axis=-1, keepdims=True)
            ds_c = p_c * (lax.dot_general(dob, vc, dn_nt, preferred_element_type=F32) + corr)
            ds_p = p_p * (lax.dot_general(dob, vp, dn_nt, preferred_element_type=F32) + corr)
            ds_cb, ds_pb = ds_c.astype(BF16), ds_p.astype(BF16)
            dq_ref[...] = (lax.dot_general(ds_cb, kc, dn_nn, preferred_element_type=F32)
                           + lax.dot_general(ds_pb, kp, dn_nn, preferred_element_type=F32)) * scale
            dk_p = lax.dot_general(ds_pb, qb, dn_tn, preferred_element_type=F32) * scale
            dk_c = lax.dot_general(ds_cb, qb, dn_tn, preferred_element_type=F32) * scale
            dv_p = lax.dot_general(p_p.astype(BF16), dob, dn_tn, preferred_element_type=F32)
            dv_c = lax.dot_general(p_c.astype(BF16), dob, dn_tn, preferred_element_type=F32)
            dk_ref[...] = ck[...] + group_sum(dk_p)
            dv_ref[...] = cv[...] + group_sum(dv_p)
            ck[...] = group_sum(dk_c)
            cv[...] = group_sum(dv_c)
            dsk = jnp.sum(jnp.exp(sk_ref[...] - lse) * corr, axis=1, keepdims=True)
            dsk_ref[...] += jnp.broadcast_to(dsk, dsk_ref.shape)

        @pl.when(n == nb)
        def _():
            dk_ref[...] = ck[...]
            dv_ref[...] = cv[...]

    cur = lambda n: jnp.minimum(n, nb - 1)
    qsp = BS((hb, ATT_BLOCK, hd), lambda h, n: (h, cur(n), 0))
    csp = BS((kb, ATT_BLOCK, hd), lambda h, n: (h, cur(n), 0))
    psp = BS((kb, ATT_BLOCK, hd), lambda h, n: (h, jnp.maximum(cur(n) - 1, 0), 0))
    lsp = BS((hb, 1, 1, ATT_BLOCK), lambda h, n: (h, cur(n), 0, 0))
    osp = BS((kb, ATT_BLOCK, hd), lambda h, n: (h, jnp.maximum(n - 1, 0), 0))
    return _pcall(body, name=name, grid=(H // hb, nb + 1),
                  in_specs=[qsp, csp, psp, csp, psp, BS((hb, 1, 1), lambda h, n: (h, 0, 0)), qsp, lsp, qsp, lsp],
                  out_specs=[qsp, osp, osp, BS((hb, 1, LANES), lambda h, n: (h, 0, 0))],
                  out_shape=[SDS(q.shape, F32), SDS(k.shape, F32), SDS(v.shape, F32), SDS((H, 1, LANES), F32)],
                  scratch_shapes=[pltpu.VMEM((kb, ATT_BLOCK, hd), F32)] * 2,
                  compiler_params=_cparams(("parallel", "arbitrary")))(q, k, k, v, v, sinks, o, lse, do, dlse)


@functools.partial(jax.custom_vjp, nondiff_argnums=(4, 5))
def band_attention(q, k, v, sinks, max_dist, name):
    return tuple(_band_fwd_call(q, k, v, sinks, max_dist, name))


def _band_attention_fwd(q, k, v, sinks, max_dist, name):
    o, lse = _band_fwd_call(q, k, v, sinks, max_dist, name)
    return (o, lse), (q, k, v, sinks, o, lse)


def _band_attention_bwd(max_dist, name, res, g):
    q, k, v, sinks, o, lse = res
    do, dlse = g
    dq, dk, dv, dsk = _band_bwd_call(q, k, v, sinks, o, lse, do, dlse, max_dist, name + "_bwd")
    return dq, dk, dv, dsk[:, :, 0:1]


band_attention.defvjp(_band_attention_fwd, _band_attention_bwd)


def _merge_weights(l_refs):
    ls = [_row_to_col(r[:, 0, :, :]) for r in l_refs]
    mx = jnp.maximum(jnp.maximum(ls[0], ls[1]), ls[2])
    es = [jnp.exp(l - mx) for l in ls]
    inv = 1.0 / (es[0] + es[1] + es[2])
    return [e * inv for e in es]


def _merge_fwd_call(os_, ls_, name):
    H, L, hd = os_[0].shape
    nb = L // ATT_BLOCK

    def body(o1, o2, o3, l1, l2, l3, out_ref):
        w = _merge_weights((l1, l2, l3))
        out_ref[...] = w[0] * o1[...] + w[1] * o2[...] + w[2] * o3[...]

    osp = BS((H, ATT_BLOCK, hd), lambda n: (0, n, 0))
    lsp = BS((H, 1, 1, ATT_BLOCK), lambda n: (0, n, 0, 0))
    return _pcall(body, name=name, grid=(nb,), in_specs=[osp] * 3 + [lsp] * 3, out_specs=osp,
                  out_shape=SDS((H, L, hd), F32), compiler_params=_cparams(("parallel",)))(*os_, *ls_)


def _merge_bwd_call(os_, ls_, do, name):
    H, L, hd = os_[0].shape
    nb = L // ATT_BLOCK

    def body(o1, o2, o3, l1, l2, l3, do_ref, d1, d2, d3, e1, e2, e3):
        w = _merge_weights((l1, l2, l3))
        dov = do_ref[...]
        dws = [jnp.sum(dov * o[...], axis=-1, keepdims=True) for o in (o1, o2, o3)]
        mean = w[0] * dws[0] + w[1] * dws[1] + w[2] * dws[2]
        for wi, dwi, dref, eref in zip(w, dws, (d1, d2, d3), (e1, e2, e3)):
            dref[...] = wi * dov
            eref[:, 0, :, :] = _col_to_row(wi * (dwi - mean))

    osp = BS((H, ATT_BLOCK, hd), lambda n: (0, n, 0))
    lsp = BS((H, 1, 1, ATT_BLOCK), lambda n: (0, n, 0, 0))
    return _pcall(body, name=name, grid=(nb,), in_specs=[osp] * 3 + [lsp] * 3 + [osp], out_specs=[osp] * 3 + [lsp] * 3,
                  out_shape=[SDS((H, L, hd), F32)] * 3 + [SDS((H, nb, 1, ATT_BLOCK), F32)] * 3,
                  compiler_params=_cparams(("parallel",)))(*os_, *ls_, do)


@functools.partial(jax.custom_vjp, nondiff_argnums=(2,))
def merge3(os_, ls_, name):
    return _merge_fwd_call(os_, ls_, name)


def _merge3_fwd(os_, ls_, name):
    return _merge_fwd_call(os_, ls_, name), (os_, ls_)


def _merge3_bwd(name, res, do):
    os_, ls_ = res
    out = _merge_bwd_call(os_, ls_, do, name + "_bwd")
    return tuple(out[:3]), tuple(out[3:])


merge3.defvjp(_merge3_fwd, _merge3_bwd)


def _xa_probs(qb, kb, scale):
    s = lax.dot_general(qb, kb, (((1,), (1,)), ((), ())), preferred_element_type=F32) * scale
    e = jnp.exp(s - jnp.max(s, axis=-1, keepdims=True))
    return e / jnp.sum(e, axis=-1, keepdims=True)


def _xa_fwd_call(q, kv, name):
    T, W = q.shape
    M = kv.shape[0]
    hd = XA_HEAD_DIM
    tq = _tile(T, 512, 8)
    scale = hd ** -0.5

    def body(q_ref, k_ref, v_ref, o_ref):
        p = _xa_probs(q_ref[...].astype(BF16), k_ref[...].astype(BF16), scale)
        o_ref[...] = jnp.dot(p.astype(BF16), v_ref[...].astype(BF16), preferred_element_type=F32)

    qsp = BS((tq, hd), lambda i, h: (i, h))
    return _pcall(body, name=name, grid=(T // tq, XA_HEADS),
                  in_specs=[qsp, BS((M, hd), lambda i, h: (0, h)), BS((M, hd), lambda i, h: (0, XA_HEADS + h))],
                  out_specs=qsp, out_shape=SDS((T, W), F32),
                  compiler_params=_cparams(("parallel", "parallel")))(q, kv, kv)


def _xa_bwd_call(q, kv, do, name):
    T, W = q.shape
    M = kv.shape[0]
    hd = XA_HEAD_DIM
    tq = _tile(T, 512, 8)
    scale = hd ** -0.5
    dn_nt = (((1,), (1,)), ((), ()))
    dn_tn = (((0,), (0,)), ((), ()))

    def body(q_ref, k_ref, v_ref, do_ref, dq_ref, dk_ref, dv_ref):
        @pl.when(pl.program_id(1) == 0)
        def _():
            dk_ref[...] = jnp.zeros_like(dk_ref)
            dv_ref[...] = jnp.zeros_like(dv_ref)

        qb, kb, vb = q_ref[...].astype(BF16), k_ref[...].astype(BF16), v_ref[...].astype(BF16)
        p = _xa_probs(qb, kb, scale)
        dob = do_ref[...].astype(BF16)
        dp = lax.dot_general(dob, vb, dn_nt, preferred_element_type=F32)
        ds = (p * (dp - jnp.sum(p * dp, axis=-1, keepdims=True))).astype(BF16)
        dq_ref[...] = jnp.dot(ds, kb, preferred_element_type=F32) * scale
        dk_ref[...] += lax.dot_general(ds, qb, dn_tn, preferred_element_type=F32) * scale
        dv_ref[...] += lax.dot_general(p.astype(BF16), dob, dn_tn, preferred_element_type=F32)

    qsp = BS((tq, hd), lambda h, i: (i, h))
    ksp = BS((M, hd), lambda h, i: (0, h))
    return _pcall(body, name=name, grid=(XA_HEADS, T // tq),
                  in_specs=[qsp, ksp, BS((M, hd), lambda h, i: (0, XA_HEADS + h)), qsp],
                  out_specs=[qsp, ksp, ksp], out_shape=[SDS((T, W), F32), SDS((M, W), F32), SDS((M, W), F32)],
                  compiler_params=_cparams(("parallel", "arbitrary")))(q, kv, kv, do)


@functools.partial(jax.custom_vjp, nondiff_argnums=(2,))
def cross_attention(q, kv, name):
    return _xa_fwd_call(q, kv, name)


def _cross_attention_fwd(q, kv, name):
    return _xa_fwd_call(q, kv, name), (q, kv)


def _cross_attention_bwd(name, res, do):
    q, kv = res
    dq, dk, dv = _xa_bwd_call(q, kv, do, name + "_bwd")
    return dq, jnp.concatenate([dk, dv], axis=1)


cross_attention.defvjp(_cross_attention_fwd, _cross_attention_bwd)


def _swiglu_fwd_call(gu, name):
    T, W2 = gu.shape
    F = W2 // 2
    tt = _tile(T, 256, 16)

    def body(g_ref, u_ref, o_ref):
        g = g_ref[...]
        o_ref[...] = ((g * _sigmoid(g)) * u_ref[...]).astype(BF16)

    return _pcall(body, name=name, grid=(T // tt,),
                  in_specs=[BS((tt, F), lambda i: (i, 0)), BS((tt, F), lambda i: (i, 1))],
                  out_specs=BS((tt, F), lambda i: (i, 0)), out_shape=SDS((T, F), BF16),
                  compiler_params=_cparams(("parallel",)))(gu, gu)


def _swiglu_bwd_call(gu, dact, name):
    T, W2 = gu.shape
    F = W2 // 2
    tt = _tile(T, 128, 16)

    def body(g_ref, u_ref, d_ref, o_ref):
        g, d = g_ref[...], d_ref[...]
        sg = _sigmoid(g)
        o_ref[:, :F] = (d * u_ref[...] * (sg * (1.0 + g * (1.0 - sg)))).astype(BF16)
        o_ref[:, F:] = (d * (g * sg)).astype(BF16)

    return _pcall(body, name=name, grid=(T // tt,),
                  in_specs=[BS((tt, F), lambda i: (i, 0)), BS((tt, F), lambda i: (i, 1)), BS((tt, F), lambda i: (i, 0))],
                  out_specs=BS((tt, W2), lambda i: (i, 0)), out_shape=SDS((T, W2), BF16),
                  compiler_params=_cparams(("parallel",)))(gu, gu, dact)


@functools.partial(jax.custom_vjp, nondiff_argnums=(6,))
def ffn_block(h, g, w1b, w1c, w2b, w2c, name):
    return _ffn_fwd(h, g, w1b, w1c, w2b, w2c, name)[0]


def _ffn_fwd(h, g, w1b, w1c, w2b, w2c, name):
    hn = _rms_fwd_call(h, g, name + "_norm", BF16)
    gu = _mm(hn, w1b, name=name + "_gu", tm=1024, tn=1408, tk=1024)
    act = _swiglu_fwd_call(gu, name + "_swiglu")
    out = _mm(act, w2b, add=h, name=name + "_down", tm=1024, tn=1024, tk=1408)
    return out, (h, g, hn, gu, act, w1b, w2b)


def _ffn_bwd(name, res, dout):
    h, g, hn, gu, act, w1b, w2b = res
    dact = _mm(dout, w2b, tb=True, name=name + "_down_da", tm=1024, tn=1408, tk=1024)
    dw2 = _mm(act, dout, ta=True, name=name + "_down_dw", tm=1408, tn=1024, tk=1024)
    dgu = _swiglu_bwd_call(gu, dact, name + "_swiglu_bwd")
    dhn = _mm(dgu, w1b, tb=True, name=name + "_gu_da", tm=1024, tn=1024, tk=1408)
    dw1 = _mm(hn, dgu, ta=True, name=name + "_gu_dw", tm=1024, tn=1408, tk=1024)
    dh, dg = _rms_bwd_call(h, g, dhn, name + "_norm_bwd", add=dout)
    return dh, dg.reshape(g.shape), jnp.zeros_like(w1b), dw1, jnp.zeros_like(w2b), dw2


ffn_block.defvjp(_ffn_fwd, _ffn_bwd)


def _final_call(h, g, target, name):
    T, Dm = h.shape
    tt = _tile(T, 512, 8)

    def body(x_ref, g_ref, t_ref, loss_ref, dx_ref, dg_ref):
        @pl.when(pl.program_id(0) == 0)
        def _():
            loss_ref[...] = jnp.zeros_like(loss_ref)
            dg_ref[...] = jnp.zeros_like(dg_ref)

        xv, gv = x_ref[...], g_ref[...]
        r = lax.rsqrt(jnp.mean(xv * xv, axis=-1, keepdims=True) + NORM_EPS)
        xh = xv * r
        err = xh * gv - t_ref[...]
        loss_ref[...] += 0.5 * jnp.sum(jnp.mean(err * err, axis=-1, keepdims=True), axis=0, keepdims=True)
        dy = err * (1.0 / Dm)
        dyg = dy * gv
        dx_ref[...] = r * (dyg - xh * jnp.mean(dyg * xh, axis=-1, keepdims=True))
        dg_ref[...] += jnp.sum(dy * xh, axis=0, keepdims=True)

    row = BS((tt, Dm), lambda i: (i, 0))
    vec = BS((1, Dm), lambda i: (0, 0))
    return _pcall(body, name=name, grid=(T // tt,), in_specs=[row, vec, row],
                  out_specs=[BS((1, 1), lambda i: (0, 0)), row, vec],
                  out_shape=[SDS((1, 1), F32), SDS((T, Dm), F32), SDS((1, Dm), F32)],
                  compiler_params=_cparams(("arbitrary",)))(h, g.reshape(1, Dm), target)


ADAMW_BLOCK_ELEMS = 64 * 1024


def _adamw_call(parts, w, m, v, name):
    shape = w.shape
    if not isinstance(parts, (list, tuple)):
        parts, shape3 = [parts], (1,) + shape
    else:
        shape3 = shape
    n_lead = shape3[0]
    r, N = shape3[-2], shape3[-1]
    Ld = math.prod(shape3[1:-2])
    w, m, v = (t.reshape(n_lead * Ld, r, N) for t in (w, m, v))
    tr = _tile(r, max(8, ADAMW_BLOCK_ELEMS // N), 8)
    c1 = 1.0 - ADAM_B1 ** ADAM_STEP
    c2 = 1.0 - ADAM_B2 ** ADAM_STEP
    outs = None
    for lead, p in enumerate(parts):
        def body(p_ref, w_ref, m_ref, v_ref, *rest):
            g_ref, d_ref, nm_ref, nv_ref = rest[-4:]
            g = p_ref[0]
            for j in range(1, N_DEV):
                g = g + p_ref[j]
            nm = ADAM_B1 * m_ref[...] + (1.0 - ADAM_B1) * g
            nv = ADAM_B2 * v_ref[...] + (1.0 - ADAM_B2) * (g * g)
            g_ref[...] = g
            nm_ref[...] = nm
            nv_ref[...] = nv
            d_ref[...] = -ADAM_LR * ((nm / c1) / (jnp.sqrt(nv / c2) + ADAM_EPS) + ADAM_WD * w_ref[...])

        base = lead * Ld
        row = BS((1, tr, N), lambda l, i, base=base: (base + l, i, 0))
        prev = [] if outs is None else list(outs)
        outs = _pcall(body, name=f"{name}_{lead}", grid=(Ld, r // tr),
                      in_specs=[BS((N_DEV, 1, tr, N), lambda l, i: (0, l, i, 0)), row, row, row]
                      + [BS(memory_space=pl.ANY)] * len(prev),
                      out_specs=[row] * 4, out_shape=[SDS((n_lead * Ld, r, N), F32)] * 4,
                      input_output_aliases={4 + j: j for j in range(len(prev))},
                      compiler_params=_cparams(("parallel", "parallel")))(p.reshape(N_DEV, Ld, r, N), w, m, v, *prev)
    return [t.reshape(shape) for t in outs]


def _place():
    return lax.axis_index("x"), lax.axis_index("y"), lax.axis_index("c")


def _all_gather(xs, name):
    n = len(xs)
    pairs = [(i, l) for i, x in enumerate(xs) for l in range(x.shape[0])]

    def body(*refs):
        x_refs, o_refs = refs[:n], refs[n:2 * n]
        send_sems, recv_sems, local_sems = refs[2 * n:]
        x_, y_, c_ = _place()
        me, sibling = (x_, y_, c_), (x_, y_, 1 - c_)
        chips = [(1 - x_, y_), (x_, 1 - y_), (1 - x_, 1 - y_)]

        def copy(e, k, block, to, from_input=False):
            i, l = pairs[e]
            px, py, pc = block
            dst = o_refs[i].at[l, 4 * px + 2 * py + pc]
            return pltpu.make_async_remote_copy(
                src_ref=x_refs[i].at[l] if from_input else dst, dst_ref=dst,
                send_sem=send_sems.at[7 * e + k], recv_sem=recv_sems.at[7 * e + k],
                device_id=to, device_id_type=pl.DeviceIdType.MESH)

        every = range(len(pairs))
        mine = [pltpu.make_async_copy(x_refs[i].at[l], o_refs[i].at[l, 4 * x_ + 2 * y_ + c_], local_sems.at[e])
                for e, (i, l) in enumerate(pairs)]
        for cp in mine:
            cp.start()
        first = [copy(e, 0, me, sibling, True) for e in every]
        first += [copy(e, 1 + j, me, (*chip, c_), True) for j, chip in enumerate(chips) for e in every]
        for cp in first:
            cp.start()
        passed = []
        for j, chip in enumerate(chips):
            for e in every:
                copy(e, 1 + j, (*chip, c_), me).wait_recv()
            for e in every:
                cp = copy(e, 4 + j, (*chip, c_), sibling)
                cp.start()
                passed.append(cp)
        for e in every:
            copy(e, 0, sibling, me).wait_recv()
        for j, chip in enumerate(chips):
            for e in every:
                copy(e, 4 + j, (*chip, 1 - c_), me).wait_recv()
        for cp in first + passed:
            cp.wait_send()
        for cp in mine:
            cp.wait()

    any_spec = BS(memory_space=pl.ANY)
    return _pcall(body, name=name, in_specs=[any_spec] * n, out_specs=[any_spec] * n,
                  out_shape=[SDS((x.shape[0], N_DEV) + x.shape[1:], x.dtype) for x in xs],
                  scratch_shapes=[pltpu.SemaphoreType.DMA((7 * len(pairs),)), pltpu.SemaphoreType.DMA((7 * len(pairs),)),
                                  pltpu.SemaphoreType.DMA((len(pairs),))],
                  compiler_params=pltpu.CompilerParams(has_side_effects=True))(*xs)


def _peer_of(k, place):
    x_, y_, c_ = place
    fx, fy, fc = (k >> 2) & 1, (k >> 1) & 1, k & 1
    return fx + x_ - 2 * fx * x_, fy + y_ - 2 * fy * y_, fc + c_ - 2 * fc * c_


def _split_copy(src_ref, land_ref, send_sems, recv_sems, e, k, place, scatter):
    x_, y_, c_ = place
    px, py, pc = _peer_of(k, place)
    return pltpu.make_async_remote_copy(
        src_ref=src_ref.at[4 * px + 2 * py + pc] if scatter else src_ref, dst_ref=land_ref.at[4 * x_ + 2 * y_ + c_],
        send_sem=send_sems.at[7 * e + k - 1], recv_sem=recv_sems.at[7 * e + k - 1],
        device_id=(px, py, pc), device_id_type=pl.DeviceIdType.MESH)


_HBM_SPEC = BS(memory_space=pltpu.HBM)
_SEM_SPEC = BS(memory_space=pltpu.SEMAPHORE)
_EFFECT = pltpu.SideEffectType.DATAFLOW_SIDE_EFFECTING


def _copies_start(srcs, scatter, name, thru=None):
    n = len(srcs)
    lands = [lax.empty(s.shape if scatter else (N_DEV,) + s.shape, s.dtype) for s in srcs]
    passed = srcs + lands + ([] if thru is None else [thru])

    def body(*refs):
        src_refs, land_refs = refs[:n], refs[n:2 * n]
        send_sems, recv_sems = refs[len(passed)], refs[len(passed) + 1]
        token = refs[-1]
        place = _place()
        for e in range(n):
            for k in range(1, N_DEV):
                _split_copy(src_refs[e], land_refs[e], send_sems, recv_sems, e, k, place, scatter).start()
        token[...] = jnp.zeros_like(token)

    hbm = lambda t: pltpu.with_memory_space_constraint(t, pltpu.HBM)
    out = _pcall(body, name=name,
                 out_shape=(pltpu.SemaphoreType.DMA((7 * n,)), pltpu.SemaphoreType.DMA((7 * n,)),
                            *[pltpu.HBM(t.shape, t.dtype) for t in passed], SDS((8, LANES), F32)),
                 in_specs=[_HBM_SPEC] * len(passed),
                 out_specs=(_SEM_SPEC, _SEM_SPEC, *[_HBM_SPEC] * len(passed), BS(memory_space=pltpu.VMEM)),
                 input_output_aliases={i: 2 + i for i in range(len(passed))},
                 compiler_params=pltpu.CompilerParams(has_side_effects=_EFFECT))(*[hbm(t) for t in passed])
    return out[0], out[1], list(out[2:2 + n]), list(out[2 + n:2 + 2 * n]), (None if thru is None else out[2 + 2 * n])


def _copies_wait(started, which, scatter, after, name):
    send_sems, recv_sems, srcs, lands, _ = started
    n = len(which)

    def body(*refs):
        src_refs, land_refs = refs[:n], refs[n:2 * n]
        send_s, recv_s = refs[2 * n], refs[2 * n + 1]
        place = _place()
        for j, e in enumerate(which):
            for k in range(1, N_DEV):
                cp = _split_copy(src_refs[j], land_refs[j], send_s, recv_s, e, k, place, scatter)
                cp.wait_send()
                cp.wait_recv()

    args = [srcs[e] for e in which] + [lands[e] for e in which]
    out = _pcall(body, name=name, out_shape=tuple(pltpu.HBM(t.shape, t.dtype) for t in args),
                 in_specs=[_HBM_SPEC] * (2 * n) + [_SEM_SPEC, _SEM_SPEC, BS(memory_space=pl.ANY)],
                 out_specs=tuple([_HBM_SPEC] * (2 * n)), input_output_aliases={i: i for i in range(2 * n)},
                 compiler_params=pltpu.CompilerParams(has_side_effects=_EFFECT))(*args, send_sems, recv_sems, after)
    return list(out[:n]), list(out[n:])


def _with_own_block(land, own_block):
    me = 4 * lax.axis_index("x") + 2 * lax.axis_index("y") + lax.axis_index("c")
    return lax.dynamic_update_index_in_dim(land, own_block, me, 0)


def _pad_flat(t, quantum=PACK_QUANTUM):
    f = t.reshape(-1)
    pad = (-f.shape[0]) % quantum
    return jnp.pad(f, (0, pad)) if pad else f


def _pack(arrs, dtype):
    return jnp.concatenate([_pad_flat(a.astype(dtype)) for a in arrs]).reshape(-1, LANES)


def _unpack(buf, shapes, lead=()):
    flat = buf.reshape(lead + (-1,))
    out, off = [], 0
    for s in shapes:
        n = math.prod(s)
        out.append(flat[..., off:off + n].reshape(lead + tuple(s)))
        off += n + (-n) % PACK_QUANTUM
    return out


def _full_from_gathered(g, axis):
    t = jnp.moveaxis(g, 0, axis)
    s = t.shape
    return t.reshape(s[:axis] + (s[axis] * s[axis + 1],) + s[axis + 2:])


def _parts_from_full(t, axis):
    s = t.shape
    t = t.reshape(s[:axis] + (N_DEV, s[axis] // N_DEV) + s[axis + 1:])
    return jnp.moveaxis(t, axis, 0)


def _heads(t, n_heads):
    T = t.shape[0]
    return t.reshape(T, n_heads, HEAD_DIM).transpose(1, 0, 2)


def _unheads(t):
    H, T, hd = t.shape
    return t.transpose(1, 0, 2).reshape(T, H * hd)


def _dilated_attention(q, k, v, name):
    H, T, hd = q.shape
    no_sink = jnp.full((1, 1, 1), NEG, F32)
    outs, lses = [], []
    for d in DILATIONS:
        Ld = T // d

        def strided(t):
            return t.reshape(H, Ld, d, hd).transpose(0, 2, 1, 3).reshape(H * d, Ld, hd)

        o, lse = band_attention(strided(q), strided(k), strided(v), jnp.broadcast_to(no_sink, (H * d, 1, 1)),
                                ATT_BLOCK, f"{name}_d{d}")
        outs.append(o.reshape(H, d, Ld, hd).transpose(0, 2, 1, 3).reshape(H, T, hd))
        lses.append(lse.reshape(H, d, Ld).transpose(0, 2, 1).reshape(H, T // ATT_BLOCK, 1, ATT_BLOCK))
    return merge3(tuple(outs), tuple(lses), name + "_merge")


STAGES = (
    ("proj0", ('mix_norm',), (('ab_w_in', 0),)),
    ("mixer0", ('lru_conv_w', 'lru_conv_b', 'lru_ba', 'lru_bx', 'lru_lambda'),
     (('lru_wa', 0), ('lru_wx', 0), ('ab_w_out', 0))),
    ("xa0", ('xa_norm', 'xa_mem_norm'), (('xa_wq', 0), ('xa_wkv', 0), ('xa_wo', 0))),
    ("ffn0", ('ffn_norm',), (('ffn_w_gate_up', 0), ('ffn_w_down', 0))),
    ("mixer1", ('mix_norm', 'c_b_qkv', 'c_sinks', 'c_b_out'), (('c_w_qkv', 0), ('c_w_out', 0))),
    ("xa1", ('xa_norm', 'xa_mem_norm'), (('xa_wq', 1), ('xa_wkv', 1), ('xa_wo', 1))),
    ("ffn1", ('ffn_norm',), (('ffn_w_gate_up', 1), ('ffn_w_down', 1))),
)


def _stage_fn(stage, Wb, tabs, mem):
    layer = int(stage[-1])
    L = f"l{layer}"

    def run(S, Cw, h):
        def lin(a, key, bias, add, name, rows=None):
            wb, wc = Wb[key], Cw[key]
            if rows is not None:
                wb, wc = wb[rows], wc[rows]
            return linear(a, wb, wc, bias, add, name)

        def norm_lin(a, gain, key, bias, name):
            return norm_linear(a, gain, Wb[key], Cw[key], bias, name)

        if stage == "proj0":
            return h, norm_lin(h, S['mix_norm'][0], ('ab_w_in', 0), None, L + "_w_in")
        if stage == "mixer0":
            h, proj = h
            C = S['lru_conv_w'].shape[-1]
            xc = conv4(proj[:, :C], S['lru_conv_w'][0], S['lru_conv_b'][0], L + "_conv")
            ga, gx = lru_gates(xc, Wb['lru_wa', 0], Cw['lru_wa', 0], S['lru_ba'][0],
                               Wb['lru_wx', 0], Cw['lru_wx', 0], S['lru_bx'][0], L + "_gates")
            rec = lru_scan(xc, ga, gx, proj[:, C:2 * C], S['lru_lambda'][0], L + "_scan")
            bw = B_HEADS * HEAD_DIM
            q = rope(proj[:, 2 * C:2 * C + bw], tabs, L + "_rope_q")
            k = rope(proj[:, 2 * C + bw:2 * C + 2 * bw], tabs, L + "_rope_k")
            v = proj[:, 2 * C + 2 * bw:]
            att = _unheads(_dilated_attention(_heads(q, B_HEADS), _heads(k, B_HEADS), _heads(v, B_HEADS), L + "_att"))
            h = lin(att, ('ab_w_out', 0), None, h, L + "_w_out_att", slice(C, None))
            return lin(rec, ('ab_w_out', 0), None, h, L + "_w_out_rec", slice(0, C))
        if stage == "mixer1":
            qw = C_HEADS * HEAD_DIM
            kw = C_KV_HEADS * HEAD_DIM
            qkv = norm_lin(h, S['mix_norm'][1], ('c_w_qkv', 0), S['c_b_qkv'][0], L + "_w_qkv")
            q = rope(qkv[:, :qw], tabs, L + "_rope_q")
            k = rope(qkv[:, qw:qw + kw], tabs, L + "_rope_k")
            v = qkv[:, qw + kw:]
            o, _ = band_attention(_heads(q, C_HEADS), _heads(k, C_KV_HEADS), _heads(v, C_KV_HEADS),
                                  S['c_sinks'][0].reshape(C_HEADS, 1, 1), ATT_BLOCK - 1, L + "_att")
            return lin(_unheads(o), ('c_w_out', 0), S['c_b_out'][0], h, L + "_w_out")
        if stage.startswith("xa"):
            xq = norm_lin(h, S['xa_norm'][layer], ('xa_wq', layer), None, L + "_xa_wq")
            xkv = norm_lin(mem, S['xa_mem_norm'][layer], ('xa_wkv', layer), None, L + "_xa_wkv")
            return lin(cross_attention(xq, xkv, L + "_xa"), ('xa_wo', layer), None, h, L + "_xa_wo")
        gu, down = ('ffn_w_gate_up', layer), ('ffn_w_down', layer)
        return ffn_block(h, S['ffn_norm'][layer], Wb[gu], Cw[gu], Wb[down], Cw[down], L + "_ffn")

    return run


def kernel(x, mem, mix_norm, ab_w_in, lru_conv_w, lru_conv_b, lru_wa, lru_ba, lru_wx, lru_bx, lru_lambda, ab_w_out, c_w_qkv, c_b_qkv, c_sinks, c_w_out, c_b_out, xa_norm, xa_mem_norm, xa_wq, xa_wkv, xa_wo, ffn_norm, ffn_w_gate_up, ffn_w_down, final_norm, loss_target, m_mix_norm, m_ab_w_in, m_lru_conv_w, m_lru_conv_b, m_lru_wa, m_lru_ba, m_lru_wx, m_lru_bx, m_lru_lambda, m_ab_w_out, m_c_w_qkv, m_c_b_qkv, m_c_sinks, m_c_w_out, m_c_b_out, m_xa_norm, m_xa_mem_norm, m_xa_wq, m_xa_wkv, m_xa_wo, m_ffn_norm, m_ffn_w_gate_up, m_ffn_w_down, m_final_norm, v_mix_norm, v_ab_w_in, v_lru_conv_w, v_lru_conv_b, v_lru_wa, v_lru_ba, v_lru_wx, v_lru_bx, v_lru_lambda, v_ab_w_out, v_c_w_qkv, v_c_b_qkv, v_c_sinks, v_c_w_out, v_c_b_out, v_xa_norm, v_xa_mem_norm, v_xa_wq, v_xa_wkv, v_xa_wo, v_ffn_norm, v_ffn_w_gate_up, v_ffn_w_down, v_final_norm):
    w_loc = dict(zip(WEIGHT_NAMES, (mix_norm, ab_w_in, lru_conv_w, lru_conv_b, lru_wa, lru_ba, lru_wx, lru_bx, lru_lambda, ab_w_out, c_w_qkv, c_b_qkv, c_sinks, c_w_out, c_b_out, xa_norm, xa_mem_norm, xa_wq, xa_wkv, xa_wo, ffn_norm, ffn_w_gate_up, ffn_w_down, final_norm)))
    m_loc = dict(zip(WEIGHT_NAMES, (m_mix_norm, m_ab_w_in, m_lru_conv_w, m_lru_conv_b, m_lru_wa, m_lru_ba, m_lru_wx, m_lru_bx, m_lru_lambda, m_ab_w_out, m_c_w_qkv, m_c_b_qkv, m_c_sinks, m_c_w_out, m_c_b_out, m_xa_norm, m_xa_mem_norm, m_xa_wq, m_xa_wkv, m_xa_wo, m_ffn_norm, m_ffn_w_gate_up, m_ffn_w_down, m_final_norm)))
    v_loc = dict(zip(WEIGHT_NAMES, (v_mix_norm, v_ab_w_in, v_lru_conv_w, v_lru_conv_b, v_lru_wa, v_lru_ba, v_lru_wx, v_lru_bx, v_lru_lambda, v_ab_w_out, v_c_w_qkv, v_c_b_qkv, v_c_sinks, v_c_w_out, v_c_b_out, v_xa_norm, v_xa_mem_norm, v_xa_wq, v_xa_wkv, v_xa_wo, v_ffn_norm, v_ffn_w_gate_up, v_ffn_w_down, v_final_norm)))

    me = 4 * lax.axis_index("x") + 2 * lax.axis_index("y") + lax.axis_index("c")

    keys = [key for _, _, stage_keys in STAGES for key in stage_keys]
    shards = [w_loc[n][l].astype(BF16) for n, l in keys]
    small_g = _all_gather([_pack([w_loc[n] for n in SMALL], F32)[None]], "gather_small")[0][0]
    gather = _copies_start(shards, False, "gather_start", thru=small_g)
    small_g = gather[4]
    S = {n: w_loc[n] for n in REPLICATED}
    for n, t in zip(SMALL, _unpack(small_g, [w_loc[n].shape for n in SMALL], lead=(N_DEV,))):
        S[n] = _full_from_gathered(t, SHARD_AXIS[n])

    tabs = _rope_tables(x.shape[1])
    Wb, vjps = {}, []
    h = x[0]
    for stage, small_names, stage_keys in STAGES:
        which = [keys.index(key) for key in stage_keys]
        _, lands = _copies_wait(gather, which, False, jax.tree.leaves(h)[-1], "gather_wait_" + stage)
        for key, e, land in zip(stage_keys, which, lands):
            Wb[key] = _full_from_gathered(_with_own_block(land, shards[e]), SHARD_AXIS[key[0]] - 1)
        carriers = {key: jnp.zeros(Wb[key].shape, F32) for key in stage_keys}
        h, vjp_fn = jax.vjp(_stage_fn(stage, Wb, tabs, mem[0]), {n: S[n] for n in small_names}, carriers, h)
        vjps.append(vjp_fn)
    loss_part, dh, dg_final = _final_call(h, S['final_norm'], loss_target[0], "final_loss")
    loss = lax.psum(loss_part[0, 0], ("x", "y", "c"))

    grads = {'final_norm': dg_final.reshape(final_norm.shape)}
    exchanges, send_keys, send_parts = [], [], []
    for (stage, small_names, stage_keys), vjp_fn in zip(reversed(STAGES), reversed(vjps)):
        g_small, g_big, dh = vjp_fn(dh)
        for n in small_names:
            grads[n] = grads[n] + g_small[n] if n in grads else g_small[n]
        send_keys += list(stage_keys)
        send_parts += [_parts_from_full(g_big[key], SHARD_AXIS[key[0]] - 1) for key in stage_keys]
        if stage == "xa1":
            continue
        if stage == "mixer0":
            small_parts = [_parts_from_full(grads[n], SHARD_AXIS[n]) for n in SMALL]
            send_keys.append("small")
            send_parts.append(jnp.stack([_pack([p[j] for p in small_parts], F32) for j in range(N_DEV)]))
        leaves, tree = jax.tree.flatten(dh)
        started = _copies_start(send_parts, True, "grad_start_" + stage, thru=leaves[0])
        dh = jax.tree.unflatten(tree, [started[4]] + leaves[1:])
        exchanges.append((stage, started, send_keys))
        send_keys, send_parts = [], []
    dx = dh

    parts = {}
    for stage, started, ex_keys in exchanges:
        srcs, lands = _copies_wait(started, list(range(len(ex_keys))), True, dx, "grad_wait_" + stage)
        for key, src, land in zip(ex_keys, srcs, lands):
            parts[key] = _with_own_block(land, lax.dynamic_index_in_dim(src, me, 0, keepdims=False))
    out = {}

    def adamw(p, names, call_name):
        if len(names) == 1:
            res = _adamw_call(p, w_loc[names[0]], m_loc[names[0]], v_loc[names[0]], call_name)
            for kind, t in zip(("grad", "delta", "new_m", "new_v"), res):
                out[kind, names[0]] = t
        else:
            res = _adamw_call(p, *[_pack([d[n] for n in names], F32) for d in (w_loc, m_loc, v_loc)], call_name)
            for kind, buf in zip(("grad", "delta", "new_m", "new_v"), res):
                for n, t in zip(names, _unpack(buf, [w_loc[n].shape for n in names])):
                    out[kind, n] = t

    for n in BIG:
        adamw([parts[n, l] for l in range(w_loc[n].shape[0])], [n], "adamw_" + n)
    adamw(parts["small"], SMALL, "adamw_small")
    rep_g = _all_gather([_pack([grads[n] for n in REPLICATED], F32)[None]], "gather_rep_grads")[0][0]
    adamw(rep_g, REPLICATED, "adamw_replicated")

    return (loss, dx[None], *[out[kind, n] for kind in ("grad", "delta", "new_m", "new_v") for n in WEIGHT_NAMES])
```

```python
import functools
import math

import jax
import jax.numpy as jnp
from jax import lax
from jax.experimental import pallas as pl
from jax.experimental.pallas import tpu as pltpu

F32 = jnp.float32
BF16 = jnp.bfloat16
SDS = jax.ShapeDtypeStruct
BS = pl.BlockSpec

N_DEV = 8
NORM_EPS = 1e-6
ROPE_THETA = 500000.0
HEAD_DIM = 64
ROT_DIM = 16
ATT_BLOCK = 128
LRU_C = 8.0
LRU_HEADS = 4
DILATIONS = (1, 4, 16)
B_HEADS = 8
C_HEADS = 16
C_KV_HEADS = 2
XA_HEADS = 4
XA_HEAD_DIM = 128
NEG = -1e30
ADAM_LR, ADAM_B1, ADAM_B2, ADAM_EPS, ADAM_WD, ADAM_STEP = 0.001, 0.9, 0.999, 1e-08, 0.01, 10
LANES = 128
VMEM_LIMIT = 48 * 1024 * 1024

WEIGHT_NAMES = ['mix_norm', 'ab_w_in', 'lru_conv_w', 'lru_conv_b', 'lru_wa', 'lru_ba', 'lru_wx', 'lru_bx',
                'lru_lambda', 'ab_w_out', 'c_w_qkv', 'c_b_qkv', 'c_sinks', 'c_w_out', 'c_b_out', 'xa_norm',
                'xa_mem_norm', 'xa_wq', 'xa_wkv', 'xa_wo', 'ffn_norm', 'ffn_w_gate_up', 'ffn_w_down', 'final_norm']
SHARD_AXIS = {'ab_w_in': 2, 'lru_conv_w': 2, 'lru_wa': 2, 'lru_ba': 2, 'lru_wx': 2, 'lru_bx': 2, 'ab_w_out': 1,
              'c_w_qkv': 2, 'c_b_qkv': 1, 'c_w_out': 1, 'c_b_out': 1, 'xa_wq': 1, 'xa_wkv': 1, 'xa_wo': 2,
              'ffn_w_gate_up': 2, 'ffn_w_down': 1}
BIG = ['ab_w_in', 'lru_wa', 'lru_wx', 'ab_w_out', 'c_w_qkv', 'c_w_out', 'xa_wq', 'xa_wkv', 'xa_wo',
       'ffn_w_gate_up', 'ffn_w_down']
SMALL = ['lru_conv_w', 'lru_ba', 'lru_bx', 'c_b_qkv', 'c_b_out']
SHARDED = BIG + SMALL
REPLICATED = [n for n in WEIGHT_NAMES if n not in SHARD_AXIS]
PACK_QUANTUM = 2048


def _pcall(body, **kw):
    return pl.pallas_call(body, **kw)


def _cparams(sem=None):
    return pltpu.CompilerParams(dimension_semantics=sem, vmem_limit_bytes=VMEM_LIMIT)


def _tile(n, target, mult=LANES):
    if n <= target:
        return n
    t = (target // mult) * mult
    while t >= mult:
        if n % t == 0:
            return t
        t -= mult
    return n


def _sigmoid(x):
    return 1.0 / (1.0 + jnp.exp(-x))


def _expm1(x):
    small = x * (1.0 + x * (0.5 + x * (1.0 / 6.0 + x * (1.0 / 24.0))))
    return jnp.where(jnp.abs(x) < 0.03, small, jnp.exp(x) - 1.0)


_GELU_C = math.sqrt(2.0 / math.pi)


def _gelu_parts(y):
    y2 = y * y
    th = jnp.tanh(_GELU_C * (y + 0.044715 * y * y2))
    g = 0.5 * y * (1.0 + th)
    dg = 0.5 * (1.0 + th) + 0.5 * y * (1.0 - th * th) * _GELU_C * (1.0 + 3.0 * 0.044715 * y2)
    return g, dg


def _mm(a, b, *, ta=False, tb=False, bias=None, add=None, name, tm=512, tn=512, tk=2048):
    M, K = (a.shape[1], a.shape[0]) if ta else a.shape
    N = b.shape[0] if tb else b.shape[1]
    tm, tn, tk = _tile(M, tm), _tile(N, tn), _tile(K, tk)
    nk = K // tk
    dn = (((0 if ta else 1,), (1 if tb else 0,)), ((), ()))

    def body(*refs):
        a_ref, b_ref = refs[0], refs[1]
        pos = 2
        bias_ref = add_ref = None
        if bias is not None:
            bias_ref = refs[pos]
            pos += 1
        if add is not None:
            add_ref = refs[pos]
            pos += 1
        o_ref = refs[pos]
        part = lax.dot_general(a_ref[...].astype(BF16), b_ref[...].astype(BF16), dn, preferred_element_type=F32)

        def finish(r):
            if bias_ref is not None:
                r = r + bias_ref[...]
            if add_ref is not None:
                r = r + add_ref[...]
            o_ref[...] = r

        if nk == 1:
            finish(part)
            return
        acc_ref = refs[pos + 1]
        k = pl.program_id(2)

        @pl.when(k == 0)
        def _():
            acc_ref[...] = part

        @pl.when((k > 0) & (k < nk - 1))
        def _():
            acc_ref[...] += part

        @pl.when(k == nk - 1)
        def _():
            finish(acc_ref[...] + part)

    in_specs = [BS((tk, tm), lambda i, j, k: (k, i)) if ta else BS((tm, tk), lambda i, j, k: (i, k)),
                BS((tn, tk), lambda i, j, k: (j, k)) if tb else BS((tk, tn), lambda i, j, k: (k, j))]
    args = [a, b]
    if bias is not None:
        in_specs.append(BS((1, tn), lambda i, j, k: (0, j)))
        args.append(bias.reshape(1, N))
    if add is not None:
        in_specs.append(BS((tm, tn), lambda i, j, k: (i, j)))
        args.append(add)
    return _pcall(body, name=name, grid=(M // tm, N // tn, nk), in_specs=in_specs,
                  out_specs=BS((tm, tn), lambda i, j, k: (i, j)), out_shape=SDS((M, N), F32),
                  scratch_shapes=[pltpu.VMEM((tm, tn), F32)] if nk > 1 else [],
                  compiler_params=_cparams(("parallel", "parallel", "arbitrary")))(*args)


def _colsum(x, name):
    T, N = x.shape
    tt = _tile(T, 512, 8)

    def body(x_ref, o_ref):
        @pl.when(pl.program_id(0) == 0)
        def _():
            o_ref[...] = jnp.zeros_like(o_ref)

        o_ref[...] += jnp.sum(x_ref[...], axis=0, keepdims=True)

    return _pcall(body, name=name, grid=(T // tt,), in_specs=[BS((tt, N), lambda i: (i, 0))],
                  out_specs=BS((1, N), lambda i: (0, 0)), out_shape=SDS((1, N), F32),
                  compiler_params=_cparams(("arbitrary",)))(x)


@functools.partial(jax.custom_vjp, nondiff_argnums=(5,))
def linear(a, wb, wc, bias, add, name):
    return _mm(a, wb, bias=bias, add=add, name=name)


def _linear_fwd(a, wb, wc, bias, add, name):
    return _mm(a, wb, bias=bias, add=add, name=name), (a, wb, bias is not None, add is not None)


def _linear_bwd(name, res, g):
    a, wb, has_bias, has_add = res
    da = _mm(g, wb, tb=True, name=name + "_da")
    dw = _mm(a, g, ta=True, name=name + "_dw")
    dbias = _colsum(g, name + "_db").reshape(-1) if has_bias else None
    return da, jnp.zeros_like(wb), dw, dbias, (g if has_add else None)


linear.defvjp(_linear_fwd, _linear_bwd)


def _rms_fwd_call(x, g, name, out_dtype=F32):
    T, Dm = x.shape
    tt = _tile(T, 512, 16)

    def body(x_ref, g_ref, o_ref):
        xv = x_ref[...]
        r = lax.rsqrt(jnp.mean(xv * xv, axis=-1, keepdims=True) + NORM_EPS)
        o_ref[...] = ((xv * r) * g_ref[...]).astype(out_dtype)

    return _pcall(body, name=name, grid=(T // tt,),
                  in_specs=[BS((tt, Dm), lambda i: (i, 0)), BS((1, Dm), lambda i: (0, 0))],
                  out_specs=BS((tt, Dm), lambda i: (i, 0)), out_shape=SDS((T, Dm), out_dtype),
                  compiler_params=_cparams(("parallel",)))(x, g.reshape(1, Dm))


def _rms_bwd_call(x, g, dy, name, add=None):
    T, Dm = x.shape
    tt = _tile(T, 512, 8)

    def body(*refs):
        x_ref, g_ref, dy_ref = refs[:3]
        dx_ref, dg_ref = refs[-2:]
        xv = x_ref[...]
        r = lax.rsqrt(jnp.mean(xv * xv, axis=-1, keepdims=True) + NORM_EPS)
        xh = xv * r
        dy = dy_ref[...]
        dyg = dy * g_ref[...]
        dx = r * (dyg - xh * jnp.mean(dyg * xh, axis=-1, keepdims=True))
        dx_ref[...] = dx if add is None else dx + refs[3][...]

        @pl.when(pl.program_id(0) == 0)
        def _():
            dg_ref[...] = jnp.zeros_like(dg_ref)

        dg_ref[...] += jnp.sum(dy * xh, axis=0, keepdims=True)

    row = BS((tt, Dm), lambda i: (i, 0))
    vec = BS((1, Dm), lambda i: (0, 0))
    extra = [] if add is None else [add]
    return _pcall(body, name=name, grid=(T // tt,), in_specs=[row, vec, row] + [row] * len(extra), out_specs=[row, vec],
                  out_shape=[SDS((T, Dm), F32), SDS((1, Dm), F32)],
                  compiler_params=_cparams(("arbitrary",)))(x, g.reshape(1, Dm), dy, *extra)


@functools.partial(jax.custom_vjp, nondiff_argnums=(2,))
def rmsnorm(x, g, name):
    return _rms_fwd_call(x, g, name)


def _rmsnorm_fwd(x, g, name):
    return _rms_fwd_call(x, g, name), (x, g)


def _rmsnorm_bwd(name, res, dy):
    x, g = res
    dx, dg = _rms_bwd_call(x, g, dy, name + "_bwd")
    return dx, dg.reshape(g.shape)


rmsnorm.defvjp(_rmsnorm_fwd, _rmsnorm_bwd)


@functools.partial(jax.custom_vjp, nondiff_argnums=(5,))
def norm_linear(x, g, wb, wc, bias, name):
    return _mm(_rms_fwd_call(x, g, name + "_norm", BF16), wb, bias=bias, name=name, tm=1024, tn=1408, tk=1024), x


def _norm_linear_fwd(x, g, wb, wc, bias, name):
    hn = _rms_fwd_call(x, g, name + "_norm", BF16)
    return (_mm(hn, wb, bias=bias, name=name, tm=1024, tn=1408, tk=1024), x), (x, g, hn, wb, bias is not None)


def _norm_linear_bwd(name, res, cts):
    x, g, hn, wb, has_bias = res
    dy, dres = cts
    dw = _mm(hn, dy, ta=True, name=name + "_dw", tm=1024, tn=1408, tk=1024)
    dhn = _mm(dy, wb, tb=True, name=name + "_da", tm=1024, tn=1024, tk=1408)
    dx, dg = _rms_bwd_call(x, g, dhn, name + "_norm_bwd", add=dres)
    dbias = _colsum(dy, name + "_db").reshape(-1) if has_bias else None
    return dx, dg.reshape(g.shape), jnp.zeros_like(wb), dw, dbias


norm_linear.defvjp(_norm_linear_fwd, _norm_linear_bwd)


def _rope_tables(T):
    half = ROT_DIM // 2
    inv = ROPE_THETA ** (-jnp.arange(0, ROT_DIM, 2, dtype=F32) / ROT_DIM)
    ang = jnp.arange(T, dtype=F32)[:, None] * inv[None, :]
    cos, sin = jnp.cos(ang), jnp.sin(ang)
    ones = jnp.ones((T, HEAD_DIM - ROT_DIM), F32)
    zeros = jnp.zeros((T, HEAD_DIM - ROT_DIM), F32)
    zh = jnp.zeros((T, half), F32)
    c = jnp.concatenate([cos, cos, ones], axis=1)
    sa = jnp.concatenate([zh, sin, zeros], axis=1)
    sb = jnp.concatenate([-sin, zh, zeros], axis=1)
    two = lambda t: jnp.concatenate([t, t], axis=1)
    return two(c), two(sa), two(sb)


def _rope_call(x, tabs, inverse, name):
    T, W = x.shape
    tt = _tile(T, 512, 8)
    reps = W // LANES
    half = ROT_DIM // 2

    def body(x_ref, c_ref, sa_ref, sb_ref, o_ref):
        xv = x_ref[...]
        c = jnp.tile(c_ref[...], (1, reps))
        sa = jnp.tile(sa_ref[...], (1, reps))
        sb = jnp.tile(sb_ref[...], (1, reps))
        if not inverse:
            o_ref[...] = xv * c + pltpu.roll(xv, half, axis=1) * sa + pltpu.roll(xv, W - half, axis=1) * sb
        else:
            o_ref[...] = xv * c + pltpu.roll(xv * sa, W - half, axis=1) + pltpu.roll(xv * sb, half, axis=1)

    row = BS((tt, W), lambda i: (i, 0))
    tab = BS((tt, LANES), lambda i: (i, 0))
    return _pcall(body, name=name, grid=(T // tt,), in_specs=[row, tab, tab, tab], out_specs=row,
                  out_shape=SDS((T, W), F32), compiler_params=_cparams(("parallel",)))(x, *tabs)


@functools.partial(jax.custom_vjp, nondiff_argnums=(2,))
def rope(x, tabs, name):
    return _rope_call(x, tabs, False, name)


def _rope_fwd(x, tabs, name):
    return _rope_call(x, tabs, False, name), tabs


def _rope_bwd(name, tabs, dy):
    return _rope_call(dy, tabs, True, name + "_bwd"), jax.tree.map(jnp.zeros_like, tabs)


rope.defvjp(_rope_fwd, _rope_bwd)


def _conv_fwd_call(x, w, b, name):
    T, C = x.shape
    tt = _tile(T, 512, 8)
    per = tt // 8

    def body(x_ref, halo_ref, w_ref, b_ref, o_ref):
        i = pl.program_id(0)
        halo = jnp.where(i > 0, halo_ref[...], 0.0)
        e = jnp.concatenate([halo, x_ref[...]], axis=0)
        acc = b_ref[...]
        for k in (3, 2, 1):
            acc = acc + pltpu.roll(e, k, axis=0)[8:, :] * w_ref[3 - k:4 - k, :]
        o_ref[...] = acc + x_ref[...] * w_ref[3:4, :]

    return _pcall(body, name=name, grid=(T // tt,),
                  in_specs=[BS((tt, C), lambda i: (i, 0)), BS((8, C), lambda i: (jnp.maximum(i * per - 1, 0), 0)),
                            BS((4, C), lambda i: (0, 0)), BS((1, C), lambda i: (0, 0))],
                  out_specs=BS((tt, C), lambda i: (i, 0)), out_shape=SDS((T, C), F32),
                  compiler_params=_cparams(("parallel",)))(x, x, w, b.reshape(1, C))


def _conv_bwd_call(x, w, dy, name):
    T, C = x.shape
    tt = _tile(T, 512, 8)
    per = tt // 8
    nt = T // tt

    def body(x_ref, halo_ref, w_ref, dy_ref, nxt_ref, dx_ref, dwb_ref):
        i = pl.program_id(0)
        halo = jnp.where(i > 0, halo_ref[...], 0.0)
        e = jnp.concatenate([halo, x_ref[...]], axis=0)
        dy = dy_ref[...]
        nxt = jnp.where(i < nt - 1, nxt_ref[...], 0.0)
        f = jnp.concatenate([dy, nxt], axis=0)
        dx = dy * w_ref[3:4, :]
        rows = [None] * 4
        rows[3] = jnp.sum(dy * x_ref[...], axis=0, keepdims=True)
        for k in (1, 2, 3):
            dx = dx + pltpu.roll(f, tt + 8 - k, axis=0)[:tt, :] * w_ref[3 - k:4 - k, :]
            rows[3 - k] = jnp.sum(dy * pltpu.roll(e, k, axis=0)[8:, :], axis=0, keepdims=True)
        dx_ref[...] = dx
        upd = jnp.concatenate(rows + [jnp.sum(dy, axis=0, keepdims=True), jnp.zeros((3, C), F32)], axis=0)

        @pl.when(i == 0)
        def _():
            dwb_ref[...] = jnp.zeros_like(dwb_ref)

        dwb_ref[...] += upd

    row = BS((tt, C), lambda i: (i, 0))
    return _pcall(body, name=name, grid=(nt,),
                  in_specs=[row, BS((8, C), lambda i: (jnp.maximum(i * per - 1, 0), 0)), BS((4, C), lambda i: (0, 0)),
                            row, BS((8, C), lambda i: (jnp.minimum((i + 1) * per, T // 8 - 1), 0))],
                  out_specs=[row, BS((8, C), lambda i: (0, 0))],
                  out_shape=[SDS((T, C), F32), SDS((8, C), F32)],
                  compiler_params=_cparams(("arbitrary",)))(x, x, w, dy, dy)


@functools.partial(jax.custom_vjp, nondiff_argnums=(3,))
def conv4(x, w, b, name):
    return _conv_fwd_call(x, w, b, name)


def _conv4_fwd(x, w, b, name):
    return _conv_fwd_call(x, w, b, name), (x, w)


def _conv4_bwd(name, res, dy):
    x, w = res
    dx, dwb = _conv_bwd_call(x, w, dy, name + "_bwd")
    return dx, dwb[0:4], dwb[4]


conv4.defvjp(_conv4_fwd, _conv4_bwd)


def _gates_fwd_call(xc, wa, ba, wx, bx, name):
    T, C = xc.shape
    hd = C // LRU_HEADS
    tt = _tile(T, 512, 8)

    def body(x_ref, wa_ref, ba_ref, wx_ref, bx_ref, ga_ref, gx_ref):
        xb = x_ref[...].astype(BF16)
        ga_ref[...] = jnp.dot(xb, wa_ref[0].astype(BF16), preferred_element_type=F32) + ba_ref[...]
        gx_ref[...] = jnp.dot(xb, wx_ref[0].astype(BF16), preferred_element_type=F32) + bx_ref[...]

    blk = BS((tt, hd), lambda i, h: (i, h))
    wsp = BS((1, hd, hd), lambda i, h: (h, 0, 0))
    bsp = BS((1, hd), lambda i, h: (0, h))
    return _pcall(body, name=name, grid=(T // tt, LRU_HEADS), in_specs=[blk, wsp, bsp, wsp, bsp],
                  out_specs=[blk, blk], out_shape=[SDS((T, C), F32)] * 2,
                  compiler_params=_cparams(("parallel", "parallel")))(xc, wa, ba.reshape(1, C), wx, bx.reshape(1, C))


def _gates_bwd_x_call(dga, dgx, wa, wx, name):
    T, C = dga.shape
    hd = C // LRU_HEADS
    tt = _tile(T, 512, 8)
    dn = (((1,), (1,)), ((), ()))

    def body(da_ref, dx_ref, wa_ref, wx_ref, o_ref):
        o_ref[...] = (lax.dot_general(da_ref[...].astype(BF16), wa_ref[0].astype(BF16), dn, preferred_element_type=F32)
                      + lax.dot_general(dx_ref[...].astype(BF16), wx_ref[0].astype(BF16), dn, preferred_element_type=F32))

    blk = BS((tt, hd), lambda i, h: (i, h))
    wsp = BS((1, hd, hd), lambda i, h: (h, 0, 0))
    return _pcall(body, name=name, grid=(T // tt, LRU_HEADS), in_specs=[blk, blk, wsp, wsp], out_specs=blk,
                  out_shape=SDS((T, C), F32), compiler_params=_cparams(("parallel", "parallel")))(dga, dgx, wa, wx)


def _gates_bwd_w_call(xc, dga, dgx, name):
    T, C = xc.shape
    hd = C // LRU_HEADS
    tt = _tile(T, 512, 8)
    dn = (((0,), (0,)), ((), ()))

    def body(x_ref, da_ref, dx_ref, dwa_ref, dwx_ref, dba_ref, dbx_ref):
        @pl.when(pl.program_id(1) == 0)
        def _():
            dwa_ref[...] = jnp.zeros_like(dwa_ref)
            dwx_ref[...] = jnp.zeros_like(dwx_ref)
            dba_ref[...] = jnp.zeros_like(dba_ref)
            dbx_ref[...] = jnp.zeros_like(dbx_ref)

        xb = x_ref[...].astype(BF16)
        da, dx = da_ref[...], dx_ref[...]
        dwa_ref[0] += lax.dot_general(xb, da.astype(BF16), dn, preferred_element_type=F32)
        dwx_ref[0] += lax.dot_general(xb, dx.astype(BF16), dn, preferred_element_type=F32)
        dba_ref[...] += jnp.sum(da, axis=0, keepdims=True)
        dbx_ref[...] += jnp.sum(dx, axis=0, keepdims=True)

    blk = BS((tt, hd), lambda h, i: (i, h))
    wsp = BS((1, hd, hd), lambda h, i: (h, 0, 0))
    bsp = BS((1, hd), lambda h, i: (0, h))
    return _pcall(body, name=name, grid=(LRU_HEADS, T // tt), in_specs=[blk, blk, blk],
                  out_specs=[wsp, wsp, bsp, bsp],
                  out_shape=[SDS((LRU_HEADS, hd, hd), F32)] * 2 + [SDS((1, C), F32)] * 2,
                  compiler_params=_cparams(("parallel", "arbitrary")))(xc, dga, dgx)


@functools.partial(jax.custom_vjp, nondiff_argnums=(7,))
def lru_gates(xc, wa, wa_c, ba, wx, wx_c, bx, name):
    return tuple(_gates_fwd_call(xc, wa, ba, wx, bx, name))


def _lru_gates_fwd(xc, wa, wa_c, ba, wx, wx_c, bx, name):
    return tuple(_gates_fwd_call(xc, wa, ba, wx, bx, name)), (xc, wa, wx, ba.shape)


def _lru_gates_bwd(name, res, g):
    xc, wa, wx, bshape = res
    dga, dgx = g
    dxc = _gates_bwd_x_call(dga, dgx, wa, wx, name + "_dx")
    dwa, dwx, dba, dbx = _gates_bwd_w_call(xc, dga, dgx, name + "_dw")
    return dxc, jnp.zeros_like(wa), dwa, dba.reshape(bshape), jnp.zeros_like(wx), dwx, dbx.reshape(bshape)


lru_gates.defvjp(_lru_gates_fwd, _lru_gates_bwd)


def _lru_coeffs(xc, ga, gx, lam):
    r = _sigmoid(ga)
    ig = _sigmoid(gx)
    z = -lam
    sp = jnp.maximum(z, 0.0) + jnp.log(1.0 + jnp.exp(-jnp.abs(z)))
    la = -LRU_C * r * sp
    a = jnp.exp(la)
    s = jnp.sqrt(-_expm1(2.0 * la))
    return r, ig, sp, a, s


LRU_TT = 256


def _scan_fwd_call(xc, ga, gx, y, lam, name):
    T, C = xc.shape
    tt = _tile(T, LRU_TT, 8)

    def body(xc_ref, ga_ref, gx_ref, y_ref, lam_ref, h_ref, rec_ref, a_buf, carry):
        @pl.when(pl.program_id(0) == 0)
        def _():
            carry[...] = jnp.zeros_like(carry)

        xcv = xc_ref[...]
        _, ig, _, a, s = _lru_coeffs(xcv, ga_ref[...], gx_ref[...], lam_ref[...])
        a_buf[...] = a
        h_ref[...] = s * (ig * xcv)

        def step(t, h):
            hn = a_buf[pl.ds(t, 1), :] * h + h_ref[pl.ds(t, 1), :]
            h_ref[pl.ds(t, 1), :] = hn
            return hn

        carry[0:1, :] = lax.fori_loop(0, tt, step, carry[0:1, :], unroll=8)
        g, _ = _gelu_parts(y_ref[...])
        rec_ref[...] = h_ref[...] * g

    row = BS((tt, C), lambda i: (i, 0))
    vec = BS((1, C), lambda i: (0, 0))
    return _pcall(body, name=name, grid=(T // tt,), in_specs=[row, row, row, row, vec], out_specs=[row, row],
                  out_shape=[SDS((T, C), F32)] * 2,
                  scratch_shapes=[pltpu.VMEM((tt, C), F32), pltpu.VMEM((8, C), F32)],
                  compiler_params=_cparams(("arbitrary",)))(xc, ga, gx, y, lam.reshape(1, C))


def _scan_bwd_call(xc, ga, gx, y, lam, h, drec, name):
    T, C = xc.shape
    tt = _tile(T, LRU_TT, 8)
    nt = T // tt
    per = tt // 8

    def body(xc_ref, ga_ref, gx_ref, y_ref, lam_ref, h_ref, halo_ref, dr_ref,
             dga_ref, dgx_ref, dxc_ref, dy_ref, dlam_ref, a_buf, g_buf, carry):
        i = pl.program_id(0)

        @pl.when(i == 0)
        def _():
            carry[...] = jnp.zeros_like(carry)
            dlam_ref[...] = jnp.zeros_like(dlam_ref)

        xcv, lam = xc_ref[...], lam_ref[...]
        r, ig, sp, a, s = _lru_coeffs(xcv, ga_ref[...], gx_ref[...], lam)
        gel, dgel = _gelu_parts(y_ref[...])
        drec = dr_ref[...]
        hv = h_ref[...]
        dy_ref[...] = drec * hv * dgel
        a_buf[...] = a
        g_buf[...] = drec * gel

        def step(j, q):
            t = tt - 1 - j
            g = g_buf[pl.ds(t, 1), :] + q
            g_buf[pl.ds(t, 1), :] = g
            return a_buf[pl.ds(t, 1), :] * g

        carry[0:1, :] = lax.fori_loop(0, tt, step, carry[0:1, :], unroll=8)
        g = g_buf[...]
        halo = jnp.where(i < nt - 1, halo_ref[...], 0.0)
        hprev = pltpu.roll(jnp.concatenate([halo, hv], axis=0), 1, axis=0)[8:, :]
        da = g * hprev
        dig = g * s * xcv
        ds = g * ig * xcv
        dla = da * a - ds * (a * a) / s
        dga_ref[...] = dla * (-LRU_C * sp) * r * (1.0 - r)
        dgx_ref[...] = dig * ig * (1.0 - ig)
        dxc_ref[...] = g * s * ig
        dlam_ref[...] += jnp.sum(dla * r, axis=0, keepdims=True) * (LRU_C * _sigmoid(-lam))

    row = BS((tt, C), lambda i: (nt - 1 - i, 0))
    vec = BS((1, C), lambda i: (0, 0))
    halo = BS((8, C), lambda i: (jnp.maximum((nt - 1 - i) * per - 1, 0), 0))
    return _pcall(body, name=name, grid=(nt,), in_specs=[row, row, row, row, vec, row, halo, row],
                  out_specs=[row, row, row, row, vec], out_shape=[SDS((T, C), F32)] * 4 + [SDS((1, C), F32)],
                  scratch_shapes=[pltpu.VMEM((tt, C), F32), pltpu.VMEM((tt, C), F32), pltpu.VMEM((8, C), F32)],
                  compiler_params=_cparams(("arbitrary",)))(xc, ga, gx, y, lam.reshape(1, C), h, h, drec)


@functools.partial(jax.custom_vjp, nondiff_argnums=(5,))
def lru_scan(xc, ga, gx, y, lam, name):
    return _scan_fwd_call(xc, ga, gx, y, lam, name)[1]


def _lru_scan_fwd(xc, ga, gx, y, lam, name):
    h, rec = _scan_fwd_call(xc, ga, gx, y, lam, name)
    return rec, (xc, ga, gx, y, lam, h)


def _lru_scan_bwd(name, res, drec):
    xc, ga, gx, y, lam, h = res
    dga, dgx, dxc, dy, dlam = _scan_bwd_call(xc, ga, gx, y, lam, h, drec, name + "_bwd")
    return dxc, dga, dgx, dy, dlam.reshape(lam.shape)


lru_scan.defvjp(_lru_scan_fwd, _lru_scan_bwd)


ATT_CG = 4


def _att_masks(n, max_dist):
    qi = lax.broadcasted_iota(jnp.int32, (2 * ATT_BLOCK, 2 * ATT_BLOCK), 0) & (ATT_BLOCK - 1)
    kj = lax.broadcasted_iota(jnp.int32, (2 * ATT_BLOCK, 2 * ATT_BLOCK), 1)
    prev = (kj < ATT_BLOCK) & (kj >= qi + (ATT_BLOCK - max_dist)) & (n > 0)
    cur = (kj >= ATT_BLOCK) & (kj - ATT_BLOCK <= qi)
    return prev | cur


def _lo_lanes(rows):
    return lax.broadcasted_iota(jnp.int32, (rows, LANES), 1) < HEAD_DIM


def _lane_half(rows):
    return lax.broadcasted_iota(jnp.int32, (rows, LANES), 1) // HEAD_DIM


def _stack_heads(x2):
    lo = _lo_lanes(ATT_BLOCK)
    zero = jnp.zeros_like(x2)
    return jnp.concatenate([jnp.where(lo, x2, zero), jnp.where(lo, zero, x2)], axis=0)


def _unstack_heads(y):
    return jnp.where(_lo_lanes(ATT_BLOCK), y[:ATT_BLOCK], y[ATT_BLOCK:])


def _per_head_col(x2):
    return jnp.concatenate([x2[:, 0:1], x2[:, HEAD_DIM:HEAD_DIM + 1]], axis=0)


def _head_sums(x2):
    lo = _lo_lanes(ATT_BLOCK)
    return jnp.concatenate([jnp.sum(jnp.where(lo, x2, 0.0), axis=1, keepdims=True),
                            jnp.sum(jnp.where(lo, 0.0, x2), axis=1, keepdims=True)], axis=0)


def _kv_pair(ref, p, shared, g):
    if not shared:
        return ref[:, p * LANES:(p + 1) * LANES]
    x = ref[...]
    return jnp.where(_lane_half(ATT_BLOCK) == g, x, pltpu.roll(x, HEAD_DIM, axis=1))


def _att_specs(d, Wq, Wk, nb, clamp):
    shared = Wk != Wq
    cur = (lambda n: jnp.minimum(n, nb - 1)) if clamp else (lambda n: n)
    qw, kw = ATT_CG * LANES, (LANES if shared else ATT_CG * LANES)
    kcol = (lambda g: 0) if shared else (lambda g: g)
    qsp = BS((ATT_BLOCK, qw), lambda g, n: (cur(n), g))
    csp = BS((ATT_BLOCK, kw), lambda g, n: (cur(n), kcol(g)))
    psp = BS((ATT_BLOCK, kw), lambda g, n: (jnp.maximum(cur(n) - 1, 0), kcol(g)))
    return shared, qsp, csp, psp, qw, kw


def _att_scores(q2, kband, n, max_dist):
    qs = _stack_heads(q2.astype(BF16))
    s = lax.dot_general(qs, kband, (((1,), (1,)), ((), ())), preferred_element_type=F32) * (HEAD_DIM ** -0.5)
    return qs, jnp.where(_att_masks(n, max_dist), s, NEG)


def _att_fwd_call(q, k, v, sinks, d, max_dist, name):
    T, Wq = q.shape
    Wk = k.shape[1]
    Ld = T // d
    nb = Ld // ATT_BLOCK
    G = d * Wq // (ATT_CG * LANES)
    shared, qsp, csp, psp, qw, kw = _att_specs(d, Wq, Wk, nb, False)
    assert not shared or (Wk == LANES and G == 2), "a shared kv pair serves two groups of 2 * ATT_CG query heads"

    def body(q_ref, kc_ref, kp_ref, vc_ref, vp_ref, sk_ref, o_ref, lse_ref):
        g, n = pl.program_id(0), pl.program_id(1)
        for p in range(ATT_CG):
            cols = slice(p * LANES, (p + 1) * LANES)
            kband = jnp.concatenate([_kv_pair(kp_ref, p, shared, g), _kv_pair(kc_ref, p, shared, g)], axis=0).astype(BF16)
            vband = jnp.concatenate([_kv_pair(vp_ref, p, shared, g), _kv_pair(vc_ref, p, shared, g)], axis=0).astype(BF16)
            _, s = _att_scores(q_ref[:, cols], kband, n, max_dist)
            sk = _per_head_col(jnp.broadcast_to(sk_ref[:, cols], (ATT_BLOCK, LANES)))
            m = jnp.maximum(jnp.max(s, axis=-1, keepdims=True), sk)
            e = jnp.exp(s - m)
            den = jnp.sum(e, axis=-1, keepdims=True) + jnp.exp(sk - m)
            o = jnp.dot((e * (1.0 / den)).astype(BF16), vband, preferred_element_type=F32)
            o_ref[:, cols] = _unstack_heads(o)
            lse_ref[:, cols] = _unstack_heads(jnp.broadcast_to(m + jnp.log(den), (2 * ATT_BLOCK, LANES)))

    sksp = BS((1, qw), lambda g, n: (0, g % (Wq // qw)))
    qv, kv_, vv = q.reshape(Ld, d * Wq), k.reshape(Ld, d * Wk), v.reshape(Ld, d * Wk)
    o, lse = _pcall(body, name=name, grid=(G, nb), in_specs=[qsp, csp, psp, csp, psp, sksp], out_specs=[qsp, qsp],
                    out_shape=[SDS((Ld, d * Wq), F32)] * 2,
                    compiler_params=_cparams(("parallel", "parallel")))(qv, kv_, kv_, vv, vv, sinks)
    return o.reshape(T, Wq), lse.reshape(T, Wq)


def _att_bwd_call(q, k, v, sinks, o, lse, do, dlse, d, max_dist, name):
    T, Wq = q.shape
    Wk = k.shape[1]
    Ld = T // d
    nb = Ld // ATT_BLOCK
    G = d * Wq // (ATT_CG * LANES)
    shared, qsp, csp, psp, qw, kw = _att_specs(d, Wq, Wk, nb, True)
    scale = HEAD_DIM ** -0.5
    dn_tn = (((0,), (0,)), ((), ()))

    def body(q_ref, kc_ref, kp_ref, vc_ref, vp_ref, sk_ref, o_ref, lse_ref, do_ref, dlse_ref,
             dq_ref, dk_ref, dv_ref, dsk_ref, ck, cv):
        g, n = pl.program_id(0), pl.program_id(1)

        @pl.when(n == 0)
        def _():
            ck[...] = jnp.zeros_like(ck)
            cv[...] = jnp.zeros_like(cv)
            dsk_ref[...] = jnp.zeros_like(dsk_ref)

        @pl.when(n < nb)
        def _():
            dk_prev, dk_cur, dv_prev, dv_cur = [], [], [], []
            for p in range(ATT_CG):
                cols = slice(p * LANES, (p + 1) * LANES)
                kband = jnp.concatenate([_kv_pair(kp_ref, p, shared, g), _kv_pair(kc_ref, p, shared, g)], axis=0).astype(BF16)
                vband = jnp.concatenate([_kv_pair(vp_ref, p, shared, g), _kv_pair(vc_ref, p, shared, g)], axis=0).astype(BF16)
                qs, s = _att_scores(q_ref[:, cols], kband, n, max_dist)
                lse_c = _per_head_col(lse_ref[:, cols])
                pr = jnp.exp(s - lse_c)
                do2 = do_ref[:, cols]
                dos = _stack_heads(do2.astype(BF16))
                corr = _head_sums(dlse_ref[:, cols]) - _head_sums(do2 * o_ref[:, cols])
                dp = lax.dot_general(dos, vband, (((1,), (1,)), ((), ())), preferred_element_type=F32)
                ds = (pr * (dp + corr)).astype(BF16)
                dq_ref[:, cols] = _unstack_heads(jnp.dot(ds, kband, preferred_element_type=F32)) * scale
                dkb = lax.dot_general(ds, qs, dn_tn, preferred_element_type=F32) * scale
                dvb = lax.dot_general(pr.astype(BF16), dos, dn_tn, preferred_element_type=F32)
                dk_prev.append(dkb[:ATT_BLOCK])
                dk_cur.append(dkb[ATT_BLOCK:])
                dv_prev.append(dvb[:ATT_BLOCK])
                dv_cur.append(dvb[ATT_BLOCK:])
                sk = _per_head_col(jnp.broadcast_to(sk_ref[:, cols], (ATT_BLOCK, LANES)))
                dsk = jnp.exp(sk - lse_c) * corr
                dsk_lo = jnp.sum(dsk[:ATT_BLOCK], axis=0, keepdims=True)
                dsk_hi = jnp.sum(dsk[ATT_BLOCK:], axis=0, keepdims=True)
                lane = lax.broadcasted_iota(jnp.int32, (8, LANES), 1)
                dsk_ref[:, cols] += jnp.where(lane == 0, dsk_lo, jnp.where(lane == HEAD_DIM, dsk_hi, 0.0))

            def gather_pairs(parts):
                if not shared:
                    return jnp.concatenate(parts, axis=1)
                tot = parts[0]
                for t in parts[1:]:
                    tot = tot + t
                tot = tot + pltpu.roll(tot, HEAD_DIM, axis=1)
                return jnp.where(_lane_half(ATT_BLOCK) == g, tot, 0.0)

            dk_ref[...] = (ck[...] + gather_pairs(dk_prev)).reshape(dk_ref.shape)
            dv_ref[...] = (cv[...] + gather_pairs(dv_prev)).reshape(dv_ref.shape)
            ck[...] = gather_pairs(dk_cur)
            cv[...] = gather_pairs(dv_cur)

        @pl.when(n == nb)
        def _():
            dk_ref[...] = ck[...].reshape(dk_ref.shape)
            dv_ref[...] = cv[...].reshape(dv_ref.shape)

    sksp = BS((1, qw), lambda g, n: (0, g % (Wq // qw)))
    if shared:
        osp = BS((1, ATT_BLOCK, kw), lambda g, n: (g, jnp.maximum(n - 1, 0), 0))
        kshape = SDS((G, Ld, Wk), F32)
    else:
        osp = BS((ATT_BLOCK, kw), lambda g, n: (jnp.maximum(n - 1, 0), g))
        kshape = SDS((Ld, d * Wk), F32)
    qv, kv_, vv = q.reshape(Ld, d * Wq), k.reshape(Ld, d * Wk), v.reshape(Ld, d * Wk)
    rs = lambda t: t.reshape(Ld, d * Wq)
    dq, dk, dv, dsk = _pcall(
        body, name=name, grid=(G, nb + 1),
        in_specs=[qsp, csp, psp, csp, psp, sksp, qsp, qsp, qsp, qsp],
        out_specs=[qsp, osp, osp, BS((8, qw), lambda g, n: (0, g))],
        out_shape=[SDS((Ld, d * Wq), F32), kshape, kshape, SDS((8, d * Wq), F32)],
        scratch_shapes=[pltpu.VMEM((ATT_BLOCK, kw), F32)] * 2,
        compiler_params=_cparams(("parallel", "arbitrary")))(qv, kv_, kv_, vv, vv, sinks, rs(o), rs(lse), rs(do), rs(dlse))
    if shared:
        dk, dv = jnp.sum(dk, axis=0), jnp.sum(dv, axis=0)
    dsk = jnp.sum(dsk[0:1].reshape(d, Wq), axis=0, keepdims=True)
    return dq.reshape(T, Wq), dk.reshape(T, Wk), dv.reshape(T, Wk), dsk


@functools.partial(jax.custom_vjp, nondiff_argnums=(4, 5, 6))
def band_attention(q, k, v, sinks, d, max_dist, name):
    return _att_fwd_call(q, k, v, sinks, d, max_dist, name)


def _band_attention_fwd(q, k, v, sinks, d, max_dist, name):
    o, lse = _att_fwd_call(q, k, v, sinks, d, max_dist, name)
    return (o, lse), (q, k, v, sinks, o, lse)


def _band_attention_bwd(d, max_dist, name, res, g):
    q, k, v, sinks, o, lse = res
    do, dlse = g
    return _att_bwd_call(q, k, v, sinks, o, lse, do, dlse, d, max_dist, name + "_bwd")


band_attention.defvjp(_band_attention_fwd, _band_attention_bwd)


def _merge_weights(ls):
    mx = jnp.maximum(jnp.maximum(ls[0], ls[1]), ls[2])
    es = [jnp.exp(l - mx) for l in ls]
    inv = 1.0 / (es[0] + es[1] + es[2])
    return [e * inv for e in es]


def _merge_fwd_call(os_, ls_, name):
    T, W = os_[0].shape
    tt = _tile(T, 512, 8)

    def body(o1, o2, o3, l1, l2, l3, out_ref):
        w = _merge_weights([l1[...], l2[...], l3[...]])
        out_ref[...] = w[0] * o1[...] + w[1] * o2[...] + w[2] * o3[...]

    row = BS((tt, W), lambda i: (i, 0))
    return _pcall(body, name=name, grid=(T // tt,), in_specs=[row] * 6, out_specs=row,
                  out_shape=SDS((T, W), F32), compiler_params=_cparams(("parallel",)))(*os_, *ls_)


def _merge_bwd_call(os_, ls_, do, name):
    T, W = os_[0].shape
    tt = _tile(T, 512, 8)

    def body(o1, o2, o3, l1, l2, l3, do_ref, d1, d2, d3, e1, e2, e3):
        w = _merge_weights([l1[...], l2[...], l3[...]])
        dov = do_ref[...]
        ts = [dov * o[...] for o in (o1, o2, o3)]
        mean = w[0] * ts[0] + w[1] * ts[1] + w[2] * ts[2]
        for wi, ti, dref, eref in zip(w, ts, (d1, d2, d3), (e1, e2, e3)):
            dref[...] = wi * dov
            eref[...] = wi * (ti - mean)

    row = BS((tt, W), lambda i: (i, 0))
    return _pcall(body, name=name, grid=(T // tt,), in_specs=[row] * 7, out_specs=[row] * 6,
                  out_shape=[SDS((T, W), F32)] * 6, compiler_params=_cparams(("parallel",)))(*os_, *ls_, do)


@functools.partial(jax.custom_vjp, nondiff_argnums=(2,))
def merge3(os_, ls_, name):
    return _merge_fwd_call(os_, ls_, name)


def _merge3_fwd(os_, ls_, name):
    return _merge_fwd_call(os_, ls_, name), (os_, ls_)


def _merge3_bwd(name, res, do):
    os_, ls_ = res
    out = _merge_bwd_call(os_, ls_, do, name + "_bwd")
    return tuple(out[:3]), tuple(out[3:])


merge3.defvjp(_merge3_fwd, _merge3_bwd)


def _xa_probs(qb, kb, scale):
    s = lax.dot_general(qb, kb, (((1,), (1,)), ((), ())), preferred_element_type=F32) * scale
    e = jnp.exp(s - jnp.max(s, axis=-1, keepdims=True))
    return e / jnp.sum(e, axis=-1, keepdims=True)


def _xa_fwd_call(q, kv, name):
    T, W = q.shape
    M = kv.shape[0]
    hd = XA_HEAD_DIM
    tq = _tile(T, 512, 8)
    scale = hd ** -0.5

    def body(q_ref, k_ref, v_ref, o_ref):
        p = _xa_probs(q_ref[...].astype(BF16), k_ref[...].astype(BF16), scale)
        o_ref[...] = jnp.dot(p.astype(BF16), v_ref[...].astype(BF16), preferred_element_type=F32)

    qsp = BS((tq, hd), lambda i, h: (i, h))
    return _pcall(body, name=name, grid=(T // tq, XA_HEADS),
                  in_specs=[qsp, BS((M, hd), lambda i, h: (0, h)), BS((M, hd), lambda i, h: (0, XA_HEADS + h))],
                  out_specs=qsp, out_shape=SDS((T, W), F32),
                  compiler_params=_cparams(("parallel", "parallel")))(q, kv, kv)


def _xa_bwd_call(q, kv, do, name):
    T, W = q.shape
    M = kv.shape[0]
    hd = XA_HEAD_DIM
    tq = _tile(T, 512, 8)
    scale = hd ** -0.5
    dn_nt = (((1,), (1,)), ((), ()))
    dn_tn = (((0,), (0,)), ((), ()))

    def body(q_ref, k_ref, v_ref, do_ref, dq_ref, dk_ref, dv_ref):
        @pl.when(pl.program_id(1) == 0)
        def _():
            dk_ref[...] = jnp.zeros_like(dk_ref)
            dv_ref[...] = jnp.zeros_like(dv_ref)

        qb, kb, vb = q_ref[...].astype(BF16), k_ref[...].astype(BF16), v_ref[...].astype(BF16)
        p = _xa_probs(qb, kb, scale)
        dob = do_ref[...].astype(BF16)
        dp = lax.dot_general(dob, vb, dn_nt, preferred_element_type=F32)
        ds = (p * (dp - jnp.sum(p * dp, axis=-1, keepdims=True))).astype(BF16)
        dq_ref[...] = jnp.dot(ds, kb, preferred_element_type=F32) * scale
        dk_ref[...] += lax.dot_general(ds, qb, dn_tn, preferred_element_type=F32) * scale
        dv_ref[...] += lax.dot_general(p.astype(BF16), dob, dn_tn, preferred_element_type=F32)

    qsp = BS((tq, hd), lambda h, i: (i, h))
    ksp = BS((M, hd), lambda h, i: (0, h))
    return _pcall(body, name=name, grid=(XA_HEADS, T // tq),
                  in_specs=[qsp, ksp, BS((M, hd), lambda h, i: (0, XA_HEADS + h)), qsp],
                  out_specs=[qsp, ksp, ksp], out_shape=[SDS((T, W), F32), SDS((M, W), F32), SDS((M, W), F32)],
                  compiler_params=_cparams(("parallel", "arbitrary")))(q, kv, kv, do)


@functools.partial(jax.custom_vjp, nondiff_argnums=(2,))
def cross_attention(q, kv, name):
    return _xa_fwd_call(q, kv, name)


def _cross_attention_fwd(q, kv, name):
    return _xa_fwd_call(q, kv, name), (q, kv)


def _cross_attention_bwd(name, res, do):
    q, kv = res
    dq, dk, dv = _xa_bwd_call(q, kv, do, name + "_bwd")
    return dq, jnp.concatenate([dk, dv], axis=1)


cross_attention.defvjp(_cross_attention_fwd, _cross_attention_bwd)


def _swiglu_fwd_call(gu, name):
    T, W2 = gu.shape
    F = W2 // 2
    tt = _tile(T, 256, 16)

    def body(g_ref, u_ref, o_ref):
        g = g_ref[...]
        o_ref[...] = ((g * _sigmoid(g)) * u_ref[...]).astype(BF16)

    return _pcall(body, name=name, grid=(T // tt,),
                  in_specs=[BS((tt, F), lambda i: (i, 0)), BS((tt, F), lambda i: (i, 1))],
                  out_specs=BS((tt, F), lambda i: (i, 0)), out_shape=SDS((T, F), BF16),
                  compiler_params=_cparams(("parallel",)))(gu, gu)


def _swiglu_bwd_call(gu, dact, name):
    T, W2 = gu.shape
    F = W2 // 2
    tt = _tile(T, 128, 16)

    def body(g_ref, u_ref, d_ref, o_ref):
        g, d = g_ref[...], d_ref[...]
        sg = _sigmoid(g)
        o_ref[:, :F] = (d * u_ref[...] * (sg * (1.0 + g * (1.0 - sg)))).astype(BF16)
        o_ref[:, F:] = (d * (g * sg)).astype(BF16)

    return _pcall(body, name=name, grid=(T // tt,),
                  in_specs=[BS((tt, F), lambda i: (i, 0)), BS((tt, F), lambda i: (i, 1)), BS((tt, F), lambda i: (i, 0))],
                  out_specs=BS((tt, W2), lambda i: (i, 0)), out_shape=SDS((T, W2), BF16),
                  compiler_params=_cparams(("parallel",)))(gu, gu, dact)


@functools.partial(jax.custom_vjp, nondiff_argnums=(6,))
def ffn_block(h, g, w1b, w1c, w2b, w2c, name):
    return _ffn_fwd(h, g, w1b, w1c, w2b, w2c, name)[0]


def _ffn_fwd(h, g, w1b, w1c, w2b, w2c, name):
    hn = _rms_fwd_call(h, g, name + "_norm", BF16)
    gu = _mm(hn, w1b, name=name + "_gu", tm=1024, tn=1408, tk=1024)
    act = _swiglu_fwd_call(gu, name + "_swiglu")
    out = _mm(act, w2b, add=h, name=name + "_down", tm=1024, tn=1024, tk=1408)
    return out, (h, g, hn, gu, act, w1b, w2b)


def _ffn_bwd(name, res, dout):
    h, g, hn, gu, act, w1b, w2b = res
    dact = _mm(dout, w2b, tb=True, name=name + "_down_da", tm=1024, tn=1408, tk=1024)
    dw2 = _mm(act, dout, ta=True, name=name + "_down_dw", tm=1408, tn=1024, tk=1024)
    dgu = _swiglu_bwd_call(gu, dact, name + "_swiglu_bwd")
    dhn = _mm(dgu, w1b, tb=True, name=name + "_gu_da", tm=1024, tn=1024, tk=1408)
    dw1 = _mm(hn, dgu, ta=True, name=name + "_gu_dw", tm=1024, tn=1408, tk=1024)
    dh, dg = _rms_bwd_call(h, g, dhn, name + "_norm_bwd", add=dout)
    return dh, dg.reshape(g.shape), jnp.zeros_like(w1b), dw1, jnp.zeros_like(w2b), dw2


ffn_block.defvjp(_ffn_fwd, _ffn_bwd)


def _final_call(h, g, target, name):
    T, Dm = h.shape
    tt = _tile(T, 512, 8)

    def body(x_ref, g_ref, t_ref, loss_ref, dx_ref, dg_ref):
        @pl.when(pl.program_id(0) == 0)
        def _():
            loss_ref[...] = jnp.zeros_like(loss_ref)
            dg_ref[...] = jnp.zeros_like(dg_ref)

        xv, gv = x_ref[...], g_ref[...]
        r = lax.rsqrt(jnp.mean(xv * xv, axis=-1, keepdims=True) + NORM_EPS)
        xh = xv * r
        err = xh * gv - t_ref[...]
        loss_ref[...] += 0.5 * jnp.sum(jnp.mean(err * err, axis=-1, keepdims=True), axis=0, keepdims=True)
        dy = err * (1.0 / Dm)
        dyg = dy * gv
        dx_ref[...] = r * (dyg - xh * jnp.mean(dyg * xh, axis=-1, keepdims=True))
        dg_ref[...] += jnp.sum(dy * xh, axis=0, keepdims=True)

    row = BS((tt, Dm), lambda i: (i, 0))
    vec = BS((1, Dm), lambda i: (0, 0))
    return _pcall(body, name=name, grid=(T // tt,), in_specs=[row, vec, row],
                  out_specs=[BS((1, 1), lambda i: (0, 0)), row, vec],
                  out_shape=[SDS((1, 1), F32), SDS((T, Dm), F32), SDS((1, Dm), F32)],
                  compiler_params=_cparams(("arbitrary",)))(h, g.reshape(1, Dm), target)


ADAMW_BLOCK_ELEMS = 64 * 1024


def _adamw_call(parts, w, m, v, name):
    shape = w.shape
    if not isinstance(parts, (list, tuple)):
        parts, shape3 = [parts], (1,) + shape
    else:
        shape3 = shape
    n_lead = shape3[0]
    r, N = shape3[-2], shape3[-1]
    Ld = math.prod(shape3[1:-2])
    w, m, v = (t.reshape(n_lead * Ld, r, N) for t in (w, m, v))
    tr = _tile(r, max(8, ADAMW_BLOCK_ELEMS // N), 8)
    c1 = 1.0 - ADAM_B1 ** ADAM_STEP
    c2 = 1.0 - ADAM_B2 ** ADAM_STEP
    outs = None
    for lead, p in enumerate(parts):
        def body(p_ref, w_ref, m_ref, v_ref, *rest):
            g_ref, d_ref, nm_ref, nv_ref = rest[-4:]
            g = p_ref[0]
            for j in range(1, N_DEV):
                g = g + p_ref[j]
            nm = ADAM_B1 * m_ref[...] + (1.0 - ADAM_B1) * g
            nv = ADAM_B2 * v_ref[...] + (1.0 - ADAM_B2) * (g * g)
            g_ref[...] = g
            nm_ref[...] = nm
            nv_ref[...] = nv
            d_ref[...] = -ADAM_LR * ((nm / c1) / (jnp.sqrt(nv / c2) + ADAM_EPS) + ADAM_WD * w_ref[...])

        base = lead * Ld
        row = BS((1, tr, N), lambda l, i, base=base: (base + l, i, 0))
        prev = [] if outs is None else list(outs)
        outs = _pcall(body, name=f"{name}_{lead}", grid=(Ld, r // tr),
                      in_specs=[BS((N_DEV, 1, tr, N), lambda l, i: (0, l, i, 0)), row, row, row]
                      + [BS(memory_space=pl.ANY)] * len(prev),
                      out_specs=[row] * 4, out_shape=[SDS((n_lead * Ld, r, N), F32)] * 4,
                      input_output_aliases={4 + j: j for j in range(len(prev))},
                      compiler_params=_cparams(("parallel", "parallel")))(p.reshape(N_DEV, Ld, r, N), w, m, v, *prev)
    return [t.reshape(shape) for t in outs]


def _place():
    return lax.axis_index("x"), lax.axis_index("y"), lax.axis_index("c")


def _all_gather(xs, name):
    n = len(xs)
    pairs = [(i, l) for i, x in enumerate(xs) for l in range(x.shape[0])]

    def body(*refs):
        x_refs, o_refs = refs[:n], refs[n:2 * n]
        send_sems, recv_sems, local_sems = refs[2 * n:]
        x_, y_, c_ = _place()
        me, sibling = (x_, y_, c_), (x_, y_, 1 - c_)
        chips = [(1 - x_, y_), (x_, 1 - y_), (1 - x_, 1 - y_)]

        def copy(e, k, block, to, from_input=False):
            i, l = pairs[e]
            px, py, pc = block
            dst = o_refs[i].at[l, 4 * px + 2 * py + pc]
            return pltpu.make_async_remote_copy(
                src_ref=x_refs[i].at[l] if from_input else dst, dst_ref=dst,
                send_sem=send_sems.at[7 * e + k], recv_sem=recv_sems.at[7 * e + k],
                device_id=to, device_id_type=pl.DeviceIdType.MESH)

        every = range(len(pairs))
        mine = [pltpu.make_async_copy(x_refs[i].at[l], o_refs[i].at[l, 4 * x_ + 2 * y_ + c_], local_sems.at[e])
                for e, (i, l) in enumerate(pairs)]
        for cp in mine:
            cp.start()
        first = [copy(e, 0, me, sibling, True) for e in every]
        first += [copy(e, 1 + j, me, (*chip, c_), True) for j, chip in enumerate(chips) for e in every]
        for cp in first:
            cp.start()
        passed = []
        for j, chip in enumerate(chips):
            for e in every:
                copy(e, 1 + j, (*chip, c_), me).wait_recv()
            for e in every:
                cp = copy(e, 4 + j, (*chip, c_), sibling)
                cp.start()
                passed.append(cp)
        for e in every:
            copy(e, 0, sibling, me).wait_recv()
        for j, chip in enumerate(chips):
            for e in every:
                copy(e, 4 + j, (*chip, 1 - c_), me).wait_recv()
        for cp in first + passed:
            cp.wait_send()
        for cp in mine:
            cp.wait()

    any_spec = BS(memory_space=pl.ANY)
    return _pcall(body, name=name, in_specs=[any_spec] * n, out_specs=[any_spec] * n,
                  out_shape=[SDS((x.shape[0], N_DEV) + x.shape[1:], x.dtype) for x in xs],
                  scratch_shapes=[pltpu.SemaphoreType.DMA((7 * len(pairs),)), pltpu.SemaphoreType.DMA((7 * len(pairs),)),
                                  pltpu.SemaphoreType.DMA((len(pairs),))],
                  compiler_params=pltpu.CompilerParams(has_side_effects=True))(*xs)


def _peer_of(k, place):
    x_, y_, c_ = place
    fx, fy, fc = (k >> 2) & 1, (k >> 1) & 1, k & 1
    return fx + x_ - 2 * fx * x_, fy + y_ - 2 * fy * y_, fc + c_ - 2 * fc * c_


def _split_copy(src_ref, land_ref, send_sems, recv_sems, e, k, place, scatter):
    x_, y_, c_ = place
    px, py, pc = _peer_of(k, place)
    return pltpu.make_async_remote_copy(
        src_ref=src_ref.at[4 * px + 2 * py + pc] if scatter else src_ref, dst_ref=land_ref.at[4 * x_ + 2 * y_ + c_],
        send_sem=send_sems.at[7 * e + k - 1], recv_sem=recv_sems.at[7 * e + k - 1],
        device_id=(px, py, pc), device_id_type=pl.DeviceIdType.MESH)


_HBM_SPEC = BS(memory_space=pltpu.HBM)
_SEM_SPEC = BS(memory_space=pltpu.SEMAPHORE)
_EFFECT = pltpu.SideEffectType.DATAFLOW_SIDE_EFFECTING


def _copies_start(srcs, scatter, name, thru=None):
    n = len(srcs)
    lands = [lax.empty(s.shape if scatter else (N_DEV,) + s.shape, s.dtype) for s in srcs]
    passed = srcs + lands + ([] if thru is None else [thru])

    def body(*refs):
        src_refs, land_refs = refs[:n], refs[n:2 * n]
        send_sems, recv_sems = refs[len(passed)], refs[len(passed) + 1]
        token = refs[-1]
        place = _place()
        for e in range(n):
            for k in range(1, N_DEV):
                _split_copy(src_refs[e], land_refs[e], send_sems, recv_sems, e, k, place, scatter).start()
        token[...] = jnp.zeros_like(token)

    hbm = lambda t: pltpu.with_memory_space_constraint(t, pltpu.HBM)
    out = _pcall(body, name=name,
                 out_shape=(pltpu.SemaphoreType.DMA((7 * n,)), pltpu.SemaphoreType.DMA((7 * n,)),
                            *[pltpu.HBM(t.shape, t.dtype) for t in passed], SDS((8, LANES), F32)),
                 in_specs=[_HBM_SPEC] * len(passed),
                 out_specs=(_SEM_SPEC, _SEM_SPEC, *[_HBM_SPEC] * len(passed), BS(memory_space=pltpu.VMEM)),
                 input_output_aliases={i: 2 + i for i in range(len(passed))},
                 compiler_params=pltpu.CompilerParams(has_side_effects=_EFFECT))(*[hbm(t) for t in passed])
    return out[0], out[1], list(out[2:2 + n]), list(out[2 + n:2 + 2 * n]), (None if thru is None else out[2 + 2 * n])


def _copies_wait(started, which, scatter, after, name):
    send_sems, recv_sems, srcs, lands, _ = started
    n = len(which)

    def body(*refs):
        src_refs, land_refs = refs[:n], refs[n:2 * n]
        send_s, recv_s = refs[2 * n], refs[2 * n + 1]
        place = _place()
        for j, e in enumerate(which):
            for k in range(1, N_DEV):
                cp = _split_copy(src_refs[j], land_refs[j], send_s, recv_s, e, k, place, scatter)
                cp.wait_send()
                cp.wait_recv()

    args = [srcs[e] for e in which] + [lands[e] for e in which]
    out = _pcall(body, name=name, out_shape=tuple(pltpu.HBM(t.shape, t.dtype) for t in args),
                 in_specs=[_HBM_SPEC] * (2 * n) + [_SEM_SPEC, _SEM_SPEC, BS(memory_space=pl.ANY)],
                 out_specs=tuple([_HBM_SPEC] * (2 * n)), input_output_aliases={i: i for i in range(2 * n)},
                 compiler_params=pltpu.CompilerParams(has_side_effects=_EFFECT))(*args, send_sems, recv_sems, after)
    return list(out[:n]), list(out[n:])


def _with_own_block(land, own_block):
    me = 4 * lax.axis_index("x") + 2 * lax.axis_index("y") + lax.axis_index("c")
    return lax.dynamic_update_index_in_dim(land, own_block, me, 0)


def _pad_flat(t, quantum=PACK_QUANTUM):
    f = t.reshape(-1)
    pad = (-f.shape[0]) % quantum
    return jnp.pad(f, (0, pad)) if pad else f


def _pack(arrs, dtype):
    return jnp.concatenate([_pad_flat(a.astype(dtype)) for a in arrs]).reshape(-1, LANES)


def _unpack(buf, shapes, lead=()):
    flat = buf.reshape(lead + (-1,))
    out, off = [], 0
    for s in shapes:
        n = math.prod(s)
        out.append(flat[..., off:off + n].reshape(lead + tuple(s)))
        off += n + (-n) % PACK_QUANTUM
    return out


def _full_from_gathered(g, axis):
    t = jnp.moveaxis(g, 0, axis)
    s = t.shape
    return t.reshape(s[:axis] + (s[axis] * s[axis + 1],) + s[axis + 2:])


def _parts_from_full(t, axis):
    s = t.shape
    t = t.reshape(s[:axis] + (N_DEV, s[axis] // N_DEV) + s[axis + 1:])
    return jnp.moveaxis(t, axis, 0)


def _head_rows(t):
    return jnp.repeat(t, HEAD_DIM).reshape(1, -1)


def _dilated_attention(q, k, v, name):
    no_sink = jnp.full((1, q.shape[1]), NEG, F32)
    outs, lses = zip(*[band_attention(q, k, v, no_sink, d, ATT_BLOCK, f"{name}_d{d}") for d in DILATIONS])
    return merge3(tuple(outs), tuple(lses), name + "_merge")


STAGES = (
    ("proj0", ('mix_norm',), (('ab_w_in', 0),)),
    ("mixer0", ('lru_conv_w', 'lru_conv_b', 'lru_ba', 'lru_bx', 'lru_lambda'),
     (('lru_wa', 0), ('lru_wx', 0), ('ab_w_out', 0))),
    ("xa0", ('xa_norm', 'xa_mem_norm'), (('xa_wq', 0), ('xa_wkv', 0), ('xa_wo', 0))),
    ("ffn0", ('ffn_norm',), (('ffn_w_gate_up', 0), ('ffn_w_down', 0))),
    ("mixer1", ('mix_norm', 'c_b_qkv', 'c_sinks', 'c_b_out'), (('c_w_qkv', 0), ('c_w_out', 0))),
    ("xa1", ('xa_norm', 'xa_mem_norm'), (('xa_wq', 1), ('xa_wkv', 1), ('xa_wo', 1))),
    ("ffn1", ('ffn_norm',), (('ffn_w_gate_up', 1), ('ffn_w_down', 1))),
)


def _stage_fn(stage, Wb, tabs, mem):
    layer = int(stage[-1])
    L = f"l{layer}"

    def run(S, Cw, h):
        def lin(a, key, bias, add, name, rows=None):
            wb, wc = Wb[key], Cw[key]
            if rows is not None:
                wb, wc = wb[rows], wc[rows]
            return linear(a, wb, wc, bias, add, name)

        def norm_lin(a, gain, key, bias, name):
            return norm_linear(a, gain, Wb[key], Cw[key], bias, name)

        if stage == "proj0":
            proj, h = norm_lin(h, S['mix_norm'][0], ('ab_w_in', 0), None, L + "_w_in")
            return h, proj
        if stage == "mixer0":
            h, proj = h
            C = S['lru_conv_w'].shape[-1]
            xc = conv4(proj[:, :C], S['lru_conv_w'][0], S['lru_conv_b'][0], L + "_conv")
            ga, gx = lru_gates(xc, Wb['lru_wa', 0], Cw['lru_wa', 0], S['lru_ba'][0],
                               Wb['lru_wx', 0], Cw['lru_wx', 0], S['lru_bx'][0], L + "_gates")
            rec = lru_scan(xc, ga, gx, proj[:, C:2 * C], S['lru_lambda'][0], L + "_scan")
            bw = B_HEADS * HEAD_DIM
            q = rope(proj[:, 2 * C:2 * C + bw], tabs, L + "_rope_q")
            k = rope(proj[:, 2 * C + bw:2 * C + 2 * bw], tabs, L + "_rope_k")
            v = proj[:, 2 * C + 2 * bw:]
            att = _dilated_attention(q, k, v, L + "_att")
            h = lin(att, ('ab_w_out', 0), None, h, L + "_w_out_att", slice(C, None))
            return lin(rec, ('ab_w_out', 0), None, h, L + "_w_out_rec", slice(0, C))
        if stage == "mixer1":
            qw = C_HEADS * HEAD_DIM
            kw = C_KV_HEADS * HEAD_DIM
            qkv, h = norm_lin(h, S['mix_norm'][1], ('c_w_qkv', 0), S['c_b_qkv'][0], L + "_w_qkv")
            q = rope(qkv[:, :qw], tabs, L + "_rope_q")
            k = rope(qkv[:, qw:qw + kw], tabs, L + "_rope_k")
            v = qkv[:, qw + kw:]
            o, _ = band_attention(q, k, v, _head_rows(S['c_sinks'][0]), 1, ATT_BLOCK - 1, L + "_att")
            return lin(o, ('c_w_out', 0), S['c_b_out'][0], h, L + "_w_out")
        if stage.startswith("xa"):
            xq, h = norm_lin(h, S['xa_norm'][layer], ('xa_wq', layer), None, L + "_xa_wq")
            xkv, _ = norm_lin(mem, S['xa_mem_norm'][layer], ('xa_wkv', layer), None, L + "_xa_wkv")
            return lin(cross_attention(xq, xkv, L + "_xa"), ('xa_wo', layer), None, h, L + "_xa_wo")
        gu, down = ('ffn_w_gate_up', layer), ('ffn_w_down', layer)
        return ffn_block(h, S['ffn_norm'][layer], Wb[gu], Cw[gu], Wb[down], Cw[down], L + "_ffn")

    return run


def kernel(x, mem, mix_norm, ab_w_in, lru_conv_w, lru_conv_b, lru_wa, lru_ba, lru_wx, lru_bx, lru_lambda, ab_w_out, c_w_qkv, c_b_qkv, c_sinks, c_w_out, c_b_out, xa_norm, xa_mem_norm, xa_wq, xa_wkv, xa_wo, ffn_norm, ffn_w_gate_up, ffn_w_down, final_norm, loss_target, m_mix_norm, m_ab_w_in, m_lru_conv_w, m_lru_conv_b, m_lru_wa, m_lru_ba, m_lru_wx, m_lru_bx, m_lru_lambda, m_ab_w_out, m_c_w_qkv, m_c_b_qkv, m_c_sinks, m_c_w_out, m_c_b_out, m_xa_norm, m_xa_mem_norm, m_xa_wq, m_xa_wkv, m_xa_wo, m_ffn_norm, m_ffn_w_gate_up, m_ffn_w_down, m_final_norm, v_mix_norm, v_ab_w_in, v_lru_conv_w, v_lru_conv_b, v_lru_wa, v_lru_ba, v_lru_wx, v_lru_bx, v_lru_lambda, v_ab_w_out, v_c_w_qkv, v_c_b_qkv, v_c_sinks, v_c_w_out, v_c_b_out, v_xa_norm, v_xa_mem_norm, v_xa_wq, v_xa_wkv, v_xa_wo, v_ffn_norm, v_ffn_w_gate_up, v_ffn_w_down, v_final_norm):
    w_loc = dict(zip(WEIGHT_NAMES, (mix_norm, ab_w_in, lru_conv_w, lru_conv_b, lru_wa, lru_ba, lru_wx, lru_bx, lru_lambda, ab_w_out, c_w_qkv, c_b_qkv, c_sinks, c_w_out, c_b_out, xa_norm, xa_mem_norm, xa_wq, xa_wkv, xa_wo, ffn_norm, ffn_w_gate_up, ffn_w_down, final_norm)))
    m_loc = dict(zip(WEIGHT_NAMES, (m_mix_norm, m_ab_w_in, m_lru_conv_w, m_lru_conv_b, m_lru_wa, m_lru_ba, m_lru_wx, m_lru_bx, m_lru_lambda, m_ab_w_out, m_c_w_qkv, m_c_b_qkv, m_c_sinks, m_c_w_out, m_c_b_out, m_xa_norm, m_xa_mem_norm, m_xa_wq, m_xa_wkv, m_xa_wo, m_ffn_norm, m_ffn_w_gate_up, m_ffn_w_down, m_final_norm)))
    v_loc = dict(zip(WEIGHT_NAMES, (v_mix_norm, v_ab_w_in, v_lru_conv_w, v_lru_conv_b, v_lru_wa, v_lru_ba, v_lru_wx, v_lru_bx, v_lru_lambda, v_ab_w_out, v_c_w_qkv, v_c_b_qkv, v_c_sinks, v_c_w_out, v_c_b_out, v_xa_norm, v_xa_mem_norm, v_xa_wq, v_xa_wkv, v_xa_wo, v_ffn_norm, v_ffn_w_gate_up, v_ffn_w_down, v_final_norm)))

    me = 4 * lax.axis_index("x") + 2 * lax.axis_index("y") + lax.axis_index("c")

    keys = [key for _, _, stage_keys in STAGES for key in stage_keys]
    shards = [w_loc[n][l].astype(BF16) for n, l in keys]
    small_g = _all_gather([_pack([w_loc[n] for n in SMALL], F32)[None]], "gather_small")[0][0]
    gather = _copies_start(shards, False, "gather_start", thru=small_g)
    small_g = gather[4]
    S = {n: w_loc[n] for n in REPLICATED}
    for n, t in zip(SMALL, _unpack(small_g, [w_loc[n].shape for n in SMALL], lead=(N_DEV,))):
        S[n] = _full_from_gathered(t, SHARD_AXIS[n])

    tabs = _rope_tables(x.shape[1])
    Wb, vjps = {}, []
    h = x[0]
    for stage, small_names, stage_keys in STAGES:
        which = [keys.index(key) for key in stage_keys]
        _, lands = _copies_wait(gather, which, False, jax.tree.leaves(h)[-1], "gather_wait_" + stage)
        for key, e, land in zip(stage_keys, which, lands):
            Wb[key] = _full_from_gathered(_with_own_block(land, shards[e]), SHARD_AXIS[key[0]] - 1)
        carriers = {key: jnp.zeros(Wb[key].shape, F32) for key in stage_keys}
        h, vjp_fn = jax.vjp(_stage_fn(stage, Wb, tabs, mem[0]), {n: S[n] for n in small_names}, carriers, h)
        vjps.append(vjp_fn)
    loss_part, dh, dg_final = _final_call(h, S['final_norm'], loss_target[0], "final_loss")

    grads = {'final_norm': dg_final.reshape(final_norm.shape)}
    exchanges, send_keys, send_parts = [], [], []
    for (stage, small_names, stage_keys), vjp_fn in zip(reversed(STAGES), reversed(vjps)):
        g_small, g_big, dh = vjp_fn(dh)
        for n in small_names:
            grads[n] = grads[n] + g_small[n] if n in grads else g_small[n]
        send_keys += list(stage_keys)
        send_parts += [_parts_from_full(g_big[key], SHARD_AXIS[key[0]] - 1) for key in stage_keys]
        if stage == "xa1":
            continue
        if stage == "mixer0":
            small_parts = [_parts_from_full(grads[n], SHARD_AXIS[n]) for n in SMALL]
            send_keys.append("small")
            send_parts.append(jnp.stack([_pack([p[j] for p in small_parts], F32) for j in range(N_DEV)]))
        leaves, tree = jax.tree.flatten(dh)
        started = _copies_start(send_parts, True, "grad_start_" + stage, thru=leaves[0])
        dh = jax.tree.unflatten(tree, [started[4]] + leaves[1:])
        exchanges.append((stage, started, send_keys))
        send_keys, send_parts = [], []
    dx = dh

    parts = {}
    for stage, started, ex_keys in exchanges:
        srcs, lands = _copies_wait(started, list(range(len(ex_keys))), True, dx, "grad_wait_" + stage)
        for key, src, land in zip(ex_keys, srcs, lands):
            parts[key] = _with_own_block(land, lax.dynamic_index_in_dim(src, me, 0, keepdims=False))
    out = {}

    def adamw(p, names, call_name):
        if len(names) == 1:
            res = _adamw_call(p, w_loc[names[0]], m_loc[names[0]], v_loc[names[0]], call_name)
            for kind, t in zip(("grad", "delta", "new_m", "new_v"), res):
                out[kind, names[0]] = t
        else:
            res = _adamw_call(p, *[_pack([d[n] for n in names], F32) for d in (w_loc, m_loc, v_loc)], call_name)
            for kind, buf in zip(("grad", "delta", "new_m", "new_v"), res):
                for n, t in zip(names, _unpack(buf, [w_loc[n].shape for n in names])):
                    out[kind, n] = t

    for n in BIG:
        adamw([parts[n, l] for l in range(w_loc[n].shape[0])], [n], "adamw_" + n)
    adamw(parts["small"], SMALL, "adamw_small")
    rep_names = REPLICATED + ["loss"]
    grads["loss"] = loss_part
    zero = jnp.zeros((1, 1), F32)
    for d in (w_loc, m_loc, v_loc):
        d["loss"] = zero
    rep_g = _all_gather([_pack([grads[n] for n in rep_names], F32)[None]], "gather_rep_grads")[0][0]
    adamw(rep_g, rep_names, "adamw_replicated")
    loss = out["grad", "loss"][0, 0]

    return (loss, dx[None], *[out[kind, n] for kind in ("grad", "delta", "new_m", "new_v") for n in WEIGHT_NAMES])
```

```python
import functools
import math

import jax
import jax.numpy as jnp
from jax import lax
from jax.experimental import pallas as pl
from jax.experimental.pallas import tpu as pltpu

F32 = jnp.float32
BF16 = jnp.bfloat16
SDS = jax.ShapeDtypeStruct
BS = pl.BlockSpec

N_DEV = 8
NORM_EPS = 1e-6
ROPE_THETA = 500000.0
HEAD_DIM = 64
ROT_DIM = 16
ATT_BLOCK = 128
LRU_C = 8.0
LRU_HEADS = 4
DILATIONS = (1, 4, 16)
B_HEADS = 8
C_HEADS = 16
C_KV_HEADS = 2
XA_HEADS = 4
XA_HEAD_DIM = 128
NEG = -1e30
ADAM_LR, ADAM_B1, ADAM_B2, ADAM_EPS, ADAM_WD, ADAM_STEP = 0.001, 0.9, 0.999, 1e-08, 0.01, 10
LANES = 128
VMEM_LIMIT = 48 * 1024 * 1024

WEIGHT_NAMES = ['mix_norm', 'ab_w_in', 'lru_conv_w', 'lru_conv_b', 'lru_wa', 'lru_ba', 'lru_wx', 'lru_bx',
                'lru_lambda', 'ab_w_out', 'c_w_qkv', 'c_b_qkv', 'c_sinks', 'c_w_out', 'c_b_out', 'xa_norm',
                'xa_mem_norm', 'xa_wq', 'xa_wkv', 'xa_wo', 'ffn_norm', 'ffn_w_gate_up', 'ffn_w_down', 'final_norm']
SHARD_AXIS = {'ab_w_in': 2, 'lru_conv_w': 2, 'lru_wa': 2, 'lru_ba': 2, 'lru_wx': 2, 'lru_bx': 2, 'ab_w_out': 1,
              'c_w_qkv': 2, 'c_b_qkv': 1, 'c_w_out': 1, 'c_b_out': 1, 'xa_wq': 1, 'xa_wkv': 1, 'xa_wo': 2,
              'ffn_w_gate_up': 2, 'ffn_w_down': 1}
BIG = ['ab_w_in', 'lru_wa', 'lru_wx', 'ab_w_out', 'c_w_qkv', 'c_w_out', 'xa_wq', 'xa_wkv', 'xa_wo',
       'ffn_w_gate_up', 'ffn_w_down']
SMALL = ['lru_conv_w', 'lru_ba', 'lru_bx', 'c_b_qkv', 'c_b_out']
SHARDED = BIG + SMALL
REPLICATED = [n for n in WEIGHT_NAMES if n not in SHARD_AXIS]
PACK_QUANTUM = 2048


def _pcall(body, **kw):
    return pl.pallas_call(body, **kw)


def _cparams(sem=None):
    return pltpu.CompilerParams(dimension_semantics=sem, vmem_limit_bytes=VMEM_LIMIT)


def _tile(n, target, mult=LANES):
    if n <= target:
        return n
    t = (target // mult) * mult
    while t >= mult:
        if n % t == 0:
            return t
        t -= mult
    return n


def _sigmoid(x):
    return 1.0 / (1.0 + jnp.exp(-x))


def _expm1(x):
    small = x * (1.0 + x * (0.5 + x * (1.0 / 6.0 + x * (1.0 / 24.0))))
    return jnp.where(jnp.abs(x) < 0.03, small, jnp.exp(x) - 1.0)


_GELU_C = math.sqrt(2.0 / math.pi)


def _gelu_parts(y):
    y2 = y * y
    th = jnp.tanh(_GELU_C * (y + 0.044715 * y * y2))
    g = 0.5 * y * (1.0 + th)
    dg = 0.5 * (1.0 + th) + 0.5 * y * (1.0 - th * th) * _GELU_C * (1.0 + 3.0 * 0.044715 * y2)
    return g, dg


def _mm(a, b, *, ta=False, tb=False, bias=None, add=None, name, tm=512, tn=512, tk=2048):
    M, K = (a.shape[1], a.shape[0]) if ta else a.shape
    N = b.shape[0] if tb else b.shape[1]
    tm, tn, tk = _tile(M, tm), _tile(N, tn), _tile(K, tk)
    nk = K // tk
    dn = (((0 if ta else 1,), (1 if tb else 0,)), ((), ()))

    def body(*refs):
        a_ref, b_ref = refs[0], refs[1]
        pos = 2
        bias_ref = add_ref = None
        if bias is not None:
            bias_ref = refs[pos]
            pos += 1
        if add is not None:
            add_ref = refs[pos]
            pos += 1
        o_ref = refs[pos]
        part = lax.dot_general(a_ref[...].astype(BF16), b_ref[...].astype(BF16), dn, preferred_element_type=F32)

        def finish(r):
            if bias_ref is not None:
                r = r + bias_ref[...]
            if add_ref is not None:
                r = r + add_ref[...]
            o_ref[...] = r

        if nk == 1:
            finish(part)
            return
        acc_ref = refs[pos + 1]
        k = pl.program_id(2)

        @pl.when(k == 0)
        def _():
            acc_ref[...] = part

        @pl.when((k > 0) & (k < nk - 1))
        def _():
            acc_ref[...] += part

        @pl.when(k == nk - 1)
        def _():
            finish(acc_ref[...] + part)

    in_specs = [BS((tk, tm), lambda i, j, k: (k, i)) if ta else BS((tm, tk), lambda i, j, k: (i, k)),
                BS((tn, tk), lambda i, j, k: (j, k)) if tb else BS((tk, tn), lambda i, j, k: (k, j))]
    args = [a, b]
    if bias is not None:
        in_specs.append(BS((1, tn), lambda i, j, k: (0, j)))
        args.append(bias.reshape(1, N))
    if add is not None:
        in_specs.append(BS((tm, tn), lambda i, j, k: (i, j)))
        args.append(add)
    return _pcall(body, name=name, grid=(M // tm, N // tn, nk), in_specs=in_specs,
                  out_specs=BS((tm, tn), lambda i, j, k: (i, j)), out_shape=SDS((M, N), F32),
                  scratch_shapes=[pltpu.VMEM((tm, tn), F32)] if nk > 1 else [],
                  compiler_params=_cparams(("parallel", "parallel", "arbitrary")))(*args)


def _colsum(x, name):
    T, N = x.shape
    tt = _tile(T, 512, 8)

    def body(x_ref, o_ref):
        @pl.when(pl.program_id(0) == 0)
        def _():
            o_ref[...] = jnp.zeros_like(o_ref)

        o_ref[...] += jnp.sum(x_ref[...], axis=0, keepdims=True)

    return _pcall(body, name=name, grid=(T // tt,), in_specs=[BS((tt, N), lambda i: (i, 0))],
                  out_specs=BS((1, N), lambda i: (0, 0)), out_shape=SDS((1, N), F32),
                  compiler_params=_cparams(("arbitrary",)))(x)


@functools.partial(jax.custom_vjp, nondiff_argnums=(5,))
def linear(a, wb, wc, bias, add, name):
    return _mm(a, wb, bias=bias, add=add, name=name)


def _linear_fwd(a, wb, wc, bias, add, name):
    return _mm(a, wb, bias=bias, add=add, name=name), (a, wb, bias is not None, add is not None)


def _linear_bwd(name, res, g):
    a, wb, has_bias, has_add = res
    da = _mm(g, wb, tb=True, name=name + "_da")
    dw = _mm(a, g, ta=True, name=name + "_dw")
    dbias = _colsum(g, name + "_db").reshape(-1) if has_bias else None
    return da, jnp.zeros_like(wb), dw, dbias, (g if has_add else None)


linear.defvjp(_linear_fwd, _linear_bwd)


def _rms_fwd_call(x, g, name, out_dtype=F32):
    T, Dm = x.shape
    tt = _tile(T, 512, 16)

    def body(x_ref, g_ref, o_ref):
        xv = x_ref[...]
        r = lax.rsqrt(jnp.mean(xv * xv, axis=-1, keepdims=True) + NORM_EPS)
        o_ref[...] = ((xv * r) * g_ref[...]).astype(out_dtype)

    return _pcall(body, name=name, grid=(T // tt,),
                  in_specs=[BS((tt, Dm), lambda i: (i, 0)), BS((1, Dm), lambda i: (0, 0))],
                  out_specs=BS((tt, Dm), lambda i: (i, 0)), out_shape=SDS((T, Dm), out_dtype),
                  compiler_params=_cparams(("parallel",)))(x, g.reshape(1, Dm))


def _rms_bwd_call(x, g, dy, name, add=None):
    T, Dm = x.shape
    tt = _tile(T, 512, 8)

    def body(*refs):
        x_ref, g_ref, dy_ref = refs[:3]
        dx_ref, dg_ref = refs[-2:]
        xv = x_ref[...]
        r = lax.rsqrt(jnp.mean(xv * xv, axis=-1, keepdims=True) + NORM_EPS)
        xh = xv * r
        dy = dy_ref[...]
        dyg = dy * g_ref[...]
        dx = r * (dyg - xh * jnp.mean(dyg * xh, axis=-1, keepdims=True))
        dx_ref[...] = dx if add is None else dx + refs[3][...]

        @pl.when(pl.program_id(0) == 0)
        def _():
            dg_ref[...] = jnp.zeros_like(dg_ref)

        dg_ref[...] += jnp.sum(dy * xh, axis=0, keepdims=True)

    row = BS((tt, Dm), lambda i: (i, 0))
    vec = BS((1, Dm), lambda i: (0, 0))
    extra = [] if add is None else [add]
    return _pcall(body, name=name, grid=(T // tt,), in_specs=[row, vec, row] + [row] * len(extra), out_specs=[row, vec],
                  out_shape=[SDS((T, Dm), F32), SDS((1, Dm), F32)],
                  compiler_params=_cparams(("arbitrary",)))(x, g.reshape(1, Dm), dy, *extra)


@functools.partial(jax.custom_vjp, nondiff_argnums=(2,))
def rmsnorm(x, g, name):
    return _rms_fwd_call(x, g, name)


def _rmsnorm_fwd(x, g, name):
    return _rms_fwd_call(x, g, name), (x, g)


def _rmsnorm_bwd(name, res, dy):
    x, g = res
    dx, dg = _rms_bwd_call(x, g, dy, name + "_bwd")
    return dx, dg.reshape(g.shape)


rmsnorm.defvjp(_rmsnorm_fwd, _rmsnorm_bwd)


@functools.partial(jax.custom_vjp, nondiff_argnums=(5,))
def norm_linear(x, g, wb, wc, bias, name):
    return _mm(_rms_fwd_call(x, g, name + "_norm", BF16), wb, bias=bias, name=name, tm=1024, tn=1408, tk=1024), x


def _norm_linear_fwd(x, g, wb, wc, bias, name):
    hn = _rms_fwd_call(x, g, name + "_norm", BF16)
    return (_mm(hn, wb, bias=bias, name=name, tm=1024, tn=1408, tk=1024), x), (x, g, hn, wb, bias is not None)


def _norm_linear_bwd(name, res, cts):
    x, g, hn, wb, has_bias = res
    dy, dres = cts
    dw = _mm(hn, dy, ta=True, name=name + "_dw", tm=1024, tn=1408, tk=1024)
    dhn = _mm(dy, wb, tb=True, name=name + "_da", tm=1024, tn=1024, tk=1408)
    dx, dg = _rms_bwd_call(x, g, dhn, name + "_norm_bwd", add=dres)
    dbias = _colsum(dy, name + "_db").reshape(-1) if has_bias else None
    return dx, dg.reshape(g.shape), jnp.zeros_like(wb), dw, dbias


norm_linear.defvjp(_norm_linear_fwd, _norm_linear_bwd)


def _rope_tables(T):
    half = ROT_DIM // 2
    inv = ROPE_THETA ** (-jnp.arange(0, ROT_DIM, 2, dtype=F32) / ROT_DIM)
    ang = jnp.arange(T, dtype=F32)[:, None] * inv[None, :]
    cos, sin = jnp.cos(ang), jnp.sin(ang)
    ones = jnp.ones((T, HEAD_DIM - ROT_DIM), F32)
    zeros = jnp.zeros((T, HEAD_DIM - ROT_DIM), F32)
    zh = jnp.zeros((T, half), F32)
    c = jnp.concatenate([cos, cos, ones], axis=1)
    sa = jnp.concatenate([zh, sin, zeros], axis=1)
    sb = jnp.concatenate([-sin, zh, zeros], axis=1)
    two = lambda t: jnp.concatenate([t, t], axis=1)
    return two(c), two(sa), two(sb)


def _rope_call(x, tabs, inverse, name):
    T, W = x.shape
    tt = _tile(T, 512, 8)
    reps = W // LANES
    half = ROT_DIM // 2

    def body(x_ref, c_ref, sa_ref, sb_ref, o_ref):
        xv = x_ref[...]
        c = jnp.tile(c_ref[...], (1, reps))
        sa = jnp.tile(sa_ref[...], (1, reps))
        sb = jnp.tile(sb_ref[...], (1, reps))
        if not inverse:
            o_ref[...] = xv * c + pltpu.roll(xv, half, axis=1) * sa + pltpu.roll(xv, W - half, axis=1) * sb
        else:
            o_ref[...] = xv * c + pltpu.roll(xv * sa, W - half, axis=1) + pltpu.roll(xv * sb, half, axis=1)

    row = BS((tt, W), lambda i: (i, 0))
    tab = BS((tt, LANES), lambda i: (i, 0))
    return _pcall(body, name=name, grid=(T // tt,), in_specs=[row, tab, tab, tab], out_specs=row,
                  out_shape=SDS((T, W), F32), compiler_params=_cparams(("parallel",)))(x, *tabs)


@functools.partial(jax.custom_vjp, nondiff_argnums=(2,))
def rope(x, tabs, name):
    return _rope_call(x, tabs, False, name)


def _rope_fwd(x, tabs, name):
    return _rope_call(x, tabs, False, name), tabs


def _rope_bwd(name, tabs, dy):
    return _rope_call(dy, tabs, True, name + "_bwd"), jax.tree.map(jnp.zeros_like, tabs)


rope.defvjp(_rope_fwd, _rope_bwd)


def _conv_fwd_call(x, w, b, name):
    T, C = x.shape
    tt = _tile(T, 512, 8)
    per = tt // 8

    def body(x_ref, halo_ref, w_ref, b_ref, o_ref):
        i = pl.program_id(0)
        halo = jnp.where(i > 0, halo_ref[...], 0.0)
        e = jnp.concatenate([halo, x_ref[...]], axis=0)
        acc = b_ref[...]
        for k in (3, 2, 1):
            acc = acc + pltpu.roll(e, k, axis=0)[8:, :] * w_ref[3 - k:4 - k, :]
        o_ref[...] = acc + x_ref[...] * w_ref[3:4, :]

    return _pcall(body, name=name, grid=(T // tt,),
                  in_specs=[BS((tt, C), lambda i: (i, 0)), BS((8, C), lambda i: (jnp.maximum(i * per - 1, 0), 0)),
                            BS((4, C), lambda i: (0, 0)), BS((1, C), lambda i: (0, 0))],
                  out_specs=BS((tt, C), lambda i: (i, 0)), out_shape=SDS((T, C), F32),
                  compiler_params=_cparams(("parallel",)))(x, x, w, b.reshape(1, C))


def _conv_bwd_call(x, w, dy, name):
    T, C = x.shape
    tt = _tile(T, 512, 8)
    per = tt // 8
    nt = T // tt

    def body(x_ref, halo_ref, w_ref, dy_ref, nxt_ref, dx_ref, dwb_ref):
        i = pl.program_id(0)
        halo = jnp.where(i > 0, halo_ref[...], 0.0)
        e = jnp.concatenate([halo, x_ref[...]], axis=0)
        dy = dy_ref[...]
        nxt = jnp.where(i < nt - 1, nxt_ref[...], 0.0)
        f = jnp.concatenate([dy, nxt], axis=0)
        dx = dy * w_ref[3:4, :]
        rows = [None] * 4
        rows[3] = jnp.sum(dy * x_ref[...], axis=0, keepdims=True)
        for k in (1, 2, 3):
            dx = dx + pltpu.roll(f, tt + 8 - k, axis=0)[:tt, :] * w_ref[3 - k:4 - k, :]
            rows[3 - k] = jnp.sum(dy * pltpu.roll(e, k, axis=0)[8:, :], axis=0, keepdims=True)
        dx_ref[...] = dx
        upd = jnp.concatenate(rows + [jnp.sum(dy, axis=0, keepdims=True), jnp.zeros((3, C), F32)], axis=0)

        @pl.when(i == 0)
        def _():
            dwb_ref[...] = jnp.zeros_like(dwb_ref)

        dwb_ref[...] += upd

    row = BS((tt, C), lambda i: (i, 0))
    return _pcall(body, name=name, grid=(nt,),
                  in_specs=[row, BS((8, C), lambda i: (jnp.maximum(i * per - 1, 0), 0)), BS((4, C), lambda i: (0, 0)),
                            row, BS((8, C), lambda i: (jnp.minimum((i + 1) * per, T // 8 - 1), 0))],
                  out_specs=[row, BS((8, C), lambda i: (0, 0))],
                  out_shape=[SDS((T, C), F32), SDS((8, C), F32)],
                  compiler_params=_cparams(("arbitrary",)))(x, x, w, dy, dy)


@functools.partial(jax.custom_vjp, nondiff_argnums=(3,))
def conv4(x, w, b, name):
    return _conv_fwd_call(x, w, b, name)


def _conv4_fwd(x, w, b, name):
    return _conv_fwd_call(x, w, b, name), (x, w)


def _conv4_bwd(name, res, dy):
    x, w = res
    dx, dwb = _conv_bwd_call(x, w, dy, name + "_bwd")
    return dx, dwb[0:4], dwb[4]


conv4.defvjp(_conv4_fwd, _conv4_bwd)


def _gates_fwd_call(xc, wa, ba, wx, bx, name):
    T, C = xc.shape
    hd = C // LRU_HEADS
    tt = _tile(T, 512, 8)

    def body(x_ref, wa_ref, ba_ref, wx_ref, bx_ref, ga_ref, gx_ref):
        xb = x_ref[...].astype(BF16)
        ga_ref[...] = jnp.dot(xb, wa_ref[0].astype(BF16), preferred_element_type=F32) + ba_ref[...]
        gx_ref[...] = jnp.dot(xb, wx_ref[0].astype(BF16), preferred_element_type=F32) + bx_ref[...]

    blk = BS((tt, hd), lambda i, h: (i, h))
    wsp = BS((1, hd, hd), lambda i, h: (h, 0, 0))
    bsp = BS((1, hd), lambda i, h: (0, h))
    return _pcall(body, name=name, grid=(T // tt, LRU_HEADS), in_specs=[blk, wsp, bsp, wsp, bsp],
                  out_specs=[blk, blk], out_shape=[SDS((T, C), F32)] * 2,
                  compiler_params=_cparams(("parallel", "parallel")))(xc, wa, ba.reshape(1, C), wx, bx.reshape(1, C))


def _gates_bwd_x_call(dga, dgx, wa, wx, name):
    T, C = dga.shape
    hd = C // LRU_HEADS
    tt = _tile(T, 512, 8)
    dn = (((1,), (1,)), ((), ()))

    def body(da_ref, dx_ref, wa_ref, wx_ref, o_ref):
        o_ref[...] = (lax.dot_general(da_ref[...].astype(BF16), wa_ref[0].astype(BF16), dn, preferred_element_type=F32)
                      + lax.dot_general(dx_ref[...].astype(BF16), wx_ref[0].astype(BF16), dn, preferred_element_type=F32))

    blk = BS((tt, hd), lambda i, h: (i, h))
    wsp = BS((1, hd, hd), lambda i, h: (h, 0, 0))
    return _pcall(body, name=name, grid=(T // tt, LRU_HEADS), in_specs=[blk, blk, wsp, wsp], out_specs=blk,
                  out_shape=SDS((T, C), F32), compiler_params=_cparams(("parallel", "parallel")))(dga, dgx, wa, wx)


def _gates_bwd_w_call(xc, dga, dgx, name):
    T, C = xc.shape
    hd = C // LRU_HEADS
    tt = _tile(T, 512, 8)
    dn = (((0,), (0,)), ((), ()))

    def body(x_ref, da_ref, dx_ref, dwa_ref, dwx_ref, dba_ref, dbx_ref):
        @pl.when(pl.program_id(1) == 0)
        def _():
            dwa_ref[...] = jnp.zeros_like(dwa_ref)
            dwx_ref[...] = jnp.zeros_like(dwx_ref)
            dba_ref[...] = jnp.zeros_like(dba_ref)
            dbx_ref[...] = jnp.zeros_like(dbx_ref)

        xb = x_ref[...].astype(BF16)
        da, dx = da_ref[...], dx_ref[...]
        dwa_ref[0] += lax.dot_general(xb, da.astype(BF16), dn, preferred_element_type=F32)
        dwx_ref[0] += lax.dot_general(xb, dx.astype(BF16), dn, preferred_element_type=F32)
        dba_ref[...] += jnp.sum(da, axis=0, keepdims=True)
        dbx_ref[...] += jnp.sum(dx, axis=0, keepdims=True)

    blk = BS((tt, hd), lambda h, i: (i, h))
    wsp = BS((1, hd, hd), lambda h, i: (h, 0, 0))
    bsp = BS((1, hd), lambda h, i: (0, h))
    return _pcall(body, name=name, grid=(LRU_HEADS, T // tt), in_specs=[blk, blk, blk],
                  out_specs=[wsp, wsp, bsp, bsp],
                  out_shape=[SDS((LRU_HEADS, hd, hd), F32)] * 2 + [SDS((1, C), F32)] * 2,
                  compiler_params=_cparams(("parallel", "arbitrary")))(xc, dga, dgx)


@functools.partial(jax.custom_vjp, nondiff_argnums=(7,))
def lru_gates(xc, wa, wa_c, ba, wx, wx_c, bx, name):
    return tuple(_gates_fwd_call(xc, wa, ba, wx, bx, name))


def _lru_gates_fwd(xc, wa, wa_c, ba, wx, wx_c, bx, name):
    return tuple(_gates_fwd_call(xc, wa, ba, wx, bx, name)), (xc, wa, wx, ba.shape)


def _lru_gates_bwd(name, res, g):
    xc, wa, wx, bshape = res
    dga, dgx = g
    dxc = _gates_bwd_x_call(dga, dgx, wa, wx, name + "_dx")
    dwa, dwx, dba, dbx = _gates_bwd_w_call(xc, dga, dgx, name + "_dw")
    return dxc, jnp.zeros_like(wa), dwa, dba.reshape(bshape), jnp.zeros_like(wx), dwx, dbx.reshape(bshape)


lru_gates.defvjp(_lru_gates_fwd, _lru_gates_bwd)


def _lru_coeffs(xc, ga, gx, lam):
    r = _sigmoid(ga)
    ig = _sigmoid(gx)
    z = -lam
    sp = jnp.maximum(z, 0.0) + jnp.log(1.0 + jnp.exp(-jnp.abs(z)))
    la = -LRU_C * r * sp
    a = jnp.exp(la)
    s = jnp.sqrt(-_expm1(2.0 * la))
    return r, ig, sp, a, s


LRU_TT = 256


def _scan_fwd_call(xc, ga, gx, y, lam, name):
    T, C = xc.shape
    tt = _tile(T, LRU_TT, 8)

    def body(xc_ref, ga_ref, gx_ref, y_ref, lam_ref, h_ref, rec_ref, a_buf, carry):
        @pl.when(pl.program_id(0) == 0)
        def _():
            carry[...] = jnp.zeros_like(carry)

        xcv = xc_ref[...]
        _, ig, _, a, s = _lru_coeffs(xcv, ga_ref[...], gx_ref[...], lam_ref[...])
        a_buf[...] = a
        h_ref[...] = s * (ig * xcv)

        def step(t, h):
            hn = a_buf[pl.ds(t, 1), :] * h + h_ref[pl.ds(t, 1), :]
            h_ref[pl.ds(t, 1), :] = hn
            return hn

        carry[0:1, :] = lax.fori_loop(0, tt, step, carry[0:1, :], unroll=8)
        g, _ = _gelu_parts(y_ref[...])
        rec_ref[...] = h_ref[...] * g

    row = BS((tt, C), lambda i: (i, 0))
    vec = BS((1, C), lambda i: (0, 0))
    return _pcall(body, name=name, grid=(T // tt,), in_specs=[row, row, row, row, vec], out_specs=[row, row],
                  out_shape=[SDS((T, C), F32)] * 2,
                  scratch_shapes=[pltpu.VMEM((tt, C), F32), pltpu.VMEM((8, C), F32)],
                  compiler_params=_cparams(("arbitrary",)))(xc, ga, gx, y, lam.reshape(1, C))


def _scan_bwd_call(xc, ga, gx, y, lam, h, drec, name):
    T, C = xc.shape
    tt = _tile(T, LRU_TT, 8)
    nt = T // tt
    per = tt // 8

    def body(xc_ref, ga_ref, gx_ref, y_ref, lam_ref, h_ref, halo_ref, dr_ref,
             dga_ref, dgx_ref, dxc_ref, dy_ref, dlam_ref, a_buf, g_buf, carry):
        i = pl.program_id(0)

        @pl.when(i == 0)
        def _():
            carry[...] = jnp.zeros_like(carry)
            dlam_ref[...] = jnp.zeros_like(dlam_ref)

        xcv, lam = xc_ref[...], lam_ref[...]
        r, ig, sp, a, s = _lru_coeffs(xcv, ga_ref[...], gx_ref[...], lam)
        gel, dgel = _gelu_parts(y_ref[...])
        drec = dr_ref[...]
        hv = h_ref[...]
        dy_ref[...] = drec * hv * dgel
        a_buf[...] = a
        g_buf[...] = drec * gel

        def step(j, q):
            t = tt - 1 - j
            g = g_buf[pl.ds(t, 1), :] + q
            g_buf[pl.ds(t, 1), :] = g
            return a_buf[pl.ds(t, 1), :] * g

        carry[0:1, :] = lax.fori_loop(0, tt, step, carry[0:1, :], unroll=8)
        g = g_buf[...]
        halo = jnp.where(i < nt - 1, halo_ref[...], 0.0)
        hprev = pltpu.roll(jnp.concatenate([halo, hv], axis=0), 1, axis=0)[8:, :]
        da = g * hprev
        dig = g * s * xcv
        ds = g * ig * xcv
        dla = da * a - ds * (a * a) / s
        dga_ref[...] = dla * (-LRU_C * sp) * r * (1.0 - r)
        dgx_ref[...] = dig * ig * (1.0 - ig)
        dxc_ref[...] = g * s * ig
        dlam_ref[...] += jnp.sum(dla * r, axis=0, keepdims=True) * (LRU_C * _sigmoid(-lam))

    row = BS((tt, C), lambda i: (nt - 1 - i, 0))
    vec = BS((1, C), lambda i: (0, 0))
    halo = BS((8, C), lambda i: (jnp.maximum((nt - 1 - i) * per - 1, 0), 0))
    return _pcall(body, name=name, grid=(nt,), in_specs=[row, row, row, row, vec, row, halo, row],
                  out_specs=[row, row, row, row, vec], out_shape=[SDS((T, C), F32)] * 4 + [SDS((1, C), F32)],
                  scratch_shapes=[pltpu.VMEM((tt, C), F32), pltpu.VMEM((tt, C), F32), pltpu.VMEM((8, C), F32)],
                  compiler_params=_cparams(("arbitrary",)))(xc, ga, gx, y, lam.reshape(1, C), h, h, drec)


@functools.partial(jax.custom_vjp, nondiff_argnums=(5,))
def lru_scan(xc, ga, gx, y, lam, name):
    return _scan_fwd_call(xc, ga, gx, y, lam, name)[1]


def _lru_scan_fwd(xc, ga, gx, y, lam, name):
    h, rec = _scan_fwd_call(xc, ga, gx, y, lam, name)
    return rec, (xc, ga, gx, y, lam, h)


def _lru_scan_bwd(name, res, drec):
    xc, ga, gx, y, lam, h = res
    dga, dgx, dxc, dy, dlam = _scan_bwd_call(xc, ga, gx, y, lam, h, drec, name + "_bwd")
    return dxc, dga, dgx, dy, dlam.reshape(lam.shape)


lru_scan.defvjp(_lru_scan_fwd, _lru_scan_bwd)


def _att_batch(d):
    return (4, 1) if d == 1 else (1, 4)


def _att_masks(n, max_dist):
    qi = lax.broadcasted_iota(jnp.int32, (1, 2 * ATT_BLOCK, 2 * ATT_BLOCK), 1) & (ATT_BLOCK - 1)
    kj = lax.broadcasted_iota(jnp.int32, (1, 2 * ATT_BLOCK, 2 * ATT_BLOCK), 2)
    prev = (kj < ATT_BLOCK) & (kj >= qi + (ATT_BLOCK - max_dist)) & (n > 0)
    cur = (kj >= ATT_BLOCK) & (kj - ATT_BLOCK <= qi)
    return prev | cur


def _lo_lanes(rows):
    return lax.broadcasted_iota(jnp.int32, (rows, LANES), 1) < HEAD_DIM


def _lane_half(rows):
    return lax.broadcasted_iota(jnp.int32, (rows, LANES), 1) // HEAD_DIM


def _stack_heads(x2):
    lo = _lo_lanes(ATT_BLOCK)
    zero = jnp.zeros_like(x2)
    return jnp.concatenate([jnp.where(lo, x2, zero), jnp.where(lo, zero, x2)], axis=0)


def _unstack_heads(y):
    return jnp.where(_lo_lanes(ATT_BLOCK), y[:ATT_BLOCK], y[ATT_BLOCK:])


def _per_head_col(x2):
    return jnp.concatenate([x2[:, 0:1], x2[:, HEAD_DIM:HEAD_DIM + 1]], axis=0)


def _head_sums(x2):
    lo = _lo_lanes(ATT_BLOCK)
    return jnp.concatenate([jnp.sum(jnp.where(lo, x2, 0.0), axis=1, keepdims=True),
                            jnp.sum(jnp.where(lo, 0.0, x2), axis=1, keepdims=True)], axis=0)


def _att_specs(d, Wq, Wk, nb, clamp):
    cgw, sb = _att_batch(d)
    shared = Wk != Wq
    cur = (lambda n: jnp.minimum(n, nb - 1)) if clamp else (lambda n: n)
    rows, qw, kw = ATT_BLOCK * d, cgw * LANES, (LANES if shared else cgw * LANES)
    kcol = (lambda g: 0) if shared else (lambda g: g)
    qsp = BS((rows, qw), lambda g, n: (cur(n), g))
    csp = BS((rows, kw), lambda g, n: (cur(n), kcol(g)))
    psp = BS((rows, kw), lambda g, n: (jnp.maximum(cur(n) - 1, 0), kcol(g)))
    return cgw, sb, shared, qsp, csp, psp, qw, kw


def _att_streams(d, sb, work):
    if d == 1:
        work([slice(None)])
        return

    def one(j, carry):
        work([pl.ds(j * sb + i, ATT_BLOCK, stride=d) for i in range(sb)])
        return carry

    lax.fori_loop(0, d // sb, one, 0)


def _att_problem_loads(rows, cgw, shared, g, q_ref, kc_ref, kp_ref, vc_ref, vp_ref, sk_ref):
    half = _lane_half(ATT_BLOCK)

    def kv(ref, r, p):
        x = ref[r, :]
        if not shared:
            return x[:, p * LANES:(p + 1) * LANES]
        return jnp.where(half == g, x, pltpu.roll(x, HEAD_DIM, axis=1))

    qs, kb, vb, sk = [], [], [], []
    for r in rows:
        qrow = q_ref[r, :]
        for p in range(cgw):
            cols = slice(p * LANES, (p + 1) * LANES)
            qs.append(_stack_heads(qrow[:, cols].astype(BF16)))
            kb.append(jnp.concatenate([kv(kp_ref, r, p), kv(kc_ref, r, p)], axis=0).astype(BF16))
            vb.append(jnp.concatenate([kv(vp_ref, r, p), kv(vc_ref, r, p)], axis=0).astype(BF16))
            sk.append(_per_head_col(jnp.broadcast_to(sk_ref[:, cols], (ATT_BLOCK, LANES))))
    return jnp.stack(qs), jnp.stack(kb), jnp.stack(vb), jnp.stack(sk)


_BDOT_NT = (((2,), (2,)), ((0,), (0,)))
_BDOT_NN = (((2,), (1,)), ((0,), (0,)))
_BDOT_TN = (((1,), (1,)), ((0,), (0,)))


def _att_fwd_call(q, k, v, sinks, d, max_dist, name):
    T, Wq = q.shape
    Wk = k.shape[1]
    nb = T // (d * ATT_BLOCK)
    cgw, sb, shared, qsp, csp, psp, qw, kw = _att_specs(d, Wq, Wk, nb, False)
    G = Wq // qw
    assert not shared or (d == 1 and Wk == LANES and G == 2), "a shared kv pair serves two groups of 8 query heads"

    def body(q_ref, kc_ref, kp_ref, vc_ref, vp_ref, sk_ref, o_ref, lse_ref):
        g, n = pl.program_id(0), pl.program_id(1)

        def work(rows):
            qs, kband, vband, sk = _att_problem_loads(rows, cgw, shared, g, q_ref, kc_ref, kp_ref, vc_ref, vp_ref, sk_ref)
            s = lax.dot_general(qs, kband, _BDOT_NT, preferred_element_type=F32) * (HEAD_DIM ** -0.5)
            s = jnp.where(_att_masks(n, max_dist), s, NEG)
            m = jnp.maximum(jnp.max(s, axis=-1, keepdims=True), sk)
            e = jnp.exp(s - m)
            den = jnp.sum(e, axis=-1, keepdims=True) + jnp.exp(sk - m)
            o = lax.dot_general((e * (1.0 / den)).astype(BF16), vband, _BDOT_NN, preferred_element_type=F32)
            lse = jnp.broadcast_to(m + jnp.log(den), o.shape)
            for i, r in enumerate(rows):
                o_ref[r, :] = jnp.concatenate([_unstack_heads(o[i * cgw + p]) for p in range(cgw)], axis=1)
                lse_ref[r, :] = jnp.concatenate([_unstack_heads(lse[i * cgw + p]) for p in range(cgw)], axis=1)

        _att_streams(d, sb, work)

    sksp = BS((1, qw), lambda g, n: (0, g))
    return _pcall(body, name=name, grid=(G, nb), in_specs=[qsp, csp, psp, csp, psp, sksp], out_specs=[qsp, qsp],
                  out_shape=[SDS((T, Wq), F32)] * 2,
                  compiler_params=_cparams(("parallel", "parallel")))(q, k, k, v, v, sinks)


def _att_bwd_call(q, k, v, sinks, o, lse, do, dlse, d, max_dist, name):
    T, Wq = q.shape
    Wk = k.shape[1]
    nb = T // (d * ATT_BLOCK)
    cgw, sb, shared, qsp, csp, psp, qw, kw = _att_specs(d, Wq, Wk, nb, True)
    G = Wq // qw
    scale = HEAD_DIM ** -0.5

    def body(q_ref, kc_ref, kp_ref, vc_ref, vp_ref, sk_ref, o_ref, lse_ref, do_ref, dlse_ref,
             dq_ref, dk_ref, dv_ref, dsk_ref, ck, cv):
        g, n = pl.program_id(0), pl.program_id(1)

        @pl.when(n == 0)
        def _():
            ck[...] = jnp.zeros_like(ck)
            cv[...] = jnp.zeros_like(cv)
            dsk_ref[...] = jnp.zeros_like(dsk_ref)

        def put(ref, r, val):
            if len(ref.shape) == 3:
                ref[0, r, :] = val
            else:
                ref[r, :] = val

        def work(rows):
            qs, kband, vband, sk = _att_problem_loads(rows, cgw, shared, g, q_ref, kc_ref, kp_ref, vc_ref, vp_ref, sk_ref)
            dos, lse_c, corr = [], [], []
            for r in rows:
                do_r, o_r, lse_r, dlse_r = do_ref[r, :], o_ref[r, :], lse_ref[r, :], dlse_ref[r, :]
                for p in range(cgw):
                    cols = slice(p * LANES, (p + 1) * LANES)
                    dos.append(_stack_heads(do_r[:, cols].astype(BF16)))
                    lse_c.append(_per_head_col(lse_r[:, cols]))
                    corr.append(_head_sums(dlse_r[:, cols]) - _head_sums(do_r[:, cols] * o_r[:, cols]))
            dos, lse_c, corr = jnp.stack(dos), jnp.stack(lse_c), jnp.stack(corr)
            s = lax.dot_general(qs, kband, _BDOT_NT, preferred_element_type=F32) * scale
            pr = jnp.exp(jnp.where(_att_masks(n, max_dist), s, NEG) - lse_c)
            dp = lax.dot_general(dos, vband, _BDOT_NT, preferred_element_type=F32)
            ds = (pr * (dp + corr)).astype(BF16)
            dq = lax.dot_general(ds, kband, _BDOT_NN, preferred_element_type=F32) * scale
            dkb = lax.dot_general(ds, qs, _BDOT_TN, preferred_element_type=F32) * scale
            dvb = lax.dot_general(pr.astype(BF16), dos, _BDOT_TN, preferred_element_type=F32)
            dsk = jnp.exp(sk - lse_c) * corr
            lane = lax.broadcasted_iota(jnp.int32, (8, LANES), 1)
            for p in range(cgw):
                tot = [jnp.sum(jnp.stack([dsk[i * cgw + p, h * ATT_BLOCK:(h + 1) * ATT_BLOCK] for i in range(len(rows))]),
                               axis=(0, 1)).reshape(1, 1) for h in range(2)]
                dsk_ref[:, p * LANES:(p + 1) * LANES] += jnp.where(lane == 0, tot[0], jnp.where(lane == HEAD_DIM, tot[1], 0.0))

            def gather_pairs(parts):
                if not shared:
                    return jnp.concatenate(parts, axis=1)
                tot = parts[0]
                for t in parts[1:]:
                    tot = tot + t
                tot = tot + pltpu.roll(tot, HEAD_DIM, axis=1)
                return jnp.where(_lane_half(ATT_BLOCK) == g, tot, 0.0)

            for i, r in enumerate(rows):
                mine = range(i * cgw, (i + 1) * cgw)
                dq_ref[r, :] = jnp.concatenate([_unstack_heads(dq[b]) for b in mine], axis=1)
                put(dk_ref, r, ck[r, :] + gather_pairs([dkb[b, :ATT_BLOCK] for b in mine]))
                put(dv_ref, r, cv[r, :] + gather_pairs([dvb[b, :ATT_BLOCK] for b in mine]))
                ck[r, :] = gather_pairs([dkb[b, ATT_BLOCK:] for b in mine])
                cv[r, :] = gather_pairs([dvb[b, ATT_BLOCK:] for b in mine])

        @pl.when(n < nb)
        def _():
            _att_streams(d, sb, work)

        @pl.when(n == nb)
        def _():
            dk_ref[...] = ck[...].reshape(dk_ref.shape)
            dv_ref[...] = cv[...].reshape(dv_ref.shape)

    sksp = BS((1, qw), lambda g, n: (0, g))
    rows = ATT_BLOCK * d
    if shared:
        osp = BS((1, rows, kw), lambda g, n: (g, jnp.maximum(n - 1, 0), 0))
        kshape = SDS((G, T, Wk), F32)
    else:
        osp = BS((rows, kw), lambda g, n: (jnp.maximum(n - 1, 0), g))
        kshape = SDS((T, Wk), F32)
    dq, dk, dv, dsk = _pcall(
        body, name=name, grid=(G, nb + 1),
        in_specs=[qsp, csp, psp, csp, psp, sksp, qsp, qsp, qsp, qsp],
        out_specs=[qsp, osp, osp, BS((8, qw), lambda g, n: (0, g))],
        out_shape=[SDS((T, Wq), F32), kshape, kshape, SDS((8, Wq), F32)],
        scratch_shapes=[pltpu.VMEM((rows, kw), F32)] * 2,
        compiler_params=_cparams(("parallel", "arbitrary")))(q, k, k, v, v, sinks, o, lse, do, dlse)
    if shared:
        dk, dv = jnp.sum(dk, axis=0), jnp.sum(dv, axis=0)
    return dq, dk, dv, dsk[0:1]


@functools.partial(jax.custom_vjp, nondiff_argnums=(4, 5, 6))
def band_attention(q, k, v, sinks, d, max_dist, name):
    return _att_fwd_call(q, k, v, sinks, d, max_dist, name)


def _band_attention_fwd(q, k, v, sinks, d, max_dist, name):
    o, lse = _att_fwd_call(q, k, v, sinks, d, max_dist, name)
    return (o, lse), (q, k, v, sinks, o, lse)


def _band_attention_bwd(d, max_dist, name, res, g):
    q, k, v, sinks, o, lse = res
    do, dlse = g
    return _att_bwd_call(q, k, v, sinks, o, lse, do, dlse, d, max_dist, name + "_bwd")


band_attention.defvjp(_band_attention_fwd, _band_attention_bwd)


def _merge_weights(ls):
    mx = jnp.maximum(jnp.maximum(ls[0], ls[1]), ls[2])
    es = [jnp.exp(l - mx) for l in ls]
    inv = 1.0 / (es[0] + es[1] + es[2])
    return [e * inv for e in es]


def _merge_fwd_call(os_, ls_, name):
    T, W = os_[0].shape
    tt = _tile(T, 512, 8)

    def body(o1, o2, o3, l1, l2, l3, out_ref):
        w = _merge_weights([l1[...], l2[...], l3[...]])
        out_ref[...] = w[0] * o1[...] + w[1] * o2[...] + w[2] * o3[...]

    row = BS((tt, W), lambda i: (i, 0))
    return _pcall(body, name=name, grid=(T // tt,), in_specs=[row] * 6, out_specs=row,
                  out_shape=SDS((T, W), F32), compiler_params=_cparams(("parallel",)))(*os_, *ls_)


def _merge_bwd_call(os_, ls_, do, name):
    T, W = os_[0].shape
    tt = _tile(T, 512, 8)

    def body(o1, o2, o3, l1, l2, l3, do_ref, d1, d2, d3, e1, e2, e3):
        w = _merge_weights([l1[...], l2[...], l3[...]])
        dov = do_ref[...]
        ts = [dov * o[...] for o in (o1, o2, o3)]
        mean = w[0] * ts[0] + w[1] * ts[1] + w[2] * ts[2]
        for wi, ti, dref, eref in zip(w, ts, (d1, d2, d3), (e1, e2, e3)):
            dref[...] = wi * dov
            eref[...] = wi * (ti - mean)

    row = BS((tt, W), lambda i: (i, 0))
    return _pcall(body, name=name, grid=(T // tt,), in_specs=[row] * 7, out_specs=[row] * 6,
                  out_shape=[SDS((T, W), F32)] * 6, compiler_params=_cparams(("parallel",)))(*os_, *ls_, do)


@functools.partial(jax.custom_vjp, nondiff_argnums=(2,))
def merge3(os_, ls_, name):
    return _merge_fwd_call(os_, ls_, name)


def _merge3_fwd(os_, ls_, name):
    return _merge_fwd_call(os_, ls_, name), (os_, ls_)


def _merge3_bwd(name, res, do):
    os_, ls_ = res
    out = _merge_bwd_call(os_, ls_, do, name + "_bwd")
    return tuple(out[:3]), tuple(out[3:])


merge3.defvjp(_merge3_fwd, _merge3_bwd)


def _xa_probs(qb, kb, scale):
    s = lax.dot_general(qb, kb, (((1,), (1,)), ((), ())), preferred_element_type=F32) * scale
    e = jnp.exp(s - jnp.max(s, axis=-1, keepdims=True))
    return e / jnp.sum(e, axis=-1, keepdims=True)


def _xa_fwd_call(q, kv, name):
    T, W = q.shape
    M = kv.shape[0]
    hd = XA_HEAD_DIM
    tq = _tile(T, 512, 8)
    scale = hd ** -0.5

    def body(q_ref, k_ref, v_ref, o_ref):
        p = _xa_probs(q_ref[...].astype(BF16), k_ref[...].astype(BF16), scale)
        o_ref[...] = jnp.dot(p.astype(BF16), v_ref[...].astype(BF16), preferred_element_type=F32)

    qsp = BS((tq, hd), lambda i, h: (i, h))
    return _pcall(body, name=name, grid=(T // tq, XA_HEADS),
                  in_specs=[qsp, BS((M, hd), lambda i, h: (0, h)), BS((M, hd), lambda i, h: (0, XA_HEADS + h))],
                  out_specs=qsp, out_shape=SDS((T, W), F32),
                  compiler_params=_cparams(("parallel", "parallel")))(q, kv, kv)


def _xa_bwd_call(q, kv, do, name):
    T, W = q.shape
    M = kv.shape[0]
    hd = XA_HEAD_DIM
    tq = _tile(T, 512, 8)
    scale = hd ** -0.5
    dn_nt = (((1,), (1,)), ((), ()))
    dn_tn = (((0,), (0,)), ((), ()))

    def body(q_ref, k_ref, v_ref, do_ref, dq_ref, dk_ref, dv_ref):
        @pl.when(pl.program_id(1) == 0)
        def _():
            dk_ref[...] = jnp.zeros_like(dk_ref)
            dv_ref[...] = jnp.zeros_like(dv_ref)

        qb, kb, vb = q_ref[...].astype(BF16), k_ref[...].astype(BF16), v_ref[...].astype(BF16)
        p = _xa_probs(qb, kb, scale)
        dob = do_ref[...].astype(BF16)
        dp = lax.dot_general(dob, vb, dn_nt, preferred_element_type=F32)
        ds = (p * (dp - jnp.sum(p * dp, axis=-1, keepdims=True))).astype(BF16)
        dq_ref[...] = jnp.dot(ds, kb, preferred_element_type=F32) * scale
        dk_ref[...] += lax.dot_general(ds, qb, dn_tn, preferred_element_type=F32) * scale
        dv_ref[...] += lax.dot_general(p.astype(BF16), dob, dn_tn, preferred_element_type=F32)

    qsp = BS((tq, hd), lambda h, i: (i, h))
    ksp = BS((M, hd), lambda h, i: (0, h))
    return _pcall(body, name=name, grid=(XA_HEADS, T // tq),
                  in_specs=[qsp, ksp, BS((M, hd), lambda h, i: (0, XA_HEADS + h)), qsp],
                  out_specs=[qsp, ksp, ksp], out_shape=[SDS((T, W), F32), SDS((M, W), F32), SDS((M, W), F32)],
                  compiler_params=_cparams(("parallel", "arbitrary")))(q, kv, kv, do)


@functools.partial(jax.custom_vjp, nondiff_argnums=(2,))
def cross_attention(q, kv, name):
    return _xa_fwd_call(q, kv, name)


def _cross_attention_fwd(q, kv, name):
    return _xa_fwd_call(q, kv, name), (q, kv)


def _cross_attention_bwd(name, res, do):
    q, kv = res
    dq, dk, dv = _xa_bwd_call(q, kv, do, name + "_bwd")
    return dq, jnp.concatenate([dk, dv], axis=1)


cross_attention.defvjp(_cross_attention_fwd, _cross_attention_bwd)


def _swiglu_fwd_call(gu, name):
    T, W2 = gu.shape
    F = W2 // 2
    tt = _tile(T, 256, 16)

    def body(g_ref, u_ref, o_ref):
        g = g_ref[...]
        o_ref[...] = ((g * _sigmoid(g)) * u_ref[...]).astype(BF16)

    return _pcall(body, name=name, grid=(T // tt,),
                  in_specs=[BS((tt, F), lambda i: (i, 0)), BS((tt, F), lambda i: (i, 1))],
                  out_specs=BS((tt, F), lambda i: (i, 0)), out_shape=SDS((T, F), BF16),
                  compiler_params=_cparams(("parallel",)))(gu, gu)


def _swiglu_bwd_call(gu, dact, name):
    T, W2 = gu.shape
    F = W2 // 2
    tt = _tile(T, 128, 16)

    def body(g_ref, u_ref, d_ref, o_ref):
        g, d = g_ref[...], d_ref[...]
        sg = _sigmoid(g)
        o_ref[:, :F] = (d * u_ref[...] * (sg * (1.0 + g * (1.0 - sg)))).astype(BF16)
        o_ref[:, F:] = (d * (g * sg)).astype(BF16)

    return _pcall(body, name=name, grid=(T // tt,),
                  in_specs=[BS((tt, F), lambda i: (i, 0)), BS((tt, F), lambda i: (i, 1)), BS((tt, F), lambda i: (i, 0))],
                  out_specs=BS((tt, W2), lambda i: (i, 0)), out_shape=SDS((T, W2), BF16),
                  compiler_params=_cparams(("parallel",)))(gu, gu, dact)


@functools.partial(jax.custom_vjp, nondiff_argnums=(6,))
def ffn_block(h, g, w1b, w1c, w2b, w2c, name):
    return _ffn_fwd(h, g, w1b, w1c, w2b, w2c, name)[0]


def _ffn_fwd(h, g, w1b, w1c, w2b, w2c, name):
    hn = _rms_fwd_call(h, g, name + "_norm", BF16)
    gu = _mm(hn, w1b, name=name + "_gu", tm=1024, tn=1408, tk=1024)
    act = _swiglu_fwd_call(gu, name + "_swiglu")
    out = _mm(act, w2b, add=h, name=name + "_down", tm=1024, tn=1024, tk=1408)
    return out, (h, g, hn, gu, act, w1b, w2b)


def _ffn_bwd(name, res, dout):
    h, g, hn, gu, act, w1b, w2b = res
    dact = _mm(dout, w2b, tb=True, name=name + "_down_da", tm=1024, tn=1408, tk=1024)
    dw2 = _mm(act, dout, ta=True, name=name + "_down_dw", tm=1408, tn=1024, tk=1024)
    dgu = _swiglu_bwd_call(gu, dact, name + "_swiglu_bwd")
    dhn = _mm(dgu, w1b, tb=True, name=name + "_gu_da", tm=1024, tn=1024, tk=1408)
    dw1 = _mm(hn, dgu, ta=True, name=name + "_gu_dw", tm=1024, tn=1408, tk=1024)
    dh, dg = _rms_bwd_call(h, g, dhn, name + "_norm_bwd", add=dout)
    return dh, dg.reshape(g.shape), jnp.zeros_like(w1b), dw1, jnp.zeros_like(w2b), dw2


ffn_block.defvjp(_ffn_fwd, _ffn_bwd)


def _final_call(h, g, target, name):
    T, Dm = h.shape
    tt = _tile(T, 512, 8)

    def body(x_ref, g_ref, t_ref, loss_ref, dx_ref, dg_ref):
        @pl.when(pl.program_id(0) == 0)
        def _():
            loss_ref[...] = jnp.zeros_like(loss_ref)
            dg_ref[...] = jnp.zeros_like(dg_ref)

        xv, gv = x_ref[...], g_ref[...]
        r = lax.rsqrt(jnp.mean(xv * xv, axis=-1, keepdims=True) + NORM_EPS)
        xh = xv * r
        err = xh * gv - t_ref[...]
        loss_ref[...] += 0.5 * jnp.sum(jnp.mean(err * err, axis=-1, keepdims=True), axis=0, keepdims=True)
        dy = err * (1.0 / Dm)
        dyg = dy * gv
        dx_ref[...] = r * (dyg - xh * jnp.mean(dyg * xh, axis=-1, keepdims=True))
        dg_ref[...] += jnp.sum(dy * xh, axis=0, keepdims=True)

    row = BS((tt, Dm), lambda i: (i, 0))
    vec = BS((1, Dm), lambda i: (0, 0))
    return _pcall(body, name=name, grid=(T // tt,), in_specs=[row, vec, row],
                  out_specs=[BS((1, 1), lambda i: (0, 0)), row, vec],
                  out_shape=[SDS((1, 1), F32), SDS((T, Dm), F32), SDS((1, Dm), F32)],
                  compiler_params=_cparams(("arbitrary",)))(h, g.reshape(1, Dm), target)


ADAMW_BLOCK_ELEMS = 64 * 1024


def _adamw_call(parts, w, m, v, name):
    shape = w.shape
    if not isinstance(parts, (list, tuple)):
        parts, shape3 = [parts], (1,) + shape
    else:
        shape3 = shape
    n_lead = shape3[0]
    r, N = shape3[-2], shape3[-1]
    Ld = math.prod(shape3[1:-2])
    w, m, v = (t.reshape(n_lead * Ld, r, N) for t in (w, m, v))
    tr = _tile(r, max(8, ADAMW_BLOCK_ELEMS // N), 8)
    c1 = 1.0 - ADAM_B1 ** ADAM_STEP
    c2 = 1.0 - ADAM_B2 ** ADAM_STEP
    outs = None
    for lead, p in enumerate(parts):
        def body(p_ref, w_ref, m_ref, v_ref, *rest):
            g_ref, d_ref, nm_ref, nv_ref = rest[-4:]
            g = p_ref[0]
            for j in range(1, N_DEV):
                g = g + p_ref[j]
            nm = ADAM_B1 * m_ref[...] + (1.0 - ADAM_B1) * g
            nv = ADAM_B2 * v_ref[...] + (1.0 - ADAM_B2) * (g * g)
            g_ref[...] = g
            nm_ref[...] = nm
            nv_ref[...] = nv
            d_ref[...] = -ADAM_LR * ((nm / c1) / (jnp.sqrt(nv / c2) + ADAM_EPS) + ADAM_WD * w_ref[...])

        base = lead * Ld
        row = BS((1, tr, N), lambda l, i, base=base: (base + l, i, 0))
        prev = [] if outs is None else list(outs)
        outs = _pcall(body, name=f"{name}_{lead}", grid=(Ld, r // tr),
                      in_specs=[BS((N_DEV, 1, tr, N), lambda l, i: (0, l, i, 0)), row, row, row]
                      + [BS(memory_space=pl.ANY)] * len(prev),
                      out_specs=[row] * 4, out_shape=[SDS((n_lead * Ld, r, N), F32)] * 4,
                      input_output_aliases={4 + j: j for j in range(len(prev))},
                      compiler_params=_cparams(("parallel", "parallel")))(p.reshape(N_DEV, Ld, r, N), w, m, v, *prev)
    return [t.reshape(shape) for t in outs]


def _place():
    return lax.axis_index("x"), lax.axis_index("y"), lax.axis_index("c")


def _all_gather(xs, name):
    n = len(xs)
    pairs = [(i, l) for i, x in enumerate(xs) for l in range(x.shape[0])]

    def body(*refs):
        x_refs, o_refs = refs[:n], refs[n:2 * n]
        send_sems, recv_sems, local_sems = refs[2 * n:]
        x_, y_, c_ = _place()
        me, sibling = (x_, y_, c_), (x_, y_, 1 - c_)
        chips = [(1 - x_, y_), (x_, 1 - y_), (1 - x_, 1 - y_)]

        def copy(e, k, block, to, from_input=False):
            i, l = pairs[e]
            px, py, pc = block
            dst = o_refs[i].at[l, 4 * px + 2 * py + pc]
            return pltpu.make_async_remote_copy(
                src_ref=x_refs[i].at[l] if from_input else dst, dst_ref=dst,
                send_sem=send_sems.at[7 * e + k], recv_sem=recv_sems.at[7 * e + k],
                device_id=to, device_id_type=pl.DeviceIdType.MESH)

        every = range(len(pairs))
        mine = [pltpu.make_async_copy(x_refs[i].at[l], o_refs[i].at[l, 4 * x_ + 2 * y_ + c_], local_sems.at[e])
                for e, (i, l) in enumerate(pairs)]
        for cp in mine:
            cp.start()
        first = [copy(e, 0, me, sibling, True) for e in every]
        first += [copy(e, 1 + j, me, (*chip, c_), True) for j, chip in enumerate(chips) for e in every]
        for cp in first:
            cp.start()
        passed = []
        for j, chip in enumerate(chips):
            for e in every:
                copy(e, 1 + j, (*chip, c_), me).wait_recv()
            for e in every:
                cp = copy(e, 4 + j, (*chip, c_), sibling)
                cp.start()
                passed.append(cp)
        for e in every:
            copy(e, 0, sibling, me).wait_recv()
        for j, chip in enumerate(chips):
            for e in every:
                copy(e, 4 + j, (*chip, 1 - c_), me).wait_recv()
        for cp in first + passed:
            cp.wait_send()
        for cp in mine:
            cp.wait()

    any_spec = BS(memory_space=pl.ANY)
    return _pcall(body, name=name, in_specs=[any_spec] * n, out_specs=[any_spec] * n,
                  out_shape=[SDS((x.shape[0], N_DEV) + x.shape[1:], x.dtype) for x in xs],
                  scratch_shapes=[pltpu.SemaphoreType.DMA((7 * len(pairs),)), pltpu.SemaphoreType.DMA((7 * len(pairs),)),
                                  pltpu.SemaphoreType.DMA((len(pairs),))],
                  compiler_params=pltpu.CompilerParams(has_side_effects=True))(*xs)


def _peer_of(k, place):
    x_, y_, c_ = place
    fx, fy, fc = (k >> 2) & 1, (k >> 1) & 1, k & 1
    return fx + x_ - 2 * fx * x_, fy + y_ - 2 * fy * y_, fc + c_ - 2 * fc * c_


def _split_copy(src_ref, land_ref, send_sems, recv_sems, e, k, place, scatter):
    x_, y_, c_ = place
    px, py, pc = _peer_of(k, place)
    return pltpu.make_async_remote_copy(
        src_ref=src_ref.at[4 * px + 2 * py + pc] if scatter else src_ref, dst_ref=land_ref.at[4 * x_ + 2 * y_ + c_],
        send_sem=send_sems.at[7 * e + k - 1], recv_sem=recv_sems.at[7 * e + k - 1],
        device_id=(px, py, pc), device_id_type=pl.DeviceIdType.MESH)


_HBM_SPEC = BS(memory_space=pltpu.HBM)
_SEM_SPEC = BS(memory_space=pltpu.SEMAPHORE)
_EFFECT = pltpu.SideEffectType.DATAFLOW_SIDE_EFFECTING


def _copies_start(srcs, scatter, name, thru=None):
    n = len(srcs)
    lands = [lax.empty(s.shape if scatter else (N_DEV,) + s.shape, s.dtype) for s in srcs]
    passed = srcs + lands + ([] if thru is None else [thru])

    def body(*refs):
        src_refs, land_refs = refs[:n], refs[n:2 * n]
        send_sems, recv_sems = refs[len(passed)], refs[len(passed) + 1]
        token = refs[-1]
        place = _place()
        for e in range(n):
            for k in range(1, N_DEV):
                _split_copy(src_refs[e], land_refs[e], send_sems, recv_sems, e, k, place, scatter).start()
        token[...] = jnp.zeros_like(token)

    hbm = lambda t: pltpu.with_memory_space_constraint(t, pltpu.HBM)
    out = _pcall(body, name=name,
                 out_shape=(pltpu.SemaphoreType.DMA((7 * n,)), pltpu.SemaphoreType.DMA((7 * n,)),
                            *[pltpu.HBM(t.shape, t.dtype) for t in passed], SDS((8, LANES), F32)),
                 in_specs=[_HBM_SPEC] * len(passed),
                 out_specs=(_SEM_SPEC, _SEM_SPEC, *[_HBM_SPEC] * len(passed), BS(memory_space=pltpu.VMEM)),
                 input_output_aliases={i: 2 + i for i in range(len(passed))},
                 compiler_params=pltpu.CompilerParams(has_side_effects=_EFFECT))(*[hbm(t) for t in passed])
    return out[0], out[1], list(out[2:2 + n]), list(out[2 + n:2 + 2 * n]), (None if thru is None else out[2 + 2 * n])


def _copies_wait(started, which, scatter, after, name):
    send_sems, recv_sems, srcs, lands, _ = started
    n = len(which)

    def body(*refs):
        src_refs, land_refs = refs[:n], refs[n:2 * n]
        send_s, recv_s = refs[2 * n], refs[2 * n + 1]
        place = _place()
        for j, e in enumerate(which):
            for k in range(1, N_DEV):
                cp = _split_copy(src_refs[j], land_refs[j], send_s, recv_s, e, k, place, scatter)
                cp.wait_send()
                cp.wait_recv()

    args = [srcs[e] for e in which] + [lands[e] for e in which]
    out = _pcall(body, name=name, out_shape=tuple(pltpu.HBM(t.shape, t.dtype) for t in args),
                 in_specs=[_HBM_SPEC] * (2 * n) + [_SEM_SPEC, _SEM_SPEC, BS(memory_space=pl.ANY)],
                 out_specs=tuple([_HBM_SPEC] * (2 * n)), input_output_aliases={i: i for i in range(2 * n)},
                 compiler_params=pltpu.CompilerParams(has_side_effects=_EFFECT))(*args, send_sems, recv_sems, after)
    return list(out[:n]), list(out[n:])


def _with_own_block(land, own_block):
    me = 4 * lax.axis_index("x") + 2 * lax.axis_index("y") + lax.axis_index("c")
    return lax.dynamic_update_index_in_dim(land, own_block, me, 0)


def _pad_flat(t, quantum=PACK_QUANTUM):
    f = t.reshape(-1)
    pad = (-f.shape[0]) % quantum
    return jnp.pad(f, (0, pad)) if pad else f


def _pack(arrs, dtype):
    return jnp.concatenate([_pad_flat(a.astype(dtype)) for a in arrs]).reshape(-1, LANES)


def _unpack(buf, shapes, lead=()):
    flat = buf.reshape(lead + (-1,))
    out, off = [], 0
    for s in shapes:
        n = math.prod(s)
        out.append(flat[..., off:off + n].reshape(lead + tuple(s)))
        off += n + (-n) % PACK_QUANTUM
    return out


def _full_from_gathered(g, axis):
    t = jnp.moveaxis(g, 0, axis)
    s = t.shape
    return t.reshape(s[:axis] + (s[axis] * s[axis + 1],) + s[axis + 2:])


def _parts_from_full(t, axis):
    s = t.shape
    t = t.reshape(s[:axis] + (N_DEV, s[axis] // N_DEV) + s[axis + 1:])
    return jnp.moveaxis(t, axis, 0)


def _head_rows(t):
    return jnp.repeat(t, HEAD_DIM).reshape(1, -1)


def _dilated_attention(q, k, v, name):
    no_sink = jnp.full((1, q.shape[1]), NEG, F32)
    outs, lses = zip(*[band_attention(q, k, v, no_sink, d, ATT_BLOCK, f"{name}_d{d}") for d in DILATIONS])
    return merge3(tuple(outs), tuple(lses), name + "_merge")


STAGES = (
    ("proj0", ('mix_norm',), (('ab_w_in', 0),)),
    ("mixer0", ('lru_conv_w', 'lru_conv_b', 'lru_ba', 'lru_bx', 'lru_lambda'),
     (('lru_wa', 0), ('lru_wx', 0), ('ab_w_out', 0))),
    ("xa0", ('xa_norm', 'xa_mem_norm'), (('xa_wq', 0), ('xa_wkv', 0), ('xa_wo', 0))),
    ("ffn0", ('ffn_norm',), (('ffn_w_gate_up', 0), ('ffn_w_down', 0))),
    ("mixer1", ('mix_norm', 'c_b_qkv', 'c_sinks', 'c_b_out'), (('c_w_qkv', 0), ('c_w_out', 0))),
    ("xa1", ('xa_norm', 'xa_mem_norm'), (('xa_wq', 1), ('xa_wkv', 1), ('xa_wo', 1))),
    ("ffn1", ('ffn_norm',), (('ffn_w_gate_up', 1), ('ffn_w_down', 1))),
)


def _stage_fn(stage, Wb, tabs, mem):
    layer = int(stage[-1])
    L = f"l{layer}"

    def run(S, Cw, h):
        def lin(a, key, bias, add, name, rows=None):
            wb, wc = Wb[key], Cw[key]
            if rows is not None:
                wb, wc = wb[rows], wc[rows]
            return linear(a, wb, wc, bias, add, name)

        def norm_lin(a, gain, key, bias, name):
            return norm_linear(a, gain, Wb[key], Cw[key], bias, name)

        if stage == "proj0":
            proj, h = norm_lin(h, S['mix_norm'][0], ('ab_w_in', 0), None, L + "_w_in")
            return h, proj
        if stage == "mixer0":
            h, proj = h
            C = S['lru_conv_w'].shape[-1]
            xc = conv4(proj[:, :C], S['lru_conv_w'][0], S['lru_conv_b'][0], L + "_conv")
            ga, gx = lru_gates(xc, Wb['lru_wa', 0], Cw['lru_wa', 0], S['lru_ba'][0],
                               Wb['lru_wx', 0], Cw['lru_wx', 0], S['lru_bx'][0], L + "_gates")
            rec = lru_scan(xc, ga, gx, proj[:, C:2 * C], S['lru_lambda'][0], L + "_scan")
            bw = B_HEADS * HEAD_DIM
            q = rope(proj[:, 2 * C:2 * C + bw], tabs, L + "_rope_q")
            k = rope(proj[:, 2 * C + bw:2 * C + 2 * bw], tabs, L + "_rope_k")
            v = proj[:, 2 * C + 2 * bw:]
            att = _dilated_attention(q, k, v, L + "_att")
            h = lin(att, ('ab_w_out', 0), None, h, L + "_w_out_att", slice(C, None))
            return lin(rec, ('ab_w_out', 0), None, h, L + "_w_out_rec", slice(0, C))
        if stage == "mixer1":
            qw = C_HEADS * HEAD_DIM
            kw = C_KV_HEADS * HEAD_DIM
            qkv, h = norm_lin(h, S['mix_norm'][1], ('c_w_qkv', 0), S['c_b_qkv'][0], L + "_w_qkv")
            q = rope(qkv[:, :qw], tabs, L + "_rope_q")
            k = rope(qkv[:, qw:qw + kw], tabs, L + "_rope_k")
            v = qkv[:, qw + kw:]
            o, _ = band_attention(q, k, v, _head_rows(S['c_sinks'][0]), 1, ATT_BLOCK - 1, L + "_att")
            return lin(o, ('c_w_out', 0), S['c_b_out'][0], h, L + "_w_out")
        if stage.startswith("xa"):
            xq, h = norm_lin(h, S['xa_norm'][layer], ('xa_wq', layer), None, L + "_xa_wq")
            xkv, _ = norm_lin(mem, S['xa_mem_norm'][layer], ('xa_wkv', layer), None, L + "_xa_wkv")
            return lin(cross_attention(xq, xkv, L + "_xa"), ('xa_wo', layer), None, h, L + "_xa_wo")
        gu, down = ('ffn_w_gate_up', layer), ('ffn_w_down', layer)
        return ffn_block(h, S['ffn_norm'][layer], Wb[gu], Cw[gu], Wb[down], Cw[down], L + "_ffn")

    return run


def kernel(x, mem, mix_norm, ab_w_in, lru_conv_w, lru_conv_b, lru_wa, lru_ba, lru_wx, lru_bx, lru_lambda, ab_w_out, c_w_qkv, c_b_qkv, c_sinks, c_w_out, c_b_out, xa_norm, xa_mem_norm, xa_wq, xa_wkv, xa_wo, ffn_norm, ffn_w_gate_up, ffn_w_down, final_norm, loss_target, m_mix_norm, m_ab_w_in, m_lru_conv_w, m_lru_conv_b, m_lru_wa, m_lru_ba, m_lru_wx, m_lru_bx, m_lru_lambda, m_ab_w_out, m_c_w_qkv, m_c_b_qkv, m_c_sinks, m_c_w_out, m_c_b_out, m_xa_norm, m_xa_mem_norm, m_xa_wq, m_xa_wkv, m_xa_wo, m_ffn_norm, m_ffn_w_gate_up, m_ffn_w_down, m_final_norm, v_mix_norm, v_ab_w_in, v_lru_conv_w, v_lru_conv_b, v_lru_wa, v_lru_ba, v_lru_wx, v_lru_bx, v_lru_lambda, v_ab_w_out, v_c_w_qkv, v_c_b_qkv, v_c_sinks, v_c_w_out, v_c_b_out, v_xa_norm, v_xa_mem_norm, v_xa_wq, v_xa_wkv, v_xa_wo, v_ffn_norm, v_ffn_w_gate_up, v_ffn_w_down, v_final_norm):
    w_loc = dict(zip(WEIGHT_NAMES, (mix_norm, ab_w_in, lru_conv_w, lru_conv_b, lru_wa, lru_ba, lru_wx, lru_bx, lru_lambda, ab_w_out, c_w_qkv, c_b_qkv, c_sinks, c_w_out, c_b_out, xa_norm, xa_mem_norm, xa_wq, xa_wkv, xa_wo, ffn_norm, ffn_w_gate_up, ffn_w_down, final_norm)))
    m_loc = dict(zip(WEIGHT_NAMES, (m_mix_norm, m_ab_w_in, m_lru_conv_w, m_lru_conv_b, m_lru_wa, m_lru_ba, m_lru_wx, m_lru_bx, m_lru_lambda, m_ab_w_out, m_c_w_qkv, m_c_b_qkv, m_c_sinks, m_c_w_out, m_c_b_out, m_xa_norm, m_xa_mem_norm, m_xa_wq, m_xa_wkv, m_xa_wo, m_ffn_norm, m_ffn_w_gate_up, m_ffn_w_down, m_final_norm)))
    v_loc = dict(zip(WEIGHT_NAMES, (v_mix_norm, v_ab_w_in, v_lru_conv_w, v_lru_conv_b, v_lru_wa, v_lru_ba, v_lru_wx, v_lru_bx, v_lru_lambda, v_ab_w_out, v_c_w_qkv, v_c_b_qkv, v_c_sinks, v_c_w_out, v_c_b_out, v_xa_norm, v_xa_mem_norm, v_xa_wq, v_xa_wkv, v_xa_wo, v_ffn_norm, v_ffn_w_gate_up, v_ffn_w_down, v_final_norm)))

    me = 4 * lax.axis_index("x") + 2 * lax.axis_index("y") + lax.axis_index("c")

    keys = [key for _, _, stage_keys in STAGES for key in stage_keys]
    shards = [w_loc[n][l].astype(BF16) for n, l in keys]
    small_g = _all_gather([_pack([w_loc[n] for n in SMALL], F32)[None]], "gather_small")[0][0]
    gather = _copies_start(shards, False, "gather_start", thru=small_g)
    small_g = gather[4]
    S = {n: w_loc[n] for n in REPLICATED}
    for n, t in zip(SMALL, _unpack(small_g, [w_loc[n].shape for n in SMALL], lead=(N_DEV,))):
        S[n] = _full_from_gathered(t, SHARD_AXIS[n])

    tabs = _rope_tables(x.shape[1])
    Wb, vjps = {}, []
    h = x[0]
    for stage, small_names, stage_keys in STAGES:
        which = [keys.index(key) for key in stage_keys]
        _, lands = _copies_wait(gather, which, False, jax.tree.leaves(h)[-1], "gather_wait_" + stage)
        for key, e, land in zip(stage_keys, which, lands):
            Wb[key] = _full_from_gathered(_with_own_block(land, shards[e]), SHARD_AXIS[key[0]] - 1)
        carriers = {key: jnp.zeros(Wb[key].shape, F32) for key in stage_keys}
        h, vjp_fn = jax.vjp(_stage_fn(stage, Wb, tabs, mem[0]), {n: S[n] for n in small_names}, carriers, h)
        vjps.append(vjp_fn)
    loss_part, dh, dg_final = _final_call(h, S['final_norm'], loss_target[0], "final_loss")

    grads = {'final_norm': dg_final.reshape(final_norm.shape)}
    exchanges, send_keys, send_parts = [], [], []
    for (stage, small_names, stage_keys), vjp_fn in zip(reversed(STAGES), reversed(vjps)):
        g_small, g_big, dh = vjp_fn(dh)
        for n in small_names:
            grads[n] = grads[n] + g_small[n] if n in grads else g_small[n]
        send_keys += list(stage_keys)
        send_parts += [_parts_from_full(g_big[key], SHARD_AXIS[key[0]] - 1) for key in stage_keys]
        if stage == "xa1":
            continue
        if stage == "mixer0":
            small_parts = [_parts_from_full(grads[n], SHARD_AXIS[n]) for n in SMALL]
            send_keys.append("small")
            send_parts.append(jnp.stack([_pack([p[j] for p in small_parts], F32) for j in range(N_DEV)]))
        leaves, tree = jax.tree.flatten(dh)
        started = _copies_start(send_parts, True, "grad_start_" + stage, thru=leaves[0])
        dh = jax.tree.unflatten(tree, [started[4]] + leaves[1:])
        exchanges.append((stage, started, send_keys))
        send_keys, send_parts = [], []
    dx = dh

    parts = {}
    for stage, started, ex_keys in exchanges:
        srcs, lands = _copies_wait(started, list(range(len(ex_keys))), True, dx, "grad_wait_" + stage)
        for key, src, land in zip(ex_keys, srcs, lands):
            parts[key] = _with_own_block(land, lax.dynamic_index_in_dim(src, me, 0, keepdims=False))
    out = {}

    def adamw(p, names, call_name):
        if len(names) == 1:
            res = _adamw_call(p, w_loc[names[0]], m_loc[names[0]], v_loc[names[0]], call_name)
            for kind, t in zip(("grad", "delta", "new_m", "new_v"), res):
                out[kind, names[0]] = t
        else:
            res = _adamw_call(p, *[_pack([d[n] for n in names], F32) for d in (w_loc, m_loc, v_loc)], call_name)
            for kind, buf in zip(("grad", "delta", "new_m", "new_v"), res):
                for n, t in zip(names, _unpack(buf, [w_loc[n].shape for n in names])):
                    out[kind, n] = t

    for n in BIG:
        adamw([parts[n, l] for l in range(w_loc[n].shape[0])], [n], "adamw_" + n)
    adamw(parts["small"], SMALL, "adamw_small")
    rep_names = REPLICATED + ["loss"]
    grads["loss"] = loss_part
    zero = jnp.zeros((1, 1), F32)
    for d in (w_loc, m_loc, v_loc):
        d["loss"] = zero
    rep_g = _all_gather([_pack([grads[n] for n in rep_names], F32)[None]], "gather_rep_grads")[0][0]
    adamw(rep_g, rep_names, "adamw_replicated")
    loss = out["grad", "loss"][0, 0]

    return (loss, dx[None], *[out[kind, n] for kind in ("grad", "delta", "new_m", "new_v") for n in WEIGHT_NAMES])
```

```python
import functools
import math

import jax
import jax.numpy as jnp
from jax import lax
from jax.experimental import pallas as pl
from jax.experimental.pallas import tpu as pltpu

F32 = jnp.float32
BF16 = jnp.bfloat16
SDS = jax.ShapeDtypeStruct
BS = pl.BlockSpec

N_DEV = 8
NORM_EPS = 1e-6
ROPE_THETA = 500000.0
HEAD_DIM = 64
ROT_DIM = 16
ATT_BLOCK = 128
LRU_C = 8.0
LRU_HEADS = 4
DILATIONS = (1, 4, 16)
B_HEADS = 8
C_HEADS = 16
C_KV_HEADS = 2
XA_HEADS = 4
XA_HEAD_DIM = 128
NEG = -1e30
ADAM_LR, ADAM_B1, ADAM_B2, ADAM_EPS, ADAM_WD, ADAM_STEP = 0.001, 0.9, 0.999, 1e-08, 0.01, 10
LANES = 128
VMEM_LIMIT = 48 * 1024 * 1024

WEIGHT_NAMES = ['mix_norm', 'ab_w_in', 'lru_conv_w', 'lru_conv_b', 'lru_wa', 'lru_ba', 'lru_wx', 'lru_bx',
                'lru_lambda', 'ab_w_out', 'c_w_qkv', 'c_b_qkv', 'c_sinks', 'c_w_out', 'c_b_out', 'xa_norm',
                'xa_mem_norm', 'xa_wq', 'xa_wkv', 'xa_wo', 'ffn_norm', 'ffn_w_gate_up', 'ffn_w_down', 'final_norm']
SHARD_AXIS = {'ab_w_in': 2, 'lru_conv_w': 2, 'lru_wa': 2, 'lru_ba': 2, 'lru_wx': 2, 'lru_bx': 2, 'ab_w_out': 1,
              'c_w_qkv': 2, 'c_b_qkv': 1, 'c_w_out': 1, 'c_b_out': 1, 'xa_wq': 1, 'xa_wkv': 1, 'xa_wo': 2,
              'ffn_w_gate_up': 2, 'ffn_w_down': 1}
BIG = ['ab_w_in', 'lru_wa', 'lru_wx', 'ab_w_out', 'c_w_qkv', 'c_w_out', 'xa_wq', 'xa_wkv', 'xa_wo',
       'ffn_w_gate_up', 'ffn_w_down']
SMALL = ['lru_conv_w', 'lru_ba', 'lru_bx', 'c_b_qkv', 'c_b_out']
SHARDED = BIG + SMALL
REPLICATED = [n for n in WEIGHT_NAMES if n not in SHARD_AXIS]
PACK_QUANTUM = 2048


def _pcall(body, **kw):
    return pl.pallas_call(body, **kw)


def _cparams(sem=None):
    return pltpu.CompilerParams(dimension_semantics=sem, vmem_limit_bytes=VMEM_LIMIT)


def _tile(n, target, mult=LANES):
    if n <= target:
        return n
    t = (target // mult) * mult
    while t >= mult:
        if n % t == 0:
            return t
        t -= mult
    return n


def _sigmoid(x):
    return 1.0 / (1.0 + jnp.exp(-x))


def _expm1(x):
    small = x * (1.0 + x * (0.5 + x * (1.0 / 6.0 + x * (1.0 / 24.0))))
    return jnp.where(jnp.abs(x) < 0.03, small, jnp.exp(x) - 1.0)


_GELU_C = math.sqrt(2.0 / math.pi)


def _gelu_parts(y):
    y2 = y * y
    th = jnp.tanh(_GELU_C * (y + 0.044715 * y * y2))
    g = 0.5 * y * (1.0 + th)
    dg = 0.5 * (1.0 + th) + 0.5 * y * (1.0 - th * th) * _GELU_C * (1.0 + 3.0 * 0.044715 * y2)
    return g, dg


def _mm(a, b, *, ta=False, tb=False, bias=None, add=None, name, tm=512, tn=512, tk=2048):
    M, K = (a.shape[1], a.shape[0]) if ta else a.shape
    N = b.shape[0] if tb else b.shape[1]
    tm, tn, tk = _tile(M, tm), _tile(N, tn), _tile(K, tk)
    nk = K // tk
    dn = (((0 if ta else 1,), (1 if tb else 0,)), ((), ()))

    def body(*refs):
        a_ref, b_ref = refs[0], refs[1]
        pos = 2
        bias_ref = add_ref = None
        if bias is not None:
            bias_ref = refs[pos]
            pos += 1
        if add is not None:
            add_ref = refs[pos]
            pos += 1
        o_ref = refs[pos]
        part = lax.dot_general(a_ref[...].astype(BF16), b_ref[...].astype(BF16), dn, preferred_element_type=F32)

        def finish(r):
            if bias_ref is not None:
                r = r + bias_ref[...]
            if add_ref is not None:
                r = r + add_ref[...]
            o_ref[...] = r

        if nk == 1:
            finish(part)
            return
        acc_ref = refs[pos + 1]
        k = pl.program_id(2)

        @pl.when(k == 0)
        def _():
            acc_ref[...] = part

        @pl.when((k > 0) & (k < nk - 1))
        def _():
            acc_ref[...] += part

        @pl.when(k == nk - 1)
        def _():
            finish(acc_ref[...] + part)

    in_specs = [BS((tk, tm), lambda i, j, k: (k, i)) if ta else BS((tm, tk), lambda i, j, k: (i, k)),
                BS((tn, tk), lambda i, j, k: (j, k)) if tb else BS((tk, tn), lambda i, j, k: (k, j))]
    args = [a, b]
    if bias is not None:
        in_specs.append(BS((1, tn), lambda i, j, k: (0, j)))
        args.append(bias.reshape(1, N))
    if add is not None:
        in_specs.append(BS((tm, tn), lambda i, j, k: (i, j)))
        args.append(add)
    return _pcall(body, name=name, grid=(M // tm, N // tn, nk), in_specs=in_specs,
                  out_specs=BS((tm, tn), lambda i, j, k: (i, j)), out_shape=SDS((M, N), F32),
                  scratch_shapes=[pltpu.VMEM((tm, tn), F32)] if nk > 1 else [],
                  compiler_params=_cparams(("parallel", "parallel", "arbitrary")))(*args)


def _colsum(x, name):
    T, N = x.shape
    tt = _tile(T, 512, 8)

    def body(x_ref, o_ref):
        @pl.when(pl.program_id(0) == 0)
        def _():
            o_ref[...] = jnp.zeros_like(o_ref)

        o_ref[...] += jnp.sum(x_ref[...], axis=0, keepdims=True)

    return _pcall(body, name=name, grid=(T // tt,), in_specs=[BS((tt, N), lambda i: (i, 0))],
                  out_specs=BS((1, N), lambda i: (0, 0)), out_shape=SDS((1, N), F32),
                  compiler_params=_cparams(("arbitrary",)))(x)


@functools.partial(jax.custom_vjp, nondiff_argnums=(5,))
def linear(a, wb, wc, bias, add, name):
    return _mm(a, wb, bias=bias, add=add, name=name)


def _linear_fwd(a, wb, wc, bias, add, name):
    return _mm(a, wb, bias=bias, add=add, name=name), (a, wb, bias is not None, add is not None)


def _linear_bwd(name, res, g):
    a, wb, has_bias, has_add = res
    da = _mm(g, wb, tb=True, name=name + "_da")
    dw = _mm(a, g, ta=True, name=name + "_dw")
    dbias = _colsum(g, name + "_db").reshape(-1) if has_bias else None
    return da, jnp.zeros_like(wb), dw, dbias, (g if has_add else None)


linear.defvjp(_linear_fwd, _linear_bwd)


def _rms_fwd_call(x, g, name, out_dtype=F32):
    T, Dm = x.shape
    tt = _tile(T, 512, 16)

    def body(x_ref, g_ref, o_ref):
        xv = x_ref[...]
        r = lax.rsqrt(jnp.mean(xv * xv, axis=-1, keepdims=True) + NORM_EPS)
        o_ref[...] = ((xv * r) * g_ref[...]).astype(out_dtype)

    return _pcall(body, name=name, grid=(T // tt,),
                  in_specs=[BS((tt, Dm), lambda i: (i, 0)), BS((1, Dm), lambda i: (0, 0))],
                  out_specs=BS((tt, Dm), lambda i: (i, 0)), out_shape=SDS((T, Dm), out_dtype),
                  compiler_params=_cparams(("parallel",)))(x, g.reshape(1, Dm))


def _rms_bwd_call(x, g, dy, name, add=None):
    T, Dm = x.shape
    tt = _tile(T, 512, 8)

    def body(*refs):
        x_ref, g_ref, dy_ref = refs[:3]
        dx_ref, dg_ref = refs[-2:]
        xv = x_ref[...]
        r = lax.rsqrt(jnp.mean(xv * xv, axis=-1, keepdims=True) + NORM_EPS)
        xh = xv * r
        dy = dy_ref[...]
        dyg = dy * g_ref[...]
        dx = r * (dyg - xh * jnp.mean(dyg * xh, axis=-1, keepdims=True))
        dx_ref[...] = dx if add is None else dx + refs[3][...]

        @pl.when(pl.program_id(0) == 0)
        def _():
            dg_ref[...] = jnp.zeros_like(dg_ref)

        dg_ref[...] += jnp.sum(dy * xh, axis=0, keepdims=True)

    row = BS((tt, Dm), lambda i: (i, 0))
    vec = BS((1, Dm), lambda i: (0, 0))
    extra = [] if add is None else [add]
    return _pcall(body, name=name, grid=(T // tt,), in_specs=[row, vec, row] + [row] * len(extra), out_specs=[row, vec],
                  out_shape=[SDS((T, Dm), F32), SDS((1, Dm), F32)],
                  compiler_params=_cparams(("arbitrary",)))(x, g.reshape(1, Dm), dy, *extra)


@functools.partial(jax.custom_vjp, nondiff_argnums=(2,))
def rmsnorm(x, g, name):
    return _rms_fwd_call(x, g, name)


def _rmsnorm_fwd(x, g, name):
    return _rms_fwd_call(x, g, name), (x, g)


def _rmsnorm_bwd(name, res, dy):
    x, g = res
    dx, dg = _rms_bwd_call(x, g, dy, name + "_bwd")
    return dx, dg.reshape(g.shape)


rmsnorm.defvjp(_rmsnorm_fwd, _rmsnorm_bwd)


@functools.partial(jax.custom_vjp, nondiff_argnums=(5,))
def norm_linear(x, g, wb, wc, bias, name):
    return _mm(_rms_fwd_call(x, g, name + "_norm", BF16), wb, bias=bias, name=name, tm=1024, tn=1408, tk=1024), x


def _norm_linear_fwd(x, g, wb, wc, bias, name):
    hn = _rms_fwd_call(x, g, name + "_norm", BF16)
    return (_mm(hn, wb, bias=bias, name=name, tm=1024, tn=1408, tk=1024), x), (x, g, hn, wb, bias is not None)


def _norm_linear_bwd(name, res, cts):
    x, g, hn, wb, has_bias = res
    dy, dres = cts
    dw = _mm(hn, dy, ta=True, name=name + "_dw", tm=1024, tn=1408, tk=1024)
    dhn = _mm(dy, wb, tb=True, name=name + "_da", tm=1024, tn=1024, tk=1408)
    dx, dg = _rms_bwd_call(x, g, dhn, name + "_norm_bwd", add=dres)
    dbias = _colsum(dy, name + "_db").reshape(-1) if has_bias else None
    return dx, dg.reshape(g.shape), jnp.zeros_like(wb), dw, dbias


norm_linear.defvjp(_norm_linear_fwd, _norm_linear_bwd)


def _rope_tables(T):
    half = ROT_DIM // 2
    inv = ROPE_THETA ** (-jnp.arange(0, ROT_DIM, 2, dtype=F32) / ROT_DIM)
    ang = jnp.arange(T, dtype=F32)[:, None] * inv[None, :]
    cos, sin = jnp.cos(ang), jnp.sin(ang)
    ones = jnp.ones((T, HEAD_DIM - ROT_DIM), F32)
    zeros = jnp.zeros((T, HEAD_DIM - ROT_DIM), F32)
    zh = jnp.zeros((T, half), F32)
    c = jnp.concatenate([cos, cos, ones], axis=1)
    sa = jnp.concatenate([zh, sin, zeros], axis=1)
    sb = jnp.concatenate([-sin, zh, zeros], axis=1)
    two = lambda t: jnp.concatenate([t, t], axis=1)
    return two(c), two(sa), two(sb)


def _rope_call(x, tabs, inverse, name):
    T, W = x.shape
    tt = _tile(T, 512, 8)
    reps = W // LANES
    half = ROT_DIM // 2

    def body(x_ref, c_ref, sa_ref, sb_ref, o_ref):
        xv = x_ref[...]
        c = jnp.tile(c_ref[...], (1, reps))
        sa = jnp.tile(sa_ref[...], (1, reps))
        sb = jnp.tile(sb_ref[...], (1, reps))
        if not inverse:
            o_ref[...] = xv * c + pltpu.roll(xv, half, axis=1) * sa + pltpu.roll(xv, W - half, axis=1) * sb
        else:
            o_ref[...] = xv * c + pltpu.roll(xv * sa, W - half, axis=1) + pltpu.roll(xv * sb, half, axis=1)

    row = BS((tt, W), lambda i: (i, 0))
    tab = BS((tt, LANES), lambda i: (i, 0))
    return _pcall(body, name=name, grid=(T // tt,), in_specs=[row, tab, tab, tab], out_specs=row,
                  out_shape=SDS((T, W), F32), compiler_params=_cparams(("parallel",)))(x, *tabs)


@functools.partial(jax.custom_vjp, nondiff_argnums=(2,))
def rope(x, tabs, name):
    return _rope_call(x, tabs, False, name)


def _rope_fwd(x, tabs, name):
    return _rope_call(x, tabs, False, name), tabs


def _rope_bwd(name, tabs, dy):
    return _rope_call(dy, tabs, True, name + "_bwd"), jax.tree.map(jnp.zeros_like, tabs)


rope.defvjp(_rope_fwd, _rope_bwd)


def _conv_fwd_call(x, w, b, name):
    T, C = x.shape
    tt = _tile(T, 512, 8)
    per = tt // 8

    def body(x_ref, halo_ref, w_ref, b_ref, o_ref):
        i = pl.program_id(0)
        halo = jnp.where(i > 0, halo_ref[...], 0.0)
        e = jnp.concatenate([halo, x_ref[...]], axis=0)
        acc = b_ref[...]
        for k in (3, 2, 1):
            acc = acc + pltpu.roll(e, k, axis=0)[8:, :] * w_ref[3 - k:4 - k, :]
        o_ref[...] = acc + x_ref[...] * w_ref[3:4, :]

    return _pcall(body, name=name, grid=(T // tt,),
                  in_specs=[BS((tt, C), lambda i: (i, 0)), BS((8, C), lambda i: (jnp.maximum(i * per - 1, 0), 0)),
                            BS((4, C), lambda i: (0, 0)), BS((1, C), lambda i: (0, 0))],
                  out_specs=BS((tt, C), lambda i: (i, 0)), out_shape=SDS((T, C), F32),
                  compiler_params=_cparams(("parallel",)))(x, x, w, b.reshape(1, C))


def _conv_bwd_call(x, w, dy, name):
    T, C = x.shape
    tt = _tile(T, 512, 8)
    per = tt // 8
    nt = T // tt

    def body(x_ref, halo_ref, w_ref, dy_ref, nxt_ref, dx_ref, dwb_ref):
        i = pl.program_id(0)
        halo = jnp.where(i > 0, halo_ref[...], 0.0)
        e = jnp.concatenate([halo, x_ref[...]], axis=0)
        dy = dy_ref[...]
        nxt = jnp.where(i < nt - 1, nxt_ref[...], 0.0)
        f = jnp.concatenate([dy, nxt], axis=0)
        dx = dy * w_ref[3:4, :]
        rows = [None] * 4
        rows[3] = jnp.sum(dy * x_ref[...], axis=0, keepdims=True)
        for k in (1, 2, 3):
            dx = dx + pltpu.roll(f, tt + 8 - k, axis=0)[:tt, :] * w_ref[3 - k:4 - k, :]
            rows[3 - k] = jnp.sum(dy * pltpu.roll(e, k, axis=0)[8:, :], axis=0, keepdims=True)
        dx_ref[...] = dx
        upd = jnp.concatenate(rows + [jnp.sum(dy, axis=0, keepdims=True), jnp.zeros((3, C), F32)], axis=0)

        @pl.when(i == 0)
        def _():
            dwb_ref[...] = jnp.zeros_like(dwb_ref)

        dwb_ref[...] += upd

    row = BS((tt, C), lambda i: (i, 0))
    return _pcall(body, name=name, grid=(nt,),
                  in_specs=[row, BS((8, C), lambda i: (jnp.maximum(i * per - 1, 0), 0)), BS((4, C), lambda i: (0, 0)),
                            row, BS((8, C), lambda i: (jnp.minimum((i + 1) * per, T // 8 - 1), 0))],
                  out_specs=[row, BS((8, C), lambda i: (0, 0))],
                  out_shape=[SDS((T, C), F32), SDS((8, C), F32)],
                  compiler_params=_cparams(("arbitrary",)))(x, x, w, dy, dy)


@functools.partial(jax.custom_vjp, nondiff_argnums=(3,))
def conv4(x, w, b, name):
    return _conv_fwd_call(x, w, b, name)


def _conv4_fwd(x, w, b, name):
    return _conv_fwd_call(x, w, b, name), (x, w)


def _conv4_bwd(name, res, dy):
    x, w = res
    dx, dwb = _conv_bwd_call(x, w, dy, name + "_bwd")
    return dx, dwb[0:4], dwb[4]


conv4.defvjp(_conv4_fwd, _conv4_bwd)


def _gates_fwd_call(xc, wa, ba, wx, bx, name):
    T, C = xc.shape
    hd = C // LRU_HEADS
    tt = _tile(T, 512, 8)

    def body(x_ref, wa_ref, ba_ref, wx_ref, bx_ref, ga_ref, gx_ref):
        xb = x_ref[...].astype(BF16)
        ga_ref[...] = jnp.dot(xb, wa_ref[0].astype(BF16), preferred_element_type=F32) + ba_ref[...]
        gx_ref[...] = jnp.dot(xb, wx_ref[0].astype(BF16), preferred_element_type=F32) + bx_ref[...]

    blk = BS((tt, hd), lambda i, h: (i, h))
    wsp = BS((1, hd, hd), lambda i, h: (h, 0, 0))
    bsp = BS((1, hd), lambda i, h: (0, h))
    return _pcall(body, name=name, grid=(T // tt, LRU_HEADS), in_specs=[blk, wsp, bsp, wsp, bsp],
                  out_specs=[blk, blk], out_shape=[SDS((T, C), F32)] * 2,
                  compiler_params=_cparams(("parallel", "parallel")))(xc, wa, ba.reshape(1, C), wx, bx.reshape(1, C))


def _gates_bwd_x_call(dga, dgx, wa, wx, name):
    T, C = dga.shape
    hd = C // LRU_HEADS
    tt = _tile(T, 512, 8)
    dn = (((1,), (1,)), ((), ()))

    def body(da_ref, dx_ref, wa_ref, wx_ref, o_ref):
        o_ref[...] = (lax.dot_general(da_ref[...].astype(BF16), wa_ref[0].astype(BF16), dn, preferred_element_type=F32)
                      + lax.dot_general(dx_ref[...].astype(BF16), wx_ref[0].astype(BF16), dn, preferred_element_type=F32))

    blk = BS((tt, hd), lambda i, h: (i, h))
    wsp = BS((1, hd, hd), lambda i, h: (h, 0, 0))
    return _pcall(body, name=name, grid=(T // tt, LRU_HEADS), in_specs=[blk, blk, wsp, wsp], out_specs=blk,
                  out_shape=SDS((T, C), F32), compiler_params=_cparams(("parallel", "parallel")))(dga, dgx, wa, wx)


def _gates_bwd_w_call(xc, dga, dgx, name):
    T, C = xc.shape
    hd = C // LRU_HEADS
    tt = _tile(T, 512, 8)
    dn = (((0,), (0,)), ((), ()))

    def body(x_ref, da_ref, dx_ref, dwa_ref, dwx_ref, dba_ref, dbx_ref):
        @pl.when(pl.program_id(1) == 0)
        def _():
            dwa_ref[...] = jnp.zeros_like(dwa_ref)
            dwx_ref[...] = jnp.zeros_like(dwx_ref)
            dba_ref[...] = jnp.zeros_like(dba_ref)
            dbx_ref[...] = jnp.zeros_like(dbx_ref)

        xb = x_ref[...].astype(BF16)
        da, dx = da_ref[...], dx_ref[...]
        dwa_ref[0] += lax.dot_general(xb, da.astype(BF16), dn, preferred_element_type=F32)
        dwx_ref[0] += lax.dot_general(xb, dx.astype(BF16), dn, preferred_element_type=F32)
        dba_ref[...] += jnp.sum(da, axis=0, keepdims=True)
        dbx_ref[...] += jnp.sum(dx, axis=0, keepdims=True)

    blk = BS((tt, hd), lambda h, i: (i, h))
    wsp = BS((1, hd, hd), lambda h, i: (h, 0, 0))
    bsp = BS((1, hd), lambda h, i: (0, h))
    return _pcall(body, name=name, grid=(LRU_HEADS, T // tt), in_specs=[blk, blk, blk],
                  out_specs=[wsp, wsp, bsp, bsp],
                  out_shape=[SDS((LRU_HEADS, hd, hd), F32)] * 2 + [SDS((1, C), F32)] * 2,
                  compiler_params=_cparams(("parallel", "arbitrary")))(xc, dga, dgx)


@functools.partial(jax.custom_vjp, nondiff_argnums=(7,))
def lru_gates(xc, wa, wa_c, ba, wx, wx_c, bx, name):
    return tuple(_gates_fwd_call(xc, wa, ba, wx, bx, name))


def _lru_gates_fwd(xc, wa, wa_c, ba, wx, wx_c, bx, name):
    return tuple(_gates_fwd_call(xc, wa, ba, wx, bx, name)), (xc, wa, wx, ba.shape)


def _lru_gates_bwd(name, res, g):
    xc, wa, wx, bshape = res
    dga, dgx = g
    dxc = _gates_bwd_x_call(dga, dgx, wa, wx, name + "_dx")
    dwa, dwx, dba, dbx = _gates_bwd_w_call(xc, dga, dgx, name + "_dw")
    return dxc, jnp.zeros_like(wa), dwa, dba.reshape(bshape), jnp.zeros_like(wx), dwx, dbx.reshape(bshape)


lru_gates.defvjp(_lru_gates_fwd, _lru_gates_bwd)


def _lru_coeffs(xc, ga, gx, lam):
    r = _sigmoid(ga)
    ig = _sigmoid(gx)
    z = -lam
    sp = jnp.maximum(z, 0.0) + jnp.log(1.0 + jnp.exp(-jnp.abs(z)))
    la = -LRU_C * r * sp
    a = jnp.exp(la)
    s = jnp.sqrt(-_expm1(2.0 * la))
    return r, ig, sp, a, s


LRU_TT = 256


def _scan_fwd_call(xc, ga, gx, y, lam, name):
    T, C = xc.shape
    tt = _tile(T, LRU_TT, 8)

    def body(xc_ref, ga_ref, gx_ref, y_ref, lam_ref, h_ref, rec_ref, a_buf, carry):
        @pl.when(pl.program_id(0) == 0)
        def _():
            carry[...] = jnp.zeros_like(carry)

        xcv = xc_ref[...]
        _, ig, _, a, s = _lru_coeffs(xcv, ga_ref[...], gx_ref[...], lam_ref[...])
        a_buf[...] = a
        h_ref[...] = s * (ig * xcv)

        def step(t, h):
            hn = a_buf[pl.ds(t, 1), :] * h + h_ref[pl.ds(t, 1), :]
            h_ref[pl.ds(t, 1), :] = hn
            return hn

        carry[0:1, :] = lax.fori_loop(0, tt, step, carry[0:1, :], unroll=8)
        g, _ = _gelu_parts(y_ref[...])
        rec_ref[...] = h_ref[...] * g

    row = BS((tt, C), lambda i: (i, 0))
    vec = BS((1, C), lambda i: (0, 0))
    return _pcall(body, name=name, grid=(T // tt,), in_specs=[row, row, row, row, vec], out_specs=[row, row],
                  out_shape=[SDS((T, C), F32)] * 2,
                  scratch_shapes=[pltpu.VMEM((tt, C), F32), pltpu.VMEM((8, C), F32)],
                  compiler_params=_cparams(("arbitrary",)))(xc, ga, gx, y, lam.reshape(1, C))


def _scan_bwd_call(xc, ga, gx, y, lam, h, drec, name):
    T, C = xc.shape
    tt = _tile(T, LRU_TT, 8)
    nt = T // tt
    per = tt // 8

    def body(xc_ref, ga_ref, gx_ref, y_ref, lam_ref, h_ref, halo_ref, dr_ref,
             dga_ref, dgx_ref, dxc_ref, dy_ref, dlam_ref, a_buf, g_buf, carry):
        i = pl.program_id(0)

        @pl.when(i == 0)
        def _():
            carry[...] = jnp.zeros_like(carry)
            dlam_ref[...] = jnp.zeros_like(dlam_ref)

        xcv, lam = xc_ref[...], lam_ref[...]
        r, ig, sp, a, s = _lru_coeffs(xcv, ga_ref[...], gx_ref[...], lam)
        gel, dgel = _gelu_parts(y_ref[...])
        drec = dr_ref[...]
        hv = h_ref[...]
        dy_ref[...] = drec * hv * dgel
        a_buf[...] = a
        g_buf[...] = drec * gel

        def step(j, q):
            t = tt - 1 - j
            g = g_buf[pl.ds(t, 1), :] + q
            g_buf[pl.ds(t, 1), :] = g
            return a_buf[pl.ds(t, 1), :] * g

        carry[0:1, :] = lax.fori_loop(0, tt, step, carry[0:1, :], unroll=8)
        g = g_buf[...]
        halo = jnp.where(i < nt - 1, halo_ref[...], 0.0)
        hprev = pltpu.roll(jnp.concatenate([halo, hv], axis=0), 1, axis=0)[8:, :]
        da = g * hprev
        dig = g * s * xcv
        ds = g * ig * xcv
        dla = da * a - ds * (a * a) / s
        dga_ref[...] = dla * (-LRU_C * sp) * r * (1.0 - r)
        dgx_ref[...] = dig * ig * (1.0 - ig)
        dxc_ref[...] = g * s * ig
        dlam_ref[...] += jnp.sum(dla * r, axis=0, keepdims=True) * (LRU_C * _sigmoid(-lam))

    row = BS((tt, C), lambda i: (nt - 1 - i, 0))
    vec = BS((1, C), lambda i: (0, 0))
    halo = BS((8, C), lambda i: (jnp.maximum((nt - 1 - i) * per - 1, 0), 0))
    return _pcall(body, name=name, grid=(nt,), in_specs=[row, row, row, row, vec, row, halo, row],
                  out_specs=[row, row, row, row, vec], out_shape=[SDS((T, C), F32)] * 4 + [SDS((1, C), F32)],
                  scratch_shapes=[pltpu.VMEM((tt, C), F32), pltpu.VMEM((tt, C), F32), pltpu.VMEM((8, C), F32)],
                  compiler_params=_cparams(("arbitrary",)))(xc, ga, gx, y, lam.reshape(1, C), h, h, drec)


@functools.partial(jax.custom_vjp, nondiff_argnums=(5,))
def lru_scan(xc, ga, gx, y, lam, name):
    return _scan_fwd_call(xc, ga, gx, y, lam, name)[1]


def _lru_scan_fwd(xc, ga, gx, y, lam, name):
    h, rec = _scan_fwd_call(xc, ga, gx, y, lam, name)
    return rec, (xc, ga, gx, y, lam, h)


def _lru_scan_bwd(name, res, drec):
    xc, ga, gx, y, lam, h = res
    dga, dgx, dxc, dy, dlam = _scan_bwd_call(xc, ga, gx, y, lam, h, drec, name + "_bwd")
    return dxc, dga, dgx, dy, dlam.reshape(lam.shape)


lru_scan.defvjp(_lru_scan_fwd, _lru_scan_bwd)


def _att_batch(d):
    return (4, 1) if d == 1 else (1, 4)


def _att_masks(n, max_dist):
    qi = lax.broadcasted_iota(jnp.int32, (1, 2 * ATT_BLOCK, 2 * ATT_BLOCK), 1) & (ATT_BLOCK - 1)
    kj = lax.broadcasted_iota(jnp.int32, (1, 2 * ATT_BLOCK, 2 * ATT_BLOCK), 2)
    prev = (kj < ATT_BLOCK) & (kj >= qi + (ATT_BLOCK - max_dist)) & (n > 0)
    cur = (kj >= ATT_BLOCK) & (kj - ATT_BLOCK <= qi)
    return prev | cur


def _lo_lanes(rows):
    return lax.broadcasted_iota(jnp.int32, (rows, LANES), 1) < HEAD_DIM


def _lane_half(rows):
    return lax.broadcasted_iota(jnp.int32, (rows, LANES), 1) // HEAD_DIM


def _stack_heads(x2):
    lo = _lo_lanes(ATT_BLOCK)
    zero = jnp.zeros_like(x2)
    return jnp.concatenate([jnp.where(lo, x2, zero), jnp.where(lo, zero, x2)], axis=0)


def _unstack_heads(y):
    return jnp.where(_lo_lanes(ATT_BLOCK), y[:ATT_BLOCK], y[ATT_BLOCK:])


def _per_head_col(x2):
    return jnp.concatenate([x2[:, 0:1], x2[:, HEAD_DIM:HEAD_DIM + 1]], axis=0)


def _head_sums(x2):
    lo = _lo_lanes(ATT_BLOCK)
    return jnp.concatenate([jnp.sum(jnp.where(lo, x2, 0.0), axis=1, keepdims=True),
                            jnp.sum(jnp.where(lo, 0.0, x2), axis=1, keepdims=True)], axis=0)


def _att_specs(d, Wq, Wk, nb, clamp):
    cgw, sb = _att_batch(d)
    shared = Wk != Wq
    cur = (lambda n: jnp.minimum(n, nb - 1)) if clamp else (lambda n: n)
    rows, qw, kw = ATT_BLOCK * d, cgw * LANES, (LANES if shared else cgw * LANES)
    kcol = (lambda g: 0) if shared else (lambda g: g)
    qsp = BS((rows, qw), lambda g, n: (cur(n), g))
    csp = BS((rows, kw), lambda g, n: (cur(n), kcol(g)))
    psp = BS((rows, kw), lambda g, n: (jnp.maximum(cur(n) - 1, 0), kcol(g)))
    return cgw, sb, shared, qsp, csp, psp, qw, kw


def _att_streams(d, sb, work):
    if d == 1:
        work([slice(None)])
        return

    def one(j, carry):
        work([pl.ds(j * sb + i, ATT_BLOCK, stride=d) for i in range(sb)])
        return carry

    lax.fori_loop(0, d // sb, one, 0)


def _att_problem_loads(rows, cgw, shared, g, q_ref, kc_ref, kp_ref, vc_ref, vp_ref, sk_ref):
    half = _lane_half(ATT_BLOCK)

    def kv(ref, r, p):
        x = ref[r, :]
        if not shared:
            return x[:, p * LANES:(p + 1) * LANES]
        return jnp.where(half == g, x, pltpu.roll(x, HEAD_DIM, axis=1))

    qs, kb, vb, sk = [], [], [], []
    for r in rows:
        qrow = q_ref[r, :]
        for p in range(cgw):
            cols = slice(p * LANES, (p + 1) * LANES)
            qs.append(_stack_heads(qrow[:, cols].astype(BF16)))
            kb.append(jnp.concatenate([kv(kp_ref, r, p), kv(kc_ref, r, p)], axis=0).astype(BF16))
            vb.append(jnp.concatenate([kv(vp_ref, r, p), kv(vc_ref, r, p)], axis=0).astype(BF16))
            sk.append(_per_head_col(jnp.broadcast_to(sk_ref[:, cols], (ATT_BLOCK, LANES))))
    return jnp.stack(qs), jnp.stack(kb), jnp.stack(vb), jnp.stack(sk)


_BDOT_NT = (((2,), (2,)), ((0,), (0,)))
_BDOT_NN = (((2,), (1,)), ((0,), (0,)))
_BDOT_TN = (((1,), (1,)), ((0,), (0,)))


def _att_fwd_call(q, k, v, sinks, d, max_dist, name):
    T, Wq = q.shape
    Wk = k.shape[1]
    nb = T // (d * ATT_BLOCK)
    cgw, sb, shared, qsp, csp, psp, qw, kw = _att_specs(d, Wq, Wk, nb, False)
    G = Wq // qw
    assert not shared or (d == 1 and Wk == LANES and G == 2), "a shared kv pair serves two groups of 8 query heads"

    def body(q_ref, kc_ref, kp_ref, vc_ref, vp_ref, sk_ref, o_ref, lse_ref):
        g, n = pl.program_id(0), pl.program_id(1)

        def work(rows):
            qs, kband, vband, sk = _att_problem_loads(rows, cgw, shared, g, q_ref, kc_ref, kp_ref, vc_ref, vp_ref, sk_ref)
            s = lax.dot_general(qs, kband, _BDOT_NT, preferred_element_type=F32) * (HEAD_DIM ** -0.5)
            s = jnp.where(_att_masks(n, max_dist), s, NEG)
            m = jnp.maximum(jnp.max(s, axis=-1, keepdims=True), sk)
            e = jnp.exp(s - m)
            den = jnp.sum(e, axis=-1, keepdims=True) + jnp.exp(sk - m)
            o = lax.dot_general((e * (1.0 / den)).astype(BF16), vband, _BDOT_NN, preferred_element_type=F32)
            lse = jnp.broadcast_to(m + jnp.log(den), o.shape)
            for i, r in enumerate(rows):
                o_ref[r, :] = jnp.concatenate([_unstack_heads(o[i * cgw + p]) for p in range(cgw)], axis=1)
                lse_ref[r, :] = jnp.concatenate([_unstack_heads(lse[i * cgw + p]) for p in range(cgw)], axis=1)

        _att_streams(d, sb, work)

    sksp = BS((1, qw), lambda g, n: (0, g))
    return _pcall(body, name=name, grid=(G, nb), in_specs=[qsp, csp, psp, csp, psp, sksp], out_specs=[qsp, qsp],
                  out_shape=[SDS((T, Wq), F32)] * 2,
                  compiler_params=_cparams(("parallel", "parallel")))(q, k, k, v, v, sinks)


def _att_bwd_call(q, k, v, sinks, o, lse, do, dlse, d, max_dist, name):
    T, Wq = q.shape
    Wk = k.shape[1]
    nb = T // (d * ATT_BLOCK)
    cgw, sb, shared, qsp, csp, psp, qw, kw = _att_specs(d, Wq, Wk, nb, True)
    G = Wq // qw
    scale = HEAD_DIM ** -0.5

    def body(q_ref, kc_ref, kp_ref, vc_ref, vp_ref, sk_ref, o_ref, lse_ref, do_ref, dlse_ref,
             dq_ref, dk_ref, dv_ref, dsk_ref, ck, cv):
        g, n = pl.program_id(0), pl.program_id(1)

        @pl.when(n == 0)
        def _():
            ck[...] = jnp.zeros_like(ck)
            cv[...] = jnp.zeros_like(cv)
            dsk_ref[...] = jnp.zeros_like(dsk_ref)

        def put(ref, r, val):
            if len(ref.shape) == 3:
                ref[0, r, :] = val
            else:
                ref[r, :] = val

        def work(rows):
            qs, kband, vband, sk = _att_problem_loads(rows, cgw, shared, g, q_ref, kc_ref, kp_ref, vc_ref, vp_ref, sk_ref)
            dos, lse_c, corr = [], [], []
            for r in rows:
                do_r, o_r, lse_r, dlse_r = do_ref[r, :], o_ref[r, :], lse_ref[r, :], dlse_ref[r, :]
                for p in range(cgw):
                    cols = slice(p * LANES, (p + 1) * LANES)
                    dos.append(_stack_heads(do_r[:, cols].astype(BF16)))
                    lse_c.append(_per_head_col(lse_r[:, cols]))
                    corr.append(_head_sums(dlse_r[:, cols]) - _head_sums(do_r[:, cols] * o_r[:, cols]))
            dos, lse_c, corr = jnp.stack(dos), jnp.stack(lse_c), jnp.stack(corr)
            s = lax.dot_general(qs, kband, _BDOT_NT, preferred_element_type=F32) * scale
            pr = jnp.exp(jnp.where(_att_masks(n, max_dist), s, NEG) - lse_c)
            dp = lax.dot_general(dos, vband, _BDOT_NT, preferred_element_type=F32)
            ds = (pr * (dp + corr)).astype(BF16)
            dq = lax.dot_general(ds, kband, _BDOT_NN, preferred_element_type=F32) * scale
            dkb = lax.dot_general(ds, qs, _BDOT_TN, preferred_element_type=F32) * scale
            dvb = lax.dot_general(pr.astype(BF16), dos, _BDOT_TN, preferred_element_type=F32)
            dsk = jnp.exp(sk - lse_c) * corr
            lane = lax.broadcasted_iota(jnp.int32, (8, LANES), 1)
            for p in range(cgw):
                tot = [jnp.sum(jnp.stack([dsk[i * cgw + p, h * ATT_BLOCK:(h + 1) * ATT_BLOCK] for i in range(len(rows))]),
                               axis=(0, 1)).reshape(1, 1) for h in range(2)]
                dsk_ref[:, p * LANES:(p + 1) * LANES] += jnp.where(lane == 0, tot[0], jnp.where(lane == HEAD_DIM, tot[1], 0.0))

            def gather_pairs(parts):
                if not shared:
                    return jnp.concatenate(parts, axis=1)
                tot = parts[0]
                for t in parts[1:]:
                    tot = tot + t
                tot = tot + pltpu.roll(tot, HEAD_DIM, axis=1)
                return jnp.where(_lane_half(ATT_BLOCK) == g, tot, 0.0)

            for i, r in enumerate(rows):
                mine = range(i * cgw, (i + 1) * cgw)
                dq_ref[r, :] = jnp.concatenate([_unstack_heads(dq[b]) for b in mine], axis=1)
                put(dk_ref, r, ck[r, :] + gather_pairs([dkb[b, :ATT_BLOCK] for b in mine]))
                put(dv_ref, r, cv[r, :] + gather_pairs([dvb[b, :ATT_BLOCK] for b in mine]))
                ck[r, :] = gather_pairs([dkb[b, ATT_BLOCK:] for b in mine])
                cv[r, :] = gather_pairs([dvb[b, ATT_BLOCK:] for b in mine])

        @pl.when(n < nb)
        def _():
            _att_streams(d, sb, work)

        @pl.when(n == nb)
        def _():
            dk_ref[...] = ck[...].reshape(dk_ref.shape)
            dv_ref[...] = cv[...].reshape(dv_ref.shape)

    sksp = BS((1, qw), lambda g, n: (0, g))
    rows = ATT_BLOCK * d
    if shared:
        osp = BS((1, rows, kw), lambda g, n: (g, jnp.maximum(n - 1, 0), 0))
        kshape = SDS((G, T, Wk), F32)
    else:
        osp = BS((rows, kw), lambda g, n: (jnp.maximum(n - 1, 0), g))
        kshape = SDS((T, Wk), F32)
    dq, dk, dv, dsk = _pcall(
        body, name=name, grid=(G, nb + 1),
        in_specs=[qsp, csp, psp, csp, psp, sksp, qsp, qsp, qsp, qsp],
        out_specs=[qsp, osp, osp, BS((8, qw), lambda g, n: (0, g))],
        out_shape=[SDS((T, Wq), F32), kshape, kshape, SDS((8, Wq), F32)],
        scratch_shapes=[pltpu.VMEM((rows, kw), F32)] * 2,
        compiler_params=_cparams(("parallel", "arbitrary")))(q, k, k, v, v, sinks, o, lse, do, dlse)
    if shared:
        dk, dv = jnp.sum(dk, axis=0), jnp.sum(dv, axis=0)
    return dq, dk, dv, dsk[0:1]


@functools.partial(jax.custom_vjp, nondiff_argnums=(4, 5, 6))
def band_attention(q, k, v, sinks, d, max_dist, name):
    return _att_fwd_call(q, k, v, sinks, d, max_dist, name)


def _band_attention_fwd(q, k, v, sinks, d, max_dist, name):
    o, lse = _att_fwd_call(q, k, v, sinks, d, max_dist, name)
    return (o, lse), (q, k, v, sinks, o, lse)


def _band_attention_bwd(d, max_dist, name, res, g):
    q, k, v, sinks, o, lse = res
    do, dlse = g
    return _att_bwd_call(q, k, v, sinks, o, lse, do, dlse, d, max_dist, name + "_bwd")


band_attention.defvjp(_band_attention_fwd, _band_attention_bwd)


def _merge_weights(ls):
    mx = jnp.maximum(jnp.maximum(ls[0], ls[1]), ls[2])
    es = [jnp.exp(l - mx) for l in ls]
    inv = 1.0 / (es[0] + es[1] + es[2])
    return [e * inv for e in es]


def _merge_fwd_call(os_, ls_, name):
    T, W = os_[0].shape
    tt = _tile(T, 512, 8)

    def body(o1, o2, o3, l1, l2, l3, out_ref):
        w = _merge_weights([l1[...], l2[...], l3[...]])
        out_ref[...] = w[0] * o1[...] + w[1] * o2[...] + w[2] * o3[...]

    row = BS((tt, W), lambda i: (i, 0))
    return _pcall(body, name=name, grid=(T // tt,), in_specs=[row] * 6, out_specs=row,
                  out_shape=SDS((T, W), F32), compiler_params=_cparams(("parallel",)))(*os_, *ls_)


def _merge_bwd_call(os_, ls_, do, name):
    T, W = os_[0].shape
    tt = _tile(T, 512, 8)

    def body(o1, o2, o3, l1, l2, l3, do_ref, d1, d2, d3, e1, e2, e3):
        w = _merge_weights([l1[...], l2[...], l3[...]])
        dov = do_ref[...]
        ts = [dov * o[...] for o in (o1, o2, o3)]
        mean = w[0] * ts[0] + w[1] * ts[1] + w[2] * ts[2]
        for wi, ti, dref, eref in zip(w, ts, (d1, d2, d3), (e1, e2, e3)):
            dref[...] = wi * dov
            eref[...] = wi * (ti - mean)

    row = BS((tt, W), lambda i: (i, 0))
    return _pcall(body, name=name, grid=(T // tt,), in_specs=[row] * 7, out_specs=[row] * 6,
                  out_shape=[SDS((T, W), F32)] * 6, compiler_params=_cparams(("parallel",)))(*os_, *ls_, do)


@functools.partial(jax.custom_vjp, nondiff_argnums=(2,))
def merge3(os_, ls_, name):
    return _merge_fwd_call(os_, ls_, name)


def _merge3_fwd(os_, ls_, name):
    return _merge_fwd_call(os_, ls_, name), (os_, ls_)


def _merge3_bwd(name, res, do):
    os_, ls_ = res
    out = _merge_bwd_call(os_, ls_, do, name + "_bwd")
    return tuple(out[:3]), tuple(out[3:])


merge3.defvjp(_merge3_fwd, _merge3_bwd)


def _xa_probs(qb, kb, scale):
    s = lax.dot_general(qb, kb, (((1,), (1,)), ((), ())), preferred_element_type=F32) * scale
    e = jnp.exp(s - jnp.max(s, axis=-1, keepdims=True))
    return e / jnp.sum(e, axis=-1, keepdims=True)


def _xa_fwd_call(q, kv, name):
    T, W = q.shape
    M = kv.shape[0]
    hd = XA_HEAD_DIM
    tq = _tile(T, 512, 8)
    scale = hd ** -0.5

    def body(q_ref, k_ref, v_ref, o_ref):
        p = _xa_probs(q_ref[...].astype(BF16), k_ref[...].astype(BF16), scale)
        o_ref[...] = jnp.dot(p.astype(BF16), v_ref[...].astype(BF16), preferred_element_type=F32)

    qsp = BS((tq, hd), lambda i, h: (i, h))
    return _pcall(body, name=name, grid=(T // tq, XA_HEADS),
                  in_specs=[qsp, BS((M, hd), lambda i, h: (0, h)), BS((M, hd), lambda i, h: (0, XA_HEADS + h))],
                  out_specs=qsp, out_shape=SDS((T, W), F32),
                  compiler_params=_cparams(("parallel", "parallel")))(q, kv, kv)


def _xa_bwd_call(q, kv, do, name):
    T, W = q.shape
    M = kv.shape[0]
    hd = XA_HEAD_DIM
    tq = _tile(T, 512, 8)
    scale = hd ** -0.5
    dn_nt = (((1,), (1,)), ((), ()))
    dn_tn = (((0,), (0,)), ((), ()))

    def body(q_ref, k_ref, v_ref, do_ref, dq_ref, dk_ref, dv_ref):
        @pl.when(pl.program_id(1) == 0)
        def _():
            dk_ref[...] = jnp.zeros_like(dk_ref)
            dv_ref[...] = jnp.zeros_like(dv_ref)

        qb, kb, vb = q_ref[...].astype(BF16), k_ref[...].astype(BF16), v_ref[...].astype(BF16)
        p = _xa_probs(qb, kb, scale)
        dob = do_ref[...].astype(BF16)
        dp = lax.dot_general(dob, vb, dn_nt, preferred_element_type=F32)
        ds = (p * (dp - jnp.sum(p * dp, axis=-1, keepdims=True))).astype(BF16)
        dq_ref[...] = jnp.dot(ds, kb, preferred_element_type=F32) * scale
        dk_ref[...] += lax.dot_general(ds, qb, dn_tn, preferred_element_type=F32) * scale
        dv_ref[...] += lax.dot_general(p.astype(BF16), dob, dn_tn, preferred_element_type=F32)

    qsp = BS((tq, hd), lambda h, i: (i, h))
    ksp = BS((M, hd), lambda h, i: (0, h))
    return _pcall(body, name=name, grid=(XA_HEADS, T // tq),
                  in_specs=[qsp, ksp, BS((M, hd), lambda h, i: (0, XA_HEADS + h)), qsp],
                  out_specs=[qsp, ksp, ksp], out_shape=[SDS((T, W), F32), SDS((M, W), F32), SDS((M, W), F32)],
                  compiler_params=_cparams(("parallel", "arbitrary")))(q, kv, kv, do)


@functools.partial(jax.custom_vjp, nondiff_argnums=(2,))
def cross_attention(q, kv, name):
    return _xa_fwd_call(q, kv, name)


def _cross_attention_fwd(q, kv, name):
    return _xa_fwd_call(q, kv, name), (q, kv)


def _cross_attention_bwd(name, res, do):
    q, kv = res
    dq, dk, dv = _xa_bwd_call(q, kv, do, name + "_bwd")
    return dq, jnp.concatenate([dk, dv], axis=1)


cross_attention.defvjp(_cross_attention_fwd, _cross_attention_bwd)


def _swiglu_fwd_call(gu, name):
    T, W2 = gu.shape
    F = W2 // 2
    tt = _tile(T, 256, 16)

    def body(g_ref, u_ref, o_ref):
        g = g_ref[...]
        o_ref[...] = ((g * _sigmoid(g)) * u_ref[...]).astype(BF16)

    return _pcall(body, name=name, grid=(T // tt,),
                  in_specs=[BS((tt, F), lambda i: (i, 0)), BS((tt, F), lambda i: (i, 1))],
                  out_specs=BS((tt, F), lambda i: (i, 0)), out_shape=SDS((T, F), BF16),
                  compiler_params=_cparams(("parallel",)))(gu, gu)


def _swiglu_bwd_call(gu, dact, name):
    T, W2 = gu.shape
    F = W2 // 2
    tt = _tile(T, 128, 16)

    def body(g_ref, u_ref, d_ref, o_ref):
        g, d = g_ref[...], d_ref[...]
        sg = _sigmoid(g)
        o_ref[:, :F] = (d * u_ref[...] * (sg * (1.0 + g * (1.0 - sg)))).astype(BF16)
        o_ref[:, F:] = (d * (g * sg)).astype(BF16)

    return _pcall(body, name=name, grid=(T // tt,),
                  in_specs=[BS((tt, F), lambda i: (i, 0)), BS((tt, F), lambda i: (i, 1)), BS((tt, F), lambda i: (i, 0))],
                  out_specs=BS((tt, W2), lambda i: (i, 0)), out_shape=SDS((T, W2), BF16),
                  compiler_params=_cparams(("parallel",)))(gu, gu, dact)


@functools.partial(jax.custom_vjp, nondiff_argnums=(6,))
def ffn_block(h, g, w1b, w1c, w2b, w2c, name):
    return _ffn_fwd(h, g, w1b, w1c, w2b, w2c, name)[0]


def _ffn_fwd(h, g, w1b, w1c, w2b, w2c, name):
    hn = _rms_fwd_call(h, g, name + "_norm", BF16)
    gu = _mm(hn, w1b, name=name + "_gu", tm=1024, tn=1408, tk=1024)
    act = _swiglu_fwd_call(gu, name + "_swiglu")
    out = _mm(act, w2b, add=h, name=name + "_down", tm=1024, tn=1024, tk=1408)
    return out, (h, g, hn, gu, act, w1b, w2b)


def _ffn_bwd(name, res, dout):
    h, g, hn, gu, act, w1b, w2b = res
    dact = _mm(dout, w2b, tb=True, name=name + "_down_da", tm=1024, tn=1408, tk=1024)
    dw2 = _mm(act, dout, ta=True, name=name + "_down_dw", tm=1408, tn=1024, tk=1024)
    dgu = _swiglu_bwd_call(gu, dact, name + "_swiglu_bwd")
    dhn = _mm(dgu, w1b, tb=True, name=name + "_gu_da", tm=1024, tn=1024, tk=1408)
    dw1 = _mm(hn, dgu, ta=True, name=name + "_gu_dw", tm=1024, tn=1408, tk=1024)
    dh, dg = _rms_bwd_call(h, g, dhn, name + "_norm_bwd", add=dout)
    return dh, dg.reshape(g.shape), jnp.zeros_like(w1b), dw1, jnp.zeros_like(w2b), dw2


ffn_block.defvjp(_ffn_fwd, _ffn_bwd)


def _final_call(h, g, target, name):
    T, Dm = h.shape
    tt = _tile(T, 512, 8)

    def body(x_ref, g_ref, t_ref, loss_ref, dx_ref, dg_ref):
        @pl.when(pl.program_id(0) == 0)
        def _():
            loss_ref[...] = jnp.zeros_like(loss_ref)
            dg_ref[...] = jnp.zeros_like(dg_ref)

        xv, gv = x_ref[...], g_ref[...]
        r = lax.rsqrt(jnp.mean(xv * xv, axis=-1, keepdims=True) + NORM_EPS)
        xh = xv * r
        err = xh * gv - t_ref[...]
        loss_ref[...] += 0.5 * jnp.sum(jnp.mean(err * err, axis=-1, keepdims=True), axis=0, keepdims=True)
        dy = err * (1.0 / Dm)
        dyg = dy * gv
        dx_ref[...] = r * (dyg - xh * jnp.mean(dyg * xh, axis=-1, keepdims=True))
        dg_ref[...] += jnp.sum(dy * xh, axis=0, keepdims=True)

    row = BS((tt, Dm), lambda i: (i, 0))
    vec = BS((1, Dm), lambda i: (0, 0))
    return _pcall(body, name=name, grid=(T // tt,), in_specs=[row, vec, row],
                  out_specs=[BS((1, 1), lambda i: (0, 0)), row, vec],
                  out_shape=[SDS((1, 1), F32), SDS((T, Dm), F32), SDS((1, Dm), F32)],
                  compiler_params=_cparams(("arbitrary",)))(h, g.reshape(1, Dm), target)


ADAMW_BLOCK_ELEMS = 64 * 1024


def _adamw_call(parts, w, m, v, name):
    shape = w.shape
    if not isinstance(parts, (list, tuple)):
        parts, shape3 = [parts], (1,) + shape
    else:
        shape3 = shape
    n_lead = shape3[0]
    r, N = shape3[-2], shape3[-1]
    Ld = math.prod(shape3[1:-2])
    w, m, v = (t.reshape(n_lead * Ld, r, N) for t in (w, m, v))
    tr = _tile(r, max(8, ADAMW_BLOCK_ELEMS // N), 8)
    c1 = 1.0 - ADAM_B1 ** ADAM_STEP
    c2 = 1.0 - ADAM_B2 ** ADAM_STEP
    outs = None
    for lead, p in enumerate(parts):
        def body(p_ref, w_ref, m_ref, v_ref, *rest):
            g_ref, d_ref, nm_ref, nv_ref = rest[-4:]
            g = p_ref[0]
            for j in range(1, N_DEV):
                g = g + p_ref[j]
            nm = ADAM_B1 * m_ref[...] + (1.0 - ADAM_B1) * g
            nv = ADAM_B2 * v_ref[...] + (1.0 - ADAM_B2) * (g * g)
            g_ref[...] = g
            nm_ref[...] = nm
            nv_ref[...] = nv
            d_ref[...] = -ADAM_LR * ((nm / c1) / (jnp.sqrt(nv / c2) + ADAM_EPS) + ADAM_WD * w_ref[...])

        base = lead * Ld
        row = BS((1, tr, N), lambda l, i, base=base: (base + l, i, 0))
        prev = [] if outs is None else list(outs)
        outs = _pcall(body, name=f"{name}_{lead}", grid=(Ld, r // tr),
                      in_specs=[BS((N_DEV, 1, tr, N), lambda l, i: (0, l, i, 0)), row, row, row]
                      + [BS(memory_space=pl.ANY)] * len(prev),
                      out_specs=[row] * 4, out_shape=[SDS((n_lead * Ld, r, N), F32)] * 4,
                      input_output_aliases={4 + j: j for j in range(len(prev))},
                      compiler_params=_cparams(("parallel", "parallel")))(p.reshape(N_DEV, Ld, r, N), w, m, v, *prev)
    return [t.reshape(shape) for t in outs]


def _place():
    return lax.axis_index("x"), lax.axis_index("y"), lax.axis_index("c")


def _all_gather(xs, name):
    n = len(xs)
    pairs = [(i, l) for i, x in enumerate(xs) for l in range(x.shape[0])]

    def body(*refs):
        x_refs, o_refs = refs[:n], refs[n:2 * n]
        send_sems, recv_sems, local_sems = refs[2 * n:]
        x_, y_, c_ = _place()
        me, sibling = (x_, y_, c_), (x_, y_, 1 - c_)
        chips = [(1 - x_, y_), (x_, 1 - y_), (1 - x_, 1 - y_)]

        def copy(e, k, block, to, from_input=False):
            i, l = pairs[e]
            px, py, pc = block
            dst = o_refs[i].at[l, 4 * px + 2 * py + pc]
            return pltpu.make_async_remote_copy(
                src_ref=x_refs[i].at[l] if from_input else dst, dst_ref=dst,
                send_sem=send_sems.at[7 * e + k], recv_sem=recv_sems.at[7 * e + k],
                device_id=to, device_id_type=pl.DeviceIdType.MESH)

        every = range(len(pairs))
        mine = [pltpu.make_async_copy(x_refs[i].at[l], o_refs[i].at[l, 4 * x_ + 2 * y_ + c_], local_sems.at[e])
                for e, (i, l) in enumerate(pairs)]
        for cp in mine:
            cp.start()
        first = [copy(e, 0, me, sibling, True) for e in every]
        first += [copy(e, 1 + j, me, (*chip, c_), True) for j, chip in enumerate(chips) for e in every]
        for cp in first:
            cp.start()
        passed = []
        for j, chip in enumerate(chips):
            for e in every:
                copy(e, 1 + j, (*chip, c_), me).wait_recv()
            for e in every:
                cp = copy(e, 4 + j, (*chip, c_), sibling)
                cp.start()
                passed.append(cp)
        for e in every:
            copy(e, 0, sibling, me).wait_recv()
        for j, chip in enumerate(chips):
            for e in every:
                copy(e, 4 + j, (*chip, 1 - c_), me).wait_recv()
        for cp in first + passed:
            cp.wait_send()
        for cp in mine:
            cp.wait()

    any_spec = BS(memory_space=pl.ANY)
    return _pcall(body, name=name, in_specs=[any_spec] * n, out_specs=[any_spec] * n,
                  out_shape=[SDS((x.shape[0], N_DEV) + x.shape[1:], x.dtype) for x in xs],
                  scratch_shapes=[pltpu.SemaphoreType.DMA((7 * len(pairs),)), pltpu.SemaphoreType.DMA((7 * len(pairs),)),
                                  pltpu.SemaphoreType.DMA((len(pairs),))],
                  compiler_params=pltpu.CompilerParams(has_side_effects=True))(*xs)


def _peer_of(k, place):
    x_, y_, c_ = place
    fx, fy, fc = (k >> 2) & 1, (k >> 1) & 1, k & 1
    return fx + x_ - 2 * fx * x_, fy + y_ - 2 * fy * y_, fc + c_ - 2 * fc * c_


def _split_copy(src_ref, land_ref, send_sems, recv_sems, e, k, place, scatter):
    x_, y_, c_ = place
    px, py, pc = _peer_of(k, place)
    return pltpu.make_async_remote_copy(
        src_ref=src_ref.at[4 * px + 2 * py + pc] if scatter else src_ref, dst_ref=land_ref.at[4 * x_ + 2 * y_ + c_],
        send_sem=send_sems.at[7 * e + k - 1], recv_sem=recv_sems.at[7 * e + k - 1],
        device_id=(px, py, pc), device_id_type=pl.DeviceIdType.MESH)


_HBM_SPEC = BS(memory_space=pltpu.HBM)
_SEM_SPEC = BS(memory_space=pltpu.SEMAPHORE)
_EFFECT = pltpu.SideEffectType.DATAFLOW_SIDE_EFFECTING


def _copies_start(srcs, scatter, name, thru=None):
    n = len(srcs)
    lands = [lax.empty(s.shape if scatter else (N_DEV,) + s.shape, s.dtype) for s in srcs]
    passed = srcs + lands + ([] if thru is None else [thru])

    def body(*refs):
        src_refs, land_refs = refs[:n], refs[n:2 * n]
        send_sems, recv_sems = refs[len(passed)], refs[len(passed) + 1]
        token = refs[-1]
        place = _place()
        for e in range(n):
            for k in range(1, N_DEV):
                _split_copy(src_refs[e], land_refs[e], send_sems, recv_sems, e, k, place, scatter).start()
        token[...] = jnp.zeros_like(token)

    hbm = lambda t: pltpu.with_memory_space_constraint(t, pltpu.HBM)
    out = _pcall(body, name=name,
                 out_shape=(pltpu.SemaphoreType.DMA((7 * n,)), pltpu.SemaphoreType.DMA((7 * n,)),
                            *[pltpu.HBM(t.shape, t.dtype) for t in passed], SDS((8, LANES), F32)),
                 in_specs=[_HBM_SPEC] * len(passed),
                 out_specs=(_SEM_SPEC, _SEM_SPEC, *[_HBM_SPEC] * len(passed), BS(memory_space=pltpu.VMEM)),
                 input_output_aliases={i: 2 + i for i in range(len(passed))},
                 compiler_params=pltpu.CompilerParams(has_side_effects=_EFFECT))(*[hbm(t) for t in passed])
    return out[0], out[1], list(out[2:2 + n]), list(out[2 + n:2 + 2 * n]), (None if thru is None else out[2 + 2 * n])


def _copies_wait(started, which, scatter, after, name):
    send_sems, recv_sems, srcs, lands, _ = started
    n = len(which)

    def body(*refs):
        src_refs, land_refs = refs[:n], refs[n:2 * n]
        send_s, recv_s = refs[2 * n], refs[2 * n + 1]
        place = _place()
        for j, e in enumerate(which):
            for k in range(1, N_DEV):
                cp = _split_copy(src_refs[j], land_refs[j], send_s, recv_s, e, k, place, scatter)
                cp.wait_send()
                cp.wait_recv()

    args = [srcs[e] for e in which] + [lands[e] for e in which]
    out = _pcall(body, name=name, out_shape=tuple(pltpu.HBM(t.shape, t.dtype) for t in args),
                 in_specs=[_HBM_SPEC] * (2 * n) + [_SEM_SPEC, _SEM_SPEC, BS(memory_space=pl.ANY)],
                 out_specs=tuple([_HBM_SPEC] * (2 * n)), input_output_aliases={i: i for i in range(2 * n)},
                 compiler_params=pltpu.CompilerParams(has_side_effects=_EFFECT))(*args, send_sems, recv_sems, after)
    return list(out[:n]), list(out[n:])


def _with_own_block(land, own_block):
    me = 4 * lax.axis_index("x") + 2 * lax.axis_index("y") + lax.axis_index("c")
    return lax.dynamic_update_index_in_dim(land, own_block, me, 0)


def _pad_flat(t, quantum=PACK_QUANTUM):
    f = t.reshape(-1)
    pad = (-f.shape[0]) % quantum
    return jnp.pad(f, (0, pad)) if pad else f


def _pack(arrs, dtype):
    return jnp.concatenate([_pad_flat(a.astype(dtype)) for a in arrs]).reshape(-1, LANES)


def _unpack(buf, shapes, lead=()):
    flat = buf.reshape(lead + (-1,))
    out, off = [], 0
    for s in shapes:
        n = math.prod(s)
        out.append(flat[..., off:off + n].reshape(lead + tuple(s)))
        off += n + (-n) % PACK_QUANTUM
    return out


def _full_from_gathered(g, axis):
    t = jnp.moveaxis(g, 0, axis)
    s = t.shape
    return t.reshape(s[:axis] + (s[axis] * s[axis + 1],) + s[axis + 2:])


def _parts_from_full(t, axis):
    s = t.shape
    t = t.reshape(s[:axis] + (N_DEV, s[axis] // N_DEV) + s[axis + 1:])
    return jnp.moveaxis(t, axis, 0)


def _head_rows(t):
    return jnp.repeat(t, HEAD_DIM).reshape(1, -1)


def _dilated_attention(q, k, v, name):
    no_sink = jnp.full((1, q.shape[1]), NEG, F32)
    outs, lses = zip(*[band_attention(q, k, v, no_sink, d, ATT_BLOCK, f"{name}_d{d}") for d in DILATIONS])
    return merge3(tuple(outs), tuple(lses), name + "_merge")


STAGES = (
    ("proj0", ('mix_norm',), (('ab_w_in', 0),)),
    ("mixer0", ('lru_conv_w', 'lru_conv_b', 'lru_ba', 'lru_bx', 'lru_lambda'),
     (('lru_wa', 0), ('lru_wx', 0), ('ab_w_out', 0))),
    ("xa0", ('xa_norm', 'xa_mem_norm'), (('xa_wq', 0), ('xa_wkv', 0), ('xa_wo', 0))),
    ("ffn0", ('ffn_norm',), (('ffn_w_gate_up', 0), ('ffn_w_down', 0))),
    ("mixer1", ('mix_norm', 'c_b_qkv', 'c_sinks', 'c_b_out'), (('c_w_qkv', 0), ('c_w_out', 0))),
    ("xa1", ('xa_norm', 'xa_mem_norm'), (('xa_wq', 1), ('xa_wkv', 1), ('xa_wo', 1))),
    ("ffn1", ('ffn_norm',), (('ffn_w_gate_up', 1), ('ffn_w_down', 1))),
)


def _stage_fn(stage, Wb, tabs, mem):
    layer = int(stage[-1])
    L = f"l{layer}"

    def run(S, Cw, h):
        def lin(a, key, bias, add, name, rows=None):
            wb, wc = Wb[key], Cw[key]
            if rows is not None:
                wb, wc = wb[rows], wc[rows]
            return linear(a, wb, wc, bias, add, name)

        def norm_lin(a, gain, key, bias, name):
            return norm_linear(a, gain, Wb[key], Cw[key], bias, name)

        if stage == "proj0":
            proj, h = norm_lin(h, S['mix_norm'][0], ('ab_w_in', 0), None, L + "_w_in")
            return h, proj
        if stage == "mixer0":
            h, proj = h
            C = S['lru_conv_w'].shape[-1]
            xc = conv4(proj[:, :C], S['lru_conv_w'][0], S['lru_conv_b'][0], L + "_conv")
            ga, gx = lru_gates(xc, Wb['lru_wa', 0], Cw['lru_wa', 0], S['lru_ba'][0],
                               Wb['lru_wx', 0], Cw['lru_wx', 0], S['lru_bx'][0], L + "_gates")
            rec = lru_scan(xc, ga, gx, proj[:, C:2 * C], S['lru_lambda'][0], L + "_scan")
            bw = B_HEADS * HEAD_DIM
            q = rope(proj[:, 2 * C:2 * C + bw], tabs, L + "_rope_q")
            k = rope(proj[:, 2 * C + bw:2 * C + 2 * bw], tabs, L + "_rope_k")
            v = proj[:, 2 * C + 2 * bw:]
            att = _dilated_attention(q, k, v, L + "_att")
            h = lin(att, ('ab_w_out', 0), None, h, L + "_w_out_att", slice(C, None))
            return lin(rec, ('ab_w_out', 0), None, h, L + "_w_out_rec", slice(0, C))
        if stage == "mixer1":
            qw = C_HEADS * HEAD_DIM
            kw = C_KV_HEADS * HEAD_DIM
            qkv, h = norm_lin(h, S['mix_norm'][1], ('c_w_qkv', 0), S['c_b_qkv'][0], L + "_w_qkv")
            q = rope(qkv[:, :qw], tabs, L + "_rope_q")
            k = rope(qkv[:, qw:qw + kw], tabs, L + "_rope_k")
            v = qkv[:, qw + kw:]
            o, _ = band_attention(q, k, v, _head_rows(S['c_sinks'][0]), 1, ATT_BLOCK - 1, L + "_att")
            return lin(o, ('c_w_out', 0), S['c_b_out'][0], h, L + "_w_out")
        if stage.startswith("xa"):
            xq, h = norm_lin(h, S['xa_norm'][layer], ('xa_wq', layer), None, L + "_xa_wq")
            xkv, _ = norm_lin(mem, S['xa_mem_norm'][layer], ('xa_wkv', layer), None, L + "_xa_wkv")
            return lin(cross_attention(xq, xkv, L + "_xa"), ('xa_wo', layer), None, h, L + "_xa_wo")
        gu, down = ('ffn_w_gate_up', layer), ('ffn_w_down', layer)
        return ffn_block(h, S['ffn_norm'][layer], Wb[gu], Cw[gu], Wb[down], Cw[down], L + "_ffn")

    return run


def kernel(x, mem, mix_norm, ab_w_in, lru_conv_w, lru_conv_b, lru_wa, lru_ba, lru_wx, lru_bx, lru_lambda, ab_w_out, c_w_qkv, c_b_qkv, c_sinks, c_w_out, c_b_out, xa_norm, xa_mem_norm, xa_wq, xa_wkv, xa_wo, ffn_norm, ffn_w_gate_up, ffn_w_down, final_norm, loss_target, m_mix_norm, m_ab_w_in, m_lru_conv_w, m_lru_conv_b, m_lru_wa, m_lru_ba, m_lru_wx, m_lru_bx, m_lru_lambda, m_ab_w_out, m_c_w_qkv, m_c_b_qkv, m_c_sinks, m_c_w_out, m_c_b_out, m_xa_norm, m_xa_mem_norm, m_xa_wq, m_xa_wkv, m_xa_wo, m_ffn_norm, m_ffn_w_gate_up, m_ffn_w_down, m_final_norm, v_mix_norm, v_ab_w_in, v_lru_conv_w, v_lru_conv_b, v_lru_wa, v_lru_ba, v_lru_wx, v_lru_bx, v_lru_lambda, v_ab_w_out, v_c_w_qkv, v_c_b_qkv, v_c_sinks, v_c_w_out, v_c_b_out, v_xa_norm, v_xa_mem_norm, v_xa_wq, v_xa_wkv, v_xa_wo, v_ffn_norm, v_ffn_w_gate_up, v_ffn_w_down, v_final_norm):
    w_loc = dict(zip(WEIGHT_NAMES, (mix_norm, ab_w_in, lru_conv_w, lru_conv_b, lru_wa, lru_ba, lru_wx, lru_bx, lru_lambda, ab_w_out, c_w_qkv, c_b_qkv, c_sinks, c_w_out, c_b_out, xa_norm, xa_mem_norm, xa_wq, xa_wkv, xa_wo, ffn_norm, ffn_w_gate_up, ffn_w_down, final_norm)))
    m_loc = dict(zip(WEIGHT_NAMES, (m_mix_norm, m_ab_w_in, m_lru_conv_w, m_lru_conv_b, m_lru_wa, m_lru_ba, m_lru_wx, m_lru_bx, m_lru_lambda, m_ab_w_out, m_c_w_qkv, m_c_b_qkv, m_c_sinks, m_c_w_out, m_c_b_out, m_xa_norm, m_xa_mem_norm, m_xa_wq, m_xa_wkv, m_xa_wo, m_ffn_norm, m_ffn_w_gate_up, m_ffn_w_down, m_final_norm)))
    v_loc = dict(zip(WEIGHT_NAMES, (v_mix_norm, v_ab_w_in, v_lru_conv_w, v_lru_conv_b, v_lru_wa, v_lru_ba, v_lru_wx, v_lru_bx, v_lru_lambda, v_ab_w_out, v_c_w_qkv, v_c_b_qkv, v_c_sinks, v_c_w_out, v_c_b_out, v_xa_norm, v_xa_mem_norm, v_xa_wq, v_xa_wkv, v_xa_wo, v_ffn_norm, v_ffn_w_gate_up, v_ffn_w_down, v_final_norm)))

    me = 4 * lax.axis_index("x") + 2 * lax.axis_index("y") + lax.axis_index("c")

    first_keys = list(STAGES[0][2])
    keys = [key for _, _, stage_keys in STAGES[1:] for key in stage_keys]
    shards = [w_loc[n][l].astype(BF16) for n, l in keys]
    first_g = _all_gather([_pack([w_loc[n] for n in SMALL], F32)[None]] + [w_loc[n][l].astype(BF16)[None] for n, l in first_keys],
                          "gather_first")
    gather = _copies_start(shards, False, "gather_start", thru=first_g[0])
    small_g = gather[4][0]
    Wb = {key: _full_from_gathered(g[0], SHARD_AXIS[key[0]] - 1) for key, g in zip(first_keys, first_g[1:])}
    S = {n: w_loc[n] for n in REPLICATED}
    for n, t in zip(SMALL, _unpack(small_g, [w_loc[n].shape for n in SMALL], lead=(N_DEV,))):
        S[n] = _full_from_gathered(t, SHARD_AXIS[n])

    tabs = _rope_tables(x.shape[1])
    vjps = []
    h = x[0]
    for stage, small_names, stage_keys in STAGES:
        which = [keys.index(key) for key in stage_keys if key in keys]
        if which:
            _, lands = _copies_wait(gather, which, False, jax.tree.leaves(h)[-1], "gather_wait_" + stage)
            for e, land in zip(which, lands):
                Wb[keys[e]] = _full_from_gathered(_with_own_block(land, shards[e]), SHARD_AXIS[keys[e][0]] - 1)
        carriers = {key: jnp.zeros(Wb[key].shape, F32) for key in stage_keys}
        h, vjp_fn = jax.vjp(_stage_fn(stage, Wb, tabs, mem[0]), {n: S[n] for n in small_names}, carriers, h)
        vjps.append(vjp_fn)
    loss_part, dh, dg_final = _final_call(h, S['final_norm'], loss_target[0], "final_loss")

    grads = {'final_norm': dg_final.reshape(final_norm.shape)}
    exchanges, send_keys, send_parts = [], [], []
    for (stage, small_names, stage_keys), vjp_fn in zip(reversed(STAGES), reversed(vjps)):
        g_small, g_big, dh = vjp_fn(dh)
        for n in small_names:
            grads[n] = grads[n] + g_small[n] if n in grads else g_small[n]
        send_keys += list(stage_keys)
        send_parts += [_parts_from_full(g_big[key], SHARD_AXIS[key[0]] - 1) for key in stage_keys]
        if stage == "xa1":
            continue
        if stage == "mixer0":
            small_parts = [_parts_from_full(grads[n], SHARD_AXIS[n]) for n in SMALL]
            send_keys.append("small")
            send_parts.append(jnp.stack([_pack([p[j] for p in small_parts], F32) for j in range(N_DEV)]))
        leaves, tree = jax.tree.flatten(dh)
        started = _copies_start(send_parts, True, "grad_start_" + stage, thru=leaves[0])
        dh = jax.tree.unflatten(tree, [started[4]] + leaves[1:])
        exchanges.append((stage, started, send_keys))
        send_keys, send_parts = [], []
    rep_names = REPLICATED + ["loss"]
    grads["loss"] = loss_part
    zero = jnp.zeros((1, 1), F32)
    for d in (w_loc, m_loc, v_loc):
        d["loss"] = zero
    rep_started = _copies_start([_pack([grads[n] for n in rep_names], F32)], False, "rep_grads_start", thru=dh)
    dx = rep_started[4]

    parts, out = {}, {}

    def end_exchange(stage, started, ex_keys, after):
        srcs, lands = _copies_wait(started, list(range(len(ex_keys))), True, after, "grad_wait_" + stage)
        for key, src, land in zip(ex_keys, srcs, lands):
            parts[key] = _with_own_block(land, lax.dynamic_index_in_dim(src, me, 0, keepdims=False))

    def adamw(p, names, call_name):
        if len(names) == 1:
            res = _adamw_call(p, w_loc[names[0]], m_loc[names[0]], v_loc[names[0]], call_name)
            for kind, t in zip(("grad", "delta", "new_m", "new_v"), res):
                out[kind, names[0]] = t
        else:
            res = _adamw_call(p, *[_pack([d[n] for n in names], F32) for d in (w_loc, m_loc, v_loc)], call_name)
            for kind, buf in zip(("grad", "delta", "new_m", "new_v"), res):
                for n, t in zip(names, _unpack(buf, [w_loc[n].shape for n in names])):
                    out[kind, n] = t

    for ex in exchanges[:-1]:
        end_exchange(*ex, dx)
    last_names = {key[0] for key in exchanges[-1][2]}
    for n in BIG:
        if n not in last_names:
            adamw([parts[n, l] for l in range(w_loc[n].shape[0])], [n], "adamw_" + n)
    adamw(parts["small"], SMALL, "adamw_small")
    end_exchange(*exchanges[-1], out["new_v", "ffn_w_gate_up"])
    for n in BIG:
        if n in last_names:
            adamw([parts[n, l] for l in range(w_loc[n].shape[0])], [n], "adamw_" + n)
    rep_src, rep_land = _copies_wait(rep_started, [0], False, out["new_v", "ab_w_in"], "rep_grads_wait")
    adamw(_with_own_block(rep_land[0], rep_src[0]), rep_names, "adamw_replicated")
    loss = out["grad", "loss"][0, 0]

    return (loss, dx[None], *[out[kind, n] for kind in ("grad", "delta", "new_m", "new_v") for n in WEIGHT_NAMES])
```

```python
import functools
import math

import jax
import jax.numpy as jnp
from jax import lax
from jax.experimental import pallas as pl
from jax.experimental.pallas import tpu as pltpu

F32 = jnp.float32
BF16 = jnp.bfloat16
SDS = jax.ShapeDtypeStruct
BS = pl.BlockSpec

N_DEV = 8
NORM_EPS = 1e-6
ROPE_THETA = 500000.0
HEAD_DIM = 64
ROT_DIM = 16
ATT_BLOCK = 128
LRU_C = 8.0
LRU_HEADS = 4
DILATIONS = (1, 4, 16)
B_HEADS = 8
C_HEADS = 16
C_KV_HEADS = 2
XA_HEADS = 4
XA_HEAD_DIM = 128
NEG = -1e30
ADAM_LR, ADAM_B1, ADAM_B2, ADAM_EPS, ADAM_WD, ADAM_STEP = 0.001, 0.9, 0.999, 1e-08, 0.01, 10
LANES = 128
VMEM_LIMIT = 48 * 1024 * 1024

WEIGHT_NAMES = ['mix_norm', 'ab_w_in', 'lru_conv_w', 'lru_conv_b', 'lru_wa', 'lru_ba', 'lru_wx', 'lru_bx',
                'lru_lambda', 'ab_w_out', 'c_w_qkv', 'c_b_qkv', 'c_sinks', 'c_w_out', 'c_b_out', 'xa_norm',
                'xa_mem_norm', 'xa_wq', 'xa_wkv', 'xa_wo', 'ffn_norm', 'ffn_w_gate_up', 'ffn_w_down', 'final_norm']
SHARD_AXIS = {'ab_w_in': 2, 'lru_conv_w': 2, 'lru_wa': 2, 'lru_ba': 2, 'lru_wx': 2, 'lru_bx': 2, 'ab_w_out': 1,
              'c_w_qkv': 2, 'c_b_qkv': 1, 'c_w_out': 1, 'c_b_out': 1, 'xa_wq': 1, 'xa_wkv': 1, 'xa_wo': 2,
              'ffn_w_gate_up': 2, 'ffn_w_down': 1}
BIG = ['ab_w_in', 'lru_wa', 'lru_wx', 'ab_w_out', 'c_w_qkv', 'c_w_out', 'xa_wq', 'xa_wkv', 'xa_wo',
       'ffn_w_gate_up', 'ffn_w_down']
SMALL = ['lru_conv_w', 'lru_ba', 'lru_bx', 'c_b_qkv', 'c_b_out']
SHARDED = BIG + SMALL
REPLICATED = [n for n in WEIGHT_NAMES if n not in SHARD_AXIS]
PACK_QUANTUM = 2048


def _pcall(body, **kw):
    return pl.pallas_call(body, **kw)


def _cparams(sem=None):
    return pltpu.CompilerParams(dimension_semantics=sem, vmem_limit_bytes=VMEM_LIMIT)


def _tile(n, target, mult=LANES):
    if n <= target:
        return n
    t = (target // mult) * mult
    while t >= mult:
        if n % t == 0:
            return t
        t -= mult
    return n


def _sigmoid(x):
    return 1.0 / (1.0 + jnp.exp(-x))


def _expm1(x):
    small = x * (1.0 + x * (0.5 + x * (1.0 / 6.0 + x * (1.0 / 24.0))))
    return jnp.where(jnp.abs(x) < 0.03, small, jnp.exp(x) - 1.0)


_GELU_C = math.sqrt(2.0 / math.pi)


def _gelu_parts(y):
    y2 = y * y
    th = jnp.tanh(_GELU_C * (y + 0.044715 * y * y2))
    g = 0.5 * y * (1.0 + th)
    dg = 0.5 * (1.0 + th) + 0.5 * y * (1.0 - th * th) * _GELU_C * (1.0 + 3.0 * 0.044715 * y2)
    return g, dg


def _mm(a, b, *, ta=False, tb=False, bias=None, add=None, name, tm=512, tn=512, tk=2048):
    M, K = (a.shape[1], a.shape[0]) if ta else a.shape
    N = b.shape[0] if tb else b.shape[1]
    tm, tn, tk = _tile(M, tm), _tile(N, tn), _tile(K, tk)
    nk = K // tk
    dn = (((0 if ta else 1,), (1 if tb else 0,)), ((), ()))

    def body(*refs):
        a_ref, b_ref = refs[0], refs[1]
        pos = 2
        bias_ref = add_ref = None
        if bias is not None:
            bias_ref = refs[pos]
            pos += 1
        if add is not None:
            add_ref = refs[pos]
            pos += 1
        o_ref = refs[pos]
        part = lax.dot_general(a_ref[...].astype(BF16), b_ref[...].astype(BF16), dn, preferred_element_type=F32)

        def finish(r):
            if bias_ref is not None:
                r = r + bias_ref[...]
            if add_ref is not None:
                r = r + add_ref[...]
            o_ref[...] = r

        if nk == 1:
            finish(part)
            return
        acc_ref = refs[pos + 1]
        k = pl.program_id(2)

        @pl.when(k == 0)
        def _():
            acc_ref[...] = part

        @pl.when((k > 0) & (k < nk - 1))
        def _():
            acc_ref[...] += part

        @pl.when(k == nk - 1)
        def _():
            finish(acc_ref[...] + part)

    in_specs = [BS((tk, tm), lambda i, j, k: (k, i)) if ta else BS((tm, tk), lambda i, j, k: (i, k)),
                BS((tn, tk), lambda i, j, k: (j, k)) if tb else BS((tk, tn), lambda i, j, k: (k, j))]
    args = [a, b]
    if bias is not None:
        in_specs.append(BS((1, tn), lambda i, j, k: (0, j)))
        args.append(bias.reshape(1, N))
    if add is not None:
        in_specs.append(BS((tm, tn), lambda i, j, k: (i, j)))
        args.append(add)
    return _pcall(body, name=name, grid=(M // tm, N // tn, nk), in_specs=in_specs,
                  out_specs=BS((tm, tn), lambda i, j, k: (i, j)), out_shape=SDS((M, N), F32),
                  scratch_shapes=[pltpu.VMEM((tm, tn), F32)] if nk > 1 else [],
                  compiler_params=_cparams(("parallel", "parallel", "arbitrary")))(*args)


def _colsum(x, name):
    T, N = x.shape
    tt = _tile(T, 512, 8)

    def body(x_ref, o_ref):
        @pl.when(pl.program_id(0) == 0)
        def _():
            o_ref[...] = jnp.zeros_like(o_ref)

        o_ref[...] += jnp.sum(x_ref[...], axis=0, keepdims=True)

    return _pcall(body, name=name, grid=(T // tt,), in_specs=[BS((tt, N), lambda i: (i, 0))],
                  out_specs=BS((1, N), lambda i: (0, 0)), out_shape=SDS((1, N), F32),
                  compiler_params=_cparams(("arbitrary",)))(x)


@functools.partial(jax.custom_vjp, nondiff_argnums=(5,))
def linear(a, wb, wc, bias, add, name):
    return _mm(a, wb, bias=bias, add=add, name=name)


def _linear_fwd(a, wb, wc, bias, add, name):
    return _mm(a, wb, bias=bias, add=add, name=name), (a, wb, bias is not None, add is not None)


def _linear_bwd(name, res, g):
    a, wb, has_bias, has_add = res
    da = _mm(g, wb, tb=True, name=name + "_da")
    dw = _mm(a, g, ta=True, name=name + "_dw")
    dbias = _colsum(g, name + "_db").reshape(-1) if has_bias else None
    return da, jnp.zeros_like(wb), dw, dbias, (g if has_add else None)


linear.defvjp(_linear_fwd, _linear_bwd)


def _rms_fwd_call(x, g, name, out_dtype=F32):
    T, Dm = x.shape
    tt = _tile(T, 512, 16)

    def body(x_ref, g_ref, o_ref):
        xv = x_ref[...]
        r = lax.rsqrt(jnp.mean(xv * xv, axis=-1, keepdims=True) + NORM_EPS)
        o_ref[...] = ((xv * r) * g_ref[...]).astype(out_dtype)

    return _pcall(body, name=name, grid=(T // tt,),
                  in_specs=[BS((tt, Dm), lambda i: (i, 0)), BS((1, Dm), lambda i: (0, 0))],
                  out_specs=BS((tt, Dm), lambda i: (i, 0)), out_shape=SDS((T, Dm), out_dtype),
                  compiler_params=_cparams(("parallel",)))(x, g.reshape(1, Dm))


def _rms_bwd_call(x, g, dy, name, add=None):
    T, Dm = x.shape
    tt = _tile(T, 512, 8)

    def body(*refs):
        x_ref, g_ref, dy_ref = refs[:3]
        dx_ref, dg_ref = refs[-2:]
        xv = x_ref[...]
        r = lax.rsqrt(jnp.mean(xv * xv, axis=-1, keepdims=True) + NORM_EPS)
        xh = xv * r
        dy = dy_ref[...]
        dyg = dy * g_ref[...]
        dx = r * (dyg - xh * jnp.mean(dyg * xh, axis=-1, keepdims=True))
        dx_ref[...] = dx if add is None else dx + refs[3][...]

        @pl.when(pl.program_id(0) == 0)
        def _():
            dg_ref[...] = jnp.zeros_like(dg_ref)

        dg_ref[...] += jnp.sum(dy * xh, axis=0, keepdims=True)

    row = BS((tt, Dm), lambda i: (i, 0))
    vec = BS((1, Dm), lambda i: (0, 0))
    extra = [] if add is None else [add]
    return _pcall(body, name=name, grid=(T // tt,), in_specs=[row, vec, row] + [row] * len(extra), out_specs=[row, vec],
                  out_shape=[SDS((T, Dm), F32), SDS((1, Dm), F32)],
                  compiler_params=_cparams(("arbitrary",)))(x, g.reshape(1, Dm), dy, *extra)


@functools.partial(jax.custom_vjp, nondiff_argnums=(2,))
def rmsnorm(x, g, name):
    return _rms_fwd_call(x, g, name)


def _rmsnorm_fwd(x, g, name):
    return _rms_fwd_call(x, g, name), (x, g)


def _rmsnorm_bwd(name, res, dy):
    x, g = res
    dx, dg = _rms_bwd_call(x, g, dy, name + "_bwd")
    return dx, dg.reshape(g.shape)


rmsnorm.defvjp(_rmsnorm_fwd, _rmsnorm_bwd)


@functools.partial(jax.custom_vjp, nondiff_argnums=(5,))
def norm_linear(x, g, wb, wc, bias, name):
    return _mm(_rms_fwd_call(x, g, name + "_norm", BF16), wb, bias=bias, name=name, tm=1024, tn=1408, tk=1024), x


def _norm_linear_fwd(x, g, wb, wc, bias, name):
    hn = _rms_fwd_call(x, g, name + "_norm", BF16)
    return (_mm(hn, wb, bias=bias, name=name, tm=1024, tn=1408, tk=1024), x), (x, g, hn, wb, bias is not None)


def _norm_linear_bwd(name, res, cts):
    x, g, hn, wb, has_bias = res
    dy, dres = cts
    dw = _mm(hn, dy, ta=True, name=name + "_dw", tm=1024, tn=1408, tk=1024)
    dhn = _mm(dy, wb, tb=True, name=name + "_da", tm=1024, tn=1024, tk=1408)
    dx, dg = _rms_bwd_call(x, g, dhn, name + "_norm_bwd", add=dres)
    dbias = _colsum(dy, name + "_db").reshape(-1) if has_bias else None
    return dx, dg.reshape(g.shape), jnp.zeros_like(wb), dw, dbias


norm_linear.defvjp(_norm_linear_fwd, _norm_linear_bwd)


def _rope_tables(T):
    half = ROT_DIM // 2
    inv = ROPE_THETA ** (-jnp.arange(0, ROT_DIM, 2, dtype=F32) / ROT_DIM)
    ang = jnp.arange(T, dtype=F32)[:, None] * inv[None, :]
    cos, sin = jnp.cos(ang), jnp.sin(ang)
    ones = jnp.ones((T, HEAD_DIM - ROT_DIM), F32)
    zeros = jnp.zeros((T, HEAD_DIM - ROT_DIM), F32)
    zh = jnp.zeros((T, half), F32)
    c = jnp.concatenate([cos, cos, ones], axis=1)
    sa = jnp.concatenate([zh, sin, zeros], axis=1)
    sb = jnp.concatenate([-sin, zh, zeros], axis=1)
    two = lambda t: jnp.concatenate([t, t], axis=1)
    return two(c), two(sa), two(sb)


def _rope_call(x, tabs, inverse, name):
    T, W = x.shape
    tt = _tile(T, 512, 8)
    reps = W // LANES
    half = ROT_DIM // 2

    def body(x_ref, c_ref, sa_ref, sb_ref, o_ref):
        xv = x_ref[...]
        c = jnp.tile(c_ref[...], (1, reps))
        sa = jnp.tile(sa_ref[...], (1, reps))
        sb = jnp.tile(sb_ref[...], (1, reps))
        if not inverse:
            o_ref[...] = xv * c + pltpu.roll(xv, half, axis=1) * sa + pltpu.roll(xv, W - half, axis=1) * sb
        else:
            o_ref[...] = xv * c + pltpu.roll(xv * sa, W - half, axis=1) + pltpu.roll(xv * sb, half, axis=1)

    row = BS((tt, W), lambda i: (i, 0))
    tab = BS((tt, LANES), lambda i: (i, 0))
    return _pcall(body, name=name, grid=(T // tt,), in_specs=[row, tab, tab, tab], out_specs=row,
                  out_shape=SDS((T, W), F32), compiler_params=_cparams(("parallel",)))(x, *tabs)


@functools.partial(jax.custom_vjp, nondiff_argnums=(2,))
def rope(x, tabs, name):
    return _rope_call(x, tabs, False, name)


def _rope_fwd(x, tabs, name):
    return _rope_call(x, tabs, False, name), tabs


def _rope_bwd(name, tabs, dy):
    return _rope_call(dy, tabs, True, name + "_bwd"), jax.tree.map(jnp.zeros_like, tabs)


rope.defvjp(_rope_fwd, _rope_bwd)


def _conv_fwd_call(x, w, b, name):
    T, C = x.shape
    tt = _tile(T, 512, 8)
    per = tt // 8

    def body(x_ref, halo_ref, w_ref, b_ref, o_ref):
        i = pl.program_id(0)
        halo = jnp.where(i > 0, halo_ref[...], 0.0)
        e = jnp.concatenate([halo, x_ref[...]], axis=0)
        acc = b_ref[...]
        for k in (3, 2, 1):
            acc = acc + pltpu.roll(e, k, axis=0)[8:, :] * w_ref[3 - k:4 - k, :]
        o_ref[...] = acc + x_ref[...] * w_ref[3:4, :]

    return _pcall(body, name=name, grid=(T // tt,),
                  in_specs=[BS((tt, C), lambda i: (i, 0)), BS((8, C), lambda i: (jnp.maximum(i * per - 1, 0), 0)),
                            BS((4, C), lambda i: (0, 0)), BS((1, C), lambda i: (0, 0))],
                  out_specs=BS((tt, C), lambda i: (i, 0)), out_shape=SDS((T, C), F32),
                  compiler_params=_cparams(("parallel",)))(x, x, w, b.reshape(1, C))


def _conv_bwd_call(x, w, dy, name):
    T, C = x.shape
    tt = _tile(T, 512, 8)
    per = tt // 8
    nt = T // tt

    def body(x_ref, halo_ref, w_ref, dy_ref, nxt_ref, dx_ref, dwb_ref):
        i = pl.program_id(0)
        halo = jnp.where(i > 0, halo_ref[...], 0.0)
        e = jnp.concatenate([halo, x_ref[...]], axis=0)
        dy = dy_ref[...]
        nxt = jnp.where(i < nt - 1, nxt_ref[...], 0.0)
        f = jnp.concatenate([dy, nxt], axis=0)
        dx = dy * w_ref[3:4, :]
        rows = [None] * 4
        rows[3] = jnp.sum(dy * x_ref[...], axis=0, keepdims=True)
        for k in (1, 2, 3):
            dx = dx + pltpu.roll(f, tt + 8 - k, axis=0)[:tt, :] * w_ref[3 - k:4 - k, :]
            rows[3 - k] = jnp.sum(dy * pltpu.roll(e, k, axis=0)[8:, :], axis=0, keepdims=True)
        dx_ref[...] = dx
        upd = jnp.concatenate(rows + [jnp.sum(dy, axis=0, keepdims=True), jnp.zeros((3, C), F32)], axis=0)

        @pl.when(i == 0)
        def _():
            dwb_ref[...] = jnp.zeros_like(dwb_ref)

        dwb_ref[...] += upd

    row = BS((tt, C), lambda i: (i, 0))
    return _pcall(body, name=name, grid=(nt,),
                  in_specs=[row, BS((8, C), lambda i: (jnp.maximum(i * per - 1, 0), 0)), BS((4, C), lambda i: (0, 0)),
                            row, BS((8, C), lambda i: (jnp.minimum((i + 1) * per, T // 8 - 1), 0))],
                  out_specs=[row, BS((8, C), lambda i: (0, 0))],
                  out_shape=[SDS((T, C), F32), SDS((8, C), F32)],
                  compiler_params=_cparams(("arbitrary",)))(x, x, w, dy, dy)


@functools.partial(jax.custom_vjp, nondiff_argnums=(3,))
def conv4(x, w, b, name):
    return _conv_fwd_call(x, w, b, name)


def _conv4_fwd(x, w, b, name):
    return _conv_fwd_call(x, w, b, name), (x, w)


def _conv4_bwd(name, res, dy):
    x, w = res
    dx, dwb = _conv_bwd_call(x, w, dy, name + "_bwd")
    return dx, dwb[0:4], dwb[4]


conv4.defvjp(_conv4_fwd, _conv4_bwd)


def _gates_fwd_call(xc, wa, ba, wx, bx, name):
    T, C = xc.shape
    hd = C // LRU_HEADS
    tt = _tile(T, 512, 8)

    def body(x_ref, wa_ref, ba_ref, wx_ref, bx_ref, ga_ref, gx_ref):
        xb = x_ref[...].astype(BF16)
        ga_ref[...] = jnp.dot(xb, wa_ref[0].astype(BF16), preferred_element_type=F32) + ba_ref[...]
        gx_ref[...] = jnp.dot(xb, wx_ref[0].astype(BF16), preferred_element_type=F32) + bx_ref[...]

    blk = BS((tt, hd), lambda i, h: (i, h))
    wsp = BS((1, hd, hd), lambda i, h: (h, 0, 0))
    bsp = BS((1, hd), lambda i, h: (0, h))
    return _pcall(body, name=name, grid=(T // tt, LRU_HEADS), in_specs=[blk, wsp, bsp, wsp, bsp],
                  out_specs=[blk, blk], out_shape=[SDS((T, C), F32)] * 2,
                  compiler_params=_cparams(("parallel", "parallel")))(xc, wa, ba.reshape(1, C), wx, bx.reshape(1, C))


def _gates_bwd_x_call(dga, dgx, wa, wx, name):
    T, C = dga.shape
    hd = C // LRU_HEADS
    tt = _tile(T, 512, 8)
    dn = (((1,), (1,)), ((), ()))

    def body(da_ref, dx_ref, wa_ref, wx_ref, o_ref):
        o_ref[...] = (lax.dot_general(da_ref[...].astype(BF16), wa_ref[0].astype(BF16), dn, preferred_element_type=F32)
                      + lax.dot_general(dx_ref[...].astype(BF16), wx_ref[0].astype(BF16), dn, preferred_element_type=F32))

    blk = BS((tt, hd), lambda i, h: (i, h))
    wsp = BS((1, hd, hd), lambda i, h: (h, 0, 0))
    return _pcall(body, name=name, grid=(T // tt, LRU_HEADS), in_specs=[blk, blk, wsp, wsp], out_specs=blk,
                  out_shape=SDS((T, C), F32), compiler_params=_cparams(("parallel", "parallel")))(dga, dgx, wa, wx)


def _gates_bwd_w_call(xc, dga, dgx, name):
    T, C = xc.shape
    hd = C // LRU_HEADS
    tt = _tile(T, 512, 8)
    dn = (((0,), (0,)), ((), ()))

    def body(x_ref, da_ref, dx_ref, dwa_ref, dwx_ref, dba_ref, dbx_ref):
        @pl.when(pl.program_id(1) == 0)
        def _():
            dwa_ref[...] = jnp.zeros_like(dwa_ref)
            dwx_ref[...] = jnp.zeros_like(dwx_ref)
            dba_ref[...] = jnp.zeros_like(dba_ref)
            dbx_ref[...] = jnp.zeros_like(dbx_ref)

        xb = x_ref[...].astype(BF16)
        da, dx = da_ref[...], dx_ref[...]
        dwa_ref[0] += lax.dot_general(xb, da.astype(BF16), dn, preferred_element_type=F32)
        dwx_ref[0] += lax.dot_general(xb, dx.astype(BF16), dn, preferred_element_type=F32)
        dba_ref[...] += jnp.sum(da, axis=0, keepdims=True)
        dbx_ref[...] += jnp.sum(dx, axis=0, keepdims=True)

    blk = BS((tt, hd), lambda h, i: (i, h))
    wsp = BS((1, hd, hd), lambda h, i: (h, 0, 0))
    bsp = BS((1, hd), lambda h, i: (0, h))
    return _pcall(body, name=name, grid=(LRU_HEADS, T // tt), in_specs=[blk, blk, blk],
                  out_specs=[wsp, wsp, bsp, bsp],
                  out_shape=[SDS((LRU_HEADS, hd, hd), F32)] * 2 + [SDS((1, C), F32)] * 2,
                  compiler_params=_cparams(("parallel", "arbitrary")))(xc, dga, dgx)


@functools.partial(jax.custom_vjp, nondiff_argnums=(7,))
def lru_gates(xc, wa, wa_c, ba, wx, wx_c, bx, name):
    return tuple(_gates_fwd_call(xc, wa, ba, wx, bx, name))


def _lru_gates_fwd(xc, wa, wa_c, ba, wx, wx_c, bx, name):
    return tuple(_gates_fwd_call(xc, wa, ba, wx, bx, name)), (xc, wa, wx, ba.shape)


def _lru_gates_bwd(name, res, g):
    xc, wa, wx, bshape = res
    dga, dgx = g
    dxc = _gates_bwd_x_call(dga, dgx, wa, wx, name + "_dx")
    dwa, dwx, dba, dbx = _gates_bwd_w_call(xc, dga, dgx, name + "_dw")
    return dxc, jnp.zeros_like(wa), dwa, dba.reshape(bshape), jnp.zeros_like(wx), dwx, dbx.reshape(bshape)


lru_gates.defvjp(_lru_gates_fwd, _lru_gates_bwd)


def _lru_coeffs(xc, ga, gx, lam):
    r = _sigmoid(ga)
    ig = _sigmoid(gx)
    z = -lam
    sp = jnp.maximum(z, 0.0) + jnp.log(1.0 + jnp.exp(-jnp.abs(z)))
    la = -LRU_C * r * sp
    a = jnp.exp(la)
    s = jnp.sqrt(-_expm1(2.0 * la))
    return r, ig, sp, a, s


LRU_TT = 256


def _scan_fwd_call(xc, ga, gx, y, lam, name):
    T, C = xc.shape
    tt = _tile(T, LRU_TT, 8)

    def body(xc_ref, ga_ref, gx_ref, y_ref, lam_ref, h_ref, rec_ref, a_buf, carry):
        @pl.when(pl.program_id(0) == 0)
        def _():
            carry[...] = jnp.zeros_like(carry)

        xcv = xc_ref[...]
        _, ig, _, a, s = _lru_coeffs(xcv, ga_ref[...], gx_ref[...], lam_ref[...])
        a_buf[...] = a
        h_ref[...] = s * (ig * xcv)

        def step(t, h):
            hn = a_buf[pl.ds(t, 1), :] * h + h_ref[pl.ds(t, 1), :]
            h_ref[pl.ds(t, 1), :] = hn
            return hn

        carry[0:1, :] = lax.fori_loop(0, tt, step, carry[0:1, :], unroll=8)
        g, _ = _gelu_parts(y_ref[...])
        rec_ref[...] = h_ref[...] * g

    row = BS((tt, C), lambda i: (i, 0))
    vec = BS((1, C), lambda i: (0, 0))
    return _pcall(body, name=name, grid=(T // tt,), in_specs=[row, row, row, row, vec], out_specs=[row, row],
                  out_shape=[SDS((T, C), F32)] * 2,
                  scratch_shapes=[pltpu.VMEM((tt, C), F32), pltpu.VMEM((8, C), F32)],
                  compiler_params=_cparams(("arbitrary",)))(xc, ga, gx, y, lam.reshape(1, C))


def _scan_bwd_call(xc, ga, gx, y, lam, h, drec, name):
    T, C = xc.shape
    tt = _tile(T, LRU_TT, 8)
    nt = T // tt
    per = tt // 8

    def body(xc_ref, ga_ref, gx_ref, y_ref, lam_ref, h_ref, halo_ref, dr_ref,
             dga_ref, dgx_ref, dxc_ref, dy_ref, dlam_ref, a_buf, g_buf, carry):
        i = pl.program_id(0)

        @pl.when(i == 0)
        def _():
            carry[...] = jnp.zeros_like(carry)
            dlam_ref[...] = jnp.zeros_like(dlam_ref)

        xcv, lam = xc_ref[...], lam_ref[...]
        r, ig, sp, a, s = _lru_coeffs(xcv, ga_ref[...], gx_ref[...], lam)
        gel, dgel = _gelu_parts(y_ref[...])
        drec = dr_ref[...]
        hv = h_ref[...]
        dy_ref[...] = drec * hv * dgel
        a_buf[...] = a
        g_buf[...] = drec * gel

        def step(j, q):
            t = tt - 1 - j
            g = g_buf[pl.ds(t, 1), :] + q
            g_buf[pl.ds(t, 1), :] = g
            return a_buf[pl.ds(t, 1), :] * g

        carry[0:1, :] = lax.fori_loop(0, tt, step, carry[0:1, :], unroll=8)
        g = g_buf[...]
        halo = jnp.where(i < nt - 1, halo_ref[...], 0.0)
        hprev = pltpu.roll(jnp.concatenate([halo, hv], axis=0), 1, axis=0)[8:, :]
        da = g * hprev
        dig = g * s * xcv
        ds = g * ig * xcv
        dla = da * a - ds * (a * a) / s
        dga_ref[...] = dla * (-LRU_C * sp) * r * (1.0 - r)
        dgx_ref[...] = dig * ig * (1.0 - ig)
        dxc_ref[...] = g * s * ig
        dlam_ref[...] += jnp.sum(dla * r, axis=0, keepdims=True) * (LRU_C * _sigmoid(-lam))

    row = BS((tt, C), lambda i: (nt - 1 - i, 0))
    vec = BS((1, C), lambda i: (0, 0))
    halo = BS((8, C), lambda i: (jnp.maximum((nt - 1 - i) * per - 1, 0), 0))
    return _pcall(body, name=name, grid=(nt,), in_specs=[row, row, row, row, vec, row, halo, row],
                  out_specs=[row, row, row, row, vec], out_shape=[SDS((T, C), F32)] * 4 + [SDS((1, C), F32)],
                  scratch_shapes=[pltpu.VMEM((tt, C), F32), pltpu.VMEM((tt, C), F32), pltpu.VMEM((8, C), F32)],
                  compiler_params=_cparams(("arbitrary",)))(xc, ga, gx, y, lam.reshape(1, C), h, h, drec)


@functools.partial(jax.custom_vjp, nondiff_argnums=(5,))
def lru_scan(xc, ga, gx, y, lam, name):
    return _scan_fwd_call(xc, ga, gx, y, lam, name)[1]


def _lru_scan_fwd(xc, ga, gx, y, lam, name):
    h, rec = _scan_fwd_call(xc, ga, gx, y, lam, name)
    return rec, (xc, ga, gx, y, lam, h)


def _lru_scan_bwd(name, res, drec):
    xc, ga, gx, y, lam, h = res
    dga, dgx, dxc, dy, dlam = _scan_bwd_call(xc, ga, gx, y, lam, h, drec, name + "_bwd")
    return dxc, dga, dgx, dy, dlam.reshape(lam.shape)


lru_scan.defvjp(_lru_scan_fwd, _lru_scan_bwd)


def _att_batch(d):
    return (4, 1) if d == 1 else (1, 4)


def _att_masks(n, max_dist):
    qi = lax.broadcasted_iota(jnp.int32, (1, 2 * ATT_BLOCK, 2 * ATT_BLOCK), 1) & (ATT_BLOCK - 1)
    kj = lax.broadcasted_iota(jnp.int32, (1, 2 * ATT_BLOCK, 2 * ATT_BLOCK), 2)
    prev = (kj < ATT_BLOCK) & (kj >= qi + (ATT_BLOCK - max_dist)) & (n > 0)
    cur = (kj >= ATT_BLOCK) & (kj - ATT_BLOCK <= qi)
    return prev | cur


def _lo_lanes(rows):
    return lax.broadcasted_iota(jnp.int32, (rows, LANES), 1) < HEAD_DIM


def _lane_half(rows):
    return lax.broadcasted_iota(jnp.int32, (rows, LANES), 1) // HEAD_DIM


def _stack_heads(x2):
    lo = _lo_lanes(ATT_BLOCK)
    zero = jnp.zeros_like(x2)
    return jnp.concatenate([jnp.where(lo, x2, zero), jnp.where(lo, zero, x2)], axis=0)


def _unstack_heads(y):
    return jnp.where(_lo_lanes(ATT_BLOCK), y[:ATT_BLOCK], y[ATT_BLOCK:])


def _per_head_col(x2):
    return jnp.concatenate([x2[:, 0:1], x2[:, HEAD_DIM:HEAD_DIM + 1]], axis=0)


def _head_sums(x2):
    lo = _lo_lanes(ATT_BLOCK)
    return jnp.concatenate([jnp.sum(jnp.where(lo, x2, 0.0), axis=1, keepdims=True),
                            jnp.sum(jnp.where(lo, 0.0, x2), axis=1, keepdims=True)], axis=0)


def _att_specs(d, Wq, Wk, nb, clamp):
    cgw, sb = _att_batch(d)
    shared = Wk != Wq
    cur = (lambda n: jnp.minimum(n, nb - 1)) if clamp else (lambda n: n)
    rows, qw, kw = ATT_BLOCK * d, cgw * LANES, (LANES if shared else cgw * LANES)
    kcol = (lambda g: 0) if shared else (lambda g: g)
    qsp = BS((rows, qw), lambda g, n: (cur(n), g))
    csp = BS((rows, kw), lambda g, n: (cur(n), kcol(g)))
    psp = BS((rows, kw), lambda g, n: (jnp.maximum(cur(n) - 1, 0), kcol(g)))
    return cgw, sb, shared, qsp, csp, psp, qw, kw


def _att_streams(d, sb, work):
    if d == 1:
        work([slice(None)])
        return

    def one(j, carry):
        work([pl.ds(j * sb + i, ATT_BLOCK, stride=d) for i in range(sb)])
        return carry

    lax.fori_loop(0, d // sb, one, 0)


def _att_problem_loads(rows, cgw, shared, g, q_ref, kc_ref, kp_ref, vc_ref, vp_ref, sk_ref):
    half = _lane_half(ATT_BLOCK)

    def kv(ref, r, p):
        x = ref[r, :]
        if not shared:
            return x[:, p * LANES:(p + 1) * LANES]
        return jnp.where(half == g, x, pltpu.roll(x, HEAD_DIM, axis=1))

    qs, kb, vb, sk = [], [], [], []
    for r in rows:
        qrow = q_ref[r, :]
        for p in range(cgw):
            cols = slice(p * LANES, (p + 1) * LANES)
            qs.append(_stack_heads(qrow[:, cols].astype(BF16)))
            kb.append(jnp.concatenate([kv(kp_ref, r, p), kv(kc_ref, r, p)], axis=0).astype(BF16))
            vb.append(jnp.concatenate([kv(vp_ref, r, p), kv(vc_ref, r, p)], axis=0).astype(BF16))
            sk.append(_per_head_col(jnp.broadcast_to(sk_ref[:, cols], (ATT_BLOCK, LANES))))
    return jnp.stack(qs), jnp.stack(kb), jnp.stack(vb), jnp.stack(sk)


_BDOT_NT = (((2,), (2,)), ((0,), (0,)))
_BDOT_NN = (((2,), (1,)), ((0,), (0,)))
_BDOT_TN = (((1,), (1,)), ((0,), (0,)))


def _att_fwd_call(q, k, v, sinks, d, max_dist, name):
    T, Wq = q.shape
    Wk = k.shape[1]
    nb = T // (d * ATT_BLOCK)
    cgw, sb, shared, qsp, csp, psp, qw, kw = _att_specs(d, Wq, Wk, nb, False)
    G = Wq // qw
    assert not shared or (d == 1 and Wk == LANES and G == 2), "a shared kv pair serves two groups of 8 query heads"

    def body(q_ref, kc_ref, kp_ref, vc_ref, vp_ref, sk_ref, o_ref, lse_ref):
        g, n = pl.program_id(0), pl.program_id(1)

        def work(rows):
            qs, kband, vband, sk = _att_problem_loads(rows, cgw, shared, g, q_ref, kc_ref, kp_ref, vc_ref, vp_ref, sk_ref)
            s = lax.dot_general(qs, kband, _BDOT_NT, preferred_element_type=F32) * (HEAD_DIM ** -0.5)
            s = jnp.where(_att_masks(n, max_dist), s, NEG)
            m = jnp.maximum(jnp.max(s, axis=-1, keepdims=True), sk)
            e = jnp.exp(s - m)
            den = jnp.sum(e, axis=-1, keepdims=True) + jnp.exp(sk - m)
            o = lax.dot_general((e * (1.0 / den)).astype(BF16), vband, _BDOT_NN, preferred_element_type=F32)
            lse = jnp.broadcast_to(m + jnp.log(den), o.shape)
            for i, r in enumerate(rows):
                o_ref[r, :] = jnp.concatenate([_unstack_heads(o[i * cgw + p]) for p in range(cgw)], axis=1)
                lse_ref[r, :] = jnp.concatenate([_unstack_heads(lse[i * cgw + p]) for p in range(cgw)], axis=1)

        _att_streams(d, sb, work)

    sksp = BS((1, qw), lambda g, n: (0, g))
    return _pcall(body, name=name, grid=(G, nb), in_specs=[qsp, csp, psp, csp, psp, sksp], out_specs=[qsp, qsp],
                  out_shape=[SDS((T, Wq), F32)] * 2,
                  compiler_params=_cparams(("parallel", "parallel")))(q, k, k, v, v, sinks)


def _att_bwd_call(q, k, v, sinks, o, lse, do, dlse, d, max_dist, name):
    T, Wq = q.shape
    Wk = k.shape[1]
    nb = T // (d * ATT_BLOCK)
    cgw, sb, shared, qsp, csp, psp, qw, kw = _att_specs(d, Wq, Wk, nb, True)
    G = Wq // qw
    scale = HEAD_DIM ** -0.5

    def body(q_ref, kc_ref, kp_ref, vc_ref, vp_ref, sk_ref, o_ref, lse_ref, do_ref, dlse_ref,
             dq_ref, dk_ref, dv_ref, dsk_ref, ck, cv):
        g, n = pl.program_id(0), pl.program_id(1)

        @pl.when(n == 0)
        def _():
            ck[...] = jnp.zeros_like(ck)
            cv[...] = jnp.zeros_like(cv)
            dsk_ref[...] = jnp.zeros_like(dsk_ref)

        def put(ref, r, val):
            if len(ref.shape) == 3:
                ref[0, r, :] = val
            else:
                ref[r, :] = val

        def work(rows):
            qs, kband, vband, sk = _att_problem_loads(rows, cgw, shared, g, q_ref, kc_ref, kp_ref, vc_ref, vp_ref, sk_ref)
            dos, lse_c, corr = [], [], []
            for r in rows:
                do_r, o_r, lse_r, dlse_r = do_ref[r, :], o_ref[r, :], lse_ref[r, :], dlse_ref[r, :]
                for p in range(cgw):
                    cols = slice(p * LANES, (p + 1) * LANES)
                    dos.append(_stack_heads(do_r[:, cols].astype(BF16)))
                    lse_c.append(_per_head_col(lse_r[:, cols]))
                    corr.append(_head_sums(dlse_r[:, cols]) - _head_sums(do_r[:, cols] * o_r[:, cols]))
            dos, lse_c, corr = jnp.stack(dos), jnp.stack(lse_c), jnp.stack(corr)
            s = lax.dot_general(qs, kband, _BDOT_NT, preferred_element_type=F32) * scale
            pr = jnp.exp(jnp.where(_att_masks(n, max_dist), s, NEG) - lse_c)
            dp = lax.dot_general(dos, vband, _BDOT_NT, preferred_element_type=F32)
            ds = (pr * (dp + corr)).astype(BF16)
            dq = lax.dot_general(ds, kband, _BDOT_NN, preferred_element_type=F32) * scale
            dkb = lax.dot_general(ds, qs, _BDOT_TN, preferred_element_type=F32) * scale
            dvb = lax.dot_general(pr.astype(BF16), dos, _BDOT_TN, preferred_element_type=F32)
            dsk = jnp.exp(sk - lse_c) * corr
            lane = lax.broadcasted_iota(jnp.int32, (8, LANES), 1)
            for p in range(cgw):
                tot = [jnp.sum(jnp.stack([dsk[i * cgw + p, h * ATT_BLOCK:(h + 1) * ATT_BLOCK] for i in range(len(rows))]),
                               axis=(0, 1)).reshape(1, 1) for h in range(2)]
                dsk_ref[:, p * LANES:(p + 1) * LANES] += jnp.where(lane == 0, tot[0], jnp.where(lane == HEAD_DIM, tot[1], 0.0))

            def gather_pairs(parts):
                if not shared:
                    return jnp.concatenate(parts, axis=1)
                tot = parts[0]
                for t in parts[1:]:
                    tot = tot + t
                tot = tot + pltpu.roll(tot, HEAD_DIM, axis=1)
                return jnp.where(_lane_half(ATT_BLOCK) == g, tot, 0.0)

            for i, r in enumerate(rows):
                mine = range(i * cgw, (i + 1) * cgw)
                dq_ref[r, :] = jnp.concatenate([_unstack_heads(dq[b]) for b in mine], axis=1)
                put(dk_ref, r, ck[r, :] + gather_pairs([dkb[b, :ATT_BLOCK] for b in mine]))
                put(dv_ref, r, cv[r, :] + gather_pairs([dvb[b, :ATT_BLOCK] for b in mine]))
                ck[r, :] = gather_pairs([dkb[b, ATT_BLOCK:] for b in mine])
                cv[r, :] = gather_pairs([dvb[b, ATT_BLOCK:] for b in mine])

        @pl.when(n < nb)
        def _():
            _att_streams(d, sb, work)

        @pl.when(n == nb)
        def _():
            dk_ref[...] = ck[...].reshape(dk_ref.shape)
            dv_ref[...] = cv[...].reshape(dv_ref.shape)

    sksp = BS((1, qw), lambda g, n: (0, g))
    rows = ATT_BLOCK * d
    if shared:
        osp = BS((1, rows, kw), lambda g, n: (g, jnp.maximum(n - 1, 0), 0))
        kshape = SDS((G, T, Wk), F32)
    else:
        osp = BS((rows, kw), lambda g, n: (jnp.maximum(n - 1, 0), g))
        kshape = SDS((T, Wk), F32)
    dq, dk, dv, dsk = _pcall(
        body, name=name, grid=(G, nb + 1),
        in_specs=[qsp, csp, psp, csp, psp, sksp, qsp, qsp, qsp, qsp],
        out_specs=[qsp, osp, osp, BS((8, qw), lambda g, n: (0, g))],
        out_shape=[SDS((T, Wq), F32), kshape, kshape, SDS((8, Wq), F32)],
        scratch_shapes=[pltpu.VMEM((rows, kw), F32)] * 2,
        compiler_params=_cparams(("parallel", "arbitrary")))(q, k, k, v, v, sinks, o, lse, do, dlse)
    if shared:
        dk, dv = jnp.sum(dk, axis=0), jnp.sum(dv, axis=0)
    return dq, dk, dv, dsk[0:1]


@functools.partial(jax.custom_vjp, nondiff_argnums=(4, 5, 6))
def band_attention(q, k, v, sinks, d, max_dist, name):
    return _att_fwd_call(q, k, v, sinks, d, max_dist, name)


def _band_attention_fwd(q, k, v, sinks, d, max_dist, name):
    o, lse = _att_fwd_call(q, k, v, sinks, d, max_dist, name)
    return (o, lse), (q, k, v, sinks, o, lse)


def _band_attention_bwd(d, max_dist, name, res, g):
    q, k, v, sinks, o, lse = res
    do, dlse = g
    return _att_bwd_call(q, k, v, sinks, o, lse, do, dlse, d, max_dist, name + "_bwd")


band_attention.defvjp(_band_attention_fwd, _band_attention_bwd)


def _merge_weights(ls):
    mx = jnp.maximum(jnp.maximum(ls[0], ls[1]), ls[2])
    es = [jnp.exp(l - mx) for l in ls]
    inv = 1.0 / (es[0] + es[1] + es[2])
    return [e * inv for e in es]


def _merge_fwd_call(os_, ls_, name):
    T, W = os_[0].shape
    tt = _tile(T, 512, 8)

    def body(o1, o2, o3, l1, l2, l3, out_ref):
        w = _merge_weights([l1[...], l2[...], l3[...]])
        out_ref[...] = w[0] * o1[...] + w[1] * o2[...] + w[2] * o3[...]

    row = BS((tt, W), lambda i: (i, 0))
    return _pcall(body, name=name, grid=(T // tt,), in_specs=[row] * 6, out_specs=row,
                  out_shape=SDS((T, W), F32), compiler_params=_cparams(("parallel",)))(*os_, *ls_)


def _merge_bwd_call(os_, ls_, do, name):
    T, W = os_[0].shape
    tt = _tile(T, 512, 8)

    def body(o1, o2, o3, l1, l2, l3, do_ref, d1, d2, d3, e1, e2, e3):
        w = _merge_weights([l1[...], l2[...], l3[...]])
        dov = do_ref[...]
        ts = [dov * o[...] for o in (o1, o2, o3)]
        mean = w[0] * ts[0] + w[1] * ts[1] + w[2] * ts[2]
        for wi, ti, dref, eref in zip(w, ts, (d1, d2, d3), (e1, e2, e3)):
            dref[...] = wi * dov
            eref[...] = wi * (ti - mean)

    row = BS((tt, W), lambda i: (i, 0))
    return _pcall(body, name=name, grid=(T // tt,), in_specs=[row] * 7, out_specs=[row] * 6,
                  out_shape=[SDS((T, W), F32)] * 6, compiler_params=_cparams(("parallel",)))(*os_, *ls_, do)


@functools.partial(jax.custom_vjp, nondiff_argnums=(2,))
def merge3(os_, ls_, name):
    return _merge_fwd_call(os_, ls_, name)


def _merge3_fwd(os_, ls_, name):
    return _merge_fwd_call(os_, ls_, name), (os_, ls_)


def _merge3_bwd(name, res, do):
    os_, ls_ = res
    out = _merge_bwd_call(os_, ls_, do, name + "_bwd")
    return tuple(out[:3]), tuple(out[3:])


merge3.defvjp(_merge3_fwd, _merge3_bwd)


def _xa_probs(qb, kb, scale):
    s = lax.dot_general(qb, kb, (((1,), (1,)), ((), ())), preferred_element_type=F32) * scale
    e = jnp.exp(s - jnp.max(s, axis=-1, keepdims=True))
    return e / jnp.sum(e, axis=-1, keepdims=True)


def _xa_fwd_call(q, kv, name):
    T, W = q.shape
    M = kv.shape[0]
    hd = XA_HEAD_DIM
    tq = _tile(T, 512, 8)
    scale = hd ** -0.5

    def body(q_ref, k_ref, v_ref, o_ref):
        p = _xa_probs(q_ref[...].astype(BF16), k_ref[...].astype(BF16), scale)
        o_ref[...] = jnp.dot(p.astype(BF16), v_ref[...].astype(BF16), preferred_element_type=F32)

    qsp = BS((tq, hd), lambda i, h: (i, h))
    return _pcall(body, name=name, grid=(T // tq, XA_HEADS),
                  in_specs=[qsp, BS((M, hd), lambda i, h: (0, h)), BS((M, hd), lambda i, h: (0, XA_HEADS + h))],
                  out_specs=qsp, out_shape=SDS((T, W), F32),
                  compiler_params=_cparams(("parallel", "parallel")))(q, kv, kv)


def _xa_bwd_call(q, kv, do, name):
    T, W = q.shape
    M = kv.shape[0]
    hd = XA_HEAD_DIM
    tq = _tile(T, 512, 8)
    scale = hd ** -0.5
    dn_nt = (((1,), (1,)), ((), ()))
    dn_tn = (((0,), (0,)), ((), ()))

    def body(q_ref, k_ref, v_ref, do_ref, dq_ref, dk_ref, dv_ref):
        @pl.when(pl.program_id(1) == 0)
        def _():
            dk_ref[...] = jnp.zeros_like(dk_ref)
            dv_ref[...] = jnp.zeros_like(dv_ref)

        qb, kb, vb = q_ref[...].astype(BF16), k_ref[...].astype(BF16), v_ref[...].astype(BF16)
        p = _xa_probs(qb, kb, scale)
        dob = do_ref[...].astype(BF16)
        dp = lax.dot_general(dob, vb, dn_nt, preferred_element_type=F32)
        ds = (p * (dp - jnp.sum(p * dp, axis=-1, keepdims=True))).astype(BF16)
        dq_ref[...] = jnp.dot(ds, kb, preferred_element_type=F32) * scale
        dk_ref[...] += lax.dot_general(ds, qb, dn_tn, preferred_element_type=F32) * scale
        dv_ref[...] += lax.dot_general(p.astype(BF16), dob, dn_tn, preferred_element_type=F32)

    qsp = BS((tq, hd), lambda h, i: (i, h))
    ksp = BS((M, hd), lambda h, i: (0, h))
    return _pcall(body, name=name, grid=(XA_HEADS, T // tq),
                  in_specs=[qsp, ksp, BS((M, hd), lambda h, i: (0, XA_HEADS + h)), qsp],
                  out_specs=[qsp, ksp, ksp], out_shape=[SDS((T, W), F32), SDS((M, W), F32), SDS((M, W), F32)],
                  compiler_params=_cparams(("parallel", "arbitrary")))(q, kv, kv, do)


@functools.partial(jax.custom_vjp, nondiff_argnums=(2,))
def cross_attention(q, kv, name):
    return _xa_fwd_call(q, kv, name)


def _cross_attention_fwd(q, kv, name):
    return _xa_fwd_call(q, kv, name), (q, kv)


def _cross_attention_bwd(name, res, do):
    q, kv = res
    dq, dk, dv = _xa_bwd_call(q, kv, do, name + "_bwd")
    return dq, jnp.concatenate([dk, dv], axis=1)


cross_attention.defvjp(_cross_attention_fwd, _cross_attention_bwd)


def _gate_up_swiglu_call(hn, w1, name):
    T, K = hn.shape
    F = w1.shape[1] // 2
    tm, tn = _tile(T, 1024), _tile(F, 256)
    nj = F // tn

    def body(a_ref, wg_ref, wu_ref, g_ref, u_ref, act_ref):
        a = a_ref[...]
        g = jnp.dot(a, wg_ref[...], preferred_element_type=F32)
        u = jnp.dot(a, wu_ref[...], preferred_element_type=F32)
        g_ref[...] = g
        u_ref[...] = u
        act_ref[...] = ((g * _sigmoid(g)) * u).astype(BF16)

    tile = BS((tm, tn), lambda i, j: (i, j))
    return _pcall(body, name=name, grid=(T // tm, nj),
                  in_specs=[BS((tm, K), lambda i, j: (i, 0)), BS((K, tn), lambda i, j: (0, j)),
                            BS((K, tn), lambda i, j: (0, j + nj))],
                  out_specs=[tile, tile, tile], out_shape=[SDS((T, F), F32), SDS((T, F), F32), SDS((T, F), BF16)],
                  compiler_params=_cparams(("parallel", "parallel")))(hn, w1, w1)


def _swiglu_bwd_call(g, u, dact, name):
    T, F = g.shape
    tt = _tile(T, 128, 16)

    def body(g_ref, u_ref, d_ref, o_ref):
        g, d = g_ref[...], d_ref[...]
        sg = _sigmoid(g)
        o_ref[:, :F] = (d * u_ref[...] * (sg * (1.0 + g * (1.0 - sg)))).astype(BF16)
        o_ref[:, F:] = (d * (g * sg)).astype(BF16)

    row = BS((tt, F), lambda i: (i, 0))
    return _pcall(body, name=name, grid=(T // tt,), in_specs=[row, row, row],
                  out_specs=BS((tt, 2 * F), lambda i: (i, 0)), out_shape=SDS((T, 2 * F), BF16),
                  compiler_params=_cparams(("parallel",)))(g, u, dact)


@functools.partial(jax.custom_vjp, nondiff_argnums=(6,))
def ffn_block(h, g, w1b, w1c, w2b, w2c, name):
    return _ffn_fwd(h, g, w1b, w1c, w2b, w2c, name)[0]


def _ffn_fwd(h, g, w1b, w1c, w2b, w2c, name):
    hn = _rms_fwd_call(h, g, name + "_norm", BF16)
    gate, up, act = _gate_up_swiglu_call(hn, w1b, name + "_gu")
    out = _mm(act, w2b, add=h, name=name + "_down", tm=1024, tn=1024, tk=1408)
    return out, (h, g, hn, gate, up, act, w1b, w2b)


def _ffn_bwd(name, res, dout):
    h, g, hn, gate, up, act, w1b, w2b = res
    dact = _mm(dout, w2b, tb=True, name=name + "_down_da", tm=1024, tn=1408, tk=1024)
    dw2 = _mm(act, dout, ta=True, name=name + "_down_dw", tm=1408, tn=1024, tk=1024)
    dgu = _swiglu_bwd_call(gate, up, dact, name + "_swiglu_bwd")
    dhn = _mm(dgu, w1b, tb=True, name=name + "_gu_da", tm=1024, tn=1024, tk=1408)
    dw1 = _mm(hn, dgu, ta=True, name=name + "_gu_dw", tm=1024, tn=1408, tk=1024)
    dh, dg = _rms_bwd_call(h, g, dhn, name + "_norm_bwd", add=dout)
    return dh, dg.reshape(g.shape), jnp.zeros_like(w1b), dw1, jnp.zeros_like(w2b), dw2


ffn_block.defvjp(_ffn_fwd, _ffn_bwd)


def _final_call(h, g, target, name):
    T, Dm = h.shape
    tt = _tile(T, 512, 8)

    def body(x_ref, g_ref, t_ref, loss_ref, dx_ref, dg_ref):
        @pl.when(pl.program_id(0) == 0)
        def _():
            loss_ref[...] = jnp.zeros_like(loss_ref)
            dg_ref[...] = jnp.zeros_like(dg_ref)

        xv, gv = x_ref[...], g_ref[...]
        r = lax.rsqrt(jnp.mean(xv * xv, axis=-1, keepdims=True) + NORM_EPS)
        xh = xv * r
        err = xh * gv - t_ref[...]
        loss_ref[...] += 0.5 * jnp.sum(jnp.mean(err * err, axis=-1, keepdims=True), axis=0, keepdims=True)
        dy = err * (1.0 / Dm)
        dyg = dy * gv
        dx_ref[...] = r * (dyg - xh * jnp.mean(dyg * xh, axis=-1, keepdims=True))
        dg_ref[...] += jnp.sum(dy * xh, axis=0, keepdims=True)

    row = BS((tt, Dm), lambda i: (i, 0))
    vec = BS((1, Dm), lambda i: (0, 0))
    return _pcall(body, name=name, grid=(T // tt,), in_specs=[row, vec, row],
                  out_specs=[BS((1, 1), lambda i: (0, 0)), row, vec],
                  out_shape=[SDS((1, 1), F32), SDS((T, Dm), F32), SDS((1, Dm), F32)],
                  compiler_params=_cparams(("arbitrary",)))(h, g.reshape(1, Dm), target)


ADAMW_BLOCK_ELEMS = 64 * 1024


def _adamw_call(parts, w, m, v, name):
    shape = w.shape
    if not isinstance(parts, (list, tuple)):
        parts, shape3 = [parts], (1,) + shape
    else:
        shape3 = shape
    n_lead = shape3[0]
    r, N = shape3[-2], shape3[-1]
    Ld = math.prod(shape3[1:-2])
    w, m, v = (t.reshape(n_lead * Ld, r, N) for t in (w, m, v))
    tr = _tile(r, max(8, ADAMW_BLOCK_ELEMS // N), 8)
    c1 = 1.0 - ADAM_B1 ** ADAM_STEP
    c2 = 1.0 - ADAM_B2 ** ADAM_STEP
    outs = None
    for lead, p in enumerate(parts):
        def body(p_ref, w_ref, m_ref, v_ref, *rest):
            g_ref, d_ref, nm_ref, nv_ref = rest[-4:]
            g = p_ref[0]
            for j in range(1, N_DEV):
                g = g + p_ref[j]
            nm = ADAM_B1 * m_ref[...] + (1.0 - ADAM_B1) * g
            nv = ADAM_B2 * v_ref[...] + (1.0 - ADAM_B2) * (g * g)
            g_ref[...] = g
            nm_ref[...] = nm
            nv_ref[...] = nv
            d_ref[...] = -ADAM_LR * ((nm / c1) / (jnp.sqrt(nv / c2) + ADAM_EPS) + ADAM_WD * w_ref[...])

        base = lead * Ld
        row = BS((1, tr, N), lambda l, i, base=base: (base + l, i, 0))
        prev = [] if outs is None else list(outs)
        outs = _pcall(body, name=f"{name}_{lead}", grid=(Ld, r // tr),
                      in_specs=[BS((N_DEV, 1, tr, N), lambda l, i: (0, l, i, 0)), row, row, row]
                      + [BS(memory_space=pl.ANY)] * len(prev),
                      out_specs=[row] * 4, out_shape=[SDS((n_lead * Ld, r, N), F32)] * 4,
                      input_output_aliases={4 + j: j for j in range(len(prev))},
                      compiler_params=_cparams(("parallel", "parallel")))(p.reshape(N_DEV, Ld, r, N), w, m, v, *prev)
    return [t.reshape(shape) for t in outs]


def _place():
    return lax.axis_index("x"), lax.axis_index("y"), lax.axis_index("c")


def _all_gather(xs, name):
    n = len(xs)
    pairs = [(i, l) for i, x in enumerate(xs) for l in range(x.shape[0])]

    def body(*refs):
        x_refs, o_refs = refs[:n], refs[n:2 * n]
        send_sems, recv_sems, local_sems = refs[2 * n:]
        x_, y_, c_ = _place()
        me, sibling = (x_, y_, c_), (x_, y_, 1 - c_)
        chips = [(1 - x_, y_), (x_, 1 - y_), (1 - x_, 1 - y_)]

        def copy(e, k, block, to, from_input=False):
            i, l = pairs[e]
            px, py, pc = block
            dst = o_refs[i].at[l, 4 * px + 2 * py + pc]
            return pltpu.make_async_remote_copy(
                src_ref=x_refs[i].at[l] if from_input else dst, dst_ref=dst,
                send_sem=send_sems.at[7 * e + k], recv_sem=recv_sems.at[7 * e + k],
                device_id=to, device_id_type=pl.DeviceIdType.MESH)

        every = range(len(pairs))
        mine = [pltpu.make_async_copy(x_refs[i].at[l], o_refs[i].at[l, 4 * x_ + 2 * y_ + c_], local_sems.at[e])
                for e, (i, l) in enumerate(pairs)]
        for cp in mine:
            cp.start()
        first = [copy(e, 0, me, sibling, True) for e in every]
        first += [copy(e, 1 + j, me, (*chip, c_), True) for j, chip in enumerate(chips) for e in every]
        for cp in first:
            cp.start()
        passed = []
        for j, chip in enumerate(chips):
            for e in every:
                copy(e, 1 + j, (*chip, c_), me).wait_recv()
            for e in every:
                cp = copy(e, 4 + j, (*chip, c_), sibling)
                cp.start()
                passed.append(cp)
        for e in every:
            copy(e, 0, sibling, me).wait_recv()
        for j, chip in enumerate(chips):
            for e in every:
                copy(e, 4 + j, (*chip, 1 - c_), me).wait_recv()
        for cp in first + passed:
            cp.wait_send()
        for cp in mine:
            cp.wait()

    any_spec = BS(memory_space=pl.ANY)
    return _pcall(body, name=name, in_specs=[any_spec] * n, out_specs=[any_spec] * n,
                  out_shape=[SDS((x.shape[0], N_DEV) + x.shape[1:], x.dtype) for x in xs],
                  scratch_shapes=[pltpu.SemaphoreType.DMA((7 * len(pairs),)), pltpu.SemaphoreType.DMA((7 * len(pairs),)),
                                  pltpu.SemaphoreType.DMA((len(pairs),))],
                  compiler_params=pltpu.CompilerParams(has_side_effects=True))(*xs)


def _peer_of(k, place):
    x_, y_, c_ = place
    fx, fy, fc = (k >> 2) & 1, (k >> 1) & 1, k & 1
    return fx + x_ - 2 * fx * x_, fy + y_ - 2 * fy * y_, fc + c_ - 2 * fc * c_


def _split_copy(src_ref, land_ref, send_sems, recv_sems, e, k, place, scatter):
    x_, y_, c_ = place
    px, py, pc = _peer_of(k, place)
    return pltpu.make_async_remote_copy(
        src_ref=src_ref.at[4 * px + 2 * py + pc] if scatter else src_ref, dst_ref=land_ref.at[4 * x_ + 2 * y_ + c_],
        send_sem=send_sems.at[7 * e + k - 1], recv_sem=recv_sems.at[7 * e + k - 1],
        device_id=(px, py, pc), device_id_type=pl.DeviceIdType.MESH)


_HBM_SPEC = BS(memory_space=pltpu.HBM)
_SEM_SPEC = BS(memory_space=pltpu.SEMAPHORE)
_EFFECT = pltpu.SideEffectType.DATAFLOW_SIDE_EFFECTING


def _copies_start(srcs, scatter, name, thru=None):
    n = len(srcs)
    lands = [lax.empty(s.shape if scatter else (N_DEV,) + s.shape, s.dtype) for s in srcs]
    passed = srcs + lands + list(thru or ())

    def body(*refs):
        src_refs, land_refs = refs[:n], refs[n:2 * n]
        send_sems, recv_sems = refs[len(passed)], refs[len(passed) + 1]
        token = refs[-1]
        place = _place()
        for e in range(n):
            for k in range(1, N_DEV):
                _split_copy(src_refs[e], land_refs[e], send_sems, recv_sems, e, k, place, scatter).start()
        token[...] = jnp.zeros_like(token)

    hbm = lambda t: pltpu.with_memory_space_constraint(t, pltpu.HBM)
    out = _pcall(body, name=name,
                 out_shape=(pltpu.SemaphoreType.DMA((7 * n,)), pltpu.SemaphoreType.DMA((7 * n,)),
                            *[pltpu.HBM(t.shape, t.dtype) for t in passed], SDS((8, LANES), F32)),
                 in_specs=[_HBM_SPEC] * len(passed),
                 out_specs=(_SEM_SPEC, _SEM_SPEC, *[_HBM_SPEC] * len(passed), BS(memory_space=pltpu.VMEM)),
                 input_output_aliases={i: 2 + i for i in range(len(passed))},
                 compiler_params=pltpu.CompilerParams(has_side_effects=_EFFECT))(*[hbm(t) for t in passed])
    return out[0], out[1], list(out[2:2 + n]), list(out[2 + n:2 + 2 * n]), list(out[2 + 2 * n:-1])


def _copies_wait(started, which, scatter, after, name):
    send_sems, recv_sems, srcs, lands, _ = started
    n = len(which)
    after = list(after) if isinstance(after, (list, tuple)) else [after]

    def body(*refs):
        src_refs, land_refs = refs[:n], refs[n:2 * n]
        send_s, recv_s = refs[2 * n], refs[2 * n + 1]
        place = _place()
        for j, e in enumerate(which):
            for k in range(1, N_DEV):
                cp = _split_copy(src_refs[j], land_refs[j], send_s, recv_s, e, k, place, scatter)
                cp.wait_send()
                cp.wait_recv()

    args = [srcs[e] for e in which] + [lands[e] for e in which]
    out = _pcall(body, name=name, out_shape=tuple(pltpu.HBM(t.shape, t.dtype) for t in args),
                 in_specs=[_HBM_SPEC] * (2 * n) + [_SEM_SPEC, _SEM_SPEC] + [BS(memory_space=pl.ANY)] * len(after),
                 out_specs=tuple([_HBM_SPEC] * (2 * n)), input_output_aliases={i: i for i in range(2 * n)},
                 compiler_params=pltpu.CompilerParams(has_side_effects=_EFFECT))(*args, send_sems, recv_sems, *after)
    return list(out[:n]), list(out[n:])


def _with_own_block(land, own_block):
    me = 4 * lax.axis_index("x") + 2 * lax.axis_index("y") + lax.axis_index("c")
    return lax.dynamic_update_index_in_dim(land, own_block, me, 0)


def _pad_flat(t, quantum=PACK_QUANTUM):
    f = t.reshape(-1)
    pad = (-f.shape[0]) % quantum
    return jnp.pad(f, (0, pad)) if pad else f


def _pack(arrs, dtype):
    return jnp.concatenate([_pad_flat(a.astype(dtype)) for a in arrs]).reshape(-1, LANES)


def _unpack(buf, shapes, lead=()):
    flat = buf.reshape(lead + (-1,))
    out, off = [], 0
    for s in shapes:
        n = math.prod(s)
        out.append(flat[..., off:off + n].reshape(lead + tuple(s)))
        off += n + (-n) % PACK_QUANTUM
    return out


def _full_from_gathered(g, axis):
    t = jnp.moveaxis(g, 0, axis)
    s = t.shape
    return t.reshape(s[:axis] + (s[axis] * s[axis + 1],) + s[axis + 2:])


def _parts_from_full(t, axis):
    s = t.shape
    t = t.reshape(s[:axis] + (N_DEV, s[axis] // N_DEV) + s[axis + 1:])
    return jnp.moveaxis(t, axis, 0)


def _head_rows(t):
    return jnp.repeat(t, HEAD_DIM).reshape(1, -1)


def _dilated_attention(q, k, v, name):
    no_sink = jnp.full((1, q.shape[1]), NEG, F32)
    outs, lses = zip(*[band_attention(q, k, v, no_sink, d, ATT_BLOCK, f"{name}_d{d}") for d in DILATIONS])
    return merge3(tuple(outs), tuple(lses), name + "_merge")


STAGES = (
    ("proj0", ('mix_norm',), (('ab_w_in', 0),)),
    ("mixer0", ('lru_conv_w', 'lru_conv_b', 'lru_ba', 'lru_bx', 'lru_lambda'),
     (('lru_wa', 0), ('lru_wx', 0), ('ab_w_out', 0))),
    ("xa0", ('xa_norm', 'xa_mem_norm'), (('xa_wq', 0), ('xa_wkv', 0), ('xa_wo', 0))),
    ("ffn0", ('ffn_norm',), (('ffn_w_gate_up', 0), ('ffn_w_down', 0))),
    ("mixer1", ('mix_norm', 'c_b_qkv', 'c_sinks', 'c_b_out'), (('c_w_qkv', 0), ('c_w_out', 0))),
    ("xa1", ('xa_norm', 'xa_mem_norm'), (('xa_wq', 1), ('xa_wkv', 1), ('xa_wo', 1))),
    ("ffn1", ('ffn_norm',), (('ffn_w_gate_up', 1), ('ffn_w_down', 1))),
)


def _stage_fn(stage, Wb, tabs, mem):
    layer = int(stage[-1])
    L = f"l{layer}"

    def run(S, Cw, h):
        def lin(a, key, bias, add, name, rows=None):
            wb, wc = Wb[key], Cw[key]
            if rows is not None:
                wb, wc = wb[rows], wc[rows]
            return linear(a, wb, wc, bias, add, name)

        def norm_lin(a, gain, key, bias, name):
            return norm_linear(a, gain, Wb[key], Cw[key], bias, name)

        if stage == "mixer0":
            h, proj = h
            C = S['lru_conv_w'].shape[-1]
            xc = conv4(proj[:, :C], S['lru_conv_w'][0], S['lru_conv_b'][0], L + "_conv")
            ga, gx = lru_gates(xc, Wb['lru_wa', 0], Cw['lru_wa', 0], S['lru_ba'][0],
                               Wb['lru_wx', 0], Cw['lru_wx', 0], S['lru_bx'][0], L + "_gates")
            rec = lru_scan(xc, ga, gx, proj[:, C:2 * C], S['lru_lambda'][0], L + "_scan")
            bw = B_HEADS * HEAD_DIM
            q = rope(proj[:, 2 * C:2 * C + bw], tabs, L + "_rope_q")
            k = rope(proj[:, 2 * C + bw:2 * C + 2 * bw], tabs, L + "_rope_k")
            v = proj[:, 2 * C + 2 * bw:]
            att = _dilated_attention(q, k, v, L + "_att")
            h = lin(att, ('ab_w_out', 0), None, h, L + "_w_out_att", slice(C, None))
            return lin(rec, ('ab_w_out', 0), None, h, L + "_w_out_rec", slice(0, C))
        if stage == "mixer1":
            qw = C_HEADS * HEAD_DIM
            kw = C_KV_HEADS * HEAD_DIM
            qkv, h = norm_lin(h, S['mix_norm'][1], ('c_w_qkv', 0), S['c_b_qkv'][0], L + "_w_qkv")
            q = rope(qkv[:, :qw], tabs, L + "_rope_q")
            k = rope(qkv[:, qw:qw + kw], tabs, L + "_rope_k")
            v = qkv[:, qw + kw:]
            o, _ = band_attention(q, k, v, _head_rows(S['c_sinks'][0]), 1, ATT_BLOCK - 1, L + "_att")
            return lin(o, ('c_w_out', 0), S['c_b_out'][0], h, L + "_w_out")
        if stage.startswith("xa"):
            xq, h = norm_lin(h, S['xa_norm'][layer], ('xa_wq', layer), None, L + "_xa_wq")
            xkv, _ = norm_lin(mem, S['xa_mem_norm'][layer], ('xa_wkv', layer), None, L + "_xa_wkv")
            return lin(cross_attention(xq, xkv, L + "_xa"), ('xa_wo', layer), None, h, L + "_xa_wo")
        gu, down = ('ffn_w_gate_up', layer), ('ffn_w_down', layer)
        return ffn_block(h, S['ffn_norm'][layer], Wb[gu], Cw[gu], Wb[down], Cw[down], L + "_ffn")

    return run


def kernel(x, mem, mix_norm, ab_w_in, lru_conv_w, lru_conv_b, lru_wa, lru_ba, lru_wx, lru_bx, lru_lambda, ab_w_out, c_w_qkv, c_b_qkv, c_sinks, c_w_out, c_b_out, xa_norm, xa_mem_norm, xa_wq, xa_wkv, xa_wo, ffn_norm, ffn_w_gate_up, ffn_w_down, final_norm, loss_target, m_mix_norm, m_ab_w_in, m_lru_conv_w, m_lru_conv_b, m_lru_wa, m_lru_ba, m_lru_wx, m_lru_bx, m_lru_lambda, m_ab_w_out, m_c_w_qkv, m_c_b_qkv, m_c_sinks, m_c_w_out, m_c_b_out, m_xa_norm, m_xa_mem_norm, m_xa_wq, m_xa_wkv, m_xa_wo, m_ffn_norm, m_ffn_w_gate_up, m_ffn_w_down, m_final_norm, v_mix_norm, v_ab_w_in, v_lru_conv_w, v_lru_conv_b, v_lru_wa, v_lru_ba, v_lru_wx, v_lru_bx, v_lru_lambda, v_ab_w_out, v_c_w_qkv, v_c_b_qkv, v_c_sinks, v_c_w_out, v_c_b_out, v_xa_norm, v_xa_mem_norm, v_xa_wq, v_xa_wkv, v_xa_wo, v_ffn_norm, v_ffn_w_gate_up, v_ffn_w_down, v_final_norm):
    w_loc = dict(zip(WEIGHT_NAMES, (mix_norm, ab_w_in, lru_conv_w, lru_conv_b, lru_wa, lru_ba, lru_wx, lru_bx, lru_lambda, ab_w_out, c_w_qkv, c_b_qkv, c_sinks, c_w_out, c_b_out, xa_norm, xa_mem_norm, xa_wq, xa_wkv, xa_wo, ffn_norm, ffn_w_gate_up, ffn_w_down, final_norm)))
    m_loc = dict(zip(WEIGHT_NAMES, (m_mix_norm, m_ab_w_in, m_lru_conv_w, m_lru_conv_b, m_lru_wa, m_lru_ba, m_lru_wx, m_lru_bx, m_lru_lambda, m_ab_w_out, m_c_w_qkv, m_c_b_qkv, m_c_sinks, m_c_w_out, m_c_b_out, m_xa_norm, m_xa_mem_norm, m_xa_wq, m_xa_wkv, m_xa_wo, m_ffn_norm, m_ffn_w_gate_up, m_ffn_w_down, m_final_norm)))
    v_loc = dict(zip(WEIGHT_NAMES, (v_mix_norm, v_ab_w_in, v_lru_conv_w, v_lru_conv_b, v_lru_wa, v_lru_ba, v_lru_wx, v_lru_bx, v_lru_lambda, v_ab_w_out, v_c_w_qkv, v_c_b_qkv, v_c_sinks, v_c_w_out, v_c_b_out, v_xa_norm, v_xa_mem_norm, v_xa_wq, v_xa_wkv, v_xa_wo, v_ffn_norm, v_ffn_w_gate_up, v_ffn_w_down, v_final_norm)))

    me = 4 * lax.axis_index("x") + 2 * lax.axis_index("y") + lax.axis_index("c")

    first_keys = list(STAGES[0][2])
    keys = [key for _, _, stage_keys in STAGES[1:] for key in stage_keys]
    shards = [w_loc[n][l].astype(BF16) for n, l in keys]
    first_g = _all_gather([_pack([w_loc[n] for n in SMALL], F32)[None]] + [w_loc[n][l].astype(BF16)[None] for n, l in first_keys],
                          "gather_first")
    gather = _copies_start(shards, False, "gather_start", thru=[first_g[0]])
    small_g = gather[4][0][0]
    Wb = {key: _full_from_gathered(g[0], SHARD_AXIS[key[0]] - 1) for key, g in zip(first_keys, first_g[1:])}
    S = {n: w_loc[n] for n in REPLICATED}
    for n, t in zip(SMALL, _unpack(small_g, [w_loc[n].shape for n in SMALL], lead=(N_DEV,))):
        S[n] = _full_from_gathered(t, SHARD_AXIS[n])

    tabs = _rope_tables(x.shape[1])
    w_in = Wb[first_keys[0]]
    hn0 = _rms_fwd_call(x[0], S['mix_norm'][0], "l0_w_in_norm", BF16)
    h = (x[0], _mm(hn0, w_in, name="l0_w_in", tm=1024, tn=1408, tk=1024))
    vjps = []
    for stage, small_names, stage_keys in STAGES[1:]:
        which = [keys.index(key) for key in stage_keys]
        _, lands = _copies_wait(gather, which, False, jax.tree.leaves(h)[-1], "gather_wait_" + stage)
        for e, land in zip(which, lands):
            Wb[keys[e]] = _full_from_gathered(_with_own_block(land, shards[e]), SHARD_AXIS[keys[e][0]] - 1)
        carriers = {key: jnp.zeros(Wb[key].shape, F32) for key in stage_keys}
        h, vjp_fn = jax.vjp(_stage_fn(stage, Wb, tabs, mem[0]), {n: S[n] for n in small_names}, carriers, h)
        vjps.append(vjp_fn)
    loss_part, dh, dg_final = _final_call(h, S['final_norm'], loss_target[0], "final_loss")

    grads = {'final_norm': dg_final.reshape(final_norm.shape)}
    exchanges, send_keys, send_parts = [], [], []

    def start_exchange(stage, dh):
        leaves, tree = jax.tree.flatten(dh)
        started = _copies_start(list(send_parts), True, "grad_start_" + stage, thru=leaves)
        exchanges.append((stage, started, list(send_keys)))
        send_keys.clear()
        send_parts.clear()
        return jax.tree.unflatten(tree, started[4])

    for (stage, small_names, stage_keys), vjp_fn in zip(reversed(STAGES[1:]), reversed(vjps)):
        g_small, g_big, dh = vjp_fn(dh)
        for n in small_names:
            grads[n] = grads[n] + g_small[n] if n in grads else g_small[n]
        send_keys += list(stage_keys)
        send_parts += [_parts_from_full(g_big[key], SHARD_AXIS[key[0]] - 1) for key in stage_keys]
        if stage == "xa1":
            continue
        if stage == "mixer0":
            small_parts = [_parts_from_full(grads[n], SHARD_AXIS[n]) for n in SMALL]
            send_keys.append("small")
            send_parts.append(jnp.stack([_pack([p[j] for p in small_parts], F32) for j in range(N_DEV)]))
        dh = start_exchange(stage, dh)
    d_res, d_proj = dh
    send_keys.append(first_keys[0])
    send_parts.append(_parts_from_full(_mm(hn0, d_proj, ta=True, name="l0_w_in_dw", tm=1024, tn=1408, tk=1024),
                                       SHARD_AXIS[first_keys[0][0]] - 1))
    d_res, d_proj = start_exchange("proj0", (d_res, d_proj))
    d_hn0 = _mm(d_proj, w_in, tb=True, name="l0_w_in_da", tm=1024, tn=1024, tk=1408)
    dx, dg0 = _rms_bwd_call(x[0], S['mix_norm'][0], d_hn0, "l0_w_in_norm_bwd", add=d_res)
    grads['mix_norm'] = grads['mix_norm'] + jnp.concatenate([dg0, jnp.zeros_like(dg0)], axis=0)
    rep_names = REPLICATED + ["loss"]
    grads["loss"] = loss_part
    zero = jnp.zeros((1, 1), F32)
    for d in (w_loc, m_loc, v_loc):
        d["loss"] = zero
    rep_started = _copies_start([_pack([grads[n] for n in rep_names], F32)], False, "rep_grads_start", thru=[dx])
    dx = rep_started[4][0]

    parts, out = {}, {}

    def end_exchange(stage, started, ex_keys, after):
        srcs, lands = _copies_wait(started, list(range(len(ex_keys))), True, after, "grad_wait_" + stage)
        for key, src, land in zip(ex_keys, srcs, lands):
            parts[key] = _with_own_block(land, lax.dynamic_index_in_dim(src, me, 0, keepdims=False))

    def adamw(p, names, call_name):
        if len(names) == 1:
            res = _adamw_call(p, w_loc[names[0]], m_loc[names[0]], v_loc[names[0]], call_name)
            for kind, t in zip(("grad", "delta", "new_m", "new_v"), res):
                out[kind, names[0]] = t
        else:
            res = _adamw_call(p, *[_pack([d[n] for n in names], F32) for d in (w_loc, m_loc, v_loc)], call_name)
            for kind, buf in zip(("grad", "delta", "new_m", "new_v"), res):
                for n, t in zip(names, _unpack(buf, [w_loc[n].shape for n in names])):
                    out[kind, n] = t

    for ex in exchanges[:-1]:
        end_exchange(*ex, dx)
    last_names = {key[0] for key in exchanges[-1][2]}
    for n in BIG:
        if n not in last_names:
            adamw([parts[n, l] for l in range(w_loc[n].shape[0])], [n], "adamw_" + n)
    adamw(parts["small"], SMALL, "adamw_small")
    end_exchange(*exchanges[-1], [out["new_v", n] for n in BIG if n not in last_names])
    for n in BIG:
        if n in last_names:
            adamw([parts[n, l] for l in range(w_loc[n].shape[0])], [n], "adamw_" + n)
    rep_src, rep_land = _copies_wait(rep_started, [0], False, out["new_v", "ab_w_in"], "rep_grads_wait")
    adamw(_with_own_block(rep_land[0], rep_src[0]), rep_names, "adamw_replicated")
    loss = out["grad", "loss"][0, 0]

    return (loss, dx[None], *[out[kind, n] for kind in ("grad", "delta", "new_m", "new_v") for n in WEIGHT_NAMES])
```

```python
import functools
import math

import jax
import jax.numpy as jnp
from jax import lax
from jax.experimental import pallas as pl
from jax.experimental.pallas import tpu as pltpu

F32 = jnp.float32
BF16 = jnp.bfloat16
SDS = jax.ShapeDtypeStruct
BS = pl.BlockSpec

N_DEV = 8
NORM_EPS = 1e-6
ROPE_THETA = 500000.0
HEAD_DIM = 64
ROT_DIM = 16
ATT_BLOCK = 128
LRU_C = 8.0
LRU_HEADS = 4
DILATIONS = (1, 4, 16)
B_HEADS = 8
C_HEADS = 16
C_KV_HEADS = 2
XA_HEADS = 4
XA_HEAD_DIM = 128
NEG = -1e30
ADAM_LR, ADAM_B1, ADAM_B2, ADAM_EPS, ADAM_WD, ADAM_STEP = 0.001, 0.9, 0.999, 1e-08, 0.01, 10
LANES = 128
VMEM_LIMIT = 48 * 1024 * 1024

WEIGHT_NAMES = ['mix_norm', 'ab_w_in', 'lru_conv_w', 'lru_conv_b', 'lru_wa', 'lru_ba', 'lru_wx', 'lru_bx',
                'lru_lambda', 'ab_w_out', 'c_w_qkv', 'c_b_qkv', 'c_sinks', 'c_w_out', 'c_b_out', 'xa_norm',
                'xa_mem_norm', 'xa_wq', 'xa_wkv', 'xa_wo', 'ffn_norm', 'ffn_w_gate_up', 'ffn_w_down', 'final_norm']
SHARD_AXIS = {'ab_w_in': 2, 'lru_conv_w': 2, 'lru_wa': 2, 'lru_ba': 2, 'lru_wx': 2, 'lru_bx': 2, 'ab_w_out': 1,
              'c_w_qkv': 2, 'c_b_qkv': 1, 'c_w_out': 1, 'c_b_out': 1, 'xa_wq': 1, 'xa_wkv': 1, 'xa_wo': 2,
              'ffn_w_gate_up': 2, 'ffn_w_down': 1}
BIG = ['ab_w_in', 'lru_wa', 'lru_wx', 'ab_w_out', 'c_w_qkv', 'c_w_out', 'xa_wq', 'xa_wkv', 'xa_wo',
       'ffn_w_gate_up', 'ffn_w_down']
SMALL = ['lru_conv_w', 'lru_ba', 'lru_bx', 'c_b_qkv', 'c_b_out']
SHARDED = BIG + SMALL
REPLICATED = [n for n in WEIGHT_NAMES if n not in SHARD_AXIS]
PACK_QUANTUM = 2048


def _pcall(body, **kw):
    return pl.pallas_call(body, **kw)


def _cparams(sem=None):
    return pltpu.CompilerParams(dimension_semantics=sem, vmem_limit_bytes=VMEM_LIMIT)


def _tile(n, target, mult=LANES):
    if n <= target:
        return n
    t = (target // mult) * mult
    while t >= mult:
        if n % t == 0:
            return t
        t -= mult
    return n


def _sigmoid(x):
    return 1.0 / (1.0 + jnp.exp(-x))


def _expm1(x):
    small = x * (1.0 + x * (0.5 + x * (1.0 / 6.0 + x * (1.0 / 24.0))))
    return jnp.where(jnp.abs(x) < 0.03, small, jnp.exp(x) - 1.0)


_GELU_C = math.sqrt(2.0 / math.pi)


def _gelu_parts(y):
    y2 = y * y
    th = jnp.tanh(_GELU_C * (y + 0.044715 * y * y2))
    g = 0.5 * y * (1.0 + th)
    dg = 0.5 * (1.0 + th) + 0.5 * y * (1.0 - th * th) * _GELU_C * (1.0 + 3.0 * 0.044715 * y2)
    return g, dg


def _mm(a, b, *, ta=False, tb=False, bias=None, add=None, name, tm=512, tn=512, tk=2048):
    M, K = (a.shape[1], a.shape[0]) if ta else a.shape
    N = b.shape[0] if tb else b.shape[1]
    tm, tn, tk = _tile(M, tm), _tile(N, tn), _tile(K, tk)
    nk = K // tk
    dn = (((0 if ta else 1,), (1 if tb else 0,)), ((), ()))

    def body(*refs):
        a_ref, b_ref = refs[0], refs[1]
        pos = 2
        bias_ref = add_ref = None
        if bias is not None:
            bias_ref = refs[pos]
            pos += 1
        if add is not None:
            add_ref = refs[pos]
            pos += 1
        o_ref = refs[pos]
        part = lax.dot_general(a_ref[...].astype(BF16), b_ref[...].astype(BF16), dn, preferred_element_type=F32)

        def finish(r):
            if bias_ref is not None:
                r = r + bias_ref[...]
            if add_ref is not None:
                r = r + add_ref[...]
            o_ref[...] = r

        if nk == 1:
            finish(part)
            return
        acc_ref = refs[pos + 1]
        k = pl.program_id(2)

        @pl.when(k == 0)
        def _():
            acc_ref[...] = part

        @pl.when((k > 0) & (k < nk - 1))
        def _():
            acc_ref[...] += part

        @pl.when(k == nk - 1)
        def _():
            finish(acc_ref[...] + part)

    in_specs = [BS((tk, tm), lambda i, j, k: (k, i)) if ta else BS((tm, tk), lambda i, j, k: (i, k)),
                BS((tn, tk), lambda i, j, k: (j, k)) if tb else BS((tk, tn), lambda i, j, k: (k, j))]
    args = [a, b]
    if bias is not None:
        in_specs.append(BS((1, tn), lambda i, j, k: (0, j)))
        args.append(bias.reshape(1, N))
    if add is not None:
        in_specs.append(BS((tm, tn), lambda i, j, k: (i, j)))
        args.append(add)
    return _pcall(body, name=name, grid=(M // tm, N // tn, nk), in_specs=in_specs,
                  out_specs=BS((tm, tn), lambda i, j, k: (i, j)), out_shape=SDS((M, N), F32),
                  scratch_shapes=[pltpu.VMEM((tm, tn), F32)] if nk > 1 else [],
                  compiler_params=_cparams(("parallel", "parallel", "arbitrary")))(*args)


def _mm_rms_bwd(dy, wb, x, g, add, name):
    T, Kc = dy.shape
    Dm = wb.shape[0]
    tm, tk = _tile(T, 1024), _tile(Kc, 1408)
    nk = Kc // tk

    def body(*refs):
        dy_ref, w_ref, x_ref, g_ref = refs[:4]
        dx_ref, dg_ref, acc_ref = refs[-3:]
        i, k = pl.program_id(0), pl.program_id(1)
        part = lax.dot_general(dy_ref[...].astype(BF16), w_ref[...], (((1,), (1,)), ((), ())), preferred_element_type=F32)

        @pl.when(k == 0)
        def _():
            acc_ref[...] = part

        @pl.when((k > 0) & (k < nk - 1))
        def _():
            acc_ref[...] += part

        @pl.when((i == 0) & (k == 0))
        def _():
            dg_ref[...] = jnp.zeros_like(dg_ref)

        @pl.when(k == nk - 1)
        def _():
            d = part if nk == 1 else acc_ref[...] + part
            xv = x_ref[...]
            r = lax.rsqrt(jnp.mean(xv * xv, axis=-1, keepdims=True) + NORM_EPS)
            xh = xv * r
            dyg = d * g_ref[...]
            dx = r * (dyg - xh * jnp.mean(dyg * xh, axis=-1, keepdims=True))
            dx_ref[...] = dx if add is None else dx + refs[4][...]
            dg_ref[...] += jnp.sum(d * xh, axis=0, keepdims=True)

    row = BS((tm, Dm), lambda i, k: (i, 0))
    vec = BS((1, Dm), lambda i, k: (0, 0))
    extra = [] if add is None else [add]
    return _pcall(body, name=name, grid=(T // tm, nk),
                  in_specs=[BS((tm, tk), lambda i, k: (i, k)), BS((Dm, tk), lambda i, k: (0, k)), row, vec] + [row] * len(extra),
                  out_specs=[row, vec], out_shape=[SDS((T, Dm), F32), SDS((1, Dm), F32)],
                  scratch_shapes=[pltpu.VMEM((tm, Dm), F32)],
                  compiler_params=_cparams(("arbitrary", "arbitrary")))(dy, wb, x, g.reshape(1, Dm), *extra)


def _colsum(x, name):
    T, N = x.shape
    tt = _tile(T, 512, 8)

    def body(x_ref, o_ref):
        @pl.when(pl.program_id(0) == 0)
        def _():
            o_ref[...] = jnp.zeros_like(o_ref)

        o_ref[...] += jnp.sum(x_ref[...], axis=0, keepdims=True)

    return _pcall(body, name=name, grid=(T // tt,), in_specs=[BS((tt, N), lambda i: (i, 0))],
                  out_specs=BS((1, N), lambda i: (0, 0)), out_shape=SDS((1, N), F32),
                  compiler_params=_cparams(("arbitrary",)))(x)


@functools.partial(jax.custom_vjp, nondiff_argnums=(5,))
def linear(a, wb, wc, bias, add, name):
    return _mm(a, wb, bias=bias, add=add, name=name)


def _linear_fwd(a, wb, wc, bias, add, name):
    return _mm(a, wb, bias=bias, add=add, name=name), (a, wb, bias is not None, add is not None)


def _linear_bwd(name, res, g):
    a, wb, has_bias, has_add = res
    da = _mm(g, wb, tb=True, name=name + "_da")
    dw = _mm(a, g, ta=True, name=name + "_dw")
    dbias = _colsum(g, name + "_db").reshape(-1) if has_bias else None
    return da, jnp.zeros_like(wb), dw, dbias, (g if has_add else None)


linear.defvjp(_linear_fwd, _linear_bwd)


def _rms_fwd_call(x, g, name, out_dtype=F32):
    T, Dm = x.shape
    tt = _tile(T, 512, 16)

    def body(x_ref, g_ref, o_ref):
        xv = x_ref[...]
        r = lax.rsqrt(jnp.mean(xv * xv, axis=-1, keepdims=True) + NORM_EPS)
        o_ref[...] = ((xv * r) * g_ref[...]).astype(out_dtype)

    return _pcall(body, name=name, grid=(T // tt,),
                  in_specs=[BS((tt, Dm), lambda i: (i, 0)), BS((1, Dm), lambda i: (0, 0))],
                  out_specs=BS((tt, Dm), lambda i: (i, 0)), out_shape=SDS((T, Dm), out_dtype),
                  compiler_params=_cparams(("parallel",)))(x, g.reshape(1, Dm))


@functools.partial(jax.custom_vjp, nondiff_argnums=(5,))
def norm_linear(x, g, wb, wc, bias, name):
    return _mm(_rms_fwd_call(x, g, name + "_norm", BF16), wb, bias=bias, name=name, tm=1024, tn=1408, tk=1024), x


def _norm_linear_fwd(x, g, wb, wc, bias, name):
    hn = _rms_fwd_call(x, g, name + "_norm", BF16)
    return (_mm(hn, wb, bias=bias, name=name, tm=1024, tn=1408, tk=1024), x), (x, g, hn, wb, bias is not None)


def _norm_linear_bwd(name, res, cts):
    x, g, hn, wb, has_bias = res
    dy, dres = cts
    dw = _mm(hn, dy, ta=True, name=name + "_dw", tm=1024, tn=1408, tk=1024)
    dx, dg = _mm_rms_bwd(dy, wb, x, g, dres, name + "_da")
    dbias = _colsum(dy, name + "_db").reshape(-1) if has_bias else None
    return dx, dg.reshape(g.shape), jnp.zeros_like(wb), dw, dbias


norm_linear.defvjp(_norm_linear_fwd, _norm_linear_bwd)


def _rope_tables(T):
    half = ROT_DIM // 2
    inv = ROPE_THETA ** (-jnp.arange(0, ROT_DIM, 2, dtype=F32) / ROT_DIM)
    lane = jnp.arange(LANES) % HEAD_DIM
    freq = jnp.where(lane < ROT_DIM, inv[lane % half], 0.0)
    ang = jnp.arange(T, dtype=F32)[:, None] * freq[None, :]
    c, s = jnp.cos(ang), jnp.sin(ang)
    sa = jnp.where((lane >= half) & (lane < ROT_DIM), s, 0.0)
    sb = jnp.where(lane < half, -s, 0.0)
    return c, sa, sb


def _rope_call(x, tabs, inverse, name):
    T, W = x.shape
    tt = _tile(T, 512, 8)
    reps = W // LANES
    half = ROT_DIM // 2

    def body(x_ref, c_ref, sa_ref, sb_ref, o_ref):
        xv = x_ref[...]
        c = jnp.tile(c_ref[...], (1, reps))
        sa = jnp.tile(sa_ref[...], (1, reps))
        sb = jnp.tile(sb_ref[...], (1, reps))
        if not inverse:
            o_ref[...] = xv * c + pltpu.roll(xv, half, axis=1) * sa + pltpu.roll(xv, W - half, axis=1) * sb
        else:
            o_ref[...] = xv * c + pltpu.roll(xv * sa, W - half, axis=1) + pltpu.roll(xv * sb, half, axis=1)

    row = BS((tt, W), lambda i: (i, 0))
    tab = BS((tt, LANES), lambda i: (i, 0))
    return _pcall(body, name=name, grid=(T // tt,), in_specs=[row, tab, tab, tab], out_specs=row,
                  out_shape=SDS((T, W), F32), compiler_params=_cparams(("parallel",)))(x, *tabs)


@functools.partial(jax.custom_vjp, nondiff_argnums=(2,))
def rope(x, tabs, name):
    return _rope_call(x, tabs, False, name)


def _rope_fwd(x, tabs, name):
    return _rope_call(x, tabs, False, name), tabs


def _rope_bwd(name, tabs, dy):
    return _rope_call(dy, tabs, True, name + "_bwd"), jax.tree.map(jnp.zeros_like, tabs)


rope.defvjp(_rope_fwd, _rope_bwd)


def _conv_fwd_call(x, w, b, name):
    T, C = x.shape
    tt = _tile(T, 512, 8)
    per = tt // 8

    def body(x_ref, halo_ref, w_ref, b_ref, o_ref):
        i = pl.program_id(0)
        halo = jnp.where(i > 0, halo_ref[...], 0.0)
        e = jnp.concatenate([halo, x_ref[...]], axis=0)
        acc = b_ref[...]
        for k in (3, 2, 1):
            acc = acc + pltpu.roll(e, k, axis=0)[8:, :] * w_ref[3 - k:4 - k, :]
        o_ref[...] = acc + x_ref[...] * w_ref[3:4, :]

    return _pcall(body, name=name, grid=(T // tt,),
                  in_specs=[BS((tt, C), lambda i: (i, 0)), BS((8, C), lambda i: (jnp.maximum(i * per - 1, 0), 0)),
                            BS((4, C), lambda i: (0, 0)), BS((1, C), lambda i: (0, 0))],
                  out_specs=BS((tt, C), lambda i: (i, 0)), out_shape=SDS((T, C), F32),
                  compiler_params=_cparams(("parallel",)))(x, x, w, b.reshape(1, C))


def _conv_bwd_call(x, w, dy, name):
    T, C = x.shape
    tt = _tile(T, 512, 8)
    per = tt // 8
    nt = T // tt

    def body(x_ref, halo_ref, w_ref, dy_ref, nxt_ref, dx_ref, dwb_ref):
        i = pl.program_id(0)
        halo = jnp.where(i > 0, halo_ref[...], 0.0)
        e = jnp.concatenate([halo, x_ref[...]], axis=0)
        dy = dy_ref[...]
        nxt = jnp.where(i < nt - 1, nxt_ref[...], 0.0)
        f = jnp.concatenate([dy, nxt], axis=0)
        dx = dy * w_ref[3:4, :]
        rows = [None] * 4
        rows[3] = jnp.sum(dy * x_ref[...], axis=0, keepdims=True)
        for k in (1, 2, 3):
            dx = dx + pltpu.roll(f, tt + 8 - k, axis=0)[:tt, :] * w_ref[3 - k:4 - k, :]
            rows[3 - k] = jnp.sum(dy * pltpu.roll(e, k, axis=0)[8:, :], axis=0, keepdims=True)
        dx_ref[...] = dx
        upd = jnp.concatenate(rows + [jnp.sum(dy, axis=0, keepdims=True), jnp.zeros((3, C), F32)], axis=0)

        @pl.when(i == 0)
        def _():
            dwb_ref[...] = jnp.zeros_like(dwb_ref)

        dwb_ref[...] += upd

    row = BS((tt, C), lambda i: (i, 0))
    return _pcall(body, name=name, grid=(nt,),
                  in_specs=[row, BS((8, C), lambda i: (jnp.maximum(i * per - 1, 0), 0)), BS((4, C), lambda i: (0, 0)),
                            row, BS((8, C), lambda i: (jnp.minimum((i + 1) * per, T // 8 - 1), 0))],
                  out_specs=[row, BS((8, C), lambda i: (0, 0))],
                  out_shape=[SDS((T, C), F32), SDS((8, C), F32)],
                  compiler_params=_cparams(("arbitrary",)))(x, x, w, dy, dy)


@functools.partial(jax.custom_vjp, nondiff_argnums=(3,))
def conv4(x, w, b, name):
    return _conv_fwd_call(x, w, b, name)


def _conv4_fwd(x, w, b, name):
    return _conv_fwd_call(x, w, b, name), (x, w)


def _conv4_bwd(name, res, dy):
    x, w = res
    dx, dwb = _conv_bwd_call(x, w, dy, name + "_bwd")
    return dx, dwb[0:4], dwb[4]


conv4.defvjp(_conv4_fwd, _conv4_bwd)


def _gates_fwd_call(xc, wa, ba, wx, bx, name):
    T, C = xc.shape
    hd = C // LRU_HEADS
    tt = _tile(T, 512, 8)

    def body(x_ref, wa_ref, ba_ref, wx_ref, bx_ref, ga_ref, gx_ref):
        xb = x_ref[...].astype(BF16)
        ga_ref[...] = jnp.dot(xb, wa_ref[0].astype(BF16), preferred_element_type=F32) + ba_ref[...]
        gx_ref[...] = jnp.dot(xb, wx_ref[0].astype(BF16), preferred_element_type=F32) + bx_ref[...]

    blk = BS((tt, hd), lambda i, h: (i, h))
    wsp = BS((1, hd, hd), lambda i, h: (h, 0, 0))
    bsp = BS((1, hd), lambda i, h: (0, h))
    return _pcall(body, name=name, grid=(T // tt, LRU_HEADS), in_specs=[blk, wsp, bsp, wsp, bsp],
                  out_specs=[blk, blk], out_shape=[SDS((T, C), F32)] * 2,
                  compiler_params=_cparams(("parallel", "parallel")))(xc, wa, ba.reshape(1, C), wx, bx.reshape(1, C))


def _gates_bwd_x_call(dga, dgx, wa, wx, name):
    T, C = dga.shape
    hd = C // LRU_HEADS
    tt = _tile(T, 512, 8)
    dn = (((1,), (1,)), ((), ()))

    def body(da_ref, dx_ref, wa_ref, wx_ref, o_ref):
        o_ref[...] = (lax.dot_general(da_ref[...].astype(BF16), wa_ref[0].astype(BF16), dn, preferred_element_type=F32)
                      + lax.dot_general(dx_ref[...].astype(BF16), wx_ref[0].astype(BF16), dn, preferred_element_type=F32))

    blk = BS((tt, hd), lambda i, h: (i, h))
    wsp = BS((1, hd, hd), lambda i, h: (h, 0, 0))
    return _pcall(body, name=name, grid=(T // tt, LRU_HEADS), in_specs=[blk, blk, wsp, wsp], out_specs=blk,
                  out_shape=SDS((T, C), F32), compiler_params=_cparams(("parallel", "parallel")))(dga, dgx, wa, wx)


def _gates_bwd_w_call(xc, dga, dgx, name):
    T, C = xc.shape
    hd = C // LRU_HEADS
    tt = _tile(T, 512, 8)
    dn = (((0,), (0,)), ((), ()))

    def body(x_ref, da_ref, dx_ref, dwa_ref, dwx_ref, dba_ref, dbx_ref):
        @pl.when(pl.program_id(1) == 0)
        def _():
            dwa_ref[...] = jnp.zeros_like(dwa_ref)
            dwx_ref[...] = jnp.zeros_like(dwx_ref)
            dba_ref[...] = jnp.zeros_like(dba_ref)
            dbx_ref[...] = jnp.zeros_like(dbx_ref)

        xb = x_ref[...].astype(BF16)
        da, dx = da_ref[...], dx_ref[...]
        dwa_ref[0] += lax.dot_general(xb, da.astype(BF16), dn, preferred_element_type=F32)
        dwx_ref[0] += lax.dot_general(xb, dx.astype(BF16), dn, preferred_element_type=F32)
        dba_ref[...] += jnp.sum(da, axis=0, keepdims=True)
        dbx_ref[...] += jnp.sum(dx, axis=0, keepdims=True)

    blk = BS((tt, hd), lambda h, i: (i, h))
    wsp = BS((1, hd, hd), lambda h, i: (h, 0, 0))
    bsp = BS((1, hd), lambda h, i: (0, h))
    return _pcall(body, name=name, grid=(LRU_HEADS, T // tt), in_specs=[blk, blk, blk],
                  out_specs=[wsp, wsp, bsp, bsp],
                  out_shape=[SDS((LRU_HEADS, hd, hd), F32)] * 2 + [SDS((1, C), F32)] * 2,
                  compiler_params=_cparams(("parallel", "arbitrary")))(xc, dga, dgx)


@functools.partial(jax.custom_vjp, nondiff_argnums=(7,))
def lru_gates(xc, wa, wa_c, ba, wx, wx_c, bx, name):
    return tuple(_gates_fwd_call(xc, wa, ba, wx, bx, name))


def _lru_gates_fwd(xc, wa, wa_c, ba, wx, wx_c, bx, name):
    return tuple(_gates_fwd_call(xc, wa, ba, wx, bx, name)), (xc, wa, wx, ba.shape)


def _lru_gates_bwd(name, res, g):
    xc, wa, wx, bshape = res
    dga, dgx = g
    dxc = _gates_bwd_x_call(dga, dgx, wa, wx, name + "_dx")
    dwa, dwx, dba, dbx = _gates_bwd_w_call(xc, dga, dgx, name + "_dw")
    return dxc, jnp.zeros_like(wa), dwa, dba.reshape(bshape), jnp.zeros_like(wx), dwx, dbx.reshape(bshape)


lru_gates.defvjp(_lru_gates_fwd, _lru_gates_bwd)


def _lru_coeffs(xc, ga, gx, lam):
    r = _sigmoid(ga)
    ig = _sigmoid(gx)
    z = -lam
    sp = jnp.maximum(z, 0.0) + jnp.log(1.0 + jnp.exp(-jnp.abs(z)))
    la = -LRU_C * r * sp
    a = jnp.exp(la)
    s = jnp.sqrt(-_expm1(2.0 * la))
    return r, ig, sp, a, s


LRU_TT = 256


def _scan_fwd_call(xc, ga, gx, y, lam, name):
    T, C = xc.shape
    tt = _tile(T, LRU_TT, 8)

    def body(xc_ref, ga_ref, gx_ref, y_ref, lam_ref, h_ref, rec_ref, a_buf, carry):
        @pl.when(pl.program_id(0) == 0)
        def _():
            carry[...] = jnp.zeros_like(carry)

        xcv = xc_ref[...]
        _, ig, _, a, s = _lru_coeffs(xcv, ga_ref[...], gx_ref[...], lam_ref[...])
        a_buf[...] = a
        h_ref[...] = s * (ig * xcv)

        def step(t, h):
            hn = a_buf[pl.ds(t, 1), :] * h + h_ref[pl.ds(t, 1), :]
            h_ref[pl.ds(t, 1), :] = hn
            return hn

        carry[0:1, :] = lax.fori_loop(0, tt, step, carry[0:1, :], unroll=8)
        g, _ = _gelu_parts(y_ref[...])
        rec_ref[...] = h_ref[...] * g

    row = BS((tt, C), lambda i: (i, 0))
    vec = BS((1, C), lambda i: (0, 0))
    return _pcall(body, name=name, grid=(T // tt,), in_specs=[row, row, row, row, vec], out_specs=[row, row],
                  out_shape=[SDS((T, C), F32)] * 2,
                  scratch_shapes=[pltpu.VMEM((tt, C), F32), pltpu.VMEM((8, C), F32)],
                  compiler_params=_cparams(("arbitrary",)))(xc, ga, gx, y, lam.reshape(1, C))


def _scan_bwd_call(xc, ga, gx, y, lam, h, drec, name):
    T, C = xc.shape
    tt = _tile(T, LRU_TT, 8)
    nt = T // tt
    per = tt // 8

    def body(xc_ref, ga_ref, gx_ref, y_ref, lam_ref, h_ref, halo_ref, dr_ref,
             dga_ref, dgx_ref, dxc_ref, dy_ref, dlam_ref, a_buf, g_buf, carry):
        i = pl.program_id(0)

        @pl.when(i == 0)
        def _():
            carry[...] = jnp.zeros_like(carry)
            dlam_ref[...] = jnp.zeros_like(dlam_ref)

        xcv, lam = xc_ref[...], lam_ref[...]
        r, ig, sp, a, s = _lru_coeffs(xcv, ga_ref[...], gx_ref[...], lam)
        gel, dgel = _gelu_parts(y_ref[...])
        drec = dr_ref[...]
        hv = h_ref[...]
        dy_ref[...] = drec * hv * dgel
        a_buf[...] = a
        g_buf[...] = drec * gel

        def step(j, q):
            t = tt - 1 - j
            g = g_buf[pl.ds(t, 1), :] + q
            g_buf[pl.ds(t, 1), :] = g
            return a_buf[pl.ds(t, 1), :] * g

        carry[0:1, :] = lax.fori_loop(0, tt, step, carry[0:1, :], unroll=8)
        g = g_buf[...]
        halo = jnp.where(i < nt - 1, halo_ref[...], 0.0)
        hprev = pltpu.roll(jnp.concatenate([halo, hv], axis=0), 1, axis=0)[8:, :]
        da = g * hprev
        dig = g * s * xcv
        ds = g * ig * xcv
        dla = da * a - ds * (a * a) / s
        dga_ref[...] = dla * (-LRU_C * sp) * r * (1.0 - r)
        dgx_ref[...] = dig * ig * (1.0 - ig)
        dxc_ref[...] = g * s * ig
        dlam_ref[...] += jnp.sum(dla * r, axis=0, keepdims=True) * (LRU_C * _sigmoid(-lam))

    row = BS((tt, C), lambda i: (nt - 1 - i, 0))
    vec = BS((1, C), lambda i: (0, 0))
    halo = BS((8, C), lambda i: (jnp.maximum((nt - 1 - i) * per - 1, 0), 0))
    return _pcall(body, name=name, grid=(nt,), in_specs=[row, row, row, row, vec, row, halo, row],
                  out_specs=[row, row, row, row, vec], out_shape=[SDS((T, C), F32)] * 4 + [SDS((1, C), F32)],
                  scratch_shapes=[pltpu.VMEM((tt, C), F32), pltpu.VMEM((tt, C), F32), pltpu.VMEM((8, C), F32)],
                  compiler_params=_cparams(("arbitrary",)))(xc, ga, gx, y, lam.reshape(1, C), h, h, drec)


@functools.partial(jax.custom_vjp, nondiff_argnums=(5,))
def lru_scan(xc, ga, gx, y, lam, name):
    return _scan_fwd_call(xc, ga, gx, y, lam, name)[1]


def _lru_scan_fwd(xc, ga, gx, y, lam, name):
    h, rec = _scan_fwd_call(xc, ga, gx, y, lam, name)
    return rec, (xc, ga, gx, y, lam, h)


def _lru_scan_bwd(name, res, drec):
    xc, ga, gx, y, lam, h = res
    dga, dgx, dxc, dy, dlam = _scan_bwd_call(xc, ga, gx, y, lam, h, drec, name + "_bwd")
    return dxc, dga, dgx, dy, dlam.reshape(lam.shape)


lru_scan.defvjp(_lru_scan_fwd, _lru_scan_bwd)


def _att_batch(d):
    return (4, 1) if d == 1 else (1, 4)


def _att_masks(n, max_dist):
    qi = lax.broadcasted_iota(jnp.int32, (1, 2 * ATT_BLOCK, 2 * ATT_BLOCK), 1) & (ATT_BLOCK - 1)
    kj = lax.broadcasted_iota(jnp.int32, (1, 2 * ATT_BLOCK, 2 * ATT_BLOCK), 2)
    prev = (kj < ATT_BLOCK) & (kj >= qi + (ATT_BLOCK - max_dist)) & (n > 0)
    cur = (kj >= ATT_BLOCK) & (kj - ATT_BLOCK <= qi)
    return prev | cur


def _lo_lanes(rows):
    return lax.broadcasted_iota(jnp.int32, (rows, LANES), 1) < HEAD_DIM


def _lane_half(rows):
    return lax.broadcasted_iota(jnp.int32, (rows, LANES), 1) // HEAD_DIM


def _stack_heads(x2):
    lo = _lo_lanes(ATT_BLOCK)
    zero = jnp.zeros_like(x2)
    return jnp.concatenate([jnp.where(lo, x2, zero), jnp.where(lo, zero, x2)], axis=0)


def _unstack_heads(y):
    return jnp.where(_lo_lanes(ATT_BLOCK), y[:ATT_BLOCK], y[ATT_BLOCK:])


def _per_head_col(x2):
    return jnp.concatenate([x2[:, 0:1], x2[:, HEAD_DIM:HEAD_DIM + 1]], axis=0)


def _head_sums(x2):
    lo = _lo_lanes(ATT_BLOCK)
    return jnp.concatenate([jnp.sum(jnp.where(lo, x2, 0.0), axis=1, keepdims=True),
                            jnp.sum(jnp.where(lo, 0.0, x2), axis=1, keepdims=True)], axis=0)


def _att_specs(d, Wq, Wk, nb, clamp):
    cgw, sb = _att_batch(d)
    shared = Wk != Wq
    cur = (lambda n: jnp.minimum(n, nb - 1)) if clamp else (lambda n: n)
    rows, qw, kw = ATT_BLOCK * d, cgw * LANES, (LANES if shared else cgw * LANES)
    kcol = (lambda g: 0) if shared else (lambda g: g)
    qsp = BS((rows, qw), lambda g, n: (cur(n), g))
    csp = BS((rows, kw), lambda g, n: (cur(n), kcol(g)))
    psp = BS((rows, kw), lambda g, n: (jnp.maximum(cur(n) - 1, 0), kcol(g)))
    return cgw, sb, shared, qsp, csp, psp, qw, kw


def _att_streams(d, sb, work):
    if d == 1:
        work([slice(None)])
        return

    def one(j, carry):
        work([pl.ds(j * sb + i, ATT_BLOCK, stride=d) for i in range(sb)])
        return carry

    lax.fori_loop(0, d // sb, one, 0)


def _att_problem_loads(rows, cgw, shared, g, q_ref, kc_ref, kp_ref, vc_ref, vp_ref, sk_ref):
    half = _lane_half(ATT_BLOCK)

    def kv(ref, r, p):
        x = ref[r, :]
        if not shared:
            return x[:, p * LANES:(p + 1) * LANES]
        return jnp.where(half == g, x, pltpu.roll(x, HEAD_DIM, axis=1))

    qs, kb, vb, sk = [], [], [], []
    for r in rows:
        qrow = q_ref[r, :]
        for p in range(cgw):
            cols = slice(p * LANES, (p + 1) * LANES)
            qs.append(_stack_heads(qrow[:, cols].astype(BF16)))
            kb.append(jnp.concatenate([kv(kp_ref, r, p), kv(kc_ref, r, p)], axis=0).astype(BF16))
            vb.append(jnp.concatenate([kv(vp_ref, r, p), kv(vc_ref, r, p)], axis=0).astype(BF16))
            sk.append(_per_head_col(jnp.broadcast_to(sk_ref[:, cols], (ATT_BLOCK, LANES))))
    return jnp.stack(qs), jnp.stack(kb), jnp.stack(vb), jnp.stack(sk)


_BDOT_NT = (((2,), (2,)), ((0,), (0,)))
_BDOT_NN = (((2,), (1,)), ((0,), (0,)))
_BDOT_TN = (((1,), (1,)), ((0,), (0,)))


def _att_fwd_call(q, k, v, sinks, d, max_dist, name):
    T, Wq = q.shape
    Wk = k.shape[1]
    nb = T // (d * ATT_BLOCK)
    cgw, sb, shared, qsp, csp, psp, qw, kw = _att_specs(d, Wq, Wk, nb, False)
    G = Wq // qw
    assert not shared or (d == 1 and Wk == LANES and G == 2), "a shared kv pair serves two groups of 8 query heads"

    def body(q_ref, kc_ref, kp_ref, vc_ref, vp_ref, sk_ref, o_ref, lse_ref):
        g, n = pl.program_id(0), pl.program_id(1)

        def work(rows):
            qs, kband, vband, sk = _att_problem_loads(rows, cgw, shared, g, q_ref, kc_ref, kp_ref, vc_ref, vp_ref, sk_ref)
            s = lax.dot_general(qs, kband, _BDOT_NT, preferred_element_type=F32) * (HEAD_DIM ** -0.5)
            s = jnp.where(_att_masks(n, max_dist), s, NEG)
            m = jnp.maximum(jnp.max(s, axis=-1, keepdims=True), sk)
            e = jnp.exp(s - m)
            den = jnp.sum(e, axis=-1, keepdims=True) + jnp.exp(sk - m)
            o = lax.dot_general((e * (1.0 / den)).astype(BF16), vband, _BDOT_NN, preferred_element_type=F32)
            lse = jnp.broadcast_to(m + jnp.log(den), o.shape)
            for i, r in enumerate(rows):
                o_ref[r, :] = jnp.concatenate([_unstack_heads(o[i * cgw + p]) for p in range(cgw)], axis=1)
                lse_ref[r, :] = jnp.concatenate([_unstack_heads(lse[i * cgw + p]) for p in range(cgw)], axis=1)

        _att_streams(d, sb, work)

    sksp = BS((1, qw), lambda g, n: (0, g))
    return _pcall(body, name=name, grid=(G, nb), in_specs=[qsp, csp, psp, csp, psp, sksp], out_specs=[qsp, qsp],
                  out_shape=[SDS((T, Wq), F32)] * 2,
                  compiler_params=_cparams(("parallel", "parallel")))(q, k, k, v, v, sinks)


def _att_bwd_call(q, k, v, sinks, o, lse, do, dlse, d, max_dist, name):
    T, Wq = q.shape
    Wk = k.shape[1]
    nb = T // (d * ATT_BLOCK)
    cgw, sb, shared, qsp, csp, psp, qw, kw = _att_specs(d, Wq, Wk, nb, True)
    G = Wq // qw
    scale = HEAD_DIM ** -0.5

    def body(q_ref, kc_ref, kp_ref, vc_ref, vp_ref, sk_ref, o_ref, lse_ref, do_ref, dlse_ref,
             dq_ref, dk_ref, dv_ref, dsk_ref, ck, cv):
        g, n = pl.program_id(0), pl.program_id(1)

        @pl.when(n == 0)
        def _():
            ck[...] = jnp.zeros_like(ck)
            cv[...] = jnp.zeros_like(cv)
            dsk_ref[...] = jnp.zeros_like(dsk_ref)

        def put(ref, r, val):
            if len(ref.shape) == 3:
                ref[0, r, :] = val
            else:
                ref[r, :] = val

        def work(rows):
            qs, kband, vband, sk = _att_problem_loads(rows, cgw, shared, g, q_ref, kc_ref, kp_ref, vc_ref, vp_ref, sk_ref)
            dos, lse_c, corr = [], [], []
            for r in rows:
                do_r, o_r, lse_r, dlse_r = do_ref[r, :], o_ref[r, :], lse_ref[r, :], dlse_ref[r, :]
                for p in range(cgw):
                    cols = slice(p * LANES, (p + 1) * LANES)
                    dos.append(_stack_heads(do_r[:, cols].astype(BF16)))
                    lse_c.append(_per_head_col(lse_r[:, cols]))
                    corr.append(_head_sums(dlse_r[:, cols]) - _head_sums(do_r[:, cols] * o_r[:, cols]))
            dos, lse_c, corr = jnp.stack(dos), jnp.stack(lse_c), jnp.stack(corr)
            s = lax.dot_general(qs, kband, _BDOT_NT, preferred_element_type=F32) * scale
            pr = jnp.exp(jnp.where(_att_masks(n, max_dist), s, NEG) - lse_c)
            dp = lax.dot_general(dos, vband, _BDOT_NT, preferred_element_type=F32)
            ds = (pr * (dp + corr)).astype(BF16)
            dq = lax.dot_general(ds, kband, _BDOT_NN, preferred_element_type=F32) * scale
            dkb = lax.dot_general(ds, qs, _BDOT_TN, preferred_element_type=F32) * scale
            dvb = lax.dot_general(pr.astype(BF16), dos, _BDOT_TN, preferred_element_type=F32)
            dsk = jnp.exp(sk - lse_c) * corr
            lane = lax.broadcasted_iota(jnp.int32, (8, LANES), 1)
            for p in range(cgw):
                tot = [jnp.sum(jnp.stack([dsk[i * cgw + p, h * ATT_BLOCK:(h + 1) * ATT_BLOCK] for i in range(len(rows))]),
                               axis=(0, 1)).reshape(1, 1) for h in range(2)]
                dsk_ref[:, p * LANES:(p + 1) * LANES] += jnp.where(lane == 0, tot[0], jnp.where(lane == HEAD_DIM, tot[1], 0.0))

            def gather_pairs(parts):
                if not shared:
                    return jnp.concatenate(parts, axis=1)
                tot = parts[0]
                for t in parts[1:]:
                    tot = tot + t
                tot = tot + pltpu.roll(tot, HEAD_DIM, axis=1)
                return jnp.where(_lane_half(ATT_BLOCK) == g, tot, 0.0)

            for i, r in enumerate(rows):
                mine = range(i * cgw, (i + 1) * cgw)
                dq_ref[r, :] = jnp.concatenate([_unstack_heads(dq[b]) for b in mine], axis=1)
                put(dk_ref, r, ck[r, :] + gather_pairs([dkb[b, :ATT_BLOCK] for b in mine]))
                put(dv_ref, r, cv[r, :] + gather_pairs([dvb[b, :ATT_BLOCK] for b in mine]))
                ck[r, :] = gather_pairs([dkb[b, ATT_BLOCK:] for b in mine])
                cv[r, :] = gather_pairs([dvb[b, ATT_BLOCK:] for b in mine])

        @pl.when(n < nb)
        def _():
            _att_streams(d, sb, work)

        @pl.when(n == nb)
        def _():
            dk_ref[...] = ck[...].reshape(dk_ref.shape)
            dv_ref[...] = cv[...].reshape(dv_ref.shape)

    sksp = BS((1, qw), lambda g, n: (0, g))
    rows = ATT_BLOCK * d
    if shared:
        osp = BS((1, rows, kw), lambda g, n: (g, jnp.maximum(n - 1, 0), 0))
        kshape = SDS((G, T, Wk), F32)
    else:
        osp = BS((rows, kw), lambda g, n: (jnp.maximum(n - 1, 0), g))
        kshape = SDS((T, Wk), F32)
    dq, dk, dv, dsk = _pcall(
        body, name=name, grid=(G, nb + 1),
        in_specs=[qsp, csp, psp, csp, psp, sksp, qsp, qsp, qsp, qsp],
        out_specs=[qsp, osp, osp, BS((8, qw), lambda g, n: (0, g))],
        out_shape=[SDS((T, Wq), F32), kshape, kshape, SDS((8, Wq), F32)],
        scratch_shapes=[pltpu.VMEM((rows, kw), F32)] * 2,
        compiler_params=_cparams(("parallel", "arbitrary")))(q, k, k, v, v, sinks, o, lse, do, dlse)
    if shared:
        dk, dv = jnp.sum(dk, axis=0), jnp.sum(dv, axis=0)
    return dq, dk, dv, dsk[0:1]


@functools.partial(jax.custom_vjp, nondiff_argnums=(4, 5, 6))
def band_attention(q, k, v, sinks, d, max_dist, name):
    return _att_fwd_call(q, k, v, sinks, d, max_dist, name)


def _band_attention_fwd(q, k, v, sinks, d, max_dist, name):
    o, lse = _att_fwd_call(q, k, v, sinks, d, max_dist, name)
    return (o, lse), (q, k, v, sinks, o, lse)


def _band_attention_bwd(d, max_dist, name, res, g):
    q, k, v, sinks, o, lse = res
    do, dlse = g
    return _att_bwd_call(q, k, v, sinks, o, lse, do, dlse, d, max_dist, name + "_bwd")


band_attention.defvjp(_band_attention_fwd, _band_attention_bwd)


def _merge_weights(ls):
    mx = jnp.maximum(jnp.maximum(ls[0], ls[1]), ls[2])
    es = [jnp.exp(l - mx) for l in ls]
    inv = 1.0 / (es[0] + es[1] + es[2])
    return [e * inv for e in es]


def _merge_fwd_call(os_, ls_, name):
    T, W = os_[0].shape
    tt = _tile(T, 512, 8)

    def body(o1, o2, o3, l1, l2, l3, out_ref):
        w = _merge_weights([l1[...], l2[...], l3[...]])
        out_ref[...] = w[0] * o1[...] + w[1] * o2[...] + w[2] * o3[...]

    row = BS((tt, W), lambda i: (i, 0))
    return _pcall(body, name=name, grid=(T // tt,), in_specs=[row] * 6, out_specs=row,
                  out_shape=SDS((T, W), F32), compiler_params=_cparams(("parallel",)))(*os_, *ls_)


def _merge_bwd_call(os_, ls_, do, name):
    T, W = os_[0].shape
    tt = _tile(T, 512, 8)

    def body(o1, o2, o3, l1, l2, l3, do_ref, d1, d2, d3, e1, e2, e3):
        w = _merge_weights([l1[...], l2[...], l3[...]])
        dov = do_ref[...]
        ts = [dov * o[...] for o in (o1, o2, o3)]
        mean = w[0] * ts[0] + w[1] * ts[1] + w[2] * ts[2]
        for wi, ti, dref, eref in zip(w, ts, (d1, d2, d3), (e1, e2, e3)):
            dref[...] = wi * dov
            eref[...] = wi * (ti - mean)

    row = BS((tt, W), lambda i: (i, 0))
    return _pcall(body, name=name, grid=(T // tt,), in_specs=[row] * 7, out_specs=[row] * 6,
                  out_shape=[SDS((T, W), F32)] * 6, compiler_params=_cparams(("parallel",)))(*os_, *ls_, do)


@functools.partial(jax.custom_vjp, nondiff_argnums=(2,))
def merge3(os_, ls_, name):
    return _merge_fwd_call(os_, ls_, name)


def _merge3_fwd(os_, ls_, name):
    return _merge_fwd_call(os_, ls_, name), (os_, ls_)


def _merge3_bwd(name, res, do):
    os_, ls_ = res
    out = _merge_bwd_call(os_, ls_, do, name + "_bwd")
    return tuple(out[:3]), tuple(out[3:])


merge3.defvjp(_merge3_fwd, _merge3_bwd)


def _xa_probs(qb, kb, scale):
    s = lax.dot_general(qb, kb, (((1,), (1,)), ((), ())), preferred_element_type=F32) * scale
    e = jnp.exp(s - jnp.max(s, axis=-1, keepdims=True))
    return e / jnp.sum(e, axis=-1, keepdims=True)


def _xa_fwd_call(q, kv, name):
    T, W = q.shape
    M = kv.shape[0]
    hd = XA_HEAD_DIM
    tq = _tile(T, 512, 8)
    scale = hd ** -0.5

    def body(q_ref, k_ref, v_ref, o_ref):
        p = _xa_probs(q_ref[...].astype(BF16), k_ref[...].astype(BF16), scale)
        o_ref[...] = jnp.dot(p.astype(BF16), v_ref[...].astype(BF16), preferred_element_type=F32)

    qsp = BS((tq, hd), lambda i, h: (i, h))
    return _pcall(body, name=name, grid=(T // tq, XA_HEADS),
                  in_specs=[qsp, BS((M, hd), lambda i, h: (0, h)), BS((M, hd), lambda i, h: (0, XA_HEADS + h))],
                  out_specs=qsp, out_shape=SDS((T, W), F32),
                  compiler_params=_cparams(("parallel", "parallel")))(q, kv, kv)


def _xa_bwd_call(q, kv, do, name):
    T, W = q.shape
    M = kv.shape[0]
    hd = XA_HEAD_DIM
    tq = _tile(T, 512, 8)
    scale = hd ** -0.5
    dn_nt = (((1,), (1,)), ((), ()))
    dn_tn = (((0,), (0,)), ((), ()))

    def body(q_ref, k_ref, v_ref, do_ref, dq_ref, dk_ref, dv_ref):
        @pl.when(pl.program_id(1) == 0)
        def _():
            dk_ref[...] = jnp.zeros_like(dk_ref)
            dv_ref[...] = jnp.zeros_like(dv_ref)

        qb, kb, vb = q_ref[...].astype(BF16), k_ref[...].astype(BF16), v_ref[...].astype(BF16)
        p = _xa_probs(qb, kb, scale)
        dob = do_ref[...].astype(BF16)
        dp = lax.dot_general(dob, vb, dn_nt, preferred_element_type=F32)
        ds = (p * (dp - jnp.sum(p * dp, axis=-1, keepdims=True))).astype(BF16)
        dq_ref[...] = jnp.dot(ds, kb, preferred_element_type=F32) * scale
        dk_ref[...] += lax.dot_general(ds, qb, dn_tn, preferred_element_type=F32) * scale
        dv_ref[...] += lax.dot_general(p.astype(BF16), dob, dn_tn, preferred_element_type=F32)

    qsp = BS((tq, hd), lambda h, i: (i, h))
    ksp = BS((M, hd), lambda h, i: (0, h))
    return _pcall(body, name=name, grid=(XA_HEADS, T // tq),
                  in_specs=[qsp, ksp, BS((M, hd), lambda h, i: (0, XA_HEADS + h)), qsp],
                  out_specs=[qsp, ksp, ksp], out_shape=[SDS((T, W), F32), SDS((M, W), F32), SDS((M, W), F32)],
                  compiler_params=_cparams(("parallel", "arbitrary")))(q, kv, kv, do)


@functools.partial(jax.custom_vjp, nondiff_argnums=(2,))
def cross_attention(q, kv, name):
    return _xa_fwd_call(q, kv, name)


def _cross_attention_fwd(q, kv, name):
    return _xa_fwd_call(q, kv, name), (q, kv)


def _cross_attention_bwd(name, res, do):
    q, kv = res
    dq, dk, dv = _xa_bwd_call(q, kv, do, name + "_bwd")
    return dq, jnp.concatenate([dk, dv], axis=1)


cross_attention.defvjp(_cross_attention_fwd, _cross_attention_bwd)


def _gate_up_swiglu_call(hn, w1, name):
    T, K = hn.shape
    F = w1.shape[1] // 2
    tm, tn = _tile(T, 1024), _tile(F, 256)
    nj = F // tn

    def body(a_ref, wg_ref, wu_ref, g_ref, u_ref, act_ref):
        a = a_ref[...]
        g = jnp.dot(a, wg_ref[...], preferred_element_type=F32)
        u = jnp.dot(a, wu_ref[...], preferred_element_type=F32)
        g_ref[...] = g
        u_ref[...] = u
        act_ref[...] = ((g * _sigmoid(g)) * u).astype(BF16)

    tile = BS((tm, tn), lambda i, j: (i, j))
    return _pcall(body, name=name, grid=(T // tm, nj),
                  in_specs=[BS((tm, K), lambda i, j: (i, 0)), BS((K, tn), lambda i, j: (0, j)),
                            BS((K, tn), lambda i, j: (0, j + nj))],
                  out_specs=[tile, tile, tile], out_shape=[SDS((T, F), F32), SDS((T, F), F32), SDS((T, F), BF16)],
                  compiler_params=_cparams(("parallel", "parallel")))(hn, w1, w1)


def _swiglu_bwd_call(g, u, dact, name):
    T, F = g.shape
    tt = _tile(T, 128, 16)

    def body(g_ref, u_ref, d_ref, o_ref):
        g, d = g_ref[...], d_ref[...]
        sg = _sigmoid(g)
        o_ref[:, :F] = (d * u_ref[...] * (sg * (1.0 + g * (1.0 - sg)))).astype(BF16)
        o_ref[:, F:] = (d * (g * sg)).astype(BF16)

    row = BS((tt, F), lambda i: (i, 0))
    return _pcall(body, name=name, grid=(T // tt,), in_specs=[row, row, row],
                  out_specs=BS((tt, 2 * F), lambda i: (i, 0)), out_shape=SDS((T, 2 * F), BF16),
                  compiler_params=_cparams(("parallel",)))(g, u, dact)


@functools.partial(jax.custom_vjp, nondiff_argnums=(6,))
def ffn_block(h, g, w1b, w1c, w2b, w2c, name):
    return _ffn_fwd(h, g, w1b, w1c, w2b, w2c, name)[0]


def _ffn_fwd(h, g, w1b, w1c, w2b, w2c, name):
    hn = _rms_fwd_call(h, g, name + "_norm", BF16)
    gate, up, act = _gate_up_swiglu_call(hn, w1b, name + "_gu")
    out = _mm(act, w2b, add=h, name=name + "_down", tm=1024, tn=1024, tk=1408)
    return out, (h, g, hn, gate, up, act, w1b, w2b)


def _ffn_bwd(name, res, dout):
    h, g, hn, gate, up, act, w1b, w2b = res
    dact = _mm(dout, w2b, tb=True, name=name + "_down_da", tm=1024, tn=1408, tk=1024)
    dw2 = _mm(act, dout, ta=True, name=name + "_down_dw", tm=1408, tn=1024, tk=1024)
    dgu = _swiglu_bwd_call(gate, up, dact, name + "_swiglu_bwd")
    dw1 = _mm(hn, dgu, ta=True, name=name + "_gu_dw", tm=1024, tn=1408, tk=1024)
    dh, dg = _mm_rms_bwd(dgu, w1b, h, g, dout, name + "_gu_da")
    return dh, dg.reshape(g.shape), jnp.zeros_like(w1b), dw1, jnp.zeros_like(w2b), dw2


ffn_block.defvjp(_ffn_fwd, _ffn_bwd)


def _final_call(h, g, target, name):
    T, Dm = h.shape
    tt = _tile(T, 512, 8)

    def body(x_ref, g_ref, t_ref, loss_ref, dx_ref, dg_ref):
        @pl.when(pl.program_id(0) == 0)
        def _():
            loss_ref[...] = jnp.zeros_like(loss_ref)
            dg_ref[...] = jnp.zeros_like(dg_ref)

        xv, gv = x_ref[...], g_ref[...]
        r = lax.rsqrt(jnp.mean(xv * xv, axis=-1, keepdims=True) + NORM_EPS)
        xh = xv * r
        err = xh * gv - t_ref[...]
        loss_ref[...] += 0.5 * jnp.sum(jnp.mean(err * err, axis=-1, keepdims=True), axis=0, keepdims=True)
        dy = err * (1.0 / Dm)
        dyg = dy * gv
        dx_ref[...] = r * (dyg - xh * jnp.mean(dyg * xh, axis=-1, keepdims=True))
        dg_ref[...] += jnp.sum(dy * xh, axis=0, keepdims=True)

    row = BS((tt, Dm), lambda i: (i, 0))
    vec = BS((1, Dm), lambda i: (0, 0))
    return _pcall(body, name=name, grid=(T // tt,), in_specs=[row, vec, row],
                  out_specs=[BS((1, 1), lambda i: (0, 0)), row, vec],
                  out_shape=[SDS((1, 1), F32), SDS((T, Dm), F32), SDS((1, Dm), F32)],
                  compiler_params=_cparams(("arbitrary",)))(h, g.reshape(1, Dm), target)


ADAMW_BLOCK_ELEMS = 64 * 1024


def _adamw_call(parts, w, m, v, name):
    shape = w.shape
    if not isinstance(parts, (list, tuple)):
        parts, shape3 = [parts], (1,) + shape
    else:
        shape3 = shape
    n_lead = shape3[0]
    r, N = shape3[-2], shape3[-1]
    Ld = math.prod(shape3[1:-2])
    w, m, v = (t.reshape(n_lead * Ld, r, N) for t in (w, m, v))
    tr = _tile(r, max(8, ADAMW_BLOCK_ELEMS // N), 8)
    c1 = 1.0 - ADAM_B1 ** ADAM_STEP
    c2 = 1.0 - ADAM_B2 ** ADAM_STEP
    outs = None
    for lead, p in enumerate(parts):
        def body(p_ref, w_ref, m_ref, v_ref, *rest):
            g_ref, d_ref, nm_ref, nv_ref = rest[-4:]
            g = p_ref[0]
            for j in range(1, N_DEV):
                g = g + p_ref[j]
            nm = ADAM_B1 * m_ref[...] + (1.0 - ADAM_B1) * g
            nv = ADAM_B2 * v_ref[...] + (1.0 - ADAM_B2) * (g * g)
            g_ref[...] = g
            nm_ref[...] = nm
            nv_ref[...] = nv
            d_ref[...] = -ADAM_LR * ((nm / c1) / (jnp.sqrt(nv / c2) + ADAM_EPS) + ADAM_WD * w_ref[...])

        base = lead * Ld
        row = BS((1, tr, N), lambda l, i, base=base: (base + l, i, 0))
        prev = [] if outs is None else list(outs)
        outs = _pcall(body, name=f"{name}_{lead}", grid=(Ld, r // tr),
                      in_specs=[BS((N_DEV, 1, tr, N), lambda l, i: (0, l, i, 0)), row, row, row]
                      + [BS(memory_space=pl.ANY)] * len(prev),
                      out_specs=[row] * 4, out_shape=[SDS((n_lead * Ld, r, N), F32)] * 4,
                      input_output_aliases={4 + j: j for j in range(len(prev))},
                      compiler_params=_cparams(("parallel", "parallel")))(p.reshape(N_DEV, Ld, r, N), w, m, v, *prev)
    return [t.reshape(shape) for t in outs]


def _place():
    return lax.axis_index("x"), lax.axis_index("y"), lax.axis_index("c")


def _all_gather(xs, name):
    n = len(xs)
    pairs = [(i, l) for i, x in enumerate(xs) for l in range(x.shape[0])]

    def body(*refs):
        x_refs, o_refs = refs[:n], refs[n:2 * n]
        send_sems, recv_sems, local_sems = refs[2 * n:]
        x_, y_, c_ = _place()
        me, sibling = (x_, y_, c_), (x_, y_, 1 - c_)
        chips = [(1 - x_, y_), (x_, 1 - y_), (1 - x_, 1 - y_)]

        def copy(e, k, block, to, from_input=False):
            i, l = pairs[e]
            px, py, pc = block
            dst = o_refs[i].at[l, 4 * px + 2 * py + pc]
            return pltpu.make_async_remote_copy(
                src_ref=x_refs[i].at[l] if from_input else dst, dst_ref=dst,
                send_sem=send_sems.at[7 * e + k], recv_sem=recv_sems.at[7 * e + k],
                device_id=to, device_id_type=pl.DeviceIdType.MESH)

        every = range(len(pairs))
        mine = [pltpu.make_async_copy(x_refs[i].at[l], o_refs[i].at[l, 4 * x_ + 2 * y_ + c_], local_sems.at[e])
                for e, (i, l) in enumerate(pairs)]
        for cp in mine:
            cp.start()
        first = [copy(e, 0, me, sibling, True) for e in every]
        first += [copy(e, 1 + j, me, (*chip, c_), True) for j, chip in enumerate(chips) for e in every]
        for cp in first:
            cp.start()
        passed = []
        for j, chip in enumerate(chips):
            for e in every:
                copy(e, 1 + j, (*chip, c_), me).wait_recv()
            for e in every:
                cp = copy(e, 4 + j, (*chip, c_), sibling)
                cp.start()
                passed.append(cp)
        for e in every:
            copy(e, 0, sibling, me).wait_recv()
        for j, chip in enumerate(chips):
            for e in every:
                copy(e, 4 + j, (*chip, 1 - c_), me).wait_recv()
        for cp in first + passed:
            cp.wait_send()
        for cp in mine:
            cp.wait()

    any_spec = BS(memory_space=pl.ANY)
    return _pcall(body, name=name, in_specs=[any_spec] * n, out_specs=[any_spec] * n,
                  out_shape=[SDS((x.shape[0], N_DEV) + x.shape[1:], x.dtype) for x in xs],
                  scratch_shapes=[pltpu.SemaphoreType.DMA((7 * len(pairs),)), pltpu.SemaphoreType.DMA((7 * len(pairs),)),
                                  pltpu.SemaphoreType.DMA((len(pairs),))],
                  compiler_params=pltpu.CompilerParams(has_side_effects=True))(*xs)


def _peer_of(k, place):
    x_, y_, c_ = place
    fx, fy, fc = (k >> 2) & 1, (k >> 1) & 1, k & 1
    return fx + x_ - 2 * fx * x_, fy + y_ - 2 * fy * y_, fc + c_ - 2 * fc * c_


def _split_copy(src_ref, land_ref, send_sems, recv_sems, e, k, place, scatter):
    x_, y_, c_ = place
    px, py, pc = _peer_of(k, place)
    return pltpu.make_async_remote_copy(
        src_ref=src_ref.at[4 * px + 2 * py + pc] if scatter else src_ref, dst_ref=land_ref.at[4 * x_ + 2 * y_ + c_],
        send_sem=send_sems.at[7 * e + k - 1], recv_sem=recv_sems.at[7 * e + k - 1],
        device_id=(px, py, pc), device_id_type=pl.DeviceIdType.MESH)


_HBM_SPEC = BS(memory_space=pltpu.HBM)
_SEM_SPEC = BS(memory_space=pltpu.SEMAPHORE)
_EFFECT = pltpu.SideEffectType.DATAFLOW_SIDE_EFFECTING


def _copies_start(srcs, scatter, name, thru=None):
    n = len(srcs)
    lands = [lax.empty(s.shape if scatter else (N_DEV,) + s.shape, s.dtype) for s in srcs]
    passed = srcs + lands + list(thru or ())

    def body(*refs):
        src_refs, land_refs = refs[:n], refs[n:2 * n]
        send_sems, recv_sems = refs[len(passed)], refs[len(passed) + 1]
        token = refs[-1]
        place = _place()
        for e in range(n):
            for k in range(1, N_DEV):
                _split_copy(src_refs[e], land_refs[e], send_sems, recv_sems, e, k, place, scatter).start()
        token[...] = jnp.zeros_like(token)

    hbm = lambda t: pltpu.with_memory_space_constraint(t, pltpu.HBM)
    out = _pcall(body, name=name,
                 out_shape=(pltpu.SemaphoreType.DMA((7 * n,)), pltpu.SemaphoreType.DMA((7 * n,)),
                            *[pltpu.HBM(t.shape, t.dtype) for t in passed], SDS((8, LANES), F32)),
                 in_specs=[_HBM_SPEC] * len(passed),
                 out_specs=(_SEM_SPEC, _SEM_SPEC, *[_HBM_SPEC] * len(passed), BS(memory_space=pltpu.VMEM)),
                 input_output_aliases={i: 2 + i for i in range(len(passed))},
                 compiler_params=pltpu.CompilerParams(has_side_effects=_EFFECT))(*[hbm(t) for t in passed])
    return out[0], out[1], list(out[2:2 + n]), list(out[2 + n:2 + 2 * n]), list(out[2 + 2 * n:-1])


def _copies_wait(started, which, scatter, after, name):
    send_sems, recv_sems, srcs, lands, _ = started
    n = len(which)
    after = list(after) if isinstance(after, (list, tuple)) else [after]

    def body(*refs):
        src_refs, land_refs = refs[:n], refs[n:2 * n]
        send_s, recv_s = refs[2 * n], refs[2 * n + 1]
        place = _place()
        for j, e in enumerate(which):
            for k in range(1, N_DEV):
                cp = _split_copy(src_refs[j], land_refs[j], send_s, recv_s, e, k, place, scatter)
                cp.wait_send()
                cp.wait_recv()

    args = [srcs[e] for e in which] + [lands[e] for e in which]
    out = _pcall(body, name=name, out_shape=tuple(pltpu.HBM(t.shape, t.dtype) for t in args),
                 in_specs=[_HBM_SPEC] * (2 * n) + [_SEM_SPEC, _SEM_SPEC] + [BS(memory_space=pl.ANY)] * len(after),
                 out_specs=tuple([_HBM_SPEC] * (2 * n)), input_output_aliases={i: i for i in range(2 * n)},
                 compiler_params=pltpu.CompilerParams(has_side_effects=_EFFECT))(*args, send_sems, recv_sems, *after)
    return list(out[:n]), list(out[n:])


def _with_own_block(land, own_block):
    me = 4 * lax.axis_index("x") + 2 * lax.axis_index("y") + lax.axis_index("c")
    return lax.dynamic_update_index_in_dim(land, own_block, me, 0)


def _pad_flat(t, quantum=PACK_QUANTUM):
    f = t.reshape(-1)
    pad = (-f.shape[0]) % quantum
    return jnp.pad(f, (0, pad)) if pad else f


def _pack(arrs, dtype):
    return jnp.concatenate([_pad_flat(a.astype(dtype)) for a in arrs]).reshape(-1, LANES)


def _unpack(buf, shapes, lead=()):
    flat = buf.reshape(lead + (-1,))
    out, off = [], 0
    for s in shapes:
        n = math.prod(s)
        out.append(flat[..., off:off + n].reshape(lead + tuple(s)))
        off += n + (-n) % PACK_QUANTUM
    return out


def _full_from_gathered(g, axis):
    t = jnp.moveaxis(g, 0, axis)
    s = t.shape
    return t.reshape(s[:axis] + (s[axis] * s[axis + 1],) + s[axis + 2:])


def _parts_from_full(t, axis):
    s = t.shape
    t = t.reshape(s[:axis] + (N_DEV, s[axis] // N_DEV) + s[axis + 1:])
    return jnp.moveaxis(t, axis, 0)


def _head_rows(t):
    return jnp.repeat(t, HEAD_DIM).reshape(1, -1)


def _dilated_attention(q, k, v, name):
    no_sink = jnp.full((1, q.shape[1]), NEG, F32)
    outs, lses = zip(*[band_attention(q, k, v, no_sink, d, ATT_BLOCK, f"{name}_d{d}") for d in DILATIONS])
    return merge3(tuple(outs), tuple(lses), name + "_merge")


STAGES = (
    ("proj0", ('mix_norm',), (('ab_w_in', 0),)),
    ("mixer0", ('lru_conv_w', 'lru_conv_b', 'lru_ba', 'lru_bx', 'lru_lambda'),
     (('lru_wa', 0), ('lru_wx', 0), ('ab_w_out', 0))),
    ("xa0", ('xa_norm', 'xa_mem_norm'), (('xa_wq', 0), ('xa_wkv', 0), ('xa_wo', 0))),
    ("ffn0", ('ffn_norm',), (('ffn_w_gate_up', 0), ('ffn_w_down', 0))),
    ("mixer1", ('mix_norm', 'c_b_qkv', 'c_sinks', 'c_b_out'), (('c_w_qkv', 0), ('c_w_out', 0))),
    ("xa1", ('xa_norm', 'xa_mem_norm'), (('xa_wq', 1), ('xa_wkv', 1), ('xa_wo', 1))),
    ("ffn1", ('ffn_norm',), (('ffn_w_gate_up', 1), ('ffn_w_down', 1))),
)


def _stage_fn(stage, Wb, tabs, mem):
    layer = int(stage[-1])
    L = f"l{layer}"

    def run(S, Cw, h):
        def lin(a, key, bias, add, name, rows=None):
            wb, wc = Wb[key], Cw[key]
            if rows is not None:
                wb, wc = wb[rows], wc[rows]
            return linear(a, wb, wc, bias, add, name)

        def norm_lin(a, gain, key, bias, name):
            return norm_linear(a, gain, Wb[key], Cw[key], bias, name)

        if stage == "mixer0":
            h, proj = h
            C = S['lru_conv_w'].shape[-1]
            xc = conv4(proj[:, :C], S['lru_conv_w'][0], S['lru_conv_b'][0], L + "_conv")
            ga, gx = lru_gates(xc, Wb['lru_wa', 0], Cw['lru_wa', 0], S['lru_ba'][0],
                               Wb['lru_wx', 0], Cw['lru_wx', 0], S['lru_bx'][0], L + "_gates")
            rec = lru_scan(xc, ga, gx, proj[:, C:2 * C], S['lru_lambda'][0], L + "_scan")
            bw = B_HEADS * HEAD_DIM
            q = rope(proj[:, 2 * C:2 * C + bw], tabs, L + "_rope_q")
            k = rope(proj[:, 2 * C + bw:2 * C + 2 * bw], tabs, L + "_rope_k")
            v = proj[:, 2 * C + 2 * bw:]
            att = _dilated_attention(q, k, v, L + "_att")
            h = lin(att, ('ab_w_out', 0), None, h, L + "_w_out_att", slice(C, None))
            return lin(rec, ('ab_w_out', 0), None, h, L + "_w_out_rec", slice(0, C))
        if stage == "mixer1":
            qw = C_HEADS * HEAD_DIM
            kw = C_KV_HEADS * HEAD_DIM
            qkv, h = norm_lin(h, S['mix_norm'][1], ('c_w_qkv', 0), S['c_b_qkv'][0], L + "_w_qkv")
            q = rope(qkv[:, :qw], tabs, L + "_rope_q")
            k = rope(qkv[:, qw:qw + kw], tabs, L + "_rope_k")
            v = qkv[:, qw + kw:]
            o, _ = band_attention(q, k, v, _head_rows(S['c_sinks'][0]), 1, ATT_BLOCK - 1, L + "_att")
            return lin(o, ('c_w_out', 0), S['c_b_out'][0], h, L + "_w_out")
        if stage.startswith("xa"):
            xq, h = norm_lin(h, S['xa_norm'][layer], ('xa_wq', layer), None, L + "_xa_wq")
            xkv, _ = norm_lin(mem, S['xa_mem_norm'][layer], ('xa_wkv', layer), None, L + "_xa_wkv")
            return lin(cross_attention(xq, xkv, L + "_xa"), ('xa_wo', layer), None, h, L + "_xa_wo")
        gu, down = ('ffn_w_gate_up', layer), ('ffn_w_down', layer)
        return ffn_block(h, S['ffn_norm'][layer], Wb[gu], Cw[gu], Wb[down], Cw[down], L + "_ffn")

    return run


def kernel(x, mem, mix_norm, ab_w_in, lru_conv_w, lru_conv_b, lru_wa, lru_ba, lru_wx, lru_bx, lru_lambda, ab_w_out, c_w_qkv, c_b_qkv, c_sinks, c_w_out, c_b_out, xa_norm, xa_mem_norm, xa_wq, xa_wkv, xa_wo, ffn_norm, ffn_w_gate_up, ffn_w_down, final_norm, loss_target, m_mix_norm, m_ab_w_in, m_lru_conv_w, m_lru_conv_b, m_lru_wa, m_lru_ba, m_lru_wx, m_lru_bx, m_lru_lambda, m_ab_w_out, m_c_w_qkv, m_c_b_qkv, m_c_sinks, m_c_w_out, m_c_b_out, m_xa_norm, m_xa_mem_norm, m_xa_wq, m_xa_wkv, m_xa_wo, m_ffn_norm, m_ffn_w_gate_up, m_ffn_w_down, m_final_norm, v_mix_norm, v_ab_w_in, v_lru_conv_w, v_lru_conv_b, v_lru_wa, v_lru_ba, v_lru_wx, v_lru_bx, v_lru_lambda, v_ab_w_out, v_c_w_qkv, v_c_b_qkv, v_c_sinks, v_c_w_out, v_c_b_out, v_xa_norm, v_xa_mem_norm, v_xa_wq, v_xa_wkv, v_xa_wo, v_ffn_norm, v_ffn_w_gate_up, v_ffn_w_down, v_final_norm):
    w_loc = dict(zip(WEIGHT_NAMES, (mix_norm, ab_w_in, lru_conv_w, lru_conv_b, lru_wa, lru_ba, lru_wx, lru_bx, lru_lambda, ab_w_out, c_w_qkv, c_b_qkv, c_sinks, c_w_out, c_b_out, xa_norm, xa_mem_norm, xa_wq, xa_wkv, xa_wo, ffn_norm, ffn_w_gate_up, ffn_w_down, final_norm)))
    m_loc = dict(zip(WEIGHT_NAMES, (m_mix_norm, m_ab_w_in, m_lru_conv_w, m_lru_conv_b, m_lru_wa, m_lru_ba, m_lru_wx, m_lru_bx, m_lru_lambda, m_ab_w_out, m_c_w_qkv, m_c_b_qkv, m_c_sinks, m_c_w_out, m_c_b_out, m_xa_norm, m_xa_mem_norm, m_xa_wq, m_xa_wkv, m_xa_wo, m_ffn_norm, m_ffn_w_gate_up, m_ffn_w_down, m_final_norm)))
    v_loc = dict(zip(WEIGHT_NAMES, (v_mix_norm, v_ab_w_in, v_lru_conv_w, v_lru_conv_b, v_lru_wa, v_lru_ba, v_lru_wx, v_lru_bx, v_lru_lambda, v_ab_w_out, v_c_w_qkv, v_c_b_qkv, v_c_sinks, v_c_w_out, v_c_b_out, v_xa_norm, v_xa_mem_norm, v_xa_wq, v_xa_wkv, v_xa_wo, v_ffn_norm, v_ffn_w_gate_up, v_ffn_w_down, v_final_norm)))

    me = 4 * lax.axis_index("x") + 2 * lax.axis_index("y") + lax.axis_index("c")

    first_keys = list(STAGES[0][2])
    keys = [key for _, _, stage_keys in STAGES[1:] for key in stage_keys]
    shards = [w_loc[n][l].astype(BF16) for n, l in keys]
    first_g = _all_gather([_pack([w_loc[n] for n in SMALL], F32)[None]] + [w_loc[n][l].astype(BF16)[None] for n, l in first_keys],
                          "gather_first")
    gather = _copies_start(shards, False, "gather_start", thru=[first_g[0]])
    small_g = gather[4][0][0]
    Wb = {key: _full_from_gathered(g[0], SHARD_AXIS[key[0]] - 1) for key, g in zip(first_keys, first_g[1:])}
    S = {n: w_loc[n] for n in REPLICATED}
    for n, t in zip(SMALL, _unpack(small_g, [w_loc[n].shape for n in SMALL], lead=(N_DEV,))):
        S[n] = _full_from_gathered(t, SHARD_AXIS[n])

    tabs = _rope_tables(x.shape[1])
    w_in = Wb[first_keys[0]]
    hn0 = _rms_fwd_call(x[0], S['mix_norm'][0], "l0_w_in_norm", BF16)
    h = (x[0], _mm(hn0, w_in, name="l0_w_in", tm=1024, tn=1408, tk=1024))
    vjps = []
    for stage, small_names, stage_keys in STAGES[1:]:
        which = [keys.index(key) for key in stage_keys]
        _, lands = _copies_wait(gather, which, False, jax.tree.leaves(h)[-1], "gather_wait_" + stage)
        for e, land in zip(which, lands):
            Wb[keys[e]] = _full_from_gathered(_with_own_block(land, shards[e]), SHARD_AXIS[keys[e][0]] - 1)
        carriers = {key: jnp.zeros(Wb[key].shape, F32) for key in stage_keys}
        h, vjp_fn = jax.vjp(_stage_fn(stage, Wb, tabs, mem[0]), {n: S[n] for n in small_names}, carriers, h)
        vjps.append(vjp_fn)
    loss_part, dh, dg_final = _final_call(h, S['final_norm'], loss_target[0], "final_loss")

    grads = {'final_norm': dg_final.reshape(final_norm.shape)}
    exchanges, send_keys, send_parts = [], [], []

    def start_exchange(stage, dh):
        leaves, tree = jax.tree.flatten(dh)
        started = _copies_start(list(send_parts), True, "grad_start_" + stage, thru=leaves)
        exchanges.append((stage, started, list(send_keys)))
        send_keys.clear()
        send_parts.clear()
        return jax.tree.unflatten(tree, started[4])

    for (stage, small_names, stage_keys), vjp_fn in zip(reversed(STAGES[1:]), reversed(vjps)):
        g_small, g_big, dh = vjp_fn(dh)
        for n in small_names:
            grads[n] = grads[n] + g_small[n] if n in grads else g_small[n]
        send_keys += list(stage_keys)
        send_parts += [_parts_from_full(g_big[key], SHARD_AXIS[key[0]] - 1) for key in stage_keys]
        if stage == "xa1":
            continue
        if stage == "mixer0":
            small_parts = [_parts_from_full(grads[n], SHARD_AXIS[n]) for n in SMALL]
            send_keys.append("small")
            send_parts.append(jnp.stack([_pack([p[j] for p in small_parts], F32) for j in range(N_DEV)]))
        dh = start_exchange(stage, dh)
    d_res, d_proj = dh
    send_keys.append(first_keys[0])
    send_parts.append(_parts_from_full(_mm(hn0, d_proj, ta=True, name="l0_w_in_dw", tm=1024, tn=1408, tk=1024),
                                       SHARD_AXIS[first_keys[0][0]] - 1))
    d_res, d_proj = start_exchange("proj0", (d_res, d_proj))
    dx, dg0 = _mm_rms_bwd(d_proj, w_in, x[0], S['mix_norm'][0], d_res, "l0_w_in_da")
    grads['mix_norm'] = grads['mix_norm'] + jnp.concatenate([dg0, jnp.zeros_like(dg0)], axis=0)
    rep_names = REPLICATED + ["loss"]
    grads["loss"] = loss_part
    zero = jnp.zeros((1, 1), F32)
    for d in (w_loc, m_loc, v_loc):
        d["loss"] = zero
    rep_started = _copies_start([_pack([grads[n] for n in rep_names], F32)], False, "rep_grads_start", thru=[dx])
    dx = rep_started[4][0]

    parts, out = {}, {}

    def end_exchange(stage, started, ex_keys, after):
        srcs, lands = _copies_wait(started, list(range(len(ex_keys))), True, after, "grad_wait_" + stage)
        for key, src, land in zip(ex_keys, srcs, lands):
            parts[key] = _with_own_block(land, lax.dynamic_index_in_dim(src, me, 0, keepdims=False))

    def adamw(p, names, call_name):
        if len(names) == 1:
            res = _adamw_call(p, w_loc[names[0]], m_loc[names[0]], v_loc[names[0]], call_name)
            for kind, t in zip(("grad", "delta", "new_m", "new_v"), res):
                out[kind, names[0]] = t
        else:
            res = _adamw_call(p, *[_pack([d[n] for n in names], F32) for d in (w_loc, m_loc, v_loc)], call_name)
            for kind, buf in zip(("grad", "delta", "new_m", "new_v"), res):
                for n, t in zip(names, _unpack(buf, [w_loc[n].shape for n in names])):
                    out[kind, n] = t

    for ex in exchanges[:-1]:
        end_exchange(*ex, dx)
    last_names = {key[0] for key in exchanges[-1][2]}
    for n in BIG:
        if n not in last_names:
            adamw([parts[n, l] for l in range(w_loc[n].shape[0])], [n], "adamw_" + n)
    adamw(parts["small"], SMALL, "adamw_small")
    end_exchange(*exchanges[-1], [out["new_v", n] for n in BIG if n not in last_names])
    for n in BIG:
        if n in last_names:
            adamw([parts[n, l] for l in range(w_loc[n].shape[0])], [n], "adamw_" + n)
    rep_src, rep_land = _copies_wait(rep_started, [0], False, out["new_v", "ab_w_in"], "rep_grads_wait")
    adamw(_with_own_block(rep_land[0], rep_src[0]), rep_names, "adamw_replicated")
    loss = out["grad", "loss"][0, 0]

    return (loss, dx[None], *[out[kind, n] for kind in ("grad", "delta", "new_m", "new_v") for n in WEIGHT_NAMES])
```

```python
import functools
import math

import jax
import jax.numpy as jnp
from jax import lax
from jax.experimental import pallas as pl
from jax.experimental.pallas import tpu as pltpu

F32 = jnp.float32
BF16 = jnp.bfloat16
SDS = jax.ShapeDtypeStruct
BS = pl.BlockSpec

N_DEV = 8
NORM_EPS = 1e-6
ROPE_THETA = 500000.0
HEAD_DIM = 64
ROT_DIM = 16
ATT_BLOCK = 128
LRU_C = 8.0
LRU_HEADS = 4
DILATIONS = (1, 4, 16)
B_HEADS = 8
C_HEADS = 16
C_KV_HEADS = 2
XA_HEADS = 4
XA_HEAD_DIM = 128
NEG = -1e30
ADAM_LR, ADAM_B1, ADAM_B2, ADAM_EPS, ADAM_WD, ADAM_STEP = 0.001, 0.9, 0.999, 1e-08, 0.01, 10
LANES = 128
VMEM_LIMIT = 48 * 1024 * 1024

WEIGHT_NAMES = ['mix_norm', 'ab_w_in', 'lru_conv_w', 'lru_conv_b', 'lru_wa', 'lru_ba', 'lru_wx', 'lru_bx',
                'lru_lambda', 'ab_w_out', 'c_w_qkv', 'c_b_qkv', 'c_sinks', 'c_w_out', 'c_b_out', 'xa_norm',
                'xa_mem_norm', 'xa_wq', 'xa_wkv', 'xa_wo', 'ffn_norm', 'ffn_w_gate_up', 'ffn_w_down', 'final_norm']
SHARD_AXIS = {'ab_w_in': 2, 'lru_conv_w': 2, 'lru_wa': 2, 'lru_ba': 2, 'lru_wx': 2, 'lru_bx': 2, 'ab_w_out': 1,
              'c_w_qkv': 2, 'c_b_qkv': 1, 'c_w_out': 1, 'c_b_out': 1, 'xa_wq': 1, 'xa_wkv': 1, 'xa_wo': 2,
              'ffn_w_gate_up': 2, 'ffn_w_down': 1}
BIG = ['ab_w_in', 'lru_wa', 'lru_wx', 'ab_w_out', 'c_w_qkv', 'c_w_out', 'xa_wq', 'xa_wkv', 'xa_wo',
       'ffn_w_gate_up', 'ffn_w_down']
SMALL = ['lru_conv_w', 'lru_ba', 'lru_bx', 'c_b_qkv', 'c_b_out']
SHARDED = BIG + SMALL
REPLICATED = [n for n in WEIGHT_NAMES if n not in SHARD_AXIS]
COLUMN_CUT = ('ab_w_in', 'c_w_qkv', 'ffn_w_gate_up')
PACK_QUANTUM = 2048


def _shard_view(name, t):
    return jnp.swapaxes(t, -1, -2) if name in COLUMN_CUT else t


def _layer_shard_axis(name):
    return 0 if name in COLUMN_CUT else SHARD_AXIS[name] - 1


def _pcall(body, **kw):
    return pl.pallas_call(body, **kw)


def _cparams(sem=None):
    return pltpu.CompilerParams(dimension_semantics=sem, vmem_limit_bytes=VMEM_LIMIT)


def _tile(n, target, mult=LANES):
    if n <= target:
        return n
    t = (target // mult) * mult
    while t >= mult:
        if n % t == 0:
            return t
        t -= mult
    return n


def _sigmoid(x):
    return 1.0 / (1.0 + jnp.exp(-x))


def _expm1(x):
    small = x * (1.0 + x * (0.5 + x * (1.0 / 6.0 + x * (1.0 / 24.0))))
    return jnp.where(jnp.abs(x) < 0.03, small, jnp.exp(x) - 1.0)


_GELU_C = math.sqrt(2.0 / math.pi)


def _gelu_parts(y):
    y2 = y * y
    th = jnp.tanh(_GELU_C * (y + 0.044715 * y * y2))
    g = 0.5 * y * (1.0 + th)
    dg = 0.5 * (1.0 + th) + 0.5 * y * (1.0 - th * th) * _GELU_C * (1.0 + 3.0 * 0.044715 * y2)
    return g, dg


def _mm(a, b, *, ta=False, tb=False, bias=None, add=None, name, tm=512, tn=512, tk=2048):
    M, K = (a.shape[1], a.shape[0]) if ta else a.shape
    N = b.shape[0] if tb else b.shape[1]
    tm, tn, tk = _tile(M, tm), _tile(N, tn), _tile(K, tk)
    nk = K // tk
    dn = (((0 if ta else 1,), (1 if tb else 0,)), ((), ()))

    def body(*refs):
        a_ref, b_ref = refs[0], refs[1]
        pos = 2
        bias_ref = add_ref = None
        if bias is not None:
            bias_ref = refs[pos]
            pos += 1
        if add is not None:
            add_ref = refs[pos]
            pos += 1
        o_ref = refs[pos]
        part = lax.dot_general(a_ref[...].astype(BF16), b_ref[...].astype(BF16), dn, preferred_element_type=F32)

        def finish(r):
            if bias_ref is not None:
                r = r + bias_ref[...]
            if add_ref is not None:
                r = r + add_ref[...]
            o_ref[...] = r

        if nk == 1:
            finish(part)
            return
        acc_ref = refs[pos + 1]
        k = pl.program_id(2)

        @pl.when(k == 0)
        def _():
            acc_ref[...] = part

        @pl.when((k > 0) & (k < nk - 1))
        def _():
            acc_ref[...] += part

        @pl.when(k == nk - 1)
        def _():
            finish(acc_ref[...] + part)

    in_specs = [BS((tk, tm), lambda i, j, k: (k, i)) if ta else BS((tm, tk), lambda i, j, k: (i, k)),
                BS((tn, tk), lambda i, j, k: (j, k)) if tb else BS((tk, tn), lambda i, j, k: (k, j))]
    args = [a, b]
    if bias is not None:
        in_specs.append(BS((1, tn), lambda i, j, k: (0, j)))
        args.append(bias.reshape(1, N))
    if add is not None:
        in_specs.append(BS((tm, tn), lambda i, j, k: (i, j)))
        args.append(add)
    return _pcall(body, name=name, grid=(M // tm, N // tn, nk), in_specs=in_specs,
                  out_specs=BS((tm, tn), lambda i, j, k: (i, j)), out_shape=SDS((M, N), F32),
                  scratch_shapes=[pltpu.VMEM((tm, tn), F32)] if nk > 1 else [],
                  compiler_params=_cparams(("parallel", "parallel", "arbitrary")))(*args)


def _mm_rms_bwd(dy, wb, x, g, add, name, wt=False):
    T, Kc = dy.shape
    Dm = wb.shape[1] if wt else wb.shape[0]
    tm, tk = _tile(T, 1024), _tile(Kc, 1408)
    nk = Kc // tk
    dn = (((1,), (0 if wt else 1,)), ((), ()))

    def body(*refs):
        dy_ref, w_ref, x_ref, g_ref = refs[:4]
        dx_ref, dg_ref, acc_ref = refs[-3:]
        i, k = pl.program_id(0), pl.program_id(1)
        part = lax.dot_general(dy_ref[...].astype(BF16), w_ref[...], dn, preferred_element_type=F32)

        @pl.when(k == 0)
        def _():
            acc_ref[...] = part

        @pl.when((k > 0) & (k < nk - 1))
        def _():
            acc_ref[...] += part

        @pl.when((i == 0) & (k == 0))
        def _():
            dg_ref[...] = jnp.zeros_like(dg_ref)

        @pl.when(k == nk - 1)
        def _():
            d = part if nk == 1 else acc_ref[...] + part
            xv = x_ref[...]
            r = lax.rsqrt(jnp.mean(xv * xv, axis=-1, keepdims=True) + NORM_EPS)
            xh = xv * r
            dyg = d * g_ref[...]
            dx = r * (dyg - xh * jnp.mean(dyg * xh, axis=-1, keepdims=True))
            dx_ref[...] = dx if add is None else dx + refs[4][...]
            dg_ref[...] += jnp.sum(d * xh, axis=0, keepdims=True)

    row = BS((tm, Dm), lambda i, k: (i, 0))
    vec = BS((1, Dm), lambda i, k: (0, 0))
    extra = [] if add is None else [add]
    return _pcall(body, name=name, grid=(T // tm, nk),
                  in_specs=[BS((tm, tk), lambda i, k: (i, k)),
                            BS((tk, Dm), lambda i, k: (k, 0)) if wt else BS((Dm, tk), lambda i, k: (0, k)), row, vec]
                  + [row] * len(extra),
                  out_specs=[row, vec], out_shape=[SDS((T, Dm), F32), SDS((1, Dm), F32)],
                  scratch_shapes=[pltpu.VMEM((tm, Dm), F32)],
                  compiler_params=_cparams(("arbitrary", "arbitrary")))(dy, wb, x, g.reshape(1, Dm), *extra)


def _colsum(x, name):
    T, N = x.shape
    tt = _tile(T, 512, 8)

    def body(x_ref, o_ref):
        @pl.when(pl.program_id(0) == 0)
        def _():
            o_ref[...] = jnp.zeros_like(o_ref)

        o_ref[...] += jnp.sum(x_ref[...], axis=0, keepdims=True)

    return _pcall(body, name=name, grid=(T // tt,), in_specs=[BS((tt, N), lambda i: (i, 0))],
                  out_specs=BS((1, N), lambda i: (0, 0)), out_shape=SDS((1, N), F32),
                  compiler_params=_cparams(("arbitrary",)))(x)


@functools.partial(jax.custom_vjp, nondiff_argnums=(5,))
def linear(a, wb, wc, bias, add, name):
    return _mm(a, wb, bias=bias, add=add, name=name)


def _linear_fwd(a, wb, wc, bias, add, name):
    return _mm(a, wb, bias=bias, add=add, name=name), (a, wb, bias is not None, add is not None)


def _linear_bwd(name, res, g):
    a, wb, has_bias, has_add = res
    da = _mm(g, wb, tb=True, name=name + "_da")
    dw = _mm(a, g, ta=True, name=name + "_dw")
    dbias = _colsum(g, name + "_db").reshape(-1) if has_bias else None
    return da, jnp.zeros_like(wb), dw, dbias, (g if has_add else None)


linear.defvjp(_linear_fwd, _linear_bwd)


def _rms_fwd_call(x, g, name, out_dtype=F32):
    T, Dm = x.shape
    tt = _tile(T, 512, 16)

    def body(x_ref, g_ref, o_ref):
        xv = x_ref[...]
        r = lax.rsqrt(jnp.mean(xv * xv, axis=-1, keepdims=True) + NORM_EPS)
        o_ref[...] = ((xv * r) * g_ref[...]).astype(out_dtype)

    return _pcall(body, name=name, grid=(T // tt,),
                  in_specs=[BS((tt, Dm), lambda i: (i, 0)), BS((1, Dm), lambda i: (0, 0))],
                  out_specs=BS((tt, Dm), lambda i: (i, 0)), out_shape=SDS((T, Dm), out_dtype),
                  compiler_params=_cparams(("parallel",)))(x, g.reshape(1, Dm))


@functools.partial(jax.custom_vjp, nondiff_argnums=(5, 6))
def norm_linear(x, g, wb, wc, bias, wt, name):
    return _norm_linear_fwd(x, g, wb, wc, bias, wt, name)[0]


def _norm_linear_fwd(x, g, wb, wc, bias, wt, name):
    hn = _rms_fwd_call(x, g, name + "_norm", BF16)
    return (_mm(hn, wb, tb=wt, bias=bias, name=name, tm=1024, tn=1408, tk=1024), x), (x, g, hn, wb, bias is not None)


def _norm_linear_bwd(wt, name, res, cts):
    x, g, hn, wb, has_bias = res
    dy, dres = cts
    if wt:
        dw = _mm(dy, hn, ta=True, name=name + "_dw", tm=1408, tn=1024, tk=1024)
    else:
        dw = _mm(hn, dy, ta=True, name=name + "_dw", tm=1024, tn=1408, tk=1024)
    dx, dg = _mm_rms_bwd(dy, wb, x, g, dres, name + "_da", wt)
    dbias = _colsum(dy, name + "_db").reshape(-1) if has_bias else None
    return dx, dg.reshape(g.shape), jnp.zeros_like(wb), dw, dbias


norm_linear.defvjp(_norm_linear_fwd, _norm_linear_bwd)


def _rope_tables(T):
    half = ROT_DIM // 2
    inv = ROPE_THETA ** (-jnp.arange(0, ROT_DIM, 2, dtype=F32) / ROT_DIM)
    lane = jnp.arange(LANES) % HEAD_DIM
    freq = jnp.where(lane < ROT_DIM, inv[lane % half], 0.0)
    ang = jnp.arange(T, dtype=F32)[:, None] * freq[None, :]
    c, s = jnp.cos(ang), jnp.sin(ang)
    sa = jnp.where((lane >= half) & (lane < ROT_DIM), s, 0.0)
    sb = jnp.where(lane < half, -s, 0.0)
    return c, sa, sb


def _rope_call(x, tabs, inverse, name):
    T, W = x.shape
    tt = _tile(T, 512, 8)
    reps = W // LANES
    half = ROT_DIM // 2

    def body(x_ref, c_ref, sa_ref, sb_ref, o_ref):
        xv = x_ref[...]
        c = jnp.tile(c_ref[...], (1, reps))
        sa = jnp.tile(sa_ref[...], (1, reps))
        sb = jnp.tile(sb_ref[...], (1, reps))
        if not inverse:
            o_ref[...] = xv * c + pltpu.roll(xv, half, axis=1) * sa + pltpu.roll(xv, W - half, axis=1) * sb
        else:
            o_ref[...] = xv * c + pltpu.roll(xv * sa, W - half, axis=1) + pltpu.roll(xv * sb, half, axis=1)

    row = BS((tt, W), lambda i: (i, 0))
    tab = BS((tt, LANES), lambda i: (i, 0))
    return _pcall(body, name=name, grid=(T // tt,), in_specs=[row, tab, tab, tab], out_specs=row,
                  out_shape=SDS((T, W), F32), compiler_params=_cparams(("parallel",)))(x, *tabs)


@functools.partial(jax.custom_vjp, nondiff_argnums=(2,))
def rope(x, tabs, name):
    return _rope_call(x, tabs, False, name)


def _rope_fwd(x, tabs, name):
    return _rope_call(x, tabs, False, name), tabs


def _rope_bwd(name, tabs, dy):
    return _rope_call(dy, tabs, True, name + "_bwd"), jax.tree.map(jnp.zeros_like, tabs)


rope.defvjp(_rope_fwd, _rope_bwd)


def _conv_fwd_call(x, w, b, name):
    T, C = x.shape
    tt = _tile(T, 512, 8)
    per = tt // 8

    def body(x_ref, halo_ref, w_ref, b_ref, o_ref):
        i = pl.program_id(0)
        halo = jnp.where(i > 0, halo_ref[...], 0.0)
        e = jnp.concatenate([halo, x_ref[...]], axis=0)
        acc = b_ref[...]
        for k in (3, 2, 1):
            acc = acc + pltpu.roll(e, k, axis=0)[8:, :] * w_ref[3 - k:4 - k, :]
        o_ref[...] = acc + x_ref[...] * w_ref[3:4, :]

    return _pcall(body, name=name, grid=(T // tt,),
                  in_specs=[BS((tt, C), lambda i: (i, 0)), BS((8, C), lambda i: (jnp.maximum(i * per - 1, 0), 0)),
                            BS((4, C), lambda i: (0, 0)), BS((1, C), lambda i: (0, 0))],
                  out_specs=BS((tt, C), lambda i: (i, 0)), out_shape=SDS((T, C), F32),
                  compiler_params=_cparams(("parallel",)))(x, x, w, b.reshape(1, C))


def _conv_bwd_call(x, w, dy, name):
    T, C = x.shape
    tt = _tile(T, 512, 8)
    per = tt // 8
    nt = T // tt

    def body(x_ref, halo_ref, w_ref, dy_ref, nxt_ref, dx_ref, dwb_ref):
        i = pl.program_id(0)
        halo = jnp.where(i > 0, halo_ref[...], 0.0)
        e = jnp.concatenate([halo, x_ref[...]], axis=0)
        dy = dy_ref[...]
        nxt = jnp.where(i < nt - 1, nxt_ref[...], 0.0)
        f = jnp.concatenate([dy, nxt], axis=0)
        dx = dy * w_ref[3:4, :]
        rows = [None] * 4
        rows[3] = jnp.sum(dy * x_ref[...], axis=0, keepdims=True)
        for k in (1, 2, 3):
            dx = dx + pltpu.roll(f, tt + 8 - k, axis=0)[:tt, :] * w_ref[3 - k:4 - k, :]
            rows[3 - k] = jnp.sum(dy * pltpu.roll(e, k, axis=0)[8:, :], axis=0, keepdims=True)
        dx_ref[...] = dx
        upd = jnp.concatenate(rows + [jnp.sum(dy, axis=0, keepdims=True), jnp.zeros((3, C), F32)], axis=0)

        @pl.when(i == 0)
        def _():
            dwb_ref[...] = jnp.zeros_like(dwb_ref)

        dwb_ref[...] += upd

    row = BS((tt, C), lambda i: (i, 0))
    return _pcall(body, name=name, grid=(nt,),
                  in_specs=[row, BS((8, C), lambda i: (jnp.maximum(i * per - 1, 0), 0)), BS((4, C), lambda i: (0, 0)),
                            row, BS((8, C), lambda i: (jnp.minimum((i + 1) * per, T // 8 - 1), 0))],
                  out_specs=[row, BS((8, C), lambda i: (0, 0))],
                  out_shape=[SDS((T, C), F32), SDS((8, C), F32)],
                  compiler_params=_cparams(("arbitrary",)))(x, x, w, dy, dy)


@functools.partial(jax.custom_vjp, nondiff_argnums=(3,))
def conv4(x, w, b, name):
    return _conv_fwd_call(x, w, b, name)


def _conv4_fwd(x, w, b, name):
    return _conv_fwd_call(x, w, b, name), (x, w)


def _conv4_bwd(name, res, dy):
    x, w = res
    dx, dwb = _conv_bwd_call(x, w, dy, name + "_bwd")
    return dx, dwb[0:4], dwb[4]


conv4.defvjp(_conv4_fwd, _conv4_bwd)


def _gates_fwd_call(xc, wa, ba, wx, bx, name):
    T, C = xc.shape
    hd = C // LRU_HEADS
    tt = _tile(T, 512, 8)

    def body(x_ref, wa_ref, ba_ref, wx_ref, bx_ref, ga_ref, gx_ref):
        xb = x_ref[...].astype(BF16)
        ga_ref[...] = jnp.dot(xb, wa_ref[0].astype(BF16), preferred_element_type=F32) + ba_ref[...]
        gx_ref[...] = jnp.dot(xb, wx_ref[0].astype(BF16), preferred_element_type=F32) + bx_ref[...]

    blk = BS((tt, hd), lambda i, h: (i, h))
    wsp = BS((1, hd, hd), lambda i, h: (h, 0, 0))
    bsp = BS((1, hd), lambda i, h: (0, h))
    return _pcall(body, name=name, grid=(T // tt, LRU_HEADS), in_specs=[blk, wsp, bsp, wsp, bsp],
                  out_specs=[blk, blk], out_shape=[SDS((T, C), F32)] * 2,
                  compiler_params=_cparams(("parallel", "parallel")))(xc, wa, ba.reshape(1, C), wx, bx.reshape(1, C))


def _gates_bwd_x_call(dga, dgx, wa, wx, name):
    T, C = dga.shape
    hd = C // LRU_HEADS
    tt = _tile(T, 512, 8)
    dn = (((1,), (1,)), ((), ()))

    def body(da_ref, dx_ref, wa_ref, wx_ref, o_ref):
        o_ref[...] = (lax.dot_general(da_ref[...].astype(BF16), wa_ref[0].astype(BF16), dn, preferred_element_type=F32)
                      + lax.dot_general(dx_ref[...].astype(BF16), wx_ref[0].astype(BF16), dn, preferred_element_type=F32))

    blk = BS((tt, hd), lambda i, h: (i, h))
    wsp = BS((1, hd, hd), lambda i, h: (h, 0, 0))
    return _pcall(body, name=name, grid=(T // tt, LRU_HEADS), in_specs=[blk, blk, wsp, wsp], out_specs=blk,
                  out_shape=SDS((T, C), F32), compiler_params=_cparams(("parallel", "parallel")))(dga, dgx, wa, wx)


def _gates_bwd_w_call(xc, dga, dgx, name):
    T, C = xc.shape
    hd = C // LRU_HEADS
    tt = _tile(T, 512, 8)
    dn = (((0,), (0,)), ((), ()))

    def body(x_ref, da_ref, dx_ref, dwa_ref, dwx_ref, dba_ref, dbx_ref):
        @pl.when(pl.program_id(1) == 0)
        def _():
            dwa_ref[...] = jnp.zeros_like(dwa_ref)
            dwx_ref[...] = jnp.zeros_like(dwx_ref)
            dba_ref[...] = jnp.zeros_like(dba_ref)
            dbx_ref[...] = jnp.zeros_like(dbx_ref)

        xb = x_ref[...].astype(BF16)
        da, dx = da_ref[...], dx_ref[...]
        dwa_ref[0] += lax.dot_general(xb, da.astype(BF16), dn, preferred_element_type=F32)
        dwx_ref[0] += lax.dot_general(xb, dx.astype(BF16), dn, preferred_element_type=F32)
        dba_ref[...] += jnp.sum(da, axis=0, keepdims=True)
        dbx_ref[...] += jnp.sum(dx, axis=0, keepdims=True)

    blk = BS((tt, hd), lambda h, i: (i, h))
    wsp = BS((1, hd, hd), lambda h, i: (h, 0, 0))
    bsp = BS((1, hd), lambda h, i: (0, h))
    return _pcall(body, name=name, grid=(LRU_HEADS, T // tt), in_specs=[blk, blk, blk],
                  out_specs=[wsp, wsp, bsp, bsp],
                  out_shape=[SDS((LRU_HEADS, hd, hd), F32)] * 2 + [SDS((1, C), F32)] * 2,
                  compiler_params=_cparams(("parallel", "arbitrary")))(xc, dga, dgx)


@functools.partial(jax.custom_vjp, nondiff_argnums=(7,))
def lru_gates(xc, wa, wa_c, ba, wx, wx_c, bx, name):
    return tuple(_gates_fwd_call(xc, wa, ba, wx, bx, name))


def _lru_gates_fwd(xc, wa, wa_c, ba, wx, wx_c, bx, name):
    return tuple(_gates_fwd_call(xc, wa, ba, wx, bx, name)), (xc, wa, wx, ba.shape)


def _lru_gates_bwd(name, res, g):
    xc, wa, wx, bshape = res
    dga, dgx = g
    dxc = _gates_bwd_x_call(dga, dgx, wa, wx, name + "_dx")
    dwa, dwx, dba, dbx = _gates_bwd_w_call(xc, dga, dgx, name + "_dw")
    return dxc, jnp.zeros_like(wa), dwa, dba.reshape(bshape), jnp.zeros_like(wx), dwx, dbx.reshape(bshape)


lru_gates.defvjp(_lru_gates_fwd, _lru_gates_bwd)


def _lru_coeffs(xc, ga, gx, lam):
    r = _sigmoid(ga)
    ig = _sigmoid(gx)
    z = -lam
    sp = jnp.maximum(z, 0.0) + jnp.log(1.0 + jnp.exp(-jnp.abs(z)))
    la = -LRU_C * r * sp
    a = jnp.exp(la)
    s = jnp.sqrt(-_expm1(2.0 * la))
    return r, ig, sp, a, s


LRU_TT = 256


def _scan_fwd_call(xc, ga, gx, y, lam, name):
    T, C = xc.shape
    tt = _tile(T, LRU_TT, 8)

    def body(xc_ref, ga_ref, gx_ref, y_ref, lam_ref, h_ref, rec_ref, a_buf, carry):
        @pl.when(pl.program_id(0) == 0)
        def _():
            carry[...] = jnp.zeros_like(carry)

        xcv = xc_ref[...]
        _, ig, _, a, s = _lru_coeffs(xcv, ga_ref[...], gx_ref[...], lam_ref[...])
        a_buf[...] = a
        h_ref[...] = s * (ig * xcv)

        def step(t, h):
            hn = a_buf[pl.ds(t, 1), :] * h + h_ref[pl.ds(t, 1), :]
            h_ref[pl.ds(t, 1), :] = hn
            return hn

        carry[0:1, :] = lax.fori_loop(0, tt, step, carry[0:1, :], unroll=8)
        g, _ = _gelu_parts(y_ref[...])
        rec_ref[...] = h_ref[...] * g

    row = BS((tt, C), lambda i: (i, 0))
    vec = BS((1, C), lambda i: (0, 0))
    return _pcall(body, name=name, grid=(T // tt,), in_specs=[row, row, row, row, vec], out_specs=[row, row],
                  out_shape=[SDS((T, C), F32)] * 2,
                  scratch_shapes=[pltpu.VMEM((tt, C), F32), pltpu.VMEM((8, C), F32)],
                  compiler_params=_cparams(("arbitrary",)))(xc, ga, gx, y, lam.reshape(1, C))


def _scan_bwd_call(xc, ga, gx, y, lam, h, drec, name):
    T, C = xc.shape
    tt = _tile(T, LRU_TT, 8)
    nt = T // tt
    per = tt // 8

    def body(xc_ref, ga_ref, gx_ref, y_ref, lam_ref, h_ref, halo_ref, dr_ref,
             dga_ref, dgx_ref, dxc_ref, dy_ref, dlam_ref, a_buf, g_buf, carry):
        i = pl.program_id(0)

        @pl.when(i == 0)
        def _():
            carry[...] = jnp.zeros_like(carry)
            dlam_ref[...] = jnp.zeros_like(dlam_ref)

        xcv, lam = xc_ref[...], lam_ref[...]
        r, ig, sp, a, s = _lru_coeffs(xcv, ga_ref[...], gx_ref[...], lam)
        gel, dgel = _gelu_parts(y_ref[...])
        drec = dr_ref[...]
        hv = h_ref[...]
        dy_ref[...] = drec * hv * dgel
        a_buf[...] = a
        g_buf[...] = drec * gel

        def step(j, q):
            t = tt - 1 - j
            g = g_buf[pl.ds(t, 1), :] + q
            g_buf[pl.ds(t, 1), :] = g
            return a_buf[pl.ds(t, 1), :] * g

        carry[0:1, :] = lax.fori_loop(0, tt, step, carry[0:1, :], unroll=8)
        g = g_buf[...]
        halo = jnp.where(i < nt - 1, halo_ref[...], 0.0)
        hprev = pltpu.roll(jnp.concatenate([halo, hv], axis=0), 1, axis=0)[8:, :]
        da = g * hprev
        dig = g * s * xcv
        ds = g * ig * xcv
        dla = da * a - ds * (a * a) / s
        dga_ref[...] = dla * (-LRU_C * sp) * r * (1.0 - r)
        dgx_ref[...] = dig * ig * (1.0 - ig)
        dxc_ref[...] = g * s * ig
        dlam_ref[...] += jnp.sum(dla * r, axis=0, keepdims=True) * (LRU_C * _sigmoid(-lam))

    row = BS((tt, C), lambda i: (nt - 1 - i, 0))
    vec = BS((1, C), lambda i: (0, 0))
    halo = BS((8, C), lambda i: (jnp.maximum((nt - 1 - i) * per - 1, 0), 0))
    return _pcall(body, name=name, grid=(nt,), in_specs=[row, row, row, row, vec, row, halo, row],
                  out_specs=[row, row, row, row, vec], out_shape=[SDS((T, C), F32)] * 4 + [SDS((1, C), F32)],
                  scratch_shapes=[pltpu.VMEM((tt, C), F32), pltpu.VMEM((tt, C), F32), pltpu.VMEM((8, C), F32)],
                  compiler_params=_cparams(("arbitrary",)))(xc, ga, gx, y, lam.reshape(1, C), h, h, drec)


@functools.partial(jax.custom_vjp, nondiff_argnums=(5,))
def lru_scan(xc, ga, gx, y, lam, name):
    return _scan_fwd_call(xc, ga, gx, y, lam, name)[1]


def _lru_scan_fwd(xc, ga, gx, y, lam, name):
    h, rec = _scan_fwd_call(xc, ga, gx, y, lam, name)
    return rec, (xc, ga, gx, y, lam, h)


def _lru_scan_bwd(name, res, drec):
    xc, ga, gx, y, lam, h = res
    dga, dgx, dxc, dy, dlam = _scan_bwd_call(xc, ga, gx, y, lam, h, drec, name + "_bwd")
    return dxc, dga, dgx, dy, dlam.reshape(lam.shape)


lru_scan.defvjp(_lru_scan_fwd, _lru_scan_bwd)


def _att_batch(d):
    return (4, 1) if d == 1 else (1, 4)


def _att_masks(n, max_dist):
    qi = lax.broadcasted_iota(jnp.int32, (1, 2 * ATT_BLOCK, 2 * ATT_BLOCK), 1) & (ATT_BLOCK - 1)
    kj = lax.broadcasted_iota(jnp.int32, (1, 2 * ATT_BLOCK, 2 * ATT_BLOCK), 2)
    prev = (kj < ATT_BLOCK) & (kj >= qi + (ATT_BLOCK - max_dist)) & (n > 0)
    cur = (kj >= ATT_BLOCK) & (kj - ATT_BLOCK <= qi)
    return prev | cur


def _lo_lanes(rows):
    return lax.broadcasted_iota(jnp.int32, (rows, LANES), 1) < HEAD_DIM


def _lane_half(rows):
    return lax.broadcasted_iota(jnp.int32, (rows, LANES), 1) // HEAD_DIM


def _stack_heads(x2):
    lo = _lo_lanes(ATT_BLOCK)
    zero = jnp.zeros_like(x2)
    return jnp.concatenate([jnp.where(lo, x2, zero), jnp.where(lo, zero, x2)], axis=0)


def _unstack_heads(y):
    return jnp.where(_lo_lanes(ATT_BLOCK), y[:ATT_BLOCK], y[ATT_BLOCK:])


def _per_head_col(x2):
    return jnp.concatenate([x2[:, 0:1], x2[:, HEAD_DIM:HEAD_DIM + 1]], axis=0)


def _head_sums(x2):
    lo = _lo_lanes(ATT_BLOCK)
    return jnp.concatenate([jnp.sum(jnp.where(lo, x2, 0.0), axis=1, keepdims=True),
                            jnp.sum(jnp.where(lo, 0.0, x2), axis=1, keepdims=True)], axis=0)


def _att_specs(d, Wq, Wk, nb, clamp):
    cgw, sb = _att_batch(d)
    shared = Wk != Wq
    cur = (lambda n: jnp.minimum(n, nb - 1)) if clamp else (lambda n: n)
    rows, qw, kw = ATT_BLOCK * d, cgw * LANES, (LANES if shared else cgw * LANES)
    kcol = (lambda g: 0) if shared else (lambda g: g)
    qsp = BS((rows, qw), lambda g, n: (cur(n), g))
    csp = BS((rows, kw), lambda g, n: (cur(n), kcol(g)))
    psp = BS((rows, kw), lambda g, n: (jnp.maximum(cur(n) - 1, 0), kcol(g)))
    return cgw, sb, shared, qsp, csp, psp, qw, kw


def _att_streams(d, sb, work):
    if d == 1:
        work([slice(None)])
        return

    def one(j, carry):
        work([pl.ds(j * sb + i, ATT_BLOCK, stride=d) for i in range(sb)])
        return carry

    lax.fori_loop(0, d // sb, one, 0)


def _att_problem_loads(rows, cgw, shared, g, q_ref, kc_ref, kp_ref, vc_ref, vp_ref, sk_ref):
    half = _lane_half(ATT_BLOCK)

    def kv(ref, r, p):
        x = ref[r, :]
        if not shared:
            return x[:, p * LANES:(p + 1) * LANES]
        return jnp.where(half == g, x, pltpu.roll(x, HEAD_DIM, axis=1))

    qs, kb, vb, sk = [], [], [], []
    for r in rows:
        qrow = q_ref[r, :]
        for p in range(cgw):
            cols = slice(p * LANES, (p + 1) * LANES)
            qs.append(_stack_heads(qrow[:, cols].astype(BF16)))
            kb.append(jnp.concatenate([kv(kp_ref, r, p), kv(kc_ref, r, p)], axis=0).astype(BF16))
            vb.append(jnp.concatenate([kv(vp_ref, r, p), kv(vc_ref, r, p)], axis=0).astype(BF16))
            sk.append(_per_head_col(jnp.broadcast_to(sk_ref[:, cols], (ATT_BLOCK, LANES))))
    return jnp.stack(qs), jnp.stack(kb), jnp.stack(vb), jnp.stack(sk)


_BDOT_NT = (((2,), (2,)), ((0,), (0,)))
_BDOT_NN = (((2,), (1,)), ((0,), (0,)))
_BDOT_TN = (((1,), (1,)), ((0,), (0,)))


def _att_fwd_call(q, k, v, sinks, d, max_dist, name):
    T, Wq = q.shape
    Wk = k.shape[1]
    nb = T // (d * ATT_BLOCK)
    cgw, sb, shared, qsp, csp, psp, qw, kw = _att_specs(d, Wq, Wk, nb, False)
    G = Wq // qw
    assert not shared or (d == 1 and Wk == LANES and G == 2), "a shared kv pair serves two groups of 8 query heads"

    def body(q_ref, kc_ref, kp_ref, vc_ref, vp_ref, sk_ref, o_ref, lse_ref):
        g, n = pl.program_id(0), pl.program_id(1)

        def work(rows):
            qs, kband, vband, sk = _att_problem_loads(rows, cgw, shared, g, q_ref, kc_ref, kp_ref, vc_ref, vp_ref, sk_ref)
            s = lax.dot_general(qs, kband, _BDOT_NT, preferred_element_type=F32) * (HEAD_DIM ** -0.5)
            s = jnp.where(_att_masks(n, max_dist), s, NEG)
            m = jnp.maximum(jnp.max(s, axis=-1, keepdims=True), sk)
            e = jnp.exp(s - m)
            den = jnp.sum(e, axis=-1, keepdims=True) + jnp.exp(sk - m)
            o = lax.dot_general((e * (1.0 / den)).astype(BF16), vband, _BDOT_NN, preferred_element_type=F32)
            lse = jnp.broadcast_to(m + jnp.log(den), o.shape)
            for i, r in enumerate(rows):
                o_ref[r, :] = jnp.concatenate([_unstack_heads(o[i * cgw + p]) for p in range(cgw)], axis=1)
                lse_ref[r, :] = jnp.concatenate([_unstack_heads(lse[i * cgw + p]) for p in range(cgw)], axis=1)

        _att_streams(d, sb, work)

    sksp = BS((1, qw), lambda g, n: (0, g))
    return _pcall(body, name=name, grid=(G, nb), in_specs=[qsp, csp, psp, csp, psp, sksp], out_specs=[qsp, qsp],
                  out_shape=[SDS((T, Wq), F32)] * 2,
                  compiler_params=_cparams(("parallel", "parallel")))(q, k, k, v, v, sinks)


def _att_bwd_call(q, k, v, sinks, o, lse, do, dlse, d, max_dist, name):
    T, Wq = q.shape
    Wk = k.shape[1]
    nb = T // (d * ATT_BLOCK)
    cgw, sb, shared, qsp, csp, psp, qw, kw = _att_specs(d, Wq, Wk, nb, True)
    G = Wq // qw
    scale = HEAD_DIM ** -0.5

    def body(q_ref, kc_ref, kp_ref, vc_ref, vp_ref, sk_ref, o_ref, lse_ref, do_ref, dlse_ref,
             dq_ref, dk_ref, dv_ref, dsk_ref, ck, cv):
        g, n = pl.program_id(0), pl.program_id(1)

        @pl.when(n == 0)
        def _():
            ck[...] = jnp.zeros_like(ck)
            cv[...] = jnp.zeros_like(cv)
            dsk_ref[...] = jnp.zeros_like(dsk_ref)

        def put(ref, r, val):
            if len(ref.shape) == 3:
                ref[0, r, :] = val
            else:
                ref[r, :] = val

        def work(rows):
            qs, kband, vband, sk = _att_problem_loads(rows, cgw, shared, g, q_ref, kc_ref, kp_ref, vc_ref, vp_ref, sk_ref)
            dos, lse_c, corr = [], [], []
            for r in rows:
                do_r, o_r, lse_r, dlse_r = do_ref[r, :], o_ref[r, :], lse_ref[r, :], dlse_ref[r, :]
                for p in range(cgw):
                    cols = slice(p * LANES, (p + 1) * LANES)
                    dos.append(_stack_heads(do_r[:, cols].astype(BF16)))
                    lse_c.append(_per_head_col(lse_r[:, cols]))
                    corr.append(_head_sums(dlse_r[:, cols]) - _head_sums(do_r[:, cols] * o_r[:, cols]))
            dos, lse_c, corr = jnp.stack(dos), jnp.stack(lse_c), jnp.stack(corr)
            s = lax.dot_general(qs, kband, _BDOT_NT, preferred_element_type=F32) * scale
            pr = jnp.exp(jnp.where(_att_masks(n, max_dist), s, NEG) - lse_c)
            dp = lax.dot_general(dos, vband, _BDOT_NT, preferred_element_type=F32)
            ds = (pr * (dp + corr)).astype(BF16)
            dq = lax.dot_general(ds, kband, _BDOT_NN, preferred_element_type=F32) * scale
            dkb = lax.dot_general(ds, qs, _BDOT_TN, preferred_element_type=F32) * scale
            dvb = lax.dot_general(pr.astype(BF16), dos, _BDOT_TN, preferred_element_type=F32)
            dsk = jnp.exp(sk - lse_c) * corr
            lane = lax.broadcasted_iota(jnp.int32, (8, LANES), 1)
            for p in range(cgw):
                tot = [jnp.sum(jnp.stack([dsk[i * cgw + p, h * ATT_BLOCK:(h + 1) * ATT_BLOCK] for i in range(len(rows))]),
                               axis=(0, 1)).reshape(1, 1) for h in range(2)]
                dsk_ref[:, p * LANES:(p + 1) * LANES] += jnp.where(lane == 0, tot[0], jnp.where(lane == HEAD_DIM, tot[1], 0.0))

            def gather_pairs(parts):
                if not shared:
                    return jnp.concatenate(parts, axis=1)
                tot = parts[0]
                for t in parts[1:]:
                    tot = tot + t
                tot = tot + pltpu.roll(tot, HEAD_DIM, axis=1)
                return jnp.where(_lane_half(ATT_BLOCK) == g, tot, 0.0)

            for i, r in enumerate(rows):
                mine = range(i * cgw, (i + 1) * cgw)
                dq_ref[r, :] = jnp.concatenate([_unstack_heads(dq[b]) for b in mine], axis=1)
                put(dk_ref, r, ck[r, :] + gather_pairs([dkb[b, :ATT_BLOCK] for b in mine]))
                put(dv_ref, r, cv[r, :] + gather_pairs([dvb[b, :ATT_BLOCK] for b in mine]))
                ck[r, :] = gather_pairs([dkb[b, ATT_BLOCK:] for b in mine])
                cv[r, :] = gather_pairs([dvb[b, ATT_BLOCK:] for b in mine])

        @pl.when(n < nb)
        def _():
            _att_streams(d, sb, work)

        @pl.when(n == nb)
        def _():
            dk_ref[...] = ck[...].reshape(dk_ref.shape)
            dv_ref[...] = cv[...].reshape(dv_ref.shape)

    sksp = BS((1, qw), lambda g, n: (0, g))
    rows = ATT_BLOCK * d
    if shared:
        osp = BS((1, rows, kw), lambda g, n: (g, jnp.maximum(n - 1, 0), 0))
        kshape = SDS((G, T, Wk), F32)
    else:
        osp = BS((rows, kw), lambda g, n: (jnp.maximum(n - 1, 0), g))
        kshape = SDS((T, Wk), F32)
    dq, dk, dv, dsk = _pcall(
        body, name=name, grid=(G, nb + 1),
        in_specs=[qsp, csp, psp, csp, psp, sksp, qsp, qsp, qsp, qsp],
        out_specs=[qsp, osp, osp, BS((8, qw), lambda g, n: (0, g))],
        out_shape=[SDS((T, Wq), F32), kshape, kshape, SDS((8, Wq), F32)],
        scratch_shapes=[pltpu.VMEM((rows, kw), F32)] * 2,
        compiler_params=_cparams(("parallel", "arbitrary")))(q, k, k, v, v, sinks, o, lse, do, dlse)
    if shared:
        dk, dv = jnp.sum(dk, axis=0), jnp.sum(dv, axis=0)
    return dq, dk, dv, dsk[0:1]


@functools.partial(jax.custom_vjp, nondiff_argnums=(4, 5, 6))
def band_attention(q, k, v, sinks, d, max_dist, name):
    return _att_fwd_call(q, k, v, sinks, d, max_dist, name)


def _band_attention_fwd(q, k, v, sinks, d, max_dist, name):
    o, lse = _att_fwd_call(q, k, v, sinks, d, max_dist, name)
    return (o, lse), (q, k, v, sinks, o, lse)


def _band_attention_bwd(d, max_dist, name, res, g):
    q, k, v, sinks, o, lse = res
    do, dlse = g
    return _att_bwd_call(q, k, v, sinks, o, lse, do, dlse, d, max_dist, name + "_bwd")


band_attention.defvjp(_band_attention_fwd, _band_attention_bwd)


def _merge_weights(ls):
    mx = jnp.maximum(jnp.maximum(ls[0], ls[1]), ls[2])
    es = [jnp.exp(l - mx) for l in ls]
    inv = 1.0 / (es[0] + es[1] + es[2])
    return [e * inv for e in es]


def _merge_fwd_call(os_, ls_, name):
    T, W = os_[0].shape
    tt = _tile(T, 512, 8)

    def body(o1, o2, o3, l1, l2, l3, out_ref):
        w = _merge_weights([l1[...], l2[...], l3[...]])
        out_ref[...] = w[0] * o1[...] + w[1] * o2[...] + w[2] * o3[...]

    row = BS((tt, W), lambda i: (i, 0))
    return _pcall(body, name=name, grid=(T // tt,), in_specs=[row] * 6, out_specs=row,
                  out_shape=SDS((T, W), F32), compiler_params=_cparams(("parallel",)))(*os_, *ls_)


def _merge_bwd_call(os_, ls_, do, name):
    T, W = os_[0].shape
    tt = _tile(T, 512, 8)

    def body(o1, o2, o3, l1, l2, l3, do_ref, d1, d2, d3, e1, e2, e3):
        w = _merge_weights([l1[...], l2[...], l3[...]])
        dov = do_ref[...]
        ts = [dov * o[...] for o in (o1, o2, o3)]
        mean = w[0] * ts[0] + w[1] * ts[1] + w[2] * ts[2]
        for wi, ti, dref, eref in zip(w, ts, (d1, d2, d3), (e1, e2, e3)):
            dref[...] = wi * dov
            eref[...] = wi * (ti - mean)

    row = BS((tt, W), lambda i: (i, 0))
    return _pcall(body, name=name, grid=(T // tt,), in_specs=[row] * 7, out_specs=[row] * 6,
                  out_shape=[SDS((T, W), F32)] * 6, compiler_params=_cparams(("parallel",)))(*os_, *ls_, do)


@functools.partial(jax.custom_vjp, nondiff_argnums=(2,))
def merge3(os_, ls_, name):
    return _merge_fwd_call(os_, ls_, name)


def _merge3_fwd(os_, ls_, name):
    return _merge_fwd_call(os_, ls_, name), (os_, ls_)


def _merge3_bwd(name, res, do):
    os_, ls_ = res
    out = _merge_bwd_call(os_, ls_, do, name + "_bwd")
    return tuple(out[:3]), tuple(out[3:])


merge3.defvjp(_merge3_fwd, _merge3_bwd)


def _xa_probs(qb, kb, scale):
    s = lax.dot_general(qb, kb, (((1,), (1,)), ((), ())), preferred_element_type=F32) * scale
    e = jnp.exp(s - jnp.max(s, axis=-1, keepdims=True))
    return e / jnp.sum(e, axis=-1, keepdims=True)


def _xa_fwd_call(q, kv, name):
    T, W = q.shape
    M = kv.shape[0]
    hd = XA_HEAD_DIM
    tq = _tile(T, 512, 8)
    scale = hd ** -0.5

    def body(q_ref, k_ref, v_ref, o_ref):
        p = _xa_probs(q_ref[...].astype(BF16), k_ref[...].astype(BF16), scale)
        o_ref[...] = jnp.dot(p.astype(BF16), v_ref[...].astype(BF16), preferred_element_type=F32)

    qsp = BS((tq, hd), lambda i, h: (i, h))
    return _pcall(body, name=name, grid=(T // tq, XA_HEADS),
                  in_specs=[qsp, BS((M, hd), lambda i, h: (0, h)), BS((M, hd), lambda i, h: (0, XA_HEADS + h))],
                  out_specs=qsp, out_shape=SDS((T, W), F32),
                  compiler_params=_cparams(("parallel", "parallel")))(q, kv, kv)


def _xa_bwd_call(q, kv, do, name):
    T, W = q.shape
    M = kv.shape[0]
    hd = XA_HEAD_DIM
    tq = _tile(T, 512, 8)
    scale = hd ** -0.5
    dn_nt = (((1,), (1,)), ((), ()))
    dn_tn = (((0,), (0,)), ((), ()))

    def body(q_ref, k_ref, v_ref, do_ref, dq_ref, dk_ref, dv_ref):
        @pl.when(pl.program_id(1) == 0)
        def _():
            dk_ref[...] = jnp.zeros_like(dk_ref)
            dv_ref[...] = jnp.zeros_like(dv_ref)

        qb, kb, vb = q_ref[...].astype(BF16), k_ref[...].astype(BF16), v_ref[...].astype(BF16)
        p = _xa_probs(qb, kb, scale)
        dob = do_ref[...].astype(BF16)
        dp = lax.dot_general(dob, vb, dn_nt, preferred_element_type=F32)
        ds = (p * (dp - jnp.sum(p * dp, axis=-1, keepdims=True))).astype(BF16)
        dq_ref[...] = jnp.dot(ds, kb, preferred_element_type=F32) * scale
        dk_ref[...] += lax.dot_general(ds, qb, dn_tn, preferred_element_type=F32) * scale
        dv_ref[...] += lax.dot_general(p.astype(BF16), dob, dn_tn, preferred_element_type=F32)

    qsp = BS((tq, hd), lambda h, i: (i, h))
    ksp = BS((M, hd), lambda h, i: (0, h))
    return _pcall(body, name=name, grid=(XA_HEADS, T // tq),
                  in_specs=[qsp, ksp, BS((M, hd), lambda h, i: (0, XA_HEADS + h)), qsp],
                  out_specs=[qsp, ksp, ksp], out_shape=[SDS((T, W), F32), SDS((M, W), F32), SDS((M, W), F32)],
                  compiler_params=_cparams(("parallel", "arbitrary")))(q, kv, kv, do)


@functools.partial(jax.custom_vjp, nondiff_argnums=(2,))
def cross_attention(q, kv, name):
    return _xa_fwd_call(q, kv, name)


def _cross_attention_fwd(q, kv, name):
    return _xa_fwd_call(q, kv, name), (q, kv)


def _cross_attention_bwd(name, res, do):
    q, kv = res
    dq, dk, dv = _xa_bwd_call(q, kv, do, name + "_bwd")
    return dq, jnp.concatenate([dk, dv], axis=1)


cross_attention.defvjp(_cross_attention_fwd, _cross_attention_bwd)


def _gate_up_swiglu_call(hn, w1t, name):
    T, K = hn.shape
    F = w1t.shape[0] // 2
    tm, tn = _tile(T, 1024), _tile(F, 256)
    nj = F // tn
    dn = (((1,), (1,)), ((), ()))

    def body(a_ref, wg_ref, wu_ref, g_ref, u_ref, act_ref):
        a = a_ref[...]
        g = lax.dot_general(a, wg_ref[...], dn, preferred_element_type=F32)
        u = lax.dot_general(a, wu_ref[...], dn, preferred_element_type=F32)
        g_ref[...] = g
        u_ref[...] = u
        act_ref[...] = ((g * _sigmoid(g)) * u).astype(BF16)

    tile = BS((tm, tn), lambda i, j: (i, j))
    return _pcall(body, name=name, grid=(T // tm, nj),
                  in_specs=[BS((tm, K), lambda i, j: (i, 0)), BS((tn, K), lambda i, j: (j, 0)),
                            BS((tn, K), lambda i, j: (j + nj, 0))],
                  out_specs=[tile, tile, tile], out_shape=[SDS((T, F), F32), SDS((T, F), F32), SDS((T, F), BF16)],
                  compiler_params=_cparams(("parallel", "parallel")))(hn, w1t, w1t)


def _swiglu_bwd_call(g, u, dact, name):
    T, F = g.shape
    tt = _tile(T, 128, 16)

    def body(g_ref, u_ref, d_ref, o_ref):
        g, d = g_ref[...], d_ref[...]
        sg = _sigmoid(g)
        o_ref[:, :F] = (d * u_ref[...] * (sg * (1.0 + g * (1.0 - sg)))).astype(BF16)
        o_ref[:, F:] = (d * (g * sg)).astype(BF16)

    row = BS((tt, F), lambda i: (i, 0))
    return _pcall(body, name=name, grid=(T // tt,), in_specs=[row, row, row],
                  out_specs=BS((tt, 2 * F), lambda i: (i, 0)), out_shape=SDS((T, 2 * F), BF16),
                  compiler_params=_cparams(("parallel",)))(g, u, dact)


@functools.partial(jax.custom_vjp, nondiff_argnums=(6,))
def ffn_block(h, g, w1b, w1c, w2b, w2c, name):
    return _ffn_fwd(h, g, w1b, w1c, w2b, w2c, name)[0]


def _ffn_fwd(h, g, w1b, w1c, w2b, w2c, name):
    hn = _rms_fwd_call(h, g, name + "_norm", BF16)
    gate, up, act = _gate_up_swiglu_call(hn, w1b, name + "_gu")
    out = _mm(act, w2b, add=h, name=name + "_down", tm=1024, tn=1024, tk=1408)
    return out, (h, g, hn, gate, up, act, w1b, w2b)


def _ffn_bwd(name, res, dout):
    h, g, hn, gate, up, act, w1b, w2b = res
    dact = _mm(dout, w2b, tb=True, name=name + "_down_da", tm=1024, tn=1408, tk=1024)
    dw2 = _mm(act, dout, ta=True, name=name + "_down_dw", tm=1408, tn=1024, tk=1024)
    dgu = _swiglu_bwd_call(gate, up, dact, name + "_swiglu_bwd")
    dw1 = _mm(dgu, hn, ta=True, name=name + "_gu_dw", tm=1408, tn=1024, tk=1024)
    dh, dg = _mm_rms_bwd(dgu, w1b, h, g, dout, name + "_gu_da", wt=True)
    return dh, dg.reshape(g.shape), jnp.zeros_like(w1b), dw1, jnp.zeros_like(w2b), dw2


ffn_block.defvjp(_ffn_fwd, _ffn_bwd)


def _final_call(h, g, target, name):
    T, Dm = h.shape
    tt = _tile(T, 512, 8)

    def body(x_ref, g_ref, t_ref, loss_ref, dx_ref, dg_ref):
        @pl.when(pl.program_id(0) == 0)
        def _():
            loss_ref[...] = jnp.zeros_like(loss_ref)
            dg_ref[...] = jnp.zeros_like(dg_ref)

        xv, gv = x_ref[...], g_ref[...]
        r = lax.rsqrt(jnp.mean(xv * xv, axis=-1, keepdims=True) + NORM_EPS)
        xh = xv * r
        err = xh * gv - t_ref[...]
        loss_ref[...] += 0.5 * jnp.sum(jnp.mean(err * err, axis=-1, keepdims=True), axis=0, keepdims=True)
        dy = err * (1.0 / Dm)
        dyg = dy * gv
        dx_ref[...] = r * (dyg - xh * jnp.mean(dyg * xh, axis=-1, keepdims=True))
        dg_ref[...] += jnp.sum(dy * xh, axis=0, keepdims=True)

    row = BS((tt, Dm), lambda i: (i, 0))
    vec = BS((1, Dm), lambda i: (0, 0))
    return _pcall(body, name=name, grid=(T // tt,), in_specs=[row, vec, row],
                  out_specs=[BS((1, 1), lambda i: (0, 0)), row, vec],
                  out_shape=[SDS((1, 1), F32), SDS((T, Dm), F32), SDS((1, Dm), F32)],
                  compiler_params=_cparams(("arbitrary",)))(h, g.reshape(1, Dm), target)


ADAMW_BLOCK_ELEMS = 64 * 1024


def _adamw_call(parts, w, m, v, name):
    shape = w.shape
    if not isinstance(parts, (list, tuple)):
        parts, shape3 = [parts], (1,) + shape
    else:
        shape3 = shape
    n_lead = shape3[0]
    r, N = shape3[-2], shape3[-1]
    Ld = math.prod(shape3[1:-2])
    w, m, v = (t.reshape(n_lead * Ld, r, N) for t in (w, m, v))
    tr = _tile(r, max(8, ADAMW_BLOCK_ELEMS // N), 8)
    c1 = 1.0 - ADAM_B1 ** ADAM_STEP
    c2 = 1.0 - ADAM_B2 ** ADAM_STEP
    outs = None
    for lead, p in enumerate(parts):
        def body(p_ref, w_ref, m_ref, v_ref, *rest):
            g_ref, d_ref, nm_ref, nv_ref = rest[-4:]
            g = p_ref[0]
            for j in range(1, N_DEV):
                g = g + p_ref[j]
            nm = ADAM_B1 * m_ref[...] + (1.0 - ADAM_B1) * g
            nv = ADAM_B2 * v_ref[...] + (1.0 - ADAM_B2) * (g * g)
            g_ref[...] = g
            nm_ref[...] = nm
            nv_ref[...] = nv
            d_ref[...] = -ADAM_LR * ((nm / c1) / (jnp.sqrt(nv / c2) + ADAM_EPS) + ADAM_WD * w_ref[...])

        base = lead * Ld
        row = BS((1, tr, N), lambda l, i, base=base: (base + l, i, 0))
        prev = [] if outs is None else list(outs)
        outs = _pcall(body, name=f"{name}_{lead}", grid=(Ld, r // tr),
                      in_specs=[BS((N_DEV, 1, tr, N), lambda l, i: (0, l, i, 0)), row, row, row]
                      + [BS(memory_space=pl.ANY)] * len(prev),
                      out_specs=[row] * 4, out_shape=[SDS((n_lead * Ld, r, N), F32)] * 4,
                      input_output_aliases={4 + j: j for j in range(len(prev))},
                      compiler_params=_cparams(("parallel", "parallel")))(p.reshape(N_DEV, Ld, r, N), w, m, v, *prev)
    return [t.reshape(shape) for t in outs]


def _place():
    return lax.axis_index("x"), lax.axis_index("y"), lax.axis_index("c")


def _all_gather(xs, name):
    n = len(xs)
    pairs = [(i, l) for i, x in enumerate(xs) for l in range(x.shape[0])]

    def body(*refs):
        x_refs, o_refs = refs[:n], refs[n:2 * n]
        send_sems, recv_sems, local_sems = refs[2 * n:]
        x_, y_, c_ = _place()
        me, sibling = (x_, y_, c_), (x_, y_, 1 - c_)
        chips = [(1 - x_, y_), (x_, 1 - y_), (1 - x_, 1 - y_)]

        def copy(e, k, block, to, from_input=False):
            i, l = pairs[e]
            px, py, pc = block
            dst = o_refs[i].at[l, 4 * px + 2 * py + pc]
            return pltpu.make_async_remote_copy(
                src_ref=x_refs[i].at[l] if from_input else dst, dst_ref=dst,
                send_sem=send_sems.at[7 * e + k], recv_sem=recv_sems.at[7 * e + k],
                device_id=to, device_id_type=pl.DeviceIdType.MESH)

        every = range(len(pairs))
        mine = [pltpu.make_async_copy(x_refs[i].at[l], o_refs[i].at[l, 4 * x_ + 2 * y_ + c_], local_sems.at[e])
                for e, (i, l) in enumerate(pairs)]
        for cp in mine:
            cp.start()
        first = [copy(e, 0, me, sibling, True) for e in every]
        first += [copy(e, 1 + j, me, (*chip, c_), True) for j, chip in enumerate(chips) for e in every]
        for cp in first:
            cp.start()
        passed = []
        for j, chip in enumerate(chips):
            for e in every:
                copy(e, 1 + j, (*chip, c_), me).wait_recv()
            for e in every:
                cp = copy(e, 4 + j, (*chip, c_), sibling)
                cp.start()
                passed.append(cp)
        for e in every:
            copy(e, 0, sibling, me).wait_recv()
        for j, chip in enumerate(chips):
            for e in every:
                copy(e, 4 + j, (*chip, 1 - c_), me).wait_recv()
        for cp in first + passed:
            cp.wait_send()
        for cp in mine:
            cp.wait()

    any_spec = BS(memory_space=pl.ANY)
    return _pcall(body, name=name, in_specs=[any_spec] * n, out_specs=[any_spec] * n,
                  out_shape=[SDS((x.shape[0], N_DEV) + x.shape[1:], x.dtype) for x in xs],
                  scratch_shapes=[pltpu.SemaphoreType.DMA((7 * len(pairs),)), pltpu.SemaphoreType.DMA((7 * len(pairs),)),
                                  pltpu.SemaphoreType.DMA((len(pairs),))],
                  compiler_params=pltpu.CompilerParams(has_side_effects=True))(*xs)


def _peer_of(k, place):
    x_, y_, c_ = place
    fx, fy, fc = (k >> 2) & 1, (k >> 1) & 1, k & 1
    return fx + x_ - 2 * fx * x_, fy + y_ - 2 * fy * y_, fc + c_ - 2 * fc * c_


def _split_copy(src_ref, land_ref, send_sems, recv_sems, e, k, place, scatter):
    x_, y_, c_ = place
    px, py, pc = _peer_of(k, place)
    return pltpu.make_async_remote_copy(
        src_ref=src_ref.at[4 * px + 2 * py + pc] if scatter else src_ref, dst_ref=land_ref.at[4 * x_ + 2 * y_ + c_],
        send_sem=send_sems.at[7 * e + k - 1], recv_sem=recv_sems.at[7 * e + k - 1],
        device_id=(px, py, pc), device_id_type=pl.DeviceIdType.MESH)


_HBM_SPEC = BS(memory_space=pltpu.HBM)
_SEM_SPEC = BS(memory_space=pltpu.SEMAPHORE)
_EFFECT = pltpu.SideEffectType.DATAFLOW_SIDE_EFFECTING


def _copies_start(srcs, scatter, name, thru=None):
    n = len(srcs)
    lands = [lax.empty(s.shape if scatter else (N_DEV,) + s.shape, s.dtype) for s in srcs]
    passed = srcs + lands + list(thru or ())

    def body(*refs):
        src_refs, land_refs = refs[:n], refs[n:2 * n]
        send_sems, recv_sems = refs[len(passed)], refs[len(passed) + 1]
        token = refs[-1]
        place = _place()
        for e in range(n):
            for k in range(1, N_DEV):
                _split_copy(src_refs[e], land_refs[e], send_sems, recv_sems, e, k, place, scatter).start()
        token[...] = jnp.zeros_like(token)

    hbm = lambda t: pltpu.with_memory_space_constraint(t, pltpu.HBM)
    out = _pcall(body, name=name,
                 out_shape=(pltpu.SemaphoreType.DMA((7 * n,)), pltpu.SemaphoreType.DMA((7 * n,)),
                            *[pltpu.HBM(t.shape, t.dtype) for t in passed], SDS((8, LANES), F32)),
                 in_specs=[_HBM_SPEC] * len(passed),
                 out_specs=(_SEM_SPEC, _SEM_SPEC, *[_HBM_SPEC] * len(passed), BS(memory_space=pltpu.VMEM)),
                 input_output_aliases={i: 2 + i for i in range(len(passed))},
                 compiler_params=pltpu.CompilerParams(has_side_effects=_EFFECT))(*[hbm(t) for t in passed])
    return out[0], out[1], list(out[2:2 + n]), list(out[2 + n:2 + 2 * n]), list(out[2 + 2 * n:-1])


def _copies_wait(started, which, scatter, after, name):
    send_sems, recv_sems, srcs, lands, _ = started
    n = len(which)
    after = list(after) if isinstance(after, (list, tuple)) else [after]

    def body(*refs):
        src_refs, land_refs = refs[:n], refs[n:2 * n]
        send_s, recv_s = refs[2 * n], refs[2 * n + 1]
        place = _place()
        for j, e in enumerate(which):
            for k in range(1, N_DEV):
                cp = _split_copy(src_refs[j], land_refs[j], send_s, recv_s, e, k, place, scatter)
                cp.wait_send()
                cp.wait_recv()

    args = [srcs[e] for e in which] + [lands[e] for e in which]
    out = _pcall(body, name=name, out_shape=tuple(pltpu.HBM(t.shape, t.dtype) for t in args),
                 in_specs=[_HBM_SPEC] * (2 * n) + [_SEM_SPEC, _SEM_SPEC] + [BS(memory_space=pl.ANY)] * len(after),
                 out_specs=tuple([_HBM_SPEC] * (2 * n)), input_output_aliases={i: i for i in range(2 * n)},
                 compiler_params=pltpu.CompilerParams(has_side_effects=_EFFECT))(*args, send_sems, recv_sems, *after)
    return list(out[:n]), list(out[n:])


def _with_own_block(land, own_block):
    me = 4 * lax.axis_index("x") + 2 * lax.axis_index("y") + lax.axis_index("c")
    return lax.dynamic_update_index_in_dim(land, own_block, me, 0)


def _pad_flat(t, quantum=PACK_QUANTUM):
    f = t.reshape(-1)
    pad = (-f.shape[0]) % quantum
    return jnp.pad(f, (0, pad)) if pad else f


def _pack(arrs, dtype):
    return jnp.concatenate([_pad_flat(a.astype(dtype)) for a in arrs]).reshape(-1, LANES)


def _unpack(buf, shapes, lead=()):
    flat = buf.reshape(lead + (-1,))
    out, off = [], 0
    for s in shapes:
        n = math.prod(s)
        out.append(flat[..., off:off + n].reshape(lead + tuple(s)))
        off += n + (-n) % PACK_QUANTUM
    return out


def _full_from_gathered(g, axis):
    t = jnp.moveaxis(g, 0, axis)
    s = t.shape
    return t.reshape(s[:axis] + (s[axis] * s[axis + 1],) + s[axis + 2:])


def _parts_from_full(t, axis):
    s = t.shape
    t = t.reshape(s[:axis] + (N_DEV, s[axis] // N_DEV) + s[axis + 1:])
    return jnp.moveaxis(t, axis, 0)


def _head_rows(t):
    return jnp.repeat(t, HEAD_DIM).reshape(1, -1)


def _dilated_attention(q, k, v, name):
    no_sink = jnp.full((1, q.shape[1]), NEG, F32)
    outs, lses = zip(*[band_attention(q, k, v, no_sink, d, ATT_BLOCK, f"{name}_d{d}") for d in DILATIONS])
    return merge3(tuple(outs), tuple(lses), name + "_merge")


STAGES = (
    ("proj0", ('mix_norm',), (('ab_w_in', 0),)),
    ("mixer0", ('lru_conv_w', 'lru_conv_b', 'lru_ba', 'lru_bx', 'lru_lambda'),
     (('lru_wa', 0), ('lru_wx', 0), ('ab_w_out', 0))),
    ("xa0", ('xa_norm', 'xa_mem_norm'), (('xa_wq', 0), ('xa_wkv', 0), ('xa_wo', 0))),
    ("ffn0", ('ffn_norm',), (('ffn_w_gate_up', 0), ('ffn_w_down', 0))),
    ("mixer1", ('mix_norm', 'c_b_qkv', 'c_sinks', 'c_b_out'), (('c_w_qkv', 0), ('c_w_out', 0))),
    ("xa1", ('xa_norm', 'xa_mem_norm'), (('xa_wq', 1), ('xa_wkv', 1), ('xa_wo', 1))),
    ("ffn1", ('ffn_norm',), (('ffn_w_gate_up', 1), ('ffn_w_down', 1))),
)


def _stage_fn(stage, Wb, tabs, mem):
    layer = int(stage[-1])
    L = f"l{layer}"

    def run(S, Cw, h):
        def lin(a, key, bias, add, name, rows=None):
            wb, wc = Wb[key], Cw[key]
            if rows is not None:
                wb, wc = wb[rows], wc[rows]
            return linear(a, wb, wc, bias, add, name)

        def norm_lin(a, gain, key, bias, name):
            return norm_linear(a, gain, Wb[key], Cw[key], bias, key[0] in COLUMN_CUT, name)

        if stage == "mixer0":
            h, proj = h
            C = S['lru_conv_w'].shape[-1]
            xc = conv4(proj[:, :C], S['lru_conv_w'][0], S['lru_conv_b'][0], L + "_conv")
            ga, gx = lru_gates(xc, Wb['lru_wa', 0], Cw['lru_wa', 0], S['lru_ba'][0],
                               Wb['lru_wx', 0], Cw['lru_wx', 0], S['lru_bx'][0], L + "_gates")
            rec = lru_scan(xc, ga, gx, proj[:, C:2 * C], S['lru_lambda'][0], L + "_scan")
            bw = B_HEADS * HEAD_DIM
            q = rope(proj[:, 2 * C:2 * C + bw], tabs, L + "_rope_q")
            k = rope(proj[:, 2 * C + bw:2 * C + 2 * bw], tabs, L + "_rope_k")
            v = proj[:, 2 * C + 2 * bw:]
            att = _dilated_attention(q, k, v, L + "_att")
            h = lin(att, ('ab_w_out', 0), None, h, L + "_w_out_att", slice(C, None))
            return lin(rec, ('ab_w_out', 0), None, h, L + "_w_out_rec", slice(0, C))
        if stage == "mixer1":
            qw = C_HEADS * HEAD_DIM
            kw = C_KV_HEADS * HEAD_DIM
            qkv, h = norm_lin(h, S['mix_norm'][1], ('c_w_qkv', 0), S['c_b_qkv'][0], L + "_w_qkv")
            q = rope(qkv[:, :qw], tabs, L + "_rope_q")
            k = rope(qkv[:, qw:qw + kw], tabs, L + "_rope_k")
            v = qkv[:, qw + kw:]
            o, _ = band_attention(q, k, v, _head_rows(S['c_sinks'][0]), 1, ATT_BLOCK - 1, L + "_att")
            return lin(o, ('c_w_out', 0), S['c_b_out'][0], h, L + "_w_out")
        if stage.startswith("xa"):
            xq, h = norm_lin(h, S['xa_norm'][layer], ('xa_wq', layer), None, L + "_xa_wq")
            xkv, _ = norm_lin(mem, S['xa_mem_norm'][layer], ('xa_wkv', layer), None, L + "_xa_wkv")
            return lin(cross_attention(xq, xkv, L + "_xa"), ('xa_wo', layer), None, h, L + "_xa_wo")
        gu, down = ('ffn_w_gate_up', layer), ('ffn_w_down', layer)
        return ffn_block(h, S['ffn_norm'][layer], Wb[gu], Cw[gu], Wb[down], Cw[down], L + "_ffn")

    return run


def kernel(x, mem, mix_norm, ab_w_in, lru_conv_w, lru_conv_b, lru_wa, lru_ba, lru_wx, lru_bx, lru_lambda, ab_w_out, c_w_qkv, c_b_qkv, c_sinks, c_w_out, c_b_out, xa_norm, xa_mem_norm, xa_wq, xa_wkv, xa_wo, ffn_norm, ffn_w_gate_up, ffn_w_down, final_norm, loss_target, m_mix_norm, m_ab_w_in, m_lru_conv_w, m_lru_conv_b, m_lru_wa, m_lru_ba, m_lru_wx, m_lru_bx, m_lru_lambda, m_ab_w_out, m_c_w_qkv, m_c_b_qkv, m_c_sinks, m_c_w_out, m_c_b_out, m_xa_norm, m_xa_mem_norm, m_xa_wq, m_xa_wkv, m_xa_wo, m_ffn_norm, m_ffn_w_gate_up, m_ffn_w_down, m_final_norm, v_mix_norm, v_ab_w_in, v_lru_conv_w, v_lru_conv_b, v_lru_wa, v_lru_ba, v_lru_wx, v_lru_bx, v_lru_lambda, v_ab_w_out, v_c_w_qkv, v_c_b_qkv, v_c_sinks, v_c_w_out, v_c_b_out, v_xa_norm, v_xa_mem_norm, v_xa_wq, v_xa_wkv, v_xa_wo, v_ffn_norm, v_ffn_w_gate_up, v_ffn_w_down, v_final_norm):
    w_loc = dict(zip(WEIGHT_NAMES, (mix_norm, ab_w_in, lru_conv_w, lru_conv_b, lru_wa, lru_ba, lru_wx, lru_bx, lru_lambda, ab_w_out, c_w_qkv, c_b_qkv, c_sinks, c_w_out, c_b_out, xa_norm, xa_mem_norm, xa_wq, xa_wkv, xa_wo, ffn_norm, ffn_w_gate_up, ffn_w_down, final_norm)))
    m_loc = dict(zip(WEIGHT_NAMES, (m_mix_norm, m_ab_w_in, m_lru_conv_w, m_lru_conv_b, m_lru_wa, m_lru_ba, m_lru_wx, m_lru_bx, m_lru_lambda, m_ab_w_out, m_c_w_qkv, m_c_b_qkv, m_c_sinks, m_c_w_out, m_c_b_out, m_xa_norm, m_xa_mem_norm, m_xa_wq, m_xa_wkv, m_xa_wo, m_ffn_norm, m_ffn_w_gate_up, m_ffn_w_down, m_final_norm)))
    v_loc = dict(zip(WEIGHT_NAMES, (v_mix_norm, v_ab_w_in, v_lru_conv_w, v_lru_conv_b, v_lru_wa, v_lru_ba, v_lru_wx, v_lru_bx, v_lru_lambda, v_ab_w_out, v_c_w_qkv, v_c_b_qkv, v_c_sinks, v_c_w_out, v_c_b_out, v_xa_norm, v_xa_mem_norm, v_xa_wq, v_xa_wkv, v_xa_wo, v_ffn_norm, v_ffn_w_gate_up, v_ffn_w_down, v_final_norm)))

    me = 4 * lax.axis_index("x") + 2 * lax.axis_index("y") + lax.axis_index("c")

    first_keys = list(STAGES[0][2])
    keys = [key for _, _, stage_keys in STAGES[1:] for key in stage_keys]
    shards = [_shard_view(n, w_loc[n])[l].astype(BF16) for n, l in keys]
    first_g = _all_gather([_pack([w_loc[n] for n in SMALL], F32)[None]]
                          + [_shard_view(n, w_loc[n])[l].astype(BF16)[None] for n, l in first_keys], "gather_first")
    gather = _copies_start(shards, False, "gather_start", thru=[first_g[0]])
    small_g = gather[4][0][0]
    Wb = {key: _full_from_gathered(g[0], _layer_shard_axis(key[0])) for key, g in zip(first_keys, first_g[1:])}
    S = {n: w_loc[n] for n in REPLICATED}
    for n, t in zip(SMALL, _unpack(small_g, [w_loc[n].shape for n in SMALL], lead=(N_DEV,))):
        S[n] = _full_from_gathered(t, SHARD_AXIS[n])

    tabs = _rope_tables(x.shape[1])
    w_in = Wb[first_keys[0]]
    hn0 = _rms_fwd_call(x[0], S['mix_norm'][0], "l0_w_in_norm", BF16)
    h = (x[0], _mm(hn0, w_in, tb=True, name="l0_w_in", tm=1024, tn=1408, tk=1024))
    vjps = []
    for stage, small_names, stage_keys in STAGES[1:]:
        which = [keys.index(key) for key in stage_keys]
        _, lands = _copies_wait(gather, which, False, jax.tree.leaves(h)[-1], "gather_wait_" + stage)
        for e, land in zip(which, lands):
            Wb[keys[e]] = _full_from_gathered(_with_own_block(land, shards[e]), _layer_shard_axis(keys[e][0]))
        carriers = {key: jnp.zeros(Wb[key].shape, F32) for key in stage_keys}
        h, vjp_fn = jax.vjp(_stage_fn(stage, Wb, tabs, mem[0]), {n: S[n] for n in small_names}, carriers, h)
        vjps.append(vjp_fn)
    loss_part, dh, dg_final = _final_call(h, S['final_norm'], loss_target[0], "final_loss")

    grads = {'final_norm': dg_final.reshape(final_norm.shape)}
    exchanges, send_keys, send_parts = [], [], []

    def start_exchange(stage, dh):
        leaves, tree = jax.tree.flatten(dh)
        started = _copies_start(list(send_parts), True, "grad_start_" + stage, thru=leaves)
        exchanges.append((stage, started, list(send_keys)))
        send_keys.clear()
        send_parts.clear()
        return jax.tree.unflatten(tree, started[4])

    for (stage, small_names, stage_keys), vjp_fn in zip(reversed(STAGES[1:]), reversed(vjps)):
        g_small, g_big, dh = vjp_fn(dh)
        for n in small_names:
            grads[n] = grads[n] + g_small[n] if n in grads else g_small[n]
        send_keys += list(stage_keys)
        send_parts += [_parts_from_full(g_big[key], _layer_shard_axis(key[0])) for key in stage_keys]
        if stage == "xa1":
            continue
        if stage == "mixer0":
            small_parts = [_parts_from_full(grads[n], SHARD_AXIS[n]) for n in SMALL]
            send_keys.append("small")
            send_parts.append(jnp.stack([_pack([p[j] for p in small_parts], F32) for j in range(N_DEV)]))
        dh = start_exchange(stage, dh)
    d_res, d_proj = dh
    send_keys.append(first_keys[0])
    send_parts.append(_parts_from_full(_mm(d_proj, hn0, ta=True, name="l0_w_in_dw", tm=1408, tn=1024, tk=1024),
                                       _layer_shard_axis(first_keys[0][0])))
    d_res, d_proj = start_exchange("proj0", (d_res, d_proj))
    dx, dg0 = _mm_rms_bwd(d_proj, w_in, x[0], S['mix_norm'][0], d_res, "l0_w_in_da", wt=True)
    grads['mix_norm'] = grads['mix_norm'] + jnp.concatenate([dg0, jnp.zeros_like(dg0)], axis=0)
    rep_names = REPLICATED + ["loss"]
    grads["loss"] = loss_part
    zero = jnp.zeros((1, 1), F32)
    for d in (w_loc, m_loc, v_loc):
        d["loss"] = zero
    rep_started = _copies_start([_pack([grads[n] for n in rep_names], F32)], False, "rep_grads_start", thru=[dx])
    dx = rep_started[4][0]

    parts, out = {}, {}

    def end_exchange(stage, started, ex_keys, after):
        srcs, lands = _copies_wait(started, list(range(len(ex_keys))), True, after, "grad_wait_" + stage)
        for key, src, land in zip(ex_keys, srcs, lands):
            parts[key] = _with_own_block(land, lax.dynamic_index_in_dim(src, me, 0, keepdims=False))

    def adamw(p, names, call_name):
        if len(names) == 1:
            n = names[0]
            res = _adamw_call(p, _shard_view(n, w_loc[n]), _shard_view(n, m_loc[n]), _shard_view(n, v_loc[n]), call_name)
            for kind, t in zip(("grad", "delta", "new_m", "new_v"), res):
                out[kind, n] = _shard_view(n, t)
        else:
            res = _adamw_call(p, *[_pack([d[n] for n in names], F32) for d in (w_loc, m_loc, v_loc)], call_name)
            for kind, buf in zip(("grad", "delta", "new_m", "new_v"), res):
                for n, t in zip(names, _unpack(buf, [w_loc[n].shape for n in names])):
                    out[kind, n] = t

    for ex in exchanges[:-1]:
        end_exchange(*ex, dx)
    last_names = {key[0] for key in exchanges[-1][2]}
    for n in BIG:
        if n not in last_names:
            adamw([parts[n, l] for l in range(w_loc[n].shape[0])], [n], "adamw_" + n)
    adamw(parts["small"], SMALL, "adamw_small")
    end_exchange(*exchanges[-1], [out["new_v", n] for n in BIG if n not in last_names])
    for n in BIG:
        if n in last_names:
            adamw([parts[n, l] for l in range(w_loc[n].shape[0])], [n], "adamw_" + n)
    rep_src, rep_land = _copies_wait(rep_started, [0], False, out["new_v", "ab_w_in"], "rep_grads_wait")
    adamw(_with_own_block(rep_land[0], rep_src[0]), rep_names, "adamw_replicated")
    loss = out["grad", "loss"][0, 0]

    return (loss, dx[None], *[out[kind, n] for kind in ("grad", "delta", "new_m", "new_v") for n in WEIGHT_NAMES])
```

```python
import functools
import math

import jax
import jax.numpy as jnp
from jax import lax
from jax.experimental import pallas as pl
from jax.experimental.pallas import tpu as pltpu

F32 = jnp.float32
BF16 = jnp.bfloat16
SDS = jax.ShapeDtypeStruct
BS = pl.BlockSpec

N_DEV = 8
NORM_EPS = 1e-6
ROPE_THETA = 500000.0
HEAD_DIM = 64
ROT_DIM = 16
ATT_BLOCK = 128
LRU_C = 8.0
LRU_HEADS = 4
DILATIONS = (1, 4, 16)
B_HEADS = 8
C_HEADS = 16
C_KV_HEADS = 2
XA_HEADS = 4
XA_HEAD_DIM = 128
NEG = -1e30
ADAM_LR, ADAM_B1, ADAM_B2, ADAM_EPS, ADAM_WD, ADAM_STEP = 0.001, 0.9, 0.999, 1e-08, 0.01, 10
LANES = 128
VMEM_LIMIT = 48 * 1024 * 1024

WEIGHT_NAMES = ['mix_norm', 'ab_w_in', 'lru_conv_w', 'lru_conv_b', 'lru_wa', 'lru_ba', 'lru_wx', 'lru_bx',
                'lru_lambda', 'ab_w_out', 'c_w_qkv', 'c_b_qkv', 'c_sinks', 'c_w_out', 'c_b_out', 'xa_norm',
                'xa_mem_norm', 'xa_wq', 'xa_wkv', 'xa_wo', 'ffn_norm', 'ffn_w_gate_up', 'ffn_w_down', 'final_norm']
SHARD_AXIS = {'ab_w_in': 2, 'lru_conv_w': 2, 'lru_wa': 2, 'lru_ba': 2, 'lru_wx': 2, 'lru_bx': 2, 'ab_w_out': 1,
              'c_w_qkv': 2, 'c_b_qkv': 1, 'c_w_out': 1, 'c_b_out': 1, 'xa_wq': 1, 'xa_wkv': 1, 'xa_wo': 2,
              'ffn_w_gate_up': 2, 'ffn_w_down': 1}
BIG = ['ab_w_in', 'lru_wa', 'lru_wx', 'ab_w_out', 'c_w_qkv', 'c_w_out', 'xa_wq', 'xa_wkv', 'xa_wo',
       'ffn_w_gate_up', 'ffn_w_down']
SMALL = ['lru_conv_w', 'lru_ba', 'lru_bx', 'c_b_qkv', 'c_b_out']
SHARDED = BIG + SMALL
REPLICATED = [n for n in WEIGHT_NAMES if n not in SHARD_AXIS]
COLUMN_CUT = ('ab_w_in', 'c_w_qkv', 'ffn_w_gate_up')
PACK_QUANTUM = 2048


def _shard_view(name, t):
    return jnp.swapaxes(t, -1, -2) if name in COLUMN_CUT else t


def _layer_shard_axis(name):
    return 0 if name in COLUMN_CUT else SHARD_AXIS[name] - 1


def _pcall(body, **kw):
    return pl.pallas_call(body, **kw)


def _cparams(sem=None):
    return pltpu.CompilerParams(dimension_semantics=sem, vmem_limit_bytes=VMEM_LIMIT)


def _tile(n, target, mult=LANES):
    if n <= target:
        return n
    t = (target // mult) * mult
    while t >= mult:
        if n % t == 0:
            return t
        t -= mult
    return n


def _sigmoid(x):
    return 1.0 / (1.0 + jnp.exp(-x))


def _expm1(x):
    small = x * (1.0 + x * (0.5 + x * (1.0 / 6.0 + x * (1.0 / 24.0))))
    return jnp.where(jnp.abs(x) < 0.03, small, jnp.exp(x) - 1.0)


_GELU_C = math.sqrt(2.0 / math.pi)


def _gelu_parts(y):
    y2 = y * y
    th = jnp.tanh(_GELU_C * (y + 0.044715 * y * y2))
    g = 0.5 * y * (1.0 + th)
    dg = 0.5 * (1.0 + th) + 0.5 * y * (1.0 - th * th) * _GELU_C * (1.0 + 3.0 * 0.044715 * y2)
    return g, dg


def _mm(a, b, *, ta=False, tb=False, bias=None, add=None, name, tm=512, tn=512, tk=2048):
    M, K = (a.shape[1], a.shape[0]) if ta else a.shape
    N = b.shape[0] if tb else b.shape[1]
    tm, tn, tk = _tile(M, tm), _tile(N, tn), _tile(K, tk)
    nk = K // tk
    dn = (((0 if ta else 1,), (1 if tb else 0,)), ((), ()))

    def body(*refs):
        a_ref, b_ref = refs[0], refs[1]
        pos = 2
        bias_ref = add_ref = None
        if bias is not None:
            bias_ref = refs[pos]
            pos += 1
        if add is not None:
            add_ref = refs[pos]
            pos += 1
        o_ref = refs[pos]
        part = lax.dot_general(a_ref[...].astype(BF16), b_ref[...].astype(BF16), dn, preferred_element_type=F32)

        def finish(r):
            if bias_ref is not None:
                r = r + bias_ref[...]
            if add_ref is not None:
                r = r + add_ref[...]
            o_ref[...] = r

        if nk == 1:
            finish(part)
            return
        acc_ref = refs[pos + 1]
        k = pl.program_id(2)

        @pl.when(k == 0)
        def _():
            acc_ref[...] = part

        @pl.when((k > 0) & (k < nk - 1))
        def _():
            acc_ref[...] += part

        @pl.when(k == nk - 1)
        def _():
            finish(acc_ref[...] + part)

    in_specs = [BS((tk, tm), lambda i, j, k: (k, i)) if ta else BS((tm, tk), lambda i, j, k: (i, k)),
                BS((tn, tk), lambda i, j, k: (j, k)) if tb else BS((tk, tn), lambda i, j, k: (k, j))]
    args = [a, b]
    if bias is not None:
        in_specs.append(BS((1, tn), lambda i, j, k: (0, j)))
        args.append(bias.reshape(1, N))
    if add is not None:
        in_specs.append(BS((tm, tn), lambda i, j, k: (i, j)))
        args.append(add)
    return _pcall(body, name=name, grid=(M // tm, N // tn, nk), in_specs=in_specs,
                  out_specs=BS((tm, tn), lambda i, j, k: (i, j)), out_shape=SDS((M, N), F32),
                  scratch_shapes=[pltpu.VMEM((tm, tn), F32)] if nk > 1 else [],
                  compiler_params=_cparams(("parallel", "parallel", "arbitrary")))(*args)


def _mm_rms_bwd(dy, wb, x, g, add, name, wt=False):
    T, Kc = dy.shape
    Dm = wb.shape[1] if wt else wb.shape[0]
    tm, tk = _tile(T, 1024), _tile(Kc, 1408)
    nk = Kc // tk
    dn = (((1,), (0 if wt else 1,)), ((), ()))

    def body(*refs):
        dy_ref, w_ref, x_ref, g_ref = refs[:4]
        dx_ref, dg_ref, acc_ref = refs[-3:]
        i, k = pl.program_id(0), pl.program_id(1)
        part = lax.dot_general(dy_ref[...].astype(BF16), w_ref[...], dn, preferred_element_type=F32)

        @pl.when(k == 0)
        def _():
            acc_ref[...] = part

        @pl.when((k > 0) & (k < nk - 1))
        def _():
            acc_ref[...] += part

        @pl.when((i == 0) & (k == 0))
        def _():
            dg_ref[...] = jnp.zeros_like(dg_ref)

        @pl.when(k == nk - 1)
        def _():
            d = part if nk == 1 else acc_ref[...] + part
            xv = x_ref[...]
            r = lax.rsqrt(jnp.mean(xv * xv, axis=-1, keepdims=True) + NORM_EPS)
            xh = xv * r
            dyg = d * g_ref[...]
            dx = r * (dyg - xh * jnp.mean(dyg * xh, axis=-1, keepdims=True))
            dx_ref[...] = dx if add is None else dx + refs[4][...]
            dg_ref[...] += jnp.sum(d * xh, axis=0, keepdims=True)

    row = BS((tm, Dm), lambda i, k: (i, 0))
    vec = BS((1, Dm), lambda i, k: (0, 0))
    extra = [] if add is None else [add]
    return _pcall(body, name=name, grid=(T // tm, nk),
                  in_specs=[BS((tm, tk), lambda i, k: (i, k)),
                            BS((tk, Dm), lambda i, k: (k, 0)) if wt else BS((Dm, tk), lambda i, k: (0, k)), row, vec]
                  + [row] * len(extra),
                  out_specs=[row, vec], out_shape=[SDS((T, Dm), F32), SDS((1, Dm), F32)],
                  scratch_shapes=[pltpu.VMEM((tm, Dm), F32)],
                  compiler_params=_cparams(("arbitrary", "arbitrary")))(dy, wb, x, g.reshape(1, Dm), *extra)


def _colsum(x, name):
    T, N = x.shape
    tt = _tile(T, 512, 8)

    def body(x_ref, o_ref):
        @pl.when(pl.program_id(0) == 0)
        def _():
            o_ref[...] = jnp.zeros_like(o_ref)

        o_ref[...] += jnp.sum(x_ref[...], axis=0, keepdims=True)

    return _pcall(body, name=name, grid=(T // tt,), in_specs=[BS((tt, N), lambda i: (i, 0))],
                  out_specs=BS((1, N), lambda i: (0, 0)), out_shape=SDS((1, N), F32),
                  compiler_params=_cparams(("arbitrary",)))(x)


@functools.partial(jax.custom_vjp, nondiff_argnums=(5,))
def linear(a, wb, wc, bias, add, name):
    return _mm(a, wb, bias=bias, add=add, name=name)


def _linear_fwd(a, wb, wc, bias, add, name):
    return _mm(a, wb, bias=bias, add=add, name=name), (a, wb, bias is not None, add is not None)


def _linear_bwd(name, res, g):
    a, wb, has_bias, has_add = res
    da = _mm(g, wb, tb=True, name=name + "_da")
    dw = _mm(a, g, ta=True, name=name + "_dw")
    dbias = _colsum(g, name + "_db").reshape(-1) if has_bias else None
    return da, jnp.zeros_like(wb), dw, dbias, (g if has_add else None)


linear.defvjp(_linear_fwd, _linear_bwd)


def _rms_fwd_call(x, g, name, out_dtype=F32):
    T, Dm = x.shape
    tt = _tile(T, 512, 16)

    def body(x_ref, g_ref, o_ref):
        xv = x_ref[...]
        r = lax.rsqrt(jnp.mean(xv * xv, axis=-1, keepdims=True) + NORM_EPS)
        o_ref[...] = ((xv * r) * g_ref[...]).astype(out_dtype)

    return _pcall(body, name=name, grid=(T // tt,),
                  in_specs=[BS((tt, Dm), lambda i: (i, 0)), BS((1, Dm), lambda i: (0, 0))],
                  out_specs=BS((tt, Dm), lambda i: (i, 0)), out_shape=SDS((T, Dm), out_dtype),
                  compiler_params=_cparams(("parallel",)))(x, g.reshape(1, Dm))


@functools.partial(jax.custom_vjp, nondiff_argnums=(5, 6))
def norm_linear(x, g, wb, wc, bias, wt, name):
    return _norm_linear_fwd(x, g, wb, wc, bias, wt, name)[0]


def _norm_linear_fwd(x, g, wb, wc, bias, wt, name):
    hn = _rms_fwd_call(x, g, name + "_norm", BF16)
    return (_mm(hn, wb, tb=wt, bias=bias, name=name, tm=1024, tn=1408, tk=1024), x), (x, g, hn, wb, bias is not None)


def _norm_linear_bwd(wt, name, res, cts):
    x, g, hn, wb, has_bias = res
    dy, dres = cts
    if wt:
        dw = _mm(dy, hn, ta=True, name=name + "_dw", tm=1408, tn=1024, tk=1024)
    else:
        dw = _mm(hn, dy, ta=True, name=name + "_dw", tm=1024, tn=1408, tk=1024)
    dx, dg = _mm_rms_bwd(dy, wb, x, g, dres, name + "_da", wt)
    dbias = _colsum(dy, name + "_db").reshape(-1) if has_bias else None
    return dx, dg.reshape(g.shape), jnp.zeros_like(wb), dw, dbias


norm_linear.defvjp(_norm_linear_fwd, _norm_linear_bwd)


def _rope_tables(T):
    half = ROT_DIM // 2
    inv = ROPE_THETA ** (-jnp.arange(0, ROT_DIM, 2, dtype=F32) / ROT_DIM)
    lane = jnp.arange(LANES) % HEAD_DIM
    freq = jnp.where(lane < ROT_DIM, inv[lane % half], 0.0)
    ang = jnp.arange(T, dtype=F32)[:, None] * freq[None, :]
    c, s = jnp.cos(ang), jnp.sin(ang)
    sa = jnp.where((lane >= half) & (lane < ROT_DIM), s, 0.0)
    sb = jnp.where(lane < half, -s, 0.0)
    return c, sa, sb


def _rope_call(x, tabs, inverse, name):
    T, W = x.shape
    tt = _tile(T, 512, 8)
    reps = W // LANES
    half = ROT_DIM // 2

    def body(x_ref, c_ref, sa_ref, sb_ref, o_ref):
        xv = x_ref[...]
        c = jnp.tile(c_ref[...], (1, reps))
        sa = jnp.tile(sa_ref[...], (1, reps))
        sb = jnp.tile(sb_ref[...], (1, reps))
        if not inverse:
            o_ref[...] = xv * c + pltpu.roll(xv, half, axis=1) * sa + pltpu.roll(xv, W - half, axis=1) * sb
        else:
            o_ref[...] = xv * c + pltpu.roll(xv * sa, W - half, axis=1) + pltpu.roll(xv * sb, half, axis=1)

    row = BS((tt, W), lambda i: (i, 0))
    tab = BS((tt, LANES), lambda i: (i, 0))
    return _pcall(body, name=name, grid=(T // tt,), in_specs=[row, tab, tab, tab], out_specs=row,
                  out_shape=SDS((T, W), F32), compiler_params=_cparams(("parallel",)))(x, *tabs)


@functools.partial(jax.custom_vjp, nondiff_argnums=(2,))
def rope(x, tabs, name):
    return _rope_call(x, tabs, False, name)


def _rope_fwd(x, tabs, name):
    return _rope_call(x, tabs, False, name), tabs


def _rope_bwd(name, tabs, dy):
    return _rope_call(dy, tabs, True, name + "_bwd"), jax.tree.map(jnp.zeros_like, tabs)


rope.defvjp(_rope_fwd, _rope_bwd)


def _conv_fwd_call(x, w, b, name):
    T, C = x.shape
    tt = _tile(T, 512, 8)
    per = tt // 8

    def body(x_ref, halo_ref, w_ref, b_ref, o_ref):
        i = pl.program_id(0)
        halo = jnp.where(i > 0, halo_ref[...], 0.0)
        e = jnp.concatenate([halo, x_ref[...]], axis=0)
        acc = b_ref[...]
        for k in (3, 2, 1):
            acc = acc + pltpu.roll(e, k, axis=0)[8:, :] * w_ref[3 - k:4 - k, :]
        o_ref[...] = acc + x_ref[...] * w_ref[3:4, :]

    return _pcall(body, name=name, grid=(T // tt,),
                  in_specs=[BS((tt, C), lambda i: (i, 0)), BS((8, C), lambda i: (jnp.maximum(i * per - 1, 0), 0)),
                            BS((4, C), lambda i: (0, 0)), BS((1, C), lambda i: (0, 0))],
                  out_specs=BS((tt, C), lambda i: (i, 0)), out_shape=SDS((T, C), F32),
                  compiler_params=_cparams(("parallel",)))(x, x, w, b.reshape(1, C))


def _conv_bwd_call(x, w, dy, name):
    T, C = x.shape
    tt = _tile(T, 512, 8)
    per = tt // 8
    nt = T // tt

    def body(x_ref, halo_ref, w_ref, dy_ref, nxt_ref, dx_ref, dwb_ref):
        i = pl.program_id(0)
        halo = jnp.where(i > 0, halo_ref[...], 0.0)
        e = jnp.concatenate([halo, x_ref[...]], axis=0)
        dy = dy_ref[...]
        nxt = jnp.where(i < nt - 1, nxt_ref[...], 0.0)
        f = jnp.concatenate([dy, nxt], axis=0)
        dx = dy * w_ref[3:4, :]
        rows = [None] * 4
        rows[3] = jnp.sum(dy * x_ref[...], axis=0, keepdims=True)
        for k in (1, 2, 3):
            dx = dx + pltpu.roll(f, tt + 8 - k, axis=0)[:tt, :] * w_ref[3 - k:4 - k, :]
            rows[3 - k] = jnp.sum(dy * pltpu.roll(e, k, axis=0)[8:, :], axis=0, keepdims=True)
        dx_ref[...] = dx
        upd = jnp.concatenate(rows + [jnp.sum(dy, axis=0, keepdims=True), jnp.zeros((3, C), F32)], axis=0)

        @pl.when(i == 0)
        def _():
            dwb_ref[...] = jnp.zeros_like(dwb_ref)

        dwb_ref[...] += upd

    row = BS((tt, C), lambda i: (i, 0))
    return _pcall(body, name=name, grid=(nt,),
                  in_specs=[row, BS((8, C), lambda i: (jnp.maximum(i * per - 1, 0), 0)), BS((4, C), lambda i: (0, 0)),
                            row, BS((8, C), lambda i: (jnp.minimum((i + 1) * per, T // 8 - 1), 0))],
                  out_specs=[row, BS((8, C), lambda i: (0, 0))],
                  out_shape=[SDS((T, C), F32), SDS((8, C), F32)],
                  compiler_params=_cparams(("arbitrary",)))(x, x, w, dy, dy)


@functools.partial(jax.custom_vjp, nondiff_argnums=(3,))
def conv4(x, w, b, name):
    return _conv_fwd_call(x, w, b, name)


def _conv4_fwd(x, w, b, name):
    return _conv_fwd_call(x, w, b, name), (x, w)


def _conv4_bwd(name, res, dy):
    x, w = res
    dx, dwb = _conv_bwd_call(x, w, dy, name + "_bwd")
    return dx, dwb[0:4], dwb[4]


conv4.defvjp(_conv4_fwd, _conv4_bwd)


def _gates_fwd_call(xc, wa, ba, wx, bx, name):
    T, C = xc.shape
    hd = C // LRU_HEADS
    tt = _tile(T, 512, 8)

    def body(x_ref, wa_ref, ba_ref, wx_ref, bx_ref, ga_ref, gx_ref):
        xb = x_ref[...].astype(BF16)
        ga_ref[...] = jnp.dot(xb, wa_ref[0].astype(BF16), preferred_element_type=F32) + ba_ref[...]
        gx_ref[...] = jnp.dot(xb, wx_ref[0].astype(BF16), preferred_element_type=F32) + bx_ref[...]

    blk = BS((tt, hd), lambda i, h: (i, h))
    wsp = BS((1, hd, hd), lambda i, h: (h, 0, 0))
    bsp = BS((1, hd), lambda i, h: (0, h))
    return _pcall(body, name=name, grid=(T // tt, LRU_HEADS), in_specs=[blk, wsp, bsp, wsp, bsp],
                  out_specs=[blk, blk], out_shape=[SDS((T, C), F32)] * 2,
                  compiler_params=_cparams(("parallel", "parallel")))(xc, wa, ba.reshape(1, C), wx, bx.reshape(1, C))


def _gates_bwd_x_call(dga, dgx, wa, wx, name):
    T, C = dga.shape
    hd = C // LRU_HEADS
    tt = _tile(T, 512, 8)
    dn = (((1,), (1,)), ((), ()))

    def body(da_ref, dx_ref, wa_ref, wx_ref, o_ref):
        o_ref[...] = (lax.dot_general(da_ref[...].astype(BF16), wa_ref[0].astype(BF16), dn, preferred_element_type=F32)
                      + lax.dot_general(dx_ref[...].astype(BF16), wx_ref[0].astype(BF16), dn, preferred_element_type=F32))

    blk = BS((tt, hd), lambda i, h: (i, h))
    wsp = BS((1, hd, hd), lambda i, h: (h, 0, 0))
    return _pcall(body, name=name, grid=(T // tt, LRU_HEADS), in_specs=[blk, blk, wsp, wsp], out_specs=blk,
                  out_shape=SDS((T, C), F32), compiler_params=_cparams(("parallel", "parallel")))(dga, dgx, wa, wx)


def _gates_bwd_w_call(xc, dga, dgx, name):
    T, C = xc.shape
    hd = C // LRU_HEADS
    tt = _tile(T, 512, 8)
    dn = (((0,), (0,)), ((), ()))

    def body(x_ref, da_ref, dx_ref, dwa_ref, dwx_ref, dba_ref, dbx_ref):
        @pl.when(pl.program_id(1) == 0)
        def _():
            dwa_ref[...] = jnp.zeros_like(dwa_ref)
            dwx_ref[...] = jnp.zeros_like(dwx_ref)
            dba_ref[...] = jnp.zeros_like(dba_ref)
            dbx_ref[...] = jnp.zeros_like(dbx_ref)

        xb = x_ref[...].astype(BF16)
        da, dx = da_ref[...], dx_ref[...]
        dwa_ref[0] += lax.dot_general(xb, da.astype(BF16), dn, preferred_element_type=F32)
        dwx_ref[0] += lax.dot_general(xb, dx.astype(BF16), dn, preferred_element_type=F32)
        dba_ref[...] += jnp.sum(da, axis=0, keepdims=True)
        dbx_ref[...] += jnp.sum(dx, axis=0, keepdims=True)

    blk = BS((tt, hd), lambda h, i: (i, h))
    wsp = BS((1, hd, hd), lambda h, i: (h, 0, 0))
    bsp = BS((1, hd), lambda h, i: (0, h))
    return _pcall(body, name=name, grid=(LRU_HEADS, T // tt), in_specs=[blk, blk, blk],
                  out_specs=[wsp, wsp, bsp, bsp],
                  out_shape=[SDS((LRU_HEADS, hd, hd), F32)] * 2 + [SDS((1, C), F32)] * 2,
                  compiler_params=_cparams(("parallel", "arbitrary")))(xc, dga, dgx)


@functools.partial(jax.custom_vjp, nondiff_argnums=(7,))
def lru_gates(xc, wa, wa_c, ba, wx, wx_c, bx, name):
    return tuple(_gates_fwd_call(xc, wa, ba, wx, bx, name))


def _lru_gates_fwd(xc, wa, wa_c, ba, wx, wx_c, bx, name):
    return tuple(_gates_fwd_call(xc, wa, ba, wx, bx, name)), (xc, wa, wx, ba.shape)


def _lru_gates_bwd(name, res, g):
    xc, wa, wx, bshape = res
    dga, dgx = g
    dxc = _gates_bwd_x_call(dga, dgx, wa, wx, name + "_dx")
    dwa, dwx, dba, dbx = _gates_bwd_w_call(xc, dga, dgx, name + "_dw")
    return dxc, jnp.zeros_like(wa), dwa, dba.reshape(bshape), jnp.zeros_like(wx), dwx, dbx.reshape(bshape)


lru_gates.defvjp(_lru_gates_fwd, _lru_gates_bwd)


def _lru_coeffs(xc, ga, gx, lam):
    r = _sigmoid(ga)
    ig = _sigmoid(gx)
    z = -lam
    sp = jnp.maximum(z, 0.0) + jnp.log(1.0 + jnp.exp(-jnp.abs(z)))
    la = -LRU_C * r * sp
    a = jnp.exp(la)
    s = jnp.sqrt(-_expm1(2.0 * la))
    return r, ig, sp, a, s


LRU_TT = 256


def _scan_fwd_call(xc, ga, gx, y, lam, name):
    T, C = xc.shape
    tt = _tile(T, LRU_TT, 8)

    def body(xc_ref, ga_ref, gx_ref, y_ref, lam_ref, h_ref, rec_ref, a_buf, carry):
        @pl.when(pl.program_id(0) == 0)
        def _():
            carry[...] = jnp.zeros_like(carry)

        xcv = xc_ref[...]
        _, ig, _, a, s = _lru_coeffs(xcv, ga_ref[...], gx_ref[...], lam_ref[...])
        a_buf[...] = a
        h_ref[...] = s * (ig * xcv)

        def step(t, h):
            hn = a_buf[pl.ds(t, 1), :] * h + h_ref[pl.ds(t, 1), :]
            h_ref[pl.ds(t, 1), :] = hn
            return hn

        carry[0:1, :] = lax.fori_loop(0, tt, step, carry[0:1, :], unroll=8)
        g, _ = _gelu_parts(y_ref[...])
        rec_ref[...] = h_ref[...] * g

    row = BS((tt, C), lambda i: (i, 0))
    vec = BS((1, C), lambda i: (0, 0))
    return _pcall(body, name=name, grid=(T // tt,), in_specs=[row, row, row, row, vec], out_specs=[row, row],
                  out_shape=[SDS((T, C), F32)] * 2,
                  scratch_shapes=[pltpu.VMEM((tt, C), F32), pltpu.VMEM((8, C), F32)],
                  compiler_params=_cparams(("arbitrary",)))(xc, ga, gx, y, lam.reshape(1, C))


def _scan_bwd_call(xc, ga, gx, y, lam, h, drec, name):
    T, C = xc.shape
    tt = _tile(T, LRU_TT, 8)
    nt = T // tt
    per = tt // 8

    def body(xc_ref, ga_ref, gx_ref, y_ref, lam_ref, h_ref, halo_ref, dr_ref,
             dga_ref, dgx_ref, dxc_ref, dy_ref, dlam_ref, a_buf, g_buf, carry):
        i = pl.program_id(0)

        @pl.when(i == 0)
        def _():
            carry[...] = jnp.zeros_like(carry)
            dlam_ref[...] = jnp.zeros_like(dlam_ref)

        xcv, lam = xc_ref[...], lam_ref[...]
        r, ig, sp, a, s = _lru_coeffs(xcv, ga_ref[...], gx_ref[...], lam)
        gel, dgel = _gelu_parts(y_ref[...])
        drec = dr_ref[...]
        hv = h_ref[...]
        dy_ref[...] = drec * hv * dgel
        a_buf[...] = a
        g_buf[...] = drec * gel

        def step(j, q):
            t = tt - 1 - j
            g = g_buf[pl.ds(t, 1), :] + q
            g_buf[pl.ds(t, 1), :] = g
            return a_buf[pl.ds(t, 1), :] * g

        carry[0:1, :] = lax.fori_loop(0, tt, step, carry[0:1, :], unroll=8)
        g = g_buf[...]
        halo = jnp.where(i < nt - 1, halo_ref[...], 0.0)
        hprev = pltpu.roll(jnp.concatenate([halo, hv], axis=0), 1, axis=0)[8:, :]
        da = g * hprev
        dig = g * s * xcv
        ds = g * ig * xcv
        dla = da * a - ds * (a * a) / s
        dga_ref[...] = dla * (-LRU_C * sp) * r * (1.0 - r)
        dgx_ref[...] = dig * ig * (1.0 - ig)
        dxc_ref[...] = g * s * ig
        dlam_ref[...] += jnp.sum(dla * r, axis=0, keepdims=True) * (LRU_C * _sigmoid(-lam))

    row = BS((tt, C), lambda i: (nt - 1 - i, 0))
    vec = BS((1, C), lambda i: (0, 0))
    halo = BS((8, C), lambda i: (jnp.maximum((nt - 1 - i) * per - 1, 0), 0))
    return _pcall(body, name=name, grid=(nt,), in_specs=[row, row, row, row, vec, row, halo, row],
                  out_specs=[row, row, row, row, vec], out_shape=[SDS((T, C), F32)] * 4 + [SDS((1, C), F32)],
                  scratch_shapes=[pltpu.VMEM((tt, C), F32), pltpu.VMEM((tt, C), F32), pltpu.VMEM((8, C), F32)],
                  compiler_params=_cparams(("arbitrary",)))(xc, ga, gx, y, lam.reshape(1, C), h, h, drec)


@functools.partial(jax.custom_vjp, nondiff_argnums=(5,))
def lru_scan(xc, ga, gx, y, lam, name):
    return _scan_fwd_call(xc, ga, gx, y, lam, name)[1]


def _lru_scan_fwd(xc, ga, gx, y, lam, name):
    h, rec = _scan_fwd_call(xc, ga, gx, y, lam, name)
    return rec, (xc, ga, gx, y, lam, h)


def _lru_scan_bwd(name, res, drec):
    xc, ga, gx, y, lam, h = res
    dga, dgx, dxc, dy, dlam = _scan_bwd_call(xc, ga, gx, y, lam, h, drec, name + "_bwd")
    return dxc, dga, dgx, dy, dlam.reshape(lam.shape)


lru_scan.defvjp(_lru_scan_fwd, _lru_scan_bwd)


def _att_batch(d, shared=False):
    if shared:
        return 8, 1
    return (4, 1) if d == 1 else (1, 4)


def _att_masks(n, max_dist):
    qi = lax.broadcasted_iota(jnp.int32, (1, 2 * ATT_BLOCK, 2 * ATT_BLOCK), 1) & (ATT_BLOCK - 1)
    kj = lax.broadcasted_iota(jnp.int32, (1, 2 * ATT_BLOCK, 2 * ATT_BLOCK), 2)
    prev = (kj < ATT_BLOCK) & (kj >= qi + (ATT_BLOCK - max_dist)) & (n > 0)
    cur = (kj >= ATT_BLOCK) & (kj - ATT_BLOCK <= qi)
    return prev | cur


def _lo_lanes(rows):
    return lax.broadcasted_iota(jnp.int32, (rows, LANES), 1) < HEAD_DIM


def _lane_half(rows):
    return lax.broadcasted_iota(jnp.int32, (rows, LANES), 1) // HEAD_DIM


def _stack_heads(x2):
    lo = _lo_lanes(ATT_BLOCK)
    zero = jnp.zeros_like(x2)
    return jnp.concatenate([jnp.where(lo, x2, zero), jnp.where(lo, zero, x2)], axis=0)


def _unstack_heads(y):
    return jnp.where(_lo_lanes(ATT_BLOCK), y[:ATT_BLOCK], y[ATT_BLOCK:])


def _per_head_col(x2):
    return jnp.concatenate([x2[:, 0:1], x2[:, HEAD_DIM:HEAD_DIM + 1]], axis=0)


def _head_sums(x2):
    lo = _lo_lanes(ATT_BLOCK)
    return jnp.concatenate([jnp.sum(jnp.where(lo, x2, 0.0), axis=1, keepdims=True),
                            jnp.sum(jnp.where(lo, 0.0, x2), axis=1, keepdims=True)], axis=0)


def _att_specs(d, Wq, Wk, nb, clamp):
    shared = Wk != Wq
    cgw, sb = _att_batch(d, shared)
    cur = (lambda n: jnp.minimum(n, nb - 1)) if clamp else (lambda n: n)
    rows, qw, kw = ATT_BLOCK * d, cgw * LANES, (LANES if shared else cgw * LANES)
    kcol = (lambda g: 0) if shared else (lambda g: g)
    qsp = BS((rows, qw), lambda g, n: (cur(n), g))
    csp = BS((rows, kw), lambda g, n: (cur(n), kcol(g)))
    psp = BS((rows, kw), lambda g, n: (jnp.maximum(cur(n) - 1, 0), kcol(g)))
    return cgw, sb, shared, qsp, csp, psp, qw, kw


def _att_streams(d, sb, work):
    if d == 1:
        work([slice(None)])
        return

    def one(j, carry):
        work([pl.ds(j * sb + i, ATT_BLOCK, stride=d) for i in range(sb)])
        return carry

    lax.fori_loop(0, d // sb, one, 0)


def _att_problem_loads(rows, cgw, shared, g, q_ref, kc_ref, kp_ref, vc_ref, vp_ref, sk_ref):
    half = _lane_half(ATT_BLOCK)

    def kv(ref, r, p):
        x = ref[r, :]
        if not shared:
            return x[:, p * LANES:(p + 1) * LANES]
        return jnp.where(half == p // 4, x, pltpu.roll(x, HEAD_DIM, axis=1))

    qs, kb, vb, sk = [], [], [], []
    for r in rows:
        qrow = q_ref[r, :]
        for p in range(cgw):
            cols = slice(p * LANES, (p + 1) * LANES)
            qs.append(_stack_heads(qrow[:, cols].astype(BF16)))
            kb.append(jnp.concatenate([kv(kp_ref, r, p), kv(kc_ref, r, p)], axis=0).astype(BF16))
            vb.append(jnp.concatenate([kv(vp_ref, r, p), kv(vc_ref, r, p)], axis=0).astype(BF16))
            sk.append(_per_head_col(jnp.broadcast_to(sk_ref[:, cols], (ATT_BLOCK, LANES))))
    return jnp.stack(qs), jnp.stack(kb), jnp.stack(vb), jnp.stack(sk)


_BDOT_NT = (((2,), (2,)), ((0,), (0,)))
_BDOT_NN = (((2,), (1,)), ((0,), (0,)))
_BDOT_TN = (((1,), (1,)), ((0,), (0,)))


def _att_fwd_call(q, k, v, sinks, d, max_dist, name):
    T, Wq = q.shape
    Wk = k.shape[1]
    nb = T // (d * ATT_BLOCK)
    cgw, sb, shared, qsp, csp, psp, qw, kw = _att_specs(d, Wq, Wk, nb, False)
    G = Wq // qw
    assert not shared or (d == 1 and Wk == LANES and G == 1 and cgw == 8), "a shared kv pair serves 2 x 8 query heads"

    def body(q_ref, kc_ref, kp_ref, vc_ref, vp_ref, sk_ref, o_ref, lse_ref):
        g, n = pl.program_id(0), pl.program_id(1)

        def work(rows):
            qs, kband, vband, sk = _att_problem_loads(rows, cgw, shared, g, q_ref, kc_ref, kp_ref, vc_ref, vp_ref, sk_ref)
            s = lax.dot_general(qs, kband, _BDOT_NT, preferred_element_type=F32) * (HEAD_DIM ** -0.5)
            s = jnp.where(_att_masks(n, max_dist), s, NEG)
            m = jnp.maximum(jnp.max(s, axis=-1, keepdims=True), sk)
            e = jnp.exp(s - m)
            den = jnp.sum(e, axis=-1, keepdims=True) + jnp.exp(sk - m)
            o = lax.dot_general((e * (1.0 / den)).astype(BF16), vband, _BDOT_NN, preferred_element_type=F32)
            lse = jnp.broadcast_to(m + jnp.log(den), o.shape)
            for i, r in enumerate(rows):
                o_ref[r, :] = jnp.concatenate([_unstack_heads(o[i * cgw + p]) for p in range(cgw)], axis=1)
                lse_ref[r, :] = jnp.concatenate([_unstack_heads(lse[i * cgw + p]) for p in range(cgw)], axis=1)

        _att_streams(d, sb, work)

    sksp = BS((1, qw), lambda g, n: (0, g))
    return _pcall(body, name=name, grid=(G, nb), in_specs=[qsp, csp, psp, csp, psp, sksp], out_specs=[qsp, qsp],
                  out_shape=[SDS((T, Wq), F32)] * 2,
                  compiler_params=_cparams(("parallel", "parallel")))(q, k, k, v, v, sinks)


def _att_bwd_call(q, k, v, sinks, o, lse, do, dlse, d, max_dist, name):
    T, Wq = q.shape
    Wk = k.shape[1]
    nb = T // (d * ATT_BLOCK)
    cgw, sb, shared, qsp, csp, psp, qw, kw = _att_specs(d, Wq, Wk, nb, True)
    G = Wq // qw
    scale = HEAD_DIM ** -0.5

    def body(q_ref, kc_ref, kp_ref, vc_ref, vp_ref, sk_ref, o_ref, lse_ref, do_ref, dlse_ref,
             dq_ref, dk_ref, dv_ref, dsk_ref, ck, cv):
        g, n = pl.program_id(0), pl.program_id(1)

        @pl.when(n == 0)
        def _():
            ck[...] = jnp.zeros_like(ck)
            cv[...] = jnp.zeros_like(cv)
            dsk_ref[...] = jnp.zeros_like(dsk_ref)

        def work(rows):
            qs, kband, vband, sk = _att_problem_loads(rows, cgw, shared, g, q_ref, kc_ref, kp_ref, vc_ref, vp_ref, sk_ref)
            dos, lse_c, corr = [], [], []
            for r in rows:
                do_r, o_r, lse_r, dlse_r = do_ref[r, :], o_ref[r, :], lse_ref[r, :], dlse_ref[r, :]
                for p in range(cgw):
                    cols = slice(p * LANES, (p + 1) * LANES)
                    dos.append(_stack_heads(do_r[:, cols].astype(BF16)))
                    lse_c.append(_per_head_col(lse_r[:, cols]))
                    corr.append(_head_sums(dlse_r[:, cols]) - _head_sums(do_r[:, cols] * o_r[:, cols]))
            dos, lse_c, corr = jnp.stack(dos), jnp.stack(lse_c), jnp.stack(corr)
            s = lax.dot_general(qs, kband, _BDOT_NT, preferred_element_type=F32) * scale
            pr = jnp.exp(jnp.where(_att_masks(n, max_dist), s, NEG) - lse_c)
            dp = lax.dot_general(dos, vband, _BDOT_NT, preferred_element_type=F32)
            ds = (pr * (dp + corr)).astype(BF16)
            dq = lax.dot_general(ds, kband, _BDOT_NN, preferred_element_type=F32) * scale
            dkb = lax.dot_general(ds, qs, _BDOT_TN, preferred_element_type=F32) * scale
            dvb = lax.dot_general(pr.astype(BF16), dos, _BDOT_TN, preferred_element_type=F32)
            dsk = jnp.exp(sk - lse_c) * corr
            lane = lax.broadcasted_iota(jnp.int32, (8, LANES), 1)
            for p in range(cgw):
                tot = [jnp.sum(jnp.stack([dsk[i * cgw + p, h * ATT_BLOCK:(h + 1) * ATT_BLOCK] for i in range(len(rows))]),
                               axis=(0, 1)).reshape(1, 1) for h in range(2)]
                dsk_ref[:, p * LANES:(p + 1) * LANES] += jnp.where(lane == 0, tot[0], jnp.where(lane == HEAD_DIM, tot[1], 0.0))

            def gather_pairs(parts):
                if not shared:
                    return jnp.concatenate(parts, axis=1)
                tot = [parts[4 * h] + parts[4 * h + 1] + parts[4 * h + 2] + parts[4 * h + 3] for h in range(2)]
                tot = [t + pltpu.roll(t, HEAD_DIM, axis=1) for t in tot]
                return jnp.where(_lo_lanes(ATT_BLOCK), tot[0], tot[1])

            for i, r in enumerate(rows):
                mine = range(i * cgw, (i + 1) * cgw)
                dq_ref[r, :] = jnp.concatenate([_unstack_heads(dq[b]) for b in mine], axis=1)
                dk_ref[r, :] = ck[r, :] + gather_pairs([dkb[b, :ATT_BLOCK] for b in mine])
                dv_ref[r, :] = cv[r, :] + gather_pairs([dvb[b, :ATT_BLOCK] for b in mine])
                ck[r, :] = gather_pairs([dkb[b, ATT_BLOCK:] for b in mine])
                cv[r, :] = gather_pairs([dvb[b, ATT_BLOCK:] for b in mine])

        @pl.when(n < nb)
        def _():
            _att_streams(d, sb, work)

        @pl.when(n == nb)
        def _():
            dk_ref[...] = ck[...]
            dv_ref[...] = cv[...]

    sksp = BS((1, qw), lambda g, n: (0, g))
    rows = ATT_BLOCK * d
    osp = BS((rows, kw), lambda g, n: (jnp.maximum(n - 1, 0), 0 if shared else g))
    kshape = SDS((T, Wk), F32)
    dq, dk, dv, dsk = _pcall(
        body, name=name, grid=(G, nb + 1),
        in_specs=[qsp, csp, psp, csp, psp, sksp, qsp, qsp, qsp, qsp],
        out_specs=[qsp, osp, osp, BS((8, qw), lambda g, n: (0, g))],
        out_shape=[SDS((T, Wq), F32), kshape, kshape, SDS((8, Wq), F32)],
        scratch_shapes=[pltpu.VMEM((rows, kw), F32)] * 2,
        compiler_params=_cparams(("parallel", "arbitrary")))(q, k, k, v, v, sinks, o, lse, do, dlse)
    return dq, dk, dv, dsk[0:1]


@functools.partial(jax.custom_vjp, nondiff_argnums=(4, 5, 6))
def band_attention(q, k, v, sinks, d, max_dist, name):
    return _att_fwd_call(q, k, v, sinks, d, max_dist, name)


def _band_attention_fwd(q, k, v, sinks, d, max_dist, name):
    o, lse = _att_fwd_call(q, k, v, sinks, d, max_dist, name)
    return (o, lse), (q, k, v, sinks, o, lse)


def _band_attention_bwd(d, max_dist, name, res, g):
    q, k, v, sinks, o, lse = res
    do, dlse = g
    return _att_bwd_call(q, k, v, sinks, o, lse, do, dlse, d, max_dist, name + "_bwd")


band_attention.defvjp(_band_attention_fwd, _band_attention_bwd)


def _merge_weights(ls):
    mx = jnp.maximum(jnp.maximum(ls[0], ls[1]), ls[2])
    es = [jnp.exp(l - mx) for l in ls]
    inv = 1.0 / (es[0] + es[1] + es[2])
    return [e * inv for e in es]


def _merge_fwd_call(os_, ls_, name):
    T, W = os_[0].shape
    tt = _tile(T, 512, 8)

    def body(o1, o2, o3, l1, l2, l3, out_ref):
        w = _merge_weights([l1[...], l2[...], l3[...]])
        out_ref[...] = w[0] * o1[...] + w[1] * o2[...] + w[2] * o3[...]

    row = BS((tt, W), lambda i: (i, 0))
    return _pcall(body, name=name, grid=(T // tt,), in_specs=[row] * 6, out_specs=row,
                  out_shape=SDS((T, W), F32), compiler_params=_cparams(("parallel",)))(*os_, *ls_)


def _merge_bwd_call(os_, ls_, do, name):
    T, W = os_[0].shape
    tt = _tile(T, 512, 8)

    def body(o1, o2, o3, l1, l2, l3, do_ref, d1, d2, d3, e1, e2, e3):
        w = _merge_weights([l1[...], l2[...], l3[...]])
        dov = do_ref[...]
        ts = [dov * o[...] for o in (o1, o2, o3)]
        mean = w[0] * ts[0] + w[1] * ts[1] + w[2] * ts[2]
        for wi, ti, dref, eref in zip(w, ts, (d1, d2, d3), (e1, e2, e3)):
            dref[...] = wi * dov
            eref[...] = wi * (ti - mean)

    row = BS((tt, W), lambda i: (i, 0))
    return _pcall(body, name=name, grid=(T // tt,), in_specs=[row] * 7, out_specs=[row] * 6,
                  out_shape=[SDS((T, W), F32)] * 6, compiler_params=_cparams(("parallel",)))(*os_, *ls_, do)


@functools.partial(jax.custom_vjp, nondiff_argnums=(2,))
def merge3(os_, ls_, name):
    return _merge_fwd_call(os_, ls_, name)


def _merge3_fwd(os_, ls_, name):
    return _merge_fwd_call(os_, ls_, name), (os_, ls_)


def _merge3_bwd(name, res, do):
    os_, ls_ = res
    out = _merge_bwd_call(os_, ls_, do, name + "_bwd")
    return tuple(out[:3]), tuple(out[3:])


merge3.defvjp(_merge3_fwd, _merge3_bwd)


def _xa_probs(qb, kb, scale):
    s = lax.dot_general(qb, kb, (((1,), (1,)), ((), ())), preferred_element_type=F32) * scale
    e = jnp.exp(s - jnp.max(s, axis=-1, keepdims=True))
    return e / jnp.sum(e, axis=-1, keepdims=True)


def _xa_fwd_call(q, kv, name):
    T, W = q.shape
    M = kv.shape[0]
    hd = XA_HEAD_DIM
    tq = _tile(T, 512, 8)
    scale = hd ** -0.5

    def body(q_ref, k_ref, v_ref, o_ref):
        p = _xa_probs(q_ref[...].astype(BF16), k_ref[...].astype(BF16), scale)
        o_ref[...] = jnp.dot(p.astype(BF16), v_ref[...].astype(BF16), preferred_element_type=F32)

    qsp = BS((tq, hd), lambda i, h: (i, h))
    return _pcall(body, name=name, grid=(T // tq, XA_HEADS),
                  in_specs=[qsp, BS((M, hd), lambda i, h: (0, h)), BS((M, hd), lambda i, h: (0, XA_HEADS + h))],
                  out_specs=qsp, out_shape=SDS((T, W), F32),
                  compiler_params=_cparams(("parallel", "parallel")))(q, kv, kv)


def _xa_bwd_call(q, kv, do, name):
    T, W = q.shape
    M = kv.shape[0]
    hd = XA_HEAD_DIM
    tq = _tile(T, 512, 8)
    scale = hd ** -0.5
    dn_nt = (((1,), (1,)), ((), ()))
    dn_tn = (((0,), (0,)), ((), ()))

    def body(q_ref, k_ref, v_ref, do_ref, dq_ref, dk_ref, dv_ref):
        @pl.when(pl.program_id(1) == 0)
        def _():
            dk_ref[...] = jnp.zeros_like(dk_ref)
            dv_ref[...] = jnp.zeros_like(dv_ref)

        qb, kb, vb = q_ref[...].astype(BF16), k_ref[...].astype(BF16), v_ref[...].astype(BF16)
        p = _xa_probs(qb, kb, scale)
        dob = do_ref[...].astype(BF16)
        dp = lax.dot_general(dob, vb, dn_nt, preferred_element_type=F32)
        ds = (p * (dp - jnp.sum(p * dp, axis=-1, keepdims=True))).astype(BF16)
        dq_ref[...] = jnp.dot(ds, kb, preferred_element_type=F32) * scale
        dk_ref[...] += lax.dot_general(ds, qb, dn_tn, preferred_element_type=F32) * scale
        dv_ref[...] += lax.dot_general(p.astype(BF16), dob, dn_tn, preferred_element_type=F32)

    qsp = BS((tq, hd), lambda h, i: (i, h))
    ksp = BS((M, hd), lambda h, i: (0, h))
    return _pcall(body, name=name, grid=(XA_HEADS, T // tq),
                  in_specs=[qsp, ksp, BS((M, hd), lambda h, i: (0, XA_HEADS + h)), qsp],
                  out_specs=[qsp, ksp, ksp], out_shape=[SDS((T, W), F32), SDS((M, W), F32), SDS((M, W), F32)],
                  compiler_params=_cparams(("parallel", "arbitrary")))(q, kv, kv, do)


@functools.partial(jax.custom_vjp, nondiff_argnums=(2,))
def cross_attention(q, kv, name):
    return _xa_fwd_call(q, kv, name)


def _cross_attention_fwd(q, kv, name):
    return _xa_fwd_call(q, kv, name), (q, kv)


def _cross_attention_bwd(name, res, do):
    q, kv = res
    dq, dk, dv = _xa_bwd_call(q, kv, do, name + "_bwd")
    return dq, jnp.concatenate([dk, dv], axis=1)


cross_attention.defvjp(_cross_attention_fwd, _cross_attention_bwd)


def _gate_up_swiglu_call(hn, w1t, name):
    T, K = hn.shape
    F = w1t.shape[0] // 2
    tm, tn = _tile(T, 1024), _tile(F, 256)
    nj = F // tn
    dn = (((1,), (1,)), ((), ()))

    def body(a_ref, wg_ref, wu_ref, g_ref, u_ref, act_ref):
        a = a_ref[...]
        g = lax.dot_general(a, wg_ref[...], dn, preferred_element_type=F32)
        u = lax.dot_general(a, wu_ref[...], dn, preferred_element_type=F32)
        g_ref[...] = g
        u_ref[...] = u
        act_ref[...] = ((g * _sigmoid(g)) * u).astype(BF16)

    tile = BS((tm, tn), lambda i, j: (i, j))
    return _pcall(body, name=name, grid=(T // tm, nj),
                  in_specs=[BS((tm, K), lambda i, j: (i, 0)), BS((tn, K), lambda i, j: (j, 0)),
                            BS((tn, K), lambda i, j: (j + nj, 0))],
                  out_specs=[tile, tile, tile], out_shape=[SDS((T, F), F32), SDS((T, F), F32), SDS((T, F), BF16)],
                  compiler_params=_cparams(("parallel", "parallel")))(hn, w1t, w1t)


def _swiglu_bwd_call(g, u, dact, name):
    T, F = g.shape
    tt = _tile(T, 128, 16)

    def body(g_ref, u_ref, d_ref, o_ref):
        g, d = g_ref[...], d_ref[...]
        sg = _sigmoid(g)
        o_ref[:, :F] = (d * u_ref[...] * (sg * (1.0 + g * (1.0 - sg)))).astype(BF16)
        o_ref[:, F:] = (d * (g * sg)).astype(BF16)

    row = BS((tt, F), lambda i: (i, 0))
    return _pcall(body, name=name, grid=(T // tt,), in_specs=[row, row, row],
                  out_specs=BS((tt, 2 * F), lambda i: (i, 0)), out_shape=SDS((T, 2 * F), BF16),
                  compiler_params=_cparams(("parallel",)))(g, u, dact)


@functools.partial(jax.custom_vjp, nondiff_argnums=(6,))
def ffn_block(h, g, w1b, w1c, w2b, w2c, name):
    return _ffn_fwd(h, g, w1b, w1c, w2b, w2c, name)[0]


def _ffn_fwd(h, g, w1b, w1c, w2b, w2c, name):
    hn = _rms_fwd_call(h, g, name + "_norm", BF16)
    gate, up, act = _gate_up_swiglu_call(hn, w1b, name + "_gu")
    out = _mm(act, w2b, add=h, name=name + "_down", tm=1024, tn=1024, tk=1408)
    return out, (h, g, hn, gate, up, act, w1b, w2b)


def _ffn_bwd(name, res, dout):
    h, g, hn, gate, up, act, w1b, w2b = res
    dact = _mm(dout, w2b, tb=True, name=name + "_down_da", tm=1024, tn=1408, tk=1024)
    dw2 = _mm(act, dout, ta=True, name=name + "_down_dw", tm=1408, tn=1024, tk=1024)
    dgu = _swiglu_bwd_call(gate, up, dact, name + "_swiglu_bwd")
    dw1 = _mm(dgu, hn, ta=True, name=name + "_gu_dw", tm=1408, tn=1024, tk=1024)
    dh, dg = _mm_rms_bwd(dgu, w1b, h, g, dout, name + "_gu_da", wt=True)
    return dh, dg.reshape(g.shape), jnp.zeros_like(w1b), dw1, jnp.zeros_like(w2b), dw2


ffn_block.defvjp(_ffn_fwd, _ffn_bwd)


def _final_call(h, g, target, name):
    T, Dm = h.shape
    tt = _tile(T, 512, 8)

    def body(x_ref, g_ref, t_ref, loss_ref, dx_ref, dg_ref):
        @pl.when(pl.program_id(0) == 0)
        def _():
            loss_ref[...] = jnp.zeros_like(loss_ref)
            dg_ref[...] = jnp.zeros_like(dg_ref)

        xv, gv = x_ref[...], g_ref[...]
        r = lax.rsqrt(jnp.mean(xv * xv, axis=-1, keepdims=True) + NORM_EPS)
        xh = xv * r
        err = xh * gv - t_ref[...]
        loss_ref[...] += 0.5 * jnp.sum(jnp.mean(err * err, axis=-1, keepdims=True), axis=0, keepdims=True)
        dy = err * (1.0 / Dm)
        dyg = dy * gv
        dx_ref[...] = r * (dyg - xh * jnp.mean(dyg * xh, axis=-1, keepdims=True))
        dg_ref[...] += jnp.sum(dy * xh, axis=0, keepdims=True)

    row = BS((tt, Dm), lambda i: (i, 0))
    vec = BS((1, Dm), lambda i: (0, 0))
    return _pcall(body, name=name, grid=(T // tt,), in_specs=[row, vec, row],
                  out_specs=[BS((1, 1), lambda i: (0, 0)), row, vec],
                  out_shape=[SDS((1, 1), F32), SDS((T, Dm), F32), SDS((1, Dm), F32)],
                  compiler_params=_cparams(("arbitrary",)))(h, g.reshape(1, Dm), target)


ADAMW_BLOCK_ELEMS = 64 * 1024


def _adamw_call(parts, w, m, v, name):
    shape = w.shape
    if not isinstance(parts, (list, tuple)):
        parts, shape3 = [parts], (1,) + shape
    else:
        shape3 = shape
    n_lead = shape3[0]
    r, N = shape3[-2], shape3[-1]
    Ld = math.prod(shape3[1:-2])
    w, m, v = (t.reshape(n_lead * Ld, r, N) for t in (w, m, v))
    tr = _tile(r, max(8, ADAMW_BLOCK_ELEMS // N), 8)
    c1 = 1.0 - ADAM_B1 ** ADAM_STEP
    c2 = 1.0 - ADAM_B2 ** ADAM_STEP
    outs = None
    for lead, p in enumerate(parts):
        def body(p_ref, w_ref, m_ref, v_ref, *rest):
            g_ref, d_ref, nm_ref, nv_ref = rest[-4:]
            g = p_ref[0]
            for j in range(1, N_DEV):
                g = g + p_ref[j]
            nm = ADAM_B1 * m_ref[...] + (1.0 - ADAM_B1) * g
            nv = ADAM_B2 * v_ref[...] + (1.0 - ADAM_B2) * (g * g)
            g_ref[...] = g
            nm_ref[...] = nm
            nv_ref[...] = nv
            d_ref[...] = -ADAM_LR * ((nm / c1) / (jnp.sqrt(nv / c2) + ADAM_EPS) + ADAM_WD * w_ref[...])

        base = lead * Ld
        row = BS((1, tr, N), lambda l, i, base=base: (base + l, i, 0))
        prev = [] if outs is None else list(outs)
        outs = _pcall(body, name=f"{name}_{lead}", grid=(Ld, r // tr),
                      in_specs=[BS((N_DEV, 1, tr, N), lambda l, i: (0, l, i, 0)), row, row, row]
                      + [BS(memory_space=pl.ANY)] * len(prev),
                      out_specs=[row] * 4, out_shape=[SDS((n_lead * Ld, r, N), F32)] * 4,
                      input_output_aliases={4 + j: j for j in range(len(prev))},
                      compiler_params=_cparams(("parallel", "parallel")))(p.reshape(N_DEV, Ld, r, N), w, m, v, *prev)
    return [t.reshape(shape) for t in outs]


def _place():
    return lax.axis_index("x"), lax.axis_index("y"), lax.axis_index("c")


def _all_gather(xs, name):
    n = len(xs)
    pairs = [(i, l) for i, x in enumerate(xs) for l in range(x.shape[0])]

    def body(*refs):
        x_refs, o_refs = refs[:n], refs[n:2 * n]
        send_sems, recv_sems, local_sems = refs[2 * n:]
        x_, y_, c_ = _place()
        me, sibling = (x_, y_, c_), (x_, y_, 1 - c_)
        chips = [(1 - x_, y_), (x_, 1 - y_), (1 - x_, 1 - y_)]

        def copy(e, k, block, to, from_input=False):
            i, l = pairs[e]
            px, py, pc = block
            dst = o_refs[i].at[l, 4 * px + 2 * py + pc]
            return pltpu.make_async_remote_copy(
                src_ref=x_refs[i].at[l] if from_input else dst, dst_ref=dst,
                send_sem=send_sems.at[7 * e + k], recv_sem=recv_sems.at[7 * e + k],
                device_id=to, device_id_type=pl.DeviceIdType.MESH)

        every = range(len(pairs))
        mine = [pltpu.make_async_copy(x_refs[i].at[l], o_refs[i].at[l, 4 * x_ + 2 * y_ + c_], local_sems.at[e])
                for e, (i, l) in enumerate(pairs)]
        for cp in mine:
            cp.start()
        first = [copy(e, 0, me, sibling, True) for e in every]
        first += [copy(e, 1 + j, me, (*chip, c_), True) for j, chip in enumerate(chips) for e in every]
        for cp in first:
            cp.start()
        passed = []
        for j, chip in enumerate(chips):
            for e in every:
                copy(e, 1 + j, (*chip, c_), me).wait_recv()
            for e in every:
                cp = copy(e, 4 + j, (*chip, c_), sibling)
                cp.start()
                passed.append(cp)
        for e in every:
            copy(e, 0, sibling, me).wait_recv()
        for j, chip in enumerate(chips):
            for e in every:
                copy(e, 4 + j, (*chip, 1 - c_), me).wait_recv()
        for cp in first + passed:
            cp.wait_send()
        for cp in mine:
            cp.wait()

    any_spec = BS(memory_space=pl.ANY)
    return _pcall(body, name=name, in_specs=[any_spec] * n, out_specs=[any_spec] * n,
                  out_shape=[SDS((x.shape[0], N_DEV) + x.shape[1:], x.dtype) for x in xs],
                  scratch_shapes=[pltpu.SemaphoreType.DMA((7 * len(pairs),)), pltpu.SemaphoreType.DMA((7 * len(pairs),)),
                                  pltpu.SemaphoreType.DMA((len(pairs),))],
                  compiler_params=pltpu.CompilerParams(has_side_effects=True))(*xs)


def _peer_of(k, place):
    x_, y_, c_ = place
    fx, fy, fc = (k >> 2) & 1, (k >> 1) & 1, k & 1
    return fx + x_ - 2 * fx * x_, fy + y_ - 2 * fy * y_, fc + c_ - 2 * fc * c_


def _split_copy(src_ref, land_ref, send_sems, recv_sems, e, k, place, scatter):
    x_, y_, c_ = place
    px, py, pc = _peer_of(k, place)
    return pltpu.make_async_remote_copy(
        src_ref=src_ref.at[4 * px + 2 * py + pc] if scatter else src_ref, dst_ref=land_ref.at[4 * x_ + 2 * y_ + c_],
        send_sem=send_sems.at[7 * e + k - 1], recv_sem=recv_sems.at[7 * e + k - 1],
        device_id=(px, py, pc), device_id_type=pl.DeviceIdType.MESH)


_HBM_SPEC = BS(memory_space=pltpu.HBM)
_SEM_SPEC = BS(memory_space=pltpu.SEMAPHORE)
_EFFECT = pltpu.SideEffectType.DATAFLOW_SIDE_EFFECTING


def _copies_start(srcs, scatter, name, thru=None):
    n = len(srcs)
    lands = [lax.empty(s.shape if scatter else (N_DEV,) + s.shape, s.dtype) for s in srcs]
    passed = srcs + lands + list(thru or ())

    def body(*refs):
        src_refs, land_refs = refs[:n], refs[n:2 * n]
        send_sems, recv_sems = refs[len(passed)], refs[len(passed) + 1]
        token = refs[-1]
        place = _place()
        for e in range(n):
            for k in range(1, N_DEV):
                _split_copy(src_refs[e], land_refs[e], send_sems, recv_sems, e, k, place, scatter).start()
        token[...] = jnp.zeros_like(token)

    hbm = lambda t: pltpu.with_memory_space_constraint(t, pltpu.HBM)
    out = _pcall(body, name=name,
                 out_shape=(pltpu.SemaphoreType.DMA((7 * n,)), pltpu.SemaphoreType.DMA((7 * n,)),
                            *[pltpu.HBM(t.shape, t.dtype) for t in passed], SDS((8, LANES), F32)),
                 in_specs=[_HBM_SPEC] * len(passed),
                 out_specs=(_SEM_SPEC, _SEM_SPEC, *[_HBM_SPEC] * len(passed), BS(memory_space=pltpu.VMEM)),
                 input_output_aliases={i: 2 + i for i in range(len(passed))},
                 compiler_params=pltpu.CompilerParams(has_side_effects=_EFFECT))(*[hbm(t) for t in passed])
    return out[0], out[1], list(out[2:2 + n]), list(out[2 + n:2 + 2 * n]), list(out[2 + 2 * n:-1])


def _copies_wait(started, which, scatter, after, name):
    send_sems, recv_sems, srcs, lands, _ = started
    n = len(which)
    after = list(after) if isinstance(after, (list, tuple)) else [after]

    def body(*refs):
        src_refs, land_refs = refs[:n], refs[n:2 * n]
        send_s, recv_s = refs[2 * n], refs[2 * n + 1]
        place = _place()
        for j, e in enumerate(which):
            for k in range(1, N_DEV):
                cp = _split_copy(src_refs[j], land_refs[j], send_s, recv_s, e, k, place, scatter)
                cp.wait_send()
                cp.wait_recv()

    args = [srcs[e] for e in which] + [lands[e] for e in which]
    out = _pcall(body, name=name, out_shape=tuple(pltpu.HBM(t.shape, t.dtype) for t in args),
                 in_specs=[_HBM_SPEC] * (2 * n) + [_SEM_SPEC, _SEM_SPEC] + [BS(memory_space=pl.ANY)] * len(after),
                 out_specs=tuple([_HBM_SPEC] * (2 * n)), input_output_aliases={i: i for i in range(2 * n)},
                 compiler_params=pltpu.CompilerParams(has_side_effects=_EFFECT))(*args, send_sems, recv_sems, *after)
    return list(out[:n]), list(out[n:])


def _with_own_block(land, own_block):
    me = 4 * lax.axis_index("x") + 2 * lax.axis_index("y") + lax.axis_index("c")
    return lax.dynamic_update_index_in_dim(land, own_block, me, 0)


def _pad_flat(t, quantum=PACK_QUANTUM):
    f = t.reshape(-1)
    pad = (-f.shape[0]) % quantum
    return jnp.pad(f, (0, pad)) if pad else f


def _pack(arrs, dtype):
    return jnp.concatenate([_pad_flat(a.astype(dtype)) for a in arrs]).reshape(-1, LANES)


def _unpack(buf, shapes, lead=()):
    flat = buf.reshape(lead + (-1,))
    out, off = [], 0
    for s in shapes:
        n = math.prod(s)
        out.append(flat[..., off:off + n].reshape(lead + tuple(s)))
        off += n + (-n) % PACK_QUANTUM
    return out


def _full_from_gathered(g, axis):
    t = jnp.moveaxis(g, 0, axis)
    s = t.shape
    return t.reshape(s[:axis] + (s[axis] * s[axis + 1],) + s[axis + 2:])


def _parts_from_full(t, axis):
    s = t.shape
    t = t.reshape(s[:axis] + (N_DEV, s[axis] // N_DEV) + s[axis + 1:])
    return jnp.moveaxis(t, axis, 0)


def _head_rows(t):
    return jnp.repeat(t, HEAD_DIM).reshape(1, -1)


def _dilated_attention(q, k, v, name):
    no_sink = jnp.full((1, q.shape[1]), NEG, F32)
    outs, lses = zip(*[band_attention(q, k, v, no_sink, d, ATT_BLOCK, f"{name}_d{d}") for d in DILATIONS])
    return merge3(tuple(outs), tuple(lses), name + "_merge")


STAGES = (
    ("proj0", ('mix_norm',), (('ab_w_in', 0),)),
    ("mixer0", ('lru_conv_w', 'lru_conv_b', 'lru_ba', 'lru_bx', 'lru_lambda'),
     (('lru_wa', 0), ('lru_wx', 0), ('ab_w_out', 0))),
    ("xa0", ('xa_norm', 'xa_mem_norm'), (('xa_wq', 0), ('xa_wkv', 0), ('xa_wo', 0))),
    ("ffn0", ('ffn_norm',), (('ffn_w_gate_up', 0), ('ffn_w_down', 0))),
    ("mixer1", ('mix_norm', 'c_b_qkv', 'c_sinks', 'c_b_out'), (('c_w_qkv', 0), ('c_w_out', 0))),
    ("xa1", ('xa_norm', 'xa_mem_norm'), (('xa_wq', 1), ('xa_wkv', 1), ('xa_wo', 1))),
    ("ffn1", ('ffn_norm',), (('ffn_w_gate_up', 1), ('ffn_w_down', 1))),
)


def _stage_fn(stage, Wb, tabs, mem):
    layer = int(stage[-1])
    L = f"l{layer}"

    def run(S, Cw, h):
        def lin(a, key, bias, add, name, rows=None):
            wb, wc = Wb[key], Cw[key]
            if rows is not None:
                wb, wc = wb[rows], wc[rows]
            return linear(a, wb, wc, bias, add, name)

        def norm_lin(a, gain, key, bias, name):
            return norm_linear(a, gain, Wb[key], Cw[key], bias, key[0] in COLUMN_CUT, name)

        if stage == "mixer0":
            h, proj = h
            C = S['lru_conv_w'].shape[-1]
            xc = conv4(proj[:, :C], S['lru_conv_w'][0], S['lru_conv_b'][0], L + "_conv")
            ga, gx = lru_gates(xc, Wb['lru_wa', 0], Cw['lru_wa', 0], S['lru_ba'][0],
                               Wb['lru_wx', 0], Cw['lru_wx', 0], S['lru_bx'][0], L + "_gates")
            rec = lru_scan(xc, ga, gx, proj[:, C:2 * C], S['lru_lambda'][0], L + "_scan")
            bw = B_HEADS * HEAD_DIM
            q = rope(proj[:, 2 * C:2 * C + bw], tabs, L + "_rope_q")
            k = rope(proj[:, 2 * C + bw:2 * C + 2 * bw], tabs, L + "_rope_k")
            v = proj[:, 2 * C + 2 * bw:]
            att = _dilated_attention(q, k, v, L + "_att")
            h = lin(att, ('ab_w_out', 0), None, h, L + "_w_out_att", slice(C, None))
            return lin(rec, ('ab_w_out', 0), None, h, L + "_w_out_rec", slice(0, C))
        if stage == "mixer1":
            qw = C_HEADS * HEAD_DIM
            kw = C_KV_HEADS * HEAD_DIM
            qkv, h = norm_lin(h, S['mix_norm'][1], ('c_w_qkv', 0), S['c_b_qkv'][0], L + "_w_qkv")
            q = rope(qkv[:, :qw], tabs, L + "_rope_q")
            k = rope(qkv[:, qw:qw + kw], tabs, L + "_rope_k")
            v = qkv[:, qw + kw:]
            o, _ = band_attention(q, k, v, _head_rows(S['c_sinks'][0]), 1, ATT_BLOCK - 1, L + "_att")
            return lin(o, ('c_w_out', 0), S['c_b_out'][0], h, L + "_w_out")
        if stage.startswith("xa"):
            xq, h = norm_lin(h, S['xa_norm'][layer], ('xa_wq', layer), None, L + "_xa_wq")
            xkv, _ = norm_lin(mem, S['xa_mem_norm'][layer], ('xa_wkv', layer), None, L + "_xa_wkv")
            return lin(cross_attention(xq, xkv, L + "_xa"), ('xa_wo', layer), None, h, L + "_xa_wo")
        gu, down = ('ffn_w_gate_up', layer), ('ffn_w_down', layer)
        return ffn_block(h, S['ffn_norm'][layer], Wb[gu], Cw[gu], Wb[down], Cw[down], L + "_ffn")

    return run


def kernel(x, mem, mix_norm, ab_w_in, lru_conv_w, lru_conv_b, lru_wa, lru_ba, lru_wx, lru_bx, lru_lambda, ab_w_out, c_w_qkv, c_b_qkv, c_sinks, c_w_out, c_b_out, xa_norm, xa_mem_norm, xa_wq, xa_wkv, xa_wo, ffn_norm, ffn_w_gate_up, ffn_w_down, final_norm, loss_target, m_mix_norm, m_ab_w_in, m_lru_conv_w, m_lru_conv_b, m_lru_wa, m_lru_ba, m_lru_wx, m_lru_bx, m_lru_lambda, m_ab_w_out, m_c_w_qkv, m_c_b_qkv, m_c_sinks, m_c_w_out, m_c_b_out, m_xa_norm, m_xa_mem_norm, m_xa_wq, m_xa_wkv, m_xa_wo, m_ffn_norm, m_ffn_w_gate_up, m_ffn_w_down, m_final_norm, v_mix_norm, v_ab_w_in, v_lru_conv_w, v_lru_conv_b, v_lru_wa, v_lru_ba, v_lru_wx, v_lru_bx, v_lru_lambda, v_ab_w_out, v_c_w_qkv, v_c_b_qkv, v_c_sinks, v_c_w_out, v_c_b_out, v_xa_norm, v_xa_mem_norm, v_xa_wq, v_xa_wkv, v_xa_wo, v_ffn_norm, v_ffn_w_gate_up, v_ffn_w_down, v_final_norm):
    w_loc = dict(zip(WEIGHT_NAMES, (mix_norm, ab_w_in, lru_conv_w, lru_conv_b, lru_wa, lru_ba, lru_wx, lru_bx, lru_lambda, ab_w_out, c_w_qkv, c_b_qkv, c_sinks, c_w_out, c_b_out, xa_norm, xa_mem_norm, xa_wq, xa_wkv, xa_wo, ffn_norm, ffn_w_gate_up, ffn_w_down, final_norm)))
    m_loc = dict(zip(WEIGHT_NAMES, (m_mix_norm, m_ab_w_in, m_lru_conv_w, m_lru_conv_b, m_lru_wa, m_lru_ba, m_lru_wx, m_lru_bx, m_lru_lambda, m_ab_w_out, m_c_w_qkv, m_c_b_qkv, m_c_sinks, m_c_w_out, m_c_b_out, m_xa_norm, m_xa_mem_norm, m_xa_wq, m_xa_wkv, m_xa_wo, m_ffn_norm, m_ffn_w_gate_up, m_ffn_w_down, m_final_norm)))
    v_loc = dict(zip(WEIGHT_NAMES, (v_mix_norm, v_ab_w_in, v_lru_conv_w, v_lru_conv_b, v_lru_wa, v_lru_ba, v_lru_wx, v_lru_bx, v_lru_lambda, v_ab_w_out, v_c_w_qkv, v_c_b_qkv, v_c_sinks, v_c_w_out, v_c_b_out, v_xa_norm, v_xa_mem_norm, v_xa_wq, v_xa_wkv, v_xa_wo, v_ffn_norm, v_ffn_w_gate_up, v_ffn_w_down, v_final_norm)))

    me = 4 * lax.axis_index("x") + 2 * lax.axis_index("y") + lax.axis_index("c")

    first_keys = list(STAGES[0][2])
    keys = [key for _, _, stage_keys in STAGES[1:] for key in stage_keys]
    shards = [_shard_view(n, w_loc[n])[l].astype(BF16) for n, l in keys]
    first_g = _all_gather([_pack([w_loc[n] for n in SMALL], F32)[None]]
                          + [_shard_view(n, w_loc[n])[l].astype(BF16)[None] for n, l in first_keys], "gather_first")
    gather = _copies_start(shards, False, "gather_start", thru=[first_g[0]])
    small_g = gather[4][0][0]
    Wb = {key: _full_from_gathered(g[0], _layer_shard_axis(key[0])) for key, g in zip(first_keys, first_g[1:])}
    S = {n: w_loc[n] for n in REPLICATED}
    for n, t in zip(SMALL, _unpack(small_g, [w_loc[n].shape for n in SMALL], lead=(N_DEV,))):
        S[n] = _full_from_gathered(t, SHARD_AXIS[n])

    tabs = _rope_tables(x.shape[1])
    w_in = Wb[first_keys[0]]
    hn0 = _rms_fwd_call(x[0], S['mix_norm'][0], "l0_w_in_norm", BF16)
    h = (x[0], _mm(hn0, w_in, tb=True, name="l0_w_in", tm=1024, tn=1408, tk=1024))
    vjps = []
    for stage, small_names, stage_keys in STAGES[1:]:
        which = [keys.index(key) for key in stage_keys]
        _, lands = _copies_wait(gather, which, False, jax.tree.leaves(h)[-1], "gather_wait_" + stage)
        for e, land in zip(which, lands):
            Wb[keys[e]] = _full_from_gathered(_with_own_block(land, shards[e]), _layer_shard_axis(keys[e][0]))
        carriers = {key: jnp.zeros(Wb[key].shape, F32) for key in stage_keys}
        h, vjp_fn = jax.vjp(_stage_fn(stage, Wb, tabs, mem[0]), {n: S[n] for n in small_names}, carriers, h)
        vjps.append(vjp_fn)
    loss_part, dh, dg_final = _final_call(h, S['final_norm'], loss_target[0], "final_loss")

    grads = {'final_norm': dg_final.reshape(final_norm.shape)}
    exchanges, send_keys, send_parts = [], [], []

    def start_exchange(stage, dh):
        leaves, tree = jax.tree.flatten(dh)
        started = _copies_start(list(send_parts), True, "grad_start_" + stage, thru=leaves)
        exchanges.append((stage, started, list(send_keys)))
        send_keys.clear()
        send_parts.clear()
        return jax.tree.unflatten(tree, started[4])

    for (stage, small_names, stage_keys), vjp_fn in zip(reversed(STAGES[1:]), reversed(vjps)):
        g_small, g_big, dh = vjp_fn(dh)
        for n in small_names:
            grads[n] = grads[n] + g_small[n] if n in grads else g_small[n]
        send_keys += list(stage_keys)
        send_parts += [_parts_from_full(g_big[key], _layer_shard_axis(key[0])) for key in stage_keys]
        if stage == "xa1":
            continue
        if stage == "mixer0":
            small_parts = [_parts_from_full(grads[n], SHARD_AXIS[n]) for n in SMALL]
            send_keys.append("small")
            send_parts.append(jnp.stack([_pack([p[j] for p in small_parts], F32) for j in range(N_DEV)]))
        dh = start_exchange(stage, dh)
    d_res, d_proj = dh
    send_keys.append(first_keys[0])
    send_parts.append(_parts_from_full(_mm(d_proj, hn0, ta=True, name="l0_w_in_dw", tm=1408, tn=1024, tk=1024),
                                       _layer_shard_axis(first_keys[0][0])))
    d_res, d_proj = start_exchange("proj0", (d_res, d_proj))
    dx, dg0 = _mm_rms_bwd(d_proj, w_in, x[0], S['mix_norm'][0], d_res, "l0_w_in_da", wt=True)
    grads['mix_norm'] = grads['mix_norm'] + jnp.concatenate([dg0, jnp.zeros_like(dg0)], axis=0)
    rep_names = REPLICATED + ["loss"]
    grads["loss"] = loss_part
    zero = jnp.zeros((1, 1), F32)
    for d in (w_loc, m_loc, v_loc):
        d["loss"] = zero
    rep_started = _copies_start([_pack([grads[n] for n in rep_names], F32)], False, "rep_grads_start", thru=[dx])
    dx = rep_started[4][0]

    parts, out = {}, {}

    def end_exchange(stage, started, ex_keys, after):
        srcs, lands = _copies_wait(started, list(range(len(ex_keys))), True, after, "grad_wait_" + stage)
        for key, src, land in zip(ex_keys, srcs, lands):
            parts[key] = _with_own_block(land, lax.dynamic_index_in_dim(src, me, 0, keepdims=False))

    def adamw(p, names, call_name):
        if len(names) == 1:
            n = names[0]
            res = _adamw_call(p, _shard_view(n, w_loc[n]), _shard_view(n, m_loc[n]), _shard_view(n, v_loc[n]), call_name)
            for kind, t in zip(("grad", "delta", "new_m", "new_v"), res):
                out[kind, n] = _shard_view(n, t)
        else:
            res = _adamw_call(p, *[_pack([d[n] for n in names], F32) for d in (w_loc, m_loc, v_loc)], call_name)
            for kind, buf in zip(("grad", "delta", "new_m", "new_v"), res):
                for n, t in zip(names, _unpack(buf, [w_loc[n].shape for n in names])):
                    out[kind, n] = t

    for ex in exchanges[:-1]:
        end_exchange(*ex, dx)
    last_names = {key[0] for key in exchanges[-1][2]}
    for n in BIG:
        if n not in last_names:
            adamw([parts[n, l] for l in range(w_loc[n].shape[0])], [n], "adamw_" + n)
    adamw(parts["small"], SMALL, "adamw_small")
    end_exchange(*exchanges[-1], [out["new_v", n] for n in BIG if n not in last_names])
    for n in BIG:
        if n in last_names:
            adamw([parts[n, l] for l in range(w_loc[n].shape[0])], [n], "adamw_" + n)
    rep_src, rep_land = _copies_wait(rep_started, [0], False, out["new_v", "ab_w_in"], "rep_grads_wait")
    adamw(_with_own_block(rep_land[0], rep_src[0]), rep_names, "adamw_replicated")
    loss = out["grad", "loss"][0, 0]

    return (loss, dx[None], *[out[kind, n] for kind in ("grad", "delta", "new_m", "new_v") for n in WEIGHT_NAMES])
```

```python
import functools
import math

import jax
import jax.numpy as jnp
from jax import lax
from jax.experimental import pallas as pl
from jax.experimental.pallas import tpu as pltpu

F32 = jnp.float32
BF16 = jnp.bfloat16
SDS = jax.ShapeDtypeStruct
BS = pl.BlockSpec

N_DEV = 8
NORM_EPS = 1e-6
ROPE_THETA = 500000.0
HEAD_DIM = 64
ROT_DIM = 16
ATT_BLOCK = 128
LRU_C = 8.0
LRU_HEADS = 4
DILATIONS = (1, 4, 16)
B_HEADS = 8
C_HEADS = 16
C_KV_HEADS = 2
XA_HEADS = 4
XA_HEAD_DIM = 128
NEG = -1e30
ADAM_LR, ADAM_B1, ADAM_B2, ADAM_EPS, ADAM_WD, ADAM_STEP = 0.001, 0.9, 0.999, 1e-08, 0.01, 10
LANES = 128
VMEM_LIMIT = 48 * 1024 * 1024

WEIGHT_NAMES = ['mix_norm', 'ab_w_in', 'lru_conv_w', 'lru_conv_b', 'lru_wa', 'lru_ba', 'lru_wx', 'lru_bx',
                'lru_lambda', 'ab_w_out', 'c_w_qkv', 'c_b_qkv', 'c_sinks', 'c_w_out', 'c_b_out', 'xa_norm',
                'xa_mem_norm', 'xa_wq', 'xa_wkv', 'xa_wo', 'ffn_norm', 'ffn_w_gate_up', 'ffn_w_down', 'final_norm']
SHARD_AXIS = {'ab_w_in': 2, 'lru_conv_w': 2, 'lru_wa': 2, 'lru_ba': 2, 'lru_wx': 2, 'lru_bx': 2, 'ab_w_out': 1,
              'c_w_qkv': 2, 'c_b_qkv': 1, 'c_w_out': 1, 'c_b_out': 1, 'xa_wq': 1, 'xa_wkv': 1, 'xa_wo': 2,
              'ffn_w_gate_up': 2, 'ffn_w_down': 1}
BIG = ['ab_w_in', 'lru_wa', 'lru_wx', 'ab_w_out', 'c_w_qkv', 'c_w_out', 'xa_wq', 'xa_wkv', 'xa_wo',
       'ffn_w_gate_up', 'ffn_w_down']
SMALL = ['lru_conv_w', 'lru_ba', 'lru_bx', 'c_b_qkv', 'c_b_out']
SHARDED = BIG + SMALL
REPLICATED = [n for n in WEIGHT_NAMES if n not in SHARD_AXIS]
COLUMN_CUT = ('ab_w_in', 'c_w_qkv', 'ffn_w_gate_up')
PACK_QUANTUM = 2048


def _shard_view(name, t):
    return jnp.swapaxes(t, -1, -2) if name in COLUMN_CUT else t


def _layer_shard_axis(name):
    return 0 if name in COLUMN_CUT else SHARD_AXIS[name] - 1


def _pcall(body, **kw):
    return pl.pallas_call(body, **kw)


def _cparams(sem=None):
    return pltpu.CompilerParams(dimension_semantics=sem, vmem_limit_bytes=VMEM_LIMIT)


def _tile(n, target, mult=LANES):
    if n <= target:
        return n
    t = (target // mult) * mult
    while t >= mult:
        if n % t == 0:
            return t
        t -= mult
    return n


def _sigmoid(x):
    return 1.0 / (1.0 + jnp.exp(-x))


def _expm1(x):
    small = x * (1.0 + x * (0.5 + x * (1.0 / 6.0 + x * (1.0 / 24.0))))
    return jnp.where(jnp.abs(x) < 0.03, small, jnp.exp(x) - 1.0)


_GELU_C = math.sqrt(2.0 / math.pi)


def _gelu_parts(y):
    y2 = y * y
    th = jnp.tanh(_GELU_C * (y + 0.044715 * y * y2))
    g = 0.5 * y * (1.0 + th)
    dg = 0.5 * (1.0 + th) + 0.5 * y * (1.0 - th * th) * _GELU_C * (1.0 + 3.0 * 0.044715 * y2)
    return g, dg


def _mm(a, b, *, ta=False, tb=False, bias=None, add=None, name, tm=512, tn=512, tk=2048):
    M, K = (a.shape[1], a.shape[0]) if ta else a.shape
    N = b.shape[0] if tb else b.shape[1]
    tm, tn, tk = _tile(M, tm), _tile(N, tn), _tile(K, tk)
    nk = K // tk
    dn = (((0 if ta else 1,), (1 if tb else 0,)), ((), ()))

    def body(*refs):
        a_ref, b_ref = refs[0], refs[1]
        pos = 2
        bias_ref = add_ref = None
        if bias is not None:
            bias_ref = refs[pos]
            pos += 1
        if add is not None:
            add_ref = refs[pos]
            pos += 1
        o_ref = refs[pos]
        part = lax.dot_general(a_ref[...].astype(BF16), b_ref[...].astype(BF16), dn, preferred_element_type=F32)

        def finish(r):
            if bias_ref is not None:
                r = r + bias_ref[...]
            if add_ref is not None:
                r = r + add_ref[...]
            o_ref[...] = r

        if nk == 1:
            finish(part)
            return
        acc_ref = refs[pos + 1]
        k = pl.program_id(2)

        @pl.when(k == 0)
        def _():
            acc_ref[...] = part

        @pl.when((k > 0) & (k < nk - 1))
        def _():
            acc_ref[...] += part

        @pl.when(k == nk - 1)
        def _():
            finish(acc_ref[...] + part)

    in_specs = [BS((tk, tm), lambda i, j, k: (k, i)) if ta else BS((tm, tk), lambda i, j, k: (i, k)),
                BS((tn, tk), lambda i, j, k: (j, k)) if tb else BS((tk, tn), lambda i, j, k: (k, j))]
    args = [a, b]
    if bias is not None:
        in_specs.append(BS((1, tn), lambda i, j, k: (0, j)))
        args.append(bias.reshape(1, N))
    if add is not None:
        in_specs.append(BS((tm, tn), lambda i, j, k: (i, j)))
        args.append(add)
    return _pcall(body, name=name, grid=(M // tm, N // tn, nk), in_specs=in_specs,
                  out_specs=BS((tm, tn), lambda i, j, k: (i, j)), out_shape=SDS((M, N), F32),
                  scratch_shapes=[pltpu.VMEM((tm, tn), F32)] if nk > 1 else [],
                  compiler_params=_cparams(("parallel", "parallel", "arbitrary")))(*args)


def _mm_rms_bwd(dy, wb, x, g, add, name, wt=False):
    T, Kc = dy.shape
    Dm = wb.shape[1] if wt else wb.shape[0]
    tm, tk = _tile(T, 1024), _tile(Kc, 1408)
    nk = Kc // tk
    dn = (((1,), (0 if wt else 1,)), ((), ()))

    def body(*refs):
        dy_ref, w_ref, x_ref, g_ref = refs[:4]
        dx_ref, dg_ref, acc_ref = refs[-3:]
        i, k = pl.program_id(0), pl.program_id(1)
        part = lax.dot_general(dy_ref[...].astype(BF16), w_ref[...], dn, preferred_element_type=F32)

        @pl.when(k == 0)
        def _():
            acc_ref[...] = part

        @pl.when((k > 0) & (k < nk - 1))
        def _():
            acc_ref[...] += part

        @pl.when((i == 0) & (k == 0))
        def _():
            dg_ref[...] = jnp.zeros_like(dg_ref)

        @pl.when(k == nk - 1)
        def _():
            d = part if nk == 1 else acc_ref[...] + part
            xv = x_ref[...]
            r = lax.rsqrt(jnp.mean(xv * xv, axis=-1, keepdims=True) + NORM_EPS)
            xh = xv * r
            dyg = d * g_ref[...]
            dx = r * (dyg - xh * jnp.mean(dyg * xh, axis=-1, keepdims=True))
            dx_ref[...] = dx if add is None else dx + refs[4][...]
            dg_ref[...] += jnp.sum(d * xh, axis=0, keepdims=True)

    row = BS((tm, Dm), lambda i, k: (i, 0))
    vec = BS((1, Dm), lambda i, k: (0, 0))
    extra = [] if add is None else [add]
    return _pcall(body, name=name, grid=(T // tm, nk),
                  in_specs=[BS((tm, tk), lambda i, k: (i, k)),
                            BS((tk, Dm), lambda i, k: (k, 0)) if wt else BS((Dm, tk), lambda i, k: (0, k)), row, vec]
                  + [row] * len(extra),
                  out_specs=[row, vec], out_shape=[SDS((T, Dm), F32), SDS((1, Dm), F32)],
                  scratch_shapes=[pltpu.VMEM((tm, Dm), F32)],
                  compiler_params=_cparams(("arbitrary", "arbitrary")))(dy, wb, x, g.reshape(1, Dm), *extra)


def _colsum(x, name):
    T, N = x.shape
    tt = _tile(T, 512, 8)

    def body(x_ref, o_ref):
        @pl.when(pl.program_id(0) == 0)
        def _():
            o_ref[...] = jnp.zeros_like(o_ref)

        o_ref[...] += jnp.sum(x_ref[...], axis=0, keepdims=True)

    return _pcall(body, name=name, grid=(T // tt,), in_specs=[BS((tt, N), lambda i: (i, 0))],
                  out_specs=BS((1, N), lambda i: (0, 0)), out_shape=SDS((1, N), F32),
                  compiler_params=_cparams(("arbitrary",)))(x)


@functools.partial(jax.custom_vjp, nondiff_argnums=(5,))
def linear(a, wb, wc, bias, add, name):
    return _mm(a, wb, bias=bias, add=add, name=name)


def _linear_fwd(a, wb, wc, bias, add, name):
    return _mm(a, wb, bias=bias, add=add, name=name), (a, wb, bias is not None, add is not None)


def _linear_bwd(name, res, g):
    a, wb, has_bias, has_add = res
    da = _mm(g, wb, tb=True, name=name + "_da")
    dw = _mm(a, g, ta=True, name=name + "_dw")
    dbias = _colsum(g, name + "_db").reshape(-1) if has_bias else None
    return da, jnp.zeros_like(wb), dw, dbias, (g if has_add else None)


linear.defvjp(_linear_fwd, _linear_bwd)


def _rms_fwd_call(x, g, name, out_dtype=F32):
    T, Dm = x.shape
    tt = _tile(T, 512, 16)

    def body(x_ref, g_ref, o_ref):
        xv = x_ref[...]
        r = lax.rsqrt(jnp.mean(xv * xv, axis=-1, keepdims=True) + NORM_EPS)
        o_ref[...] = ((xv * r) * g_ref[...]).astype(out_dtype)

    return _pcall(body, name=name, grid=(T // tt,),
                  in_specs=[BS((tt, Dm), lambda i: (i, 0)), BS((1, Dm), lambda i: (0, 0))],
                  out_specs=BS((tt, Dm), lambda i: (i, 0)), out_shape=SDS((T, Dm), out_dtype),
                  compiler_params=_cparams(("parallel",)))(x, g.reshape(1, Dm))


@functools.partial(jax.custom_vjp, nondiff_argnums=(5, 6))
def norm_linear(x, g, wb, wc, bias, wt, name):
    return _norm_linear_fwd(x, g, wb, wc, bias, wt, name)[0]


def _norm_linear_fwd(x, g, wb, wc, bias, wt, name):
    hn = _rms_fwd_call(x, g, name + "_norm", BF16)
    return (_mm(hn, wb, tb=wt, bias=bias, name=name, tm=1024, tn=1408, tk=1024), x), (x, g, hn, wb, bias is not None)


def _norm_linear_bwd(wt, name, res, cts):
    x, g, hn, wb, has_bias = res
    dy, dres = cts
    if wt:
        dw = _mm(dy, hn, ta=True, name=name + "_dw", tm=1408, tn=1024, tk=1024)
    else:
        dw = _mm(hn, dy, ta=True, name=name + "_dw", tm=1024, tn=1408, tk=1024)
    dx, dg = _mm_rms_bwd(dy, wb, x, g, dres, name + "_da", wt)
    dbias = _colsum(dy, name + "_db").reshape(-1) if has_bias else None
    return dx, dg.reshape(g.shape), jnp.zeros_like(wb), dw, dbias


norm_linear.defvjp(_norm_linear_fwd, _norm_linear_bwd)


def _rope_tables(T):
    half = ROT_DIM // 2
    inv = ROPE_THETA ** (-jnp.arange(0, ROT_DIM, 2, dtype=F32) / ROT_DIM)
    lane = jnp.arange(LANES) % HEAD_DIM
    freq = jnp.where(lane < ROT_DIM, inv[lane % half], 0.0)
    ang = jnp.arange(T, dtype=F32)[:, None] * freq[None, :]
    c, s = jnp.cos(ang), jnp.sin(ang)
    sa = jnp.where((lane >= half) & (lane < ROT_DIM), s, 0.0)
    sb = jnp.where(lane < half, -s, 0.0)
    return c, sa, sb


def _rope_call(x, tabs, inverse, name):
    T, W = x.shape
    tt = _tile(T, 512, 8)
    reps = W // LANES
    half = ROT_DIM // 2

    def body(x_ref, c_ref, sa_ref, sb_ref, o_ref):
        xv = x_ref[...]
        c = jnp.tile(c_ref[...], (1, reps))
        sa = jnp.tile(sa_ref[...], (1, reps))
        sb = jnp.tile(sb_ref[...], (1, reps))
        if not inverse:
            o_ref[...] = xv * c + pltpu.roll(xv, half, axis=1) * sa + pltpu.roll(xv, W - half, axis=1) * sb
        else:
            o_ref[...] = xv * c + pltpu.roll(xv * sa, W - half, axis=1) + pltpu.roll(xv * sb, half, axis=1)

    row = BS((tt, W), lambda i: (i, 0))
    tab = BS((tt, LANES), lambda i: (i, 0))
    return _pcall(body, name=name, grid=(T // tt,), in_specs=[row, tab, tab, tab], out_specs=row,
                  out_shape=SDS((T, W), F32), compiler_params=_cparams(("parallel",)))(x, *tabs)


@functools.partial(jax.custom_vjp, nondiff_argnums=(2,))
def rope(x, tabs, name):
    return _rope_call(x, tabs, False, name)


def _rope_fwd(x, tabs, name):
    return _rope_call(x, tabs, False, name), tabs


def _rope_bwd(name, tabs, dy):
    return _rope_call(dy, tabs, True, name + "_bwd"), jax.tree.map(jnp.zeros_like, tabs)


rope.defvjp(_rope_fwd, _rope_bwd)


def _conv_fwd_call(x, w, b, name):
    T, C = x.shape
    tt = _tile(T, 512, 8)
    per = tt // 8

    def body(x_ref, halo_ref, w_ref, b_ref, o_ref):
        i = pl.program_id(0)
        halo = jnp.where(i > 0, halo_ref[...], 0.0)
        e = jnp.concatenate([halo, x_ref[...]], axis=0)
        acc = b_ref[...]
        for k in (3, 2, 1):
            acc = acc + pltpu.roll(e, k, axis=0)[8:, :] * w_ref[3 - k:4 - k, :]
        o_ref[...] = acc + x_ref[...] * w_ref[3:4, :]

    return _pcall(body, name=name, grid=(T // tt,),
                  in_specs=[BS((tt, C), lambda i: (i, 0)), BS((8, C), lambda i: (jnp.maximum(i * per - 1, 0), 0)),
                            BS((4, C), lambda i: (0, 0)), BS((1, C), lambda i: (0, 0))],
                  out_specs=BS((tt, C), lambda i: (i, 0)), out_shape=SDS((T, C), F32),
                  compiler_params=_cparams(("parallel",)))(x, x, w, b.reshape(1, C))


def _conv_bwd_call(x, w, dy, name):
    T, C = x.shape
    tt = _tile(T, 512, 8)
    per = tt // 8
    nt = T // tt

    def body(x_ref, halo_ref, w_ref, dy_ref, nxt_ref, dx_ref, dwb_ref):
        i = pl.program_id(0)
        halo = jnp.where(i > 0, halo_ref[...], 0.0)
        e = jnp.concatenate([halo, x_ref[...]], axis=0)
        dy = dy_ref[...]
        nxt = jnp.where(i < nt - 1, nxt_ref[...], 0.0)
        f = jnp.concatenate([dy, nxt], axis=0)
        dx = dy * w_ref[3:4, :]
        rows = [None] * 4
        rows[3] = jnp.sum(dy * x_ref[...], axis=0, keepdims=True)
        for k in (1, 2, 3):
            dx = dx + pltpu.roll(f, tt + 8 - k, axis=0)[:tt, :] * w_ref[3 - k:4 - k, :]
            rows[3 - k] = jnp.sum(dy * pltpu.roll(e, k, axis=0)[8:, :], axis=0, keepdims=True)
        dx_ref[...] = dx
        upd = jnp.concatenate(rows + [jnp.sum(dy, axis=0, keepdims=True), jnp.zeros((3, C), F32)], axis=0)

        @pl.when(i == 0)
        def _():
            dwb_ref[...] = jnp.zeros_like(dwb_ref)

        dwb_ref[...] += upd

    row = BS((tt, C), lambda i: (i, 0))
    return _pcall(body, name=name, grid=(nt,),
                  in_specs=[row, BS((8, C), lambda i: (jnp.maximum(i * per - 1, 0), 0)), BS((4, C), lambda i: (0, 0)),
                            row, BS((8, C), lambda i: (jnp.minimum((i + 1) * per, T // 8 - 1), 0))],
                  out_specs=[row, BS((8, C), lambda i: (0, 0))],
                  out_shape=[SDS((T, C), F32), SDS((8, C), F32)],
                  compiler_params=_cparams(("arbitrary",)))(x, x, w, dy, dy)


@functools.partial(jax.custom_vjp, nondiff_argnums=(3,))
def conv4(x, w, b, name):
    return _conv_fwd_call(x, w, b, name)


def _conv4_fwd(x, w, b, name):
    return _conv_fwd_call(x, w, b, name), (x, w)


def _conv4_bwd(name, res, dy):
    x, w = res
    dx, dwb = _conv_bwd_call(x, w, dy, name + "_bwd")
    return dx, dwb[0:4], dwb[4]


conv4.defvjp(_conv4_fwd, _conv4_bwd)


def _gates_fwd_call(xc, wa, ba, wx, bx, name):
    T, C = xc.shape
    hd = C // LRU_HEADS
    tt = _tile(T, 512, 8)

    def body(x_ref, wa_ref, ba_ref, wx_ref, bx_ref, ga_ref, gx_ref):
        xb = x_ref[...].astype(BF16)
        ga_ref[...] = jnp.dot(xb, wa_ref[0].astype(BF16), preferred_element_type=F32) + ba_ref[...]
        gx_ref[...] = jnp.dot(xb, wx_ref[0].astype(BF16), preferred_element_type=F32) + bx_ref[...]

    blk = BS((tt, hd), lambda i, h: (i, h))
    wsp = BS((1, hd, hd), lambda i, h: (h, 0, 0))
    bsp = BS((1, hd), lambda i, h: (0, h))
    return _pcall(body, name=name, grid=(T // tt, LRU_HEADS), in_specs=[blk, wsp, bsp, wsp, bsp],
                  out_specs=[blk, blk], out_shape=[SDS((T, C), F32)] * 2,
                  compiler_params=_cparams(("parallel", "parallel")))(xc, wa, ba.reshape(1, C), wx, bx.reshape(1, C))


def _gates_bwd_x_call(dga, dgx, wa, wx, name):
    T, C = dga.shape
    hd = C // LRU_HEADS
    tt = _tile(T, 512, 8)
    dn = (((1,), (1,)), ((), ()))

    def body(da_ref, dx_ref, wa_ref, wx_ref, o_ref):
        o_ref[...] = (lax.dot_general(da_ref[...].astype(BF16), wa_ref[0].astype(BF16), dn, preferred_element_type=F32)
                      + lax.dot_general(dx_ref[...].astype(BF16), wx_ref[0].astype(BF16), dn, preferred_element_type=F32))

    blk = BS((tt, hd), lambda i, h: (i, h))
    wsp = BS((1, hd, hd), lambda i, h: (h, 0, 0))
    return _pcall(body, name=name, grid=(T // tt, LRU_HEADS), in_specs=[blk, blk, wsp, wsp], out_specs=blk,
                  out_shape=SDS((T, C), F32), compiler_params=_cparams(("parallel", "parallel")))(dga, dgx, wa, wx)


def _gates_bwd_w_call(xc, dga, dgx, name):
    T, C = xc.shape
    hd = C // LRU_HEADS
    tt = _tile(T, 512, 8)
    dn = (((0,), (0,)), ((), ()))

    def body(x_ref, da_ref, dx_ref, dwa_ref, dwx_ref, dba_ref, dbx_ref):
        @pl.when(pl.program_id(1) == 0)
        def _():
            dwa_ref[...] = jnp.zeros_like(dwa_ref)
            dwx_ref[...] = jnp.zeros_like(dwx_ref)
            dba_ref[...] = jnp.zeros_like(dba_ref)
            dbx_ref[...] = jnp.zeros_like(dbx_ref)

        xb = x_ref[...].astype(BF16)
        da, dx = da_ref[...], dx_ref[...]
        dwa_ref[0] += lax.dot_general(xb, da.astype(BF16), dn, preferred_element_type=F32)
        dwx_ref[0] += lax.dot_general(xb, dx.astype(BF16), dn, preferred_element_type=F32)
        dba_ref[...] += jnp.sum(da, axis=0, keepdims=True)
        dbx_ref[...] += jnp.sum(dx, axis=0, keepdims=True)

    blk = BS((tt, hd), lambda h, i: (i, h))
    wsp = BS((1, hd, hd), lambda h, i: (h, 0, 0))
    bsp = BS((1, hd), lambda h, i: (0, h))
    return _pcall(body, name=name, grid=(LRU_HEADS, T // tt), in_specs=[blk, blk, blk],
                  out_specs=[wsp, wsp, bsp, bsp],
                  out_shape=[SDS((LRU_HEADS, hd, hd), F32)] * 2 + [SDS((1, C), F32)] * 2,
                  compiler_params=_cparams(("parallel", "arbitrary")))(xc, dga, dgx)


@functools.partial(jax.custom_vjp, nondiff_argnums=(7,))
def lru_gates(xc, wa, wa_c, ba, wx, wx_c, bx, name):
    return tuple(_gates_fwd_call(xc, wa, ba, wx, bx, name))


def _lru_gates_fwd(xc, wa, wa_c, ba, wx, wx_c, bx, name):
    return tuple(_gates_fwd_call(xc, wa, ba, wx, bx, name)), (xc, wa, wx, ba.shape)


def _lru_gates_bwd(name, res, g):
    xc, wa, wx, bshape = res
    dga, dgx = g
    dxc = _gates_bwd_x_call(dga, dgx, wa, wx, name + "_dx")
    dwa, dwx, dba, dbx = _gates_bwd_w_call(xc, dga, dgx, name + "_dw")
    return dxc, jnp.zeros_like(wa), dwa, dba.reshape(bshape), jnp.zeros_like(wx), dwx, dbx.reshape(bshape)


lru_gates.defvjp(_lru_gates_fwd, _lru_gates_bwd)


def _lru_coeffs(xc, ga, gx, lam):
    r = _sigmoid(ga)
    ig = _sigmoid(gx)
    z = -lam
    sp = jnp.maximum(z, 0.0) + jnp.log(1.0 + jnp.exp(-jnp.abs(z)))
    la = -LRU_C * r * sp
    a = jnp.exp(la)
    s = jnp.sqrt(-_expm1(2.0 * la))
    return r, ig, sp, a, s


LRU_TT = 256


def _scan_fwd_call(xc, ga, gx, y, lam, name):
    T, C = xc.shape
    tt = _tile(T, LRU_TT, 8)

    def body(xc_ref, ga_ref, gx_ref, y_ref, lam_ref, h_ref, rec_ref, a_buf, carry):
        @pl.when(pl.program_id(0) == 0)
        def _():
            carry[...] = jnp.zeros_like(carry)

        xcv = xc_ref[...]
        _, ig, _, a, s = _lru_coeffs(xcv, ga_ref[...], gx_ref[...], lam_ref[...])
        a_buf[...] = a
        h_ref[...] = s * (ig * xcv)

        def step(t, h):
            hn = a_buf[pl.ds(t, 1), :] * h + h_ref[pl.ds(t, 1), :]
            h_ref[pl.ds(t, 1), :] = hn
            return hn

        carry[0:1, :] = lax.fori_loop(0, tt, step, carry[0:1, :], unroll=8)
        g, _ = _gelu_parts(y_ref[...])
        rec_ref[...] = h_ref[...] * g

    row = BS((tt, C), lambda i: (i, 0))
    vec = BS((1, C), lambda i: (0, 0))
    return _pcall(body, name=name, grid=(T // tt,), in_specs=[row, row, row, row, vec], out_specs=[row, row],
                  out_shape=[SDS((T, C), F32)] * 2,
                  scratch_shapes=[pltpu.VMEM((tt, C), F32), pltpu.VMEM((8, C), F32)],
                  compiler_params=_cparams(("arbitrary",)))(xc, ga, gx, y, lam.reshape(1, C))


def _scan_bwd_call(xc, ga, gx, y, lam, h, drec, name):
    T, C = xc.shape
    tt = _tile(T, LRU_TT, 8)
    nt = T // tt
    per = tt // 8

    def body(xc_ref, ga_ref, gx_ref, y_ref, lam_ref, h_ref, halo_ref, dr_ref,
             dga_ref, dgx_ref, dxc_ref, dy_ref, dlam_ref, a_buf, g_buf, carry):
        i = pl.program_id(0)

        @pl.when(i == 0)
        def _():
            carry[...] = jnp.zeros_like(carry)
            dlam_ref[...] = jnp.zeros_like(dlam_ref)

        xcv, lam = xc_ref[...], lam_ref[...]
        r, ig, sp, a, s = _lru_coeffs(xcv, ga_ref[...], gx_ref[...], lam)
        gel, dgel = _gelu_parts(y_ref[...])
        drec = dr_ref[...]
        hv = h_ref[...]
        dy_ref[...] = drec * hv * dgel
        a_buf[...] = a
        g_buf[...] = drec * gel

        def step(j, q):
            t = tt - 1 - j
            g = g_buf[pl.ds(t, 1), :] + q
            g_buf[pl.ds(t, 1), :] = g
            return a_buf[pl.ds(t, 1), :] * g

        carry[0:1, :] = lax.fori_loop(0, tt, step, carry[0:1, :], unroll=8)
        g = g_buf[...]
        halo = jnp.where(i < nt - 1, halo_ref[...], 0.0)
        hprev = pltpu.roll(jnp.concatenate([halo, hv], axis=0), 1, axis=0)[8:, :]
        da = g * hprev
        dig = g * s * xcv
        ds = g * ig * xcv
        dla = da * a - ds * (a * a) / s
        dga_ref[...] = dla * (-LRU_C * sp) * r * (1.0 - r)
        dgx_ref[...] = dig * ig * (1.0 - ig)
        dxc_ref[...] = g * s * ig
        dlam_ref[...] += jnp.sum(dla * r, axis=0, keepdims=True) * (LRU_C * _sigmoid(-lam))

    row = BS((tt, C), lambda i: (nt - 1 - i, 0))
    vec = BS((1, C), lambda i: (0, 0))
    halo = BS((8, C), lambda i: (jnp.maximum((nt - 1 - i) * per - 1, 0), 0))
    return _pcall(body, name=name, grid=(nt,), in_specs=[row, row, row, row, vec, row, halo, row],
                  out_specs=[row, row, row, row, vec], out_shape=[SDS((T, C), F32)] * 4 + [SDS((1, C), F32)],
                  scratch_shapes=[pltpu.VMEM((tt, C), F32), pltpu.VMEM((tt, C), F32), pltpu.VMEM((8, C), F32)],
                  compiler_params=_cparams(("arbitrary",)))(xc, ga, gx, y, lam.reshape(1, C), h, h, drec)


@functools.partial(jax.custom_vjp, nondiff_argnums=(5,))
def lru_scan(xc, ga, gx, y, lam, name):
    return _scan_fwd_call(xc, ga, gx, y, lam, name)[1]


def _lru_scan_fwd(xc, ga, gx, y, lam, name):
    h, rec = _scan_fwd_call(xc, ga, gx, y, lam, name)
    return rec, (xc, ga, gx, y, lam, h)


def _lru_scan_bwd(name, res, drec):
    xc, ga, gx, y, lam, h = res
    dga, dgx, dxc, dy, dlam = _scan_bwd_call(xc, ga, gx, y, lam, h, drec, name + "_bwd")
    return dxc, dga, dgx, dy, dlam.reshape(lam.shape)


lru_scan.defvjp(_lru_scan_fwd, _lru_scan_bwd)


def _att_batch(d, shared=False):
    if shared:
        return 8, 1
    return (4, 1) if d == 1 else (1, min(d, 8))


def _att_masks(n, max_dist):
    qi = lax.broadcasted_iota(jnp.int32, (1, 2 * ATT_BLOCK, 2 * ATT_BLOCK), 1) & (ATT_BLOCK - 1)
    kj = lax.broadcasted_iota(jnp.int32, (1, 2 * ATT_BLOCK, 2 * ATT_BLOCK), 2)
    prev = (kj < ATT_BLOCK) & (kj >= qi + (ATT_BLOCK - max_dist)) & (n > 0)
    cur = (kj >= ATT_BLOCK) & (kj - ATT_BLOCK <= qi)
    return prev | cur


def _lo_lanes(rows):
    return lax.broadcasted_iota(jnp.int32, (rows, LANES), 1) < HEAD_DIM


def _lane_half(rows):
    return lax.broadcasted_iota(jnp.int32, (rows, LANES), 1) // HEAD_DIM


def _stack_heads(x2):
    lo = _lo_lanes(ATT_BLOCK)
    zero = jnp.zeros_like(x2)
    return jnp.concatenate([jnp.where(lo, x2, zero), jnp.where(lo, zero, x2)], axis=0)


def _unstack_heads(y):
    return jnp.where(_lo_lanes(ATT_BLOCK), y[:ATT_BLOCK], y[ATT_BLOCK:])


def _per_head_col(x2):
    return jnp.concatenate([x2[:, 0:1], x2[:, HEAD_DIM:HEAD_DIM + 1]], axis=0)


def _head_sums(x2):
    lo = _lo_lanes(ATT_BLOCK)
    return jnp.concatenate([jnp.sum(jnp.where(lo, x2, 0.0), axis=1, keepdims=True),
                            jnp.sum(jnp.where(lo, 0.0, x2), axis=1, keepdims=True)], axis=0)


def _att_specs(d, Wq, Wk, nb, clamp):
    shared = Wk != Wq
    cgw, sb = _att_batch(d, shared)
    cur = (lambda n: jnp.minimum(n, nb - 1)) if clamp else (lambda n: n)
    rows, qw, kw = ATT_BLOCK * d, cgw * LANES, (LANES if shared else cgw * LANES)
    kcol = (lambda g: 0) if shared else (lambda g: g)
    qsp = BS((rows, qw), lambda g, n: (cur(n), g))
    csp = BS((rows, kw), lambda g, n: (cur(n), kcol(g)))
    psp = BS((rows, kw), lambda g, n: (jnp.maximum(cur(n) - 1, 0), kcol(g)))
    return cgw, sb, shared, qsp, csp, psp, qw, kw


def _att_streams(d, sb, work):
    if d == 1:
        work([slice(None)])
        return

    def one(j, carry):
        work([pl.ds(j * sb + i, ATT_BLOCK, stride=d) for i in range(sb)])
        return carry

    lax.fori_loop(0, d // sb, one, 0)


def _att_problem_loads(rows, cgw, shared, g, q_ref, kc_ref, kp_ref, vc_ref, vp_ref, sk_ref):
    half = _lane_half(ATT_BLOCK)

    def kv(ref, r, p):
        x = ref[r, :]
        if not shared:
            return x[:, p * LANES:(p + 1) * LANES]
        return jnp.where(half == p // 4, x, pltpu.roll(x, HEAD_DIM, axis=1))

    qs, kb, vb, sk = [], [], [], []
    for r in rows:
        qrow = q_ref[r, :]
        for p in range(cgw):
            cols = slice(p * LANES, (p + 1) * LANES)
            qs.append(_stack_heads(qrow[:, cols].astype(BF16)))
            kb.append(jnp.concatenate([kv(kp_ref, r, p), kv(kc_ref, r, p)], axis=0).astype(BF16))
            vb.append(jnp.concatenate([kv(vp_ref, r, p), kv(vc_ref, r, p)], axis=0).astype(BF16))
            sk.append(_per_head_col(jnp.broadcast_to(sk_ref[:, cols], (ATT_BLOCK, LANES))))
    return jnp.stack(qs), jnp.stack(kb), jnp.stack(vb), jnp.stack(sk)


_BDOT_NT = (((2,), (2,)), ((0,), (0,)))
_BDOT_NN = (((2,), (1,)), ((0,), (0,)))
_BDOT_TN = (((1,), (1,)), ((0,), (0,)))


def _att_fwd_call(q, k, v, sinks, d, max_dist, name):
    T, Wq = q.shape
    Wk = k.shape[1]
    nb = T // (d * ATT_BLOCK)
    cgw, sb, shared, qsp, csp, psp, qw, kw = _att_specs(d, Wq, Wk, nb, False)
    G = Wq // qw
    assert not shared or (d == 1 and Wk == LANES and G == 1 and cgw == 8), "a shared kv pair serves 2 x 8 query heads"

    def body(q_ref, kc_ref, kp_ref, vc_ref, vp_ref, sk_ref, o_ref, lse_ref):
        g, n = pl.program_id(0), pl.program_id(1)

        def work(rows):
            qs, kband, vband, sk = _att_problem_loads(rows, cgw, shared, g, q_ref, kc_ref, kp_ref, vc_ref, vp_ref, sk_ref)
            s = lax.dot_general(qs, kband, _BDOT_NT, preferred_element_type=F32) * (HEAD_DIM ** -0.5)
            s = jnp.where(_att_masks(n, max_dist), s, NEG)
            m = jnp.maximum(jnp.max(s, axis=-1, keepdims=True), sk)
            e = jnp.exp(s - m)
            den = jnp.sum(e, axis=-1, keepdims=True) + jnp.exp(sk - m)
            o = lax.dot_general((e * (1.0 / den)).astype(BF16), vband, _BDOT_NN, preferred_element_type=F32)
            lse = jnp.broadcast_to(m + jnp.log(den), o.shape)
            for i, r in enumerate(rows):
                o_ref[r, :] = jnp.concatenate([_unstack_heads(o[i * cgw + p]) for p in range(cgw)], axis=1)
                lse_ref[r, :] = jnp.concatenate([_unstack_heads(lse[i * cgw + p]) for p in range(cgw)], axis=1)

        _att_streams(d, sb, work)

    sksp = BS((1, qw), lambda g, n: (0, g))
    return _pcall(body, name=name, grid=(G, nb), in_specs=[qsp, csp, psp, csp, psp, sksp], out_specs=[qsp, qsp],
                  out_shape=[SDS((T, Wq), F32)] * 2,
                  compiler_params=_cparams(("parallel", "parallel")))(q, k, k, v, v, sinks)


def _att_bwd_call(q, k, v, sinks, o, lse, do, dlse, d, max_dist, name):
    T, Wq = q.shape
    Wk = k.shape[1]
    nb = T // (d * ATT_BLOCK)
    cgw, sb, shared, qsp, csp, psp, qw, kw = _att_specs(d, Wq, Wk, nb, True)
    G = Wq // qw
    scale = HEAD_DIM ** -0.5

    def body(*refs):
        q_ref, kc_ref, kp_ref, vc_ref, vp_ref, sk_ref, o_ref, lse_ref, do_ref = refs[:9]
        dlse_ref = refs[9] if dlse is not None else None
        dq_ref, dk_ref, dv_ref, dsk_ref, ck, cv = refs[-6:]
        g, n = pl.program_id(0), pl.program_id(1)

        @pl.when(n == 0)
        def _():
            ck[...] = jnp.zeros_like(ck)
            cv[...] = jnp.zeros_like(cv)
            dsk_ref[...] = jnp.zeros_like(dsk_ref)

        def work(rows):
            qs, kband, vband, sk = _att_problem_loads(rows, cgw, shared, g, q_ref, kc_ref, kp_ref, vc_ref, vp_ref, sk_ref)
            dos, lse_c, corr = [], [], []
            for r in rows:
                do_r, o_r, lse_r = do_ref[r, :], o_ref[r, :], lse_ref[r, :]
                dlse_r = dlse_ref[r, :] if dlse_ref is not None else None
                for p in range(cgw):
                    cols = slice(p * LANES, (p + 1) * LANES)
                    dos.append(_stack_heads(do_r[:, cols].astype(BF16)))
                    lse_c.append(_per_head_col(lse_r[:, cols]))
                    delta = _head_sums(do_r[:, cols] * o_r[:, cols])
                    corr.append(-delta if dlse_r is None else _head_sums(dlse_r[:, cols]) - delta)
            dos, lse_c, corr = jnp.stack(dos), jnp.stack(lse_c), jnp.stack(corr)
            s = lax.dot_general(qs, kband, _BDOT_NT, preferred_element_type=F32) * scale
            pr = jnp.exp(jnp.where(_att_masks(n, max_dist), s, NEG) - lse_c)
            dp = lax.dot_general(dos, vband, _BDOT_NT, preferred_element_type=F32)
            ds = (pr * (dp + corr)).astype(BF16)
            dq = lax.dot_general(ds, kband, _BDOT_NN, preferred_element_type=F32) * scale
            dkb = lax.dot_general(ds, qs, _BDOT_TN, preferred_element_type=F32) * scale
            dvb = lax.dot_general(pr.astype(BF16), dos, _BDOT_TN, preferred_element_type=F32)
            dsk = jnp.exp(sk - lse_c) * corr
            lane = lax.broadcasted_iota(jnp.int32, (8, LANES), 1)
            for p in range(cgw):
                tot = [jnp.sum(jnp.stack([dsk[i * cgw + p, h * ATT_BLOCK:(h + 1) * ATT_BLOCK] for i in range(len(rows))]),
                               axis=(0, 1)).reshape(1, 1) for h in range(2)]
                dsk_ref[:, p * LANES:(p + 1) * LANES] += jnp.where(lane == 0, tot[0], jnp.where(lane == HEAD_DIM, tot[1], 0.0))

            def gather_pairs(parts):
                if not shared:
                    return jnp.concatenate(parts, axis=1)
                tot = [parts[4 * h] + parts[4 * h + 1] + parts[4 * h + 2] + parts[4 * h + 3] for h in range(2)]
                tot = [t + pltpu.roll(t, HEAD_DIM, axis=1) for t in tot]
                return jnp.where(_lo_lanes(ATT_BLOCK), tot[0], tot[1])

            for i, r in enumerate(rows):
                mine = range(i * cgw, (i + 1) * cgw)
                dq_ref[r, :] = jnp.concatenate([_unstack_heads(dq[b]) for b in mine], axis=1)
                dk_ref[r, :] = ck[r, :] + gather_pairs([dkb[b, :ATT_BLOCK] for b in mine])
                dv_ref[r, :] = cv[r, :] + gather_pairs([dvb[b, :ATT_BLOCK] for b in mine])
                ck[r, :] = gather_pairs([dkb[b, ATT_BLOCK:] for b in mine])
                cv[r, :] = gather_pairs([dvb[b, ATT_BLOCK:] for b in mine])

        @pl.when(n < nb)
        def _():
            _att_streams(d, sb, work)

        @pl.when(n == nb)
        def _():
            dk_ref[...] = ck[...]
            dv_ref[...] = cv[...]

    sksp = BS((1, qw), lambda g, n: (0, g))
    rows = ATT_BLOCK * d
    osp = BS((rows, kw), lambda g, n: (jnp.maximum(n - 1, 0), 0 if shared else g))
    kshape = SDS((T, Wk), F32)
    dq, dk, dv, dsk = _pcall(
        body, name=name, grid=(G, nb + 1),
        in_specs=[qsp, csp, psp, csp, psp, sksp, qsp, qsp, qsp] + ([qsp] if dlse is not None else []),
        out_specs=[qsp, osp, osp, BS((8, qw), lambda g, n: (0, g))],
        out_shape=[SDS((T, Wq), F32), kshape, kshape, SDS((8, Wq), F32)],
        scratch_shapes=[pltpu.VMEM((rows, kw), F32)] * 2,
        compiler_params=_cparams(("parallel", "arbitrary")))(q, k, k, v, v, sinks, o, lse, do,
                                                              *([dlse] if dlse is not None else []))
    return dq, dk, dv, dsk[0:1]


@functools.partial(jax.custom_vjp, nondiff_argnums=(4, 5, 6, 7))
def band_attention(q, k, v, sinks, d, max_dist, with_lse, name):
    return _band_attention_fwd(q, k, v, sinks, d, max_dist, with_lse, name)[0]


def _band_attention_fwd(q, k, v, sinks, d, max_dist, with_lse, name):
    o, lse = _att_fwd_call(q, k, v, sinks, d, max_dist, name)
    return ((o, lse) if with_lse else o), (q, k, v, sinks, o, lse)


def _band_attention_bwd(d, max_dist, with_lse, name, res, g):
    q, k, v, sinks, o, lse = res
    do, dlse = g if with_lse else (g, None)
    return _att_bwd_call(q, k, v, sinks, o, lse, do, dlse, d, max_dist, name + "_bwd")


band_attention.defvjp(_band_attention_fwd, _band_attention_bwd)


def _merge_weights(ls):
    mx = jnp.maximum(jnp.maximum(ls[0], ls[1]), ls[2])
    es = [jnp.exp(l - mx) for l in ls]
    inv = 1.0 / (es[0] + es[1] + es[2])
    return [e * inv for e in es]


def _merge_fwd_call(os_, ls_, name):
    T, W = os_[0].shape
    tt = _tile(T, 512, 8)

    def body(o1, o2, o3, l1, l2, l3, out_ref):
        w = _merge_weights([l1[...], l2[...], l3[...]])
        out_ref[...] = w[0] * o1[...] + w[1] * o2[...] + w[2] * o3[...]

    row = BS((tt, W), lambda i: (i, 0))
    return _pcall(body, name=name, grid=(T // tt,), in_specs=[row] * 6, out_specs=row,
                  out_shape=SDS((T, W), F32), compiler_params=_cparams(("parallel",)))(*os_, *ls_)


def _merge_bwd_call(os_, ls_, do, name):
    T, W = os_[0].shape
    tt = _tile(T, 512, 8)

    def body(o1, o2, o3, l1, l2, l3, do_ref, d1, d2, d3, e1, e2, e3):
        w = _merge_weights([l1[...], l2[...], l3[...]])
        dov = do_ref[...]
        ts = [dov * o[...] for o in (o1, o2, o3)]
        mean = w[0] * ts[0] + w[1] * ts[1] + w[2] * ts[2]
        for wi, ti, dref, eref in zip(w, ts, (d1, d2, d3), (e1, e2, e3)):
            dref[...] = wi * dov
            eref[...] = wi * (ti - mean)

    row = BS((tt, W), lambda i: (i, 0))
    return _pcall(body, name=name, grid=(T // tt,), in_specs=[row] * 7, out_specs=[row] * 6,
                  out_shape=[SDS((T, W), F32)] * 6, compiler_params=_cparams(("parallel",)))(*os_, *ls_, do)


@functools.partial(jax.custom_vjp, nondiff_argnums=(2,))
def merge3(os_, ls_, name):
    return _merge_fwd_call(os_, ls_, name)


def _merge3_fwd(os_, ls_, name):
    return _merge_fwd_call(os_, ls_, name), (os_, ls_)


def _merge3_bwd(name, res, do):
    os_, ls_ = res
    out = _merge_bwd_call(os_, ls_, do, name + "_bwd")
    return tuple(out[:3]), tuple(out[3:])


merge3.defvjp(_merge3_fwd, _merge3_bwd)


def _xa_probs(qb, kb, scale):
    s = lax.dot_general(qb, kb, (((1,), (1,)), ((), ())), preferred_element_type=F32) * scale
    e = jnp.exp(s - jnp.max(s, axis=-1, keepdims=True))
    return e / jnp.sum(e, axis=-1, keepdims=True)


def _xa_fwd_call(q, kv, name):
    T, W = q.shape
    M = kv.shape[0]
    hd = XA_HEAD_DIM
    tq = _tile(T, 512, 8)
    scale = hd ** -0.5

    def body(q_ref, k_ref, v_ref, o_ref):
        p = _xa_probs(q_ref[...].astype(BF16), k_ref[...].astype(BF16), scale)
        o_ref[...] = jnp.dot(p.astype(BF16), v_ref[...].astype(BF16), preferred_element_type=F32)

    qsp = BS((tq, hd), lambda i, h: (i, h))
    return _pcall(body, name=name, grid=(T // tq, XA_HEADS),
                  in_specs=[qsp, BS((M, hd), lambda i, h: (0, h)), BS((M, hd), lambda i, h: (0, XA_HEADS + h))],
                  out_specs=qsp, out_shape=SDS((T, W), F32),
                  compiler_params=_cparams(("parallel", "parallel")))(q, kv, kv)


def _xa_bwd_call(q, kv, do, name):
    T, W = q.shape
    M = kv.shape[0]
    hd = XA_HEAD_DIM
    tq = _tile(T, 512, 8)
    scale = hd ** -0.5
    dn_nt = (((1,), (1,)), ((), ()))
    dn_tn = (((0,), (0,)), ((), ()))

    def body(q_ref, k_ref, v_ref, do_ref, dq_ref, dk_ref, dv_ref):
        @pl.when(pl.program_id(1) == 0)
        def _():
            dk_ref[...] = jnp.zeros_like(dk_ref)
            dv_ref[...] = jnp.zeros_like(dv_ref)

        qb, kb, vb = q_ref[...].astype(BF16), k_ref[...].astype(BF16), v_ref[...].astype(BF16)
        p = _xa_probs(qb, kb, scale)
        dob = do_ref[...].astype(BF16)
        dp = lax.dot_general(dob, vb, dn_nt, preferred_element_type=F32)
        ds = (p * (dp - jnp.sum(p * dp, axis=-1, keepdims=True))).astype(BF16)
        dq_ref[...] = jnp.dot(ds, kb, preferred_element_type=F32) * scale
        dk_ref[...] += lax.dot_general(ds, qb, dn_tn, preferred_element_type=F32) * scale
        dv_ref[...] += lax.dot_general(p.astype(BF16), dob, dn_tn, preferred_element_type=F32)

    qsp = BS((tq, hd), lambda h, i: (i, h))
    ksp = BS((M, hd), lambda h, i: (0, h))
    return _pcall(body, name=name, grid=(XA_HEADS, T // tq),
                  in_specs=[qsp, ksp, BS((M, hd), lambda h, i: (0, XA_HEADS + h)), qsp],
                  out_specs=[qsp, ksp, ksp], out_shape=[SDS((T, W), F32), SDS((M, W), F32), SDS((M, W), F32)],
                  compiler_params=_cparams(("parallel", "arbitrary")))(q, kv, kv, do)


@functools.partial(jax.custom_vjp, nondiff_argnums=(2,))
def cross_attention(q, kv, name):
    return _xa_fwd_call(q, kv, name)


def _cross_attention_fwd(q, kv, name):
    return _xa_fwd_call(q, kv, name), (q, kv)


def _cross_attention_bwd(name, res, do):
    q, kv = res
    dq, dk, dv = _xa_bwd_call(q, kv, do, name + "_bwd")
    return dq, jnp.concatenate([dk, dv], axis=1)


cross_attention.defvjp(_cross_attention_fwd, _cross_attention_bwd)


def _gate_up_swiglu_call(hn, w1t, name):
    T, K = hn.shape
    F = w1t.shape[0] // 2
    tm, tn = _tile(T, 1024), _tile(F, 256)
    nj = F // tn
    dn = (((1,), (1,)), ((), ()))

    def body(a_ref, wg_ref, wu_ref, g_ref, u_ref, act_ref):
        a = a_ref[...]
        g = lax.dot_general(a, wg_ref[...], dn, preferred_element_type=F32)
        u = lax.dot_general(a, wu_ref[...], dn, preferred_element_type=F32)
        g_ref[...] = g
        u_ref[...] = u
        act_ref[...] = ((g * _sigmoid(g)) * u).astype(BF16)

    tile = BS((tm, tn), lambda i, j: (i, j))
    return _pcall(body, name=name, grid=(T // tm, nj),
                  in_specs=[BS((tm, K), lambda i, j: (i, 0)), BS((tn, K), lambda i, j: (j, 0)),
                            BS((tn, K), lambda i, j: (j + nj, 0))],
                  out_specs=[tile, tile, tile], out_shape=[SDS((T, F), F32), SDS((T, F), F32), SDS((T, F), BF16)],
                  compiler_params=_cparams(("parallel", "parallel")))(hn, w1t, w1t)


def _swiglu_bwd_call(g, u, dact, name):
    T, F = g.shape
    tt = _tile(T, 128, 16)

    def body(g_ref, u_ref, d_ref, o_ref):
        g, d = g_ref[...], d_ref[...]
        sg = _sigmoid(g)
        o_ref[:, :F] = (d * u_ref[...] * (sg * (1.0 + g * (1.0 - sg)))).astype(BF16)
        o_ref[:, F:] = (d * (g * sg)).astype(BF16)

    row = BS((tt, F), lambda i: (i, 0))
    return _pcall(body, name=name, grid=(T // tt,), in_specs=[row, row, row],
                  out_specs=BS((tt, 2 * F), lambda i: (i, 0)), out_shape=SDS((T, 2 * F), BF16),
                  compiler_params=_cparams(("parallel",)))(g, u, dact)


@functools.partial(jax.custom_vjp, nondiff_argnums=(6,))
def ffn_block(h, g, w1b, w1c, w2b, w2c, name):
    return _ffn_fwd(h, g, w1b, w1c, w2b, w2c, name)[0]


def _ffn_fwd(h, g, w1b, w1c, w2b, w2c, name):
    hn = _rms_fwd_call(h, g, name + "_norm", BF16)
    gate, up, act = _gate_up_swiglu_call(hn, w1b, name + "_gu")
    out = _mm(act, w2b, add=h, name=name + "_down", tm=1024, tn=1024, tk=1408)
    return out, (h, g, hn, gate, up, act, w1b, w2b)


def _ffn_bwd(name, res, dout):
    h, g, hn, gate, up, act, w1b, w2b = res
    dact = _mm(dout, w2b, tb=True, name=name + "_down_da", tm=1024, tn=1408, tk=1024)
    dw2 = _mm(act, dout, ta=True, name=name + "_down_dw", tm=1408, tn=1024, tk=1024)
    dgu = _swiglu_bwd_call(gate, up, dact, name + "_swiglu_bwd")
    dw1 = _mm(dgu, hn, ta=True, name=name + "_gu_dw", tm=1408, tn=1024, tk=1024)
    dh, dg = _mm_rms_bwd(dgu, w1b, h, g, dout, name + "_gu_da", wt=True)
    return dh, dg.reshape(g.shape), jnp.zeros_like(w1b), dw1, jnp.zeros_like(w2b), dw2


ffn_block.defvjp(_ffn_fwd, _ffn_bwd)


def _final_call(h, g, target, name):
    T, Dm = h.shape
    tt = _tile(T, 512, 8)

    def body(x_ref, g_ref, t_ref, loss_ref, dx_ref, dg_ref):
        @pl.when(pl.program_id(0) == 0)
        def _():
            loss_ref[...] = jnp.zeros_like(loss_ref)
            dg_ref[...] = jnp.zeros_like(dg_ref)

        xv, gv = x_ref[...], g_ref[...]
        r = lax.rsqrt(jnp.mean(xv * xv, axis=-1, keepdims=True) + NORM_EPS)
        xh = xv * r
        err = xh * gv - t_ref[...]
        loss_ref[...] += 0.5 * jnp.sum(jnp.mean(err * err, axis=-1, keepdims=True), axis=0, keepdims=True)
        dy = err * (1.0 / Dm)
        dyg = dy * gv
        dx_ref[...] = r * (dyg - xh * jnp.mean(dyg * xh, axis=-1, keepdims=True))
        dg_ref[...] += jnp.sum(dy * xh, axis=0, keepdims=True)

    row = BS((tt, Dm), lambda i: (i, 0))
    vec = BS((1, Dm), lambda i: (0, 0))
    return _pcall(body, name=name, grid=(T // tt,), in_specs=[row, vec, row],
                  out_specs=[BS((1, 1), lambda i: (0, 0)), row, vec],
                  out_shape=[SDS((1, 1), F32), SDS((T, Dm), F32), SDS((1, Dm), F32)],
                  compiler_params=_cparams(("arbitrary",)))(h, g.reshape(1, Dm), target)


ADAMW_BLOCK_ELEMS = 64 * 1024


def _adamw_call(parts, w, m, v, name):
    shape = w.shape
    if not isinstance(parts, (list, tuple)):
        parts, shape3 = [parts], (1,) + shape
    else:
        shape3 = shape
    n_lead = shape3[0]
    r, N = shape3[-2], shape3[-1]
    Ld = math.prod(shape3[1:-2])
    w, m, v = (t.reshape(n_lead * Ld, r, N) for t in (w, m, v))
    tr = _tile(r, max(8, ADAMW_BLOCK_ELEMS // N), 8)
    c1 = 1.0 - ADAM_B1 ** ADAM_STEP
    c2 = 1.0 - ADAM_B2 ** ADAM_STEP
    outs = None
    for lead, p in enumerate(parts):
        def body(p_ref, w_ref, m_ref, v_ref, *rest):
            g_ref, d_ref, nm_ref, nv_ref = rest[-4:]
            g = p_ref[0]
            for j in range(1, N_DEV):
                g = g + p_ref[j]
            nm = ADAM_B1 * m_ref[...] + (1.0 - ADAM_B1) * g
            nv = ADAM_B2 * v_ref[...] + (1.0 - ADAM_B2) * (g * g)
            g_ref[...] = g
            nm_ref[...] = nm
            nv_ref[...] = nv
            d_ref[...] = -ADAM_LR * ((nm / c1) / (jnp.sqrt(nv / c2) + ADAM_EPS) + ADAM_WD * w_ref[...])

        base = lead * Ld
        row = BS((1, tr, N), lambda l, i, base=base: (base + l, i, 0))
        prev = [] if outs is None else list(outs)
        outs = _pcall(body, name=f"{name}_{lead}", grid=(Ld, r // tr),
                      in_specs=[BS((N_DEV, 1, tr, N), lambda l, i: (0, l, i, 0)), row, row, row]
                      + [BS(memory_space=pl.ANY)] * len(prev),
                      out_specs=[row] * 4, out_shape=[SDS((n_lead * Ld, r, N), F32)] * 4,
                      input_output_aliases={4 + j: j for j in range(len(prev))},
                      compiler_params=_cparams(("parallel", "parallel")))(p.reshape(N_DEV, Ld, r, N), w, m, v, *prev)
    return [t.reshape(shape) for t in outs]


def _place():
    return lax.axis_index("x"), lax.axis_index("y"), lax.axis_index("c")


def _all_gather(xs, name):
    n = len(xs)
    pairs = [(i, l) for i, x in enumerate(xs) for l in range(x.shape[0])]

    def body(*refs):
        x_refs, o_refs = refs[:n], refs[n:2 * n]
        send_sems, recv_sems, local_sems = refs[2 * n:]
        x_, y_, c_ = _place()
        me, sibling = (x_, y_, c_), (x_, y_, 1 - c_)
        chips = [(1 - x_, y_), (x_, 1 - y_), (1 - x_, 1 - y_)]

        def copy(e, k, block, to, from_input=False):
            i, l = pairs[e]
            px, py, pc = block
            dst = o_refs[i].at[l, 4 * px + 2 * py + pc]
            return pltpu.make_async_remote_copy(
                src_ref=x_refs[i].at[l] if from_input else dst, dst_ref=dst,
                send_sem=send_sems.at[7 * e + k], recv_sem=recv_sems.at[7 * e + k],
                device_id=to, device_id_type=pl.DeviceIdType.MESH)

        every = range(len(pairs))
        mine = [pltpu.make_async_copy(x_refs[i].at[l], o_refs[i].at[l, 4 * x_ + 2 * y_ + c_], local_sems.at[e])
                for e, (i, l) in enumerate(pairs)]
        for cp in mine:
            cp.start()
        first = [copy(e, 0, me, sibling, True) for e in every]
        first += [copy(e, 1 + j, me, (*chip, c_), True) for j, chip in enumerate(chips) for e in every]
        for cp in first:
            cp.start()
        passed = []
        for j, chip in enumerate(chips):
            for e in every:
                copy(e, 1 + j, (*chip, c_), me).wait_recv()
            for e in every:
                cp = copy(e, 4 + j, (*chip, c_), sibling)
                cp.start()
                passed.append(cp)
        for e in every:
            copy(e, 0, sibling, me).wait_recv()
        for j, chip in enumerate(chips):
            for e in every:
                copy(e, 4 + j, (*chip, 1 - c_), me).wait_recv()
        for cp in first + passed:
            cp.wait_send()
        for cp in mine:
            cp.wait()

    any_spec = BS(memory_space=pl.ANY)
    return _pcall(body, name=name, in_specs=[any_spec] * n, out_specs=[any_spec] * n,
                  out_shape=[SDS((x.shape[0], N_DEV) + x.shape[1:], x.dtype) for x in xs],
                  scratch_shapes=[pltpu.SemaphoreType.DMA((7 * len(pairs),)), pltpu.SemaphoreType.DMA((7 * len(pairs),)),
                                  pltpu.SemaphoreType.DMA((len(pairs),))],
                  compiler_params=pltpu.CompilerParams(has_side_effects=True))(*xs)


def _peer_of(k, place):
    x_, y_, c_ = place
    fx, fy, fc = (k >> 2) & 1, (k >> 1) & 1, k & 1
    return fx + x_ - 2 * fx * x_, fy + y_ - 2 * fy * y_, fc + c_ - 2 * fc * c_


def _split_copy(src_ref, land_ref, send_sems, recv_sems, e, k, place, scatter):
    x_, y_, c_ = place
    px, py, pc = _peer_of(k, place)
    return pltpu.make_async_remote_copy(
        src_ref=src_ref.at[4 * px + 2 * py + pc] if scatter else src_ref, dst_ref=land_ref.at[4 * x_ + 2 * y_ + c_],
        send_sem=send_sems.at[7 * e + k - 1], recv_sem=recv_sems.at[7 * e + k - 1],
        device_id=(px, py, pc), device_id_type=pl.DeviceIdType.MESH)


_HBM_SPEC = BS(memory_space=pltpu.HBM)
_SEM_SPEC = BS(memory_space=pltpu.SEMAPHORE)
_EFFECT = pltpu.SideEffectType.DATAFLOW_SIDE_EFFECTING


def _copies_start(srcs, scatter, name, thru=None):
    n = len(srcs)
    lands = [lax.empty(s.shape if scatter else (N_DEV,) + s.shape, s.dtype) for s in srcs]
    passed = srcs + lands + list(thru or ())

    def body(*refs):
        src_refs, land_refs = refs[:n], refs[n:2 * n]
        send_sems, recv_sems = refs[len(passed)], refs[len(passed) + 1]
        token = refs[-1]
        place = _place()
        for e in range(n):
            for k in range(1, N_DEV):
                _split_copy(src_refs[e], land_refs[e], send_sems, recv_sems, e, k, place, scatter).start()
        token[...] = jnp.zeros_like(token)

    hbm = lambda t: pltpu.with_memory_space_constraint(t, pltpu.HBM)
    out = _pcall(body, name=name,
                 out_shape=(pltpu.SemaphoreType.DMA((7 * n,)), pltpu.SemaphoreType.DMA((7 * n,)),
                            *[pltpu.HBM(t.shape, t.dtype) for t in passed], SDS((8, LANES), F32)),
                 in_specs=[_HBM_SPEC] * len(passed),
                 out_specs=(_SEM_SPEC, _SEM_SPEC, *[_HBM_SPEC] * len(passed), BS(memory_space=pltpu.VMEM)),
                 input_output_aliases={i: 2 + i for i in range(len(passed))},
                 compiler_params=pltpu.CompilerParams(has_side_effects=_EFFECT))(*[hbm(t) for t in passed])
    return out[0], out[1], list(out[2:2 + n]), list(out[2 + n:2 + 2 * n]), list(out[2 + 2 * n:-1])


def _copies_wait(started, which, scatter, after, name):
    send_sems, recv_sems, srcs, lands, _ = started
    n = len(which)
    after = list(after) if isinstance(after, (list, tuple)) else [after]

    def body(*refs):
        src_refs, land_refs = refs[:n], refs[n:2 * n]
        send_s, recv_s = refs[2 * n], refs[2 * n + 1]
        place = _place()
        for j, e in enumerate(which):
            for k in range(1, N_DEV):
                cp = _split_copy(src_refs[j], land_refs[j], send_s, recv_s, e, k, place, scatter)
                cp.wait_send()
                cp.wait_recv()

    args = [srcs[e] for e in which] + [lands[e] for e in which]
    out = _pcall(body, name=name, out_shape=tuple(pltpu.HBM(t.shape, t.dtype) for t in args),
                 in_specs=[_HBM_SPEC] * (2 * n) + [_SEM_SPEC, _SEM_SPEC] + [BS(memory_space=pl.ANY)] * len(after),
                 out_specs=tuple([_HBM_SPEC] * (2 * n)), input_output_aliases={i: i for i in range(2 * n)},
                 compiler_params=pltpu.CompilerParams(has_side_effects=_EFFECT))(*args, send_sems, recv_sems, *after)
    return list(out[:n]), list(out[n:])


def _with_own_block(land, own_block):
    me = 4 * lax.axis_index("x") + 2 * lax.axis_index("y") + lax.axis_index("c")
    return lax.dynamic_update_index_in_dim(land, own_block, me, 0)


def _pad_flat(t, quantum=PACK_QUANTUM):
    f = t.reshape(-1)
    pad = (-f.shape[0]) % quantum
    return jnp.pad(f, (0, pad)) if pad else f


def _pack(arrs, dtype):
    return jnp.concatenate([_pad_flat(a.astype(dtype)) for a in arrs]).reshape(-1, LANES)


def _unpack(buf, shapes, lead=()):
    flat = buf.reshape(lead + (-1,))
    out, off = [], 0
    for s in shapes:
        n = math.prod(s)
        out.append(flat[..., off:off + n].reshape(lead + tuple(s)))
        off += n + (-n) % PACK_QUANTUM
    return out


def _full_from_gathered(g, axis):
    t = jnp.moveaxis(g, 0, axis)
    s = t.shape
    return t.reshape(s[:axis] + (s[axis] * s[axis + 1],) + s[axis + 2:])


def _parts_from_full(t, axis):
    s = t.shape
    t = t.reshape(s[:axis] + (N_DEV, s[axis] // N_DEV) + s[axis + 1:])
    return jnp.moveaxis(t, axis, 0)


def _head_rows(t):
    return jnp.repeat(t, HEAD_DIM).reshape(1, -1)


def _dilated_attention(q, k, v, name):
    no_sink = jnp.full((1, q.shape[1]), NEG, F32)
    outs, lses = zip(*[band_attention(q, k, v, no_sink, d, ATT_BLOCK, True, f"{name}_d{d}") for d in DILATIONS])
    return merge3(tuple(outs), tuple(lses), name + "_merge")


STAGES = (
    ("proj0", ('mix_norm',), (('ab_w_in', 0),)),
    ("mixer0", ('lru_conv_w', 'lru_conv_b', 'lru_ba', 'lru_bx', 'lru_lambda'),
     (('lru_wa', 0), ('lru_wx', 0), ('ab_w_out', 0))),
    ("xa0", ('xa_norm', 'xa_mem_norm'), (('xa_wq', 0), ('xa_wkv', 0), ('xa_wo', 0))),
    ("ffn0", ('ffn_norm',), (('ffn_w_gate_up', 0), ('ffn_w_down', 0))),
    ("mixer1", ('mix_norm', 'c_b_qkv', 'c_sinks', 'c_b_out'), (('c_w_qkv', 0), ('c_w_out', 0))),
    ("xa1", ('xa_norm', 'xa_mem_norm'), (('xa_wq', 1), ('xa_wkv', 1), ('xa_wo', 1))),
    ("ffn1", ('ffn_norm',), (('ffn_w_gate_up', 1), ('ffn_w_down', 1))),
)


def _stage_fn(stage, Wb, tabs, mem):
    layer = int(stage[-1])
    L = f"l{layer}"

    def run(S, Cw, h):
        def lin(a, key, bias, add, name, rows=None):
            wb, wc = Wb[key], Cw[key]
            if rows is not None:
                wb, wc = wb[rows], wc[rows]
            return linear(a, wb, wc, bias, add, name)

        def norm_lin(a, gain, key, bias, name):
            return norm_linear(a, gain, Wb[key], Cw[key], bias, key[0] in COLUMN_CUT, name)

        if stage == "mixer0":
            h, proj = h
            C = S['lru_conv_w'].shape[-1]
            xc = conv4(proj[:, :C], S['lru_conv_w'][0], S['lru_conv_b'][0], L + "_conv")
            ga, gx = lru_gates(xc, Wb['lru_wa', 0], Cw['lru_wa', 0], S['lru_ba'][0],
                               Wb['lru_wx', 0], Cw['lru_wx', 0], S['lru_bx'][0], L + "_gates")
            rec = lru_scan(xc, ga, gx, proj[:, C:2 * C], S['lru_lambda'][0], L + "_scan")
            bw = B_HEADS * HEAD_DIM
            q = rope(proj[:, 2 * C:2 * C + bw], tabs, L + "_rope_q")
            k = rope(proj[:, 2 * C + bw:2 * C + 2 * bw], tabs, L + "_rope_k")
            v = proj[:, 2 * C + 2 * bw:]
            att = _dilated_attention(q, k, v, L + "_att")
            h = lin(att, ('ab_w_out', 0), None, h, L + "_w_out_att", slice(C, None))
            return lin(rec, ('ab_w_out', 0), None, h, L + "_w_out_rec", slice(0, C))
        if stage == "mixer1":
            qw = C_HEADS * HEAD_DIM
            kw = C_KV_HEADS * HEAD_DIM
            qkv, h = norm_lin(h, S['mix_norm'][1], ('c_w_qkv', 0), S['c_b_qkv'][0], L + "_w_qkv")
            q = rope(qkv[:, :qw], tabs, L + "_rope_q")
            k = rope(qkv[:, qw:qw + kw], tabs, L + "_rope_k")
            v = qkv[:, qw + kw:]
            o = band_attention(q, k, v, _head_rows(S['c_sinks'][0]), 1, ATT_BLOCK - 1, False, L + "_att")
            return lin(o, ('c_w_out', 0), S['c_b_out'][0], h, L + "_w_out")
        if stage.startswith("xa"):
            xq, h = norm_lin(h, S['xa_norm'][layer], ('xa_wq', layer), None, L + "_xa_wq")
            xkv, _ = norm_lin(mem, S['xa_mem_norm'][layer], ('xa_wkv', layer), None, L + "_xa_wkv")
            return lin(cross_attention(xq, xkv, L + "_xa"), ('xa_wo', layer), None, h, L + "_xa_wo")
        gu, down = ('ffn_w_gate_up', layer), ('ffn_w_down', layer)
        return ffn_block(h, S['ffn_norm'][layer], Wb[gu], Cw[gu], Wb[down], Cw[down], L + "_ffn")

    return run


def kernel(x, mem, mix_norm, ab_w_in, lru_conv_w, lru_conv_b, lru_wa, lru_ba, lru_wx, lru_bx, lru_lambda, ab_w_out, c_w_qkv, c_b_qkv, c_sinks, c_w_out, c_b_out, xa_norm, xa_mem_norm, xa_wq, xa_wkv, xa_wo, ffn_norm, ffn_w_gate_up, ffn_w_down, final_norm, loss_target, m_mix_norm, m_ab_w_in, m_lru_conv_w, m_lru_conv_b, m_lru_wa, m_lru_ba, m_lru_wx, m_lru_bx, m_lru_lambda, m_ab_w_out, m_c_w_qkv, m_c_b_qkv, m_c_sinks, m_c_w_out, m_c_b_out, m_xa_norm, m_xa_mem_norm, m_xa_wq, m_xa_wkv, m_xa_wo, m_ffn_norm, m_ffn_w_gate_up, m_ffn_w_down, m_final_norm, v_mix_norm, v_ab_w_in, v_lru_conv_w, v_lru_conv_b, v_lru_wa, v_lru_ba, v_lru_wx, v_lru_bx, v_lru_lambda, v_ab_w_out, v_c_w_qkv, v_c_b_qkv, v_c_sinks, v_c_w_out, v_c_b_out, v_xa_norm, v_xa_mem_norm, v_xa_wq, v_xa_wkv, v_xa_wo, v_ffn_norm, v_ffn_w_gate_up, v_ffn_w_down, v_final_norm):
    w_loc = dict(zip(WEIGHT_NAMES, (mix_norm, ab_w_in, lru_conv_w, lru_conv_b, lru_wa, lru_ba, lru_wx, lru_bx, lru_lambda, ab_w_out, c_w_qkv, c_b_qkv, c_sinks, c_w_out, c_b_out, xa_norm, xa_mem_norm, xa_wq, xa_wkv, xa_wo, ffn_norm, ffn_w_gate_up, ffn_w_down, final_norm)))
    m_loc = dict(zip(WEIGHT_NAMES, (m_mix_norm, m_ab_w_in, m_lru_conv_w, m_lru_conv_b, m_lru_wa, m_lru_ba, m_lru_wx, m_lru_bx, m_lru_lambda, m_ab_w_out, m_c_w_qkv, m_c_b_qkv, m_c_sinks, m_c_w_out, m_c_b_out, m_xa_norm, m_xa_mem_norm, m_xa_wq, m_xa_wkv, m_xa_wo, m_ffn_norm, m_ffn_w_gate_up, m_ffn_w_down, m_final_norm)))
    v_loc = dict(zip(WEIGHT_NAMES, (v_mix_norm, v_ab_w_in, v_lru_conv_w, v_lru_conv_b, v_lru_wa, v_lru_ba, v_lru_wx, v_lru_bx, v_lru_lambda, v_ab_w_out, v_c_w_qkv, v_c_b_qkv, v_c_sinks, v_c_w_out, v_c_b_out, v_xa_norm, v_xa_mem_norm, v_xa_wq, v_xa_wkv, v_xa_wo, v_ffn_norm, v_ffn_w_gate_up, v_ffn_w_down, v_final_norm)))

    me = 4 * lax.axis_index("x") + 2 * lax.axis_index("y") + lax.axis_index("c")

    first_keys = list(STAGES[0][2])
    keys = [key for _, _, stage_keys in STAGES[1:] for key in stage_keys]
    shards = [_shard_view(n, w_loc[n])[l].astype(BF16) for n, l in keys]
    first_g = _all_gather([_pack([w_loc[n] for n in SMALL], F32)[None]]
                          + [_shard_view(n, w_loc[n])[l].astype(BF16)[None] for n, l in first_keys], "gather_first")
    gather = _copies_start(shards, False, "gather_start", thru=[first_g[0]])
    small_g = gather[4][0][0]
    Wb = {key: _full_from_gathered(g[0], _layer_shard_axis(key[0])) for key, g in zip(first_keys, first_g[1:])}
    S = {n: w_loc[n] for n in REPLICATED}
    for n, t in zip(SMALL, _unpack(small_g, [w_loc[n].shape for n in SMALL], lead=(N_DEV,))):
        S[n] = _full_from_gathered(t, SHARD_AXIS[n])

    tabs = _rope_tables(x.shape[1])
    w_in = Wb[first_keys[0]]
    hn0 = _rms_fwd_call(x[0], S['mix_norm'][0], "l0_w_in_norm", BF16)
    h = (x[0], _mm(hn0, w_in, tb=True, name="l0_w_in", tm=1024, tn=1408, tk=1024))
    vjps = []
    for stage, small_names, stage_keys in STAGES[1:]:
        which = [keys.index(key) for key in stage_keys]
        _, lands = _copies_wait(gather, which, False, jax.tree.leaves(h)[-1], "gather_wait_" + stage)
        for e, land in zip(which, lands):
            Wb[keys[e]] = _full_from_gathered(_with_own_block(land, shards[e]), _layer_shard_axis(keys[e][0]))
        carriers = {key: jnp.zeros(Wb[key].shape, F32) for key in stage_keys}
        h, vjp_fn = jax.vjp(_stage_fn(stage, Wb, tabs, mem[0]), {n: S[n] for n in small_names}, carriers, h)
        vjps.append(vjp_fn)
    loss_part, dh, dg_final = _final_call(h, S['final_norm'], loss_target[0], "final_loss")

    grads = {'final_norm': dg_final.reshape(final_norm.shape)}
    exchanges, send_keys, send_parts = [], [], []

    def start_exchange(stage, dh):
        leaves, tree = jax.tree.flatten(dh)
        started = _copies_start(list(send_parts), True, "grad_start_" + stage, thru=leaves)
        exchanges.append((stage, started, list(send_keys)))
        send_keys.clear()
        send_parts.clear()
        return jax.tree.unflatten(tree, started[4])

    for (stage, small_names, stage_keys), vjp_fn in zip(reversed(STAGES[1:]), reversed(vjps)):
        g_small, g_big, dh = vjp_fn(dh)
        for n in small_names:
            grads[n] = grads[n] + g_small[n] if n in grads else g_small[n]
        send_keys += list(stage_keys)
        send_parts += [_parts_from_full(g_big[key], _layer_shard_axis(key[0])) for key in stage_keys]
        if stage == "xa1":
            continue
        if stage == "mixer0":
            small_parts = [_parts_from_full(grads[n], SHARD_AXIS[n]) for n in SMALL]
            send_keys.append("small")
            send_parts.append(jnp.stack([_pack([p[j] for p in small_parts], F32) for j in range(N_DEV)]))
        dh = start_exchange(stage, dh)
    d_res, d_proj = dh
    send_keys.append(first_keys[0])
    send_parts.append(_parts_from_full(_mm(d_proj, hn0, ta=True, name="l0_w_in_dw", tm=1408, tn=1024, tk=1024),
                                       _layer_shard_axis(first_keys[0][0])))
    d_res, d_proj = start_exchange("proj0", (d_res, d_proj))
    dx, dg0 = _mm_rms_bwd(d_proj, w_in, x[0], S['mix_norm'][0], d_res, "l0_w_in_da", wt=True)
    grads['mix_norm'] = grads['mix_norm'] + jnp.concatenate([dg0, jnp.zeros_like(dg0)], axis=0)
    rep_names = REPLICATED + ["loss"]
    grads["loss"] = loss_part
    zero = jnp.zeros((1, 1), F32)
    for d in (w_loc, m_loc, v_loc):
        d["loss"] = zero
    rep_started = _copies_start([_pack([grads[n] for n in rep_names], F32)], False, "rep_grads_start")

    parts, out = {}, {}

    def end_exchange(stage, started, ex_keys, after):
        srcs, lands = _copies_wait(started, list(range(len(ex_keys))), True, after, "grad_wait_" + stage)
        for key, src, land in zip(ex_keys, srcs, lands):
            parts[key] = _with_own_block(land, lax.dynamic_index_in_dim(src, me, 0, keepdims=False))

    def adamw(p, names, call_name):
        if len(names) == 1:
            n = names[0]
            res = _adamw_call(p, _shard_view(n, w_loc[n]), _shard_view(n, m_loc[n]), _shard_view(n, v_loc[n]), call_name)
            for kind, t in zip(("grad", "delta", "new_m", "new_v"), res):
                out[kind, n] = _shard_view(n, t)
        else:
            res = _adamw_call(p, *[_pack([d[n] for n in names], F32) for d in (w_loc, m_loc, v_loc)], call_name)
            for kind, buf in zip(("grad", "delta", "new_m", "new_v"), res):
                for n, t in zip(names, _unpack(buf, [w_loc[n].shape for n in names])):
                    out[kind, n] = t

    for ex in exchanges[:-1]:
        end_exchange(*ex, dx)
    last_names = {key[0] for key in exchanges[-1][2]}
    for n in BIG:
        if n not in last_names:
            adamw([parts[n, l] for l in range(w_loc[n].shape[0])], [n], "adamw_" + n)
    adamw(parts["small"], SMALL, "adamw_small")
    end_exchange(*exchanges[-1], [out["new_v", n] for n in BIG if n not in last_names])
    for n in BIG:
        if n in last_names:
            adamw([parts[n, l] for l in range(w_loc[n].shape[0])], [n], "adamw_" + n)
    rep_src, rep_land = _copies_wait(rep_started, [0], False, out["new_v", "ab_w_in"], "rep_grads_wait")
    adamw(_with_own_block(rep_land[0], rep_src[0]), rep_names, "adamw_replicated")
    loss = out["grad", "loss"][0, 0]

    return (loss, dx[None], *[out[kind, n] for kind in ("grad", "delta", "new_m", "new_v") for n in WEIGHT_NAMES])
```

```python
import functools
import math

import jax
import jax.numpy as jnp
from jax import lax
from jax.experimental import pallas as pl
from jax.experimental.pallas import tpu as pltpu

F32 = jnp.float32
BF16 = jnp.bfloat16
SDS = jax.ShapeDtypeStruct
BS = pl.BlockSpec

N_DEV = 8
NORM_EPS = 1e-6
ROPE_THETA = 500000.0
HEAD_DIM = 64
ROT_DIM = 16
ATT_BLOCK = 128
LRU_C = 8.0
LRU_HEADS = 4
DILATIONS = (1, 4, 16)
B_HEADS = 8
C_HEADS = 16
C_KV_HEADS = 2
XA_HEADS = 4
XA_HEAD_DIM = 128
NEG = -1e30
ADAM_LR, ADAM_B1, ADAM_B2, ADAM_EPS, ADAM_WD, ADAM_STEP = 0.001, 0.9, 0.999, 1e-08, 0.01, 10
LANES = 128
VMEM_LIMIT = 48 * 1024 * 1024

WEIGHT_NAMES = ['mix_norm', 'ab_w_in', 'lru_conv_w', 'lru_conv_b', 'lru_wa', 'lru_ba', 'lru_wx', 'lru_bx',
                'lru_lambda', 'ab_w_out', 'c_w_qkv', 'c_b_qkv', 'c_sinks', 'c_w_out', 'c_b_out', 'xa_norm',
                'xa_mem_norm', 'xa_wq', 'xa_wkv', 'xa_wo', 'ffn_norm', 'ffn_w_gate_up', 'ffn_w_down', 'final_norm']
SHARD_AXIS = {'ab_w_in': 2, 'lru_conv_w': 2, 'lru_wa': 2, 'lru_ba': 2, 'lru_wx': 2, 'lru_bx': 2, 'ab_w_out': 1,
              'c_w_qkv': 2, 'c_b_qkv': 1, 'c_w_out': 1, 'c_b_out': 1, 'xa_wq': 1, 'xa_wkv': 1, 'xa_wo': 2,
              'ffn_w_gate_up': 2, 'ffn_w_down': 1}
BIG = ['ab_w_in', 'lru_wa', 'lru_wx', 'ab_w_out', 'c_w_qkv', 'c_w_out', 'xa_wq', 'xa_wkv', 'xa_wo',
       'ffn_w_gate_up', 'ffn_w_down']
SMALL = ['lru_conv_w', 'lru_ba', 'lru_bx', 'c_b_qkv', 'c_b_out']
SHARDED = BIG + SMALL
REPLICATED = [n for n in WEIGHT_NAMES if n not in SHARD_AXIS]
COLUMN_CUT = ('ab_w_in', 'c_w_qkv', 'ffn_w_gate_up')
PACK_QUANTUM = 2048


def _shard_view(name, t):
    return jnp.swapaxes(t, -1, -2) if name in COLUMN_CUT else t


def _layer_shard_axis(name):
    return 0 if name in COLUMN_CUT else SHARD_AXIS[name] - 1


def _pcall(body, **kw):
    return pl.pallas_call(body, **kw)


def _cparams(sem=None):
    return pltpu.CompilerParams(dimension_semantics=sem, vmem_limit_bytes=VMEM_LIMIT)


def _tile(n, target, mult=LANES):
    if n <= target:
        return n
    t = (target // mult) * mult
    while t >= mult:
        if n % t == 0:
            return t
        t -= mult
    return n


def _sigmoid(x):
    return 1.0 / (1.0 + jnp.exp(-x))


def _expm1(x):
    small = x * (1.0 + x * (0.5 + x * (1.0 / 6.0 + x * (1.0 / 24.0))))
    return jnp.where(jnp.abs(x) < 0.03, small, jnp.exp(x) - 1.0)


_GELU_C = math.sqrt(2.0 / math.pi)


def _gelu_parts(y):
    y2 = y * y
    th = jnp.tanh(_GELU_C * (y + 0.044715 * y * y2))
    g = 0.5 * y * (1.0 + th)
    dg = 0.5 * (1.0 + th) + 0.5 * y * (1.0 - th * th) * _GELU_C * (1.0 + 3.0 * 0.044715 * y2)
    return g, dg


def _rotate(xv, tab_refs, inverse=False):
    W = xv.shape[1]
    half = ROT_DIM // 2
    c, sa, sb = (jnp.tile(t[...], (1, W // LANES)) for t in tab_refs)
    if not inverse:
        return xv * c + pltpu.roll(xv, half, axis=1) * sa + pltpu.roll(xv, W - half, axis=1) * sb
    return xv * c + pltpu.roll(xv * sa, W - half, axis=1) + pltpu.roll(xv * sb, half, axis=1)


def _mm(a, b, *, ta=False, tb=False, bias=None, add=None, rope=None, name, tm=512, tn=512, tk=2048):
    M, K = (a.shape[1], a.shape[0]) if ta else a.shape
    N = b.shape[0] if tb else b.shape[1]
    tm, tn, tk = _tile(M, tm), _tile(N, tn), _tile(K, tk)
    nk = K // tk
    dn = (((0 if ta else 1,), (1 if tb else 0,)), ((), ()))

    def body(*refs):
        a_ref, b_ref = refs[0], refs[1]
        pos = 2
        bias_ref = add_ref = None
        if bias is not None:
            bias_ref = refs[pos]
            pos += 1
        if add is not None:
            add_ref = refs[pos]
            pos += 1
        tab_refs = refs[pos:pos + 3] if rope is not None else None
        pos += 3 if rope is not None else 0
        o_ref = refs[pos]
        part = lax.dot_general(a_ref[...].astype(BF16), b_ref[...].astype(BF16), dn, preferred_element_type=F32)

        def finish(r):
            if bias_ref is not None:
                r = r + bias_ref[...]
            if add_ref is not None:
                r = r + add_ref[...]
            o_ref[...] = r if tab_refs is None else _rotate(r, tab_refs)

        if nk == 1:
            finish(part)
            return
        acc_ref = refs[pos + 1]
        k = pl.program_id(2)

        @pl.when(k == 0)
        def _():
            acc_ref[...] = part

        @pl.when((k > 0) & (k < nk - 1))
        def _():
            acc_ref[...] += part

        @pl.when(k == nk - 1)
        def _():
            finish(acc_ref[...] + part)

    in_specs = [BS((tk, tm), lambda i, j, k: (k, i)) if ta else BS((tm, tk), lambda i, j, k: (i, k)),
                BS((tn, tk), lambda i, j, k: (j, k)) if tb else BS((tk, tn), lambda i, j, k: (k, j))]
    args = [a, b]
    if bias is not None:
        in_specs.append(BS((1, tn), lambda i, j, k: (0, j)))
        args.append(bias.reshape(1, N))
    if add is not None:
        in_specs.append(BS((tm, tn), lambda i, j, k: (i, j)))
        args.append(add)
    if rope is not None:
        in_specs += [BS((tm, LANES), lambda i, j, k: (i, 0))] * 3
        args += list(rope)
    return _pcall(body, name=name, grid=(M // tm, N // tn, nk), in_specs=in_specs,
                  out_specs=BS((tm, tn), lambda i, j, k: (i, j)), out_shape=SDS((M, N), F32),
                  scratch_shapes=[pltpu.VMEM((tm, tn), F32)] if nk > 1 else [],
                  compiler_params=_cparams(("parallel", "parallel", "arbitrary")))(*args)


def _mm_rms_bwd(dy, wb, x, g, add, name, wt=False):
    T, Kc = dy.shape
    Dm = wb.shape[1] if wt else wb.shape[0]
    tm, tk = _tile(T, 1024), _tile(Kc, 1408)
    nk = Kc // tk
    dn = (((1,), (0 if wt else 1,)), ((), ()))

    def body(*refs):
        dy_ref, w_ref, x_ref, g_ref = refs[:4]
        dx_ref, dg_ref, acc_ref = refs[-3:]
        i, k = pl.program_id(0), pl.program_id(1)
        part = lax.dot_general(dy_ref[...].astype(BF16), w_ref[...], dn, preferred_element_type=F32)

        @pl.when(k == 0)
        def _():
            acc_ref[...] = part

        @pl.when((k > 0) & (k < nk - 1))
        def _():
            acc_ref[...] += part

        @pl.when((i == 0) & (k == 0))
        def _():
            dg_ref[...] = jnp.zeros_like(dg_ref)

        @pl.when(k == nk - 1)
        def _():
            d = part if nk == 1 else acc_ref[...] + part
            xv = x_ref[...]
            r = lax.rsqrt(jnp.mean(xv * xv, axis=-1, keepdims=True) + NORM_EPS)
            xh = xv * r
            dyg = d * g_ref[...]
            dx = r * (dyg - xh * jnp.mean(dyg * xh, axis=-1, keepdims=True))
            dx_ref[...] = dx if add is None else dx + refs[4][...]
            dg_ref[...] += jnp.sum(d * xh, axis=0, keepdims=True)

    row = BS((tm, Dm), lambda i, k: (i, 0))
    vec = BS((1, Dm), lambda i, k: (0, 0))
    extra = [] if add is None else [add]
    return _pcall(body, name=name, grid=(T // tm, nk),
                  in_specs=[BS((tm, tk), lambda i, k: (i, k)),
                            BS((tk, Dm), lambda i, k: (k, 0)) if wt else BS((Dm, tk), lambda i, k: (0, k)), row, vec]
                  + [row] * len(extra),
                  out_specs=[row, vec], out_shape=[SDS((T, Dm), F32), SDS((1, Dm), F32)],
                  scratch_shapes=[pltpu.VMEM((tm, Dm), F32)],
                  compiler_params=_cparams(("arbitrary", "arbitrary")))(dy, wb, x, g.reshape(1, Dm), *extra)


def _colsum(x, name):
    T, N = x.shape
    tt = _tile(T, 512, 8)

    def body(x_ref, o_ref):
        @pl.when(pl.program_id(0) == 0)
        def _():
            o_ref[...] = jnp.zeros_like(o_ref)

        o_ref[...] += jnp.sum(x_ref[...], axis=0, keepdims=True)

    return _pcall(body, name=name, grid=(T // tt,), in_specs=[BS((tt, N), lambda i: (i, 0))],
                  out_specs=BS((1, N), lambda i: (0, 0)), out_shape=SDS((1, N), F32),
                  compiler_params=_cparams(("arbitrary",)))(x)


@functools.partial(jax.custom_vjp, nondiff_argnums=(5,))
def linear(a, wb, wc, bias, add, name):
    return _mm(a, wb, bias=bias, add=add, name=name)


def _linear_fwd(a, wb, wc, bias, add, name):
    return _mm(a, wb, bias=bias, add=add, name=name), (a, wb, bias is not None, add is not None)


def _linear_bwd(name, res, g):
    a, wb, has_bias, has_add = res
    da = _mm(g, wb, tb=True, name=name + "_da")
    dw = _mm(a, g, ta=True, name=name + "_dw")
    dbias = _colsum(g, name + "_db").reshape(-1) if has_bias else None
    return da, jnp.zeros_like(wb), dw, dbias, (g if has_add else None)


linear.defvjp(_linear_fwd, _linear_bwd)


def _rms_fwd_call(x, g, name, out_dtype=F32):
    T, Dm = x.shape
    tt = _tile(T, 512, 16)

    def body(x_ref, g_ref, o_ref):
        xv = x_ref[...]
        r = lax.rsqrt(jnp.mean(xv * xv, axis=-1, keepdims=True) + NORM_EPS)
        o_ref[...] = ((xv * r) * g_ref[...]).astype(out_dtype)

    return _pcall(body, name=name, grid=(T // tt,),
                  in_specs=[BS((tt, Dm), lambda i: (i, 0)), BS((1, Dm), lambda i: (0, 0))],
                  out_specs=BS((tt, Dm), lambda i: (i, 0)), out_shape=SDS((T, Dm), out_dtype),
                  compiler_params=_cparams(("parallel",)))(x, g.reshape(1, Dm))


@functools.partial(jax.custom_vjp, nondiff_argnums=(5, 6))
def norm_linear(x, g, wb, wc, bias, wt, name):
    return _norm_linear_fwd(x, g, wb, wc, bias, wt, name)[0]


def _norm_linear_fwd(x, g, wb, wc, bias, wt, name):
    hn = _rms_fwd_call(x, g, name + "_norm", BF16)
    return (_mm(hn, wb, tb=wt, bias=bias, name=name, tm=1024, tn=1408, tk=1024), x), (x, g, hn, wb, bias is not None)


def _norm_linear_bwd(wt, name, res, cts):
    x, g, hn, wb, has_bias = res
    dy, dres = cts
    if wt:
        dw = _mm(dy, hn, ta=True, name=name + "_dw", tm=1408, tn=1024, tk=1024)
    else:
        dw = _mm(hn, dy, ta=True, name=name + "_dw", tm=1024, tn=1408, tk=1024)
    dx, dg = _mm_rms_bwd(dy, wb, x, g, dres, name + "_da", wt)
    dbias = _colsum(dy, name + "_db").reshape(-1) if has_bias else None
    return dx, dg.reshape(g.shape), jnp.zeros_like(wb), dw, dbias


norm_linear.defvjp(_norm_linear_fwd, _norm_linear_bwd)


def _rope_tables(T):
    half = ROT_DIM // 2
    inv = ROPE_THETA ** (-jnp.arange(0, ROT_DIM, 2, dtype=F32) / ROT_DIM)
    lane = jnp.arange(LANES) % HEAD_DIM
    freq = jnp.where(lane < ROT_DIM, inv[lane % half], 0.0)
    ang = jnp.arange(T, dtype=F32)[:, None] * freq[None, :]
    c, s = jnp.cos(ang), jnp.sin(ang)
    sa = jnp.where((lane >= half) & (lane < ROT_DIM), s, 0.0)
    sb = jnp.where(lane < half, -s, 0.0)
    return c, sa, sb


def _rope_call(x, tabs, inverse, name):
    T, W = x.shape
    tt = _tile(T, 512, 8)

    def body(x_ref, c_ref, sa_ref, sb_ref, o_ref):
        o_ref[...] = _rotate(x_ref[...], (c_ref, sa_ref, sb_ref), inverse)

    row = BS((tt, W), lambda i: (i, 0))
    tab = BS((tt, LANES), lambda i: (i, 0))
    return _pcall(body, name=name, grid=(T // tt,), in_specs=[row, tab, tab, tab], out_specs=row,
                  out_shape=SDS((T, W), F32), compiler_params=_cparams(("parallel",)))(x, *tabs)


def _join_pieces_call(pieces, rotated, tabs, name):
    T = pieces[0].shape[0]
    widths = [p.shape[1] for p in pieces]
    tt = _tile(T, 256, 8)

    def body(*refs):
        tab_refs, o_ref = refs[len(pieces):len(pieces) + 3], refs[-1]
        off = 0
        for p_ref, w, r in zip(refs, widths, rotated):
            o_ref[:, off:off + w] = _rotate(p_ref[...], tab_refs, inverse=True) if r else p_ref[...]
            off += w

    return _pcall(body, name=name, grid=(T // tt,),
                  in_specs=[BS((tt, w), lambda i: (i, 0)) for w in widths] + [BS((tt, LANES), lambda i: (i, 0))] * 3,
                  out_specs=BS((tt, sum(widths)), lambda i: (i, 0)), out_shape=SDS((T, sum(widths)), F32),
                  compiler_params=_cparams(("parallel",)))(*pieces, *tabs)


@functools.partial(jax.custom_vjp, nondiff_argnums=(2,))
def rope(x, tabs, name):
    return _rope_call(x, tabs, False, name)


def _rope_fwd(x, tabs, name):
    return _rope_call(x, tabs, False, name), tabs


def _rope_bwd(name, tabs, dy):
    return _rope_call(dy, tabs, True, name + "_bwd"), jax.tree.map(jnp.zeros_like, tabs)


rope.defvjp(_rope_fwd, _rope_bwd)


def _conv_fwd_call(x, w, b, name):
    T, C = x.shape
    tt = _tile(T, 512, 8)
    per = tt // 8

    def body(x_ref, halo_ref, w_ref, b_ref, o_ref):
        i = pl.program_id(0)
        halo = jnp.where(i > 0, halo_ref[...], 0.0)
        e = jnp.concatenate([halo, x_ref[...]], axis=0)
        acc = b_ref[...]
        for k in (3, 2, 1):
            acc = acc + pltpu.roll(e, k, axis=0)[8:, :] * w_ref[3 - k:4 - k, :]
        o_ref[...] = acc + x_ref[...] * w_ref[3:4, :]

    return _pcall(body, name=name, grid=(T // tt,),
                  in_specs=[BS((tt, C), lambda i: (i, 0)), BS((8, C), lambda i: (jnp.maximum(i * per - 1, 0), 0)),
                            BS((4, C), lambda i: (0, 0)), BS((1, C), lambda i: (0, 0))],
                  out_specs=BS((tt, C), lambda i: (i, 0)), out_shape=SDS((T, C), F32),
                  compiler_params=_cparams(("parallel",)))(x, x, w, b.reshape(1, C))


def _conv_bwd_call(x, w, dy, name):
    T, C = x.shape
    tt = _tile(T, 512, 8)
    per = tt // 8
    nt = T // tt

    def body(x_ref, halo_ref, w_ref, dy_ref, nxt_ref, dx_ref, dwb_ref):
        i = pl.program_id(0)
        halo = jnp.where(i > 0, halo_ref[...], 0.0)
        e = jnp.concatenate([halo, x_ref[...]], axis=0)
        dy = dy_ref[...]
        nxt = jnp.where(i < nt - 1, nxt_ref[...], 0.0)
        f = jnp.concatenate([dy, nxt], axis=0)
        dx = dy * w_ref[3:4, :]
        rows = [None] * 4
        rows[3] = jnp.sum(dy * x_ref[...], axis=0, keepdims=True)
        for k in (1, 2, 3):
            dx = dx + pltpu.roll(f, tt + 8 - k, axis=0)[:tt, :] * w_ref[3 - k:4 - k, :]
            rows[3 - k] = jnp.sum(dy * pltpu.roll(e, k, axis=0)[8:, :], axis=0, keepdims=True)
        dx_ref[...] = dx
        upd = jnp.concatenate(rows + [jnp.sum(dy, axis=0, keepdims=True), jnp.zeros((3, C), F32)], axis=0)

        @pl.when(i == 0)
        def _():
            dwb_ref[...] = jnp.zeros_like(dwb_ref)

        dwb_ref[...] += upd

    row = BS((tt, C), lambda i: (i, 0))
    return _pcall(body, name=name, grid=(nt,),
                  in_specs=[row, BS((8, C), lambda i: (jnp.maximum(i * per - 1, 0), 0)), BS((4, C), lambda i: (0, 0)),
                            row, BS((8, C), lambda i: (jnp.minimum((i + 1) * per, T // 8 - 1), 0))],
                  out_specs=[row, BS((8, C), lambda i: (0, 0))],
                  out_shape=[SDS((T, C), F32), SDS((8, C), F32)],
                  compiler_params=_cparams(("arbitrary",)))(x, x, w, dy, dy)


@functools.partial(jax.custom_vjp, nondiff_argnums=(3,))
def conv4(x, w, b, name):
    return _conv_fwd_call(x, w, b, name)


def _conv4_fwd(x, w, b, name):
    return _conv_fwd_call(x, w, b, name), (x, w)


def _conv4_bwd(name, res, dy):
    x, w = res
    dx, dwb = _conv_bwd_call(x, w, dy, name + "_bwd")
    return dx, dwb[0:4], dwb[4]


conv4.defvjp(_conv4_fwd, _conv4_bwd)


def _gates_fwd_call(xc, wa, ba, wx, bx, name):
    T, C = xc.shape
    hd = C // LRU_HEADS
    tt = _tile(T, 512, 8)

    def body(x_ref, wa_ref, ba_ref, wx_ref, bx_ref, ga_ref, gx_ref):
        xb = x_ref[...].astype(BF16)
        ga_ref[...] = jnp.dot(xb, wa_ref[0].astype(BF16), preferred_element_type=F32) + ba_ref[...]
        gx_ref[...] = jnp.dot(xb, wx_ref[0].astype(BF16), preferred_element_type=F32) + bx_ref[...]

    blk = BS((tt, hd), lambda i, h: (i, h))
    wsp = BS((1, hd, hd), lambda i, h: (h, 0, 0))
    bsp = BS((1, hd), lambda i, h: (0, h))
    return _pcall(body, name=name, grid=(T // tt, LRU_HEADS), in_specs=[blk, wsp, bsp, wsp, bsp],
                  out_specs=[blk, blk], out_shape=[SDS((T, C), F32)] * 2,
                  compiler_params=_cparams(("parallel", "parallel")))(xc, wa, ba.reshape(1, C), wx, bx.reshape(1, C))


def _gates_bwd_x_call(dga, dgx, wa, wx, name):
    T, C = dga.shape
    hd = C // LRU_HEADS
    tt = _tile(T, 512, 8)
    dn = (((1,), (1,)), ((), ()))

    def body(da_ref, dx_ref, wa_ref, wx_ref, o_ref):
        o_ref[...] = (lax.dot_general(da_ref[...].astype(BF16), wa_ref[0].astype(BF16), dn, preferred_element_type=F32)
                      + lax.dot_general(dx_ref[...].astype(BF16), wx_ref[0].astype(BF16), dn, preferred_element_type=F32))

    blk = BS((tt, hd), lambda i, h: (i, h))
    wsp = BS((1, hd, hd), lambda i, h: (h, 0, 0))
    return _pcall(body, name=name, grid=(T // tt, LRU_HEADS), in_specs=[blk, blk, wsp, wsp], out_specs=blk,
                  out_shape=SDS((T, C), F32), compiler_params=_cparams(("parallel", "parallel")))(dga, dgx, wa, wx)


def _gates_bwd_w_call(xc, dga, dgx, name):
    T, C = xc.shape
    hd = C // LRU_HEADS
    tt = _tile(T, 512, 8)
    dn = (((0,), (0,)), ((), ()))

    def body(x_ref, da_ref, dx_ref, dwa_ref, dwx_ref, dba_ref, dbx_ref):
        @pl.when(pl.program_id(1) == 0)
        def _():
            dwa_ref[...] = jnp.zeros_like(dwa_ref)
            dwx_ref[...] = jnp.zeros_like(dwx_ref)
            dba_ref[...] = jnp.zeros_like(dba_ref)
            dbx_ref[...] = jnp.zeros_like(dbx_ref)

        xb = x_ref[...].astype(BF16)
        da, dx = da_ref[...], dx_ref[...]
        dwa_ref[0] += lax.dot_general(xb, da.astype(BF16), dn, preferred_element_type=F32)
        dwx_ref[0] += lax.dot_general(xb, dx.astype(BF16), dn, preferred_element_type=F32)
        dba_ref[...] += jnp.sum(da, axis=0, keepdims=True)
        dbx_ref[...] += jnp.sum(dx, axis=0, keepdims=True)

    blk = BS((tt, hd), lambda h, i: (i, h))
    wsp = BS((1, hd, hd), lambda h, i: (h, 0, 0))
    bsp = BS((1, hd), lambda h, i: (0, h))
    return _pcall(body, name=name, grid=(LRU_HEADS, T // tt), in_specs=[blk, blk, blk],
                  out_specs=[wsp, wsp, bsp, bsp],
                  out_shape=[SDS((LRU_HEADS, hd, hd), F32)] * 2 + [SDS((1, C), F32)] * 2,
                  compiler_params=_cparams(("parallel", "arbitrary")))(xc, dga, dgx)


@functools.partial(jax.custom_vjp, nondiff_argnums=(7,))
def lru_gates(xc, wa, wa_c, ba, wx, wx_c, bx, name):
    return tuple(_gates_fwd_call(xc, wa, ba, wx, bx, name))


def _lru_gates_fwd(xc, wa, wa_c, ba, wx, wx_c, bx, name):
    return tuple(_gates_fwd_call(xc, wa, ba, wx, bx, name)), (xc, wa, wx, ba.shape)


def _lru_gates_bwd(name, res, g):
    xc, wa, wx, bshape = res
    dga, dgx = g
    dxc = _gates_bwd_x_call(dga, dgx, wa, wx, name + "_dx")
    dwa, dwx, dba, dbx = _gates_bwd_w_call(xc, dga, dgx, name + "_dw")
    return dxc, jnp.zeros_like(wa), dwa, dba.reshape(bshape), jnp.zeros_like(wx), dwx, dbx.reshape(bshape)


lru_gates.defvjp(_lru_gates_fwd, _lru_gates_bwd)


def _lru_coeffs(xc, ga, gx, lam):
    r = _sigmoid(ga)
    ig = _sigmoid(gx)
    z = -lam
    sp = jnp.maximum(z, 0.0) + jnp.log(1.0 + jnp.exp(-jnp.abs(z)))
    la = -LRU_C * r * sp
    a = jnp.exp(la)
    s = jnp.sqrt(-_expm1(2.0 * la))
    return r, ig, sp, a, s


LRU_TT = 256


def _scan_fwd_call(xc, ga, gx, y, lam, name):
    T, C = xc.shape
    tt = _tile(T, LRU_TT, 8)

    def body(xc_ref, ga_ref, gx_ref, y_ref, lam_ref, h_ref, rec_ref, a_buf, carry):
        @pl.when(pl.program_id(0) == 0)
        def _():
            carry[...] = jnp.zeros_like(carry)

        xcv = xc_ref[...]
        _, ig, _, a, s = _lru_coeffs(xcv, ga_ref[...], gx_ref[...], lam_ref[...])
        a_buf[...] = a
        h_ref[...] = s * (ig * xcv)

        def step(t, h):
            hn = a_buf[pl.ds(t, 1), :] * h + h_ref[pl.ds(t, 1), :]
            h_ref[pl.ds(t, 1), :] = hn
            return hn

        carry[0:1, :] = lax.fori_loop(0, tt, step, carry[0:1, :], unroll=8)
        g, _ = _gelu_parts(y_ref[...])
        rec_ref[...] = h_ref[...] * g

    row = BS((tt, C), lambda i: (i, 0))
    vec = BS((1, C), lambda i: (0, 0))
    return _pcall(body, name=name, grid=(T // tt,), in_specs=[row, row, row, row, vec], out_specs=[row, row],
                  out_shape=[SDS((T, C), F32)] * 2,
                  scratch_shapes=[pltpu.VMEM((tt, C), F32), pltpu.VMEM((8, C), F32)],
                  compiler_params=_cparams(("arbitrary",)))(xc, ga, gx, y, lam.reshape(1, C))


def _scan_bwd_call(xc, ga, gx, y, lam, h, drec, name):
    T, C = xc.shape
    tt = _tile(T, LRU_TT, 8)
    nt = T // tt
    per = tt // 8

    def body(xc_ref, ga_ref, gx_ref, y_ref, lam_ref, h_ref, halo_ref, dr_ref,
             dga_ref, dgx_ref, dxc_ref, dy_ref, dlam_ref, a_buf, g_buf, carry):
        i = pl.program_id(0)

        @pl.when(i == 0)
        def _():
            carry[...] = jnp.zeros_like(carry)
            dlam_ref[...] = jnp.zeros_like(dlam_ref)

        xcv, lam = xc_ref[...], lam_ref[...]
        r, ig, sp, a, s = _lru_coeffs(xcv, ga_ref[...], gx_ref[...], lam)
        gel, dgel = _gelu_parts(y_ref[...])
        drec = dr_ref[...]
        hv = h_ref[...]
        dy_ref[...] = drec * hv * dgel
        a_buf[...] = a
        g_buf[...] = drec * gel

        def step(j, q):
            t = tt - 1 - j
            g = g_buf[pl.ds(t, 1), :] + q
            g_buf[pl.ds(t, 1), :] = g
            return a_buf[pl.ds(t, 1), :] * g

        carry[0:1, :] = lax.fori_loop(0, tt, step, carry[0:1, :], unroll=8)
        g = g_buf[...]
        halo = jnp.where(i < nt - 1, halo_ref[...], 0.0)
        hprev = pltpu.roll(jnp.concatenate([halo, hv], axis=0), 1, axis=0)[8:, :]
        da = g * hprev
        dig = g * s * xcv
        ds = g * ig * xcv
        dla = da * a - ds * (a * a) / s
        dga_ref[...] = dla * (-LRU_C * sp) * r * (1.0 - r)
        dgx_ref[...] = dig * ig * (1.0 - ig)
        dxc_ref[...] = g * s * ig
        dlam_ref[...] += jnp.sum(dla * r, axis=0, keepdims=True) * (LRU_C * _sigmoid(-lam))

    row = BS((tt, C), lambda i: (nt - 1 - i, 0))
    vec = BS((1, C), lambda i: (0, 0))
    halo = BS((8, C), lambda i: (jnp.maximum((nt - 1 - i) * per - 1, 0), 0))
    return _pcall(body, name=name, grid=(nt,), in_specs=[row, row, row, row, vec, row, halo, row],
                  out_specs=[row, row, row, row, vec], out_shape=[SDS((T, C), F32)] * 4 + [SDS((1, C), F32)],
                  scratch_shapes=[pltpu.VMEM((tt, C), F32), pltpu.VMEM((tt, C), F32), pltpu.VMEM((8, C), F32)],
                  compiler_params=_cparams(("arbitrary",)))(xc, ga, gx, y, lam.reshape(1, C), h, h, drec)


@functools.partial(jax.custom_vjp, nondiff_argnums=(5,))
def lru_scan(xc, ga, gx, y, lam, name):
    return _scan_fwd_call(xc, ga, gx, y, lam, name)[1]


def _lru_scan_fwd(xc, ga, gx, y, lam, name):
    h, rec = _scan_fwd_call(xc, ga, gx, y, lam, name)
    return rec, (xc, ga, gx, y, lam, h)


def _lru_scan_bwd(name, res, drec):
    xc, ga, gx, y, lam, h = res
    dga, dgx, dxc, dy, dlam = _scan_bwd_call(xc, ga, gx, y, lam, h, drec, name + "_bwd")
    return dxc, dga, dgx, dy, dlam.reshape(lam.shape)


lru_scan.defvjp(_lru_scan_fwd, _lru_scan_bwd)


def _att_batch(d, shared=False):
    if shared:
        return 8, 1
    return (4, 1) if d == 1 else (1, min(d, 8))


def _att_masks(n, max_dist):
    qi = lax.broadcasted_iota(jnp.int32, (1, 2 * ATT_BLOCK, 2 * ATT_BLOCK), 1) & (ATT_BLOCK - 1)
    kj = lax.broadcasted_iota(jnp.int32, (1, 2 * ATT_BLOCK, 2 * ATT_BLOCK), 2)
    prev = (kj < ATT_BLOCK) & (kj >= qi + (ATT_BLOCK - max_dist)) & (n > 0)
    cur = (kj >= ATT_BLOCK) & (kj - ATT_BLOCK <= qi)
    return prev | cur


def _lo_lanes(rows):
    return lax.broadcasted_iota(jnp.int32, (rows, LANES), 1) < HEAD_DIM


def _lane_half(rows):
    return lax.broadcasted_iota(jnp.int32, (rows, LANES), 1) // HEAD_DIM


def _stack_heads(x2):
    lo = _lo_lanes(ATT_BLOCK)
    zero = jnp.zeros_like(x2)
    return jnp.concatenate([jnp.where(lo, x2, zero), jnp.where(lo, zero, x2)], axis=0)


def _unstack_heads(y):
    return jnp.where(_lo_lanes(ATT_BLOCK), y[:ATT_BLOCK], y[ATT_BLOCK:])


def _per_head_col(x2):
    return jnp.concatenate([x2[:, 0:1], x2[:, HEAD_DIM:HEAD_DIM + 1]], axis=0)


def _head_sums(x2):
    lo = _lo_lanes(ATT_BLOCK)
    return jnp.concatenate([jnp.sum(jnp.where(lo, x2, 0.0), axis=1, keepdims=True),
                            jnp.sum(jnp.where(lo, 0.0, x2), axis=1, keepdims=True)], axis=0)


def _att_specs(d, Wq, Wk, nb, clamp):
    shared = Wk != Wq
    cgw, sb = _att_batch(d, shared)
    cur = (lambda n: jnp.minimum(n, nb - 1)) if clamp else (lambda n: n)
    rows, qw, kw = ATT_BLOCK * d, cgw * LANES, (LANES if shared else cgw * LANES)
    kcol = (lambda g: 0) if shared else (lambda g: g)
    qsp = BS((rows, qw), lambda g, n: (cur(n), g))
    csp = BS((rows, kw), lambda g, n: (cur(n), kcol(g)))
    psp = BS((rows, kw), lambda g, n: (jnp.maximum(cur(n) - 1, 0), kcol(g)))
    return cgw, sb, shared, qsp, csp, psp, qw, kw


def _att_streams(d, sb, work):
    if d == 1:
        work([slice(None)])
        return

    def one(j, carry):
        work([pl.ds(j * sb + i, ATT_BLOCK, stride=d) for i in range(sb)])
        return carry

    lax.fori_loop(0, d // sb, one, 0)


def _att_problem_loads(rows, cgw, shared, g, q_ref, kc_ref, kp_ref, vc_ref, vp_ref, sk_ref):
    half = _lane_half(ATT_BLOCK)

    def kv(ref, r, p):
        x = ref[r, :]
        if not shared:
            return x[:, p * LANES:(p + 1) * LANES]
        return jnp.where(half == p // 4, x, pltpu.roll(x, HEAD_DIM, axis=1))

    qs, kb, vb, sk = [], [], [], []
    for r in rows:
        qrow = q_ref[r, :]
        for p in range(cgw):
            cols = slice(p * LANES, (p + 1) * LANES)
            qs.append(_stack_heads(qrow[:, cols].astype(BF16)))
            kb.append(jnp.concatenate([kv(kp_ref, r, p), kv(kc_ref, r, p)], axis=0).astype(BF16))
            vb.append(jnp.concatenate([kv(vp_ref, r, p), kv(vc_ref, r, p)], axis=0).astype(BF16))
            sk.append(_per_head_col(jnp.broadcast_to(sk_ref[:, cols], (ATT_BLOCK, LANES))))
    return jnp.stack(qs), jnp.stack(kb), jnp.stack(vb), jnp.stack(sk)


_BDOT_NT = (((2,), (2,)), ((0,), (0,)))
_BDOT_NN = (((2,), (1,)), ((0,), (0,)))
_BDOT_TN = (((1,), (1,)), ((0,), (0,)))


def _att_fwd_call(q, k, v, sinks, d, max_dist, name):
    T, Wq = q.shape
    Wk = k.shape[1]
    nb = T // (d * ATT_BLOCK)
    cgw, sb, shared, qsp, csp, psp, qw, kw = _att_specs(d, Wq, Wk, nb, False)
    G = Wq // qw
    assert not shared or (d == 1 and Wk == LANES and G == 1 and cgw == 8), "a shared kv pair serves 2 x 8 query heads"

    def body(q_ref, kc_ref, kp_ref, vc_ref, vp_ref, sk_ref, o_ref, lse_ref):
        g, n = pl.program_id(0), pl.program_id(1)

        def work(rows):
            qs, kband, vband, sk = _att_problem_loads(rows, cgw, shared, g, q_ref, kc_ref, kp_ref, vc_ref, vp_ref, sk_ref)
            s = lax.dot_general(qs, kband, _BDOT_NT, preferred_element_type=F32) * (HEAD_DIM ** -0.5)
            s = jnp.where(_att_masks(n, max_dist), s, NEG)
            m = jnp.maximum(jnp.max(s, axis=-1, keepdims=True), sk)
            e = jnp.exp(s - m)
            den = jnp.sum(e, axis=-1, keepdims=True) + jnp.exp(sk - m)
            o = lax.dot_general((e * (1.0 / den)).astype(BF16), vband, _BDOT_NN, preferred_element_type=F32)
            lse = jnp.broadcast_to(m + jnp.log(den), o.shape)
            for i, r in enumerate(rows):
                o_ref[r, :] = jnp.concatenate([_unstack_heads(o[i * cgw + p]) for p in range(cgw)], axis=1)
                lse_ref[r, :] = jnp.concatenate([_unstack_heads(lse[i * cgw + p]) for p in range(cgw)], axis=1)

        _att_streams(d, sb, work)

    sksp = BS((1, qw), lambda g, n: (0, g))
    return _pcall(body, name=name, grid=(G, nb), in_specs=[qsp, csp, psp, csp, psp, sksp], out_specs=[qsp, qsp],
                  out_shape=[SDS((T, Wq), F32)] * 2,
                  compiler_params=_cparams(("parallel", "parallel")))(q, k, k, v, v, sinks)


def _att_bwd_call(q, k, v, sinks, o, lse, do, dlse, d, max_dist, name):
    T, Wq = q.shape
    Wk = k.shape[1]
    nb = T // (d * ATT_BLOCK)
    cgw, sb, shared, qsp, csp, psp, qw, kw = _att_specs(d, Wq, Wk, nb, True)
    G = Wq // qw
    scale = HEAD_DIM ** -0.5

    def body(*refs):
        q_ref, kc_ref, kp_ref, vc_ref, vp_ref, sk_ref, o_ref, lse_ref, do_ref = refs[:9]
        dlse_ref = refs[9] if dlse is not None else None
        dq_ref, dk_ref, dv_ref, dsk_ref, ck, cv = refs[-6:]
        g, n = pl.program_id(0), pl.program_id(1)

        @pl.when(n == 0)
        def _():
            ck[...] = jnp.zeros_like(ck)
            cv[...] = jnp.zeros_like(cv)
            dsk_ref[...] = jnp.zeros_like(dsk_ref)

        def work(rows):
            qs, kband, vband, sk = _att_problem_loads(rows, cgw, shared, g, q_ref, kc_ref, kp_ref, vc_ref, vp_ref, sk_ref)
            dos, lse_c, corr = [], [], []
            for r in rows:
                do_r, o_r, lse_r = do_ref[r, :], o_ref[r, :], lse_ref[r, :]
                dlse_r = dlse_ref[r, :] if dlse_ref is not None else None
                for p in range(cgw):
                    cols = slice(p * LANES, (p + 1) * LANES)
                    dos.append(_stack_heads(do_r[:, cols].astype(BF16)))
                    lse_c.append(_per_head_col(lse_r[:, cols]))
                    delta = _head_sums(do_r[:, cols] * o_r[:, cols])
                    corr.append(-delta if dlse_r is None else _head_sums(dlse_r[:, cols]) - delta)
            dos, lse_c, corr = jnp.stack(dos), jnp.stack(lse_c), jnp.stack(corr)
            s = lax.dot_general(qs, kband, _BDOT_NT, preferred_element_type=F32) * scale
            pr = jnp.exp(jnp.where(_att_masks(n, max_dist), s, NEG) - lse_c)
            dp = lax.dot_general(dos, vband, _BDOT_NT, preferred_element_type=F32)
            ds = (pr * (dp + corr)).astype(BF16)
            dq = lax.dot_general(ds, kband, _BDOT_NN, preferred_element_type=F32) * scale
            dkb = lax.dot_general(ds, qs, _BDOT_TN, preferred_element_type=F32) * scale
            dvb = lax.dot_general(pr.astype(BF16), dos, _BDOT_TN, preferred_element_type=F32)
            dsk = jnp.exp(sk - lse_c) * corr
            lane = lax.broadcasted_iota(jnp.int32, (8, LANES), 1)
            for p in range(cgw):
                tot = [jnp.sum(jnp.stack([dsk[i * cgw + p, h * ATT_BLOCK:(h + 1) * ATT_BLOCK] for i in range(len(rows))]),
                               axis=(0, 1)).reshape(1, 1) for h in range(2)]
                dsk_ref[:, p * LANES:(p + 1) * LANES] += jnp.where(lane == 0, tot[0], jnp.where(lane == HEAD_DIM, tot[1], 0.0))

            def gather_pairs(parts):
                if not shared:
                    return jnp.concatenate(parts, axis=1)
                tot = [parts[4 * h] + parts[4 * h + 1] + parts[4 * h + 2] + parts[4 * h + 3] for h in range(2)]
                tot = [t + pltpu.roll(t, HEAD_DIM, axis=1) for t in tot]
                return jnp.where(_lo_lanes(ATT_BLOCK), tot[0], tot[1])

            for i, r in enumerate(rows):
                mine = range(i * cgw, (i + 1) * cgw)
                dq_ref[r, :] = jnp.concatenate([_unstack_heads(dq[b]) for b in mine], axis=1)
                dk_ref[r, :] = ck[r, :] + gather_pairs([dkb[b, :ATT_BLOCK] for b in mine])
                dv_ref[r, :] = cv[r, :] + gather_pairs([dvb[b, :ATT_BLOCK] for b in mine])
                ck[r, :] = gather_pairs([dkb[b, ATT_BLOCK:] for b in mine])
                cv[r, :] = gather_pairs([dvb[b, ATT_BLOCK:] for b in mine])

        @pl.when(n < nb)
        def _():
            _att_streams(d, sb, work)

        @pl.when(n == nb)
        def _():
            dk_ref[...] = ck[...]
            dv_ref[...] = cv[...]

    sksp = BS((1, qw), lambda g, n: (0, g))
    rows = ATT_BLOCK * d
    osp = BS((rows, kw), lambda g, n: (jnp.maximum(n - 1, 0), 0 if shared else g))
    kshape = SDS((T, Wk), F32)
    dq, dk, dv, dsk = _pcall(
        body, name=name, grid=(G, nb + 1),
        in_specs=[qsp, csp, psp, csp, psp, sksp, qsp, qsp, qsp] + ([qsp] if dlse is not None else []),
        out_specs=[qsp, osp, osp, BS((8, qw), lambda g, n: (0, g))],
        out_shape=[SDS((T, Wq), F32), kshape, kshape, SDS((8, Wq), F32)],
        scratch_shapes=[pltpu.VMEM((rows, kw), F32)] * 2,
        compiler_params=_cparams(("parallel", "arbitrary")))(q, k, k, v, v, sinks, o, lse, do,
                                                              *([dlse] if dlse is not None else []))
    return dq, dk, dv, dsk[0:1]


@functools.partial(jax.custom_vjp, nondiff_argnums=(4, 5, 6, 7))
def band_attention(q, k, v, sinks, d, max_dist, with_lse, name):
    return _band_attention_fwd(q, k, v, sinks, d, max_dist, with_lse, name)[0]


def _band_attention_fwd(q, k, v, sinks, d, max_dist, with_lse, name):
    o, lse = _att_fwd_call(q, k, v, sinks, d, max_dist, name)
    return ((o, lse) if with_lse else o), (q, k, v, sinks, o, lse)


def _band_attention_bwd(d, max_dist, with_lse, name, res, g):
    q, k, v, sinks, o, lse = res
    do, dlse = g if with_lse else (g, None)
    return _att_bwd_call(q, k, v, sinks, o, lse, do, dlse, d, max_dist, name + "_bwd")


band_attention.defvjp(_band_attention_fwd, _band_attention_bwd)


def _merge_weights(ls):
    mx = jnp.maximum(jnp.maximum(ls[0], ls[1]), ls[2])
    es = [jnp.exp(l - mx) for l in ls]
    inv = 1.0 / (es[0] + es[1] + es[2])
    return [e * inv for e in es]


def _merge_fwd_call(os_, ls_, name):
    T, W = os_[0].shape
    tt = _tile(T, 512, 8)

    def body(o1, o2, o3, l1, l2, l3, out_ref):
        w = _merge_weights([l1[...], l2[...], l3[...]])
        out_ref[...] = w[0] * o1[...] + w[1] * o2[...] + w[2] * o3[...]

    row = BS((tt, W), lambda i: (i, 0))
    return _pcall(body, name=name, grid=(T // tt,), in_specs=[row] * 6, out_specs=row,
                  out_shape=SDS((T, W), F32), compiler_params=_cparams(("parallel",)))(*os_, *ls_)


def _merge_bwd_call(os_, ls_, do, name):
    T, W = os_[0].shape
    tt = _tile(T, 512, 8)

    def body(o1, o2, o3, l1, l2, l3, do_ref, d1, d2, d3, e1, e2, e3):
        w = _merge_weights([l1[...], l2[...], l3[...]])
        dov = do_ref[...]
        ts = [dov * o[...] for o in (o1, o2, o3)]
        mean = w[0] * ts[0] + w[1] * ts[1] + w[2] * ts[2]
        for wi, ti, dref, eref in zip(w, ts, (d1, d2, d3), (e1, e2, e3)):
            dref[...] = wi * dov
            eref[...] = wi * (ti - mean)

    row = BS((tt, W), lambda i: (i, 0))
    return _pcall(body, name=name, grid=(T // tt,), in_specs=[row] * 7, out_specs=[row] * 6,
                  out_shape=[SDS((T, W), F32)] * 6, compiler_params=_cparams(("parallel",)))(*os_, *ls_, do)


@functools.partial(jax.custom_vjp, nondiff_argnums=(2,))
def merge3(os_, ls_, name):
    return _merge_fwd_call(os_, ls_, name)


def _merge3_fwd(os_, ls_, name):
    return _merge_fwd_call(os_, ls_, name), (os_, ls_)


def _merge3_bwd(name, res, do):
    os_, ls_ = res
    out = _merge_bwd_call(os_, ls_, do, name + "_bwd")
    return tuple(out[:3]), tuple(out[3:])


merge3.defvjp(_merge3_fwd, _merge3_bwd)


def _xa_probs(qb, kb, scale):
    s = lax.dot_general(qb, kb, (((1,), (1,)), ((), ())), preferred_element_type=F32) * scale
    e = jnp.exp(s - jnp.max(s, axis=-1, keepdims=True))
    return e / jnp.sum(e, axis=-1, keepdims=True)


def _xa_fwd_call(q, kv, name):
    T, W = q.shape
    M = kv.shape[0]
    hd = XA_HEAD_DIM
    tq = _tile(T, 512, 8)
    scale = hd ** -0.5

    def body(q_ref, k_ref, v_ref, o_ref):
        p = _xa_probs(q_ref[...].astype(BF16), k_ref[...].astype(BF16), scale)
        o_ref[...] = jnp.dot(p.astype(BF16), v_ref[...].astype(BF16), preferred_element_type=F32)

    qsp = BS((tq, hd), lambda i, h: (i, h))
    return _pcall(body, name=name, grid=(T // tq, XA_HEADS),
                  in_specs=[qsp, BS((M, hd), lambda i, h: (0, h)), BS((M, hd), lambda i, h: (0, XA_HEADS + h))],
                  out_specs=qsp, out_shape=SDS((T, W), F32),
                  compiler_params=_cparams(("parallel", "parallel")))(q, kv, kv)


def _xa_bwd_call(q, kv, do, name):
    T, W = q.shape
    M = kv.shape[0]
    hd = XA_HEAD_DIM
    tq = _tile(T, 512, 8)
    scale = hd ** -0.5
    dn_nt = (((1,), (1,)), ((), ()))
    dn_tn = (((0,), (0,)), ((), ()))

    def body(q_ref, k_ref, v_ref, do_ref, dq_ref, dk_ref, dv_ref):
        @pl.when(pl.program_id(1) == 0)
        def _():
            dk_ref[...] = jnp.zeros_like(dk_ref)
            dv_ref[...] = jnp.zeros_like(dv_ref)

        qb, kb, vb = q_ref[...].astype(BF16), k_ref[...].astype(BF16), v_ref[...].astype(BF16)
        p = _xa_probs(qb, kb, scale)
        dob = do_ref[...].astype(BF16)
        dp = lax.dot_general(dob, vb, dn_nt, preferred_element_type=F32)
        ds = (p * (dp - jnp.sum(p * dp, axis=-1, keepdims=True))).astype(BF16)
        dq_ref[...] = jnp.dot(ds, kb, preferred_element_type=F32) * scale
        dk_ref[...] += lax.dot_general(ds, qb, dn_tn, preferred_element_type=F32) * scale
        dv_ref[...] += lax.dot_general(p.astype(BF16), dob, dn_tn, preferred_element_type=F32)

    qsp = BS((tq, hd), lambda h, i: (i, h))
    ksp = BS((M, hd), lambda h, i: (0, h))
    return _pcall(body, name=name, grid=(XA_HEADS, T // tq),
                  in_specs=[qsp, ksp, BS((M, hd), lambda h, i: (0, XA_HEADS + h)), qsp],
                  out_specs=[qsp, ksp, ksp], out_shape=[SDS((T, W), F32), SDS((M, W), F32), SDS((M, W), F32)],
                  compiler_params=_cparams(("parallel", "arbitrary")))(q, kv, kv, do)


@functools.partial(jax.custom_vjp, nondiff_argnums=(2,))
def cross_attention(q, kv, name):
    return _xa_fwd_call(q, kv, name)


def _cross_attention_fwd(q, kv, name):
    return _xa_fwd_call(q, kv, name), (q, kv)


def _cross_attention_bwd(name, res, do):
    q, kv = res
    dq, dk, dv = _xa_bwd_call(q, kv, do, name + "_bwd")
    return dq, jnp.concatenate([dk, dv], axis=1)


cross_attention.defvjp(_cross_attention_fwd, _cross_attention_bwd)


def _gate_up_swiglu_call(hn, w1t, name):
    T, K = hn.shape
    F = w1t.shape[0] // 2
    tm, tn = _tile(T, 1024), _tile(F, 256)
    nj = F // tn
    dn = (((1,), (1,)), ((), ()))

    def body(a_ref, wg_ref, wu_ref, g_ref, u_ref, act_ref):
        a = a_ref[...]
        g = lax.dot_general(a, wg_ref[...], dn, preferred_element_type=F32)
        u = lax.dot_general(a, wu_ref[...], dn, preferred_element_type=F32)
        g_ref[...] = g
        u_ref[...] = u
        act_ref[...] = ((g * _sigmoid(g)) * u).astype(BF16)

    tile = BS((tm, tn), lambda i, j: (i, j))
    return _pcall(body, name=name, grid=(T // tm, nj),
                  in_specs=[BS((tm, K), lambda i, j: (i, 0)), BS((tn, K), lambda i, j: (j, 0)),
                            BS((tn, K), lambda i, j: (j + nj, 0))],
                  out_specs=[tile, tile, tile], out_shape=[SDS((T, F), F32), SDS((T, F), F32), SDS((T, F), BF16)],
                  compiler_params=_cparams(("parallel", "parallel")))(hn, w1t, w1t)


def _swiglu_bwd_call(g, u, dact, name):
    T, F = g.shape
    tt = _tile(T, 128, 16)

    def body(g_ref, u_ref, d_ref, o_ref):
        g, d = g_ref[...], d_ref[...]
        sg = _sigmoid(g)
        o_ref[:, :F] = (d * u_ref[...] * (sg * (1.0 + g * (1.0 - sg)))).astype(BF16)
        o_ref[:, F:] = (d * (g * sg)).astype(BF16)

    row = BS((tt, F), lambda i: (i, 0))
    return _pcall(body, name=name, grid=(T // tt,), in_specs=[row, row, row],
                  out_specs=BS((tt, 2 * F), lambda i: (i, 0)), out_shape=SDS((T, 2 * F), BF16),
                  compiler_params=_cparams(("parallel",)))(g, u, dact)


@functools.partial(jax.custom_vjp, nondiff_argnums=(6,))
def ffn_block(h, g, w1b, w1c, w2b, w2c, name):
    return _ffn_fwd(h, g, w1b, w1c, w2b, w2c, name)[0]


def _ffn_fwd(h, g, w1b, w1c, w2b, w2c, name):
    hn = _rms_fwd_call(h, g, name + "_norm", BF16)
    gate, up, act = _gate_up_swiglu_call(hn, w1b, name + "_gu")
    out = _mm(act, w2b, add=h, name=name + "_down", tm=1024, tn=1024, tk=1408)
    return out, (h, g, hn, gate, up, act, w1b, w2b)


def _ffn_bwd(name, res, dout):
    h, g, hn, gate, up, act, w1b, w2b = res
    dact = _mm(dout, w2b, tb=True, name=name + "_down_da", tm=1024, tn=1408, tk=1024)
    dw2 = _mm(act, dout, ta=True, name=name + "_down_dw", tm=1408, tn=1024, tk=1024)
    dgu = _swiglu_bwd_call(gate, up, dact, name + "_swiglu_bwd")
    dw1 = _mm(dgu, hn, ta=True, name=name + "_gu_dw", tm=1408, tn=1024, tk=1024)
    dh, dg = _mm_rms_bwd(dgu, w1b, h, g, dout, name + "_gu_da", wt=True)
    return dh, dg.reshape(g.shape), jnp.zeros_like(w1b), dw1, jnp.zeros_like(w2b), dw2


ffn_block.defvjp(_ffn_fwd, _ffn_bwd)


def _final_call(h, g, target, name):
    T, Dm = h.shape
    tt = _tile(T, 512, 8)

    def body(x_ref, g_ref, t_ref, loss_ref, dx_ref, dg_ref):
        @pl.when(pl.program_id(0) == 0)
        def _():
            loss_ref[...] = jnp.zeros_like(loss_ref)
            dg_ref[...] = jnp.zeros_like(dg_ref)

        xv, gv = x_ref[...], g_ref[...]
        r = lax.rsqrt(jnp.mean(xv * xv, axis=-1, keepdims=True) + NORM_EPS)
        xh = xv * r
        err = xh * gv - t_ref[...]
        loss_ref[...] += 0.5 * jnp.sum(jnp.mean(err * err, axis=-1, keepdims=True), axis=0, keepdims=True)
        dy = err * (1.0 / Dm)
        dyg = dy * gv
        dx_ref[...] = r * (dyg - xh * jnp.mean(dyg * xh, axis=-1, keepdims=True))
        dg_ref[...] += jnp.sum(dy * xh, axis=0, keepdims=True)

    row = BS((tt, Dm), lambda i: (i, 0))
    vec = BS((1, Dm), lambda i: (0, 0))
    return _pcall(body, name=name, grid=(T // tt,), in_specs=[row, vec, row],
                  out_specs=[BS((1, 1), lambda i: (0, 0)), row, vec],
                  out_shape=[SDS((1, 1), F32), SDS((T, Dm), F32), SDS((1, Dm), F32)],
                  compiler_params=_cparams(("arbitrary",)))(h, g.reshape(1, Dm), target)


ADAMW_BLOCK_ELEMS = 64 * 1024


def _adamw_call(parts, w, m, v, name):
    shape = w.shape
    if not isinstance(parts, (list, tuple)):
        parts, shape3 = [parts], (1,) + shape
    else:
        shape3 = shape
    n_lead = shape3[0]
    r, N = shape3[-2], shape3[-1]
    Ld = math.prod(shape3[1:-2])
    w, m, v = (t.reshape(n_lead * Ld, r, N) for t in (w, m, v))
    tr = _tile(r, max(8, ADAMW_BLOCK_ELEMS // N), 8)
    c1 = 1.0 - ADAM_B1 ** ADAM_STEP
    c2 = 1.0 - ADAM_B2 ** ADAM_STEP
    outs = None
    for lead, p in enumerate(parts):
        def body(p_ref, w_ref, m_ref, v_ref, *rest):
            g_ref, d_ref, nm_ref, nv_ref = rest[-4:]
            g = p_ref[0]
            for j in range(1, N_DEV):
                g = g + p_ref[j]
            nm = ADAM_B1 * m_ref[...] + (1.0 - ADAM_B1) * g
            nv = ADAM_B2 * v_ref[...] + (1.0 - ADAM_B2) * (g * g)
            g_ref[...] = g
            nm_ref[...] = nm
            nv_ref[...] = nv
            d_ref[...] = -ADAM_LR * ((nm / c1) / (jnp.sqrt(nv / c2) + ADAM_EPS) + ADAM_WD * w_ref[...])

        base = lead * Ld
        row = BS((1, tr, N), lambda l, i, base=base: (base + l, i, 0))
        prev = [] if outs is None else list(outs)
        outs = _pcall(body, name=f"{name}_{lead}", grid=(Ld, r // tr),
                      in_specs=[BS((N_DEV, 1, tr, N), lambda l, i: (0, l, i, 0)), row, row, row]
                      + [BS(memory_space=pl.ANY)] * len(prev),
                      out_specs=[row] * 4, out_shape=[SDS((n_lead * Ld, r, N), F32)] * 4,
                      input_output_aliases={4 + j: j for j in range(len(prev))},
                      compiler_params=_cparams(("parallel", "parallel")))(p.reshape(N_DEV, Ld, r, N), w, m, v, *prev)
    return [t.reshape(shape) for t in outs]


def _place():
    return lax.axis_index("x"), lax.axis_index("y"), lax.axis_index("c")


def _all_gather(xs, name):
    n = len(xs)
    pairs = [(i, l) for i, x in enumerate(xs) for l in range(x.shape[0])]

    def body(*refs):
        x_refs, o_refs = refs[:n], refs[n:2 * n]
        send_sems, recv_sems, local_sems = refs[2 * n:]
        x_, y_, c_ = _place()
        me, sibling = (x_, y_, c_), (x_, y_, 1 - c_)
        chips = [(1 - x_, y_), (x_, 1 - y_), (1 - x_, 1 - y_)]

        def copy(e, k, block, to, from_input=False):
            i, l = pairs[e]
            px, py, pc = block
            dst = o_refs[i].at[l, 4 * px + 2 * py + pc]
            return pltpu.make_async_remote_copy(
                src_ref=x_refs[i].at[l] if from_input else dst, dst_ref=dst,
                send_sem=send_sems.at[7 * e + k], recv_sem=recv_sems.at[7 * e + k],
                device_id=to, device_id_type=pl.DeviceIdType.MESH)

        every = range(len(pairs))
        mine = [pltpu.make_async_copy(x_refs[i].at[l], o_refs[i].at[l, 4 * x_ + 2 * y_ + c_], local_sems.at[e])
                for e, (i, l) in enumerate(pairs)]
        for cp in mine:
            cp.start()
        first = [copy(e, 0, me, sibling, True) for e in every]
        first += [copy(e, 1 + j, me, (*chip, c_), True) for j, chip in enumerate(chips) for e in every]
        for cp in first:
            cp.start()
        passed = []
        for j, chip in enumerate(chips):
            for e in every:
                copy(e, 1 + j, (*chip, c_), me).wait_recv()
            for e in every:
                cp = copy(e, 4 + j, (*chip, c_), sibling)
                cp.start()
                passed.append(cp)
        for e in every:
            copy(e, 0, sibling, me).wait_recv()
        for j, chip in enumerate(chips):
            for e in every:
                copy(e, 4 + j, (*chip, 1 - c_), me).wait_recv()
        for cp in first + passed:
            cp.wait_send()
        for cp in mine:
            cp.wait()

    any_spec = BS(memory_space=pl.ANY)
    return _pcall(body, name=name, in_specs=[any_spec] * n, out_specs=[any_spec] * n,
                  out_shape=[SDS((x.shape[0], N_DEV) + x.shape[1:], x.dtype) for x in xs],
                  scratch_shapes=[pltpu.SemaphoreType.DMA((7 * len(pairs),)), pltpu.SemaphoreType.DMA((7 * len(pairs),)),
                                  pltpu.SemaphoreType.DMA((len(pairs),))],
                  compiler_params=pltpu.CompilerParams(has_side_effects=True))(*xs)


def _peer_of(k, place):
    x_, y_, c_ = place
    fx, fy, fc = (k >> 2) & 1, (k >> 1) & 1, k & 1
    return fx + x_ - 2 * fx * x_, fy + y_ - 2 * fy * y_, fc + c_ - 2 * fc * c_


def _split_copy(src_ref, land_ref, send_sems, recv_sems, e, k, place, scatter):
    x_, y_, c_ = place
    px, py, pc = _peer_of(k, place)
    return pltpu.make_async_remote_copy(
        src_ref=src_ref.at[4 * px + 2 * py + pc] if scatter else src_ref, dst_ref=land_ref.at[4 * x_ + 2 * y_ + c_],
        send_sem=send_sems.at[7 * e + k - 1], recv_sem=recv_sems.at[7 * e + k - 1],
        device_id=(px, py, pc), device_id_type=pl.DeviceIdType.MESH)


_HBM_SPEC = BS(memory_space=pltpu.HBM)
_SEM_SPEC = BS(memory_space=pltpu.SEMAPHORE)
_EFFECT = pltpu.SideEffectType.DATAFLOW_SIDE_EFFECTING


def _copies_start(srcs, scatter, name, thru=None):
    n = len(srcs)
    lands = [lax.empty(s.shape if scatter else (N_DEV,) + s.shape, s.dtype) for s in srcs]
    passed = srcs + lands + list(thru or ())

    def body(*refs):
        src_refs, land_refs = refs[:n], refs[n:2 * n]
        send_sems, recv_sems = refs[len(passed)], refs[len(passed) + 1]
        token = refs[-1]
        place = _place()
        for e in range(n):
            for k in range(1, N_DEV):
                _split_copy(src_refs[e], land_refs[e], send_sems, recv_sems, e, k, place, scatter).start()
        token[...] = jnp.zeros_like(token)

    hbm = lambda t: pltpu.with_memory_space_constraint(t, pltpu.HBM)
    out = _pcall(body, name=name,
                 out_shape=(pltpu.SemaphoreType.DMA((7 * n,)), pltpu.SemaphoreType.DMA((7 * n,)),
                            *[pltpu.HBM(t.shape, t.dtype) for t in passed], SDS((8, LANES), F32)),
                 in_specs=[_HBM_SPEC] * len(passed),
                 out_specs=(_SEM_SPEC, _SEM_SPEC, *[_HBM_SPEC] * len(passed), BS(memory_space=pltpu.VMEM)),
                 input_output_aliases={i: 2 + i for i in range(len(passed))},
                 compiler_params=pltpu.CompilerParams(has_side_effects=_EFFECT))(*[hbm(t) for t in passed])
    return out[0], out[1], list(out[2:2 + n]), list(out[2 + n:2 + 2 * n]), list(out[2 + 2 * n:-1])


def _copies_wait(started, which, scatter, after, name):
    send_sems, recv_sems, srcs, lands, _ = started
    n = len(which)
    after = list(after) if isinstance(after, (list, tuple)) else [after]

    def body(*refs):
        src_refs, land_refs = refs[:n], refs[n:2 * n]
        send_s, recv_s = refs[2 * n], refs[2 * n + 1]
        place = _place()
        for j, e in enumerate(which):
            for k in range(1, N_DEV):
                cp = _split_copy(src_refs[j], land_refs[j], send_s, recv_s, e, k, place, scatter)
                cp.wait_send()
                cp.wait_recv()

    args = [srcs[e] for e in which] + [lands[e] for e in which]
    out = _pcall(body, name=name, out_shape=tuple(pltpu.HBM(t.shape, t.dtype) for t in args),
                 in_specs=[_HBM_SPEC] * (2 * n) + [_SEM_SPEC, _SEM_SPEC] + [BS(memory_space=pl.ANY)] * len(after),
                 out_specs=tuple([_HBM_SPEC] * (2 * n)), input_output_aliases={i: i for i in range(2 * n)},
                 compiler_params=pltpu.CompilerParams(has_side_effects=_EFFECT))(*args, send_sems, recv_sems, *after)
    return list(out[:n]), list(out[n:])


def _with_own_block(land, own_block):
    me = 4 * lax.axis_index("x") + 2 * lax.axis_index("y") + lax.axis_index("c")
    return lax.dynamic_update_index_in_dim(land, own_block, me, 0)


def _pad_flat(t, quantum=PACK_QUANTUM):
    f = t.reshape(-1)
    pad = (-f.shape[0]) % quantum
    return jnp.pad(f, (0, pad)) if pad else f


def _pack(arrs, dtype):
    return jnp.concatenate([_pad_flat(a.astype(dtype)) for a in arrs]).reshape(-1, LANES)


def _unpack(buf, shapes, lead=()):
    flat = buf.reshape(lead + (-1,))
    out, off = [], 0
    for s in shapes:
        n = math.prod(s)
        out.append(flat[..., off:off + n].reshape(lead + tuple(s)))
        off += n + (-n) % PACK_QUANTUM
    return out


def _full_from_gathered(g, axis):
    t = jnp.moveaxis(g, 0, axis)
    s = t.shape
    return t.reshape(s[:axis] + (s[axis] * s[axis + 1],) + s[axis + 2:])


def _parts_from_full(t, axis):
    s = t.shape
    t = t.reshape(s[:axis] + (N_DEV, s[axis] // N_DEV) + s[axis + 1:])
    return jnp.moveaxis(t, axis, 0)


def _head_rows(t):
    return jnp.repeat(t, HEAD_DIM).reshape(1, -1)


def _dilated_attention(q, k, v, name):
    no_sink = jnp.full((1, q.shape[1]), NEG, F32)
    outs, lses = zip(*[band_attention(q, k, v, no_sink, d, ATT_BLOCK, True, f"{name}_d{d}") for d in DILATIONS])
    return merge3(tuple(outs), tuple(lses), name + "_merge")


STAGES = (
    ("proj0", ('mix_norm',), (('ab_w_in', 0),)),
    ("mixer0", ('lru_conv_w', 'lru_conv_b', 'lru_ba', 'lru_bx', 'lru_lambda'),
     (('lru_wa', 0), ('lru_wx', 0), ('ab_w_out', 0))),
    ("xa0", ('xa_norm', 'xa_mem_norm'), (('xa_wq', 0), ('xa_wkv', 0), ('xa_wo', 0))),
    ("ffn0", ('ffn_norm',), (('ffn_w_gate_up', 0), ('ffn_w_down', 0))),
    ("mixer1", ('mix_norm', 'c_b_qkv', 'c_sinks', 'c_b_out'), (('c_w_qkv', 0), ('c_w_out', 0))),
    ("xa1", ('xa_norm', 'xa_mem_norm'), (('xa_wq', 1), ('xa_wkv', 1), ('xa_wo', 1))),
    ("ffn1", ('ffn_norm',), (('ffn_w_gate_up', 1), ('ffn_w_down', 1))),
)


def _stage_fn(stage, Wb, tabs, mem):
    layer = int(stage[-1])
    L = f"l{layer}"

    def run(S, Cw, h):
        def lin(a, key, bias, add, name, rows=None):
            wb, wc = Wb[key], Cw[key]
            if rows is not None:
                wb, wc = wb[rows], wc[rows]
            return linear(a, wb, wc, bias, add, name)

        def norm_lin(a, gain, key, bias, name):
            return norm_linear(a, gain, Wb[key], Cw[key], bias, key[0] in COLUMN_CUT, name)

        if stage == "mixer0":
            h, x_br, y_br, q, k, v = h
            C = S['lru_conv_w'].shape[-1]
            xc = conv4(x_br, S['lru_conv_w'][0], S['lru_conv_b'][0], L + "_conv")
            ga, gx = lru_gates(xc, Wb['lru_wa', 0], Cw['lru_wa', 0], S['lru_ba'][0],
                               Wb['lru_wx', 0], Cw['lru_wx', 0], S['lru_bx'][0], L + "_gates")
            rec = lru_scan(xc, ga, gx, y_br, S['lru_lambda'][0], L + "_scan")
            att = _dilated_attention(q, k, v, L + "_att")
            h = lin(att, ('ab_w_out', 0), None, h, L + "_w_out_att", slice(C, None))
            return lin(rec, ('ab_w_out', 0), None, h, L + "_w_out_rec", slice(0, C))
        if stage == "mixer1":
            qw = C_HEADS * HEAD_DIM
            kw = C_KV_HEADS * HEAD_DIM
            qkv, h = norm_lin(h, S['mix_norm'][1], ('c_w_qkv', 0), S['c_b_qkv'][0], L + "_w_qkv")
            q = rope(qkv[:, :qw], tabs, L + "_rope_q")
            k = rope(qkv[:, qw:qw + kw], tabs, L + "_rope_k")
            v = qkv[:, qw + kw:]
            o = band_attention(q, k, v, _head_rows(S['c_sinks'][0]), 1, ATT_BLOCK - 1, False, L + "_att")
            return lin(o, ('c_w_out', 0), S['c_b_out'][0], h, L + "_w_out")
        if stage.startswith("xa"):
            xq, h = norm_lin(h, S['xa_norm'][layer], ('xa_wq', layer), None, L + "_xa_wq")
            xkv, _ = norm_lin(mem, S['xa_mem_norm'][layer], ('xa_wkv', layer), None, L + "_xa_wkv")
            return lin(cross_attention(xq, xkv, L + "_xa"), ('xa_wo', layer), None, h, L + "_xa_wo")
        gu, down = ('ffn_w_gate_up', layer), ('ffn_w_down', layer)
        return ffn_block(h, S['ffn_norm'][layer], Wb[gu], Cw[gu], Wb[down], Cw[down], L + "_ffn")

    return run


def kernel(x, mem, mix_norm, ab_w_in, lru_conv_w, lru_conv_b, lru_wa, lru_ba, lru_wx, lru_bx, lru_lambda, ab_w_out, c_w_qkv, c_b_qkv, c_sinks, c_w_out, c_b_out, xa_norm, xa_mem_norm, xa_wq, xa_wkv, xa_wo, ffn_norm, ffn_w_gate_up, ffn_w_down, final_norm, loss_target, m_mix_norm, m_ab_w_in, m_lru_conv_w, m_lru_conv_b, m_lru_wa, m_lru_ba, m_lru_wx, m_lru_bx, m_lru_lambda, m_ab_w_out, m_c_w_qkv, m_c_b_qkv, m_c_sinks, m_c_w_out, m_c_b_out, m_xa_norm, m_xa_mem_norm, m_xa_wq, m_xa_wkv, m_xa_wo, m_ffn_norm, m_ffn_w_gate_up, m_ffn_w_down, m_final_norm, v_mix_norm, v_ab_w_in, v_lru_conv_w, v_lru_conv_b, v_lru_wa, v_lru_ba, v_lru_wx, v_lru_bx, v_lru_lambda, v_ab_w_out, v_c_w_qkv, v_c_b_qkv, v_c_sinks, v_c_w_out, v_c_b_out, v_xa_norm, v_xa_mem_norm, v_xa_wq, v_xa_wkv, v_xa_wo, v_ffn_norm, v_ffn_w_gate_up, v_ffn_w_down, v_final_norm):
    w_loc = dict(zip(WEIGHT_NAMES, (mix_norm, ab_w_in, lru_conv_w, lru_conv_b, lru_wa, lru_ba, lru_wx, lru_bx, lru_lambda, ab_w_out, c_w_qkv, c_b_qkv, c_sinks, c_w_out, c_b_out, xa_norm, xa_mem_norm, xa_wq, xa_wkv, xa_wo, ffn_norm, ffn_w_gate_up, ffn_w_down, final_norm)))
    m_loc = dict(zip(WEIGHT_NAMES, (m_mix_norm, m_ab_w_in, m_lru_conv_w, m_lru_conv_b, m_lru_wa, m_lru_ba, m_lru_wx, m_lru_bx, m_lru_lambda, m_ab_w_out, m_c_w_qkv, m_c_b_qkv, m_c_sinks, m_c_w_out, m_c_b_out, m_xa_norm, m_xa_mem_norm, m_xa_wq, m_xa_wkv, m_xa_wo, m_ffn_norm, m_ffn_w_gate_up, m_ffn_w_down, m_final_norm)))
    v_loc = dict(zip(WEIGHT_NAMES, (v_mix_norm, v_ab_w_in, v_lru_conv_w, v_lru_conv_b, v_lru_wa, v_lru_ba, v_lru_wx, v_lru_bx, v_lru_lambda, v_ab_w_out, v_c_w_qkv, v_c_b_qkv, v_c_sinks, v_c_w_out, v_c_b_out, v_xa_norm, v_xa_mem_norm, v_xa_wq, v_xa_wkv, v_xa_wo, v_ffn_norm, v_ffn_w_gate_up, v_ffn_w_down, v_final_norm)))

    me = 4 * lax.axis_index("x") + 2 * lax.axis_index("y") + lax.axis_index("c")

    first_keys = list(STAGES[0][2])
    keys = [key for _, _, stage_keys in STAGES[1:] for key in stage_keys]
    shards = [_shard_view(n, w_loc[n])[l].astype(BF16) for n, l in keys]
    first_g = _all_gather([_pack([w_loc[n] for n in SMALL], F32)[None]]
                          + [_shard_view(n, w_loc[n])[l].astype(BF16)[None] for n, l in first_keys], "gather_first")
    gather = _copies_start(shards, False, "gather_start", thru=[first_g[0]])
    small_g = gather[4][0][0]
    Wb = {key: _full_from_gathered(g[0], _layer_shard_axis(key[0])) for key, g in zip(first_keys, first_g[1:])}
    S = {n: w_loc[n] for n in REPLICATED}
    for n, t in zip(SMALL, _unpack(small_g, [w_loc[n].shape for n in SMALL], lead=(N_DEV,))):
        S[n] = _full_from_gathered(t, SHARD_AXIS[n])

    tabs = _rope_tables(x.shape[1])
    w_in = Wb[first_keys[0]]
    hn0 = _rms_fwd_call(x[0], S['mix_norm'][0], "l0_w_in_norm", BF16)
    lru_w, att_w = N_DEV * lru_conv_w.shape[-1], B_HEADS * HEAD_DIM
    pieces = (("x", lru_w, False), ("y", lru_w, False), ("q", att_w, True), ("k", att_w, True), ("v", att_w, False))
    h, row = [x[0]], 0
    for piece, width, rotated in pieces:
        h.append(_mm(hn0, w_in[row:row + width], tb=True, rope=tabs if rotated else None, name="l0_w_in_" + piece,
                     tm=1024, tn=1024, tk=1024))
        row += width
    h = tuple(h)
    vjps = []
    for stage, small_names, stage_keys in STAGES[1:]:
        which = [keys.index(key) for key in stage_keys]
        _, lands = _copies_wait(gather, which, False, jax.tree.leaves(h)[-1], "gather_wait_" + stage)
        for e, land in zip(which, lands):
            Wb[keys[e]] = _full_from_gathered(_with_own_block(land, shards[e]), _layer_shard_axis(keys[e][0]))
        carriers = {key: jnp.zeros(Wb[key].shape, F32) for key in stage_keys}
        h, vjp_fn = jax.vjp(_stage_fn(stage, Wb, tabs, mem[0]), {n: S[n] for n in small_names}, carriers, h)
        vjps.append(vjp_fn)
    loss_part, dh, dg_final = _final_call(h, S['final_norm'], loss_target[0], "final_loss")

    grads = {'final_norm': dg_final.reshape(final_norm.shape)}
    exchanges, send_keys, send_parts = [], [], []

    def start_exchange(stage, dh):
        leaves, tree = jax.tree.flatten(dh)
        started = _copies_start(list(send_parts), True, "grad_start_" + stage, thru=leaves)
        exchanges.append((stage, started, list(send_keys)))
        send_keys.clear()
        send_parts.clear()
        return jax.tree.unflatten(tree, started[4])

    for (stage, small_names, stage_keys), vjp_fn in zip(reversed(STAGES[1:]), reversed(vjps)):
        g_small, g_big, dh = vjp_fn(dh)
        for n in small_names:
            grads[n] = grads[n] + g_small[n] if n in grads else g_small[n]
        send_keys += list(stage_keys)
        send_parts += [_parts_from_full(g_big[key], _layer_shard_axis(key[0])) for key in stage_keys]
        if stage == "xa1":
            continue
        if stage == "mixer0":
            small_parts = [_parts_from_full(grads[n], SHARD_AXIS[n]) for n in SMALL]
            send_keys.append("small")
            send_parts.append(jnp.stack([_pack([p[j] for p in small_parts], F32) for j in range(N_DEV)]))
        dh = start_exchange(stage, dh)
    d_res, d_proj = dh[0], _join_pieces_call(dh[1:], [rotated for _, _, rotated in pieces], tabs, "l0_w_in_dproj")
    send_keys.append(first_keys[0])
    send_parts.append(_parts_from_full(_mm(d_proj, hn0, ta=True, name="l0_w_in_dw", tm=1408, tn=1024, tk=1024),
                                       _layer_shard_axis(first_keys[0][0])))
    d_res, d_proj = start_exchange("proj0", (d_res, d_proj))
    dx, dg0 = _mm_rms_bwd(d_proj, w_in, x[0], S['mix_norm'][0], d_res, "l0_w_in_da", wt=True)
    grads['mix_norm'] = grads['mix_norm'] + jnp.concatenate([dg0, jnp.zeros_like(dg0)], axis=0)
    rep_names = REPLICATED + ["loss"]
    grads["loss"] = loss_part
    zero = jnp.zeros((1, 1), F32)
    for d in (w_loc, m_loc, v_loc):
        d["loss"] = zero
    rep_started = _copies_start([_pack([grads[n] for n in rep_names], F32)], False, "rep_grads_start")

    parts, out = {}, {}

    def end_exchange(stage, started, ex_keys, after):
        srcs, lands = _copies_wait(started, list(range(len(ex_keys))), True, after, "grad_wait_" + stage)
        for key, src, land in zip(ex_keys, srcs, lands):
            parts[key] = _with_own_block(land, lax.dynamic_index_in_dim(src, me, 0, keepdims=False))

    def adamw(p, names, call_name):
        if len(names) == 1:
            n = names[0]
            res = _adamw_call(p, _shard_view(n, w_loc[n]), _shard_view(n, m_loc[n]), _shard_view(n, v_loc[n]), call_name)
            for kind, t in zip(("grad", "delta", "new_m", "new_v"), res):
                out[kind, n] = _shard_view(n, t)
        else:
            res = _adamw_call(p, *[_pack([d[n] for n in names], F32) for d in (w_loc, m_loc, v_loc)], call_name)
            for kind, buf in zip(("grad", "delta", "new_m", "new_v"), res):
                for n, t in zip(names, _unpack(buf, [w_loc[n].shape for n in names])):
                    out[kind, n] = t

    for ex in exchanges[:-1]:
        end_exchange(*ex, dx)
    last_names = {key[0] for key in exchanges[-1][2]}
    for n in BIG:
        if n not in last_names:
            adamw([parts[n, l] for l in range(w_loc[n].shape[0])], [n], "adamw_" + n)
    adamw(parts["small"], SMALL, "adamw_small")
    end_exchange(*exchanges[-1], [out["new_v", n] for n in BIG if n not in last_names])
    for n in BIG:
        if n in last_names:
            adamw([parts[n, l] for l in range(w_loc[n].shape[0])], [n], "adamw_" + n)
    rep_src, rep_land = _copies_wait(rep_started, [0], False, out["new_v", "ab_w_in"], "rep_grads_wait")
    adamw(_with_own_block(rep_land[0], rep_src[0]), rep_names, "adamw_replicated")
    loss = out["grad", "loss"][0, 0]

    return (loss, dx[None], *[out[kind, n] for kind in ("grad", "delta", "new_m", "new_v") for n in WEIGHT_NAMES])
```

```python
import functools
import math

import jax
import jax.numpy as jnp
from jax import lax
from jax.experimental import pallas as pl
from jax.experimental.pallas import tpu as pltpu

F32 = jnp.float32
BF16 = jnp.bfloat16
SDS = jax.ShapeDtypeStruct
BS = pl.BlockSpec

N_DEV = 8
NORM_EPS = 1e-6
ROPE_THETA = 500000.0
HEAD_DIM = 64
ROT_DIM = 16
ATT_BLOCK = 128
LRU_C = 8.0
LRU_HEADS = 4
DILATIONS = (1, 4, 16)
B_HEADS = 8
C_HEADS = 16
C_KV_HEADS = 2
XA_HEADS = 4
XA_HEAD_DIM = 128
NEG = -1e30
ADAM_LR, ADAM_B1, ADAM_B2, ADAM_EPS, ADAM_WD, ADAM_STEP = 0.001, 0.9, 0.999, 1e-08, 0.01, 10
LANES = 128
VMEM_LIMIT = 48 * 1024 * 1024

WEIGHT_NAMES = ['mix_norm', 'ab_w_in', 'lru_conv_w', 'lru_conv_b', 'lru_wa', 'lru_ba', 'lru_wx', 'lru_bx',
                'lru_lambda', 'ab_w_out', 'c_w_qkv', 'c_b_qkv', 'c_sinks', 'c_w_out', 'c_b_out', 'xa_norm',
                'xa_mem_norm', 'xa_wq', 'xa_wkv', 'xa_wo', 'ffn_norm', 'ffn_w_gate_up', 'ffn_w_down', 'final_norm']
SHARD_AXIS = {'ab_w_in': 2, 'lru_conv_w': 2, 'lru_wa': 2, 'lru_ba': 2, 'lru_wx': 2, 'lru_bx': 2, 'ab_w_out': 1,
              'c_w_qkv': 2, 'c_b_qkv': 1, 'c_w_out': 1, 'c_b_out': 1, 'xa_wq': 1, 'xa_wkv': 1, 'xa_wo': 2,
              'ffn_w_gate_up': 2, 'ffn_w_down': 1}
BIG = ['ab_w_in', 'lru_wa', 'lru_wx', 'ab_w_out', 'c_w_qkv', 'c_w_out', 'xa_wq', 'xa_wkv', 'xa_wo',
       'ffn_w_gate_up', 'ffn_w_down']
SMALL = ['lru_conv_w', 'lru_ba', 'lru_bx', 'c_b_qkv', 'c_b_out']
REPLICATED = [n for n in WEIGHT_NAMES if n not in SHARD_AXIS]
COLUMN_CUT = ('ab_w_in', 'c_w_qkv', 'ffn_w_gate_up')
PACK_QUANTUM = 2048


def _shard_view(name, t):
    return jnp.swapaxes(t, -1, -2) if name in COLUMN_CUT else t


def _layer_shard_axis(name):
    return 0 if name in COLUMN_CUT else SHARD_AXIS[name] - 1


def _pcall(body, **kw):
    return pl.pallas_call(body, **kw)


def _cparams(sem=None):
    return pltpu.CompilerParams(dimension_semantics=sem, vmem_limit_bytes=VMEM_LIMIT)


def _tile(n, target, mult=LANES):
    if n <= target:
        return n
    t = (target // mult) * mult
    while t >= mult:
        if n % t == 0:
            return t
        t -= mult
    return n


def _sigmoid(x):
    return 1.0 / (1.0 + jnp.exp(-x))


def _expm1(x):
    small = x * (1.0 + x * (0.5 + x * (1.0 / 6.0 + x * (1.0 / 24.0))))
    return jnp.where(jnp.abs(x) < 0.03, small, jnp.exp(x) - 1.0)


_GELU_C = math.sqrt(2.0 / math.pi)


def _gelu_parts(y):
    y2 = y * y
    th = jnp.tanh(_GELU_C * (y + 0.044715 * y * y2))
    g = 0.5 * y * (1.0 + th)
    dg = 0.5 * (1.0 + th) + 0.5 * y * (1.0 - th * th) * _GELU_C * (1.0 + 3.0 * 0.044715 * y2)
    return g, dg


def _rotate(xv, tab_refs, inverse=False):
    W = xv.shape[1]
    half = ROT_DIM // 2
    c, sa, sb = (jnp.tile(t[...], (1, W // LANES)) for t in tab_refs)
    if not inverse:
        return xv * c + pltpu.roll(xv, half, axis=1) * sa + pltpu.roll(xv, W - half, axis=1) * sb
    return xv * c + pltpu.roll(xv * sa, W - half, axis=1) + pltpu.roll(xv * sb, half, axis=1)


def _mm(a, b, *, ta=False, tb=False, bias=None, add=None, rope=None, name, tm=512, tn=512, tk=2048):
    M, K = (a.shape[1], a.shape[0]) if ta else a.shape
    N = b.shape[0] if tb else b.shape[1]
    tm, tn, tk = _tile(M, tm), _tile(N, tn), _tile(K, tk)
    nk = K // tk
    dn = (((0 if ta else 1,), (1 if tb else 0,)), ((), ()))

    def body(*refs):
        a_ref, b_ref = refs[0], refs[1]
        pos = 2
        bias_ref = add_ref = None
        if bias is not None:
            bias_ref = refs[pos]
            pos += 1
        if add is not None:
            add_ref = refs[pos]
            pos += 1
        tab_refs = refs[pos:pos + 3] if rope is not None else None
        pos += 3 if rope is not None else 0
        o_ref = refs[pos]
        part = lax.dot_general(a_ref[...].astype(BF16), b_ref[...].astype(BF16), dn, preferred_element_type=F32)

        def finish(r):
            if bias_ref is not None:
                r = r + bias_ref[...]
            if add_ref is not None:
                r = r + add_ref[...]
            o_ref[...] = r if tab_refs is None else _rotate(r, tab_refs)

        if nk == 1:
            finish(part)
            return
        acc_ref = refs[pos + 1]
        k = pl.program_id(2)

        @pl.when(k == 0)
        def _():
            acc_ref[...] = part

        @pl.when((k > 0) & (k < nk - 1))
        def _():
            acc_ref[...] += part

        @pl.when(k == nk - 1)
        def _():
            finish(acc_ref[...] + part)

    in_specs = [BS((tk, tm), lambda i, j, k: (k, i)) if ta else BS((tm, tk), lambda i, j, k: (i, k)),
                BS((tn, tk), lambda i, j, k: (j, k)) if tb else BS((tk, tn), lambda i, j, k: (k, j))]
    args = [a, b]
    if bias is not None:
        in_specs.append(BS((1, tn), lambda i, j, k: (0, j)))
        args.append(bias.reshape(1, N))
    if add is not None:
        in_specs.append(BS((tm, tn), lambda i, j, k: (i, j)))
        args.append(add)
    if rope is not None:
        in_specs += [BS((tm, LANES), lambda i, j, k: (i, 0))] * 3
        args += list(rope)
    return _pcall(body, name=name, grid=(M // tm, N // tn, nk), in_specs=in_specs,
                  out_specs=BS((tm, tn), lambda i, j, k: (i, j)), out_shape=SDS((M, N), F32),
                  scratch_shapes=[pltpu.VMEM((tm, tn), F32)] if nk > 1 else [],
                  compiler_params=_cparams(("parallel", "parallel", "arbitrary")))(*args)


def _mm_rms_bwd(dy, wb, x, g, add, name, wt=False):
    T, Kc = dy.shape
    Dm = wb.shape[1] if wt else wb.shape[0]
    tm, tk = _tile(T, 1024), _tile(Kc, 1408)
    nk = Kc // tk
    dn = (((1,), (0 if wt else 1,)), ((), ()))

    def body(*refs):
        dy_ref, w_ref, x_ref, g_ref = refs[:4]
        dx_ref, dg_ref, acc_ref = refs[-3:]
        i, k = pl.program_id(0), pl.program_id(1)
        part = lax.dot_general(dy_ref[...].astype(BF16), w_ref[...], dn, preferred_element_type=F32)

        @pl.when(k == 0)
        def _():
            acc_ref[...] = part

        @pl.when((k > 0) & (k < nk - 1))
        def _():
            acc_ref[...] += part

        @pl.when((i == 0) & (k == 0))
        def _():
            dg_ref[...] = jnp.zeros_like(dg_ref)

        @pl.when(k == nk - 1)
        def _():
            d = part if nk == 1 else acc_ref[...] + part
            xv = x_ref[...]
            r = lax.rsqrt(jnp.mean(xv * xv, axis=-1, keepdims=True) + NORM_EPS)
            xh = xv * r
            dyg = d * g_ref[...]
            dx = r * (dyg - xh * jnp.mean(dyg * xh, axis=-1, keepdims=True))
            dx_ref[...] = dx if add is None else dx + refs[4][...]
            dg_ref[...] += jnp.sum(d * xh, axis=0, keepdims=True)

    row = BS((tm, Dm), lambda i, k: (i, 0))
    vec = BS((1, Dm), lambda i, k: (0, 0))
    extra = [] if add is None else [add]
    return _pcall(body, name=name, grid=(T // tm, nk),
                  in_specs=[BS((tm, tk), lambda i, k: (i, k)),
                            BS((tk, Dm), lambda i, k: (k, 0)) if wt else BS((Dm, tk), lambda i, k: (0, k)), row, vec]
                  + [row] * len(extra),
                  out_specs=[row, vec], out_shape=[SDS((T, Dm), F32), SDS((1, Dm), F32)],
                  scratch_shapes=[pltpu.VMEM((tm, Dm), F32)],
                  compiler_params=_cparams(("arbitrary", "arbitrary")))(dy, wb, x, g.reshape(1, Dm), *extra)


def _colsum(x, name):
    T, N = x.shape
    tt = _tile(T, 512, 8)

    def body(x_ref, o_ref):
        @pl.when(pl.program_id(0) == 0)
        def _():
            o_ref[...] = jnp.zeros_like(o_ref)

        o_ref[...] += jnp.sum(x_ref[...], axis=0, keepdims=True)

    return _pcall(body, name=name, grid=(T // tt,), in_specs=[BS((tt, N), lambda i: (i, 0))],
                  out_specs=BS((1, N), lambda i: (0, 0)), out_shape=SDS((1, N), F32),
                  compiler_params=_cparams(("arbitrary",)))(x)


@functools.partial(jax.custom_vjp, nondiff_argnums=(5,))
def linear(a, wb, wc, bias, add, name):
    return _mm(a, wb, bias=bias, add=add, name=name)


def _linear_fwd(a, wb, wc, bias, add, name):
    return _mm(a, wb, bias=bias, add=add, name=name), (a, wb, bias is not None, add is not None)


def _linear_bwd(name, res, g):
    a, wb, has_bias, has_add = res
    da = _mm(g, wb, tb=True, name=name + "_da")
    dw = _mm(a, g, ta=True, name=name + "_dw")
    dbias = _colsum(g, name + "_db").reshape(-1) if has_bias else None
    return da, jnp.zeros_like(wb), dw, dbias, (g if has_add else None)


linear.defvjp(_linear_fwd, _linear_bwd)


def _rms_fwd_call(x, g, name, out_dtype=F32):
    T, Dm = x.shape
    tt = _tile(T, 512, 16)

    def body(x_ref, g_ref, o_ref):
        xv = x_ref[...]
        r = lax.rsqrt(jnp.mean(xv * xv, axis=-1, keepdims=True) + NORM_EPS)
        o_ref[...] = ((xv * r) * g_ref[...]).astype(out_dtype)

    return _pcall(body, name=name, grid=(T // tt,),
                  in_specs=[BS((tt, Dm), lambda i: (i, 0)), BS((1, Dm), lambda i: (0, 0))],
                  out_specs=BS((tt, Dm), lambda i: (i, 0)), out_shape=SDS((T, Dm), out_dtype),
                  compiler_params=_cparams(("parallel",)))(x, g.reshape(1, Dm))


@functools.partial(jax.custom_vjp, nondiff_argnums=(5, 6))
def norm_linear(x, g, wb, wc, bias, wt, name):
    return _norm_linear_fwd(x, g, wb, wc, bias, wt, name)[0]


def _norm_linear_fwd(x, g, wb, wc, bias, wt, name):
    hn = _rms_fwd_call(x, g, name + "_norm", BF16)
    return (_mm(hn, wb, tb=wt, bias=bias, name=name, tm=1024, tn=1408, tk=1024), x), (x, g, hn, wb, bias is not None)


def _norm_linear_bwd(wt, name, res, cts):
    x, g, hn, wb, has_bias = res
    dy, dres = cts
    if wt:
        dw = _mm(dy, hn, ta=True, name=name + "_dw", tm=1408, tn=1024, tk=1024)
    else:
        dw = _mm(hn, dy, ta=True, name=name + "_dw", tm=1024, tn=1408, tk=1024)
    dx, dg = _mm_rms_bwd(dy, wb, x, g, dres, name + "_da", wt)
    dbias = _colsum(dy, name + "_db").reshape(-1) if has_bias else None
    return dx, dg.reshape(g.shape), jnp.zeros_like(wb), dw, dbias


norm_linear.defvjp(_norm_linear_fwd, _norm_linear_bwd)


def _rope_tables(T):
    half = ROT_DIM // 2
    inv = ROPE_THETA ** (-jnp.arange(0, ROT_DIM, 2, dtype=F32) / ROT_DIM)
    lane = jnp.arange(LANES) % HEAD_DIM
    freq = jnp.where(lane < ROT_DIM, inv[lane % half], 0.0)
    ang = jnp.arange(T, dtype=F32)[:, None] * freq[None, :]
    c, s = jnp.cos(ang), jnp.sin(ang)
    sa = jnp.where((lane >= half) & (lane < ROT_DIM), s, 0.0)
    sb = jnp.where(lane < half, -s, 0.0)
    return c, sa, sb


def _join_pieces_call(pieces, rotated, tabs, name):
    T = pieces[0].shape[0]
    widths = [p.shape[1] for p in pieces]
    tt = _tile(T, 256, 8)

    def body(*refs):
        tab_refs, o_ref = refs[len(pieces):len(pieces) + 3], refs[-1]
        off = 0
        for p_ref, w, r in zip(refs, widths, rotated):
            o_ref[:, off:off + w] = _rotate(p_ref[...], tab_refs, inverse=True) if r else p_ref[...]
            off += w

    return _pcall(body, name=name, grid=(T // tt,),
                  in_specs=[BS((tt, w), lambda i: (i, 0)) for w in widths] + [BS((tt, LANES), lambda i: (i, 0))] * 3,
                  out_specs=BS((tt, sum(widths)), lambda i: (i, 0)), out_shape=SDS((T, sum(widths)), F32),
                  compiler_params=_cparams(("parallel",)))(*pieces, *tabs)


@functools.partial(jax.custom_vjp, nondiff_argnums=(6, 7))
def norm_linear_pieces(x, g, wb, wc, bias, tabs, pieces, name):
    return _norm_linear_pieces_fwd(x, g, wb, wc, bias, tabs, pieces, name)[0]


def _norm_linear_pieces_fwd(x, g, wb, wc, bias, tabs, pieces, name):
    hn = _rms_fwd_call(x, g, name + "_norm", BF16)
    outs, row = [], 0
    for i, (width, rotated) in enumerate(pieces):
        outs.append(_mm(hn, wb[row:row + width], tb=True, bias=None if bias is None else bias[row:row + width],
                        rope=tabs if rotated else None, name=f"{name}_{i}", tm=1024, tn=1024, tk=1024))
        row += width
    return (*outs, x), (x, g, hn, wb, tabs, bias is not None)


def _norm_linear_pieces_bwd(pieces, name, res, cts):
    x, g, hn, wb, tabs, has_bias = res
    dy = _join_pieces_call(cts[:-1], [rotated for _, rotated in pieces], tabs, name + "_join")
    dw = _mm(dy, hn, ta=True, name=name + "_dw", tm=1408, tn=1024, tk=1024)
    dx, dg = _mm_rms_bwd(dy, wb, x, g, cts[-1], name + "_da", wt=True)
    dbias = _colsum(dy, name + "_db").reshape(-1) if has_bias else None
    return dx, dg.reshape(g.shape), jnp.zeros_like(wb), dw, dbias, jax.tree.map(jnp.zeros_like, tabs)


norm_linear_pieces.defvjp(_norm_linear_pieces_fwd, _norm_linear_pieces_bwd)


def _conv_fwd_call(x, w, b, name):
    T, C = x.shape
    tt = _tile(T, 512, 8)
    per = tt // 8

    def body(x_ref, halo_ref, w_ref, b_ref, o_ref):
        i = pl.program_id(0)
        halo = jnp.where(i > 0, halo_ref[...], 0.0)
        e = jnp.concatenate([halo, x_ref[...]], axis=0)
        acc = b_ref[...]
        for k in (3, 2, 1):
            acc = acc + pltpu.roll(e, k, axis=0)[8:, :] * w_ref[3 - k:4 - k, :]
        o_ref[...] = acc + x_ref[...] * w_ref[3:4, :]

    return _pcall(body, name=name, grid=(T // tt,),
                  in_specs=[BS((tt, C), lambda i: (i, 0)), BS((8, C), lambda i: (jnp.maximum(i * per - 1, 0), 0)),
                            BS((4, C), lambda i: (0, 0)), BS((1, C), lambda i: (0, 0))],
                  out_specs=BS((tt, C), lambda i: (i, 0)), out_shape=SDS((T, C), F32),
                  compiler_params=_cparams(("parallel",)))(x, x, w, b.reshape(1, C))


def _conv_bwd_call(x, w, dy, name):
    T, C = x.shape
    tt = _tile(T, 512, 8)
    per = tt // 8
    nt = T // tt

    def body(x_ref, halo_ref, w_ref, dy_ref, nxt_ref, dx_ref, dwb_ref):
        i = pl.program_id(0)
        halo = jnp.where(i > 0, halo_ref[...], 0.0)
        e = jnp.concatenate([halo, x_ref[...]], axis=0)
        dy = dy_ref[...]
        nxt = jnp.where(i < nt - 1, nxt_ref[...], 0.0)
        f = jnp.concatenate([dy, nxt], axis=0)
        dx = dy * w_ref[3:4, :]
        rows = [None] * 4
        rows[3] = jnp.sum(dy * x_ref[...], axis=0, keepdims=True)
        for k in (1, 2, 3):
            dx = dx + pltpu.roll(f, tt + 8 - k, axis=0)[:tt, :] * w_ref[3 - k:4 - k, :]
            rows[3 - k] = jnp.sum(dy * pltpu.roll(e, k, axis=0)[8:, :], axis=0, keepdims=True)
        dx_ref[...] = dx
        upd = jnp.concatenate(rows + [jnp.sum(dy, axis=0, keepdims=True), jnp.zeros((3, C), F32)], axis=0)

        @pl.when(i == 0)
        def _():
            dwb_ref[...] = jnp.zeros_like(dwb_ref)

        dwb_ref[...] += upd

    row = BS((tt, C), lambda i: (i, 0))
    return _pcall(body, name=name, grid=(nt,),
                  in_specs=[row, BS((8, C), lambda i: (jnp.maximum(i * per - 1, 0), 0)), BS((4, C), lambda i: (0, 0)),
                            row, BS((8, C), lambda i: (jnp.minimum((i + 1) * per, T // 8 - 1), 0))],
                  out_specs=[row, BS((8, C), lambda i: (0, 0))],
                  out_shape=[SDS((T, C), F32), SDS((8, C), F32)],
                  compiler_params=_cparams(("arbitrary",)))(x, x, w, dy, dy)


@functools.partial(jax.custom_vjp, nondiff_argnums=(3,))
def conv4(x, w, b, name):
    return _conv_fwd_call(x, w, b, name)


def _conv4_fwd(x, w, b, name):
    return _conv_fwd_call(x, w, b, name), (x, w)


def _conv4_bwd(name, res, dy):
    x, w = res
    dx, dwb = _conv_bwd_call(x, w, dy, name + "_bwd")
    return dx, dwb[0:4], dwb[4]


conv4.defvjp(_conv4_fwd, _conv4_bwd)


def _gates_fwd_call(xc, wa, ba, wx, bx, name):
    T, C = xc.shape
    hd = C // LRU_HEADS
    tt = _tile(T, 512, 8)

    def body(x_ref, wa_ref, ba_ref, wx_ref, bx_ref, ga_ref, gx_ref):
        xb = x_ref[...].astype(BF16)
        ga_ref[...] = jnp.dot(xb, wa_ref[0].astype(BF16), preferred_element_type=F32) + ba_ref[...]
        gx_ref[...] = jnp.dot(xb, wx_ref[0].astype(BF16), preferred_element_type=F32) + bx_ref[...]

    blk = BS((tt, hd), lambda i, h: (i, h))
    wsp = BS((1, hd, hd), lambda i, h: (h, 0, 0))
    bsp = BS((1, hd), lambda i, h: (0, h))
    return _pcall(body, name=name, grid=(T // tt, LRU_HEADS), in_specs=[blk, wsp, bsp, wsp, bsp],
                  out_specs=[blk, blk], out_shape=[SDS((T, C), F32)] * 2,
                  compiler_params=_cparams(("parallel", "parallel")))(xc, wa, ba.reshape(1, C), wx, bx.reshape(1, C))


def _gates_bwd_x_call(dga, dgx, wa, wx, name):
    T, C = dga.shape
    hd = C // LRU_HEADS
    tt = _tile(T, 512, 8)
    dn = (((1,), (1,)), ((), ()))

    def body(da_ref, dx_ref, wa_ref, wx_ref, o_ref):
        o_ref[...] = (lax.dot_general(da_ref[...].astype(BF16), wa_ref[0].astype(BF16), dn, preferred_element_type=F32)
                      + lax.dot_general(dx_ref[...].astype(BF16), wx_ref[0].astype(BF16), dn, preferred_element_type=F32))

    blk = BS((tt, hd), lambda i, h: (i, h))
    wsp = BS((1, hd, hd), lambda i, h: (h, 0, 0))
    return _pcall(body, name=name, grid=(T // tt, LRU_HEADS), in_specs=[blk, blk, wsp, wsp], out_specs=blk,
                  out_shape=SDS((T, C), F32), compiler_params=_cparams(("parallel", "parallel")))(dga, dgx, wa, wx)


def _gates_bwd_w_call(xc, dga, dgx, name):
    T, C = xc.shape
    hd = C // LRU_HEADS
    tt = _tile(T, 512, 8)
    dn = (((0,), (0,)), ((), ()))

    def body(x_ref, da_ref, dx_ref, dwa_ref, dwx_ref, dba_ref, dbx_ref):
        @pl.when(pl.program_id(1) == 0)
        def _():
            dwa_ref[...] = jnp.zeros_like(dwa_ref)
            dwx_ref[...] = jnp.zeros_like(dwx_ref)
            dba_ref[...] = jnp.zeros_like(dba_ref)
            dbx_ref[...] = jnp.zeros_like(dbx_ref)

        xb = x_ref[...].astype(BF16)
        da, dx = da_ref[...], dx_ref[...]
        dwa_ref[0] += lax.dot_general(xb, da.astype(BF16), dn, preferred_element_type=F32)
        dwx_ref[0] += lax.dot_general(xb, dx.astype(BF16), dn, preferred_element_type=F32)
        dba_ref[...] += jnp.sum(da, axis=0, keepdims=True)
        dbx_ref[...] += jnp.sum(dx, axis=0, keepdims=True)

    blk = BS((tt, hd), lambda h, i: (i, h))
    wsp = BS((1, hd, hd), lambda h, i: (h, 0, 0))
    bsp = BS((1, hd), lambda h, i: (0, h))
    return _pcall(body, name=name, grid=(LRU_HEADS, T // tt), in_specs=[blk, blk, blk],
                  out_specs=[wsp, wsp, bsp, bsp],
                  out_shape=[SDS((LRU_HEADS, hd, hd), F32)] * 2 + [SDS((1, C), F32)] * 2,
                  compiler_params=_cparams(("parallel", "arbitrary")))(xc, dga, dgx)


@functools.partial(jax.custom_vjp, nondiff_argnums=(7,))
def lru_gates(xc, wa, wa_c, ba, wx, wx_c, bx, name):
    return tuple(_gates_fwd_call(xc, wa, ba, wx, bx, name))


def _lru_gates_fwd(xc, wa, wa_c, ba, wx, wx_c, bx, name):
    return tuple(_gates_fwd_call(xc, wa, ba, wx, bx, name)), (xc, wa, wx, ba.shape)


def _lru_gates_bwd(name, res, g):
    xc, wa, wx, bshape = res
    dga, dgx = g
    dxc = _gates_bwd_x_call(dga, dgx, wa, wx, name + "_dx")
    dwa, dwx, dba, dbx = _gates_bwd_w_call(xc, dga, dgx, name + "_dw")
    return dxc, jnp.zeros_like(wa), dwa, dba.reshape(bshape), jnp.zeros_like(wx), dwx, dbx.reshape(bshape)


lru_gates.defvjp(_lru_gates_fwd, _lru_gates_bwd)


def _lru_coeffs(xc, ga, gx, lam):
    r = _sigmoid(ga)
    ig = _sigmoid(gx)
    z = -lam
    sp = jnp.maximum(z, 0.0) + jnp.log(1.0 + jnp.exp(-jnp.abs(z)))
    la = -LRU_C * r * sp
    a = jnp.exp(la)
    s = jnp.sqrt(-_expm1(2.0 * la))
    return r, ig, sp, a, s


LRU_TT = 256


def _scan_fwd_call(xc, ga, gx, y, lam, name):
    T, C = xc.shape
    tt = _tile(T, LRU_TT, 8)

    def body(xc_ref, ga_ref, gx_ref, y_ref, lam_ref, h_ref, rec_ref, a_buf, carry):
        @pl.when(pl.program_id(0) == 0)
        def _():
            carry[...] = jnp.zeros_like(carry)

        xcv = xc_ref[...]
        _, ig, _, a, s = _lru_coeffs(xcv, ga_ref[...], gx_ref[...], lam_ref[...])
        a_buf[...] = a
        h_ref[...] = s * (ig * xcv)

        def step(t, h):
            hn = a_buf[pl.ds(t, 1), :] * h + h_ref[pl.ds(t, 1), :]
            h_ref[pl.ds(t, 1), :] = hn
            return hn

        carry[0:1, :] = lax.fori_loop(0, tt, step, carry[0:1, :], unroll=8)
        g, _ = _gelu_parts(y_ref[...])
        rec_ref[...] = h_ref[...] * g

    row = BS((tt, C), lambda i: (i, 0))
    vec = BS((1, C), lambda i: (0, 0))
    return _pcall(body, name=name, grid=(T // tt,), in_specs=[row, row, row, row, vec], out_specs=[row, row],
                  out_shape=[SDS((T, C), F32)] * 2,
                  scratch_shapes=[pltpu.VMEM((tt, C), F32), pltpu.VMEM((8, C), F32)],
                  compiler_params=_cparams(("arbitrary",)))(xc, ga, gx, y, lam.reshape(1, C))


def _scan_bwd_call(xc, ga, gx, y, lam, h, drec, name):
    T, C = xc.shape
    tt = _tile(T, LRU_TT, 8)
    nt = T // tt
    per = tt // 8

    def body(xc_ref, ga_ref, gx_ref, y_ref, lam_ref, h_ref, halo_ref, dr_ref,
             dga_ref, dgx_ref, dxc_ref, dy_ref, dlam_ref, a_buf, g_buf, carry):
        i = pl.program_id(0)

        @pl.when(i == 0)
        def _():
            carry[...] = jnp.zeros_like(carry)
            dlam_ref[...] = jnp.zeros_like(dlam_ref)

        xcv, lam = xc_ref[...], lam_ref[...]
        r, ig, sp, a, s = _lru_coeffs(xcv, ga_ref[...], gx_ref[...], lam)
        gel, dgel = _gelu_parts(y_ref[...])
        drec = dr_ref[...]
        hv = h_ref[...]
        dy_ref[...] = drec * hv * dgel
        a_buf[...] = a
        g_buf[...] = drec * gel

        def step(j, q):
            t = tt - 1 - j
            g = g_buf[pl.ds(t, 1), :] + q
            g_buf[pl.ds(t, 1), :] = g
            return a_buf[pl.ds(t, 1), :] * g

        carry[0:1, :] = lax.fori_loop(0, tt, step, carry[0:1, :], unroll=8)
        g = g_buf[...]
        halo = jnp.where(i < nt - 1, halo_ref[...], 0.0)
        hprev = pltpu.roll(jnp.concatenate([halo, hv], axis=0), 1, axis=0)[8:, :]
        da = g * hprev
        dig = g * s * xcv
        ds = g * ig * xcv
        dla = da * a - ds * (a * a) / s
        dga_ref[...] = dla * (-LRU_C * sp) * r * (1.0 - r)
        dgx_ref[...] = dig * ig * (1.0 - ig)
        dxc_ref[...] = g * s * ig
        dlam_ref[...] += jnp.sum(dla * r, axis=0, keepdims=True) * (LRU_C * _sigmoid(-lam))

    row = BS((tt, C), lambda i: (nt - 1 - i, 0))
    vec = BS((1, C), lambda i: (0, 0))
    halo = BS((8, C), lambda i: (jnp.maximum((nt - 1 - i) * per - 1, 0), 0))
    return _pcall(body, name=name, grid=(nt,), in_specs=[row, row, row, row, vec, row, halo, row],
                  out_specs=[row, row, row, row, vec], out_shape=[SDS((T, C), F32)] * 4 + [SDS((1, C), F32)],
                  scratch_shapes=[pltpu.VMEM((tt, C), F32), pltpu.VMEM((tt, C), F32), pltpu.VMEM((8, C), F32)],
                  compiler_params=_cparams(("arbitrary",)))(xc, ga, gx, y, lam.reshape(1, C), h, h, drec)


@functools.partial(jax.custom_vjp, nondiff_argnums=(5,))
def lru_scan(xc, ga, gx, y, lam, name):
    return _scan_fwd_call(xc, ga, gx, y, lam, name)[1]


def _lru_scan_fwd(xc, ga, gx, y, lam, name):
    h, rec = _scan_fwd_call(xc, ga, gx, y, lam, name)
    return rec, (xc, ga, gx, y, lam, h)


def _lru_scan_bwd(name, res, drec):
    xc, ga, gx, y, lam, h = res
    dga, dgx, dxc, dy, dlam = _scan_bwd_call(xc, ga, gx, y, lam, h, drec, name + "_bwd")
    return dxc, dga, dgx, dy, dlam.reshape(lam.shape)


lru_scan.defvjp(_lru_scan_fwd, _lru_scan_bwd)


def _att_batch(d, shared=False):
    if shared:
        return 8, 1
    return (4, 1) if d == 1 else (1, min(d, 8))


def _att_masks(n, max_dist):
    qi = lax.broadcasted_iota(jnp.int32, (1, 2 * ATT_BLOCK, 2 * ATT_BLOCK), 1) & (ATT_BLOCK - 1)
    kj = lax.broadcasted_iota(jnp.int32, (1, 2 * ATT_BLOCK, 2 * ATT_BLOCK), 2)
    prev = (kj < ATT_BLOCK) & (kj >= qi + (ATT_BLOCK - max_dist)) & (n > 0)
    cur = (kj >= ATT_BLOCK) & (kj - ATT_BLOCK <= qi)
    return prev | cur


def _lo_lanes(rows):
    return lax.broadcasted_iota(jnp.int32, (rows, LANES), 1) < HEAD_DIM


def _lane_half(rows):
    return lax.broadcasted_iota(jnp.int32, (rows, LANES), 1) // HEAD_DIM


def _stack_heads(x2):
    lo = _lo_lanes(ATT_BLOCK)
    zero = jnp.zeros_like(x2)
    return jnp.concatenate([jnp.where(lo, x2, zero), jnp.where(lo, zero, x2)], axis=0)


def _unstack_heads(y):
    return jnp.where(_lo_lanes(ATT_BLOCK), y[:ATT_BLOCK], y[ATT_BLOCK:])


def _per_head_col(x2):
    return jnp.concatenate([x2[:, 0:1], x2[:, HEAD_DIM:HEAD_DIM + 1]], axis=0)


def _head_sums(x2):
    lo = _lo_lanes(ATT_BLOCK)
    return jnp.concatenate([jnp.sum(jnp.where(lo, x2, 0.0), axis=1, keepdims=True),
                            jnp.sum(jnp.where(lo, 0.0, x2), axis=1, keepdims=True)], axis=0)


def _att_specs(d, Wq, Wk, nb, clamp):
    shared = Wk != Wq
    cgw, sb = _att_batch(d, shared)
    cur = (lambda n: jnp.minimum(n, nb - 1)) if clamp else (lambda n: n)
    rows, qw, kw = ATT_BLOCK * d, cgw * LANES, (LANES if shared else cgw * LANES)
    kcol = (lambda g: 0) if shared else (lambda g: g)
    qsp = BS((rows, qw), lambda g, n: (cur(n), g))
    csp = BS((rows, kw), lambda g, n: (cur(n), kcol(g)))
    psp = BS((rows, kw), lambda g, n: (jnp.maximum(cur(n) - 1, 0), kcol(g)))
    return cgw, sb, shared, qsp, csp, psp, qw, kw


def _att_streams(d, sb, work):
    if d == 1:
        work([slice(None)])
        return

    def one(j, carry):
        work([pl.ds(j * sb + i, ATT_BLOCK, stride=d) for i in range(sb)])
        return carry

    lax.fori_loop(0, d // sb, one, 0)


def _att_problem_loads(rows, cgw, shared, g, q_ref, kc_ref, kp_ref, vc_ref, vp_ref, sk_ref):
    half = _lane_half(ATT_BLOCK)

    def kv(ref, r, p):
        x = ref[r, :]
        if not shared:
            return x[:, p * LANES:(p + 1) * LANES]
        return jnp.where(half == p // 4, x, pltpu.roll(x, HEAD_DIM, axis=1))

    qs, kb, vb, sk = [], [], [], []
    for r in rows:
        qrow = q_ref[r, :]
        for p in range(cgw):
            cols = slice(p * LANES, (p + 1) * LANES)
            qs.append(_stack_heads(qrow[:, cols].astype(BF16)))
            kb.append(jnp.concatenate([kv(kp_ref, r, p), kv(kc_ref, r, p)], axis=0).astype(BF16))
            vb.append(jnp.concatenate([kv(vp_ref, r, p), kv(vc_ref, r, p)], axis=0).astype(BF16))
            sk.append(_per_head_col(jnp.broadcast_to(sk_ref[:, cols], (ATT_BLOCK, LANES))))
    return jnp.stack(qs), jnp.stack(kb), jnp.stack(vb), jnp.stack(sk)


_BDOT_NT = (((2,), (2,)), ((0,), (0,)))
_BDOT_NN = (((2,), (1,)), ((0,), (0,)))
_BDOT_TN = (((1,), (1,)), ((0,), (0,)))


def _att_fwd_call(q, k, v, sinks, d, max_dist, name):
    T, Wq = q.shape
    Wk = k.shape[1]
    nb = T // (d * ATT_BLOCK)
    cgw, sb, shared, qsp, csp, psp, qw, kw = _att_specs(d, Wq, Wk, nb, False)
    G = Wq // qw
    assert not shared or (d == 1 and Wk == LANES and G == 1 and cgw == 8), "a shared kv pair serves 2 x 8 query heads"

    def body(q_ref, kc_ref, kp_ref, vc_ref, vp_ref, sk_ref, o_ref, lse_ref):
        g, n = pl.program_id(0), pl.program_id(1)

        def work(rows):
            qs, kband, vband, sk = _att_problem_loads(rows, cgw, shared, g, q_ref, kc_ref, kp_ref, vc_ref, vp_ref, sk_ref)
            s = lax.dot_general(qs, kband, _BDOT_NT, preferred_element_type=F32) * (HEAD_DIM ** -0.5)
            s = jnp.where(_att_masks(n, max_dist), s, NEG)
            m = jnp.maximum(jnp.max(s, axis=-1, keepdims=True), sk)
            e = jnp.exp(s - m)
            den = jnp.sum(e, axis=-1, keepdims=True) + jnp.exp(sk - m)
            o = lax.dot_general((e * (1.0 / den)).astype(BF16), vband, _BDOT_NN, preferred_element_type=F32)
            lse = jnp.broadcast_to(m + jnp.log(den), o.shape)
            for i, r in enumerate(rows):
                o_ref[r, :] = jnp.concatenate([_unstack_heads(o[i * cgw + p]) for p in range(cgw)], axis=1)
                lse_ref[r, :] = jnp.concatenate([_unstack_heads(lse[i * cgw + p]) for p in range(cgw)], axis=1)

        _att_streams(d, sb, work)

    sksp = BS((1, qw), lambda g, n: (0, g))
    return _pcall(body, name=name, grid=(G, nb), in_specs=[qsp, csp, psp, csp, psp, sksp], out_specs=[qsp, qsp],
                  out_shape=[SDS((T, Wq), F32)] * 2,
                  compiler_params=_cparams(("parallel", "parallel")))(q, k, k, v, v, sinks)


def _att_bwd_call(q, k, v, sinks, o, lse, do, dlse, d, max_dist, name):
    T, Wq = q.shape
    Wk = k.shape[1]
    nb = T // (d * ATT_BLOCK)
    cgw, sb, shared, qsp, csp, psp, qw, kw = _att_specs(d, Wq, Wk, nb, True)
    G = Wq // qw
    scale = HEAD_DIM ** -0.5

    def body(*refs):
        q_ref, kc_ref, kp_ref, vc_ref, vp_ref, sk_ref, o_ref, lse_ref, do_ref = refs[:9]
        dlse_ref = refs[9] if dlse is not None else None
        dq_ref, dk_ref, dv_ref, dsk_ref, ck, cv = refs[-6:]
        g, n = pl.program_id(0), pl.program_id(1)

        @pl.when(n == 0)
        def _():
            ck[...] = jnp.zeros_like(ck)
            cv[...] = jnp.zeros_like(cv)
            dsk_ref[...] = jnp.zeros_like(dsk_ref)

        def work(rows):
            qs, kband, vband, sk = _att_problem_loads(rows, cgw, shared, g, q_ref, kc_ref, kp_ref, vc_ref, vp_ref, sk_ref)
            dos, lse_c, corr = [], [], []
            for r in rows:
                do_r, o_r, lse_r = do_ref[r, :], o_ref[r, :], lse_ref[r, :]
                dlse_r = dlse_ref[r, :] if dlse_ref is not None else None
                for p in range(cgw):
                    cols = slice(p * LANES, (p + 1) * LANES)
                    dos.append(_stack_heads(do_r[:, cols].astype(BF16)))
                    lse_c.append(_per_head_col(lse_r[:, cols]))
                    delta = _head_sums(do_r[:, cols] * o_r[:, cols])
                    corr.append(-delta if dlse_r is None else _head_sums(dlse_r[:, cols]) - delta)
            dos, lse_c, corr = jnp.stack(dos), jnp.stack(lse_c), jnp.stack(corr)
            s = lax.dot_general(qs, kband, _BDOT_NT, preferred_element_type=F32) * scale
            pr = jnp.exp(jnp.where(_att_masks(n, max_dist), s, NEG) - lse_c)
            dp = lax.dot_general(dos, vband, _BDOT_NT, preferred_element_type=F32)
            ds = (pr * (dp + corr)).astype(BF16)
            dq = lax.dot_general(ds, kband, _BDOT_NN, preferred_element_type=F32) * scale
            dkb = lax.dot_general(ds, qs, _BDOT_TN, preferred_element_type=F32) * scale
            dvb = lax.dot_general(pr.astype(BF16), dos, _BDOT_TN, preferred_element_type=F32)
            dsk = jnp.exp(sk - lse_c) * corr
            lane = lax.broadcasted_iota(jnp.int32, (8, LANES), 1)
            for p in range(cgw):
                tot = [jnp.sum(jnp.stack([dsk[i * cgw + p, h * ATT_BLOCK:(h + 1) * ATT_BLOCK] for i in range(len(rows))]),
                               axis=(0, 1)).reshape(1, 1) for h in range(2)]
                dsk_ref[:, p * LANES:(p + 1) * LANES] += jnp.where(lane == 0, tot[0], jnp.where(lane == HEAD_DIM, tot[1], 0.0))

            def gather_pairs(parts):
                if not shared:
                    return jnp.concatenate(parts, axis=1)
                tot = [parts[4 * h] + parts[4 * h + 1] + parts[4 * h + 2] + parts[4 * h + 3] for h in range(2)]
                tot = [t + pltpu.roll(t, HEAD_DIM, axis=1) for t in tot]
                return jnp.where(_lo_lanes(ATT_BLOCK), tot[0], tot[1])

            for i, r in enumerate(rows):
                mine = range(i * cgw, (i + 1) * cgw)
                dq_ref[r, :] = jnp.concatenate([_unstack_heads(dq[b]) for b in mine], axis=1)
                dk_ref[r, :] = ck[r, :] + gather_pairs([dkb[b, :ATT_BLOCK] for b in mine])
                dv_ref[r, :] = cv[r, :] + gather_pairs([dvb[b, :ATT_BLOCK] for b in mine])
                ck[r, :] = gather_pairs([dkb[b, ATT_BLOCK:] for b in mine])
                cv[r, :] = gather_pairs([dvb[b, ATT_BLOCK:] for b in mine])

        @pl.when(n < nb)
        def _():
            _att_streams(d, sb, work)

        @pl.when(n == nb)
        def _():
            dk_ref[...] = ck[...]
            dv_ref[...] = cv[...]

    sksp = BS((1, qw), lambda g, n: (0, g))
    rows = ATT_BLOCK * d
    osp = BS((rows, kw), lambda g, n: (jnp.maximum(n - 1, 0), 0 if shared else g))
    kshape = SDS((T, Wk), F32)
    dq, dk, dv, dsk = _pcall(
        body, name=name, grid=(G, nb + 1),
        in_specs=[qsp, csp, psp, csp, psp, sksp, qsp, qsp, qsp] + ([qsp] if dlse is not None else []),
        out_specs=[qsp, osp, osp, BS((8, qw), lambda g, n: (0, g))],
        out_shape=[SDS((T, Wq), F32), kshape, kshape, SDS((8, Wq), F32)],
        scratch_shapes=[pltpu.VMEM((rows, kw), F32)] * 2,
        compiler_params=_cparams(("parallel", "arbitrary")))(q, k, k, v, v, sinks, o, lse, do,
                                                              *([dlse] if dlse is not None else []))
    return dq, dk, dv, dsk[0:1]


@functools.partial(jax.custom_vjp, nondiff_argnums=(4, 5, 6, 7))
def band_attention(q, k, v, sinks, d, max_dist, with_lse, name):
    return _band_attention_fwd(q, k, v, sinks, d, max_dist, with_lse, name)[0]


def _band_attention_fwd(q, k, v, sinks, d, max_dist, with_lse, name):
    o, lse = _att_fwd_call(q, k, v, sinks, d, max_dist, name)
    return ((o, lse) if with_lse else o), (q, k, v, sinks, o, lse)


def _band_attention_bwd(d, max_dist, with_lse, name, res, g):
    q, k, v, sinks, o, lse = res
    do, dlse = g if with_lse else (g, None)
    return _att_bwd_call(q, k, v, sinks, o, lse, do, dlse, d, max_dist, name + "_bwd")


band_attention.defvjp(_band_attention_fwd, _band_attention_bwd)


def _merge_weights(ls):
    mx = jnp.maximum(jnp.maximum(ls[0], ls[1]), ls[2])
    es = [jnp.exp(l - mx) for l in ls]
    inv = 1.0 / (es[0] + es[1] + es[2])
    return [e * inv for e in es]


def _merge_fwd_call(os_, ls_, name):
    T, W = os_[0].shape
    tt = _tile(T, 512, 8)

    def body(o1, o2, o3, l1, l2, l3, out_ref):
        w = _merge_weights([l1[...], l2[...], l3[...]])
        out_ref[...] = w[0] * o1[...] + w[1] * o2[...] + w[2] * o3[...]

    row = BS((tt, W), lambda i: (i, 0))
    return _pcall(body, name=name, grid=(T // tt,), in_specs=[row] * 6, out_specs=row,
                  out_shape=SDS((T, W), F32), compiler_params=_cparams(("parallel",)))(*os_, *ls_)


def _merge_bwd_call(os_, ls_, do, name):
    T, W = os_[0].shape
    tt = _tile(T, 512, 8)

    def body(o1, o2, o3, l1, l2, l3, do_ref, d1, d2, d3, e1, e2, e3):
        w = _merge_weights([l1[...], l2[...], l3[...]])
        dov = do_ref[...]
        ts = [dov * o[...] for o in (o1, o2, o3)]
        mean = w[0] * ts[0] + w[1] * ts[1] + w[2] * ts[2]
        for wi, ti, dref, eref in zip(w, ts, (d1, d2, d3), (e1, e2, e3)):
            dref[...] = wi * dov
            eref[...] = wi * (ti - mean)

    row = BS((tt, W), lambda i: (i, 0))
    return _pcall(body, name=name, grid=(T // tt,), in_specs=[row] * 7, out_specs=[row] * 6,
                  out_shape=[SDS((T, W), F32)] * 6, compiler_params=_cparams(("parallel",)))(*os_, *ls_, do)


@functools.partial(jax.custom_vjp, nondiff_argnums=(2,))
def merge3(os_, ls_, name):
    return _merge_fwd_call(os_, ls_, name)


def _merge3_fwd(os_, ls_, name):
    return _merge_fwd_call(os_, ls_, name), (os_, ls_)


def _merge3_bwd(name, res, do):
    os_, ls_ = res
    out = _merge_bwd_call(os_, ls_, do, name + "_bwd")
    return tuple(out[:3]), tuple(out[3:])


merge3.defvjp(_merge3_fwd, _merge3_bwd)


def _xa_probs(qb, kb, scale):
    s = lax.dot_general(qb, kb, (((1,), (1,)), ((), ())), preferred_element_type=F32) * scale
    e = jnp.exp(s - jnp.max(s, axis=-1, keepdims=True))
    return e / jnp.sum(e, axis=-1, keepdims=True)


def _xa_fwd_call(q, kv, name):
    T, W = q.shape
    M = kv.shape[0]
    hd = XA_HEAD_DIM
    tq = _tile(T, 512, 8)
    scale = hd ** -0.5

    def body(q_ref, k_ref, v_ref, o_ref):
        p = _xa_probs(q_ref[...].astype(BF16), k_ref[...].astype(BF16), scale)
        o_ref[...] = jnp.dot(p.astype(BF16), v_ref[...].astype(BF16), preferred_element_type=F32)

    qsp = BS((tq, hd), lambda i, h: (i, h))
    return _pcall(body, name=name, grid=(T // tq, XA_HEADS),
                  in_specs=[qsp, BS((M, hd), lambda i, h: (0, h)), BS((M, hd), lambda i, h: (0, XA_HEADS + h))],
                  out_specs=qsp, out_shape=SDS((T, W), F32),
                  compiler_params=_cparams(("parallel", "parallel")))(q, kv, kv)


def _xa_bwd_call(q, kv, do, name):
    T, W = q.shape
    M = kv.shape[0]
    hd = XA_HEAD_DIM
    tq = _tile(T, 512, 8)
    scale = hd ** -0.5
    dn_nt = (((1,), (1,)), ((), ()))
    dn_tn = (((0,), (0,)), ((), ()))

    def body(q_ref, k_ref, v_ref, do_ref, dq_ref, dk_ref, dv_ref):
        @pl.when(pl.program_id(1) == 0)
        def _():
            dk_ref[...] = jnp.zeros_like(dk_ref)
            dv_ref[...] = jnp.zeros_like(dv_ref)

        qb, kb, vb = q_ref[...].astype(BF16), k_ref[...].astype(BF16), v_ref[...].astype(BF16)
        p = _xa_probs(qb, kb, scale)
        dob = do_ref[...].astype(BF16)
        dp = lax.dot_general(dob, vb, dn_nt, preferred_element_type=F32)
        ds = (p * (dp - jnp.sum(p * dp, axis=-1, keepdims=True))).astype(BF16)
        dq_ref[...] = jnp.dot(ds, kb, preferred_element_type=F32) * scale
        dk_ref[...] += lax.dot_general(ds, qb, dn_tn, preferred_element_type=F32) * scale
        dv_ref[...] += lax.dot_general(p.astype(BF16), dob, dn_tn, preferred_element_type=F32)

    qsp = BS((tq, hd), lambda h, i: (i, h))
    ksp = BS((M, hd), lambda h, i: (0, h))
    return _pcall(body, name=name, grid=(XA_HEADS, T // tq),
                  in_specs=[qsp, ksp, BS((M, hd), lambda h, i: (0, XA_HEADS + h)), qsp],
                  out_specs=[qsp, ksp, ksp], out_shape=[SDS((T, W), F32), SDS((M, W), F32), SDS((M, W), F32)],
                  compiler_params=_cparams(("parallel", "arbitrary")))(q, kv, kv, do)


@functools.partial(jax.custom_vjp, nondiff_argnums=(2,))
def cross_attention(q, kv, name):
    return _xa_fwd_call(q, kv, name)


def _cross_attention_fwd(q, kv, name):
    return _xa_fwd_call(q, kv, name), (q, kv)


def _cross_attention_bwd(name, res, do):
    q, kv = res
    dq, dk, dv = _xa_bwd_call(q, kv, do, name + "_bwd")
    return dq, jnp.concatenate([dk, dv], axis=1)


cross_attention.defvjp(_cross_attention_fwd, _cross_attention_bwd)


def _gate_up_swiglu_call(hn, w1t, name):
    T, K = hn.shape
    F = w1t.shape[0] // 2
    tm, tn = _tile(T, 2048), _tile(F, 256)
    nj = F // tn
    dn = (((1,), (1,)), ((), ()))

    def body(a_ref, wg_ref, wu_ref, g_ref, u_ref, act_ref):
        a = a_ref[...]
        g = lax.dot_general(a, wg_ref[...], dn, preferred_element_type=F32)
        u = lax.dot_general(a, wu_ref[...], dn, preferred_element_type=F32)
        g_ref[...] = g
        u_ref[...] = u
        act_ref[...] = ((g * _sigmoid(g)) * u).astype(BF16)

    tile = BS((tm, tn), lambda i, j: (i, j))
    return _pcall(body, name=name, grid=(T // tm, nj),
                  in_specs=[BS((tm, K), lambda i, j: (i, 0)), BS((tn, K), lambda i, j: (j, 0)),
                            BS((tn, K), lambda i, j: (j + nj, 0))],
                  out_specs=[tile, tile, tile], out_shape=[SDS((T, F), F32), SDS((T, F), F32), SDS((T, F), BF16)],
                  compiler_params=_cparams(("parallel", "parallel")))(hn, w1t, w1t)


def _swiglu_bwd_call(g, u, dact, name):
    T, F = g.shape
    tt = _tile(T, 128, 16)

    def body(g_ref, u_ref, d_ref, o_ref):
        g, d = g_ref[...], d_ref[...]
        sg = _sigmoid(g)
        o_ref[:, :F] = (d * u_ref[...] * (sg * (1.0 + g * (1.0 - sg)))).astype(BF16)
        o_ref[:, F:] = (d * (g * sg)).astype(BF16)

    row = BS((tt, F), lambda i: (i, 0))
    return _pcall(body, name=name, grid=(T // tt,), in_specs=[row, row, row],
                  out_specs=BS((tt, 2 * F), lambda i: (i, 0)), out_shape=SDS((T, 2 * F), BF16),
                  compiler_params=_cparams(("parallel",)))(g, u, dact)


@functools.partial(jax.custom_vjp, nondiff_argnums=(6,))
def ffn_block(h, g, w1b, w1c, w2b, w2c, name):
    return _ffn_fwd(h, g, w1b, w1c, w2b, w2c, name)[0]


def _ffn_fwd(h, g, w1b, w1c, w2b, w2c, name):
    hn = _rms_fwd_call(h, g, name + "_norm", BF16)
    gate, up, act = _gate_up_swiglu_call(hn, w1b, name + "_gu")
    out = _mm(act, w2b, add=h, name=name + "_down", tm=1024, tn=1024, tk=1408)
    return out, (h, g, hn, gate, up, act, w1b, w2b)


def _ffn_bwd(name, res, dout):
    h, g, hn, gate, up, act, w1b, w2b = res
    dact = _mm(dout, w2b, tb=True, name=name + "_down_da", tm=1024, tn=1408, tk=1024)
    dw2 = _mm(act, dout, ta=True, name=name + "_down_dw", tm=1408, tn=1024, tk=1024)
    dgu = _swiglu_bwd_call(gate, up, dact, name + "_swiglu_bwd")
    dw1 = _mm(dgu, hn, ta=True, name=name + "_gu_dw", tm=1408, tn=1024, tk=1024)
    dh, dg = _mm_rms_bwd(dgu, w1b, h, g, dout, name + "_gu_da", wt=True)
    return dh, dg.reshape(g.shape), jnp.zeros_like(w1b), dw1, jnp.zeros_like(w2b), dw2


ffn_block.defvjp(_ffn_fwd, _ffn_bwd)


def _final_call(h, g, target, name):
    T, Dm = h.shape
    tt = _tile(T, 512, 8)

    def body(x_ref, g_ref, t_ref, loss_ref, dx_ref, dg_ref):
        @pl.when(pl.program_id(0) == 0)
        def _():
            loss_ref[...] = jnp.zeros_like(loss_ref)
            dg_ref[...] = jnp.zeros_like(dg_ref)

        xv, gv = x_ref[...], g_ref[...]
        r = lax.rsqrt(jnp.mean(xv * xv, axis=-1, keepdims=True) + NORM_EPS)
        xh = xv * r
        err = xh * gv - t_ref[...]
        loss_ref[...] += 0.5 * jnp.sum(jnp.mean(err * err, axis=-1, keepdims=True), axis=0, keepdims=True)
        dy = err * (1.0 / Dm)
        dyg = dy * gv
        dx_ref[...] = r * (dyg - xh * jnp.mean(dyg * xh, axis=-1, keepdims=True))
        dg_ref[...] += jnp.sum(dy * xh, axis=0, keepdims=True)

    row = BS((tt, Dm), lambda i: (i, 0))
    vec = BS((1, Dm), lambda i: (0, 0))
    return _pcall(body, name=name, grid=(T // tt,), in_specs=[row, vec, row],
                  out_specs=[BS((1, 1), lambda i: (0, 0)), row, vec],
                  out_shape=[SDS((1, 1), F32), SDS((T, Dm), F32), SDS((1, Dm), F32)],
                  compiler_params=_cparams(("arbitrary",)))(h, g.reshape(1, Dm), target)


ADAMW_BLOCK_ELEMS = 64 * 1024


def _adamw_call(parts, w, m, v, name):
    shape = w.shape
    if not isinstance(parts, (list, tuple)):
        parts, shape3 = [parts], (1,) + shape
    else:
        shape3 = shape
    n_lead = shape3[0]
    r, N = shape3[-2], shape3[-1]
    Ld = math.prod(shape3[1:-2])
    w, m, v = (t.reshape(n_lead * Ld, r, N) for t in (w, m, v))
    tr = _tile(r, max(8, ADAMW_BLOCK_ELEMS // N), 8)
    c1 = 1.0 - ADAM_B1 ** ADAM_STEP
    c2 = 1.0 - ADAM_B2 ** ADAM_STEP
    outs = None
    for lead, p in enumerate(parts):
        def body(p_ref, w_ref, m_ref, v_ref, *rest):
            g_ref, d_ref, nm_ref, nv_ref = rest[-4:]
            g = p_ref[0]
            for j in range(1, N_DEV):
                g = g + p_ref[j]
            nm = ADAM_B1 * m_ref[...] + (1.0 - ADAM_B1) * g
            nv = ADAM_B2 * v_ref[...] + (1.0 - ADAM_B2) * (g * g)
            g_ref[...] = g
            nm_ref[...] = nm
            nv_ref[...] = nv
            d_ref[...] = -ADAM_LR * ((nm / c1) / (jnp.sqrt(nv / c2) + ADAM_EPS) + ADAM_WD * w_ref[...])

        base = lead * Ld
        row = BS((1, tr, N), lambda l, i, base=base: (base + l, i, 0))
        prev = [] if outs is None else list(outs)
        outs = _pcall(body, name=f"{name}_{lead}", grid=(Ld, r // tr),
                      in_specs=[BS((N_DEV, 1, tr, N), lambda l, i: (0, l, i, 0)), row, row, row]
                      + [BS(memory_space=pl.ANY)] * len(prev),
                      out_specs=[row] * 4, out_shape=[SDS((n_lead * Ld, r, N), F32)] * 4,
                      input_output_aliases={4 + j: j for j in range(len(prev))},
                      compiler_params=_cparams(("parallel", "parallel")))(p.reshape(N_DEV, Ld, r, N), w, m, v, *prev)
    return [t.reshape(shape) for t in outs]


def _place():
    return lax.axis_index("x"), lax.axis_index("y"), lax.axis_index("c")


def _all_gather(xs, name):
    n = len(xs)
    pairs = [(i, l) for i, x in enumerate(xs) for l in range(x.shape[0])]

    def body(*refs):
        x_refs, o_refs = refs[:n], refs[n:2 * n]
        send_sems, recv_sems, local_sems = refs[2 * n:]
        x_, y_, c_ = _place()
        me, sibling = (x_, y_, c_), (x_, y_, 1 - c_)
        chips = [(1 - x_, y_), (x_, 1 - y_), (1 - x_, 1 - y_)]

        def copy(e, k, block, to, from_input=False):
            i, l = pairs[e]
            px, py, pc = block
            dst = o_refs[i].at[l, 4 * px + 2 * py + pc]
            return pltpu.make_async_remote_copy(
                src_ref=x_refs[i].at[l] if from_input else dst, dst_ref=dst,
                send_sem=send_sems.at[7 * e + k], recv_sem=recv_sems.at[7 * e + k],
                device_id=to, device_id_type=pl.DeviceIdType.MESH)

        every = range(len(pairs))
        mine = [pltpu.make_async_copy(x_refs[i].at[l], o_refs[i].at[l, 4 * x_ + 2 * y_ + c_], local_sems.at[e])
                for e, (i, l) in enumerate(pairs)]
        for cp in mine:
            cp.start()
        first = [copy(e, 0, me, sibling, True) for e in every]
        first += [copy(e, 1 + j, me, (*chip, c_), True) for j, chip in enumerate(chips) for e in every]
        for cp in first:
            cp.start()
        passed = []
        for j, chip in enumerate(chips):
            for e in every:
                copy(e, 1 + j, (*chip, c_), me).wait_recv()
            for e in every:
                cp = copy(e, 4 + j, (*chip, c_), sibling)
                cp.start()
                passed.append(cp)
        for e in every:
            copy(e, 0, sibling, me).wait_recv()
        for j, chip in enumerate(chips):
            for e in every:
                copy(e, 4 + j, (*chip, 1 - c_), me).wait_recv()
        for cp in first + passed:
            cp.wait_send()
        for cp in mine:
            cp.wait()

    any_spec = BS(memory_space=pl.ANY)
    return _pcall(body, name=name, in_specs=[any_spec] * n, out_specs=[any_spec] * n,
                  out_shape=[SDS((x.shape[0], N_DEV) + x.shape[1:], x.dtype) for x in xs],
                  scratch_shapes=[pltpu.SemaphoreType.DMA((7 * len(pairs),)), pltpu.SemaphoreType.DMA((7 * len(pairs),)),
                                  pltpu.SemaphoreType.DMA((len(pairs),))],
                  compiler_params=pltpu.CompilerParams(has_side_effects=True))(*xs)


def _peer_of(k, place):
    x_, y_, c_ = place
    fx, fy, fc = (k >> 2) & 1, (k >> 1) & 1, k & 1
    return fx + x_ - 2 * fx * x_, fy + y_ - 2 * fy * y_, fc + c_ - 2 * fc * c_


def _split_copy(src_ref, land_ref, send_sems, recv_sems, e, k, place, scatter):
    x_, y_, c_ = place
    px, py, pc = _peer_of(k, place)
    return pltpu.make_async_remote_copy(
        src_ref=src_ref.at[4 * px + 2 * py + pc] if scatter else src_ref, dst_ref=land_ref.at[4 * x_ + 2 * y_ + c_],
        send_sem=send_sems.at[7 * e + k - 1], recv_sem=recv_sems.at[7 * e + k - 1],
        device_id=(px, py, pc), device_id_type=pl.DeviceIdType.MESH)


_HBM_SPEC = BS(memory_space=pltpu.HBM)
_SEM_SPEC = BS(memory_space=pltpu.SEMAPHORE)
_EFFECT = pltpu.SideEffectType.DATAFLOW_SIDE_EFFECTING


def _copies_start(srcs, scatter, name, thru=None):
    n = len(srcs)
    lands = [lax.empty(s.shape if scatter else (N_DEV,) + s.shape, s.dtype) for s in srcs]
    passed = srcs + lands + list(thru or ())

    def body(*refs):
        src_refs, land_refs = refs[:n], refs[n:2 * n]
        send_sems, recv_sems = refs[len(passed)], refs[len(passed) + 1]
        token = refs[-1]
        place = _place()
        for e in range(n):
            for k in range(1, N_DEV):
                _split_copy(src_refs[e], land_refs[e], send_sems, recv_sems, e, k, place, scatter).start()
        token[...] = jnp.zeros_like(token)

    hbm = lambda t: pltpu.with_memory_space_constraint(t, pltpu.HBM)
    out = _pcall(body, name=name,
                 out_shape=(pltpu.SemaphoreType.DMA((7 * n,)), pltpu.SemaphoreType.DMA((7 * n,)),
                            *[pltpu.HBM(t.shape, t.dtype) for t in passed], SDS((8, LANES), F32)),
                 in_specs=[_HBM_SPEC] * len(passed),
                 out_specs=(_SEM_SPEC, _SEM_SPEC, *[_HBM_SPEC] * len(passed), BS(memory_space=pltpu.VMEM)),
                 input_output_aliases={i: 2 + i for i in range(len(passed))},
                 compiler_params=pltpu.CompilerParams(has_side_effects=_EFFECT))(*[hbm(t) for t in passed])
    return out[0], out[1], list(out[2:2 + n]), list(out[2 + n:2 + 2 * n]), list(out[2 + 2 * n:-1])


def _copies_wait(started, which, scatter, after, name):
    send_sems, recv_sems, srcs, lands, _ = started
    n = len(which)
    after = list(after) if isinstance(after, (list, tuple)) else [after]

    def body(*refs):
        src_refs, land_refs = refs[:n], refs[n:2 * n]
        send_s, recv_s = refs[2 * n], refs[2 * n + 1]
        place = _place()
        for j, e in enumerate(which):
            for k in range(1, N_DEV):
                cp = _split_copy(src_refs[j], land_refs[j], send_s, recv_s, e, k, place, scatter)
                cp.wait_send()
                cp.wait_recv()

    args = [srcs[e] for e in which] + [lands[e] for e in which]
    out = _pcall(body, name=name, out_shape=tuple(pltpu.HBM(t.shape, t.dtype) for t in args),
                 in_specs=[_HBM_SPEC] * (2 * n) + [_SEM_SPEC, _SEM_SPEC] + [BS(memory_space=pl.ANY)] * len(after),
                 out_specs=tuple([_HBM_SPEC] * (2 * n)), input_output_aliases={i: i for i in range(2 * n)},
                 compiler_params=pltpu.CompilerParams(has_side_effects=_EFFECT))(*args, send_sems, recv_sems, *after)
    return list(out[:n]), list(out[n:])


def _with_own_block(land, own_block):
    me = 4 * lax.axis_index("x") + 2 * lax.axis_index("y") + lax.axis_index("c")
    return lax.dynamic_update_index_in_dim(land, own_block, me, 0)


def _pad_flat(t, quantum=PACK_QUANTUM):
    f = t.reshape(-1)
    pad = (-f.shape[0]) % quantum
    return jnp.pad(f, (0, pad)) if pad else f


def _pack(arrs, dtype):
    return jnp.concatenate([_pad_flat(a.astype(dtype)) for a in arrs]).reshape(-1, LANES)


def _unpack(buf, shapes, lead=()):
    flat = buf.reshape(lead + (-1,))
    out, off = [], 0
    for s in shapes:
        n = math.prod(s)
        out.append(flat[..., off:off + n].reshape(lead + tuple(s)))
        off += n + (-n) % PACK_QUANTUM
    return out


def _full_from_gathered(g, axis):
    t = jnp.moveaxis(g, 0, axis)
    s = t.shape
    return t.reshape(s[:axis] + (s[axis] * s[axis + 1],) + s[axis + 2:])


def _parts_from_full(t, axis):
    s = t.shape
    t = t.reshape(s[:axis] + (N_DEV, s[axis] // N_DEV) + s[axis + 1:])
    return jnp.moveaxis(t, axis, 0)


def _head_rows(t):
    return jnp.repeat(t, HEAD_DIM).reshape(1, -1)


def _dilated_attention(q, k, v, name):
    no_sink = jnp.full((1, q.shape[1]), NEG, F32)
    outs, lses = zip(*[band_attention(q, k, v, no_sink, d, ATT_BLOCK, True, f"{name}_d{d}") for d in DILATIONS])
    return merge3(tuple(outs), tuple(lses), name + "_merge")


STAGES = (
    ("proj0", ('mix_norm',), (('ab_w_in', 0),)),
    ("mixer0", ('lru_conv_w', 'lru_conv_b', 'lru_ba', 'lru_bx', 'lru_lambda'),
     (('lru_wa', 0), ('lru_wx', 0), ('ab_w_out', 0))),
    ("xa0", ('xa_norm', 'xa_mem_norm'), (('xa_wq', 0), ('xa_wkv', 0), ('xa_wo', 0))),
    ("ffn0", ('ffn_norm',), (('ffn_w_gate_up', 0), ('ffn_w_down', 0))),
    ("mixer1", ('mix_norm', 'c_b_qkv', 'c_sinks', 'c_b_out'), (('c_w_qkv', 0), ('c_w_out', 0))),
    ("xa1", ('xa_norm', 'xa_mem_norm'), (('xa_wq', 1), ('xa_wkv', 1), ('xa_wo', 1))),
    ("ffn1", ('ffn_norm',), (('ffn_w_gate_up', 1), ('ffn_w_down', 1))),
)


def _stage_fn(stage, Wb, tabs, mem):
    layer = int(stage[-1])
    L = f"l{layer}"

    def run(S, Cw, h):
        def lin(a, key, bias, add, name, rows=None):
            wb, wc = Wb[key], Cw[key]
            if rows is not None:
                wb, wc = wb[rows], wc[rows]
            return linear(a, wb, wc, bias, add, name)

        def norm_lin(a, gain, key, bias, name):
            return norm_linear(a, gain, Wb[key], Cw[key], bias, key[0] in COLUMN_CUT, name)

        if stage == "mixer0":
            h, x_br, y_br, q, k, v = h
            C = S['lru_conv_w'].shape[-1]
            xc = conv4(x_br, S['lru_conv_w'][0], S['lru_conv_b'][0], L + "_conv")
            ga, gx = lru_gates(xc, Wb['lru_wa', 0], Cw['lru_wa', 0], S['lru_ba'][0],
                               Wb['lru_wx', 0], Cw['lru_wx', 0], S['lru_bx'][0], L + "_gates")
            rec = lru_scan(xc, ga, gx, y_br, S['lru_lambda'][0], L + "_scan")
            att = _dilated_attention(q, k, v, L + "_att")
            h = lin(att, ('ab_w_out', 0), None, h, L + "_w_out_att", slice(C, None))
            return lin(rec, ('ab_w_out', 0), None, h, L + "_w_out_rec", slice(0, C))
        if stage == "mixer1":
            qw = C_HEADS * HEAD_DIM
            kw = C_KV_HEADS * HEAD_DIM
            q, k, v, h = norm_linear_pieces(h, S['mix_norm'][1], Wb['c_w_qkv', 0], Cw['c_w_qkv', 0], S['c_b_qkv'][0], tabs,
                                            ((qw, True), (kw, True), (kw, False)), L + "_w_qkv")
            o = band_attention(q, k, v, _head_rows(S['c_sinks'][0]), 1, ATT_BLOCK - 1, False, L + "_att")
            return lin(o, ('c_w_out', 0), S['c_b_out'][0], h, L + "_w_out")
        if stage.startswith("xa"):
            xq, h = norm_lin(h, S['xa_norm'][layer], ('xa_wq', layer), None, L + "_xa_wq")
            xkv, _ = norm_lin(mem, S['xa_mem_norm'][layer], ('xa_wkv', layer), None, L + "_xa_wkv")
            return lin(cross_attention(xq, xkv, L + "_xa"), ('xa_wo', layer), None, h, L + "_xa_wo")
        gu, down = ('ffn_w_gate_up', layer), ('ffn_w_down', layer)
        return ffn_block(h, S['ffn_norm'][layer], Wb[gu], Cw[gu], Wb[down], Cw[down], L + "_ffn")

    return run


def kernel(x, mem, mix_norm, ab_w_in, lru_conv_w, lru_conv_b, lru_wa, lru_ba, lru_wx, lru_bx, lru_lambda, ab_w_out, c_w_qkv, c_b_qkv, c_sinks, c_w_out, c_b_out, xa_norm, xa_mem_norm, xa_wq, xa_wkv, xa_wo, ffn_norm, ffn_w_gate_up, ffn_w_down, final_norm, loss_target, m_mix_norm, m_ab_w_in, m_lru_conv_w, m_lru_conv_b, m_lru_wa, m_lru_ba, m_lru_wx, m_lru_bx, m_lru_lambda, m_ab_w_out, m_c_w_qkv, m_c_b_qkv, m_c_sinks, m_c_w_out, m_c_b_out, m_xa_norm, m_xa_mem_norm, m_xa_wq, m_xa_wkv, m_xa_wo, m_ffn_norm, m_ffn_w_gate_up, m_ffn_w_down, m_final_norm, v_mix_norm, v_ab_w_in, v_lru_conv_w, v_lru_conv_b, v_lru_wa, v_lru_ba, v_lru_wx, v_lru_bx, v_lru_lambda, v_ab_w_out, v_c_w_qkv, v_c_b_qkv, v_c_sinks, v_c_w_out, v_c_b_out, v_xa_norm, v_xa_mem_norm, v_xa_wq, v_xa_wkv, v_xa_wo, v_ffn_norm, v_ffn_w_gate_up, v_ffn_w_down, v_final_norm):
    w_loc = dict(zip(WEIGHT_NAMES, (mix_norm, ab_w_in, lru_conv_w, lru_conv_b, lru_wa, lru_ba, lru_wx, lru_bx, lru_lambda, ab_w_out, c_w_qkv, c_b_qkv, c_sinks, c_w_out, c_b_out, xa_norm, xa_mem_norm, xa_wq, xa_wkv, xa_wo, ffn_norm, ffn_w_gate_up, ffn_w_down, final_norm)))
    m_loc = dict(zip(WEIGHT_NAMES, (m_mix_norm, m_ab_w_in, m_lru_conv_w, m_lru_conv_b, m_lru_wa, m_lru_ba, m_lru_wx, m_lru_bx, m_lru_lambda, m_ab_w_out, m_c_w_qkv, m_c_b_qkv, m_c_sinks, m_c_w_out, m_c_b_out, m_xa_norm, m_xa_mem_norm, m_xa_wq, m_xa_wkv, m_xa_wo, m_ffn_norm, m_ffn_w_gate_up, m_ffn_w_down, m_final_norm)))
    v_loc = dict(zip(WEIGHT_NAMES, (v_mix_norm, v_ab_w_in, v_lru_conv_w, v_lru_conv_b, v_lru_wa, v_lru_ba, v_lru_wx, v_lru_bx, v_lru_lambda, v_ab_w_out, v_c_w_qkv, v_c_b_qkv, v_c_sinks, v_c_w_out, v_c_b_out, v_xa_norm, v_xa_mem_norm, v_xa_wq, v_xa_wkv, v_xa_wo, v_ffn_norm, v_ffn_w_gate_up, v_ffn_w_down, v_final_norm)))

    me = 4 * lax.axis_index("x") + 2 * lax.axis_index("y") + lax.axis_index("c")

    first_keys = list(STAGES[0][2])
    keys = [key for _, _, stage_keys in STAGES[1:] for key in stage_keys]
    shards = [_shard_view(n, w_loc[n])[l].astype(BF16) for n, l in keys]
    first_g = _all_gather([_pack([w_loc[n] for n in SMALL], F32)[None]]
                          + [_shard_view(n, w_loc[n])[l].astype(BF16)[None] for n, l in first_keys], "gather_first")
    gather = _copies_start(shards, False, "gather_start", thru=[first_g[0]])
    small_g = gather[4][0][0]
    Wb = {key: _full_from_gathered(g[0], _layer_shard_axis(key[0])) for key, g in zip(first_keys, first_g[1:])}
    S = {n: w_loc[n] for n in REPLICATED}
    for n, t in zip(SMALL, _unpack(small_g, [w_loc[n].shape for n in SMALL], lead=(N_DEV,))):
        S[n] = _full_from_gathered(t, SHARD_AXIS[n])

    tabs = _rope_tables(x.shape[1])
    w_in = Wb[first_keys[0]]
    hn0 = _rms_fwd_call(x[0], S['mix_norm'][0], "l0_w_in_norm", BF16)
    lru_w, att_w = N_DEV * lru_conv_w.shape[-1], B_HEADS * HEAD_DIM
    pieces = (("x", lru_w, False), ("y", lru_w, False), ("q", att_w, True), ("k", att_w, True), ("v", att_w, False))
    h, row = [x[0]], 0
    for piece, width, rotated in pieces:
        h.append(_mm(hn0, w_in[row:row + width], tb=True, rope=tabs if rotated else None, name="l0_w_in_" + piece,
                     tm=1024, tn=1024, tk=1024))
        row += width
    h = tuple(h)
    vjps = []
    for stage, small_names, stage_keys in STAGES[1:]:
        which = [keys.index(key) for key in stage_keys]
        _, lands = _copies_wait(gather, which, False, jax.tree.leaves(h)[-1], "gather_wait_" + stage)
        for e, land in zip(which, lands):
            Wb[keys[e]] = _full_from_gathered(_with_own_block(land, shards[e]), _layer_shard_axis(keys[e][0]))
        carriers = {key: jnp.zeros(Wb[key].shape, F32) for key in stage_keys}
        h, vjp_fn = jax.vjp(_stage_fn(stage, Wb, tabs, mem[0]), {n: S[n] for n in small_names}, carriers, h)
        vjps.append(vjp_fn)
    loss_part, dh, dg_final = _final_call(h, S['final_norm'], loss_target[0], "final_loss")

    grads = {'final_norm': dg_final.reshape(final_norm.shape)}
    exchanges, send_keys, send_parts = [], [], []

    def start_exchange(stage, dh):
        leaves, tree = jax.tree.flatten(dh)
        started = _copies_start(list(send_parts), True, "grad_start_" + stage, thru=leaves)
        exchanges.append((stage, started, list(send_keys)))
        send_keys.clear()
        send_parts.clear()
        return jax.tree.unflatten(tree, started[4])

    for (stage, small_names, stage_keys), vjp_fn in zip(reversed(STAGES[1:]), reversed(vjps)):
        g_small, g_big, dh = vjp_fn(dh)
        for n in small_names:
            grads[n] = grads[n] + g_small[n] if n in grads else g_small[n]
        send_keys += list(stage_keys)
        send_parts += [_parts_from_full(g_big[key], _layer_shard_axis(key[0])) for key in stage_keys]
        if stage == "xa1":
            continue
        if stage == "mixer0":
            small_parts = [_parts_from_full(grads[n], SHARD_AXIS[n]) for n in SMALL]
            send_keys.append("small")
            send_parts.append(jnp.stack([_pack([p[j] for p in small_parts], F32) for j in range(N_DEV)]))
        dh = start_exchange(stage, dh)
    d_res, d_proj = dh[0], _join_pieces_call(dh[1:], [rotated for _, _, rotated in pieces], tabs, "l0_w_in_dproj")
    send_keys.append(first_keys[0])
    send_parts.append(_parts_from_full(_mm(d_proj, hn0, ta=True, name="l0_w_in_dw", tm=1408, tn=1024, tk=1024),
                                       _layer_shard_axis(first_keys[0][0])))
    d_res, d_proj = start_exchange("proj0", (d_res, d_proj))
    dx, dg0 = _mm_rms_bwd(d_proj, w_in, x[0], S['mix_norm'][0], d_res, "l0_w_in_da", wt=True)
    grads['mix_norm'] = grads['mix_norm'] + jnp.concatenate([dg0, jnp.zeros_like(dg0)], axis=0)
    rep_names = REPLICATED + ["loss"]
    grads["loss"] = loss_part
    zero = jnp.zeros((1, 1), F32)
    for d in (w_loc, m_loc, v_loc):
        d["loss"] = zero
    rep_started = _copies_start([_pack([grads[n] for n in rep_names], F32)], False, "rep_grads_start")

    parts, out = {}, {}

    def end_exchange(stage, started, ex_keys, after):
        srcs, lands = _copies_wait(started, list(range(len(ex_keys))), True, after, "grad_wait_" + stage)
        for key, src, land in zip(ex_keys, srcs, lands):
            parts[key] = _with_own_block(land, lax.dynamic_index_in_dim(src, me, 0, keepdims=False))

    def adamw(p, names, call_name):
        if len(names) == 1:
            n = names[0]
            res = _adamw_call(p, _shard_view(n, w_loc[n]), _shard_view(n, m_loc[n]), _shard_view(n, v_loc[n]), call_name)
            for kind, t in zip(("grad", "delta", "new_m", "new_v"), res):
                out[kind, n] = _shard_view(n, t)
        else:
            res = _adamw_call(p, *[_pack([d[n] for n in names], F32) for d in (w_loc, m_loc, v_loc)], call_name)
            for kind, buf in zip(("grad", "delta", "new_m", "new_v"), res):
                for n, t in zip(names, _unpack(buf, [w_loc[n].shape for n in names])):
                    out[kind, n] = t

    for ex in exchanges[:-1]:
        end_exchange(*ex, dx)
    last_names = {key[0] for key in exchanges[-1][2]}
    for n in BIG:
        if n not in last_names:
            adamw([parts[n, l] for l in range(w_loc[n].shape[0])], [n], "adamw_" + n)
    adamw(parts["small"], SMALL, "adamw_small")
    end_exchange(*exchanges[-1], [out["new_v", n] for n in BIG if n not in last_names])
    for n in BIG:
        if n in last_names:
            adamw([parts[n, l] for l in range(w_loc[n].shape[0])], [n], "adamw_" + n)
    rep_src, rep_land = _copies_wait(rep_started, [0], False, out["new_v", "ab_w_in"], "rep_grads_wait")
    adamw(_with_own_block(rep_land[0], rep_src[0]), rep_names, "adamw_replicated")
    loss = out["grad", "loss"][0, 0]

    return (loss, dx[None], *[out[kind, n] for kind in ("grad", "delta", "new_m", "new_v") for n in WEIGHT_NAMES])
```

```python
import functools
import math

import jax
import jax.numpy as jnp
from jax import lax
from jax.experimental import pallas as pl
from jax.experimental.pallas import tpu as pltpu

F32 = jnp.float32
BF16 = jnp.bfloat16
SDS = jax.ShapeDtypeStruct
BS = pl.BlockSpec

N_DEV = 8
NORM_EPS = 1e-6
ROPE_THETA = 500000.0
HEAD_DIM = 64
ROT_DIM = 16
ATT_BLOCK = 128
LRU_C = 8.0
LRU_HEADS = 4
DILATIONS = (1, 4, 16)
B_HEADS = 8
C_HEADS = 16
C_KV_HEADS = 2
XA_HEADS = 4
XA_HEAD_DIM = 128
NEG = -1e30
ADAM_LR, ADAM_B1, ADAM_B2, ADAM_EPS, ADAM_WD, ADAM_STEP = 0.001, 0.9, 0.999, 1e-08, 0.01, 10
LANES = 128
VMEM_LIMIT = 48 * 1024 * 1024

WEIGHT_NAMES = ['mix_norm', 'ab_w_in', 'lru_conv_w', 'lru_conv_b', 'lru_wa', 'lru_ba', 'lru_wx', 'lru_bx',
                'lru_lambda', 'ab_w_out', 'c_w_qkv', 'c_b_qkv', 'c_sinks', 'c_w_out', 'c_b_out', 'xa_norm',
                'xa_mem_norm', 'xa_wq', 'xa_wkv', 'xa_wo', 'ffn_norm', 'ffn_w_gate_up', 'ffn_w_down', 'final_norm']
SHARD_AXIS = {'ab_w_in': 2, 'lru_conv_w': 2, 'lru_wa': 2, 'lru_ba': 2, 'lru_wx': 2, 'lru_bx': 2, 'ab_w_out': 1,
              'c_w_qkv': 2, 'c_b_qkv': 1, 'c_w_out': 1, 'c_b_out': 1, 'xa_wq': 1, 'xa_wkv': 1, 'xa_wo': 2,
              'ffn_w_gate_up': 2, 'ffn_w_down': 1}
BIG = ['ab_w_in', 'lru_wa', 'lru_wx', 'ab_w_out', 'c_w_qkv', 'c_w_out', 'xa_wq', 'xa_wkv', 'xa_wo',
       'ffn_w_gate_up', 'ffn_w_down']
SMALL = ['lru_conv_w', 'lru_ba', 'lru_bx', 'c_b_qkv', 'c_b_out']
REPLICATED = [n for n in WEIGHT_NAMES if n not in SHARD_AXIS]
COLUMN_CUT = ('ab_w_in', 'c_w_qkv', 'ffn_w_gate_up')
PACK_QUANTUM = 2048


def _shard_view(name, t):
    return jnp.swapaxes(t, -1, -2) if name in COLUMN_CUT else t


def _layer_shard_axis(name):
    return 0 if name in COLUMN_CUT else SHARD_AXIS[name] - 1


def _pcall(body, **kw):
    return pl.pallas_call(body, **kw)


def _cparams(sem=None):
    return pltpu.CompilerParams(dimension_semantics=sem, vmem_limit_bytes=VMEM_LIMIT)


def _tile(n, target, mult=LANES):
    if n <= target:
        return n
    t = (target // mult) * mult
    while t >= mult:
        if n % t == 0:
            return t
        t -= mult
    return n


def _sigmoid(x):
    return 1.0 / (1.0 + jnp.exp(-x))


def _expm1(x):
    small = x * (1.0 + x * (0.5 + x * (1.0 / 6.0 + x * (1.0 / 24.0))))
    return jnp.where(jnp.abs(x) < 0.03, small, jnp.exp(x) - 1.0)


_GELU_C = math.sqrt(2.0 / math.pi)


def _gelu_parts(y):
    y2 = y * y
    th = jnp.tanh(_GELU_C * (y + 0.044715 * y * y2))
    g = 0.5 * y * (1.0 + th)
    dg = 0.5 * (1.0 + th) + 0.5 * y * (1.0 - th * th) * _GELU_C * (1.0 + 3.0 * 0.044715 * y2)
    return g, dg


def _rotate(xv, tab_refs, inverse=False):
    W = xv.shape[1]
    half = ROT_DIM // 2
    c, sa, sb = (jnp.tile(t[...], (1, W // LANES)) for t in tab_refs)
    if not inverse:
        return xv * c + pltpu.roll(xv, half, axis=1) * sa + pltpu.roll(xv, W - half, axis=1) * sb
    return xv * c + pltpu.roll(xv * sa, W - half, axis=1) + pltpu.roll(xv * sb, half, axis=1)


def _mm(a, b, *, ta=False, tb=False, bias=None, add=None, rope=None, name, tm=512, tn=512, tk=2048):
    M, K = (a.shape[1], a.shape[0]) if ta else a.shape
    N = b.shape[0] if tb else b.shape[1]
    tm, tn, tk = _tile(M, tm), _tile(N, tn), _tile(K, tk)
    nk = K // tk
    dn = (((0 if ta else 1,), (1 if tb else 0,)), ((), ()))

    def body(*refs):
        a_ref, b_ref = refs[0], refs[1]
        pos = 2
        bias_ref = add_ref = None
        if bias is not None:
            bias_ref = refs[pos]
            pos += 1
        if add is not None:
            add_ref = refs[pos]
            pos += 1
        tab_refs = refs[pos:pos + 3] if rope is not None else None
        pos += 3 if rope is not None else 0
        o_ref = refs[pos]
        part = lax.dot_general(a_ref[...].astype(BF16), b_ref[...].astype(BF16), dn, preferred_element_type=F32)

        def finish(r):
            if bias_ref is not None:
                r = r + bias_ref[...]
            if add_ref is not None:
                r = r + add_ref[...]
            o_ref[...] = r if tab_refs is None else _rotate(r, tab_refs)

        if nk == 1:
            finish(part)
            return
        acc_ref = refs[pos + 1]
        k = pl.program_id(2)

        @pl.when(k == 0)
        def _():
            acc_ref[...] = part

        @pl.when((k > 0) & (k < nk - 1))
        def _():
            acc_ref[...] += part

        @pl.when(k == nk - 1)
        def _():
            finish(acc_ref[...] + part)

    in_specs = [BS((tk, tm), lambda i, j, k: (k, i)) if ta else BS((tm, tk), lambda i, j, k: (i, k)),
                BS((tn, tk), lambda i, j, k: (j, k)) if tb else BS((tk, tn), lambda i, j, k: (k, j))]
    args = [a, b]
    if bias is not None:
        in_specs.append(BS((1, tn), lambda i, j, k: (0, j)))
        args.append(bias.reshape(1, N))
    if add is not None:
        in_specs.append(BS((tm, tn), lambda i, j, k: (i, j)))
        args.append(add)
    if rope is not None:
        in_specs += [BS((tm, LANES), lambda i, j, k: (i, 0))] * 3
        args += list(rope)
    return _pcall(body, name=name, grid=(M // tm, N // tn, nk), in_specs=in_specs,
                  out_specs=BS((tm, tn), lambda i, j, k: (i, j)), out_shape=SDS((M, N), F32),
                  scratch_shapes=[pltpu.VMEM((tm, tn), F32)] if nk > 1 else [],
                  compiler_params=_cparams(("parallel", "parallel", "arbitrary")))(*args)


def _mm_rms_bwd(dy, wb, x, g, add, name, wt=False):
    T, Kc = dy.shape
    Dm = wb.shape[1] if wt else wb.shape[0]
    tm, tk = _tile(T, 1024), _tile(Kc, 1408)
    nk = Kc // tk
    dn = (((1,), (0 if wt else 1,)), ((), ()))

    def body(*refs):
        dy_ref, w_ref, x_ref, g_ref = refs[:4]
        dx_ref, dg_ref, acc_ref = refs[-3:]
        i, k = pl.program_id(0), pl.program_id(1)
        part = lax.dot_general(dy_ref[...].astype(BF16), w_ref[...], dn, preferred_element_type=F32)

        @pl.when(k == 0)
        def _():
            acc_ref[...] = part

        @pl.when((k > 0) & (k < nk - 1))
        def _():
            acc_ref[...] += part

        @pl.when((i == 0) & (k == 0))
        def _():
            dg_ref[...] = jnp.zeros_like(dg_ref)

        @pl.when(k == nk - 1)
        def _():
            d = part if nk == 1 else acc_ref[...] + part
            xv = x_ref[...]
            r = lax.rsqrt(jnp.mean(xv * xv, axis=-1, keepdims=True) + NORM_EPS)
            xh = xv * r
            dyg = d * g_ref[...]
            dx = r * (dyg - xh * jnp.mean(dyg * xh, axis=-1, keepdims=True))
            dx_ref[...] = dx if add is None else dx + refs[4][...]
            dg_ref[...] += jnp.sum(d * xh, axis=0, keepdims=True)

    row = BS((tm, Dm), lambda i, k: (i, 0))
    vec = BS((1, Dm), lambda i, k: (0, 0))
    extra = [] if add is None else [add]
    return _pcall(body, name=name, grid=(T // tm, nk),
                  in_specs=[BS((tm, tk), lambda i, k: (i, k)),
                            BS((tk, Dm), lambda i, k: (k, 0)) if wt else BS((Dm, tk), lambda i, k: (0, k)), row, vec]
                  + [row] * len(extra),
                  out_specs=[row, vec], out_shape=[SDS((T, Dm), F32), SDS((1, Dm), F32)],
                  scratch_shapes=[pltpu.VMEM((tm, Dm), F32)],
                  compiler_params=_cparams(("arbitrary", "arbitrary")))(dy, wb, x, g.reshape(1, Dm), *extra)


def _colsum(x, name):
    T, N = x.shape
    tt = _tile(T, 512, 8)

    def body(x_ref, o_ref):
        @pl.when(pl.program_id(0) == 0)
        def _():
            o_ref[...] = jnp.zeros_like(o_ref)

        o_ref[...] += jnp.sum(x_ref[...], axis=0, keepdims=True)

    return _pcall(body, name=name, grid=(T // tt,), in_specs=[BS((tt, N), lambda i: (i, 0))],
                  out_specs=BS((1, N), lambda i: (0, 0)), out_shape=SDS((1, N), F32),
                  compiler_params=_cparams(("arbitrary",)))(x)


@functools.partial(jax.custom_vjp, nondiff_argnums=(5,))
def linear(a, wb, wc, bias, add, name):
    return _linear_fwd(a, wb, wc, bias, add, name)[0]


def _linear_fwd(a, wb, wc, bias, add, name):
    out = _mm(a, wb, bias=bias, add=add, name=name, tm=1024, tn=1024, tk=1024)
    return out, (a, wb, bias is not None, add is not None)


def _linear_bwd(name, res, g):
    a, wb, has_bias, has_add = res
    da = _mm(g, wb, tb=True, name=name + "_da", tm=1024, tn=1024, tk=1024)
    dw = _mm(a, g, ta=True, name=name + "_dw", tm=1024, tn=1024, tk=1024)
    dbias = _colsum(g, name + "_db").reshape(-1) if has_bias else None
    return da, jnp.zeros_like(wb), dw, dbias, (g if has_add else None)


linear.defvjp(_linear_fwd, _linear_bwd)


def _rms_fwd_call(x, g, name, out_dtype=F32):
    T, Dm = x.shape
    tt = _tile(T, 512, 16)

    def body(x_ref, g_ref, o_ref):
        xv = x_ref[...]
        r = lax.rsqrt(jnp.mean(xv * xv, axis=-1, keepdims=True) + NORM_EPS)
        o_ref[...] = ((xv * r) * g_ref[...]).astype(out_dtype)

    return _pcall(body, name=name, grid=(T // tt,),
                  in_specs=[BS((tt, Dm), lambda i: (i, 0)), BS((1, Dm), lambda i: (0, 0))],
                  out_specs=BS((tt, Dm), lambda i: (i, 0)), out_shape=SDS((T, Dm), out_dtype),
                  compiler_params=_cparams(("parallel",)))(x, g.reshape(1, Dm))


@functools.partial(jax.custom_vjp, nondiff_argnums=(5, 6))
def norm_linear(x, g, wb, wc, bias, wt, name):
    return _norm_linear_fwd(x, g, wb, wc, bias, wt, name)[0]


def _norm_linear_fwd(x, g, wb, wc, bias, wt, name):
    hn = _rms_fwd_call(x, g, name + "_norm", BF16)
    return (_mm(hn, wb, tb=wt, bias=bias, name=name, tm=1024, tn=1408, tk=1024), x), (x, g, hn, wb, bias is not None)


def _norm_linear_bwd(wt, name, res, cts):
    x, g, hn, wb, has_bias = res
    dy, dres = cts
    if wt:
        dw = _mm(dy, hn, ta=True, name=name + "_dw", tm=1408, tn=1024, tk=2048)
    else:
        dw = _mm(hn, dy, ta=True, name=name + "_dw", tm=1024, tn=1408, tk=1024)
    dx, dg = _mm_rms_bwd(dy, wb, x, g, dres, name + "_da", wt)
    dbias = _colsum(dy, name + "_db").reshape(-1) if has_bias else None
    return dx, dg.reshape(g.shape), jnp.zeros_like(wb), dw, dbias


norm_linear.defvjp(_norm_linear_fwd, _norm_linear_bwd)


def _rope_tables(T):
    half = ROT_DIM // 2
    inv = ROPE_THETA ** (-jnp.arange(0, ROT_DIM, 2, dtype=F32) / ROT_DIM)
    lane = jnp.arange(LANES) % HEAD_DIM
    freq = jnp.where(lane < ROT_DIM, inv[lane % half], 0.0)
    ang = jnp.arange(T, dtype=F32)[:, None] * freq[None, :]
    c, s = jnp.cos(ang), jnp.sin(ang)
    sa = jnp.where((lane >= half) & (lane < ROT_DIM), s, 0.0)
    sb = jnp.where(lane < half, -s, 0.0)
    return c, sa, sb


def _join_pieces_call(pieces, rotated, tabs, name):
    T = pieces[0].shape[0]
    widths = [p.shape[1] for p in pieces]
    tt = _tile(T, 256, 8)

    def body(*refs):
        tab_refs, o_ref = refs[len(pieces):len(pieces) + 3], refs[-1]
        off = 0
        for p_ref, w, r in zip(refs, widths, rotated):
            o_ref[:, off:off + w] = _rotate(p_ref[...], tab_refs, inverse=True) if r else p_ref[...]
            off += w

    return _pcall(body, name=name, grid=(T // tt,),
                  in_specs=[BS((tt, w), lambda i: (i, 0)) for w in widths] + [BS((tt, LANES), lambda i: (i, 0))] * 3,
                  out_specs=BS((tt, sum(widths)), lambda i: (i, 0)), out_shape=SDS((T, sum(widths)), F32),
                  compiler_params=_cparams(("parallel",)))(*pieces, *tabs)


@functools.partial(jax.custom_vjp, nondiff_argnums=(6, 7))
def norm_linear_pieces(x, g, wb, wc, bias, tabs, pieces, name):
    return _norm_linear_pieces_fwd(x, g, wb, wc, bias, tabs, pieces, name)[0]


def _norm_linear_pieces_fwd(x, g, wb, wc, bias, tabs, pieces, name):
    hn = _rms_fwd_call(x, g, name + "_norm", BF16)
    outs, row = [], 0
    for i, (width, rotated) in enumerate(pieces):
        outs.append(_mm(hn, wb[row:row + width], tb=True, bias=None if bias is None else bias[row:row + width],
                        rope=tabs if rotated else None, name=f"{name}_{i}", tm=2048, tn=1024, tk=1024))
        row += width
    return (*outs, x), (x, g, hn, wb, tabs, bias is not None)


def _norm_linear_pieces_bwd(pieces, name, res, cts):
    x, g, hn, wb, tabs, has_bias = res
    dy = _join_pieces_call(cts[:-1], [rotated for _, rotated in pieces], tabs, name + "_join")
    dw = _mm(dy, hn, ta=True, name=name + "_dw", tm=1408, tn=1024, tk=1024)
    dx, dg = _mm_rms_bwd(dy, wb, x, g, cts[-1], name + "_da", wt=True)
    dbias = _colsum(dy, name + "_db").reshape(-1) if has_bias else None
    return dx, dg.reshape(g.shape), jnp.zeros_like(wb), dw, dbias, jax.tree.map(jnp.zeros_like, tabs)


norm_linear_pieces.defvjp(_norm_linear_pieces_fwd, _norm_linear_pieces_bwd)


def _conv_fwd_call(x, w, b, name):
    T, C = x.shape
    tt = _tile(T, 512, 8)
    per = tt // 8

    def body(x_ref, halo_ref, w_ref, b_ref, o_ref):
        i = pl.program_id(0)
        halo = jnp.where(i > 0, halo_ref[...], 0.0)
        e = jnp.concatenate([halo, x_ref[...]], axis=0)
        acc = b_ref[...]
        for k in (3, 2, 1):
            acc = acc + pltpu.roll(e, k, axis=0)[8:, :] * w_ref[3 - k:4 - k, :]
        o_ref[...] = acc + x_ref[...] * w_ref[3:4, :]

    return _pcall(body, name=name, grid=(T // tt,),
                  in_specs=[BS((tt, C), lambda i: (i, 0)), BS((8, C), lambda i: (jnp.maximum(i * per - 1, 0), 0)),
                            BS((4, C), lambda i: (0, 0)), BS((1, C), lambda i: (0, 0))],
                  out_specs=BS((tt, C), lambda i: (i, 0)), out_shape=SDS((T, C), F32),
                  compiler_params=_cparams(("parallel",)))(x, x, w, b.reshape(1, C))


def _conv_bwd_call(x, w, dy, name):
    T, C = x.shape
    tt = _tile(T, 512, 8)
    per = tt // 8
    nt = T // tt

    def body(x_ref, halo_ref, w_ref, dy_ref, nxt_ref, dx_ref, dwb_ref):
        i = pl.program_id(0)
        halo = jnp.where(i > 0, halo_ref[...], 0.0)
        e = jnp.concatenate([halo, x_ref[...]], axis=0)
        dy = dy_ref[...]
        nxt = jnp.where(i < nt - 1, nxt_ref[...], 0.0)
        f = jnp.concatenate([dy, nxt], axis=0)
        dx = dy * w_ref[3:4, :]
        rows = [None] * 4
        rows[3] = jnp.sum(dy * x_ref[...], axis=0, keepdims=True)
        for k in (1, 2, 3):
            dx = dx + pltpu.roll(f, tt + 8 - k, axis=0)[:tt, :] * w_ref[3 - k:4 - k, :]
            rows[3 - k] = jnp.sum(dy * pltpu.roll(e, k, axis=0)[8:, :], axis=0, keepdims=True)
        dx_ref[...] = dx
        upd = jnp.concatenate(rows + [jnp.sum(dy, axis=0, keepdims=True), jnp.zeros((3, C), F32)], axis=0)

        @pl.when(i == 0)
        def _():
            dwb_ref[...] = jnp.zeros_like(dwb_ref)

        dwb_ref[...] += upd

    row = BS((tt, C), lambda i: (i, 0))
    return _pcall(body, name=name, grid=(nt,),
                  in_specs=[row, BS((8, C), lambda i: (jnp.maximum(i * per - 1, 0), 0)), BS((4, C), lambda i: (0, 0)),
                            row, BS((8, C), lambda i: (jnp.minimum((i + 1) * per, T // 8 - 1), 0))],
                  out_specs=[row, BS((8, C), lambda i: (0, 0))],
                  out_shape=[SDS((T, C), F32), SDS((8, C), F32)],
                  compiler_params=_cparams(("arbitrary",)))(x, x, w, dy, dy)


@functools.partial(jax.custom_vjp, nondiff_argnums=(3,))
def conv4(x, w, b, name):
    return _conv_fwd_call(x, w, b, name)


def _conv4_fwd(x, w, b, name):
    return _conv_fwd_call(x, w, b, name), (x, w)


def _conv4_bwd(name, res, dy):
    x, w = res
    dx, dwb = _conv_bwd_call(x, w, dy, name + "_bwd")
    return dx, dwb[0:4], dwb[4]


conv4.defvjp(_conv4_fwd, _conv4_bwd)


def _gates_fwd_call(xc, wa, ba, wx, bx, name):
    T, C = xc.shape
    hd = C // LRU_HEADS
    tt = _tile(T, 512, 8)

    def body(x_ref, wa_ref, ba_ref, wx_ref, bx_ref, ga_ref, gx_ref):
        xb = x_ref[...].astype(BF16)
        ga_ref[...] = jnp.dot(xb, wa_ref[0].astype(BF16), preferred_element_type=F32) + ba_ref[...]
        gx_ref[...] = jnp.dot(xb, wx_ref[0].astype(BF16), preferred_element_type=F32) + bx_ref[...]

    blk = BS((tt, hd), lambda i, h: (i, h))
    wsp = BS((1, hd, hd), lambda i, h: (h, 0, 0))
    bsp = BS((1, hd), lambda i, h: (0, h))
    return _pcall(body, name=name, grid=(T // tt, LRU_HEADS), in_specs=[blk, wsp, bsp, wsp, bsp],
                  out_specs=[blk, blk], out_shape=[SDS((T, C), F32)] * 2,
                  compiler_params=_cparams(("parallel", "parallel")))(xc, wa, ba.reshape(1, C), wx, bx.reshape(1, C))


def _gates_bwd_x_call(dga, dgx, wa, wx, name):
    T, C = dga.shape
    hd = C // LRU_HEADS
    tt = _tile(T, 512, 8)
    dn = (((1,), (1,)), ((), ()))

    def body(da_ref, dx_ref, wa_ref, wx_ref, o_ref):
        o_ref[...] = (lax.dot_general(da_ref[...].astype(BF16), wa_ref[0].astype(BF16), dn, preferred_element_type=F32)
                      + lax.dot_general(dx_ref[...].astype(BF16), wx_ref[0].astype(BF16), dn, preferred_element_type=F32))

    blk = BS((tt, hd), lambda i, h: (i, h))
    wsp = BS((1, hd, hd), lambda i, h: (h, 0, 0))
    return _pcall(body, name=name, grid=(T // tt, LRU_HEADS), in_specs=[blk, blk, wsp, wsp], out_specs=blk,
                  out_shape=SDS((T, C), F32), compiler_params=_cparams(("parallel", "parallel")))(dga, dgx, wa, wx)


def _gates_bwd_w_call(xc, dga, dgx, name):
    T, C = xc.shape
    hd = C // LRU_HEADS
    tt = _tile(T, 512, 8)
    dn = (((0,), (0,)), ((), ()))

    def body(x_ref, da_ref, dx_ref, dwa_ref, dwx_ref, dba_ref, dbx_ref):
        @pl.when(pl.program_id(1) == 0)
        def _():
            dwa_ref[...] = jnp.zeros_like(dwa_ref)
            dwx_ref[...] = jnp.zeros_like(dwx_ref)
            dba_ref[...] = jnp.zeros_like(dba_ref)
            dbx_ref[...] = jnp.zeros_like(dbx_ref)

        xb = x_ref[...].astype(BF16)
        da, dx = da_ref[...], dx_ref[...]
        dwa_ref[0] += lax.dot_general(xb, da.astype(BF16), dn, preferred_element_type=F32)
        dwx_ref[0] += lax.dot_general(xb, dx.astype(BF16), dn, preferred_element_type=F32)
        dba_ref[...] += jnp.sum(da, axis=0, keepdims=True)
        dbx_ref[...] += jnp.sum(dx, axis=0, keepdims=True)

    blk = BS((tt, hd), lambda h, i: (i, h))
    wsp = BS((1, hd, hd), lambda h, i: (h, 0, 0))
    bsp = BS((1, hd), lambda h, i: (0, h))
    return _pcall(body, name=name, grid=(LRU_HEADS, T // tt), in_specs=[blk, blk, blk],
                  out_specs=[wsp, wsp, bsp, bsp],
                  out_shape=[SDS((LRU_HEADS, hd, hd), F32)] * 2 + [SDS((1, C), F32)] * 2,
                  compiler_params=_cparams(("parallel", "arbitrary")))(xc, dga, dgx)


@functools.partial(jax.custom_vjp, nondiff_argnums=(7,))
def lru_gates(xc, wa, wa_c, ba, wx, wx_c, bx, name):
    return tuple(_gates_fwd_call(xc, wa, ba, wx, bx, name))


def _lru_gates_fwd(xc, wa, wa_c, ba, wx, wx_c, bx, name):
    return tuple(_gates_fwd_call(xc, wa, ba, wx, bx, name)), (xc, wa, wx, ba.shape)


def _lru_gates_bwd(name, res, g):
    xc, wa, wx, bshape = res
    dga, dgx = g
    dxc = _gates_bwd_x_call(dga, dgx, wa, wx, name + "_dx")
    dwa, dwx, dba, dbx = _gates_bwd_w_call(xc, dga, dgx, name + "_dw")
    return dxc, jnp.zeros_like(wa), dwa, dba.reshape(bshape), jnp.zeros_like(wx), dwx, dbx.reshape(bshape)


lru_gates.defvjp(_lru_gates_fwd, _lru_gates_bwd)


def _lru_coeffs(xc, ga, gx, lam):
    r = _sigmoid(ga)
    ig = _sigmoid(gx)
    z = -lam
    sp = jnp.maximum(z, 0.0) + jnp.log(1.0 + jnp.exp(-jnp.abs(z)))
    la = -LRU_C * r * sp
    a = jnp.exp(la)
    s = jnp.sqrt(-_expm1(2.0 * la))
    return r, ig, sp, a, s


LRU_TT = 256


def _scan_fwd_call(xc, ga, gx, y, lam, name):
    T, C = xc.shape
    tt = _tile(T, LRU_TT, 8)

    def body(xc_ref, ga_ref, gx_ref, y_ref, lam_ref, h_ref, rec_ref, a_buf, carry):
        @pl.when(pl.program_id(0) == 0)
        def _():
            carry[...] = jnp.zeros_like(carry)

        xcv = xc_ref[...]
        _, ig, _, a, s = _lru_coeffs(xcv, ga_ref[...], gx_ref[...], lam_ref[...])
        a_buf[...] = a
        h_ref[...] = s * (ig * xcv)

        def step(t, h):
            hn = a_buf[pl.ds(t, 1), :] * h + h_ref[pl.ds(t, 1), :]
            h_ref[pl.ds(t, 1), :] = hn
            return hn

        carry[0:1, :] = lax.fori_loop(0, tt, step, carry[0:1, :], unroll=8)
        g, _ = _gelu_parts(y_ref[...])
        rec_ref[...] = h_ref[...] * g

    row = BS((tt, C), lambda i: (i, 0))
    vec = BS((1, C), lambda i: (0, 0))
    return _pcall(body, name=name, grid=(T // tt,), in_specs=[row, row, row, row, vec], out_specs=[row, row],
                  out_shape=[SDS((T, C), F32)] * 2,
                  scratch_shapes=[pltpu.VMEM((tt, C), F32), pltpu.VMEM((8, C), F32)],
                  compiler_params=_cparams(("arbitrary",)))(xc, ga, gx, y, lam.reshape(1, C))


def _scan_bwd_call(xc, ga, gx, y, lam, h, drec, name):
    T, C = xc.shape
    tt = _tile(T, LRU_TT, 8)
    nt = T // tt
    per = tt // 8

    def body(xc_ref, ga_ref, gx_ref, y_ref, lam_ref, h_ref, halo_ref, dr_ref,
             dga_ref, dgx_ref, dxc_ref, dy_ref, dlam_ref, a_buf, g_buf, carry):
        i = pl.program_id(0)

        @pl.when(i == 0)
        def _():
            carry[...] = jnp.zeros_like(carry)
            dlam_ref[...] = jnp.zeros_like(dlam_ref)

        xcv, lam = xc_ref[...], lam_ref[...]
        r, ig, sp, a, s = _lru_coeffs(xcv, ga_ref[...], gx_ref[...], lam)
        gel, dgel = _gelu_parts(y_ref[...])
        drec = dr_ref[...]
        hv = h_ref[...]
        dy_ref[...] = drec * hv * dgel
        a_buf[...] = a
        g_buf[...] = drec * gel

        def step(j, q):
            t = tt - 1 - j
            g = g_buf[pl.ds(t, 1), :] + q
            g_buf[pl.ds(t, 1), :] = g
            return a_buf[pl.ds(t, 1), :] * g

        carry[0:1, :] = lax.fori_loop(0, tt, step, carry[0:1, :], unroll=8)
        g = g_buf[...]
        halo = jnp.where(i < nt - 1, halo_ref[...], 0.0)
        hprev = pltpu.roll(jnp.concatenate([halo, hv], axis=0), 1, axis=0)[8:, :]
        da = g * hprev
        dig = g * s * xcv
        ds = g * ig * xcv
        dla = da * a - ds * (a * a) / s
        dga_ref[...] = dla * (-LRU_C * sp) * r * (1.0 - r)
        dgx_ref[...] = dig * ig * (1.0 - ig)
        dxc_ref[...] = g * s * ig
        dlam_ref[...] += jnp.sum(dla * r, axis=0, keepdims=True) * (LRU_C * _sigmoid(-lam))

    row = BS((tt, C), lambda i: (nt - 1 - i, 0))
    vec = BS((1, C), lambda i: (0, 0))
    halo = BS((8, C), lambda i: (jnp.maximum((nt - 1 - i) * per - 1, 0), 0))
    return _pcall(body, name=name, grid=(nt,), in_specs=[row, row, row, row, vec, row, halo, row],
                  out_specs=[row, row, row, row, vec], out_shape=[SDS((T, C), F32)] * 4 + [SDS((1, C), F32)],
                  scratch_shapes=[pltpu.VMEM((tt, C), F32), pltpu.VMEM((tt, C), F32), pltpu.VMEM((8, C), F32)],
                  compiler_params=_cparams(("arbitrary",)))(xc, ga, gx, y, lam.reshape(1, C), h, h, drec)


@functools.partial(jax.custom_vjp, nondiff_argnums=(5,))
def lru_scan(xc, ga, gx, y, lam, name):
    return _scan_fwd_call(xc, ga, gx, y, lam, name)[1]


def _lru_scan_fwd(xc, ga, gx, y, lam, name):
    h, rec = _scan_fwd_call(xc, ga, gx, y, lam, name)
    return rec, (xc, ga, gx, y, lam, h)


def _lru_scan_bwd(name, res, drec):
    xc, ga, gx, y, lam, h = res
    dga, dgx, dxc, dy, dlam = _scan_bwd_call(xc, ga, gx, y, lam, h, drec, name + "_bwd")
    return dxc, dga, dgx, dy, dlam.reshape(lam.shape)


lru_scan.defvjp(_lru_scan_fwd, _lru_scan_bwd)


def _att_batch(d, shared=False):
    if shared:
        return 8, 1
    return (4, 1) if d == 1 else (1, min(d, 8))


def _att_masks(n, max_dist):
    qi = lax.broadcasted_iota(jnp.int32, (1, 2 * ATT_BLOCK, 2 * ATT_BLOCK), 1) & (ATT_BLOCK - 1)
    kj = lax.broadcasted_iota(jnp.int32, (1, 2 * ATT_BLOCK, 2 * ATT_BLOCK), 2)
    prev = (kj < ATT_BLOCK) & (kj >= qi + (ATT_BLOCK - max_dist)) & (n > 0)
    cur = (kj >= ATT_BLOCK) & (kj - ATT_BLOCK <= qi)
    return prev | cur


def _lo_lanes(rows):
    return lax.broadcasted_iota(jnp.int32, (rows, LANES), 1) < HEAD_DIM


def _lane_half(rows):
    return lax.broadcasted_iota(jnp.int32, (rows, LANES), 1) // HEAD_DIM


def _stack_heads(x2):
    lo = _lo_lanes(ATT_BLOCK)
    zero = jnp.zeros_like(x2)
    return jnp.concatenate([jnp.where(lo, x2, zero), jnp.where(lo, zero, x2)], axis=0)


def _unstack_heads(y):
    return jnp.where(_lo_lanes(ATT_BLOCK), y[:ATT_BLOCK], y[ATT_BLOCK:])


def _per_head_col(x2):
    return jnp.concatenate([x2[:, 0:1], x2[:, HEAD_DIM:HEAD_DIM + 1]], axis=0)


def _head_sums(x2):
    lo = _lo_lanes(ATT_BLOCK)
    return jnp.concatenate([jnp.sum(jnp.where(lo, x2, 0.0), axis=1, keepdims=True),
                            jnp.sum(jnp.where(lo, 0.0, x2), axis=1, keepdims=True)], axis=0)


def _att_specs(d, Wq, Wk, nb, clamp):
    shared = Wk != Wq
    cgw, sb = _att_batch(d, shared)
    cur = (lambda n: jnp.minimum(n, nb - 1)) if clamp else (lambda n: n)
    rows, qw, kw = ATT_BLOCK * d, cgw * LANES, (LANES if shared else cgw * LANES)
    kcol = (lambda g: 0) if shared else (lambda g: g)
    qsp = BS((rows, qw), lambda g, n: (cur(n), g))
    csp = BS((rows, kw), lambda g, n: (cur(n), kcol(g)))
    psp = BS((rows, kw), lambda g, n: (jnp.maximum(cur(n) - 1, 0), kcol(g)))
    return cgw, sb, shared, qsp, csp, psp, qw, kw


def _att_streams(d, sb, work):
    if d == 1:
        work([slice(None)])
        return

    def one(j, carry):
        work([pl.ds(j * sb + i, ATT_BLOCK, stride=d) for i in range(sb)])
        return carry

    lax.fori_loop(0, d // sb, one, 0)


def _att_problem_loads(rows, cgw, shared, g, q_ref, kc_ref, kp_ref, vc_ref, vp_ref, sk_ref):
    half = _lane_half(ATT_BLOCK)

    def kv(ref, r, p):
        x = ref[r, :]
        if not shared:
            return x[:, p * LANES:(p + 1) * LANES]
        return jnp.where(half == p // 4, x, pltpu.roll(x, HEAD_DIM, axis=1))

    qs, kb, vb, sk = [], [], [], []
    for r in rows:
        qrow = q_ref[r, :]
        for p in range(cgw):
            cols = slice(p * LANES, (p + 1) * LANES)
            qs.append(_stack_heads(qrow[:, cols].astype(BF16)))
            kb.append(jnp.concatenate([kv(kp_ref, r, p), kv(kc_ref, r, p)], axis=0).astype(BF16))
            vb.append(jnp.concatenate([kv(vp_ref, r, p), kv(vc_ref, r, p)], axis=0).astype(BF16))
            sk.append(_per_head_col(jnp.broadcast_to(sk_ref[:, cols], (ATT_BLOCK, LANES))))
    return jnp.stack(qs), jnp.stack(kb), jnp.stack(vb), jnp.stack(sk)


_BDOT_NT = (((2,), (2,)), ((0,), (0,)))
_BDOT_NN = (((2,), (1,)), ((0,), (0,)))
_BDOT_TN = (((1,), (1,)), ((0,), (0,)))


def _att_fwd_call(q, k, v, sinks, d, max_dist, name):
    T, Wq = q.shape
    Wk = k.shape[1]
    nb = T // (d * ATT_BLOCK)
    cgw, sb, shared, qsp, csp, psp, qw, kw = _att_specs(d, Wq, Wk, nb, False)
    G = Wq // qw
    assert not shared or (d == 1 and Wk == LANES and G == 1 and cgw == 8), "a shared kv pair serves 2 x 8 query heads"

    def body(q_ref, kc_ref, kp_ref, vc_ref, vp_ref, sk_ref, o_ref, lse_ref):
        g, n = pl.program_id(0), pl.program_id(1)

        def work(rows):
            qs, kband, vband, sk = _att_problem_loads(rows, cgw, shared, g, q_ref, kc_ref, kp_ref, vc_ref, vp_ref, sk_ref)
            s = lax.dot_general(qs, kband, _BDOT_NT, preferred_element_type=F32) * (HEAD_DIM ** -0.5)
            s = jnp.where(_att_masks(n, max_dist), s, NEG)
            m = jnp.maximum(jnp.max(s, axis=-1, keepdims=True), sk)
            e = jnp.exp(s - m)
            den = jnp.sum(e, axis=-1, keepdims=True) + jnp.exp(sk - m)
            o = lax.dot_general((e * (1.0 / den)).astype(BF16), vband, _BDOT_NN, preferred_element_type=F32)
            lse = jnp.broadcast_to(m + jnp.log(den), o.shape)
            for i, r in enumerate(rows):
                o_ref[r, :] = jnp.concatenate([_unstack_heads(o[i * cgw + p]) for p in range(cgw)], axis=1)
                lse_ref[r, :] = jnp.concatenate([_unstack_heads(lse[i * cgw + p]) for p in range(cgw)], axis=1)

        _att_streams(d, sb, work)

    sksp = BS((1, qw), lambda g, n: (0, g))
    return _pcall(body, name=name, grid=(G, nb), in_specs=[qsp, csp, psp, csp, psp, sksp], out_specs=[qsp, qsp],
                  out_shape=[SDS((T, Wq), F32)] * 2,
                  compiler_params=_cparams(("parallel", "parallel")))(q, k, k, v, v, sinks)


def _att_bwd_call(q, k, v, sinks, o, lse, do, dlse, d, max_dist, name):
    T, Wq = q.shape
    Wk = k.shape[1]
    nb = T // (d * ATT_BLOCK)
    cgw, sb, shared, qsp, csp, psp, qw, kw = _att_specs(d, Wq, Wk, nb, True)
    G = Wq // qw
    scale = HEAD_DIM ** -0.5

    def body(*refs):
        q_ref, kc_ref, kp_ref, vc_ref, vp_ref, sk_ref, o_ref, lse_ref, do_ref = refs[:9]
        dlse_ref = refs[9] if dlse is not None else None
        dq_ref, dk_ref, dv_ref, dsk_ref, ck, cv = refs[-6:]
        g, n = pl.program_id(0), pl.program_id(1)

        @pl.when(n == 0)
        def _():
            ck[...] = jnp.zeros_like(ck)
            cv[...] = jnp.zeros_like(cv)
            dsk_ref[...] = jnp.zeros_like(dsk_ref)

        def work(rows):
            qs, kband, vband, sk = _att_problem_loads(rows, cgw, shared, g, q_ref, kc_ref, kp_ref, vc_ref, vp_ref, sk_ref)
            dos, lse_c, corr = [], [], []
            for r in rows:
                do_r, o_r, lse_r = do_ref[r, :], o_ref[r, :], lse_ref[r, :]
                dlse_r = dlse_ref[r, :] if dlse_ref is not None else None
                for p in range(cgw):
                    cols = slice(p * LANES, (p + 1) * LANES)
                    dos.append(_stack_heads(do_r[:, cols].astype(BF16)))
                    lse_c.append(_per_head_col(lse_r[:, cols]))
                    delta = _head_sums(do_r[:, cols] * o_r[:, cols])
                    corr.append(-delta if dlse_r is None else _head_sums(dlse_r[:, cols]) - delta)
            dos, lse_c, corr = jnp.stack(dos), jnp.stack(lse_c), jnp.stack(corr)
            s = lax.dot_general(qs, kband, _BDOT_NT, preferred_element_type=F32) * scale
            pr = jnp.exp(jnp.where(_att_masks(n, max_dist), s, NEG) - lse_c)
            dp = lax.dot_general(dos, vband, _BDOT_NT, preferred_element_type=F32)
            ds = (pr * (dp + corr)).astype(BF16)
            dq = lax.dot_general(ds, kband, _BDOT_NN, preferred_element_type=F32) * scale
            dkb = lax.dot_general(ds, qs, _BDOT_TN, preferred_element_type=F32) * scale
            dvb = lax.dot_general(pr.astype(BF16), dos, _BDOT_TN, preferred_element_type=F32)
            dsk = jnp.exp(sk - lse_c) * corr
            lane = lax.broadcasted_iota(jnp.int32, (8, LANES), 1)
            for p in range(cgw):
                tot = [jnp.sum(jnp.stack([dsk[i * cgw + p, h * ATT_BLOCK:(h + 1) * ATT_BLOCK] for i in range(len(rows))]),
                               axis=(0, 1)).reshape(1, 1) for h in range(2)]
                dsk_ref[:, p * LANES:(p + 1) * LANES] += jnp.where(lane == 0, tot[0], jnp.where(lane == HEAD_DIM, tot[1], 0.0))

            def gather_pairs(parts):
                if not shared:
                    return jnp.concatenate(parts, axis=1)
                tot = [parts[4 * h] + parts[4 * h + 1] + parts[4 * h + 2] + parts[4 * h + 3] for h in range(2)]
                tot = [t + pltpu.roll(t, HEAD_DIM, axis=1) for t in tot]
                return jnp.where(_lo_lanes(ATT_BLOCK), tot[0], tot[1])

            for i, r in enumerate(rows):
                mine = range(i * cgw, (i + 1) * cgw)
                dq_ref[r, :] = jnp.concatenate([_unstack_heads(dq[b]) for b in mine], axis=1)
                dk_ref[r, :] = ck[r, :] + gather_pairs([dkb[b, :ATT_BLOCK] for b in mine])
                dv_ref[r, :] = cv[r, :] + gather_pairs([dvb[b, :ATT_BLOCK] for b in mine])
                ck[r, :] = gather_pairs([dkb[b, ATT_BLOCK:] for b in mine])
                cv[r, :] = gather_pairs([dvb[b, ATT_BLOCK:] for b in mine])

        @pl.when(n < nb)
        def _():
            _att_streams(d, sb, work)

        @pl.when(n == nb)
        def _():
            dk_ref[...] = ck[...]
            dv_ref[...] = cv[...]

    sksp = BS((1, qw), lambda g, n: (0, g))
    rows = ATT_BLOCK * d
    osp = BS((rows, kw), lambda g, n: (jnp.maximum(n - 1, 0), 0 if shared else g))
    kshape = SDS((T, Wk), F32)
    dq, dk, dv, dsk = _pcall(
        body, name=name, grid=(G, nb + 1),
        in_specs=[qsp, csp, psp, csp, psp, sksp, qsp, qsp, qsp] + ([qsp] if dlse is not None else []),
        out_specs=[qsp, osp, osp, BS((8, qw), lambda g, n: (0, g))],
        out_shape=[SDS((T, Wq), F32), kshape, kshape, SDS((8, Wq), F32)],
        scratch_shapes=[pltpu.VMEM((rows, kw), F32)] * 2,
        compiler_params=_cparams(("parallel", "arbitrary")))(q, k, k, v, v, sinks, o, lse, do,
                                                              *([dlse] if dlse is not None else []))
    return dq, dk, dv, dsk[0:1]


@functools.partial(jax.custom_vjp, nondiff_argnums=(4, 5, 6, 7))
def band_attention(q, k, v, sinks, d, max_dist, with_lse, name):
    return _band_attention_fwd(q, k, v, sinks, d, max_dist, with_lse, name)[0]


def _band_attention_fwd(q, k, v, sinks, d, max_dist, with_lse, name):
    o, lse = _att_fwd_call(q, k, v, sinks, d, max_dist, name)
    return ((o, lse) if with_lse else o), (q, k, v, sinks, o, lse)


def _band_attention_bwd(d, max_dist, with_lse, name, res, g):
    q, k, v, sinks, o, lse = res
    do, dlse = g if with_lse else (g, None)
    return _att_bwd_call(q, k, v, sinks, o, lse, do, dlse, d, max_dist, name + "_bwd")


band_attention.defvjp(_band_attention_fwd, _band_attention_bwd)


def _merge_weights(ls):
    mx = jnp.maximum(jnp.maximum(ls[0], ls[1]), ls[2])
    es = [jnp.exp(l - mx) for l in ls]
    inv = 1.0 / (es[0] + es[1] + es[2])
    return [e * inv for e in es]


def _merge_fwd_call(os_, ls_, name):
    T, W = os_[0].shape
    tt = _tile(T, 512, 8)

    def body(o1, o2, o3, l1, l2, l3, out_ref):
        w = _merge_weights([l1[...], l2[...], l3[...]])
        out_ref[...] = w[0] * o1[...] + w[1] * o2[...] + w[2] * o3[...]

    row = BS((tt, W), lambda i: (i, 0))
    return _pcall(body, name=name, grid=(T // tt,), in_specs=[row] * 6, out_specs=row,
                  out_shape=SDS((T, W), F32), compiler_params=_cparams(("parallel",)))(*os_, *ls_)


def _merge_bwd_call(os_, ls_, do, name):
    T, W = os_[0].shape
    tt = _tile(T, 512, 8)

    def body(o1, o2, o3, l1, l2, l3, do_ref, d1, d2, d3, e1, e2, e3):
        w = _merge_weights([l1[...], l2[...], l3[...]])
        dov = do_ref[...]
        ts = [dov * o[...] for o in (o1, o2, o3)]
        mean = w[0] * ts[0] + w[1] * ts[1] + w[2] * ts[2]
        for wi, ti, dref, eref in zip(w, ts, (d1, d2, d3), (e1, e2, e3)):
            dref[...] = wi * dov
            eref[...] = wi * (ti - mean)

    row = BS((tt, W), lambda i: (i, 0))
    return _pcall(body, name=name, grid=(T // tt,), in_specs=[row] * 7, out_specs=[row] * 6,
                  out_shape=[SDS((T, W), F32)] * 6, compiler_params=_cparams(("parallel",)))(*os_, *ls_, do)


@functools.partial(jax.custom_vjp, nondiff_argnums=(2,))
def merge3(os_, ls_, name):
    return _merge_fwd_call(os_, ls_, name)


def _merge3_fwd(os_, ls_, name):
    return _merge_fwd_call(os_, ls_, name), (os_, ls_)


def _merge3_bwd(name, res, do):
    os_, ls_ = res
    out = _merge_bwd_call(os_, ls_, do, name + "_bwd")
    return tuple(out[:3]), tuple(out[3:])


merge3.defvjp(_merge3_fwd, _merge3_bwd)


def _xa_probs(qb, kb, scale):
    s = lax.dot_general(qb, kb, (((1,), (1,)), ((), ())), preferred_element_type=F32) * scale
    e = jnp.exp(s - jnp.max(s, axis=-1, keepdims=True))
    return e / jnp.sum(e, axis=-1, keepdims=True)


def _xa_fwd_call(q, kv, name):
    T, W = q.shape
    M = kv.shape[0]
    hd = XA_HEAD_DIM
    tq = _tile(T, 512, 8)
    scale = hd ** -0.5

    def body(q_ref, k_ref, v_ref, o_ref):
        p = _xa_probs(q_ref[...].astype(BF16), k_ref[...].astype(BF16), scale)
        o_ref[...] = jnp.dot(p.astype(BF16), v_ref[...].astype(BF16), preferred_element_type=F32)

    qsp = BS((tq, hd), lambda i, h: (i, h))
    return _pcall(body, name=name, grid=(T // tq, XA_HEADS),
                  in_specs=[qsp, BS((M, hd), lambda i, h: (0, h)), BS((M, hd), lambda i, h: (0, XA_HEADS + h))],
                  out_specs=qsp, out_shape=SDS((T, W), F32),
                  compiler_params=_cparams(("parallel", "parallel")))(q, kv, kv)


def _xa_bwd_call(q, kv, do, name):
    T, W = q.shape
    M = kv.shape[0]
    hd = XA_HEAD_DIM
    tq = _tile(T, 512, 8)
    scale = hd ** -0.5
    dn_nt = (((1,), (1,)), ((), ()))
    dn_tn = (((0,), (0,)), ((), ()))

    def body(q_ref, k_ref, v_ref, do_ref, dq_ref, dk_ref, dv_ref):
        @pl.when(pl.program_id(1) == 0)
        def _():
            dk_ref[...] = jnp.zeros_like(dk_ref)
            dv_ref[...] = jnp.zeros_like(dv_ref)

        qb, kb, vb = q_ref[...].astype(BF16), k_ref[...].astype(BF16), v_ref[...].astype(BF16)
        p = _xa_probs(qb, kb, scale)
        dob = do_ref[...].astype(BF16)
        dp = lax.dot_general(dob, vb, dn_nt, preferred_element_type=F32)
        ds = (p * (dp - jnp.sum(p * dp, axis=-1, keepdims=True))).astype(BF16)
        dq_ref[...] = jnp.dot(ds, kb, preferred_element_type=F32) * scale
        dk_ref[...] += lax.dot_general(ds, qb, dn_tn, preferred_element_type=F32) * scale
        dv_ref[...] += lax.dot_general(p.astype(BF16), dob, dn_tn, preferred_element_type=F32)

    qsp = BS((tq, hd), lambda h, i: (i, h))
    ksp = BS((M, hd), lambda h, i: (0, h))
    return _pcall(body, name=name, grid=(XA_HEADS, T // tq),
                  in_specs=[qsp, ksp, BS((M, hd), lambda h, i: (0, XA_HEADS + h)), qsp],
                  out_specs=[qsp, ksp, ksp], out_shape=[SDS((T, W), F32), SDS((M, W), F32), SDS((M, W), F32)],
                  compiler_params=_cparams(("parallel", "arbitrary")))(q, kv, kv, do)


@functools.partial(jax.custom_vjp, nondiff_argnums=(2,))
def cross_attention(q, kv, name):
    return _xa_fwd_call(q, kv, name)


def _cross_attention_fwd(q, kv, name):
    return _xa_fwd_call(q, kv, name), (q, kv)


def _cross_attention_bwd(name, res, do):
    q, kv = res
    dq, dk, dv = _xa_bwd_call(q, kv, do, name + "_bwd")
    return dq, jnp.concatenate([dk, dv], axis=1)


cross_attention.defvjp(_cross_attention_fwd, _cross_attention_bwd)


def _gate_up_swiglu_call(hn, w1t, name):
    T, K = hn.shape
    F = w1t.shape[0] // 2
    tm, tn = _tile(T, 2048), _tile(F, 256)
    nj = F // tn
    dn = (((1,), (1,)), ((), ()))

    def body(a_ref, wg_ref, wu_ref, g_ref, u_ref, act_ref):
        a = a_ref[...]
        g = lax.dot_general(a, wg_ref[...], dn, preferred_element_type=F32)
        u = lax.dot_general(a, wu_ref[...], dn, preferred_element_type=F32)
        g_ref[...] = g
        u_ref[...] = u
        act_ref[...] = ((g * _sigmoid(g)) * u).astype(BF16)

    tile = BS((tm, tn), lambda i, j: (i, j))
    return _pcall(body, name=name, grid=(T // tm, nj),
                  in_specs=[BS((tm, K), lambda i, j: (i, 0)), BS((tn, K), lambda i, j: (j, 0)),
                            BS((tn, K), lambda i, j: (j + nj, 0))],
                  out_specs=[tile, tile, tile], out_shape=[SDS((T, F), F32), SDS((T, F), F32), SDS((T, F), BF16)],
                  compiler_params=_cparams(("parallel", "parallel")))(hn, w1t, w1t)


def _swiglu_bwd_call(g, u, dact, name):
    T, F = g.shape
    tt = _tile(T, 128, 16)

    def body(g_ref, u_ref, d_ref, o_ref):
        g, d = g_ref[...], d_ref[...]
        sg = _sigmoid(g)
        o_ref[:, :F] = (d * u_ref[...] * (sg * (1.0 + g * (1.0 - sg)))).astype(BF16)
        o_ref[:, F:] = (d * (g * sg)).astype(BF16)

    row = BS((tt, F), lambda i: (i, 0))
    return _pcall(body, name=name, grid=(T // tt,), in_specs=[row, row, row],
                  out_specs=BS((tt, 2 * F), lambda i: (i, 0)), out_shape=SDS((T, 2 * F), BF16),
                  compiler_params=_cparams(("parallel",)))(g, u, dact)


@functools.partial(jax.custom_vjp, nondiff_argnums=(6,))
def ffn_block(h, g, w1b, w1c, w2b, w2c, name):
    return _ffn_fwd(h, g, w1b, w1c, w2b, w2c, name)[0]


def _ffn_fwd(h, g, w1b, w1c, w2b, w2c, name):
    hn = _rms_fwd_call(h, g, name + "_norm", BF16)
    gate, up, act = _gate_up_swiglu_call(hn, w1b, name + "_gu")
    out = _mm(act, w2b, add=h, name=name + "_down", tm=1024, tn=1024, tk=2816)
    return out, (h, g, hn, gate, up, act, w1b, w2b)


def _ffn_bwd(name, res, dout):
    h, g, hn, gate, up, act, w1b, w2b = res
    dact = _mm(dout, w2b, tb=True, name=name + "_down_da", tm=1024, tn=1408, tk=1024)
    dw2 = _mm(act, dout, ta=True, name=name + "_down_dw", tm=1408, tn=1024, tk=1024)
    dgu = _swiglu_bwd_call(gate, up, dact, name + "_swiglu_bwd")
    dw1 = _mm(dgu, hn, ta=True, name=name + "_gu_dw", tm=1408, tn=1024, tk=2048)
    dh, dg = _mm_rms_bwd(dgu, w1b, h, g, dout, name + "_gu_da", wt=True)
    return dh, dg.reshape(g.shape), jnp.zeros_like(w1b), dw1, jnp.zeros_like(w2b), dw2


ffn_block.defvjp(_ffn_fwd, _ffn_bwd)


def _final_call(h, g, target, name):
    T, Dm = h.shape
    tt = _tile(T, 512, 8)

    def body(x_ref, g_ref, t_ref, loss_ref, dx_ref, dg_ref):
        @pl.when(pl.program_id(0) == 0)
        def _():
            loss_ref[...] = jnp.zeros_like(loss_ref)
            dg_ref[...] = jnp.zeros_like(dg_ref)

        xv, gv = x_ref[...], g_ref[...]
        r = lax.rsqrt(jnp.mean(xv * xv, axis=-1, keepdims=True) + NORM_EPS)
        xh = xv * r
        err = xh * gv - t_ref[...]
        loss_ref[...] += 0.5 * jnp.sum(jnp.mean(err * err, axis=-1, keepdims=True), axis=0, keepdims=True)
        dy = err * (1.0 / Dm)
        dyg = dy * gv
        dx_ref[...] = r * (dyg - xh * jnp.mean(dyg * xh, axis=-1, keepdims=True))
        dg_ref[...] += jnp.sum(dy * xh, axis=0, keepdims=True)

    row = BS((tt, Dm), lambda i: (i, 0))
    vec = BS((1, Dm), lambda i: (0, 0))
    return _pcall(body, name=name, grid=(T // tt,), in_specs=[row, vec, row],
                  out_specs=[BS((1, 1), lambda i: (0, 0)), row, vec],
                  out_shape=[SDS((1, 1), F32), SDS((T, Dm), F32), SDS((1, Dm), F32)],
                  compiler_params=_cparams(("arbitrary",)))(h, g.reshape(1, Dm), target)


ADAMW_BLOCK_ELEMS = 64 * 1024


def _adamw_call(parts, w, m, v, name):
    shape = w.shape
    if not isinstance(parts, (list, tuple)):
        parts, shape3 = [parts], (1,) + shape
    else:
        shape3 = shape
    n_lead = shape3[0]
    r, N = shape3[-2], shape3[-1]
    Ld = math.prod(shape3[1:-2])
    w, m, v = (t.reshape(n_lead * Ld, r, N) for t in (w, m, v))
    tr = _tile(r, max(8, ADAMW_BLOCK_ELEMS // N), 8)
    c1 = 1.0 - ADAM_B1 ** ADAM_STEP
    c2 = 1.0 - ADAM_B2 ** ADAM_STEP
    outs = None
    for lead, p in enumerate(parts):
        def body(p_ref, w_ref, m_ref, v_ref, *rest):
            g_ref, d_ref, nm_ref, nv_ref = rest[-4:]
            g = p_ref[0]
            for j in range(1, N_DEV):
                g = g + p_ref[j]
            nm = ADAM_B1 * m_ref[...] + (1.0 - ADAM_B1) * g
            nv = ADAM_B2 * v_ref[...] + (1.0 - ADAM_B2) * (g * g)
            g_ref[...] = g
            nm_ref[...] = nm
            nv_ref[...] = nv
            d_ref[...] = -ADAM_LR * ((nm / c1) / (jnp.sqrt(nv / c2) + ADAM_EPS) + ADAM_WD * w_ref[...])

        base = lead * Ld
        row = BS((1, tr, N), lambda l, i, base=base: (base + l, i, 0))
        prev = [] if outs is None else list(outs)
        outs = _pcall(body, name=f"{name}_{lead}", grid=(Ld, r // tr),
                      in_specs=[BS((N_DEV, 1, tr, N), lambda l, i: (0, l, i, 0)), row, row, row]
                      + [BS(memory_space=pl.ANY)] * len(prev),
                      out_specs=[row] * 4, out_shape=[SDS((n_lead * Ld, r, N), F32)] * 4,
                      input_output_aliases={4 + j: j for j in range(len(prev))},
                      compiler_params=_cparams(("parallel", "parallel")))(p.reshape(N_DEV, Ld, r, N), w, m, v, *prev)
    return [t.reshape(shape) for t in outs]


def _place():
    return lax.axis_index("x"), lax.axis_index("y"), lax.axis_index("c")


def _all_gather(xs, name):
    n = len(xs)
    pairs = [(i, l) for i, x in enumerate(xs) for l in range(x.shape[0])]

    def body(*refs):
        x_refs, o_refs = refs[:n], refs[n:2 * n]
        send_sems, recv_sems, local_sems = refs[2 * n:]
        x_, y_, c_ = _place()
        me, sibling = (x_, y_, c_), (x_, y_, 1 - c_)
        chips = [(1 - x_, y_), (x_, 1 - y_), (1 - x_, 1 - y_)]

        def copy(e, k, block, to, from_input=False):
            i, l = pairs[e]
            px, py, pc = block
            dst = o_refs[i].at[l, 4 * px + 2 * py + pc]
            return pltpu.make_async_remote_copy(
                src_ref=x_refs[i].at[l] if from_input else dst, dst_ref=dst,
                send_sem=send_sems.at[7 * e + k], recv_sem=recv_sems.at[7 * e + k],
                device_id=to, device_id_type=pl.DeviceIdType.MESH)

        every = range(len(pairs))
        mine = [pltpu.make_async_copy(x_refs[i].at[l], o_refs[i].at[l, 4 * x_ + 2 * y_ + c_], local_sems.at[e])
                for e, (i, l) in enumerate(pairs)]
        for cp in mine:
            cp.start()
        first = [copy(e, 0, me, sibling, True) for e in every]
        first += [copy(e, 1 + j, me, (*chip, c_), True) for j, chip in enumerate(chips) for e in every]
        for cp in first:
            cp.start()
        passed = []
        for j, chip in enumerate(chips):
            for e in every:
                copy(e, 1 + j, (*chip, c_), me).wait_recv()
            for e in every:
                cp = copy(e, 4 + j, (*chip, c_), sibling)
                cp.start()
                passed.append(cp)
        for e in every:
            copy(e, 0, sibling, me).wait_recv()
        for j, chip in enumerate(chips):
            for e in every:
                copy(e, 4 + j, (*chip, 1 - c_), me).wait_recv()
        for cp in first + passed:
            cp.wait_send()
        for cp in mine:
            cp.wait()

    any_spec = BS(memory_space=pl.ANY)
    return _pcall(body, name=name, in_specs=[any_spec] * n, out_specs=[any_spec] * n,
                  out_shape=[SDS((x.shape[0], N_DEV) + x.shape[1:], x.dtype) for x in xs],
                  scratch_shapes=[pltpu.SemaphoreType.DMA((7 * len(pairs),)), pltpu.SemaphoreType.DMA((7 * len(pairs),)),
                                  pltpu.SemaphoreType.DMA((len(pairs),))],
                  compiler_params=pltpu.CompilerParams(has_side_effects=True))(*xs)


def _peer_of(k, place):
    x_, y_, c_ = place
    fx, fy, fc = (k >> 2) & 1, (k >> 1) & 1, k & 1
    return fx + x_ - 2 * fx * x_, fy + y_ - 2 * fy * y_, fc + c_ - 2 * fc * c_


def _split_copy(src_ref, land_ref, send_sems, recv_sems, e, k, place, scatter):
    x_, y_, c_ = place
    px, py, pc = _peer_of(k, place)
    return pltpu.make_async_remote_copy(
        src_ref=src_ref.at[4 * px + 2 * py + pc] if scatter else src_ref, dst_ref=land_ref.at[4 * x_ + 2 * y_ + c_],
        send_sem=send_sems.at[7 * e + k - 1], recv_sem=recv_sems.at[7 * e + k - 1],
        device_id=(px, py, pc), device_id_type=pl.DeviceIdType.MESH)


_HBM_SPEC = BS(memory_space=pltpu.HBM)
_SEM_SPEC = BS(memory_space=pltpu.SEMAPHORE)
_EFFECT = pltpu.SideEffectType.DATAFLOW_SIDE_EFFECTING


def _copies_start(srcs, scatter, name, thru=None):
    n = len(srcs)
    lands = [lax.empty(s.shape if scatter else (N_DEV,) + s.shape, s.dtype) for s in srcs]
    passed = srcs + lands + list(thru or ())

    def body(*refs):
        src_refs, land_refs = refs[:n], refs[n:2 * n]
        send_sems, recv_sems = refs[len(passed)], refs[len(passed) + 1]
        token = refs[-1]
        place = _place()
        for e in range(n):
            for k in range(1, N_DEV):
                _split_copy(src_refs[e], land_refs[e], send_sems, recv_sems, e, k, place, scatter).start()
        token[...] = jnp.zeros_like(token)

    hbm = lambda t: pltpu.with_memory_space_constraint(t, pltpu.HBM)
    out = _pcall(body, name=name,
                 out_shape=(pltpu.SemaphoreType.DMA((7 * n,)), pltpu.SemaphoreType.DMA((7 * n,)),
                            *[pltpu.HBM(t.shape, t.dtype) for t in passed], SDS((8, LANES), F32)),
                 in_specs=[_HBM_SPEC] * len(passed),
                 out_specs=(_SEM_SPEC, _SEM_SPEC, *[_HBM_SPEC] * len(passed), BS(memory_space=pltpu.VMEM)),
                 input_output_aliases={i: 2 + i for i in range(len(passed))},
                 compiler_params=pltpu.CompilerParams(has_side_effects=_EFFECT))(*[hbm(t) for t in passed])
    return out[0], out[1], list(out[2:2 + n]), list(out[2 + n:2 + 2 * n]), list(out[2 + 2 * n:-1])


def _copies_wait(started, which, scatter, after, name):
    send_sems, recv_sems, srcs, lands, _ = started
    n = len(which)
    after = list(after) if isinstance(after, (list, tuple)) else [after]

    def body(*refs):
        src_refs, land_refs = refs[:n], refs[n:2 * n]
        send_s, recv_s = refs[2 * n], refs[2 * n + 1]
        place = _place()
        for j, e in enumerate(which):
            for k in range(1, N_DEV):
                cp = _split_copy(src_refs[j], land_refs[j], send_s, recv_s, e, k, place, scatter)
                cp.wait_send()
                cp.wait_recv()

    args = [srcs[e] for e in which] + [lands[e] for e in which]
    out = _pcall(body, name=name, out_shape=tuple(pltpu.HBM(t.shape, t.dtype) for t in args),
                 in_specs=[_HBM_SPEC] * (2 * n) + [_SEM_SPEC, _SEM_SPEC] + [BS(memory_space=pl.ANY)] * len(after),
                 out_specs=tuple([_HBM_SPEC] * (2 * n)), input_output_aliases={i: i for i in range(2 * n)},
                 compiler_params=pltpu.CompilerParams(has_side_effects=_EFFECT))(*args, send_sems, recv_sems, *after)
    return list(out[:n]), list(out[n:])


def _with_own_block(land, own_block):
    me = 4 * lax.axis_index("x") + 2 * lax.axis_index("y") + lax.axis_index("c")
    return lax.dynamic_update_index_in_dim(land, own_block, me, 0)


def _pad_flat(t, quantum=PACK_QUANTUM):
    f = t.reshape(-1)
    pad = (-f.shape[0]) % quantum
    return jnp.pad(f, (0, pad)) if pad else f


def _pack(arrs, dtype):
    return jnp.concatenate([_pad_flat(a.astype(dtype)) for a in arrs]).reshape(-1, LANES)


def _unpack(buf, shapes, lead=()):
    flat = buf.reshape(lead + (-1,))
    out, off = [], 0
    for s in shapes:
        n = math.prod(s)
        out.append(flat[..., off:off + n].reshape(lead + tuple(s)))
        off += n + (-n) % PACK_QUANTUM
    return out


def _full_from_gathered(g, axis):
    t = jnp.moveaxis(g, 0, axis)
    s = t.shape
    return t.reshape(s[:axis] + (s[axis] * s[axis + 1],) + s[axis + 2:])


def _parts_from_full(t, axis):
    s = t.shape
    t = t.reshape(s[:axis] + (N_DEV, s[axis] // N_DEV) + s[axis + 1:])
    return jnp.moveaxis(t, axis, 0)


def _head_rows(t):
    return jnp.repeat(t, HEAD_DIM).reshape(1, -1)


def _dilated_attention(q, k, v, name):
    no_sink = jnp.full((1, q.shape[1]), NEG, F32)
    outs, lses = zip(*[band_attention(q, k, v, no_sink, d, ATT_BLOCK, True, f"{name}_d{d}") for d in DILATIONS])
    return merge3(tuple(outs), tuple(lses), name + "_merge")


STAGES = (
    ("proj0", ('mix_norm',), (('ab_w_in', 0),)),
    ("mixer0", ('lru_conv_w', 'lru_conv_b', 'lru_ba', 'lru_bx', 'lru_lambda'),
     (('lru_wa', 0), ('lru_wx', 0), ('ab_w_out', 0))),
    ("xa0", ('xa_norm', 'xa_mem_norm'), (('xa_wq', 0), ('xa_wkv', 0), ('xa_wo', 0))),
    ("ffn0", ('ffn_norm',), (('ffn_w_gate_up', 0), ('ffn_w_down', 0))),
    ("mixer1", ('mix_norm', 'c_b_qkv', 'c_sinks', 'c_b_out'), (('c_w_qkv', 0), ('c_w_out', 0))),
    ("xa1", ('xa_norm', 'xa_mem_norm'), (('xa_wq', 1), ('xa_wkv', 1), ('xa_wo', 1))),
    ("ffn1", ('ffn_norm',), (('ffn_w_gate_up', 1), ('ffn_w_down', 1))),
)


def _stage_fn(stage, Wb, tabs, mem):
    layer = int(stage[-1])
    L = f"l{layer}"

    def run(S, Cw, h):
        def lin(a, key, bias, add, name, rows=None):
            wb, wc = Wb[key], Cw[key]
            if rows is not None:
                wb, wc = wb[rows], wc[rows]
            return linear(a, wb, wc, bias, add, name)

        def norm_lin(a, gain, key, bias, name):
            return norm_linear(a, gain, Wb[key], Cw[key], bias, key[0] in COLUMN_CUT, name)

        if stage == "mixer0":
            h, x_br, y_br, q, k, v = h
            C = S['lru_conv_w'].shape[-1]
            xc = conv4(x_br, S['lru_conv_w'][0], S['lru_conv_b'][0], L + "_conv")
            ga, gx = lru_gates(xc, Wb['lru_wa', 0], Cw['lru_wa', 0], S['lru_ba'][0],
                               Wb['lru_wx', 0], Cw['lru_wx', 0], S['lru_bx'][0], L + "_gates")
            rec = lru_scan(xc, ga, gx, y_br, S['lru_lambda'][0], L + "_scan")
            att = _dilated_attention(q, k, v, L + "_att")
            h = lin(att, ('ab_w_out', 0), None, h, L + "_w_out_att", slice(C, None))
            return lin(rec, ('ab_w_out', 0), None, h, L + "_w_out_rec", slice(0, C))
        if stage == "mixer1":
            qw = C_HEADS * HEAD_DIM
            kw = C_KV_HEADS * HEAD_DIM
            q, k, v, h = norm_linear_pieces(h, S['mix_norm'][1], Wb['c_w_qkv', 0], Cw['c_w_qkv', 0], S['c_b_qkv'][0], tabs,
                                            ((qw, True), (kw, True), (kw, False)), L + "_w_qkv")
            o = band_attention(q, k, v, _head_rows(S['c_sinks'][0]), 1, ATT_BLOCK - 1, False, L + "_att")
            return lin(o, ('c_w_out', 0), S['c_b_out'][0], h, L + "_w_out")
        if stage.startswith("xa"):
            xq, h = norm_lin(h, S['xa_norm'][layer], ('xa_wq', layer), None, L + "_xa_wq")
            xkv, _ = norm_lin(mem, S['xa_mem_norm'][layer], ('xa_wkv', layer), None, L + "_xa_wkv")
            return lin(cross_attention(xq, xkv, L + "_xa"), ('xa_wo', layer), None, h, L + "_xa_wo")
        gu, down = ('ffn_w_gate_up', layer), ('ffn_w_down', layer)
        return ffn_block(h, S['ffn_norm'][layer], Wb[gu], Cw[gu], Wb[down], Cw[down], L + "_ffn")

    return run


def kernel(x, mem, mix_norm, ab_w_in, lru_conv_w, lru_conv_b, lru_wa, lru_ba, lru_wx, lru_bx, lru_lambda, ab_w_out, c_w_qkv, c_b_qkv, c_sinks, c_w_out, c_b_out, xa_norm, xa_mem_norm, xa_wq, xa_wkv, xa_wo, ffn_norm, ffn_w_gate_up, ffn_w_down, final_norm, loss_target, m_mix_norm, m_ab_w_in, m_lru_conv_w, m_lru_conv_b, m_lru_wa, m_lru_ba, m_lru_wx, m_lru_bx, m_lru_lambda, m_ab_w_out, m_c_w_qkv, m_c_b_qkv, m_c_sinks, m_c_w_out, m_c_b_out, m_xa_norm, m_xa_mem_norm, m_xa_wq, m_xa_wkv, m_xa_wo, m_ffn_norm, m_ffn_w_gate_up, m_ffn_w_down, m_final_norm, v_mix_norm, v_ab_w_in, v_lru_conv_w, v_lru_conv_b, v_lru_wa, v_lru_ba, v_lru_wx, v_lru_bx, v_lru_lambda, v_ab_w_out, v_c_w_qkv, v_c_b_qkv, v_c_sinks, v_c_w_out, v_c_b_out, v_xa_norm, v_xa_mem_norm, v_xa_wq, v_xa_wkv, v_xa_wo, v_ffn_norm, v_ffn_w_gate_up, v_ffn_w_down, v_final_norm):
    w_loc = dict(zip(WEIGHT_NAMES, (mix_norm, ab_w_in, lru_conv_w, lru_conv_b, lru_wa, lru_ba, lru_wx, lru_bx, lru_lambda, ab_w_out, c_w_qkv, c_b_qkv, c_sinks, c_w_out, c_b_out, xa_norm, xa_mem_norm, xa_wq, xa_wkv, xa_wo, ffn_norm, ffn_w_gate_up, ffn_w_down, final_norm)))
    m_loc = dict(zip(WEIGHT_NAMES, (m_mix_norm, m_ab_w_in, m_lru_conv_w, m_lru_conv_b, m_lru_wa, m_lru_ba, m_lru_wx, m_lru_bx, m_lru_lambda, m_ab_w_out, m_c_w_qkv, m_c_b_qkv, m_c_sinks, m_c_w_out, m_c_b_out, m_xa_norm, m_xa_mem_norm, m_xa_wq, m_xa_wkv, m_xa_wo, m_ffn_norm, m_ffn_w_gate_up, m_ffn_w_down, m_final_norm)))
    v_loc = dict(zip(WEIGHT_NAMES, (v_mix_norm, v_ab_w_in, v_lru_conv_w, v_lru_conv_b, v_lru_wa, v_lru_ba, v_lru_wx, v_lru_bx, v_lru_lambda, v_ab_w_out, v_c_w_qkv, v_c_b_qkv, v_c_sinks, v_c_w_out, v_c_b_out, v_xa_norm, v_xa_mem_norm, v_xa_wq, v_xa_wkv, v_xa_wo, v_ffn_norm, v_ffn_w_gate_up, v_ffn_w_down, v_final_norm)))

    me = 4 * lax.axis_index("x") + 2 * lax.axis_index("y") + lax.axis_index("c")

    first_keys = list(STAGES[0][2])
    keys = [key for _, _, stage_keys in STAGES[1:] for key in stage_keys]
    shards = [_shard_view(n, w_loc[n])[l].astype(BF16) for n, l in keys]
    first_g = _all_gather([_pack([w_loc[n] for n in SMALL], F32)[None]]
                          + [_shard_view(n, w_loc[n])[l].astype(BF16)[None] for n, l in first_keys], "gather_first")
    gather = _copies_start(shards, False, "gather_start", thru=[first_g[0]])
    small_g = gather[4][0][0]
    Wb = {key: _full_from_gathered(g[0], _layer_shard_axis(key[0])) for key, g in zip(first_keys, first_g[1:])}
    S = {n: w_loc[n] for n in REPLICATED}
    for n, t in zip(SMALL, _unpack(small_g, [w_loc[n].shape for n in SMALL], lead=(N_DEV,))):
        S[n] = _full_from_gathered(t, SHARD_AXIS[n])

    tabs = _rope_tables(x.shape[1])
    w_in = Wb[first_keys[0]]
    hn0 = _rms_fwd_call(x[0], S['mix_norm'][0], "l0_w_in_norm", BF16)
    lru_w, att_w = N_DEV * lru_conv_w.shape[-1], B_HEADS * HEAD_DIM
    pieces = (("x", lru_w, False), ("y", lru_w, False), ("q", att_w, True), ("k", att_w, True), ("v", att_w, False))
    h, row = [x[0]], 0
    for piece, width, rotated in pieces:
        h.append(_mm(hn0, w_in[row:row + width], tb=True, rope=tabs if rotated else None, name="l0_w_in_" + piece,
                     tm=2048, tn=1024, tk=1024))
        row += width
    h = tuple(h)
    vjps = []
    for stage, small_names, stage_keys in STAGES[1:]:
        which = [keys.index(key) for key in stage_keys]
        _, lands = _copies_wait(gather, which, False, jax.tree.leaves(h)[-1], "gather_wait_" + stage)
        for e, land in zip(which, lands):
            Wb[keys[e]] = _full_from_gathered(_with_own_block(land, shards[e]), _layer_shard_axis(keys[e][0]))
        carriers = {key: jnp.zeros(Wb[key].shape, F32) for key in stage_keys}
        h, vjp_fn = jax.vjp(_stage_fn(stage, Wb, tabs, mem[0]), {n: S[n] for n in small_names}, carriers, h)
        vjps.append(vjp_fn)
    loss_part, dh, dg_final = _final_call(h, S['final_norm'], loss_target[0], "final_loss")

    grads = {'final_norm': dg_final.reshape(final_norm.shape)}
    exchanges, send_keys, send_parts = [], [], []

    def start_exchange(stage, dh):
        leaves, tree = jax.tree.flatten(dh)
        started = _copies_start(list(send_parts), True, "grad_start_" + stage, thru=leaves)
        exchanges.append((stage, started, list(send_keys)))
        send_keys.clear()
        send_parts.clear()
        return jax.tree.unflatten(tree, started[4])

    for (stage, small_names, stage_keys), vjp_fn in zip(reversed(STAGES[1:]), reversed(vjps)):
        g_small, g_big, dh = vjp_fn(dh)
        for n in small_names:
            grads[n] = grads[n] + g_small[n] if n in grads else g_small[n]
        send_keys += list(stage_keys)
        send_parts += [_parts_from_full(g_big[key], _layer_shard_axis(key[0])) for key in stage_keys]
        if stage == "xa1":
            continue
        if stage == "mixer0":
            small_parts = [_parts_from_full(grads[n], SHARD_AXIS[n]) for n in SMALL]
            send_keys.append("small")
            send_parts.append(jnp.stack([_pack([p[j] for p in small_parts], F32) for j in range(N_DEV)]))
        dh = start_exchange(stage, dh)
    d_res, d_proj = dh[0], _join_pieces_call(dh[1:], [rotated for _, _, rotated in pieces], tabs, "l0_w_in_dproj")
    send_keys.append(first_keys[0])
    send_parts.append(_parts_from_full(_mm(d_proj, hn0, ta=True, name="l0_w_in_dw", tm=1408, tn=1024, tk=2048),
                                       _layer_shard_axis(first_keys[0][0])))
    d_res, d_proj = start_exchange("proj0", (d_res, d_proj))
    dx, dg0 = _mm_rms_bwd(d_proj, w_in, x[0], S['mix_norm'][0], d_res, "l0_w_in_da", wt=True)
    grads['mix_norm'] = grads['mix_norm'] + jnp.concatenate([dg0, jnp.zeros_like(dg0)], axis=0)
    rep_names = REPLICATED + ["loss"]
    grads["loss"] = loss_part
    zero = jnp.zeros((1, 1), F32)
    for d in (w_loc, m_loc, v_loc):
        d["loss"] = zero
    rep_started = _copies_start([_pack([grads[n] for n in rep_names], F32)], False, "rep_grads_start")

    parts, out = {}, {}

    def end_exchange(stage, started, ex_keys, after):
        srcs, lands = _copies_wait(started, list(range(len(ex_keys))), True, after, "grad_wait_" + stage)
        for key, src, land in zip(ex_keys, srcs, lands):
            parts[key] = _with_own_block(land, lax.dynamic_index_in_dim(src, me, 0, keepdims=False))

    def adamw(p, names, call_name):
        if len(names) == 1:
            n = names[0]
            res = _adamw_call(p, _shard_view(n, w_loc[n]), _shard_view(n, m_loc[n]), _shard_view(n, v_loc[n]), call_name)
            for kind, t in zip(("grad", "delta", "new_m", "new_v"), res):
                out[kind, n] = _shard_view(n, t)
        else:
            res = _adamw_call(p, *[_pack([d[n] for n in names], F32) for d in (w_loc, m_loc, v_loc)], call_name)
            for kind, buf in zip(("grad", "delta", "new_m", "new_v"), res):
                for n, t in zip(names, _unpack(buf, [w_loc[n].shape for n in names])):
                    out[kind, n] = t

    for ex in exchanges[:-1]:
        end_exchange(*ex, dx)
    last_names = {key[0] for key in exchanges[-1][2]}
    for n in BIG:
        if n not in last_names:
            adamw([parts[n, l] for l in range(w_loc[n].shape[0])], [n], "adamw_" + n)
    adamw(parts["small"], SMALL, "adamw_small")
    end_exchange(*exchanges[-1], [out["new_v", n] for n in BIG if n not in last_names])
    for n in BIG:
        if n in last_names:
            adamw([parts[n, l] for l in range(w_loc[n].shape[0])], [n], "adamw_" + n)
    rep_src, rep_land = _copies_wait(rep_started, [0], False, out["new_v", "ab_w_in"], "rep_grads_wait")
    adamw(_with_own_block(rep_land[0], rep_src[0]), rep_names, "adamw_replicated")
    loss = out["grad", "loss"][0, 0]

    return (loss, dx[None], *[out[kind, n] for kind in ("grad", "delta", "new_m", "new_v") for n in WEIGHT_NAMES])
```

```python
import functools
import math

import jax
import jax.numpy as jnp
from jax import lax
from jax.experimental import pallas as pl
from jax.experimental.pallas import tpu as pltpu

F32 = jnp.float32
BF16 = jnp.bfloat16
SDS = jax.ShapeDtypeStruct
BS = pl.BlockSpec

N_DEV = 8
NORM_EPS = 1e-6
ROPE_THETA = 500000.0
HEAD_DIM = 64
ROT_DIM = 16
ATT_BLOCK = 128
LRU_C = 8.0
LRU_HEADS = 4
DILATIONS = (1, 4, 16)
B_HEADS = 8
C_HEADS = 16
C_KV_HEADS = 2
XA_HEADS = 4
XA_HEAD_DIM = 128
NEG = -1e30
ADAM_LR, ADAM_B1, ADAM_B2, ADAM_EPS, ADAM_WD, ADAM_STEP = 0.001, 0.9, 0.999, 1e-08, 0.01, 10
LANES = 128
VMEM_LIMIT = 48 * 1024 * 1024

WEIGHT_NAMES = ['mix_norm', 'ab_w_in', 'lru_conv_w', 'lru_conv_b', 'lru_wa', 'lru_ba', 'lru_wx', 'lru_bx',
                'lru_lambda', 'ab_w_out', 'c_w_qkv', 'c_b_qkv', 'c_sinks', 'c_w_out', 'c_b_out', 'xa_norm',
                'xa_mem_norm', 'xa_wq', 'xa_wkv', 'xa_wo', 'ffn_norm', 'ffn_w_gate_up', 'ffn_w_down', 'final_norm']
SHARD_AXIS = {'ab_w_in': 2, 'lru_conv_w': 2, 'lru_wa': 2, 'lru_ba': 2, 'lru_wx': 2, 'lru_bx': 2, 'ab_w_out': 1,
              'c_w_qkv': 2, 'c_b_qkv': 1, 'c_w_out': 1, 'c_b_out': 1, 'xa_wq': 1, 'xa_wkv': 1, 'xa_wo': 2,
              'ffn_w_gate_up': 2, 'ffn_w_down': 1}
BIG = ['ab_w_in', 'lru_wa', 'lru_wx', 'ab_w_out', 'c_w_qkv', 'c_w_out', 'xa_wq', 'xa_wkv', 'xa_wo',
       'ffn_w_gate_up', 'ffn_w_down']
SMALL = ['lru_conv_w', 'lru_ba', 'lru_bx', 'c_b_qkv', 'c_b_out']
REPLICATED = [n for n in WEIGHT_NAMES if n not in SHARD_AXIS]
COLUMN_CUT = ('ab_w_in', 'c_w_qkv', 'ffn_w_gate_up')
PACK_QUANTUM = 2048


def _shard_view(name, t):
    return jnp.swapaxes(t, -1, -2) if name in COLUMN_CUT else t


def _layer_shard_axis(name):
    return 0 if name in COLUMN_CUT else SHARD_AXIS[name] - 1


def _pcall(body, **kw):
    return pl.pallas_call(body, **kw)


def _cparams(sem=None):
    return pltpu.CompilerParams(dimension_semantics=sem, vmem_limit_bytes=VMEM_LIMIT)


def _tile(n, target, mult=LANES):
    if n <= target:
        return n
    t = (target // mult) * mult
    while t >= mult:
        if n % t == 0:
            return t
        t -= mult
    return n


def _sigmoid(x):
    return 1.0 / (1.0 + jnp.exp(-x))


def _expm1(x):
    small = x * (1.0 + x * (0.5 + x * (1.0 / 6.0 + x * (1.0 / 24.0))))
    return jnp.where(jnp.abs(x) < 0.03, small, jnp.exp(x) - 1.0)


_GELU_C = math.sqrt(2.0 / math.pi)


def _gelu_parts(y):
    y2 = y * y
    th = jnp.tanh(_GELU_C * (y + 0.044715 * y * y2))
    g = 0.5 * y * (1.0 + th)
    dg = 0.5 * (1.0 + th) + 0.5 * y * (1.0 - th * th) * _GELU_C * (1.0 + 3.0 * 0.044715 * y2)
    return g, dg


def _rotate(xv, tab_refs, inverse=False):
    W = xv.shape[1]
    half = ROT_DIM // 2
    c, sa, sb = (jnp.tile(t[...], (1, W // LANES)) for t in tab_refs)
    if not inverse:
        return xv * c + pltpu.roll(xv, half, axis=1) * sa + pltpu.roll(xv, W - half, axis=1) * sb
    return xv * c + pltpu.roll(xv * sa, W - half, axis=1) + pltpu.roll(xv * sb, half, axis=1)


def _mm(a, b, *, ta=False, tb=False, bias=None, add=None, rope=None, name, tm=512, tn=512, tk=2048):
    M, K = (a.shape[1], a.shape[0]) if ta else a.shape
    N = b.shape[0] if tb else b.shape[1]
    tm, tn, tk = _tile(M, tm), _tile(N, tn), _tile(K, tk)
    nk = K // tk
    dn = (((0 if ta else 1,), (1 if tb else 0,)), ((), ()))

    def body(*refs):
        a_ref, b_ref = refs[0], refs[1]
        pos = 2
        bias_ref = add_ref = None
        if bias is not None:
            bias_ref = refs[pos]
            pos += 1
        if add is not None:
            add_ref = refs[pos]
            pos += 1
        tab_refs = refs[pos:pos + 3] if rope is not None else None
        pos += 3 if rope is not None else 0
        o_ref = refs[pos]
        part = lax.dot_general(a_ref[...].astype(BF16), b_ref[...].astype(BF16), dn, preferred_element_type=F32)

        def finish(r):
            if bias_ref is not None:
                r = r + bias_ref[...]
            if add_ref is not None:
                r = r + add_ref[...]
            o_ref[...] = r if tab_refs is None else _rotate(r, tab_refs)

        if nk == 1:
            finish(part)
            return
        acc_ref = refs[pos + 1]
        k = pl.program_id(2)

        @pl.when(k == 0)
        def _():
            acc_ref[...] = part

        @pl.when((k > 0) & (k < nk - 1))
        def _():
            acc_ref[...] += part

        @pl.when(k == nk - 1)
        def _():
            finish(acc_ref[...] + part)

    in_specs = [BS((tk, tm), lambda i, j, k: (k, i)) if ta else BS((tm, tk), lambda i, j, k: (i, k)),
                BS((tn, tk), lambda i, j, k: (j, k)) if tb else BS((tk, tn), lambda i, j, k: (k, j))]
    args = [a, b]
    if bias is not None:
        in_specs.append(BS((1, tn), lambda i, j, k: (0, j)))
        args.append(bias.reshape(1, N))
    if add is not None:
        in_specs.append(BS((tm, tn), lambda i, j, k: (i, j)))
        args.append(add)
    if rope is not None:
        in_specs += [BS((tm, LANES), lambda i, j, k: (i, 0))] * 3
        args += list(rope)
    return _pcall(body, name=name, grid=(M // tm, N // tn, nk), in_specs=in_specs,
                  out_specs=BS((tm, tn), lambda i, j, k: (i, j)), out_shape=SDS((M, N), F32),
                  scratch_shapes=[pltpu.VMEM((tm, tn), F32)] if nk > 1 else [],
                  compiler_params=_cparams(("parallel", "parallel", "arbitrary")))(*args)


def _mm_rms_bwd(dy, wb, x, g, add, name, wt=False):
    T, Kc = dy.shape
    Dm = wb.shape[1] if wt else wb.shape[0]
    tm, tk = _tile(T, 1024), _tile(Kc, 1408)
    nk = Kc // tk
    dn = (((1,), (0 if wt else 1,)), ((), ()))

    def body(*refs):
        dy_ref, w_ref, x_ref, g_ref = refs[:4]
        dx_ref, dg_ref, acc_ref = refs[-3:]
        i, k = pl.program_id(0), pl.program_id(1)
        part = lax.dot_general(dy_ref[...].astype(BF16), w_ref[...], dn, preferred_element_type=F32)

        @pl.when(k == 0)
        def _():
            acc_ref[...] = part

        @pl.when((k > 0) & (k < nk - 1))
        def _():
            acc_ref[...] += part

        @pl.when((i == 0) & (k == 0))
        def _():
            dg_ref[...] = jnp.zeros_like(dg_ref)

        @pl.when(k == nk - 1)
        def _():
            d = part if nk == 1 else acc_ref[...] + part
            xv = x_ref[...]
            r = lax.rsqrt(jnp.mean(xv * xv, axis=-1, keepdims=True) + NORM_EPS)
            xh = xv * r
            dyg = d * g_ref[...]
            dx = r * (dyg - xh * jnp.mean(dyg * xh, axis=-1, keepdims=True))
            dx_ref[...] = dx if add is None else dx + refs[4][...]
            dg_ref[...] += jnp.sum(d * xh, axis=0, keepdims=True)

    row = BS((tm, Dm), lambda i, k: (i, 0))
    vec = BS((1, Dm), lambda i, k: (0, 0))
    extra = [] if add is None else [add]
    return _pcall(body, name=name, grid=(T // tm, nk),
                  in_specs=[BS((tm, tk), lambda i, k: (i, k)),
                            BS((tk, Dm), lambda i, k: (k, 0)) if wt else BS((Dm, tk), lambda i, k: (0, k)), row, vec]
                  + [row] * len(extra),
                  out_specs=[row, vec], out_shape=[SDS((T, Dm), F32), SDS((1, Dm), F32)],
                  scratch_shapes=[pltpu.VMEM((tm, Dm), F32)],
                  compiler_params=_cparams(("arbitrary", "arbitrary")))(dy, wb, x, g.reshape(1, Dm), *extra)


def _colsum(x, name):
    T, N = x.shape
    tt = _tile(T, 512, 8)

    def body(x_ref, o_ref):
        @pl.when(pl.program_id(0) == 0)
        def _():
            o_ref[...] = jnp.zeros_like(o_ref)

        o_ref[...] += jnp.sum(x_ref[...], axis=0, keepdims=True)

    return _pcall(body, name=name, grid=(T // tt,), in_specs=[BS((tt, N), lambda i: (i, 0))],
                  out_specs=BS((1, N), lambda i: (0, 0)), out_shape=SDS((1, N), F32),
                  compiler_params=_cparams(("arbitrary",)))(x)


@functools.partial(jax.custom_vjp, nondiff_argnums=(5,))
def linear(a, wb, wc, bias, add, name):
    return _linear_fwd(a, wb, wc, bias, add, name)[0]


def _linear_fwd(a, wb, wc, bias, add, name):
    out = _mm(a, wb, bias=bias, add=add, name=name, tm=1024, tn=1024, tk=1024)
    return out, (a, wb, bias is not None, add is not None)


def _linear_bwd(name, res, g):
    a, wb, has_bias, has_add = res
    da = _mm(g, wb, tb=True, name=name + "_da", tm=1024, tn=1024, tk=1024)
    dw = _mm(a, g, ta=True, name=name + "_dw", tm=1024, tn=1024, tk=1024)
    dbias = _colsum(g, name + "_db").reshape(-1) if has_bias else None
    return da, jnp.zeros_like(wb), dw, dbias, (g if has_add else None)


linear.defvjp(_linear_fwd, _linear_bwd)


def _rms_fwd_call(x, g, name, out_dtype=F32):
    T, Dm = x.shape
    tt = _tile(T, 512, 16)

    def body(x_ref, g_ref, o_ref):
        xv = x_ref[...]
        r = lax.rsqrt(jnp.mean(xv * xv, axis=-1, keepdims=True) + NORM_EPS)
        o_ref[...] = ((xv * r) * g_ref[...]).astype(out_dtype)

    return _pcall(body, name=name, grid=(T // tt,),
                  in_specs=[BS((tt, Dm), lambda i: (i, 0)), BS((1, Dm), lambda i: (0, 0))],
                  out_specs=BS((tt, Dm), lambda i: (i, 0)), out_shape=SDS((T, Dm), out_dtype),
                  compiler_params=_cparams(("parallel",)))(x, g.reshape(1, Dm))


@functools.partial(jax.custom_vjp, nondiff_argnums=(5, 6))
def norm_linear(x, g, wb, wc, bias, wt, name):
    return _norm_linear_fwd(x, g, wb, wc, bias, wt, name)[0]


def _norm_linear_fwd(x, g, wb, wc, bias, wt, name):
    hn = _rms_fwd_call(x, g, name + "_norm", BF16)
    return (_mm(hn, wb, tb=wt, bias=bias, name=name, tm=1024, tn=1408, tk=1024), x), (x, g, hn, wb, bias is not None)


def _norm_linear_bwd(wt, name, res, cts):
    x, g, hn, wb, has_bias = res
    dy, dres = cts
    if wt:
        dw = _mm(dy, hn, ta=True, name=name + "_dw", tm=1408, tn=1024, tk=2048)
    else:
        dw = _mm(hn, dy, ta=True, name=name + "_dw", tm=1024, tn=1408, tk=1024)
    dx, dg = _mm_rms_bwd(dy, wb, x, g, dres, name + "_da", wt)
    dbias = _colsum(dy, name + "_db").reshape(-1) if has_bias else None
    return dx, dg.reshape(g.shape), jnp.zeros_like(wb), dw, dbias


norm_linear.defvjp(_norm_linear_fwd, _norm_linear_bwd)


def _rope_tables(T):
    half = ROT_DIM // 2
    inv = ROPE_THETA ** (-jnp.arange(0, ROT_DIM, 2, dtype=F32) / ROT_DIM)
    lane = jnp.arange(LANES) % HEAD_DIM
    freq = jnp.where(lane < ROT_DIM, inv[lane % half], 0.0)
    ang = jnp.arange(T, dtype=F32)[:, None] * freq[None, :]
    c, s = jnp.cos(ang), jnp.sin(ang)
    sa = jnp.where((lane >= half) & (lane < ROT_DIM), s, 0.0)
    sb = jnp.where(lane < half, -s, 0.0)
    return c, sa, sb


def _join_pieces_call(pieces, rotated, tabs, name):
    T = pieces[0].shape[0]
    widths = [p.shape[1] for p in pieces]
    tt = _tile(T, 256, 8)

    def body(*refs):
        tab_refs, o_ref = refs[len(pieces):len(pieces) + 3], refs[-1]
        off = 0
        for p_ref, w, r in zip(refs, widths, rotated):
            o_ref[:, off:off + w] = _rotate(p_ref[...], tab_refs, inverse=True) if r else p_ref[...]
            off += w

    return _pcall(body, name=name, grid=(T // tt,),
                  in_specs=[BS((tt, w), lambda i: (i, 0)) for w in widths] + [BS((tt, LANES), lambda i: (i, 0))] * 3,
                  out_specs=BS((tt, sum(widths)), lambda i: (i, 0)), out_shape=SDS((T, sum(widths)), F32),
                  compiler_params=_cparams(("parallel",)))(*pieces, *tabs)


@functools.partial(jax.custom_vjp, nondiff_argnums=(6, 7))
def norm_linear_pieces(x, g, wb, wc, bias, tabs, pieces, name):
    return _norm_linear_pieces_fwd(x, g, wb, wc, bias, tabs, pieces, name)[0]


def _norm_linear_pieces_fwd(x, g, wb, wc, bias, tabs, pieces, name):
    hn = _rms_fwd_call(x, g, name + "_norm", BF16)
    outs, row = [], 0
    for i, (width, rotated) in enumerate(pieces):
        outs.append(_mm(hn, wb[row:row + width], tb=True, bias=None if bias is None else bias[row:row + width],
                        rope=tabs if rotated else None, name=f"{name}_{i}", tm=2048, tn=1024, tk=1024))
        row += width
    return (*outs, x), (x, g, hn, wb, tabs, bias is not None)


def _norm_linear_pieces_bwd(pieces, name, res, cts):
    x, g, hn, wb, tabs, has_bias = res
    dy = _join_pieces_call(cts[:-1], [rotated for _, rotated in pieces], tabs, name + "_join")
    dw = _mm(dy, hn, ta=True, name=name + "_dw", tm=1408, tn=1024, tk=1024)
    dx, dg = _mm_rms_bwd(dy, wb, x, g, cts[-1], name + "_da", wt=True)
    dbias = _colsum(dy, name + "_db").reshape(-1) if has_bias else None
    return dx, dg.reshape(g.shape), jnp.zeros_like(wb), dw, dbias, jax.tree.map(jnp.zeros_like, tabs)


norm_linear_pieces.defvjp(_norm_linear_pieces_fwd, _norm_linear_pieces_bwd)


def _conv_fwd_call(x, w, b, name):
    T, C = x.shape
    tt = _tile(T, 512, 8)
    per = tt // 8

    def body(x_ref, halo_ref, w_ref, b_ref, o_ref):
        i = pl.program_id(0)
        halo = jnp.where(i > 0, halo_ref[...], 0.0)
        e = jnp.concatenate([halo, x_ref[...]], axis=0)
        acc = b_ref[...]
        for k in (3, 2, 1):
            acc = acc + pltpu.roll(e, k, axis=0)[8:, :] * w_ref[3 - k:4 - k, :]
        o_ref[...] = acc + x_ref[...] * w_ref[3:4, :]

    return _pcall(body, name=name, grid=(T // tt,),
                  in_specs=[BS((tt, C), lambda i: (i, 0)), BS((8, C), lambda i: (jnp.maximum(i * per - 1, 0), 0)),
                            BS((4, C), lambda i: (0, 0)), BS((1, C), lambda i: (0, 0))],
                  out_specs=BS((tt, C), lambda i: (i, 0)), out_shape=SDS((T, C), F32),
                  compiler_params=_cparams(("parallel",)))(x, x, w, b.reshape(1, C))


def _conv_bwd_call(x, w, dy, name):
    T, C = x.shape
    tt = _tile(T, 512, 8)
    per = tt // 8
    nt = T // tt

    def body(x_ref, halo_ref, w_ref, dy_ref, nxt_ref, dx_ref, dwb_ref):
        i = pl.program_id(0)
        halo = jnp.where(i > 0, halo_ref[...], 0.0)
        e = jnp.concatenate([halo, x_ref[...]], axis=0)
        dy = dy_ref[...]
        nxt = jnp.where(i < nt - 1, nxt_ref[...], 0.0)
        f = jnp.concatenate([dy, nxt], axis=0)
        dx = dy * w_ref[3:4, :]
        rows = [None] * 4
        rows[3] = jnp.sum(dy * x_ref[...], axis=0, keepdims=True)
        for k in (1, 2, 3):
            dx = dx + pltpu.roll(f, tt + 8 - k, axis=0)[:tt, :] * w_ref[3 - k:4 - k, :]
            rows[3 - k] = jnp.sum(dy * pltpu.roll(e, k, axis=0)[8:, :], axis=0, keepdims=True)
        dx_ref[...] = dx
        upd = jnp.concatenate(rows + [jnp.sum(dy, axis=0, keepdims=True), jnp.zeros((3, C), F32)], axis=0)

        @pl.when(i == 0)
        def _():
            dwb_ref[...] = jnp.zeros_like(dwb_ref)

        dwb_ref[...] += upd

    row = BS((tt, C), lambda i: (i, 0))
    return _pcall(body, name=name, grid=(nt,),
                  in_specs=[row, BS((8, C), lambda i: (jnp.maximum(i * per - 1, 0), 0)), BS((4, C), lambda i: (0, 0)),
                            row, BS((8, C), lambda i: (jnp.minimum((i + 1) * per, T // 8 - 1), 0))],
                  out_specs=[row, BS((8, C), lambda i: (0, 0))],
                  out_shape=[SDS((T, C), F32), SDS((8, C), F32)],
                  compiler_params=_cparams(("arbitrary",)))(x, x, w, dy, dy)


@functools.partial(jax.custom_vjp, nondiff_argnums=(3,))
def conv4(x, w, b, name):
    return _conv_fwd_call(x, w, b, name)


def _conv4_fwd(x, w, b, name):
    return _conv_fwd_call(x, w, b, name), (x, w)


def _conv4_bwd(name, res, dy):
    x, w = res
    dx, dwb = _conv_bwd_call(x, w, dy, name + "_bwd")
    return dx, dwb[0:4], dwb[4]


conv4.defvjp(_conv4_fwd, _conv4_bwd)


def _gates_fwd_call(xc, wa, ba, wx, bx, name):
    T, C = xc.shape
    hd = C // LRU_HEADS
    tt = _tile(T, 512, 8)

    def body(x_ref, wa_ref, ba_ref, wx_ref, bx_ref, ga_ref, gx_ref):
        xb = x_ref[...].astype(BF16)
        ga_ref[...] = jnp.dot(xb, wa_ref[0].astype(BF16), preferred_element_type=F32) + ba_ref[...]
        gx_ref[...] = jnp.dot(xb, wx_ref[0].astype(BF16), preferred_element_type=F32) + bx_ref[...]

    blk = BS((tt, hd), lambda i, h: (i, h))
    wsp = BS((1, hd, hd), lambda i, h: (h, 0, 0))
    bsp = BS((1, hd), lambda i, h: (0, h))
    return _pcall(body, name=name, grid=(T // tt, LRU_HEADS), in_specs=[blk, wsp, bsp, wsp, bsp],
                  out_specs=[blk, blk], out_shape=[SDS((T, C), F32)] * 2,
                  compiler_params=_cparams(("parallel", "parallel")))(xc, wa, ba.reshape(1, C), wx, bx.reshape(1, C))


def _gates_bwd_x_call(dga, dgx, wa, wx, name):
    T, C = dga.shape
    hd = C // LRU_HEADS
    tt = _tile(T, 512, 8)
    dn = (((1,), (1,)), ((), ()))

    def body(da_ref, dx_ref, wa_ref, wx_ref, o_ref):
        o_ref[...] = (lax.dot_general(da_ref[...].astype(BF16), wa_ref[0].astype(BF16), dn, preferred_element_type=F32)
                      + lax.dot_general(dx_ref[...].astype(BF16), wx_ref[0].astype(BF16), dn, preferred_element_type=F32))

    blk = BS((tt, hd), lambda i, h: (i, h))
    wsp = BS((1, hd, hd), lambda i, h: (h, 0, 0))
    return _pcall(body, name=name, grid=(T // tt, LRU_HEADS), in_specs=[blk, blk, wsp, wsp], out_specs=blk,
                  out_shape=SDS((T, C), F32), compiler_params=_cparams(("parallel", "parallel")))(dga, dgx, wa, wx)


def _gates_bwd_w_call(xc, dga, dgx, name):
    T, C = xc.shape
    hd = C // LRU_HEADS
    tt = _tile(T, 512, 8)
    dn = (((0,), (0,)), ((), ()))

    def body(x_ref, da_ref, dx_ref, dwa_ref, dwx_ref, dba_ref, dbx_ref):
        @pl.when(pl.program_id(1) == 0)
        def _():
            dwa_ref[...] = jnp.zeros_like(dwa_ref)
            dwx_ref[...] = jnp.zeros_like(dwx_ref)
            dba_ref[...] = jnp.zeros_like(dba_ref)
            dbx_ref[...] = jnp.zeros_like(dbx_ref)

        xb = x_ref[...].astype(BF16)
        da, dx = da_ref[...], dx_ref[...]
        dwa_ref[0] += lax.dot_general(xb, da.astype(BF16), dn, preferred_element_type=F32)
        dwx_ref[0] += lax.dot_general(xb, dx.astype(BF16), dn, preferred_element_type=F32)
        dba_ref[...] += jnp.sum(da, axis=0, keepdims=True)
        dbx_ref[...] += jnp.sum(dx, axis=0, keepdims=True)

    blk = BS((tt, hd), lambda h, i: (i, h))
    wsp = BS((1, hd, hd), lambda h, i: (h, 0, 0))
    bsp = BS((1, hd), lambda h, i: (0, h))
    return _pcall(body, name=name, grid=(LRU_HEADS, T // tt), in_specs=[blk, blk, blk],
                  out_specs=[wsp, wsp, bsp, bsp],
                  out_shape=[SDS((LRU_HEADS, hd, hd), F32)] * 2 + [SDS((1, C), F32)] * 2,
                  compiler_params=_cparams(("parallel", "arbitrary")))(xc, dga, dgx)


@functools.partial(jax.custom_vjp, nondiff_argnums=(7,))
def lru_gates(xc, wa, wa_c, ba, wx, wx_c, bx, name):
    return tuple(_gates_fwd_call(xc, wa, ba, wx, bx, name))


def _lru_gates_fwd(xc, wa, wa_c, ba, wx, wx_c, bx, name):
    return tuple(_gates_fwd_call(xc, wa, ba, wx, bx, name)), (xc, wa, wx, ba.shape)


def _lru_gates_bwd(name, res, g):
    xc, wa, wx, bshape = res
    dga, dgx = g
    dxc = _gates_bwd_x_call(dga, dgx, wa, wx, name + "_dx")
    dwa, dwx, dba, dbx = _gates_bwd_w_call(xc, dga, dgx, name + "_dw")
    return dxc, jnp.zeros_like(wa), dwa, dba.reshape(bshape), jnp.zeros_like(wx), dwx, dbx.reshape(bshape)


lru_gates.defvjp(_lru_gates_fwd, _lru_gates_bwd)


def _lru_coeffs(xc, ga, gx, lam):
    r = _sigmoid(ga)
    ig = _sigmoid(gx)
    z = -lam
    sp = jnp.maximum(z, 0.0) + jnp.log(1.0 + jnp.exp(-jnp.abs(z)))
    la = -LRU_C * r * sp
    a = jnp.exp(la)
    s = jnp.sqrt(-_expm1(2.0 * la))
    return r, ig, sp, a, s


LRU_TT = 256


def _scan_fwd_call(xc, ga, gx, y, lam, name):
    T, C = xc.shape
    tt = _tile(T, LRU_TT, 8)

    def body(xc_ref, ga_ref, gx_ref, y_ref, lam_ref, h_ref, rec_ref, a_buf, carry):
        @pl.when(pl.program_id(0) == 0)
        def _():
            carry[...] = jnp.zeros_like(carry)

        xcv = xc_ref[...]
        _, ig, _, a, s = _lru_coeffs(xcv, ga_ref[...], gx_ref[...], lam_ref[...])
        a_buf[...] = a
        h_ref[...] = s * (ig * xcv)

        def step(t, h):
            hn = a_buf[pl.ds(t, 1), :] * h + h_ref[pl.ds(t, 1), :]
            h_ref[pl.ds(t, 1), :] = hn
            return hn

        carry[0:1, :] = lax.fori_loop(0, tt, step, carry[0:1, :], unroll=8)
        g, _ = _gelu_parts(y_ref[...])
        rec_ref[...] = h_ref[...] * g

    row = BS((tt, C), lambda i: (i, 0))
    vec = BS((1, C), lambda i: (0, 0))
    return _pcall(body, name=name, grid=(T // tt,), in_specs=[row, row, row, row, vec], out_specs=[row, row],
                  out_shape=[SDS((T, C), F32)] * 2,
                  scratch_shapes=[pltpu.VMEM((tt, C), F32), pltpu.VMEM((8, C), F32)],
                  compiler_params=_cparams(("arbitrary",)))(xc, ga, gx, y, lam.reshape(1, C))


def _scan_bwd_call(xc, ga, gx, y, lam, h, drec, name):
    T, C = xc.shape
    tt = _tile(T, LRU_TT, 8)
    nt = T // tt
    per = tt // 8

    def body(xc_ref, ga_ref, gx_ref, y_ref, lam_ref, h_ref, halo_ref, dr_ref,
             dga_ref, dgx_ref, dxc_ref, dy_ref, dlam_ref, a_buf, g_buf, carry):
        i = pl.program_id(0)

        @pl.when(i == 0)
        def _():
            carry[...] = jnp.zeros_like(carry)
            dlam_ref[...] = jnp.zeros_like(dlam_ref)

        xcv, lam = xc_ref[...], lam_ref[...]
        r, ig, sp, a, s = _lru_coeffs(xcv, ga_ref[...], gx_ref[...], lam)
        gel, dgel = _gelu_parts(y_ref[...])
        drec = dr_ref[...]
        hv = h_ref[...]
        dy_ref[...] = drec * hv * dgel
        a_buf[...] = a
        g_buf[...] = drec * gel

        def step(j, q):
            t = tt - 1 - j
            g = g_buf[pl.ds(t, 1), :] + q
            g_buf[pl.ds(t, 1), :] = g
            return a_buf[pl.ds(t, 1), :] * g

        carry[0:1, :] = lax.fori_loop(0, tt, step, carry[0:1, :], unroll=8)
        g = g_buf[...]
        halo = jnp.where(i < nt - 1, halo_ref[...], 0.0)
        hprev = pltpu.roll(jnp.concatenate([halo, hv], axis=0), 1, axis=0)[8:, :]
        da = g * hprev
        dig = g * s * xcv
        ds = g * ig * xcv
        dla = da * a - ds * (a * a) / s
        dga_ref[...] = dla * (-LRU_C * sp) * r * (1.0 - r)
        dgx_ref[...] = dig * ig * (1.0 - ig)
        dxc_ref[...] = g * s * ig
        dlam_ref[...] += jnp.sum(dla * r, axis=0, keepdims=True) * (LRU_C * _sigmoid(-lam))

    row = BS((tt, C), lambda i: (nt - 1 - i, 0))
    vec = BS((1, C), lambda i: (0, 0))
    halo = BS((8, C), lambda i: (jnp.maximum((nt - 1 - i) * per - 1, 0), 0))
    return _pcall(body, name=name, grid=(nt,), in_specs=[row, row, row, row, vec, row, halo, row],
                  out_specs=[row, row, row, row, vec], out_shape=[SDS((T, C), F32)] * 4 + [SDS((1, C), F32)],
                  scratch_shapes=[pltpu.VMEM((tt, C), F32), pltpu.VMEM((tt, C), F32), pltpu.VMEM((8, C), F32)],
                  compiler_params=_cparams(("arbitrary",)))(xc, ga, gx, y, lam.reshape(1, C), h, h, drec)


@functools.partial(jax.custom_vjp, nondiff_argnums=(5,))
def lru_scan(xc, ga, gx, y, lam, name):
    return _scan_fwd_call(xc, ga, gx, y, lam, name)[1]


def _lru_scan_fwd(xc, ga, gx, y, lam, name):
    h, rec = _scan_fwd_call(xc, ga, gx, y, lam, name)
    return rec, (xc, ga, gx, y, lam, h)


def _lru_scan_bwd(name, res, drec):
    xc, ga, gx, y, lam, h = res
    dga, dgx, dxc, dy, dlam = _scan_bwd_call(xc, ga, gx, y, lam, h, drec, name + "_bwd")
    return dxc, dga, dgx, dy, dlam.reshape(lam.shape)


lru_scan.defvjp(_lru_scan_fwd, _lru_scan_bwd)


def _att_batch(d, shared=False):
    if shared:
        return 8, 1
    return (4, 1) if d == 1 else (1, min(d, 8))


def _att_masks(n, max_dist):
    qi = lax.broadcasted_iota(jnp.int32, (1, 2 * ATT_BLOCK, 2 * ATT_BLOCK), 1) & (ATT_BLOCK - 1)
    kj = lax.broadcasted_iota(jnp.int32, (1, 2 * ATT_BLOCK, 2 * ATT_BLOCK), 2)
    prev = (kj < ATT_BLOCK) & (kj >= qi + (ATT_BLOCK - max_dist)) & (n > 0)
    cur = (kj >= ATT_BLOCK) & (kj - ATT_BLOCK <= qi)
    return prev | cur


def _lo_lanes(rows):
    return lax.broadcasted_iota(jnp.int32, (rows, LANES), 1) < HEAD_DIM


def _lane_half(rows):
    return lax.broadcasted_iota(jnp.int32, (rows, LANES), 1) // HEAD_DIM


def _stack_heads(x2):
    lo = _lo_lanes(ATT_BLOCK)
    zero = jnp.zeros_like(x2)
    return jnp.concatenate([jnp.where(lo, x2, zero), jnp.where(lo, zero, x2)], axis=0)


def _unstack_heads(y):
    return jnp.where(_lo_lanes(ATT_BLOCK), y[:ATT_BLOCK], y[ATT_BLOCK:])


def _per_head_col(x2):
    return jnp.concatenate([x2[:, 0:1], x2[:, HEAD_DIM:HEAD_DIM + 1]], axis=0)


def _head_sums(x2):
    lo = _lo_lanes(ATT_BLOCK)
    return jnp.concatenate([jnp.sum(jnp.where(lo, x2, 0.0), axis=1, keepdims=True),
                            jnp.sum(jnp.where(lo, 0.0, x2), axis=1, keepdims=True)], axis=0)


def _att_specs(d, Wq, Wk, nb, clamp):
    shared = Wk != Wq
    cgw, sb = _att_batch(d, shared)
    cur = (lambda n: jnp.minimum(n, nb - 1)) if clamp else (lambda n: n)
    rows, qw, kw = ATT_BLOCK * d, cgw * LANES, (LANES if shared else cgw * LANES)
    kcol = (lambda g: 0) if shared else (lambda g: g)
    qsp = BS((rows, qw), lambda g, n: (cur(n), g))
    csp = BS((rows, kw), lambda g, n: (cur(n), kcol(g)))
    psp = BS((rows, kw), lambda g, n: (jnp.maximum(cur(n) - 1, 0), kcol(g)))
    return cgw, sb, shared, qsp, csp, psp, qw, kw


def _att_streams(d, sb, work):
    if d == 1:
        work([slice(None)])
        return

    def one(j, carry):
        work([pl.ds(j * sb + i, ATT_BLOCK, stride=d) for i in range(sb)])
        return carry

    lax.fori_loop(0, d // sb, one, 0)


def _att_problem_loads(rows, cgw, shared, g, q_ref, kc_ref, kp_ref, vc_ref, vp_ref, sk_ref):
    half = _lane_half(ATT_BLOCK)

    def kv(ref, r, p):
        x = ref[r, :]
        if not shared:
            return x[:, p * LANES:(p + 1) * LANES]
        return jnp.where(half == p // 4, x, pltpu.roll(x, HEAD_DIM, axis=1))

    qs, kb, vb, sk = [], [], [], []
    for r in rows:
        qrow = q_ref[r, :]
        for p in range(cgw):
            cols = slice(p * LANES, (p + 1) * LANES)
            qs.append(_stack_heads(qrow[:, cols].astype(BF16)))
            kb.append(jnp.concatenate([kv(kp_ref, r, p), kv(kc_ref, r, p)], axis=0).astype(BF16))
            vb.append(jnp.concatenate([kv(vp_ref, r, p), kv(vc_ref, r, p)], axis=0).astype(BF16))
            sk.append(_per_head_col(jnp.broadcast_to(sk_ref[:, cols], (ATT_BLOCK, LANES))))
    return jnp.stack(qs), jnp.stack(kb), jnp.stack(vb), jnp.stack(sk)


_BDOT_NT = (((2,), (2,)), ((0,), (0,)))
_BDOT_NN = (((2,), (1,)), ((0,), (0,)))
_BDOT_TN = (((1,), (1,)), ((0,), (0,)))


def _att_fwd_call(q, k, v, sinks, d, max_dist, name):
    T, Wq = q.shape
    Wk = k.shape[1]
    nb = T // (d * ATT_BLOCK)
    cgw, sb, shared, qsp, csp, psp, qw, kw = _att_specs(d, Wq, Wk, nb, False)
    G = Wq // qw
    assert not shared or (d == 1 and Wk == LANES and G == 1 and cgw == 8), "a shared kv pair serves 2 x 8 query heads"

    def body(q_ref, kc_ref, kp_ref, vc_ref, vp_ref, sk_ref, o_ref, lse_ref):
        g, n = pl.program_id(0), pl.program_id(1)

        def work(rows):
            qs, kband, vband, sk = _att_problem_loads(rows, cgw, shared, g, q_ref, kc_ref, kp_ref, vc_ref, vp_ref, sk_ref)
            s = lax.dot_general(qs, kband, _BDOT_NT, preferred_element_type=F32) * (HEAD_DIM ** -0.5)
            s = jnp.where(_att_masks(n, max_dist), s, NEG)
            m = jnp.maximum(jnp.max(s, axis=-1, keepdims=True), sk)
            e = jnp.exp(s - m)
            den = jnp.sum(e, axis=-1, keepdims=True) + jnp.exp(sk - m)
            o = lax.dot_general((e * (1.0 / den)).astype(BF16), vband, _BDOT_NN, preferred_element_type=F32)
            lse = jnp.broadcast_to(m + jnp.log(den), o.shape)
            for i, r in enumerate(rows):
                o_ref[r, :] = jnp.concatenate([_unstack_heads(o[i * cgw + p]) for p in range(cgw)], axis=1)
                lse_ref[r, :] = jnp.concatenate([_unstack_heads(lse[i * cgw + p]) for p in range(cgw)], axis=1)

        _att_streams(d, sb, work)

    sksp = BS((1, qw), lambda g, n: (0, g))
    return _pcall(body, name=name, grid=(G, nb), in_specs=[qsp, csp, psp, csp, psp, sksp], out_specs=[qsp, qsp],
                  out_shape=[SDS((T, Wq), F32)] * 2,
                  compiler_params=_cparams(("parallel", "parallel")))(q, k, k, v, v, sinks)


def _att_bwd_call(q, k, v, sinks, o, lse, do, dlse, d, max_dist, name):
    T, Wq = q.shape
    Wk = k.shape[1]
    nb = T // (d * ATT_BLOCK)
    cgw, sb, shared, qsp, csp, psp, qw, kw = _att_specs(d, Wq, Wk, nb, True)
    G = Wq // qw
    scale = HEAD_DIM ** -0.5

    def body(*refs):
        q_ref, kc_ref, kp_ref, vc_ref, vp_ref, sk_ref, o_ref, lse_ref, do_ref = refs[:9]
        dlse_ref = refs[9] if dlse is not None else None
        dq_ref, dk_ref, dv_ref, dsk_ref, ck, cv = refs[-6:]
        g, n = pl.program_id(0), pl.program_id(1)

        @pl.when(n == 0)
        def _():
            ck[...] = jnp.zeros_like(ck)
            cv[...] = jnp.zeros_like(cv)
            dsk_ref[...] = jnp.zeros_like(dsk_ref)

        def work(rows):
            qs, kband, vband, sk = _att_problem_loads(rows, cgw, shared, g, q_ref, kc_ref, kp_ref, vc_ref, vp_ref, sk_ref)
            dos, lse_c, corr = [], [], []
            for r in rows:
                do_r, o_r, lse_r = do_ref[r, :], o_ref[r, :], lse_ref[r, :]
                dlse_r = dlse_ref[r, :] if dlse_ref is not None else None
                for p in range(cgw):
                    cols = slice(p * LANES, (p + 1) * LANES)
                    dos.append(_stack_heads(do_r[:, cols].astype(BF16)))
                    lse_c.append(_per_head_col(lse_r[:, cols]))
                    delta = _head_sums(do_r[:, cols] * o_r[:, cols])
                    corr.append(-delta if dlse_r is None else _head_sums(dlse_r[:, cols]) - delta)
            dos, lse_c, corr = jnp.stack(dos), jnp.stack(lse_c), jnp.stack(corr)
            s = lax.dot_general(qs, kband, _BDOT_NT, preferred_element_type=F32) * scale
            pr = jnp.exp(jnp.where(_att_masks(n, max_dist), s, NEG) - lse_c)
            dp = lax.dot_general(dos, vband, _BDOT_NT, preferred_element_type=F32)
            ds = (pr * (dp + corr)).astype(BF16)
            dq = lax.dot_general(ds, kband, _BDOT_NN, preferred_element_type=F32) * scale
            dkb = lax.dot_general(ds, qs, _BDOT_TN, preferred_element_type=F32) * scale
            dvb = lax.dot_general(pr.astype(BF16), dos, _BDOT_TN, preferred_element_type=F32)
            dsk = jnp.exp(sk - lse_c) * corr
            lane = lax.broadcasted_iota(jnp.int32, (8, LANES), 1)
            for p in range(cgw):
                tot = [jnp.sum(jnp.stack([dsk[i * cgw + p, h * ATT_BLOCK:(h + 1) * ATT_BLOCK] for i in range(len(rows))]),
                               axis=(0, 1)).reshape(1, 1) for h in range(2)]
                dsk_ref[:, p * LANES:(p + 1) * LANES] += jnp.where(lane == 0, tot[0], jnp.where(lane == HEAD_DIM, tot[1], 0.0))

            def gather_pairs(parts):
                if not shared:
                    return jnp.concatenate(parts, axis=1)
                tot = [parts[4 * h] + parts[4 * h + 1] + parts[4 * h + 2] + parts[4 * h + 3] for h in range(2)]
                tot = [t + pltpu.roll(t, HEAD_DIM, axis=1) for t in tot]
                return jnp.where(_lo_lanes(ATT_BLOCK), tot[0], tot[1])

            for i, r in enumerate(rows):
                mine = range(i * cgw, (i + 1) * cgw)
                dq_ref[r, :] = jnp.concatenate([_unstack_heads(dq[b]) for b in mine], axis=1)
                dk_ref[r, :] = ck[r, :] + gather_pairs([dkb[b, :ATT_BLOCK] for b in mine])
                dv_ref[r, :] = cv[r, :] + gather_pairs([dvb[b, :ATT_BLOCK] for b in mine])
                ck[r, :] = gather_pairs([dkb[b, ATT_BLOCK:] for b in mine])
                cv[r, :] = gather_pairs([dvb[b, ATT_BLOCK:] for b in mine])

        @pl.when(n < nb)
        def _():
            _att_streams(d, sb, work)

        @pl.when(n == nb)
        def _():
            dk_ref[...] = ck[...]
            dv_ref[...] = cv[...]

    sksp = BS((1, qw), lambda g, n: (0, g))
    rows = ATT_BLOCK * d
    osp = BS((rows, kw), lambda g, n: (jnp.maximum(n - 1, 0), 0 if shared else g))
    kshape = SDS((T, Wk), F32)
    dq, dk, dv, dsk = _pcall(
        body, name=name, grid=(G, nb + 1),
        in_specs=[qsp, csp, psp, csp, psp, sksp, qsp, qsp, qsp] + ([qsp] if dlse is not None else []),
        out_specs=[qsp, osp, osp, BS((8, qw), lambda g, n: (0, g))],
        out_shape=[SDS((T, Wq), F32), kshape, kshape, SDS((8, Wq), F32)],
        scratch_shapes=[pltpu.VMEM((rows, kw), F32)] * 2,
        compiler_params=_cparams(("parallel", "arbitrary")))(q, k, k, v, v, sinks, o, lse, do,
                                                              *([dlse] if dlse is not None else []))
    return dq, dk, dv, dsk[0:1]


@functools.partial(jax.custom_vjp, nondiff_argnums=(4, 5, 6, 7))
def band_attention(q, k, v, sinks, d, max_dist, with_lse, name):
    return _band_attention_fwd(q, k, v, sinks, d, max_dist, with_lse, name)[0]


def _band_attention_fwd(q, k, v, sinks, d, max_dist, with_lse, name):
    o, lse = _att_fwd_call(q, k, v, sinks, d, max_dist, name)
    return ((o, lse) if with_lse else o), (q, k, v, sinks, o, lse)


def _band_attention_bwd(d, max_dist, with_lse, name, res, g):
    q, k, v, sinks, o, lse = res
    do, dlse = g if with_lse else (g, None)
    return _att_bwd_call(q, k, v, sinks, o, lse, do, dlse, d, max_dist, name + "_bwd")


band_attention.defvjp(_band_attention_fwd, _band_attention_bwd)


def _merge_weights(ls):
    mx = jnp.maximum(jnp.maximum(ls[0], ls[1]), ls[2])
    es = [jnp.exp(l - mx) for l in ls]
    inv = 1.0 / (es[0] + es[1] + es[2])
    return [e * inv for e in es]


def _merge_fwd_call(os_, ls_, name):
    T, W = os_[0].shape
    tt = _tile(T, 512, 8)

    def body(o1, o2, o3, l1, l2, l3, out_ref):
        w = _merge_weights([l1[...], l2[...], l3[...]])
        out_ref[...] = w[0] * o1[...] + w[1] * o2[...] + w[2] * o3[...]

    row = BS((tt, W), lambda i: (i, 0))
    return _pcall(body, name=name, grid=(T // tt,), in_specs=[row] * 6, out_specs=row,
                  out_shape=SDS((T, W), F32), compiler_params=_cparams(("parallel",)))(*os_, *ls_)


def _merge_bwd_call(os_, ls_, do, name):
    T, W = os_[0].shape
    tt = _tile(T, 512, 8)

    def body(o1, o2, o3, l1, l2, l3, do_ref, d1, d2, d3, e1, e2, e3):
        w = _merge_weights([l1[...], l2[...], l3[...]])
        dov = do_ref[...]
        ts = [dov * o[...] for o in (o1, o2, o3)]
        mean = w[0] * ts[0] + w[1] * ts[1] + w[2] * ts[2]
        for wi, ti, dref, eref in zip(w, ts, (d1, d2, d3), (e1, e2, e3)):
            dref[...] = wi * dov
            eref[...] = wi * (ti - mean)

    row = BS((tt, W), lambda i: (i, 0))
    return _pcall(body, name=name, grid=(T // tt,), in_specs=[row] * 7, out_specs=[row] * 6,
                  out_shape=[SDS((T, W), F32)] * 6, compiler_params=_cparams(("parallel",)))(*os_, *ls_, do)


@functools.partial(jax.custom_vjp, nondiff_argnums=(2,))
def merge3(os_, ls_, name):
    return _merge_fwd_call(os_, ls_, name)


def _merge3_fwd(os_, ls_, name):
    return _merge_fwd_call(os_, ls_, name), (os_, ls_)


def _merge3_bwd(name, res, do):
    os_, ls_ = res
    out = _merge_bwd_call(os_, ls_, do, name + "_bwd")
    return tuple(out[:3]), tuple(out[3:])


merge3.defvjp(_merge3_fwd, _merge3_bwd)


def _xa_probs(qb, kb, scale):
    s = lax.dot_general(qb, kb, (((1,), (1,)), ((), ())), preferred_element_type=F32) * scale
    e = jnp.exp(s - jnp.max(s, axis=-1, keepdims=True))
    return e / jnp.sum(e, axis=-1, keepdims=True)


def _xa_fwd_call(q, kv, name):
    T, W = q.shape
    M = kv.shape[0]
    hd = XA_HEAD_DIM
    tq = _tile(T, 512, 8)
    scale = hd ** -0.5

    def body(q_ref, k_ref, v_ref, o_ref):
        p = _xa_probs(q_ref[...].astype(BF16), k_ref[...].astype(BF16), scale)
        o_ref[...] = jnp.dot(p.astype(BF16), v_ref[...].astype(BF16), preferred_element_type=F32)

    qsp = BS((tq, hd), lambda i, h: (i, h))
    return _pcall(body, name=name, grid=(T // tq, XA_HEADS),
                  in_specs=[qsp, BS((M, hd), lambda i, h: (0, h)), BS((M, hd), lambda i, h: (0, XA_HEADS + h))],
                  out_specs=qsp, out_shape=SDS((T, W), F32),
                  compiler_params=_cparams(("parallel", "parallel")))(q, kv, kv)


def _xa_bwd_call(q, kv, do, name):
    T, W = q.shape
    M = kv.shape[0]
    hd = XA_HEAD_DIM
    tq = _tile(T, 512, 8)
    scale = hd ** -0.5
    dn_nt = (((1,), (1,)), ((), ()))
    dn_tn = (((0,), (0,)), ((), ()))

    def body(q_ref, k_ref, v_ref, do_ref, dq_ref, dk_ref, dv_ref):
        @pl.when(pl.program_id(1) == 0)
        def _():
            dk_ref[...] = jnp.zeros_like(dk_ref)
            dv_ref[...] = jnp.zeros_like(dv_ref)

        qb, kb, vb = q_ref[...].astype(BF16), k_ref[...].astype(BF16), v_ref[...].astype(BF16)
        p = _xa_probs(qb, kb, scale)
        dob = do_ref[...].astype(BF16)
        dp = lax.dot_general(dob, vb, dn_nt, preferred_element_type=F32)
        ds = (p * (dp - jnp.sum(p * dp, axis=-1, keepdims=True))).astype(BF16)
        dq_ref[...] = jnp.dot(ds, kb, preferred_element_type=F32) * scale
        dk_ref[...] += lax.dot_general(ds, qb, dn_tn, preferred_element_type=F32) * scale
        dv_ref[...] += lax.dot_general(p.astype(BF16), dob, dn_tn, preferred_element_type=F32)

    qsp = BS((tq, hd), lambda h, i: (i, h))
    ksp = BS((M, hd), lambda h, i: (0, h))
    return _pcall(body, name=name, grid=(XA_HEADS, T // tq),
                  in_specs=[qsp, ksp, BS((M, hd), lambda h, i: (0, XA_HEADS + h)), qsp],
                  out_specs=[qsp, ksp, ksp], out_shape=[SDS((T, W), F32), SDS((M, W), F32), SDS((M, W), F32)],
                  compiler_params=_cparams(("parallel", "arbitrary")))(q, kv, kv, do)


@functools.partial(jax.custom_vjp, nondiff_argnums=(2,))
def cross_attention(q, kv, name):
    return _xa_fwd_call(q, kv, name)


def _cross_attention_fwd(q, kv, name):
    return _xa_fwd_call(q, kv, name), (q, kv)


def _cross_attention_bwd(name, res, do):
    q, kv = res
    dq, dk, dv = _xa_bwd_call(q, kv, do, name + "_bwd")
    return dq, jnp.concatenate([dk, dv], axis=1)


cross_attention.defvjp(_cross_attention_fwd, _cross_attention_bwd)


def _gate_up_swiglu_call(hn, w1t, name):
    T, K = hn.shape
    F = w1t.shape[0] // 2
    tm, tn = _tile(T, 2048), _tile(F, 256)
    nj = F // tn
    dn = (((1,), (1,)), ((), ()))

    def body(a_ref, wg_ref, wu_ref, g_ref, u_ref, act_ref):
        a = a_ref[...]
        g = lax.dot_general(a, wg_ref[...], dn, preferred_element_type=F32)
        u = lax.dot_general(a, wu_ref[...], dn, preferred_element_type=F32)
        g_ref[...] = g
        u_ref[...] = u
        act_ref[...] = ((g * _sigmoid(g)) * u).astype(BF16)

    tile = BS((tm, tn), lambda i, j: (i, j))
    return _pcall(body, name=name, grid=(T // tm, nj),
                  in_specs=[BS((tm, K), lambda i, j: (i, 0)), BS((tn, K), lambda i, j: (j, 0)),
                            BS((tn, K), lambda i, j: (j + nj, 0))],
                  out_specs=[tile, tile, tile], out_shape=[SDS((T, F), F32), SDS((T, F), F32), SDS((T, F), BF16)],
                  compiler_params=_cparams(("parallel", "parallel")))(hn, w1t, w1t)


def _swiglu_bwd_call(g, u, dact, name):
    T, F = g.shape
    tt = _tile(T, 128, 16)

    def body(g_ref, u_ref, d_ref, o_ref):
        g, d = g_ref[...], d_ref[...]
        sg = _sigmoid(g)
        o_ref[:, :F] = (d * u_ref[...] * (sg * (1.0 + g * (1.0 - sg)))).astype(BF16)
        o_ref[:, F:] = (d * (g * sg)).astype(BF16)

    row = BS((tt, F), lambda i: (i, 0))
    return _pcall(body, name=name, grid=(T // tt,), in_specs=[row, row, row],
                  out_specs=BS((tt, 2 * F), lambda i: (i, 0)), out_shape=SDS((T, 2 * F), BF16),
                  compiler_params=_cparams(("parallel",)))(g, u, dact)


@functools.partial(jax.custom_vjp, nondiff_argnums=(6,))
def ffn_block(h, g, w1b, w1c, w2b, w2c, name):
    return _ffn_fwd(h, g, w1b, w1c, w2b, w2c, name)[0]


def _ffn_fwd(h, g, w1b, w1c, w2b, w2c, name):
    hn = _rms_fwd_call(h, g, name + "_norm", BF16)
    gate, up, act = _gate_up_swiglu_call(hn, w1b, name + "_gu")
    out = _mm(act, w2b, add=h, name=name + "_down", tm=1024, tn=1024, tk=2816)
    return out, (h, g, hn, gate, up, act, w1b, w2b)


def _ffn_bwd(name, res, dout):
    h, g, hn, gate, up, act, w1b, w2b = res
    dact = _mm(dout, w2b, tb=True, name=name + "_down_da", tm=1024, tn=1408, tk=1024)
    dw2 = _mm(act, dout, ta=True, name=name + "_down_dw", tm=1408, tn=1024, tk=1024)
    dgu = _swiglu_bwd_call(gate, up, dact, name + "_swiglu_bwd")
    dw1 = _mm(dgu, hn, ta=True, name=name + "_gu_dw", tm=1408, tn=1024, tk=2048)
    dh, dg = _mm_rms_bwd(dgu, w1b, h, g, dout, name + "_gu_da", wt=True)
    return dh, dg.reshape(g.shape), jnp.zeros_like(w1b), dw1, jnp.zeros_like(w2b), dw2


ffn_block.defvjp(_ffn_fwd, _ffn_bwd)


def _final_call(h, g, target, name):
    T, Dm = h.shape
    tt = _tile(T, 512, 8)

    def body(x_ref, g_ref, t_ref, loss_ref, dx_ref, dg_ref):
        @pl.when(pl.program_id(0) == 0)
        def _():
            loss_ref[...] = jnp.zeros_like(loss_ref)
            dg_ref[...] = jnp.zeros_like(dg_ref)

        xv, gv = x_ref[...], g_ref[...]
        r = lax.rsqrt(jnp.mean(xv * xv, axis=-1, keepdims=True) + NORM_EPS)
        xh = xv * r
        err = xh * gv - t_ref[...]
        loss_ref[...] += 0.5 * jnp.sum(jnp.mean(err * err, axis=-1, keepdims=True), axis=0, keepdims=True)
        dy = err * (1.0 / Dm)
        dyg = dy * gv
        dx_ref[...] = r * (dyg - xh * jnp.mean(dyg * xh, axis=-1, keepdims=True))
        dg_ref[...] += jnp.sum(dy * xh, axis=0, keepdims=True)

    row = BS((tt, Dm), lambda i: (i, 0))
    vec = BS((1, Dm), lambda i: (0, 0))
    return _pcall(body, name=name, grid=(T // tt,), in_specs=[row, vec, row],
                  out_specs=[BS((1, 1), lambda i: (0, 0)), row, vec],
                  out_shape=[SDS((1, 1), F32), SDS((T, Dm), F32), SDS((1, Dm), F32)],
                  compiler_params=_cparams(("arbitrary",)))(h, g.reshape(1, Dm), target)


ADAMW_BLOCK_ELEMS = 64 * 1024


def _adamw_call(parts, w, m, v, name):
    shape = w.shape
    if not isinstance(parts, (list, tuple)):
        parts, shape3 = [parts], (1,) + shape
    else:
        shape3 = shape
    n_lead = shape3[0]
    r, N = shape3[-2], shape3[-1]
    Ld = math.prod(shape3[1:-2])
    w, m, v = (t.reshape(n_lead * Ld, r, N) for t in (w, m, v))
    tr = _tile(r, max(8, ADAMW_BLOCK_ELEMS // N), 8)
    c1 = 1.0 - ADAM_B1 ** ADAM_STEP
    c2 = 1.0 - ADAM_B2 ** ADAM_STEP
    outs = None
    for lead, p in enumerate(parts):
        def body(p_ref, w_ref, m_ref, v_ref, *rest):
            g_ref, d_ref, nm_ref, nv_ref = rest[-4:]
            g = p_ref[0]
            for j in range(1, N_DEV):
                g = g + p_ref[j]
            nm = ADAM_B1 * m_ref[...] + (1.0 - ADAM_B1) * g
            nv = ADAM_B2 * v_ref[...] + (1.0 - ADAM_B2) * (g * g)
            g_ref[...] = g
            nm_ref[...] = nm
            nv_ref[...] = nv
            d_ref[...] = -ADAM_LR * ((nm / c1) / (jnp.sqrt(nv / c2) + ADAM_EPS) + ADAM_WD * w_ref[...])

        base = lead * Ld
        row = BS((1, tr, N), lambda l, i, base=base: (base + l, i, 0))
        prev = [] if outs is None else list(outs)
        outs = _pcall(body, name=f"{name}_{lead}", grid=(Ld, r // tr),
                      in_specs=[BS((N_DEV, 1, tr, N), lambda l, i: (0, l, i, 0)), row, row, row]
                      + [BS(memory_space=pl.ANY)] * len(prev),
                      out_specs=[row] * 4, out_shape=[SDS((n_lead * Ld, r, N), F32)] * 4,
                      input_output_aliases={4 + j: j for j in range(len(prev))},
                      compiler_params=_cparams(("parallel", "parallel")))(p.reshape(N_DEV, Ld, r, N), w, m, v, *prev)
    return [t.reshape(shape) for t in outs]


def _place():
    return lax.axis_index("x"), lax.axis_index("y"), lax.axis_index("c")


def _all_gather(xs, name):
    n = len(xs)
    pairs = [(i, l) for i, x in enumerate(xs) for l in range(x.shape[0])]

    def body(*refs):
        x_refs, o_refs = refs[:n], refs[n:2 * n]
        send_sems, recv_sems, local_sems = refs[2 * n:]
        x_, y_, c_ = _place()
        me, sibling = (x_, y_, c_), (x_, y_, 1 - c_)
        chips = [(1 - x_, y_), (x_, 1 - y_), (1 - x_, 1 - y_)]

        def copy(e, k, block, to, from_input=False):
            i, l = pairs[e]
            px, py, pc = block
            dst = o_refs[i].at[l, 4 * px + 2 * py + pc]
            return pltpu.make_async_remote_copy(
                src_ref=x_refs[i].at[l] if from_input else dst, dst_ref=dst,
                send_sem=send_sems.at[7 * e + k], recv_sem=recv_sems.at[7 * e + k],
                device_id=to, device_id_type=pl.DeviceIdType.MESH)

        every = range(len(pairs))
        mine = [pltpu.make_async_copy(x_refs[i].at[l], o_refs[i].at[l, 4 * x_ + 2 * y_ + c_], local_sems.at[e])
                for e, (i, l) in enumerate(pairs)]
        for cp in mine:
            cp.start()
        first = [copy(e, 0, me, sibling, True) for e in every]
        first += [copy(e, 1 + j, me, (*chip, c_), True) for j, chip in enumerate(chips) for e in every]
        for cp in first:
            cp.start()
        passed = []
        for j, chip in enumerate(chips):
            for e in every:
                copy(e, 1 + j, (*chip, c_), me).wait_recv()
            for e in every:
                cp = copy(e, 4 + j, (*chip, c_), sibling)
                cp.start()
                passed.append(cp)
        for e in every:
            copy(e, 0, sibling, me).wait_recv()
        for j, chip in enumerate(chips):
            for e in every:
                copy(e, 4 + j, (*chip, 1 - c_), me).wait_recv()
        for cp in first + passed:
            cp.wait_send()
        for cp in mine:
            cp.wait()

    any_spec = BS(memory_space=pl.ANY)
    return _pcall(body, name=name, in_specs=[any_spec] * n, out_specs=[any_spec] * n,
                  out_shape=[SDS((x.shape[0], N_DEV) + x.shape[1:], x.dtype) for x in xs],
                  scratch_shapes=[pltpu.SemaphoreType.DMA((7 * len(pairs),)), pltpu.SemaphoreType.DMA((7 * len(pairs),)),
                                  pltpu.SemaphoreType.DMA((len(pairs),))],
                  compiler_params=pltpu.CompilerParams(has_side_effects=True))(*xs)


def _peer_of(k, place):
    x_, y_, c_ = place
    fx, fy, fc = (k >> 2) & 1, (k >> 1) & 1, k & 1
    return fx + x_ - 2 * fx * x_, fy + y_ - 2 * fy * y_, fc + c_ - 2 * fc * c_


def _split_copy(src_ref, land_ref, send_sems, recv_sems, e, k, place, scatter):
    x_, y_, c_ = place
    px, py, pc = _peer_of(k, place)
    return pltpu.make_async_remote_copy(
        src_ref=src_ref.at[4 * px + 2 * py + pc] if scatter else src_ref, dst_ref=land_ref.at[4 * x_ + 2 * y_ + c_],
        send_sem=send_sems.at[7 * e + k - 1], recv_sem=recv_sems.at[7 * e + k - 1],
        device_id=(px, py, pc), device_id_type=pl.DeviceIdType.MESH)


def _own_copy(src_ref, land_ref, sems, slot, place, scatter):
    x_, y_, c_ = place
    me = 4 * x_ + 2 * y_ + c_
    return pltpu.make_async_copy(src_ref.at[me] if scatter else src_ref, land_ref.at[me], sems.at[slot])


_HBM_SPEC = BS(memory_space=pltpu.HBM)
_SEM_SPEC = BS(memory_space=pltpu.SEMAPHORE)
_EFFECT = pltpu.SideEffectType.DATAFLOW_SIDE_EFFECTING


def _copies_start(srcs, scatter, name, thru=None):
    n = len(srcs)
    lands = [lax.empty(s.shape if scatter else (N_DEV,) + s.shape, s.dtype) for s in srcs]
    passed = srcs + lands + list(thru or ())

    def body(*refs):
        src_refs, land_refs = refs[:n], refs[n:2 * n]
        send_sems, recv_sems = refs[len(passed)], refs[len(passed) + 1]
        token = refs[-1]
        place = _place()
        for e in range(n):
            for k in range(1, N_DEV):
                _split_copy(src_refs[e], land_refs[e], send_sems, recv_sems, e, k, place, scatter).start()
            _own_copy(src_refs[e], land_refs[e], send_sems, 7 * n + e, place, scatter).start()
        token[...] = jnp.zeros_like(token)

    hbm = lambda t: pltpu.with_memory_space_constraint(t, pltpu.HBM)
    out = _pcall(body, name=name,
                 out_shape=(pltpu.SemaphoreType.DMA((8 * n,)), pltpu.SemaphoreType.DMA((7 * n,)),
                            *[pltpu.HBM(t.shape, t.dtype) for t in passed], SDS((8, LANES), F32)),
                 in_specs=[_HBM_SPEC] * len(passed),
                 out_specs=(_SEM_SPEC, _SEM_SPEC, *[_HBM_SPEC] * len(passed), BS(memory_space=pltpu.VMEM)),
                 input_output_aliases={i: 2 + i for i in range(len(passed))},
                 compiler_params=pltpu.CompilerParams(has_side_effects=_EFFECT))(*[hbm(t) for t in passed])
    return out[0], out[1], list(out[2:2 + n]), list(out[2 + n:2 + 2 * n]), list(out[2 + 2 * n:-1])


def _copies_wait(started, which, scatter, after, name):
    send_sems, recv_sems, srcs, lands, _ = started
    n, n_started = len(which), len(srcs)
    after = list(after) if isinstance(after, (list, tuple)) else [after]

    def body(*refs):
        src_refs, land_refs = refs[:n], refs[n:2 * n]
        send_s, recv_s = refs[2 * n], refs[2 * n + 1]
        place = _place()
        for j, e in enumerate(which):
            for k in range(1, N_DEV):
                cp = _split_copy(src_refs[j], land_refs[j], send_s, recv_s, e, k, place, scatter)
                cp.wait_send()
                cp.wait_recv()
            _own_copy(src_refs[j], land_refs[j], send_s, 7 * n_started + e, place, scatter).wait()

    args = [srcs[e] for e in which] + [lands[e] for e in which]
    out = _pcall(body, name=name, out_shape=tuple(pltpu.HBM(t.shape, t.dtype) for t in args),
                 in_specs=[_HBM_SPEC] * (2 * n) + [_SEM_SPEC, _SEM_SPEC] + [BS(memory_space=pl.ANY)] * len(after),
                 out_specs=tuple([_HBM_SPEC] * (2 * n)), input_output_aliases={i: i for i in range(2 * n)},
                 compiler_params=pltpu.CompilerParams(has_side_effects=_EFFECT))(*args, send_sems, recv_sems, *after)
    return list(out[:n]), list(out[n:])


def _pad_flat(t, quantum=PACK_QUANTUM):
    f = t.reshape(-1)
    pad = (-f.shape[0]) % quantum
    return jnp.pad(f, (0, pad)) if pad else f


def _pack(arrs, dtype):
    return jnp.concatenate([_pad_flat(a.astype(dtype)) for a in arrs]).reshape(-1, LANES)


def _unpack(buf, shapes, lead=()):
    flat = buf.reshape(lead + (-1,))
    out, off = [], 0
    for s in shapes:
        n = math.prod(s)
        out.append(flat[..., off:off + n].reshape(lead + tuple(s)))
        off += n + (-n) % PACK_QUANTUM
    return out


def _full_from_gathered(g, axis):
    t = jnp.moveaxis(g, 0, axis)
    s = t.shape
    return t.reshape(s[:axis] + (s[axis] * s[axis + 1],) + s[axis + 2:])


def _parts_from_full(t, axis):
    s = t.shape
    t = t.reshape(s[:axis] + (N_DEV, s[axis] // N_DEV) + s[axis + 1:])
    return jnp.moveaxis(t, axis, 0)


def _head_rows(t):
    return jnp.repeat(t, HEAD_DIM).reshape(1, -1)


def _dilated_attention(q, k, v, name):
    no_sink = jnp.full((1, q.shape[1]), NEG, F32)
    outs, lses = zip(*[band_attention(q, k, v, no_sink, d, ATT_BLOCK, True, f"{name}_d{d}") for d in DILATIONS])
    return merge3(tuple(outs), tuple(lses), name + "_merge")


STAGES = (
    ("proj0", ('mix_norm',), (('ab_w_in', 0),)),
    ("mixer0", ('lru_conv_w', 'lru_conv_b', 'lru_ba', 'lru_bx', 'lru_lambda'),
     (('lru_wa', 0), ('lru_wx', 0), ('ab_w_out', 0))),
    ("xa0", ('xa_norm', 'xa_mem_norm'), (('xa_wq', 0), ('xa_wkv', 0), ('xa_wo', 0))),
    ("ffn0", ('ffn_norm',), (('ffn_w_gate_up', 0), ('ffn_w_down', 0))),
    ("mixer1", ('mix_norm', 'c_b_qkv', 'c_sinks', 'c_b_out'), (('c_w_qkv', 0), ('c_w_out', 0))),
    ("xa1", ('xa_norm', 'xa_mem_norm'), (('xa_wq', 1), ('xa_wkv', 1), ('xa_wo', 1))),
    ("ffn1", ('ffn_norm',), (('ffn_w_gate_up', 1), ('ffn_w_down', 1))),
)


def _stage_fn(stage, Wb, tabs, mem):
    layer = int(stage[-1])
    L = f"l{layer}"

    def run(S, Cw, h):
        def lin(a, key, bias, add, name, rows=None):
            wb, wc = Wb[key], Cw[key]
            if rows is not None:
                wb, wc = wb[rows], wc[rows]
            return linear(a, wb, wc, bias, add, name)

        def norm_lin(a, gain, key, bias, name):
            return norm_linear(a, gain, Wb[key], Cw[key], bias, key[0] in COLUMN_CUT, name)

        if stage == "mixer0":
            h, x_br, y_br, q, k, v = h
            C = S['lru_conv_w'].shape[-1]
            xc = conv4(x_br, S['lru_conv_w'][0], S['lru_conv_b'][0], L + "_conv")
            ga, gx = lru_gates(xc, Wb['lru_wa', 0], Cw['lru_wa', 0], S['lru_ba'][0],
                               Wb['lru_wx', 0], Cw['lru_wx', 0], S['lru_bx'][0], L + "_gates")
            rec = lru_scan(xc, ga, gx, y_br, S['lru_lambda'][0], L + "_scan")
            att = _dilated_attention(q, k, v, L + "_att")
            h = lin(att, ('ab_w_out', 0), None, h, L + "_w_out_att", slice(C, None))
            return lin(rec, ('ab_w_out', 0), None, h, L + "_w_out_rec", slice(0, C))
        if stage == "mixer1":
            qw = C_HEADS * HEAD_DIM
            kw = C_KV_HEADS * HEAD_DIM
            q, k, v, h = norm_linear_pieces(h, S['mix_norm'][1], Wb['c_w_qkv', 0], Cw['c_w_qkv', 0], S['c_b_qkv'][0], tabs,
                                            ((qw, True), (kw, True), (kw, False)), L + "_w_qkv")
            o = band_attention(q, k, v, _head_rows(S['c_sinks'][0]), 1, ATT_BLOCK - 1, False, L + "_att")
            return lin(o, ('c_w_out', 0), S['c_b_out'][0], h, L + "_w_out")
        if stage.startswith("xa"):
            xq, h = norm_lin(h, S['xa_norm'][layer], ('xa_wq', layer), None, L + "_xa_wq")
            xkv, _ = norm_lin(mem, S['xa_mem_norm'][layer], ('xa_wkv', layer), None, L + "_xa_wkv")
            return lin(cross_attention(xq, xkv, L + "_xa"), ('xa_wo', layer), None, h, L + "_xa_wo")
        gu, down = ('ffn_w_gate_up', layer), ('ffn_w_down', layer)
        return ffn_block(h, S['ffn_norm'][layer], Wb[gu], Cw[gu], Wb[down], Cw[down], L + "_ffn")

    return run


def kernel(x, mem, mix_norm, ab_w_in, lru_conv_w, lru_conv_b, lru_wa, lru_ba, lru_wx, lru_bx, lru_lambda, ab_w_out, c_w_qkv, c_b_qkv, c_sinks, c_w_out, c_b_out, xa_norm, xa_mem_norm, xa_wq, xa_wkv, xa_wo, ffn_norm, ffn_w_gate_up, ffn_w_down, final_norm, loss_target, m_mix_norm, m_ab_w_in, m_lru_conv_w, m_lru_conv_b, m_lru_wa, m_lru_ba, m_lru_wx, m_lru_bx, m_lru_lambda, m_ab_w_out, m_c_w_qkv, m_c_b_qkv, m_c_sinks, m_c_w_out, m_c_b_out, m_xa_norm, m_xa_mem_norm, m_xa_wq, m_xa_wkv, m_xa_wo, m_ffn_norm, m_ffn_w_gate_up, m_ffn_w_down, m_final_norm, v_mix_norm, v_ab_w_in, v_lru_conv_w, v_lru_conv_b, v_lru_wa, v_lru_ba, v_lru_wx, v_lru_bx, v_lru_lambda, v_ab_w_out, v_c_w_qkv, v_c_b_qkv, v_c_sinks, v_c_w_out, v_c_b_out, v_xa_norm, v_xa_mem_norm, v_xa_wq, v_xa_wkv, v_xa_wo, v_ffn_norm, v_ffn_w_gate_up, v_ffn_w_down, v_final_norm):
    w_loc = dict(zip(WEIGHT_NAMES, (mix_norm, ab_w_in, lru_conv_w, lru_conv_b, lru_wa, lru_ba, lru_wx, lru_bx, lru_lambda, ab_w_out, c_w_qkv, c_b_qkv, c_sinks, c_w_out, c_b_out, xa_norm, xa_mem_norm, xa_wq, xa_wkv, xa_wo, ffn_norm, ffn_w_gate_up, ffn_w_down, final_norm)))
    m_loc = dict(zip(WEIGHT_NAMES, (m_mix_norm, m_ab_w_in, m_lru_conv_w, m_lru_conv_b, m_lru_wa, m_lru_ba, m_lru_wx, m_lru_bx, m_lru_lambda, m_ab_w_out, m_c_w_qkv, m_c_b_qkv, m_c_sinks, m_c_w_out, m_c_b_out, m_xa_norm, m_xa_mem_norm, m_xa_wq, m_xa_wkv, m_xa_wo, m_ffn_norm, m_ffn_w_gate_up, m_ffn_w_down, m_final_norm)))
    v_loc = dict(zip(WEIGHT_NAMES, (v_mix_norm, v_ab_w_in, v_lru_conv_w, v_lru_conv_b, v_lru_wa, v_lru_ba, v_lru_wx, v_lru_bx, v_lru_lambda, v_ab_w_out, v_c_w_qkv, v_c_b_qkv, v_c_sinks, v_c_w_out, v_c_b_out, v_xa_norm, v_xa_mem_norm, v_xa_wq, v_xa_wkv, v_xa_wo, v_ffn_norm, v_ffn_w_gate_up, v_ffn_w_down, v_final_norm)))

    first_keys = list(STAGES[0][2])
    keys = [key for _, _, stage_keys in STAGES[1:] for key in stage_keys]
    shards = [_shard_view(n, w_loc[n])[l].astype(BF16) for n, l in keys]
    first_g = _all_gather([_pack([w_loc[n] for n in SMALL], F32)[None]]
                          + [_shard_view(n, w_loc[n])[l].astype(BF16)[None] for n, l in first_keys], "gather_first")
    gather = _copies_start(shards, False, "gather_start", thru=[first_g[0]])
    small_g = gather[4][0][0]
    Wb = {key: _full_from_gathered(g[0], _layer_shard_axis(key[0])) for key, g in zip(first_keys, first_g[1:])}
    S = {n: w_loc[n] for n in REPLICATED}
    for n, t in zip(SMALL, _unpack(small_g, [w_loc[n].shape for n in SMALL], lead=(N_DEV,))):
        S[n] = _full_from_gathered(t, SHARD_AXIS[n])

    tabs = _rope_tables(x.shape[1])
    w_in = Wb[first_keys[0]]
    hn0 = _rms_fwd_call(x[0], S['mix_norm'][0], "l0_w_in_norm", BF16)
    lru_w, att_w = N_DEV * lru_conv_w.shape[-1], B_HEADS * HEAD_DIM
    pieces = (("x", lru_w, False), ("y", lru_w, False), ("q", att_w, True), ("k", att_w, True), ("v", att_w, False))
    h, row = [x[0]], 0
    for piece, width, rotated in pieces:
        h.append(_mm(hn0, w_in[row:row + width], tb=True, rope=tabs if rotated else None, name="l0_w_in_" + piece,
                     tm=2048, tn=1024, tk=1024))
        row += width
    h = tuple(h)
    vjps = []
    for stage, small_names, stage_keys in STAGES[1:]:
        which = [keys.index(key) for key in stage_keys]
        _, lands = _copies_wait(gather, which, False, jax.tree.leaves(h)[-1], "gather_wait_" + stage)
        for e, land in zip(which, lands):
            Wb[keys[e]] = _full_from_gathered(land, _layer_shard_axis(keys[e][0]))
        carriers = {key: jnp.zeros(Wb[key].shape, F32) for key in stage_keys}
        h, vjp_fn = jax.vjp(_stage_fn(stage, Wb, tabs, mem[0]), {n: S[n] for n in small_names}, carriers, h)
        vjps.append(vjp_fn)
    loss_part, dh, dg_final = _final_call(h, S['final_norm'], loss_target[0], "final_loss")

    grads = {'final_norm': dg_final.reshape(final_norm.shape)}
    exchanges, send_keys, send_parts = [], [], []

    def start_exchange(stage, dh):
        leaves, tree = jax.tree.flatten(dh)
        started = _copies_start(list(send_parts), True, "grad_start_" + stage, thru=leaves)
        exchanges.append((stage, started, list(send_keys)))
        send_keys.clear()
        send_parts.clear()
        return jax.tree.unflatten(tree, started[4])

    for (stage, small_names, stage_keys), vjp_fn in zip(reversed(STAGES[1:]), reversed(vjps)):
        g_small, g_big, dh = vjp_fn(dh)
        for n in small_names:
            grads[n] = grads[n] + g_small[n] if n in grads else g_small[n]
        send_keys += list(stage_keys)
        send_parts += [_parts_from_full(g_big[key], _layer_shard_axis(key[0])) for key in stage_keys]
        if stage == "xa1":
            continue
        if stage == "mixer0":
            small_parts = [_parts_from_full(grads[n], SHARD_AXIS[n]) for n in SMALL]
            send_keys.append("small")
            send_parts.append(jnp.stack([_pack([p[j] for p in small_parts], F32) for j in range(N_DEV)]))
        dh = start_exchange(stage, dh)
    d_res, d_proj = dh[0], _join_pieces_call(dh[1:], [rotated for _, _, rotated in pieces], tabs, "l0_w_in_dproj")
    send_keys.append(first_keys[0])
    send_parts.append(_parts_from_full(_mm(d_proj, hn0, ta=True, name="l0_w_in_dw", tm=1408, tn=1024, tk=2048),
                                       _layer_shard_axis(first_keys[0][0])))
    d_res, d_proj = start_exchange("proj0", (d_res, d_proj))
    dx, dg0 = _mm_rms_bwd(d_proj, w_in, x[0], S['mix_norm'][0], d_res, "l0_w_in_da", wt=True)
    grads['mix_norm'] = grads['mix_norm'] + jnp.concatenate([dg0, jnp.zeros_like(dg0)], axis=0)
    rep_names = REPLICATED + ["loss"]
    grads["loss"] = loss_part
    zero = jnp.zeros((1, 1), F32)
    for d in (w_loc, m_loc, v_loc):
        d["loss"] = zero
    rep_started = _copies_start([_pack([grads[n] for n in rep_names], F32)], False, "rep_grads_start")

    parts, out = {}, {}

    def end_exchange(stage, started, ex_keys, after):
        _, lands = _copies_wait(started, list(range(len(ex_keys))), True, after, "grad_wait_" + stage)
        parts.update(zip(ex_keys, lands))

    def adamw(p, names, call_name):
        if len(names) == 1:
            n = names[0]
            res = _adamw_call(p, _shard_view(n, w_loc[n]), _shard_view(n, m_loc[n]), _shard_view(n, v_loc[n]), call_name)
            for kind, t in zip(("grad", "delta", "new_m", "new_v"), res):
                out[kind, n] = _shard_view(n, t)
        else:
            res = _adamw_call(p, *[_pack([d[n] for n in names], F32) for d in (w_loc, m_loc, v_loc)], call_name)
            for kind, buf in zip(("grad", "delta", "new_m", "new_v"), res):
                for n, t in zip(names, _unpack(buf, [w_loc[n].shape for n in names])):
                    out[kind, n] = t

    for ex in exchanges[:-1]:
        end_exchange(*ex, dx)
    last_names = {key[0] for key in exchanges[-1][2]}
    for n in BIG:
        if n not in last_names:
            adamw([parts[n, l] for l in range(w_loc[n].shape[0])], [n], "adamw_" + n)
    adamw(parts["small"], SMALL, "adamw_small")
    end_exchange(*exchanges[-1], [out["new_v", n] for n in BIG if n not in last_names])
    for n in BIG:
        if n in last_names:
            adamw([parts[n, l] for l in range(w_loc[n].shape[0])], [n], "adamw_" + n)
    _, rep_land = _copies_wait(rep_started, [0], False, out["new_v", "ab_w_in"], "rep_grads_wait")
    adamw(rep_land[0], rep_names, "adamw_replicated")
    loss = out["grad", "loss"][0, 0]

    return (loss, dx[None], *[out[kind, n] for kind in ("grad", "delta", "new_m", "new_v") for n in WEIGHT_NAMES])
```

```python
import functools
import math

import jax
import jax.numpy as jnp
from jax import lax
from jax.experimental import pallas as pl
from jax.experimental.pallas import tpu as pltpu

F32 = jnp.float32
BF16 = jnp.bfloat16
SDS = jax.ShapeDtypeStruct
BS = pl.BlockSpec

N_DEV = 8
NORM_EPS = 1e-6
ROPE_THETA = 500000.0
HEAD_DIM = 64
ROT_DIM = 16
ATT_BLOCK = 128
LRU_C = 8.0
LRU_HEADS = 4
DILATIONS = (1, 4, 16)
B_HEADS = 8
C_HEADS = 16
C_KV_HEADS = 2
XA_HEADS = 4
XA_HEAD_DIM = 128
NEG = -1e30
ADAM_LR, ADAM_B1, ADAM_B2, ADAM_EPS, ADAM_WD, ADAM_STEP = 0.001, 0.9, 0.999, 1e-08, 0.01, 10
LANES = 128
VMEM_LIMIT = 48 * 1024 * 1024

WEIGHT_NAMES = ['mix_norm', 'ab_w_in', 'lru_conv_w', 'lru_conv_b', 'lru_wa', 'lru_ba', 'lru_wx', 'lru_bx',
                'lru_lambda', 'ab_w_out', 'c_w_qkv', 'c_b_qkv', 'c_sinks', 'c_w_out', 'c_b_out', 'xa_norm',
                'xa_mem_norm', 'xa_wq', 'xa_wkv', 'xa_wo', 'ffn_norm', 'ffn_w_gate_up', 'ffn_w_down', 'final_norm']
SHARD_AXIS = {'ab_w_in': 2, 'lru_conv_w': 2, 'lru_wa': 2, 'lru_ba': 2, 'lru_wx': 2, 'lru_bx': 2, 'ab_w_out': 1,
              'c_w_qkv': 2, 'c_b_qkv': 1, 'c_w_out': 1, 'c_b_out': 1, 'xa_wq': 1, 'xa_wkv': 1, 'xa_wo': 2,
              'ffn_w_gate_up': 2, 'ffn_w_down': 1}
BIG = ['ab_w_in', 'lru_wa', 'lru_wx', 'ab_w_out', 'c_w_qkv', 'c_w_out', 'xa_wq', 'xa_wkv', 'xa_wo',
       'ffn_w_gate_up', 'ffn_w_down']
SMALL = ['lru_conv_w', 'lru_ba', 'lru_bx', 'c_b_qkv', 'c_b_out']
REPLICATED = [n for n in WEIGHT_NAMES if n not in SHARD_AXIS]
COLUMN_CUT = ('ab_w_in', 'c_w_qkv', 'ffn_w_gate_up')
PACK_QUANTUM = 2048


def _shard_view(name, t):
    return jnp.swapaxes(t, -1, -2) if name in COLUMN_CUT else t


def _layer_shard_axis(name):
    return 0 if name in COLUMN_CUT else SHARD_AXIS[name] - 1


def _pcall(body, **kw):
    return pl.pallas_call(body, **kw)


def _cparams(sem=None):
    return pltpu.CompilerParams(dimension_semantics=sem, vmem_limit_bytes=VMEM_LIMIT)


def _tile(n, target, mult=LANES):
    if n <= target:
        return n
    t = (target // mult) * mult
    while t >= mult:
        if n % t == 0:
            return t
        t -= mult
    return n


def _sigmoid(x):
    return 1.0 / (1.0 + jnp.exp(-x))


def _expm1(x):
    small = x * (1.0 + x * (0.5 + x * (1.0 / 6.0 + x * (1.0 / 24.0))))
    return jnp.where(jnp.abs(x) < 0.03, small, jnp.exp(x) - 1.0)


_GELU_C = math.sqrt(2.0 / math.pi)


def _gelu_parts(y):
    y2 = y * y
    th = jnp.tanh(_GELU_C * (y + 0.044715 * y * y2))
    g = 0.5 * y * (1.0 + th)
    dg = 0.5 * (1.0 + th) + 0.5 * y * (1.0 - th * th) * _GELU_C * (1.0 + 3.0 * 0.044715 * y2)
    return g, dg


def _rotate(xv, tab_refs, inverse=False):
    W = xv.shape[1]
    half = ROT_DIM // 2
    c, sa, sb = (jnp.tile(t[...], (1, W // LANES)) for t in tab_refs)
    if not inverse:
        return xv * c + pltpu.roll(xv, half, axis=1) * sa + pltpu.roll(xv, W - half, axis=1) * sb
    return xv * c + pltpu.roll(xv * sa, W - half, axis=1) + pltpu.roll(xv * sb, half, axis=1)


def _mm(a, b, *, ta=False, tb=False, bias=None, add=None, rope=None, name, tm=512, tn=512, tk=2048):
    M, K = (a.shape[1], a.shape[0]) if ta else a.shape
    N = b.shape[0] if tb else b.shape[1]
    tm, tn, tk = _tile(M, tm), _tile(N, tn), _tile(K, tk)
    nk = K // tk
    dn = (((0 if ta else 1,), (1 if tb else 0,)), ((), ()))

    def body(*refs):
        a_ref, b_ref = refs[0], refs[1]
        pos = 2
        bias_ref = add_ref = None
        if bias is not None:
            bias_ref = refs[pos]
            pos += 1
        if add is not None:
            add_ref = refs[pos]
            pos += 1
        tab_refs = refs[pos:pos + 3] if rope is not None else None
        pos += 3 if rope is not None else 0
        o_ref = refs[pos]
        part = lax.dot_general(a_ref[...].astype(BF16), b_ref[...].astype(BF16), dn, preferred_element_type=F32)

        def finish(r):
            if bias_ref is not None:
                r = r + bias_ref[...]
            if add_ref is not None:
                r = r + add_ref[...]
            o_ref[...] = r if tab_refs is None else _rotate(r, tab_refs)

        if nk == 1:
            finish(part)
            return
        acc_ref = refs[pos + 1]
        k = pl.program_id(2)

        @pl.when(k == 0)
        def _():
            acc_ref[...] = part

        @pl.when((k > 0) & (k < nk - 1))
        def _():
            acc_ref[...] += part

        @pl.when(k == nk - 1)
        def _():
            finish(acc_ref[...] + part)

    in_specs = [BS((tk, tm), lambda i, j, k: (k, i)) if ta else BS((tm, tk), lambda i, j, k: (i, k)),
                BS((tn, tk), lambda i, j, k: (j, k)) if tb else BS((tk, tn), lambda i, j, k: (k, j))]
    args = [a, b]
    if bias is not None:
        in_specs.append(BS((1, tn), lambda i, j, k: (0, j)))
        args.append(bias.reshape(1, N))
    if add is not None:
        in_specs.append(BS((tm, tn), lambda i, j, k: (i, j)))
        args.append(add)
    if rope is not None:
        in_specs += [BS((tm, LANES), lambda i, j, k: (i, 0))] * 3
        args += list(rope)
    return _pcall(body, name=name, grid=(M // tm, N // tn, nk), in_specs=in_specs,
                  out_specs=BS((tm, tn), lambda i, j, k: (i, j)), out_shape=SDS((M, N), F32),
                  scratch_shapes=[pltpu.VMEM((tm, tn), F32)] if nk > 1 else [],
                  compiler_params=_cparams(("parallel", "parallel", "arbitrary")))(*args)


def _mm_rms_bwd(dy, wb, x, g, add, name, wt=False):
    T, Kc = dy.shape
    Dm = wb.shape[1] if wt else wb.shape[0]
    tm, tk = _tile(T, 1024), _tile(Kc, 1408)
    nk = Kc // tk
    dn = (((1,), (0 if wt else 1,)), ((), ()))

    def body(*refs):
        dy_ref, w_ref, x_ref, g_ref = refs[:4]
        dx_ref, dg_ref, acc_ref = refs[-3:]
        i, k = pl.program_id(0), pl.program_id(1)
        part = lax.dot_general(dy_ref[...].astype(BF16), w_ref[...], dn, preferred_element_type=F32)

        @pl.when(k == 0)
        def _():
            acc_ref[...] = part

        @pl.when((k > 0) & (k < nk - 1))
        def _():
            acc_ref[...] += part

        @pl.when((i == 0) & (k == 0))
        def _():
            dg_ref[...] = jnp.zeros_like(dg_ref)

        @pl.when(k == nk - 1)
        def _():
            d = part if nk == 1 else acc_ref[...] + part
            xv = x_ref[...]
            r = lax.rsqrt(jnp.mean(xv * xv, axis=-1, keepdims=True) + NORM_EPS)
            xh = xv * r
            dyg = d * g_ref[...]
            dx = r * (dyg - xh * jnp.mean(dyg * xh, axis=-1, keepdims=True))
            dx_ref[...] = dx if add is None else dx + refs[4][...]
            dg_ref[...] += jnp.sum(d * xh, axis=0, keepdims=True)

    row = BS((tm, Dm), lambda i, k: (i, 0))
    vec = BS((1, Dm), lambda i, k: (0, 0))
    extra = [] if add is None else [add]
    return _pcall(body, name=name, grid=(T // tm, nk),
                  in_specs=[BS((tm, tk), lambda i, k: (i, k)),
                            BS((tk, Dm), lambda i, k: (k, 0)) if wt else BS((Dm, tk), lambda i, k: (0, k)), row, vec]
                  + [row] * len(extra),
                  out_specs=[row, vec], out_shape=[SDS((T, Dm), F32), SDS((1, Dm), F32)],
                  scratch_shapes=[pltpu.VMEM((tm, Dm), F32)],
                  compiler_params=_cparams(("arbitrary", "arbitrary")))(dy, wb, x, g.reshape(1, Dm), *extra)


def _colsum(x, name):
    T, N = x.shape
    tt = _tile(T, 512, 8)

    def body(x_ref, o_ref):
        @pl.when(pl.program_id(0) == 0)
        def _():
            o_ref[...] = jnp.zeros_like(o_ref)

        o_ref[...] += jnp.sum(x_ref[...], axis=0, keepdims=True)

    return _pcall(body, name=name, grid=(T // tt,), in_specs=[BS((tt, N), lambda i: (i, 0))],
                  out_specs=BS((1, N), lambda i: (0, 0)), out_shape=SDS((1, N), F32),
                  compiler_params=_cparams(("arbitrary",)))(x)


@functools.partial(jax.custom_vjp, nondiff_argnums=(5,))
def linear(a, wb, wc, bias, add, name):
    return _linear_fwd(a, wb, wc, bias, add, name)[0]


def _linear_fwd(a, wb, wc, bias, add, name):
    out = _mm(a, wb, bias=bias, add=add, name=name, tm=1024, tn=1024, tk=1024)
    return out, (a, wb, bias is not None, add is not None)


def _linear_bwd(name, res, g):
    a, wb, has_bias, has_add = res
    da = _mm(g, wb, tb=True, name=name + "_da", tm=1024, tn=1024, tk=1024)
    dw = _mm(a, g, ta=True, name=name + "_dw", tm=1024, tn=1024, tk=1024)
    dbias = _colsum(g, name + "_db").reshape(-1) if has_bias else None
    return da, jnp.zeros_like(wb), dw, dbias, (g if has_add else None)


linear.defvjp(_linear_fwd, _linear_bwd)


def _rms_fwd_call(x, g, name, out_dtype=F32):
    T, Dm = x.shape
    tt = _tile(T, 512, 16)

    def body(x_ref, g_ref, o_ref):
        xv = x_ref[...]
        r = lax.rsqrt(jnp.mean(xv * xv, axis=-1, keepdims=True) + NORM_EPS)
        o_ref[...] = ((xv * r) * g_ref[...]).astype(out_dtype)

    return _pcall(body, name=name, grid=(T // tt,),
                  in_specs=[BS((tt, Dm), lambda i: (i, 0)), BS((1, Dm), lambda i: (0, 0))],
                  out_specs=BS((tt, Dm), lambda i: (i, 0)), out_shape=SDS((T, Dm), out_dtype),
                  compiler_params=_cparams(("parallel",)))(x, g.reshape(1, Dm))


@functools.partial(jax.custom_vjp, nondiff_argnums=(5, 6))
def norm_linear(x, g, wb, wc, bias, wt, name):
    return _norm_linear_fwd(x, g, wb, wc, bias, wt, name)[0]


def _norm_linear_fwd(x, g, wb, wc, bias, wt, name):
    hn = _rms_fwd_call(x, g, name + "_norm", BF16)
    return (_mm(hn, wb, tb=wt, bias=bias, name=name, tm=1024, tn=1408, tk=1024), x), (x, g, hn, wb, bias is not None)


def _norm_linear_bwd(wt, name, res, cts):
    x, g, hn, wb, has_bias = res
    dy, dres = cts
    if wt:
        dw = _mm(dy, hn, ta=True, name=name + "_dw", tm=1408, tn=1024, tk=2048)
    else:
        dw = _mm(hn, dy, ta=True, name=name + "_dw", tm=1024, tn=1408, tk=1024)
    dx, dg = _mm_rms_bwd(dy, wb, x, g, dres, name + "_da", wt)
    dbias = _colsum(dy, name + "_db").reshape(-1) if has_bias else None
    return dx, dg.reshape(g.shape), jnp.zeros_like(wb), dw, dbias


norm_linear.defvjp(_norm_linear_fwd, _norm_linear_bwd)


def _rope_tables(T):
    half = ROT_DIM // 2
    inv = ROPE_THETA ** (-jnp.arange(0, ROT_DIM, 2, dtype=F32) / ROT_DIM)
    lane = jnp.arange(LANES) % HEAD_DIM
    freq = jnp.where(lane < ROT_DIM, inv[lane % half], 0.0)
    ang = jnp.arange(T, dtype=F32)[:, None] * freq[None, :]
    c, s = jnp.cos(ang), jnp.sin(ang)
    sa = jnp.where((lane >= half) & (lane < ROT_DIM), s, 0.0)
    sb = jnp.where(lane < half, -s, 0.0)
    return c, sa, sb


def _join_pieces_call(pieces, rotated, tabs, name):
    T = pieces[0].shape[0]
    widths = [p.shape[1] for p in pieces]
    W = sum(widths)
    tt = _tile(T, 512, 16)

    def body(*refs):
        tab_refs, o_ref, sum_ref = refs[len(pieces):len(pieces) + 3], refs[-2], refs[-1]

        @pl.when(pl.program_id(0) == 0)
        def _():
            sum_ref[...] = jnp.zeros_like(sum_ref)

        off = 0
        for p_ref, w, r in zip(refs, widths, rotated):
            piece = _rotate(p_ref[...], tab_refs, inverse=True) if r else p_ref[...]
            o_ref[:, off:off + w] = piece.astype(BF16)
            sum_ref[:, off:off + w] += jnp.sum(piece, axis=0, keepdims=True)
            off += w

    return _pcall(body, name=name, grid=(T // tt,),
                  in_specs=[BS((tt, w), lambda i: (i, 0)) for w in widths] + [BS((tt, LANES), lambda i: (i, 0))] * 3,
                  out_specs=[BS((tt, W), lambda i: (i, 0)), BS((1, W), lambda i: (0, 0))],
                  out_shape=[SDS((T, W), BF16), SDS((1, W), F32)],
                  compiler_params=_cparams(("arbitrary",)))(*pieces, *tabs)


@functools.partial(jax.custom_vjp, nondiff_argnums=(6, 7))
def norm_linear_pieces(x, g, wb, wc, bias, tabs, pieces, name):
    return _norm_linear_pieces_fwd(x, g, wb, wc, bias, tabs, pieces, name)[0]


def _norm_linear_pieces_fwd(x, g, wb, wc, bias, tabs, pieces, name):
    hn = _rms_fwd_call(x, g, name + "_norm", BF16)
    outs, row = [], 0
    for i, (width, rotated) in enumerate(pieces):
        outs.append(_mm(hn, wb[row:row + width], tb=True, bias=None if bias is None else bias[row:row + width],
                        rope=tabs if rotated else None, name=f"{name}_{i}", tm=2048, tn=1024, tk=1024))
        row += width
    return (*outs, x), (x, g, hn, wb, tabs, bias is not None)


def _norm_linear_pieces_bwd(pieces, name, res, cts):
    x, g, hn, wb, tabs, has_bias = res
    dy, dy_sum = _join_pieces_call(cts[:-1], [rotated for _, rotated in pieces], tabs, name + "_join")
    dw = _mm(dy, hn, ta=True, name=name + "_dw", tm=1408, tn=1024, tk=2048)
    dx, dg = _mm_rms_bwd(dy, wb, x, g, cts[-1], name + "_da", wt=True)
    dbias = dy_sum.reshape(-1) if has_bias else None
    return dx, dg.reshape(g.shape), jnp.zeros_like(wb), dw, dbias, jax.tree.map(jnp.zeros_like, tabs)


norm_linear_pieces.defvjp(_norm_linear_pieces_fwd, _norm_linear_pieces_bwd)


def _conv_fwd_call(x, w, b, name):
    T, C = x.shape
    tt = _tile(T, 512, 8)
    per = tt // 8

    def body(x_ref, halo_ref, w_ref, b_ref, o_ref):
        i = pl.program_id(0)
        halo = jnp.where(i > 0, halo_ref[...], 0.0)
        e = jnp.concatenate([halo, x_ref[...]], axis=0)
        acc = b_ref[...]
        for k in (3, 2, 1):
            acc = acc + pltpu.roll(e, k, axis=0)[8:, :] * w_ref[3 - k:4 - k, :]
        o_ref[...] = acc + x_ref[...] * w_ref[3:4, :]

    return _pcall(body, name=name, grid=(T // tt,),
                  in_specs=[BS((tt, C), lambda i: (i, 0)), BS((8, C), lambda i: (jnp.maximum(i * per - 1, 0), 0)),
                            BS((4, C), lambda i: (0, 0)), BS((1, C), lambda i: (0, 0))],
                  out_specs=BS((tt, C), lambda i: (i, 0)), out_shape=SDS((T, C), F32),
                  compiler_params=_cparams(("parallel",)))(x, x, w, b.reshape(1, C))


def _conv_bwd_call(x, w, dy, name):
    T, C = x.shape
    tt = _tile(T, 512, 8)
    per = tt // 8
    nt = T // tt

    def body(x_ref, halo_ref, w_ref, dy_ref, nxt_ref, dx_ref, dwb_ref):
        i = pl.program_id(0)
        halo = jnp.where(i > 0, halo_ref[...], 0.0)
        e = jnp.concatenate([halo, x_ref[...]], axis=0)
        dy = dy_ref[...]
        nxt = jnp.where(i < nt - 1, nxt_ref[...], 0.0)
        f = jnp.concatenate([dy, nxt], axis=0)
        dx = dy * w_ref[3:4, :]
        rows = [None] * 4
        rows[3] = jnp.sum(dy * x_ref[...], axis=0, keepdims=True)
        for k in (1, 2, 3):
            dx = dx + pltpu.roll(f, tt + 8 - k, axis=0)[:tt, :] * w_ref[3 - k:4 - k, :]
            rows[3 - k] = jnp.sum(dy * pltpu.roll(e, k, axis=0)[8:, :], axis=0, keepdims=True)
        dx_ref[...] = dx
        upd = jnp.concatenate(rows + [jnp.sum(dy, axis=0, keepdims=True), jnp.zeros((3, C), F32)], axis=0)

        @pl.when(i == 0)
        def _():
            dwb_ref[...] = jnp.zeros_like(dwb_ref)

        dwb_ref[...] += upd

    row = BS((tt, C), lambda i: (i, 0))
    return _pcall(body, name=name, grid=(nt,),
                  in_specs=[row, BS((8, C), lambda i: (jnp.maximum(i * per - 1, 0), 0)), BS((4, C), lambda i: (0, 0)),
                            row, BS((8, C), lambda i: (jnp.minimum((i + 1) * per, T // 8 - 1), 0))],
                  out_specs=[row, BS((8, C), lambda i: (0, 0))],
                  out_shape=[SDS((T, C), F32), SDS((8, C), F32)],
                  compiler_params=_cparams(("arbitrary",)))(x, x, w, dy, dy)


@functools.partial(jax.custom_vjp, nondiff_argnums=(3,))
def conv4(x, w, b, name):
    return _conv_fwd_call(x, w, b, name)


def _conv4_fwd(x, w, b, name):
    return _conv_fwd_call(x, w, b, name), (x, w)


def _conv4_bwd(name, res, dy):
    x, w = res
    dx, dwb = _conv_bwd_call(x, w, dy, name + "_bwd")
    return dx, dwb[0:4], dwb[4]


conv4.defvjp(_conv4_fwd, _conv4_bwd)


def _gates_fwd_call(xc, wa, ba, wx, bx, name):
    T, C = xc.shape
    hd = C // LRU_HEADS
    tt = _tile(T, 512, 8)

    def body(x_ref, wa_ref, ba_ref, wx_ref, bx_ref, ga_ref, gx_ref):
        xb = x_ref[...].astype(BF16)
        ga_ref[...] = jnp.dot(xb, wa_ref[0].astype(BF16), preferred_element_type=F32) + ba_ref[...]
        gx_ref[...] = jnp.dot(xb, wx_ref[0].astype(BF16), preferred_element_type=F32) + bx_ref[...]

    blk = BS((tt, hd), lambda i, h: (i, h))
    wsp = BS((1, hd, hd), lambda i, h: (h, 0, 0))
    bsp = BS((1, hd), lambda i, h: (0, h))
    return _pcall(body, name=name, grid=(T // tt, LRU_HEADS), in_specs=[blk, wsp, bsp, wsp, bsp],
                  out_specs=[blk, blk], out_shape=[SDS((T, C), F32)] * 2,
                  compiler_params=_cparams(("parallel", "parallel")))(xc, wa, ba.reshape(1, C), wx, bx.reshape(1, C))


def _gates_bwd_x_call(dga, dgx, wa, wx, name):
    T, C = dga.shape
    hd = C // LRU_HEADS
    tt = _tile(T, 512, 8)
    dn = (((1,), (1,)), ((), ()))

    def body(da_ref, dx_ref, wa_ref, wx_ref, o_ref):
        o_ref[...] = (lax.dot_general(da_ref[...].astype(BF16), wa_ref[0].astype(BF16), dn, preferred_element_type=F32)
                      + lax.dot_general(dx_ref[...].astype(BF16), wx_ref[0].astype(BF16), dn, preferred_element_type=F32))

    blk = BS((tt, hd), lambda i, h: (i, h))
    wsp = BS((1, hd, hd), lambda i, h: (h, 0, 0))
    return _pcall(body, name=name, grid=(T // tt, LRU_HEADS), in_specs=[blk, blk, wsp, wsp], out_specs=blk,
                  out_shape=SDS((T, C), F32), compiler_params=_cparams(("parallel", "parallel")))(dga, dgx, wa, wx)


def _gates_bwd_w_call(xc, dga, dgx, name):
    T, C = xc.shape
    hd = C // LRU_HEADS
    tt = _tile(T, 512, 8)
    dn = (((0,), (0,)), ((), ()))

    def body(x_ref, da_ref, dx_ref, dwa_ref, dwx_ref, dba_ref, dbx_ref):
        @pl.when(pl.program_id(1) == 0)
        def _():
            dwa_ref[...] = jnp.zeros_like(dwa_ref)
            dwx_ref[...] = jnp.zeros_like(dwx_ref)
            dba_ref[...] = jnp.zeros_like(dba_ref)
            dbx_ref[...] = jnp.zeros_like(dbx_ref)

        xb = x_ref[...].astype(BF16)
        da, dx = da_ref[...], dx_ref[...]
        dwa_ref[0] += lax.dot_general(xb, da.astype(BF16), dn, preferred_element_type=F32)
        dwx_ref[0] += lax.dot_general(xb, dx.astype(BF16), dn, preferred_element_type=F32)
        dba_ref[...] += jnp.sum(da, axis=0, keepdims=True)
        dbx_ref[...] += jnp.sum(dx, axis=0, keepdims=True)

    blk = BS((tt, hd), lambda h, i: (i, h))
    wsp = BS((1, hd, hd), lambda h, i: (h, 0, 0))
    bsp = BS((1, hd), lambda h, i: (0, h))
    return _pcall(body, name=name, grid=(LRU_HEADS, T // tt), in_specs=[blk, blk, blk],
                  out_specs=[wsp, wsp, bsp, bsp],
                  out_shape=[SDS((LRU_HEADS, hd, hd), F32)] * 2 + [SDS((1, C), F32)] * 2,
                  compiler_params=_cparams(("parallel", "arbitrary")))(xc, dga, dgx)


@functools.partial(jax.custom_vjp, nondiff_argnums=(7,))
def lru_gates(xc, wa, wa_c, ba, wx, wx_c, bx, name):
    return tuple(_gates_fwd_call(xc, wa, ba, wx, bx, name))


def _lru_gates_fwd(xc, wa, wa_c, ba, wx, wx_c, bx, name):
    return tuple(_gates_fwd_call(xc, wa, ba, wx, bx, name)), (xc, wa, wx, ba.shape)


def _lru_gates_bwd(name, res, g):
    xc, wa, wx, bshape = res
    dga, dgx = g
    dxc = _gates_bwd_x_call(dga, dgx, wa, wx, name + "_dx")
    dwa, dwx, dba, dbx = _gates_bwd_w_call(xc, dga, dgx, name + "_dw")
    return dxc, jnp.zeros_like(wa), dwa, dba.reshape(bshape), jnp.zeros_like(wx), dwx, dbx.reshape(bshape)


lru_gates.defvjp(_lru_gates_fwd, _lru_gates_bwd)


def _lru_coeffs(xc, ga, gx, lam):
    r = _sigmoid(ga)
    ig = _sigmoid(gx)
    z = -lam
    sp = jnp.maximum(z, 0.0) + jnp.log(1.0 + jnp.exp(-jnp.abs(z)))
    la = -LRU_C * r * sp
    a = jnp.exp(la)
    s = jnp.sqrt(-_expm1(2.0 * la))
    return r, ig, sp, a, s


LRU_TT = 256


def _scan_fwd_call(xc, ga, gx, y, lam, name):
    T, C = xc.shape
    tt = _tile(T, LRU_TT, 8)

    def body(xc_ref, ga_ref, gx_ref, y_ref, lam_ref, h_ref, rec_ref, a_buf, carry):
        @pl.when(pl.program_id(0) == 0)
        def _():
            carry[...] = jnp.zeros_like(carry)

        xcv = xc_ref[...]
        _, ig, _, a, s = _lru_coeffs(xcv, ga_ref[...], gx_ref[...], lam_ref[...])
        a_buf[...] = a
        h_ref[...] = s * (ig * xcv)

        def step(t, h):
            hn = a_buf[pl.ds(t, 1), :] * h + h_ref[pl.ds(t, 1), :]
            h_ref[pl.ds(t, 1), :] = hn
            return hn

        carry[0:1, :] = lax.fori_loop(0, tt, step, carry[0:1, :], unroll=8)
        g, _ = _gelu_parts(y_ref[...])
        rec_ref[...] = h_ref[...] * g

    row = BS((tt, C), lambda i: (i, 0))
    vec = BS((1, C), lambda i: (0, 0))
    return _pcall(body, name=name, grid=(T // tt,), in_specs=[row, row, row, row, vec], out_specs=[row, row],
                  out_shape=[SDS((T, C), F32)] * 2,
                  scratch_shapes=[pltpu.VMEM((tt, C), F32), pltpu.VMEM((8, C), F32)],
                  compiler_params=_cparams(("arbitrary",)))(xc, ga, gx, y, lam.reshape(1, C))


def _scan_bwd_call(xc, ga, gx, y, lam, h, drec, name):
    T, C = xc.shape
    tt = _tile(T, LRU_TT, 8)
    nt = T // tt
    per = tt // 8

    def body(xc_ref, ga_ref, gx_ref, y_ref, lam_ref, h_ref, halo_ref, dr_ref,
             dga_ref, dgx_ref, dxc_ref, dy_ref, dlam_ref, a_buf, g_buf, carry):
        i = pl.program_id(0)

        @pl.when(i == 0)
        def _():
            carry[...] = jnp.zeros_like(carry)
            dlam_ref[...] = jnp.zeros_like(dlam_ref)

        xcv, lam = xc_ref[...], lam_ref[...]
        r, ig, sp, a, s = _lru_coeffs(xcv, ga_ref[...], gx_ref[...], lam)
        gel, dgel = _gelu_parts(y_ref[...])
        drec = dr_ref[...]
        hv = h_ref[...]
        dy_ref[...] = drec * hv * dgel
        a_buf[...] = a
        g_buf[...] = drec * gel

        def step(j, q):
            t = tt - 1 - j
            g = g_buf[pl.ds(t, 1), :] + q
            g_buf[pl.ds(t, 1), :] = g
            return a_buf[pl.ds(t, 1), :] * g

        carry[0:1, :] = lax.fori_loop(0, tt, step, carry[0:1, :], unroll=8)
        g = g_buf[...]
        halo = jnp.where(i < nt - 1, halo_ref[...], 0.0)
        hprev = pltpu.roll(jnp.concatenate([halo, hv], axis=0), 1, axis=0)[8:, :]
        da = g * hprev
        dig = g * s * xcv
        ds = g * ig * xcv
        dla = da * a - ds * (a * a) / s
        dga_ref[...] = dla * (-LRU_C * sp) * r * (1.0 - r)
        dgx_ref[...] = dig * ig * (1.0 - ig)
        dxc_ref[...] = g * s * ig
        dlam_ref[...] += jnp.sum(dla * r, axis=0, keepdims=True) * (LRU_C * _sigmoid(-lam))

    row = BS((tt, C), lambda i: (nt - 1 - i, 0))
    vec = BS((1, C), lambda i: (0, 0))
    halo = BS((8, C), lambda i: (jnp.maximum((nt - 1 - i) * per - 1, 0), 0))
    return _pcall(body, name=name, grid=(nt,), in_specs=[row, row, row, row, vec, row, halo, row],
                  out_specs=[row, row, row, row, vec], out_shape=[SDS((T, C), F32)] * 4 + [SDS((1, C), F32)],
                  scratch_shapes=[pltpu.VMEM((tt, C), F32), pltpu.VMEM((tt, C), F32), pltpu.VMEM((8, C), F32)],
                  compiler_params=_cparams(("arbitrary",)))(xc, ga, gx, y, lam.reshape(1, C), h, h, drec)


@functools.partial(jax.custom_vjp, nondiff_argnums=(5,))
def lru_scan(xc, ga, gx, y, lam, name):
    return _scan_fwd_call(xc, ga, gx, y, lam, name)[1]


def _lru_scan_fwd(xc, ga, gx, y, lam, name):
    h, rec = _scan_fwd_call(xc, ga, gx, y, lam, name)
    return rec, (xc, ga, gx, y, lam, h)


def _lru_scan_bwd(name, res, drec):
    xc, ga, gx, y, lam, h = res
    dga, dgx, dxc, dy, dlam = _scan_bwd_call(xc, ga, gx, y, lam, h, drec, name + "_bwd")
    return dxc, dga, dgx, dy, dlam.reshape(lam.shape)


lru_scan.defvjp(_lru_scan_fwd, _lru_scan_bwd)


def _att_batch(d, shared=False):
    if shared:
        return 8, 1
    return (4, 1) if d == 1 else (1, min(d, 8))


def _att_masks(n, max_dist):
    qi = lax.broadcasted_iota(jnp.int32, (1, 2 * ATT_BLOCK, 2 * ATT_BLOCK), 1) & (ATT_BLOCK - 1)
    kj = lax.broadcasted_iota(jnp.int32, (1, 2 * ATT_BLOCK, 2 * ATT_BLOCK), 2)
    prev = (kj < ATT_BLOCK) & (kj >= qi + (ATT_BLOCK - max_dist)) & (n > 0)
    cur = (kj >= ATT_BLOCK) & (kj - ATT_BLOCK <= qi)
    return prev | cur


def _lo_lanes(rows):
    return lax.broadcasted_iota(jnp.int32, (rows, LANES), 1) < HEAD_DIM


def _lane_half(rows):
    return lax.broadcasted_iota(jnp.int32, (rows, LANES), 1) // HEAD_DIM


def _stack_heads(x2):
    lo = _lo_lanes(ATT_BLOCK)
    zero = jnp.zeros_like(x2)
    return jnp.concatenate([jnp.where(lo, x2, zero), jnp.where(lo, zero, x2)], axis=0)


def _unstack_heads(y):
    return jnp.where(_lo_lanes(ATT_BLOCK), y[:ATT_BLOCK], y[ATT_BLOCK:])


def _per_head_col(x2):
    return jnp.concatenate([x2[:, 0:1], x2[:, HEAD_DIM:HEAD_DIM + 1]], axis=0)


def _head_sums(x2):
    lo = _lo_lanes(ATT_BLOCK)
    return jnp.concatenate([jnp.sum(jnp.where(lo, x2, 0.0), axis=1, keepdims=True),
                            jnp.sum(jnp.where(lo, 0.0, x2), axis=1, keepdims=True)], axis=0)


def _att_specs(d, Wq, Wk, nb, clamp):
    shared = Wk != Wq
    cgw, sb = _att_batch(d, shared)
    cur = (lambda n: jnp.minimum(n, nb - 1)) if clamp else (lambda n: n)
    rows, qw, kw = ATT_BLOCK * d, cgw * LANES, (LANES if shared else cgw * LANES)
    kcol = (lambda g: 0) if shared else (lambda g: g)
    qsp = BS((rows, qw), lambda g, n: (cur(n), g))
    csp = BS((rows, kw), lambda g, n: (cur(n), kcol(g)))
    psp = BS((rows, kw), lambda g, n: (jnp.maximum(cur(n) - 1, 0), kcol(g)))
    return cgw, sb, shared, qsp, csp, psp, qw, kw


def _att_streams(d, sb, work):
    if d == 1:
        work([slice(None)])
        return

    def one(j, carry):
        work([pl.ds(j * sb + i, ATT_BLOCK, stride=d) for i in range(sb)])
        return carry

    lax.fori_loop(0, d // sb, one, 0)


def _att_problem_loads(rows, cgw, shared, g, q_ref, kc_ref, kp_ref, vc_ref, vp_ref, sk_ref):
    half = _lane_half(ATT_BLOCK)

    def kv(ref, r, p):
        x = ref[r, :]
        if not shared:
            return x[:, p * LANES:(p + 1) * LANES]
        return jnp.where(half == p // 4, x, pltpu.roll(x, HEAD_DIM, axis=1))

    qs, kb, vb, sk = [], [], [], []
    for r in rows:
        qrow = q_ref[r, :]
        for p in range(cgw):
            cols = slice(p * LANES, (p + 1) * LANES)
            qs.append(_stack_heads(qrow[:, cols].astype(BF16)))
            kb.append(jnp.concatenate([kv(kp_ref, r, p), kv(kc_ref, r, p)], axis=0).astype(BF16))
            vb.append(jnp.concatenate([kv(vp_ref, r, p), kv(vc_ref, r, p)], axis=0).astype(BF16))
            sk.append(_per_head_col(jnp.broadcast_to(sk_ref[:, cols], (ATT_BLOCK, LANES))))
    return jnp.stack(qs), jnp.stack(kb), jnp.stack(vb), jnp.stack(sk)


_BDOT_NT = (((2,), (2,)), ((0,), (0,)))
_BDOT_NN = (((2,), (1,)), ((0,), (0,)))
_BDOT_TN = (((1,), (1,)), ((0,), (0,)))


def _att_fwd_call(q, k, v, sinks, d, max_dist, name):
    T, Wq = q.shape
    Wk = k.shape[1]
    nb = T // (d * ATT_BLOCK)
    cgw, sb, shared, qsp, csp, psp, qw, kw = _att_specs(d, Wq, Wk, nb, False)
    G = Wq // qw
    assert not shared or (d == 1 and Wk == LANES and G == 1 and cgw == 8), "a shared kv pair serves 2 x 8 query heads"

    def body(q_ref, kc_ref, kp_ref, vc_ref, vp_ref, sk_ref, o_ref, lse_ref):
        g, n = pl.program_id(0), pl.program_id(1)

        def work(rows):
            qs, kband, vband, sk = _att_problem_loads(rows, cgw, shared, g, q_ref, kc_ref, kp_ref, vc_ref, vp_ref, sk_ref)
            s = lax.dot_general(qs, kband, _BDOT_NT, preferred_element_type=F32) * (HEAD_DIM ** -0.5)
            s = jnp.where(_att_masks(n, max_dist), s, NEG)
            m = jnp.maximum(jnp.max(s, axis=-1, keepdims=True), sk)
            e = jnp.exp(s - m)
            den = jnp.sum(e, axis=-1, keepdims=True) + jnp.exp(sk - m)
            o = lax.dot_general((e * (1.0 / den)).astype(BF16), vband, _BDOT_NN, preferred_element_type=F32)
            lse = jnp.broadcast_to(m + jnp.log(den), o.shape)
            for i, r in enumerate(rows):
                o_ref[r, :] = jnp.concatenate([_unstack_heads(o[i * cgw + p]) for p in range(cgw)], axis=1)
                lse_ref[r, :] = jnp.concatenate([_unstack_heads(lse[i * cgw + p]) for p in range(cgw)], axis=1)

        _att_streams(d, sb, work)

    sksp = BS((1, qw), lambda g, n: (0, g))
    return _pcall(body, name=name, grid=(G, nb), in_specs=[qsp, csp, psp, csp, psp, sksp], out_specs=[qsp, qsp],
                  out_shape=[SDS((T, Wq), F32)] * 2,
                  compiler_params=_cparams(("parallel", "parallel")))(q, k, k, v, v, sinks)


def _att_bwd_call(q, k, v, sinks, o, lse, do, dlse, d, max_dist, name):
    T, Wq = q.shape
    Wk = k.shape[1]
    nb = T // (d * ATT_BLOCK)
    cgw, sb, shared, qsp, csp, psp, qw, kw = _att_specs(d, Wq, Wk, nb, True)
    G = Wq // qw
    scale = HEAD_DIM ** -0.5

    def body(*refs):
        q_ref, kc_ref, kp_ref, vc_ref, vp_ref, sk_ref, o_ref, lse_ref, do_ref = refs[:9]
        dlse_ref = refs[9] if dlse is not None else None
        dq_ref, dk_ref, dv_ref, dsk_ref, ck, cv = refs[-6:]
        g, n = pl.program_id(0), pl.program_id(1)

        @pl.when(n == 0)
        def _():
            ck[...] = jnp.zeros_like(ck)
            cv[...] = jnp.zeros_like(cv)
            dsk_ref[...] = jnp.zeros_like(dsk_ref)

        def work(rows):
            qs, kband, vband, sk = _att_problem_loads(rows, cgw, shared, g, q_ref, kc_ref, kp_ref, vc_ref, vp_ref, sk_ref)
            dos, lse_c, corr = [], [], []
            for r in rows:
                do_r, o_r, lse_r = do_ref[r, :], o_ref[r, :], lse_ref[r, :]
                dlse_r = dlse_ref[r, :] if dlse_ref is not None else None
                for p in range(cgw):
                    cols = slice(p * LANES, (p + 1) * LANES)
                    dos.append(_stack_heads(do_r[:, cols].astype(BF16)))
                    lse_c.append(_per_head_col(lse_r[:, cols]))
                    delta = _head_sums(do_r[:, cols] * o_r[:, cols])
                    corr.append(-delta if dlse_r is None else _head_sums(dlse_r[:, cols]) - delta)
            dos, lse_c, corr = jnp.stack(dos), jnp.stack(lse_c), jnp.stack(corr)
            s = lax.dot_general(qs, kband, _BDOT_NT, preferred_element_type=F32) * scale
            pr = jnp.exp(jnp.where(_att_masks(n, max_dist), s, NEG) - lse_c)
            dp = lax.dot_general(dos, vband, _BDOT_NT, preferred_element_type=F32)
            ds = (pr * (dp + corr)).astype(BF16)
            dq = lax.dot_general(ds, kband, _BDOT_NN, preferred_element_type=F32) * scale
            dkb = lax.dot_general(ds, qs, _BDOT_TN, preferred_element_type=F32) * scale
            dvb = lax.dot_general(pr.astype(BF16), dos, _BDOT_TN, preferred_element_type=F32)
            dsk = jnp.exp(sk - lse_c) * corr
            lane = lax.broadcasted_iota(jnp.int32, (8, LANES), 1)
            for p in range(cgw):
                tot = [jnp.sum(jnp.stack([dsk[i * cgw + p, h * ATT_BLOCK:(h + 1) * ATT_BLOCK] for i in range(len(rows))]),
                               axis=(0, 1)).reshape(1, 1) for h in range(2)]
                dsk_ref[:, p * LANES:(p + 1) * LANES] += jnp.where(lane == 0, tot[0], jnp.where(lane == HEAD_DIM, tot[1], 0.0))

            def gather_pairs(parts):
                if not shared:
                    return jnp.concatenate(parts, axis=1)
                tot = [parts[4 * h] + parts[4 * h + 1] + parts[4 * h + 2] + parts[4 * h + 3] for h in range(2)]
                tot = [t + pltpu.roll(t, HEAD_DIM, axis=1) for t in tot]
                return jnp.where(_lo_lanes(ATT_BLOCK), tot[0], tot[1])

            for i, r in enumerate(rows):
                mine = range(i * cgw, (i + 1) * cgw)
                dq_ref[r, :] = jnp.concatenate([_unstack_heads(dq[b]) for b in mine], axis=1)
                dk_ref[r, :] = ck[r, :] + gather_pairs([dkb[b, :ATT_BLOCK] for b in mine])
                dv_ref[r, :] = cv[r, :] + gather_pairs([dvb[b, :ATT_BLOCK] for b in mine])
                ck[r, :] = gather_pairs([dkb[b, ATT_BLOCK:] for b in mine])
                cv[r, :] = gather_pairs([dvb[b, ATT_BLOCK:] for b in mine])

        @pl.when(n < nb)
        def _():
            _att_streams(d, sb, work)

        @pl.when(n == nb)
        def _():
            dk_ref[...] = ck[...]
            dv_ref[...] = cv[...]

    sksp = BS((1, qw), lambda g, n: (0, g))
    rows = ATT_BLOCK * d
    osp = BS((rows, kw), lambda g, n: (jnp.maximum(n - 1, 0), 0 if shared else g))
    kshape = SDS((T, Wk), F32)
    dq, dk, dv, dsk = _pcall(
        body, name=name, grid=(G, nb + 1),
        in_specs=[qsp, csp, psp, csp, psp, sksp, qsp, qsp, qsp] + ([qsp] if dlse is not None else []),
        out_specs=[qsp, osp, osp, BS((8, qw), lambda g, n: (0, g))],
        out_shape=[SDS((T, Wq), F32), kshape, kshape, SDS((8, Wq), F32)],
        scratch_shapes=[pltpu.VMEM((rows, kw), F32)] * 2,
        compiler_params=_cparams(("parallel", "arbitrary")))(q, k, k, v, v, sinks, o, lse, do,
                                                              *([dlse] if dlse is not None else []))
    return dq, dk, dv, dsk[0:1]


@functools.partial(jax.custom_vjp, nondiff_argnums=(4, 5, 6, 7))
def band_attention(q, k, v, sinks, d, max_dist, with_lse, name):
    return _band_attention_fwd(q, k, v, sinks, d, max_dist, with_lse, name)[0]


def _band_attention_fwd(q, k, v, sinks, d, max_dist, with_lse, name):
    o, lse = _att_fwd_call(q, k, v, sinks, d, max_dist, name)
    return ((o, lse) if with_lse else o), (q, k, v, sinks, o, lse)


def _band_attention_bwd(d, max_dist, with_lse, name, res, g):
    q, k, v, sinks, o, lse = res
    do, dlse = g if with_lse else (g, None)
    return _att_bwd_call(q, k, v, sinks, o, lse, do, dlse, d, max_dist, name + "_bwd")


band_attention.defvjp(_band_attention_fwd, _band_attention_bwd)


def _merge_weights(ls):
    mx = jnp.maximum(jnp.maximum(ls[0], ls[1]), ls[2])
    es = [jnp.exp(l - mx) for l in ls]
    inv = 1.0 / (es[0] + es[1] + es[2])
    return [e * inv for e in es]


def _merge_fwd_call(os_, ls_, name):
    T, W = os_[0].shape
    tt = _tile(T, 512, 8)

    def body(o1, o2, o3, l1, l2, l3, out_ref):
        w = _merge_weights([l1[...], l2[...], l3[...]])
        out_ref[...] = w[0] * o1[...] + w[1] * o2[...] + w[2] * o3[...]

    row = BS((tt, W), lambda i: (i, 0))
    return _pcall(body, name=name, grid=(T // tt,), in_specs=[row] * 6, out_specs=row,
                  out_shape=SDS((T, W), F32), compiler_params=_cparams(("parallel",)))(*os_, *ls_)


def _merge_bwd_call(os_, ls_, do, name):
    T, W = os_[0].shape
    tt = _tile(T, 512, 8)

    def body(o1, o2, o3, l1, l2, l3, do_ref, d1, d2, d3, e1, e2, e3):
        w = _merge_weights([l1[...], l2[...], l3[...]])
        dov = do_ref[...]
        ts = [dov * o[...] for o in (o1, o2, o3)]
        mean = w[0] * ts[0] + w[1] * ts[1] + w[2] * ts[2]
        for wi, ti, dref, eref in zip(w, ts, (d1, d2, d3), (e1, e2, e3)):
            dref[...] = wi * dov
            eref[...] = wi * (ti - mean)

    row = BS((tt, W), lambda i: (i, 0))
    return _pcall(body, name=name, grid=(T // tt,), in_specs=[row] * 7, out_specs=[row] * 6,
                  out_shape=[SDS((T, W), F32)] * 6, compiler_params=_cparams(("parallel",)))(*os_, *ls_, do)


@functools.partial(jax.custom_vjp, nondiff_argnums=(2,))
def merge3(os_, ls_, name):
    return _merge_fwd_call(os_, ls_, name)


def _merge3_fwd(os_, ls_, name):
    return _merge_fwd_call(os_, ls_, name), (os_, ls_)


def _merge3_bwd(name, res, do):
    os_, ls_ = res
    out = _merge_bwd_call(os_, ls_, do, name + "_bwd")
    return tuple(out[:3]), tuple(out[3:])


merge3.defvjp(_merge3_fwd, _merge3_bwd)


def _xa_probs(qb, kb, scale):
    s = lax.dot_general(qb, kb, (((1,), (1,)), ((), ())), preferred_element_type=F32) * scale
    e = jnp.exp(s - jnp.max(s, axis=-1, keepdims=True))
    return e / jnp.sum(e, axis=-1, keepdims=True)


def _xa_fwd_call(q, kv, name):
    T, W = q.shape
    M = kv.shape[0]
    hd = XA_HEAD_DIM
    tq = _tile(T, 512, 8)
    scale = hd ** -0.5

    def body(q_ref, k_ref, v_ref, o_ref):
        p = _xa_probs(q_ref[...].astype(BF16), k_ref[...].astype(BF16), scale)
        o_ref[...] = jnp.dot(p.astype(BF16), v_ref[...].astype(BF16), preferred_element_type=F32)

    qsp = BS((tq, hd), lambda i, h: (i, h))
    return _pcall(body, name=name, grid=(T // tq, XA_HEADS),
                  in_specs=[qsp, BS((M, hd), lambda i, h: (0, h)), BS((M, hd), lambda i, h: (0, XA_HEADS + h))],
                  out_specs=qsp, out_shape=SDS((T, W), F32),
                  compiler_params=_cparams(("parallel", "parallel")))(q, kv, kv)


def _xa_bwd_call(q, kv, do, name):
    T, W = q.shape
    M = kv.shape[0]
    hd = XA_HEAD_DIM
    tq = _tile(T, 512, 8)
    scale = hd ** -0.5
    dn_nt = (((1,), (1,)), ((), ()))
    dn_tn = (((0,), (0,)), ((), ()))

    def body(q_ref, k_ref, v_ref, do_ref, dq_ref, dk_ref, dv_ref):
        @pl.when(pl.program_id(1) == 0)
        def _():
            dk_ref[...] = jnp.zeros_like(dk_ref)
            dv_ref[...] = jnp.zeros_like(dv_ref)

        qb, kb, vb = q_ref[...].astype(BF16), k_ref[...].astype(BF16), v_ref[...].astype(BF16)
        p = _xa_probs(qb, kb, scale)
        dob = do_ref[...].astype(BF16)
        dp = lax.dot_general(dob, vb, dn_nt, preferred_element_type=F32)
        ds = (p * (dp - jnp.sum(p * dp, axis=-1, keepdims=True))).astype(BF16)
        dq_ref[...] = jnp.dot(ds, kb, preferred_element_type=F32) * scale
        dk_ref[...] += lax.dot_general(ds, qb, dn_tn, preferred_element_type=F32) * scale
        dv_ref[...] += lax.dot_general(p.astype(BF16), dob, dn_tn, preferred_element_type=F32)

    qsp = BS((tq, hd), lambda h, i: (i, h))
    ksp = BS((M, hd), lambda h, i: (0, h))
    return _pcall(body, name=name, grid=(XA_HEADS, T // tq),
                  in_specs=[qsp, ksp, BS((M, hd), lambda h, i: (0, XA_HEADS + h)), qsp],
                  out_specs=[qsp, ksp, ksp], out_shape=[SDS((T, W), F32), SDS((M, W), F32), SDS((M, W), F32)],
                  compiler_params=_cparams(("parallel", "arbitrary")))(q, kv, kv, do)


@functools.partial(jax.custom_vjp, nondiff_argnums=(2,))
def cross_attention(q, kv, name):
    return _xa_fwd_call(q, kv, name)


def _cross_attention_fwd(q, kv, name):
    return _xa_fwd_call(q, kv, name), (q, kv)


def _cross_attention_bwd(name, res, do):
    q, kv = res
    dq, dk, dv = _xa_bwd_call(q, kv, do, name + "_bwd")
    return dq, jnp.concatenate([dk, dv], axis=1)


cross_attention.defvjp(_cross_attention_fwd, _cross_attention_bwd)


def _gate_up_swiglu_call(hn, w1t, name):
    T, K = hn.shape
    F = w1t.shape[0] // 2
    tm, tn = _tile(T, 2048), _tile(F, 256)
    nj = F // tn
    dn = (((1,), (1,)), ((), ()))

    def body(a_ref, wg_ref, wu_ref, g_ref, u_ref, act_ref):
        a = a_ref[...]
        g = lax.dot_general(a, wg_ref[...], dn, preferred_element_type=F32)
        u = lax.dot_general(a, wu_ref[...], dn, preferred_element_type=F32)
        g_ref[...] = g
        u_ref[...] = u
        act_ref[...] = ((g * _sigmoid(g)) * u).astype(BF16)

    tile = BS((tm, tn), lambda i, j: (i, j))
    return _pcall(body, name=name, grid=(T // tm, nj),
                  in_specs=[BS((tm, K), lambda i, j: (i, 0)), BS((tn, K), lambda i, j: (j, 0)),
                            BS((tn, K), lambda i, j: (j + nj, 0))],
                  out_specs=[tile, tile, tile], out_shape=[SDS((T, F), F32), SDS((T, F), F32), SDS((T, F), BF16)],
                  compiler_params=_cparams(("parallel", "parallel")))(hn, w1t, w1t)


def _swiglu_bwd_call(g, u, dact, name):
    T, F = g.shape
    tt = _tile(T, 128, 16)

    def body(g_ref, u_ref, d_ref, o_ref):
        g, d = g_ref[...], d_ref[...]
        sg = _sigmoid(g)
        o_ref[:, :F] = (d * u_ref[...] * (sg * (1.0 + g * (1.0 - sg)))).astype(BF16)
        o_ref[:, F:] = (d * (g * sg)).astype(BF16)

    row = BS((tt, F), lambda i: (i, 0))
    return _pcall(body, name=name, grid=(T // tt,), in_specs=[row, row, row],
                  out_specs=BS((tt, 2 * F), lambda i: (i, 0)), out_shape=SDS((T, 2 * F), BF16),
                  compiler_params=_cparams(("parallel",)))(g, u, dact)


@functools.partial(jax.custom_vjp, nondiff_argnums=(6,))
def ffn_block(h, g, w1b, w1c, w2b, w2c, name):
    return _ffn_fwd(h, g, w1b, w1c, w2b, w2c, name)[0]


def _ffn_fwd(h, g, w1b, w1c, w2b, w2c, name):
    hn = _rms_fwd_call(h, g, name + "_norm", BF16)
    gate, up, act = _gate_up_swiglu_call(hn, w1b, name + "_gu")
    out = _mm(act, w2b, add=h, name=name + "_down", tm=1024, tn=1024, tk=2816)
    return out, (h, g, hn, gate, up, act, w1b, w2b)


def _ffn_bwd(name, res, dout):
    h, g, hn, gate, up, act, w1b, w2b = res
    dact = _mm(dout, w2b, tb=True, name=name + "_down_da", tm=1024, tn=1408, tk=1024)
    dw2 = _mm(act, dout, ta=True, name=name + "_down_dw", tm=1408, tn=1024, tk=1024)
    dgu = _swiglu_bwd_call(gate, up, dact, name + "_swiglu_bwd")
    dw1 = _mm(dgu, hn, ta=True, name=name + "_gu_dw", tm=1408, tn=1024, tk=2048)
    dh, dg = _mm_rms_bwd(dgu, w1b, h, g, dout, name + "_gu_da", wt=True)
    return dh, dg.reshape(g.shape), jnp.zeros_like(w1b), dw1, jnp.zeros_like(w2b), dw2


ffn_block.defvjp(_ffn_fwd, _ffn_bwd)


def _final_call(h, g, target, name):
    T, Dm = h.shape
    tt = _tile(T, 512, 8)

    def body(x_ref, g_ref, t_ref, loss_ref, dx_ref, dg_ref):
        @pl.when(pl.program_id(0) == 0)
        def _():
            loss_ref[...] = jnp.zeros_like(loss_ref)
            dg_ref[...] = jnp.zeros_like(dg_ref)

        xv, gv = x_ref[...], g_ref[...]
        r = lax.rsqrt(jnp.mean(xv * xv, axis=-1, keepdims=True) + NORM_EPS)
        xh = xv * r
        err = xh * gv - t_ref[...]
        loss_ref[...] += 0.5 * jnp.sum(jnp.mean(err * err, axis=-1, keepdims=True), axis=0, keepdims=True)
        dy = err * (1.0 / Dm)
        dyg = dy * gv
        dx_ref[...] = r * (dyg - xh * jnp.mean(dyg * xh, axis=-1, keepdims=True))
        dg_ref[...] += jnp.sum(dy * xh, axis=0, keepdims=True)

    row = BS((tt, Dm), lambda i: (i, 0))
    vec = BS((1, Dm), lambda i: (0, 0))
    return _pcall(body, name=name, grid=(T // tt,), in_specs=[row, vec, row],
                  out_specs=[BS((1, 1), lambda i: (0, 0)), row, vec],
                  out_shape=[SDS((1, 1), F32), SDS((T, Dm), F32), SDS((1, Dm), F32)],
                  compiler_params=_cparams(("arbitrary",)))(h, g.reshape(1, Dm), target)


ADAMW_BLOCK_ELEMS = 64 * 1024


def _adamw_call(parts, w, m, v, name):
    shape = w.shape
    if not isinstance(parts, (list, tuple)):
        parts, shape3 = [parts], (1,) + shape
    else:
        shape3 = shape
    n_lead = shape3[0]
    r, N = shape3[-2], shape3[-1]
    Ld = math.prod(shape3[1:-2])
    w, m, v = (t.reshape(n_lead * Ld, r, N) for t in (w, m, v))
    tr = _tile(r, max(8, ADAMW_BLOCK_ELEMS // N), 8)
    c1 = 1.0 - ADAM_B1 ** ADAM_STEP
    c2 = 1.0 - ADAM_B2 ** ADAM_STEP
    outs = None
    for lead, p in enumerate(parts):
        def body(p_ref, w_ref, m_ref, v_ref, *rest):
            g_ref, d_ref, nm_ref, nv_ref = rest[-4:]
            g = p_ref[0]
            for j in range(1, N_DEV):
                g = g + p_ref[j]
            nm = ADAM_B1 * m_ref[...] + (1.0 - ADAM_B1) * g
            nv = ADAM_B2 * v_ref[...] + (1.0 - ADAM_B2) * (g * g)
            g_ref[...] = g
            nm_ref[...] = nm
            nv_ref[...] = nv
            d_ref[...] = -ADAM_LR * ((nm / c1) / (jnp.sqrt(nv / c2) + ADAM_EPS) + ADAM_WD * w_ref[...])

        base = lead * Ld
        row = BS((1, tr, N), lambda l, i, base=base: (base + l, i, 0))
        prev = [] if outs is None else list(outs)
        outs = _pcall(body, name=f"{name}_{lead}", grid=(Ld, r // tr),
                      in_specs=[BS((N_DEV, 1, tr, N), lambda l, i: (0, l, i, 0)), row, row, row]
                      + [BS(memory_space=pl.ANY)] * len(prev),
                      out_specs=[row] * 4, out_shape=[SDS((n_lead * Ld, r, N), F32)] * 4,
                      input_output_aliases={4 + j: j for j in range(len(prev))},
                      compiler_params=_cparams(("parallel", "parallel")))(p.reshape(N_DEV, Ld, r, N), w, m, v, *prev)
    return [t.reshape(shape) for t in outs]


def _place():
    return lax.axis_index("x"), lax.axis_index("y"), lax.axis_index("c")


def _all_gather(xs, name):
    n = len(xs)
    pairs = [(i, l) for i, x in enumerate(xs) for l in range(x.shape[0])]

    def body(*refs):
        x_refs, o_refs = refs[:n], refs[n:2 * n]
        send_sems, recv_sems, local_sems = refs[2 * n:]
        x_, y_, c_ = _place()
        me, sibling = (x_, y_, c_), (x_, y_, 1 - c_)
        chips = [(1 - x_, y_), (x_, 1 - y_), (1 - x_, 1 - y_)]

        def copy(e, k, block, to, from_input=False):
            i, l = pairs[e]
            px, py, pc = block
            dst = o_refs[i].at[l, 4 * px + 2 * py + pc]
            return pltpu.make_async_remote_copy(
                src_ref=x_refs[i].at[l] if from_input else dst, dst_ref=dst,
                send_sem=send_sems.at[7 * e + k], recv_sem=recv_sems.at[7 * e + k],
                device_id=to, device_id_type=pl.DeviceIdType.MESH)

        every = range(len(pairs))
        mine = [pltpu.make_async_copy(x_refs[i].at[l], o_refs[i].at[l, 4 * x_ + 2 * y_ + c_], local_sems.at[e])
                for e, (i, l) in enumerate(pairs)]
        for cp in mine:
            cp.start()
        first = [copy(e, 0, me, sibling, True) for e in every]
        first += [copy(e, 1 + j, me, (*chip, c_), True) for j, chip in enumerate(chips) for e in every]
        for cp in first:
            cp.start()
        passed = []
        for j, chip in enumerate(chips):
            for e in every:
                copy(e, 1 + j, (*chip, c_), me).wait_recv()
            for e in every:
                cp = copy(e, 4 + j, (*chip, c_), sibling)
                cp.start()
                passed.append(cp)
        for e in every:
            copy(e, 0, sibling, me).wait_recv()
        for j, chip in enumerate(chips):
            for e in every:
                copy(e, 4 + j, (*chip, 1 - c_), me).wait_recv()
        for cp in first + passed:
            cp.wait_send()
        for cp in mine:
            cp.wait()

    any_spec = BS(memory_space=pl.ANY)
    return _pcall(body, name=name, in_specs=[any_spec] * n, out_specs=[any_spec] * n,
                  out_shape=[SDS((x.shape[0], N_DEV) + x.shape[1:], x.dtype) for x in xs],
                  scratch_shapes=[pltpu.SemaphoreType.DMA((7 * len(pairs),)), pltpu.SemaphoreType.DMA((7 * len(pairs),)),
                                  pltpu.SemaphoreType.DMA((len(pairs),))],
                  compiler_params=pltpu.CompilerParams(has_side_effects=True))(*xs)


def _peer_of(k, place):
    x_, y_, c_ = place
    fx, fy, fc = (k >> 2) & 1, (k >> 1) & 1, k & 1
    return fx + x_ - 2 * fx * x_, fy + y_ - 2 * fy * y_, fc + c_ - 2 * fc * c_


def _split_copy(src_ref, land_ref, send_sems, recv_sems, e, k, place, scatter):
    x_, y_, c_ = place
    px, py, pc = _peer_of(k, place)
    return pltpu.make_async_remote_copy(
        src_ref=src_ref.at[4 * px + 2 * py + pc] if scatter else src_ref, dst_ref=land_ref.at[4 * x_ + 2 * y_ + c_],
        send_sem=send_sems.at[7 * e + k - 1], recv_sem=recv_sems.at[7 * e + k - 1],
        device_id=(px, py, pc), device_id_type=pl.DeviceIdType.MESH)


def _own_copy(src_ref, land_ref, sems, slot, place, scatter):
    x_, y_, c_ = place
    me = 4 * x_ + 2 * y_ + c_
    return pltpu.make_async_copy(src_ref.at[me] if scatter else src_ref, land_ref.at[me], sems.at[slot])


_HBM_SPEC = BS(memory_space=pltpu.HBM)
_SEM_SPEC = BS(memory_space=pltpu.SEMAPHORE)
_EFFECT = pltpu.SideEffectType.DATAFLOW_SIDE_EFFECTING


def _copies_start(srcs, scatter, name, thru=None):
    n = len(srcs)
    lands = [lax.empty(s.shape if scatter else (N_DEV,) + s.shape, s.dtype) for s in srcs]
    passed = srcs + lands + list(thru or ())

    def body(*refs):
        src_refs, land_refs = refs[:n], refs[n:2 * n]
        send_sems, recv_sems = refs[len(passed)], refs[len(passed) + 1]
        token = refs[-1]
        place = _place()
        for e in range(n):
            for k in range(1, N_DEV):
                _split_copy(src_refs[e], land_refs[e], send_sems, recv_sems, e, k, place, scatter).start()
            _own_copy(src_refs[e], land_refs[e], send_sems, 7 * n + e, place, scatter).start()
        token[...] = jnp.zeros_like(token)

    hbm = lambda t: pltpu.with_memory_space_constraint(t, pltpu.HBM)
    out = _pcall(body, name=name,
                 out_shape=(pltpu.SemaphoreType.DMA((8 * n,)), pltpu.SemaphoreType.DMA((7 * n,)),
                            *[pltpu.HBM(t.shape, t.dtype) for t in passed], SDS((8, LANES), F32)),
                 in_specs=[_HBM_SPEC] * len(passed),
                 out_specs=(_SEM_SPEC, _SEM_SPEC, *[_HBM_SPEC] * len(passed), BS(memory_space=pltpu.VMEM)),
                 input_output_aliases={i: 2 + i for i in range(len(passed))},
                 compiler_params=pltpu.CompilerParams(has_side_effects=_EFFECT))(*[hbm(t) for t in passed])
    return out[0], out[1], list(out[2:2 + n]), list(out[2 + n:2 + 2 * n]), list(out[2 + 2 * n:-1])


def _copies_wait(started, which, scatter, after, name):
    send_sems, recv_sems, srcs, lands, _ = started
    n, n_started = len(which), len(srcs)
    after = list(after) if isinstance(after, (list, tuple)) else [after]

    def body(*refs):
        src_refs, land_refs = refs[:n], refs[n:2 * n]
        send_s, recv_s = refs[2 * n], refs[2 * n + 1]
        place = _place()
        for j, e in enumerate(which):
            for k in range(1, N_DEV):
                cp = _split_copy(src_refs[j], land_refs[j], send_s, recv_s, e, k, place, scatter)
                cp.wait_send()
                cp.wait_recv()
            _own_copy(src_refs[j], land_refs[j], send_s, 7 * n_started + e, place, scatter).wait()

    args = [srcs[e] for e in which] + [lands[e] for e in which]
    out = _pcall(body, name=name, out_shape=tuple(pltpu.HBM(t.shape, t.dtype) for t in args),
                 in_specs=[_HBM_SPEC] * (2 * n) + [_SEM_SPEC, _SEM_SPEC] + [BS(memory_space=pl.ANY)] * len(after),
                 out_specs=tuple([_HBM_SPEC] * (2 * n)), input_output_aliases={i: i for i in range(2 * n)},
                 compiler_params=pltpu.CompilerParams(has_side_effects=_EFFECT))(*args, send_sems, recv_sems, *after)
    return list(out[:n]), list(out[n:])


def _pad_flat(t, quantum=PACK_QUANTUM):
    f = t.reshape(-1)
    pad = (-f.shape[0]) % quantum
    return jnp.pad(f, (0, pad)) if pad else f


def _pack(arrs, dtype):
    return jnp.concatenate([_pad_flat(a.astype(dtype)) for a in arrs]).reshape(-1, LANES)


def _unpack(buf, shapes, lead=()):
    flat = buf.reshape(lead + (-1,))
    out, off = [], 0
    for s in shapes:
        n = math.prod(s)
        out.append(flat[..., off:off + n].reshape(lead + tuple(s)))
        off += n + (-n) % PACK_QUANTUM
    return out


def _full_from_gathered(g, axis):
    t = jnp.moveaxis(g, 0, axis)
    s = t.shape
    return t.reshape(s[:axis] + (s[axis] * s[axis + 1],) + s[axis + 2:])


def _parts_from_full(t, axis):
    s = t.shape
    t = t.reshape(s[:axis] + (N_DEV, s[axis] // N_DEV) + s[axis + 1:])
    return jnp.moveaxis(t, axis, 0)


def _head_rows(t):
    return jnp.repeat(t, HEAD_DIM).reshape(1, -1)


def _dilated_attention(q, k, v, name):
    no_sink = jnp.full((1, q.shape[1]), NEG, F32)
    outs, lses = zip(*[band_attention(q, k, v, no_sink, d, ATT_BLOCK, True, f"{name}_d{d}") for d in DILATIONS])
    return merge3(tuple(outs), tuple(lses), name + "_merge")


STAGES = (
    ("proj0", ('mix_norm',), (('ab_w_in', 0),)),
    ("mixer0", ('lru_conv_w', 'lru_conv_b', 'lru_ba', 'lru_bx', 'lru_lambda'),
     (('lru_wa', 0), ('lru_wx', 0), ('ab_w_out', 0))),
    ("xa0", ('xa_norm', 'xa_mem_norm'), (('xa_wq', 0), ('xa_wkv', 0), ('xa_wo', 0))),
    ("ffn0", ('ffn_norm',), (('ffn_w_gate_up', 0), ('ffn_w_down', 0))),
    ("mixer1", ('mix_norm', 'c_b_qkv', 'c_sinks', 'c_b_out'), (('c_w_qkv', 0), ('c_w_out', 0))),
    ("xa1", ('xa_norm', 'xa_mem_norm'), (('xa_wq', 1), ('xa_wkv', 1), ('xa_wo', 1))),
    ("ffn1", ('ffn_norm',), (('ffn_w_gate_up', 1), ('ffn_w_down', 1))),
)


def _stage_fn(stage, Wb, tabs, mem):
    layer = int(stage[-1])
    L = f"l{layer}"

    def run(S, Cw, h):
        def lin(a, key, bias, add, name, rows=None):
            wb, wc = Wb[key], Cw[key]
            if rows is not None:
                wb, wc = wb[rows], wc[rows]
            return linear(a, wb, wc, bias, add, name)

        def norm_lin(a, gain, key, bias, name):
            return norm_linear(a, gain, Wb[key], Cw[key], bias, key[0] in COLUMN_CUT, name)

        if stage == "mixer0":
            h, x_br, y_br, q, k, v = h
            C = S['lru_conv_w'].shape[-1]
            xc = conv4(x_br, S['lru_conv_w'][0], S['lru_conv_b'][0], L + "_conv")
            ga, gx = lru_gates(xc, Wb['lru_wa', 0], Cw['lru_wa', 0], S['lru_ba'][0],
                               Wb['lru_wx', 0], Cw['lru_wx', 0], S['lru_bx'][0], L + "_gates")
            rec = lru_scan(xc, ga, gx, y_br, S['lru_lambda'][0], L + "_scan")
            att = _dilated_attention(q, k, v, L + "_att")
            h = lin(att, ('ab_w_out', 0), None, h, L + "_w_out_att", slice(C, None))
            return lin(rec, ('ab_w_out', 0), None, h, L + "_w_out_rec", slice(0, C))
        if stage == "mixer1":
            qw = C_HEADS * HEAD_DIM
            kw = C_KV_HEADS * HEAD_DIM
            q, k, v, h = norm_linear_pieces(h, S['mix_norm'][1], Wb['c_w_qkv', 0], Cw['c_w_qkv', 0], S['c_b_qkv'][0], tabs,
                                            ((qw, True), (kw, True), (kw, False)), L + "_w_qkv")
            o = band_attention(q, k, v, _head_rows(S['c_sinks'][0]), 1, ATT_BLOCK - 1, False, L + "_att")
            return lin(o, ('c_w_out', 0), S['c_b_out'][0], h, L + "_w_out")
        if stage.startswith("xa"):
            xq, h = norm_lin(h, S['xa_norm'][layer], ('xa_wq', layer), None, L + "_xa_wq")
            xkv, _ = norm_lin(mem, S['xa_mem_norm'][layer], ('xa_wkv', layer), None, L + "_xa_wkv")
            return lin(cross_attention(xq, xkv, L + "_xa"), ('xa_wo', layer), None, h, L + "_xa_wo")
        gu, down = ('ffn_w_gate_up', layer), ('ffn_w_down', layer)
        return ffn_block(h, S['ffn_norm'][layer], Wb[gu], Cw[gu], Wb[down], Cw[down], L + "_ffn")

    return run


def kernel(x, mem, mix_norm, ab_w_in, lru_conv_w, lru_conv_b, lru_wa, lru_ba, lru_wx, lru_bx, lru_lambda, ab_w_out, c_w_qkv, c_b_qkv, c_sinks, c_w_out, c_b_out, xa_norm, xa_mem_norm, xa_wq, xa_wkv, xa_wo, ffn_norm, ffn_w_gate_up, ffn_w_down, final_norm, loss_target, m_mix_norm, m_ab_w_in, m_lru_conv_w, m_lru_conv_b, m_lru_wa, m_lru_ba, m_lru_wx, m_lru_bx, m_lru_lambda, m_ab_w_out, m_c_w_qkv, m_c_b_qkv, m_c_sinks, m_c_w_out, m_c_b_out, m_xa_norm, m_xa_mem_norm, m_xa_wq, m_xa_wkv, m_xa_wo, m_ffn_norm, m_ffn_w_gate_up, m_ffn_w_down, m_final_norm, v_mix_norm, v_ab_w_in, v_lru_conv_w, v_lru_conv_b, v_lru_wa, v_lru_ba, v_lru_wx, v_lru_bx, v_lru_lambda, v_ab_w_out, v_c_w_qkv, v_c_b_qkv, v_c_sinks, v_c_w_out, v_c_b_out, v_xa_norm, v_xa_mem_norm, v_xa_wq, v_xa_wkv, v_xa_wo, v_ffn_norm, v_ffn_w_gate_up, v_ffn_w_down, v_final_norm):
    w_loc = dict(zip(WEIGHT_NAMES, (mix_norm, ab_w_in, lru_conv_w, lru_conv_b, lru_wa, lru_ba, lru_wx, lru_bx, lru_lambda, ab_w_out, c_w_qkv, c_b_qkv, c_sinks, c_w_out, c_b_out, xa_norm, xa_mem_norm, xa_wq, xa_wkv, xa_wo, ffn_norm, ffn_w_gate_up, ffn_w_down, final_norm)))
    m_loc = dict(zip(WEIGHT_NAMES, (m_mix_norm, m_ab_w_in, m_lru_conv_w, m_lru_conv_b, m_lru_wa, m_lru_ba, m_lru_wx, m_lru_bx, m_lru_lambda, m_ab_w_out, m_c_w_qkv, m_c_b_qkv, m_c_sinks, m_c_w_out, m_c_b_out, m_xa_norm, m_xa_mem_norm, m_xa_wq, m_xa_wkv, m_xa_wo, m_ffn_norm, m_ffn_w_gate_up, m_ffn_w_down, m_final_norm)))
    v_loc = dict(zip(WEIGHT_NAMES, (v_mix_norm, v_ab_w_in, v_lru_conv_w, v_lru_conv_b, v_lru_wa, v_lru_ba, v_lru_wx, v_lru_bx, v_lru_lambda, v_ab_w_out, v_c_w_qkv, v_c_b_qkv, v_c_sinks, v_c_w_out, v_c_b_out, v_xa_norm, v_xa_mem_norm, v_xa_wq, v_xa_wkv, v_xa_wo, v_ffn_norm, v_ffn_w_gate_up, v_ffn_w_down, v_final_norm)))

    first_keys = list(STAGES[0][2])
    keys = [key for _, _, stage_keys in STAGES[1:] for key in stage_keys]
    shards = [_shard_view(n, w_loc[n])[l].astype(BF16) for n, l in keys]
    first_g = _all_gather([_pack([w_loc[n] for n in SMALL], F32)[None]]
                          + [_shard_view(n, w_loc[n])[l].astype(BF16)[None] for n, l in first_keys], "gather_first")
    gather = _copies_start(shards, False, "gather_start", thru=[first_g[0]])
    small_g = gather[4][0][0]
    Wb = {key: _full_from_gathered(g[0], _layer_shard_axis(key[0])) for key, g in zip(first_keys, first_g[1:])}
    S = {n: w_loc[n] for n in REPLICATED}
    for n, t in zip(SMALL, _unpack(small_g, [w_loc[n].shape for n in SMALL], lead=(N_DEV,))):
        S[n] = _full_from_gathered(t, SHARD_AXIS[n])

    tabs = _rope_tables(x.shape[1])
    w_in = Wb[first_keys[0]]
    hn0 = _rms_fwd_call(x[0], S['mix_norm'][0], "l0_w_in_norm", BF16)
    lru_w, att_w = N_DEV * lru_conv_w.shape[-1], B_HEADS * HEAD_DIM
    pieces = (("x", lru_w, False), ("y", lru_w, False), ("q", att_w, True), ("k", att_w, True), ("v", att_w, False))
    h, row = [x[0]], 0
    for piece, width, rotated in pieces:
        h.append(_mm(hn0, w_in[row:row + width], tb=True, rope=tabs if rotated else None, name="l0_w_in_" + piece,
                     tm=2048, tn=1024, tk=1024))
        row += width
    h = tuple(h)
    vjps = []
    for stage, small_names, stage_keys in STAGES[1:]:
        which = [keys.index(key) for key in stage_keys]
        _, lands = _copies_wait(gather, which, False, jax.tree.leaves(h)[-1], "gather_wait_" + stage)
        for e, land in zip(which, lands):
            Wb[keys[e]] = _full_from_gathered(land, _layer_shard_axis(keys[e][0]))
        carriers = {key: jnp.zeros(Wb[key].shape, F32) for key in stage_keys}
        h, vjp_fn = jax.vjp(_stage_fn(stage, Wb, tabs, mem[0]), {n: S[n] for n in small_names}, carriers, h)
        vjps.append(vjp_fn)
    loss_part, dh, dg_final = _final_call(h, S['final_norm'], loss_target[0], "final_loss")

    grads = {'final_norm': dg_final.reshape(final_norm.shape)}
    exchanges, send_keys, send_parts = [], [], []

    def start_exchange(stage, dh):
        leaves, tree = jax.tree.flatten(dh)
        started = _copies_start(list(send_parts), True, "grad_start_" + stage, thru=leaves)
        exchanges.append((stage, started, list(send_keys)))
        send_keys.clear()
        send_parts.clear()
        return jax.tree.unflatten(tree, started[4])

    for (stage, small_names, stage_keys), vjp_fn in zip(reversed(STAGES[1:]), reversed(vjps)):
        g_small, g_big, dh = vjp_fn(dh)
        for n in small_names:
            grads[n] = grads[n] + g_small[n] if n in grads else g_small[n]
        send_keys += list(stage_keys)
        send_parts += [_parts_from_full(g_big[key], _layer_shard_axis(key[0])) for key in stage_keys]
        if stage == "xa1":
            continue
        if stage == "mixer0":
            small_parts = [_parts_from_full(grads[n], SHARD_AXIS[n]) for n in SMALL]
            send_keys.append("small")
            send_parts.append(jnp.stack([_pack([p[j] for p in small_parts], F32) for j in range(N_DEV)]))
        dh = start_exchange(stage, dh)
    d_res, d_proj = dh[0], _join_pieces_call(dh[1:], [rotated for _, _, rotated in pieces], tabs, "l0_w_in_dproj")[0]
    send_keys.append(first_keys[0])
    send_parts.append(_parts_from_full(_mm(d_proj, hn0, ta=True, name="l0_w_in_dw", tm=1408, tn=1024, tk=2048),
                                       _layer_shard_axis(first_keys[0][0])))
    d_res, d_proj = start_exchange("proj0", (d_res, d_proj))
    dx, dg0 = _mm_rms_bwd(d_proj, w_in, x[0], S['mix_norm'][0], d_res, "l0_w_in_da", wt=True)
    grads['mix_norm'] = grads['mix_norm'] + jnp.concatenate([dg0, jnp.zeros_like(dg0)], axis=0)
    rep_names = REPLICATED + ["loss"]
    grads["loss"] = loss_part
    zero = jnp.zeros((1, 1), F32)
    for d in (w_loc, m_loc, v_loc):
        d["loss"] = zero
    rep_started = _copies_start([_pack([grads[n] for n in rep_names], F32)], False, "rep_grads_start")

    parts, out = {}, {}

    def end_exchange(stage, started, ex_keys, after):
        _, lands = _copies_wait(started, list(range(len(ex_keys))), True, after, "grad_wait_" + stage)
        parts.update(zip(ex_keys, lands))

    def adamw(p, names, call_name):
        if len(names) == 1:
            n = names[0]
            res = _adamw_call(p, _shard_view(n, w_loc[n]), _shard_view(n, m_loc[n]), _shard_view(n, v_loc[n]), call_name)
            for kind, t in zip(("grad", "delta", "new_m", "new_v"), res):
                out[kind, n] = _shard_view(n, t)
        else:
            res = _adamw_call(p, *[_pack([d[n] for n in names], F32) for d in (w_loc, m_loc, v_loc)], call_name)
            for kind, buf in zip(("grad", "delta", "new_m", "new_v"), res):
                for n, t in zip(names, _unpack(buf, [w_loc[n].shape for n in names])):
                    out[kind, n] = t

    for ex in exchanges[:-1]:
        end_exchange(*ex, dx)
    last_names = {key[0] for key in exchanges[-1][2]}
    for n in BIG:
        if n not in last_names:
            adamw([parts[n, l] for l in range(w_loc[n].shape[0])], [n], "adamw_" + n)
    adamw(parts["small"], SMALL, "adamw_small")
    end_exchange(*exchanges[-1], [out["new_v", n] for n in BIG if n not in last_names])
    for n in BIG:
        if n in last_names:
            adamw([parts[n, l] for l in range(w_loc[n].shape[0])], [n], "adamw_" + n)
    _, rep_land = _copies_wait(rep_started, [0], False, out["new_v", "ab_w_in"], "rep_grads_wait")
    adamw(rep_land[0], rep_names, "adamw_replicated")
    loss = out["grad", "loss"][0, 0]

    return (loss, dx[None], *[out[kind, n] for kind in ("grad", "delta", "new_m", "new_v") for n in WEIGHT_NAMES])
```

```python
import functools
import math

import jax
import jax.numpy as jnp
from jax import lax
from jax.experimental import pallas as pl
from jax.experimental.pallas import tpu as pltpu

F32 = jnp.float32
BF16 = jnp.bfloat16
SDS = jax.ShapeDtypeStruct
BS = pl.BlockSpec

N_DEV = 8
NORM_EPS = 1e-6
ROPE_THETA = 500000.0
HEAD_DIM = 64
ROT_DIM = 16
ATT_BLOCK = 128
LRU_C = 8.0
LRU_HEADS = 4
DILATIONS = (1, 4, 16)
B_HEADS = 8
C_HEADS = 16
C_KV_HEADS = 2
XA_HEADS = 4
XA_HEAD_DIM = 128
NEG = -1e30
ADAM_LR, ADAM_B1, ADAM_B2, ADAM_EPS, ADAM_WD, ADAM_STEP = 0.001, 0.9, 0.999, 1e-08, 0.01, 10
LANES = 128
VMEM_LIMIT = 48 * 1024 * 1024

WEIGHT_NAMES = ['mix_norm', 'ab_w_in', 'lru_conv_w', 'lru_conv_b', 'lru_wa', 'lru_ba', 'lru_wx', 'lru_bx',
                'lru_lambda', 'ab_w_out', 'c_w_qkv', 'c_b_qkv', 'c_sinks', 'c_w_out', 'c_b_out', 'xa_norm',
                'xa_mem_norm', 'xa_wq', 'xa_wkv', 'xa_wo', 'ffn_norm', 'ffn_w_gate_up', 'ffn_w_down', 'final_norm']
SHARD_AXIS = {'ab_w_in': 2, 'lru_conv_w': 2, 'lru_wa': 2, 'lru_ba': 2, 'lru_wx': 2, 'lru_bx': 2, 'ab_w_out': 1,
              'c_w_qkv': 2, 'c_b_qkv': 1, 'c_w_out': 1, 'c_b_out': 1, 'xa_wq': 1, 'xa_wkv': 1, 'xa_wo': 2,
              'ffn_w_gate_up': 2, 'ffn_w_down': 1}
BIG = ['ab_w_in', 'lru_wa', 'lru_wx', 'ab_w_out', 'c_w_qkv', 'c_w_out', 'xa_wq', 'xa_wkv', 'xa_wo',
       'ffn_w_gate_up', 'ffn_w_down']
SMALL = ['lru_conv_w', 'lru_ba', 'lru_bx', 'c_b_qkv', 'c_b_out']
REPLICATED = [n for n in WEIGHT_NAMES if n not in SHARD_AXIS]
COLUMN_CUT = ('ab_w_in', 'c_w_qkv', 'ffn_w_gate_up')
PACK_QUANTUM = 2048


def _shard_view(name, t):
    return jnp.swapaxes(t, -1, -2) if name in COLUMN_CUT else t


def _layer_shard_axis(name):
    return 0 if name in COLUMN_CUT else SHARD_AXIS[name] - 1


def _pcall(body, **kw):
    return pl.pallas_call(body, **kw)


def _cparams(sem=None):
    return pltpu.CompilerParams(dimension_semantics=sem, vmem_limit_bytes=VMEM_LIMIT)


def _tile(n, target, mult=LANES):
    if n <= target:
        return n
    t = (target // mult) * mult
    while t >= mult:
        if n % t == 0:
            return t
        t -= mult
    return n


def _sigmoid(x):
    return 1.0 / (1.0 + jnp.exp(-x))


def _expm1(x):
    small = x * (1.0 + x * (0.5 + x * (1.0 / 6.0 + x * (1.0 / 24.0))))
    return jnp.where(jnp.abs(x) < 0.03, small, jnp.exp(x) - 1.0)


_GELU_C = math.sqrt(2.0 / math.pi)


def _gelu_parts(y):
    y2 = y * y
    th = jnp.tanh(_GELU_C * (y + 0.044715 * y * y2))
    g = 0.5 * y * (1.0 + th)
    dg = 0.5 * (1.0 + th) + 0.5 * y * (1.0 - th * th) * _GELU_C * (1.0 + 3.0 * 0.044715 * y2)
    return g, dg


def _rotate(xv, tab_refs, inverse=False):
    W = xv.shape[1]
    half = ROT_DIM // 2
    c, sa, sb = (jnp.tile(t[...], (1, W // LANES)) for t in tab_refs)
    if not inverse:
        return xv * c + pltpu.roll(xv, half, axis=1) * sa + pltpu.roll(xv, W - half, axis=1) * sb
    return xv * c + pltpu.roll(xv * sa, W - half, axis=1) + pltpu.roll(xv * sb, half, axis=1)


def _mm(a, b, *, ta=False, tb=False, bias=None, add=None, rope=None, name, tm=512, tn=512, tk=2048):
    M, K = (a.shape[1], a.shape[0]) if ta else a.shape
    N = b.shape[0] if tb else b.shape[1]
    tm, tn, tk = _tile(M, tm), _tile(N, tn), _tile(K, tk)
    nk = K // tk
    dn = (((0 if ta else 1,), (1 if tb else 0,)), ((), ()))

    def body(*refs):
        a_ref, b_ref = refs[0], refs[1]
        pos = 2
        bias_ref = add_ref = None
        if bias is not None:
            bias_ref = refs[pos]
            pos += 1
        if add is not None:
            add_ref = refs[pos]
            pos += 1
        tab_refs = refs[pos:pos + 3] if rope is not None else None
        pos += 3 if rope is not None else 0
        o_ref = refs[pos]
        part = lax.dot_general(a_ref[...].astype(BF16), b_ref[...].astype(BF16), dn, preferred_element_type=F32)

        def finish(r):
            if bias_ref is not None:
                r = r + bias_ref[...]
            if add_ref is not None:
                r = r + add_ref[...]
            o_ref[...] = r if tab_refs is None else _rotate(r, tab_refs)

        if nk == 1:
            finish(part)
            return
        acc_ref = refs[pos + 1]
        k = pl.program_id(2)

        @pl.when(k == 0)
        def _():
            acc_ref[...] = part

        @pl.when((k > 0) & (k < nk - 1))
        def _():
            acc_ref[...] += part

        @pl.when(k == nk - 1)
        def _():
            finish(acc_ref[...] + part)

    in_specs = [BS((tk, tm), lambda i, j, k: (k, i)) if ta else BS((tm, tk), lambda i, j, k: (i, k)),
                BS((tn, tk), lambda i, j, k: (j, k)) if tb else BS((tk, tn), lambda i, j, k: (k, j))]
    args = [a, b]
    if bias is not None:
        in_specs.append(BS((1, tn), lambda i, j, k: (0, j)))
        args.append(bias.reshape(1, N))
    if add is not None:
        in_specs.append(BS((tm, tn), lambda i, j, k: (i, j)))
        args.append(add)
    if rope is not None:
        in_specs += [BS((tm, LANES), lambda i, j, k: (i, 0))] * 3
        args += list(rope)
    return _pcall(body, name=name, grid=(M // tm, N // tn, nk), in_specs=in_specs,
                  out_specs=BS((tm, tn), lambda i, j, k: (i, j)), out_shape=SDS((M, N), F32),
                  scratch_shapes=[pltpu.VMEM((tm, tn), F32)] if nk > 1 else [],
                  compiler_params=_cparams(("parallel", "parallel", "arbitrary")))(*args)


def _mm_rms_bwd(dy, wb, x, g, add, name, wt=False):
    T, Kc = dy.shape
    Dm = wb.shape[1] if wt else wb.shape[0]
    tm, tk = _tile(T, 1024), _tile(Kc, 1408)
    nk = Kc // tk
    dn = (((1,), (0 if wt else 1,)), ((), ()))

    def body(*refs):
        dy_ref, w_ref, x_ref, g_ref = refs[:4]
        dx_ref, dg_ref, acc_ref = refs[-3:]
        i, k = pl.program_id(0), pl.program_id(1)
        part = lax.dot_general(dy_ref[...].astype(BF16), w_ref[...], dn, preferred_element_type=F32)

        @pl.when(k == 0)
        def _():
            acc_ref[...] = part

        @pl.when((k > 0) & (k < nk - 1))
        def _():
            acc_ref[...] += part

        @pl.when((i == 0) & (k == 0))
        def _():
            dg_ref[...] = jnp.zeros_like(dg_ref)

        @pl.when(k == nk - 1)
        def _():
            d = part if nk == 1 else acc_ref[...] + part
            xv = x_ref[...]
            r = lax.rsqrt(jnp.mean(xv * xv, axis=-1, keepdims=True) + NORM_EPS)
            xh = xv * r
            dyg = d * g_ref[...]
            dx = r * (dyg - xh * jnp.mean(dyg * xh, axis=-1, keepdims=True))
            dx_ref[...] = dx if add is None else dx + refs[4][...]
            dg_ref[...] += jnp.sum(d * xh, axis=0, keepdims=True)

    row = BS((tm, Dm), lambda i, k: (i, 0))
    vec = BS((1, Dm), lambda i, k: (0, 0))
    extra = [] if add is None else [add]
    return _pcall(body, name=name, grid=(T // tm, nk),
                  in_specs=[BS((tm, tk), lambda i, k: (i, k)),
                            BS((tk, Dm), lambda i, k: (k, 0)) if wt else BS((Dm, tk), lambda i, k: (0, k)), row, vec]
                  + [row] * len(extra),
                  out_specs=[row, vec], out_shape=[SDS((T, Dm), F32), SDS((1, Dm), F32)],
                  scratch_shapes=[pltpu.VMEM((tm, Dm), F32)],
                  compiler_params=_cparams(("arbitrary", "arbitrary")))(dy, wb, x, g.reshape(1, Dm), *extra)


def _colsum(x, name):
    T, N = x.shape
    tt = _tile(T, 1024, 8)

    def body(x_ref, o_ref):
        @pl.when(pl.program_id(0) == 0)
        def _():
            o_ref[...] = jnp.zeros_like(o_ref)

        o_ref[...] += jnp.sum(x_ref[...], axis=0, keepdims=True)

    return _pcall(body, name=name, grid=(T // tt,), in_specs=[BS((tt, N), lambda i: (i, 0))],
                  out_specs=BS((1, N), lambda i: (0, 0)), out_shape=SDS((1, N), F32),
                  compiler_params=_cparams(("arbitrary",)))(x)


@functools.partial(jax.custom_vjp, nondiff_argnums=(5,))
def linear(a, wb, wc, bias, add, name):
    return _linear_fwd(a, wb, wc, bias, add, name)[0]


def _linear_fwd(a, wb, wc, bias, add, name):
    out = _mm(a, wb, bias=bias, add=add, name=name, tm=1024, tn=1024, tk=1024)
    return out, (a, wb, bias is not None, add is not None)


def _linear_bwd(name, res, g):
    a, wb, has_bias, has_add = res
    da = _mm(g, wb, tb=True, name=name + "_da", tm=1024, tn=1024, tk=1024)
    dw = _mm(a, g, ta=True, name=name + "_dw", tm=1024, tn=1024, tk=1024)
    dbias = _colsum(g, name + "_db").reshape(-1) if has_bias else None
    return da, jnp.zeros_like(wb), dw, dbias, (g if has_add else None)


linear.defvjp(_linear_fwd, _linear_bwd)


def _rms_fwd_call(x, g, name, out_dtype=F32):
    T, Dm = x.shape
    tt = _tile(T, 1024, 16)

    def body(x_ref, g_ref, o_ref):
        xv = x_ref[...]
        r = lax.rsqrt(jnp.mean(xv * xv, axis=-1, keepdims=True) + NORM_EPS)
        o_ref[...] = ((xv * r) * g_ref[...]).astype(out_dtype)

    return _pcall(body, name=name, grid=(T // tt,),
                  in_specs=[BS((tt, Dm), lambda i: (i, 0)), BS((1, Dm), lambda i: (0, 0))],
                  out_specs=BS((tt, Dm), lambda i: (i, 0)), out_shape=SDS((T, Dm), out_dtype),
                  compiler_params=_cparams(("parallel",)))(x, g.reshape(1, Dm))


@functools.partial(jax.custom_vjp, nondiff_argnums=(5, 6))
def norm_linear(x, g, wb, wc, bias, wt, name):
    return _norm_linear_fwd(x, g, wb, wc, bias, wt, name)[0]


def _norm_linear_fwd(x, g, wb, wc, bias, wt, name):
    hn = _rms_fwd_call(x, g, name + "_norm", BF16)
    return (_mm(hn, wb, tb=wt, bias=bias, name=name, tm=1024, tn=1408, tk=1024), x), (x, g, hn, wb, bias is not None)


def _norm_linear_bwd(wt, name, res, cts):
    x, g, hn, wb, has_bias = res
    dy, dres = cts
    if wt:
        dw = _mm(dy, hn, ta=True, name=name + "_dw", tm=1408, tn=1024, tk=2048)
    else:
        dw = _mm(hn, dy, ta=True, name=name + "_dw", tm=1024, tn=1408, tk=1024)
    dx, dg = _mm_rms_bwd(dy, wb, x, g, dres, name + "_da", wt)
    dbias = _colsum(dy, name + "_db").reshape(-1) if has_bias else None
    return dx, dg.reshape(g.shape), jnp.zeros_like(wb), dw, dbias


norm_linear.defvjp(_norm_linear_fwd, _norm_linear_bwd)


def _rope_tables(T):
    half = ROT_DIM // 2
    inv = ROPE_THETA ** (-jnp.arange(0, ROT_DIM, 2, dtype=F32) / ROT_DIM)
    lane = jnp.arange(LANES) % HEAD_DIM
    freq = jnp.where(lane < ROT_DIM, inv[lane % half], 0.0)
    ang = jnp.arange(T, dtype=F32)[:, None] * freq[None, :]
    c, s = jnp.cos(ang), jnp.sin(ang)
    sa = jnp.where((lane >= half) & (lane < ROT_DIM), s, 0.0)
    sb = jnp.where(lane < half, -s, 0.0)
    return c, sa, sb


def _join_pieces_call(pieces, rotated, tabs, name):
    T = pieces[0].shape[0]
    widths = [p.shape[1] for p in pieces]
    W = sum(widths)
    tt = _tile(T, 512, 16)

    def body(*refs):
        tab_refs, o_ref, sum_ref = refs[len(pieces):len(pieces) + 3], refs[-2], refs[-1]

        @pl.when(pl.program_id(0) == 0)
        def _():
            sum_ref[...] = jnp.zeros_like(sum_ref)

        off = 0
        for p_ref, w, r in zip(refs, widths, rotated):
            piece = _rotate(p_ref[...], tab_refs, inverse=True) if r else p_ref[...]
            o_ref[:, off:off + w] = piece.astype(BF16)
            sum_ref[:, off:off + w] += jnp.sum(piece, axis=0, keepdims=True)
            off += w

    return _pcall(body, name=name, grid=(T // tt,),
                  in_specs=[BS((tt, w), lambda i: (i, 0)) for w in widths] + [BS((tt, LANES), lambda i: (i, 0))] * 3,
                  out_specs=[BS((tt, W), lambda i: (i, 0)), BS((1, W), lambda i: (0, 0))],
                  out_shape=[SDS((T, W), BF16), SDS((1, W), F32)],
                  compiler_params=_cparams(("arbitrary",)))(*pieces, *tabs)


@functools.partial(jax.custom_vjp, nondiff_argnums=(6, 7))
def norm_linear_pieces(x, g, wb, wc, bias, tabs, pieces, name):
    return _norm_linear_pieces_fwd(x, g, wb, wc, bias, tabs, pieces, name)[0]


def _norm_linear_pieces_fwd(x, g, wb, wc, bias, tabs, pieces, name):
    hn = _rms_fwd_call(x, g, name + "_norm", BF16)
    outs, row = [], 0
    for i, (width, rotated) in enumerate(pieces):
        outs.append(_mm(hn, wb[row:row + width], tb=True, bias=None if bias is None else bias[row:row + width],
                        rope=tabs if rotated else None, name=f"{name}_{i}", tm=2048, tn=1024, tk=1024))
        row += width
    return (*outs, x), (x, g, hn, wb, tabs, bias is not None)


def _norm_linear_pieces_bwd(pieces, name, res, cts):
    x, g, hn, wb, tabs, has_bias = res
    dy, dy_sum = _join_pieces_call(cts[:-1], [rotated for _, rotated in pieces], tabs, name + "_join")
    dw = _mm(dy, hn, ta=True, name=name + "_dw", tm=1408, tn=1024, tk=2048)
    dx, dg = _mm_rms_bwd(dy, wb, x, g, cts[-1], name + "_da", wt=True)
    dbias = dy_sum.reshape(-1) if has_bias else None
    return dx, dg.reshape(g.shape), jnp.zeros_like(wb), dw, dbias, jax.tree.map(jnp.zeros_like, tabs)


norm_linear_pieces.defvjp(_norm_linear_pieces_fwd, _norm_linear_pieces_bwd)


def _conv_fwd_call(x, w, b, name):
    T, C = x.shape
    tt = _tile(T, 1024, 8)
    per = tt // 8

    def body(x_ref, halo_ref, w_ref, b_ref, o_ref):
        i = pl.program_id(0)
        halo = jnp.where(i > 0, halo_ref[...], 0.0)
        e = jnp.concatenate([halo, x_ref[...]], axis=0)
        acc = b_ref[...]
        for k in (3, 2, 1):
            acc = acc + pltpu.roll(e, k, axis=0)[8:, :] * w_ref[3 - k:4 - k, :]
        o_ref[...] = acc + x_ref[...] * w_ref[3:4, :]

    return _pcall(body, name=name, grid=(T // tt,),
                  in_specs=[BS((tt, C), lambda i: (i, 0)), BS((8, C), lambda i: (jnp.maximum(i * per - 1, 0), 0)),
                            BS((4, C), lambda i: (0, 0)), BS((1, C), lambda i: (0, 0))],
                  out_specs=BS((tt, C), lambda i: (i, 0)), out_shape=SDS((T, C), F32),
                  compiler_params=_cparams(("parallel",)))(x, x, w, b.reshape(1, C))


def _conv_bwd_call(x, w, dy, name):
    T, C = x.shape
    tt = _tile(T, 1024, 8)
    per = tt // 8
    nt = T // tt

    def body(x_ref, halo_ref, w_ref, dy_ref, nxt_ref, dx_ref, dwb_ref):
        i = pl.program_id(0)
        halo = jnp.where(i > 0, halo_ref[...], 0.0)
        e = jnp.concatenate([halo, x_ref[...]], axis=0)
        dy = dy_ref[...]
        nxt = jnp.where(i < nt - 1, nxt_ref[...], 0.0)
        f = jnp.concatenate([dy, nxt], axis=0)
        dx = dy * w_ref[3:4, :]
        rows = [None] * 4
        rows[3] = jnp.sum(dy * x_ref[...], axis=0, keepdims=True)
        for k in (1, 2, 3):
            dx = dx + pltpu.roll(f, tt + 8 - k, axis=0)[:tt, :] * w_ref[3 - k:4 - k, :]
            rows[3 - k] = jnp.sum(dy * pltpu.roll(e, k, axis=0)[8:, :], axis=0, keepdims=True)
        dx_ref[...] = dx
        upd = jnp.concatenate(rows + [jnp.sum(dy, axis=0, keepdims=True), jnp.zeros((3, C), F32)], axis=0)

        @pl.when(i == 0)
        def _():
            dwb_ref[...] = jnp.zeros_like(dwb_ref)

        dwb_ref[...] += upd

    row = BS((tt, C), lambda i: (i, 0))
    return _pcall(body, name=name, grid=(nt,),
                  in_specs=[row, BS((8, C), lambda i: (jnp.maximum(i * per - 1, 0), 0)), BS((4, C), lambda i: (0, 0)),
                            row, BS((8, C), lambda i: (jnp.minimum((i + 1) * per, T // 8 - 1), 0))],
                  out_specs=[row, BS((8, C), lambda i: (0, 0))],
                  out_shape=[SDS((T, C), F32), SDS((8, C), F32)],
                  compiler_params=_cparams(("arbitrary",)))(x, x, w, dy, dy)


@functools.partial(jax.custom_vjp, nondiff_argnums=(3,))
def conv4(x, w, b, name):
    return _conv_fwd_call(x, w, b, name)


def _conv4_fwd(x, w, b, name):
    return _conv_fwd_call(x, w, b, name), (x, w)


def _conv4_bwd(name, res, dy):
    x, w = res
    dx, dwb = _conv_bwd_call(x, w, dy, name + "_bwd")
    return dx, dwb[0:4], dwb[4]


conv4.defvjp(_conv4_fwd, _conv4_bwd)


def _gates_fwd_call(xc, wa, ba, wx, bx, name):
    T, C = xc.shape
    hd = C // LRU_HEADS
    tt = _tile(T, 1024, 8)

    def body(x_ref, wa_ref, ba_ref, wx_ref, bx_ref, ga_ref, gx_ref):
        xb = x_ref[...].astype(BF16)
        ga_ref[...] = jnp.dot(xb, wa_ref[0].astype(BF16), preferred_element_type=F32) + ba_ref[...]
        gx_ref[...] = jnp.dot(xb, wx_ref[0].astype(BF16), preferred_element_type=F32) + bx_ref[...]

    blk = BS((tt, hd), lambda i, h: (i, h))
    wsp = BS((1, hd, hd), lambda i, h: (h, 0, 0))
    bsp = BS((1, hd), lambda i, h: (0, h))
    return _pcall(body, name=name, grid=(T // tt, LRU_HEADS), in_specs=[blk, wsp, bsp, wsp, bsp],
                  out_specs=[blk, blk], out_shape=[SDS((T, C), F32)] * 2,
                  compiler_params=_cparams(("parallel", "parallel")))(xc, wa, ba.reshape(1, C), wx, bx.reshape(1, C))


def _gates_bwd_x_call(dga, dgx, wa, wx, name):
    T, C = dga.shape
    hd = C // LRU_HEADS
    tt = _tile(T, 1024, 8)
    dn = (((1,), (1,)), ((), ()))

    def body(da_ref, dx_ref, wa_ref, wx_ref, o_ref):
        o_ref[...] = (lax.dot_general(da_ref[...].astype(BF16), wa_ref[0].astype(BF16), dn, preferred_element_type=F32)
                      + lax.dot_general(dx_ref[...].astype(BF16), wx_ref[0].astype(BF16), dn, preferred_element_type=F32))

    blk = BS((tt, hd), lambda i, h: (i, h))
    wsp = BS((1, hd, hd), lambda i, h: (h, 0, 0))
    return _pcall(body, name=name, grid=(T // tt, LRU_HEADS), in_specs=[blk, blk, wsp, wsp], out_specs=blk,
                  out_shape=SDS((T, C), F32), compiler_params=_cparams(("parallel", "parallel")))(dga, dgx, wa, wx)


def _gates_bwd_w_call(xc, dga, dgx, name):
    T, C = xc.shape
    hd = C // LRU_HEADS
    tt = _tile(T, 1024, 8)
    dn = (((0,), (0,)), ((), ()))

    def body(x_ref, da_ref, dx_ref, dwa_ref, dwx_ref, dba_ref, dbx_ref):
        @pl.when(pl.program_id(1) == 0)
        def _():
            dwa_ref[...] = jnp.zeros_like(dwa_ref)
            dwx_ref[...] = jnp.zeros_like(dwx_ref)
            dba_ref[...] = jnp.zeros_like(dba_ref)
            dbx_ref[...] = jnp.zeros_like(dbx_ref)

        xb = x_ref[...].astype(BF16)
        da, dx = da_ref[...], dx_ref[...]
        dwa_ref[0] += lax.dot_general(xb, da.astype(BF16), dn, preferred_element_type=F32)
        dwx_ref[0] += lax.dot_general(xb, dx.astype(BF16), dn, preferred_element_type=F32)
        dba_ref[...] += jnp.sum(da, axis=0, keepdims=True)
        dbx_ref[...] += jnp.sum(dx, axis=0, keepdims=True)

    blk = BS((tt, hd), lambda h, i: (i, h))
    wsp = BS((1, hd, hd), lambda h, i: (h, 0, 0))
    bsp = BS((1, hd), lambda h, i: (0, h))
    return _pcall(body, name=name, grid=(LRU_HEADS, T // tt), in_specs=[blk, blk, blk],
                  out_specs=[wsp, wsp, bsp, bsp],
                  out_shape=[SDS((LRU_HEADS, hd, hd), F32)] * 2 + [SDS((1, C), F32)] * 2,
                  compiler_params=_cparams(("parallel", "arbitrary")))(xc, dga, dgx)


@functools.partial(jax.custom_vjp, nondiff_argnums=(7,))
def lru_gates(xc, wa, wa_c, ba, wx, wx_c, bx, name):
    return tuple(_gates_fwd_call(xc, wa, ba, wx, bx, name))


def _lru_gates_fwd(xc, wa, wa_c, ba, wx, wx_c, bx, name):
    return tuple(_gates_fwd_call(xc, wa, ba, wx, bx, name)), (xc, wa, wx, ba.shape)


def _lru_gates_bwd(name, res, g):
    xc, wa, wx, bshape = res
    dga, dgx = g
    dxc = _gates_bwd_x_call(dga, dgx, wa, wx, name + "_dx")
    dwa, dwx, dba, dbx = _gates_bwd_w_call(xc, dga, dgx, name + "_dw")
    return dxc, jnp.zeros_like(wa), dwa, dba.reshape(bshape), jnp.zeros_like(wx), dwx, dbx.reshape(bshape)


lru_gates.defvjp(_lru_gates_fwd, _lru_gates_bwd)


def _lru_coeffs(xc, ga, gx, lam):
    r = _sigmoid(ga)
    ig = _sigmoid(gx)
    z = -lam
    sp = jnp.maximum(z, 0.0) + jnp.log(1.0 + jnp.exp(-jnp.abs(z)))
    la = -LRU_C * r * sp
    a = jnp.exp(la)
    s = jnp.sqrt(-_expm1(2.0 * la))
    return r, ig, sp, a, s


LRU_TT = 256


def _scan_fwd_call(xc, ga, gx, y, lam, name):
    T, C = xc.shape
    tt = _tile(T, LRU_TT, 8)

    def body(xc_ref, ga_ref, gx_ref, y_ref, lam_ref, h_ref, rec_ref, a_buf, carry):
        @pl.when(pl.program_id(0) == 0)
        def _():
            carry[...] = jnp.zeros_like(carry)

        xcv = xc_ref[...]
        _, ig, _, a, s = _lru_coeffs(xcv, ga_ref[...], gx_ref[...], lam_ref[...])
        a_buf[...] = a
        h_ref[...] = s * (ig * xcv)

        def step(t, h):
            hn = a_buf[pl.ds(t, 1), :] * h + h_ref[pl.ds(t, 1), :]
            h_ref[pl.ds(t, 1), :] = hn
            return hn

        carry[0:1, :] = lax.fori_loop(0, tt, step, carry[0:1, :], unroll=8)
        g, _ = _gelu_parts(y_ref[...])
        rec_ref[...] = h_ref[...] * g

    row = BS((tt, C), lambda i: (i, 0))
    vec = BS((1, C), lambda i: (0, 0))
    return _pcall(body, name=name, grid=(T // tt,), in_specs=[row, row, row, row, vec], out_specs=[row, row],
                  out_shape=[SDS((T, C), F32)] * 2,
                  scratch_shapes=[pltpu.VMEM((tt, C), F32), pltpu.VMEM((8, C), F32)],
                  compiler_params=_cparams(("arbitrary",)))(xc, ga, gx, y, lam.reshape(1, C))


def _scan_bwd_call(xc, ga, gx, y, lam, h, drec, name):
    T, C = xc.shape
    tt = _tile(T, LRU_TT, 8)
    nt = T // tt
    per = tt // 8

    def body(xc_ref, ga_ref, gx_ref, y_ref, lam_ref, h_ref, halo_ref, dr_ref,
             dga_ref, dgx_ref, dxc_ref, dy_ref, dlam_ref, a_buf, g_buf, carry):
        i = pl.program_id(0)

        @pl.when(i == 0)
        def _():
            carry[...] = jnp.zeros_like(carry)
            dlam_ref[...] = jnp.zeros_like(dlam_ref)

        xcv, lam = xc_ref[...], lam_ref[...]
        r, ig, sp, a, s = _lru_coeffs(xcv, ga_ref[...], gx_ref[...], lam)
        gel, dgel = _gelu_parts(y_ref[...])
        drec = dr_ref[...]
        hv = h_ref[...]
        dy_ref[...] = drec * hv * dgel
        a_buf[...] = a
        g_buf[...] = drec * gel

        def step(j, q):
            t = tt - 1 - j
            g = g_buf[pl.ds(t, 1), :] + q
            g_buf[pl.ds(t, 1), :] = g
            return a_buf[pl.ds(t, 1), :] * g

        carry[0:1, :] = lax.fori_loop(0, tt, step, carry[0:1, :], unroll=8)
        g = g_buf[...]
        halo = jnp.where(i < nt - 1, halo_ref[...], 0.0)
        hprev = pltpu.roll(jnp.concatenate([halo, hv], axis=0), 1, axis=0)[8:, :]
        da = g * hprev
        dig = g * s * xcv
        ds = g * ig * xcv
        dla = da * a - ds * (a * a) / s
        dga_ref[...] = dla * (-LRU_C * sp) * r * (1.0 - r)
        dgx_ref[...] = dig * ig * (1.0 - ig)
        dxc_ref[...] = g * s * ig
        dlam_ref[...] += jnp.sum(dla * r, axis=0, keepdims=True) * (LRU_C * _sigmoid(-lam))

    row = BS((tt, C), lambda i: (nt - 1 - i, 0))
    vec = BS((1, C), lambda i: (0, 0))
    halo = BS((8, C), lambda i: (jnp.maximum((nt - 1 - i) * per - 1, 0), 0))
    return _pcall(body, name=name, grid=(nt,), in_specs=[row, row, row, row, vec, row, halo, row],
                  out_specs=[row, row, row, row, vec], out_shape=[SDS((T, C), F32)] * 4 + [SDS((1, C), F32)],
                  scratch_shapes=[pltpu.VMEM((tt, C), F32), pltpu.VMEM((tt, C), F32), pltpu.VMEM((8, C), F32)],
                  compiler_params=_cparams(("arbitrary",)))(xc, ga, gx, y, lam.reshape(1, C), h, h, drec)


@functools.partial(jax.custom_vjp, nondiff_argnums=(5,))
def lru_scan(xc, ga, gx, y, lam, name):
    return _scan_fwd_call(xc, ga, gx, y, lam, name)[1]


def _lru_scan_fwd(xc, ga, gx, y, lam, name):
    h, rec = _scan_fwd_call(xc, ga, gx, y, lam, name)
    return rec, (xc, ga, gx, y, lam, h)


def _lru_scan_bwd(name, res, drec):
    xc, ga, gx, y, lam, h = res
    dga, dgx, dxc, dy, dlam = _scan_bwd_call(xc, ga, gx, y, lam, h, drec, name + "_bwd")
    return dxc, dga, dgx, dy, dlam.reshape(lam.shape)


lru_scan.defvjp(_lru_scan_fwd, _lru_scan_bwd)


def _att_batch(d, shared=False):
    if shared:
        return 8, 1
    return (4, 1) if d == 1 else (1, min(d, 8))


def _att_masks(n, max_dist):
    qi = lax.broadcasted_iota(jnp.int32, (1, 2 * ATT_BLOCK, 2 * ATT_BLOCK), 1) & (ATT_BLOCK - 1)
    kj = lax.broadcasted_iota(jnp.int32, (1, 2 * ATT_BLOCK, 2 * ATT_BLOCK), 2)
    prev = (kj < ATT_BLOCK) & (kj >= qi + (ATT_BLOCK - max_dist)) & (n > 0)
    cur = (kj >= ATT_BLOCK) & (kj - ATT_BLOCK <= qi)
    return prev | cur


def _lo_lanes(rows):
    return lax.broadcasted_iota(jnp.int32, (rows, LANES), 1) < HEAD_DIM


def _lane_half(rows):
    return lax.broadcasted_iota(jnp.int32, (rows, LANES), 1) // HEAD_DIM


def _stack_heads(x2):
    lo = _lo_lanes(ATT_BLOCK)
    zero = jnp.zeros_like(x2)
    return jnp.concatenate([jnp.where(lo, x2, zero), jnp.where(lo, zero, x2)], axis=0)


def _unstack_heads(y):
    return jnp.where(_lo_lanes(ATT_BLOCK), y[:ATT_BLOCK], y[ATT_BLOCK:])


def _per_head_col(x2):
    return jnp.concatenate([x2[:, 0:1], x2[:, HEAD_DIM:HEAD_DIM + 1]], axis=0)


def _head_sums(x2):
    lo = _lo_lanes(ATT_BLOCK)
    return jnp.concatenate([jnp.sum(jnp.where(lo, x2, 0.0), axis=1, keepdims=True),
                            jnp.sum(jnp.where(lo, 0.0, x2), axis=1, keepdims=True)], axis=0)


def _att_specs(d, Wq, Wk, nb, clamp):
    shared = Wk != Wq
    cgw, sb = _att_batch(d, shared)
    cur = (lambda n: jnp.minimum(n, nb - 1)) if clamp else (lambda n: n)
    rows, qw, kw = ATT_BLOCK * d, cgw * LANES, (LANES if shared else cgw * LANES)
    kcol = (lambda g: 0) if shared else (lambda g: g)
    qsp = BS((rows, qw), lambda g, n: (cur(n), g))
    csp = BS((rows, kw), lambda g, n: (cur(n), kcol(g)))
    psp = BS((rows, kw), lambda g, n: (jnp.maximum(cur(n) - 1, 0), kcol(g)))
    return cgw, sb, shared, qsp, csp, psp, qw, kw


def _att_streams(d, sb, work):
    if d == 1:
        work([slice(None)])
        return

    def one(j, carry):
        work([pl.ds(j * sb + i, ATT_BLOCK, stride=d) for i in range(sb)])
        return carry

    lax.fori_loop(0, d // sb, one, 0)


def _att_problem_loads(rows, cgw, shared, g, q_ref, kc_ref, kp_ref, vc_ref, vp_ref, sk_ref):
    half = _lane_half(ATT_BLOCK)

    def kv(ref, r, p):
        x = ref[r, :]
        if not shared:
            return x[:, p * LANES:(p + 1) * LANES]
        return jnp.where(half == p // 4, x, pltpu.roll(x, HEAD_DIM, axis=1))

    qs, kb, vb, sk = [], [], [], []
    for r in rows:
        qrow = q_ref[r, :]
        for p in range(cgw):
            cols = slice(p * LANES, (p + 1) * LANES)
            qs.append(_stack_heads(qrow[:, cols].astype(BF16)))
            kb.append(jnp.concatenate([kv(kp_ref, r, p), kv(kc_ref, r, p)], axis=0).astype(BF16))
            vb.append(jnp.concatenate([kv(vp_ref, r, p), kv(vc_ref, r, p)], axis=0).astype(BF16))
            sk.append(_per_head_col(jnp.broadcast_to(sk_ref[:, cols], (ATT_BLOCK, LANES))))
    return jnp.stack(qs), jnp.stack(kb), jnp.stack(vb), jnp.stack(sk)


_BDOT_NT = (((2,), (2,)), ((0,), (0,)))
_BDOT_NN = (((2,), (1,)), ((0,), (0,)))
_BDOT_TN = (((1,), (1,)), ((0,), (0,)))


def _att_fwd_call(q, k, v, sinks, d, max_dist, name):
    T, Wq = q.shape
    Wk = k.shape[1]
    nb = T // (d * ATT_BLOCK)
    cgw, sb, shared, qsp, csp, psp, qw, kw = _att_specs(d, Wq, Wk, nb, False)
    G = Wq // qw
    assert not shared or (d == 1 and Wk == LANES and G == 1 and cgw == 8), "a shared kv pair serves 2 x 8 query heads"

    def body(q_ref, kc_ref, kp_ref, vc_ref, vp_ref, sk_ref, o_ref, lse_ref):
        g, n = pl.program_id(0), pl.program_id(1)

        def work(rows):
            qs, kband, vband, sk = _att_problem_loads(rows, cgw, shared, g, q_ref, kc_ref, kp_ref, vc_ref, vp_ref, sk_ref)
            s = lax.dot_general(qs, kband, _BDOT_NT, preferred_element_type=F32) * (HEAD_DIM ** -0.5)
            s = jnp.where(_att_masks(n, max_dist), s, NEG)
            m = jnp.maximum(jnp.max(s, axis=-1, keepdims=True), sk)
            e = jnp.exp(s - m)
            den = jnp.sum(e, axis=-1, keepdims=True) + jnp.exp(sk - m)
            o = lax.dot_general((e * (1.0 / den)).astype(BF16), vband, _BDOT_NN, preferred_element_type=F32)
            lse = jnp.broadcast_to(m + jnp.log(den), o.shape)
            for i, r in enumerate(rows):
                o_ref[r, :] = jnp.concatenate([_unstack_heads(o[i * cgw + p]) for p in range(cgw)], axis=1)
                lse_ref[r, :] = jnp.concatenate([_unstack_heads(lse[i * cgw + p]) for p in range(cgw)], axis=1)

        _att_streams(d, sb, work)

    sksp = BS((1, qw), lambda g, n: (0, g))
    return _pcall(body, name=name, grid=(G, nb), in_specs=[qsp, csp, psp, csp, psp, sksp], out_specs=[qsp, qsp],
                  out_shape=[SDS((T, Wq), F32)] * 2,
                  compiler_params=_cparams(("parallel", "parallel")))(q, k, k, v, v, sinks)


def _att_bwd_call(q, k, v, sinks, o, lse, do, dlse, d, max_dist, name):
    T, Wq = q.shape
    Wk = k.shape[1]
    nb = T // (d * ATT_BLOCK)
    cgw, sb, shared, qsp, csp, psp, qw, kw = _att_specs(d, Wq, Wk, nb, True)
    G = Wq // qw
    scale = HEAD_DIM ** -0.5

    def body(*refs):
        q_ref, kc_ref, kp_ref, vc_ref, vp_ref, sk_ref, o_ref, lse_ref, do_ref = refs[:9]
        dlse_ref = refs[9] if dlse is not None else None
        dq_ref, dk_ref, dv_ref, dsk_ref, ck, cv = refs[-6:]
        g, n = pl.program_id(0), pl.program_id(1)

        @pl.when(n == 0)
        def _():
            ck[...] = jnp.zeros_like(ck)
            cv[...] = jnp.zeros_like(cv)
            dsk_ref[...] = jnp.zeros_like(dsk_ref)

        def work(rows):
            qs, kband, vband, sk = _att_problem_loads(rows, cgw, shared, g, q_ref, kc_ref, kp_ref, vc_ref, vp_ref, sk_ref)
            dos, lse_c, corr = [], [], []
            for r in rows:
                do_r, o_r, lse_r = do_ref[r, :], o_ref[r, :], lse_ref[r, :]
                dlse_r = dlse_ref[r, :] if dlse_ref is not None else None
                for p in range(cgw):
                    cols = slice(p * LANES, (p + 1) * LANES)
                    dos.append(_stack_heads(do_r[:, cols].astype(BF16)))
                    lse_c.append(_per_head_col(lse_r[:, cols]))
                    delta = _head_sums(do_r[:, cols] * o_r[:, cols])
                    corr.append(-delta if dlse_r is None else _head_sums(dlse_r[:, cols]) - delta)
            dos, lse_c, corr = jnp.stack(dos), jnp.stack(lse_c), jnp.stack(corr)
            s = lax.dot_general(qs, kband, _BDOT_NT, preferred_element_type=F32) * scale
            pr = jnp.exp(jnp.where(_att_masks(n, max_dist), s, NEG) - lse_c)
            dp = lax.dot_general(dos, vband, _BDOT_NT, preferred_element_type=F32)
            ds = (pr * (dp + corr)).astype(BF16)
            dq = lax.dot_general(ds, kband, _BDOT_NN, preferred_element_type=F32) * scale
            dkb = lax.dot_general(ds, qs, _BDOT_TN, preferred_element_type=F32) * scale
            dvb = lax.dot_general(pr.astype(BF16), dos, _BDOT_TN, preferred_element_type=F32)
            dsk = jnp.exp(sk - lse_c) * corr
            lane = lax.broadcasted_iota(jnp.int32, (8, LANES), 1)
            for p in range(cgw):
                tot = [jnp.sum(jnp.stack([dsk[i * cgw + p, h * ATT_BLOCK:(h + 1) * ATT_BLOCK] for i in range(len(rows))]),
                               axis=(0, 1)).reshape(1, 1) for h in range(2)]
                dsk_ref[:, p * LANES:(p + 1) * LANES] += jnp.where(lane == 0, tot[0], jnp.where(lane == HEAD_DIM, tot[1], 0.0))

            def gather_pairs(parts):
                if not shared:
                    return jnp.concatenate(parts, axis=1)
                tot = [parts[4 * h] + parts[4 * h + 1] + parts[4 * h + 2] + parts[4 * h + 3] for h in range(2)]
                tot = [t + pltpu.roll(t, HEAD_DIM, axis=1) for t in tot]
                return jnp.where(_lo_lanes(ATT_BLOCK), tot[0], tot[1])

            for i, r in enumerate(rows):
                mine = range(i * cgw, (i + 1) * cgw)
                dq_ref[r, :] = jnp.concatenate([_unstack_heads(dq[b]) for b in mine], axis=1)
                dk_ref[r, :] = ck[r, :] + gather_pairs([dkb[b, :ATT_BLOCK] for b in mine])
                dv_ref[r, :] = cv[r, :] + gather_pairs([dvb[b, :ATT_BLOCK] for b in mine])
                ck[r, :] = gather_pairs([dkb[b, ATT_BLOCK:] for b in mine])
                cv[r, :] = gather_pairs([dvb[b, ATT_BLOCK:] for b in mine])

        @pl.when(n < nb)
        def _():
            _att_streams(d, sb, work)

        @pl.when(n == nb)
        def _():
            dk_ref[...] = ck[...]
            dv_ref[...] = cv[...]

    sksp = BS((1, qw), lambda g, n: (0, g))
    rows = ATT_BLOCK * d
    osp = BS((rows, kw), lambda g, n: (jnp.maximum(n - 1, 0), 0 if shared else g))
    kshape = SDS((T, Wk), F32)
    dq, dk, dv, dsk = _pcall(
        body, name=name, grid=(G, nb + 1),
        in_specs=[qsp, csp, psp, csp, psp, sksp, qsp, qsp, qsp] + ([qsp] if dlse is not None else []),
        out_specs=[qsp, osp, osp, BS((8, qw), lambda g, n: (0, g))],
        out_shape=[SDS((T, Wq), F32), kshape, kshape, SDS((8, Wq), F32)],
        scratch_shapes=[pltpu.VMEM((rows, kw), F32)] * 2,
        compiler_params=_cparams(("parallel", "arbitrary")))(q, k, k, v, v, sinks, o, lse, do,
                                                              *([dlse] if dlse is not None else []))
    return dq, dk, dv, dsk[0:1]


@functools.partial(jax.custom_vjp, nondiff_argnums=(4, 5, 6, 7))
def band_attention(q, k, v, sinks, d, max_dist, with_lse, name):
    return _band_attention_fwd(q, k, v, sinks, d, max_dist, with_lse, name)[0]


def _band_attention_fwd(q, k, v, sinks, d, max_dist, with_lse, name):
    o, lse = _att_fwd_call(q, k, v, sinks, d, max_dist, name)
    return ((o, lse) if with_lse else o), (q, k, v, sinks, o, lse)


def _band_attention_bwd(d, max_dist, with_lse, name, res, g):
    q, k, v, sinks, o, lse = res
    do, dlse = g if with_lse else (g, None)
    return _att_bwd_call(q, k, v, sinks, o, lse, do, dlse, d, max_dist, name + "_bwd")


band_attention.defvjp(_band_attention_fwd, _band_attention_bwd)


def _merge_weights(ls):
    mx = jnp.maximum(jnp.maximum(ls[0], ls[1]), ls[2])
    es = [jnp.exp(l - mx) for l in ls]
    inv = 1.0 / (es[0] + es[1] + es[2])
    return [e * inv for e in es]


def _merge_fwd_call(os_, ls_, name):
    T, W = os_[0].shape
    tt = _tile(T, 512, 8)

    def body(o1, o2, o3, l1, l2, l3, out_ref):
        w = _merge_weights([l1[...], l2[...], l3[...]])
        out_ref[...] = w[0] * o1[...] + w[1] * o2[...] + w[2] * o3[...]

    row = BS((tt, W), lambda i: (i, 0))
    return _pcall(body, name=name, grid=(T // tt,), in_specs=[row] * 6, out_specs=row,
                  out_shape=SDS((T, W), F32), compiler_params=_cparams(("parallel",)))(*os_, *ls_)


def _merge_bwd_call(os_, ls_, do, name):
    T, W = os_[0].shape
    tt = _tile(T, 512, 8)

    def body(o1, o2, o3, l1, l2, l3, do_ref, d1, d2, d3, e1, e2, e3):
        w = _merge_weights([l1[...], l2[...], l3[...]])
        dov = do_ref[...]
        ts = [dov * o[...] for o in (o1, o2, o3)]
        mean = w[0] * ts[0] + w[1] * ts[1] + w[2] * ts[2]
        for wi, ti, dref, eref in zip(w, ts, (d1, d2, d3), (e1, e2, e3)):
            dref[...] = wi * dov
            eref[...] = wi * (ti - mean)

    row = BS((tt, W), lambda i: (i, 0))
    return _pcall(body, name=name, grid=(T // tt,), in_specs=[row] * 7, out_specs=[row] * 6,
                  out_shape=[SDS((T, W), F32)] * 6, compiler_params=_cparams(("parallel",)))(*os_, *ls_, do)


@functools.partial(jax.custom_vjp, nondiff_argnums=(2,))
def merge3(os_, ls_, name):
    return _merge_fwd_call(os_, ls_, name)


def _merge3_fwd(os_, ls_, name):
    return _merge_fwd_call(os_, ls_, name), (os_, ls_)


def _merge3_bwd(name, res, do):
    os_, ls_ = res
    out = _merge_bwd_call(os_, ls_, do, name + "_bwd")
    return tuple(out[:3]), tuple(out[3:])


merge3.defvjp(_merge3_fwd, _merge3_bwd)


def _xa_probs(qb, kb, scale):
    s = lax.dot_general(qb, kb, (((1,), (1,)), ((), ())), preferred_element_type=F32) * scale
    e = jnp.exp(s - jnp.max(s, axis=-1, keepdims=True))
    return e / jnp.sum(e, axis=-1, keepdims=True)


def _xa_fwd_call(q, kv, name):
    T, W = q.shape
    M = kv.shape[0]
    hd = XA_HEAD_DIM
    tq = _tile(T, 2048, 8)
    scale = hd ** -0.5

    def body(q_ref, k_ref, v_ref, o_ref):
        p = _xa_probs(q_ref[...].astype(BF16), k_ref[...].astype(BF16), scale)
        o_ref[...] = jnp.dot(p.astype(BF16), v_ref[...].astype(BF16), preferred_element_type=F32)

    qsp = BS((tq, hd), lambda i, h: (i, h))
    return _pcall(body, name=name, grid=(T // tq, XA_HEADS),
                  in_specs=[qsp, BS((M, hd), lambda i, h: (0, h)), BS((M, hd), lambda i, h: (0, XA_HEADS + h))],
                  out_specs=qsp, out_shape=SDS((T, W), F32),
                  compiler_params=_cparams(("parallel", "parallel")))(q, kv, kv)


def _xa_bwd_call(q, kv, do, name):
    T, W = q.shape
    M = kv.shape[0]
    hd = XA_HEAD_DIM
    tq = _tile(T, 2048, 8)
    scale = hd ** -0.5
    dn_nt = (((1,), (1,)), ((), ()))
    dn_tn = (((0,), (0,)), ((), ()))

    def body(q_ref, k_ref, v_ref, do_ref, dq_ref, dk_ref, dv_ref):
        @pl.when(pl.program_id(1) == 0)
        def _():
            dk_ref[...] = jnp.zeros_like(dk_ref)
            dv_ref[...] = jnp.zeros_like(dv_ref)

        qb, kb, vb = q_ref[...].astype(BF16), k_ref[...].astype(BF16), v_ref[...].astype(BF16)
        p = _xa_probs(qb, kb, scale)
        dob = do_ref[...].astype(BF16)
        dp = lax.dot_general(dob, vb, dn_nt, preferred_element_type=F32)
        ds = (p * (dp - jnp.sum(p * dp, axis=-1, keepdims=True))).astype(BF16)
        dq_ref[...] = jnp.dot(ds, kb, preferred_element_type=F32) * scale
        dk_ref[...] += lax.dot_general(ds, qb, dn_tn, preferred_element_type=F32) * scale
        dv_ref[...] += lax.dot_general(p.astype(BF16), dob, dn_tn, preferred_element_type=F32)

    qsp = BS((tq, hd), lambda h, i: (i, h))
    ksp = BS((M, hd), lambda h, i: (0, h))
    return _pcall(body, name=name, grid=(XA_HEADS, T // tq),
                  in_specs=[qsp, ksp, BS((M, hd), lambda h, i: (0, XA_HEADS + h)), qsp],
                  out_specs=[qsp, ksp, ksp], out_shape=[SDS((T, W), F32), SDS((M, W), F32), SDS((M, W), F32)],
                  compiler_params=_cparams(("parallel", "arbitrary")))(q, kv, kv, do)


@functools.partial(jax.custom_vjp, nondiff_argnums=(2,))
def cross_attention(q, kv, name):
    return _xa_fwd_call(q, kv, name)


def _cross_attention_fwd(q, kv, name):
    return _xa_fwd_call(q, kv, name), (q, kv)


def _cross_attention_bwd(name, res, do):
    q, kv = res
    dq, dk, dv = _xa_bwd_call(q, kv, do, name + "_bwd")
    return dq, jnp.concatenate([dk, dv], axis=1)


cross_attention.defvjp(_cross_attention_fwd, _cross_attention_bwd)


def _gate_up_swiglu_call(hn, w1t, name):
    T, K = hn.shape
    F = w1t.shape[0] // 2
    tm, tn = _tile(T, 2048), _tile(F, 256)
    nj = F // tn
    dn = (((1,), (1,)), ((), ()))

    def body(a_ref, wg_ref, wu_ref, g_ref, u_ref, act_ref):
        a = a_ref[...]
        g = lax.dot_general(a, wg_ref[...], dn, preferred_element_type=F32)
        u = lax.dot_general(a, wu_ref[...], dn, preferred_element_type=F32)
        g_ref[...] = g
        u_ref[...] = u
        act_ref[...] = ((g * _sigmoid(g)) * u).astype(BF16)

    tile = BS((tm, tn), lambda i, j: (i, j))
    return _pcall(body, name=name, grid=(T // tm, nj),
                  in_specs=[BS((tm, K), lambda i, j: (i, 0)), BS((tn, K), lambda i, j: (j, 0)),
                            BS((tn, K), lambda i, j: (j + nj, 0))],
                  out_specs=[tile, tile, tile], out_shape=[SDS((T, F), F32), SDS((T, F), F32), SDS((T, F), BF16)],
                  compiler_params=_cparams(("parallel", "parallel")))(hn, w1t, w1t)


def _swiglu_bwd_call(g, u, dact, name):
    T, F = g.shape
    tt = _tile(T, 256, 16)

    def body(g_ref, u_ref, d_ref, o_ref):
        g, d = g_ref[...], d_ref[...]
        sg = _sigmoid(g)
        o_ref[:, :F] = (d * u_ref[...] * (sg * (1.0 + g * (1.0 - sg)))).astype(BF16)
        o_ref[:, F:] = (d * (g * sg)).astype(BF16)

    row = BS((tt, F), lambda i: (i, 0))
    return _pcall(body, name=name, grid=(T // tt,), in_specs=[row, row, row],
                  out_specs=BS((tt, 2 * F), lambda i: (i, 0)), out_shape=SDS((T, 2 * F), BF16),
                  compiler_params=_cparams(("parallel",)))(g, u, dact)


@functools.partial(jax.custom_vjp, nondiff_argnums=(6,))
def ffn_block(h, g, w1b, w1c, w2b, w2c, name):
    return _ffn_fwd(h, g, w1b, w1c, w2b, w2c, name)[0]


def _ffn_fwd(h, g, w1b, w1c, w2b, w2c, name):
    hn = _rms_fwd_call(h, g, name + "_norm", BF16)
    gate, up, act = _gate_up_swiglu_call(hn, w1b, name + "_gu")
    out = _mm(act, w2b, add=h, name=name + "_down", tm=1024, tn=1024, tk=2816)
    return out, (h, g, hn, gate, up, act, w1b, w2b)


def _ffn_bwd(name, res, dout):
    h, g, hn, gate, up, act, w1b, w2b = res
    dact = _mm(dout, w2b, tb=True, name=name + "_down_da", tm=1024, tn=1408, tk=1024)
    dw2 = _mm(act, dout, ta=True, name=name + "_down_dw", tm=1408, tn=1024, tk=1024)
    dgu = _swiglu_bwd_call(gate, up, dact, name + "_swiglu_bwd")
    dw1 = _mm(dgu, hn, ta=True, name=name + "_gu_dw", tm=1408, tn=1024, tk=2048)
    dh, dg = _mm_rms_bwd(dgu, w1b, h, g, dout, name + "_gu_da", wt=True)
    return dh, dg.reshape(g.shape), jnp.zeros_like(w1b), dw1, jnp.zeros_like(w2b), dw2


ffn_block.defvjp(_ffn_fwd, _ffn_bwd)


def _final_call(h, g, target, name):
    T, Dm = h.shape
    tt = _tile(T, 1024, 8)

    def body(x_ref, g_ref, t_ref, loss_ref, dx_ref, dg_ref):
        @pl.when(pl.program_id(0) == 0)
        def _():
            loss_ref[...] = jnp.zeros_like(loss_ref)
            dg_ref[...] = jnp.zeros_like(dg_ref)

        xv, gv = x_ref[...], g_ref[...]
        r = lax.rsqrt(jnp.mean(xv * xv, axis=-1, keepdims=True) + NORM_EPS)
        xh = xv * r
        err = xh * gv - t_ref[...]
        loss_ref[...] += 0.5 * jnp.sum(jnp.mean(err * err, axis=-1, keepdims=True), axis=0, keepdims=True)
        dy = err * (1.0 / Dm)
        dyg = dy * gv
        dx_ref[...] = r * (dyg - xh * jnp.mean(dyg * xh, axis=-1, keepdims=True))
        dg_ref[...] += jnp.sum(dy * xh, axis=0, keepdims=True)

    row = BS((tt, Dm), lambda i: (i, 0))
    vec = BS((1, Dm), lambda i: (0, 0))
    return _pcall(body, name=name, grid=(T // tt,), in_specs=[row, vec, row],
                  out_specs=[BS((1, 1), lambda i: (0, 0)), row, vec],
                  out_shape=[SDS((1, 1), F32), SDS((T, Dm), F32), SDS((1, Dm), F32)],
                  compiler_params=_cparams(("arbitrary",)))(h, g.reshape(1, Dm), target)


ADAMW_BLOCK_ELEMS = 256 * 1024


def _adamw_call(parts, w, m, v, name):
    shape = w.shape
    if not isinstance(parts, (list, tuple)):
        parts, shape3 = [parts], (1,) + shape
    else:
        shape3 = shape
    n_lead = shape3[0]
    r, N = shape3[-2], shape3[-1]
    Ld = math.prod(shape3[1:-2])
    w, m, v = (t.reshape(n_lead * Ld, r, N) for t in (w, m, v))
    tr = _tile(r, max(8, ADAMW_BLOCK_ELEMS // N), 8)
    c1 = 1.0 - ADAM_B1 ** ADAM_STEP
    c2 = 1.0 - ADAM_B2 ** ADAM_STEP
    outs = None
    for lead, p in enumerate(parts):
        def body(p_ref, w_ref, m_ref, v_ref, *rest):
            g_ref, d_ref, nm_ref, nv_ref = rest[-4:]
            g = p_ref[0]
            for j in range(1, N_DEV):
                g = g + p_ref[j]
            nm = ADAM_B1 * m_ref[...] + (1.0 - ADAM_B1) * g
            nv = ADAM_B2 * v_ref[...] + (1.0 - ADAM_B2) * (g * g)
            g_ref[...] = g
            nm_ref[...] = nm
            nv_ref[...] = nv
            d_ref[...] = -ADAM_LR * ((nm / c1) / (jnp.sqrt(nv / c2) + ADAM_EPS) + ADAM_WD * w_ref[...])

        base = lead * Ld
        row = BS((1, tr, N), lambda l, i, base=base: (base + l, i, 0))
        prev = [] if outs is None else list(outs)
        outs = _pcall(body, name=f"{name}_{lead}", grid=(Ld, r // tr),
                      in_specs=[BS((N_DEV, 1, tr, N), lambda l, i: (0, l, i, 0)), row, row, row]
                      + [BS(memory_space=pl.ANY)] * len(prev),
                      out_specs=[row] * 4, out_shape=[SDS((n_lead * Ld, r, N), F32)] * 4,
                      input_output_aliases={4 + j: j for j in range(len(prev))},
                      compiler_params=_cparams(("parallel", "parallel")))(p.reshape(N_DEV, Ld, r, N), w, m, v, *prev)
    return [t.reshape(shape) for t in outs]


def _place():
    return lax.axis_index("x"), lax.axis_index("y"), lax.axis_index("c")


def _all_gather(xs, name):
    n = len(xs)
    pairs = [(i, l) for i, x in enumerate(xs) for l in range(x.shape[0])]

    def body(*refs):
        x_refs, o_refs = refs[:n], refs[n:2 * n]
        send_sems, recv_sems, local_sems = refs[2 * n:]
        x_, y_, c_ = _place()
        me, sibling = (x_, y_, c_), (x_, y_, 1 - c_)
        chips = [(1 - x_, y_), (x_, 1 - y_), (1 - x_, 1 - y_)]

        def copy(e, k, block, to, from_input=False):
            i, l = pairs[e]
            px, py, pc = block
            dst = o_refs[i].at[l, 4 * px + 2 * py + pc]
            return pltpu.make_async_remote_copy(
                src_ref=x_refs[i].at[l] if from_input else dst, dst_ref=dst,
                send_sem=send_sems.at[7 * e + k], recv_sem=recv_sems.at[7 * e + k],
                device_id=to, device_id_type=pl.DeviceIdType.MESH)

        every = range(len(pairs))
        mine = [pltpu.make_async_copy(x_refs[i].at[l], o_refs[i].at[l, 4 * x_ + 2 * y_ + c_], local_sems.at[e])
                for e, (i, l) in enumerate(pairs)]
        for cp in mine:
            cp.start()
        first = [copy(e, 0, me, sibling, True) for e in every]
        first += [copy(e, 1 + j, me, (*chip, c_), True) for j, chip in enumerate(chips) for e in every]
        for cp in first:
            cp.start()
        passed = []
        for j, chip in enumerate(chips):
            for e in every:
                copy(e, 1 + j, (*chip, c_), me).wait_recv()
            for e in every:
                cp = copy(e, 4 + j, (*chip, c_), sibling)
                cp.start()
                passed.append(cp)
        for e in every:
            copy(e, 0, sibling, me).wait_recv()
        for j, chip in enumerate(chips):
            for e in every:
                copy(e, 4 + j, (*chip, 1 - c_), me).wait_recv()
        for cp in first + passed:
            cp.wait_send()
        for cp in mine:
            cp.wait()

    any_spec = BS(memory_space=pl.ANY)
    return _pcall(body, name=name, in_specs=[any_spec] * n, out_specs=[any_spec] * n,
                  out_shape=[SDS((x.shape[0], N_DEV) + x.shape[1:], x.dtype) for x in xs],
                  scratch_shapes=[pltpu.SemaphoreType.DMA((7 * len(pairs),)), pltpu.SemaphoreType.DMA((7 * len(pairs),)),
                                  pltpu.SemaphoreType.DMA((len(pairs),))],
                  compiler_params=pltpu.CompilerParams(has_side_effects=True))(*xs)


def _peer_of(k, place):
    x_, y_, c_ = place
    fx, fy, fc = (k >> 2) & 1, (k >> 1) & 1, k & 1
    return fx + x_ - 2 * fx * x_, fy + y_ - 2 * fy * y_, fc + c_ - 2 * fc * c_


def _split_copy(src_ref, land_ref, send_sems, recv_sems, e, k, place, scatter):
    x_, y_, c_ = place
    px, py, pc = _peer_of(k, place)
    return pltpu.make_async_remote_copy(
        src_ref=src_ref.at[4 * px + 2 * py + pc] if scatter else src_ref, dst_ref=land_ref.at[4 * x_ + 2 * y_ + c_],
        send_sem=send_sems.at[7 * e + k - 1], recv_sem=recv_sems.at[7 * e + k - 1],
        device_id=(px, py, pc), device_id_type=pl.DeviceIdType.MESH)


def _own_copy(src_ref, land_ref, sems, slot, place, scatter):
    x_, y_, c_ = place
    me = 4 * x_ + 2 * y_ + c_
    return pltpu.make_async_copy(src_ref.at[me] if scatter else src_ref, land_ref.at[me], sems.at[slot])


_HBM_SPEC = BS(memory_space=pltpu.HBM)
_SEM_SPEC = BS(memory_space=pltpu.SEMAPHORE)
_EFFECT = pltpu.SideEffectType.DATAFLOW_SIDE_EFFECTING


def _copies_start(srcs, scatter, name, thru=None):
    n = len(srcs)
    lands = [lax.empty(s.shape if scatter else (N_DEV,) + s.shape, s.dtype) for s in srcs]
    passed = srcs + lands + list(thru or ())

    def body(*refs):
        src_refs, land_refs = refs[:n], refs[n:2 * n]
        send_sems, recv_sems = refs[len(passed)], refs[len(passed) + 1]
        token = refs[-1]
        place = _place()
        for e in range(n):
            for k in range(1, N_DEV):
                _split_copy(src_refs[e], land_refs[e], send_sems, recv_sems, e, k, place, scatter).start()
            _own_copy(src_refs[e], land_refs[e], send_sems, 7 * n + e, place, scatter).start()
        token[...] = jnp.zeros_like(token)

    hbm = lambda t: pltpu.with_memory_space_constraint(t, pltpu.HBM)
    out = _pcall(body, name=name,
                 out_shape=(pltpu.SemaphoreType.DMA((8 * n,)), pltpu.SemaphoreType.DMA((7 * n,)),
                            *[pltpu.HBM(t.shape, t.dtype) for t in passed], SDS((8, LANES), F32)),
                 in_specs=[_HBM_SPEC] * len(passed),
                 out_specs=(_SEM_SPEC, _SEM_SPEC, *[_HBM_SPEC] * len(passed), BS(memory_space=pltpu.VMEM)),
                 input_output_aliases={i: 2 + i for i in range(len(passed))},
                 compiler_params=pltpu.CompilerParams(has_side_effects=_EFFECT))(*[hbm(t) for t in passed])
    return out[0], out[1], list(out[2:2 + n]), list(out[2 + n:2 + 2 * n]), list(out[2 + 2 * n:-1])


def _copies_wait(started, which, scatter, after, name):
    send_sems, recv_sems, srcs, lands, _ = started
    n, n_started = len(which), len(srcs)
    after = list(after) if isinstance(after, (list, tuple)) else [after]

    def body(*refs):
        src_refs, land_refs = refs[:n], refs[n:2 * n]
        send_s, recv_s = refs[2 * n], refs[2 * n + 1]
        place = _place()
        for j, e in enumerate(which):
            for k in range(1, N_DEV):
                cp = _split_copy(src_refs[j], land_refs[j], send_s, recv_s, e, k, place, scatter)
                cp.wait_send()
                cp.wait_recv()
            _own_copy(src_refs[j], land_refs[j], send_s, 7 * n_started + e, place, scatter).wait()

    args = [srcs[e] for e in which] + [lands[e] for e in which]
    out = _pcall(body, name=name, out_shape=tuple(pltpu.HBM(t.shape, t.dtype) for t in args),
                 in_specs=[_HBM_SPEC] * (2 * n) + [_SEM_SPEC, _SEM_SPEC] + [BS(memory_space=pl.ANY)] * len(after),
                 out_specs=tuple([_HBM_SPEC] * (2 * n)), input_output_aliases={i: i for i in range(2 * n)},
                 compiler_params=pltpu.CompilerParams(has_side_effects=_EFFECT))(*args, send_sems, recv_sems, *after)
    return list(out[:n]), list(out[n:])


def _pad_flat(t, quantum=PACK_QUANTUM):
    f = t.reshape(-1)
    pad = (-f.shape[0]) % quantum
    return jnp.pad(f, (0, pad)) if pad else f


def _pack(arrs, dtype):
    return jnp.concatenate([_pad_flat(a.astype(dtype)) for a in arrs]).reshape(-1, LANES)


def _unpack(buf, shapes, lead=()):
    flat = buf.reshape(lead + (-1,))
    out, off = [], 0
    for s in shapes:
        n = math.prod(s)
        out.append(flat[..., off:off + n].reshape(lead + tuple(s)))
        off += n + (-n) % PACK_QUANTUM
    return out


def _full_from_gathered(g, axis):
    t = jnp.moveaxis(g, 0, axis)
    s = t.shape
    return t.reshape(s[:axis] + (s[axis] * s[axis + 1],) + s[axis + 2:])


def _parts_from_full(t, axis):
    s = t.shape
    t = t.reshape(s[:axis] + (N_DEV, s[axis] // N_DEV) + s[axis + 1:])
    return jnp.moveaxis(t, axis, 0)


def _head_rows(t):
    return jnp.repeat(t, HEAD_DIM).reshape(1, -1)


def _dilated_attention(q, k, v, name):
    no_sink = jnp.full((1, q.shape[1]), NEG, F32)
    outs, lses = zip(*[band_attention(q, k, v, no_sink, d, ATT_BLOCK, True, f"{name}_d{d}") for d in DILATIONS])
    return merge3(tuple(outs), tuple(lses), name + "_merge")


STAGES = (
    ("proj0", ('mix_norm',), (('ab_w_in', 0),)),
    ("mixer0", ('lru_conv_w', 'lru_conv_b', 'lru_ba', 'lru_bx', 'lru_lambda'),
     (('lru_wa', 0), ('lru_wx', 0), ('ab_w_out', 0))),
    ("xa0", ('xa_norm', 'xa_mem_norm'), (('xa_wq', 0), ('xa_wkv', 0), ('xa_wo', 0))),
    ("ffn0", ('ffn_norm',), (('ffn_w_gate_up', 0), ('ffn_w_down', 0))),
    ("mixer1", ('mix_norm', 'c_b_qkv', 'c_sinks', 'c_b_out'), (('c_w_qkv', 0), ('c_w_out', 0))),
    ("xa1", ('xa_norm', 'xa_mem_norm'), (('xa_wq', 1), ('xa_wkv', 1), ('xa_wo', 1))),
    ("ffn1", ('ffn_norm',), (('ffn_w_gate_up', 1), ('ffn_w_down', 1))),
)


def _stage_fn(stage, Wb, tabs, mem):
    layer = int(stage[-1])
    L = f"l{layer}"

    def run(S, Cw, h):
        def lin(a, key, bias, add, name, rows=None):
            wb, wc = Wb[key], Cw[key]
            if rows is not None:
                wb, wc = wb[rows], wc[rows]
            return linear(a, wb, wc, bias, add, name)

        def norm_lin(a, gain, key, bias, name):
            return norm_linear(a, gain, Wb[key], Cw[key], bias, key[0] in COLUMN_CUT, name)

        if stage == "mixer0":
            h, x_br, y_br, q, k, v = h
            C = S['lru_conv_w'].shape[-1]
            xc = conv4(x_br, S['lru_conv_w'][0], S['lru_conv_b'][0], L + "_conv")
            ga, gx = lru_gates(xc, Wb['lru_wa', 0], Cw['lru_wa', 0], S['lru_ba'][0],
                               Wb['lru_wx', 0], Cw['lru_wx', 0], S['lru_bx'][0], L + "_gates")
            rec = lru_scan(xc, ga, gx, y_br, S['lru_lambda'][0], L + "_scan")
            att = _dilated_attention(q, k, v, L + "_att")
            h = lin(att, ('ab_w_out', 0), None, h, L + "_w_out_att", slice(C, None))
            return lin(rec, ('ab_w_out', 0), None, h, L + "_w_out_rec", slice(0, C))
        if stage == "mixer1":
            qw = C_HEADS * HEAD_DIM
            kw = C_KV_HEADS * HEAD_DIM
            q, k, v, h = norm_linear_pieces(h, S['mix_norm'][1], Wb['c_w_qkv', 0], Cw['c_w_qkv', 0], S['c_b_qkv'][0], tabs,
                                            ((qw, True), (kw, True), (kw, False)), L + "_w_qkv")
            o = band_attention(q, k, v, _head_rows(S['c_sinks'][0]), 1, ATT_BLOCK - 1, False, L + "_att")
            return lin(o, ('c_w_out', 0), S['c_b_out'][0], h, L + "_w_out")
        if stage.startswith("xa"):
            xq, h = norm_lin(h, S['xa_norm'][layer], ('xa_wq', layer), None, L + "_xa_wq")
            xkv, _ = norm_lin(mem, S['xa_mem_norm'][layer], ('xa_wkv', layer), None, L + "_xa_wkv")
            return lin(cross_attention(xq, xkv, L + "_xa"), ('xa_wo', layer), None, h, L + "_xa_wo")
        gu, down = ('ffn_w_gate_up', layer), ('ffn_w_down', layer)
        return ffn_block(h, S['ffn_norm'][layer], Wb[gu], Cw[gu], Wb[down], Cw[down], L + "_ffn")

    return run


def kernel(x, mem, mix_norm, ab_w_in, lru_conv_w, lru_conv_b, lru_wa, lru_ba, lru_wx, lru_bx, lru_lambda, ab_w_out, c_w_qkv, c_b_qkv, c_sinks, c_w_out, c_b_out, xa_norm, xa_mem_norm, xa_wq, xa_wkv, xa_wo, ffn_norm, ffn_w_gate_up, ffn_w_down, final_norm, loss_target, m_mix_norm, m_ab_w_in, m_lru_conv_w, m_lru_conv_b, m_lru_wa, m_lru_ba, m_lru_wx, m_lru_bx, m_lru_lambda, m_ab_w_out, m_c_w_qkv, m_c_b_qkv, m_c_sinks, m_c_w_out, m_c_b_out, m_xa_norm, m_xa_mem_norm, m_xa_wq, m_xa_wkv, m_xa_wo, m_ffn_norm, m_ffn_w_gate_up, m_ffn_w_down, m_final_norm, v_mix_norm, v_ab_w_in, v_lru_conv_w, v_lru_conv_b, v_lru_wa, v_lru_ba, v_lru_wx, v_lru_bx, v_lru_lambda, v_ab_w_out, v_c_w_qkv, v_c_b_qkv, v_c_sinks, v_c_w_out, v_c_b_out, v_xa_norm, v_xa_mem_norm, v_xa_wq, v_xa_wkv, v_xa_wo, v_ffn_norm, v_ffn_w_gate_up, v_ffn_w_down, v_final_norm):
    w_loc = dict(zip(WEIGHT_NAMES, (mix_norm, ab_w_in, lru_conv_w, lru_conv_b, lru_wa, lru_ba, lru_wx, lru_bx, lru_lambda, ab_w_out, c_w_qkv, c_b_qkv, c_sinks, c_w_out, c_b_out, xa_norm, xa_mem_norm, xa_wq, xa_wkv, xa_wo, ffn_norm, ffn_w_gate_up, ffn_w_down, final_norm)))
    m_loc = dict(zip(WEIGHT_NAMES, (m_mix_norm, m_ab_w_in, m_lru_conv_w, m_lru_conv_b, m_lru_wa, m_lru_ba, m_lru_wx, m_lru_bx, m_lru_lambda, m_ab_w_out, m_c_w_qkv, m_c_b_qkv, m_c_sinks, m_c_w_out, m_c_b_out, m_xa_norm, m_xa_mem_norm, m_xa_wq, m_xa_wkv, m_xa_wo, m_ffn_norm, m_ffn_w_gate_up, m_ffn_w_down, m_final_norm)))
    v_loc = dict(zip(WEIGHT_NAMES, (v_mix_norm, v_ab_w_in, v_lru_conv_w, v_lru_conv_b, v_lru_wa, v_lru_ba, v_lru_wx, v_lru_bx, v_lru_lambda, v_ab_w_out, v_c_w_qkv, v_c_b_qkv, v_c_sinks, v_c_w_out, v_c_b_out, v_xa_norm, v_xa_mem_norm, v_xa_wq, v_xa_wkv, v_xa_wo, v_ffn_norm, v_ffn_w_gate_up, v_ffn_w_down, v_final_norm)))

    first_keys = list(STAGES[0][2])
    keys = [key for _, _, stage_keys in STAGES[1:] for key in stage_keys]
    shards = [_shard_view(n, w_loc[n])[l].astype(BF16) for n, l in keys]
    first_g = _all_gather([_pack([w_loc[n] for n in SMALL], F32)[None]]
                          + [_shard_view(n, w_loc[n])[l].astype(BF16)[None] for n, l in first_keys], "gather_first")
    gather = _copies_start(shards, False, "gather_start", thru=[first_g[0]])
    small_g = gather[4][0][0]
    Wb = {key: _full_from_gathered(g[0], _layer_shard_axis(key[0])) for key, g in zip(first_keys, first_g[1:])}
    S = {n: w_loc[n] for n in REPLICATED}
    for n, t in zip(SMALL, _unpack(small_g, [w_loc[n].shape for n in SMALL], lead=(N_DEV,))):
        S[n] = _full_from_gathered(t, SHARD_AXIS[n])

    tabs = _rope_tables(x.shape[1])
    w_in = Wb[first_keys[0]]
    hn0 = _rms_fwd_call(x[0], S['mix_norm'][0], "l0_w_in_norm", BF16)
    lru_w, att_w = N_DEV * lru_conv_w.shape[-1], B_HEADS * HEAD_DIM
    pieces = (("x", lru_w, False), ("y", lru_w, False), ("q", att_w, True), ("k", att_w, True), ("v", att_w, False))
    h, row = [x[0]], 0
    for piece, width, rotated in pieces:
        h.append(_mm(hn0, w_in[row:row + width], tb=True, rope=tabs if rotated else None, name="l0_w_in_" + piece,
                     tm=2048, tn=1024, tk=1024))
        row += width
    h = tuple(h)
    vjps = []
    for stage, small_names, stage_keys in STAGES[1:]:
        which = [keys.index(key) for key in stage_keys]
        _, lands = _copies_wait(gather, which, False, jax.tree.leaves(h)[-1], "gather_wait_" + stage)
        for e, land in zip(which, lands):
            Wb[keys[e]] = _full_from_gathered(land, _layer_shard_axis(keys[e][0]))
        carriers = {key: jnp.zeros(Wb[key].shape, F32) for key in stage_keys}
        h, vjp_fn = jax.vjp(_stage_fn(stage, Wb, tabs, mem[0]), {n: S[n] for n in small_names}, carriers, h)
        vjps.append(vjp_fn)
    loss_part, dh, dg_final = _final_call(h, S['final_norm'], loss_target[0], "final_loss")

    grads = {'final_norm': dg_final.reshape(final_norm.shape)}
    exchanges, send_keys, send_parts = [], [], []

    def start_exchange(stage, dh):
        leaves, tree = jax.tree.flatten(dh)
        started = _copies_start(list(send_parts), True, "grad_start_" + stage, thru=leaves)
        exchanges.append((stage, started, list(send_keys)))
        send_keys.clear()
        send_parts.clear()
        return jax.tree.unflatten(tree, started[4])

    for (stage, small_names, stage_keys), vjp_fn in zip(reversed(STAGES[1:]), reversed(vjps)):
        g_small, g_big, dh = vjp_fn(dh)
        for n in small_names:
            grads[n] = grads[n] + g_small[n] if n in grads else g_small[n]
        send_keys += list(stage_keys)
        send_parts += [_parts_from_full(g_big[key], _layer_shard_axis(key[0])) for key in stage_keys]
        if stage == "xa1":
            continue
        if stage == "mixer0":
            small_parts = [_parts_from_full(grads[n], SHARD_AXIS[n]) for n in SMALL]
            send_keys.append("small")
            send_parts.append(jnp.stack([_pack([p[j] for p in small_parts], F32) for j in range(N_DEV)]))
        dh = start_exchange(stage, dh)
    d_res, d_proj = dh[0], _join_pieces_call(dh[1:], [rotated for _, _, rotated in pieces], tabs, "l0_w_in_dproj")[0]
    send_keys.append(first_keys[0])
    send_parts.append(_parts_from_full(_mm(d_proj, hn0, ta=True, name="l0_w_in_dw", tm=1408, tn=1024, tk=2048),
                                       _layer_shard_axis(first_keys[0][0])))
    d_res, d_proj = start_exchange("proj0", (d_res, d_proj))
    dx, dg0 = _mm_rms_bwd(d_proj, w_in, x[0], S['mix_norm'][0], d_res, "l0_w_in_da", wt=True)
    grads['mix_norm'] = grads['mix_norm'] + jnp.concatenate([dg0, jnp.zeros_like(dg0)], axis=0)
    rep_names = REPLICATED + ["loss"]
    grads["loss"] = loss_part
    zero = jnp.zeros((1, 1), F32)
    for d in (w_loc, m_loc, v_loc):
        d["loss"] = zero
    rep_started = _copies_start([_pack([grads[n] for n in rep_names], F32)], False, "rep_grads_start")

    parts, out = {}, {}

    def end_exchange(stage, started, ex_keys, after):
        _, lands = _copies_wait(started, list(range(len(ex_keys))), True, after, "grad_wait_" + stage)
        parts.update(zip(ex_keys, lands))

    def adamw(p, names, call_name):
        if len(names) == 1:
            n = names[0]
            res = _adamw_call(p, _shard_view(n, w_loc[n]), _shard_view(n, m_loc[n]), _shard_view(n, v_loc[n]), call_name)
            for kind, t in zip(("grad", "delta", "new_m", "new_v"), res):
                out[kind, n] = _shard_view(n, t)
        else:
            res = _adamw_call(p, *[_pack([d[n] for n in names], F32) for d in (w_loc, m_loc, v_loc)], call_name)
            for kind, buf in zip(("grad", "delta", "new_m", "new_v"), res):
                for n, t in zip(names, _unpack(buf, [w_loc[n].shape for n in names])):
                    out[kind, n] = t

    for ex in exchanges[:-1]:
        end_exchange(*ex, dx)
    last_names = {key[0] for key in exchanges[-1][2]}
    for n in BIG:
        if n not in last_names:
            adamw([parts[n, l] for l in range(w_loc[n].shape[0])], [n], "adamw_" + n)
    adamw(parts["small"], SMALL, "adamw_small")
    end_exchange(*exchanges[-1], [out["new_v", n] for n in BIG if n not in last_names])
    for n in BIG:
        if n in last_names:
            adamw([parts[n, l] for l in range(w_loc[n].shape[0])], [n], "adamw_" + n)
    _, rep_land = _copies_wait(rep_started, [0], False, out["new_v", "ab_w_in"], "rep_grads_wait")
    adamw(rep_land[0], rep_names, "adamw_replicated")
    loss = out["grad", "loss"][0, 0]

    return (loss, dx[None], *[out[kind, n] for kind in ("grad", "delta", "new_m", "new_v") for n in WEIGHT_NAMES])
```

```python
import functools
import math

import jax
import jax.numpy as jnp
from jax import lax
from jax.experimental import pallas as pl
from jax.experimental.pallas import tpu as pltpu

F32 = jnp.float32
BF16 = jnp.bfloat16
SDS = jax.ShapeDtypeStruct
BS = pl.BlockSpec

N_DEV = 8
NORM_EPS = 1e-6
ROPE_THETA = 500000.0
HEAD_DIM = 64
ROT_DIM = 16
ATT_BLOCK = 128
LRU_C = 8.0
LRU_HEADS = 4
DILATIONS = (1, 4, 16)
B_HEADS = 8
C_HEADS = 16
C_KV_HEADS = 2
XA_HEADS = 4
XA_HEAD_DIM = 128
NEG = -1e30
ADAM_LR, ADAM_B1, ADAM_B2, ADAM_EPS, ADAM_WD, ADAM_STEP = 0.001, 0.9, 0.999, 1e-08, 0.01, 10
LANES = 128
VMEM_LIMIT = 48 * 1024 * 1024

WEIGHT_NAMES = ['mix_norm', 'ab_w_in', 'lru_conv_w', 'lru_conv_b', 'lru_wa', 'lru_ba', 'lru_wx', 'lru_bx',
                'lru_lambda', 'ab_w_out', 'c_w_qkv', 'c_b_qkv', 'c_sinks', 'c_w_out', 'c_b_out', 'xa_norm',
                'xa_mem_norm', 'xa_wq', 'xa_wkv', 'xa_wo', 'ffn_norm', 'ffn_w_gate_up', 'ffn_w_down', 'final_norm']
SHARD_AXIS = {'ab_w_in': 2, 'lru_conv_w': 2, 'lru_wa': 2, 'lru_ba': 2, 'lru_wx': 2, 'lru_bx': 2, 'ab_w_out': 1,
              'c_w_qkv': 2, 'c_b_qkv': 1, 'c_w_out': 1, 'c_b_out': 1, 'xa_wq': 1, 'xa_wkv': 1, 'xa_wo': 2,
              'ffn_w_gate_up': 2, 'ffn_w_down': 1}
BIG = ['ab_w_in', 'lru_wa', 'lru_wx', 'ab_w_out', 'c_w_qkv', 'c_w_out', 'xa_wq', 'xa_wkv', 'xa_wo',
       'ffn_w_gate_up', 'ffn_w_down']
SMALL = ['lru_conv_w', 'lru_ba', 'lru_bx', 'c_b_qkv', 'c_b_out']
REPLICATED = [n for n in WEIGHT_NAMES if n not in SHARD_AXIS]
COLUMN_CUT = ('ab_w_in', 'c_w_qkv', 'ffn_w_gate_up')
PACK_QUANTUM = 2048


def _shard_view(name, t):
    return jnp.swapaxes(t, -1, -2) if name in COLUMN_CUT else t


def _layer_shard_axis(name):
    return 0 if name in COLUMN_CUT else SHARD_AXIS[name] - 1


def _pcall(body, **kw):
    return pl.pallas_call(body, **kw)


def _cparams(sem=None):
    return pltpu.CompilerParams(dimension_semantics=sem, vmem_limit_bytes=VMEM_LIMIT)


def _tile(n, target, mult=LANES):
    if n <= target:
        return n
    t = (target // mult) * mult
    while t >= mult:
        if n % t == 0:
            return t
        t -= mult
    return n


def _sigmoid(x):
    return 1.0 / (1.0 + jnp.exp(-x))


def _expm1(x):
    small = x * (1.0 + x * (0.5 + x * (1.0 / 6.0 + x * (1.0 / 24.0))))
    return jnp.where(jnp.abs(x) < 0.03, small, jnp.exp(x) - 1.0)


_GELU_C = math.sqrt(2.0 / math.pi)


def _gelu_parts(y):
    y2 = y * y
    th = jnp.tanh(_GELU_C * (y + 0.044715 * y * y2))
    g = 0.5 * y * (1.0 + th)
    dg = 0.5 * (1.0 + th) + 0.5 * y * (1.0 - th * th) * _GELU_C * (1.0 + 3.0 * 0.044715 * y2)
    return g, dg


def _rotate(xv, tab_refs, inverse=False):
    W = xv.shape[1]
    half = ROT_DIM // 2
    c, sa, sb = (jnp.tile(t[...], (1, W // LANES)) for t in tab_refs)
    if not inverse:
        return xv * c + pltpu.roll(xv, half, axis=1) * sa + pltpu.roll(xv, W - half, axis=1) * sb
    return xv * c + pltpu.roll(xv * sa, W - half, axis=1) + pltpu.roll(xv * sb, half, axis=1)


def _mm(a, b, *, ta=False, tb=False, bias=None, add=None, rope=None, name, tm=512, tn=512, tk=2048):
    M, K = (a.shape[1], a.shape[0]) if ta else a.shape
    N = b.shape[0] if tb else b.shape[1]
    tm, tn, tk = _tile(M, tm), _tile(N, tn), _tile(K, tk)
    nk = K // tk
    dn = (((0 if ta else 1,), (1 if tb else 0,)), ((), ()))

    def body(*refs):
        a_ref, b_ref = refs[0], refs[1]
        pos = 2
        bias_ref = add_ref = None
        if bias is not None:
            bias_ref = refs[pos]
            pos += 1
        if add is not None:
            add_ref = refs[pos]
            pos += 1
        tab_refs = refs[pos:pos + 3] if rope is not None else None
        pos += 3 if rope is not None else 0
        o_ref = refs[pos]
        part = lax.dot_general(a_ref[...].astype(BF16), b_ref[...].astype(BF16), dn, preferred_element_type=F32)

        def finish(r):
            if bias_ref is not None:
                r = r + bias_ref[...]
            if add_ref is not None:
                r = r + add_ref[...]
            o_ref[...] = r if tab_refs is None else _rotate(r, tab_refs)

        if nk == 1:
            finish(part)
            return
        acc_ref = refs[pos + 1]
        k = pl.program_id(2)

        @pl.when(k == 0)
        def _():
            acc_ref[...] = part

        @pl.when((k > 0) & (k < nk - 1))
        def _():
            acc_ref[...] += part

        @pl.when(k == nk - 1)
        def _():
            finish(acc_ref[...] + part)

    in_specs = [BS((tk, tm), lambda i, j, k: (k, i)) if ta else BS((tm, tk), lambda i, j, k: (i, k)),
                BS((tn, tk), lambda i, j, k: (j, k)) if tb else BS((tk, tn), lambda i, j, k: (k, j))]
    args = [a, b]
    if bias is not None:
        in_specs.append(BS((1, tn), lambda i, j, k: (0, j)))
        args.append(bias.reshape(1, N))
    if add is not None:
        in_specs.append(BS((tm, tn), lambda i, j, k: (i, j)))
        args.append(add)
    if rope is not None:
        in_specs += [BS((tm, LANES), lambda i, j, k: (i, 0))] * 3
        args += list(rope)
    return _pcall(body, name=name, grid=(M // tm, N // tn, nk), in_specs=in_specs,
                  out_specs=BS((tm, tn), lambda i, j, k: (i, j)), out_shape=SDS((M, N), F32),
                  scratch_shapes=[pltpu.VMEM((tm, tn), F32)] if nk > 1 else [],
                  compiler_params=_cparams(("parallel", "parallel", "arbitrary")))(*args)


def _mm_rms_bwd(dy, wb, x, g, add, name, wt=False):
    T, Kc = dy.shape
    Dm = wb.shape[1] if wt else wb.shape[0]
    tm, tk = _tile(T, 1024), _tile(Kc, 1408)
    nk = Kc // tk
    dn = (((1,), (0 if wt else 1,)), ((), ()))

    def body(*refs):
        dy_ref, w_ref, x_ref, g_ref = refs[:4]
        dx_ref, dg_ref, acc_ref = refs[-3:]
        i, k = pl.program_id(0), pl.program_id(1)
        part = lax.dot_general(dy_ref[...].astype(BF16), w_ref[...], dn, preferred_element_type=F32)

        @pl.when(k == 0)
        def _():
            acc_ref[...] = part

        @pl.when((k > 0) & (k < nk - 1))
        def _():
            acc_ref[...] += part

        @pl.when((i == 0) & (k == 0))
        def _():
            dg_ref[...] = jnp.zeros_like(dg_ref)

        @pl.when(k == nk - 1)
        def _():
            d = part if nk == 1 else acc_ref[...] + part
            xv = x_ref[...]
            r = lax.rsqrt(jnp.mean(xv * xv, axis=-1, keepdims=True) + NORM_EPS)
            xh = xv * r
            dyg = d * g_ref[...]
            dx = r * (dyg - xh * jnp.mean(dyg * xh, axis=-1, keepdims=True))
            dx_ref[...] = dx if add is None else dx + refs[4][...]
            dg_ref[...] += jnp.sum(d * xh, axis=0, keepdims=True)

    row = BS((tm, Dm), lambda i, k: (i, 0))
    vec = BS((1, Dm), lambda i, k: (0, 0))
    extra = [] if add is None else [add]
    return _pcall(body, name=name, grid=(T // tm, nk),
                  in_specs=[BS((tm, tk), lambda i, k: (i, k)),
                            BS((tk, Dm), lambda i, k: (k, 0)) if wt else BS((Dm, tk), lambda i, k: (0, k)), row, vec]
                  + [row] * len(extra),
                  out_specs=[row, vec], out_shape=[SDS((T, Dm), F32), SDS((1, Dm), F32)],
                  scratch_shapes=[pltpu.VMEM((tm, Dm), F32)],
                  compiler_params=_cparams(("arbitrary", "arbitrary")))(dy, wb, x, g.reshape(1, Dm), *extra)


def _colsum(x, name):
    T, N = x.shape
    tt = _tile(T, 1024, 8)

    def body(x_ref, o_ref):
        @pl.when(pl.program_id(0) == 0)
        def _():
            o_ref[...] = jnp.zeros_like(o_ref)

        o_ref[...] += jnp.sum(x_ref[...], axis=0, keepdims=True)

    return _pcall(body, name=name, grid=(T // tt,), in_specs=[BS((tt, N), lambda i: (i, 0))],
                  out_specs=BS((1, N), lambda i: (0, 0)), out_shape=SDS((1, N), F32),
                  compiler_params=_cparams(("arbitrary",)))(x)


@functools.partial(jax.custom_vjp, nondiff_argnums=(5,))
def linear(a, wb, wc, bias, add, name):
    return _linear_fwd(a, wb, wc, bias, add, name)[0]


def _linear_fwd(a, wb, wc, bias, add, name):
    out = _mm(a, wb, bias=bias, add=add, name=name, tm=1024, tn=1024, tk=1024)
    return out, (a, wb, bias is not None, add is not None)


def _linear_bwd(name, res, g):
    a, wb, has_bias, has_add = res
    da = _mm(g, wb, tb=True, name=name + "_da", tm=1024, tn=1024, tk=1024)
    dw = _mm(a, g, ta=True, name=name + "_dw", tm=1024, tn=1024, tk=1024)
    dbias = _colsum(g, name + "_db").reshape(-1) if has_bias else None
    return da, jnp.zeros_like(wb), dw, dbias, (g if has_add else None)


linear.defvjp(_linear_fwd, _linear_bwd)


def _rms_fwd_call(x, g, name, out_dtype=F32):
    T, Dm = x.shape
    tt = _tile(T, 2048, 16)

    def body(x_ref, g_ref, o_ref):
        xv = x_ref[...]
        r = lax.rsqrt(jnp.mean(xv * xv, axis=-1, keepdims=True) + NORM_EPS)
        o_ref[...] = ((xv * r) * g_ref[...]).astype(out_dtype)

    return _pcall(body, name=name, grid=(T // tt,),
                  in_specs=[BS((tt, Dm), lambda i: (i, 0)), BS((1, Dm), lambda i: (0, 0))],
                  out_specs=BS((tt, Dm), lambda i: (i, 0)), out_shape=SDS((T, Dm), out_dtype),
                  compiler_params=_cparams(("parallel",)))(x, g.reshape(1, Dm))


@functools.partial(jax.custom_vjp, nondiff_argnums=(5, 6))
def norm_linear(x, g, wb, wc, bias, wt, name):
    return _norm_linear_fwd(x, g, wb, wc, bias, wt, name)[0]


def _norm_linear_fwd(x, g, wb, wc, bias, wt, name):
    hn = _rms_fwd_call(x, g, name + "_norm", BF16)
    return (_mm(hn, wb, tb=wt, bias=bias, name=name, tm=1024, tn=1408, tk=1024), x), (x, g, hn, wb, bias is not None)


def _norm_linear_bwd(wt, name, res, cts):
    x, g, hn, wb, has_bias = res
    dy, dres = cts
    if wt:
        dw = _mm(dy, hn, ta=True, name=name + "_dw", tm=1408, tn=1024, tk=2048)
    else:
        dw = _mm(hn, dy, ta=True, name=name + "_dw", tm=1024, tn=1408, tk=1024)
    dx, dg = _mm_rms_bwd(dy, wb, x, g, dres, name + "_da", wt)
    dbias = _colsum(dy, name + "_db").reshape(-1) if has_bias else None
    return dx, dg.reshape(g.shape), jnp.zeros_like(wb), dw, dbias


norm_linear.defvjp(_norm_linear_fwd, _norm_linear_bwd)


def _rope_tables(T):
    half = ROT_DIM // 2
    inv = ROPE_THETA ** (-jnp.arange(0, ROT_DIM, 2, dtype=F32) / ROT_DIM)
    lane = jnp.arange(LANES) % HEAD_DIM
    freq = jnp.where(lane < ROT_DIM, inv[lane % half], 0.0)
    ang = jnp.arange(T, dtype=F32)[:, None] * freq[None, :]
    c, s = jnp.cos(ang), jnp.sin(ang)
    sa = jnp.where((lane >= half) & (lane < ROT_DIM), s, 0.0)
    sb = jnp.where(lane < half, -s, 0.0)
    return c, sa, sb


def _join_pieces_call(pieces, rotated, tabs, name):
    T = pieces[0].shape[0]
    widths = [p.shape[1] for p in pieces]
    W = sum(widths)
    tt = _tile(T, 512, 16)

    def body(*refs):
        tab_refs, o_ref, sum_ref = refs[len(pieces):len(pieces) + 3], refs[-2], refs[-1]

        @pl.when(pl.program_id(0) == 0)
        def _():
            sum_ref[...] = jnp.zeros_like(sum_ref)

        off = 0
        for p_ref, w, r in zip(refs, widths, rotated):
            piece = _rotate(p_ref[...], tab_refs, inverse=True) if r else p_ref[...]
            o_ref[:, off:off + w] = piece.astype(BF16)
            sum_ref[:, off:off + w] += jnp.sum(piece, axis=0, keepdims=True)
            off += w

    return _pcall(body, name=name, grid=(T // tt,),
                  in_specs=[BS((tt, w), lambda i: (i, 0)) for w in widths] + [BS((tt, LANES), lambda i: (i, 0))] * 3,
                  out_specs=[BS((tt, W), lambda i: (i, 0)), BS((1, W), lambda i: (0, 0))],
                  out_shape=[SDS((T, W), BF16), SDS((1, W), F32)],
                  compiler_params=_cparams(("arbitrary",)))(*pieces, *tabs)


@functools.partial(jax.custom_vjp, nondiff_argnums=(6, 7))
def norm_linear_pieces(x, g, wb, wc, bias, tabs, pieces, name):
    return _norm_linear_pieces_fwd(x, g, wb, wc, bias, tabs, pieces, name)[0]


def _norm_linear_pieces_fwd(x, g, wb, wc, bias, tabs, pieces, name):
    hn = _rms_fwd_call(x, g, name + "_norm", BF16)
    outs, row = [], 0
    for i, (width, rotated) in enumerate(pieces):
        outs.append(_mm(hn, wb[row:row + width], tb=True, bias=None if bias is None else bias[row:row + width],
                        rope=tabs if rotated else None, name=f"{name}_{i}", tm=2048, tn=1024, tk=1024))
        row += width
    return (*outs, x), (x, g, hn, wb, tabs, bias is not None)


def _norm_linear_pieces_bwd(pieces, name, res, cts):
    x, g, hn, wb, tabs, has_bias = res
    dy, dy_sum = _join_pieces_call(cts[:-1], [rotated for _, rotated in pieces], tabs, name + "_join")
    dw = _mm(dy, hn, ta=True, name=name + "_dw", tm=1408, tn=1024, tk=2048)
    dx, dg = _mm_rms_bwd(dy, wb, x, g, cts[-1], name + "_da", wt=True)
    dbias = dy_sum.reshape(-1) if has_bias else None
    return dx, dg.reshape(g.shape), jnp.zeros_like(wb), dw, dbias, jax.tree.map(jnp.zeros_like, tabs)


norm_linear_pieces.defvjp(_norm_linear_pieces_fwd, _norm_linear_pieces_bwd)


def _conv_fwd_call(x, w, b, name):
    T, C = x.shape
    tt = _tile(T, 2048, 8)
    per = tt // 8

    def body(x_ref, halo_ref, w_ref, b_ref, o_ref):
        i = pl.program_id(0)
        halo = jnp.where(i > 0, halo_ref[...], 0.0)
        e = jnp.concatenate([halo, x_ref[...]], axis=0)
        acc = b_ref[...]
        for k in (3, 2, 1):
            acc = acc + pltpu.roll(e, k, axis=0)[8:, :] * w_ref[3 - k:4 - k, :]
        o_ref[...] = acc + x_ref[...] * w_ref[3:4, :]

    return _pcall(body, name=name, grid=(T // tt,),
                  in_specs=[BS((tt, C), lambda i: (i, 0)), BS((8, C), lambda i: (jnp.maximum(i * per - 1, 0), 0)),
                            BS((4, C), lambda i: (0, 0)), BS((1, C), lambda i: (0, 0))],
                  out_specs=BS((tt, C), lambda i: (i, 0)), out_shape=SDS((T, C), F32),
                  compiler_params=_cparams(("parallel",)))(x, x, w, b.reshape(1, C))


def _conv_bwd_call(x, w, dy, name):
    T, C = x.shape
    tt = _tile(T, 1024, 8)
    per = tt // 8
    nt = T // tt

    def body(x_ref, halo_ref, w_ref, dy_ref, nxt_ref, dx_ref, dwb_ref):
        i = pl.program_id(0)
        halo = jnp.where(i > 0, halo_ref[...], 0.0)
        e = jnp.concatenate([halo, x_ref[...]], axis=0)
        dy = dy_ref[...]
        nxt = jnp.where(i < nt - 1, nxt_ref[...], 0.0)
        f = jnp.concatenate([dy, nxt], axis=0)
        dx = dy * w_ref[3:4, :]
        rows = [None] * 4
        rows[3] = jnp.sum(dy * x_ref[...], axis=0, keepdims=True)
        for k in (1, 2, 3):
            dx = dx + pltpu.roll(f, tt + 8 - k, axis=0)[:tt, :] * w_ref[3 - k:4 - k, :]
            rows[3 - k] = jnp.sum(dy * pltpu.roll(e, k, axis=0)[8:, :], axis=0, keepdims=True)
        dx_ref[...] = dx
        upd = jnp.concatenate(rows + [jnp.sum(dy, axis=0, keepdims=True), jnp.zeros((3, C), F32)], axis=0)

        @pl.when(i == 0)
        def _():
            dwb_ref[...] = jnp.zeros_like(dwb_ref)

        dwb_ref[...] += upd

    row = BS((tt, C), lambda i: (i, 0))
    return _pcall(body, name=name, grid=(nt,),
                  in_specs=[row, BS((8, C), lambda i: (jnp.maximum(i * per - 1, 0), 0)), BS((4, C), lambda i: (0, 0)),
                            row, BS((8, C), lambda i: (jnp.minimum((i + 1) * per, T // 8 - 1), 0))],
                  out_specs=[row, BS((8, C), lambda i: (0, 0))],
                  out_shape=[SDS((T, C), F32), SDS((8, C), F32)],
                  compiler_params=_cparams(("arbitrary",)))(x, x, w, dy, dy)


@functools.partial(jax.custom_vjp, nondiff_argnums=(3,))
def conv4(x, w, b, name):
    return _conv_fwd_call(x, w, b, name)


def _conv4_fwd(x, w, b, name):
    return _conv_fwd_call(x, w, b, name), (x, w)


def _conv4_bwd(name, res, dy):
    x, w = res
    dx, dwb = _conv_bwd_call(x, w, dy, name + "_bwd")
    return dx, dwb[0:4], dwb[4]


conv4.defvjp(_conv4_fwd, _conv4_bwd)


def _gates_fwd_call(xc, wa, ba, wx, bx, name):
    T, C = xc.shape
    hd = C // LRU_HEADS
    tt = _tile(T, 1024, 8)

    def body(x_ref, wa_ref, ba_ref, wx_ref, bx_ref, ga_ref, gx_ref):
        xb = x_ref[...].astype(BF16)
        ga_ref[...] = jnp.dot(xb, wa_ref[0].astype(BF16), preferred_element_type=F32) + ba_ref[...]
        gx_ref[...] = jnp.dot(xb, wx_ref[0].astype(BF16), preferred_element_type=F32) + bx_ref[...]

    blk = BS((tt, hd), lambda i, h: (i, h))
    wsp = BS((1, hd, hd), lambda i, h: (h, 0, 0))
    bsp = BS((1, hd), lambda i, h: (0, h))
    return _pcall(body, name=name, grid=(T // tt, LRU_HEADS), in_specs=[blk, wsp, bsp, wsp, bsp],
                  out_specs=[blk, blk], out_shape=[SDS((T, C), F32)] * 2,
                  compiler_params=_cparams(("parallel", "parallel")))(xc, wa, ba.reshape(1, C), wx, bx.reshape(1, C))


def _gates_bwd_x_call(dga, dgx, wa, wx, name):
    T, C = dga.shape
    hd = C // LRU_HEADS
    tt = _tile(T, 1024, 8)
    dn = (((1,), (1,)), ((), ()))

    def body(da_ref, dx_ref, wa_ref, wx_ref, o_ref):
        o_ref[...] = (lax.dot_general(da_ref[...].astype(BF16), wa_ref[0].astype(BF16), dn, preferred_element_type=F32)
                      + lax.dot_general(dx_ref[...].astype(BF16), wx_ref[0].astype(BF16), dn, preferred_element_type=F32))

    blk = BS((tt, hd), lambda i, h: (i, h))
    wsp = BS((1, hd, hd), lambda i, h: (h, 0, 0))
    return _pcall(body, name=name, grid=(T // tt, LRU_HEADS), in_specs=[blk, blk, wsp, wsp], out_specs=blk,
                  out_shape=SDS((T, C), F32), compiler_params=_cparams(("parallel", "parallel")))(dga, dgx, wa, wx)


def _gates_bwd_w_call(xc, dga, dgx, name):
    T, C = xc.shape
    hd = C // LRU_HEADS
    tt = _tile(T, 1024, 8)
    dn = (((0,), (0,)), ((), ()))

    def body(x_ref, da_ref, dx_ref, dwa_ref, dwx_ref, dba_ref, dbx_ref):
        @pl.when(pl.program_id(1) == 0)
        def _():
            dwa_ref[...] = jnp.zeros_like(dwa_ref)
            dwx_ref[...] = jnp.zeros_like(dwx_ref)
            dba_ref[...] = jnp.zeros_like(dba_ref)
            dbx_ref[...] = jnp.zeros_like(dbx_ref)

        xb = x_ref[...].astype(BF16)
        da, dx = da_ref[...], dx_ref[...]
        dwa_ref[0] += lax.dot_general(xb, da.astype(BF16), dn, preferred_element_type=F32)
        dwx_ref[0] += lax.dot_general(xb, dx.astype(BF16), dn, preferred_element_type=F32)
        dba_ref[...] += jnp.sum(da, axis=0, keepdims=True)
        dbx_ref[...] += jnp.sum(dx, axis=0, keepdims=True)

    blk = BS((tt, hd), lambda h, i: (i, h))
    wsp = BS((1, hd, hd), lambda h, i: (h, 0, 0))
    bsp = BS((1, hd), lambda h, i: (0, h))
    return _pcall(body, name=name, grid=(LRU_HEADS, T // tt), in_specs=[blk, blk, blk],
                  out_specs=[wsp, wsp, bsp, bsp],
                  out_shape=[SDS((LRU_HEADS, hd, hd), F32)] * 2 + [SDS((1, C), F32)] * 2,
                  compiler_params=_cparams(("parallel", "arbitrary")))(xc, dga, dgx)


@functools.partial(jax.custom_vjp, nondiff_argnums=(7,))
def lru_gates(xc, wa, wa_c, ba, wx, wx_c, bx, name):
    return tuple(_gates_fwd_call(xc, wa, ba, wx, bx, name))


def _lru_gates_fwd(xc, wa, wa_c, ba, wx, wx_c, bx, name):
    return tuple(_gates_fwd_call(xc, wa, ba, wx, bx, name)), (xc, wa, wx, ba.shape)


def _lru_gates_bwd(name, res, g):
    xc, wa, wx, bshape = res
    dga, dgx = g
    dxc = _gates_bwd_x_call(dga, dgx, wa, wx, name + "_dx")
    dwa, dwx, dba, dbx = _gates_bwd_w_call(xc, dga, dgx, name + "_dw")
    return dxc, jnp.zeros_like(wa), dwa, dba.reshape(bshape), jnp.zeros_like(wx), dwx, dbx.reshape(bshape)


lru_gates.defvjp(_lru_gates_fwd, _lru_gates_bwd)


def _lru_coeffs(xc, ga, gx, lam):
    r = _sigmoid(ga)
    ig = _sigmoid(gx)
    z = -lam
    sp = jnp.maximum(z, 0.0) + jnp.log(1.0 + jnp.exp(-jnp.abs(z)))
    la = -LRU_C * r * sp
    a = jnp.exp(la)
    s = jnp.sqrt(-_expm1(2.0 * la))
    return r, ig, sp, a, s


LRU_TT = 256


def _scan_fwd_call(xc, ga, gx, y, lam, name):
    T, C = xc.shape
    tt = _tile(T, LRU_TT, 8)

    def body(xc_ref, ga_ref, gx_ref, y_ref, lam_ref, h_ref, rec_ref, a_buf, carry):
        @pl.when(pl.program_id(0) == 0)
        def _():
            carry[...] = jnp.zeros_like(carry)

        xcv = xc_ref[...]
        _, ig, _, a, s = _lru_coeffs(xcv, ga_ref[...], gx_ref[...], lam_ref[...])
        a_buf[...] = a
        h_ref[...] = s * (ig * xcv)

        def step(t, h):
            hn = a_buf[pl.ds(t, 1), :] * h + h_ref[pl.ds(t, 1), :]
            h_ref[pl.ds(t, 1), :] = hn
            return hn

        carry[0:1, :] = lax.fori_loop(0, tt, step, carry[0:1, :], unroll=8)
        g, _ = _gelu_parts(y_ref[...])
        rec_ref[...] = h_ref[...] * g

    row = BS((tt, C), lambda i: (i, 0))
    vec = BS((1, C), lambda i: (0, 0))
    return _pcall(body, name=name, grid=(T // tt,), in_specs=[row, row, row, row, vec], out_specs=[row, row],
                  out_shape=[SDS((T, C), F32)] * 2,
                  scratch_shapes=[pltpu.VMEM((tt, C), F32), pltpu.VMEM((8, C), F32)],
                  compiler_params=_cparams(("arbitrary",)))(xc, ga, gx, y, lam.reshape(1, C))


def _scan_bwd_call(xc, ga, gx, y, lam, h, drec, name):
    T, C = xc.shape
    tt = _tile(T, LRU_TT, 8)
    nt = T // tt
    per = tt // 8

    def body(xc_ref, ga_ref, gx_ref, y_ref, lam_ref, h_ref, halo_ref, dr_ref,
             dga_ref, dgx_ref, dxc_ref, dy_ref, dlam_ref, a_buf, g_buf, carry):
        i = pl.program_id(0)

        @pl.when(i == 0)
        def _():
            carry[...] = jnp.zeros_like(carry)
            dlam_ref[...] = jnp.zeros_like(dlam_ref)

        xcv, lam = xc_ref[...], lam_ref[...]
        r, ig, sp, a, s = _lru_coeffs(xcv, ga_ref[...], gx_ref[...], lam)
        gel, dgel = _gelu_parts(y_ref[...])
        drec = dr_ref[...]
        hv = h_ref[...]
        dy_ref[...] = drec * hv * dgel
        a_buf[...] = a
        g_buf[...] = drec * gel

        def step(j, q):
            t = tt - 1 - j
            g = g_buf[pl.ds(t, 1), :] + q
            g_buf[pl.ds(t, 1), :] = g
            return a_buf[pl.ds(t, 1), :] * g

        carry[0:1, :] = lax.fori_loop(0, tt, step, carry[0:1, :], unroll=8)
        g = g_buf[...]
        halo = jnp.where(i < nt - 1, halo_ref[...], 0.0)
        hprev = pltpu.roll(jnp.concatenate([halo, hv], axis=0), 1, axis=0)[8:, :]
        da = g * hprev
        dig = g * s * xcv
        ds = g * ig * xcv
        dla = da * a - ds * (a * a) / s
        dga_ref[...] = dla * (-LRU_C * sp) * r * (1.0 - r)
        dgx_ref[...] = dig * ig * (1.0 - ig)
        dxc_ref[...] = g * s * ig
        dlam_ref[...] += jnp.sum(dla * r, axis=0, keepdims=True) * (LRU_C * _sigmoid(-lam))

    row = BS((tt, C), lambda i: (nt - 1 - i, 0))
    vec = BS((1, C), lambda i: (0, 0))
    halo = BS((8, C), lambda i: (jnp.maximum((nt - 1 - i) * per - 1, 0), 0))
    return _pcall(body, name=name, grid=(nt,), in_specs=[row, row, row, row, vec, row, halo, row],
                  out_specs=[row, row, row, row, vec], out_shape=[SDS((T, C), F32)] * 4 + [SDS((1, C), F32)],
                  scratch_shapes=[pltpu.VMEM((tt, C), F32), pltpu.VMEM((tt, C), F32), pltpu.VMEM((8, C), F32)],
                  compiler_params=_cparams(("arbitrary",)))(xc, ga, gx, y, lam.reshape(1, C), h, h, drec)


@functools.partial(jax.custom_vjp, nondiff_argnums=(5,))
def lru_scan(xc, ga, gx, y, lam, name):
    return _scan_fwd_call(xc, ga, gx, y, lam, name)[1]


def _lru_scan_fwd(xc, ga, gx, y, lam, name):
    h, rec = _scan_fwd_call(xc, ga, gx, y, lam, name)
    return rec, (xc, ga, gx, y, lam, h)


def _lru_scan_bwd(name, res, drec):
    xc, ga, gx, y, lam, h = res
    dga, dgx, dxc, dy, dlam = _scan_bwd_call(xc, ga, gx, y, lam, h, drec, name + "_bwd")
    return dxc, dga, dgx, dy, dlam.reshape(lam.shape)


lru_scan.defvjp(_lru_scan_fwd, _lru_scan_bwd)


def _att_batch(d, shared=False):
    if shared:
        return 8, 1
    return (4, 1) if d == 1 else (1, min(d, 8))


def _att_masks(n, max_dist):
    qi = lax.broadcasted_iota(jnp.int32, (1, 2 * ATT_BLOCK, 2 * ATT_BLOCK), 1) & (ATT_BLOCK - 1)
    kj = lax.broadcasted_iota(jnp.int32, (1, 2 * ATT_BLOCK, 2 * ATT_BLOCK), 2)
    prev = (kj < ATT_BLOCK) & (kj >= qi + (ATT_BLOCK - max_dist)) & (n > 0)
    cur = (kj >= ATT_BLOCK) & (kj - ATT_BLOCK <= qi)
    return prev | cur


def _lo_lanes(rows):
    return lax.broadcasted_iota(jnp.int32, (rows, LANES), 1) < HEAD_DIM


def _lane_half(rows):
    return lax.broadcasted_iota(jnp.int32, (rows, LANES), 1) // HEAD_DIM


def _stack_heads(x2):
    lo = _lo_lanes(ATT_BLOCK)
    zero = jnp.zeros_like(x2)
    return jnp.concatenate([jnp.where(lo, x2, zero), jnp.where(lo, zero, x2)], axis=0)


def _unstack_heads(y):
    return jnp.where(_lo_lanes(ATT_BLOCK), y[:ATT_BLOCK], y[ATT_BLOCK:])


def _per_head_col(x2):
    return jnp.concatenate([x2[:, 0:1], x2[:, HEAD_DIM:HEAD_DIM + 1]], axis=0)


def _head_sums(x2):
    lo = _lo_lanes(ATT_BLOCK)
    return jnp.concatenate([jnp.sum(jnp.where(lo, x2, 0.0), axis=1, keepdims=True),
                            jnp.sum(jnp.where(lo, 0.0, x2), axis=1, keepdims=True)], axis=0)


def _att_specs(d, Wq, Wk, nb, clamp):
    shared = Wk != Wq
    cgw, sb = _att_batch(d, shared)
    cur = (lambda n: jnp.minimum(n, nb - 1)) if clamp else (lambda n: n)
    rows, qw, kw = ATT_BLOCK * d, cgw * LANES, (LANES if shared else cgw * LANES)
    kcol = (lambda g: 0) if shared else (lambda g: g)
    qsp = BS((rows, qw), lambda g, n: (cur(n), g))
    csp = BS((rows, kw), lambda g, n: (cur(n), kcol(g)))
    psp = BS((rows, kw), lambda g, n: (jnp.maximum(cur(n) - 1, 0), kcol(g)))
    return cgw, sb, shared, qsp, csp, psp, qw, kw


def _att_streams(d, sb, work):
    if d == 1:
        work([slice(None)])
        return

    def one(j, carry):
        work([pl.ds(j * sb + i, ATT_BLOCK, stride=d) for i in range(sb)])
        return carry

    lax.fori_loop(0, d // sb, one, 0)


def _att_problem_loads(rows, cgw, shared, g, q_ref, kc_ref, kp_ref, vc_ref, vp_ref, sk_ref):
    half = _lane_half(ATT_BLOCK)

    def kv(ref, r, p):
        x = ref[r, :]
        if not shared:
            return x[:, p * LANES:(p + 1) * LANES]
        return jnp.where(half == p // 4, x, pltpu.roll(x, HEAD_DIM, axis=1))

    qs, kb, vb, sk = [], [], [], []
    for r in rows:
        qrow = q_ref[r, :]
        for p in range(cgw):
            cols = slice(p * LANES, (p + 1) * LANES)
            qs.append(_stack_heads(qrow[:, cols].astype(BF16)))
            kb.append(jnp.concatenate([kv(kp_ref, r, p), kv(kc_ref, r, p)], axis=0).astype(BF16))
            vb.append(jnp.concatenate([kv(vp_ref, r, p), kv(vc_ref, r, p)], axis=0).astype(BF16))
            sk.append(_per_head_col(jnp.broadcast_to(sk_ref[:, cols], (ATT_BLOCK, LANES))))
    return jnp.stack(qs), jnp.stack(kb), jnp.stack(vb), jnp.stack(sk)


_BDOT_NT = (((2,), (2,)), ((0,), (0,)))
_BDOT_NN = (((2,), (1,)), ((0,), (0,)))
_BDOT_TN = (((1,), (1,)), ((0,), (0,)))


def _att_fwd_call(q, k, v, sinks, d, max_dist, name):
    T, Wq = q.shape
    Wk = k.shape[1]
    nb = T // (d * ATT_BLOCK)
    cgw, sb, shared, qsp, csp, psp, qw, kw = _att_specs(d, Wq, Wk, nb, False)
    G = Wq // qw
    assert not shared or (d == 1 and Wk == LANES and G == 1 and cgw == 8), "a shared kv pair serves 2 x 8 query heads"

    def body(q_ref, kc_ref, kp_ref, vc_ref, vp_ref, sk_ref, o_ref, lse_ref):
        g, n = pl.program_id(0), pl.program_id(1)

        def work(rows):
            qs, kband, vband, sk = _att_problem_loads(rows, cgw, shared, g, q_ref, kc_ref, kp_ref, vc_ref, vp_ref, sk_ref)
            s = lax.dot_general(qs, kband, _BDOT_NT, preferred_element_type=F32) * (HEAD_DIM ** -0.5)
            s = jnp.where(_att_masks(n, max_dist), s, NEG)
            m = jnp.maximum(jnp.max(s, axis=-1, keepdims=True), sk)
            e = jnp.exp(s - m)
            den = jnp.sum(e, axis=-1, keepdims=True) + jnp.exp(sk - m)
            o = lax.dot_general((e * (1.0 / den)).astype(BF16), vband, _BDOT_NN, preferred_element_type=F32)
            lse = jnp.broadcast_to(m + jnp.log(den), o.shape)
            for i, r in enumerate(rows):
                o_ref[r, :] = jnp.concatenate([_unstack_heads(o[i * cgw + p]) for p in range(cgw)], axis=1)
                lse_ref[r, :] = jnp.concatenate([_unstack_heads(lse[i * cgw + p]) for p in range(cgw)], axis=1)

        _att_streams(d, sb, work)

    sksp = BS((1, qw), lambda g, n: (0, g))
    return _pcall(body, name=name, grid=(G, nb), in_specs=[qsp, csp, psp, csp, psp, sksp], out_specs=[qsp, qsp],
                  out_shape=[SDS((T, Wq), F32)] * 2,
                  compiler_params=_cparams(("parallel", "parallel")))(q, k, k, v, v, sinks)


def _att_bwd_call(q, k, v, sinks, o, lse, do, dlse, d, max_dist, name):
    T, Wq = q.shape
    Wk = k.shape[1]
    nb = T // (d * ATT_BLOCK)
    cgw, sb, shared, qsp, csp, psp, qw, kw = _att_specs(d, Wq, Wk, nb, True)
    G = Wq // qw
    scale = HEAD_DIM ** -0.5

    def body(*refs):
        q_ref, kc_ref, kp_ref, vc_ref, vp_ref, sk_ref, o_ref, lse_ref, do_ref = refs[:9]
        dlse_ref = refs[9] if dlse is not None else None
        dq_ref, dk_ref, dv_ref, dsk_ref, ck, cv = refs[-6:]
        g, n = pl.program_id(0), pl.program_id(1)

        @pl.when(n == 0)
        def _():
            ck[...] = jnp.zeros_like(ck)
            cv[...] = jnp.zeros_like(cv)
            dsk_ref[...] = jnp.zeros_like(dsk_ref)

        def work(rows):
            qs, kband, vband, sk = _att_problem_loads(rows, cgw, shared, g, q_ref, kc_ref, kp_ref, vc_ref, vp_ref, sk_ref)
            dos, lse_c, corr = [], [], []
            for r in rows:
                do_r, o_r, lse_r = do_ref[r, :], o_ref[r, :], lse_ref[r, :]
                dlse_r = dlse_ref[r, :] if dlse_ref is not None else None
                for p in range(cgw):
                    cols = slice(p * LANES, (p + 1) * LANES)
                    dos.append(_stack_heads(do_r[:, cols].astype(BF16)))
                    lse_c.append(_per_head_col(lse_r[:, cols]))
                    delta = _head_sums(do_r[:, cols] * o_r[:, cols])
                    corr.append(-delta if dlse_r is None else _head_sums(dlse_r[:, cols]) - delta)
            dos, lse_c, corr = jnp.stack(dos), jnp.stack(lse_c), jnp.stack(corr)
            s = lax.dot_general(qs, kband, _BDOT_NT, preferred_element_type=F32) * scale
            pr = jnp.exp(jnp.where(_att_masks(n, max_dist), s, NEG) - lse_c)
            dp = lax.dot_general(dos, vband, _BDOT_NT, preferred_element_type=F32)
            ds = (pr * (dp + corr)).astype(BF16)
            dq = lax.dot_general(ds, kband, _BDOT_NN, preferred_element_type=F32) * scale
            dkb = lax.dot_general(ds, qs, _BDOT_TN, preferred_element_type=F32) * scale
            dvb = lax.dot_general(pr.astype(BF16), dos, _BDOT_TN, preferred_element_type=F32)
            dsk = jnp.exp(sk - lse_c) * corr
            lane = lax.broadcasted_iota(jnp.int32, (8, LANES), 1)
            for p in range(cgw):
                tot = [jnp.sum(jnp.stack([dsk[i * cgw + p, h * ATT_BLOCK:(h + 1) * ATT_BLOCK] for i in range(len(rows))]),
                               axis=(0, 1)).reshape(1, 1) for h in range(2)]
                dsk_ref[:, p * LANES:(p + 1) * LANES] += jnp.where(lane == 0, tot[0], jnp.where(lane == HEAD_DIM, tot[1], 0.0))

            def gather_pairs(parts):
                if not shared:
                    return jnp.concatenate(parts, axis=1)
                tot = [parts[4 * h] + parts[4 * h + 1] + parts[4 * h + 2] + parts[4 * h + 3] for h in range(2)]
                tot = [t + pltpu.roll(t, HEAD_DIM, axis=1) for t in tot]
                return jnp.where(_lo_lanes(ATT_BLOCK), tot[0], tot[1])

            for i, r in enumerate(rows):
                mine = range(i * cgw, (i + 1) * cgw)
                dq_ref[r, :] = jnp.concatenate([_unstack_heads(dq[b]) for b in mine], axis=1)
                dk_ref[r, :] = ck[r, :] + gather_pairs([dkb[b, :ATT_BLOCK] for b in mine])
                dv_ref[r, :] = cv[r, :] + gather_pairs([dvb[b, :ATT_BLOCK] for b in mine])
                ck[r, :] = gather_pairs([dkb[b, ATT_BLOCK:] for b in mine])
                cv[r, :] = gather_pairs([dvb[b, ATT_BLOCK:] for b in mine])

        @pl.when(n < nb)
        def _():
            _att_streams(d, sb, work)

        @pl.when(n == nb)
        def _():
            dk_ref[...] = ck[...]
            dv_ref[...] = cv[...]

    sksp = BS((1, qw), lambda g, n: (0, g))
    rows = ATT_BLOCK * d
    osp = BS((rows, kw), lambda g, n: (jnp.maximum(n - 1, 0), 0 if shared else g))
    kshape = SDS((T, Wk), F32)
    dq, dk, dv, dsk = _pcall(
        body, name=name, grid=(G, nb + 1),
        in_specs=[qsp, csp, psp, csp, psp, sksp, qsp, qsp, qsp] + ([qsp] if dlse is not None else []),
        out_specs=[qsp, osp, osp, BS((8, qw), lambda g, n: (0, g))],
        out_shape=[SDS((T, Wq), F32), kshape, kshape, SDS((8, Wq), F32)],
        scratch_shapes=[pltpu.VMEM((rows, kw), F32)] * 2,
        compiler_params=_cparams(("parallel", "arbitrary")))(q, k, k, v, v, sinks, o, lse, do,
                                                              *([dlse] if dlse is not None else []))
    return dq, dk, dv, dsk[0:1]


@functools.partial(jax.custom_vjp, nondiff_argnums=(4, 5, 6, 7))
def band_attention(q, k, v, sinks, d, max_dist, with_lse, name):
    return _band_attention_fwd(q, k, v, sinks, d, max_dist, with_lse, name)[0]


def _band_attention_fwd(q, k, v, sinks, d, max_dist, with_lse, name):
    o, lse = _att_fwd_call(q, k, v, sinks, d, max_dist, name)
    return ((o, lse) if with_lse else o), (q, k, v, sinks, o, lse)


def _band_attention_bwd(d, max_dist, with_lse, name, res, g):
    q, k, v, sinks, o, lse = res
    do, dlse = g if with_lse else (g, None)
    return _att_bwd_call(q, k, v, sinks, o, lse, do, dlse, d, max_dist, name + "_bwd")


band_attention.defvjp(_band_attention_fwd, _band_attention_bwd)


def _merge_weights(ls):
    mx = jnp.maximum(jnp.maximum(ls[0], ls[1]), ls[2])
    es = [jnp.exp(l - mx) for l in ls]
    inv = 1.0 / (es[0] + es[1] + es[2])
    return [e * inv for e in es]


def _merge_fwd_call(os_, ls_, name):
    T, W = os_[0].shape
    tt = _tile(T, 512, 8)

    def body(o1, o2, o3, l1, l2, l3, out_ref):
        w = _merge_weights([l1[...], l2[...], l3[...]])
        out_ref[...] = w[0] * o1[...] + w[1] * o2[...] + w[2] * o3[...]

    row = BS((tt, W), lambda i: (i, 0))
    return _pcall(body, name=name, grid=(T // tt,), in_specs=[row] * 6, out_specs=row,
                  out_shape=SDS((T, W), F32), compiler_params=_cparams(("parallel",)))(*os_, *ls_)


def _merge_bwd_call(os_, ls_, do, name):
    T, W = os_[0].shape
    tt = _tile(T, 512, 8)

    def body(o1, o2, o3, l1, l2, l3, do_ref, d1, d2, d3, e1, e2, e3):
        w = _merge_weights([l1[...], l2[...], l3[...]])
        dov = do_ref[...]
        ts = [dov * o[...] for o in (o1, o2, o3)]
        mean = w[0] * ts[0] + w[1] * ts[1] + w[2] * ts[2]
        for wi, ti, dref, eref in zip(w, ts, (d1, d2, d3), (e1, e2, e3)):
            dref[...] = wi * dov
            eref[...] = wi * (ti - mean)

    row = BS((tt, W), lambda i: (i, 0))
    return _pcall(body, name=name, grid=(T // tt,), in_specs=[row] * 7, out_specs=[row] * 6,
                  out_shape=[SDS((T, W), F32)] * 6, compiler_params=_cparams(("parallel",)))(*os_, *ls_, do)


@functools.partial(jax.custom_vjp, nondiff_argnums=(2,))
def merge3(os_, ls_, name):
    return _merge_fwd_call(os_, ls_, name)


def _merge3_fwd(os_, ls_, name):
    return _merge_fwd_call(os_, ls_, name), (os_, ls_)


def _merge3_bwd(name, res, do):
    os_, ls_ = res
    out = _merge_bwd_call(os_, ls_, do, name + "_bwd")
    return tuple(out[:3]), tuple(out[3:])


merge3.defvjp(_merge3_fwd, _merge3_bwd)


def _xa_probs(qb, kb, scale):
    s = lax.dot_general(qb, kb, (((1,), (1,)), ((), ())), preferred_element_type=F32) * scale
    e = jnp.exp(s - jnp.max(s, axis=-1, keepdims=True))
    return e / jnp.sum(e, axis=-1, keepdims=True)


def _xa_fwd_call(q, kv, name):
    T, W = q.shape
    M = kv.shape[0]
    hd = XA_HEAD_DIM
    tq = _tile(T, 4096, 8)
    scale = hd ** -0.5

    def body(q_ref, k_ref, v_ref, o_ref):
        p = _xa_probs(q_ref[...].astype(BF16), k_ref[...].astype(BF16), scale)
        o_ref[...] = jnp.dot(p.astype(BF16), v_ref[...].astype(BF16), preferred_element_type=F32)

    qsp = BS((tq, hd), lambda i, h: (i, h))
    return _pcall(body, name=name, grid=(T // tq, XA_HEADS),
                  in_specs=[qsp, BS((M, hd), lambda i, h: (0, h)), BS((M, hd), lambda i, h: (0, XA_HEADS + h))],
                  out_specs=qsp, out_shape=SDS((T, W), F32),
                  compiler_params=_cparams(("parallel", "parallel")))(q, kv, kv)


def _xa_bwd_call(q, kv, do, name):
    T, W = q.shape
    M = kv.shape[0]
    hd = XA_HEAD_DIM
    tq = _tile(T, 4096, 8)
    scale = hd ** -0.5
    dn_nt = (((1,), (1,)), ((), ()))
    dn_tn = (((0,), (0,)), ((), ()))

    def body(q_ref, k_ref, v_ref, do_ref, dq_ref, dk_ref, dv_ref):
        @pl.when(pl.program_id(1) == 0)
        def _():
            dk_ref[...] = jnp.zeros_like(dk_ref)
            dv_ref[...] = jnp.zeros_like(dv_ref)

        qb, kb, vb = q_ref[...].astype(BF16), k_ref[...].astype(BF16), v_ref[...].astype(BF16)
        p = _xa_probs(qb, kb, scale)
        dob = do_ref[...].astype(BF16)
        dp = lax.dot_general(dob, vb, dn_nt, preferred_element_type=F32)
        ds = (p * (dp - jnp.sum(p * dp, axis=-1, keepdims=True))).astype(BF16)
        dq_ref[...] = jnp.dot(ds, kb, preferred_element_type=F32) * scale
        dk_ref[...] += lax.dot_general(ds, qb, dn_tn, preferred_element_type=F32) * scale
        dv_ref[...] += lax.dot_general(p.astype(BF16), dob, dn_tn, preferred_element_type=F32)

    qsp = BS((tq, hd), lambda h, i: (i, h))
    ksp = BS((M, hd), lambda h, i: (0, h))
    return _pcall(body, name=name, grid=(XA_HEADS, T // tq),
                  in_specs=[qsp, ksp, BS((M, hd), lambda h, i: (0, XA_HEADS + h)), qsp],
                  out_specs=[qsp, ksp, ksp], out_shape=[SDS((T, W), F32), SDS((M, W), F32), SDS((M, W), F32)],
                  compiler_params=_cparams(("parallel", "arbitrary")))(q, kv, kv, do)


@functools.partial(jax.custom_vjp, nondiff_argnums=(2,))
def cross_attention(q, kv, name):
    return _xa_fwd_call(q, kv, name)


def _cross_attention_fwd(q, kv, name):
    return _xa_fwd_call(q, kv, name), (q, kv)


def _cross_attention_bwd(name, res, do):
    q, kv = res
    dq, dk, dv = _xa_bwd_call(q, kv, do, name + "_bwd")
    return dq, jnp.concatenate([dk, dv], axis=1)


cross_attention.defvjp(_cross_attention_fwd, _cross_attention_bwd)


def _gate_up_swiglu_call(hn, w1t, name):
    T, K = hn.shape
    F = w1t.shape[0] // 2
    tm, tn = _tile(T, 4096), _tile(F, 256)
    nj = F // tn
    dn = (((1,), (1,)), ((), ()))

    def body(a_ref, wg_ref, wu_ref, g_ref, u_ref, act_ref):
        a = a_ref[...]
        g = lax.dot_general(a, wg_ref[...], dn, preferred_element_type=F32)
        u = lax.dot_general(a, wu_ref[...], dn, preferred_element_type=F32)
        g_ref[...] = g
        u_ref[...] = u
        act_ref[...] = ((g * _sigmoid(g)) * u).astype(BF16)

    tile = BS((tm, tn), lambda i, j: (i, j))
    return _pcall(body, name=name, grid=(T // tm, nj),
                  in_specs=[BS((tm, K), lambda i, j: (i, 0)), BS((tn, K), lambda i, j: (j, 0)),
                            BS((tn, K), lambda i, j: (j + nj, 0))],
                  out_specs=[tile, tile, tile], out_shape=[SDS((T, F), F32), SDS((T, F), F32), SDS((T, F), BF16)],
                  compiler_params=_cparams(("parallel", "parallel")))(hn, w1t, w1t)


def _swiglu_bwd_call(g, u, dact, name):
    T, F = g.shape
    tt = _tile(T, 256, 16)

    def body(g_ref, u_ref, d_ref, o_ref):
        g, d = g_ref[...], d_ref[...]
        sg = _sigmoid(g)
        o_ref[:, :F] = (d * u_ref[...] * (sg * (1.0 + g * (1.0 - sg)))).astype(BF16)
        o_ref[:, F:] = (d * (g * sg)).astype(BF16)

    row = BS((tt, F), lambda i: (i, 0))
    return _pcall(body, name=name, grid=(T // tt,), in_specs=[row, row, row],
                  out_specs=BS((tt, 2 * F), lambda i: (i, 0)), out_shape=SDS((T, 2 * F), BF16),
                  compiler_params=_cparams(("parallel",)))(g, u, dact)


@functools.partial(jax.custom_vjp, nondiff_argnums=(6,))
def ffn_block(h, g, w1b, w1c, w2b, w2c, name):
    return _ffn_fwd(h, g, w1b, w1c, w2b, w2c, name)[0]


def _ffn_fwd(h, g, w1b, w1c, w2b, w2c, name):
    hn = _rms_fwd_call(h, g, name + "_norm", BF16)
    gate, up, act = _gate_up_swiglu_call(hn, w1b, name + "_gu")
    out = _mm(act, w2b, add=h, name=name + "_down", tm=1024, tn=1024, tk=2816)
    return out, (h, g, hn, gate, up, act, w1b, w2b)


def _ffn_bwd(name, res, dout):
    h, g, hn, gate, up, act, w1b, w2b = res
    dact = _mm(dout, w2b, tb=True, name=name + "_down_da", tm=1024, tn=1408, tk=1024)
    dw2 = _mm(act, dout, ta=True, name=name + "_down_dw", tm=1408, tn=1024, tk=1024)
    dgu = _swiglu_bwd_call(gate, up, dact, name + "_swiglu_bwd")
    dw1 = _mm(dgu, hn, ta=True, name=name + "_gu_dw", tm=1408, tn=1024, tk=2048)
    dh, dg = _mm_rms_bwd(dgu, w1b, h, g, dout, name + "_gu_da", wt=True)
    return dh, dg.reshape(g.shape), jnp.zeros_like(w1b), dw1, jnp.zeros_like(w2b), dw2


ffn_block.defvjp(_ffn_fwd, _ffn_bwd)


def _final_call(h, g, target, name):
    T, Dm = h.shape
    tt = _tile(T, 1024, 8)

    def body(x_ref, g_ref, t_ref, loss_ref, dx_ref, dg_ref):
        @pl.when(pl.program_id(0) == 0)
        def _():
            loss_ref[...] = jnp.zeros_like(loss_ref)
            dg_ref[...] = jnp.zeros_like(dg_ref)

        xv, gv = x_ref[...], g_ref[...]
        r = lax.rsqrt(jnp.mean(xv * xv, axis=-1, keepdims=True) + NORM_EPS)
        xh = xv * r
        err = xh * gv - t_ref[...]
        loss_ref[...] += 0.5 * jnp.sum(jnp.mean(err * err, axis=-1, keepdims=True), axis=0, keepdims=True)
        dy = err * (1.0 / Dm)
        dyg = dy * gv
        dx_ref[...] = r * (dyg - xh * jnp.mean(dyg * xh, axis=-1, keepdims=True))
        dg_ref[...] += jnp.sum(dy * xh, axis=0, keepdims=True)

    row = BS((tt, Dm), lambda i: (i, 0))
    vec = BS((1, Dm), lambda i: (0, 0))
    return _pcall(body, name=name, grid=(T // tt,), in_specs=[row, vec, row],
                  out_specs=[BS((1, 1), lambda i: (0, 0)), row, vec],
                  out_shape=[SDS((1, 1), F32), SDS((T, Dm), F32), SDS((1, Dm), F32)],
                  compiler_params=_cparams(("arbitrary",)))(h, g.reshape(1, Dm), target)


ADAMW_BLOCK_ELEMS = 256 * 1024


def _adamw_call(parts, w, m, v, name):
    shape = w.shape
    if not isinstance(parts, (list, tuple)):
        parts, shape3 = [parts], (1,) + shape
    else:
        shape3 = shape
    n_lead = shape3[0]
    r, N = shape3[-2], shape3[-1]
    Ld = math.prod(shape3[1:-2])
    w, m, v = (t.reshape(n_lead * Ld, r, N) for t in (w, m, v))
    tr = _tile(r, max(8, ADAMW_BLOCK_ELEMS // N), 8)
    c1 = 1.0 - ADAM_B1 ** ADAM_STEP
    c2 = 1.0 - ADAM_B2 ** ADAM_STEP
    outs = None
    for lead, p in enumerate(parts):
        def body(p_ref, w_ref, m_ref, v_ref, *rest):
            g_ref, d_ref, nm_ref, nv_ref = rest[-4:]
            g = p_ref[0]
            for j in range(1, N_DEV):
                g = g + p_ref[j]
            nm = ADAM_B1 * m_ref[...] + (1.0 - ADAM_B1) * g
            nv = ADAM_B2 * v_ref[...] + (1.0 - ADAM_B2) * (g * g)
            g_ref[...] = g
            nm_ref[...] = nm
            nv_ref[...] = nv
            d_ref[...] = -ADAM_LR * ((nm / c1) / (jnp.sqrt(nv / c2) + ADAM_EPS) + ADAM_WD * w_ref[...])

        base = lead * Ld
        row = BS((1, tr, N), lambda l, i, base=base: (base + l, i, 0))
        prev = [] if outs is None else list(outs)
        outs = _pcall(body, name=f"{name}_{lead}", grid=(Ld, r // tr),
                      in_specs=[BS((N_DEV, 1, tr, N), lambda l, i: (0, l, i, 0)), row, row, row]
                      + [BS(memory_space=pl.ANY)] * len(prev),
                      out_specs=[row] * 4, out_shape=[SDS((n_lead * Ld, r, N), F32)] * 4,
                      input_output_aliases={4 + j: j for j in range(len(prev))},
                      compiler_params=_cparams(("parallel", "parallel")))(p.reshape(N_DEV, Ld, r, N), w, m, v, *prev)
    return [t.reshape(shape) for t in outs]


def _place():
    return lax.axis_index("x"), lax.axis_index("y"), lax.axis_index("c")


def _all_gather(xs, name):
    n = len(xs)
    pairs = [(i, l) for i, x in enumerate(xs) for l in range(x.shape[0])]

    def body(*refs):
        x_refs, o_refs = refs[:n], refs[n:2 * n]
        send_sems, recv_sems, local_sems = refs[2 * n:]
        x_, y_, c_ = _place()
        me, sibling = (x_, y_, c_), (x_, y_, 1 - c_)
        chips = [(1 - x_, y_), (x_, 1 - y_), (1 - x_, 1 - y_)]

        def copy(e, k, block, to, from_input=False):
            i, l = pairs[e]
            px, py, pc = block
            dst = o_refs[i].at[l, 4 * px + 2 * py + pc]
            return pltpu.make_async_remote_copy(
                src_ref=x_refs[i].at[l] if from_input else dst, dst_ref=dst,
                send_sem=send_sems.at[7 * e + k], recv_sem=recv_sems.at[7 * e + k],
                device_id=to, device_id_type=pl.DeviceIdType.MESH)

        every = range(len(pairs))
        mine = [pltpu.make_async_copy(x_refs[i].at[l], o_refs[i].at[l, 4 * x_ + 2 * y_ + c_], local_sems.at[e])
                for e, (i, l) in enumerate(pairs)]
        for cp in mine:
            cp.start()
        first = [copy(e, 0, me, sibling, True) for e in every]
        first += [copy(e, 1 + j, me, (*chip, c_), True) for j, chip in enumerate(chips) for e in every]
        for cp in first:
            cp.start()
        passed = []
        for j, chip in enumerate(chips):
            for e in every:
                copy(e, 1 + j, (*chip, c_), me).wait_recv()
            for e in every:
                cp = copy(e, 4 + j, (*chip, c_), sibling)
                cp.start()
                passed.append(cp)
        for e in every:
            copy(e, 0, sibling, me).wait_recv()
        for j, chip in enumerate(chips):
            for e in every:
                copy(e, 4 + j, (*chip, 1 - c_), me).wait_recv()
        for cp in first + passed:
            cp.wait_send()
        for cp in mine:
            cp.wait()

    any_spec = BS(memory_space=pl.ANY)
    return _pcall(body, name=name, in_specs=[any_spec] * n, out_specs=[any_spec] * n,
                  out_shape=[SDS((x.shape[0], N_DEV) + x.shape[1:], x.dtype) for x in xs],
                  scratch_shapes=[pltpu.SemaphoreType.DMA((7 * len(pairs),)), pltpu.SemaphoreType.DMA((7 * len(pairs),)),
                                  pltpu.SemaphoreType.DMA((len(pairs),))],
                  compiler_params=pltpu.CompilerParams(has_side_effects=True))(*xs)


def _peer_of(k, place):
    x_, y_, c_ = place
    fx, fy, fc = (k >> 2) & 1, (k >> 1) & 1, k & 1
    return fx + x_ - 2 * fx * x_, fy + y_ - 2 * fy * y_, fc + c_ - 2 * fc * c_


def _split_copy(src_ref, land_ref, send_sems, recv_sems, e, k, place, scatter):
    x_, y_, c_ = place
    px, py, pc = _peer_of(k, place)
    return pltpu.make_async_remote_copy(
        src_ref=src_ref.at[4 * px + 2 * py + pc] if scatter else src_ref, dst_ref=land_ref.at[4 * x_ + 2 * y_ + c_],
        send_sem=send_sems.at[7 * e + k - 1], recv_sem=recv_sems.at[7 * e + k - 1],
        device_id=(px, py, pc), device_id_type=pl.DeviceIdType.MESH)


def _own_copy(src_ref, land_ref, sems, slot, place, scatter):
    x_, y_, c_ = place
    me = 4 * x_ + 2 * y_ + c_
    return pltpu.make_async_copy(src_ref.at[me] if scatter else src_ref, land_ref.at[me], sems.at[slot])


_HBM_SPEC = BS(memory_space=pltpu.HBM)
_SEM_SPEC = BS(memory_space=pltpu.SEMAPHORE)
_EFFECT = pltpu.SideEffectType.DATAFLOW_SIDE_EFFECTING


def _copies_start(srcs, scatter, name, thru=None):
    n = len(srcs)
    lands = [lax.empty(s.shape if scatter else (N_DEV,) + s.shape, s.dtype) for s in srcs]
    passed = srcs + lands + list(thru or ())

    def body(*refs):
        src_refs, land_refs = refs[:n], refs[n:2 * n]
        send_sems, recv_sems = refs[len(passed)], refs[len(passed) + 1]
        token = refs[-1]
        place = _place()
        for e in range(n):
            for k in range(1, N_DEV):
                _split_copy(src_refs[e], land_refs[e], send_sems, recv_sems, e, k, place, scatter).start()
            _own_copy(src_refs[e], land_refs[e], send_sems, 7 * n + e, place, scatter).start()
        token[...] = jnp.zeros_like(token)

    hbm = lambda t: pltpu.with_memory_space_constraint(t, pltpu.HBM)
    out = _pcall(body, name=name,
                 out_shape=(pltpu.SemaphoreType.DMA((8 * n,)), pltpu.SemaphoreType.DMA((7 * n,)),
                            *[pltpu.HBM(t.shape, t.dtype) for t in passed], SDS((8, LANES), F32)),
                 in_specs=[_HBM_SPEC] * len(passed),
                 out_specs=(_SEM_SPEC, _SEM_SPEC, *[_HBM_SPEC] * len(passed), BS(memory_space=pltpu.VMEM)),
                 input_output_aliases={i: 2 + i for i in range(len(passed))},
                 compiler_params=pltpu.CompilerParams(has_side_effects=_EFFECT))(*[hbm(t) for t in passed])
    return out[0], out[1], list(out[2:2 + n]), list(out[2 + n:2 + 2 * n]), list(out[2 + 2 * n:-1])


def _copies_wait(started, which, scatter, after, name):
    send_sems, recv_sems, srcs, lands, _ = started
    n, n_started = len(which), len(srcs)
    after = list(after) if isinstance(after, (list, tuple)) else [after]

    def body(*refs):
        src_refs, land_refs = refs[:n], refs[n:2 * n]
        send_s, recv_s = refs[2 * n], refs[2 * n + 1]
        place = _place()
        for j, e in enumerate(which):
            for k in range(1, N_DEV):
                cp = _split_copy(src_refs[j], land_refs[j], send_s, recv_s, e, k, place, scatter)
                cp.wait_send()
                cp.wait_recv()
            _own_copy(src_refs[j], land_refs[j], send_s, 7 * n_started + e, place, scatter).wait()

    args = [srcs[e] for e in which] + [lands[e] for e in which]
    out = _pcall(body, name=name, out_shape=tuple(pltpu.HBM(t.shape, t.dtype) for t in args),
                 in_specs=[_HBM_SPEC] * (2 * n) + [_SEM_SPEC, _SEM_SPEC] + [BS(memory_space=pl.ANY)] * len(after),
                 out_specs=tuple([_HBM_SPEC] * (2 * n)), input_output_aliases={i: i for i in range(2 * n)},
                 compiler_params=pltpu.CompilerParams(has_side_effects=_EFFECT))(*args, send_sems, recv_sems, *after)
    return list(out[:n]), list(out[n:])


def _pad_flat(t, quantum=PACK_QUANTUM):
    f = t.reshape(-1)
    pad = (-f.shape[0]) % quantum
    return jnp.pad(f, (0, pad)) if pad else f


def _pack(arrs, dtype):
    return jnp.concatenate([_pad_flat(a.astype(dtype)) for a in arrs]).reshape(-1, LANES)


def _unpack(buf, shapes, lead=()):
    flat = buf.reshape(lead + (-1,))
    out, off = [], 0
    for s in shapes:
        n = math.prod(s)
        out.append(flat[..., off:off + n].reshape(lead + tuple(s)))
        off += n + (-n) % PACK_QUANTUM
    return out


def _full_from_gathered(g, axis):
    t = jnp.moveaxis(g, 0, axis)
    s = t.shape
    return t.reshape(s[:axis] + (s[axis] * s[axis + 1],) + s[axis + 2:])


def _parts_from_full(t, axis):
    s = t.shape
    t = t.reshape(s[:axis] + (N_DEV, s[axis] // N_DEV) + s[axis + 1:])
    return jnp.moveaxis(t, axis, 0)


def _head_rows(t):
    return jnp.repeat(t, HEAD_DIM).reshape(1, -1)


def _dilated_attention(q, k, v, name):
    no_sink = jnp.full((1, q.shape[1]), NEG, F32)
    outs, lses = zip(*[band_attention(q, k, v, no_sink, d, ATT_BLOCK, True, f"{name}_d{d}") for d in DILATIONS])
    return merge3(tuple(outs), tuple(lses), name + "_merge")


STAGES = (
    ("proj0", ('mix_norm',), (('ab_w_in', 0),)),
    ("mixer0", ('lru_conv_w', 'lru_conv_b', 'lru_ba', 'lru_bx', 'lru_lambda'),
     (('lru_wa', 0), ('lru_wx', 0), ('ab_w_out', 0))),
    ("xa0", ('xa_norm', 'xa_mem_norm'), (('xa_wq', 0), ('xa_wkv', 0), ('xa_wo', 0))),
    ("ffn0", ('ffn_norm',), (('ffn_w_gate_up', 0), ('ffn_w_down', 0))),
    ("mixer1", ('mix_norm', 'c_b_qkv', 'c_sinks', 'c_b_out'), (('c_w_qkv', 0), ('c_w_out', 0))),
    ("xa1", ('xa_norm', 'xa_mem_norm'), (('xa_wq', 1), ('xa_wkv', 1), ('xa_wo', 1))),
    ("ffn1", ('ffn_norm',), (('ffn_w_gate_up', 1), ('ffn_w_down', 1))),
)


def _stage_fn(stage, Wb, tabs, mem):
    layer = int(stage[-1])
    L = f"l{layer}"

    def run(S, Cw, h):
        def lin(a, key, bias, add, name, rows=None):
            wb, wc = Wb[key], Cw[key]
            if rows is not None:
                wb, wc = wb[rows], wc[rows]
            return linear(a, wb, wc, bias, add, name)

        def norm_lin(a, gain, key, bias, name):
            return norm_linear(a, gain, Wb[key], Cw[key], bias, key[0] in COLUMN_CUT, name)

        if stage == "mixer0":
            h, x_br, y_br, q, k, v = h
            C = S['lru_conv_w'].shape[-1]
            xc = conv4(x_br, S['lru_conv_w'][0], S['lru_conv_b'][0], L + "_conv")
            ga, gx = lru_gates(xc, Wb['lru_wa', 0], Cw['lru_wa', 0], S['lru_ba'][0],
                               Wb['lru_wx', 0], Cw['lru_wx', 0], S['lru_bx'][0], L + "_gates")
            rec = lru_scan(xc, ga, gx, y_br, S['lru_lambda'][0], L + "_scan")
            att = _dilated_attention(q, k, v, L + "_att")
            h = lin(att, ('ab_w_out', 0), None, h, L + "_w_out_att", slice(C, None))
            return lin(rec, ('ab_w_out', 0), None, h, L + "_w_out_rec", slice(0, C))
        if stage == "mixer1":
            qw = C_HEADS * HEAD_DIM
            kw = C_KV_HEADS * HEAD_DIM
            q, k, v, h = norm_linear_pieces(h, S['mix_norm'][1], Wb['c_w_qkv', 0], Cw['c_w_qkv', 0], S['c_b_qkv'][0], tabs,
                                            ((qw, True), (kw, True), (kw, False)), L + "_w_qkv")
            o = band_attention(q, k, v, _head_rows(S['c_sinks'][0]), 1, ATT_BLOCK - 1, False, L + "_att")
            return lin(o, ('c_w_out', 0), S['c_b_out'][0], h, L + "_w_out")
        if stage.startswith("xa"):
            xq, h = norm_lin(h, S['xa_norm'][layer], ('xa_wq', layer), None, L + "_xa_wq")
            xkv, _ = norm_lin(mem, S['xa_mem_norm'][layer], ('xa_wkv', layer), None, L + "_xa_wkv")
            return lin(cross_attention(xq, xkv, L + "_xa"), ('xa_wo', layer), None, h, L + "_xa_wo")
        gu, down = ('ffn_w_gate_up', layer), ('ffn_w_down', layer)
        return ffn_block(h, S['ffn_norm'][layer], Wb[gu], Cw[gu], Wb[down], Cw[down], L + "_ffn")

    return run


def kernel(x, mem, mix_norm, ab_w_in, lru_conv_w, lru_conv_b, lru_wa, lru_ba, lru_wx, lru_bx, lru_lambda, ab_w_out, c_w_qkv, c_b_qkv, c_sinks, c_w_out, c_b_out, xa_norm, xa_mem_norm, xa_wq, xa_wkv, xa_wo, ffn_norm, ffn_w_gate_up, ffn_w_down, final_norm, loss_target, m_mix_norm, m_ab_w_in, m_lru_conv_w, m_lru_conv_b, m_lru_wa, m_lru_ba, m_lru_wx, m_lru_bx, m_lru_lambda, m_ab_w_out, m_c_w_qkv, m_c_b_qkv, m_c_sinks, m_c_w_out, m_c_b_out, m_xa_norm, m_xa_mem_norm, m_xa_wq, m_xa_wkv, m_xa_wo, m_ffn_norm, m_ffn_w_gate_up, m_ffn_w_down, m_final_norm, v_mix_norm, v_ab_w_in, v_lru_conv_w, v_lru_conv_b, v_lru_wa, v_lru_ba, v_lru_wx, v_lru_bx, v_lru_lambda, v_ab_w_out, v_c_w_qkv, v_c_b_qkv, v_c_sinks, v_c_w_out, v_c_b_out, v_xa_norm, v_xa_mem_norm, v_xa_wq, v_xa_wkv, v_xa_wo, v_ffn_norm, v_ffn_w_gate_up, v_ffn_w_down, v_final_norm):
    w_loc = dict(zip(WEIGHT_NAMES, (mix_norm, ab_w_in, lru_conv_w, lru_conv_b, lru_wa, lru_ba, lru_wx, lru_bx, lru_lambda, ab_w_out, c_w_qkv, c_b_qkv, c_sinks, c_w_out, c_b_out, xa_norm, xa_mem_norm, xa_wq, xa_wkv, xa_wo, ffn_norm, ffn_w_gate_up, ffn_w_down, final_norm)))
    m_loc = dict(zip(WEIGHT_NAMES, (m_mix_norm, m_ab_w_in, m_lru_conv_w, m_lru_conv_b, m_lru_wa, m_lru_ba, m_lru_wx, m_lru_bx, m_lru_lambda, m_ab_w_out, m_c_w_qkv, m_c_b_qkv, m_c_sinks, m_c_w_out, m_c_b_out, m_xa_norm, m_xa_mem_norm, m_xa_wq, m_xa_wkv, m_xa_wo, m_ffn_norm, m_ffn_w_gate_up, m_ffn_w_down, m_final_norm)))
    v_loc = dict(zip(WEIGHT_NAMES, (v_mix_norm, v_ab_w_in, v_lru_conv_w, v_lru_conv_b, v_lru_wa, v_lru_ba, v_lru_wx, v_lru_bx, v_lru_lambda, v_ab_w_out, v_c_w_qkv, v_c_b_qkv, v_c_sinks, v_c_w_out, v_c_b_out, v_xa_norm, v_xa_mem_norm, v_xa_wq, v_xa_wkv, v_xa_wo, v_ffn_norm, v_ffn_w_gate_up, v_ffn_w_down, v_final_norm)))

    first_keys = list(STAGES[0][2])
    keys = [key for _, _, stage_keys in STAGES[1:] for key in stage_keys]
    shards = [_shard_view(n, w_loc[n])[l].astype(BF16) for n, l in keys]
    first_g = _all_gather([_pack([w_loc[n] for n in SMALL], F32)[None]]
                          + [_shard_view(n, w_loc[n])[l].astype(BF16)[None] for n, l in first_keys], "gather_first")
    gather = _copies_start(shards, False, "gather_start", thru=[first_g[0]])
    small_g = gather[4][0][0]
    Wb = {key: _full_from_gathered(g[0], _layer_shard_axis(key[0])) for key, g in zip(first_keys, first_g[1:])}
    S = {n: w_loc[n] for n in REPLICATED}
    for n, t in zip(SMALL, _unpack(small_g, [w_loc[n].shape for n in SMALL], lead=(N_DEV,))):
        S[n] = _full_from_gathered(t, SHARD_AXIS[n])

    tabs = _rope_tables(x.shape[1])
    w_in = Wb[first_keys[0]]
    hn0 = _rms_fwd_call(x[0], S['mix_norm'][0], "l0_w_in_norm", BF16)
    lru_w, att_w = N_DEV * lru_conv_w.shape[-1], B_HEADS * HEAD_DIM
    pieces = (("x", lru_w, False), ("y", lru_w, False), ("q", att_w, True), ("k", att_w, True), ("v", att_w, False))
    h, row = [x[0]], 0
    for piece, width, rotated in pieces:
        h.append(_mm(hn0, w_in[row:row + width], tb=True, rope=tabs if rotated else None, name="l0_w_in_" + piece,
                     tm=2048, tn=1024, tk=1024))
        row += width
    h = tuple(h)
    vjps = []
    for stage, small_names, stage_keys in STAGES[1:]:
        which = [keys.index(key) for key in stage_keys]
        _, lands = _copies_wait(gather, which, False, jax.tree.leaves(h)[-1], "gather_wait_" + stage)
        for e, land in zip(which, lands):
            Wb[keys[e]] = _full_from_gathered(land, _layer_shard_axis(keys[e][0]))
        carriers = {key: jnp.zeros(Wb[key].shape, F32) for key in stage_keys}
        h, vjp_fn = jax.vjp(_stage_fn(stage, Wb, tabs, mem[0]), {n: S[n] for n in small_names}, carriers, h)
        vjps.append(vjp_fn)
    loss_part, dh, dg_final = _final_call(h, S['final_norm'], loss_target[0], "final_loss")

    grads = {'final_norm': dg_final.reshape(final_norm.shape)}
    exchanges, send_keys, send_parts = [], [], []

    def start_exchange(stage, dh):
        leaves, tree = jax.tree.flatten(dh)
        started = _copies_start(list(send_parts), True, "grad_start_" + stage, thru=leaves)
        exchanges.append((stage, started, list(send_keys)))
        send_keys.clear()
        send_parts.clear()
        return jax.tree.unflatten(tree, started[4])

    for (stage, small_names, stage_keys), vjp_fn in zip(reversed(STAGES[1:]), reversed(vjps)):
        g_small, g_big, dh = vjp_fn(dh)
        for n in small_names:
            grads[n] = grads[n] + g_small[n] if n in grads else g_small[n]
        send_keys += list(stage_keys)
        send_parts += [_parts_from_full(g_big[key], _layer_shard_axis(key[0])) for key in stage_keys]
        if stage == "xa1":
            continue
        if stage == "mixer0":
            small_parts = [_parts_from_full(grads[n], SHARD_AXIS[n]) for n in SMALL]
            send_keys.append("small")
            send_parts.append(jnp.stack([_pack([p[j] for p in small_parts], F32) for j in range(N_DEV)]))
        dh = start_exchange(stage, dh)
    d_res, d_proj = dh[0], _join_pieces_call(dh[1:], [rotated for _, _, rotated in pieces], tabs, "l0_w_in_dproj")[0]
    send_keys.append(first_keys[0])
    send_parts.append(_parts_from_full(_mm(d_proj, hn0, ta=True, name="l0_w_in_dw", tm=1408, tn=1024, tk=2048),
                                       _layer_shard_axis(first_keys[0][0])))
    d_res, d_proj = start_exchange("proj0", (d_res, d_proj))
    dx, dg0 = _mm_rms_bwd(d_proj, w_in, x[0], S['mix_norm'][0], d_res, "l0_w_in_da", wt=True)
    grads['mix_norm'] = grads['mix_norm'] + jnp.concatenate([dg0, jnp.zeros_like(dg0)], axis=0)
    rep_names = REPLICATED + ["loss"]
    grads["loss"] = loss_part
    zero = jnp.zeros((1, 1), F32)
    for d in (w_loc, m_loc, v_loc):
        d["loss"] = zero
    rep_started = _copies_start([_pack([grads[n] for n in rep_names], F32)], False, "rep_grads_start")

    parts, out = {}, {}

    def end_exchange(stage, started, ex_keys, after):
        _, lands = _copies_wait(started, list(range(len(ex_keys))), True, after, "grad_wait_" + stage)
        parts.update(zip(ex_keys, lands))

    def adamw(p, names, call_name):
        if len(names) == 1:
            n = names[0]
            res = _adamw_call(p, _shard_view(n, w_loc[n]), _shard_view(n, m_loc[n]), _shard_view(n, v_loc[n]), call_name)
            for kind, t in zip(("grad", "delta", "new_m", "new_v"), res):
                out[kind, n] = _shard_view(n, t)
        else:
            res = _adamw_call(p, *[_pack([d[n] for n in names], F32) for d in (w_loc, m_loc, v_loc)], call_name)
            for kind, buf in zip(("grad", "delta", "new_m", "new_v"), res):
                for n, t in zip(names, _unpack(buf, [w_loc[n].shape for n in names])):
                    out[kind, n] = t

    for ex in exchanges[:-1]:
        end_exchange(*ex, dx)
    last_names = {key[0] for key in exchanges[-1][2]}
    for n in BIG:
        if n not in last_names:
            adamw([parts[n, l] for l in range(w_loc[n].shape[0])], [n], "adamw_" + n)
    adamw(parts["small"], SMALL, "adamw_small")
    end_exchange(*exchanges[-1], [out["new_v", n] for n in BIG if n not in last_names])
    for n in BIG:
        if n in last_names:
            adamw([parts[n, l] for l in range(w_loc[n].shape[0])], [n], "adamw_" + n)
    _, rep_land = _copies_wait(rep_started, [0], False, out["new_v", "ab_w_in"], "rep_grads_wait")
    adamw(rep_land[0], rep_names, "adamw_replicated")
    loss = out["grad", "loss"][0, 0]

    return (loss, dx[None], *[out[kind, n] for kind in ("grad", "delta", "new_m", "new_v") for n in WEIGHT_NAMES])
```

```python
import functools
import math

import jax
import jax.numpy as jnp
from jax import lax
from jax.experimental import pallas as pl
from jax.experimental.pallas import tpu as pltpu

F32 = jnp.float32
BF16 = jnp.bfloat16
SDS = jax.ShapeDtypeStruct
BS = pl.BlockSpec

N_DEV = 8
NORM_EPS = 1e-6
ROPE_THETA = 500000.0
HEAD_DIM = 64
ROT_DIM = 16
ATT_BLOCK = 128
LRU_C = 8.0
LRU_HEADS = 4
DILATIONS = (1, 4, 16)
B_HEADS = 8
C_HEADS = 16
C_KV_HEADS = 2
XA_HEADS = 4
XA_HEAD_DIM = 128
NEG = -1e30
ADAM_LR, ADAM_B1, ADAM_B2, ADAM_EPS, ADAM_WD, ADAM_STEP = 0.001, 0.9, 0.999, 1e-08, 0.01, 10
LANES = 128
VMEM_LIMIT = 48 * 1024 * 1024

WEIGHT_NAMES = ['mix_norm', 'ab_w_in', 'lru_conv_w', 'lru_conv_b', 'lru_wa', 'lru_ba', 'lru_wx', 'lru_bx',
                'lru_lambda', 'ab_w_out', 'c_w_qkv', 'c_b_qkv', 'c_sinks', 'c_w_out', 'c_b_out', 'xa_norm',
                'xa_mem_norm', 'xa_wq', 'xa_wkv', 'xa_wo', 'ffn_norm', 'ffn_w_gate_up', 'ffn_w_down', 'final_norm']
SHARD_AXIS = {'ab_w_in': 2, 'lru_conv_w': 2, 'lru_wa': 2, 'lru_ba': 2, 'lru_wx': 2, 'lru_bx': 2, 'ab_w_out': 1,
              'c_w_qkv': 2, 'c_b_qkv': 1, 'c_w_out': 1, 'c_b_out': 1, 'xa_wq': 1, 'xa_wkv': 1, 'xa_wo': 2,
              'ffn_w_gate_up': 2, 'ffn_w_down': 1}
BIG = ['ab_w_in', 'lru_wa', 'lru_wx', 'ab_w_out', 'c_w_qkv', 'c_w_out', 'xa_wq', 'xa_wkv', 'xa_wo',
       'ffn_w_gate_up', 'ffn_w_down']
SMALL = ['lru_conv_w', 'lru_ba', 'lru_bx', 'c_b_qkv', 'c_b_out']
REPLICATED = [n for n in WEIGHT_NAMES if n not in SHARD_AXIS]
COLUMN_CUT = ('ab_w_in', 'c_w_qkv', 'ffn_w_gate_up')
PACK_QUANTUM = 2048


def _shard_view(name, t):
    return jnp.swapaxes(t, -1, -2) if name in COLUMN_CUT else t


def _layer_shard_axis(name):
    return 0 if name in COLUMN_CUT else SHARD_AXIS[name] - 1


def _pcall(body, **kw):
    return pl.pallas_call(body, **kw)


def _cparams(sem=None):
    return pltpu.CompilerParams(dimension_semantics=sem, vmem_limit_bytes=VMEM_LIMIT)


def _tile(n, target, mult=LANES):
    if n <= target:
        return n
    t = (target // mult) * mult
    while t >= mult:
        if n % t == 0:
            return t
        t -= mult
    return n


def _sigmoid(x):
    return 1.0 / (1.0 + jnp.exp(-x))


def _expm1(x):
    small = x * (1.0 + x * (0.5 + x * (1.0 / 6.0 + x * (1.0 / 24.0))))
    return jnp.where(jnp.abs(x) < 0.03, small, jnp.exp(x) - 1.0)


_GELU_C = math.sqrt(2.0 / math.pi)


def _gelu_parts(y):
    y2 = y * y
    th = jnp.tanh(_GELU_C * (y + 0.044715 * y * y2))
    g = 0.5 * y * (1.0 + th)
    dg = 0.5 * (1.0 + th) + 0.5 * y * (1.0 - th * th) * _GELU_C * (1.0 + 3.0 * 0.044715 * y2)
    return g, dg


def _rotate(xv, tab_refs, inverse=False):
    W = xv.shape[1]
    half = ROT_DIM // 2
    c, sa, sb = (jnp.tile(t[...], (1, W // LANES)) for t in tab_refs)
    if not inverse:
        return xv * c + pltpu.roll(xv, half, axis=1) * sa + pltpu.roll(xv, W - half, axis=1) * sb
    return xv * c + pltpu.roll(xv * sa, W - half, axis=1) + pltpu.roll(xv * sb, half, axis=1)


def _mm(a, b, *, ta=False, tb=False, bias=None, add=None, rope=None, name, tm=512, tn=512, tk=2048):
    M, K = (a.shape[1], a.shape[0]) if ta else a.shape
    N = b.shape[0] if tb else b.shape[1]
    tm, tn, tk = _tile(M, tm), _tile(N, tn), _tile(K, tk)
    nk = K // tk
    dn = (((0 if ta else 1,), (1 if tb else 0,)), ((), ()))

    def body(*refs):
        a_ref, b_ref = refs[0], refs[1]
        pos = 2
        bias_ref = add_ref = None
        if bias is not None:
            bias_ref = refs[pos]
            pos += 1
        if add is not None:
            add_ref = refs[pos]
            pos += 1
        tab_refs = refs[pos:pos + 3] if rope is not None else None
        pos += 3 if rope is not None else 0
        o_ref = refs[pos]
        part = lax.dot_general(a_ref[...].astype(BF16), b_ref[...].astype(BF16), dn, preferred_element_type=F32)

        def finish(r):
            if bias_ref is not None:
                r = r + bias_ref[...]
            if add_ref is not None:
                r = r + add_ref[...]
            o_ref[...] = r if tab_refs is None else _rotate(r, tab_refs)

        if nk == 1:
            finish(part)
            return
        acc_ref = refs[pos + 1]
        k = pl.program_id(2)

        @pl.when(k == 0)
        def _():
            acc_ref[...] = part

        @pl.when((k > 0) & (k < nk - 1))
        def _():
            acc_ref[...] += part

        @pl.when(k == nk - 1)
        def _():
            finish(acc_ref[...] + part)

    in_specs = [BS((tk, tm), lambda i, j, k: (k, i)) if ta else BS((tm, tk), lambda i, j, k: (i, k)),
                BS((tn, tk), lambda i, j, k: (j, k)) if tb else BS((tk, tn), lambda i, j, k: (k, j))]
    args = [a, b]
    if bias is not None:
        in_specs.append(BS((1, tn), lambda i, j, k: (0, j)))
        args.append(bias.reshape(1, N))
    if add is not None:
        in_specs.append(BS((tm, tn), lambda i, j, k: (i, j)))
        args.append(add)
    if rope is not None:
        in_specs += [BS((tm, LANES), lambda i, j, k: (i, 0))] * 3
        args += list(rope)
    return _pcall(body, name=name, grid=(M // tm, N // tn, nk), in_specs=in_specs,
                  out_specs=BS((tm, tn), lambda i, j, k: (i, j)), out_shape=SDS((M, N), F32),
                  scratch_shapes=[pltpu.VMEM((tm, tn), F32)] if nk > 1 else [],
                  compiler_params=_cparams(("parallel", "parallel", "arbitrary")))(*args)


def _mm_rms_bwd(dy, wb, x, g, add, name, wt=False):
    T, Kc = dy.shape
    Dm = wb.shape[1] if wt else wb.shape[0]
    tm, tk = _tile(T, 1024), _tile(Kc, 1408)
    nk = Kc // tk
    dn = (((1,), (0 if wt else 1,)), ((), ()))

    def body(*refs):
        dy_ref, w_ref, x_ref, g_ref = refs[:4]
        dx_ref, dg_ref, acc_ref = refs[-3:]
        i, k = pl.program_id(0), pl.program_id(1)
        part = lax.dot_general(dy_ref[...].astype(BF16), w_ref[...], dn, preferred_element_type=F32)

        @pl.when(k == 0)
        def _():
            acc_ref[...] = part

        @pl.when((k > 0) & (k < nk - 1))
        def _():
            acc_ref[...] += part

        @pl.when((i == 0) & (k == 0))
        def _():
            dg_ref[...] = jnp.zeros_like(dg_ref)

        @pl.when(k == nk - 1)
        def _():
            d = part if nk == 1 else acc_ref[...] + part
            xv = x_ref[...]
            r = lax.rsqrt(jnp.mean(xv * xv, axis=-1, keepdims=True) + NORM_EPS)
            xh = xv * r
            dyg = d * g_ref[...]
            dx = r * (dyg - xh * jnp.mean(dyg * xh, axis=-1, keepdims=True))
            dx_ref[...] = dx if add is None else dx + refs[4][...]
            dg_ref[...] += jnp.sum(d * xh, axis=0, keepdims=True)

    row = BS((tm, Dm), lambda i, k: (i, 0))
    vec = BS((1, Dm), lambda i, k: (0, 0))
    extra = [] if add is None else [add]
    return _pcall(body, name=name, grid=(T // tm, nk),
                  in_specs=[BS((tm, tk), lambda i, k: (i, k)),
                            BS((tk, Dm), lambda i, k: (k, 0)) if wt else BS((Dm, tk), lambda i, k: (0, k)), row, vec]
                  + [row] * len(extra),
                  out_specs=[row, vec], out_shape=[SDS((T, Dm), F32), SDS((1, Dm), F32)],
                  scratch_shapes=[pltpu.VMEM((tm, Dm), F32)],
                  compiler_params=_cparams(("arbitrary", "arbitrary")))(dy, wb, x, g.reshape(1, Dm), *extra)


def _colsum(x, name):
    T, N = x.shape
    tt = _tile(T, 1024, 8)

    def body(x_ref, o_ref):
        @pl.when(pl.program_id(0) == 0)
        def _():
            o_ref[...] = jnp.zeros_like(o_ref)

        o_ref[...] += jnp.sum(x_ref[...], axis=0, keepdims=True)

    return _pcall(body, name=name, grid=(T // tt,), in_specs=[BS((tt, N), lambda i: (i, 0))],
                  out_specs=BS((1, N), lambda i: (0, 0)), out_shape=SDS((1, N), F32),
                  compiler_params=_cparams(("arbitrary",)))(x)


@functools.partial(jax.custom_vjp, nondiff_argnums=(5,))
def linear(a, wb, wc, bias, add, name):
    return _linear_fwd(a, wb, wc, bias, add, name)[0]


def _linear_fwd(a, wb, wc, bias, add, name):
    out = _mm(a, wb, bias=bias, add=add, name=name, tm=1024, tn=1024, tk=1024)
    return out, (a, wb, bias is not None, add is not None)


def _linear_bwd(name, res, g):
    a, wb, has_bias, has_add = res
    da = _mm(g, wb, tb=True, name=name + "_da", tm=1024, tn=1024, tk=1024)
    dw = _mm(a, g, ta=True, name=name + "_dw", tm=1024, tn=1024, tk=1024)
    dbias = _colsum(g, name + "_db").reshape(-1) if has_bias else None
    return da, jnp.zeros_like(wb), dw, dbias, (g if has_add else None)


linear.defvjp(_linear_fwd, _linear_bwd)


def _rms_fwd_call(x, g, name, out_dtype=F32):
    T, Dm = x.shape
    tt = _tile(T, 2048, 16)

    def body(x_ref, g_ref, o_ref):
        xv = x_ref[...]
        r = lax.rsqrt(jnp.mean(xv * xv, axis=-1, keepdims=True) + NORM_EPS)
        o_ref[...] = ((xv * r) * g_ref[...]).astype(out_dtype)

    return _pcall(body, name=name, grid=(T // tt,),
                  in_specs=[BS((tt, Dm), lambda i: (i, 0)), BS((1, Dm), lambda i: (0, 0))],
                  out_specs=BS((tt, Dm), lambda i: (i, 0)), out_shape=SDS((T, Dm), out_dtype),
                  compiler_params=_cparams(("parallel",)))(x, g.reshape(1, Dm))


@functools.partial(jax.custom_vjp, nondiff_argnums=(5, 6))
def norm_linear(x, g, wb, wc, bias, wt, name):
    return _norm_linear_fwd(x, g, wb, wc, bias, wt, name)[0]


def _norm_linear_fwd(x, g, wb, wc, bias, wt, name):
    hn = _rms_fwd_call(x, g, name + "_norm", BF16)
    return (_mm(hn, wb, tb=wt, bias=bias, name=name, tm=1024, tn=1408, tk=1024), x), (x, g, hn, wb, bias is not None)


def _norm_linear_bwd(wt, name, res, cts):
    x, g, hn, wb, has_bias = res
    dy, dres = cts
    if wt:
        dw = _mm(dy, hn, ta=True, name=name + "_dw", tm=1408, tn=1024, tk=2048)
    else:
        dw = _mm(hn, dy, ta=True, name=name + "_dw", tm=1024, tn=1408, tk=1024)
    dx, dg = _mm_rms_bwd(dy, wb, x, g, dres, name + "_da", wt)
    dbias = _colsum(dy, name + "_db").reshape(-1) if has_bias else None
    return dx, dg.reshape(g.shape), jnp.zeros_like(wb), dw, dbias


norm_linear.defvjp(_norm_linear_fwd, _norm_linear_bwd)


def _rope_tables(T):
    half = ROT_DIM // 2
    inv = ROPE_THETA ** (-jnp.arange(0, ROT_DIM, 2, dtype=F32) / ROT_DIM)
    lane = jnp.arange(LANES) % HEAD_DIM
    freq = jnp.where(lane < ROT_DIM, inv[lane % half], 0.0)
    ang = jnp.arange(T, dtype=F32)[:, None] * freq[None, :]
    c, s = jnp.cos(ang), jnp.sin(ang)
    sa = jnp.where((lane >= half) & (lane < ROT_DIM), s, 0.0)
    sb = jnp.where(lane < half, -s, 0.0)
    return c, sa, sb


def _join_pieces_call(pieces, rotated, tabs, name):
    T = pieces[0].shape[0]
    widths = [p.shape[1] for p in pieces]
    W = sum(widths)
    tt = _tile(T, 512, 16)

    def body(*refs):
        tab_refs, o_ref, sum_ref = refs[len(pieces):len(pieces) + 3], refs[-2], refs[-1]

        @pl.when(pl.program_id(0) == 0)
        def _():
            sum_ref[...] = jnp.zeros_like(sum_ref)

        off = 0
        for p_ref, w, r in zip(refs, widths, rotated):
            piece = _rotate(p_ref[...], tab_refs, inverse=True) if r else p_ref[...]
            o_ref[:, off:off + w] = piece.astype(BF16)
            sum_ref[:, off:off + w] += jnp.sum(piece, axis=0, keepdims=True)
            off += w

    return _pcall(body, name=name, grid=(T // tt,),
                  in_specs=[BS((tt, w), lambda i: (i, 0)) for w in widths] + [BS((tt, LANES), lambda i: (i, 0))] * 3,
                  out_specs=[BS((tt, W), lambda i: (i, 0)), BS((1, W), lambda i: (0, 0))],
                  out_shape=[SDS((T, W), BF16), SDS((1, W), F32)],
                  compiler_params=_cparams(("arbitrary",)))(*pieces, *tabs)


@functools.partial(jax.custom_vjp, nondiff_argnums=(6, 7))
def norm_linear_pieces(x, g, wb, wc, bias, tabs, pieces, name):
    return _norm_linear_pieces_fwd(x, g, wb, wc, bias, tabs, pieces, name)[0]


def _norm_linear_pieces_fwd(x, g, wb, wc, bias, tabs, pieces, name):
    hn = _rms_fwd_call(x, g, name + "_norm", BF16)
    outs, row = [], 0
    for i, (width, rotated) in enumerate(pieces):
        outs.append(_mm(hn, wb[row:row + width], tb=True, bias=None if bias is None else bias[row:row + width],
                        rope=tabs if rotated else None, name=f"{name}_{i}", tm=2048, tn=1024, tk=1024))
        row += width
    return (*outs, x), (x, g, hn, wb, tabs, bias is not None)


def _norm_linear_pieces_bwd(pieces, name, res, cts):
    x, g, hn, wb, tabs, has_bias = res
    dy, dy_sum = _join_pieces_call(cts[:-1], [rotated for _, rotated in pieces], tabs, name + "_join")
    dw = _mm(dy, hn, ta=True, name=name + "_dw", tm=1408, tn=1024, tk=2048)
    dx, dg = _mm_rms_bwd(dy, wb, x, g, cts[-1], name + "_da", wt=True)
    dbias = dy_sum.reshape(-1) if has_bias else None
    return dx, dg.reshape(g.shape), jnp.zeros_like(wb), dw, dbias, jax.tree.map(jnp.zeros_like, tabs)


norm_linear_pieces.defvjp(_norm_linear_pieces_fwd, _norm_linear_pieces_bwd)


def _conv_fwd_call(x, w, b, name):
    T, C = x.shape
    tt = _tile(T, 2048, 8)
    per = tt // 8

    def body(x_ref, halo_ref, w_ref, b_ref, o_ref):
        i = pl.program_id(0)
        halo = jnp.where(i > 0, halo_ref[...], 0.0)
        e = jnp.concatenate([halo, x_ref[...]], axis=0)
        acc = b_ref[...]
        for k in (3, 2, 1):
            acc = acc + pltpu.roll(e, k, axis=0)[8:, :] * w_ref[3 - k:4 - k, :]
        o_ref[...] = acc + x_ref[...] * w_ref[3:4, :]

    return _pcall(body, name=name, grid=(T // tt,),
                  in_specs=[BS((tt, C), lambda i: (i, 0)), BS((8, C), lambda i: (jnp.maximum(i * per - 1, 0), 0)),
                            BS((4, C), lambda i: (0, 0)), BS((1, C), lambda i: (0, 0))],
                  out_specs=BS((tt, C), lambda i: (i, 0)), out_shape=SDS((T, C), F32),
                  compiler_params=_cparams(("parallel",)))(x, x, w, b.reshape(1, C))


def _conv_bwd_call(x, w, dy, name):
    T, C = x.shape
    tt = _tile(T, 1024, 8)
    per = tt // 8
    nt = T // tt

    def body(x_ref, halo_ref, w_ref, dy_ref, nxt_ref, dx_ref, dwb_ref):
        i = pl.program_id(0)
        halo = jnp.where(i > 0, halo_ref[...], 0.0)
        e = jnp.concatenate([halo, x_ref[...]], axis=0)
        dy = dy_ref[...]
        nxt = jnp.where(i < nt - 1, nxt_ref[...], 0.0)
        f = jnp.concatenate([dy, nxt], axis=0)
        dx = dy * w_ref[3:4, :]
        rows = [None] * 4
        rows[3] = jnp.sum(dy * x_ref[...], axis=0, keepdims=True)
        for k in (1, 2, 3):
            dx = dx + pltpu.roll(f, tt + 8 - k, axis=0)[:tt, :] * w_ref[3 - k:4 - k, :]
            rows[3 - k] = jnp.sum(dy * pltpu.roll(e, k, axis=0)[8:, :], axis=0, keepdims=True)
        dx_ref[...] = dx
        upd = jnp.concatenate(rows + [jnp.sum(dy, axis=0, keepdims=True), jnp.zeros((3, C), F32)], axis=0)

        @pl.when(i == 0)
        def _():
            dwb_ref[...] = jnp.zeros_like(dwb_ref)

        dwb_ref[...] += upd

    row = BS((tt, C), lambda i: (i, 0))
    return _pcall(body, name=name, grid=(nt,),
                  in_specs=[row, BS((8, C), lambda i: (jnp.maximum(i * per - 1, 0), 0)), BS((4, C), lambda i: (0, 0)),
                            row, BS((8, C), lambda i: (jnp.minimum((i + 1) * per, T // 8 - 1), 0))],
                  out_specs=[row, BS((8, C), lambda i: (0, 0))],
                  out_shape=[SDS((T, C), F32), SDS((8, C), F32)],
                  compiler_params=_cparams(("arbitrary",)))(x, x, w, dy, dy)


@functools.partial(jax.custom_vjp, nondiff_argnums=(3,))
def conv4(x, w, b, name):
    return _conv_fwd_call(x, w, b, name)


def _conv4_fwd(x, w, b, name):
    return _conv_fwd_call(x, w, b, name), (x, w)


def _conv4_bwd(name, res, dy):
    x, w = res
    dx, dwb = _conv_bwd_call(x, w, dy, name + "_bwd")
    return dx, dwb[0:4], dwb[4]


conv4.defvjp(_conv4_fwd, _conv4_bwd)


def _gates_fwd_call(xc, wa, ba, wx, bx, name):
    T, C = xc.shape
    hd = C // LRU_HEADS
    tt = _tile(T, 2048, 8)

    def body(x_ref, wa_ref, ba_ref, wx_ref, bx_ref, ga_ref, gx_ref):
        xb = x_ref[...].astype(BF16)
        ga_ref[...] = jnp.dot(xb, wa_ref[0].astype(BF16), preferred_element_type=F32) + ba_ref[...]
        gx_ref[...] = jnp.dot(xb, wx_ref[0].astype(BF16), preferred_element_type=F32) + bx_ref[...]

    blk = BS((tt, hd), lambda i, h: (i, h))
    wsp = BS((1, hd, hd), lambda i, h: (h, 0, 0))
    bsp = BS((1, hd), lambda i, h: (0, h))
    return _pcall(body, name=name, grid=(T // tt, LRU_HEADS), in_specs=[blk, wsp, bsp, wsp, bsp],
                  out_specs=[blk, blk], out_shape=[SDS((T, C), F32)] * 2,
                  compiler_params=_cparams(("parallel", "parallel")))(xc, wa, ba.reshape(1, C), wx, bx.reshape(1, C))


def _gates_bwd_x_call(dga, dgx, wa, wx, name):
    T, C = dga.shape
    hd = C // LRU_HEADS
    tt = _tile(T, 2048, 8)
    dn = (((1,), (1,)), ((), ()))

    def body(da_ref, dx_ref, wa_ref, wx_ref, o_ref):
        o_ref[...] = (lax.dot_general(da_ref[...].astype(BF16), wa_ref[0].astype(BF16), dn, preferred_element_type=F32)
                      + lax.dot_general(dx_ref[...].astype(BF16), wx_ref[0].astype(BF16), dn, preferred_element_type=F32))

    blk = BS((tt, hd), lambda i, h: (i, h))
    wsp = BS((1, hd, hd), lambda i, h: (h, 0, 0))
    return _pcall(body, name=name, grid=(T // tt, LRU_HEADS), in_specs=[blk, blk, wsp, wsp], out_specs=blk,
                  out_shape=SDS((T, C), F32), compiler_params=_cparams(("parallel", "parallel")))(dga, dgx, wa, wx)


def _gates_bwd_w_call(xc, dga, dgx, name):
    T, C = xc.shape
    hd = C // LRU_HEADS
    tt = _tile(T, 2048, 8)
    dn = (((0,), (0,)), ((), ()))

    def body(x_ref, da_ref, dx_ref, dwa_ref, dwx_ref, dba_ref, dbx_ref):
        @pl.when(pl.program_id(1) == 0)
        def _():
            dwa_ref[...] = jnp.zeros_like(dwa_ref)
            dwx_ref[...] = jnp.zeros_like(dwx_ref)
            dba_ref[...] = jnp.zeros_like(dba_ref)
            dbx_ref[...] = jnp.zeros_like(dbx_ref)

        xb = x_ref[...].astype(BF16)
        da, dx = da_ref[...], dx_ref[...]
        dwa_ref[0] += lax.dot_general(xb, da.astype(BF16), dn, preferred_element_type=F32)
        dwx_ref[0] += lax.dot_general(xb, dx.astype(BF16), dn, preferred_element_type=F32)
        dba_ref[...] += jnp.sum(da, axis=0, keepdims=True)
        dbx_ref[...] += jnp.sum(dx, axis=0, keepdims=True)

    blk = BS((tt, hd), lambda h, i: (i, h))
    wsp = BS((1, hd, hd), lambda h, i: (h, 0, 0))
    bsp = BS((1, hd), lambda h, i: (0, h))
    return _pcall(body, name=name, grid=(LRU_HEADS, T // tt), in_specs=[blk, blk, blk],
                  out_specs=[wsp, wsp, bsp, bsp],
                  out_shape=[SDS((LRU_HEADS, hd, hd), F32)] * 2 + [SDS((1, C), F32)] * 2,
                  compiler_params=_cparams(("parallel", "arbitrary")))(xc, dga, dgx)


@functools.partial(jax.custom_vjp, nondiff_argnums=(7,))
def lru_gates(xc, wa, wa_c, ba, wx, wx_c, bx, name):
    return tuple(_gates_fwd_call(xc, wa, ba, wx, bx, name))


def _lru_gates_fwd(xc, wa, wa_c, ba, wx, wx_c, bx, name):
    return tuple(_gates_fwd_call(xc, wa, ba, wx, bx, name)), (xc, wa, wx, ba.shape)


def _lru_gates_bwd(name, res, g):
    xc, wa, wx, bshape = res
    dga, dgx = g
    dxc = _gates_bwd_x_call(dga, dgx, wa, wx, name + "_dx")
    dwa, dwx, dba, dbx = _gates_bwd_w_call(xc, dga, dgx, name + "_dw")
    return dxc, jnp.zeros_like(wa), dwa, dba.reshape(bshape), jnp.zeros_like(wx), dwx, dbx.reshape(bshape)


lru_gates.defvjp(_lru_gates_fwd, _lru_gates_bwd)


def _lru_coeffs(xc, ga, gx, lam):
    r = _sigmoid(ga)
    ig = _sigmoid(gx)
    z = -lam
    sp = jnp.maximum(z, 0.0) + jnp.log(1.0 + jnp.exp(-jnp.abs(z)))
    la = -LRU_C * r * sp
    a = jnp.exp(la)
    s = jnp.sqrt(-_expm1(2.0 * la))
    return r, ig, sp, a, s


LRU_TT = 256


def _scan_fwd_call(xc, ga, gx, y, lam, name):
    T, C = xc.shape
    tt = _tile(T, 2 * LRU_TT, 8)

    def body(xc_ref, ga_ref, gx_ref, y_ref, lam_ref, h_ref, rec_ref, a_buf, carry):
        @pl.when(pl.program_id(0) == 0)
        def _():
            carry[...] = jnp.zeros_like(carry)

        xcv = xc_ref[...]
        _, ig, _, a, s = _lru_coeffs(xcv, ga_ref[...], gx_ref[...], lam_ref[...])
        a_buf[...] = a
        h_ref[...] = s * (ig * xcv)

        def step(t, h):
            hn = a_buf[pl.ds(t, 1), :] * h + h_ref[pl.ds(t, 1), :]
            h_ref[pl.ds(t, 1), :] = hn
            return hn

        carry[0:1, :] = lax.fori_loop(0, tt, step, carry[0:1, :], unroll=8)
        g, _ = _gelu_parts(y_ref[...])
        rec_ref[...] = h_ref[...] * g

    row = BS((tt, C), lambda i: (i, 0))
    vec = BS((1, C), lambda i: (0, 0))
    return _pcall(body, name=name, grid=(T // tt,), in_specs=[row, row, row, row, vec], out_specs=[row, row],
                  out_shape=[SDS((T, C), F32)] * 2,
                  scratch_shapes=[pltpu.VMEM((tt, C), F32), pltpu.VMEM((8, C), F32)],
                  compiler_params=_cparams(("arbitrary",)))(xc, ga, gx, y, lam.reshape(1, C))


def _scan_bwd_call(xc, ga, gx, y, lam, h, drec, name):
    T, C = xc.shape
    tt = _tile(T, LRU_TT, 8)
    nt = T // tt
    per = tt // 8

    def body(xc_ref, ga_ref, gx_ref, y_ref, lam_ref, h_ref, halo_ref, dr_ref,
             dga_ref, dgx_ref, dxc_ref, dy_ref, dlam_ref, a_buf, g_buf, carry):
        i = pl.program_id(0)

        @pl.when(i == 0)
        def _():
            carry[...] = jnp.zeros_like(carry)
            dlam_ref[...] = jnp.zeros_like(dlam_ref)

        xcv, lam = xc_ref[...], lam_ref[...]
        r, ig, sp, a, s = _lru_coeffs(xcv, ga_ref[...], gx_ref[...], lam)
        gel, dgel = _gelu_parts(y_ref[...])
        drec = dr_ref[...]
        hv = h_ref[...]
        dy_ref[...] = drec * hv * dgel
        a_buf[...] = a
        g_buf[...] = drec * gel

        def step(j, q):
            t = tt - 1 - j
            g = g_buf[pl.ds(t, 1), :] + q
            g_buf[pl.ds(t, 1), :] = g
            return a_buf[pl.ds(t, 1), :] * g

        carry[0:1, :] = lax.fori_loop(0, tt, step, carry[0:1, :], unroll=8)
        g = g_buf[...]
        halo = jnp.where(i < nt - 1, halo_ref[...], 0.0)
        hprev = pltpu.roll(jnp.concatenate([halo, hv], axis=0), 1, axis=0)[8:, :]
        da = g * hprev
        dig = g * s * xcv
        ds = g * ig * xcv
        dla = da * a - ds * (a * a) / s
        dga_ref[...] = dla * (-LRU_C * sp) * r * (1.0 - r)
        dgx_ref[...] = dig * ig * (1.0 - ig)
        dxc_ref[...] = g * s * ig
        dlam_ref[...] += jnp.sum(dla * r, axis=0, keepdims=True) * (LRU_C * _sigmoid(-lam))

    row = BS((tt, C), lambda i: (nt - 1 - i, 0))
    vec = BS((1, C), lambda i: (0, 0))
    halo = BS((8, C), lambda i: (jnp.maximum((nt - 1 - i) * per - 1, 0), 0))
    return _pcall(body, name=name, grid=(nt,), in_specs=[row, row, row, row, vec, row, halo, row],
                  out_specs=[row, row, row, row, vec], out_shape=[SDS((T, C), F32)] * 4 + [SDS((1, C), F32)],
                  scratch_shapes=[pltpu.VMEM((tt, C), F32), pltpu.VMEM((tt, C), F32), pltpu.VMEM((8, C), F32)],
                  compiler_params=_cparams(("arbitrary",)))(xc, ga, gx, y, lam.reshape(1, C), h, h, drec)


@functools.partial(jax.custom_vjp, nondiff_argnums=(5,))
def lru_scan(xc, ga, gx, y, lam, name):
    return _scan_fwd_call(xc, ga, gx, y, lam, name)[1]


def _lru_scan_fwd(xc, ga, gx, y, lam, name):
    h, rec = _scan_fwd_call(xc, ga, gx, y, lam, name)
    return rec, (xc, ga, gx, y, lam, h)


def _lru_scan_bwd(name, res, drec):
    xc, ga, gx, y, lam, h = res
    dga, dgx, dxc, dy, dlam = _scan_bwd_call(xc, ga, gx, y, lam, h, drec, name + "_bwd")
    return dxc, dga, dgx, dy, dlam.reshape(lam.shape)


lru_scan.defvjp(_lru_scan_fwd, _lru_scan_bwd)


def _att_batch(d, shared=False):
    if shared:
        return 8, 1
    return (4, 1) if d == 1 else (1, min(d, 8))


def _att_masks(n, max_dist):
    qi = lax.broadcasted_iota(jnp.int32, (1, 2 * ATT_BLOCK, 2 * ATT_BLOCK), 1) & (ATT_BLOCK - 1)
    kj = lax.broadcasted_iota(jnp.int32, (1, 2 * ATT_BLOCK, 2 * ATT_BLOCK), 2)
    prev = (kj < ATT_BLOCK) & (kj >= qi + (ATT_BLOCK - max_dist)) & (n > 0)
    cur = (kj >= ATT_BLOCK) & (kj - ATT_BLOCK <= qi)
    return prev | cur


def _lo_lanes(rows):
    return lax.broadcasted_iota(jnp.int32, (rows, LANES), 1) < HEAD_DIM


def _lane_half(rows):
    return lax.broadcasted_iota(jnp.int32, (rows, LANES), 1) // HEAD_DIM


def _stack_heads(x2):
    lo = _lo_lanes(ATT_BLOCK)
    zero = jnp.zeros_like(x2)
    return jnp.concatenate([jnp.where(lo, x2, zero), jnp.where(lo, zero, x2)], axis=0)


def _unstack_heads(y):
    return jnp.where(_lo_lanes(ATT_BLOCK), y[:ATT_BLOCK], y[ATT_BLOCK:])


def _per_head_col(x2):
    return jnp.concatenate([x2[:, 0:1], x2[:, HEAD_DIM:HEAD_DIM + 1]], axis=0)


def _head_sums(x2):
    lo = _lo_lanes(ATT_BLOCK)
    return jnp.concatenate([jnp.sum(jnp.where(lo, x2, 0.0), axis=1, keepdims=True),
                            jnp.sum(jnp.where(lo, 0.0, x2), axis=1, keepdims=True)], axis=0)


def _att_specs(d, Wq, Wk, nb, clamp):
    shared = Wk != Wq
    cgw, sb = _att_batch(d, shared)
    cur = (lambda n: jnp.minimum(n, nb - 1)) if clamp else (lambda n: n)
    rows, qw, kw = ATT_BLOCK * d, cgw * LANES, (LANES if shared else cgw * LANES)
    kcol = (lambda g: 0) if shared else (lambda g: g)
    qsp = BS((rows, qw), lambda g, n: (cur(n), g))
    csp = BS((rows, kw), lambda g, n: (cur(n), kcol(g)))
    psp = BS((rows, kw), lambda g, n: (jnp.maximum(cur(n) - 1, 0), kcol(g)))
    return cgw, sb, shared, qsp, csp, psp, qw, kw


def _att_streams(d, sb, work):
    if d == 1:
        work([slice(None)])
        return

    def one(j, carry):
        work([pl.ds(j * sb + i, ATT_BLOCK, stride=d) for i in range(sb)])
        return carry

    lax.fori_loop(0, d // sb, one, 0)


def _att_problem_loads(rows, cgw, shared, g, q_ref, kc_ref, kp_ref, vc_ref, vp_ref, sk_ref):
    half = _lane_half(ATT_BLOCK)

    def kv(ref, r, p):
        x = ref[r, :]
        if not shared:
            return x[:, p * LANES:(p + 1) * LANES]
        return jnp.where(half == p // 4, x, pltpu.roll(x, HEAD_DIM, axis=1))

    qs, kb, vb, sk = [], [], [], []
    for r in rows:
        qrow = q_ref[r, :]
        for p in range(cgw):
            cols = slice(p * LANES, (p + 1) * LANES)
            qs.append(_stack_heads(qrow[:, cols].astype(BF16)))
            kb.append(jnp.concatenate([kv(kp_ref, r, p), kv(kc_ref, r, p)], axis=0).astype(BF16))
            vb.append(jnp.concatenate([kv(vp_ref, r, p), kv(vc_ref, r, p)], axis=0).astype(BF16))
            sk.append(_per_head_col(jnp.broadcast_to(sk_ref[:, cols], (ATT_BLOCK, LANES))))
    return jnp.stack(qs), jnp.stack(kb), jnp.stack(vb), jnp.stack(sk)


_BDOT_NT = (((2,), (2,)), ((0,), (0,)))
_BDOT_NN = (((2,), (1,)), ((0,), (0,)))
_BDOT_TN = (((1,), (1,)), ((0,), (0,)))


def _att_fwd_call(q, k, v, sinks, d, max_dist, name):
    T, Wq = q.shape
    Wk = k.shape[1]
    nb = T // (d * ATT_BLOCK)
    cgw, sb, shared, qsp, csp, psp, qw, kw = _att_specs(d, Wq, Wk, nb, False)
    G = Wq // qw
    assert not shared or (d == 1 and Wk == LANES and G == 1 and cgw == 8), "a shared kv pair serves 2 x 8 query heads"

    def body(q_ref, kc_ref, kp_ref, vc_ref, vp_ref, sk_ref, o_ref, lse_ref):
        g, n = pl.program_id(0), pl.program_id(1)

        def work(rows):
            qs, kband, vband, sk = _att_problem_loads(rows, cgw, shared, g, q_ref, kc_ref, kp_ref, vc_ref, vp_ref, sk_ref)
            s = lax.dot_general(qs, kband, _BDOT_NT, preferred_element_type=F32) * (HEAD_DIM ** -0.5)
            s = jnp.where(_att_masks(n, max_dist), s, NEG)
            m = jnp.maximum(jnp.max(s, axis=-1, keepdims=True), sk)
            e = jnp.exp(s - m)
            den = jnp.sum(e, axis=-1, keepdims=True) + jnp.exp(sk - m)
            o = lax.dot_general((e * (1.0 / den)).astype(BF16), vband, _BDOT_NN, preferred_element_type=F32)
            lse = jnp.broadcast_to(m + jnp.log(den), o.shape)
            for i, r in enumerate(rows):
                o_ref[r, :] = jnp.concatenate([_unstack_heads(o[i * cgw + p]) for p in range(cgw)], axis=1)
                lse_ref[r, :] = jnp.concatenate([_unstack_heads(lse[i * cgw + p]) for p in range(cgw)], axis=1)

        _att_streams(d, sb, work)

    sksp = BS((1, qw), lambda g, n: (0, g))
    return _pcall(body, name=name, grid=(G, nb), in_specs=[qsp, csp, psp, csp, psp, sksp], out_specs=[qsp, qsp],
                  out_shape=[SDS((T, Wq), F32)] * 2,
                  compiler_params=_cparams(("parallel", "parallel")))(q, k, k, v, v, sinks)


def _att_bwd_call(q, k, v, sinks, o, lse, do, dlse, d, max_dist, name):
    T, Wq = q.shape
    Wk = k.shape[1]
    nb = T // (d * ATT_BLOCK)
    cgw, sb, shared, qsp, csp, psp, qw, kw = _att_specs(d, Wq, Wk, nb, True)
    G = Wq // qw
    scale = HEAD_DIM ** -0.5

    def body(*refs):
        q_ref, kc_ref, kp_ref, vc_ref, vp_ref, sk_ref, o_ref, lse_ref, do_ref = refs[:9]
        dlse_ref = refs[9] if dlse is not None else None
        dq_ref, dk_ref, dv_ref, dsk_ref, ck, cv = refs[-6:]
        g, n = pl.program_id(0), pl.program_id(1)

        @pl.when(n == 0)
        def _():
            ck[...] = jnp.zeros_like(ck)
            cv[...] = jnp.zeros_like(cv)
            dsk_ref[...] = jnp.zeros_like(dsk_ref)

        def work(rows):
            qs, kband, vband, sk = _att_problem_loads(rows, cgw, shared, g, q_ref, kc_ref, kp_ref, vc_ref, vp_ref, sk_ref)
            dos, lse_c, corr = [], [], []
            for r in rows:
                do_r, o_r, lse_r = do_ref[r, :], o_ref[r, :], lse_ref[r, :]
                dlse_r = dlse_ref[r, :] if dlse_ref is not None else None
                for p in range(cgw):
                    cols = slice(p * LANES, (p + 1) * LANES)
                    dos.append(_stack_heads(do_r[:, cols].astype(BF16)))
                    lse_c.append(_per_head_col(lse_r[:, cols]))
                    delta = _head_sums(do_r[:, cols] * o_r[:, cols])
                    corr.append(-delta if dlse_r is None else _head_sums(dlse_r[:, cols]) - delta)
            dos, lse_c, corr = jnp.stack(dos), jnp.stack(lse_c), jnp.stack(corr)
            s = lax.dot_general(qs, kband, _BDOT_NT, preferred_element_type=F32) * scale
            pr = jnp.exp(jnp.where(_att_masks(n, max_dist), s, NEG) - lse_c)
            dp = lax.dot_general(dos, vband, _BDOT_NT, preferred_element_type=F32)
            ds = (pr * (dp + corr)).astype(BF16)
            dq = lax.dot_general(ds, kband, _BDOT_NN, preferred_element_type=F32) * scale
            dkb = lax.dot_general(ds, qs, _BDOT_TN, preferred_element_type=F32) * scale
            dvb = lax.dot_general(pr.astype(BF16), dos, _BDOT_TN, preferred_element_type=F32)
            dsk = jnp.exp(sk - lse_c) * corr
            lane = lax.broadcasted_iota(jnp.int32, (8, LANES), 1)
            for p in range(cgw):
                tot = [jnp.sum(jnp.stack([dsk[i * cgw + p, h * ATT_BLOCK:(h + 1) * ATT_BLOCK] for i in range(len(rows))]),
                               axis=(0, 1)).reshape(1, 1) for h in range(2)]
                dsk_ref[:, p * LANES:(p + 1) * LANES] += jnp.where(lane == 0, tot[0], jnp.where(lane == HEAD_DIM, tot[1], 0.0))

            def gather_pairs(parts):
                if not shared:
                    return jnp.concatenate(parts, axis=1)
                tot = [parts[4 * h] + parts[4 * h + 1] + parts[4 * h + 2] + parts[4 * h + 3] for h in range(2)]
                tot = [t + pltpu.roll(t, HEAD_DIM, axis=1) for t in tot]
                return jnp.where(_lo_lanes(ATT_BLOCK), tot[0], tot[1])

            for i, r in enumerate(rows):
                mine = range(i * cgw, (i + 1) * cgw)
                dq_ref[r, :] = jnp.concatenate([_unstack_heads(dq[b]) for b in mine], axis=1)
                dk_ref[r, :] = ck[r, :] + gather_pairs([dkb[b, :ATT_BLOCK] for b in mine])
                dv_ref[r, :] = cv[r, :] + gather_pairs([dvb[b, :ATT_BLOCK] for b in mine])
                ck[r, :] = gather_pairs([dkb[b, ATT_BLOCK:] for b in mine])
                cv[r, :] = gather_pairs([dvb[b, ATT_BLOCK:] for b in mine])

        @pl.when(n < nb)
        def _():
            _att_streams(d, sb, work)

        @pl.when(n == nb)
        def _():
            dk_ref[...] = ck[...]
            dv_ref[...] = cv[...]

    sksp = BS((1, qw), lambda g, n: (0, g))
    rows = ATT_BLOCK * d
    osp = BS((rows, kw), lambda g, n: (jnp.maximum(n - 1, 0), 0 if shared else g))
    kshape = SDS((T, Wk), F32)
    dq, dk, dv, dsk = _pcall(
        body, name=name, grid=(G, nb + 1),
        in_specs=[qsp, csp, psp, csp, psp, sksp, qsp, qsp, qsp] + ([qsp] if dlse is not None else []),
        out_specs=[qsp, osp, osp, BS((8, qw), lambda g, n: (0, g))],
        out_shape=[SDS((T, Wq), F32), kshape, kshape, SDS((8, Wq), F32)],
        scratch_shapes=[pltpu.VMEM((rows, kw), F32)] * 2,
        compiler_params=_cparams(("parallel", "arbitrary")))(q, k, k, v, v, sinks, o, lse, do,
                                                              *([dlse] if dlse is not None else []))
    return dq, dk, dv, dsk[0:1]


@functools.partial(jax.custom_vjp, nondiff_argnums=(4, 5, 6, 7))
def band_attention(q, k, v, sinks, d, max_dist, with_lse, name):
    return _band_attention_fwd(q, k, v, sinks, d, max_dist, with_lse, name)[0]


def _band_attention_fwd(q, k, v, sinks, d, max_dist, with_lse, name):
    o, lse = _att_fwd_call(q, k, v, sinks, d, max_dist, name)
    return ((o, lse) if with_lse else o), (q, k, v, sinks, o, lse)


def _band_attention_bwd(d, max_dist, with_lse, name, res, g):
    q, k, v, sinks, o, lse = res
    do, dlse = g if with_lse else (g, None)
    return _att_bwd_call(q, k, v, sinks, o, lse, do, dlse, d, max_dist, name + "_bwd")


band_attention.defvjp(_band_attention_fwd, _band_attention_bwd)


def _merge_weights(ls):
    mx = jnp.maximum(jnp.maximum(ls[0], ls[1]), ls[2])
    es = [jnp.exp(l - mx) for l in ls]
    inv = 1.0 / (es[0] + es[1] + es[2])
    return [e * inv for e in es]


def _merge_fwd_call(os_, ls_, name):
    T, W = os_[0].shape
    tt = _tile(T, 1024, 8)

    def body(o1, o2, o3, l1, l2, l3, out_ref):
        w = _merge_weights([l1[...], l2[...], l3[...]])
        out_ref[...] = w[0] * o1[...] + w[1] * o2[...] + w[2] * o3[...]

    row = BS((tt, W), lambda i: (i, 0))
    return _pcall(body, name=name, grid=(T // tt,), in_specs=[row] * 6, out_specs=row,
                  out_shape=SDS((T, W), F32), compiler_params=_cparams(("parallel",)))(*os_, *ls_)


def _merge_bwd_call(os_, ls_, do, name):
    T, W = os_[0].shape
    tt = _tile(T, 512, 8)

    def body(o1, o2, o3, l1, l2, l3, do_ref, d1, d2, d3, e1, e2, e3):
        w = _merge_weights([l1[...], l2[...], l3[...]])
        dov = do_ref[...]
        ts = [dov * o[...] for o in (o1, o2, o3)]
        mean = w[0] * ts[0] + w[1] * ts[1] + w[2] * ts[2]
        for wi, ti, dref, eref in zip(w, ts, (d1, d2, d3), (e1, e2, e3)):
            dref[...] = wi * dov
            eref[...] = wi * (ti - mean)

    row = BS((tt, W), lambda i: (i, 0))
    return _pcall(body, name=name, grid=(T // tt,), in_specs=[row] * 7, out_specs=[row] * 6,
                  out_shape=[SDS((T, W), F32)] * 6, compiler_params=_cparams(("parallel",)))(*os_, *ls_, do)


@functools.partial(jax.custom_vjp, nondiff_argnums=(2,))
def merge3(os_, ls_, name):
    return _merge_fwd_call(os_, ls_, name)


def _merge3_fwd(os_, ls_, name):
    return _merge_fwd_call(os_, ls_, name), (os_, ls_)


def _merge3_bwd(name, res, do):
    os_, ls_ = res
    out = _merge_bwd_call(os_, ls_, do, name + "_bwd")
    return tuple(out[:3]), tuple(out[3:])


merge3.defvjp(_merge3_fwd, _merge3_bwd)


def _xa_probs(qb, kb, scale):
    s = lax.dot_general(qb, kb, (((1,), (1,)), ((), ())), preferred_element_type=F32) * scale
    e = jnp.exp(s - jnp.max(s, axis=-1, keepdims=True))
    return e / jnp.sum(e, axis=-1, keepdims=True)


def _xa_fwd_call(q, kv, name):
    T, W = q.shape
    M = kv.shape[0]
    hd = XA_HEAD_DIM
    tq = _tile(T, 4096, 8)
    scale = hd ** -0.5

    def body(q_ref, k_ref, v_ref, o_ref):
        p = _xa_probs(q_ref[...].astype(BF16), k_ref[...].astype(BF16), scale)
        o_ref[...] = jnp.dot(p.astype(BF16), v_ref[...].astype(BF16), preferred_element_type=F32)

    qsp = BS((tq, hd), lambda i, h: (i, h))
    return _pcall(body, name=name, grid=(T // tq, XA_HEADS),
                  in_specs=[qsp, BS((M, hd), lambda i, h: (0, h)), BS((M, hd), lambda i, h: (0, XA_HEADS + h))],
                  out_specs=qsp, out_shape=SDS((T, W), F32),
                  compiler_params=_cparams(("parallel", "parallel")))(q, kv, kv)


def _xa_bwd_call(q, kv, do, name):
    T, W = q.shape
    M = kv.shape[0]
    hd = XA_HEAD_DIM
    tq = _tile(T, 4096, 8)
    scale = hd ** -0.5
    dn_nt = (((1,), (1,)), ((), ()))
    dn_tn = (((0,), (0,)), ((), ()))

    def body(q_ref, k_ref, v_ref, do_ref, dq_ref, dk_ref, dv_ref):
        @pl.when(pl.program_id(1) == 0)
        def _():
            dk_ref[...] = jnp.zeros_like(dk_ref)
            dv_ref[...] = jnp.zeros_like(dv_ref)

        qb, kb, vb = q_ref[...].astype(BF16), k_ref[...].astype(BF16), v_ref[...].astype(BF16)
        p = _xa_probs(qb, kb, scale)
        dob = do_ref[...].astype(BF16)
        dp = lax.dot_general(dob, vb, dn_nt, preferred_element_type=F32)
        ds = (p * (dp - jnp.sum(p * dp, axis=-1, keepdims=True))).astype(BF16)
        dq_ref[...] = jnp.dot(ds, kb, preferred_element_type=F32) * scale
        dk_ref[...] += lax.dot_general(ds, qb, dn_tn, preferred_element_type=F32) * scale
        dv_ref[...] += lax.dot_general(p.astype(BF16), dob, dn_tn, preferred_element_type=F32)

    qsp = BS((tq, hd), lambda h, i: (i, h))
    ksp = BS((M, hd), lambda h, i: (0, h))
    return _pcall(body, name=name, grid=(XA_HEADS, T // tq),
                  in_specs=[qsp, ksp, BS((M, hd), lambda h, i: (0, XA_HEADS + h)), qsp],
                  out_specs=[qsp, ksp, ksp], out_shape=[SDS((T, W), F32), SDS((M, W), F32), SDS((M, W), F32)],
                  compiler_params=_cparams(("parallel", "arbitrary")))(q, kv, kv, do)


@functools.partial(jax.custom_vjp, nondiff_argnums=(2,))
def cross_attention(q, kv, name):
    return _xa_fwd_call(q, kv, name)


def _cross_attention_fwd(q, kv, name):
    return _xa_fwd_call(q, kv, name), (q, kv)


def _cross_attention_bwd(name, res, do):
    q, kv = res
    dq, dk, dv = _xa_bwd_call(q, kv, do, name + "_bwd")
    return dq, jnp.concatenate([dk, dv], axis=1)


cross_attention.defvjp(_cross_attention_fwd, _cross_attention_bwd)


def _gate_up_swiglu_call(hn, w1t, name):
    T, K = hn.shape
    F = w1t.shape[0] // 2
    tm, tn = _tile(T, 4096), _tile(F, 256)
    nj = F // tn
    dn = (((1,), (1,)), ((), ()))

    def body(a_ref, wg_ref, wu_ref, g_ref, u_ref, act_ref):
        a = a_ref[...]
        g = lax.dot_general(a, wg_ref[...], dn, preferred_element_type=F32)
        u = lax.dot_general(a, wu_ref[...], dn, preferred_element_type=F32)
        g_ref[...] = g
        u_ref[...] = u
        act_ref[...] = ((g * _sigmoid(g)) * u).astype(BF16)

    tile = BS((tm, tn), lambda i, j: (i, j))
    return _pcall(body, name=name, grid=(T // tm, nj),
                  in_specs=[BS((tm, K), lambda i, j: (i, 0)), BS((tn, K), lambda i, j: (j, 0)),
                            BS((tn, K), lambda i, j: (j + nj, 0))],
                  out_specs=[tile, tile, tile], out_shape=[SDS((T, F), F32), SDS((T, F), F32), SDS((T, F), BF16)],
                  compiler_params=_cparams(("parallel", "parallel")))(hn, w1t, w1t)


def _swiglu_bwd_call(g, u, dact, name):
    T, F = g.shape
    tt = _tile(T, 256, 16)

    def body(g_ref, u_ref, d_ref, o_ref):
        g, d = g_ref[...], d_ref[...]
        sg = _sigmoid(g)
        o_ref[:, :F] = (d * u_ref[...] * (sg * (1.0 + g * (1.0 - sg)))).astype(BF16)
        o_ref[:, F:] = (d * (g * sg)).astype(BF16)

    row = BS((tt, F), lambda i: (i, 0))
    return _pcall(body, name=name, grid=(T // tt,), in_specs=[row, row, row],
                  out_specs=BS((tt, 2 * F), lambda i: (i, 0)), out_shape=SDS((T, 2 * F), BF16),
                  compiler_params=_cparams(("parallel",)))(g, u, dact)


@functools.partial(jax.custom_vjp, nondiff_argnums=(6,))
def ffn_block(h, g, w1b, w1c, w2b, w2c, name):
    return _ffn_fwd(h, g, w1b, w1c, w2b, w2c, name)[0]


def _ffn_fwd(h, g, w1b, w1c, w2b, w2c, name):
    hn = _rms_fwd_call(h, g, name + "_norm", BF16)
    gate, up, act = _gate_up_swiglu_call(hn, w1b, name + "_gu")
    out = _mm(act, w2b, add=h, name=name + "_down", tm=1024, tn=1024, tk=2816)
    return out, (h, g, hn, gate, up, act, w1b, w2b)


def _ffn_bwd(name, res, dout):
    h, g, hn, gate, up, act, w1b, w2b = res
    dact = _mm(dout, w2b, tb=True, name=name + "_down_da", tm=1024, tn=1408, tk=1024)
    dw2 = _mm(act, dout, ta=True, name=name + "_down_dw", tm=1408, tn=1024, tk=1024)
    dgu = _swiglu_bwd_call(gate, up, dact, name + "_swiglu_bwd")
    dw1 = _mm(dgu, hn, ta=True, name=name + "_gu_dw", tm=1408, tn=1024, tk=2048)
    dh, dg = _mm_rms_bwd(dgu, w1b, h, g, dout, name + "_gu_da", wt=True)
    return dh, dg.reshape(g.shape), jnp.zeros_like(w1b), dw1, jnp.zeros_like(w2b), dw2


ffn_block.defvjp(_ffn_fwd, _ffn_bwd)


def _final_call(h, g, target, name):
    T, Dm = h.shape
    tt = _tile(T, 1024, 8)

    def body(x_ref, g_ref, t_ref, loss_ref, dx_ref, dg_ref):
        @pl.when(pl.program_id(0) == 0)
        def _():
            loss_ref[...] = jnp.zeros_like(loss_ref)
            dg_ref[...] = jnp.zeros_like(dg_ref)

        xv, gv = x_ref[...], g_ref[...]
        r = lax.rsqrt(jnp.mean(xv * xv, axis=-1, keepdims=True) + NORM_EPS)
        xh = xv * r
        err = xh * gv - t_ref[...]
        loss_ref[...] += 0.5 * jnp.sum(jnp.mean(err * err, axis=-1, keepdims=True), axis=0, keepdims=True)
        dy = err * (1.0 / Dm)
        dyg = dy * gv
        dx_ref[...] = r * (dyg - xh * jnp.mean(dyg * xh, axis=-1, keepdims=True))
        dg_ref[...] += jnp.sum(dy * xh, axis=0, keepdims=True)

    row = BS((tt, Dm), lambda i: (i, 0))
    vec = BS((1, Dm), lambda i: (0, 0))
    return _pcall(body, name=name, grid=(T // tt,), in_specs=[row, vec, row],
                  out_specs=[BS((1, 1), lambda i: (0, 0)), row, vec],
                  out_shape=[SDS((1, 1), F32), SDS((T, Dm), F32), SDS((1, Dm), F32)],
                  compiler_params=_cparams(("arbitrary",)))(h, g.reshape(1, Dm), target)


ADAMW_BLOCK_ELEMS = 256 * 1024


def _adamw_call(parts, w, m, v, name):
    shape = w.shape
    if not isinstance(parts, (list, tuple)):
        parts, shape3 = [parts], (1,) + shape
    else:
        shape3 = shape
    n_lead = shape3[0]
    r, N = shape3[-2], shape3[-1]
    Ld = math.prod(shape3[1:-2])
    w, m, v = (t.reshape(n_lead * Ld, r, N) for t in (w, m, v))
    tr = _tile(r, max(8, ADAMW_BLOCK_ELEMS // N), 8)
    c1 = 1.0 - ADAM_B1 ** ADAM_STEP
    c2 = 1.0 - ADAM_B2 ** ADAM_STEP
    outs = None
    for lead, p in enumerate(parts):
        def body(p_ref, w_ref, m_ref, v_ref, *rest):
            g_ref, d_ref, nm_ref, nv_ref = rest[-4:]
            g = p_ref[0]
            for j in range(1, N_DEV):
                g = g + p_ref[j]
            nm = ADAM_B1 * m_ref[...] + (1.0 - ADAM_B1) * g
            nv = ADAM_B2 * v_ref[...] + (1.0 - ADAM_B2) * (g * g)
            g_ref[...] = g
            nm_ref[...] = nm
            nv_ref[...] = nv
            d_ref[...] = -ADAM_LR * ((nm / c1) / (jnp.sqrt(nv / c2) + ADAM_EPS) + ADAM_WD * w_ref[...])

        base = lead * Ld
        row = BS((1, tr, N), lambda l, i, base=base: (base + l, i, 0))
        prev = [] if outs is None else list(outs)
        outs = _pcall(body, name=f"{name}_{lead}", grid=(Ld, r // tr),
                      in_specs=[BS((N_DEV, 1, tr, N), lambda l, i: (0, l, i, 0)), row, row, row]
                      + [BS(memory_space=pl.ANY)] * len(prev),
                      out_specs=[row] * 4, out_shape=[SDS((n_lead * Ld, r, N), F32)] * 4,
                      input_output_aliases={4 + j: j for j in range(len(prev))},
                      compiler_params=_cparams(("parallel", "parallel")))(p.reshape(N_DEV, Ld, r, N), w, m, v, *prev)
    return [t.reshape(shape) for t in outs]


def _place():
    return lax.axis_index("x"), lax.axis_index("y"), lax.axis_index("c")


def _all_gather(xs, name):
    n = len(xs)
    pairs = [(i, l) for i, x in enumerate(xs) for l in range(x.shape[0])]

    def body(*refs):
        x_refs, o_refs = refs[:n], refs[n:2 * n]
        send_sems, recv_sems, local_sems = refs[2 * n:]
        x_, y_, c_ = _place()
        me, sibling = (x_, y_, c_), (x_, y_, 1 - c_)
        chips = [(1 - x_, y_), (x_, 1 - y_), (1 - x_, 1 - y_)]

        def copy(e, k, block, to, from_input=False):
            i, l = pairs[e]
            px, py, pc = block
            dst = o_refs[i].at[l, 4 * px + 2 * py + pc]
            return pltpu.make_async_remote_copy(
                src_ref=x_refs[i].at[l] if from_input else dst, dst_ref=dst,
                send_sem=send_sems.at[7 * e + k], recv_sem=recv_sems.at[7 * e + k],
                device_id=to, device_id_type=pl.DeviceIdType.MESH)

        every = range(len(pairs))
        mine = [pltpu.make_async_copy(x_refs[i].at[l], o_refs[i].at[l, 4 * x_ + 2 * y_ + c_], local_sems.at[e])
                for e, (i, l) in enumerate(pairs)]
        for cp in mine:
            cp.start()
        first = [copy(e, 0, me, sibling, True) for e in every]
        first += [copy(e, 1 + j, me, (*chip, c_), True) for j, chip in enumerate(chips) for e in every]
        for cp in first:
            cp.start()
        passed = []
        for j, chip in enumerate(chips):
            for e in every:
                copy(e, 1 + j, (*chip, c_), me).wait_recv()
            for e in every:
                cp = copy(e, 4 + j, (*chip, c_), sibling)
                cp.start()
                passed.append(cp)
        for e in every:
            copy(e, 0, sibling, me).wait_recv()
        for j, chip in enumerate(chips):
            for e in every:
                copy(e, 4 + j, (*chip, 1 - c_), me).wait_recv()
        for cp in first + passed:
            cp.wait_send()
        for cp in mine:
            cp.wait()

    any_spec = BS(memory_space=pl.ANY)
    return _pcall(body, name=name, in_specs=[any_spec] * n, out_specs=[any_spec] * n,
                  out_shape=[SDS((x.shape[0], N_DEV) + x.shape[1:], x.dtype) for x in xs],
                  scratch_shapes=[pltpu.SemaphoreType.DMA((7 * len(pairs),)), pltpu.SemaphoreType.DMA((7 * len(pairs),)),
                                  pltpu.SemaphoreType.DMA((len(pairs),))],
                  compiler_params=pltpu.CompilerParams(has_side_effects=True))(*xs)


def _peer_of(k, place):
    x_, y_, c_ = place
    fx, fy, fc = (k >> 2) & 1, (k >> 1) & 1, k & 1
    return fx + x_ - 2 * fx * x_, fy + y_ - 2 * fy * y_, fc + c_ - 2 * fc * c_


def _split_copy(src_ref, land_ref, send_sems, recv_sems, e, k, place, scatter):
    x_, y_, c_ = place
    px, py, pc = _peer_of(k, place)
    return pltpu.make_async_remote_copy(
        src_ref=src_ref.at[4 * px + 2 * py + pc] if scatter else src_ref, dst_ref=land_ref.at[4 * x_ + 2 * y_ + c_],
        send_sem=send_sems.at[7 * e + k - 1], recv_sem=recv_sems.at[7 * e + k - 1],
        device_id=(px, py, pc), device_id_type=pl.DeviceIdType.MESH)


def _own_copy(src_ref, land_ref, sems, slot, place, scatter):
    x_, y_, c_ = place
    me = 4 * x_ + 2 * y_ + c_
    return pltpu.make_async_copy(src_ref.at[me] if scatter else src_ref, land_ref.at[me], sems.at[slot])


_HBM_SPEC = BS(memory_space=pltpu.HBM)
_SEM_SPEC = BS(memory_space=pltpu.SEMAPHORE)
_EFFECT = pltpu.SideEffectType.DATAFLOW_SIDE_EFFECTING


def _copies_start(srcs, scatter, name, thru=None):
    n = len(srcs)
    lands = [lax.empty(s.shape if scatter else (N_DEV,) + s.shape, s.dtype) for s in srcs]
    passed = srcs + lands + list(thru or ())

    def body(*refs):
        src_refs, land_refs = refs[:n], refs[n:2 * n]
        send_sems, recv_sems = refs[len(passed)], refs[len(passed) + 1]
        token = refs[-1]
        place = _place()
        for e in range(n):
            for k in range(1, N_DEV):
                _split_copy(src_refs[e], land_refs[e], send_sems, recv_sems, e, k, place, scatter).start()
            _own_copy(src_refs[e], land_refs[e], send_sems, 7 * n + e, place, scatter).start()
        token[...] = jnp.zeros_like(token)

    hbm = lambda t: pltpu.with_memory_space_constraint(t, pltpu.HBM)
    out = _pcall(body, name=name,
                 out_shape=(pltpu.SemaphoreType.DMA((8 * n,)), pltpu.SemaphoreType.DMA((7 * n,)),
                            *[pltpu.HBM(t.shape, t.dtype) for t in passed], SDS((8, LANES), F32)),
                 in_specs=[_HBM_SPEC] * len(passed),
                 out_specs=(_SEM_SPEC, _SEM_SPEC, *[_HBM_SPEC] * len(passed), BS(memory_space=pltpu.VMEM)),
                 input_output_aliases={i: 2 + i for i in range(len(passed))},
                 compiler_params=pltpu.CompilerParams(has_side_effects=_EFFECT))(*[hbm(t) for t in passed])
    return out[0], out[1], list(out[2:2 + n]), list(out[2 + n:2 + 2 * n]), list(out[2 + 2 * n:-1])


def _copies_wait(started, which, scatter, after, name):
    send_sems, recv_sems, srcs, lands, _ = started
    n, n_started = len(which), len(srcs)
    after = list(after) if isinstance(after, (list, tuple)) else [after]

    def body(*refs):
        src_refs, land_refs = refs[:n], refs[n:2 * n]
        send_s, recv_s = refs[2 * n], refs[2 * n + 1]
        place = _place()
        for j, e in enumerate(which):
            for k in range(1, N_DEV):
                cp = _split_copy(src_refs[j], land_refs[j], send_s, recv_s, e, k, place, scatter)
                cp.wait_send()
                cp.wait_recv()
            _own_copy(src_refs[j], land_refs[j], send_s, 7 * n_started + e, place, scatter).wait()

    args = [srcs[e] for e in which] + [lands[e] for e in which]
    out = _pcall(body, name=name, out_shape=tuple(pltpu.HBM(t.shape, t.dtype) for t in args),
                 in_specs=[_HBM_SPEC] * (2 * n) + [_SEM_SPEC, _SEM_SPEC] + [BS(memory_space=pl.ANY)] * len(after),
                 out_specs=tuple([_HBM_SPEC] * (2 * n)), input_output_aliases={i: i for i in range(2 * n)},
                 compiler_params=pltpu.CompilerParams(has_side_effects=_EFFECT))(*args, send_sems, recv_sems, *after)
    return list(out[:n]), list(out[n:])


def _pad_flat(t, quantum=PACK_QUANTUM):
    f = t.reshape(-1)
    pad = (-f.shape[0]) % quantum
    return jnp.pad(f, (0, pad)) if pad else f


def _pack(arrs, dtype):
    return jnp.concatenate([_pad_flat(a.astype(dtype)) for a in arrs]).reshape(-1, LANES)


def _unpack(buf, shapes, lead=()):
    flat = buf.reshape(lead + (-1,))
    out, off = [], 0
    for s in shapes:
        n = math.prod(s)
        out.append(flat[..., off:off + n].reshape(lead + tuple(s)))
        off += n + (-n) % PACK_QUANTUM
    return out


def _full_from_gathered(g, axis):
    t = jnp.moveaxis(g, 0, axis)
    s = t.shape
    return t.reshape(s[:axis] + (s[axis] * s[axis + 1],) + s[axis + 2:])


def _parts_from_full(t, axis):
    s = t.shape
    t = t.reshape(s[:axis] + (N_DEV, s[axis] // N_DEV) + s[axis + 1:])
    return jnp.moveaxis(t, axis, 0)


def _head_rows(t):
    return jnp.repeat(t, HEAD_DIM).reshape(1, -1)


def _dilated_attention(q, k, v, name):
    no_sink = jnp.full((1, q.shape[1]), NEG, F32)
    outs, lses = zip(*[band_attention(q, k, v, no_sink, d, ATT_BLOCK, True, f"{name}_d{d}") for d in DILATIONS])
    return merge3(tuple(outs), tuple(lses), name + "_merge")


STAGES = (
    ("proj0", ('mix_norm',), (('ab_w_in', 0),)),
    ("mixer0", ('lru_conv_w', 'lru_conv_b', 'lru_ba', 'lru_bx', 'lru_lambda'),
     (('lru_wa', 0), ('lru_wx', 0), ('ab_w_out', 0))),
    ("xa0", ('xa_norm', 'xa_mem_norm'), (('xa_wq', 0), ('xa_wkv', 0), ('xa_wo', 0))),
    ("ffn0", ('ffn_norm',), (('ffn_w_gate_up', 0), ('ffn_w_down', 0))),
    ("mixer1", ('mix_norm', 'c_b_qkv', 'c_sinks', 'c_b_out'), (('c_w_qkv', 0), ('c_w_out', 0))),
    ("xa1", ('xa_norm', 'xa_mem_norm'), (('xa_wq', 1), ('xa_wkv', 1), ('xa_wo', 1))),
    ("ffn1", ('ffn_norm',), (('ffn_w_gate_up', 1), ('ffn_w_down', 1))),
)


def _stage_fn(stage, Wb, tabs, mem):
    layer = int(stage[-1])
    L = f"l{layer}"

    def run(S, Cw, h):
        def lin(a, key, bias, add, name, rows=None):
            wb, wc = Wb[key], Cw[key]
            if rows is not None:
                wb, wc = wb[rows], wc[rows]
            return linear(a, wb, wc, bias, add, name)

        def norm_lin(a, gain, key, bias, name):
            return norm_linear(a, gain, Wb[key], Cw[key], bias, key[0] in COLUMN_CUT, name)

        if stage == "mixer0":
            h, x_br, y_br, q, k, v = h
            C = S['lru_conv_w'].shape[-1]
            xc = conv4(x_br, S['lru_conv_w'][0], S['lru_conv_b'][0], L + "_conv")
            ga, gx = lru_gates(xc, Wb['lru_wa', 0], Cw['lru_wa', 0], S['lru_ba'][0],
                               Wb['lru_wx', 0], Cw['lru_wx', 0], S['lru_bx'][0], L + "_gates")
            rec = lru_scan(xc, ga, gx, y_br, S['lru_lambda'][0], L + "_scan")
            att = _dilated_attention(q, k, v, L + "_att")
            h = lin(att, ('ab_w_out', 0), None, h, L + "_w_out_att", slice(C, None))
            return lin(rec, ('ab_w_out', 0), None, h, L + "_w_out_rec", slice(0, C))
        if stage == "mixer1":
            qw = C_HEADS * HEAD_DIM
            kw = C_KV_HEADS * HEAD_DIM
            q, k, v, h = norm_linear_pieces(h, S['mix_norm'][1], Wb['c_w_qkv', 0], Cw['c_w_qkv', 0], S['c_b_qkv'][0], tabs,
                                            ((qw, True), (kw, True), (kw, False)), L + "_w_qkv")
            o = band_attention(q, k, v, _head_rows(S['c_sinks'][0]), 1, ATT_BLOCK - 1, False, L + "_att")
            return lin(o, ('c_w_out', 0), S['c_b_out'][0], h, L + "_w_out")
        if stage.startswith("xa"):
            xq, h = norm_lin(h, S['xa_norm'][layer], ('xa_wq', layer), None, L + "_xa_wq")
            xkv, _ = norm_lin(mem, S['xa_mem_norm'][layer], ('xa_wkv', layer), None, L + "_xa_wkv")
            return lin(cross_attention(xq, xkv, L + "_xa"), ('xa_wo', layer), None, h, L + "_xa_wo")
        gu, down = ('ffn_w_gate_up', layer), ('ffn_w_down', layer)
        return ffn_block(h, S['ffn_norm'][layer], Wb[gu], Cw[gu], Wb[down], Cw[down], L + "_ffn")

    return run


def kernel(x, mem, mix_norm, ab_w_in, lru_conv_w, lru_conv_b, lru_wa, lru_ba, lru_wx, lru_bx, lru_lambda, ab_w_out, c_w_qkv, c_b_qkv, c_sinks, c_w_out, c_b_out, xa_norm, xa_mem_norm, xa_wq, xa_wkv, xa_wo, ffn_norm, ffn_w_gate_up, ffn_w_down, final_norm, loss_target, m_mix_norm, m_ab_w_in, m_lru_conv_w, m_lru_conv_b, m_lru_wa, m_lru_ba, m_lru_wx, m_lru_bx, m_lru_lambda, m_ab_w_out, m_c_w_qkv, m_c_b_qkv, m_c_sinks, m_c_w_out, m_c_b_out, m_xa_norm, m_xa_mem_norm, m_xa_wq, m_xa_wkv, m_xa_wo, m_ffn_norm, m_ffn_w_gate_up, m_ffn_w_down, m_final_norm, v_mix_norm, v_ab_w_in, v_lru_conv_w, v_lru_conv_b, v_lru_wa, v_lru_ba, v_lru_wx, v_lru_bx, v_lru_lambda, v_ab_w_out, v_c_w_qkv, v_c_b_qkv, v_c_sinks, v_c_w_out, v_c_b_out, v_xa_norm, v_xa_mem_norm, v_xa_wq, v_xa_wkv, v_xa_wo, v_ffn_norm, v_ffn_w_gate_up, v_ffn_w_down, v_final_norm):
    w_loc = dict(zip(WEIGHT_NAMES, (mix_norm, ab_w_in, lru_conv_w, lru_conv_b, lru_wa, lru_ba, lru_wx, lru_bx, lru_lambda, ab_w_out, c_w_qkv, c_b_qkv, c_sinks, c_w_out, c_b_out, xa_norm, xa_mem_norm, xa_wq, xa_wkv, xa_wo, ffn_norm, ffn_w_gate_up, ffn_w_down, final_norm)))
    m_loc = dict(zip(WEIGHT_NAMES, (m_mix_norm, m_ab_w_in, m_lru_conv_w, m_lru_conv_b, m_lru_wa, m_lru_ba, m_lru_wx, m_lru_bx, m_lru_lambda, m_ab_w_out, m_c_w_qkv, m_c_b_qkv, m_c_sinks, m_c_w_out, m_c_b_out, m_xa_norm, m_xa_mem_norm, m_xa_wq, m_xa_wkv, m_xa_wo, m_ffn_norm, m_ffn_w_gate_up, m_ffn_w_down, m_final_norm)))
    v_loc = dict(zip(WEIGHT_NAMES, (v_mix_norm, v_ab_w_in, v_lru_conv_w, v_lru_conv_b, v_lru_wa, v_lru_ba, v_lru_wx, v_lru_bx, v_lru_lambda, v_ab_w_out, v_c_w_qkv, v_c_b_qkv, v_c_sinks, v_c_w_out, v_c_b_out, v_xa_norm, v_xa_mem_norm, v_xa_wq, v_xa_wkv, v_xa_wo, v_ffn_norm, v_ffn_w_gate_up, v_ffn_w_down, v_final_norm)))

    first_keys = list(STAGES[0][2])
    keys = [key for _, _, stage_keys in STAGES[1:] for key in stage_keys]
    shards = [_shard_view(n, w_loc[n])[l].astype(BF16) for n, l in keys]
    first_g = _all_gather([_pack([w_loc[n] for n in SMALL], F32)[None]]
                          + [_shard_view(n, w_loc[n])[l].astype(BF16)[None] for n, l in first_keys], "gather_first")
    gather = _copies_start(shards, False, "gather_start", thru=[first_g[0]])
    small_g = gather[4][0][0]
    Wb = {key: _full_from_gathered(g[0], _layer_shard_axis(key[0])) for key, g in zip(first_keys, first_g[1:])}
    S = {n: w_loc[n] for n in REPLICATED}
    for n, t in zip(SMALL, _unpack(small_g, [w_loc[n].shape for n in SMALL], lead=(N_DEV,))):
        S[n] = _full_from_gathered(t, SHARD_AXIS[n])

    tabs = _rope_tables(x.shape[1])
    w_in = Wb[first_keys[0]]
    hn0 = _rms_fwd_call(x[0], S['mix_norm'][0], "l0_w_in_norm", BF16)
    lru_w, att_w = N_DEV * lru_conv_w.shape[-1], B_HEADS * HEAD_DIM
    pieces = (("x", lru_w, False), ("y", lru_w, False), ("q", att_w, True), ("k", att_w, True), ("v", att_w, False))
    h, row = [x[0]], 0
    for piece, width, rotated in pieces:
        h.append(_mm(hn0, w_in[row:row + width], tb=True, rope=tabs if rotated else None, name="l0_w_in_" + piece,
                     tm=2048, tn=1024, tk=1024))
        row += width
    h = tuple(h)
    vjps = []
    for stage, small_names, stage_keys in STAGES[1:]:
        which = [keys.index(key) for key in stage_keys]
        _, lands = _copies_wait(gather, which, False, jax.tree.leaves(h)[-1], "gather_wait_" + stage)
        for e, land in zip(which, lands):
            Wb[keys[e]] = _full_from_gathered(land, _layer_shard_axis(keys[e][0]))
        carriers = {key: jnp.zeros(Wb[key].shape, F32) for key in stage_keys}
        h, vjp_fn = jax.vjp(_stage_fn(stage, Wb, tabs, mem[0]), {n: S[n] for n in small_names}, carriers, h)
        vjps.append(vjp_fn)
    loss_part, dh, dg_final = _final_call(h, S['final_norm'], loss_target[0], "final_loss")

    grads = {'final_norm': dg_final.reshape(final_norm.shape)}
    exchanges, send_keys, send_parts = [], [], []

    def start_exchange(stage, dh):
        leaves, tree = jax.tree.flatten(dh)
        started = _copies_start(list(send_parts), True, "grad_start_" + stage, thru=leaves)
        exchanges.append((stage, started, list(send_keys)))
        send_keys.clear()
        send_parts.clear()
        return jax.tree.unflatten(tree, started[4])

    for (stage, small_names, stage_keys), vjp_fn in zip(reversed(STAGES[1:]), reversed(vjps)):
        g_small, g_big, dh = vjp_fn(dh)
        for n in small_names:
            grads[n] = grads[n] + g_small[n] if n in grads else g_small[n]
        send_keys += list(stage_keys)
        send_parts += [_parts_from_full(g_big[key], _layer_shard_axis(key[0])) for key in stage_keys]
        if stage == "xa1":
            continue
        if stage == "mixer0":
            small_parts = [_parts_from_full(grads[n], SHARD_AXIS[n]) for n in SMALL]
            send_keys.append("small")
            send_parts.append(jnp.stack([_pack([p[j] for p in small_parts], F32) for j in range(N_DEV)]))
        dh = start_exchange(stage, dh)
    d_res, d_proj = dh[0], _join_pieces_call(dh[1:], [rotated for _, _, rotated in pieces], tabs, "l0_w_in_dproj")[0]
    send_keys.append(first_keys[0])
    send_parts.append(_parts_from_full(_mm(d_proj, hn0, ta=True, name="l0_w_in_dw", tm=1408, tn=1024, tk=2048),
                                       _layer_shard_axis(first_keys[0][0])))
    d_res, d_proj = start_exchange("proj0", (d_res, d_proj))
    dx, dg0 = _mm_rms_bwd(d_proj, w_in, x[0], S['mix_norm'][0], d_res, "l0_w_in_da", wt=True)
    grads['mix_norm'] = grads['mix_norm'] + jnp.concatenate([dg0, jnp.zeros_like(dg0)], axis=0)
    rep_names = REPLICATED + ["loss"]
    grads["loss"] = loss_part
    zero = jnp.zeros((1, 1), F32)
    for d in (w_loc, m_loc, v_loc):
        d["loss"] = zero
    rep_started = _copies_start([_pack([grads[n] for n in rep_names], F32)], False, "rep_grads_start")

    parts, out = {}, {}

    def end_exchange(stage, started, ex_keys, after):
        _, lands = _copies_wait(started, list(range(len(ex_keys))), True, after, "grad_wait_" + stage)
        parts.update(zip(ex_keys, lands))

    def adamw(p, names, call_name):
        if len(names) == 1:
            n = names[0]
            res = _adamw_call(p, _shard_view(n, w_loc[n]), _shard_view(n, m_loc[n]), _shard_view(n, v_loc[n]), call_name)
            for kind, t in zip(("grad", "delta", "new_m", "new_v"), res):
                out[kind, n] = _shard_view(n, t)
        else:
            res = _adamw_call(p, *[_pack([d[n] for n in names], F32) for d in (w_loc, m_loc, v_loc)], call_name)
            for kind, buf in zip(("grad", "delta", "new_m", "new_v"), res):
                for n, t in zip(names, _unpack(buf, [w_loc[n].shape for n in names])):
                    out[kind, n] = t

    for ex in exchanges[:-1]:
        end_exchange(*ex, dx)
    last_names = {key[0] for key in exchanges[-1][2]}
    for n in BIG:
        if n not in last_names:
            adamw([parts[n, l] for l in range(w_loc[n].shape[0])], [n], "adamw_" + n)
    adamw(parts["small"], SMALL, "adamw_small")
    end_exchange(*exchanges[-1], [out["new_v", n] for n in BIG if n not in last_names])
    for n in BIG:
        if n in last_names:
            adamw([parts[n, l] for l in range(w_loc[n].shape[0])], [n], "adamw_" + n)
    _, rep_land = _copies_wait(rep_started, [0], False, out["new_v", "ab_w_in"], "rep_grads_wait")
    adamw(rep_land[0], rep_names, "adamw_replicated")
    loss = out["grad", "loss"][0, 0]

    return (loss, dx[None], *[out[kind, n] for kind in ("grad", "delta", "new_m", "new_v") for n in WEIGHT_NAMES])
```

```python
import functools
import math

import jax
import jax.numpy as jnp
from jax import lax
from jax.experimental import pallas as pl
from jax.experimental.pallas import tpu as pltpu

F32 = jnp.float32
BF16 = jnp.bfloat16
SDS = jax.ShapeDtypeStruct
BS = pl.BlockSpec

N_DEV = 8
NORM_EPS = 1e-6
ROPE_THETA = 500000.0
HEAD_DIM = 64
ROT_DIM = 16
ATT_BLOCK = 128
LRU_C = 8.0
LRU_HEADS = 4
DILATIONS = (1, 4, 16)
B_HEADS = 8
C_HEADS = 16
C_KV_HEADS = 2
XA_HEADS = 4
XA_HEAD_DIM = 128
NEG = -1e30
ADAM_LR, ADAM_B1, ADAM_B2, ADAM_EPS, ADAM_WD, ADAM_STEP = 0.001, 0.9, 0.999, 1e-08, 0.01, 10
LANES = 128
VMEM_LIMIT = 48 * 1024 * 1024

WEIGHT_NAMES = ['mix_norm', 'ab_w_in', 'lru_conv_w', 'lru_conv_b', 'lru_wa', 'lru_ba', 'lru_wx', 'lru_bx',
                'lru_lambda', 'ab_w_out', 'c_w_qkv', 'c_b_qkv', 'c_sinks', 'c_w_out', 'c_b_out', 'xa_norm',
                'xa_mem_norm', 'xa_wq', 'xa_wkv', 'xa_wo', 'ffn_norm', 'ffn_w_gate_up', 'ffn_w_down', 'final_norm']
SHARD_AXIS = {'ab_w_in': 2, 'lru_conv_w': 2, 'lru_wa': 2, 'lru_ba': 2, 'lru_wx': 2, 'lru_bx': 2, 'ab_w_out': 1,
              'c_w_qkv': 2, 'c_b_qkv': 1, 'c_w_out': 1, 'c_b_out': 1, 'xa_wq': 1, 'xa_wkv': 1, 'xa_wo': 2,
              'ffn_w_gate_up': 2, 'ffn_w_down': 1}
BIG = ['ab_w_in', 'lru_wa', 'lru_wx', 'ab_w_out', 'c_w_qkv', 'c_w_out', 'xa_wq', 'xa_wkv', 'xa_wo',
       'ffn_w_gate_up', 'ffn_w_down']
SMALL = ['lru_conv_w', 'lru_ba', 'lru_bx', 'c_b_qkv', 'c_b_out']
REPLICATED = [n for n in WEIGHT_NAMES if n not in SHARD_AXIS]
COLUMN_CUT = ('ab_w_in', 'c_w_qkv', 'ffn_w_gate_up')
PACK_QUANTUM = 2048


def _shard_view(name, t):
    return jnp.swapaxes(t, -1, -2) if name in COLUMN_CUT else t


def _layer_shard_axis(name):
    return 0 if name in COLUMN_CUT else SHARD_AXIS[name] - 1


def _pcall(body, **kw):
    return pl.pallas_call(body, **kw)


def _cparams(sem=None):
    return pltpu.CompilerParams(dimension_semantics=sem, vmem_limit_bytes=VMEM_LIMIT)


def _tile(n, target, mult=LANES):
    if n <= target:
        return n
    t = (target // mult) * mult
    while t >= mult:
        if n % t == 0:
            return t
        t -= mult
    return n


def _sigmoid(x):
    return 1.0 / (1.0 + jnp.exp(-x))


def _expm1(x):
    small = x * (1.0 + x * (0.5 + x * (1.0 / 6.0 + x * (1.0 / 24.0))))
    return jnp.where(jnp.abs(x) < 0.03, small, jnp.exp(x) - 1.0)


_GELU_C = math.sqrt(2.0 / math.pi)


def _gelu_parts(y):
    y2 = y * y
    th = jnp.tanh(_GELU_C * (y + 0.044715 * y * y2))
    g = 0.5 * y * (1.0 + th)
    dg = 0.5 * (1.0 + th) + 0.5 * y * (1.0 - th * th) * _GELU_C * (1.0 + 3.0 * 0.044715 * y2)
    return g, dg


def _rotate(xv, tab_refs, inverse=False):
    W = xv.shape[1]
    half = ROT_DIM // 2
    c, sa, sb = (jnp.tile(t[...], (1, W // LANES)) for t in tab_refs)
    if not inverse:
        return xv * c + pltpu.roll(xv, half, axis=1) * sa + pltpu.roll(xv, W - half, axis=1) * sb
    return xv * c + pltpu.roll(xv * sa, W - half, axis=1) + pltpu.roll(xv * sb, half, axis=1)


def _mm(a, b, *, ta=False, tb=False, bias=None, add=None, rope=None, name, tm=512, tn=512, tk=2048):
    halves = a.ndim == 3
    if halves:
        M, K = 2 * a.shape[2], a.shape[1]
        tm = min(tm, a.shape[2])
    else:
        M, K = (a.shape[1], a.shape[0]) if ta else a.shape
    N = b.shape[0] if tb else b.shape[1]
    tm, tn, tk = _tile(a.shape[2] if halves else M, tm), _tile(N, tn), _tile(K, tk)
    nk = K // tk
    dn = (((0 if ta else 1,), (1 if tb else 0,)), ((), ()))

    def body(*refs):
        a_ref, b_ref = refs[0], refs[1]
        pos = 2
        bias_ref = add_ref = None
        if bias is not None:
            bias_ref = refs[pos]
            pos += 1
        if add is not None:
            add_ref = refs[pos]
            pos += 1
        tab_refs = refs[pos:pos + 3] if rope is not None else None
        pos += 3 if rope is not None else 0
        o_ref = refs[pos]
        part = lax.dot_general(a_ref[...].astype(BF16), b_ref[...].astype(BF16), dn, preferred_element_type=F32)

        def finish(r):
            if bias_ref is not None:
                r = r + bias_ref[...]
            if add_ref is not None:
                r = r + add_ref[...]
            o_ref[...] = r if tab_refs is None else _rotate(r, tab_refs)

        if nk == 1:
            finish(part)
            return
        acc_ref = refs[pos + 1]
        k = pl.program_id(2)

        @pl.when(k == 0)
        def _():
            acc_ref[...] = part

        @pl.when((k > 0) & (k < nk - 1))
        def _():
            acc_ref[...] += part

        @pl.when(k == nk - 1)
        def _():
            finish(acc_ref[...] + part)

    per_half = (M // 2) // tm if halves else 0
    in_specs = [BS((None, tk, tm), lambda i, j, k: (i // per_half, k, i % per_half)) if halves
                else BS((tk, tm), lambda i, j, k: (k, i)) if ta else BS((tm, tk), lambda i, j, k: (i, k)),
                BS((tn, tk), lambda i, j, k: (j, k)) if tb else BS((tk, tn), lambda i, j, k: (k, j))]
    args = [a, b]
    if bias is not None:
        in_specs.append(BS((1, tn), lambda i, j, k: (0, j)))
        args.append(bias.reshape(1, N))
    if add is not None:
        in_specs.append(BS((tm, tn), lambda i, j, k: (i, j)))
        args.append(add)
    if rope is not None:
        in_specs += [BS((tm, LANES), lambda i, j, k: (i, 0))] * 3
        args += list(rope)
    return _pcall(body, name=name, grid=(M // tm, N // tn, nk), in_specs=in_specs,
                  out_specs=BS((tm, tn), lambda i, j, k: (i, j)), out_shape=SDS((M, N), F32),
                  scratch_shapes=[pltpu.VMEM((tm, tn), F32)] if nk > 1 else [],
                  compiler_params=_cparams(("parallel", "parallel", "arbitrary")))(*args)


def _mm_rms_bwd(dy, wb, x, g, add, name, wt=False):
    halves = dy.ndim == 3
    T, Kc = (dy.shape[1], 2 * dy.shape[2]) if halves else dy.shape
    Dm = wb.shape[1] if wt else wb.shape[0]
    tm, tk = _tile(T, 1024), _tile(Kc // 2 if halves else Kc, 1408)
    nk = Kc // tk
    per_half = (Kc // 2) // tk if halves else 0
    dn = (((1,), (0 if wt else 1,)), ((), ()))

    def body(*refs):
        dy_ref, w_ref, x_ref, g_ref = refs[:4]
        dx_ref, dg_ref, acc_ref = refs[-3:]
        i, k = pl.program_id(0), pl.program_id(1)
        part = lax.dot_general(dy_ref[...].astype(BF16), w_ref[...], dn, preferred_element_type=F32)

        @pl.when(k == 0)
        def _():
            acc_ref[...] = part

        @pl.when((k > 0) & (k < nk - 1))
        def _():
            acc_ref[...] += part

        @pl.when((i == 0) & (k == 0))
        def _():
            dg_ref[...] = jnp.zeros_like(dg_ref)

        @pl.when(k == nk - 1)
        def _():
            d = part if nk == 1 else acc_ref[...] + part
            xv = x_ref[...]
            r = lax.rsqrt(jnp.mean(xv * xv, axis=-1, keepdims=True) + NORM_EPS)
            xh = xv * r
            dyg = d * g_ref[...]
            dx = r * (dyg - xh * jnp.mean(dyg * xh, axis=-1, keepdims=True))
            dx_ref[...] = dx if add is None else dx + refs[4][...]
            dg_ref[...] += jnp.sum(d * xh, axis=0, keepdims=True)

    row = BS((tm, Dm), lambda i, k: (i, 0))
    vec = BS((1, Dm), lambda i, k: (0, 0))
    extra = [] if add is None else [add]
    return _pcall(body, name=name, grid=(T // tm, nk),
                  in_specs=[BS((None, tm, tk), lambda i, k: (k // per_half, i, k % per_half)) if halves
                            else BS((tm, tk), lambda i, k: (i, k)),
                            BS((tk, Dm), lambda i, k: (k, 0)) if wt else BS((Dm, tk), lambda i, k: (0, k)), row, vec]
                  + [row] * len(extra),
                  out_specs=[row, vec], out_shape=[SDS((T, Dm), F32), SDS((1, Dm), F32)],
                  scratch_shapes=[pltpu.VMEM((tm, Dm), F32)],
                  compiler_params=_cparams(("arbitrary", "arbitrary")))(dy, wb, x, g.reshape(1, Dm), *extra)


def _colsum(x, name):
    T, N = x.shape
    tt = _tile(T, 1024, 8)

    def body(x_ref, o_ref):
        @pl.when(pl.program_id(0) == 0)
        def _():
            o_ref[...] = jnp.zeros_like(o_ref)

        o_ref[...] += jnp.sum(x_ref[...], axis=0, keepdims=True)

    return _pcall(body, name=name, grid=(T // tt,), in_specs=[BS((tt, N), lambda i: (i, 0))],
                  out_specs=BS((1, N), lambda i: (0, 0)), out_shape=SDS((1, N), F32),
                  compiler_params=_cparams(("arbitrary",)))(x)


@functools.partial(jax.custom_vjp, nondiff_argnums=(5,))
def linear(a, wb, wc, bias, add, name):
    return _linear_fwd(a, wb, wc, bias, add, name)[0]


def _linear_fwd(a, wb, wc, bias, add, name):
    out = _mm(a, wb, bias=bias, add=add, name=name, tm=1024, tn=1024, tk=1024)
    return out, (a, wb, bias is not None, add is not None)


def _linear_bwd(name, res, g):
    a, wb, has_bias, has_add = res
    da = _mm(g, wb, tb=True, name=name + "_da", tm=1024, tn=1024, tk=1024)
    dw = _mm(a, g, ta=True, name=name + "_dw", tm=1024, tn=1024, tk=1024)
    dbias = _colsum(g, name + "_db").reshape(-1) if has_bias else None
    return da, jnp.zeros_like(wb), dw, dbias, (g if has_add else None)


linear.defvjp(_linear_fwd, _linear_bwd)


def _rms_fwd_call(x, g, name, out_dtype=F32):
    T, Dm = x.shape
    tt = _tile(T, 2048, 16)

    def body(x_ref, g_ref, o_ref):
        xv = x_ref[...]
        r = lax.rsqrt(jnp.mean(xv * xv, axis=-1, keepdims=True) + NORM_EPS)
        o_ref[...] = ((xv * r) * g_ref[...]).astype(out_dtype)

    return _pcall(body, name=name, grid=(T // tt,),
                  in_specs=[BS((tt, Dm), lambda i: (i, 0)), BS((1, Dm), lambda i: (0, 0))],
                  out_specs=BS((tt, Dm), lambda i: (i, 0)), out_shape=SDS((T, Dm), out_dtype),
                  compiler_params=_cparams(("parallel",)))(x, g.reshape(1, Dm))


@functools.partial(jax.custom_vjp, nondiff_argnums=(5, 6))
def norm_linear(x, g, wb, wc, bias, wt, name):
    return _norm_linear_fwd(x, g, wb, wc, bias, wt, name)[0]


def _norm_linear_fwd(x, g, wb, wc, bias, wt, name):
    hn = _rms_fwd_call(x, g, name + "_norm", BF16)
    return (_mm(hn, wb, tb=wt, bias=bias, name=name, tm=1024, tn=1408, tk=1024), x), (x, g, hn, wb, bias is not None)


def _norm_linear_bwd(wt, name, res, cts):
    x, g, hn, wb, has_bias = res
    dy, dres = cts
    if wt:
        dw = _mm(dy, hn, ta=True, name=name + "_dw", tm=1408, tn=1024, tk=2048)
    else:
        dw = _mm(hn, dy, ta=True, name=name + "_dw", tm=1024, tn=1408, tk=1024)
    dx, dg = _mm_rms_bwd(dy, wb, x, g, dres, name + "_da", wt)
    dbias = _colsum(dy, name + "_db").reshape(-1) if has_bias else None
    return dx, dg.reshape(g.shape), jnp.zeros_like(wb), dw, dbias


norm_linear.defvjp(_norm_linear_fwd, _norm_linear_bwd)


def _rope_tables(T):
    half = ROT_DIM // 2
    inv = ROPE_THETA ** (-jnp.arange(0, ROT_DIM, 2, dtype=F32) / ROT_DIM)
    lane = jnp.arange(LANES) % HEAD_DIM
    freq = jnp.where(lane < ROT_DIM, inv[lane % half], 0.0)
    ang = jnp.arange(T, dtype=F32)[:, None] * freq[None, :]
    c, s = jnp.cos(ang), jnp.sin(ang)
    sa = jnp.where((lane >= half) & (lane < ROT_DIM), s, 0.0)
    sb = jnp.where(lane < half, -s, 0.0)
    return c, sa, sb


def _join_pieces_call(pieces, rotated, tabs, name):
    T = pieces[0].shape[0]
    widths = [p.shape[1] for p in pieces]
    W = sum(widths)
    tt = _tile(T, 512, 16)

    def body(*refs):
        tab_refs, o_ref, sum_ref = refs[len(pieces):len(pieces) + 3], refs[-2], refs[-1]

        @pl.when(pl.program_id(0) == 0)
        def _():
            sum_ref[...] = jnp.zeros_like(sum_ref)

        off = 0
        for p_ref, w, r in zip(refs, widths, rotated):
            piece = _rotate(p_ref[...], tab_refs, inverse=True) if r else p_ref[...]
            o_ref[:, off:off + w] = piece.astype(BF16)
            sum_ref[:, off:off + w] += jnp.sum(piece, axis=0, keepdims=True)
            off += w

    return _pcall(body, name=name, grid=(T // tt,),
                  in_specs=[BS((tt, w), lambda i: (i, 0)) for w in widths] + [BS((tt, LANES), lambda i: (i, 0))] * 3,
                  out_specs=[BS((tt, W), lambda i: (i, 0)), BS((1, W), lambda i: (0, 0))],
                  out_shape=[SDS((T, W), BF16), SDS((1, W), F32)],
                  compiler_params=_cparams(("arbitrary",)))(*pieces, *tabs)


@functools.partial(jax.custom_vjp, nondiff_argnums=(6, 7))
def norm_linear_pieces(x, g, wb, wc, bias, tabs, pieces, name):
    return _norm_linear_pieces_fwd(x, g, wb, wc, bias, tabs, pieces, name)[0]


def _norm_linear_pieces_fwd(x, g, wb, wc, bias, tabs, pieces, name):
    hn = _rms_fwd_call(x, g, name + "_norm", BF16)
    outs, row = [], 0
    for i, (width, rotated) in enumerate(pieces):
        outs.append(_mm(hn, wb[row:row + width], tb=True, bias=None if bias is None else bias[row:row + width],
                        rope=tabs if rotated else None, name=f"{name}_{i}", tm=2048, tn=1024, tk=1024))
        row += width
    return (*outs, x), (x, g, hn, wb, tabs, bias is not None)


def _norm_linear_pieces_bwd(pieces, name, res, cts):
    x, g, hn, wb, tabs, has_bias = res
    dy, dy_sum = _join_pieces_call(cts[:-1], [rotated for _, rotated in pieces], tabs, name + "_join")
    dw = _mm(dy, hn, ta=True, name=name + "_dw", tm=1408, tn=1024, tk=2048)
    dx, dg = _mm_rms_bwd(dy, wb, x, g, cts[-1], name + "_da", wt=True)
    dbias = dy_sum.reshape(-1) if has_bias else None
    return dx, dg.reshape(g.shape), jnp.zeros_like(wb), dw, dbias, jax.tree.map(jnp.zeros_like, tabs)


norm_linear_pieces.defvjp(_norm_linear_pieces_fwd, _norm_linear_pieces_bwd)


def _conv_fwd_call(x, w, b, name):
    T, C = x.shape
    tt = _tile(T, 2048, 8)
    per = tt // 8

    def body(x_ref, halo_ref, w_ref, b_ref, o_ref):
        i = pl.program_id(0)
        halo = jnp.where(i > 0, halo_ref[...], 0.0)
        e = jnp.concatenate([halo, x_ref[...]], axis=0)
        acc = b_ref[...]
        for k in (3, 2, 1):
            acc = acc + pltpu.roll(e, k, axis=0)[8:, :] * w_ref[3 - k:4 - k, :]
        o_ref[...] = acc + x_ref[...] * w_ref[3:4, :]

    return _pcall(body, name=name, grid=(T // tt,),
                  in_specs=[BS((tt, C), lambda i: (i, 0)), BS((8, C), lambda i: (jnp.maximum(i * per - 1, 0), 0)),
                            BS((4, C), lambda i: (0, 0)), BS((1, C), lambda i: (0, 0))],
                  out_specs=BS((tt, C), lambda i: (i, 0)), out_shape=SDS((T, C), F32),
                  compiler_params=_cparams(("parallel",)))(x, x, w, b.reshape(1, C))


def _conv_bwd_call(x, w, dy, name):
    T, C = x.shape
    tt = _tile(T, 1024, 8)
    per = tt // 8
    nt = T // tt

    def body(x_ref, halo_ref, w_ref, dy_ref, nxt_ref, dx_ref, dwb_ref):
        i = pl.program_id(0)
        halo = jnp.where(i > 0, halo_ref[...], 0.0)
        e = jnp.concatenate([halo, x_ref[...]], axis=0)
        dy = dy_ref[...]
        nxt = jnp.where(i < nt - 1, nxt_ref[...], 0.0)
        f = jnp.concatenate([dy, nxt], axis=0)
        dx = dy * w_ref[3:4, :]
        rows = [None] * 4
        rows[3] = jnp.sum(dy * x_ref[...], axis=0, keepdims=True)
        for k in (1, 2, 3):
            dx = dx + pltpu.roll(f, tt + 8 - k, axis=0)[:tt, :] * w_ref[3 - k:4 - k, :]
            rows[3 - k] = jnp.sum(dy * pltpu.roll(e, k, axis=0)[8:, :], axis=0, keepdims=True)
        dx_ref[...] = dx
        upd = jnp.concatenate(rows + [jnp.sum(dy, axis=0, keepdims=True), jnp.zeros((3, C), F32)], axis=0)

        @pl.when(i == 0)
        def _():
            dwb_ref[...] = jnp.zeros_like(dwb_ref)

        dwb_ref[...] += upd

    row = BS((tt, C), lambda i: (i, 0))
    return _pcall(body, name=name, grid=(nt,),
                  in_specs=[row, BS((8, C), lambda i: (jnp.maximum(i * per - 1, 0), 0)), BS((4, C), lambda i: (0, 0)),
                            row, BS((8, C), lambda i: (jnp.minimum((i + 1) * per, T // 8 - 1), 0))],
                  out_specs=[row, BS((8, C), lambda i: (0, 0))],
                  out_shape=[SDS((T, C), F32), SDS((8, C), F32)],
                  compiler_params=_cparams(("arbitrary",)))(x, x, w, dy, dy)


@functools.partial(jax.custom_vjp, nondiff_argnums=(3,))
def conv4(x, w, b, name):
    return _conv_fwd_call(x, w, b, name)


def _conv4_fwd(x, w, b, name):
    return _conv_fwd_call(x, w, b, name), (x, w)


def _conv4_bwd(name, res, dy):
    x, w = res
    dx, dwb = _conv_bwd_call(x, w, dy, name + "_bwd")
    return dx, dwb[0:4], dwb[4]


conv4.defvjp(_conv4_fwd, _conv4_bwd)


def _gates_fwd_call(xc, wa, ba, wx, bx, name):
    T, C = xc.shape
    hd = C // LRU_HEADS
    tt = _tile(T, 2048, 8)

    def body(x_ref, wa_ref, ba_ref, wx_ref, bx_ref, ga_ref, gx_ref):
        xb = x_ref[...].astype(BF16)
        ga_ref[...] = jnp.dot(xb, wa_ref[0].astype(BF16), preferred_element_type=F32) + ba_ref[...]
        gx_ref[...] = jnp.dot(xb, wx_ref[0].astype(BF16), preferred_element_type=F32) + bx_ref[...]

    blk = BS((tt, hd), lambda i, h: (i, h))
    wsp = BS((1, hd, hd), lambda i, h: (h, 0, 0))
    bsp = BS((1, hd), lambda i, h: (0, h))
    return _pcall(body, name=name, grid=(T // tt, LRU_HEADS), in_specs=[blk, wsp, bsp, wsp, bsp],
                  out_specs=[blk, blk], out_shape=[SDS((T, C), F32)] * 2,
                  compiler_params=_cparams(("parallel", "parallel")))(xc, wa, ba.reshape(1, C), wx, bx.reshape(1, C))


def _gates_bwd_x_call(dga, dgx, wa, wx, name):
    T, C = dga.shape
    hd = C // LRU_HEADS
    tt = _tile(T, 2048, 8)
    dn = (((1,), (1,)), ((), ()))

    def body(da_ref, dx_ref, wa_ref, wx_ref, o_ref):
        o_ref[...] = (lax.dot_general(da_ref[...].astype(BF16), wa_ref[0].astype(BF16), dn, preferred_element_type=F32)
                      + lax.dot_general(dx_ref[...].astype(BF16), wx_ref[0].astype(BF16), dn, preferred_element_type=F32))

    blk = BS((tt, hd), lambda i, h: (i, h))
    wsp = BS((1, hd, hd), lambda i, h: (h, 0, 0))
    return _pcall(body, name=name, grid=(T // tt, LRU_HEADS), in_specs=[blk, blk, wsp, wsp], out_specs=blk,
                  out_shape=SDS((T, C), F32), compiler_params=_cparams(("parallel", "parallel")))(dga, dgx, wa, wx)


def _gates_bwd_w_call(xc, dga, dgx, name):
    T, C = xc.shape
    hd = C // LRU_HEADS
    tt = _tile(T, 2048, 8)
    dn = (((0,), (0,)), ((), ()))

    def body(x_ref, da_ref, dx_ref, dwa_ref, dwx_ref, dba_ref, dbx_ref):
        @pl.when(pl.program_id(1) == 0)
        def _():
            dwa_ref[...] = jnp.zeros_like(dwa_ref)
            dwx_ref[...] = jnp.zeros_like(dwx_ref)
            dba_ref[...] = jnp.zeros_like(dba_ref)
            dbx_ref[...] = jnp.zeros_like(dbx_ref)

        xb = x_ref[...].astype(BF16)
        da, dx = da_ref[...], dx_ref[...]
        dwa_ref[0] += lax.dot_general(xb, da.astype(BF16), dn, preferred_element_type=F32)
        dwx_ref[0] += lax.dot_general(xb, dx.astype(BF16), dn, preferred_element_type=F32)
        dba_ref[...] += jnp.sum(da, axis=0, keepdims=True)
        dbx_ref[...] += jnp.sum(dx, axis=0, keepdims=True)

    blk = BS((tt, hd), lambda h, i: (i, h))
    wsp = BS((1, hd, hd), lambda h, i: (h, 0, 0))
    bsp = BS((1, hd), lambda h, i: (0, h))
    return _pcall(body, name=name, grid=(LRU_HEADS, T // tt), in_specs=[blk, blk, blk],
                  out_specs=[wsp, wsp, bsp, bsp],
                  out_shape=[SDS((LRU_HEADS, hd, hd), F32)] * 2 + [SDS((1, C), F32)] * 2,
                  compiler_params=_cparams(("parallel", "arbitrary")))(xc, dga, dgx)


@functools.partial(jax.custom_vjp, nondiff_argnums=(7,))
def lru_gates(xc, wa, wa_c, ba, wx, wx_c, bx, name):
    return tuple(_gates_fwd_call(xc, wa, ba, wx, bx, name))


def _lru_gates_fwd(xc, wa, wa_c, ba, wx, wx_c, bx, name):
    return tuple(_gates_fwd_call(xc, wa, ba, wx, bx, name)), (xc, wa, wx, ba.shape)


def _lru_gates_bwd(name, res, g):
    xc, wa, wx, bshape = res
    dga, dgx = g
    dxc = _gates_bwd_x_call(dga, dgx, wa, wx, name + "_dx")
    dwa, dwx, dba, dbx = _gates_bwd_w_call(xc, dga, dgx, name + "_dw")
    return dxc, jnp.zeros_like(wa), dwa, dba.reshape(bshape), jnp.zeros_like(wx), dwx, dbx.reshape(bshape)


lru_gates.defvjp(_lru_gates_fwd, _lru_gates_bwd)


def _lru_coeffs(xc, ga, gx, lam):
    r = _sigmoid(ga)
    ig = _sigmoid(gx)
    z = -lam
    sp = jnp.maximum(z, 0.0) + jnp.log(1.0 + jnp.exp(-jnp.abs(z)))
    la = -LRU_C * r * sp
    a = jnp.exp(la)
    s = jnp.sqrt(-_expm1(2.0 * la))
    return r, ig, sp, a, s


LRU_TT = 256


def _scan_fwd_call(xc, ga, gx, y, lam, name):
    T, C = xc.shape
    tt = _tile(T, 2 * LRU_TT, 8)

    def body(xc_ref, ga_ref, gx_ref, y_ref, lam_ref, h_ref, rec_ref, a_buf, carry):
        @pl.when(pl.program_id(0) == 0)
        def _():
            carry[...] = jnp.zeros_like(carry)

        xcv = xc_ref[...]
        _, ig, _, a, s = _lru_coeffs(xcv, ga_ref[...], gx_ref[...], lam_ref[...])
        a_buf[...] = a
        h_ref[...] = s * (ig * xcv)

        def step(t, h):
            hn = a_buf[pl.ds(t, 1), :] * h + h_ref[pl.ds(t, 1), :]
            h_ref[pl.ds(t, 1), :] = hn
            return hn

        carry[0:1, :] = lax.fori_loop(0, tt, step, carry[0:1, :], unroll=8)
        g, _ = _gelu_parts(y_ref[...])
        rec_ref[...] = h_ref[...] * g

    row = BS((tt, C), lambda i: (i, 0))
    vec = BS((1, C), lambda i: (0, 0))
    return _pcall(body, name=name, grid=(T // tt,), in_specs=[row, row, row, row, vec], out_specs=[row, row],
                  out_shape=[SDS((T, C), F32)] * 2,
                  scratch_shapes=[pltpu.VMEM((tt, C), F32), pltpu.VMEM((8, C), F32)],
                  compiler_params=_cparams(("arbitrary",)))(xc, ga, gx, y, lam.reshape(1, C))


def _scan_bwd_call(xc, ga, gx, y, lam, h, drec, name):
    T, C = xc.shape
    tt = _tile(T, LRU_TT, 8)
    nt = T // tt
    per = tt // 8

    def body(xc_ref, ga_ref, gx_ref, y_ref, lam_ref, h_ref, halo_ref, dr_ref,
             dga_ref, dgx_ref, dxc_ref, dy_ref, dlam_ref, a_buf, g_buf, carry):
        i = pl.program_id(0)

        @pl.when(i == 0)
        def _():
            carry[...] = jnp.zeros_like(carry)
            dlam_ref[...] = jnp.zeros_like(dlam_ref)

        xcv, lam = xc_ref[...], lam_ref[...]
        r, ig, sp, a, s = _lru_coeffs(xcv, ga_ref[...], gx_ref[...], lam)
        gel, dgel = _gelu_parts(y_ref[...])
        drec = dr_ref[...]
        hv = h_ref[...]
        dy_ref[...] = drec * hv * dgel
        a_buf[...] = a
        g_buf[...] = drec * gel

        def step(j, q):
            t = tt - 1 - j
            g = g_buf[pl.ds(t, 1), :] + q
            g_buf[pl.ds(t, 1), :] = g
            return a_buf[pl.ds(t, 1), :] * g

        carry[0:1, :] = lax.fori_loop(0, tt, step, carry[0:1, :], unroll=8)
        g = g_buf[...]
        halo = jnp.where(i < nt - 1, halo_ref[...], 0.0)
        hprev = pltpu.roll(jnp.concatenate([halo, hv], axis=0), 1, axis=0)[8:, :]
        da = g * hprev
        dig = g * s * xcv
        ds = g * ig * xcv
        dla = da * a - ds * (a * a) / s
        dga_ref[...] = dla * (-LRU_C * sp) * r * (1.0 - r)
        dgx_ref[...] = dig * ig * (1.0 - ig)
        dxc_ref[...] = g * s * ig
        dlam_ref[...] += jnp.sum(dla * r, axis=0, keepdims=True) * (LRU_C * _sigmoid(-lam))

    row = BS((tt, C), lambda i: (nt - 1 - i, 0))
    vec = BS((1, C), lambda i: (0, 0))
    halo = BS((8, C), lambda i: (jnp.maximum((nt - 1 - i) * per - 1, 0), 0))
    return _pcall(body, name=name, grid=(nt,), in_specs=[row, row, row, row, vec, row, halo, row],
                  out_specs=[row, row, row, row, vec], out_shape=[SDS((T, C), F32)] * 4 + [SDS((1, C), F32)],
                  scratch_shapes=[pltpu.VMEM((tt, C), F32), pltpu.VMEM((tt, C), F32), pltpu.VMEM((8, C), F32)],
                  compiler_params=_cparams(("arbitrary",)))(xc, ga, gx, y, lam.reshape(1, C), h, h, drec)


@functools.partial(jax.custom_vjp, nondiff_argnums=(5,))
def lru_scan(xc, ga, gx, y, lam, name):
    return _scan_fwd_call(xc, ga, gx, y, lam, name)[1]


def _lru_scan_fwd(xc, ga, gx, y, lam, name):
    h, rec = _scan_fwd_call(xc, ga, gx, y, lam, name)
    return rec, (xc, ga, gx, y, lam, h)


def _lru_scan_bwd(name, res, drec):
    xc, ga, gx, y, lam, h = res
    dga, dgx, dxc, dy, dlam = _scan_bwd_call(xc, ga, gx, y, lam, h, drec, name + "_bwd")
    return dxc, dga, dgx, dy, dlam.reshape(lam.shape)


lru_scan.defvjp(_lru_scan_fwd, _lru_scan_bwd)


def _att_batch(d, shared=False):
    if shared:
        return 8, 1
    return (4, 1) if d == 1 else (1, min(d, 8))


def _att_masks(n, max_dist):
    qi = lax.broadcasted_iota(jnp.int32, (1, 2 * ATT_BLOCK, 2 * ATT_BLOCK), 1) & (ATT_BLOCK - 1)
    kj = lax.broadcasted_iota(jnp.int32, (1, 2 * ATT_BLOCK, 2 * ATT_BLOCK), 2)
    prev = (kj < ATT_BLOCK) & (kj >= qi + (ATT_BLOCK - max_dist)) & (n > 0)
    cur = (kj >= ATT_BLOCK) & (kj - ATT_BLOCK <= qi)
    return prev | cur


def _lo_lanes(rows):
    return lax.broadcasted_iota(jnp.int32, (rows, LANES), 1) < HEAD_DIM


def _lane_half(rows):
    return lax.broadcasted_iota(jnp.int32, (rows, LANES), 1) // HEAD_DIM


def _stack_heads(x2):
    lo = _lo_lanes(ATT_BLOCK)
    zero = jnp.zeros_like(x2)
    return jnp.concatenate([jnp.where(lo, x2, zero), jnp.where(lo, zero, x2)], axis=0)


def _unstack_heads(y):
    return jnp.where(_lo_lanes(ATT_BLOCK), y[:ATT_BLOCK], y[ATT_BLOCK:])


def _per_head_col(x2):
    return jnp.concatenate([x2[:, 0:1], x2[:, HEAD_DIM:HEAD_DIM + 1]], axis=0)


def _head_sums(x2):
    lo = _lo_lanes(ATT_BLOCK)
    return jnp.concatenate([jnp.sum(jnp.where(lo, x2, 0.0), axis=1, keepdims=True),
                            jnp.sum(jnp.where(lo, 0.0, x2), axis=1, keepdims=True)], axis=0)


def _att_specs(d, Wq, Wk, nb, clamp):
    shared = Wk != Wq
    cgw, sb = _att_batch(d, shared)
    cur = (lambda n: jnp.minimum(n, nb - 1)) if clamp else (lambda n: n)
    rows, qw, kw = ATT_BLOCK * d, cgw * LANES, (LANES if shared else cgw * LANES)
    kcol = (lambda g: 0) if shared else (lambda g: g)
    qsp = BS((rows, qw), lambda g, n: (cur(n), g))
    csp = BS((rows, kw), lambda g, n: (cur(n), kcol(g)))
    psp = BS((rows, kw), lambda g, n: (jnp.maximum(cur(n) - 1, 0), kcol(g)))
    return cgw, sb, shared, qsp, csp, psp, qw, kw


def _att_streams(d, sb, work):
    if d == 1:
        work([slice(None)])
        return

    def one(j, carry):
        work([pl.ds(j * sb + i, ATT_BLOCK, stride=d) for i in range(sb)])
        return carry

    lax.fori_loop(0, d // sb, one, 0)


def _att_problem_loads(rows, cgw, shared, g, q_ref, kc_ref, kp_ref, vc_ref, vp_ref, sk_ref):
    half = _lane_half(ATT_BLOCK)

    def kv(ref, r, p):
        x = ref[r, :]
        if not shared:
            return x[:, p * LANES:(p + 1) * LANES]
        return jnp.where(half == p // 4, x, pltpu.roll(x, HEAD_DIM, axis=1))

    qs, kb, vb, sk = [], [], [], []
    for r in rows:
        qrow = q_ref[r, :]
        for p in range(cgw):
            cols = slice(p * LANES, (p + 1) * LANES)
            qs.append(_stack_heads(qrow[:, cols].astype(BF16)))
            kb.append(jnp.concatenate([kv(kp_ref, r, p), kv(kc_ref, r, p)], axis=0).astype(BF16))
            vb.append(jnp.concatenate([kv(vp_ref, r, p), kv(vc_ref, r, p)], axis=0).astype(BF16))
            sk.append(_per_head_col(jnp.broadcast_to(sk_ref[:, cols], (ATT_BLOCK, LANES))))
    return jnp.stack(qs), jnp.stack(kb), jnp.stack(vb), jnp.stack(sk)


_BDOT_NT = (((2,), (2,)), ((0,), (0,)))
_BDOT_NN = (((2,), (1,)), ((0,), (0,)))
_BDOT_TN = (((1,), (1,)), ((0,), (0,)))


def _att_fwd_call(q, k, v, sinks, d, max_dist, name):
    T, Wq = q.shape
    Wk = k.shape[1]
    nb = T // (d * ATT_BLOCK)
    cgw, sb, shared, qsp, csp, psp, qw, kw = _att_specs(d, Wq, Wk, nb, False)
    G = Wq // qw
    assert not shared or (d == 1 and Wk == LANES and G == 1 and cgw == 8), "a shared kv pair serves 2 x 8 query heads"

    def body(q_ref, kc_ref, kp_ref, vc_ref, vp_ref, sk_ref, o_ref, lse_ref):
        g, n = pl.program_id(0), pl.program_id(1)

        def work(rows):
            qs, kband, vband, sk = _att_problem_loads(rows, cgw, shared, g, q_ref, kc_ref, kp_ref, vc_ref, vp_ref, sk_ref)
            s = lax.dot_general(qs, kband, _BDOT_NT, preferred_element_type=F32) * (HEAD_DIM ** -0.5)
            s = jnp.where(_att_masks(n, max_dist), s, NEG)
            m = jnp.maximum(jnp.max(s, axis=-1, keepdims=True), sk)
            e = jnp.exp(s - m)
            den = jnp.sum(e, axis=-1, keepdims=True) + jnp.exp(sk - m)
            o = lax.dot_general((e * (1.0 / den)).astype(BF16), vband, _BDOT_NN, preferred_element_type=F32)
            lse = jnp.broadcast_to(m + jnp.log(den), o.shape)
            for i, r in enumerate(rows):
                o_ref[r, :] = jnp.concatenate([_unstack_heads(o[i * cgw + p]) for p in range(cgw)], axis=1)
                lse_ref[r, :] = jnp.concatenate([_unstack_heads(lse[i * cgw + p]) for p in range(cgw)], axis=1)

        _att_streams(d, sb, work)

    sksp = BS((1, qw), lambda g, n: (0, g))
    return _pcall(body, name=name, grid=(G, nb), in_specs=[qsp, csp, psp, csp, psp, sksp], out_specs=[qsp, qsp],
                  out_shape=[SDS((T, Wq), F32)] * 2,
                  compiler_params=_cparams(("parallel", "parallel")))(q, k, k, v, v, sinks)


def _att_bwd_call(q, k, v, sinks, o, lse, do, dlse, d, max_dist, name):
    T, Wq = q.shape
    Wk = k.shape[1]
    nb = T // (d * ATT_BLOCK)
    cgw, sb, shared, qsp, csp, psp, qw, kw = _att_specs(d, Wq, Wk, nb, True)
    G = Wq // qw
    scale = HEAD_DIM ** -0.5

    def body(*refs):
        q_ref, kc_ref, kp_ref, vc_ref, vp_ref, sk_ref, o_ref, lse_ref, do_ref = refs[:9]
        dlse_ref = refs[9] if dlse is not None else None
        dq_ref, dk_ref, dv_ref, dsk_ref, ck, cv = refs[-6:]
        g, n = pl.program_id(0), pl.program_id(1)

        @pl.when(n == 0)
        def _():
            ck[...] = jnp.zeros_like(ck)
            cv[...] = jnp.zeros_like(cv)
            dsk_ref[...] = jnp.zeros_like(dsk_ref)

        def work(rows):
            qs, kband, vband, sk = _att_problem_loads(rows, cgw, shared, g, q_ref, kc_ref, kp_ref, vc_ref, vp_ref, sk_ref)
            dos, lse_c, corr = [], [], []
            for r in rows:
                do_r, o_r, lse_r = do_ref[r, :], o_ref[r, :], lse_ref[r, :]
                dlse_r = dlse_ref[r, :] if dlse_ref is not None else None
                for p in range(cgw):
                    cols = slice(p * LANES, (p + 1) * LANES)
                    dos.append(_stack_heads(do_r[:, cols].astype(BF16)))
                    lse_c.append(_per_head_col(lse_r[:, cols]))
                    delta = _head_sums(do_r[:, cols] * o_r[:, cols])
                    corr.append(-delta if dlse_r is None else _head_sums(dlse_r[:, cols]) - delta)
            dos, lse_c, corr = jnp.stack(dos), jnp.stack(lse_c), jnp.stack(corr)
            s = lax.dot_general(qs, kband, _BDOT_NT, preferred_element_type=F32) * scale
            pr = jnp.exp(jnp.where(_att_masks(n, max_dist), s, NEG) - lse_c)
            dp = lax.dot_general(dos, vband, _BDOT_NT, preferred_element_type=F32)
            ds = (pr * (dp + corr)).astype(BF16)
            dq = lax.dot_general(ds, kband, _BDOT_NN, preferred_element_type=F32) * scale
            dkb = lax.dot_general(ds, qs, _BDOT_TN, preferred_element_type=F32) * scale
            dvb = lax.dot_general(pr.astype(BF16), dos, _BDOT_TN, preferred_element_type=F32)
            dsk = jnp.exp(sk - lse_c) * corr
            lane = lax.broadcasted_iota(jnp.int32, (8, LANES), 1)
            for p in range(cgw):
                tot = [jnp.sum(jnp.stack([dsk[i * cgw + p, h * ATT_BLOCK:(h + 1) * ATT_BLOCK] for i in range(len(rows))]),
                               axis=(0, 1)).reshape(1, 1) for h in range(2)]
                dsk_ref[:, p * LANES:(p + 1) * LANES] += jnp.where(lane == 0, tot[0], jnp.where(lane == HEAD_DIM, tot[1], 0.0))

            def gather_pairs(parts):
                if not shared:
                    return jnp.concatenate(parts, axis=1)
                tot = [parts[4 * h] + parts[4 * h + 1] + parts[4 * h + 2] + parts[4 * h + 3] for h in range(2)]
                tot = [t + pltpu.roll(t, HEAD_DIM, axis=1) for t in tot]
                return jnp.where(_lo_lanes(ATT_BLOCK), tot[0], tot[1])

            for i, r in enumerate(rows):
                mine = range(i * cgw, (i + 1) * cgw)
                dq_ref[r, :] = jnp.concatenate([_unstack_heads(dq[b]) for b in mine], axis=1)
                dk_ref[r, :] = ck[r, :] + gather_pairs([dkb[b, :ATT_BLOCK] for b in mine])
                dv_ref[r, :] = cv[r, :] + gather_pairs([dvb[b, :ATT_BLOCK] for b in mine])
                ck[r, :] = gather_pairs([dkb[b, ATT_BLOCK:] for b in mine])
                cv[r, :] = gather_pairs([dvb[b, ATT_BLOCK:] for b in mine])

        @pl.when(n < nb)
        def _():
            _att_streams(d, sb, work)

        @pl.when(n == nb)
        def _():
            dk_ref[...] = ck[...]
            dv_ref[...] = cv[...]

    sksp = BS((1, qw), lambda g, n: (0, g))
    rows = ATT_BLOCK * d
    osp = BS((rows, kw), lambda g, n: (jnp.maximum(n - 1, 0), 0 if shared else g))
    kshape = SDS((T, Wk), F32)
    dq, dk, dv, dsk = _pcall(
        body, name=name, grid=(G, nb + 1),
        in_specs=[qsp, csp, psp, csp, psp, sksp, qsp, qsp, qsp] + ([qsp] if dlse is not None else []),
        out_specs=[qsp, osp, osp, BS((8, qw), lambda g, n: (0, g))],
        out_shape=[SDS((T, Wq), F32), kshape, kshape, SDS((8, Wq), F32)],
        scratch_shapes=[pltpu.VMEM((rows, kw), F32)] * 2,
        compiler_params=_cparams(("parallel", "arbitrary")))(q, k, k, v, v, sinks, o, lse, do,
                                                              *([dlse] if dlse is not None else []))
    return dq, dk, dv, dsk[0:1]


@functools.partial(jax.custom_vjp, nondiff_argnums=(4, 5, 6, 7))
def band_attention(q, k, v, sinks, d, max_dist, with_lse, name):
    return _band_attention_fwd(q, k, v, sinks, d, max_dist, with_lse, name)[0]


def _band_attention_fwd(q, k, v, sinks, d, max_dist, with_lse, name):
    o, lse = _att_fwd_call(q, k, v, sinks, d, max_dist, name)
    return ((o, lse) if with_lse else o), (q, k, v, sinks, o, lse)


def _band_attention_bwd(d, max_dist, with_lse, name, res, g):
    q, k, v, sinks, o, lse = res
    do, dlse = g if with_lse else (g, None)
    return _att_bwd_call(q, k, v, sinks, o, lse, do, dlse, d, max_dist, name + "_bwd")


band_attention.defvjp(_band_attention_fwd, _band_attention_bwd)


def _merge_weights(ls):
    mx = jnp.maximum(jnp.maximum(ls[0], ls[1]), ls[2])
    es = [jnp.exp(l - mx) for l in ls]
    inv = 1.0 / (es[0] + es[1] + es[2])
    return [e * inv for e in es]


def _merge_fwd_call(os_, ls_, name):
    T, W = os_[0].shape
    tt = _tile(T, 1024, 8)

    def body(o1, o2, o3, l1, l2, l3, out_ref):
        w = _merge_weights([l1[...], l2[...], l3[...]])
        out_ref[...] = w[0] * o1[...] + w[1] * o2[...] + w[2] * o3[...]

    row = BS((tt, W), lambda i: (i, 0))
    return _pcall(body, name=name, grid=(T // tt,), in_specs=[row] * 6, out_specs=row,
                  out_shape=SDS((T, W), F32), compiler_params=_cparams(("parallel",)))(*os_, *ls_)


def _merge_bwd_call(os_, ls_, do, name):
    T, W = os_[0].shape
    tt = _tile(T, 512, 8)

    def body(o1, o2, o3, l1, l2, l3, do_ref, d1, d2, d3, e1, e2, e3):
        w = _merge_weights([l1[...], l2[...], l3[...]])
        dov = do_ref[...]
        ts = [dov * o[...] for o in (o1, o2, o3)]
        mean = w[0] * ts[0] + w[1] * ts[1] + w[2] * ts[2]
        for wi, ti, dref, eref in zip(w, ts, (d1, d2, d3), (e1, e2, e3)):
            dref[...] = wi * dov
            eref[...] = wi * (ti - mean)

    row = BS((tt, W), lambda i: (i, 0))
    return _pcall(body, name=name, grid=(T // tt,), in_specs=[row] * 7, out_specs=[row] * 6,
                  out_shape=[SDS((T, W), F32)] * 6, compiler_params=_cparams(("parallel",)))(*os_, *ls_, do)


@functools.partial(jax.custom_vjp, nondiff_argnums=(2,))
def merge3(os_, ls_, name):
    return _merge_fwd_call(os_, ls_, name)


def _merge3_fwd(os_, ls_, name):
    return _merge_fwd_call(os_, ls_, name), (os_, ls_)


def _merge3_bwd(name, res, do):
    os_, ls_ = res
    out = _merge_bwd_call(os_, ls_, do, name + "_bwd")
    return tuple(out[:3]), tuple(out[3:])


merge3.defvjp(_merge3_fwd, _merge3_bwd)


def _xa_probs(qb, kb, scale):
    s = lax.dot_general(qb, kb, (((1,), (1,)), ((), ())), preferred_element_type=F32) * scale
    e = jnp.exp(s - jnp.max(s, axis=-1, keepdims=True))
    return e / jnp.sum(e, axis=-1, keepdims=True)


def _xa_fwd_call(q, kv, name):
    T, W = q.shape
    M = kv.shape[0]
    hd = XA_HEAD_DIM
    tq = _tile(T, 4096, 8)
    scale = hd ** -0.5

    def body(q_ref, k_ref, v_ref, o_ref):
        p = _xa_probs(q_ref[...].astype(BF16), k_ref[...].astype(BF16), scale)
        o_ref[...] = jnp.dot(p.astype(BF16), v_ref[...].astype(BF16), preferred_element_type=F32)

    qsp = BS((tq, hd), lambda i, h: (i, h))
    return _pcall(body, name=name, grid=(T // tq, XA_HEADS),
                  in_specs=[qsp, BS((M, hd), lambda i, h: (0, h)), BS((M, hd), lambda i, h: (0, XA_HEADS + h))],
                  out_specs=qsp, out_shape=SDS((T, W), F32),
                  compiler_params=_cparams(("parallel", "parallel")))(q, kv, kv)


def _xa_bwd_call(q, kv, do, name):
    T, W = q.shape
    M = kv.shape[0]
    hd = XA_HEAD_DIM
    tq = _tile(T, 4096, 8)
    scale = hd ** -0.5
    dn_nt = (((1,), (1,)), ((), ()))
    dn_tn = (((0,), (0,)), ((), ()))

    def body(q_ref, k_ref, v_ref, do_ref, dq_ref, dk_ref, dv_ref):
        @pl.when(pl.program_id(1) == 0)
        def _():
            dk_ref[...] = jnp.zeros_like(dk_ref)
            dv_ref[...] = jnp.zeros_like(dv_ref)

        qb, kb, vb = q_ref[...].astype(BF16), k_ref[...].astype(BF16), v_ref[...].astype(BF16)
        p = _xa_probs(qb, kb, scale)
        dob = do_ref[...].astype(BF16)
        dp = lax.dot_general(dob, vb, dn_nt, preferred_element_type=F32)
        ds = (p * (dp - jnp.sum(p * dp, axis=-1, keepdims=True))).astype(BF16)
        dq_ref[...] = jnp.dot(ds, kb, preferred_element_type=F32) * scale
        dk_ref[...] += lax.dot_general(ds, qb, dn_tn, preferred_element_type=F32) * scale
        dv_ref[...] += lax.dot_general(p.astype(BF16), dob, dn_tn, preferred_element_type=F32)

    qsp = BS((tq, hd), lambda h, i: (i, h))
    ksp = BS((M, hd), lambda h, i: (0, h))
    return _pcall(body, name=name, grid=(XA_HEADS, T // tq),
                  in_specs=[qsp, ksp, BS((M, hd), lambda h, i: (0, XA_HEADS + h)), qsp],
                  out_specs=[qsp, ksp, ksp], out_shape=[SDS((T, W), F32), SDS((M, W), F32), SDS((M, W), F32)],
                  compiler_params=_cparams(("parallel", "arbitrary")))(q, kv, kv, do)


@functools.partial(jax.custom_vjp, nondiff_argnums=(2,))
def cross_attention(q, kv, name):
    return _xa_fwd_call(q, kv, name)


def _cross_attention_fwd(q, kv, name):
    return _xa_fwd_call(q, kv, name), (q, kv)


def _cross_attention_bwd(name, res, do):
    q, kv = res
    dq, dk, dv = _xa_bwd_call(q, kv, do, name + "_bwd")
    return dq, jnp.concatenate([dk, dv], axis=1)


cross_attention.defvjp(_cross_attention_fwd, _cross_attention_bwd)


def _gate_up_swiglu_call(hn, w1t, name):
    T, K = hn.shape
    F = w1t.shape[0] // 2
    tm, tn = _tile(T, 4096), _tile(F, 256)
    nj = F // tn
    dn = (((1,), (1,)), ((), ()))

    def body(a_ref, wg_ref, wu_ref, g_ref, u_ref, act_ref):
        a = a_ref[...]
        g = lax.dot_general(a, wg_ref[...], dn, preferred_element_type=F32)
        u = lax.dot_general(a, wu_ref[...], dn, preferred_element_type=F32)
        g_ref[...] = g
        u_ref[...] = u
        act_ref[...] = ((g * _sigmoid(g)) * u).astype(BF16)

    tile = BS((tm, tn), lambda i, j: (i, j))
    return _pcall(body, name=name, grid=(T // tm, nj),
                  in_specs=[BS((tm, K), lambda i, j: (i, 0)), BS((tn, K), lambda i, j: (j, 0)),
                            BS((tn, K), lambda i, j: (j + nj, 0))],
                  out_specs=[tile, tile, tile], out_shape=[SDS((T, F), F32), SDS((T, F), F32), SDS((T, F), BF16)],
                  compiler_params=_cparams(("parallel", "parallel")))(hn, w1t, w1t)


def _down_da_swiglu_bwd_call(dout, w2, g, u, name):
    T, F = g.shape
    Dm = dout.shape[1]
    tm, tn = _tile(T, 2048), _tile(F, 256)

    def body(d_ref, w_ref, g_ref, u_ref, o_ref):
        d = lax.dot_general(d_ref[...].astype(BF16), w_ref[...], (((1,), (1,)), ((), ())), preferred_element_type=F32)
        g = g_ref[...]
        sg = _sigmoid(g)
        o_ref[0] = (d * u_ref[...] * (sg * (1.0 + g * (1.0 - sg)))).astype(BF16)
        o_ref[1] = (d * (g * sg)).astype(BF16)

    tile = BS((tm, tn), lambda i, j: (i, j))
    return _pcall(body, name=name, grid=(T // tm, F // tn),
                  in_specs=[BS((tm, Dm), lambda i, j: (i, 0)), BS((tn, Dm), lambda i, j: (j, 0)), tile, tile],
                  out_specs=BS((2, tm, tn), lambda i, j: (0, i, j)), out_shape=SDS((2, T, F), BF16),
                  compiler_params=_cparams(("parallel", "parallel")))(dout, w2, g, u)


@functools.partial(jax.custom_vjp, nondiff_argnums=(6,))
def ffn_block(h, g, w1b, w1c, w2b, w2c, name):
    return _ffn_fwd(h, g, w1b, w1c, w2b, w2c, name)[0]


def _ffn_fwd(h, g, w1b, w1c, w2b, w2c, name):
    hn = _rms_fwd_call(h, g, name + "_norm", BF16)
    gate, up, act = _gate_up_swiglu_call(hn, w1b, name + "_gu")
    out = _mm(act, w2b, add=h, name=name + "_down", tm=1024, tn=1024, tk=2816)
    return out, (h, g, hn, gate, up, act, w1b, w2b)


def _ffn_bwd(name, res, dout):
    h, g, hn, gate, up, act, w1b, w2b = res
    dw2 = _mm(act, dout, ta=True, name=name + "_down_dw", tm=1408, tn=1024, tk=1024)
    dgu = _down_da_swiglu_bwd_call(dout, w2b, gate, up, name + "_down_da")
    dw1 = _mm(dgu, hn, ta=True, name=name + "_gu_dw", tm=1408, tn=1024, tk=2048)
    dh, dg = _mm_rms_bwd(dgu, w1b, h, g, dout, name + "_gu_da", wt=True)
    return dh, dg.reshape(g.shape), jnp.zeros_like(w1b), dw1, jnp.zeros_like(w2b), dw2


ffn_block.defvjp(_ffn_fwd, _ffn_bwd)


def _final_call(h, g, target, name):
    T, Dm = h.shape
    tt = _tile(T, 1024, 8)

    def body(x_ref, g_ref, t_ref, loss_ref, dx_ref, dg_ref):
        @pl.when(pl.program_id(0) == 0)
        def _():
            loss_ref[...] = jnp.zeros_like(loss_ref)
            dg_ref[...] = jnp.zeros_like(dg_ref)

        xv, gv = x_ref[...], g_ref[...]
        r = lax.rsqrt(jnp.mean(xv * xv, axis=-1, keepdims=True) + NORM_EPS)
        xh = xv * r
        err = xh * gv - t_ref[...]
        loss_ref[...] += 0.5 * jnp.sum(jnp.mean(err * err, axis=-1, keepdims=True), axis=0, keepdims=True)
        dy = err * (1.0 / Dm)
        dyg = dy * gv
        dx_ref[...] = r * (dyg - xh * jnp.mean(dyg * xh, axis=-1, keepdims=True))
        dg_ref[...] += jnp.sum(dy * xh, axis=0, keepdims=True)

    row = BS((tt, Dm), lambda i: (i, 0))
    vec = BS((1, Dm), lambda i: (0, 0))
    return _pcall(body, name=name, grid=(T // tt,), in_specs=[row, vec, row],
                  out_specs=[BS((1, 1), lambda i: (0, 0)), row, vec],
                  out_shape=[SDS((1, 1), F32), SDS((T, Dm), F32), SDS((1, Dm), F32)],
                  compiler_params=_cparams(("arbitrary",)))(h, g.reshape(1, Dm), target)


ADAMW_BLOCK_ELEMS = 256 * 1024


def _adamw_call(parts, w, m, v, name):
    shape = w.shape
    if not isinstance(parts, (list, tuple)):
        parts, shape3 = [parts], (1,) + shape
    else:
        shape3 = shape
    n_lead = shape3[0]
    r, N = shape3[-2], shape3[-1]
    Ld = math.prod(shape3[1:-2])
    w, m, v = (t.reshape(n_lead * Ld, r, N) for t in (w, m, v))
    tr = _tile(r, max(8, ADAMW_BLOCK_ELEMS // N), 8)
    c1 = 1.0 - ADAM_B1 ** ADAM_STEP
    c2 = 1.0 - ADAM_B2 ** ADAM_STEP
    outs = None
    for lead, p in enumerate(parts):
        def body(p_ref, w_ref, m_ref, v_ref, *rest):
            g_ref, d_ref, nm_ref, nv_ref = rest[-4:]
            g = p_ref[0]
            for j in range(1, N_DEV):
                g = g + p_ref[j]
            nm = ADAM_B1 * m_ref[...] + (1.0 - ADAM_B1) * g
            nv = ADAM_B2 * v_ref[...] + (1.0 - ADAM_B2) * (g * g)
            g_ref[...] = g
            nm_ref[...] = nm
            nv_ref[...] = nv
            d_ref[...] = -ADAM_LR * ((nm / c1) / (jnp.sqrt(nv / c2) + ADAM_EPS) + ADAM_WD * w_ref[...])

        base = lead * Ld
        row = BS((1, tr, N), lambda l, i, base=base: (base + l, i, 0))
        prev = [] if outs is None else list(outs)
        outs = _pcall(body, name=f"{name}_{lead}", grid=(Ld, r // tr),
                      in_specs=[BS((N_DEV, 1, tr, N), lambda l, i: (0, l, i, 0)), row, row, row]
                      + [BS(memory_space=pl.ANY)] * len(prev),
                      out_specs=[row] * 4, out_shape=[SDS((n_lead * Ld, r, N), F32)] * 4,
                      input_output_aliases={4 + j: j for j in range(len(prev))},
                      compiler_params=_cparams(("parallel", "parallel")))(p.reshape(N_DEV, Ld, r, N), w, m, v, *prev)
    return [t.reshape(shape) for t in outs]


def _place():
    return lax.axis_index("x"), lax.axis_index("y"), lax.axis_index("c")


def _all_gather(xs, name):
    n = len(xs)
    pairs = [(i, l) for i, x in enumerate(xs) for l in range(x.shape[0])]

    def body(*refs):
        x_refs, o_refs = refs[:n], refs[n:2 * n]
        send_sems, recv_sems, local_sems = refs[2 * n:]
        x_, y_, c_ = _place()
        me, sibling = (x_, y_, c_), (x_, y_, 1 - c_)
        chips = [(1 - x_, y_), (x_, 1 - y_), (1 - x_, 1 - y_)]

        def copy(e, k, block, to, from_input=False):
            i, l = pairs[e]
            px, py, pc = block
            dst = o_refs[i].at[l, 4 * px + 2 * py + pc]
            return pltpu.make_async_remote_copy(
                src_ref=x_refs[i].at[l] if from_input else dst, dst_ref=dst,
                send_sem=send_sems.at[7 * e + k], recv_sem=recv_sems.at[7 * e + k],
                device_id=to, device_id_type=pl.DeviceIdType.MESH)

        every = range(len(pairs))
        mine = [pltpu.make_async_copy(x_refs[i].at[l], o_refs[i].at[l, 4 * x_ + 2 * y_ + c_], local_sems.at[e])
                for e, (i, l) in enumerate(pairs)]
        for cp in mine:
            cp.start()
        first = [copy(e, 0, me, sibling, True) for e in every]
        first += [copy(e, 1 + j, me, (*chip, c_), True) for j, chip in enumerate(chips) for e in every]
        for cp in first:
            cp.start()
        passed = []
        for j, chip in enumerate(chips):
            for e in every:
                copy(e, 1 + j, (*chip, c_), me).wait_recv()
            for e in every:
                cp = copy(e, 4 + j, (*chip, c_), sibling)
                cp.start()
                passed.append(cp)
        for e in every:
            copy(e, 0, sibling, me).wait_recv()
        for j, chip in enumerate(chips):
            for e in every:
                copy(e, 4 + j, (*chip, 1 - c_), me).wait_recv()
        for cp in first + passed:
            cp.wait_send()
        for cp in mine:
            cp.wait()

    any_spec = BS(memory_space=pl.ANY)
    return _pcall(body, name=name, in_specs=[any_spec] * n, out_specs=[any_spec] * n,
                  out_shape=[SDS((x.shape[0], N_DEV) + x.shape[1:], x.dtype) for x in xs],
                  scratch_shapes=[pltpu.SemaphoreType.DMA((7 * len(pairs),)), pltpu.SemaphoreType.DMA((7 * len(pairs),)),
                                  pltpu.SemaphoreType.DMA((len(pairs),))],
                  compiler_params=pltpu.CompilerParams(has_side_effects=True))(*xs)


def _peer_of(k, place):
    x_, y_, c_ = place
    fx, fy, fc = (k >> 2) & 1, (k >> 1) & 1, k & 1
    return fx + x_ - 2 * fx * x_, fy + y_ - 2 * fy * y_, fc + c_ - 2 * fc * c_


def _split_copy(src_ref, land_ref, send_sems, recv_sems, e, k, place, scatter):
    x_, y_, c_ = place
    px, py, pc = _peer_of(k, place)
    return pltpu.make_async_remote_copy(
        src_ref=src_ref.at[4 * px + 2 * py + pc] if scatter else src_ref, dst_ref=land_ref.at[4 * x_ + 2 * y_ + c_],
        send_sem=send_sems.at[7 * e + k - 1], recv_sem=recv_sems.at[7 * e + k - 1],
        device_id=(px, py, pc), device_id_type=pl.DeviceIdType.MESH)


def _own_copy(src_ref, land_ref, sems, slot, place, scatter):
    x_, y_, c_ = place
    me = 4 * x_ + 2 * y_ + c_
    return pltpu.make_async_copy(src_ref.at[me] if scatter else src_ref, land_ref.at[me], sems.at[slot])


_HBM_SPEC = BS(memory_space=pltpu.HBM)
_SEM_SPEC = BS(memory_space=pltpu.SEMAPHORE)
_EFFECT = pltpu.SideEffectType.DATAFLOW_SIDE_EFFECTING


def _copies_start(srcs, scatter, name, thru=None):
    n = len(srcs)
    lands = [lax.empty(s.shape if scatter else (N_DEV,) + s.shape, s.dtype) for s in srcs]
    passed = srcs + lands + list(thru or ())

    def body(*refs):
        src_refs, land_refs = refs[:n], refs[n:2 * n]
        send_sems, recv_sems = refs[len(passed)], refs[len(passed) + 1]
        token = refs[-1]
        place = _place()
        for e in range(n):
            for k in range(1, N_DEV):
                _split_copy(src_refs[e], land_refs[e], send_sems, recv_sems, e, k, place, scatter).start()
            _own_copy(src_refs[e], land_refs[e], send_sems, 7 * n + e, place, scatter).start()
        token[...] = jnp.zeros_like(token)

    hbm = lambda t: pltpu.with_memory_space_constraint(t, pltpu.HBM)
    out = _pcall(body, name=name,
                 out_shape=(pltpu.SemaphoreType.DMA((8 * n,)), pltpu.SemaphoreType.DMA((7 * n,)),
                            *[pltpu.HBM(t.shape, t.dtype) for t in passed], SDS((8, LANES), F32)),
                 in_specs=[_HBM_SPEC] * len(passed),
                 out_specs=(_SEM_SPEC, _SEM_SPEC, *[_HBM_SPEC] * len(passed), BS(memory_space=pltpu.VMEM)),
                 input_output_aliases={i: 2 + i for i in range(len(passed))},
                 compiler_params=pltpu.CompilerParams(has_side_effects=_EFFECT))(*[hbm(t) for t in passed])
    return out[0], out[1], list(out[2:2 + n]), list(out[2 + n:2 + 2 * n]), list(out[2 + 2 * n:-1])


def _copies_wait(started, which, scatter, after, name):
    send_sems, recv_sems, srcs, lands, _ = started
    n, n_started = len(which), len(srcs)
    after = list(after) if isinstance(after, (list, tuple)) else [after]

    def body(*refs):
        src_refs, land_refs = refs[:n], refs[n:2 * n]
        send_s, recv_s = refs[2 * n], refs[2 * n + 1]
        place = _place()
        for j, e in enumerate(which):
            for k in range(1, N_DEV):
                cp = _split_copy(src_refs[j], land_refs[j], send_s, recv_s, e, k, place, scatter)
                cp.wait_send()
                cp.wait_recv()
            _own_copy(src_refs[j], land_refs[j], send_s, 7 * n_started + e, place, scatter).wait()

    args = [srcs[e] for e in which] + [lands[e] for e in which]
    out = _pcall(body, name=name, out_shape=tuple(pltpu.HBM(t.shape, t.dtype) for t in args),
                 in_specs=[_HBM_SPEC] * (2 * n) + [_SEM_SPEC, _SEM_SPEC] + [BS(memory_space=pl.ANY)] * len(after),
                 out_specs=tuple([_HBM_SPEC] * (2 * n)), input_output_aliases={i: i for i in range(2 * n)},
                 compiler_params=pltpu.CompilerParams(has_side_effects=_EFFECT))(*args, send_sems, recv_sems, *after)
    return list(out[:n]), list(out[n:])


def _pad_flat(t, quantum=PACK_QUANTUM):
    f = t.reshape(-1)
    pad = (-f.shape[0]) % quantum
    return jnp.pad(f, (0, pad)) if pad else f


def _pack(arrs, dtype):
    return jnp.concatenate([_pad_flat(a.astype(dtype)) for a in arrs]).reshape(-1, LANES)


def _unpack(buf, shapes, lead=()):
    flat = buf.reshape(lead + (-1,))
    out, off = [], 0
    for s in shapes:
        n = math.prod(s)
        out.append(flat[..., off:off + n].reshape(lead + tuple(s)))
        off += n + (-n) % PACK_QUANTUM
    return out


def _full_from_gathered(g, axis):
    t = jnp.moveaxis(g, 0, axis)
    s = t.shape
    return t.reshape(s[:axis] + (s[axis] * s[axis + 1],) + s[axis + 2:])


def _parts_from_full(t, axis):
    s = t.shape
    t = t.reshape(s[:axis] + (N_DEV, s[axis] // N_DEV) + s[axis + 1:])
    return jnp.moveaxis(t, axis, 0)


def _head_rows(t):
    return jnp.repeat(t, HEAD_DIM).reshape(1, -1)


def _dilated_attention(q, k, v, name):
    no_sink = jnp.full((1, q.shape[1]), NEG, F32)
    outs, lses = zip(*[band_attention(q, k, v, no_sink, d, ATT_BLOCK, True, f"{name}_d{d}") for d in DILATIONS])
    return merge3(tuple(outs), tuple(lses), name + "_merge")


STAGES = (
    ("proj0", ('mix_norm',), (('ab_w_in', 0),)),
    ("mixer0", ('lru_conv_w', 'lru_conv_b', 'lru_ba', 'lru_bx', 'lru_lambda'),
     (('lru_wa', 0), ('lru_wx', 0), ('ab_w_out', 0))),
    ("xa0", ('xa_norm', 'xa_mem_norm'), (('xa_wq', 0), ('xa_wkv', 0), ('xa_wo', 0))),
    ("ffn0", ('ffn_norm',), (('ffn_w_gate_up', 0), ('ffn_w_down', 0))),
    ("mixer1", ('mix_norm', 'c_b_qkv', 'c_sinks', 'c_b_out'), (('c_w_qkv', 0), ('c_w_out', 0))),
    ("xa1", ('xa_norm', 'xa_mem_norm'), (('xa_wq', 1), ('xa_wkv', 1), ('xa_wo', 1))),
    ("ffn1", ('ffn_norm',), (('ffn_w_gate_up', 1), ('ffn_w_down', 1))),
)


def _stage_fn(stage, Wb, tabs, mem):
    layer = int(stage[-1])
    L = f"l{layer}"

    def run(S, Cw, h):
        def lin(a, key, bias, add, name, rows=None):
            wb, wc = Wb[key], Cw[key]
            if rows is not None:
                wb, wc = wb[rows], wc[rows]
            return linear(a, wb, wc, bias, add, name)

        def norm_lin(a, gain, key, bias, name):
            return norm_linear(a, gain, Wb[key], Cw[key], bias, key[0] in COLUMN_CUT, name)

        if stage == "mixer0":
            h, x_br, y_br, q, k, v = h
            C = S['lru_conv_w'].shape[-1]
            xc = conv4(x_br, S['lru_conv_w'][0], S['lru_conv_b'][0], L + "_conv")
            ga, gx = lru_gates(xc, Wb['lru_wa', 0], Cw['lru_wa', 0], S['lru_ba'][0],
                               Wb['lru_wx', 0], Cw['lru_wx', 0], S['lru_bx'][0], L + "_gates")
            rec = lru_scan(xc, ga, gx, y_br, S['lru_lambda'][0], L + "_scan")
            att = _dilated_attention(q, k, v, L + "_att")
            h = lin(att, ('ab_w_out', 0), None, h, L + "_w_out_att", slice(C, None))
            return lin(rec, ('ab_w_out', 0), None, h, L + "_w_out_rec", slice(0, C))
        if stage == "mixer1":
            qw = C_HEADS * HEAD_DIM
            kw = C_KV_HEADS * HEAD_DIM
            q, k, v, h = norm_linear_pieces(h, S['mix_norm'][1], Wb['c_w_qkv', 0], Cw['c_w_qkv', 0], S['c_b_qkv'][0], tabs,
                                            ((qw, True), (kw, True), (kw, False)), L + "_w_qkv")
            o = band_attention(q, k, v, _head_rows(S['c_sinks'][0]), 1, ATT_BLOCK - 1, False, L + "_att")
            return lin(o, ('c_w_out', 0), S['c_b_out'][0], h, L + "_w_out")
        if stage.startswith("xa"):
            xq, h = norm_lin(h, S['xa_norm'][layer], ('xa_wq', layer), None, L + "_xa_wq")
            xkv, _ = norm_lin(mem, S['xa_mem_norm'][layer], ('xa_wkv', layer), None, L + "_xa_wkv")
            return lin(cross_attention(xq, xkv, L + "_xa"), ('xa_wo', layer), None, h, L + "_xa_wo")
        gu, down = ('ffn_w_gate_up', layer), ('ffn_w_down', layer)
        return ffn_block(h, S['ffn_norm'][layer], Wb[gu], Cw[gu], Wb[down], Cw[down], L + "_ffn")

    return run


def kernel(x, mem, mix_norm, ab_w_in, lru_conv_w, lru_conv_b, lru_wa, lru_ba, lru_wx, lru_bx, lru_lambda, ab_w_out, c_w_qkv, c_b_qkv, c_sinks, c_w_out, c_b_out, xa_norm, xa_mem_norm, xa_wq, xa_wkv, xa_wo, ffn_norm, ffn_w_gate_up, ffn_w_down, final_norm, loss_target, m_mix_norm, m_ab_w_in, m_lru_conv_w, m_lru_conv_b, m_lru_wa, m_lru_ba, m_lru_wx, m_lru_bx, m_lru_lambda, m_ab_w_out, m_c_w_qkv, m_c_b_qkv, m_c_sinks, m_c_w_out, m_c_b_out, m_xa_norm, m_xa_mem_norm, m_xa_wq, m_xa_wkv, m_xa_wo, m_ffn_norm, m_ffn_w_gate_up, m_ffn_w_down, m_final_norm, v_mix_norm, v_ab_w_in, v_lru_conv_w, v_lru_conv_b, v_lru_wa, v_lru_ba, v_lru_wx, v_lru_bx, v_lru_lambda, v_ab_w_out, v_c_w_qkv, v_c_b_qkv, v_c_sinks, v_c_w_out, v_c_b_out, v_xa_norm, v_xa_mem_norm, v_xa_wq, v_xa_wkv, v_xa_wo, v_ffn_norm, v_ffn_w_gate_up, v_ffn_w_down, v_final_norm):
    w_loc = dict(zip(WEIGHT_NAMES, (mix_norm, ab_w_in, lru_conv_w, lru_conv_b, lru_wa, lru_ba, lru_wx, lru_bx, lru_lambda, ab_w_out, c_w_qkv, c_b_qkv, c_sinks, c_w_out, c_b_out, xa_norm, xa_mem_norm, xa_wq, xa_wkv, xa_wo, ffn_norm, ffn_w_gate_up, ffn_w_down, final_norm)))
    m_loc = dict(zip(WEIGHT_NAMES, (m_mix_norm, m_ab_w_in, m_lru_conv_w, m_lru_conv_b, m_lru_wa, m_lru_ba, m_lru_wx, m_lru_bx, m_lru_lambda, m_ab_w_out, m_c_w_qkv, m_c_b_qkv, m_c_sinks, m_c_w_out, m_c_b_out, m_xa_norm, m_xa_mem_norm, m_xa_wq, m_xa_wkv, m_xa_wo, m_ffn_norm, m_ffn_w_gate_up, m_ffn_w_down, m_final_norm)))
    v_loc = dict(zip(WEIGHT_NAMES, (v_mix_norm, v_ab_w_in, v_lru_conv_w, v_lru_conv_b, v_lru_wa, v_lru_ba, v_lru_wx, v_lru_bx, v_lru_lambda, v_ab_w_out, v_c_w_qkv, v_c_b_qkv, v_c_sinks, v_c_w_out, v_c_b_out, v_xa_norm, v_xa_mem_norm, v_xa_wq, v_xa_wkv, v_xa_wo, v_ffn_norm, v_ffn_w_gate_up, v_ffn_w_down, v_final_norm)))

    first_keys = list(STAGES[0][2])
    keys = [key for _, _, stage_keys in STAGES[1:] for key in stage_keys]
    shards = [_shard_view(n, w_loc[n])[l].astype(BF16) for n, l in keys]
    first_g = _all_gather([_pack([w_loc[n] for n in SMALL], F32)[None]]
                          + [_shard_view(n, w_loc[n])[l].astype(BF16)[None] for n, l in first_keys], "gather_first")
    gather = _copies_start(shards, False, "gather_start", thru=[first_g[0]])
    small_g = gather[4][0][0]
    Wb = {key: _full_from_gathered(g[0], _layer_shard_axis(key[0])) for key, g in zip(first_keys, first_g[1:])}
    S = {n: w_loc[n] for n in REPLICATED}
    for n, t in zip(SMALL, _unpack(small_g, [w_loc[n].shape for n in SMALL], lead=(N_DEV,))):
        S[n] = _full_from_gathered(t, SHARD_AXIS[n])

    tabs = _rope_tables(x.shape[1])
    w_in = Wb[first_keys[0]]
    hn0 = _rms_fwd_call(x[0], S['mix_norm'][0], "l0_w_in_norm", BF16)
    lru_w, att_w = N_DEV * lru_conv_w.shape[-1], B_HEADS * HEAD_DIM
    pieces = (("x", lru_w, False), ("y", lru_w, False), ("q", att_w, True), ("k", att_w, True), ("v", att_w, False))
    h, row = [x[0]], 0
    for piece, width, rotated in pieces:
        h.append(_mm(hn0, w_in[row:row + width], tb=True, rope=tabs if rotated else None, name="l0_w_in_" + piece,
                     tm=2048, tn=1024, tk=1024))
        row += width
    h = tuple(h)
    vjps = []
    for stage, small_names, stage_keys in STAGES[1:]:
        which = [keys.index(key) for key in stage_keys]
        _, lands = _copies_wait(gather, which, False, jax.tree.leaves(h)[-1], "gather_wait_" + stage)
        for e, land in zip(which, lands):
            Wb[keys[e]] = _full_from_gathered(land, _layer_shard_axis(keys[e][0]))
        carriers = {key: jnp.zeros(Wb[key].shape, F32) for key in stage_keys}
        h, vjp_fn = jax.vjp(_stage_fn(stage, Wb, tabs, mem[0]), {n: S[n] for n in small_names}, carriers, h)
        vjps.append(vjp_fn)
    loss_part, dh, dg_final = _final_call(h, S['final_norm'], loss_target[0], "final_loss")

    grads = {'final_norm': dg_final.reshape(final_norm.shape)}
    exchanges, send_keys, send_parts = [], [], []

    def start_exchange(stage, dh):
        leaves, tree = jax.tree.flatten(dh)
        started = _copies_start(list(send_parts), True, "grad_start_" + stage, thru=leaves)
        exchanges.append((stage, started, list(send_keys)))
        send_keys.clear()
        send_parts.clear()
        return jax.tree.unflatten(tree, started[4])

    for (stage, small_names, stage_keys), vjp_fn in zip(reversed(STAGES[1:]), reversed(vjps)):
        g_small, g_big, dh = vjp_fn(dh)
        for n in small_names:
            grads[n] = grads[n] + g_small[n] if n in grads else g_small[n]
        send_keys += list(stage_keys)
        send_parts += [_parts_from_full(g_big[key], _layer_shard_axis(key[0])) for key in stage_keys]
        if stage == "xa1":
            continue
        if stage == "mixer0":
            small_parts = [_parts_from_full(grads[n], SHARD_AXIS[n]) for n in SMALL]
            send_keys.append("small")
            send_parts.append(jnp.stack([_pack([p[j] for p in small_parts], F32) for j in range(N_DEV)]))
        dh = start_exchange(stage, dh)
    d_res, d_proj = dh[0], _join_pieces_call(dh[1:], [rotated for _, _, rotated in pieces], tabs, "l0_w_in_dproj")[0]
    send_keys.append(first_keys[0])
    send_parts.append(_parts_from_full(_mm(d_proj, hn0, ta=True, name="l0_w_in_dw", tm=1408, tn=1024, tk=2048),
                                       _layer_shard_axis(first_keys[0][0])))
    d_res, d_proj = start_exchange("proj0", (d_res, d_proj))
    dx, dg0 = _mm_rms_bwd(d_proj, w_in, x[0], S['mix_norm'][0], d_res, "l0_w_in_da", wt=True)
    grads['mix_norm'] = grads['mix_norm'] + jnp.concatenate([dg0, jnp.zeros_like(dg0)], axis=0)
    rep_names = REPLICATED + ["loss"]
    grads["loss"] = loss_part
    zero = jnp.zeros((1, 1), F32)
    for d in (w_loc, m_loc, v_loc):
        d["loss"] = zero
    rep_started = _copies_start([_pack([grads[n] for n in rep_names], F32)], False, "rep_grads_start")

    parts, out = {}, {}

    def end_exchange(stage, started, ex_keys, after):
        _, lands = _copies_wait(started, list(range(len(ex_keys))), True, after, "grad_wait_" + stage)
        parts.update(zip(ex_keys, lands))

    def adamw(p, names, call_name):
        if len(names) == 1:
            n = names[0]
            res = _adamw_call(p, _shard_view(n, w_loc[n]), _shard_view(n, m_loc[n]), _shard_view(n, v_loc[n]), call_name)
            for kind, t in zip(("grad", "delta", "new_m", "new_v"), res):
                out[kind, n] = _shard_view(n, t)
        else:
            res = _adamw_call(p, *[_pack([d[n] for n in names], F32) for d in (w_loc, m_loc, v_loc)], call_name)
            for kind, buf in zip(("grad", "delta", "new_m", "new_v"), res):
                for n, t in zip(names, _unpack(buf, [w_loc[n].shape for n in names])):
                    out[kind, n] = t

    for ex in exchanges[:-1]:
        end_exchange(*ex, dx)
    last_names = {key[0] for key in exchanges[-1][2]}
    for n in BIG:
        if n not in last_names:
            adamw([parts[n, l] for l in range(w_loc[n].shape[0])], [n], "adamw_" + n)
    adamw(parts["small"], SMALL, "adamw_small")
    end_exchange(*exchanges[-1], [out["new_v", n] for n in BIG if n not in last_names])
    for n in BIG:
        if n in last_names:
            adamw([parts[n, l] for l in range(w_loc[n].shape[0])], [n], "adamw_" + n)
    _, rep_land = _copies_wait(rep_started, [0], False, out["new_v", "ab_w_in"], "rep_grads_wait")
    adamw(rep_land[0], rep_names, "adamw_replicated")
    loss = out["grad", "loss"][0, 0]

    return (loss, dx[None], *[out[kind, n] for kind in ("grad", "delta", "new_m", "new_v") for n in WEIGHT_NAMES])
```

```python
import functools
import math

import jax
import jax.numpy as jnp
from jax import lax
from jax.experimental import pallas as pl
from jax.experimental.pallas import tpu as pltpu

F32 = jnp.float32
BF16 = jnp.bfloat16
SDS = jax.ShapeDtypeStruct
BS = pl.BlockSpec

N_DEV = 8
NORM_EPS = 1e-6
ROPE_THETA = 500000.0
HEAD_DIM = 64
ROT_DIM = 16
ATT_BLOCK = 128
LRU_C = 8.0
LRU_HEADS = 4
DILATIONS = (1, 4, 16)
B_HEADS = 8
C_HEADS = 16
C_KV_HEADS = 2
XA_HEADS = 4
XA_HEAD_DIM = 128
NEG = -1e30
ADAM_LR, ADAM_B1, ADAM_B2, ADAM_EPS, ADAM_WD, ADAM_STEP = 0.001, 0.9, 0.999, 1e-08, 0.01, 10
LANES = 128
VMEM_LIMIT = 48 * 1024 * 1024

WEIGHT_NAMES = ['mix_norm', 'ab_w_in', 'lru_conv_w', 'lru_conv_b', 'lru_wa', 'lru_ba', 'lru_wx', 'lru_bx',
                'lru_lambda', 'ab_w_out', 'c_w_qkv', 'c_b_qkv', 'c_sinks', 'c_w_out', 'c_b_out', 'xa_norm',
                'xa_mem_norm', 'xa_wq', 'xa_wkv', 'xa_wo', 'ffn_norm', 'ffn_w_gate_up', 'ffn_w_down', 'final_norm']
SHARD_AXIS = {'ab_w_in': 2, 'lru_conv_w': 2, 'lru_wa': 2, 'lru_ba': 2, 'lru_wx': 2, 'lru_bx': 2, 'ab_w_out': 1,
              'c_w_qkv': 2, 'c_b_qkv': 1, 'c_w_out': 1, 'c_b_out': 1, 'xa_wq': 1, 'xa_wkv': 1, 'xa_wo': 2,
              'ffn_w_gate_up': 2, 'ffn_w_down': 1}
BIG = ['ab_w_in', 'lru_wa', 'lru_wx', 'ab_w_out', 'c_w_qkv', 'c_w_out', 'xa_wq', 'xa_wkv', 'xa_wo',
       'ffn_w_gate_up', 'ffn_w_down']
SMALL = ['lru_conv_w', 'lru_ba', 'lru_bx', 'c_b_qkv', 'c_b_out']
REPLICATED = [n for n in WEIGHT_NAMES if n not in SHARD_AXIS]
COLUMN_CUT = ('ab_w_in', 'c_w_qkv', 'ffn_w_gate_up')
PACK_QUANTUM = 2048


def _shard_view(name, t):
    return jnp.swapaxes(t, -1, -2) if name in COLUMN_CUT else t


def _layer_shard_axis(name):
    return 0 if name in COLUMN_CUT else SHARD_AXIS[name] - 1


def _pcall(body, **kw):
    return pl.pallas_call(body, **kw)


def _cparams(sem=None):
    return pltpu.CompilerParams(dimension_semantics=sem, vmem_limit_bytes=VMEM_LIMIT)


def _tile(n, target, mult=LANES):
    if n <= target:
        return n
    t = (target // mult) * mult
    while t >= mult:
        if n % t == 0:
            return t
        t -= mult
    return n


def _sigmoid(x):
    return 1.0 / (1.0 + jnp.exp(-x))


def _expm1(x):
    small = x * (1.0 + x * (0.5 + x * (1.0 / 6.0 + x * (1.0 / 24.0))))
    return jnp.where(jnp.abs(x) < 0.03, small, jnp.exp(x) - 1.0)


_GELU_C = math.sqrt(2.0 / math.pi)


def _gelu_parts(y):
    y2 = y * y
    th = jnp.tanh(_GELU_C * (y + 0.044715 * y * y2))
    g = 0.5 * y * (1.0 + th)
    dg = 0.5 * (1.0 + th) + 0.5 * y * (1.0 - th * th) * _GELU_C * (1.0 + 3.0 * 0.044715 * y2)
    return g, dg


def _rotate(xv, tab_refs, inverse=False):
    W = xv.shape[1]
    half = ROT_DIM // 2
    c, sa, sb = (jnp.tile(t[...], (1, W // LANES)) for t in tab_refs)
    if not inverse:
        return xv * c + pltpu.roll(xv, half, axis=1) * sa + pltpu.roll(xv, W - half, axis=1) * sb
    return xv * c + pltpu.roll(xv * sa, W - half, axis=1) + pltpu.roll(xv * sb, half, axis=1)


def _mm(a, b, *, ta=False, tb=False, bias=None, add=None, rope=None, name, tm=512, tn=512, tk=2048):
    halves = a.ndim == 3
    if halves:
        M, K = 2 * a.shape[2], a.shape[1]
        tm = min(tm, a.shape[2])
    else:
        M, K = (a.shape[1], a.shape[0]) if ta else a.shape
    N = b.shape[0] if tb else b.shape[1]
    tm, tn, tk = _tile(a.shape[2] if halves else M, tm), _tile(N, tn), _tile(K, tk)
    nk = K // tk
    dn = (((0 if ta else 1,), (1 if tb else 0,)), ((), ()))

    def body(*refs):
        a_ref, b_ref = refs[0], refs[1]
        pos = 2
        bias_ref = add_ref = None
        if bias is not None:
            bias_ref = refs[pos]
            pos += 1
        if add is not None:
            add_ref = refs[pos]
            pos += 1
        tab_refs = refs[pos:pos + 3] if rope is not None else None
        pos += 3 if rope is not None else 0
        o_ref = refs[pos]
        part = lax.dot_general(a_ref[...].astype(BF16), b_ref[...].astype(BF16), dn, preferred_element_type=F32)

        def finish(r):
            if bias_ref is not None:
                r = r + bias_ref[...]
            if add_ref is not None:
                r = r + add_ref[...]
            o_ref[...] = r if tab_refs is None else _rotate(r, tab_refs)

        if nk == 1:
            finish(part)
            return
        acc_ref = refs[pos + 1]
        k = pl.program_id(2)

        @pl.when(k == 0)
        def _():
            acc_ref[...] = part

        @pl.when((k > 0) & (k < nk - 1))
        def _():
            acc_ref[...] += part

        @pl.when(k == nk - 1)
        def _():
            finish(acc_ref[...] + part)

    per_half = (M // 2) // tm if halves else 0
    in_specs = [BS((None, tk, tm), lambda i, j, k: (i // per_half, k, i % per_half)) if halves
                else BS((tk, tm), lambda i, j, k: (k, i)) if ta else BS((tm, tk), lambda i, j, k: (i, k)),
                BS((tn, tk), lambda i, j, k: (j, k)) if tb else BS((tk, tn), lambda i, j, k: (k, j))]
    args = [a, b]
    if bias is not None:
        in_specs.append(BS((1, tn), lambda i, j, k: (0, j)))
        args.append(bias.reshape(1, N))
    if add is not None:
        in_specs.append(BS((tm, tn), lambda i, j, k: (i, j)))
        args.append(add)
    if rope is not None:
        in_specs += [BS((tm, LANES), lambda i, j, k: (i, 0))] * 3
        args += list(rope)
    return _pcall(body, name=name, grid=(M // tm, N // tn, nk), in_specs=in_specs,
                  out_specs=BS((tm, tn), lambda i, j, k: (i, j)), out_shape=SDS((M, N), F32),
                  scratch_shapes=[pltpu.VMEM((tm, tn), F32)] if nk > 1 else [],
                  compiler_params=_cparams(("parallel", "parallel", "arbitrary")))(*args)


def _mm_rms_bwd(dy, wb, x, g, add, name, wt=False):
    halves = dy.ndim == 3
    T, Kc = (dy.shape[1], 2 * dy.shape[2]) if halves else dy.shape
    Dm = wb.shape[1] if wt else wb.shape[0]
    tm, tk = _tile(T, 1024), _tile(Kc // 2 if halves else Kc, 1408)
    nk = Kc // tk
    per_half = (Kc // 2) // tk if halves else 0
    dn = (((1,), (0 if wt else 1,)), ((), ()))

    def body(*refs):
        dy_ref, w_ref, x_ref, g_ref = refs[:4]
        dx_ref, dg_ref, acc_ref = refs[-3:]
        i, k = pl.program_id(0), pl.program_id(1)
        part = lax.dot_general(dy_ref[...].astype(BF16), w_ref[...], dn, preferred_element_type=F32)

        @pl.when(k == 0)
        def _():
            acc_ref[...] = part

        @pl.when((k > 0) & (k < nk - 1))
        def _():
            acc_ref[...] += part

        @pl.when((i == 0) & (k == 0))
        def _():
            dg_ref[...] = jnp.zeros_like(dg_ref)

        @pl.when(k == nk - 1)
        def _():
            d = part if nk == 1 else acc_ref[...] + part
            xv = x_ref[...]
            r = lax.rsqrt(jnp.mean(xv * xv, axis=-1, keepdims=True) + NORM_EPS)
            xh = xv * r
            dyg = d * g_ref[...]
            dx = r * (dyg - xh * jnp.mean(dyg * xh, axis=-1, keepdims=True))
            dx_ref[...] = dx if add is None else dx + refs[4][...]
            dg_ref[...] += jnp.sum(d * xh, axis=0, keepdims=True)

    row = BS((tm, Dm), lambda i, k: (i, 0))
    vec = BS((1, Dm), lambda i, k: (0, 0))
    extra = [] if add is None else [add]
    return _pcall(body, name=name, grid=(T // tm, nk),
                  in_specs=[BS((None, tm, tk), lambda i, k: (k // per_half, i, k % per_half)) if halves
                            else BS((tm, tk), lambda i, k: (i, k)),
                            BS((tk, Dm), lambda i, k: (k, 0)) if wt else BS((Dm, tk), lambda i, k: (0, k)), row, vec]
                  + [row] * len(extra),
                  out_specs=[row, vec], out_shape=[SDS((T, Dm), F32), SDS((1, Dm), F32)],
                  scratch_shapes=[pltpu.VMEM((tm, Dm), F32)],
                  compiler_params=_cparams(("arbitrary", "arbitrary")))(dy, wb, x, g.reshape(1, Dm), *extra)


def _colsum(x, name):
    T, N = x.shape
    tt = _tile(T, 1024, 8)

    def body(x_ref, o_ref):
        @pl.when(pl.program_id(0) == 0)
        def _():
            o_ref[...] = jnp.zeros_like(o_ref)

        o_ref[...] += jnp.sum(x_ref[...], axis=0, keepdims=True)

    return _pcall(body, name=name, grid=(T // tt,), in_specs=[BS((tt, N), lambda i: (i, 0))],
                  out_specs=BS((1, N), lambda i: (0, 0)), out_shape=SDS((1, N), F32),
                  compiler_params=_cparams(("arbitrary",)))(x)


@functools.partial(jax.custom_vjp, nondiff_argnums=(5,))
def linear(a, wb, wc, bias, add, name):
    return _linear_fwd(a, wb, wc, bias, add, name)[0]


def _linear_fwd(a, wb, wc, bias, add, name):
    out = _mm(a, wb, bias=bias, add=add, name=name, tm=1024, tn=1024, tk=1024)
    return out, (a, wb, bias is not None, add is not None)


def _linear_bwd(name, res, g):
    a, wb, has_bias, has_add = res
    da = _mm(g, wb, tb=True, name=name + "_da", tm=1024, tn=1024, tk=1024)
    dw = _mm(a, g, ta=True, name=name + "_dw", tm=1024, tn=1024, tk=1024)
    dbias = _colsum(g, name + "_db").reshape(-1) if has_bias else None
    return da, jnp.zeros_like(wb), dw, dbias, (g if has_add else None)


linear.defvjp(_linear_fwd, _linear_bwd)


def _rms_fwd_call(x, g, name, out_dtype=F32):
    T, Dm = x.shape
    tt = _tile(T, 2048, 16)

    def body(x_ref, g_ref, o_ref):
        xv = x_ref[...]
        r = lax.rsqrt(jnp.mean(xv * xv, axis=-1, keepdims=True) + NORM_EPS)
        o_ref[...] = ((xv * r) * g_ref[...]).astype(out_dtype)

    return _pcall(body, name=name, grid=(T // tt,),
                  in_specs=[BS((tt, Dm), lambda i: (i, 0)), BS((1, Dm), lambda i: (0, 0))],
                  out_specs=BS((tt, Dm), lambda i: (i, 0)), out_shape=SDS((T, Dm), out_dtype),
                  compiler_params=_cparams(("parallel",)))(x, g.reshape(1, Dm))


@functools.partial(jax.custom_vjp, nondiff_argnums=(5, 6))
def norm_linear(x, g, wb, wc, bias, wt, name):
    return _norm_linear_fwd(x, g, wb, wc, bias, wt, name)[0]


def _norm_linear_fwd(x, g, wb, wc, bias, wt, name):
    hn = _rms_fwd_call(x, g, name + "_norm", BF16)
    return (_mm(hn, wb, tb=wt, bias=bias, name=name, tm=1024, tn=1408, tk=1024), x), (x, g, hn, wb, bias is not None)


def _norm_linear_bwd(wt, name, res, cts):
    x, g, hn, wb, has_bias = res
    dy, dres = cts
    if wt:
        dw = _mm(dy, hn, ta=True, name=name + "_dw", tm=1408, tn=1024, tk=2048)
    else:
        dw = _mm(hn, dy, ta=True, name=name + "_dw", tm=1024, tn=1408, tk=1024)
    dx, dg = _mm_rms_bwd(dy, wb, x, g, dres, name + "_da", wt)
    dbias = _colsum(dy, name + "_db").reshape(-1) if has_bias else None
    return dx, dg.reshape(g.shape), jnp.zeros_like(wb), dw, dbias


norm_linear.defvjp(_norm_linear_fwd, _norm_linear_bwd)


def _rope_tables(T):
    half = ROT_DIM // 2
    inv = ROPE_THETA ** (-jnp.arange(0, ROT_DIM, 2, dtype=F32) / ROT_DIM)
    lane = jnp.arange(LANES) % HEAD_DIM
    freq = jnp.where(lane < ROT_DIM, inv[lane % half], 0.0)
    ang = jnp.arange(T, dtype=F32)[:, None] * freq[None, :]
    c, s = jnp.cos(ang), jnp.sin(ang)
    sa = jnp.where((lane >= half) & (lane < ROT_DIM), s, 0.0)
    sb = jnp.where(lane < half, -s, 0.0)
    return c, sa, sb


def _join_pieces_call(pieces, rotated, tabs, name):
    T = pieces[0].shape[0]
    widths = [p.shape[1] for p in pieces]
    W = sum(widths)
    tt = _tile(T, 512, 16)

    def body(*refs):
        tab_refs, o_ref, sum_ref = refs[len(pieces):len(pieces) + 3], refs[-2], refs[-1]

        @pl.when(pl.program_id(0) == 0)
        def _():
            sum_ref[...] = jnp.zeros_like(sum_ref)

        off = 0
        for p_ref, w, r in zip(refs, widths, rotated):
            piece = _rotate(p_ref[...], tab_refs, inverse=True) if r else p_ref[...]
            o_ref[:, off:off + w] = piece.astype(BF16)
            sum_ref[:, off:off + w] += jnp.sum(piece, axis=0, keepdims=True)
            off += w

    return _pcall(body, name=name, grid=(T // tt,),
                  in_specs=[BS((tt, w), lambda i: (i, 0)) for w in widths] + [BS((tt, LANES), lambda i: (i, 0))] * 3,
                  out_specs=[BS((tt, W), lambda i: (i, 0)), BS((1, W), lambda i: (0, 0))],
                  out_shape=[SDS((T, W), BF16), SDS((1, W), F32)],
                  compiler_params=_cparams(("arbitrary",)))(*pieces, *tabs)


@functools.partial(jax.custom_vjp, nondiff_argnums=(6, 7))
def norm_linear_pieces(x, g, wb, wc, bias, tabs, pieces, name):
    return _norm_linear_pieces_fwd(x, g, wb, wc, bias, tabs, pieces, name)[0]


def _norm_linear_pieces_fwd(x, g, wb, wc, bias, tabs, pieces, name):
    hn = _rms_fwd_call(x, g, name + "_norm", BF16)
    outs, row = [], 0
    for i, (width, rotated) in enumerate(pieces):
        outs.append(_mm(hn, wb[row:row + width], tb=True, bias=None if bias is None else bias[row:row + width],
                        rope=tabs if rotated else None, name=f"{name}_{i}", tm=2048, tn=1024, tk=1024))
        row += width
    return (*outs, x), (x, g, hn, wb, tabs, bias is not None)


def _norm_linear_pieces_bwd(pieces, name, res, cts):
    x, g, hn, wb, tabs, has_bias = res
    dy, dy_sum = _join_pieces_call(cts[:-1], [rotated for _, rotated in pieces], tabs, name + "_join")
    dw = _mm(dy, hn, ta=True, name=name + "_dw", tm=1408, tn=1024, tk=2048)
    dx, dg = _mm_rms_bwd(dy, wb, x, g, cts[-1], name + "_da", wt=True)
    dbias = dy_sum.reshape(-1) if has_bias else None
    return dx, dg.reshape(g.shape), jnp.zeros_like(wb), dw, dbias, jax.tree.map(jnp.zeros_like, tabs)


norm_linear_pieces.defvjp(_norm_linear_pieces_fwd, _norm_linear_pieces_bwd)


def _conv_fwd_call(x, w, b, name):
    T, C = x.shape
    tt = _tile(T, 2048, 8)
    per = tt // 8

    def body(x_ref, halo_ref, w_ref, b_ref, o_ref):
        i = pl.program_id(0)
        halo = jnp.where(i > 0, halo_ref[...], 0.0)
        e = jnp.concatenate([halo, x_ref[...]], axis=0)
        acc = b_ref[...]
        for k in (3, 2, 1):
            acc = acc + pltpu.roll(e, k, axis=0)[8:, :] * w_ref[3 - k:4 - k, :]
        o_ref[...] = acc + x_ref[...] * w_ref[3:4, :]

    return _pcall(body, name=name, grid=(T // tt,),
                  in_specs=[BS((tt, C), lambda i: (i, 0)), BS((8, C), lambda i: (jnp.maximum(i * per - 1, 0), 0)),
                            BS((4, C), lambda i: (0, 0)), BS((1, C), lambda i: (0, 0))],
                  out_specs=BS((tt, C), lambda i: (i, 0)), out_shape=SDS((T, C), F32),
                  compiler_params=_cparams(("parallel",)))(x, x, w, b.reshape(1, C))


def _conv_bwd_call(x, w, dy, name):
    T, C = x.shape
    tt = _tile(T, 1024, 8)
    per = tt // 8
    nt = T // tt

    def body(x_ref, halo_ref, w_ref, dy_ref, nxt_ref, dx_ref, dwb_ref):
        i = pl.program_id(0)
        halo = jnp.where(i > 0, halo_ref[...], 0.0)
        e = jnp.concatenate([halo, x_ref[...]], axis=0)
        dy = dy_ref[...]
        nxt = jnp.where(i < nt - 1, nxt_ref[...], 0.0)
        f = jnp.concatenate([dy, nxt], axis=0)
        dx = dy * w_ref[3:4, :]
        rows = [None] * 4
        rows[3] = jnp.sum(dy * x_ref[...], axis=0, keepdims=True)
        for k in (1, 2, 3):
            dx = dx + pltpu.roll(f, tt + 8 - k, axis=0)[:tt, :] * w_ref[3 - k:4 - k, :]
            rows[3 - k] = jnp.sum(dy * pltpu.roll(e, k, axis=0)[8:, :], axis=0, keepdims=True)
        dx_ref[...] = dx
        upd = jnp.concatenate(rows + [jnp.sum(dy, axis=0, keepdims=True), jnp.zeros((3, C), F32)], axis=0)

        @pl.when(i == 0)
        def _():
            dwb_ref[...] = jnp.zeros_like(dwb_ref)

        dwb_ref[...] += upd

    row = BS((tt, C), lambda i: (i, 0))
    return _pcall(body, name=name, grid=(nt,),
                  in_specs=[row, BS((8, C), lambda i: (jnp.maximum(i * per - 1, 0), 0)), BS((4, C), lambda i: (0, 0)),
                            row, BS((8, C), lambda i: (jnp.minimum((i + 1) * per, T // 8 - 1), 0))],
                  out_specs=[row, BS((8, C), lambda i: (0, 0))],
                  out_shape=[SDS((T, C), F32), SDS((8, C), F32)],
                  compiler_params=_cparams(("arbitrary",)))(x, x, w, dy, dy)


@functools.partial(jax.custom_vjp, nondiff_argnums=(3,))
def conv4(x, w, b, name):
    return _conv_fwd_call(x, w, b, name)


def _conv4_fwd(x, w, b, name):
    return _conv_fwd_call(x, w, b, name), (x, w)


def _conv4_bwd(name, res, dy):
    x, w = res
    dx, dwb = _conv_bwd_call(x, w, dy, name + "_bwd")
    return dx, dwb[0:4], dwb[4]


conv4.defvjp(_conv4_fwd, _conv4_bwd)


def _gates_fwd_call(xc, wa, ba, wx, bx, name):
    T, C = xc.shape
    hd = C // LRU_HEADS
    tt = _tile(T, 2048, 8)

    def body(x_ref, wa_ref, ba_ref, wx_ref, bx_ref, ga_ref, gx_ref):
        xb = x_ref[...].astype(BF16)
        ga_ref[...] = jnp.dot(xb, wa_ref[0].astype(BF16), preferred_element_type=F32) + ba_ref[...]
        gx_ref[...] = jnp.dot(xb, wx_ref[0].astype(BF16), preferred_element_type=F32) + bx_ref[...]

    blk = BS((tt, hd), lambda i, h: (i, h))
    wsp = BS((1, hd, hd), lambda i, h: (h, 0, 0))
    bsp = BS((1, hd), lambda i, h: (0, h))
    return _pcall(body, name=name, grid=(T // tt, LRU_HEADS), in_specs=[blk, wsp, bsp, wsp, bsp],
                  out_specs=[blk, blk], out_shape=[SDS((T, C), F32)] * 2,
                  compiler_params=_cparams(("parallel", "parallel")))(xc, wa, ba.reshape(1, C), wx, bx.reshape(1, C))


def _gates_bwd_x_call(dga, dgx, wa, wx, add, name):
    T, C = dga.shape
    hd = C // LRU_HEADS
    tt = _tile(T, 2048, 8)
    dn = (((1,), (1,)), ((), ()))

    def body(da_ref, dx_ref, wa_ref, wx_ref, add_ref, o_ref):
        o_ref[...] = (lax.dot_general(da_ref[...].astype(BF16), wa_ref[0].astype(BF16), dn, preferred_element_type=F32)
                      + lax.dot_general(dx_ref[...].astype(BF16), wx_ref[0].astype(BF16), dn, preferred_element_type=F32)
                      + add_ref[...])

    blk = BS((tt, hd), lambda i, h: (i, h))
    wsp = BS((1, hd, hd), lambda i, h: (h, 0, 0))
    return _pcall(body, name=name, grid=(T // tt, LRU_HEADS), in_specs=[blk, blk, wsp, wsp, blk], out_specs=blk,
                  out_shape=SDS((T, C), F32), compiler_params=_cparams(("parallel", "parallel")))(dga, dgx, wa, wx, add)


def _gates_bwd_w_call(xc, dga, dgx, name):
    T, C = xc.shape
    hd = C // LRU_HEADS
    tt = _tile(T, 2048, 8)
    dn = (((0,), (0,)), ((), ()))

    def body(x_ref, da_ref, dx_ref, dwa_ref, dwx_ref, dba_ref, dbx_ref):
        @pl.when(pl.program_id(1) == 0)
        def _():
            dwa_ref[...] = jnp.zeros_like(dwa_ref)
            dwx_ref[...] = jnp.zeros_like(dwx_ref)
            dba_ref[...] = jnp.zeros_like(dba_ref)
            dbx_ref[...] = jnp.zeros_like(dbx_ref)

        xb = x_ref[...].astype(BF16)
        da, dx = da_ref[...], dx_ref[...]
        dwa_ref[0] += lax.dot_general(xb, da.astype(BF16), dn, preferred_element_type=F32)
        dwx_ref[0] += lax.dot_general(xb, dx.astype(BF16), dn, preferred_element_type=F32)
        dba_ref[...] += jnp.sum(da, axis=0, keepdims=True)
        dbx_ref[...] += jnp.sum(dx, axis=0, keepdims=True)

    blk = BS((tt, hd), lambda h, i: (i, h))
    wsp = BS((1, hd, hd), lambda h, i: (h, 0, 0))
    bsp = BS((1, hd), lambda h, i: (0, h))
    return _pcall(body, name=name, grid=(LRU_HEADS, T // tt), in_specs=[blk, blk, blk],
                  out_specs=[wsp, wsp, bsp, bsp],
                  out_shape=[SDS((LRU_HEADS, hd, hd), F32)] * 2 + [SDS((1, C), F32)] * 2,
                  compiler_params=_cparams(("parallel", "arbitrary")))(xc, dga, dgx)


@functools.partial(jax.custom_vjp, nondiff_argnums=(7,))
def lru_gates(xc, wa, wa_c, ba, wx, wx_c, bx, name):
    return _lru_gates_fwd(xc, wa, wa_c, ba, wx, wx_c, bx, name)[0]


def _lru_gates_fwd(xc, wa, wa_c, ba, wx, wx_c, bx, name):
    return (*_gates_fwd_call(xc, wa, ba, wx, bx, name), xc), (xc, wa, wx, ba.shape)


def _lru_gates_bwd(name, res, g):
    xc, wa, wx, bshape = res
    dga, dgx, dxc_other = g
    dxc = _gates_bwd_x_call(dga, dgx, wa, wx, dxc_other, name + "_dx")
    dwa, dwx, dba, dbx = _gates_bwd_w_call(xc, dga, dgx, name + "_dw")
    return dxc, jnp.zeros_like(wa), dwa, dba.reshape(bshape), jnp.zeros_like(wx), dwx, dbx.reshape(bshape)


lru_gates.defvjp(_lru_gates_fwd, _lru_gates_bwd)


def _lru_coeffs(xc, ga, gx, lam):
    r = _sigmoid(ga)
    ig = _sigmoid(gx)
    z = -lam
    sp = jnp.maximum(z, 0.0) + jnp.log(1.0 + jnp.exp(-jnp.abs(z)))
    la = -LRU_C * r * sp
    a = jnp.exp(la)
    s = jnp.sqrt(-_expm1(2.0 * la))
    return r, ig, sp, a, s


LRU_TT = 256


def _scan_fwd_call(xc, ga, gx, y, lam, name):
    T, C = xc.shape
    tt = _tile(T, 2 * LRU_TT, 8)

    def body(xc_ref, ga_ref, gx_ref, y_ref, lam_ref, h_ref, rec_ref, a_buf, carry):
        @pl.when(pl.program_id(0) == 0)
        def _():
            carry[...] = jnp.zeros_like(carry)

        xcv = xc_ref[...]
        _, ig, _, a, s = _lru_coeffs(xcv, ga_ref[...], gx_ref[...], lam_ref[...])
        a_buf[...] = a
        h_ref[...] = s * (ig * xcv)

        def step(t, h):
            hn = a_buf[pl.ds(t, 1), :] * h + h_ref[pl.ds(t, 1), :]
            h_ref[pl.ds(t, 1), :] = hn
            return hn

        carry[0:1, :] = lax.fori_loop(0, tt, step, carry[0:1, :], unroll=8)
        g, _ = _gelu_parts(y_ref[...])
        rec_ref[...] = h_ref[...] * g

    row = BS((tt, C), lambda i: (i, 0))
    vec = BS((1, C), lambda i: (0, 0))
    return _pcall(body, name=name, grid=(T // tt,), in_specs=[row, row, row, row, vec], out_specs=[row, row],
                  out_shape=[SDS((T, C), F32)] * 2,
                  scratch_shapes=[pltpu.VMEM((tt, C), F32), pltpu.VMEM((8, C), F32)],
                  compiler_params=_cparams(("arbitrary",)))(xc, ga, gx, y, lam.reshape(1, C))


def _scan_bwd_call(xc, ga, gx, y, lam, h, drec, name):
    T, C = xc.shape
    tt = _tile(T, LRU_TT, 8)
    nt = T // tt
    per = tt // 8

    def body(xc_ref, ga_ref, gx_ref, y_ref, lam_ref, h_ref, halo_ref, dr_ref,
             dga_ref, dgx_ref, dxc_ref, dy_ref, dlam_ref, a_buf, g_buf, carry):
        i = pl.program_id(0)

        @pl.when(i == 0)
        def _():
            carry[...] = jnp.zeros_like(carry)
            dlam_ref[...] = jnp.zeros_like(dlam_ref)

        xcv, lam = xc_ref[...], lam_ref[...]
        r, ig, sp, a, s = _lru_coeffs(xcv, ga_ref[...], gx_ref[...], lam)
        gel, dgel = _gelu_parts(y_ref[...])
        drec = dr_ref[...]
        hv = h_ref[...]
        dy_ref[...] = drec * hv * dgel
        a_buf[...] = a
        g_buf[...] = drec * gel

        def step(j, q):
            t = tt - 1 - j
            g = g_buf[pl.ds(t, 1), :] + q
            g_buf[pl.ds(t, 1), :] = g
            return a_buf[pl.ds(t, 1), :] * g

        carry[0:1, :] = lax.fori_loop(0, tt, step, carry[0:1, :], unroll=8)
        g = g_buf[...]
        halo = jnp.where(i < nt - 1, halo_ref[...], 0.0)
        hprev = pltpu.roll(jnp.concatenate([halo, hv], axis=0), 1, axis=0)[8:, :]
        da = g * hprev
        dig = g * s * xcv
        ds = g * ig * xcv
        dla = da * a - ds * (a * a) / s
        dga_ref[...] = dla * (-LRU_C * sp) * r * (1.0 - r)
        dgx_ref[...] = dig * ig * (1.0 - ig)
        dxc_ref[...] = g * s * ig
        dlam_ref[...] += jnp.sum(dla * r, axis=0, keepdims=True) * (LRU_C * _sigmoid(-lam))

    row = BS((tt, C), lambda i: (nt - 1 - i, 0))
    vec = BS((1, C), lambda i: (0, 0))
    halo = BS((8, C), lambda i: (jnp.maximum((nt - 1 - i) * per - 1, 0), 0))
    return _pcall(body, name=name, grid=(nt,), in_specs=[row, row, row, row, vec, row, halo, row],
                  out_specs=[row, row, row, row, vec], out_shape=[SDS((T, C), F32)] * 4 + [SDS((1, C), F32)],
                  scratch_shapes=[pltpu.VMEM((tt, C), F32), pltpu.VMEM((tt, C), F32), pltpu.VMEM((8, C), F32)],
                  compiler_params=_cparams(("arbitrary",)))(xc, ga, gx, y, lam.reshape(1, C), h, h, drec)


@functools.partial(jax.custom_vjp, nondiff_argnums=(5,))
def lru_scan(xc, ga, gx, y, lam, name):
    return _scan_fwd_call(xc, ga, gx, y, lam, name)[1]


def _lru_scan_fwd(xc, ga, gx, y, lam, name):
    h, rec = _scan_fwd_call(xc, ga, gx, y, lam, name)
    return rec, (xc, ga, gx, y, lam, h)


def _lru_scan_bwd(name, res, drec):
    xc, ga, gx, y, lam, h = res
    dga, dgx, dxc, dy, dlam = _scan_bwd_call(xc, ga, gx, y, lam, h, drec, name + "_bwd")
    return dxc, dga, dgx, dy, dlam.reshape(lam.shape)


lru_scan.defvjp(_lru_scan_fwd, _lru_scan_bwd)


def _att_batch(d, shared=False):
    if shared:
        return 8, 1
    return (4, 1) if d == 1 else (1, min(d, 8))


def _att_masks(n, max_dist):
    qi = lax.broadcasted_iota(jnp.int32, (1, 2 * ATT_BLOCK, 2 * ATT_BLOCK), 1) & (ATT_BLOCK - 1)
    kj = lax.broadcasted_iota(jnp.int32, (1, 2 * ATT_BLOCK, 2 * ATT_BLOCK), 2)
    prev = (kj < ATT_BLOCK) & (kj >= qi + (ATT_BLOCK - max_dist)) & (n > 0)
    cur = (kj >= ATT_BLOCK) & (kj - ATT_BLOCK <= qi)
    return prev | cur


def _lo_lanes(rows):
    return lax.broadcasted_iota(jnp.int32, (rows, LANES), 1) < HEAD_DIM


def _lane_half(rows):
    return lax.broadcasted_iota(jnp.int32, (rows, LANES), 1) // HEAD_DIM


def _stack_heads(x2):
    lo = _lo_lanes(ATT_BLOCK)
    zero = jnp.zeros_like(x2)
    return jnp.concatenate([jnp.where(lo, x2, zero), jnp.where(lo, zero, x2)], axis=0)


def _unstack_heads(y):
    return jnp.where(_lo_lanes(ATT_BLOCK), y[:ATT_BLOCK], y[ATT_BLOCK:])


def _per_head_col(x2):
    return jnp.concatenate([x2[:, 0:1], x2[:, HEAD_DIM:HEAD_DIM + 1]], axis=0)


def _head_sums(x2):
    lo = _lo_lanes(ATT_BLOCK)
    return jnp.concatenate([jnp.sum(jnp.where(lo, x2, 0.0), axis=1, keepdims=True),
                            jnp.sum(jnp.where(lo, 0.0, x2), axis=1, keepdims=True)], axis=0)


def _att_specs(d, Wq, Wk, nb, clamp):
    shared = Wk != Wq
    cgw, sb = _att_batch(d, shared)
    cur = (lambda n: jnp.minimum(n, nb - 1)) if clamp else (lambda n: n)
    rows, qw, kw = ATT_BLOCK * d, cgw * LANES, (LANES if shared else cgw * LANES)
    kcol = (lambda g: 0) if shared else (lambda g: g)
    qsp = BS((rows, qw), lambda g, n: (cur(n), g))
    csp = BS((rows, kw), lambda g, n: (cur(n), kcol(g)))
    psp = BS((rows, kw), lambda g, n: (jnp.maximum(cur(n) - 1, 0), kcol(g)))
    return cgw, sb, shared, qsp, csp, psp, qw, kw


def _att_streams(d, sb, work):
    if d == 1:
        work([slice(None)])
        return

    def one(j, carry):
        work([pl.ds(j * sb + i, ATT_BLOCK, stride=d) for i in range(sb)])
        return carry

    lax.fori_loop(0, d // sb, one, 0)


def _att_problem_loads(rows, cgw, shared, g, q_ref, kc_ref, kp_ref, vc_ref, vp_ref, sk_ref):
    half = _lane_half(ATT_BLOCK)

    def kv(ref, r, p):
        x = ref[r, :]
        if not shared:
            return x[:, p * LANES:(p + 1) * LANES]
        return jnp.where(half == p // 4, x, pltpu.roll(x, HEAD_DIM, axis=1))

    qs, kb, vb, sk = [], [], [], []
    for r in rows:
        qrow = q_ref[r, :]
        for p in range(cgw):
            cols = slice(p * LANES, (p + 1) * LANES)
            qs.append(_stack_heads(qrow[:, cols].astype(BF16)))
            kb.append(jnp.concatenate([kv(kp_ref, r, p), kv(kc_ref, r, p)], axis=0).astype(BF16))
            vb.append(jnp.concatenate([kv(vp_ref, r, p), kv(vc_ref, r, p)], axis=0).astype(BF16))
            sk.append(_per_head_col(jnp.broadcast_to(sk_ref[:, cols], (ATT_BLOCK, LANES))))
    return jnp.stack(qs), jnp.stack(kb), jnp.stack(vb), jnp.stack(sk)


_BDOT_NT = (((2,), (2,)), ((0,), (0,)))
_BDOT_NN = (((2,), (1,)), ((0,), (0,)))
_BDOT_TN = (((1,), (1,)), ((0,), (0,)))


def _att_fwd_call(q, k, v, sinks, d, max_dist, name):
    T, Wq = q.shape
    Wk = k.shape[1]
    nb = T // (d * ATT_BLOCK)
    cgw, sb, shared, qsp, csp, psp, qw, kw = _att_specs(d, Wq, Wk, nb, False)
    G = Wq // qw
    assert not shared or (d == 1 and Wk == LANES and G == 1 and cgw == 8), "a shared kv pair serves 2 x 8 query heads"

    def body(q_ref, kc_ref, kp_ref, vc_ref, vp_ref, sk_ref, o_ref, lse_ref):
        g, n = pl.program_id(0), pl.program_id(1)

        def work(rows):
            qs, kband, vband, sk = _att_problem_loads(rows, cgw, shared, g, q_ref, kc_ref, kp_ref, vc_ref, vp_ref, sk_ref)
            s = lax.dot_general(qs, kband, _BDOT_NT, preferred_element_type=F32) * (HEAD_DIM ** -0.5)
            s = jnp.where(_att_masks(n, max_dist), s, NEG)
            m = jnp.maximum(jnp.max(s, axis=-1, keepdims=True), sk)
            e = jnp.exp(s - m)
            den = jnp.sum(e, axis=-1, keepdims=True) + jnp.exp(sk - m)
            o = lax.dot_general((e * (1.0 / den)).astype(BF16), vband, _BDOT_NN, preferred_element_type=F32)
            lse = jnp.broadcast_to(m + jnp.log(den), o.shape)
            for i, r in enumerate(rows):
                o_ref[r, :] = jnp.concatenate([_unstack_heads(o[i * cgw + p]) for p in range(cgw)], axis=1)
                lse_ref[r, :] = jnp.concatenate([_unstack_heads(lse[i * cgw + p]) for p in range(cgw)], axis=1)

        _att_streams(d, sb, work)

    sksp = BS((1, qw), lambda g, n: (0, g))
    return _pcall(body, name=name, grid=(G, nb), in_specs=[qsp, csp, psp, csp, psp, sksp], out_specs=[qsp, qsp],
                  out_shape=[SDS((T, Wq), F32)] * 2,
                  compiler_params=_cparams(("parallel", "parallel")))(q, k, k, v, v, sinks)


def _att_bwd_call(q, k, v, sinks, o, lse, do, dlse, d, max_dist, name):
    T, Wq = q.shape
    Wk = k.shape[1]
    nb = T // (d * ATT_BLOCK)
    cgw, sb, shared, qsp, csp, psp, qw, kw = _att_specs(d, Wq, Wk, nb, True)
    G = Wq // qw
    scale = HEAD_DIM ** -0.5

    def body(*refs):
        q_ref, kc_ref, kp_ref, vc_ref, vp_ref, sk_ref, o_ref, lse_ref, do_ref = refs[:9]
        dlse_ref = refs[9] if dlse is not None else None
        dq_ref, dk_ref, dv_ref, dsk_ref, ck, cv = refs[-6:]
        g, n = pl.program_id(0), pl.program_id(1)

        @pl.when(n == 0)
        def _():
            ck[...] = jnp.zeros_like(ck)
            cv[...] = jnp.zeros_like(cv)
            dsk_ref[...] = jnp.zeros_like(dsk_ref)

        def work(rows):
            qs, kband, vband, sk = _att_problem_loads(rows, cgw, shared, g, q_ref, kc_ref, kp_ref, vc_ref, vp_ref, sk_ref)
            dos, lse_c, corr = [], [], []
            for r in rows:
                do_r, o_r, lse_r = do_ref[r, :], o_ref[r, :], lse_ref[r, :]
                dlse_r = dlse_ref[r, :] if dlse_ref is not None else None
                for p in range(cgw):
                    cols = slice(p * LANES, (p + 1) * LANES)
                    dos.append(_stack_heads(do_r[:, cols].astype(BF16)))
                    lse_c.append(_per_head_col(lse_r[:, cols]))
                    delta = _head_sums(do_r[:, cols] * o_r[:, cols])
                    corr.append(-delta if dlse_r is None else _head_sums(dlse_r[:, cols]) - delta)
            dos, lse_c, corr = jnp.stack(dos), jnp.stack(lse_c), jnp.stack(corr)
            s = lax.dot_general(qs, kband, _BDOT_NT, preferred_element_type=F32) * scale
            pr = jnp.exp(jnp.where(_att_masks(n, max_dist), s, NEG) - lse_c)
            dp = lax.dot_general(dos, vband, _BDOT_NT, preferred_element_type=F32)
            ds = (pr * (dp + corr)).astype(BF16)
            dq = lax.dot_general(ds, kband, _BDOT_NN, preferred_element_type=F32) * scale
            dkb = lax.dot_general(ds, qs, _BDOT_TN, preferred_element_type=F32) * scale
            dvb = lax.dot_general(pr.astype(BF16), dos, _BDOT_TN, preferred_element_type=F32)
            dsk = jnp.exp(sk - lse_c) * corr
            lane = lax.broadcasted_iota(jnp.int32, (8, LANES), 1)
            for p in range(cgw):
                tot = [jnp.sum(jnp.stack([dsk[i * cgw + p, h * ATT_BLOCK:(h + 1) * ATT_BLOCK] for i in range(len(rows))]),
                               axis=(0, 1)).reshape(1, 1) for h in range(2)]
                dsk_ref[:, p * LANES:(p + 1) * LANES] += jnp.where(lane == 0, tot[0], jnp.where(lane == HEAD_DIM, tot[1], 0.0))

            def gather_pairs(parts):
                if not shared:
                    return jnp.concatenate(parts, axis=1)
                tot = [parts[4 * h] + parts[4 * h + 1] + parts[4 * h + 2] + parts[4 * h + 3] for h in range(2)]
                tot = [t + pltpu.roll(t, HEAD_DIM, axis=1) for t in tot]
                return jnp.where(_lo_lanes(ATT_BLOCK), tot[0], tot[1])

            for i, r in enumerate(rows):
                mine = range(i * cgw, (i + 1) * cgw)
                dq_ref[r, :] = jnp.concatenate([_unstack_heads(dq[b]) for b in mine], axis=1)
                dk_ref[r, :] = ck[r, :] + gather_pairs([dkb[b, :ATT_BLOCK] for b in mine])
                dv_ref[r, :] = cv[r, :] + gather_pairs([dvb[b, :ATT_BLOCK] for b in mine])
                ck[r, :] = gather_pairs([dkb[b, ATT_BLOCK:] for b in mine])
                cv[r, :] = gather_pairs([dvb[b, ATT_BLOCK:] for b in mine])

        @pl.when(n < nb)
        def _():
            _att_streams(d, sb, work)

        @pl.when(n == nb)
        def _():
            dk_ref[...] = ck[...]
            dv_ref[...] = cv[...]

    sksp = BS((1, qw), lambda g, n: (0, g))
    rows = ATT_BLOCK * d
    osp = BS((rows, kw), lambda g, n: (jnp.maximum(n - 1, 0), 0 if shared else g))
    kshape = SDS((T, Wk), F32)
    dq, dk, dv, dsk = _pcall(
        body, name=name, grid=(G, nb + 1),
        in_specs=[qsp, csp, psp, csp, psp, sksp, qsp, qsp, qsp] + ([qsp] if dlse is not None else []),
        out_specs=[qsp, osp, osp, BS((8, qw), lambda g, n: (0, g))],
        out_shape=[SDS((T, Wq), F32), kshape, kshape, SDS((8, Wq), F32)],
        scratch_shapes=[pltpu.VMEM((rows, kw), F32)] * 2,
        compiler_params=_cparams(("parallel", "arbitrary")))(q, k, k, v, v, sinks, o, lse, do,
                                                              *([dlse] if dlse is not None else []))
    return dq, dk, dv, dsk[0:1]


@functools.partial(jax.custom_vjp, nondiff_argnums=(4, 5, 6, 7))
def band_attention(q, k, v, sinks, d, max_dist, with_lse, name):
    return _band_attention_fwd(q, k, v, sinks, d, max_dist, with_lse, name)[0]


def _band_attention_fwd(q, k, v, sinks, d, max_dist, with_lse, name):
    o, lse = _att_fwd_call(q, k, v, sinks, d, max_dist, name)
    return ((o, lse) if with_lse else o), (q, k, v, sinks, o, lse)


def _band_attention_bwd(d, max_dist, with_lse, name, res, g):
    q, k, v, sinks, o, lse = res
    do, dlse = g if with_lse else (g, None)
    return _att_bwd_call(q, k, v, sinks, o, lse, do, dlse, d, max_dist, name + "_bwd")


band_attention.defvjp(_band_attention_fwd, _band_attention_bwd)


def _merge_weights(ls):
    mx = jnp.maximum(jnp.maximum(ls[0], ls[1]), ls[2])
    es = [jnp.exp(l - mx) for l in ls]
    inv = 1.0 / (es[0] + es[1] + es[2])
    return [e * inv for e in es]


def _merge_fwd_call(os_, ls_, name):
    T, W = os_[0].shape
    tt = _tile(T, 1024, 8)

    def body(o1, o2, o3, l1, l2, l3, out_ref):
        w = _merge_weights([l1[...], l2[...], l3[...]])
        out_ref[...] = w[0] * o1[...] + w[1] * o2[...] + w[2] * o3[...]

    row = BS((tt, W), lambda i: (i, 0))
    return _pcall(body, name=name, grid=(T // tt,), in_specs=[row] * 6, out_specs=row,
                  out_shape=SDS((T, W), F32), compiler_params=_cparams(("parallel",)))(*os_, *ls_)


def _merge_bwd_call(os_, ls_, do, name):
    T, W = os_[0].shape
    tt = _tile(T, 512, 8)

    def body(o1, o2, o3, l1, l2, l3, do_ref, d1, d2, d3, e1, e2, e3):
        w = _merge_weights([l1[...], l2[...], l3[...]])
        dov = do_ref[...]
        ts = [dov * o[...] for o in (o1, o2, o3)]
        mean = w[0] * ts[0] + w[1] * ts[1] + w[2] * ts[2]
        for wi, ti, dref, eref in zip(w, ts, (d1, d2, d3), (e1, e2, e3)):
            dref[...] = wi * dov
            eref[...] = wi * (ti - mean)

    row = BS((tt, W), lambda i: (i, 0))
    return _pcall(body, name=name, grid=(T // tt,), in_specs=[row] * 7, out_specs=[row] * 6,
                  out_shape=[SDS((T, W), F32)] * 6, compiler_params=_cparams(("parallel",)))(*os_, *ls_, do)


@functools.partial(jax.custom_vjp, nondiff_argnums=(2,))
def merge3(os_, ls_, name):
    return _merge_fwd_call(os_, ls_, name)


def _merge3_fwd(os_, ls_, name):
    return _merge_fwd_call(os_, ls_, name), (os_, ls_)


def _merge3_bwd(name, res, do):
    os_, ls_ = res
    out = _merge_bwd_call(os_, ls_, do, name + "_bwd")
    return tuple(out[:3]), tuple(out[3:])


merge3.defvjp(_merge3_fwd, _merge3_bwd)


def _xa_probs(qb, kb, scale):
    s = lax.dot_general(qb, kb, (((1,), (1,)), ((), ())), preferred_element_type=F32) * scale
    e = jnp.exp(s - jnp.max(s, axis=-1, keepdims=True))
    return e / jnp.sum(e, axis=-1, keepdims=True)


def _xa_fwd_call(q, kv, name):
    T, W = q.shape
    M = kv.shape[0]
    hd = XA_HEAD_DIM
    tq = _tile(T, 4096, 8)
    scale = hd ** -0.5

    def body(q_ref, k_ref, v_ref, o_ref):
        p = _xa_probs(q_ref[...].astype(BF16), k_ref[...].astype(BF16), scale)
        o_ref[...] = jnp.dot(p.astype(BF16), v_ref[...].astype(BF16), preferred_element_type=F32)

    qsp = BS((tq, hd), lambda i, h: (i, h))
    return _pcall(body, name=name, grid=(T // tq, XA_HEADS),
                  in_specs=[qsp, BS((M, hd), lambda i, h: (0, h)), BS((M, hd), lambda i, h: (0, XA_HEADS + h))],
                  out_specs=qsp, out_shape=SDS((T, W), F32),
                  compiler_params=_cparams(("parallel", "parallel")))(q, kv, kv)


def _xa_bwd_call(q, kv, do, name):
    T, W = q.shape
    M = kv.shape[0]
    hd = XA_HEAD_DIM
    tq = _tile(T, 4096, 8)
    scale = hd ** -0.5
    dn_nt = (((1,), (1,)), ((), ()))
    dn_tn = (((0,), (0,)), ((), ()))

    def body(q_ref, k_ref, v_ref, do_ref, dq_ref, dk_ref, dv_ref):
        @pl.when(pl.program_id(1) == 0)
        def _():
            dk_ref[...] = jnp.zeros_like(dk_ref)
            dv_ref[...] = jnp.zeros_like(dv_ref)

        qb, kb, vb = q_ref[...].astype(BF16), k_ref[...].astype(BF16), v_ref[...].astype(BF16)
        p = _xa_probs(qb, kb, scale)
        dob = do_ref[...].astype(BF16)
        dp = lax.dot_general(dob, vb, dn_nt, preferred_element_type=F32)
        ds = (p * (dp - jnp.sum(p * dp, axis=-1, keepdims=True))).astype(BF16)
        dq_ref[...] = jnp.dot(ds, kb, preferred_element_type=F32) * scale
        dk_ref[...] += lax.dot_general(ds, qb, dn_tn, preferred_element_type=F32) * scale
        dv_ref[...] += lax.dot_general(p.astype(BF16), dob, dn_tn, preferred_element_type=F32)

    qsp = BS((tq, hd), lambda h, i: (i, h))
    ksp = BS((M, hd), lambda h, i: (0, h))
    return _pcall(body, name=name, grid=(XA_HEADS, T // tq),
                  in_specs=[qsp, ksp, BS((M, hd), lambda h, i: (0, XA_HEADS + h)), qsp],
                  out_specs=[qsp, ksp, ksp], out_shape=[SDS((T, W), F32), SDS((M, W), F32), SDS((M, W), F32)],
                  compiler_params=_cparams(("parallel", "arbitrary")))(q, kv, kv, do)


@functools.partial(jax.custom_vjp, nondiff_argnums=(2,))
def cross_attention(q, kv, name):
    return _xa_fwd_call(q, kv, name)


def _cross_attention_fwd(q, kv, name):
    return _xa_fwd_call(q, kv, name), (q, kv)


def _cross_attention_bwd(name, res, do):
    q, kv = res
    dq, dk, dv = _xa_bwd_call(q, kv, do, name + "_bwd")
    return dq, jnp.concatenate([dk, dv], axis=1)


cross_attention.defvjp(_cross_attention_fwd, _cross_attention_bwd)


def _gate_up_swiglu_call(hn, w1t, name):
    T, K = hn.shape
    F = w1t.shape[0] // 2
    tm, tn = _tile(T, 4096), _tile(F, 256)
    nj = F // tn
    dn = (((1,), (1,)), ((), ()))

    def body(a_ref, wg_ref, wu_ref, g_ref, u_ref, act_ref):
        a = a_ref[...]
        g = lax.dot_general(a, wg_ref[...], dn, preferred_element_type=F32)
        u = lax.dot_general(a, wu_ref[...], dn, preferred_element_type=F32)
        g_ref[...] = g
        u_ref[...] = u
        act_ref[...] = ((g * _sigmoid(g)) * u).astype(BF16)

    tile = BS((tm, tn), lambda i, j: (i, j))
    return _pcall(body, name=name, grid=(T // tm, nj),
                  in_specs=[BS((tm, K), lambda i, j: (i, 0)), BS((tn, K), lambda i, j: (j, 0)),
                            BS((tn, K), lambda i, j: (j + nj, 0))],
                  out_specs=[tile, tile, tile], out_shape=[SDS((T, F), F32), SDS((T, F), F32), SDS((T, F), BF16)],
                  compiler_params=_cparams(("parallel", "parallel")))(hn, w1t, w1t)


def _down_da_swiglu_bwd_call(dout, w2, g, u, name):
    T, F = g.shape
    Dm = dout.shape[1]
    tm, tn = _tile(T, 2048), _tile(F, 256)

    def body(d_ref, w_ref, g_ref, u_ref, o_ref):
        d = lax.dot_general(d_ref[...].astype(BF16), w_ref[...], (((1,), (1,)), ((), ())), preferred_element_type=F32)
        g = g_ref[...]
        sg = _sigmoid(g)
        o_ref[0] = (d * u_ref[...] * (sg * (1.0 + g * (1.0 - sg)))).astype(BF16)
        o_ref[1] = (d * (g * sg)).astype(BF16)

    tile = BS((tm, tn), lambda i, j: (i, j))
    return _pcall(body, name=name, grid=(T // tm, F // tn),
                  in_specs=[BS((tm, Dm), lambda i, j: (i, 0)), BS((tn, Dm), lambda i, j: (j, 0)), tile, tile],
                  out_specs=BS((2, tm, tn), lambda i, j: (0, i, j)), out_shape=SDS((2, T, F), BF16),
                  compiler_params=_cparams(("parallel", "parallel")))(dout, w2, g, u)


@functools.partial(jax.custom_vjp, nondiff_argnums=(6,))
def ffn_block(h, g, w1b, w1c, w2b, w2c, name):
    return _ffn_fwd(h, g, w1b, w1c, w2b, w2c, name)[0]


def _ffn_fwd(h, g, w1b, w1c, w2b, w2c, name):
    hn = _rms_fwd_call(h, g, name + "_norm", BF16)
    gate, up, act = _gate_up_swiglu_call(hn, w1b, name + "_gu")
    out = _mm(act, w2b, add=h, name=name + "_down", tm=1024, tn=1024, tk=2816)
    return out, (h, g, hn, gate, up, act, w1b, w2b)


def _ffn_bwd(name, res, dout):
    h, g, hn, gate, up, act, w1b, w2b = res
    dw2 = _mm(act, dout, ta=True, name=name + "_down_dw", tm=1408, tn=1024, tk=1024)
    dgu = _down_da_swiglu_bwd_call(dout, w2b, gate, up, name + "_down_da")
    dw1 = _mm(dgu, hn, ta=True, name=name + "_gu_dw", tm=1408, tn=1024, tk=2048)
    dh, dg = _mm_rms_bwd(dgu, w1b, h, g, dout, name + "_gu_da", wt=True)
    return dh, dg.reshape(g.shape), jnp.zeros_like(w1b), dw1, jnp.zeros_like(w2b), dw2


ffn_block.defvjp(_ffn_fwd, _ffn_bwd)


def _final_call(h, g, target, name):
    T, Dm = h.shape
    tt = _tile(T, 1024, 8)

    def body(x_ref, g_ref, t_ref, loss_ref, dx_ref, dg_ref):
        @pl.when(pl.program_id(0) == 0)
        def _():
            loss_ref[...] = jnp.zeros_like(loss_ref)
            dg_ref[...] = jnp.zeros_like(dg_ref)

        xv, gv = x_ref[...], g_ref[...]
        r = lax.rsqrt(jnp.mean(xv * xv, axis=-1, keepdims=True) + NORM_EPS)
        xh = xv * r
        err = xh * gv - t_ref[...]
        loss_ref[...] += 0.5 * jnp.sum(jnp.mean(err * err, axis=-1, keepdims=True), axis=0, keepdims=True)
        dy = err * (1.0 / Dm)
        dyg = dy * gv
        dx_ref[...] = r * (dyg - xh * jnp.mean(dyg * xh, axis=-1, keepdims=True))
        dg_ref[...] += jnp.sum(dy * xh, axis=0, keepdims=True)

    row = BS((tt, Dm), lambda i: (i, 0))
    vec = BS((1, Dm), lambda i: (0, 0))
    return _pcall(body, name=name, grid=(T // tt,), in_specs=[row, vec, row],
                  out_specs=[BS((1, 1), lambda i: (0, 0)), row, vec],
                  out_shape=[SDS((1, 1), F32), SDS((T, Dm), F32), SDS((1, Dm), F32)],
                  compiler_params=_cparams(("arbitrary",)))(h, g.reshape(1, Dm), target)


ADAMW_BLOCK_ELEMS = 256 * 1024


def _adamw_call(parts, w, m, v, name):
    shape = w.shape
    if not isinstance(parts, (list, tuple)):
        parts, shape3 = [parts], (1,) + shape
    else:
        shape3 = shape
    n_lead = shape3[0]
    r, N = shape3[-2], shape3[-1]
    Ld = math.prod(shape3[1:-2])
    w, m, v = (t.reshape(n_lead * Ld, r, N) for t in (w, m, v))
    tr = _tile(r, max(8, ADAMW_BLOCK_ELEMS // N), 8)
    c1 = 1.0 - ADAM_B1 ** ADAM_STEP
    c2 = 1.0 - ADAM_B2 ** ADAM_STEP
    outs = None
    for lead, p in enumerate(parts):
        def body(p_ref, w_ref, m_ref, v_ref, *rest):
            g_ref, d_ref, nm_ref, nv_ref = rest[-4:]
            g = p_ref[0]
            for j in range(1, N_DEV):
                g = g + p_ref[j]
            nm = ADAM_B1 * m_ref[...] + (1.0 - ADAM_B1) * g
            nv = ADAM_B2 * v_ref[...] + (1.0 - ADAM_B2) * (g * g)
            g_ref[...] = g
            nm_ref[...] = nm
            nv_ref[...] = nv
            d_ref[...] = -ADAM_LR * ((nm / c1) / (jnp.sqrt(nv / c2) + ADAM_EPS) + ADAM_WD * w_ref[...])

        base = lead * Ld
        row = BS((1, tr, N), lambda l, i, base=base: (base + l, i, 0))
        prev = [] if outs is None else list(outs)
        outs = _pcall(body, name=f"{name}_{lead}", grid=(Ld, r // tr),
                      in_specs=[BS((N_DEV, 1, tr, N), lambda l, i: (0, l, i, 0)), row, row, row]
                      + [BS(memory_space=pl.ANY)] * len(prev),
                      out_specs=[row] * 4, out_shape=[SDS((n_lead * Ld, r, N), F32)] * 4,
                      input_output_aliases={4 + j: j for j in range(len(prev))},
                      compiler_params=_cparams(("parallel", "parallel")))(p.reshape(N_DEV, Ld, r, N), w, m, v, *prev)
    return [t.reshape(shape) for t in outs]


def _place():
    return lax.axis_index("x"), lax.axis_index("y"), lax.axis_index("c")


def _all_gather(xs, name):
    n = len(xs)
    pairs = [(i, l) for i, x in enumerate(xs) for l in range(x.shape[0])]

    def body(*refs):
        x_refs, o_refs = refs[:n], refs[n:2 * n]
        send_sems, recv_sems, local_sems = refs[2 * n:]
        x_, y_, c_ = _place()
        me, sibling = (x_, y_, c_), (x_, y_, 1 - c_)
        chips = [(1 - x_, y_), (x_, 1 - y_), (1 - x_, 1 - y_)]

        def copy(e, k, block, to, from_input=False):
            i, l = pairs[e]
            px, py, pc = block
            dst = o_refs[i].at[l, 4 * px + 2 * py + pc]
            return pltpu.make_async_remote_copy(
                src_ref=x_refs[i].at[l] if from_input else dst, dst_ref=dst,
                send_sem=send_sems.at[7 * e + k], recv_sem=recv_sems.at[7 * e + k],
                device_id=to, device_id_type=pl.DeviceIdType.MESH)

        every = range(len(pairs))
        mine = [pltpu.make_async_copy(x_refs[i].at[l], o_refs[i].at[l, 4 * x_ + 2 * y_ + c_], local_sems.at[e])
                for e, (i, l) in enumerate(pairs)]
        for cp in mine:
            cp.start()
        first = [copy(e, 0, me, sibling, True) for e in every]
        first += [copy(e, 1 + j, me, (*chip, c_), True) for j, chip in enumerate(chips) for e in every]
        for cp in first:
            cp.start()
        passed = []
        for j, chip in enumerate(chips):
            for e in every:
                copy(e, 1 + j, (*chip, c_), me).wait_recv()
            for e in every:
                cp = copy(e, 4 + j, (*chip, c_), sibling)
                cp.start()
                passed.append(cp)
        for e in every:
            copy(e, 0, sibling, me).wait_recv()
        for j, chip in enumerate(chips):
            for e in every:
                copy(e, 4 + j, (*chip, 1 - c_), me).wait_recv()
        for cp in first + passed:
            cp.wait_send()
        for cp in mine:
            cp.wait()

    any_spec = BS(memory_space=pl.ANY)
    return _pcall(body, name=name, in_specs=[any_spec] * n, out_specs=[any_spec] * n,
                  out_shape=[SDS((x.shape[0], N_DEV) + x.shape[1:], x.dtype) for x in xs],
                  scratch_shapes=[pltpu.SemaphoreType.DMA((7 * len(pairs),)), pltpu.SemaphoreType.DMA((7 * len(pairs),)),
                                  pltpu.SemaphoreType.DMA((len(pairs),))],
                  compiler_params=pltpu.CompilerParams(has_side_effects=True))(*xs)


def _peer_of(k, place):
    x_, y_, c_ = place
    fx, fy, fc = (k >> 2) & 1, (k >> 1) & 1, k & 1
    return fx + x_ - 2 * fx * x_, fy + y_ - 2 * fy * y_, fc + c_ - 2 * fc * c_


def _split_copy(src_ref, land_ref, send_sems, recv_sems, e, k, place, scatter):
    x_, y_, c_ = place
    px, py, pc = _peer_of(k, place)
    return pltpu.make_async_remote_copy(
        src_ref=src_ref.at[4 * px + 2 * py + pc] if scatter else src_ref, dst_ref=land_ref.at[4 * x_ + 2 * y_ + c_],
        send_sem=send_sems.at[7 * e + k - 1], recv_sem=recv_sems.at[7 * e + k - 1],
        device_id=(px, py, pc), device_id_type=pl.DeviceIdType.MESH)


def _own_copy(src_ref, land_ref, sems, slot, place, scatter):
    x_, y_, c_ = place
    me = 4 * x_ + 2 * y_ + c_
    return pltpu.make_async_copy(src_ref.at[me] if scatter else src_ref, land_ref.at[me], sems.at[slot])


_HBM_SPEC = BS(memory_space=pltpu.HBM)
_SEM_SPEC = BS(memory_space=pltpu.SEMAPHORE)
_EFFECT = pltpu.SideEffectType.DATAFLOW_SIDE_EFFECTING


def _copies_start(srcs, scatter, name, thru=None):
    n = len(srcs)
    lands = [lax.empty(s.shape if scatter else (N_DEV,) + s.shape, s.dtype) for s in srcs]
    passed = srcs + lands + list(thru or ())

    def body(*refs):
        src_refs, land_refs = refs[:n], refs[n:2 * n]
        send_sems, recv_sems = refs[len(passed)], refs[len(passed) + 1]
        token = refs[-1]
        place = _place()
        for e in range(n):
            for k in range(1, N_DEV):
                _split_copy(src_refs[e], land_refs[e], send_sems, recv_sems, e, k, place, scatter).start()
            _own_copy(src_refs[e], land_refs[e], send_sems, 7 * n + e, place, scatter).start()
        token[...] = jnp.zeros_like(token)

    hbm = lambda t: pltpu.with_memory_space_constraint(t, pltpu.HBM)
    out = _pcall(body, name=name,
                 out_shape=(pltpu.SemaphoreType.DMA((8 * n,)), pltpu.SemaphoreType.DMA((7 * n,)),
                            *[pltpu.HBM(t.shape, t.dtype) for t in passed], SDS((8, LANES), F32)),
                 in_specs=[_HBM_SPEC] * len(passed),
                 out_specs=(_SEM_SPEC, _SEM_SPEC, *[_HBM_SPEC] * len(passed), BS(memory_space=pltpu.VMEM)),
                 input_output_aliases={i: 2 + i for i in range(len(passed))},
                 compiler_params=pltpu.CompilerParams(has_side_effects=_EFFECT))(*[hbm(t) for t in passed])
    return out[0], out[1], list(out[2:2 + n]), list(out[2 + n:2 + 2 * n]), list(out[2 + 2 * n:-1])


def _copies_wait(started, which, scatter, after, name):
    send_sems, recv_sems, srcs, lands, _ = started
    n, n_started = len(which), len(srcs)
    after = list(after) if isinstance(after, (list, tuple)) else [after]

    def body(*refs):
        src_refs, land_refs = refs[:n], refs[n:2 * n]
        send_s, recv_s = refs[2 * n], refs[2 * n + 1]
        place = _place()
        for j, e in enumerate(which):
            for k in range(1, N_DEV):
                cp = _split_copy(src_refs[j], land_refs[j], send_s, recv_s, e, k, place, scatter)
                cp.wait_send()
                cp.wait_recv()
            _own_copy(src_refs[j], land_refs[j], send_s, 7 * n_started + e, place, scatter).wait()

    args = [srcs[e] for e in which] + [lands[e] for e in which]
    out = _pcall(body, name=name, out_shape=tuple(pltpu.HBM(t.shape, t.dtype) for t in args),
                 in_specs=[_HBM_SPEC] * (2 * n) + [_SEM_SPEC, _SEM_SPEC] + [BS(memory_space=pl.ANY)] * len(after),
                 out_specs=tuple([_HBM_SPEC] * (2 * n)), input_output_aliases={i: i for i in range(2 * n)},
                 compiler_params=pltpu.CompilerParams(has_side_effects=_EFFECT))(*args, send_sems, recv_sems, *after)
    return list(out[:n]), list(out[n:])


def _pad_flat(t, quantum=PACK_QUANTUM):
    f = t.reshape(-1)
    pad = (-f.shape[0]) % quantum
    return jnp.pad(f, (0, pad)) if pad else f


def _pack(arrs, dtype):
    return jnp.concatenate([_pad_flat(a.astype(dtype)) for a in arrs]).reshape(-1, LANES)


def _unpack(buf, shapes, lead=()):
    flat = buf.reshape(lead + (-1,))
    out, off = [], 0
    for s in shapes:
        n = math.prod(s)
        out.append(flat[..., off:off + n].reshape(lead + tuple(s)))
        off += n + (-n) % PACK_QUANTUM
    return out


def _full_from_gathered(g, axis):
    t = jnp.moveaxis(g, 0, axis)
    s = t.shape
    return t.reshape(s[:axis] + (s[axis] * s[axis + 1],) + s[axis + 2:])


def _parts_from_full(t, axis):
    s = t.shape
    t = t.reshape(s[:axis] + (N_DEV, s[axis] // N_DEV) + s[axis + 1:])
    return jnp.moveaxis(t, axis, 0)


def _head_rows(t):
    return jnp.repeat(t, HEAD_DIM).reshape(1, -1)


def _dilated_attention(q, k, v, name):
    no_sink = jnp.full((1, q.shape[1]), NEG, F32)
    outs, lses = zip(*[band_attention(q, k, v, no_sink, d, ATT_BLOCK, True, f"{name}_d{d}") for d in DILATIONS])
    return merge3(tuple(outs), tuple(lses), name + "_merge")


STAGES = (
    ("proj0", ('mix_norm',), (('ab_w_in', 0),)),
    ("mixer0", ('lru_conv_w', 'lru_conv_b', 'lru_ba', 'lru_bx', 'lru_lambda'),
     (('lru_wa', 0), ('lru_wx', 0), ('ab_w_out', 0))),
    ("xa0", ('xa_norm', 'xa_mem_norm'), (('xa_wq', 0), ('xa_wkv', 0), ('xa_wo', 0))),
    ("ffn0", ('ffn_norm',), (('ffn_w_gate_up', 0), ('ffn_w_down', 0))),
    ("mixer1", ('mix_norm', 'c_b_qkv', 'c_sinks', 'c_b_out'), (('c_w_qkv', 0), ('c_w_out', 0))),
    ("xa1", ('xa_norm', 'xa_mem_norm'), (('xa_wq', 1), ('xa_wkv', 1), ('xa_wo', 1))),
    ("ffn1", ('ffn_norm',), (('ffn_w_gate_up', 1), ('ffn_w_down', 1))),
)


def _stage_fn(stage, Wb, tabs, mem):
    layer = int(stage[-1])
    L = f"l{layer}"

    def run(S, Cw, h):
        def lin(a, key, bias, add, name, rows=None):
            wb, wc = Wb[key], Cw[key]
            if rows is not None:
                wb, wc = wb[rows], wc[rows]
            return linear(a, wb, wc, bias, add, name)

        def norm_lin(a, gain, key, bias, name):
            return norm_linear(a, gain, Wb[key], Cw[key], bias, key[0] in COLUMN_CUT, name)

        if stage == "mixer0":
            h, x_br, y_br, q, k, v = h
            C = S['lru_conv_w'].shape[-1]
            xc = conv4(x_br, S['lru_conv_w'][0], S['lru_conv_b'][0], L + "_conv")
            ga, gx, xc = lru_gates(xc, Wb['lru_wa', 0], Cw['lru_wa', 0], S['lru_ba'][0],
                               Wb['lru_wx', 0], Cw['lru_wx', 0], S['lru_bx'][0], L + "_gates")
            rec = lru_scan(xc, ga, gx, y_br, S['lru_lambda'][0], L + "_scan")
            att = _dilated_attention(q, k, v, L + "_att")
            h = lin(att, ('ab_w_out', 0), None, h, L + "_w_out_att", slice(C, None))
            return lin(rec, ('ab_w_out', 0), None, h, L + "_w_out_rec", slice(0, C))
        if stage == "mixer1":
            qw = C_HEADS * HEAD_DIM
            kw = C_KV_HEADS * HEAD_DIM
            q, k, v, h = norm_linear_pieces(h, S['mix_norm'][1], Wb['c_w_qkv', 0], Cw['c_w_qkv', 0], S['c_b_qkv'][0], tabs,
                                            ((qw, True), (kw, True), (kw, False)), L + "_w_qkv")
            o = band_attention(q, k, v, _head_rows(S['c_sinks'][0]), 1, ATT_BLOCK - 1, False, L + "_att")
            return lin(o, ('c_w_out', 0), S['c_b_out'][0], h, L + "_w_out")
        if stage.startswith("xa"):
            xq, h = norm_lin(h, S['xa_norm'][layer], ('xa_wq', layer), None, L + "_xa_wq")
            xkv, _ = norm_lin(mem, S['xa_mem_norm'][layer], ('xa_wkv', layer), None, L + "_xa_wkv")
            return lin(cross_attention(xq, xkv, L + "_xa"), ('xa_wo', layer), None, h, L + "_xa_wo")
        gu, down = ('ffn_w_gate_up', layer), ('ffn_w_down', layer)
        return ffn_block(h, S['ffn_norm'][layer], Wb[gu], Cw[gu], Wb[down], Cw[down], L + "_ffn")

    return run


def kernel(x, mem, mix_norm, ab_w_in, lru_conv_w, lru_conv_b, lru_wa, lru_ba, lru_wx, lru_bx, lru_lambda, ab_w_out, c_w_qkv, c_b_qkv, c_sinks, c_w_out, c_b_out, xa_norm, xa_mem_norm, xa_wq, xa_wkv, xa_wo, ffn_norm, ffn_w_gate_up, ffn_w_down, final_norm, loss_target, m_mix_norm, m_ab_w_in, m_lru_conv_w, m_lru_conv_b, m_lru_wa, m_lru_ba, m_lru_wx, m_lru_bx, m_lru_lambda, m_ab_w_out, m_c_w_qkv, m_c_b_qkv, m_c_sinks, m_c_w_out, m_c_b_out, m_xa_norm, m_xa_mem_norm, m_xa_wq, m_xa_wkv, m_xa_wo, m_ffn_norm, m_ffn_w_gate_up, m_ffn_w_down, m_final_norm, v_mix_norm, v_ab_w_in, v_lru_conv_w, v_lru_conv_b, v_lru_wa, v_lru_ba, v_lru_wx, v_lru_bx, v_lru_lambda, v_ab_w_out, v_c_w_qkv, v_c_b_qkv, v_c_sinks, v_c_w_out, v_c_b_out, v_xa_norm, v_xa_mem_norm, v_xa_wq, v_xa_wkv, v_xa_wo, v_ffn_norm, v_ffn_w_gate_up, v_ffn_w_down, v_final_norm):
    w_loc = dict(zip(WEIGHT_NAMES, (mix_norm, ab_w_in, lru_conv_w, lru_conv_b, lru_wa, lru_ba, lru_wx, lru_bx, lru_lambda, ab_w_out, c_w_qkv, c_b_qkv, c_sinks, c_w_out, c_b_out, xa_norm, xa_mem_norm, xa_wq, xa_wkv, xa_wo, ffn_norm, ffn_w_gate_up, ffn_w_down, final_norm)))
    m_loc = dict(zip(WEIGHT_NAMES, (m_mix_norm, m_ab_w_in, m_lru_conv_w, m_lru_conv_b, m_lru_wa, m_lru_ba, m_lru_wx, m_lru_bx, m_lru_lambda, m_ab_w_out, m_c_w_qkv, m_c_b_qkv, m_c_sinks, m_c_w_out, m_c_b_out, m_xa_norm, m_xa_mem_norm, m_xa_wq, m_xa_wkv, m_xa_wo, m_ffn_norm, m_ffn_w_gate_up, m_ffn_w_down, m_final_norm)))
    v_loc = dict(zip(WEIGHT_NAMES, (v_mix_norm, v_ab_w_in, v_lru_conv_w, v_lru_conv_b, v_lru_wa, v_lru_ba, v_lru_wx, v_lru_bx, v_lru_lambda, v_ab_w_out, v_c_w_qkv, v_c_b_qkv, v_c_sinks, v_c_w_out, v_c_b_out, v_xa_norm, v_xa_mem_norm, v_xa_wq, v_xa_wkv, v_xa_wo, v_ffn_norm, v_ffn_w_gate_up, v_ffn_w_down, v_final_norm)))

    first_keys = list(STAGES[0][2])
    keys = [key for _, _, stage_keys in STAGES[1:] for key in stage_keys]
    shards = [_shard_view(n, w_loc[n])[l].astype(BF16) for n, l in keys]
    first_g = _all_gather([_pack([w_loc[n] for n in SMALL], F32)[None]]
                          + [_shard_view(n, w_loc[n])[l].astype(BF16)[None] for n, l in first_keys], "gather_first")
    gather = _copies_start(shards, False, "gather_start", thru=[first_g[0]])
    small_g = gather[4][0][0]
    Wb = {key: _full_from_gathered(g[0], _layer_shard_axis(key[0])) for key, g in zip(first_keys, first_g[1:])}
    S = {n: w_loc[n] for n in REPLICATED}
    for n, t in zip(SMALL, _unpack(small_g, [w_loc[n].shape for n in SMALL], lead=(N_DEV,))):
        S[n] = _full_from_gathered(t, SHARD_AXIS[n])

    tabs = _rope_tables(x.shape[1])
    w_in = Wb[first_keys[0]]
    hn0 = _rms_fwd_call(x[0], S['mix_norm'][0], "l0_w_in_norm", BF16)
    lru_w, att_w = N_DEV * lru_conv_w.shape[-1], B_HEADS * HEAD_DIM
    pieces = (("x", lru_w, False), ("y", lru_w, False), ("q", att_w, True), ("k", att_w, True), ("v", att_w, False))
    h, row = [x[0]], 0
    for piece, width, rotated in pieces:
        h.append(_mm(hn0, w_in[row:row + width], tb=True, rope=tabs if rotated else None, name="l0_w_in_" + piece,
                     tm=2048, tn=1024, tk=1024))
        row += width
    h = tuple(h)
    vjps = []
    for stage, small_names, stage_keys in STAGES[1:]:
        which = [keys.index(key) for key in stage_keys]
        _, lands = _copies_wait(gather, which, False, jax.tree.leaves(h)[-1], "gather_wait_" + stage)
        for e, land in zip(which, lands):
            Wb[keys[e]] = _full_from_gathered(land, _layer_shard_axis(keys[e][0]))
        carriers = {key: jnp.zeros(Wb[key].shape, F32) for key in stage_keys}
        h, vjp_fn = jax.vjp(_stage_fn(stage, Wb, tabs, mem[0]), {n: S[n] for n in small_names}, carriers, h)
        vjps.append(vjp_fn)
    loss_part, dh, dg_final = _final_call(h, S['final_norm'], loss_target[0], "final_loss")

    grads = {'final_norm': dg_final.reshape(final_norm.shape)}
    exchanges, send_keys, send_parts = [], [], []

    def start_exchange(stage, dh):
        leaves, tree = jax.tree.flatten(dh)
        started = _copies_start(list(send_parts), True, "grad_start_" + stage, thru=leaves)
        exchanges.append((stage, started, list(send_keys)))
        send_keys.clear()
        send_parts.clear()
        return jax.tree.unflatten(tree, started[4])

    for (stage, small_names, stage_keys), vjp_fn in zip(reversed(STAGES[1:]), reversed(vjps)):
        g_small, g_big, dh = vjp_fn(dh)
        for n in small_names:
            grads[n] = grads[n] + g_small[n] if n in grads else g_small[n]
        send_keys += list(stage_keys)
        send_parts += [_parts_from_full(g_big[key], _layer_shard_axis(key[0])) for key in stage_keys]
        if stage == "xa1":
            continue
        if stage == "mixer0":
            small_parts = [_parts_from_full(grads[n], SHARD_AXIS[n]) for n in SMALL]
            send_keys.append("small")
            send_parts.append(jnp.stack([_pack([p[j] for p in small_parts], F32) for j in range(N_DEV)]))
        dh = start_exchange(stage, dh)
    d_res, d_proj = dh[0], _join_pieces_call(dh[1:], [rotated for _, _, rotated in pieces], tabs, "l0_w_in_dproj")[0]
    send_keys.append(first_keys[0])
    send_parts.append(_parts_from_full(_mm(d_proj, hn0, ta=True, name="l0_w_in_dw", tm=1408, tn=1024, tk=2048),
                                       _layer_shard_axis(first_keys[0][0])))
    d_res, d_proj = start_exchange("proj0", (d_res, d_proj))
    dx, dg0 = _mm_rms_bwd(d_proj, w_in, x[0], S['mix_norm'][0], d_res, "l0_w_in_da", wt=True)
    grads['mix_norm'] = grads['mix_norm'] + jnp.concatenate([dg0, jnp.zeros_like(dg0)], axis=0)
    rep_names = REPLICATED + ["loss"]
    grads["loss"] = loss_part
    zero = jnp.zeros((1, 1), F32)
    for d in (w_loc, m_loc, v_loc):
        d["loss"] = zero
    rep_started = _copies_start([_pack([grads[n] for n in rep_names], F32)], False, "rep_grads_start")

    parts, out = {}, {}

    def end_exchange(stage, started, ex_keys, after):
        _, lands = _copies_wait(started, list(range(len(ex_keys))), True, after, "grad_wait_" + stage)
        parts.update(zip(ex_keys, lands))

    def adamw(p, names, call_name):
        if len(names) == 1:
            n = names[0]
            res = _adamw_call(p, _shard_view(n, w_loc[n]), _shard_view(n, m_loc[n]), _shard_view(n, v_loc[n]), call_name)
            for kind, t in zip(("grad", "delta", "new_m", "new_v"), res):
                out[kind, n] = _shard_view(n, t)
        else:
            res = _adamw_call(p, *[_pack([d[n] for n in names], F32) for d in (w_loc, m_loc, v_loc)], call_name)
            for kind, buf in zip(("grad", "delta", "new_m", "new_v"), res):
                for n, t in zip(names, _unpack(buf, [w_loc[n].shape for n in names])):
                    out[kind, n] = t

    for ex in exchanges[:-1]:
        end_exchange(*ex, dx)
    last_names = {key[0] for key in exchanges[-1][2]}
    for n in BIG:
        if n not in last_names:
            adamw([parts[n, l] for l in range(w_loc[n].shape[0])], [n], "adamw_" + n)
    adamw(parts["small"], SMALL, "adamw_small")
    end_exchange(*exchanges[-1], [out["new_v", n] for n in BIG if n not in last_names])
    for n in BIG:
        if n in last_names:
            adamw([parts[n, l] for l in range(w_loc[n].shape[0])], [n], "adamw_" + n)
    _, rep_land = _copies_wait(rep_started, [0], False, out["new_v", "ab_w_in"], "rep_grads_wait")
    adamw(rep_land[0], rep_names, "adamw_replicated")
    loss = out["grad", "loss"][0, 0]

    return (loss, dx[None], *[out[kind, n] for kind in ("grad", "delta", "new_m", "new_v") for n in WEIGHT_NAMES])
```

```python
import functools
import math

import jax
import jax.numpy as jnp
from jax import lax
from jax.experimental import pallas as pl
from jax.experimental.pallas import tpu as pltpu

F32 = jnp.float32
BF16 = jnp.bfloat16
SDS = jax.ShapeDtypeStruct
BS = pl.BlockSpec

N_DEV = 8
NORM_EPS = 1e-6
ROPE_THETA = 500000.0
HEAD_DIM = 64
ROT_DIM = 16
ATT_BLOCK = 128
LRU_C = 8.0
LRU_HEADS = 4
DILATIONS = (1, 4, 16)
B_HEADS = 8
C_HEADS = 16
C_KV_HEADS = 2
XA_HEADS = 4
XA_HEAD_DIM = 128
NEG = -1e30
ADAM_LR, ADAM_B1, ADAM_B2, ADAM_EPS, ADAM_WD, ADAM_STEP = 0.001, 0.9, 0.999, 1e-08, 0.01, 10
LANES = 128
VMEM_LIMIT = 48 * 1024 * 1024

WEIGHT_NAMES = ['mix_norm', 'ab_w_in', 'lru_conv_w', 'lru_conv_b', 'lru_wa', 'lru_ba', 'lru_wx', 'lru_bx',
                'lru_lambda', 'ab_w_out', 'c_w_qkv', 'c_b_qkv', 'c_sinks', 'c_w_out', 'c_b_out', 'xa_norm',
                'xa_mem_norm', 'xa_wq', 'xa_wkv', 'xa_wo', 'ffn_norm', 'ffn_w_gate_up', 'ffn_w_down', 'final_norm']
SHARD_AXIS = {'ab_w_in': 2, 'lru_conv_w': 2, 'lru_wa': 2, 'lru_ba': 2, 'lru_wx': 2, 'lru_bx': 2, 'ab_w_out': 1,
              'c_w_qkv': 2, 'c_b_qkv': 1, 'c_w_out': 1, 'c_b_out': 1, 'xa_wq': 1, 'xa_wkv': 1, 'xa_wo': 2,
              'ffn_w_gate_up': 2, 'ffn_w_down': 1}
BIG = ['ab_w_in', 'lru_wa', 'lru_wx', 'ab_w_out', 'c_w_qkv', 'c_w_out', 'xa_wq', 'xa_wkv', 'xa_wo',
       'ffn_w_gate_up', 'ffn_w_down']
SMALL = ['lru_conv_w', 'lru_ba', 'lru_bx', 'c_b_qkv', 'c_b_out']
REPLICATED = [n for n in WEIGHT_NAMES if n not in SHARD_AXIS]
COLUMN_CUT = ('ab_w_in', 'c_w_qkv', 'ffn_w_gate_up')
PACK_QUANTUM = 2048


def _shard_view(name, t):
    return jnp.swapaxes(t, -1, -2) if name in COLUMN_CUT else t


def _layer_shard_axis(name):
    return 0 if name in COLUMN_CUT else SHARD_AXIS[name] - 1


def _pcall(body, **kw):
    return pl.pallas_call(body, **kw)


def _cparams(sem=None):
    return pltpu.CompilerParams(dimension_semantics=sem, vmem_limit_bytes=VMEM_LIMIT)


def _tile(n, target, mult=LANES):
    if n <= target:
        return n
    t = (target // mult) * mult
    while t >= mult:
        if n % t == 0:
            return t
        t -= mult
    return n


def _sigmoid(x):
    return 1.0 / (1.0 + jnp.exp(-x))


def _expm1(x):
    small = x * (1.0 + x * (0.5 + x * (1.0 / 6.0 + x * (1.0 / 24.0))))
    return jnp.where(jnp.abs(x) < 0.03, small, jnp.exp(x) - 1.0)


_GELU_C = math.sqrt(2.0 / math.pi)


def _gelu_parts(y):
    y2 = y * y
    th = jnp.tanh(_GELU_C * (y + 0.044715 * y * y2))
    g = 0.5 * y * (1.0 + th)
    dg = 0.5 * (1.0 + th) + 0.5 * y * (1.0 - th * th) * _GELU_C * (1.0 + 3.0 * 0.044715 * y2)
    return g, dg


def _rotate(xv, tab_refs, inverse=False):
    W = xv.shape[1]
    half = ROT_DIM // 2
    c, sa, sb = (jnp.tile(t[...], (1, W // LANES)) for t in tab_refs)
    if not inverse:
        return xv * c + pltpu.roll(xv, half, axis=1) * sa + pltpu.roll(xv, W - half, axis=1) * sb
    return xv * c + pltpu.roll(xv * sa, W - half, axis=1) + pltpu.roll(xv * sb, half, axis=1)


def _mm(a, b, *, ta=False, tb=False, bias=None, add=None, rope=None, name, tm=512, tn=512, tk=2048):
    halves = a.ndim == 3
    if halves:
        M, K = 2 * a.shape[2], a.shape[1]
        tm = min(tm, a.shape[2])
    else:
        M, K = (a.shape[1], a.shape[0]) if ta else a.shape
    N = b.shape[0] if tb else b.shape[1]
    tm, tn, tk = _tile(a.shape[2] if halves else M, tm), _tile(N, tn), _tile(K, tk)
    nk = K // tk
    dn = (((0 if ta else 1,), (1 if tb else 0,)), ((), ()))

    def body(*refs):
        a_ref, b_ref = refs[0], refs[1]
        pos = 2
        bias_ref = add_ref = None
        if bias is not None:
            bias_ref = refs[pos]
            pos += 1
        if add is not None:
            add_ref = refs[pos]
            pos += 1
        tab_refs = refs[pos:pos + 3] if rope is not None else None
        pos += 3 if rope is not None else 0
        o_ref = refs[pos]
        part = lax.dot_general(a_ref[...].astype(BF16), b_ref[...].astype(BF16), dn, preferred_element_type=F32)

        def finish(r):
            if bias_ref is not None:
                r = r + bias_ref[...]
            if add_ref is not None:
                r = r + add_ref[...]
            o_ref[...] = r if tab_refs is None else _rotate(r, tab_refs)

        if nk == 1:
            finish(part)
            return
        acc_ref = refs[pos + 1]
        k = pl.program_id(2)

        @pl.when(k == 0)
        def _():
            acc_ref[...] = part

        @pl.when((k > 0) & (k < nk - 1))
        def _():
            acc_ref[...] += part

        @pl.when(k == nk - 1)
        def _():
            finish(acc_ref[...] + part)

    per_half = (M // 2) // tm if halves else 0
    in_specs = [BS((None, tk, tm), lambda i, j, k: (i // per_half, k, i % per_half)) if halves
                else BS((tk, tm), lambda i, j, k: (k, i)) if ta else BS((tm, tk), lambda i, j, k: (i, k)),
                BS((tn, tk), lambda i, j, k: (j, k)) if tb else BS((tk, tn), lambda i, j, k: (k, j))]
    args = [a, b]
    if bias is not None:
        in_specs.append(BS((1, tn), lambda i, j, k: (0, j)))
        args.append(bias.reshape(1, N))
    if add is not None:
        in_specs.append(BS((tm, tn), lambda i, j, k: (i, j)))
        args.append(add)
    if rope is not None:
        in_specs += [BS((tm, LANES), lambda i, j, k: (i, 0))] * 3
        args += list(rope)
    return _pcall(body, name=name, grid=(M // tm, N // tn, nk), in_specs=in_specs,
                  out_specs=BS((tm, tn), lambda i, j, k: (i, j)), out_shape=SDS((M, N), F32),
                  scratch_shapes=[pltpu.VMEM((tm, tn), F32)] if nk > 1 else [],
                  compiler_params=_cparams(("parallel", "parallel", "arbitrary")))(*args)


def _mm_rms_bwd(dy, wb, x, g, add, name, wt=False):
    halves = dy.ndim == 3
    T, Kc = (dy.shape[1], 2 * dy.shape[2]) if halves else dy.shape
    Dm = wb.shape[1] if wt else wb.shape[0]
    tm, tk = _tile(T, 1024), _tile(Kc // 2 if halves else Kc, 1408)
    nk = Kc // tk
    per_half = (Kc // 2) // tk if halves else 0
    dn = (((1,), (0 if wt else 1,)), ((), ()))

    def body(*refs):
        dy_ref, w_ref, x_ref, g_ref = refs[:4]
        dx_ref, dg_ref, acc_ref = refs[-3:]
        i, k = pl.program_id(0), pl.program_id(1)
        part = lax.dot_general(dy_ref[...].astype(BF16), w_ref[...], dn, preferred_element_type=F32)

        @pl.when(k == 0)
        def _():
            acc_ref[...] = part

        @pl.when((k > 0) & (k < nk - 1))
        def _():
            acc_ref[...] += part

        @pl.when((i == 0) & (k == 0))
        def _():
            dg_ref[...] = jnp.zeros_like(dg_ref)

        @pl.when(k == nk - 1)
        def _():
            d = part if nk == 1 else acc_ref[...] + part
            xv = x_ref[...]
            r = lax.rsqrt(jnp.mean(xv * xv, axis=-1, keepdims=True) + NORM_EPS)
            xh = xv * r
            dyg = d * g_ref[...]
            dx = r * (dyg - xh * jnp.mean(dyg * xh, axis=-1, keepdims=True))
            dx_ref[...] = dx if add is None else dx + refs[4][...]
            dg_ref[...] += jnp.sum(d * xh, axis=0, keepdims=True)

    row = BS((tm, Dm), lambda i, k: (i, 0))
    vec = BS((1, Dm), lambda i, k: (0, 0))
    extra = [] if add is None else [add]
    return _pcall(body, name=name, grid=(T // tm, nk),
                  in_specs=[BS((None, tm, tk), lambda i, k: (k // per_half, i, k % per_half)) if halves
                            else BS((tm, tk), lambda i, k: (i, k)),
                            BS((tk, Dm), lambda i, k: (k, 0)) if wt else BS((Dm, tk), lambda i, k: (0, k)), row, vec]
                  + [row] * len(extra),
                  out_specs=[row, vec], out_shape=[SDS((T, Dm), F32), SDS((1, Dm), F32)],
                  scratch_shapes=[pltpu.VMEM((tm, Dm), F32)],
                  compiler_params=_cparams(("arbitrary", "arbitrary")))(dy, wb, x, g.reshape(1, Dm), *extra)


def _colsum(x, name):
    T, N = x.shape
    tt = _tile(T, 1024, 8)

    def body(x_ref, o_ref):
        @pl.when(pl.program_id(0) == 0)
        def _():
            o_ref[...] = jnp.zeros_like(o_ref)

        o_ref[...] += jnp.sum(x_ref[...], axis=0, keepdims=True)

    return _pcall(body, name=name, grid=(T // tt,), in_specs=[BS((tt, N), lambda i: (i, 0))],
                  out_specs=BS((1, N), lambda i: (0, 0)), out_shape=SDS((1, N), F32),
                  compiler_params=_cparams(("arbitrary",)))(x)


@functools.partial(jax.custom_vjp, nondiff_argnums=(5,))
def linear(a, wb, wc, bias, add, name):
    return _linear_fwd(a, wb, wc, bias, add, name)[0]


def _linear_fwd(a, wb, wc, bias, add, name):
    out = _mm(a, wb, bias=bias, add=add, name=name, tm=1024, tn=1024, tk=1024)
    return out, (a, wb, bias is not None, add is not None)


def _linear_bwd(name, res, g):
    a, wb, has_bias, has_add = res
    da = _mm(g, wb, tb=True, name=name + "_da", tm=1024, tn=1024, tk=1024)
    dw = _mm(a, g, ta=True, name=name + "_dw", tm=1024, tn=1024, tk=1024)
    dbias = _colsum(g, name + "_db").reshape(-1) if has_bias else None
    return da, jnp.zeros_like(wb), dw, dbias, (g if has_add else None)


linear.defvjp(_linear_fwd, _linear_bwd)


def _rms_fwd_call(x, g, name, out_dtype=F32):
    T, Dm = x.shape
    tt = _tile(T, 2048, 16)

    def body(x_ref, g_ref, o_ref):
        xv = x_ref[...]
        r = lax.rsqrt(jnp.mean(xv * xv, axis=-1, keepdims=True) + NORM_EPS)
        o_ref[...] = ((xv * r) * g_ref[...]).astype(out_dtype)

    return _pcall(body, name=name, grid=(T // tt,),
                  in_specs=[BS((tt, Dm), lambda i: (i, 0)), BS((1, Dm), lambda i: (0, 0))],
                  out_specs=BS((tt, Dm), lambda i: (i, 0)), out_shape=SDS((T, Dm), out_dtype),
                  compiler_params=_cparams(("parallel",)))(x, g.reshape(1, Dm))


@functools.partial(jax.custom_vjp, nondiff_argnums=(5, 6))
def norm_linear(x, g, wb, wc, bias, wt, name):
    return _norm_linear_fwd(x, g, wb, wc, bias, wt, name)[0]


def _norm_linear_fwd(x, g, wb, wc, bias, wt, name):
    hn = _rms_fwd_call(x, g, name + "_norm", BF16)
    return (_mm(hn, wb, tb=wt, bias=bias, name=name, tm=1024, tn=1408, tk=1024), x), (x, g, hn, wb, bias is not None)


def _norm_linear_bwd(wt, name, res, cts):
    x, g, hn, wb, has_bias = res
    dy, dres = cts
    if wt:
        dw = _mm(dy, hn, ta=True, name=name + "_dw", tm=1408, tn=1024, tk=2048)
    else:
        dw = _mm(hn, dy, ta=True, name=name + "_dw", tm=1024, tn=1408, tk=1024)
    dx, dg = _mm_rms_bwd(dy, wb, x, g, dres, name + "_da", wt)
    dbias = _colsum(dy, name + "_db").reshape(-1) if has_bias else None
    return dx, dg.reshape(g.shape), jnp.zeros_like(wb), dw, dbias


norm_linear.defvjp(_norm_linear_fwd, _norm_linear_bwd)


def _rope_tables(T):
    half = ROT_DIM // 2
    inv = ROPE_THETA ** (-jnp.arange(0, ROT_DIM, 2, dtype=F32) / ROT_DIM)
    lane = jnp.arange(LANES) % HEAD_DIM
    freq = jnp.where(lane < ROT_DIM, inv[lane % half], 0.0)
    ang = jnp.arange(T, dtype=F32)[:, None] * freq[None, :]
    c, s = jnp.cos(ang), jnp.sin(ang)
    sa = jnp.where((lane >= half) & (lane < ROT_DIM), s, 0.0)
    sb = jnp.where(lane < half, -s, 0.0)
    return c, sa, sb


def _join_pieces_call(pieces, rotated, tabs, name):
    T = pieces[0].shape[0]
    widths = [p.shape[1] for p in pieces]
    W = sum(widths)
    tt = _tile(T, 512, 16)

    def body(*refs):
        tab_refs, o_ref, sum_ref = refs[len(pieces):len(pieces) + 3], refs[-2], refs[-1]

        @pl.when(pl.program_id(0) == 0)
        def _():
            sum_ref[...] = jnp.zeros_like(sum_ref)

        off = 0
        for p_ref, w, r in zip(refs, widths, rotated):
            piece = _rotate(p_ref[...], tab_refs, inverse=True) if r else p_ref[...]
            o_ref[:, off:off + w] = piece.astype(BF16)
            sum_ref[:, off:off + w] += jnp.sum(piece, axis=0, keepdims=True)
            off += w

    return _pcall(body, name=name, grid=(T // tt,),
                  in_specs=[BS((tt, w), lambda i: (i, 0)) for w in widths] + [BS((tt, LANES), lambda i: (i, 0))] * 3,
                  out_specs=[BS((tt, W), lambda i: (i, 0)), BS((1, W), lambda i: (0, 0))],
                  out_shape=[SDS((T, W), BF16), SDS((1, W), F32)],
                  compiler_params=_cparams(("arbitrary",)))(*pieces, *tabs)


@functools.partial(jax.custom_vjp, nondiff_argnums=(6, 7))
def norm_linear_pieces(x, g, wb, wc, bias, tabs, pieces, name):
    return _norm_linear_pieces_fwd(x, g, wb, wc, bias, tabs, pieces, name)[0]


def _norm_linear_pieces_fwd(x, g, wb, wc, bias, tabs, pieces, name):
    hn = _rms_fwd_call(x, g, name + "_norm", BF16)
    outs, row = [], 0
    for i, (width, rotated) in enumerate(pieces):
        outs.append(_mm(hn, wb[row:row + width], tb=True, bias=None if bias is None else bias[row:row + width],
                        rope=tabs if rotated else None, name=f"{name}_{i}", tm=2048, tn=1024, tk=1024))
        row += width
    return (*outs, x), (x, g, hn, wb, tabs, bias is not None)


def _norm_linear_pieces_bwd(pieces, name, res, cts):
    x, g, hn, wb, tabs, has_bias = res
    dy, dy_sum = _join_pieces_call(cts[:-1], [rotated for _, rotated in pieces], tabs, name + "_join")
    dw = _mm(dy, hn, ta=True, name=name + "_dw", tm=1408, tn=1024, tk=2048)
    dx, dg = _mm_rms_bwd(dy, wb, x, g, cts[-1], name + "_da", wt=True)
    dbias = dy_sum.reshape(-1) if has_bias else None
    return dx, dg.reshape(g.shape), jnp.zeros_like(wb), dw, dbias, jax.tree.map(jnp.zeros_like, tabs)


norm_linear_pieces.defvjp(_norm_linear_pieces_fwd, _norm_linear_pieces_bwd)


def _conv_fwd_call(x, w, b, name):
    T, C = x.shape
    tt = _tile(T, 2048, 8)
    per = tt // 8

    def body(x_ref, halo_ref, w_ref, b_ref, o_ref):
        i = pl.program_id(0)
        halo = jnp.where(i > 0, halo_ref[...], 0.0)
        e = jnp.concatenate([halo, x_ref[...]], axis=0)
        acc = b_ref[...]
        for k in (3, 2, 1):
            acc = acc + pltpu.roll(e, k, axis=0)[8:, :] * w_ref[3 - k:4 - k, :]
        o_ref[...] = acc + x_ref[...] * w_ref[3:4, :]

    return _pcall(body, name=name, grid=(T // tt,),
                  in_specs=[BS((tt, C), lambda i: (i, 0)), BS((8, C), lambda i: (jnp.maximum(i * per - 1, 0), 0)),
                            BS((4, C), lambda i: (0, 0)), BS((1, C), lambda i: (0, 0))],
                  out_specs=BS((tt, C), lambda i: (i, 0)), out_shape=SDS((T, C), F32),
                  compiler_params=_cparams(("parallel",)))(x, x, w, b.reshape(1, C))


def _conv_bwd_call(x, w, dy, name):
    T, C = x.shape
    tt = _tile(T, 1024, 8)
    per = tt // 8
    nt = T // tt

    def body(x_ref, halo_ref, w_ref, dy_ref, nxt_ref, dx_ref, dwb_ref):
        i = pl.program_id(0)
        halo = jnp.where(i > 0, halo_ref[...], 0.0)
        e = jnp.concatenate([halo, x_ref[...]], axis=0)
        dy = dy_ref[...]
        nxt = jnp.where(i < nt - 1, nxt_ref[...], 0.0)
        f = jnp.concatenate([dy, nxt], axis=0)
        dx = dy * w_ref[3:4, :]
        rows = [None] * 4
        rows[3] = jnp.sum(dy * x_ref[...], axis=0, keepdims=True)
        for k in (1, 2, 3):
            dx = dx + pltpu.roll(f, tt + 8 - k, axis=0)[:tt, :] * w_ref[3 - k:4 - k, :]
            rows[3 - k] = jnp.sum(dy * pltpu.roll(e, k, axis=0)[8:, :], axis=0, keepdims=True)
        dx_ref[...] = dx
        upd = jnp.concatenate(rows + [jnp.sum(dy, axis=0, keepdims=True), jnp.zeros((3, C), F32)], axis=0)

        @pl.when(i == 0)
        def _():
            dwb_ref[...] = jnp.zeros_like(dwb_ref)

        dwb_ref[...] += upd

    row = BS((tt, C), lambda i: (i, 0))
    return _pcall(body, name=name, grid=(nt,),
                  in_specs=[row, BS((8, C), lambda i: (jnp.maximum(i * per - 1, 0), 0)), BS((4, C), lambda i: (0, 0)),
                            row, BS((8, C), lambda i: (jnp.minimum((i + 1) * per, T // 8 - 1), 0))],
                  out_specs=[row, BS((8, C), lambda i: (0, 0))],
                  out_shape=[SDS((T, C), F32), SDS((8, C), F32)],
                  compiler_params=_cparams(("arbitrary",)))(x, x, w, dy, dy)


@functools.partial(jax.custom_vjp, nondiff_argnums=(3,))
def conv4(x, w, b, name):
    return _conv_fwd_call(x, w, b, name)


def _conv4_fwd(x, w, b, name):
    return _conv_fwd_call(x, w, b, name), (x, w)


def _conv4_bwd(name, res, dy):
    x, w = res
    dx, dwb = _conv_bwd_call(x, w, dy, name + "_bwd")
    return dx, dwb[0:4], dwb[4]


conv4.defvjp(_conv4_fwd, _conv4_bwd)


def _gates_fwd_call(xc, wa, ba, wx, bx, name):
    T, C = xc.shape
    hd = C // LRU_HEADS
    tt = _tile(T, 2048, 8)

    def body(x_ref, wa_ref, ba_ref, wx_ref, bx_ref, ga_ref, gx_ref):
        xb = x_ref[...].astype(BF16)
        ga_ref[...] = jnp.dot(xb, wa_ref[0].astype(BF16), preferred_element_type=F32) + ba_ref[...]
        gx_ref[...] = jnp.dot(xb, wx_ref[0].astype(BF16), preferred_element_type=F32) + bx_ref[...]

    blk = BS((tt, hd), lambda i, h: (i, h))
    wsp = BS((1, hd, hd), lambda i, h: (h, 0, 0))
    bsp = BS((1, hd), lambda i, h: (0, h))
    return _pcall(body, name=name, grid=(T // tt, LRU_HEADS), in_specs=[blk, wsp, bsp, wsp, bsp],
                  out_specs=[blk, blk], out_shape=[SDS((T, C), F32)] * 2,
                  compiler_params=_cparams(("parallel", "parallel")))(xc, wa, ba.reshape(1, C), wx, bx.reshape(1, C))


def _gates_bwd_x_call(dga, dgx, wa, wx, add, name):
    T, C = dga.shape
    hd = C // LRU_HEADS
    tt = _tile(T, 2048, 8)
    dn = (((1,), (1,)), ((), ()))

    def body(da_ref, dx_ref, wa_ref, wx_ref, add_ref, o_ref):
        o_ref[...] = (lax.dot_general(da_ref[...].astype(BF16), wa_ref[0].astype(BF16), dn, preferred_element_type=F32)
                      + lax.dot_general(dx_ref[...].astype(BF16), wx_ref[0].astype(BF16), dn, preferred_element_type=F32)
                      + add_ref[...])

    blk = BS((tt, hd), lambda i, h: (i, h))
    wsp = BS((1, hd, hd), lambda i, h: (h, 0, 0))
    return _pcall(body, name=name, grid=(T // tt, LRU_HEADS), in_specs=[blk, blk, wsp, wsp, blk], out_specs=blk,
                  out_shape=SDS((T, C), F32), compiler_params=_cparams(("parallel", "parallel")))(dga, dgx, wa, wx, add)


def _gates_bwd_w_call(xc, dga, dgx, name):
    T, C = xc.shape
    hd = C // LRU_HEADS
    tt = _tile(T, 2048, 8)
    dn = (((0,), (0,)), ((), ()))

    def body(x_ref, da_ref, dx_ref, dwa_ref, dwx_ref, dba_ref, dbx_ref):
        @pl.when(pl.program_id(1) == 0)
        def _():
            dwa_ref[...] = jnp.zeros_like(dwa_ref)
            dwx_ref[...] = jnp.zeros_like(dwx_ref)
            dba_ref[...] = jnp.zeros_like(dba_ref)
            dbx_ref[...] = jnp.zeros_like(dbx_ref)

        xb = x_ref[...].astype(BF16)
        da, dx = da_ref[...], dx_ref[...]
        dwa_ref[0] += lax.dot_general(xb, da.astype(BF16), dn, preferred_element_type=F32)
        dwx_ref[0] += lax.dot_general(xb, dx.astype(BF16), dn, preferred_element_type=F32)
        dba_ref[...] += jnp.sum(da, axis=0, keepdims=True)
        dbx_ref[...] += jnp.sum(dx, axis=0, keepdims=True)

    blk = BS((tt, hd), lambda h, i: (i, h))
    wsp = BS((1, hd, hd), lambda h, i: (h, 0, 0))
    bsp = BS((1, hd), lambda h, i: (0, h))
    return _pcall(body, name=name, grid=(LRU_HEADS, T // tt), in_specs=[blk, blk, blk],
                  out_specs=[wsp, wsp, bsp, bsp],
                  out_shape=[SDS((LRU_HEADS, hd, hd), F32)] * 2 + [SDS((1, C), F32)] * 2,
                  compiler_params=_cparams(("parallel", "arbitrary")))(xc, dga, dgx)


@functools.partial(jax.custom_vjp, nondiff_argnums=(7,))
def lru_gates(xc, wa, wa_c, ba, wx, wx_c, bx, name):
    return _lru_gates_fwd(xc, wa, wa_c, ba, wx, wx_c, bx, name)[0]


def _lru_gates_fwd(xc, wa, wa_c, ba, wx, wx_c, bx, name):
    return (*_gates_fwd_call(xc, wa, ba, wx, bx, name), xc), (xc, wa, wx, ba.shape)


def _lru_gates_bwd(name, res, g):
    xc, wa, wx, bshape = res
    dga, dgx, dxc_other = g
    dxc = _gates_bwd_x_call(dga, dgx, wa, wx, dxc_other, name + "_dx")
    dwa, dwx, dba, dbx = _gates_bwd_w_call(xc, dga, dgx, name + "_dw")
    return dxc, jnp.zeros_like(wa), dwa, dba.reshape(bshape), jnp.zeros_like(wx), dwx, dbx.reshape(bshape)


lru_gates.defvjp(_lru_gates_fwd, _lru_gates_bwd)


def _lru_coeffs(xc, ga, gx, lam):
    r = _sigmoid(ga)
    ig = _sigmoid(gx)
    z = -lam
    sp = jnp.maximum(z, 0.0) + jnp.log(1.0 + jnp.exp(-jnp.abs(z)))
    la = -LRU_C * r * sp
    a = jnp.exp(la)
    s = jnp.sqrt(-_expm1(2.0 * la))
    return r, ig, sp, a, s


LRU_TT = 256


def _scan_fwd_call(xc, ga, gx, y, lam, name):
    T, C = xc.shape
    tt = _tile(T, 2 * LRU_TT, 8)

    def body(xc_ref, ga_ref, gx_ref, y_ref, lam_ref, h_ref, rec_ref, a_buf, carry):
        @pl.when(pl.program_id(0) == 0)
        def _():
            carry[...] = jnp.zeros_like(carry)

        xcv = xc_ref[...]
        _, ig, _, a, s = _lru_coeffs(xcv, ga_ref[...], gx_ref[...], lam_ref[...])
        a_buf[...] = a
        h_ref[...] = s * (ig * xcv)

        def step(t, h):
            hn = a_buf[pl.ds(t, 1), :] * h + h_ref[pl.ds(t, 1), :]
            h_ref[pl.ds(t, 1), :] = hn
            return hn

        carry[0:1, :] = lax.fori_loop(0, tt, step, carry[0:1, :], unroll=8)
        g, _ = _gelu_parts(y_ref[...])
        rec_ref[...] = h_ref[...] * g

    row = BS((tt, C), lambda i: (i, 0))
    vec = BS((1, C), lambda i: (0, 0))
    return _pcall(body, name=name, grid=(T // tt,), in_specs=[row, row, row, row, vec], out_specs=[row, row],
                  out_shape=[SDS((T, C), F32)] * 2,
                  scratch_shapes=[pltpu.VMEM((tt, C), F32), pltpu.VMEM((8, C), F32)],
                  compiler_params=_cparams(("arbitrary",)))(xc, ga, gx, y, lam.reshape(1, C))


def _scan_bwd_call(xc, ga, gx, y, lam, h, drec, name):
    T, C = xc.shape
    tt = _tile(T, LRU_TT, 8)
    nt = T // tt
    per = tt // 8

    def body(xc_ref, ga_ref, gx_ref, y_ref, lam_ref, h_ref, halo_ref, dr_ref,
             dga_ref, dgx_ref, dxc_ref, dy_ref, dlam_ref, a_buf, g_buf, carry):
        i = pl.program_id(0)

        @pl.when(i == 0)
        def _():
            carry[...] = jnp.zeros_like(carry)
            dlam_ref[...] = jnp.zeros_like(dlam_ref)

        xcv, lam = xc_ref[...], lam_ref[...]
        r, ig, sp, a, s = _lru_coeffs(xcv, ga_ref[...], gx_ref[...], lam)
        gel, dgel = _gelu_parts(y_ref[...])
        drec = dr_ref[...]
        hv = h_ref[...]
        dy_ref[...] = drec * hv * dgel
        a_buf[...] = a
        g_buf[...] = drec * gel

        def step(j, q):
            t = tt - 1 - j
            g = g_buf[pl.ds(t, 1), :] + q
            g_buf[pl.ds(t, 1), :] = g
            return a_buf[pl.ds(t, 1), :] * g

        carry[0:1, :] = lax.fori_loop(0, tt, step, carry[0:1, :], unroll=8)
        g = g_buf[...]
        halo = jnp.where(i < nt - 1, halo_ref[...], 0.0)
        hprev = pltpu.roll(jnp.concatenate([halo, hv], axis=0), 1, axis=0)[8:, :]
        da = g * hprev
        dig = g * s * xcv
        ds = g * ig * xcv
        dla = da * a - ds * (a * a) / s
        dga_ref[...] = dla * (-LRU_C * sp) * r * (1.0 - r)
        dgx_ref[...] = dig * ig * (1.0 - ig)
        dxc_ref[...] = g * s * ig
        dlam_ref[...] += jnp.sum(dla * r, axis=0, keepdims=True) * (LRU_C * _sigmoid(-lam))

    row = BS((tt, C), lambda i: (nt - 1 - i, 0))
    vec = BS((1, C), lambda i: (0, 0))
    halo = BS((8, C), lambda i: (jnp.maximum((nt - 1 - i) * per - 1, 0), 0))
    return _pcall(body, name=name, grid=(nt,), in_specs=[row, row, row, row, vec, row, halo, row],
                  out_specs=[row, row, row, row, vec], out_shape=[SDS((T, C), F32)] * 4 + [SDS((1, C), F32)],
                  scratch_shapes=[pltpu.VMEM((tt, C), F32), pltpu.VMEM((tt, C), F32), pltpu.VMEM((8, C), F32)],
                  compiler_params=_cparams(("arbitrary",)))(xc, ga, gx, y, lam.reshape(1, C), h, h, drec)


@functools.partial(jax.custom_vjp, nondiff_argnums=(5,))
def lru_scan(xc, ga, gx, y, lam, name):
    return _scan_fwd_call(xc, ga, gx, y, lam, name)[1]


def _lru_scan_fwd(xc, ga, gx, y, lam, name):
    h, rec = _scan_fwd_call(xc, ga, gx, y, lam, name)
    return rec, (xc, ga, gx, y, lam, h)


def _lru_scan_bwd(name, res, drec):
    xc, ga, gx, y, lam, h = res
    dga, dgx, dxc, dy, dlam = _scan_bwd_call(xc, ga, gx, y, lam, h, drec, name + "_bwd")
    return dxc, dga, dgx, dy, dlam.reshape(lam.shape)


lru_scan.defvjp(_lru_scan_fwd, _lru_scan_bwd)


def _att_batch(d, shared=False):
    if shared:
        return 8, 1
    return (4, 1) if d == 1 else (1, min(d, 8))


def _att_masks(n, max_dist):
    qi = lax.broadcasted_iota(jnp.int32, (1, 2 * ATT_BLOCK, 2 * ATT_BLOCK), 1) & (ATT_BLOCK - 1)
    kj = lax.broadcasted_iota(jnp.int32, (1, 2 * ATT_BLOCK, 2 * ATT_BLOCK), 2)
    prev = (kj < ATT_BLOCK) & (kj >= qi + (ATT_BLOCK - max_dist)) & (n > 0)
    cur = (kj >= ATT_BLOCK) & (kj - ATT_BLOCK <= qi)
    return prev | cur


def _lo_lanes(rows):
    return lax.broadcasted_iota(jnp.int32, (rows, LANES), 1) < HEAD_DIM


def _lane_half(rows):
    return lax.broadcasted_iota(jnp.int32, (rows, LANES), 1) // HEAD_DIM


def _stack_heads(x2):
    lo = _lo_lanes(ATT_BLOCK)
    zero = jnp.zeros_like(x2)
    return jnp.concatenate([jnp.where(lo, x2, zero), jnp.where(lo, zero, x2)], axis=0)


def _unstack_heads(y):
    return jnp.where(_lo_lanes(ATT_BLOCK), y[:ATT_BLOCK], y[ATT_BLOCK:])


def _per_head_col(x2):
    return jnp.concatenate([x2[:, 0:1], x2[:, HEAD_DIM:HEAD_DIM + 1]], axis=0)


def _head_sums(x2):
    lo = _lo_lanes(ATT_BLOCK)
    return jnp.concatenate([jnp.sum(jnp.where(lo, x2, 0.0), axis=1, keepdims=True),
                            jnp.sum(jnp.where(lo, 0.0, x2), axis=1, keepdims=True)], axis=0)


def _att_specs(d, Wq, Wk, nb, clamp):
    shared = Wk != Wq
    cgw, sb = _att_batch(d, shared)
    cur = (lambda n: jnp.minimum(n, nb - 1)) if clamp else (lambda n: n)
    rows, qw, kw = ATT_BLOCK * d, cgw * LANES, (LANES if shared else cgw * LANES)
    kcol = (lambda g: 0) if shared else (lambda g: g)
    qsp = BS((rows, qw), lambda g, n: (cur(n), g))
    csp = BS((rows, kw), lambda g, n: (cur(n), kcol(g)))
    psp = BS((rows, kw), lambda g, n: (jnp.maximum(cur(n) - 1, 0), kcol(g)))
    return cgw, sb, shared, qsp, csp, psp, qw, kw


def _att_streams(d, sb, work):
    if d == 1:
        work([slice(None)])
        return

    def one(j, carry):
        work([pl.ds(j * sb + i, ATT_BLOCK, stride=d) for i in range(sb)])
        return carry

    lax.fori_loop(0, d // sb, one, 0)


def _att_problem_loads(rows, cgw, shared, g, q_ref, kc_ref, kp_ref, vc_ref, vp_ref, sk_ref):
    half = _lane_half(ATT_BLOCK)

    def kv(ref, r, p):
        x = ref[r, :]
        if not shared:
            return x[:, p * LANES:(p + 1) * LANES]
        return jnp.where(half == p // 4, x, pltpu.roll(x, HEAD_DIM, axis=1))

    qs, kb, vb, sk = [], [], [], []
    for r in rows:
        qrow = q_ref[r, :]
        for p in range(cgw):
            cols = slice(p * LANES, (p + 1) * LANES)
            qs.append(_stack_heads(qrow[:, cols].astype(BF16)))
            kb.append(jnp.concatenate([kv(kp_ref, r, p), kv(kc_ref, r, p)], axis=0).astype(BF16))
            vb.append(jnp.concatenate([kv(vp_ref, r, p), kv(vc_ref, r, p)], axis=0).astype(BF16))
            sk.append(_per_head_col(jnp.broadcast_to(sk_ref[:, cols], (ATT_BLOCK, LANES))))
    return jnp.stack(qs), jnp.stack(kb), jnp.stack(vb), jnp.stack(sk)


_BDOT_NT = (((2,), (2,)), ((0,), (0,)))
_BDOT_NN = (((2,), (1,)), ((0,), (0,)))
_BDOT_TN = (((1,), (1,)), ((0,), (0,)))


def _att_fwd_call(q, k, v, sinks, d, max_dist, name):
    T, Wq = q.shape
    Wk = k.shape[1]
    nb = T // (d * ATT_BLOCK)
    cgw, sb, shared, qsp, csp, psp, qw, kw = _att_specs(d, Wq, Wk, nb, False)
    G = Wq // qw
    assert not shared or (d == 1 and Wk == LANES and G == 1 and cgw == 8), "a shared kv pair serves 2 x 8 query heads"

    def body(q_ref, kc_ref, kp_ref, vc_ref, vp_ref, sk_ref, o_ref, lse_ref):
        g, n = pl.program_id(0), pl.program_id(1)

        def work(rows):
            qs, kband, vband, sk = _att_problem_loads(rows, cgw, shared, g, q_ref, kc_ref, kp_ref, vc_ref, vp_ref, sk_ref)
            s = lax.dot_general(qs, kband, _BDOT_NT, preferred_element_type=F32) * (HEAD_DIM ** -0.5)
            s = jnp.where(_att_masks(n, max_dist), s, NEG)
            m = jnp.maximum(jnp.max(s, axis=-1, keepdims=True), sk)
            e = jnp.exp(s - m)
            den = jnp.sum(e, axis=-1, keepdims=True) + jnp.exp(sk - m)
            o = lax.dot_general((e * (1.0 / den)).astype(BF16), vband, _BDOT_NN, preferred_element_type=F32)
            lse = jnp.broadcast_to(m + jnp.log(den), o.shape)
            for i, r in enumerate(rows):
                o_ref[r, :] = jnp.concatenate([_unstack_heads(o[i * cgw + p]) for p in range(cgw)], axis=1)
                lse_ref[r, :] = jnp.concatenate([_unstack_heads(lse[i * cgw + p]) for p in range(cgw)], axis=1)

        _att_streams(d, sb, work)

    sksp = BS((1, qw), lambda g, n: (0, g))
    return _pcall(body, name=name, grid=(G, nb), in_specs=[qsp, csp, psp, csp, psp, sksp], out_specs=[qsp, qsp],
                  out_shape=[SDS((T, Wq), F32)] * 2,
                  compiler_params=_cparams(("parallel", "parallel")))(q, k, k, v, v, sinks)


def _att_bwd_call(q, k, v, sinks, o, lse, do, dlse, d, max_dist, name, acc=None):
    T, Wq = q.shape
    Wk = k.shape[1]
    nb = T // (d * ATT_BLOCK)
    cgw, sb, shared, qsp, csp, psp, qw, kw = _att_specs(d, Wq, Wk, nb, True)
    G = Wq // qw
    scale = HEAD_DIM ** -0.5

    def body(*refs):
        q_ref, kc_ref, kp_ref, vc_ref, vp_ref, sk_ref, o_ref, lse_ref, do_ref = refs[:9]
        dlse_ref = refs[9] if dlse is not None else None
        aq_ref, ak_ref, av_ref = refs[-9:-6] if acc is not None else (None, None, None)
        dq_ref, dk_ref, dv_ref, dsk_ref, ck, cv = refs[-6:]
        g, n = pl.program_id(0), pl.program_id(1)

        def plus(val, a_ref, r):
            return val if a_ref is None else val + a_ref[r, :]

        @pl.when(n == 0)
        def _():
            ck[...] = jnp.zeros_like(ck)
            cv[...] = jnp.zeros_like(cv)
            dsk_ref[...] = jnp.zeros_like(dsk_ref)

        def work(rows):
            qs, kband, vband, sk = _att_problem_loads(rows, cgw, shared, g, q_ref, kc_ref, kp_ref, vc_ref, vp_ref, sk_ref)
            dos, lse_c, corr = [], [], []
            for r in rows:
                do_r, o_r, lse_r = do_ref[r, :], o_ref[r, :], lse_ref[r, :]
                dlse_r = dlse_ref[r, :] if dlse_ref is not None else None
                for p in range(cgw):
                    cols = slice(p * LANES, (p + 1) * LANES)
                    dos.append(_stack_heads(do_r[:, cols].astype(BF16)))
                    lse_c.append(_per_head_col(lse_r[:, cols]))
                    delta = _head_sums(do_r[:, cols] * o_r[:, cols])
                    corr.append(-delta if dlse_r is None else _head_sums(dlse_r[:, cols]) - delta)
            dos, lse_c, corr = jnp.stack(dos), jnp.stack(lse_c), jnp.stack(corr)
            s = lax.dot_general(qs, kband, _BDOT_NT, preferred_element_type=F32) * scale
            pr = jnp.exp(jnp.where(_att_masks(n, max_dist), s, NEG) - lse_c)
            dp = lax.dot_general(dos, vband, _BDOT_NT, preferred_element_type=F32)
            ds = (pr * (dp + corr)).astype(BF16)
            dq = lax.dot_general(ds, kband, _BDOT_NN, preferred_element_type=F32) * scale
            dkb = lax.dot_general(ds, qs, _BDOT_TN, preferred_element_type=F32) * scale
            dvb = lax.dot_general(pr.astype(BF16), dos, _BDOT_TN, preferred_element_type=F32)
            dsk = jnp.exp(sk - lse_c) * corr
            lane = lax.broadcasted_iota(jnp.int32, (8, LANES), 1)
            for p in range(cgw):
                tot = [jnp.sum(jnp.stack([dsk[i * cgw + p, h * ATT_BLOCK:(h + 1) * ATT_BLOCK] for i in range(len(rows))]),
                               axis=(0, 1)).reshape(1, 1) for h in range(2)]
                dsk_ref[:, p * LANES:(p + 1) * LANES] += jnp.where(lane == 0, tot[0], jnp.where(lane == HEAD_DIM, tot[1], 0.0))

            def gather_pairs(parts):
                if not shared:
                    return jnp.concatenate(parts, axis=1)
                tot = [parts[4 * h] + parts[4 * h + 1] + parts[4 * h + 2] + parts[4 * h + 3] for h in range(2)]
                tot = [t + pltpu.roll(t, HEAD_DIM, axis=1) for t in tot]
                return jnp.where(_lo_lanes(ATT_BLOCK), tot[0], tot[1])

            for i, r in enumerate(rows):
                mine = range(i * cgw, (i + 1) * cgw)
                dq_ref[r, :] = plus(jnp.concatenate([_unstack_heads(dq[b]) for b in mine], axis=1), aq_ref, r)
                dk_ref[r, :] = plus(ck[r, :] + gather_pairs([dkb[b, :ATT_BLOCK] for b in mine]), ak_ref, r)
                dv_ref[r, :] = plus(cv[r, :] + gather_pairs([dvb[b, :ATT_BLOCK] for b in mine]), av_ref, r)
                ck[r, :] = gather_pairs([dkb[b, ATT_BLOCK:] for b in mine])
                cv[r, :] = gather_pairs([dvb[b, ATT_BLOCK:] for b in mine])

        @pl.when(n < nb)
        def _():
            _att_streams(d, sb, work)

        @pl.when(n == nb)
        def _():
            dk_ref[...] = plus(ck[...], ak_ref, slice(None))
            dv_ref[...] = plus(cv[...], av_ref, slice(None))

    sksp = BS((1, qw), lambda g, n: (0, g))
    rows = ATT_BLOCK * d
    osp = BS((rows, kw), lambda g, n: (jnp.maximum(n - 1, 0), 0 if shared else g))
    kshape = SDS((T, Wk), F32)
    dq, dk, dv, dsk = _pcall(
        body, name=name, grid=(G, nb + 1),
        in_specs=[qsp, csp, psp, csp, psp, sksp, qsp, qsp, qsp] + ([qsp] if dlse is not None else [])
        + ([qsp, osp, osp] if acc is not None else []),
        out_specs=[qsp, osp, osp, BS((8, qw), lambda g, n: (0, g))],
        out_shape=[SDS((T, Wq), F32), kshape, kshape, SDS((8, Wq), F32)],
        scratch_shapes=[pltpu.VMEM((rows, kw), F32)] * 2,
        compiler_params=_cparams(("parallel", "arbitrary")))(q, k, k, v, v, sinks, o, lse, do,
                                                              *([dlse] if dlse is not None else []), *(acc or ()))
    return dq, dk, dv, dsk[0:1]


@functools.partial(jax.custom_vjp, nondiff_argnums=(4, 5, 6, 7))
def band_attention(q, k, v, sinks, d, max_dist, with_lse, name):
    return _band_attention_fwd(q, k, v, sinks, d, max_dist, with_lse, name)[0]


def _band_attention_fwd(q, k, v, sinks, d, max_dist, with_lse, name):
    o, lse = _att_fwd_call(q, k, v, sinks, d, max_dist, name)
    return ((o, lse) if with_lse else o), (q, k, v, sinks, o, lse)


def _band_attention_bwd(d, max_dist, with_lse, name, res, g):
    q, k, v, sinks, o, lse = res
    do, dlse = g if with_lse else (g, None)
    return _att_bwd_call(q, k, v, sinks, o, lse, do, dlse, d, max_dist, name + "_bwd")


band_attention.defvjp(_band_attention_fwd, _band_attention_bwd)


def _merge_weights(ls):
    mx = jnp.maximum(jnp.maximum(ls[0], ls[1]), ls[2])
    es = [jnp.exp(l - mx) for l in ls]
    inv = 1.0 / (es[0] + es[1] + es[2])
    return [e * inv for e in es]


def _merge_fwd_call(os_, ls_, name):
    T, W = os_[0].shape
    tt = _tile(T, 1024, 8)

    def body(o1, o2, o3, l1, l2, l3, out_ref):
        w = _merge_weights([l1[...], l2[...], l3[...]])
        out_ref[...] = w[0] * o1[...] + w[1] * o2[...] + w[2] * o3[...]

    row = BS((tt, W), lambda i: (i, 0))
    return _pcall(body, name=name, grid=(T // tt,), in_specs=[row] * 6, out_specs=row,
                  out_shape=SDS((T, W), F32), compiler_params=_cparams(("parallel",)))(*os_, *ls_)


def _merge_bwd_call(os_, ls_, do, name):
    T, W = os_[0].shape
    tt = _tile(T, 512, 8)

    def body(o1, o2, o3, l1, l2, l3, do_ref, d1, d2, d3, e1, e2, e3):
        w = _merge_weights([l1[...], l2[...], l3[...]])
        dov = do_ref[...]
        ts = [dov * o[...] for o in (o1, o2, o3)]
        mean = w[0] * ts[0] + w[1] * ts[1] + w[2] * ts[2]
        for wi, ti, dref, eref in zip(w, ts, (d1, d2, d3), (e1, e2, e3)):
            dref[...] = wi * dov
            eref[...] = wi * (ti - mean)

    row = BS((tt, W), lambda i: (i, 0))
    return _pcall(body, name=name, grid=(T // tt,), in_specs=[row] * 7, out_specs=[row] * 6,
                  out_shape=[SDS((T, W), F32)] * 6, compiler_params=_cparams(("parallel",)))(*os_, *ls_, do)


@functools.partial(jax.custom_vjp, nondiff_argnums=(2,))
def merge3(os_, ls_, name):
    return _merge_fwd_call(os_, ls_, name)


def _merge3_fwd(os_, ls_, name):
    return _merge_fwd_call(os_, ls_, name), (os_, ls_)


def _merge3_bwd(name, res, do):
    os_, ls_ = res
    out = _merge_bwd_call(os_, ls_, do, name + "_bwd")
    return tuple(out[:3]), tuple(out[3:])


merge3.defvjp(_merge3_fwd, _merge3_bwd)


def _xa_probs(qb, kb, scale):
    s = lax.dot_general(qb, kb, (((1,), (1,)), ((), ())), preferred_element_type=F32) * scale
    e = jnp.exp(s - jnp.max(s, axis=-1, keepdims=True))
    return e / jnp.sum(e, axis=-1, keepdims=True)


def _xa_fwd_call(q, kv, name):
    T, W = q.shape
    M = kv.shape[0]
    hd = XA_HEAD_DIM
    tq = _tile(T, 4096, 8)
    scale = hd ** -0.5

    def body(q_ref, k_ref, v_ref, o_ref):
        p = _xa_probs(q_ref[...].astype(BF16), k_ref[...].astype(BF16), scale)
        o_ref[...] = jnp.dot(p.astype(BF16), v_ref[...].astype(BF16), preferred_element_type=F32)

    qsp = BS((tq, hd), lambda i, h: (i, h))
    return _pcall(body, name=name, grid=(T // tq, XA_HEADS),
                  in_specs=[qsp, BS((M, hd), lambda i, h: (0, h)), BS((M, hd), lambda i, h: (0, XA_HEADS + h))],
                  out_specs=qsp, out_shape=SDS((T, W), F32),
                  compiler_params=_cparams(("parallel", "parallel")))(q, kv, kv)


def _xa_bwd_call(q, kv, do, name):
    T, W = q.shape
    M = kv.shape[0]
    hd = XA_HEAD_DIM
    tq = _tile(T, 4096, 8)
    scale = hd ** -0.5
    dn_nt = (((1,), (1,)), ((), ()))
    dn_tn = (((0,), (0,)), ((), ()))

    def body(q_ref, k_ref, v_ref, do_ref, dq_ref, dk_ref, dv_ref):
        @pl.when(pl.program_id(1) == 0)
        def _():
            dk_ref[...] = jnp.zeros_like(dk_ref)
            dv_ref[...] = jnp.zeros_like(dv_ref)

        qb, kb, vb = q_ref[...].astype(BF16), k_ref[...].astype(BF16), v_ref[...].astype(BF16)
        p = _xa_probs(qb, kb, scale)
        dob = do_ref[...].astype(BF16)
        dp = lax.dot_general(dob, vb, dn_nt, preferred_element_type=F32)
        ds = (p * (dp - jnp.sum(p * dp, axis=-1, keepdims=True))).astype(BF16)
        dq_ref[...] = jnp.dot(ds, kb, preferred_element_type=F32) * scale
        dk_ref[...] += lax.dot_general(ds, qb, dn_tn, preferred_element_type=F32) * scale
        dv_ref[...] += lax.dot_general(p.astype(BF16), dob, dn_tn, preferred_element_type=F32)

    qsp = BS((tq, hd), lambda h, i: (i, h))
    ksp = BS((M, hd), lambda h, i: (0, h))
    return _pcall(body, name=name, grid=(XA_HEADS, T // tq),
                  in_specs=[qsp, ksp, BS((M, hd), lambda h, i: (0, XA_HEADS + h)), qsp],
                  out_specs=[qsp, ksp, ksp], out_shape=[SDS((T, W), F32), SDS((M, W), F32), SDS((M, W), F32)],
                  compiler_params=_cparams(("parallel", "arbitrary")))(q, kv, kv, do)


@functools.partial(jax.custom_vjp, nondiff_argnums=(2,))
def cross_attention(q, kv, name):
    return _xa_fwd_call(q, kv, name)


def _cross_attention_fwd(q, kv, name):
    return _xa_fwd_call(q, kv, name), (q, kv)


def _cross_attention_bwd(name, res, do):
    q, kv = res
    dq, dk, dv = _xa_bwd_call(q, kv, do, name + "_bwd")
    return dq, jnp.concatenate([dk, dv], axis=1)


cross_attention.defvjp(_cross_attention_fwd, _cross_attention_bwd)


def _gate_up_swiglu_call(hn, w1t, name):
    T, K = hn.shape
    F = w1t.shape[0] // 2
    tm, tn = _tile(T, 4096), _tile(F, 256)
    nj = F // tn
    dn = (((1,), (1,)), ((), ()))

    def body(a_ref, wg_ref, wu_ref, g_ref, u_ref, act_ref):
        a = a_ref[...]
        g = lax.dot_general(a, wg_ref[...], dn, preferred_element_type=F32)
        u = lax.dot_general(a, wu_ref[...], dn, preferred_element_type=F32)
        g_ref[...] = g
        u_ref[...] = u
        act_ref[...] = ((g * _sigmoid(g)) * u).astype(BF16)

    tile = BS((tm, tn), lambda i, j: (i, j))
    return _pcall(body, name=name, grid=(T // tm, nj),
                  in_specs=[BS((tm, K), lambda i, j: (i, 0)), BS((tn, K), lambda i, j: (j, 0)),
                            BS((tn, K), lambda i, j: (j + nj, 0))],
                  out_specs=[tile, tile, tile], out_shape=[SDS((T, F), F32), SDS((T, F), F32), SDS((T, F), BF16)],
                  compiler_params=_cparams(("parallel", "parallel")))(hn, w1t, w1t)


def _down_da_swiglu_bwd_call(dout, w2, g, u, name):
    T, F = g.shape
    Dm = dout.shape[1]
    tm, tn = _tile(T, 2048), _tile(F, 256)

    def body(d_ref, w_ref, g_ref, u_ref, o_ref):
        d = lax.dot_general(d_ref[...].astype(BF16), w_ref[...], (((1,), (1,)), ((), ())), preferred_element_type=F32)
        g = g_ref[...]
        sg = _sigmoid(g)
        o_ref[0] = (d * u_ref[...] * (sg * (1.0 + g * (1.0 - sg)))).astype(BF16)
        o_ref[1] = (d * (g * sg)).astype(BF16)

    tile = BS((tm, tn), lambda i, j: (i, j))
    return _pcall(body, name=name, grid=(T // tm, F // tn),
                  in_specs=[BS((tm, Dm), lambda i, j: (i, 0)), BS((tn, Dm), lambda i, j: (j, 0)), tile, tile],
                  out_specs=BS((2, tm, tn), lambda i, j: (0, i, j)), out_shape=SDS((2, T, F), BF16),
                  compiler_params=_cparams(("parallel", "parallel")))(dout, w2, g, u)


@functools.partial(jax.custom_vjp, nondiff_argnums=(6,))
def ffn_block(h, g, w1b, w1c, w2b, w2c, name):
    return _ffn_fwd(h, g, w1b, w1c, w2b, w2c, name)[0]


def _ffn_fwd(h, g, w1b, w1c, w2b, w2c, name):
    hn = _rms_fwd_call(h, g, name + "_norm", BF16)
    gate, up, act = _gate_up_swiglu_call(hn, w1b, name + "_gu")
    out = _mm(act, w2b, add=h, name=name + "_down", tm=1024, tn=1024, tk=2816)
    return out, (h, g, hn, gate, up, act, w1b, w2b)


def _ffn_bwd(name, res, dout):
    h, g, hn, gate, up, act, w1b, w2b = res
    dw2 = _mm(act, dout, ta=True, name=name + "_down_dw", tm=1408, tn=1024, tk=1024)
    dgu = _down_da_swiglu_bwd_call(dout, w2b, gate, up, name + "_down_da")
    dw1 = _mm(dgu, hn, ta=True, name=name + "_gu_dw", tm=1408, tn=1024, tk=2048)
    dh, dg = _mm_rms_bwd(dgu, w1b, h, g, dout, name + "_gu_da", wt=True)
    return dh, dg.reshape(g.shape), jnp.zeros_like(w1b), dw1, jnp.zeros_like(w2b), dw2


ffn_block.defvjp(_ffn_fwd, _ffn_bwd)


def _final_call(h, g, target, name):
    T, Dm = h.shape
    tt = _tile(T, 1024, 8)

    def body(x_ref, g_ref, t_ref, loss_ref, dx_ref, dg_ref):
        @pl.when(pl.program_id(0) == 0)
        def _():
            loss_ref[...] = jnp.zeros_like(loss_ref)
            dg_ref[...] = jnp.zeros_like(dg_ref)

        xv, gv = x_ref[...], g_ref[...]
        r = lax.rsqrt(jnp.mean(xv * xv, axis=-1, keepdims=True) + NORM_EPS)
        xh = xv * r
        err = xh * gv - t_ref[...]
        loss_ref[...] += 0.5 * jnp.sum(jnp.mean(err * err, axis=-1, keepdims=True), axis=0, keepdims=True)
        dy = err * (1.0 / Dm)
        dyg = dy * gv
        dx_ref[...] = r * (dyg - xh * jnp.mean(dyg * xh, axis=-1, keepdims=True))
        dg_ref[...] += jnp.sum(dy * xh, axis=0, keepdims=True)

    row = BS((tt, Dm), lambda i: (i, 0))
    vec = BS((1, Dm), lambda i: (0, 0))
    return _pcall(body, name=name, grid=(T // tt,), in_specs=[row, vec, row],
                  out_specs=[BS((1, 1), lambda i: (0, 0)), row, vec],
                  out_shape=[SDS((1, 1), F32), SDS((T, Dm), F32), SDS((1, Dm), F32)],
                  compiler_params=_cparams(("arbitrary",)))(h, g.reshape(1, Dm), target)


ADAMW_BLOCK_ELEMS = 256 * 1024


def _adamw_call(parts, w, m, v, name):
    shape = w.shape
    if not isinstance(parts, (list, tuple)):
        parts, shape3 = [parts], (1,) + shape
    else:
        shape3 = shape
    n_lead = shape3[0]
    r, N = shape3[-2], shape3[-1]
    Ld = math.prod(shape3[1:-2])
    w, m, v = (t.reshape(n_lead * Ld, r, N) for t in (w, m, v))
    tr = _tile(r, max(8, ADAMW_BLOCK_ELEMS // N), 8)
    c1 = 1.0 - ADAM_B1 ** ADAM_STEP
    c2 = 1.0 - ADAM_B2 ** ADAM_STEP
    outs = None
    for lead, p in enumerate(parts):
        def body(p_ref, w_ref, m_ref, v_ref, *rest):
            g_ref, d_ref, nm_ref, nv_ref = rest[-4:]
            g = p_ref[0]
            for j in range(1, N_DEV):
                g = g + p_ref[j]
            nm = ADAM_B1 * m_ref[...] + (1.0 - ADAM_B1) * g
            nv = ADAM_B2 * v_ref[...] + (1.0 - ADAM_B2) * (g * g)
            g_ref[...] = g
            nm_ref[...] = nm
            nv_ref[...] = nv
            d_ref[...] = -ADAM_LR * ((nm / c1) / (jnp.sqrt(nv / c2) + ADAM_EPS) + ADAM_WD * w_ref[...])

        base = lead * Ld
        row = BS((1, tr, N), lambda l, i, base=base: (base + l, i, 0))
        prev = [] if outs is None else list(outs)
        outs = _pcall(body, name=f"{name}_{lead}", grid=(Ld, r // tr),
                      in_specs=[BS((N_DEV, 1, tr, N), lambda l, i: (0, l, i, 0)), row, row, row]
                      + [BS(memory_space=pl.ANY)] * len(prev),
                      out_specs=[row] * 4, out_shape=[SDS((n_lead * Ld, r, N), F32)] * 4,
                      input_output_aliases={4 + j: j for j in range(len(prev))},
                      compiler_params=_cparams(("parallel", "parallel")))(p.reshape(N_DEV, Ld, r, N), w, m, v, *prev)
    return [t.reshape(shape) for t in outs]


def _place():
    return lax.axis_index("x"), lax.axis_index("y"), lax.axis_index("c")


def _all_gather(xs, name):
    n = len(xs)
    pairs = [(i, l) for i, x in enumerate(xs) for l in range(x.shape[0])]

    def body(*refs):
        x_refs, o_refs = refs[:n], refs[n:2 * n]
        send_sems, recv_sems, local_sems = refs[2 * n:]
        x_, y_, c_ = _place()
        me, sibling = (x_, y_, c_), (x_, y_, 1 - c_)
        chips = [(1 - x_, y_), (x_, 1 - y_), (1 - x_, 1 - y_)]

        def copy(e, k, block, to, from_input=False):
            i, l = pairs[e]
            px, py, pc = block
            dst = o_refs[i].at[l, 4 * px + 2 * py + pc]
            return pltpu.make_async_remote_copy(
                src_ref=x_refs[i].at[l] if from_input else dst, dst_ref=dst,
                send_sem=send_sems.at[7 * e + k], recv_sem=recv_sems.at[7 * e + k],
                device_id=to, device_id_type=pl.DeviceIdType.MESH)

        every = range(len(pairs))
        mine = [pltpu.make_async_copy(x_refs[i].at[l], o_refs[i].at[l, 4 * x_ + 2 * y_ + c_], local_sems.at[e])
                for e, (i, l) in enumerate(pairs)]
        for cp in mine:
            cp.start()
        first = [copy(e, 0, me, sibling, True) for e in every]
        first += [copy(e, 1 + j, me, (*chip, c_), True) for j, chip in enumerate(chips) for e in every]
        for cp in first:
            cp.start()
        passed = []
        for j, chip in enumerate(chips):
            for e in every:
                copy(e, 1 + j, (*chip, c_), me).wait_recv()
            for e in every:
                cp = copy(e, 4 + j, (*chip, c_), sibling)
                cp.start()
                passed.append(cp)
        for e in every:
            copy(e, 0, sibling, me).wait_recv()
        for j, chip in enumerate(chips):
            for e in every:
                copy(e, 4 + j, (*chip, 1 - c_), me).wait_recv()
        for cp in first + passed:
            cp.wait_send()
        for cp in mine:
            cp.wait()

    any_spec = BS(memory_space=pl.ANY)
    return _pcall(body, name=name, in_specs=[any_spec] * n, out_specs=[any_spec] * n,
                  out_shape=[SDS((x.shape[0], N_DEV) + x.shape[1:], x.dtype) for x in xs],
                  scratch_shapes=[pltpu.SemaphoreType.DMA((7 * len(pairs),)), pltpu.SemaphoreType.DMA((7 * len(pairs),)),
                                  pltpu.SemaphoreType.DMA((len(pairs),))],
                  compiler_params=pltpu.CompilerParams(has_side_effects=True))(*xs)


def _peer_of(k, place):
    x_, y_, c_ = place
    fx, fy, fc = (k >> 2) & 1, (k >> 1) & 1, k & 1
    return fx + x_ - 2 * fx * x_, fy + y_ - 2 * fy * y_, fc + c_ - 2 * fc * c_


def _split_copy(src_ref, land_ref, send_sems, recv_sems, e, k, place, scatter):
    x_, y_, c_ = place
    px, py, pc = _peer_of(k, place)
    return pltpu.make_async_remote_copy(
        src_ref=src_ref.at[4 * px + 2 * py + pc] if scatter else src_ref, dst_ref=land_ref.at[4 * x_ + 2 * y_ + c_],
        send_sem=send_sems.at[7 * e + k - 1], recv_sem=recv_sems.at[7 * e + k - 1],
        device_id=(px, py, pc), device_id_type=pl.DeviceIdType.MESH)


def _own_copy(src_ref, land_ref, sems, slot, place, scatter):
    x_, y_, c_ = place
    me = 4 * x_ + 2 * y_ + c_
    return pltpu.make_async_copy(src_ref.at[me] if scatter else src_ref, land_ref.at[me], sems.at[slot])


_HBM_SPEC = BS(memory_space=pltpu.HBM)
_SEM_SPEC = BS(memory_space=pltpu.SEMAPHORE)
_EFFECT = pltpu.SideEffectType.DATAFLOW_SIDE_EFFECTING


def _copies_start(srcs, scatter, name, thru=None):
    n = len(srcs)
    lands = [lax.empty(s.shape if scatter else (N_DEV,) + s.shape, s.dtype) for s in srcs]
    passed = srcs + lands + list(thru or ())

    def body(*refs):
        src_refs, land_refs = refs[:n], refs[n:2 * n]
        send_sems, recv_sems = refs[len(passed)], refs[len(passed) + 1]
        token = refs[-1]
        place = _place()
        for e in range(n):
            for k in range(1, N_DEV):
                _split_copy(src_refs[e], land_refs[e], send_sems, recv_sems, e, k, place, scatter).start()
            _own_copy(src_refs[e], land_refs[e], send_sems, 7 * n + e, place, scatter).start()
        token[...] = jnp.zeros_like(token)

    hbm = lambda t: pltpu.with_memory_space_constraint(t, pltpu.HBM)
    out = _pcall(body, name=name,
                 out_shape=(pltpu.SemaphoreType.DMA((8 * n,)), pltpu.SemaphoreType.DMA((7 * n,)),
                            *[pltpu.HBM(t.shape, t.dtype) for t in passed], SDS((8, LANES), F32)),
                 in_specs=[_HBM_SPEC] * len(passed),
                 out_specs=(_SEM_SPEC, _SEM_SPEC, *[_HBM_SPEC] * len(passed), BS(memory_space=pltpu.VMEM)),
                 input_output_aliases={i: 2 + i for i in range(len(passed))},
                 compiler_params=pltpu.CompilerParams(has_side_effects=_EFFECT))(*[hbm(t) for t in passed])
    return out[0], out[1], list(out[2:2 + n]), list(out[2 + n:2 + 2 * n]), list(out[2 + 2 * n:-1])


def _copies_wait(started, which, scatter, after, name):
    send_sems, recv_sems, srcs, lands, _ = started
    n, n_started = len(which), len(srcs)
    after = list(after) if isinstance(after, (list, tuple)) else [after]

    def body(*refs):
        src_refs, land_refs = refs[:n], refs[n:2 * n]
        send_s, recv_s = refs[2 * n], refs[2 * n + 1]
        place = _place()
        for j, e in enumerate(which):
            for k in range(1, N_DEV):
                cp = _split_copy(src_refs[j], land_refs[j], send_s, recv_s, e, k, place, scatter)
                cp.wait_send()
                cp.wait_recv()
            _own_copy(src_refs[j], land_refs[j], send_s, 7 * n_started + e, place, scatter).wait()

    args = [srcs[e] for e in which] + [lands[e] for e in which]
    out = _pcall(body, name=name, out_shape=tuple(pltpu.HBM(t.shape, t.dtype) for t in args),
                 in_specs=[_HBM_SPEC] * (2 * n) + [_SEM_SPEC, _SEM_SPEC] + [BS(memory_space=pl.ANY)] * len(after),
                 out_specs=tuple([_HBM_SPEC] * (2 * n)), input_output_aliases={i: i for i in range(2 * n)},
                 compiler_params=pltpu.CompilerParams(has_side_effects=_EFFECT))(*args, send_sems, recv_sems, *after)
    return list(out[:n]), list(out[n:])


def _pad_flat(t, quantum=PACK_QUANTUM):
    f = t.reshape(-1)
    pad = (-f.shape[0]) % quantum
    return jnp.pad(f, (0, pad)) if pad else f


def _pack(arrs, dtype):
    return jnp.concatenate([_pad_flat(a.astype(dtype)) for a in arrs]).reshape(-1, LANES)


def _unpack(buf, shapes, lead=()):
    flat = buf.reshape(lead + (-1,))
    out, off = [], 0
    for s in shapes:
        n = math.prod(s)
        out.append(flat[..., off:off + n].reshape(lead + tuple(s)))
        off += n + (-n) % PACK_QUANTUM
    return out


def _full_from_gathered(g, axis):
    t = jnp.moveaxis(g, 0, axis)
    s = t.shape
    return t.reshape(s[:axis] + (s[axis] * s[axis + 1],) + s[axis + 2:])


def _parts_from_full(t, axis):
    s = t.shape
    t = t.reshape(s[:axis] + (N_DEV, s[axis] // N_DEV) + s[axis + 1:])
    return jnp.moveaxis(t, axis, 0)


def _head_rows(t):
    return jnp.repeat(t, HEAD_DIM).reshape(1, -1)


@functools.partial(jax.custom_vjp, nondiff_argnums=(3,))
def _dilated_attention(q, k, v, name):
    return _dilated_attention_fwd(q, k, v, name)[0]


def _dilated_attention_fwd(q, k, v, name):
    no_sink = jnp.full((1, q.shape[1]), NEG, F32)
    outs, lses = zip(*[_att_fwd_call(q, k, v, no_sink, d, ATT_BLOCK, f"{name}_d{d}") for d in DILATIONS])
    return _merge_fwd_call(outs, lses, name + "_merge"), (q, k, v, outs, lses)


def _dilated_attention_bwd(name, res, do):
    q, k, v, outs, lses = res
    no_sink = jnp.full((1, q.shape[1]), NEG, F32)
    parts = _merge_bwd_call(outs, lses, do, name + "_merge_bwd")
    acc = None
    for i, d in enumerate(DILATIONS):
        acc = _att_bwd_call(q, k, v, no_sink, outs[i], lses[i], parts[i], parts[3 + i], d, ATT_BLOCK,
                            f"{name}_d{d}_bwd", acc)[:3]
    return acc


_dilated_attention.defvjp(_dilated_attention_fwd, _dilated_attention_bwd)


STAGES = (
    ("proj0", ('mix_norm',), (('ab_w_in', 0),)),
    ("mixer0", ('lru_conv_w', 'lru_conv_b', 'lru_ba', 'lru_bx', 'lru_lambda'),
     (('lru_wa', 0), ('lru_wx', 0), ('ab_w_out', 0))),
    ("xa0", ('xa_norm', 'xa_mem_norm'), (('xa_wq', 0), ('xa_wkv', 0), ('xa_wo', 0))),
    ("ffn0", ('ffn_norm',), (('ffn_w_gate_up', 0), ('ffn_w_down', 0))),
    ("mixer1", ('mix_norm', 'c_b_qkv', 'c_sinks', 'c_b_out'), (('c_w_qkv', 0), ('c_w_out', 0))),
    ("xa1", ('xa_norm', 'xa_mem_norm'), (('xa_wq', 1), ('xa_wkv', 1), ('xa_wo', 1))),
    ("ffn1", ('ffn_norm',), (('ffn_w_gate_up', 1), ('ffn_w_down', 1))),
)


def _stage_fn(stage, Wb, tabs, mem):
    layer = int(stage[-1])
    L = f"l{layer}"

    def run(S, Cw, h):
        def lin(a, key, bias, add, name, rows=None):
            wb, wc = Wb[key], Cw[key]
            if rows is not None:
                wb, wc = wb[rows], wc[rows]
            return linear(a, wb, wc, bias, add, name)

        def norm_lin(a, gain, key, bias, name):
            return norm_linear(a, gain, Wb[key], Cw[key], bias, key[0] in COLUMN_CUT, name)

        if stage == "mixer0":
            h, x_br, y_br, q, k, v = h
            C = S['lru_conv_w'].shape[-1]
            xc = conv4(x_br, S['lru_conv_w'][0], S['lru_conv_b'][0], L + "_conv")
            ga, gx, xc = lru_gates(xc, Wb['lru_wa', 0], Cw['lru_wa', 0], S['lru_ba'][0],
                               Wb['lru_wx', 0], Cw['lru_wx', 0], S['lru_bx'][0], L + "_gates")
            rec = lru_scan(xc, ga, gx, y_br, S['lru_lambda'][0], L + "_scan")
            att = _dilated_attention(q, k, v, L + "_att")
            h = lin(att, ('ab_w_out', 0), None, h, L + "_w_out_att", slice(C, None))
            return lin(rec, ('ab_w_out', 0), None, h, L + "_w_out_rec", slice(0, C))
        if stage == "mixer1":
            qw = C_HEADS * HEAD_DIM
            kw = C_KV_HEADS * HEAD_DIM
            q, k, v, h = norm_linear_pieces(h, S['mix_norm'][1], Wb['c_w_qkv', 0], Cw['c_w_qkv', 0], S['c_b_qkv'][0], tabs,
                                            ((qw, True), (kw, True), (kw, False)), L + "_w_qkv")
            o = band_attention(q, k, v, _head_rows(S['c_sinks'][0]), 1, ATT_BLOCK - 1, False, L + "_att")
            return lin(o, ('c_w_out', 0), S['c_b_out'][0], h, L + "_w_out")
        if stage.startswith("xa"):
            xq, h = norm_lin(h, S['xa_norm'][layer], ('xa_wq', layer), None, L + "_xa_wq")
            xkv, _ = norm_lin(mem, S['xa_mem_norm'][layer], ('xa_wkv', layer), None, L + "_xa_wkv")
            return lin(cross_attention(xq, xkv, L + "_xa"), ('xa_wo', layer), None, h, L + "_xa_wo")
        gu, down = ('ffn_w_gate_up', layer), ('ffn_w_down', layer)
        return ffn_block(h, S['ffn_norm'][layer], Wb[gu], Cw[gu], Wb[down], Cw[down], L + "_ffn")

    return run


def kernel(x, mem, mix_norm, ab_w_in, lru_conv_w, lru_conv_b, lru_wa, lru_ba, lru_wx, lru_bx, lru_lambda, ab_w_out, c_w_qkv, c_b_qkv, c_sinks, c_w_out, c_b_out, xa_norm, xa_mem_norm, xa_wq, xa_wkv, xa_wo, ffn_norm, ffn_w_gate_up, ffn_w_down, final_norm, loss_target, m_mix_norm, m_ab_w_in, m_lru_conv_w, m_lru_conv_b, m_lru_wa, m_lru_ba, m_lru_wx, m_lru_bx, m_lru_lambda, m_ab_w_out, m_c_w_qkv, m_c_b_qkv, m_c_sinks, m_c_w_out, m_c_b_out, m_xa_norm, m_xa_mem_norm, m_xa_wq, m_xa_wkv, m_xa_wo, m_ffn_norm, m_ffn_w_gate_up, m_ffn_w_down, m_final_norm, v_mix_norm, v_ab_w_in, v_lru_conv_w, v_lru_conv_b, v_lru_wa, v_lru_ba, v_lru_wx, v_lru_bx, v_lru_lambda, v_ab_w_out, v_c_w_qkv, v_c_b_qkv, v_c_sinks, v_c_w_out, v_c_b_out, v_xa_norm, v_xa_mem_norm, v_xa_wq, v_xa_wkv, v_xa_wo, v_ffn_norm, v_ffn_w_gate_up, v_ffn_w_down, v_final_norm):
    w_loc = dict(zip(WEIGHT_NAMES, (mix_norm, ab_w_in, lru_conv_w, lru_conv_b, lru_wa, lru_ba, lru_wx, lru_bx, lru_lambda, ab_w_out, c_w_qkv, c_b_qkv, c_sinks, c_w_out, c_b_out, xa_norm, xa_mem_norm, xa_wq, xa_wkv, xa_wo, ffn_norm, ffn_w_gate_up, ffn_w_down, final_norm)))
    m_loc = dict(zip(WEIGHT_NAMES, (m_mix_norm, m_ab_w_in, m_lru_conv_w, m_lru_conv_b, m_lru_wa, m_lru_ba, m_lru_wx, m_lru_bx, m_lru_lambda, m_ab_w_out, m_c_w_qkv, m_c_b_qkv, m_c_sinks, m_c_w_out, m_c_b_out, m_xa_norm, m_xa_mem_norm, m_xa_wq, m_xa_wkv, m_xa_wo, m_ffn_norm, m_ffn_w_gate_up, m_ffn_w_down, m_final_norm)))
    v_loc = dict(zip(WEIGHT_NAMES, (v_mix_norm, v_ab_w_in, v_lru_conv_w, v_lru_conv_b, v_lru_wa, v_lru_ba, v_lru_wx, v_lru_bx, v_lru_lambda, v_ab_w_out, v_c_w_qkv, v_c_b_qkv, v_c_sinks, v_c_w_out, v_c_b_out, v_xa_norm, v_xa_mem_norm, v_xa_wq, v_xa_wkv, v_xa_wo, v_ffn_norm, v_ffn_w_gate_up, v_ffn_w_down, v_final_norm)))

    first_keys = list(STAGES[0][2])
    keys = [key for _, _, stage_keys in STAGES[1:] for key in stage_keys]
    shards = [_shard_view(n, w_loc[n])[l].astype(BF16) for n, l in keys]
    first_g = _all_gather([_pack([w_loc[n] for n in SMALL], F32)[None]]
                          + [_shard_view(n, w_loc[n])[l].astype(BF16)[None] for n, l in first_keys], "gather_first")
    gather = _copies_start(shards, False, "gather_start", thru=[first_g[0]])
    small_g = gather[4][0][0]
    Wb = {key: _full_from_gathered(g[0], _layer_shard_axis(key[0])) for key, g in zip(first_keys, first_g[1:])}
    S = {n: w_loc[n] for n in REPLICATED}
    for n, t in zip(SMALL, _unpack(small_g, [w_loc[n].shape for n in SMALL], lead=(N_DEV,))):
        S[n] = _full_from_gathered(t, SHARD_AXIS[n])

    tabs = _rope_tables(x.shape[1])
    w_in = Wb[first_keys[0]]
    hn0 = _rms_fwd_call(x[0], S['mix_norm'][0], "l0_w_in_norm", BF16)
    lru_w, att_w = N_DEV * lru_conv_w.shape[-1], B_HEADS * HEAD_DIM
    pieces = (("x", lru_w, False), ("y", lru_w, False), ("q", att_w, True), ("k", att_w, True), ("v", att_w, False))
    h, row = [x[0]], 0
    for piece, width, rotated in pieces:
        h.append(_mm(hn0, w_in[row:row + width], tb=True, rope=tabs if rotated else None, name="l0_w_in_" + piece,
                     tm=2048, tn=1024, tk=1024))
        row += width
    h = tuple(h)
    vjps = []
    for stage, small_names, stage_keys in STAGES[1:]:
        which = [keys.index(key) for key in stage_keys]
        _, lands = _copies_wait(gather, which, False, jax.tree.leaves(h)[-1], "gather_wait_" + stage)
        for e, land in zip(which, lands):
            Wb[keys[e]] = _full_from_gathered(land, _layer_shard_axis(keys[e][0]))
        carriers = {key: jnp.zeros(Wb[key].shape, F32) for key in stage_keys}
        h, vjp_fn = jax.vjp(_stage_fn(stage, Wb, tabs, mem[0]), {n: S[n] for n in small_names}, carriers, h)
        vjps.append(vjp_fn)
    loss_part, dh, dg_final = _final_call(h, S['final_norm'], loss_target[0], "final_loss")

    grads = {'final_norm': dg_final.reshape(final_norm.shape)}
    exchanges, send_keys, send_parts = [], [], []

    def start_exchange(stage, dh):
        leaves, tree = jax.tree.flatten(dh)
        started = _copies_start(list(send_parts), True, "grad_start_" + stage, thru=leaves)
        exchanges.append((stage, started, list(send_keys)))
        send_keys.clear()
        send_parts.clear()
        return jax.tree.unflatten(tree, started[4])

    for (stage, small_names, stage_keys), vjp_fn in zip(reversed(STAGES[1:]), reversed(vjps)):
        g_small, g_big, dh = vjp_fn(dh)
        for n in small_names:
            grads[n] = grads[n] + g_small[n] if n in grads else g_small[n]
        send_keys += list(stage_keys)
        send_parts += [_parts_from_full(g_big[key], _layer_shard_axis(key[0])) for key in stage_keys]
        if stage == "xa1":
            continue
        if stage == "mixer0":
            small_parts = [_parts_from_full(grads[n], SHARD_AXIS[n]) for n in SMALL]
            send_keys.append("small")
            send_parts.append(jnp.stack([_pack([p[j] for p in small_parts], F32) for j in range(N_DEV)]))
        dh = start_exchange(stage, dh)
    d_res, d_proj = dh[0], _join_pieces_call(dh[1:], [rotated for _, _, rotated in pieces], tabs, "l0_w_in_dproj")[0]
    send_keys.append(first_keys[0])
    send_parts.append(_parts_from_full(_mm(d_proj, hn0, ta=True, name="l0_w_in_dw", tm=1408, tn=1024, tk=2048),
                                       _layer_shard_axis(first_keys[0][0])))
    d_res, d_proj = start_exchange("proj0", (d_res, d_proj))
    dx, dg0 = _mm_rms_bwd(d_proj, w_in, x[0], S['mix_norm'][0], d_res, "l0_w_in_da", wt=True)
    grads['mix_norm'] = grads['mix_norm'] + jnp.concatenate([dg0, jnp.zeros_like(dg0)], axis=0)
    rep_names = REPLICATED + ["loss"]
    grads["loss"] = loss_part
    zero = jnp.zeros((1, 1), F32)
    for d in (w_loc, m_loc, v_loc):
        d["loss"] = zero
    rep_started = _copies_start([_pack([grads[n] for n in rep_names], F32)], False, "rep_grads_start")

    parts, out = {}, {}

    def end_exchange(stage, started, ex_keys, after):
        _, lands = _copies_wait(started, list(range(len(ex_keys))), True, after, "grad_wait_" + stage)
        parts.update(zip(ex_keys, lands))

    def adamw(p, names, call_name):
        if len(names) == 1:
            n = names[0]
            res = _adamw_call(p, _shard_view(n, w_loc[n]), _shard_view(n, m_loc[n]), _shard_view(n, v_loc[n]), call_name)
            for kind, t in zip(("grad", "delta", "new_m", "new_v"), res):
                out[kind, n] = _shard_view(n, t)
        else:
            res = _adamw_call(p, *[_pack([d[n] for n in names], F32) for d in (w_loc, m_loc, v_loc)], call_name)
            for kind, buf in zip(("grad", "delta", "new_m", "new_v"), res):
                for n, t in zip(names, _unpack(buf, [w_loc[n].shape for n in names])):
                    out[kind, n] = t

    for ex in exchanges[:-1]:
        end_exchange(*ex, dx)
    last_names = {key[0] for key in exchanges[-1][2]}
    for n in BIG:
        if n not in last_names:
            adamw([parts[n, l] for l in range(w_loc[n].shape[0])], [n], "adamw_" + n)
    adamw(parts["small"], SMALL, "adamw_small")
    end_exchange(*exchanges[-1], [out["new_v", n] for n in BIG if n not in last_names])
    for n in BIG:
        if n in last_names:
            adamw([parts[n, l] for l in range(w_loc[n].shape[0])], [n], "adamw_" + n)
    _, rep_land = _copies_wait(rep_started, [0], False, out["new_v", "ab_w_in"], "rep_grads_wait")
    adamw(rep_land[0], rep_names, "adamw_replicated")
    loss = out["grad", "loss"][0, 0]

    return (loss, dx[None], *[out[kind, n] for kind in ("grad", "delta", "new_m", "new_v") for n in WEIGHT_NAMES])
```

```python
import functools
import math

import jax
import jax.numpy as jnp
from jax import lax
from jax.experimental import pallas as pl
from jax.experimental.pallas import tpu as pltpu

F32 = jnp.float32
BF16 = jnp.bfloat16
SDS = jax.ShapeDtypeStruct
BS = pl.BlockSpec

N_DEV = 8
NORM_EPS = 1e-6
ROPE_THETA = 500000.0
HEAD_DIM = 64
ROT_DIM = 16
ATT_BLOCK = 128
LRU_C = 8.0
LRU_HEADS = 4
DILATIONS = (1, 4, 16)
B_HEADS = 8
C_HEADS = 16
C_KV_HEADS = 2
XA_HEADS = 4
XA_HEAD_DIM = 128
NEG = -1e30
ADAM_LR, ADAM_B1, ADAM_B2, ADAM_EPS, ADAM_WD, ADAM_STEP = 0.001, 0.9, 0.999, 1e-08, 0.01, 10
LANES = 128
VMEM_LIMIT = 48 * 1024 * 1024

WEIGHT_NAMES = ['mix_norm', 'ab_w_in', 'lru_conv_w', 'lru_conv_b', 'lru_wa', 'lru_ba', 'lru_wx', 'lru_bx',
                'lru_lambda', 'ab_w_out', 'c_w_qkv', 'c_b_qkv', 'c_sinks', 'c_w_out', 'c_b_out', 'xa_norm',
                'xa_mem_norm', 'xa_wq', 'xa_wkv', 'xa_wo', 'ffn_norm', 'ffn_w_gate_up', 'ffn_w_down', 'final_norm']
SHARD_AXIS = {'ab_w_in': 2, 'lru_conv_w': 2, 'lru_wa': 2, 'lru_ba': 2, 'lru_wx': 2, 'lru_bx': 2, 'ab_w_out': 1,
              'c_w_qkv': 2, 'c_b_qkv': 1, 'c_w_out': 1, 'c_b_out': 1, 'xa_wq': 1, 'xa_wkv': 1, 'xa_wo': 2,
              'ffn_w_gate_up': 2, 'ffn_w_down': 1}
BIG = ['ab_w_in', 'lru_wa', 'lru_wx', 'ab_w_out', 'c_w_qkv', 'c_w_out', 'xa_wq', 'xa_wkv', 'xa_wo',
       'ffn_w_gate_up', 'ffn_w_down']
SMALL = ['lru_conv_w', 'lru_ba', 'lru_bx', 'c_b_qkv', 'c_b_out']
REPLICATED = [n for n in WEIGHT_NAMES if n not in SHARD_AXIS]
COLUMN_CUT = ('ab_w_in', 'c_w_qkv', 'ffn_w_gate_up')
PACK_QUANTUM = 2048


def _shard_view(name, t):
    return jnp.swapaxes(t, -1, -2) if name in COLUMN_CUT else t


def _layer_shard_axis(name):
    return 0 if name in COLUMN_CUT else SHARD_AXIS[name] - 1


def _pcall(body, **kw):
    return pl.pallas_call(body, **kw)


def _cparams(sem=None):
    return pltpu.CompilerParams(dimension_semantics=sem, vmem_limit_bytes=VMEM_LIMIT)


def _tile(n, target, mult=LANES):
    if n <= target:
        return n
    t = (target // mult) * mult
    while t >= mult:
        if n % t == 0:
            return t
        t -= mult
    return n


def _sigmoid(x):
    return 1.0 / (1.0 + jnp.exp(-x))


def _expm1(x):
    small = x * (1.0 + x * (0.5 + x * (1.0 / 6.0 + x * (1.0 / 24.0))))
    return jnp.where(jnp.abs(x) < 0.03, small, jnp.exp(x) - 1.0)


_GELU_C = math.sqrt(2.0 / math.pi)


def _gelu_parts(y):
    y2 = y * y
    th = jnp.tanh(_GELU_C * (y + 0.044715 * y * y2))
    g = 0.5 * y * (1.0 + th)
    dg = 0.5 * (1.0 + th) + 0.5 * y * (1.0 - th * th) * _GELU_C * (1.0 + 3.0 * 0.044715 * y2)
    return g, dg


def _rotate(xv, tab_refs, inverse=False):
    W = xv.shape[1]
    half = ROT_DIM // 2
    c, sa, sb = (jnp.tile(t[...], (1, W // LANES)) for t in tab_refs)
    if not inverse:
        return xv * c + pltpu.roll(xv, half, axis=1) * sa + pltpu.roll(xv, W - half, axis=1) * sb
    return xv * c + pltpu.roll(xv * sa, W - half, axis=1) + pltpu.roll(xv * sb, half, axis=1)


def _mm(a, b, *, ta=False, tb=False, bias=None, add=None, rope=None, name, tm=512, tn=512, tk=2048):
    halves = a.ndim == 3
    if halves:
        M, K = 2 * a.shape[2], a.shape[1]
        tm = min(tm, a.shape[2])
    else:
        M, K = (a.shape[1], a.shape[0]) if ta else a.shape
    N = b.shape[0] if tb else b.shape[1]
    tm, tn, tk = _tile(a.shape[2] if halves else M, tm), _tile(N, tn), _tile(K, tk)
    nk = K // tk
    dn = (((0 if ta else 1,), (1 if tb else 0,)), ((), ()))

    def body(*refs):
        a_ref, b_ref = refs[0], refs[1]
        pos = 2
        bias_ref = add_ref = None
        if bias is not None:
            bias_ref = refs[pos]
            pos += 1
        if add is not None:
            add_ref = refs[pos]
            pos += 1
        tab_refs = refs[pos:pos + 3] if rope is not None else None
        pos += 3 if rope is not None else 0
        o_ref = refs[pos]
        part = lax.dot_general(a_ref[...].astype(BF16), b_ref[...].astype(BF16), dn, preferred_element_type=F32)

        def finish(r):
            if bias_ref is not None:
                r = r + bias_ref[...]
            if add_ref is not None:
                r = r + add_ref[...]
            o_ref[...] = r if tab_refs is None else _rotate(r, tab_refs)

        if nk == 1:
            finish(part)
            return
        acc_ref = refs[pos + 1]
        k = pl.program_id(2)

        @pl.when(k == 0)
        def _():
            acc_ref[...] = part

        @pl.when((k > 0) & (k < nk - 1))
        def _():
            acc_ref[...] += part

        @pl.when(k == nk - 1)
        def _():
            finish(acc_ref[...] + part)

    per_half = (M // 2) // tm if halves else 0
    in_specs = [BS((None, tk, tm), lambda i, j, k: (i // per_half, k, i % per_half)) if halves
                else BS((tk, tm), lambda i, j, k: (k, i)) if ta else BS((tm, tk), lambda i, j, k: (i, k)),
                BS((tn, tk), lambda i, j, k: (j, k)) if tb else BS((tk, tn), lambda i, j, k: (k, j))]
    args = [a, b]
    if bias is not None:
        in_specs.append(BS((1, tn), lambda i, j, k: (0, j)))
        args.append(bias.reshape(1, N))
    if add is not None:
        in_specs.append(BS((tm, tn), lambda i, j, k: (i, j)))
        args.append(add)
    if rope is not None:
        in_specs += [BS((tm, LANES), lambda i, j, k: (i, 0))] * 3
        args += list(rope)
    return _pcall(body, name=name, grid=(M // tm, N // tn, nk), in_specs=in_specs,
                  out_specs=BS((tm, tn), lambda i, j, k: (i, j)), out_shape=SDS((M, N), F32),
                  scratch_shapes=[pltpu.VMEM((tm, tn), F32)] if nk > 1 else [],
                  compiler_params=_cparams(("parallel", "parallel", "arbitrary")))(*args)


def _mm_rms_bwd(dy, wb, x, g, add, name, wt=False):
    halves = dy.ndim == 3
    T, Kc = (dy.shape[1], 2 * dy.shape[2]) if halves else dy.shape
    Dm = wb.shape[1] if wt else wb.shape[0]
    tm, tk = _tile(T, 1024), _tile(Kc // 2 if halves else Kc, 1408)
    nk = Kc // tk
    per_half = (Kc // 2) // tk if halves else 0
    dn = (((1,), (0 if wt else 1,)), ((), ()))

    def body(*refs):
        dy_ref, w_ref, x_ref, g_ref = refs[:4]
        dx_ref, dg_ref, acc_ref = refs[-3:]
        i, k = pl.program_id(0), pl.program_id(1)
        part = lax.dot_general(dy_ref[...].astype(BF16), w_ref[...], dn, preferred_element_type=F32)

        @pl.when(k == 0)
        def _():
            acc_ref[...] = part

        @pl.when((k > 0) & (k < nk - 1))
        def _():
            acc_ref[...] += part

        @pl.when((i == 0) & (k == 0))
        def _():
            dg_ref[...] = jnp.zeros_like(dg_ref)

        @pl.when(k == nk - 1)
        def _():
            d = part if nk == 1 else acc_ref[...] + part
            xv = x_ref[...]
            r = lax.rsqrt(jnp.mean(xv * xv, axis=-1, keepdims=True) + NORM_EPS)
            xh = xv * r
            dyg = d * g_ref[...]
            dx = r * (dyg - xh * jnp.mean(dyg * xh, axis=-1, keepdims=True))
            dx_ref[...] = dx if add is None else dx + refs[4][...]
            dg_ref[...] += jnp.sum(d * xh, axis=0, keepdims=True)

    row = BS((tm, Dm), lambda i, k: (i, 0))
    vec = BS((1, Dm), lambda i, k: (0, 0))
    extra = [] if add is None else [add]
    return _pcall(body, name=name, grid=(T // tm, nk),
                  in_specs=[BS((None, tm, tk), lambda i, k: (k // per_half, i, k % per_half)) if halves
                            else BS((tm, tk), lambda i, k: (i, k)),
                            BS((tk, Dm), lambda i, k: (k, 0)) if wt else BS((Dm, tk), lambda i, k: (0, k)), row, vec]
                  + [row] * len(extra),
                  out_specs=[row, vec], out_shape=[SDS((T, Dm), F32), SDS((1, Dm), F32)],
                  scratch_shapes=[pltpu.VMEM((tm, Dm), F32)],
                  compiler_params=_cparams(("arbitrary", "arbitrary")))(dy, wb, x, g.reshape(1, Dm), *extra)


def _colsum(x, name):
    T, N = x.shape
    tt = _tile(T, 1024, 8)

    def body(x_ref, o_ref):
        @pl.when(pl.program_id(0) == 0)
        def _():
            o_ref[...] = jnp.zeros_like(o_ref)

        o_ref[...] += jnp.sum(x_ref[...], axis=0, keepdims=True)

    return _pcall(body, name=name, grid=(T // tt,), in_specs=[BS((tt, N), lambda i: (i, 0))],
                  out_specs=BS((1, N), lambda i: (0, 0)), out_shape=SDS((1, N), F32),
                  compiler_params=_cparams(("arbitrary",)))(x)


@functools.partial(jax.custom_vjp, nondiff_argnums=(5,))
def linear(a, wb, wc, bias, add, name):
    return _linear_fwd(a, wb, wc, bias, add, name)[0]


def _linear_fwd(a, wb, wc, bias, add, name):
    out = _mm(a, wb, bias=bias, add=add, name=name, tm=1024, tn=1024, tk=1024)
    return out, (a, wb, bias is not None, add is not None)


def _linear_bwd(name, res, g):
    a, wb, has_bias, has_add = res
    da = _mm(g, wb, tb=True, name=name + "_da", tm=1024, tn=1024, tk=1024)
    dw = _mm(a, g, ta=True, name=name + "_dw", tm=1024, tn=1024, tk=1024)
    dbias = _colsum(g, name + "_db").reshape(-1) if has_bias else None
    return da, jnp.zeros_like(wb), dw, dbias, (g if has_add else None)


linear.defvjp(_linear_fwd, _linear_bwd)


def _rms_fwd_call(x, g, name, out_dtype=F32):
    T, Dm = x.shape
    tt = _tile(T, 2048, 16)

    def body(x_ref, g_ref, o_ref):
        xv = x_ref[...]
        r = lax.rsqrt(jnp.mean(xv * xv, axis=-1, keepdims=True) + NORM_EPS)
        o_ref[...] = ((xv * r) * g_ref[...]).astype(out_dtype)

    return _pcall(body, name=name, grid=(T // tt,),
                  in_specs=[BS((tt, Dm), lambda i: (i, 0)), BS((1, Dm), lambda i: (0, 0))],
                  out_specs=BS((tt, Dm), lambda i: (i, 0)), out_shape=SDS((T, Dm), out_dtype),
                  compiler_params=_cparams(("parallel",)))(x, g.reshape(1, Dm))


@functools.partial(jax.custom_vjp, nondiff_argnums=(5, 6))
def norm_linear(x, g, wb, wc, bias, wt, name):
    return _norm_linear_fwd(x, g, wb, wc, bias, wt, name)[0]


def _norm_linear_fwd(x, g, wb, wc, bias, wt, name):
    hn = _rms_fwd_call(x, g, name + "_norm", BF16)
    return (_mm(hn, wb, tb=wt, bias=bias, name=name, tm=1024, tn=1408, tk=1024), x), (x, g, hn, wb, bias is not None)


def _norm_linear_bwd(wt, name, res, cts):
    x, g, hn, wb, has_bias = res
    dy, dres = cts
    if wt:
        dw = _mm(dy, hn, ta=True, name=name + "_dw", tm=1408, tn=1024, tk=2048)
    else:
        dw = _mm(hn, dy, ta=True, name=name + "_dw", tm=1024, tn=1408, tk=1024)
    dx, dg = _mm_rms_bwd(dy, wb, x, g, dres, name + "_da", wt)
    dbias = _colsum(dy, name + "_db").reshape(-1) if has_bias else None
    return dx, dg.reshape(g.shape), jnp.zeros_like(wb), dw, dbias


norm_linear.defvjp(_norm_linear_fwd, _norm_linear_bwd)


def _rope_tables(T):
    half = ROT_DIM // 2
    inv = ROPE_THETA ** (-jnp.arange(0, ROT_DIM, 2, dtype=F32) / ROT_DIM)
    lane = jnp.arange(LANES) % HEAD_DIM
    freq = jnp.where(lane < ROT_DIM, inv[lane % half], 0.0)
    ang = jnp.arange(T, dtype=F32)[:, None] * freq[None, :]
    c, s = jnp.cos(ang), jnp.sin(ang)
    sa = jnp.where((lane >= half) & (lane < ROT_DIM), s, 0.0)
    sb = jnp.where(lane < half, -s, 0.0)
    return c, sa, sb


def _join_pieces_call(pieces, rotated, tabs, name):
    T = pieces[0].shape[0]
    widths = [p.shape[1] for p in pieces]
    W = sum(widths)
    tt = _tile(T, 512, 16)

    def body(*refs):
        tab_refs, o_ref, sum_ref = refs[len(pieces):len(pieces) + 3], refs[-2], refs[-1]

        @pl.when(pl.program_id(0) == 0)
        def _():
            sum_ref[...] = jnp.zeros_like(sum_ref)

        off = 0
        for p_ref, w, r in zip(refs, widths, rotated):
            piece = _rotate(p_ref[...], tab_refs, inverse=True) if r else p_ref[...]
            o_ref[:, off:off + w] = piece.astype(BF16)
            sum_ref[:, off:off + w] += jnp.sum(piece, axis=0, keepdims=True)
            off += w

    return _pcall(body, name=name, grid=(T // tt,),
                  in_specs=[BS((tt, w), lambda i: (i, 0)) for w in widths] + [BS((tt, LANES), lambda i: (i, 0))] * 3,
                  out_specs=[BS((tt, W), lambda i: (i, 0)), BS((1, W), lambda i: (0, 0))],
                  out_shape=[SDS((T, W), BF16), SDS((1, W), F32)],
                  compiler_params=_cparams(("arbitrary",)))(*pieces, *tabs)


@functools.partial(jax.custom_vjp, nondiff_argnums=(6, 7))
def norm_linear_pieces(x, g, wb, wc, bias, tabs, pieces, name):
    return _norm_linear_pieces_fwd(x, g, wb, wc, bias, tabs, pieces, name)[0]


def _norm_linear_pieces_fwd(x, g, wb, wc, bias, tabs, pieces, name):
    hn = _rms_fwd_call(x, g, name + "_norm", BF16)
    outs, row = [], 0
    for i, (width, rotated) in enumerate(pieces):
        outs.append(_mm(hn, wb[row:row + width], tb=True, bias=None if bias is None else bias[row:row + width],
                        rope=tabs if rotated else None, name=f"{name}_{i}", tm=2048, tn=1024, tk=1024))
        row += width
    return (*outs, x), (x, g, hn, wb, tabs, bias is not None)


def _norm_linear_pieces_bwd(pieces, name, res, cts):
    x, g, hn, wb, tabs, has_bias = res
    dy, dy_sum = _join_pieces_call(cts[:-1], [rotated for _, rotated in pieces], tabs, name + "_join")
    dw = _mm(dy, hn, ta=True, name=name + "_dw", tm=1408, tn=1024, tk=2048)
    dx, dg = _mm_rms_bwd(dy, wb, x, g, cts[-1], name + "_da", wt=True)
    dbias = dy_sum.reshape(-1) if has_bias else None
    return dx, dg.reshape(g.shape), jnp.zeros_like(wb), dw, dbias, jax.tree.map(jnp.zeros_like, tabs)


norm_linear_pieces.defvjp(_norm_linear_pieces_fwd, _norm_linear_pieces_bwd)


def _conv_fwd_call(x, w, b, name):
    T, C = x.shape
    tt = _tile(T, 2048, 8)
    per = tt // 8

    def body(x_ref, halo_ref, w_ref, b_ref, o_ref):
        i = pl.program_id(0)
        halo = jnp.where(i > 0, halo_ref[...], 0.0)
        e = jnp.concatenate([halo, x_ref[...]], axis=0)
        acc = b_ref[...]
        for k in (3, 2, 1):
            acc = acc + pltpu.roll(e, k, axis=0)[8:, :] * w_ref[3 - k:4 - k, :]
        o_ref[...] = acc + x_ref[...] * w_ref[3:4, :]

    return _pcall(body, name=name, grid=(T // tt,),
                  in_specs=[BS((tt, C), lambda i: (i, 0)), BS((8, C), lambda i: (jnp.maximum(i * per - 1, 0), 0)),
                            BS((4, C), lambda i: (0, 0)), BS((1, C), lambda i: (0, 0))],
                  out_specs=BS((tt, C), lambda i: (i, 0)), out_shape=SDS((T, C), F32),
                  compiler_params=_cparams(("parallel",)))(x, x, w, b.reshape(1, C))


def _conv_bwd_call(x, w, dy, name):
    T, C = x.shape
    tt = _tile(T, 1024, 8)
    per = tt // 8
    nt = T // tt

    def body(x_ref, halo_ref, w_ref, dy_ref, nxt_ref, dx_ref, dwb_ref):
        i = pl.program_id(0)
        halo = jnp.where(i > 0, halo_ref[...], 0.0)
        e = jnp.concatenate([halo, x_ref[...]], axis=0)
        dy = dy_ref[...]
        nxt = jnp.where(i < nt - 1, nxt_ref[...], 0.0)
        f = jnp.concatenate([dy, nxt], axis=0)
        dx = dy * w_ref[3:4, :]
        rows = [None] * 4
        rows[3] = jnp.sum(dy * x_ref[...], axis=0, keepdims=True)
        for k in (1, 2, 3):
            dx = dx + pltpu.roll(f, tt + 8 - k, axis=0)[:tt, :] * w_ref[3 - k:4 - k, :]
            rows[3 - k] = jnp.sum(dy * pltpu.roll(e, k, axis=0)[8:, :], axis=0, keepdims=True)
        dx_ref[...] = dx
        upd = jnp.concatenate(rows + [jnp.sum(dy, axis=0, keepdims=True), jnp.zeros((3, C), F32)], axis=0)

        @pl.when(i == 0)
        def _():
            dwb_ref[...] = jnp.zeros_like(dwb_ref)

        dwb_ref[...] += upd

    row = BS((tt, C), lambda i: (i, 0))
    return _pcall(body, name=name, grid=(nt,),
                  in_specs=[row, BS((8, C), lambda i: (jnp.maximum(i * per - 1, 0), 0)), BS((4, C), lambda i: (0, 0)),
                            row, BS((8, C), lambda i: (jnp.minimum((i + 1) * per, T // 8 - 1), 0))],
                  out_specs=[row, BS((8, C), lambda i: (0, 0))],
                  out_shape=[SDS((T, C), F32), SDS((8, C), F32)],
                  compiler_params=_cparams(("arbitrary",)))(x, x, w, dy, dy)


@functools.partial(jax.custom_vjp, nondiff_argnums=(3,))
def conv4(x, w, b, name):
    return _conv_fwd_call(x, w, b, name)


def _conv4_fwd(x, w, b, name):
    return _conv_fwd_call(x, w, b, name), (x, w)


def _conv4_bwd(name, res, dy):
    x, w = res
    dx, dwb = _conv_bwd_call(x, w, dy, name + "_bwd")
    return dx, dwb[0:4], dwb[4]


conv4.defvjp(_conv4_fwd, _conv4_bwd)


def _gates_fwd_call(xc, wa, ba, wx, bx, name):
    T, C = xc.shape
    hd = C // LRU_HEADS
    tt = _tile(T, 2048, 8)

    def body(x_ref, wa_ref, ba_ref, wx_ref, bx_ref, ga_ref, gx_ref):
        xb = x_ref[...].astype(BF16)
        ga_ref[...] = jnp.dot(xb, wa_ref[0].astype(BF16), preferred_element_type=F32) + ba_ref[...]
        gx_ref[...] = jnp.dot(xb, wx_ref[0].astype(BF16), preferred_element_type=F32) + bx_ref[...]

    blk = BS((tt, hd), lambda i, h: (i, h))
    wsp = BS((1, hd, hd), lambda i, h: (h, 0, 0))
    bsp = BS((1, hd), lambda i, h: (0, h))
    return _pcall(body, name=name, grid=(T // tt, LRU_HEADS), in_specs=[blk, wsp, bsp, wsp, bsp],
                  out_specs=[blk, blk], out_shape=[SDS((T, C), F32)] * 2,
                  compiler_params=_cparams(("parallel", "parallel")))(xc, wa, ba.reshape(1, C), wx, bx.reshape(1, C))


def _gates_bwd_x_call(dga, dgx, wa, wx, add, name):
    T, C = dga.shape
    hd = C // LRU_HEADS
    tt = _tile(T, 2048, 8)
    dn = (((1,), (1,)), ((), ()))

    def body(da_ref, dx_ref, wa_ref, wx_ref, add_ref, o_ref):
        o_ref[...] = (lax.dot_general(da_ref[...].astype(BF16), wa_ref[0].astype(BF16), dn, preferred_element_type=F32)
                      + lax.dot_general(dx_ref[...].astype(BF16), wx_ref[0].astype(BF16), dn, preferred_element_type=F32)
                      + add_ref[...])

    blk = BS((tt, hd), lambda i, h: (i, h))
    wsp = BS((1, hd, hd), lambda i, h: (h, 0, 0))
    return _pcall(body, name=name, grid=(T // tt, LRU_HEADS), in_specs=[blk, blk, wsp, wsp, blk], out_specs=blk,
                  out_shape=SDS((T, C), F32), compiler_params=_cparams(("parallel", "parallel")))(dga, dgx, wa, wx, add)


def _gates_bwd_w_call(xc, dga, dgx, name):
    T, C = xc.shape
    hd = C // LRU_HEADS
    tt = _tile(T, 2048, 8)
    dn = (((0,), (0,)), ((), ()))

    def body(x_ref, da_ref, dx_ref, dwa_ref, dwx_ref, dba_ref, dbx_ref):
        @pl.when(pl.program_id(1) == 0)
        def _():
            dwa_ref[...] = jnp.zeros_like(dwa_ref)
            dwx_ref[...] = jnp.zeros_like(dwx_ref)
            dba_ref[...] = jnp.zeros_like(dba_ref)
            dbx_ref[...] = jnp.zeros_like(dbx_ref)

        xb = x_ref[...].astype(BF16)
        da, dx = da_ref[...], dx_ref[...]
        dwa_ref[0] += lax.dot_general(xb, da.astype(BF16), dn, preferred_element_type=F32)
        dwx_ref[0] += lax.dot_general(xb, dx.astype(BF16), dn, preferred_element_type=F32)
        dba_ref[...] += jnp.sum(da, axis=0, keepdims=True)
        dbx_ref[...] += jnp.sum(dx, axis=0, keepdims=True)

    blk = BS((tt, hd), lambda h, i: (i, h))
    wsp = BS((1, hd, hd), lambda h, i: (h, 0, 0))
    bsp = BS((1, hd), lambda h, i: (0, h))
    return _pcall(body, name=name, grid=(LRU_HEADS, T // tt), in_specs=[blk, blk, blk],
                  out_specs=[wsp, wsp, bsp, bsp],
                  out_shape=[SDS((LRU_HEADS, hd, hd), F32)] * 2 + [SDS((1, C), F32)] * 2,
                  compiler_params=_cparams(("parallel", "arbitrary")))(xc, dga, dgx)


@functools.partial(jax.custom_vjp, nondiff_argnums=(7,))
def lru_gates(xc, wa, wa_c, ba, wx, wx_c, bx, name):
    return _lru_gates_fwd(xc, wa, wa_c, ba, wx, wx_c, bx, name)[0]


def _lru_gates_fwd(xc, wa, wa_c, ba, wx, wx_c, bx, name):
    return (*_gates_fwd_call(xc, wa, ba, wx, bx, name), xc), (xc, wa, wx, ba.shape)


def _lru_gates_bwd(name, res, g):
    xc, wa, wx, bshape = res
    dga, dgx, dxc_other = g
    dxc = _gates_bwd_x_call(dga, dgx, wa, wx, dxc_other, name + "_dx")
    dwa, dwx, dba, dbx = _gates_bwd_w_call(xc, dga, dgx, name + "_dw")
    return dxc, jnp.zeros_like(wa), dwa, dba.reshape(bshape), jnp.zeros_like(wx), dwx, dbx.reshape(bshape)


lru_gates.defvjp(_lru_gates_fwd, _lru_gates_bwd)


def _lru_coeffs(xc, ga, gx, lam):
    r = _sigmoid(ga)
    ig = _sigmoid(gx)
    z = -lam
    sp = jnp.maximum(z, 0.0) + jnp.log(1.0 + jnp.exp(-jnp.abs(z)))
    la = -LRU_C * r * sp
    a = jnp.exp(la)
    s = jnp.sqrt(-_expm1(2.0 * la))
    return r, ig, sp, a, s


LRU_TT = 256


def _scan_fwd_call(xc, ga, gx, y, lam, name):
    T, C = xc.shape
    tt = _tile(T, 2 * LRU_TT, 8)

    def body(xc_ref, ga_ref, gx_ref, y_ref, lam_ref, h_ref, rec_ref, a_buf, carry):
        @pl.when(pl.program_id(0) == 0)
        def _():
            carry[...] = jnp.zeros_like(carry)

        xcv = xc_ref[...]
        _, ig, _, a, s = _lru_coeffs(xcv, ga_ref[...], gx_ref[...], lam_ref[...])
        a_buf[...] = a
        h_ref[...] = s * (ig * xcv)

        def step(t, h):
            hn = a_buf[pl.ds(t, 1), :] * h + h_ref[pl.ds(t, 1), :]
            h_ref[pl.ds(t, 1), :] = hn
            return hn

        carry[0:1, :] = lax.fori_loop(0, tt, step, carry[0:1, :], unroll=8)
        g, _ = _gelu_parts(y_ref[...])
        rec_ref[...] = h_ref[...] * g

    row = BS((tt, C), lambda i: (i, 0))
    vec = BS((1, C), lambda i: (0, 0))
    return _pcall(body, name=name, grid=(T // tt,), in_specs=[row, row, row, row, vec], out_specs=[row, row],
                  out_shape=[SDS((T, C), F32)] * 2,
                  scratch_shapes=[pltpu.VMEM((tt, C), F32), pltpu.VMEM((8, C), F32)],
                  compiler_params=_cparams(("arbitrary",)))(xc, ga, gx, y, lam.reshape(1, C))


def _scan_bwd_call(xc, ga, gx, y, lam, h, drec, name):
    T, C = xc.shape
    tt = _tile(T, LRU_TT, 8)
    nt = T // tt
    per = tt // 8

    def body(xc_ref, ga_ref, gx_ref, y_ref, lam_ref, h_ref, halo_ref, dr_ref,
             dga_ref, dgx_ref, dxc_ref, dy_ref, dlam_ref, a_buf, g_buf, carry):
        i = pl.program_id(0)

        @pl.when(i == 0)
        def _():
            carry[...] = jnp.zeros_like(carry)
            dlam_ref[...] = jnp.zeros_like(dlam_ref)

        xcv, lam = xc_ref[...], lam_ref[...]
        r, ig, sp, a, s = _lru_coeffs(xcv, ga_ref[...], gx_ref[...], lam)
        gel, dgel = _gelu_parts(y_ref[...])
        drec = dr_ref[...]
        hv = h_ref[...]
        dy_ref[...] = drec * hv * dgel
        a_buf[...] = a
        g_buf[...] = drec * gel

        def step(j, q):
            t = tt - 1 - j
            g = g_buf[pl.ds(t, 1), :] + q
            g_buf[pl.ds(t, 1), :] = g
            return a_buf[pl.ds(t, 1), :] * g

        carry[0:1, :] = lax.fori_loop(0, tt, step, carry[0:1, :], unroll=8)
        g = g_buf[...]
        halo = jnp.where(i < nt - 1, halo_ref[...], 0.0)
        hprev = pltpu.roll(jnp.concatenate([halo, hv], axis=0), 1, axis=0)[8:, :]
        da = g * hprev
        dig = g * s * xcv
        ds = g * ig * xcv
        dla = da * a - ds * (a * a) / s
        dga_ref[...] = dla * (-LRU_C * sp) * r * (1.0 - r)
        dgx_ref[...] = dig * ig * (1.0 - ig)
        dxc_ref[...] = g * s * ig
        dlam_ref[...] += jnp.sum(dla * r, axis=0, keepdims=True) * (LRU_C * _sigmoid(-lam))

    row = BS((tt, C), lambda i: (nt - 1 - i, 0))
    vec = BS((1, C), lambda i: (0, 0))
    halo = BS((8, C), lambda i: (jnp.maximum((nt - 1 - i) * per - 1, 0), 0))
    return _pcall(body, name=name, grid=(nt,), in_specs=[row, row, row, row, vec, row, halo, row],
                  out_specs=[row, row, row, row, vec], out_shape=[SDS((T, C), F32)] * 4 + [SDS((1, C), F32)],
                  scratch_shapes=[pltpu.VMEM((tt, C), F32), pltpu.VMEM((tt, C), F32), pltpu.VMEM((8, C), F32)],
                  compiler_params=_cparams(("arbitrary",)))(xc, ga, gx, y, lam.reshape(1, C), h, h, drec)


@functools.partial(jax.custom_vjp, nondiff_argnums=(5,))
def lru_scan(xc, ga, gx, y, lam, name):
    return _scan_fwd_call(xc, ga, gx, y, lam, name)[1]


def _lru_scan_fwd(xc, ga, gx, y, lam, name):
    h, rec = _scan_fwd_call(xc, ga, gx, y, lam, name)
    return rec, (xc, ga, gx, y, lam, h)


def _lru_scan_bwd(name, res, drec):
    xc, ga, gx, y, lam, h = res
    dga, dgx, dxc, dy, dlam = _scan_bwd_call(xc, ga, gx, y, lam, h, drec, name + "_bwd")
    return dxc, dga, dgx, dy, dlam.reshape(lam.shape)


lru_scan.defvjp(_lru_scan_fwd, _lru_scan_bwd)


def _att_batch(d, shared=False):
    if shared:
        return 8, 1
    return (4, 1) if d == 1 else (1, min(d, 8))


def _att_masks(n, max_dist):
    qi = lax.broadcasted_iota(jnp.int32, (1, 2 * ATT_BLOCK, 2 * ATT_BLOCK), 1) & (ATT_BLOCK - 1)
    kj = lax.broadcasted_iota(jnp.int32, (1, 2 * ATT_BLOCK, 2 * ATT_BLOCK), 2)
    prev = (kj < ATT_BLOCK) & (kj >= qi + (ATT_BLOCK - max_dist)) & (n > 0)
    cur = (kj >= ATT_BLOCK) & (kj - ATT_BLOCK <= qi)
    return prev | cur


def _lo_lanes(rows):
    return lax.broadcasted_iota(jnp.int32, (rows, LANES), 1) < HEAD_DIM


def _lane_half(rows):
    return lax.broadcasted_iota(jnp.int32, (rows, LANES), 1) // HEAD_DIM


def _stack_heads(x2):
    lo = _lo_lanes(ATT_BLOCK)
    zero = jnp.zeros_like(x2)
    return jnp.concatenate([jnp.where(lo, x2, zero), jnp.where(lo, zero, x2)], axis=0)


def _unstack_heads(y):
    return jnp.where(_lo_lanes(ATT_BLOCK), y[:ATT_BLOCK], y[ATT_BLOCK:])


def _per_head_col(x2):
    return jnp.concatenate([x2[:, 0:1], x2[:, HEAD_DIM:HEAD_DIM + 1]], axis=0)


def _head_sums(x2):
    lo = _lo_lanes(ATT_BLOCK)
    return jnp.concatenate([jnp.sum(jnp.where(lo, x2, 0.0), axis=1, keepdims=True),
                            jnp.sum(jnp.where(lo, 0.0, x2), axis=1, keepdims=True)], axis=0)


def _att_specs(d, Wq, Wk, nb, clamp):
    shared = Wk != Wq
    cgw, sb = _att_batch(d, shared)
    cur = (lambda n: jnp.minimum(n, nb - 1)) if clamp else (lambda n: n)
    rows, qw, kw = ATT_BLOCK * d, cgw * LANES, (LANES if shared else cgw * LANES)
    kcol = (lambda g: 0) if shared else (lambda g: g)
    qsp = BS((rows, qw), lambda g, n: (cur(n), g))
    csp = BS((rows, kw), lambda g, n: (cur(n), kcol(g)))
    psp = BS((rows, kw), lambda g, n: (jnp.maximum(cur(n) - 1, 0), kcol(g)))
    return cgw, sb, shared, qsp, csp, psp, qw, kw


def _att_streams(d, sb, work):
    if d == 1:
        work([slice(None)])
        return

    def one(j, carry):
        work([pl.ds(j * sb + i, ATT_BLOCK, stride=d) for i in range(sb)])
        return carry

    lax.fori_loop(0, d // sb, one, 0)


def _att_problem_loads(rows, cgw, shared, g, q_ref, kc_ref, kp_ref, vc_ref, vp_ref, sk_ref):
    half = _lane_half(ATT_BLOCK)

    def kv(ref, r, p):
        x = ref[r, :]
        if not shared:
            return x[:, p * LANES:(p + 1) * LANES]
        return jnp.where(half == p // 4, x, pltpu.roll(x, HEAD_DIM, axis=1))

    qs, kb, vb, sk = [], [], [], []
    for r in rows:
        qrow = q_ref[r, :]
        for p in range(cgw):
            cols = slice(p * LANES, (p + 1) * LANES)
            qs.append(_stack_heads(qrow[:, cols].astype(BF16)))
            kb.append(jnp.concatenate([kv(kp_ref, r, p), kv(kc_ref, r, p)], axis=0).astype(BF16))
            vb.append(jnp.concatenate([kv(vp_ref, r, p), kv(vc_ref, r, p)], axis=0).astype(BF16))
            sk.append(_per_head_col(jnp.broadcast_to(sk_ref[:, cols], (ATT_BLOCK, LANES))))
    return jnp.stack(qs), jnp.stack(kb), jnp.stack(vb), jnp.stack(sk)


_BDOT_NT = (((2,), (2,)), ((0,), (0,)))
_BDOT_NN = (((2,), (1,)), ((0,), (0,)))
_BDOT_TN = (((1,), (1,)), ((0,), (0,)))


def _att_fwd_call(q, k, v, sinks, d, max_dist, name):
    T, Wq = q.shape
    Wk = k.shape[1]
    nb = T // (d * ATT_BLOCK)
    cgw, sb, shared, qsp, csp, psp, qw, kw = _att_specs(d, Wq, Wk, nb, False)
    G = Wq // qw
    assert not shared or (d == 1 and Wk == LANES and G == 1 and cgw == 8), "a shared kv pair serves 2 x 8 query heads"

    def body(q_ref, kc_ref, kp_ref, vc_ref, vp_ref, sk_ref, o_ref, lse_ref):
        g, n = pl.program_id(0), pl.program_id(1)

        def work(rows):
            qs, kband, vband, sk = _att_problem_loads(rows, cgw, shared, g, q_ref, kc_ref, kp_ref, vc_ref, vp_ref, sk_ref)
            s = lax.dot_general(qs, kband, _BDOT_NT, preferred_element_type=F32) * (HEAD_DIM ** -0.5)
            s = jnp.where(_att_masks(n, max_dist), s, NEG)
            m = jnp.maximum(jnp.max(s, axis=-1, keepdims=True), sk)
            e = jnp.exp(s - m)
            den = jnp.sum(e, axis=-1, keepdims=True) + jnp.exp(sk - m)
            o = lax.dot_general((e * (1.0 / den)).astype(BF16), vband, _BDOT_NN, preferred_element_type=F32)
            lse = jnp.broadcast_to(m + jnp.log(den), o.shape)
            for i, r in enumerate(rows):
                o_ref[r, :] = jnp.concatenate([_unstack_heads(o[i * cgw + p]) for p in range(cgw)], axis=1)
                lse_ref[r, :] = jnp.concatenate([_unstack_heads(lse[i * cgw + p]) for p in range(cgw)], axis=1)

        _att_streams(d, sb, work)

    sksp = BS((1, qw), lambda g, n: (0, g))
    return _pcall(body, name=name, grid=(G, nb), in_specs=[qsp, csp, psp, csp, psp, sksp], out_specs=[qsp, qsp],
                  out_shape=[SDS((T, Wq), F32)] * 2,
                  compiler_params=_cparams(("parallel", "parallel")))(q, k, k, v, v, sinks)


def _att_bwd_call(q, k, v, sinks, o, lse, do, dlse, d, max_dist, name, acc=None):
    T, Wq = q.shape
    Wk = k.shape[1]
    nb = T // (d * ATT_BLOCK)
    cgw, sb, shared, qsp, csp, psp, qw, kw = _att_specs(d, Wq, Wk, nb, True)
    G = Wq // qw
    scale = HEAD_DIM ** -0.5

    def body(*refs):
        q_ref, kc_ref, kp_ref, vc_ref, vp_ref, sk_ref, o_ref, lse_ref, do_ref = refs[:9]
        dlse_ref = refs[9] if dlse is not None else None
        aq_ref, ak_ref, av_ref = refs[-9:-6] if acc is not None else (None, None, None)
        dq_ref, dk_ref, dv_ref, dsk_ref, ck, cv = refs[-6:]
        g, n = pl.program_id(0), pl.program_id(1)

        def plus(val, a_ref, r):
            return val if a_ref is None else val + a_ref[r, :]

        @pl.when(n == 0)
        def _():
            ck[...] = jnp.zeros_like(ck)
            cv[...] = jnp.zeros_like(cv)
            dsk_ref[...] = jnp.zeros_like(dsk_ref)

        def work(rows):
            qs, kband, vband, sk = _att_problem_loads(rows, cgw, shared, g, q_ref, kc_ref, kp_ref, vc_ref, vp_ref, sk_ref)
            dos, lse_c, corr = [], [], []
            for r in rows:
                do_r, o_r, lse_r = do_ref[r, :], o_ref[r, :], lse_ref[r, :]
                dlse_r = dlse_ref[r, :] if dlse_ref is not None else None
                for p in range(cgw):
                    cols = slice(p * LANES, (p + 1) * LANES)
                    dos.append(_stack_heads(do_r[:, cols].astype(BF16)))
                    lse_c.append(_per_head_col(lse_r[:, cols]))
                    delta = _head_sums(do_r[:, cols] * o_r[:, cols])
                    corr.append(-delta if dlse_r is None else _head_sums(dlse_r[:, cols]) - delta)
            dos, lse_c, corr = jnp.stack(dos), jnp.stack(lse_c), jnp.stack(corr)
            s = lax.dot_general(qs, kband, _BDOT_NT, preferred_element_type=F32) * scale
            pr = jnp.exp(jnp.where(_att_masks(n, max_dist), s, NEG) - lse_c)
            dp = lax.dot_general(dos, vband, _BDOT_NT, preferred_element_type=F32)
            ds = (pr * (dp + corr)).astype(BF16)
            dq = lax.dot_general(ds, kband, _BDOT_NN, preferred_element_type=F32) * scale
            dkb = lax.dot_general(ds, qs, _BDOT_TN, preferred_element_type=F32) * scale
            dvb = lax.dot_general(pr.astype(BF16), dos, _BDOT_TN, preferred_element_type=F32)
            dsk = jnp.exp(sk - lse_c) * corr
            lane = lax.broadcasted_iota(jnp.int32, (8, LANES), 1)
            for p in range(cgw):
                tot = [jnp.sum(jnp.stack([dsk[i * cgw + p, h * ATT_BLOCK:(h + 1) * ATT_BLOCK] for i in range(len(rows))]),
                               axis=(0, 1)).reshape(1, 1) for h in range(2)]
                dsk_ref[:, p * LANES:(p + 1) * LANES] += jnp.where(lane == 0, tot[0], jnp.where(lane == HEAD_DIM, tot[1], 0.0))

            def gather_pairs(parts):
                if not shared:
                    return jnp.concatenate(parts, axis=1)
                tot = [parts[4 * h] + parts[4 * h + 1] + parts[4 * h + 2] + parts[4 * h + 3] for h in range(2)]
                tot = [t + pltpu.roll(t, HEAD_DIM, axis=1) for t in tot]
                return jnp.where(_lo_lanes(ATT_BLOCK), tot[0], tot[1])

            for i, r in enumerate(rows):
                mine = range(i * cgw, (i + 1) * cgw)
                dq_ref[r, :] = plus(jnp.concatenate([_unstack_heads(dq[b]) for b in mine], axis=1), aq_ref, r)
                dk_ref[r, :] = plus(ck[r, :] + gather_pairs([dkb[b, :ATT_BLOCK] for b in mine]), ak_ref, r)
                dv_ref[r, :] = plus(cv[r, :] + gather_pairs([dvb[b, :ATT_BLOCK] for b in mine]), av_ref, r)
                ck[r, :] = gather_pairs([dkb[b, ATT_BLOCK:] for b in mine])
                cv[r, :] = gather_pairs([dvb[b, ATT_BLOCK:] for b in mine])

        @pl.when(n < nb)
        def _():
            _att_streams(d, sb, work)

        @pl.when(n == nb)
        def _():
            dk_ref[...] = plus(ck[...], ak_ref, slice(None))
            dv_ref[...] = plus(cv[...], av_ref, slice(None))

    sksp = BS((1, qw), lambda g, n: (0, g))
    rows = ATT_BLOCK * d
    osp = BS((rows, kw), lambda g, n: (jnp.maximum(n - 1, 0), 0 if shared else g))
    kshape = SDS((T, Wk), F32)
    dq, dk, dv, dsk = _pcall(
        body, name=name, grid=(G, nb + 1),
        in_specs=[qsp, csp, psp, csp, psp, sksp, qsp, qsp, qsp] + ([qsp] if dlse is not None else [])
        + ([qsp, osp, osp] if acc is not None else []),
        out_specs=[qsp, osp, osp, BS((8, qw), lambda g, n: (0, g))],
        out_shape=[SDS((T, Wq), F32), kshape, kshape, SDS((8, Wq), F32)],
        scratch_shapes=[pltpu.VMEM((rows, kw), F32)] * 2,
        compiler_params=_cparams(("parallel", "arbitrary")))(q, k, k, v, v, sinks, o, lse, do,
                                                              *([dlse] if dlse is not None else []), *(acc or ()))
    return dq, dk, dv, dsk[0:1]


@functools.partial(jax.custom_vjp, nondiff_argnums=(4, 5, 6, 7))
def band_attention(q, k, v, sinks, d, max_dist, with_lse, name):
    return _band_attention_fwd(q, k, v, sinks, d, max_dist, with_lse, name)[0]


def _band_attention_fwd(q, k, v, sinks, d, max_dist, with_lse, name):
    o, lse = _att_fwd_call(q, k, v, sinks, d, max_dist, name)
    return ((o, lse) if with_lse else o), (q, k, v, sinks, o, lse)


def _band_attention_bwd(d, max_dist, with_lse, name, res, g):
    q, k, v, sinks, o, lse = res
    do, dlse = g if with_lse else (g, None)
    return _att_bwd_call(q, k, v, sinks, o, lse, do, dlse, d, max_dist, name + "_bwd")


band_attention.defvjp(_band_attention_fwd, _band_attention_bwd)


def _merge_weights(ls):
    mx = jnp.maximum(jnp.maximum(ls[0], ls[1]), ls[2])
    es = [jnp.exp(l - mx) for l in ls]
    inv = 1.0 / (es[0] + es[1] + es[2])
    return [e * inv for e in es]


def _merge_fwd_call(os_, ls_, name):
    T, W = os_[0].shape
    tt = _tile(T, 1024, 8)

    def body(o1, o2, o3, l1, l2, l3, out_ref):
        w = _merge_weights([l1[...], l2[...], l3[...]])
        out_ref[...] = w[0] * o1[...] + w[1] * o2[...] + w[2] * o3[...]

    row = BS((tt, W), lambda i: (i, 0))
    return _pcall(body, name=name, grid=(T // tt,), in_specs=[row] * 6, out_specs=row,
                  out_shape=SDS((T, W), F32), compiler_params=_cparams(("parallel",)))(*os_, *ls_)


def _merge_bwd_call(os_, ls_, do, name):
    T, W = os_[0].shape
    tt = _tile(T, 512, 8)

    def body(o1, o2, o3, l1, l2, l3, do_ref, d1, d2, d3, e1, e2, e3):
        w = _merge_weights([l1[...], l2[...], l3[...]])
        dov = do_ref[...]
        ts = [dov * o[...] for o in (o1, o2, o3)]
        mean = w[0] * ts[0] + w[1] * ts[1] + w[2] * ts[2]
        for wi, ti, dref, eref in zip(w, ts, (d1, d2, d3), (e1, e2, e3)):
            dref[...] = wi * dov
            eref[...] = wi * (ti - mean)

    row = BS((tt, W), lambda i: (i, 0))
    return _pcall(body, name=name, grid=(T // tt,), in_specs=[row] * 7, out_specs=[row] * 6,
                  out_shape=[SDS((T, W), F32)] * 6, compiler_params=_cparams(("parallel",)))(*os_, *ls_, do)


@functools.partial(jax.custom_vjp, nondiff_argnums=(2,))
def merge3(os_, ls_, name):
    return _merge_fwd_call(os_, ls_, name)


def _merge3_fwd(os_, ls_, name):
    return _merge_fwd_call(os_, ls_, name), (os_, ls_)


def _merge3_bwd(name, res, do):
    os_, ls_ = res
    out = _merge_bwd_call(os_, ls_, do, name + "_bwd")
    return tuple(out[:3]), tuple(out[3:])


merge3.defvjp(_merge3_fwd, _merge3_bwd)


def _xa_probs(qb, kb, scale):
    s = lax.dot_general(qb, kb, (((1,), (1,)), ((), ())), preferred_element_type=F32) * scale
    e = jnp.exp(s - jnp.max(s, axis=-1, keepdims=True))
    return e / jnp.sum(e, axis=-1, keepdims=True)


def _xa_fwd_call(q, kv, name):
    T, W = q.shape
    M = kv.shape[0]
    hd = XA_HEAD_DIM
    tq = _tile(T, 4096, 8)
    scale = hd ** -0.5

    def body(q_ref, k_ref, v_ref, o_ref):
        p = _xa_probs(q_ref[...].astype(BF16), k_ref[...].astype(BF16), scale)
        o_ref[...] = jnp.dot(p.astype(BF16), v_ref[...].astype(BF16), preferred_element_type=F32)

    qsp = BS((tq, hd), lambda i, h: (i, h))
    return _pcall(body, name=name, grid=(T // tq, XA_HEADS),
                  in_specs=[qsp, BS((M, hd), lambda i, h: (0, h)), BS((M, hd), lambda i, h: (0, XA_HEADS + h))],
                  out_specs=qsp, out_shape=SDS((T, W), F32),
                  compiler_params=_cparams(("parallel", "parallel")))(q, kv, kv)


def _xa_bwd_call(q, kv, do, name):
    T, W = q.shape
    M = kv.shape[0]
    hd = XA_HEAD_DIM
    tq = _tile(T, 4096, 8)
    scale = hd ** -0.5
    dn_nt = (((1,), (1,)), ((), ()))
    dn_tn = (((0,), (0,)), ((), ()))

    def body(q_ref, k_ref, v_ref, do_ref, dq_ref, dk_ref, dv_ref):
        @pl.when(pl.program_id(1) == 0)
        def _():
            dk_ref[...] = jnp.zeros_like(dk_ref)
            dv_ref[...] = jnp.zeros_like(dv_ref)

        qb, kb, vb = q_ref[...].astype(BF16), k_ref[...].astype(BF16), v_ref[...].astype(BF16)
        p = _xa_probs(qb, kb, scale)
        dob = do_ref[...].astype(BF16)
        dp = lax.dot_general(dob, vb, dn_nt, preferred_element_type=F32)
        ds = (p * (dp - jnp.sum(p * dp, axis=-1, keepdims=True))).astype(BF16)
        dq_ref[...] = jnp.dot(ds, kb, preferred_element_type=F32) * scale
        dk_ref[...] += lax.dot_general(ds, qb, dn_tn, preferred_element_type=F32) * scale
        dv_ref[...] += lax.dot_general(p.astype(BF16), dob, dn_tn, preferred_element_type=F32)

    qsp = BS((tq, hd), lambda h, i: (i, h))
    ksp = BS((M, hd), lambda h, i: (0, h))
    return _pcall(body, name=name, grid=(XA_HEADS, T // tq),
                  in_specs=[qsp, ksp, BS((M, hd), lambda h, i: (0, XA_HEADS + h)), qsp],
                  out_specs=[qsp, ksp, ksp], out_shape=[SDS((T, W), F32), SDS((M, W), F32), SDS((M, W), F32)],
                  compiler_params=_cparams(("parallel", "arbitrary")))(q, kv, kv, do)


@functools.partial(jax.custom_vjp, nondiff_argnums=(2,))
def cross_attention(q, kv, name):
    return _xa_fwd_call(q, kv, name)


def _cross_attention_fwd(q, kv, name):
    return _xa_fwd_call(q, kv, name), (q, kv)


def _cross_attention_bwd(name, res, do):
    q, kv = res
    dq, dk, dv = _xa_bwd_call(q, kv, do, name + "_bwd")
    return dq, jnp.concatenate([dk, dv], axis=1)


cross_attention.defvjp(_cross_attention_fwd, _cross_attention_bwd)


def _gate_up_swiglu_call(hn, w1t, name):
    T, K = hn.shape
    F = w1t.shape[0] // 2
    tm, tn = _tile(T, 4096), _tile(F, 256)
    nj = F // tn
    dn = (((1,), (1,)), ((), ()))

    def body(a_ref, wg_ref, wu_ref, g_ref, u_ref, act_ref):
        a = a_ref[...]
        g = lax.dot_general(a, wg_ref[...], dn, preferred_element_type=F32)
        u = lax.dot_general(a, wu_ref[...], dn, preferred_element_type=F32)
        g_ref[...] = g
        u_ref[...] = u
        act_ref[...] = ((g * _sigmoid(g)) * u).astype(BF16)

    tile = BS((tm, tn), lambda i, j: (i, j))
    return _pcall(body, name=name, grid=(T // tm, nj),
                  in_specs=[BS((tm, K), lambda i, j: (i, 0)), BS((tn, K), lambda i, j: (j, 0)),
                            BS((tn, K), lambda i, j: (j + nj, 0))],
                  out_specs=[tile, tile, tile], out_shape=[SDS((T, F), F32), SDS((T, F), F32), SDS((T, F), BF16)],
                  compiler_params=_cparams(("parallel", "parallel")))(hn, w1t, w1t)


def _down_da_swiglu_bwd_call(dout, w2, g, u, name):
    T, F = g.shape
    Dm = dout.shape[1]
    tm, tn = _tile(T, 2048), _tile(F, 256)

    def body(d_ref, w_ref, g_ref, u_ref, o_ref):
        d = lax.dot_general(d_ref[...].astype(BF16), w_ref[...], (((1,), (1,)), ((), ())), preferred_element_type=F32)
        g = g_ref[...]
        sg = _sigmoid(g)
        o_ref[0] = (d * u_ref[...] * (sg * (1.0 + g * (1.0 - sg)))).astype(BF16)
        o_ref[1] = (d * (g * sg)).astype(BF16)

    tile = BS((tm, tn), lambda i, j: (i, j))
    return _pcall(body, name=name, grid=(T // tm, F // tn),
                  in_specs=[BS((tm, Dm), lambda i, j: (i, 0)), BS((tn, Dm), lambda i, j: (j, 0)), tile, tile],
                  out_specs=BS((2, tm, tn), lambda i, j: (0, i, j)), out_shape=SDS((2, T, F), BF16),
                  compiler_params=_cparams(("parallel", "parallel")))(dout, w2, g, u)


@functools.partial(jax.custom_vjp, nondiff_argnums=(6,))
def ffn_block(h, g, w1b, w1c, w2b, w2c, name):
    return _ffn_fwd(h, g, w1b, w1c, w2b, w2c, name)[0]


def _ffn_fwd(h, g, w1b, w1c, w2b, w2c, name):
    hn = _rms_fwd_call(h, g, name + "_norm", BF16)
    gate, up, act = _gate_up_swiglu_call(hn, w1b, name + "_gu")
    out = _mm(act, w2b, add=h, name=name + "_down", tm=1024, tn=1024, tk=2816)
    return out, (h, g, hn, gate, up, act, w1b, w2b)


def _ffn_bwd(name, res, dout):
    h, g, hn, gate, up, act, w1b, w2b = res
    dw2 = _mm(act, dout, ta=True, name=name + "_down_dw", tm=1408, tn=1024, tk=1024)
    dgu = _down_da_swiglu_bwd_call(dout, w2b, gate, up, name + "_down_da")
    dw1 = _mm(dgu, hn, ta=True, name=name + "_gu_dw", tm=1408, tn=1024, tk=2048)
    dh, dg = _mm_rms_bwd(dgu, w1b, h, g, dout, name + "_gu_da", wt=True)
    return dh, dg.reshape(g.shape), jnp.zeros_like(w1b), dw1, jnp.zeros_like(w2b), dw2


ffn_block.defvjp(_ffn_fwd, _ffn_bwd)


def _final_call(h, g, target, name):
    T, Dm = h.shape
    tt = _tile(T, 1024, 8)

    def body(x_ref, g_ref, t_ref, loss_ref, dx_ref, dg_ref):
        @pl.when(pl.program_id(0) == 0)
        def _():
            loss_ref[...] = jnp.zeros_like(loss_ref)
            dg_ref[...] = jnp.zeros_like(dg_ref)

        xv, gv = x_ref[...], g_ref[...]
        r = lax.rsqrt(jnp.mean(xv * xv, axis=-1, keepdims=True) + NORM_EPS)
        xh = xv * r
        err = xh * gv - t_ref[...]
        loss_ref[...] += 0.5 * jnp.sum(jnp.mean(err * err, axis=-1, keepdims=True), axis=0, keepdims=True)
        dy = err * (1.0 / Dm)
        dyg = dy * gv
        dx_ref[...] = r * (dyg - xh * jnp.mean(dyg * xh, axis=-1, keepdims=True))
        dg_ref[...] += jnp.sum(dy * xh, axis=0, keepdims=True)

    row = BS((tt, Dm), lambda i: (i, 0))
    vec = BS((1, Dm), lambda i: (0, 0))
    return _pcall(body, name=name, grid=(T // tt,), in_specs=[row, vec, row],
                  out_specs=[BS((1, 1), lambda i: (0, 0)), row, vec],
                  out_shape=[SDS((1, 1), F32), SDS((T, Dm), F32), SDS((1, Dm), F32)],
                  compiler_params=_cparams(("arbitrary",)))(h, g.reshape(1, Dm), target)


ADAMW_BLOCK_ELEMS = 256 * 1024


def _adamw_call(parts, w, m, v, name):
    shape = w.shape
    if not isinstance(parts, (list, tuple)):
        parts, shape3 = [parts], (1,) + shape
    else:
        shape3 = shape
    n_lead = shape3[0]
    r, N = shape3[-2], shape3[-1]
    Ld = math.prod(shape3[1:-2])
    w, m, v = (t.reshape(n_lead * Ld, r, N) for t in (w, m, v))
    tr = _tile(r, max(8, ADAMW_BLOCK_ELEMS // N), 8)
    c1 = 1.0 - ADAM_B1 ** ADAM_STEP
    c2 = 1.0 - ADAM_B2 ** ADAM_STEP
    outs = None
    for lead, p in enumerate(parts):
        def body(p_ref, w_ref, m_ref, v_ref, *rest):
            g_ref, d_ref, nm_ref, nv_ref = rest[-4:]
            g = p_ref[0]
            for j in range(1, N_DEV):
                g = g + p_ref[j]
            nm = ADAM_B1 * m_ref[...] + (1.0 - ADAM_B1) * g
            nv = ADAM_B2 * v_ref[...] + (1.0 - ADAM_B2) * (g * g)
            g_ref[...] = g
            nm_ref[...] = nm
            nv_ref[...] = nv
            d_ref[...] = -ADAM_LR * ((nm / c1) / (jnp.sqrt(nv / c2) + ADAM_EPS) + ADAM_WD * w_ref[...])

        base = lead * Ld
        row = BS((1, tr, N), lambda l, i, base=base: (base + l, i, 0))
        prev = [] if outs is None else list(outs)
        outs = _pcall(body, name=f"{name}_{lead}", grid=(Ld, r // tr),
                      in_specs=[BS((N_DEV, 1, tr, N), lambda l, i: (0, l, i, 0)), row, row, row]
                      + [BS(memory_space=pl.ANY)] * len(prev),
                      out_specs=[row] * 4, out_shape=[SDS((n_lead * Ld, r, N), F32)] * 4,
                      input_output_aliases={4 + j: j for j in range(len(prev))},
                      compiler_params=_cparams(("parallel", "parallel")))(p.reshape(N_DEV, Ld, r, N), w, m, v, *prev)
    return [t.reshape(shape) for t in outs]


def _place():
    return lax.axis_index("x"), lax.axis_index("y"), lax.axis_index("c")


def _all_gather(xs, name):
    n = len(xs)
    pairs = [(i, l) for i, x in enumerate(xs) for l in range(x.shape[0])]

    def body(*refs):
        x_refs, o_refs = refs[:n], refs[n:2 * n]
        send_sems, recv_sems, local_sems = refs[2 * n:]
        x_, y_, c_ = _place()
        me, sibling = (x_, y_, c_), (x_, y_, 1 - c_)
        chips = [(1 - x_, y_), (x_, 1 - y_), (1 - x_, 1 - y_)]

        def copy(e, k, block, to, from_input=False):
            i, l = pairs[e]
            px, py, pc = block
            dst = o_refs[i].at[l, 4 * px + 2 * py + pc]
            return pltpu.make_async_remote_copy(
                src_ref=x_refs[i].at[l] if from_input else dst, dst_ref=dst,
                send_sem=send_sems.at[7 * e + k], recv_sem=recv_sems.at[7 * e + k],
                device_id=to, device_id_type=pl.DeviceIdType.MESH)

        every = range(len(pairs))
        mine = [pltpu.make_async_copy(x_refs[i].at[l], o_refs[i].at[l, 4 * x_ + 2 * y_ + c_], local_sems.at[e])
                for e, (i, l) in enumerate(pairs)]
        for cp in mine:
            cp.start()
        first = [copy(e, 0, me, sibling, True) for e in every]
        first += [copy(e, 1 + j, me, (*chip, c_), True) for j, chip in enumerate(chips) for e in every]
        for cp in first:
            cp.start()
        passed = []
        for j, chip in enumerate(chips):
            for e in every:
                copy(e, 1 + j, (*chip, c_), me).wait_recv()
            for e in every:
                cp = copy(e, 4 + j, (*chip, c_), sibling)
                cp.start()
                passed.append(cp)
        for e in every:
            copy(e, 0, sibling, me).wait_recv()
        for j, chip in enumerate(chips):
            for e in every:
                copy(e, 4 + j, (*chip, 1 - c_), me).wait_recv()
        for cp in first + passed:
            cp.wait_send()
        for cp in mine:
            cp.wait()

    any_spec = BS(memory_space=pl.ANY)
    return _pcall(body, name=name, in_specs=[any_spec] * n, out_specs=[any_spec] * n,
                  out_shape=[SDS((x.shape[0], N_DEV) + x.shape[1:], x.dtype) for x in xs],
                  scratch_shapes=[pltpu.SemaphoreType.DMA((7 * len(pairs),)), pltpu.SemaphoreType.DMA((7 * len(pairs),)),
                                  pltpu.SemaphoreType.DMA((len(pairs),))],
                  compiler_params=pltpu.CompilerParams(has_side_effects=True))(*xs)


def _peer_of(k, place):
    x_, y_, c_ = place
    fx, fy, fc = (k >> 2) & 1, (k >> 1) & 1, k & 1
    return fx + x_ - 2 * fx * x_, fy + y_ - 2 * fy * y_, fc + c_ - 2 * fc * c_


def _split_copy(src_ref, land_ref, send_sems, recv_sems, e, k, place, scatter):
    x_, y_, c_ = place
    px, py, pc = _peer_of(k, place)
    return pltpu.make_async_remote_copy(
        src_ref=src_ref.at[4 * px + 2 * py + pc] if scatter else src_ref, dst_ref=land_ref.at[4 * x_ + 2 * y_ + c_],
        send_sem=send_sems.at[7 * e + k - 1], recv_sem=recv_sems.at[7 * e + k - 1],
        device_id=(px, py, pc), device_id_type=pl.DeviceIdType.MESH)


def _own_copy(src_ref, land_ref, sems, slot, place, scatter):
    x_, y_, c_ = place
    me = 4 * x_ + 2 * y_ + c_
    return pltpu.make_async_copy(src_ref.at[me] if scatter else src_ref, land_ref.at[me], sems.at[slot])


_HBM_SPEC = BS(memory_space=pltpu.HBM)
_SEM_SPEC = BS(memory_space=pltpu.SEMAPHORE)
_EFFECT = pltpu.SideEffectType.DATAFLOW_SIDE_EFFECTING


def _copies_start(srcs, scatter, name, thru=None):
    n = len(srcs)
    lands = [lax.empty(s.shape if scatter else (N_DEV,) + s.shape, s.dtype) for s in srcs]
    passed = srcs + lands + list(thru or ())

    def body(*refs):
        src_refs, land_refs = refs[:n], refs[n:2 * n]
        send_sems, recv_sems = refs[len(passed)], refs[len(passed) + 1]
        token = refs[-1]
        place = _place()
        for e in range(n):
            for k in range(1, N_DEV):
                _split_copy(src_refs[e], land_refs[e], send_sems, recv_sems, e, k, place, scatter).start()
            _own_copy(src_refs[e], land_refs[e], send_sems, 7 * n + e, place, scatter).start()
        token[...] = jnp.zeros_like(token)

    hbm = lambda t: pltpu.with_memory_space_constraint(t, pltpu.HBM)
    out = _pcall(body, name=name,
                 out_shape=(pltpu.SemaphoreType.DMA((8 * n,)), pltpu.SemaphoreType.DMA((7 * n,)),
                            *[pltpu.HBM(t.shape, t.dtype) for t in passed], SDS((8, LANES), F32)),
                 in_specs=[_HBM_SPEC] * len(passed),
                 out_specs=(_SEM_SPEC, _SEM_SPEC, *[_HBM_SPEC] * len(passed), BS(memory_space=pltpu.VMEM)),
                 input_output_aliases={i: 2 + i for i in range(len(passed))},
                 compiler_params=pltpu.CompilerParams(has_side_effects=_EFFECT))(*[hbm(t) for t in passed])
    return out[0], out[1], list(out[2:2 + n]), list(out[2 + n:2 + 2 * n]), list(out[2 + 2 * n:-1])


def _copies_wait(started, which, scatter, after, name):
    send_sems, recv_sems, srcs, lands, _ = started
    n, n_started = len(which), len(srcs)
    after = list(after) if isinstance(after, (list, tuple)) else [after]

    def body(*refs):
        src_refs, land_refs = refs[:n], refs[n:2 * n]
        send_s, recv_s = refs[2 * n], refs[2 * n + 1]
        place = _place()
        for j, e in enumerate(which):
            for k in range(1, N_DEV):
                cp = _split_copy(src_refs[j], land_refs[j], send_s, recv_s, e, k, place, scatter)
                cp.wait_send()
                cp.wait_recv()
            _own_copy(src_refs[j], land_refs[j], send_s, 7 * n_started + e, place, scatter).wait()

    args = [srcs[e] for e in which] + [lands[e] for e in which]
    out = _pcall(body, name=name, out_shape=tuple(pltpu.HBM(t.shape, t.dtype) for t in args),
                 in_specs=[_HBM_SPEC] * (2 * n) + [_SEM_SPEC, _SEM_SPEC] + [BS(memory_space=pl.ANY)] * len(after),
                 out_specs=tuple([_HBM_SPEC] * (2 * n)), input_output_aliases={i: i for i in range(2 * n)},
                 compiler_params=pltpu.CompilerParams(has_side_effects=_EFFECT))(*args, send_sems, recv_sems, *after)
    return list(out[:n]), list(out[n:])


def _pad_flat(t, quantum=PACK_QUANTUM):
    f = t.reshape(-1)
    pad = (-f.shape[0]) % quantum
    return jnp.pad(f, (0, pad)) if pad else f


def _pack(arrs, dtype):
    return jnp.concatenate([_pad_flat(a.astype(dtype)) for a in arrs]).reshape(-1, LANES)


def _unpack(buf, shapes, lead=()):
    flat = buf.reshape(lead + (-1,))
    out, off = [], 0
    for s in shapes:
        n = math.prod(s)
        out.append(flat[..., off:off + n].reshape(lead + tuple(s)))
        off += n + (-n) % PACK_QUANTUM
    return out


def _full_from_gathered(g, axis):
    t = jnp.moveaxis(g, 0, axis)
    s = t.shape
    return t.reshape(s[:axis] + (s[axis] * s[axis + 1],) + s[axis + 2:])


def _parts_from_full(t, axis):
    s = t.shape
    t = t.reshape(s[:axis] + (N_DEV, s[axis] // N_DEV) + s[axis + 1:])
    return jnp.moveaxis(t, axis, 0)


def _head_rows(t):
    return jnp.repeat(t, HEAD_DIM).reshape(1, -1)


@functools.partial(jax.custom_vjp, nondiff_argnums=(3,))
def _dilated_attention(q, k, v, name):
    return _dilated_attention_fwd(q, k, v, name)[0]


def _dilated_attention_fwd(q, k, v, name):
    no_sink = jnp.full((1, q.shape[1]), NEG, F32)
    outs, lses = zip(*[_att_fwd_call(q, k, v, no_sink, d, ATT_BLOCK, f"{name}_d{d}") for d in DILATIONS])
    return _merge_fwd_call(outs, lses, name + "_merge"), (q, k, v, outs, lses)


def _dilated_attention_bwd(name, res, do):
    q, k, v, outs, lses = res
    no_sink = jnp.full((1, q.shape[1]), NEG, F32)
    parts = _merge_bwd_call(outs, lses, do, name + "_merge_bwd")
    acc = None
    for i, d in reversed(list(enumerate(DILATIONS))):
        acc = _att_bwd_call(q, k, v, no_sink, outs[i], lses[i], parts[i], parts[3 + i], d, ATT_BLOCK,
                            f"{name}_d{d}_bwd", acc)[:3]
    return acc


_dilated_attention.defvjp(_dilated_attention_fwd, _dilated_attention_bwd)


STAGES = (
    ("proj0", ('mix_norm',), (('ab_w_in', 0),)),
    ("mixer0", ('lru_conv_w', 'lru_conv_b', 'lru_ba', 'lru_bx', 'lru_lambda'),
     (('lru_wa', 0), ('lru_wx', 0), ('ab_w_out', 0))),
    ("xa0", ('xa_norm', 'xa_mem_norm'), (('xa_wq', 0), ('xa_wkv', 0), ('xa_wo', 0))),
    ("ffn0", ('ffn_norm',), (('ffn_w_gate_up', 0), ('ffn_w_down', 0))),
    ("mixer1", ('mix_norm', 'c_b_qkv', 'c_sinks', 'c_b_out'), (('c_w_qkv', 0), ('c_w_out', 0))),
    ("xa1", ('xa_norm', 'xa_mem_norm'), (('xa_wq', 1), ('xa_wkv', 1), ('xa_wo', 1))),
    ("ffn1", ('ffn_norm',), (('ffn_w_gate_up', 1), ('ffn_w_down', 1))),
)


def _stage_fn(stage, Wb, tabs, mem):
    layer = int(stage[-1])
    L = f"l{layer}"

    def run(S, Cw, h):
        def lin(a, key, bias, add, name, rows=None):
            wb, wc = Wb[key], Cw[key]
            if rows is not None:
                wb, wc = wb[rows], wc[rows]
            return linear(a, wb, wc, bias, add, name)

        def norm_lin(a, gain, key, bias, name):
            return norm_linear(a, gain, Wb[key], Cw[key], bias, key[0] in COLUMN_CUT, name)

        if stage == "mixer0":
            h, x_br, y_br, q, k, v = h
            C = S['lru_conv_w'].shape[-1]
            xc = conv4(x_br, S['lru_conv_w'][0], S['lru_conv_b'][0], L + "_conv")
            ga, gx, xc = lru_gates(xc, Wb['lru_wa', 0], Cw['lru_wa', 0], S['lru_ba'][0],
                               Wb['lru_wx', 0], Cw['lru_wx', 0], S['lru_bx'][0], L + "_gates")
            rec = lru_scan(xc, ga, gx, y_br, S['lru_lambda'][0], L + "_scan")
            att = _dilated_attention(q, k, v, L + "_att")
            h = lin(att, ('ab_w_out', 0), None, h, L + "_w_out_att", slice(C, None))
            return lin(rec, ('ab_w_out', 0), None, h, L + "_w_out_rec", slice(0, C))
        if stage == "mixer1":
            qw = C_HEADS * HEAD_DIM
            kw = C_KV_HEADS * HEAD_DIM
            q, k, v, h = norm_linear_pieces(h, S['mix_norm'][1], Wb['c_w_qkv', 0], Cw['c_w_qkv', 0], S['c_b_qkv'][0], tabs,
                                            ((qw, True), (kw, True), (kw, False)), L + "_w_qkv")
            o = band_attention(q, k, v, _head_rows(S['c_sinks'][0]), 1, ATT_BLOCK - 1, False, L + "_att")
            return lin(o, ('c_w_out', 0), S['c_b_out'][0], h, L + "_w_out")
        if stage.startswith("xa"):
            xq, h = norm_lin(h, S['xa_norm'][layer], ('xa_wq', layer), None, L + "_xa_wq")
            xkv, _ = norm_lin(mem, S['xa_mem_norm'][layer], ('xa_wkv', layer), None, L + "_xa_wkv")
            return lin(cross_attention(xq, xkv, L + "_xa"), ('xa_wo', layer), None, h, L + "_xa_wo")
        gu, down = ('ffn_w_gate_up', layer), ('ffn_w_down', layer)
        return ffn_block(h, S['ffn_norm'][layer], Wb[gu], Cw[gu], Wb[down], Cw[down], L + "_ffn")

    return run


def kernel(x, mem, mix_norm, ab_w_in, lru_conv_w, lru_conv_b, lru_wa, lru_ba, lru_wx, lru_bx, lru_lambda, ab_w_out, c_w_qkv, c_b_qkv, c_sinks, c_w_out, c_b_out, xa_norm, xa_mem_norm, xa_wq, xa_wkv, xa_wo, ffn_norm, ffn_w_gate_up, ffn_w_down, final_norm, loss_target, m_mix_norm, m_ab_w_in, m_lru_conv_w, m_lru_conv_b, m_lru_wa, m_lru_ba, m_lru_wx, m_lru_bx, m_lru_lambda, m_ab_w_out, m_c_w_qkv, m_c_b_qkv, m_c_sinks, m_c_w_out, m_c_b_out, m_xa_norm, m_xa_mem_norm, m_xa_wq, m_xa_wkv, m_xa_wo, m_ffn_norm, m_ffn_w_gate_up, m_ffn_w_down, m_final_norm, v_mix_norm, v_ab_w_in, v_lru_conv_w, v_lru_conv_b, v_lru_wa, v_lru_ba, v_lru_wx, v_lru_bx, v_lru_lambda, v_ab_w_out, v_c_w_qkv, v_c_b_qkv, v_c_sinks, v_c_w_out, v_c_b_out, v_xa_norm, v_xa_mem_norm, v_xa_wq, v_xa_wkv, v_xa_wo, v_ffn_norm, v_ffn_w_gate_up, v_ffn_w_down, v_final_norm):
    w_loc = dict(zip(WEIGHT_NAMES, (mix_norm, ab_w_in, lru_conv_w, lru_conv_b, lru_wa, lru_ba, lru_wx, lru_bx, lru_lambda, ab_w_out, c_w_qkv, c_b_qkv, c_sinks, c_w_out, c_b_out, xa_norm, xa_mem_norm, xa_wq, xa_wkv, xa_wo, ffn_norm, ffn_w_gate_up, ffn_w_down, final_norm)))
    m_loc = dict(zip(WEIGHT_NAMES, (m_mix_norm, m_ab_w_in, m_lru_conv_w, m_lru_conv_b, m_lru_wa, m_lru_ba, m_lru_wx, m_lru_bx, m_lru_lambda, m_ab_w_out, m_c_w_qkv, m_c_b_qkv, m_c_sinks, m_c_w_out, m_c_b_out, m_xa_norm, m_xa_mem_norm, m_xa_wq, m_xa_wkv, m_xa_wo, m_ffn_norm, m_ffn_w_gate_up, m_ffn_w_down, m_final_norm)))
    v_loc = dict(zip(WEIGHT_NAMES, (v_mix_norm, v_ab_w_in, v_lru_conv_w, v_lru_conv_b, v_lru_wa, v_lru_ba, v_lru_wx, v_lru_bx, v_lru_lambda, v_ab_w_out, v_c_w_qkv, v_c_b_qkv, v_c_sinks, v_c_w_out, v_c_b_out, v_xa_norm, v_xa_mem_norm, v_xa_wq, v_xa_wkv, v_xa_wo, v_ffn_norm, v_ffn_w_gate_up, v_ffn_w_down, v_final_norm)))

    first_keys = list(STAGES[0][2])
    keys = [key for _, _, stage_keys in STAGES[1:] for key in stage_keys]
    shards = [_shard_view(n, w_loc[n])[l].astype(BF16) for n, l in keys]
    first_g = _all_gather([_pack([w_loc[n] for n in SMALL], F32)[None]]
                          + [_shard_view(n, w_loc[n])[l].astype(BF16)[None] for n, l in first_keys], "gather_first")
    gather = _copies_start(shards, False, "gather_start", thru=[first_g[0]])
    small_g = gather[4][0][0]
    Wb = {key: _full_from_gathered(g[0], _layer_shard_axis(key[0])) for key, g in zip(first_keys, first_g[1:])}
    S = {n: w_loc[n] for n in REPLICATED}
    for n, t in zip(SMALL, _unpack(small_g, [w_loc[n].shape for n in SMALL], lead=(N_DEV,))):
        S[n] = _full_from_gathered(t, SHARD_AXIS[n])

    tabs = _rope_tables(x.shape[1])
    w_in = Wb[first_keys[0]]
    hn0 = _rms_fwd_call(x[0], S['mix_norm'][0], "l0_w_in_norm", BF16)
    lru_w, att_w = N_DEV * lru_conv_w.shape[-1], B_HEADS * HEAD_DIM
    pieces = (("x", lru_w, False), ("y", lru_w, False), ("q", att_w, True), ("k", att_w, True), ("v", att_w, False))
    h, row = [x[0]], 0
    for piece, width, rotated in pieces:
        h.append(_mm(hn0, w_in[row:row + width], tb=True, rope=tabs if rotated else None, name="l0_w_in_" + piece,
                     tm=2048, tn=1024, tk=1024))
        row += width
    h = tuple(h)
    vjps = []
    for stage, small_names, stage_keys in STAGES[1:]:
        which = [keys.index(key) for key in stage_keys]
        _, lands = _copies_wait(gather, which, False, jax.tree.leaves(h)[-1], "gather_wait_" + stage)
        for e, land in zip(which, lands):
            Wb[keys[e]] = _full_from_gathered(land, _layer_shard_axis(keys[e][0]))
        carriers = {key: jnp.zeros(Wb[key].shape, F32) for key in stage_keys}
        h, vjp_fn = jax.vjp(_stage_fn(stage, Wb, tabs, mem[0]), {n: S[n] for n in small_names}, carriers, h)
        vjps.append(vjp_fn)
    loss_part, dh, dg_final = _final_call(h, S['final_norm'], loss_target[0], "final_loss")

    grads = {'final_norm': dg_final.reshape(final_norm.shape)}
    exchanges, send_keys, send_parts = [], [], []

    def start_exchange(stage, dh):
        leaves, tree = jax.tree.flatten(dh)
        started = _copies_start(list(send_parts), True, "grad_start_" + stage, thru=leaves)
        exchanges.append((stage, started, list(send_keys)))
        send_keys.clear()
        send_parts.clear()
        return jax.tree.unflatten(tree, started[4])

    for (stage, small_names, stage_keys), vjp_fn in zip(reversed(STAGES[1:]), reversed(vjps)):
        g_small, g_big, dh = vjp_fn(dh)
        for n in small_names:
            grads[n] = grads[n] + g_small[n] if n in grads else g_small[n]
        send_keys += list(stage_keys)
        send_parts += [_parts_from_full(g_big[key], _layer_shard_axis(key[0])) for key in stage_keys]
        if stage == "xa1":
            continue
        if stage == "mixer0":
            small_parts = [_parts_from_full(grads[n], SHARD_AXIS[n]) for n in SMALL]
            send_keys.append("small")
            send_parts.append(jnp.stack([_pack([p[j] for p in small_parts], F32) for j in range(N_DEV)]))
        dh = start_exchange(stage, dh)
    d_res, d_proj = dh[0], _join_pieces_call(dh[1:], [rotated for _, _, rotated in pieces], tabs, "l0_w_in_dproj")[0]
    send_keys.append(first_keys[0])
    send_parts.append(_parts_from_full(_mm(d_proj, hn0, ta=True, name="l0_w_in_dw", tm=1408, tn=1024, tk=2048),
                                       _layer_shard_axis(first_keys[0][0])))
    d_res, d_proj = start_exchange("proj0", (d_res, d_proj))
    dx, dg0 = _mm_rms_bwd(d_proj, w_in, x[0], S['mix_norm'][0], d_res, "l0_w_in_da", wt=True)
    grads['mix_norm'] = grads['mix_norm'] + jnp.concatenate([dg0, jnp.zeros_like(dg0)], axis=0)
    rep_names = REPLICATED + ["loss"]
    grads["loss"] = loss_part
    zero = jnp.zeros((1, 1), F32)
    for d in (w_loc, m_loc, v_loc):
        d["loss"] = zero
    rep_started = _copies_start([_pack([grads[n] for n in rep_names], F32)], False, "rep_grads_start")

    parts, out = {}, {}

    def end_exchange(stage, started, ex_keys, after):
        _, lands = _copies_wait(started, list(range(len(ex_keys))), True, after, "grad_wait_" + stage)
        parts.update(zip(ex_keys, lands))

    def adamw(p, names, call_name):
        if len(names) == 1:
            n = names[0]
            res = _adamw_call(p, _shard_view(n, w_loc[n]), _shard_view(n, m_loc[n]), _shard_view(n, v_loc[n]), call_name)
            for kind, t in zip(("grad", "delta", "new_m", "new_v"), res):
                out[kind, n] = _shard_view(n, t)
        else:
            res = _adamw_call(p, *[_pack([d[n] for n in names], F32) for d in (w_loc, m_loc, v_loc)], call_name)
            for kind, buf in zip(("grad", "delta", "new_m", "new_v"), res):
                for n, t in zip(names, _unpack(buf, [w_loc[n].shape for n in names])):
                    out[kind, n] = t

    for ex in exchanges[:-1]:
        end_exchange(*ex, dx)
    last_names = {key[0] for key in exchanges[-1][2]}
    for n in BIG:
        if n not in last_names:
            adamw([parts[n, l] for l in range(w_loc[n].shape[0])], [n], "adamw_" + n)
    adamw(parts["small"], SMALL, "adamw_small")
    end_exchange(*exchanges[-1], [out["new_v", n] for n in BIG if n not in last_names])
    for n in BIG:
        if n in last_names:
            adamw([parts[n, l] for l in range(w_loc[n].shape[0])], [n], "adamw_" + n)
    _, rep_land = _copies_wait(rep_started, [0], False, out["new_v", "ab_w_in"], "rep_grads_wait")
    adamw(rep_land[0], rep_names, "adamw_replicated")
    loss = out["grad", "loss"][0, 0]

    return (loss, dx[None], *[out[kind, n] for kind in ("grad", "delta", "new_m", "new_v") for n in WEIGHT_NAMES])
```
